```python
import jax, jax.numpy as jnp
from jax import lax
import numpy as np

D_MODEL = 1024
BATCH = 16
SEQ = 2048
DEPTH = 1

GRID_W = 64
CTX_LEN = 256
EPS = 1e-6
CONV_WIDTH = D_MODEL
CONV_K = 31
GLA_HEADS = 4
GLA_DK = D_MODEL // 2
GLA_DV = D_MODEL
HEAD_K = GLA_DK // GLA_HEADS
HEAD_V = GLA_DV // GLA_HEADS
GATE_RANK = 16
GATE_TAU = 16.0
CHUNK = 64
IN_SIZES = (CONV_WIDTH, CONV_WIDTH, CONV_WIDTH, GLA_DK, GLA_DK, GLA_DV, GATE_RANK, GATE_RANK, GLA_DV, D_MODEL, D_MODEL)
N_IN = 3 * CONV_WIDTH + 2 * GLA_DK + 2 * GLA_DV + 2 * GATE_RANK + 2 * D_MODEL
STATE_SIZES = (GLA_DK, GLA_DV, GATE_RANK, GATE_RANK)
STATE_LO = 3 * CONV_WIDTH + GLA_DK
STATE_HI = STATE_LO + GLA_DK + GLA_DV + 2 * GATE_RANK

kernel_name = "hybrid_conformer_gla_dit_block"


def split_cols(p, sizes):
    idx = [int(i) for i in np.cumsum(sizes)[:-1]]
    return jnp.split(p, idx, axis=-1)


def rmsnorm(x, g):
    xf = x.astype(jnp.float32)
    y = xf * lax.rsqrt(jnp.mean(xf * xf, axis=-1, keepdims=True) + EPS)
    return y * g.astype(jnp.float32)


def layernorm(x, g, b):
    xf = x.astype(jnp.float32)
    mu = jnp.mean(xf, axis=-1, keepdims=True)
    var = jnp.mean(jnp.square(xf - mu), axis=-1, keepdims=True)
    return (xf - mu) * lax.rsqrt(var + EPS) * g + b


def ada_mod(cvec, w, b):
    m = jax.nn.silu(cvec) @ w + b
    return jnp.split(m, 3, axis=-1)


def dwconv(x, w, b):
    C = x.shape[-1]
    y = lax.conv_general_dilated(x, w.astype(x.dtype)[:, None, :], (1,), [(CONV_K // 2, CONV_K // 2)],
                                 dimension_numbers=('NWC', 'WIO', 'NWC'), feature_group_count=C)
    return y + b.astype(x.dtype)


def axial_dwconv(a, w, b, rows):
    Bn, S, C = a.shape
    half = C // 2
    ah = a[..., :half].reshape(Bn * rows, GRID_W, half)
    yh = dwconv(ah, w[:, :half], b[:half]).reshape(Bn, S, half)
    av = a[..., half:].reshape(Bn, rows, GRID_W, C - half).transpose(0, 2, 1, 3).reshape(Bn * GRID_W, rows, C - half)
    yv = dwconv(av, w[:, half:], b[half:]).reshape(Bn, GRID_W, rows, C - half).transpose(0, 2, 1, 3).reshape(Bn, S, C - half)
    return jnp.concatenate([yh, yv], axis=-1)


def conv_branch(glu_v, glu_g, z, conv_fn, ln_g, ln_b, proj):
    a = glu_v * jax.nn.sigmoid(glu_g)
    a = conv_fn(a)
    a = jax.nn.silu(layernorm(a, ln_g, ln_b))
    return (a * jax.nn.silu(z)) @ proj


def heads(t, hd):
    Bn, T, _ = t.shape
    return t.reshape(Bn, T, GLA_HEADS, hd).transpose(0, 2, 1, 3).astype(jnp.float32)


def log_decay(lr, up, bias):
    return heads(jax.nn.log_sigmoid((lr @ up + bias).astype(jnp.float32)) / GATE_TAU, HEAD_K)


def flip(t):
    return jnp.flip(t, axis=2)


def gla_final_state(k, v, g):
    b = jnp.cumsum(g, axis=2)
    w = jnp.exp(b[:, :, -1:, :] - b)
    return jnp.einsum('bhtd,bhte->bhde', k * w, v)


def gla_chunk_scan(q, k, v, g, s0):
    Bn, H, T, _ = q.shape
    dv = v.shape[-1]
    n = T // CHUNK

    def chunks(t):
        return jnp.moveaxis(t.reshape(Bn, H, n, CHUNK, t.shape[-1]), 2, 0)

    lower = jnp.tril(jnp.ones((CHUNK, CHUNK), dtype=bool))

    def step(S, inp):
        qc, kc, vc, gc = inp
        b = jnp.cumsum(gc, axis=2)
        o_inter = jnp.einsum('bhld,bhde->bhle', qc * jnp.exp(b), S)
        rel = jnp.where(lower[:, :, None], b[:, :, :, None, :] - b[:, :, None, :, :], -jnp.inf)
        A = jnp.einsum('bhid,bhjd,bhijd->bhij', qc, kc, jnp.exp(rel))
        o_intra = jnp.einsum('bhij,bhje->bhie', A, vc)
        b_last = b[:, :, -1:, :]
        S_new = jnp.exp(b_last[:, :, 0, :, None]) * S + jnp.einsum('bhld,bhle->bhde', kc * jnp.exp(b_last - b), vc)
        return S_new, o_inter + o_intra

    S_fin, o = lax.scan(step, s0, (chunks(q), chunks(k), chunks(v), chunks(g)))
    return jnp.moveaxis(o, 0, 2).reshape(Bn, H, T, dv), S_fin


def bidir_gla(q, k, v, gf, gb, s_f, s_b):
    o_f, _ = gla_chunk_scan(q, k, v, gf, s_f)
    o_b, _ = gla_chunk_scan(flip(q), flip(k), flip(v), flip(gb), s_b)
    return o_f + flip(o_b)


def gla_output(o, r, norm_g, proj):
    o = o * lax.rsqrt(jnp.mean(o * o, axis=-1, keepdims=True) + EPS) * norm_g
    Bn, H, T, dv = o.shape
    o = o.transpose(0, 2, 1, 3).reshape(Bn, T, H * dv)
    return (o * jax.nn.silu(r)) @ proj


def _fwd_setup_inputs(seed: int = 0) -> dict:
    key = jax.random.key(seed)
    ks = jax.random.split(key, 24)

    def nrm(k, shape, scale=1.0):
        return jax.random.normal(k, shape, jnp.float32) * scale

    return {
        "x": nrm(ks[0], (BATCH, SEQ, D_MODEL)),
        "c": nrm(ks[1], (BATCH, D_MODEL)),
        "ctx": nrm(ks[2], (BATCH, CTX_LEN, D_MODEL)),
        "c_ctx": nrm(ks[3], (D_MODEL,)),
        "ada_w": nrm(ks[4], (DEPTH, D_MODEL, 3 * D_MODEL), D_MODEL ** -0.5),
        "ada_b": nrm(ks[5], (DEPTH, 3 * D_MODEL), 0.02),
        "norm_g": 1.0 + nrm(ks[6], (DEPTH, D_MODEL), 0.02),
        "w_in": nrm(ks[7], (DEPTH, D_MODEL, N_IN), D_MODEL ** -0.5),
        "b_in": nrm(ks[8], (DEPTH, N_IN), 0.02),
        "conv_w": nrm(ks[9], (DEPTH, CONV_K, CONV_WIDTH), CONV_K ** -0.5),
        "conv_b": nrm(ks[10], (DEPTH, CONV_WIDTH), 0.02),
        "conv_ln_g": 1.0 + nrm(ks[11], (DEPTH, CONV_WIDTH), 0.02),
        "conv_ln_b": nrm(ks[12], (DEPTH, CONV_WIDTH), 0.02),
        "conv_proj": nrm(ks[13], (DEPTH, CONV_WIDTH, D_MODEL), CONV_WIDTH ** -0.5),
        "decay_up_fwd": nrm(ks[14], (DEPTH, GATE_RANK, GLA_DK), GATE_RANK ** -0.5),
        "decay_bias_fwd": nrm(ks[15], (DEPTH, GLA_DK), 0.1),
        "decay_up_bwd": nrm(ks[16], (DEPTH, GATE_RANK, GLA_DK), GATE_RANK ** -0.5),
        "decay_bias_bwd": nrm(ks[17], (DEPTH, GLA_DK), 0.1),
        "gla_norm_g": 1.0 + nrm(ks[18], (DEPTH, HEAD_V), 0.02),
        "gla_proj": nrm(ks[19], (DEPTH, GLA_DV, D_MODEL), GLA_DV ** -0.5),
        "w_out": nrm(ks[20], (DEPTH, D_MODEL, D_MODEL), D_MODEL ** -0.5),
        "final_norm_g": 1.0 + nrm(ks[21], (D_MODEL,), 0.02),
    }


def _fwd_reference(x, c, ctx, c_ctx, ada_w, ada_b, norm_g, w_in, b_in, conv_w, conv_b, conv_ln_g, conv_ln_b,
              conv_proj, decay_up_fwd, decay_bias_fwd, decay_up_bwd, decay_bias_bwd, gla_norm_g, gla_proj,
              w_out, final_norm_g):
    Bn, S, _ = x.shape
    rows = S // GRID_W
    h = x
    hc = ctx
    for l in range(DEPTH):
        last = l == DEPTH - 1
        shift, scale, gate = ada_mod(c, ada_w[l], ada_b[l])
        shift_c, scale_c, gate_c = ada_mod(c_ctx, ada_w[l], ada_b[l])
        u = rmsnorm(h, norm_g[l]) * (1.0 + scale[:, None, :]) + shift[:, None, :]
        uc = rmsnorm(hc, norm_g[l]) * (1.0 + scale_c) + shift_c

        if last:
            pc = uc @ w_in[l][:, STATE_LO:STATE_HI] + b_in[l][STATE_LO:STATE_HI]
            kc_, vc_, afc, abc = split_cols(pc, STATE_SIZES)
        else:
            (gvc, ggc, zc, qc_, kc_, vc_, afc, abc, rc, mgc_conv, mgc_gla) = split_cols(uc @ w_in[l] + b_in[l], IN_SIZES)
        k_ctx = heads(kc_, HEAD_K)
        v_ctx = heads(vc_, HEAD_V)
        gf_ctx = log_decay(afc, decay_up_fwd[l], decay_bias_fwd[l])
        gb_ctx = log_decay(abc, decay_up_bwd[l], decay_bias_bwd[l])
        s_f = gla_final_state(k_ctx, v_ctx, gf_ctx)
        s_b = gla_final_state(flip(k_ctx), flip(v_ctx), flip(gb_ctx))

        (gv, gg, z, q_, k_, v_, af, ab, r, mg_conv, mg_gla) = split_cols(u @ w_in[l] + b_in[l], IN_SIZES)
        y_conv = conv_branch(gv, gg, z, lambda a: axial_dwconv(a, conv_w[l], conv_b[l], rows),
                             conv_ln_g[l], conv_ln_b[l], conv_proj[l])
        q = heads(q_, HEAD_K) * (HEAD_K ** -0.5)
        k = heads(k_, HEAD_K)
        v = heads(v_, HEAD_V)
        gf = log_decay(af, decay_up_fwd[l], decay_bias_fwd[l])
        gb = log_decay(ab, decay_up_bwd[l], decay_bias_bwd[l])
        o = bidir_gla(q, k, v, gf, gb, s_f, s_b)
        y_gla = gla_output(o, r, gla_norm_g[l], gla_proj[l])
        merged = jax.nn.sigmoid(mg_conv) * y_conv + jax.nn.sigmoid(mg_gla) * y_gla
        h_new = h + gate[:, None, :] * (merged @ w_out[l])

        if not last:
            yc_conv = conv_branch(gvc, ggc, zc, lambda a: dwconv(a, conv_w[l], conv_b[l]),
                                  conv_ln_g[l], conv_ln_b[l], conv_proj[l])
            q_ctx = heads(qc_, HEAD_K) * (HEAD_K ** -0.5)
            zero_state = jnp.zeros((Bn, GLA_HEADS, HEAD_K, HEAD_V), jnp.float32)
            oc = bidir_gla(q_ctx, k_ctx, v_ctx, gf_ctx, gb_ctx, zero_state, zero_state)
            yc_gla = gla_output(oc, rc, gla_norm_g[l], gla_proj[l])
            merged_c = jax.nn.sigmoid(mgc_conv) * yc_conv + jax.nn.sigmoid(mgc_gla) * yc_gla
            hc = hc + gate_c * (merged_c @ w_out[l])
        h = h_new
    return rmsnorm(h, final_norm_g)


import jax as _jax
import jax.numpy as _jnp

TWIN_FORMAT = 'train_step'
FWD_PARAMS = ['x', 'c', 'ctx', 'c_ctx', 'ada_w', 'ada_b', 'norm_g', 'w_in', 'b_in', 'conv_w', 'conv_b', 'conv_ln_g', 'conv_ln_b', 'conv_proj', 'decay_up_fwd', 'decay_bias_fwd', 'decay_up_bwd', 'decay_bias_bwd', 'gla_norm_g', 'gla_proj', 'w_out', 'final_norm_g']
TWIN_WEIGHTS = ['c_ctx', 'ada_w', 'ada_b', 'norm_g', 'w_in', 'b_in', 'conv_w', 'conv_b', 'conv_ln_g', 'conv_ln_b', 'conv_proj', 'decay_up_fwd', 'decay_bias_fwd', 'decay_up_bwd', 'decay_bias_bwd', 'gla_norm_g', 'gla_proj', 'w_out', 'final_norm_g']
TWIN_DIFF_INPUT = 'x'
TWIN_INPUTS = ['x', 'c', 'ctx', 'c_ctx', 'ada_w', 'ada_b', 'norm_g', 'w_in', 'b_in', 'conv_w', 'conv_b', 'conv_ln_g', 'conv_ln_b', 'conv_proj', 'decay_up_fwd', 'decay_bias_fwd', 'decay_up_bwd', 'decay_bias_bwd', 'gla_norm_g', 'gla_proj', 'w_out', 'final_norm_g', 'loss_target', 'm_c_ctx', 'm_ada_w', 'm_ada_b', 'm_norm_g', 'm_w_in', 'm_b_in', 'm_conv_w', 'm_conv_b', 'm_conv_ln_g', 'm_conv_ln_b', 'm_conv_proj', 'm_decay_up_fwd', 'm_decay_bias_fwd', 'm_decay_up_bwd', 'm_decay_bias_bwd', 'm_gla_norm_g', 'm_gla_proj', 'm_w_out', 'm_final_norm_g', 'v_c_ctx', 'v_ada_w', 'v_ada_b', 'v_norm_g', 'v_w_in', 'v_b_in', 'v_conv_w', 'v_conv_b', 'v_conv_ln_g', 'v_conv_ln_b', 'v_conv_proj', 'v_decay_up_fwd', 'v_decay_bias_fwd', 'v_decay_up_bwd', 'v_decay_bias_bwd', 'v_gla_norm_g', 'v_gla_proj', 'v_w_out', 'v_final_norm_g']
TWIN_OUTPUTS = ['loss', 'grad_x', 'grad_c_ctx', 'grad_ada_w', 'grad_ada_b', 'grad_norm_g', 'grad_w_in', 'grad_b_in', 'grad_conv_w', 'grad_conv_b', 'grad_conv_ln_g', 'grad_conv_ln_b', 'grad_conv_proj', 'grad_decay_up_fwd', 'grad_decay_bias_fwd', 'grad_decay_up_bwd', 'grad_decay_bias_bwd', 'grad_gla_norm_g', 'grad_gla_proj', 'grad_w_out', 'grad_final_norm_g', 'delta_c_ctx', 'delta_ada_w', 'delta_ada_b', 'delta_norm_g', 'delta_w_in', 'delta_b_in', 'delta_conv_w', 'delta_conv_b', 'delta_conv_ln_g', 'delta_conv_ln_b', 'delta_conv_proj', 'delta_decay_up_fwd', 'delta_decay_bias_fwd', 'delta_decay_up_bwd', 'delta_decay_bias_bwd', 'delta_gla_norm_g', 'delta_gla_proj', 'delta_w_out', 'delta_final_norm_g', 'new_m_c_ctx', 'new_m_ada_w', 'new_m_ada_b', 'new_m_norm_g', 'new_m_w_in', 'new_m_b_in', 'new_m_conv_w', 'new_m_conv_b', 'new_m_conv_ln_g', 'new_m_conv_ln_b', 'new_m_conv_proj', 'new_m_decay_up_fwd', 'new_m_decay_bias_fwd', 'new_m_decay_up_bwd', 'new_m_decay_bias_bwd', 'new_m_gla_norm_g', 'new_m_gla_proj', 'new_m_w_out', 'new_m_final_norm_g', 'new_v_c_ctx', 'new_v_ada_w', 'new_v_ada_b', 'new_v_norm_g', 'new_v_w_in', 'new_v_b_in', 'new_v_conv_w', 'new_v_conv_b', 'new_v_conv_ln_g', 'new_v_conv_ln_b', 'new_v_conv_proj', 'new_v_decay_up_fwd', 'new_v_decay_bias_fwd', 'new_v_decay_up_bwd', 'new_v_decay_bias_bwd', 'new_v_gla_norm_g', 'new_v_gla_proj', 'new_v_w_out', 'new_v_final_norm_g']
TWIN_LEAF_KINDS = {'loss': 'loss', 'grad_x': 'grad_x', 'grad_c_ctx': 'grad_w', 'grad_ada_w': 'grad_w', 'grad_ada_b': 'grad_w', 'grad_norm_g': 'grad_w', 'grad_w_in': 'grad_w', 'grad_b_in': 'grad_w', 'grad_conv_w': 'grad_w', 'grad_conv_b': 'grad_w', 'grad_conv_ln_g': 'grad_w', 'grad_conv_ln_b': 'grad_w', 'grad_conv_proj': 'grad_w', 'grad_decay_up_fwd': 'grad_w', 'grad_decay_bias_fwd': 'grad_w', 'grad_decay_up_bwd': 'grad_w', 'grad_decay_bias_bwd': 'grad_w', 'grad_gla_norm_g': 'grad_w', 'grad_gla_proj': 'grad_w', 'grad_w_out': 'grad_w', 'grad_final_norm_g': 'grad_w', 'delta_c_ctx': 'delta_w', 'delta_ada_w': 'delta_w', 'delta_ada_b': 'delta_w', 'delta_norm_g': 'delta_w', 'delta_w_in': 'delta_w', 'delta_b_in': 'delta_w', 'delta_conv_w': 'delta_w', 'delta_conv_b': 'delta_w', 'delta_conv_ln_g': 'delta_w', 'delta_conv_ln_b': 'delta_w', 'delta_conv_proj': 'delta_w', 'delta_decay_up_fwd': 'delta_w', 'delta_decay_bias_fwd': 'delta_w', 'delta_decay_up_bwd': 'delta_w', 'delta_decay_bias_bwd': 'delta_w', 'delta_gla_norm_g': 'delta_w', 'delta_gla_proj': 'delta_w', 'delta_w_out': 'delta_w', 'delta_final_norm_g': 'delta_w', 'new_m_c_ctx': 'new_m', 'new_m_ada_w': 'new_m', 'new_m_ada_b': 'new_m', 'new_m_norm_g': 'new_m', 'new_m_w_in': 'new_m', 'new_m_b_in': 'new_m', 'new_m_conv_w': 'new_m', 'new_m_conv_b': 'new_m', 'new_m_conv_ln_g': 'new_m', 'new_m_conv_ln_b': 'new_m', 'new_m_conv_proj': 'new_m', 'new_m_decay_up_fwd': 'new_m', 'new_m_decay_bias_fwd': 'new_m', 'new_m_decay_up_bwd': 'new_m', 'new_m_decay_bias_bwd': 'new_m', 'new_m_gla_norm_g': 'new_m', 'new_m_gla_proj': 'new_m', 'new_m_w_out': 'new_m', 'new_m_final_norm_g': 'new_m', 'new_v_c_ctx': 'new_v', 'new_v_ada_w': 'new_v', 'new_v_ada_b': 'new_v', 'new_v_norm_g': 'new_v', 'new_v_w_in': 'new_v', 'new_v_b_in': 'new_v', 'new_v_conv_w': 'new_v', 'new_v_conv_b': 'new_v', 'new_v_conv_ln_g': 'new_v', 'new_v_conv_ln_b': 'new_v', 'new_v_conv_proj': 'new_v', 'new_v_decay_up_fwd': 'new_v', 'new_v_decay_bias_fwd': 'new_v', 'new_v_decay_up_bwd': 'new_v', 'new_v_decay_bias_bwd': 'new_v', 'new_v_gla_norm_g': 'new_v', 'new_v_gla_proj': 'new_v', 'new_v_w_out': 'new_v', 'new_v_final_norm_g': 'new_v'}


def _forward(args):
    return _fwd_reference(*[args[k] for k in FWD_PARAMS])


def _output_shape():
    out = _jax.eval_shape(lambda: _forward(_fwd_setup_inputs(0)))
    return out.shape, out.dtype

N_MICROBATCH = 1
ADAM_LR = 0.001
ADAM_B1 = 0.9
ADAM_B2 = 0.999
ADAM_EPS = 1e-08
ADAM_WD = 0.01
ADAM_STEP = 10
PER_EXAMPLE_BATCH_AXIS = {'x': 0, 'c': 0, 'ctx': 0, 'loss_target': 0}
SHARED_INPUTS = []
_WEIGHT_DTYPES = {'c_ctx': _jnp.float32, 'ada_w': _jnp.float32, 'ada_b': _jnp.float32, 'norm_g': _jnp.float32, 'w_in': _jnp.float32, 'b_in': _jnp.float32, 'conv_w': _jnp.float32, 'conv_b': _jnp.float32, 'conv_ln_g': _jnp.float32, 'conv_ln_b': _jnp.float32, 'conv_proj': _jnp.float32, 'decay_up_fwd': _jnp.float32, 'decay_bias_fwd': _jnp.float32, 'decay_up_bwd': _jnp.float32, 'decay_bias_bwd': _jnp.float32, 'gla_norm_g': _jnp.float32, 'gla_proj': _jnp.float32, 'w_out': _jnp.float32, 'final_norm_g': _jnp.float32}
MOMENT_SCALE = {'c_ctx': 1.399276e-02, 'ada_w': 6.628277e-02, 'ada_b': 1.129580e-01, 'norm_g': 1.002958e-01, 'w_in': 4.495936e-02, 'b_in': 4.636620e-02, 'conv_w': 3.162009e-02, 'conv_b': 5.502699e-02, 'conv_ln_g': 3.663925e-02, 'conv_ln_b': 3.215171e-02, 'conv_proj': 2.983632e-02, 'decay_up_fwd': 2.165677e-02, 'decay_bias_fwd': 3.347405e-02, 'decay_up_bwd': 1.840284e-02, 'decay_bias_bwd': 3.156863e-02, 'gla_norm_g': 1.075168e-01, 'gla_proj': 4.882545e-02, 'w_out': 5.670662e-02, 'final_norm_g': 3.208379e+01}


def _to_microbatches(a, axis):
    t = _jnp.moveaxis(a, axis, 0)
    t = t.reshape((N_MICROBATCH, t.shape[0] // N_MICROBATCH) + t.shape[1:])
    return _jnp.moveaxis(t, 1, axis + 1)


def setup_inputs(seed: int = 0) -> dict:
    inp = _fwd_setup_inputs(seed)
    key = _jax.random.fold_in(_jax.random.key(seed), 7919)
    shape, _ = _output_shape()
    out = dict(inp)
    out["loss_target"] = _jax.random.normal(_jax.random.fold_in(key, 0), shape, _jnp.float32)
    for i, name in enumerate(TWIN_WEIGHTS):
        w = inp[name].astype(_jnp.float32)
        if MOMENT_SCALE is None:
            s = _jnp.sqrt(_jnp.mean(_jnp.square(w)) + 1e-30)
        else:
            s = MOMENT_SCALE[name]
        km, kv = _jax.random.split(_jax.random.fold_in(key, i + 1))
        out[name] = w
        out["m_" + name] = s * _jax.random.normal(km, w.shape, _jnp.float32)
        out["v_" + name] = (s * s) * _jax.random.uniform(kv, w.shape, _jnp.float32, 0.5, 1.5)
    if N_MICROBATCH > 1:
        for name, axis in PER_EXAMPLE_BATCH_AXIS.items():
            out[name] = _to_microbatches(out[name], axis)
    return {'x': out['x'], 'c': out['c'], 'ctx': out['ctx'], 'c_ctx': out['c_ctx'], 'ada_w': out['ada_w'], 'ada_b': out['ada_b'], 'norm_g': out['norm_g'], 'w_in': out['w_in'], 'b_in': out['b_in'], 'conv_w': out['conv_w'], 'conv_b': out['conv_b'], 'conv_ln_g': out['conv_ln_g'], 'conv_ln_b': out['conv_ln_b'], 'conv_proj': out['conv_proj'], 'decay_up_fwd': out['decay_up_fwd'], 'decay_bias_fwd': out['decay_bias_fwd'], 'decay_up_bwd': out['decay_up_bwd'], 'decay_bias_bwd': out['decay_bias_bwd'], 'gla_norm_g': out['gla_norm_g'], 'gla_proj': out['gla_proj'], 'w_out': out['w_out'], 'final_norm_g': out['final_norm_g'], 'loss_target': out['loss_target'], 'm_c_ctx': out['m_c_ctx'], 'm_ada_w': out['m_ada_w'], 'm_ada_b': out['m_ada_b'], 'm_norm_g': out['m_norm_g'], 'm_w_in': out['m_w_in'], 'm_b_in': out['m_b_in'], 'm_conv_w': out['m_conv_w'], 'm_conv_b': out['m_conv_b'], 'm_conv_ln_g': out['m_conv_ln_g'], 'm_conv_ln_b': out['m_conv_ln_b'], 'm_conv_proj': out['m_conv_proj'], 'm_decay_up_fwd': out['m_decay_up_fwd'], 'm_decay_bias_fwd': out['m_decay_bias_fwd'], 'm_decay_up_bwd': out['m_decay_up_bwd'], 'm_decay_bias_bwd': out['m_decay_bias_bwd'], 'm_gla_norm_g': out['m_gla_norm_g'], 'm_gla_proj': out['m_gla_proj'], 'm_w_out': out['m_w_out'], 'm_final_norm_g': out['m_final_norm_g'], 'v_c_ctx': out['v_c_ctx'], 'v_ada_w': out['v_ada_w'], 'v_ada_b': out['v_ada_b'], 'v_norm_g': out['v_norm_g'], 'v_w_in': out['v_w_in'], 'v_b_in': out['v_b_in'], 'v_conv_w': out['v_conv_w'], 'v_conv_b': out['v_conv_b'], 'v_conv_ln_g': out['v_conv_ln_g'], 'v_conv_ln_b': out['v_conv_ln_b'], 'v_conv_proj': out['v_conv_proj'], 'v_decay_up_fwd': out['v_decay_up_fwd'], 'v_decay_bias_fwd': out['v_decay_bias_fwd'], 'v_decay_up_bwd': out['v_decay_up_bwd'], 'v_decay_bias_bwd': out['v_decay_bias_bwd'], 'v_gla_norm_g': out['v_gla_norm_g'], 'v_gla_proj': out['v_gla_proj'], 'v_w_out': out['v_w_out'], 'v_final_norm_g': out['v_final_norm_g']}


def _loss(weights, diff, rest, loss_target):
    with _jax.named_scope("forward"):
        args = {**rest, TWIN_DIFF_INPUT: diff, **{k: w.astype(_WEIGHT_DTYPES[k]) for k, w in weights.items()}}
        y = _forward(args)
    with _jax.named_scope("loss_head"):
        err = _jnp.square(y.astype(_jnp.float32) - loss_target)
        return 0.5 * _jnp.sum(_jnp.mean(err, axis=-1)) if err.ndim else 0.5 * err


def _adamw(w, g, m, v):
    m = ADAM_B1 * m + (1.0 - ADAM_B1) * g
    v = ADAM_B2 * v + (1.0 - ADAM_B2) * _jnp.square(g)
    m_hat = m / (1.0 - ADAM_B1 ** ADAM_STEP)
    v_hat = v / (1.0 - ADAM_B2 ** ADAM_STEP)
    delta = -ADAM_LR * (m_hat / (_jnp.sqrt(v_hat) + ADAM_EPS) + ADAM_WD * w)
    return delta, m, v


def reference(x, c, ctx, c_ctx, ada_w, ada_b, norm_g, w_in, b_in, conv_w, conv_b, conv_ln_g, conv_ln_b, conv_proj, decay_up_fwd, decay_bias_fwd, decay_up_bwd, decay_bias_bwd, gla_norm_g, gla_proj, w_out, final_norm_g, loss_target, m_c_ctx, m_ada_w, m_ada_b, m_norm_g, m_w_in, m_b_in, m_conv_w, m_conv_b, m_conv_ln_g, m_conv_ln_b, m_conv_proj, m_decay_up_fwd, m_decay_bias_fwd, m_decay_up_bwd, m_decay_bias_bwd, m_gla_norm_g, m_gla_proj, m_w_out, m_final_norm_g, v_c_ctx, v_ada_w, v_ada_b, v_norm_g, v_w_in, v_b_in, v_conv_w, v_conv_b, v_conv_ln_g, v_conv_ln_b, v_conv_proj, v_decay_up_fwd, v_decay_bias_fwd, v_decay_up_bwd, v_decay_bias_bwd, v_gla_norm_g, v_gla_proj, v_w_out, v_final_norm_g):
    given = dict(x=x, c=c, ctx=ctx, c_ctx=c_ctx, ada_w=ada_w, ada_b=ada_b, norm_g=norm_g, w_in=w_in, b_in=b_in, conv_w=conv_w, conv_b=conv_b, conv_ln_g=conv_ln_g, conv_ln_b=conv_ln_b, conv_proj=conv_proj, decay_up_fwd=decay_up_fwd, decay_bias_fwd=decay_bias_fwd, decay_up_bwd=decay_up_bwd, decay_bias_bwd=decay_bias_bwd, gla_norm_g=gla_norm_g, gla_proj=gla_proj, w_out=w_out, final_norm_g=final_norm_g, loss_target=loss_target, m_c_ctx=m_c_ctx, m_ada_w=m_ada_w, m_ada_b=m_ada_b, m_norm_g=m_norm_g, m_w_in=m_w_in, m_b_in=m_b_in, m_conv_w=m_conv_w, m_conv_b=m_conv_b, m_conv_ln_g=m_conv_ln_g, m_conv_ln_b=m_conv_ln_b, m_conv_proj=m_conv_proj, m_decay_up_fwd=m_decay_up_fwd, m_decay_bias_fwd=m_decay_bias_fwd, m_decay_up_bwd=m_decay_up_bwd, m_decay_bias_bwd=m_decay_bias_bwd, m_gla_norm_g=m_gla_norm_g, m_gla_proj=m_gla_proj, m_w_out=m_w_out, m_final_norm_g=m_final_norm_g, v_c_ctx=v_c_ctx, v_ada_w=v_ada_w, v_ada_b=v_ada_b, v_norm_g=v_norm_g, v_w_in=v_w_in, v_b_in=v_b_in, v_conv_w=v_conv_w, v_conv_b=v_conv_b, v_conv_ln_g=v_conv_ln_g, v_conv_ln_b=v_conv_ln_b, v_conv_proj=v_conv_proj, v_decay_up_fwd=v_decay_up_fwd, v_decay_bias_fwd=v_decay_bias_fwd, v_decay_up_bwd=v_decay_up_bwd, v_decay_bias_bwd=v_decay_bias_bwd, v_gla_norm_g=v_gla_norm_g, v_gla_proj=v_gla_proj, v_w_out=v_w_out, v_final_norm_g=v_final_norm_g)
    weights = {n: given[n] for n in TWIN_WEIGHTS}
    shared = {n: given[n] for n in SHARED_INPUTS}
    per_example = {n: given[n] for n in ['x', 'c', 'ctx']}
    grad_fn = _jax.value_and_grad(_loss, argnums=(0, 1))

    def one_microbatch(ex, loss_target):
        ex = dict(ex)
        diff = ex.pop(TWIN_DIFF_INPUT)
        return grad_fn(weights, diff, {**shared, **ex}, loss_target)

    if N_MICROBATCH == 1:
        loss, (grad_w, grad_x) = one_microbatch(per_example, given["loss_target"])
    else:
        def body(carry, xs):
            loss_sum, grad_sum = carry
            l_k, (gw_k, gx_k) = one_microbatch(xs[0], xs[1])
            with _jax.named_scope("update"):
                return (loss_sum + l_k, _jax.tree.map(_jnp.add, grad_sum, gw_k)), gx_k

        init = (_jnp.zeros((), _jnp.float32), _jax.tree.map(_jnp.zeros_like, weights))
        (loss, grad_w), grad_x = _jax.lax.scan(body, init, (per_example, given["loss_target"]))
    with _jax.named_scope("update"):
        delta_w, new_m, new_v = {}, {}, {}
        for n in TWIN_WEIGHTS:
            delta_w[n], new_m[n], new_v[n] = _adamw(weights[n], grad_w[n], given["m_" + n], given["v_" + n])
    return (loss, grad_x, *[grad_w[n] for n in TWIN_WEIGHTS], *[delta_w[n] for n in TWIN_WEIGHTS],
            *[new_m[n] for n in TWIN_WEIGHTS], *[new_v[n] for n in TWIN_WEIGHTS])
```

```python
import functools
import math

import jax
import jax.numpy as jnp
from jax import lax
from jax.experimental import pallas as pl
from jax.experimental.pallas import tpu as pltpu

F32 = jnp.float32
BF16 = jnp.bfloat16
MESH = pl.DeviceIdType.MESH

N_DEV = 8
GRID_W = 64
CHUNK = 64
HEADS = 4
EPS = 1e-6
GATE_TAU = 16.0
LANE = 128
ADAM_LR, ADAM_B1, ADAM_B2, ADAM_EPS, ADAM_WD, ADAM_STEP = 0.001, 0.9, 0.999, 1e-08, 0.01, 10
VMEM_LIMIT = 56 * 1024 * 1024


def _params(**kw):
    return pltpu.CompilerParams(vmem_limit_bytes=VMEM_LIMIT, **kw)


def _tile(n, pref):
    t = (min(pref, n) // LANE) * LANE
    while t >= LANE:
        if n % t == 0:
            return t
        t -= LANE
    return n


def _mm(a, b):
    return jnp.dot(a.astype(BF16), b.astype(BF16), preferred_element_type=F32)


def _mm_nt(a, b):
    return lax.dot_general(a.astype(BF16), b.astype(BF16), (((1,), (1,)), ((), ())), preferred_element_type=F32)


def _mm_tn(a, b):
    return lax.dot_general(a.astype(BF16), b.astype(BF16), (((0,), (0,)), ((), ())), preferred_element_type=F32)


def _mm_hi(a, b):
    return jnp.dot(a, b, precision=lax.Precision.HIGHEST, preferred_element_type=F32)


def _mm_nt_hi(a, b):
    return lax.dot_general(a, b, (((1,), (1,)), ((), ())), precision=lax.Precision.HIGHEST, preferred_element_type=F32)


def _mm_tn_hi(a, b):
    return lax.dot_general(a, b, (((0,), (0,)), ((), ())), precision=lax.Precision.HIGHEST, preferred_element_type=F32)


def _sigmoid(x):
    return 1.0 / (1.0 + jnp.exp(-x))


def _dsilu(x, s):
    return s * (1.0 + x * (1.0 - s))


def _rowsel(table, idx, n):
    out = table[0:1, :]
    for r in range(1, n):
        out = jnp.where(idx == r, table[r:r + 1, :], out)
    return out


def _ada_fwd(cv, ada_w, ada_b):
    def body(cv_ref, w_ref, b_ref, o_ref):
        c = cv_ref[...]
        o_ref[...] = _mm(c * _sigmoid(c), w_ref[...]) + b_ref[...]

    return pl.pallas_call(body, name="ada_fwd", out_shape=jax.ShapeDtypeStruct((cv.shape[0], ada_w.shape[1]), F32),
                          compiler_params=_params())(cv, ada_w, ada_b)


def _ada_bwd(cv, ada_w, dmod):
    def body(cv_ref, w_ref, dm_ref, dw_ref, db_ref, dc_ref):
        c = cv_ref[...]
        s = _sigmoid(c)
        dm = dm_ref[...]
        dw_ref[...] = _mm_tn_hi(c * s, dm)
        db_ref[...] = jnp.sum(dm, axis=0, keepdims=True)
        dc_ref[...] = _mm_nt(dm, w_ref[...]) * _dsilu(c, s)

    d, n3 = ada_w.shape
    return pl.pallas_call(
        body, name="ada_bwd",
        out_shape=(jax.ShapeDtypeStruct((d, n3), F32), jax.ShapeDtypeStruct((1, n3), F32),
                   jax.ShapeDtypeStruct(cv.shape, F32)),
        compiler_params=_params())(cv, ada_w, dmod)


def _norm_fwd(x2, ctx2, mod, norm_g, nb, tm):
    tl, d = x2.shape
    tc = ctx2.shape[0]
    nl, nc = tl // tm, tc // tm
    per_ex = nl // nb

    def body(x_ref, c_ref, mod_ref, g_ref, u_ref):
        i = pl.program_id(0)
        xv = jnp.where(i < nl, x_ref[...], c_ref[...])
        row = jnp.where(i < nl, i // per_ex, nb)
        m = _rowsel(mod_ref[...], row, nb + 1)
        shift, scale = m[:, 0:d], m[:, d:2 * d]
        rstd = lax.rsqrt(jnp.mean(xv * xv, axis=-1, keepdims=True) + EPS)
        u_ref[...] = (xv * rstd * g_ref[...] * (1.0 + scale) + shift).astype(BF16)

    return pl.pallas_call(
        body, name="norm_fwd", grid=(nl + nc,),
        in_specs=[pl.BlockSpec((tm, d), lambda i: (jnp.minimum(i, nl - 1), 0)),
                  pl.BlockSpec((tm, d), lambda i: (jnp.maximum(i - nl, 0), 0)),
                  pl.BlockSpec(mod.shape, lambda i: (0, 0)),
                  pl.BlockSpec((1, d), lambda i: (0, 0))],
        out_specs=pl.BlockSpec((tm, d), lambda i: (i, 0)),
        out_shape=jax.ShapeDtypeStruct((tl + tc, d), BF16),
        compiler_params=_params())(x2, ctx2, mod, norm_g)


def _norm_bwd(x2, ctx2, mod, norm_g, du_a1, du_a2, du_b, gx1, nb, tm):
    tl, d = x2.shape
    tc = ctx2.shape[0]
    nl, nc = tl // tm, tc // tm
    per_ex = nl // nb
    nrow = mod.shape[0]

    def body(x_ref, c_ref, mod_ref, g_ref, d1_ref, d2_ref, d3_ref, gx_ref, gxo_ref, dmod_ref, dg_ref):
        i = pl.program_id(0)

        @pl.when(i == 0)
        def _():
            dmod_ref[...] = jnp.zeros_like(dmod_ref)
            dg_ref[...] = jnp.zeros_like(dg_ref)

        lat = i < nl
        xv = jnp.where(lat, x_ref[...], c_ref[...])
        row = jnp.where(lat, i // per_ex, nb)
        m = _rowsel(mod_ref[...], row, nb + 1)
        scale = m[:, d:2 * d]
        g = g_ref[...]
        du = d3_ref[...] + jnp.where(lat, d1_ref[...] + d2_ref[...], 0.0)
        rstd = lax.rsqrt(jnp.mean(xv * xv, axis=-1, keepdims=True) + EPS)
        xh = xv * rstd
        dshift = jnp.sum(du, axis=0, keepdims=True)
        dscale = jnp.sum(du * xh * g, axis=0, keepdims=True)
        dxn = du * (1.0 + scale)
        dg_ref[...] += jnp.sum(dxn * xh, axis=0, keepdims=True)
        dxh = dxn * g
        dx = rstd * (dxh - xh * jnp.mean(dxh * xh, axis=-1, keepdims=True))
        @pl.when(lat)
        def _():
            gxo_ref[...] = dx + gx_ref[...]

        for r in range(nb + 1):
            dmod_ref[r:r + 1, 0:d] += jnp.where(row == r, dshift, 0.0)
            dmod_ref[r:r + 1, d:2 * d] += jnp.where(row == r, dscale, 0.0)

    lat_map = lambda i: (jnp.minimum(i, nl - 1), 0)
    return pl.pallas_call(
        body, name="norm_bwd", grid=(nl + nc,),
        in_specs=[pl.BlockSpec((tm, d), lat_map),
                  pl.BlockSpec((tm, d), lambda i: (jnp.maximum(i - nl, 0), 0)),
                  pl.BlockSpec(mod.shape, lambda i: (0, 0)),
                  pl.BlockSpec((1, d), lambda i: (0, 0)),
                  pl.BlockSpec((tm, d), lat_map), pl.BlockSpec((tm, d), lat_map),
                  pl.BlockSpec((tm, d), lambda i: (i, 0)),
                  pl.BlockSpec((tm, d), lat_map)],
        out_specs=(pl.BlockSpec((tm, d), lat_map),
                   pl.BlockSpec((nrow, 3 * d), lambda i: (0, 0)),
                   pl.BlockSpec((1, d), lambda i: (0, 0))),
        out_shape=(jax.ShapeDtypeStruct((tl, d), F32), jax.ShapeDtypeStruct((nrow, 3 * d), F32),
                   jax.ShapeDtypeStruct((1, d), F32)),
        compiler_params=_params())(x2, ctx2, mod, norm_g, du_a1, du_a2, du_b, gx1)


def _matmul_bias(name, u, w, b, rows, tm, tn):
    d, n = w.shape

    def body(u_ref, w_ref, b_ref, o_ref):
        o_ref[...] = jnp.dot(u_ref[...], w_ref[...], preferred_element_type=F32) + b_ref[...]

    return pl.pallas_call(
        body, name=name, grid=(n // tn, rows // tm),
        in_specs=[pl.BlockSpec((tm, d), lambda j, i: (i, 0)),
                  pl.BlockSpec((d, tn), lambda j, i: (0, j)),
                  pl.BlockSpec((1, tn), lambda j, i: (0, j))],
        out_specs=pl.BlockSpec((tm, tn), lambda j, i: (i, j)),
        out_shape=jax.ShapeDtypeStruct((rows, n), F32),
        compiler_params=_params())(u, w, b)


def _matmul_nt(name, a, w, koff, tm, tk):
    r, kc = a.shape
    d = w.shape[0]
    nk = kc // tk

    def body(a_ref, w_ref, o_ref):
        k = pl.program_id(1)
        p = lax.dot_general(a_ref[...], w_ref[...], (((1,), (1,)), ((), ())), preferred_element_type=F32)

        @pl.when(k == 0)
        def _():
            o_ref[...] = p

        @pl.when(k > 0)
        def _():
            o_ref[...] += p

    return pl.pallas_call(
        body, name=name, grid=(r // tm, nk),
        in_specs=[pl.BlockSpec((tm, tk), lambda i, k: (i, k)),
                  pl.BlockSpec((d, tk), lambda i, k: (0, koff + k))],
        out_specs=pl.BlockSpec((tm, d), lambda i, k: (i, 0)),
        out_shape=jax.ShapeDtypeStruct((r, d), F32),
        compiler_params=_params())(a, w)


def _matmul_tn(name, a, b, rows, tk, tn):
    m = a.shape[1]
    n = b.shape[1]

    def body(a_ref, b_ref, o_ref, s_ref):
        k = pl.program_id(1)
        bv = b_ref[...]
        p = lax.dot_general(a_ref[...], bv, (((0,), (0,)), ((), ())), preferred_element_type=F32)
        cs = jnp.sum(bv.astype(F32), axis=0, keepdims=True)

        @pl.when(k == 0)
        def _():
            o_ref[...] = p
            s_ref[...] = cs

        @pl.when(k > 0)
        def _():
            o_ref[...] += p
            s_ref[...] += cs

    return pl.pallas_call(
        body, name=name, grid=(n // tn, rows // tk),
        in_specs=[pl.BlockSpec((tk, m), lambda j, k: (k, 0)),
                  pl.BlockSpec((tk, tn), lambda j, k: (k, j))],
        out_specs=(pl.BlockSpec((m, tn), lambda j, k: (0, j)), pl.BlockSpec((1, tn), lambda j, k: (0, j))),
        out_shape=(jax.ShapeDtypeStruct((m, n), F32), jax.ShapeDtypeStruct((1, n), F32)),
        compiler_params=_params())(a, b)


def _conv_taps(pad_ref, w, ktaps, rows, width, horizontal, flip):
    half = ktaps // 2
    acc = None
    for t in range(ktaps):
        s = (half - t) if flip else (t - half)
        if horizontal:
            win = pad_ref[:, pl.ds(16 + s, width), :]
        else:
            win = pad_ref[pl.ds(half + s, rows), :, :]
        term = win * w[t:t + 1, :]
        acc = term if acc is None else acc + term
    return acc


def _conv_fwd(pa, conv_w, conv_b, nb, s, cb):
    ktaps, d = conv_w.shape
    rows, width = s // GRID_W, GRID_W
    half_k = ktaps // 2
    nblk = d // cb
    nh = nblk // 2

    def body(gv_ref, gg_ref, w_ref, b_ref, o_ref, ph_ref, pv_ref):
        j = pl.program_id(1)
        a0 = (gv_ref[...] * _sigmoid(gg_ref[...])).reshape(rows, width, cb)
        w = w_ref[...]

        @pl.when(j < nh)
        def _():
            ph_ref[:, 0:16, :] = jnp.zeros((rows, 16, cb), F32)
            ph_ref[:, 16 + width:32 + width, :] = jnp.zeros((rows, 16, cb), F32)
            ph_ref[:, 16:16 + width, :] = a0
            acc = _conv_taps(ph_ref, w, ktaps, rows, width, True, False)
            o_ref[...] = acc.reshape(s, cb) + b_ref[...]

        @pl.when(j >= nh)
        def _():
            pv_ref[0:half_k, :, :] = jnp.zeros((half_k, width, cb), F32)
            pv_ref[half_k + rows:2 * half_k + rows, :, :] = jnp.zeros((half_k, width, cb), F32)
            pv_ref[half_k:half_k + rows, :, :] = a0
            acc = _conv_taps(pv_ref, w, ktaps, rows, width, False, False)
            o_ref[...] = acc.reshape(s, cb) + b_ref[...]

    return pl.pallas_call(
        body, name="conv_fwd", grid=(nb, nblk),
        in_specs=[pl.BlockSpec((s, cb), lambda b, j: (b, j)),
                  pl.BlockSpec((s, cb), lambda b, j: (b, nblk + j)),
                  pl.BlockSpec((ktaps, cb), lambda b, j: (0, j)),
                  pl.BlockSpec((1, cb), lambda b, j: (0, j))],
        out_specs=pl.BlockSpec((s, cb), lambda b, j: (b, j)),
        out_shape=jax.ShapeDtypeStruct((nb * s, d), F32),
        scratch_shapes=[pltpu.VMEM((rows, width + 32, cb), F32), pltpu.VMEM((rows + 2 * half_k, width, cb), F32)],
        compiler_params=_params())(pa, pa, conv_w, conv_b)


def _conv_bwd(pa, da1, conv_w, nb, s, cb):
    ktaps, d = conv_w.shape
    rows, width = s // GRID_W, GRID_W
    half_k = ktaps // 2
    nblk = d // cb
    nh = nblk // 2

    def body(gv_ref, gg_ref, da_ref, w_ref, dgv_ref, dgg_ref, dw_ref, db_ref, pha_ref, phd_ref, pva_ref, pvd_ref):
        j = pl.program_id(0)
        b = pl.program_id(1)
        gv = gv_ref[...]
        sg = _sigmoid(gg_ref[...])
        a0 = (gv * sg).reshape(rows, width, cb)
        da1v = da_ref[...]
        d3 = da1v.reshape(rows, width, cb)
        w = w_ref[...]

        @pl.when(b == 0)
        def _():
            dw_ref[...] = jnp.zeros_like(dw_ref)
            db_ref[...] = jnp.zeros_like(db_ref)

        db_ref[...] += jnp.sum(da1v, axis=0, keepdims=True)

        def finish(da0_3):
            da0 = da0_3.reshape(s, cb)
            dgv_ref[...] = (da0 * sg).astype(BF16)
            dgg_ref[...] = (da0 * gv * sg * (1.0 - sg)).astype(BF16)

        @pl.when(j < nh)
        def _():
            for ref, val in ((pha_ref, a0), (phd_ref, d3)):
                ref[:, 0:16, :] = jnp.zeros((rows, 16, cb), F32)
                ref[:, 16 + width:32 + width, :] = jnp.zeros((rows, 16, cb), F32)
                ref[:, 16:16 + width, :] = val
            finish(_conv_taps(phd_ref, w, ktaps, rows, width, True, True))
            for t in range(ktaps):
                win = pha_ref[:, pl.ds(16 + t - half_k, width), :]
                dw_ref[t:t + 1, :] += jnp.sum(jnp.sum(win * d3, axis=0), axis=0, keepdims=True)

        @pl.when(j >= nh)
        def _():
            for ref, val in ((pva_ref, a0), (pvd_ref, d3)):
                ref[0:half_k, :, :] = jnp.zeros((half_k, width, cb), F32)
                ref[half_k + rows:2 * half_k + rows, :, :] = jnp.zeros((half_k, width, cb), F32)
                ref[half_k:half_k + rows, :, :] = val
            finish(_conv_taps(pvd_ref, w, ktaps, rows, width, False, True))
            for t in range(ktaps):
                win = pva_ref[pl.ds(t, rows), :, :]
                dw_ref[t:t + 1, :] += jnp.sum(jnp.sum(win * d3, axis=0), axis=0, keepdims=True)

    return pl.pallas_call(
        body, name="conv_bwd", grid=(nblk, nb),
        in_specs=[pl.BlockSpec((s, cb), lambda j, b: (b, j)),
                  pl.BlockSpec((s, cb), lambda j, b: (b, nblk + j)),
                  pl.BlockSpec((s, cb), lambda j, b: (b, j)),
                  pl.BlockSpec((ktaps, cb), lambda j, b: (0, j))],
        out_specs=(pl.BlockSpec((s, cb), lambda j, b: (b, j)),
                   pl.BlockSpec((s, cb), lambda j, b: (b, j)),
                   pl.BlockSpec((ktaps, cb), lambda j, b: (0, j)),
                   pl.BlockSpec((1, cb), lambda j, b: (0, j))),
        out_shape=(jax.ShapeDtypeStruct((nb * s, d), BF16), jax.ShapeDtypeStruct((nb * s, d), BF16),
                   jax.ShapeDtypeStruct((ktaps, d), F32), jax.ShapeDtypeStruct((1, d), F32)),
        scratch_shapes=[pltpu.VMEM((rows, width + 32, cb), F32), pltpu.VMEM((rows, width + 32, cb), F32),
                        pltpu.VMEM((rows + 2 * half_k, width, cb), F32),
                        pltpu.VMEM((rows + 2 * half_k, width, cb), F32)],
        compiler_params=_params())(pa, pa, da1, conv_w)


def _log_sigmoid(x):
    return jnp.minimum(x, 0.0) - jnp.log(1.0 + jnp.exp(-jnp.abs(x)))


def _decay_fwd(pb, up2, bias2, tm, lr_blk):
    t_all = pb.shape[0]
    n2 = up2.shape[1]

    def body(lr_ref, up_ref, b_ref, g_ref):
        logits = _mm_hi(lr_ref[...], up_ref[...]) + b_ref[...]
        g_ref[...] = _log_sigmoid(logits) * (1.0 / GATE_TAU)

    return pl.pallas_call(
        body, name="decay_fwd", grid=(t_all // tm,),
        in_specs=[pl.BlockSpec((tm, LANE), lambda i: (i, lr_blk)),
                  pl.BlockSpec(up2.shape, lambda i: (0, 0)),
                  pl.BlockSpec((1, n2), lambda i: (0, 0))],
        out_specs=pl.BlockSpec((tm, n2), lambda i: (i, 0)),
        out_shape=jax.ShapeDtypeStruct((t_all, n2), F32),
        compiler_params=_params())(pb, up2, bias2)


def _decay_bwd(pb, up2, bias2, grads_f, grads_b, tm, lr_blk, dk_, dv_):
    t_all = pb.shape[0]
    n2 = up2.shape[1]
    nbw = 2 * dk_ + dv_ + LANE

    def body(lr_ref, up_ref, b_ref, dqf, dkf, dvf, dgf, dqb, dkb, dvb, dgb, dp_ref, dup_ref, dbias_ref):
        i = pl.program_id(0)

        @pl.when(i == 0)
        def _():
            dup_ref[...] = jnp.zeros_like(dup_ref)
            dbias_ref[...] = jnp.zeros_like(dbias_ref)

        lr = lr_ref[...]
        up = up_ref[...]
        logits = _mm_hi(lr, up) + b_ref[...]
        dg = jnp.concatenate([dgf[...], dgb[...]], axis=1)
        dlog = dg * (1.0 / GATE_TAU) * _sigmoid(-logits)
        dup_ref[...] += _mm_tn_hi(lr, dlog)
        dbias_ref[...] += jnp.sum(dlog, axis=0, keepdims=True)
        dp_ref[:, 0:dk_] = (dqf[...] + dqb[...]).astype(BF16)
        dp_ref[:, dk_:2 * dk_] = (dkf[...] + dkb[...]).astype(BF16)
        dp_ref[:, 2 * dk_:2 * dk_ + dv_] = (dvf[...] + dvb[...]).astype(BF16)
        dp_ref[:, 2 * dk_ + dv_:nbw] = _mm_nt_hi(dlog, up).astype(BF16)

    row = lambda w: pl.BlockSpec((tm, w), lambda i: (i, 0))
    return pl.pallas_call(
        body, name="decay_bwd", grid=(t_all // tm,),
        in_specs=[pl.BlockSpec((tm, LANE), lambda i: (i, lr_blk)),
                  pl.BlockSpec(up2.shape, lambda i: (0, 0)),
                  pl.BlockSpec((1, n2), lambda i: (0, 0)),
                  row(dk_), row(dk_), row(dv_), row(dk_), row(dk_), row(dk_), row(dv_), row(dk_)],
        out_specs=(row(nbw), pl.BlockSpec(up2.shape, lambda i: (0, 0)), pl.BlockSpec((1, n2), lambda i: (0, 0))),
        out_shape=(jax.ShapeDtypeStruct((t_all, nbw), BF16), jax.ShapeDtypeStruct(up2.shape, F32),
                   jax.ShapeDtypeStruct((1, n2), F32)),
        compiler_params=_params())(pb, up2, bias2, *grads_f, *grads_b)


def _scan_chunk_index(nb, nl, nc, rev):
    base_ctx = nb * nl

    def idx(b, s):
        ctx_i = (nc - 1 - s) if rev else s
        lat_i = (nl - 1 - (s - nc)) if rev else (s - nc)
        return jnp.where(s < nc, base_ctx + b * nc + ctx_i, b * nl + lat_i)

    return idx


def _chunk_terms(q, k, g, cm, far, mid):
    b = _mm_hi(cm, g)
    bf, bm = b[far:far + 1, :], b[mid:mid + 1, :]
    e = jnp.exp(b)
    em = jnp.exp(b - bm)
    eim = jnp.exp(bm - b)
    ed = jnp.exp(bf - b)
    return dict(e=e, em=em, eim=eim, ed=ed, dec=jnp.exp(bf), qe=q * e, qem=q * em, kim=k * eim, kd=k * ed)


def _gla_fwd(pb, g_all, nb, s_len, c_len, dk_, dv_, rev):
    c = CHUNK
    nl, nc = s_len // c, c_len // c
    ns = nl + nc
    hk, hv = dk_ // HEADS, dv_ // HEADS
    t_all = pb.shape[0]
    far, mid = (0, c // 2) if rev else (c - 1, c // 2)
    scale = hk ** -0.5
    cidx = _scan_chunk_index(nb, nl, nc, rev)
    gcol = 1 if rev else 0

    def body(q_ref, k_ref, v_ref, g_ref, o_ref, zs_ref, z_scr):
        s = pl.program_id(1)

        @pl.when(s == 0)
        def _():
            z_scr[...] = jnp.zeros_like(z_scr)

        ii = lax.broadcasted_iota(jnp.int32, (c, c), 0)
        jj = lax.broadcasted_iota(jnp.int32, (c, c), 1)
        mask = (ii <= jj) if rev else (ii >= jj)
        cm = mask.astype(F32)
        qs = jnp.where(s >= nc, scale, 0.0)
        for h in range(HEADS):
            ks, vs = slice(h * hk, (h + 1) * hk), slice(h * hv, (h + 1) * hv)
            q = q_ref[:, ks] * qs
            k = k_ref[:, ks]
            v = v_ref[:, vs]
            t = _chunk_terms(q, k, g_ref[:, ks], cm, far, mid)
            a = jnp.where(mask, _mm_nt(t["qem"], t["kim"]), 0.0)
            z = z_scr[h]
            zs_ref[0, 0, h] = z
            o_ref[:, vs] = _mm(a, v) + _mm_nt(t["qe"], z)
            z_scr[h] = z * t["dec"] + _mm_tn(v, t["kd"])

    return pl.pallas_call(
        body, name="gla_fwd_rev" if rev else "gla_fwd", grid=(nb, ns),
        in_specs=[pl.BlockSpec((c, dk_), lambda b, s: (cidx(b, s), 0)),
                  pl.BlockSpec((c, dk_), lambda b, s: (cidx(b, s), 1)),
                  pl.BlockSpec((c, dv_), lambda b, s: (cidx(b, s), 1)),
                  pl.BlockSpec((c, dk_), lambda b, s: (cidx(b, s), gcol))],
        out_specs=(pl.BlockSpec((c, dv_), lambda b, s: (cidx(b, s), 0)),
                   pl.BlockSpec((1, 1, HEADS, hv, hk), lambda b, s: (b, s, 0, 0, 0))),
        out_shape=(jax.ShapeDtypeStruct((t_all, dv_), F32), jax.ShapeDtypeStruct((nb, ns, HEADS, hv, hk), F32)),
        scratch_shapes=[pltpu.VMEM((HEADS, hv, hk), F32)],
        compiler_params=_params())(pb, pb, pb, g_all)


def _gla_bwd(pb, g_all, do, zs, nb, s_len, c_len, dk_, dv_, rev):
    c = CHUNK
    nl, nc = s_len // c, c_len // c
    ns = nl + nc
    hk, hv = dk_ // HEADS, dv_ // HEADS
    t_all = pb.shape[0]
    far, mid = (0, c // 2) if rev else (c - 1, c // 2)
    scale = hk ** -0.5
    cidx_f = _scan_chunk_index(nb, nl, nc, rev)
    cidx = lambda b, s: cidx_f(b, ns - 1 - s)
    gcol = 1 if rev else 0

    def do_idx(b, s):
        return jnp.minimum(cidx(b, s), nb * nl - 1)

    def body(q_ref, k_ref, v_ref, g_ref, do_ref, zs_ref, dq_ref, dk_ref, dv_ref, dg_ref, dz_scr):
        s = pl.program_id(1)
        step = ns - 1 - s

        @pl.when(s == 0)
        def _():
            dz_scr[...] = jnp.zeros_like(dz_scr)

        ii = lax.broadcasted_iota(jnp.int32, (c, c), 0)
        jj = lax.broadcasted_iota(jnp.int32, (c, c), 1)
        mask = (ii <= jj) if rev else (ii >= jj)
        mask_t = (ii >= jj) if rev else (ii <= jj)
        cm, cm_t = mask.astype(F32), mask_t.astype(F32)
        lat = step >= nc
        qs = jnp.where(lat, scale, 0.0)
        dmul = jnp.where(lat, 1.0, 0.0)
        far_row = lax.broadcasted_iota(jnp.int32, (c, hk), 0) == far
        for h in range(HEADS):
            ks, vs = slice(h * hk, (h + 1) * hk), slice(h * hv, (h + 1) * hv)
            q = q_ref[:, ks] * qs
            k = k_ref[:, ks]
            v = v_ref[:, vs]
            d_o = do_ref[:, vs] * dmul
            t = _chunk_terms(q, k, g_ref[:, ks], cm, far, mid)
            qem, kim, qe, kd = t["qem"], t["kim"], t["qe"], t["kd"]
            a_t = jnp.where(mask_t, _mm_nt(kim, qem), 0.0)
            d_a = jnp.where(mask, _mm_nt(d_o, v), 0.0)
            d_at = jnp.where(mask_t, _mm_nt(v, d_o), 0.0)
            z = zs_ref[0, 0, h]
            dzn = dz_scr[h]
            dv_ref[:, vs] = _mm(a_t, d_o) + _mm_nt(kd, dzn)
            dqem = _mm(d_a, kim)
            dkim = _mm(d_at, qem)
            dqe = _mm(d_o, z)
            dkd = _mm(v, dzn)
            ddec = jnp.sum(z * dzn, axis=0, keepdims=True)
            dz_scr[h] = dzn * t["dec"] + _mm_tn(d_o, qe)
            dq_ref[:, ks] = (dqem * t["em"] + dqe * t["e"]) * qs
            dk_ref[:, ks] = dkim * t["eim"] + dkd * t["ed"]
            db = dqem * qem - dkim * kim + dqe * qe - dkd * kd
            extra = jnp.sum(dkd * kd, axis=0, keepdims=True) + ddec * t["dec"]
            db = db + jnp.where(far_row, extra, 0.0)
            dg_ref[:, ks] = _mm_hi(cm_t, db)

    return pl.pallas_call(
        body, name="gla_bwd_rev" if rev else "gla_bwd", grid=(nb, ns),
        in_specs=[pl.BlockSpec((c, dk_), lambda b, s: (cidx(b, s), 0)),
                  pl.BlockSpec((c, dk_), lambda b, s: (cidx(b, s), 1)),
                  pl.BlockSpec((c, dv_), lambda b, s: (cidx(b, s), 1)),
                  pl.BlockSpec((c, dk_), lambda b, s: (cidx(b, s), gcol)),
                  pl.BlockSpec((c, dv_), lambda b, s: (do_idx(b, s), 0)),
                  pl.BlockSpec((1, 1, HEADS, hv, hk), lambda b, s: (b, ns - 1 - s, 0, 0, 0))],
        out_specs=(pl.BlockSpec((c, dk_), lambda b, s: (cidx(b, s), 0)),
                   pl.BlockSpec((c, dk_), lambda b, s: (cidx(b, s), 0)),
                   pl.BlockSpec((c, dv_), lambda b, s: (cidx(b, s), 0)),
                   pl.BlockSpec((c, dk_), lambda b, s: (cidx(b, s), 0))),
        out_shape=(jax.ShapeDtypeStruct((t_all, dk_), F32), jax.ShapeDtypeStruct((t_all, dk_), F32),
                   jax.ShapeDtypeStruct((t_all, dv_), F32), jax.ShapeDtypeStruct((t_all, dk_), F32)),
        scratch_shapes=[pltpu.VMEM((HEADS, hv, hk), F32)],
        compiler_params=_params())(pb, pb, pb, g_all, do, zs)


def _tail(a1, pa, o_f, o_b, x2, tgt, mod, wc, wg, wo, ln_g, ln_b, gn_t, fg, nb, tm):
    tl, d = x2.shape
    nt = tl // tm
    per_ex = nt // nb
    hv = d // HEADS
    nrow = mod.shape[0]

    def body(a1_ref, z_ref, r_ref, mc_ref, mg_ref, of_ref, ob_ref, x_ref, t_ref, mod_ref, wc_ref, wg_ref, wo_ref,
             lng_ref, lnb_ref, gn_ref, fg_ref,
             dp_ref, da1_ref, do_ref, gx_ref, mrg_ref, dmo_ref, yci_ref, dyc_ref, ogi_ref, dyg_ref, sm_ref):
        i = pl.program_id(0)

        @pl.when(i == 0)
        def _():
            sm_ref[...] = jnp.zeros_like(sm_ref)

        bidx = i // per_ex
        gate = _rowsel(mod_ref[...], bidx, nb)[:, 2 * d:3 * d]
        lng, lnb, gn, fgv = lng_ref[...], lnb_ref[...], gn_ref[...], fg_ref[...]
        wc_, wg_, wo_ = wc_ref[...], wg_ref[...], wo_ref[...]

        a1v = a1_ref[...]
        mu = jnp.mean(a1v, axis=-1, keepdims=True)
        xc = a1v - mu
        rs = lax.rsqrt(jnp.mean(xc * xc, axis=-1, keepdims=True) + EPS)
        xh = xc * rs
        a2 = xh * lng + lnb
        s2 = _sigmoid(a2)
        a3 = a2 * s2
        zv = z_ref[...]
        sz = _sigmoid(zv)
        siluz = zv * sz
        ycin = a3 * siluz
        yconv = _mm(ycin, wc_)

        o = of_ref[...] + ob_ref[...]
        ohat_parts, rn_parts = [], []
        for h in range(HEADS):
            oh = o[:, h * hv:(h + 1) * hv]
            rn = lax.rsqrt(jnp.mean(oh * oh, axis=-1, keepdims=True) + EPS)
            ohat_parts.append(oh * rn)
            rn_parts.append(rn)
        ohat = jnp.concatenate(ohat_parts, axis=1)
        on = ohat * gn
        rv = r_ref[...]
        sr = _sigmoid(rv)
        silur = rv * sr
        ogin = on * silur
        ygla = _mm(ogin, wg_)

        sc = _sigmoid(mc_ref[...])
        sg = _sigmoid(mg_ref[...])
        merged = sc * yconv + sg * ygla
        mo = _mm(merged, wo_)
        hn = x_ref[...] + gate * mo
        rf = lax.rsqrt(jnp.mean(hn * hn, axis=-1, keepdims=True) + EPS)
        yh = hn * rf
        err = yh * fgv - t_ref[...]
        loss_part = 0.5 * jnp.sum(err * err) * (1.0 / d)

        dy = err * (1.0 / d)
        dfg = jnp.sum(dy * yh, axis=0, keepdims=True)
        dyh = dy * fgv
        dhn = rf * (dyh - yh * jnp.mean(dyh * yh, axis=-1, keepdims=True))
        gx_ref[...] = dhn
        dgate = jnp.sum(dhn * mo, axis=0, keepdims=True)
        dmo = gate * dhn
        dmerged = _mm_nt(dmo, wo_)
        dyconv = dmerged * sc
        dygla = dmerged * sg
        dp_ref[:, 2 * d:3 * d] = (dmerged * yconv * sc * (1.0 - sc)).astype(BF16)
        dp_ref[:, 3 * d:4 * d] = (dmerged * ygla * sg * (1.0 - sg)).astype(BF16)
        dycin = _mm_nt(dyconv, wc_)
        dogin = _mm_nt(dygla, wg_)
        mrg_ref[...] = merged.astype(BF16)
        dmo_ref[...] = dmo.astype(BF16)
        yci_ref[...] = ycin.astype(BF16)
        dyc_ref[...] = dyconv.astype(BF16)
        ogi_ref[...] = ogin.astype(BF16)
        dyg_ref[...] = dygla.astype(BF16)

        da3 = dycin * siluz
        dp_ref[:, 0:d] = (dycin * a3 * _dsilu(zv, sz)).astype(BF16)
        da2 = da3 * _dsilu(a2, s2)
        dlng = jnp.sum(da2 * xh, axis=0, keepdims=True)
        dlnb = jnp.sum(da2, axis=0, keepdims=True)
        dxh = da2 * lng
        da1_ref[...] = rs * (dxh - jnp.mean(dxh, axis=-1, keepdims=True)
                             - xh * jnp.mean(dxh * xh, axis=-1, keepdims=True))

        don = dogin * silur
        dp_ref[:, d:2 * d] = (dogin * on * _dsilu(rv, sr)).astype(BF16)
        dgn = jnp.sum(don * ohat, axis=0, keepdims=True)
        dyn = don * gn
        for h in range(HEADS):
            vs = slice(h * hv, (h + 1) * hv)
            oh_hat = ohat_parts[h]
            dh = dyn[:, vs]
            do_ref[:, vs] = rn_parts[h] * (dh - oh_hat * jnp.mean(dh * oh_hat, axis=-1, keepdims=True))

        sm_ref[0:1, :] += dfg
        sm_ref[1:2, :] += dlng
        sm_ref[2:3, :] += dlnb
        sm_ref[3:4, :] += dgn
        sm_ref[4:5, :] += jnp.zeros((1, d), F32) + loss_part
        for b in range(nb):
            sm_ref[8 + b:9 + b, :] += jnp.where(bidx == b, dgate, 0.0)

    row = pl.BlockSpec((tm, d), lambda i: (i, 0))
    pcol = lambda blk: pl.BlockSpec((tm, d), lambda i: (i, blk))
    full = lambda arr: pl.BlockSpec(arr.shape, lambda i: (0,) * arr.ndim)
    bfo = jax.ShapeDtypeStruct((tl, d), BF16)
    f32o = jax.ShapeDtypeStruct((tl, d), F32)
    return pl.pallas_call(
        body, name="tail", grid=(nt,),
        in_specs=[row, pcol(2), pcol(3), pcol(4), pcol(5), row, row, row, row, full(mod), full(wc), full(wg),
                  full(wo), full(ln_g), full(ln_b), full(gn_t), full(fg)],
        out_specs=(pl.BlockSpec((tm, 4 * d), lambda i: (i, 0)), row, row, row, row, row, row, row, row, row,
                   pl.BlockSpec((16, d), lambda i: (0, 0))),
        out_shape=(jax.ShapeDtypeStruct((tl, 4 * d), BF16), f32o, f32o, f32o, bfo, bfo, bfo, bfo, bfo, bfo,
                   jax.ShapeDtypeStruct((16, d), F32)),
        compiler_params=_params())(a1, pa, pa, pa, pa, o_f, o_b, x2, tgt, mod, wc, wg, wo, ln_g, ln_b, gn_t, fg)


def _local_step(x, c, ctx, tgt, c_ctx, ada_w, ada_b, norm_g, w_a, b_a, w_b, b_b, conv_w, conv_b, ln_g, ln_b,
                conv_proj, up2, bias2, gla_norm_g, gla_proj, w_out, final_norm_g):
    nb, s_len, d = x.shape
    c_len = ctx.shape[1]
    dk_, dv_ = d // 2, d
    tl, tc = nb * s_len, nb * c_len
    tm = math.gcd(256, c_len)
    x2, ctx2, tgt2 = x.reshape(tl, d), ctx.reshape(tc, d), tgt.reshape(tl, d)

    cv = jnp.zeros((8, d), F32).at[0:nb].set(c).at[nb].set(c_ctx.reshape(d))
    mod = _ada_fwd(cv, ada_w, ada_b)
    u = _norm_fwd(x2, ctx2, mod, norm_g, nb, tm)
    tmm = math.gcd(512, tl, tc)
    pa = _matmul_bias("inproj_a", u, w_a, b_a, tl, tmm, _tile(6 * d, 1536))
    pb = _matmul_bias("inproj_b", u, w_b, b_b, tl + tc, tmm, 2 * d + LANE)

    cb = min(LANE, d // 2)
    a1 = _conv_fwd(pa, conv_w, conv_b, nb, s_len, cb)
    lr_blk = (2 * dk_ + dv_) // LANE
    g_all = _decay_fwd(pb, up2, bias2, tm, lr_blk)
    o_f, zs_f = _gla_fwd(pb, g_all, nb, s_len, c_len, dk_, dv_, False)
    o_b, zs_b = _gla_fwd(pb, g_all, nb, s_len, c_len, dk_, dv_, True)

    gn_t = jnp.tile(gla_norm_g, (1, HEADS))
    tt = math.gcd(128, s_len)
    (dp_a2, da1, d_o, gx1, merged, dmo, ycin, dyconv, ogin, dygla, small) = _tail(
        a1, pa, o_f, o_b, x2, tgt2, mod, conv_proj, gla_proj, w_out, ln_g, ln_b, gn_t, final_norm_g, nb, tt)

    tk = math.gcd(512, tl)
    d_w_out, _ = _matmul_tn("dw_out", merged, dmo, tl, tk, _tile(d, 1024))
    d_conv_proj, _ = _matmul_tn("dw_conv_proj", ycin, dyconv, tl, tk, _tile(d, 1024))
    d_gla_proj, _ = _matmul_tn("dw_gla_proj", ogin, dygla, tl, tk, _tile(d, 1024))

    d_gv, d_gg, d_conv_w, d_conv_b = _conv_bwd(pa, da1, conv_w, nb, s_len, cb)
    dp_a1 = jnp.concatenate([d_gv, d_gg], axis=1)
    grads_f = _gla_bwd(pb, g_all, d_o, zs_f, nb, s_len, c_len, dk_, dv_, False)
    grads_b = _gla_bwd(pb, g_all, d_o, zs_b, nb, s_len, c_len, dk_, dv_, True)
    dp_b, d_up2, d_bias2 = _decay_bwd(pb, up2, bias2, grads_f, grads_b, tm, lr_blk, dk_, dv_)

    tka = _tile(2 * d, 1024)
    du_a1 = _matmul_nt("du_a1", dp_a1, w_a, 0, tmm, tka)
    du_a2 = _matmul_nt("du_a2", dp_a2, w_a, (2 * d) // tka, tmm, tka)
    du_b = _matmul_nt("du_b", dp_b, w_b, 0, tmm, 2 * d + LANE)
    dw_a1, db_a1 = _matmul_tn("dw_a1", u, dp_a1, tl, tk, _tile(2 * d, 1024))
    dw_a2, db_a2 = _matmul_tn("dw_a2", u, dp_a2, tl, tk, _tile(4 * d, 1024))
    tkb = math.gcd(512, tl + tc)
    dw_b, db_b = _matmul_tn("dw_b", u, dp_b, tl + tc, tkb, 2 * d + LANE)

    grad_x2, dmod_ss, d_norm_g = _norm_bwd(x2, ctx2, mod, norm_g, du_a1, du_a2, du_b, gx1, nb, tm)
    dmod = dmod_ss.at[0:nb, 2 * d:3 * d].set(small[8:8 + nb])
    d_ada_w, d_ada_b, d_cv = _ada_bwd(cv, ada_w, dmod)

    return dict(
        loss=small[4, 0], grad_x=grad_x2.reshape(nb, s_len, d), c_ctx=d_cv[nb], ada_w=d_ada_w, ada_b=d_ada_b,
        norm_g=d_norm_g, w_a=jnp.concatenate([dw_a1, dw_a2], axis=1), b_a=jnp.concatenate([db_a1, db_a2], axis=1),
        w_b=dw_b, b_b=db_b, conv_w=d_conv_w, conv_b=d_conv_b, conv_ln_g=small[1:2], conv_ln_b=small[2:3],
        conv_proj=d_conv_proj, up2=d_up2, bias2=d_bias2,
        gla_norm_g=small[3:4].reshape(HEADS, d // HEADS).sum(axis=0, keepdims=True),
        gla_proj=d_gla_proj, w_out=d_w_out, final_norm_g=small[0:1])


def _regroup(o, d, r):
    a = jnp.concatenate([o[..., 0:3 * d], o[..., 5 * d + 2 * r:8 * d + 2 * r]], axis=-1)
    pad = jnp.zeros(o.shape[:-1] + (LANE - 2 * r,), o.dtype)
    b = jnp.concatenate([o[..., 3 * d:5 * d + 2 * r], pad], axis=-1)
    return a, b


def _ungroup(a, b, d, r):
    return jnp.concatenate([a[..., 0:3 * d], b[..., 0:2 * d + 2 * r], a[..., 3 * d:6 * d]], axis=-1)


def _mesh_pos():
    return lax.axis_index("x"), lax.axis_index("y"), lax.axis_index("c")


def _all_gather(arrs):
    n = len(arrs)

    def body(*refs):
        ins, outs = refs[:n], refs[n:2 * n]
        send_sems, recv_sems, local_sems = refs[2 * n:]
        x, y, c = _mesh_pos()
        me, sibling = (x, y, c), (x, y, 1 - c)
        chips = [(1 - x, y), (x, 1 - y), (1 - x, 1 - y)]

        def slot(a, pos):
            return outs[a].at[4 * pos[0] + 2 * pos[1] + pos[2]]

        def copy(a, k, block, to, src=None):
            return pltpu.make_async_remote_copy(
                src_ref=slot(a, block) if src is None else src, dst_ref=slot(a, block),
                send_sem=send_sems.at[7 * a + k], recv_sem=recv_sems.at[7 * a + k],
                device_id=to, device_id_type=MESH)

        mine = [pltpu.make_async_copy(ins[a], slot(a, me), local_sems.at[a]) for a in range(n)]
        for cp in mine:
            cp.start()
        first = []
        for a in range(n):
            first.append(copy(a, 0, me, sibling, src=ins[a]))
            first += [copy(a, 1 + j, me, (*chip, c), src=ins[a]) for j, chip in enumerate(chips)]
        for cp in first:
            cp.start()
        passed = []
        for j, chip in enumerate(chips):
            for a in range(n):
                copy(a, 1 + j, (*chip, c), me).wait_recv()
                fwd = copy(a, 4 + j, (*chip, c), sibling)
                fwd.start()
                passed.append(fwd)
        for a in range(n):
            copy(a, 0, sibling, me).wait_recv()
            for j, chip in enumerate(chips):
                copy(a, 4 + j, (*chip, 1 - c), me).wait_recv()
        for cp in first + passed:
            cp.wait_send()
        for cp in mine:
            cp.wait()

    anyspec = pl.BlockSpec(memory_space=pl.ANY)
    return pl.pallas_call(
        body, name="all_gather",
        out_shape=tuple(jax.ShapeDtypeStruct((N_DEV,) + a.shape, a.dtype) for a in arrs),
        in_specs=[anyspec] * n, out_specs=tuple([anyspec] * n),
        scratch_shapes=[pltpu.SemaphoreType.DMA((7 * n,)), pltpu.SemaphoreType.DMA((7 * n,)),
                        pltpu.SemaphoreType.DMA((n,))],
    )(*arrs)


def _exchange(arrs):
    n = len(arrs)

    def body(*refs):
        ins, outs = refs[:n], refs[n:2 * n]
        send_sems, recv_sems, local_sems = refs[2 * n:]
        x, y, c = _mesh_pos()
        me_i = 4 * x + 2 * y + c
        mine = [pltpu.make_async_copy(ins[a].at[me_i], outs[a].at[me_i], local_sems.at[a]) for a in range(n)]
        for cp in mine:
            cp.start()
        copies = []
        for rel in range(1, N_DEV):
            px = 1 - x if rel & 4 else x
            py = 1 - y if rel & 2 else y
            pc = 1 - c if rel & 1 else c
            peer_i = 4 * px + 2 * py + pc
            for a in range(n):
                copies.append(pltpu.make_async_remote_copy(
                    src_ref=ins[a].at[peer_i], dst_ref=outs[a].at[me_i],
                    send_sem=send_sems.at[7 * a + rel - 1], recv_sem=recv_sems.at[7 * a + rel - 1],
                    device_id=(px, py, pc), device_id_type=MESH))
        for cp in copies:
            cp.start()
        for cp in copies:
            cp.wait_recv()
        for cp in copies:
            cp.wait_send()
        for cp in mine:
            cp.wait()

    anyspec = pl.BlockSpec(memory_space=pl.ANY)
    return pl.pallas_call(
        body, name="grad_exchange",
        out_shape=tuple(jax.ShapeDtypeStruct(a.shape, a.dtype) for a in arrs),
        in_specs=[anyspec] * n, out_specs=tuple([anyspec] * n),
        scratch_shapes=[pltpu.SemaphoreType.DMA((7 * n,)), pltpu.SemaphoreType.DMA((7 * n,)),
                        pltpu.SemaphoreType.DMA((n,))],
    )(*arrs)


def _sum_adam(name, parts, w, m, v):
    r, cdim = w.shape
    tr = r if (r % 8 or r <= 256) else math.gcd(r, 256)
    bc1 = 1.0 - ADAM_B1 ** ADAM_STEP
    bc2 = 1.0 - ADAM_B2 ** ADAM_STEP

    def body(p_ref, w_ref, m_ref, v_ref, g_ref, d_ref, nm_ref, nv_ref):
        g = p_ref[0].astype(F32)
        for k in range(1, N_DEV):
            g = g + p_ref[k].astype(F32)
        mn = ADAM_B1 * m_ref[...] + (1.0 - ADAM_B1) * g
        vn = ADAM_B2 * v_ref[...] + (1.0 - ADAM_B2) * (g * g)
        g_ref[...] = g
        nm_ref[...] = mn
        nv_ref[...] = vn
        d_ref[...] = -ADAM_LR * ((mn / bc1) / (jnp.sqrt(vn / bc2) + ADAM_EPS) + ADAM_WD * w_ref[...])

    blk = pl.BlockSpec((tr, cdim), lambda i: (i, 0))
    o = jax.ShapeDtypeStruct((r, cdim), F32)
    return pl.pallas_call(
        body, name=name, grid=(r // tr,),
        in_specs=[pl.BlockSpec((N_DEV, tr, cdim), lambda i: (0, i, 0)), blk, blk, blk],
        out_specs=(blk, blk, blk, blk), out_shape=(o, o, o, o),
        compiler_params=_params())(parts, w, m, v)


_SMALL = ("c_ctx", "ada_b", "norm_g", "b_in", "conv_b", "conv_ln_g", "conv_ln_b", "decay_bias_fwd",
          "decay_bias_bwd", "gla_norm_g", "final_norm_g")
_BIG = ("ada_w", "w_in", "conv_w", "conv_proj", "decay_up_fwd", "decay_up_bwd", "gla_proj", "w_out")
_WEIGHTS = ("c_ctx", "ada_w", "ada_b", "norm_g", "w_in", "b_in", "conv_w", "conv_b", "conv_ln_g", "conv_ln_b",
            "conv_proj", "decay_up_fwd", "decay_bias_fwd", "decay_up_bwd", "decay_bias_bwd", "gla_norm_g",
            "gla_proj", "w_out", "final_norm_g")


def _as2d(a):
    if a.ndim == 1:
        return a.reshape(1, -1)
    return a.reshape(-1, a.shape[-1])


def kernel(x, c, ctx, c_ctx, ada_w, ada_b, norm_g, w_in, b_in, conv_w, conv_b, conv_ln_g, conv_ln_b, conv_proj, decay_up_fwd, decay_bias_fwd, decay_up_bwd, decay_bias_bwd, gla_norm_g, gla_proj, w_out, final_norm_g, loss_target, m_c_ctx, m_ada_w, m_ada_b, m_norm_g, m_w_in, m_b_in, m_conv_w, m_conv_b, m_conv_ln_g, m_conv_ln_b, m_conv_proj, m_decay_up_fwd, m_decay_bias_fwd, m_decay_up_bwd, m_decay_bias_bwd, m_gla_norm_g, m_gla_proj, m_w_out, m_final_norm_g, v_c_ctx, v_ada_w, v_ada_b, v_norm_g, v_w_in, v_b_in, v_conv_w, v_conv_b, v_conv_ln_g, v_conv_ln_b, v_conv_proj, v_decay_up_fwd, v_decay_bias_fwd, v_decay_up_bwd, v_decay_bias_bwd, v_gla_norm_g, v_gla_proj, v_w_out, v_final_norm_g):
    env = dict(locals())
    wts = {k: env[k] for k in _WEIGHTS}
    d = x.shape[-1]
    r = decay_up_fwd.shape[1]
    dk_ = d // 2
    n_in = w_in.shape[-1] * N_DEV

    proj3 = jnp.concatenate([conv_proj[0], gla_proj[0], w_out[0]], axis=0).astype(BF16)
    small_pack = jnp.concatenate([
        jnp.pad(conv_w[0], ((0, 32 - conv_w.shape[1]), (0, 0))),
        jnp.concatenate([decay_up_fwd[0], decay_up_bwd[0]], axis=1)], axis=0)
    g_win, g_ada, g_proj, g_small = _all_gather(
        [w_in[0].astype(BF16), ada_w[0].astype(BF16), proj3, small_pack])

    w_in_full = g_win.transpose(1, 0, 2).reshape(d, n_in)
    w_a, w_b = _regroup(w_in_full, d, r)
    ada_w_full = g_ada.transpose(1, 0, 2).reshape(d, 3 * d)
    ds = d // N_DEV
    conv_proj_full = g_proj[:, 0:ds].reshape(d, d)
    gla_proj_full = g_proj[:, ds:2 * ds].reshape(d, d)
    w_out_full = g_proj[:, 2 * ds:3 * ds].reshape(d, d)
    ktaps = conv_w.shape[1]
    conv_w_full = g_small[:, 0:ktaps].transpose(1, 0, 2).reshape(ktaps, d)
    up_f = g_small[:, 32:32 + r, 0:dk_ // N_DEV].transpose(1, 0, 2).reshape(r, dk_)
    up_b = g_small[:, 32:32 + r, dk_ // N_DEV:].transpose(1, 0, 2).reshape(r, dk_)
    up2 = jnp.zeros((LANE, 2 * dk_), F32).at[0:r, 0:dk_].set(up_f).at[r:2 * r, dk_:].set(up_b)
    bias2 = jnp.concatenate([decay_bias_fwd, decay_bias_bwd], axis=1)
    b_a, b_b = _regroup(b_in, d, r)

    g = _local_step(x, c, ctx, loss_target, c_ctx, ada_w_full, ada_b, norm_g[0:1], w_a, b_a, w_b, b_b,
                    conv_w_full, conv_b, conv_ln_g, conv_ln_b, conv_proj_full, up2, bias2, gla_norm_g,
                    gla_proj_full, w_out_full, final_norm_g.reshape(1, d))

    dw_in = _ungroup(g["w_a"], g["w_b"], d, r)
    e_win = dw_in.reshape(d, N_DEV, n_in // N_DEV).transpose(1, 0, 2).astype(BF16)
    e_ada = g["ada_w"].reshape(d, N_DEV, 3 * d // N_DEV).transpose(1, 0, 2).astype(BF16)
    e_proj = jnp.concatenate([g["conv_proj"].reshape(N_DEV, ds, d), g["gla_proj"].reshape(N_DEV, ds, d),
                              g["w_out"].reshape(N_DEV, ds, d)], axis=1).astype(BF16)
    d_up_f, d_up_b = g["up2"][0:r, 0:dk_], g["up2"][r:2 * r, dk_:]
    e_small = jnp.concatenate([
        jnp.pad(g["conv_w"], ((0, 32 - ktaps), (0, 0))).reshape(32, N_DEV, ds).transpose(1, 0, 2),
        jnp.concatenate([d_up_f.reshape(r, N_DEV, dk_ // N_DEV).transpose(1, 0, 2),
                         d_up_b.reshape(r, N_DEV, dk_ // N_DEV).transpose(1, 0, 2)], axis=2)], axis=1)
    x_win, x_ada, x_proj, x_small = _exchange([e_win, e_ada, e_proj, e_small])

    db_in = _ungroup(g["b_a"], g["b_b"], d, r)
    small_g = dict(c_ctx=g["c_ctx"].reshape(1, d), ada_b=g["ada_b"], norm_g=g["norm_g"], b_in=db_in,
                   conv_b=g["conv_b"], conv_ln_g=g["conv_ln_g"], conv_ln_b=g["conv_ln_b"],
                   decay_bias_fwd=g["bias2"][:, 0:dk_], decay_bias_bwd=g["bias2"][:, dk_:],
                   gla_norm_g=g["gla_norm_g"], final_norm_g=g["final_norm_g"])
    sizes = [wts[k].size for k in _SMALL]
    n_small = sum(sizes) + 1
    n_pad = -n_small % LANE
    pack = lambda parts: jnp.concatenate([p.reshape(1, -1) for p in parts] + [jnp.zeros((1, n_pad + 1), F32)], axis=1)
    gpack = jnp.concatenate([small_g[k].reshape(1, -1) for k in _SMALL]
                            + [g["loss"].reshape(1, 1), jnp.zeros((1, n_pad), F32)], axis=1)
    (gpacks,) = _all_gather([gpack])
    sg, sd, sm, sv = _sum_adam("small_adam", gpacks, pack([wts[k] for k in _SMALL]),
                               pack([env["m_" + k] for k in _SMALL]), pack([env["v_" + k] for k in _SMALL]))

    out = {}
    off = 0
    for k, n in zip(_SMALL, sizes):
        for pre, arr in (("grad_", sg), ("delta_", sd), ("new_m_", sm), ("new_v_", sv)):
            out[pre + k] = arr[0, off:off + n].reshape(wts[k].shape)
        off += n
    loss = sg[0, off]

    def big(name, parts, wname):
        w2 = _as2d(wts[wname])
        res = _sum_adam(name, parts, w2, _as2d(env["m_" + wname]), _as2d(env["v_" + wname]))
        for pre, arr in zip(("grad_", "delta_", "new_m_", "new_v_"), res):
            out[pre + wname] = arr.reshape(wts[wname].shape)

    big("adam_w_in", x_win, "w_in")
    big("adam_ada_w", x_ada, "ada_w")
    big("adam_conv_proj", x_proj[:, 0:ds], "conv_proj")
    big("adam_gla_proj", x_proj[:, ds:2 * ds], "gla_proj")
    big("adam_w_out", x_proj[:, 2 * ds:3 * ds], "w_out")
    big("adam_conv_w", x_small[:, 0:ktaps], "conv_w")
    big("adam_up_f", x_small[:, 32:32 + r, 0:dk_ // N_DEV], "decay_up_fwd")
    big("adam_up_b", x_small[:, 32:32 + r, dk_ // N_DEV:], "decay_up_bwd")

    return (loss, g["grad_x"], *[out["grad_" + k] for k in _WEIGHTS], *[out["delta_" + k] for k in _WEIGHTS],
            *[out["new_m_" + k] for k in _WEIGHTS], *[out["new_v_" + k] for k in _WEIGHTS])
```

```python
import functools
import math

import jax
import jax.numpy as jnp
from jax import lax
from jax.experimental import pallas as pl
from jax.experimental.pallas import tpu as pltpu

F32 = jnp.float32
BF16 = jnp.bfloat16
MESH = pl.DeviceIdType.MESH

N_DEV = 8
GRID_W = 64
CHUNK = 64
HEADS = 4
EPS = 1e-6
GATE_TAU = 16.0
LANE = 128
ADAM_LR, ADAM_B1, ADAM_B2, ADAM_EPS, ADAM_WD, ADAM_STEP = 0.001, 0.9, 0.999, 1e-08, 0.01, 10
VMEM_LIMIT = 56 * 1024 * 1024


def _params(**kw):
    return pltpu.CompilerParams(vmem_limit_bytes=VMEM_LIMIT, **kw)


def _tile(n, pref):
    t = (min(pref, n) // LANE) * LANE
    while t >= LANE:
        if n % t == 0:
            return t
        t -= LANE
    return n


def _mm(a, b):
    return jnp.dot(a.astype(BF16), b.astype(BF16), preferred_element_type=F32)


def _mm_nt(a, b):
    return lax.dot_general(a.astype(BF16), b.astype(BF16), (((1,), (1,)), ((), ())), preferred_element_type=F32)


def _mm_tn(a, b):
    return lax.dot_general(a.astype(BF16), b.astype(BF16), (((0,), (0,)), ((), ())), preferred_element_type=F32)


def _mm_hi(a, b):
    return jnp.dot(a, b, precision=lax.Precision.HIGHEST, preferred_element_type=F32)


def _mm_nt_hi(a, b):
    return lax.dot_general(a, b, (((1,), (1,)), ((), ())), precision=lax.Precision.HIGHEST, preferred_element_type=F32)


def _mm_tn_hi(a, b):
    return lax.dot_general(a, b, (((0,), (0,)), ((), ())), precision=lax.Precision.HIGHEST, preferred_element_type=F32)


def _sigmoid(x):
    return 1.0 / (1.0 + jnp.exp(-x))


def _dsilu(x, s):
    return s * (1.0 + x * (1.0 - s))


def _rowsel(table, idx, n):
    out = table[0:1, :]
    for r in range(1, n):
        out = jnp.where(idx == r, table[r:r + 1, :], out)
    return out


def _ada_fwd(cv, ada_w, ada_b):
    def body(cv_ref, w_ref, b_ref, o_ref):
        c = cv_ref[...]
        o_ref[...] = _mm(c * _sigmoid(c), w_ref[...]) + b_ref[...]

    return pl.pallas_call(body, name="ada_fwd", out_shape=jax.ShapeDtypeStruct((cv.shape[0], ada_w.shape[1]), F32),
                          compiler_params=_params())(cv, ada_w, ada_b)


def _ada_bwd(cv, ada_w, dmod):
    def body(cv_ref, w_ref, dm_ref, dw_ref, db_ref, dc_ref):
        c = cv_ref[...]
        s = _sigmoid(c)
        dm = dm_ref[...]
        dw_ref[...] = _mm_tn_hi(c * s, dm)
        db_ref[...] = jnp.sum(dm, axis=0, keepdims=True)
        dc_ref[...] = _mm_nt(dm, w_ref[...]) * _dsilu(c, s)

    d, n3 = ada_w.shape
    return pl.pallas_call(
        body, name="ada_bwd",
        out_shape=(jax.ShapeDtypeStruct((d, n3), F32), jax.ShapeDtypeStruct((1, n3), F32),
                   jax.ShapeDtypeStruct(cv.shape, F32)),
        compiler_params=_params())(cv, ada_w, dmod)


class _Tiles:
    def __init__(self, nb, s_len, c_len, tm):
        self.nb, self.tm = nb, tm
        self.lat, self.ctx = s_len // tm, c_len // tm
        self.per_ex = self.lat + self.ctx
        self.n_all, self.n_lat = nb * self.per_ex, nb * self.lat

    def is_lat(self, i):
        return i % self.per_ex < self.lat

    def lat_of_all(self, i):
        return (i // self.per_ex) * self.lat + jnp.minimum(i % self.per_ex, self.lat - 1)

    def ctx_of_all(self, i):
        return (i // self.per_ex) * self.ctx + jnp.maximum(i % self.per_ex - self.lat, 0)

    def all_of_lat(self, t):
        return (t // self.lat) * self.per_ex + t % self.lat


def _norm_fwd(x2, ctx2, mod, norm_g, tiles):
    tl, d = x2.shape
    tc = ctx2.shape[0]
    nb, tm = tiles.nb, tiles.tm

    def body(x_ref, c_ref, mod_ref, g_ref, u_ref):
        i = pl.program_id(0)
        lat = tiles.is_lat(i)
        xv = jnp.where(lat, x_ref[...], c_ref[...])
        row = jnp.where(lat, i // tiles.per_ex, nb)
        m = _rowsel(mod_ref[...], row, nb + 1)
        shift, scale = m[:, 0:d], m[:, d:2 * d]
        rstd = lax.rsqrt(jnp.mean(xv * xv, axis=-1, keepdims=True) + EPS)
        u_ref[...] = (xv * rstd * g_ref[...] * (1.0 + scale) + shift).astype(BF16)

    return pl.pallas_call(
        body, name="norm_fwd", grid=(tiles.n_all,),
        in_specs=[pl.BlockSpec((tm, d), lambda i: (tiles.lat_of_all(i), 0)),
                  pl.BlockSpec((tm, d), lambda i: (tiles.ctx_of_all(i), 0)),
                  pl.BlockSpec(mod.shape, lambda i: (0, 0)),
                  pl.BlockSpec((1, d), lambda i: (0, 0))],
        out_specs=pl.BlockSpec((tm, d), lambda i: (i, 0)),
        out_shape=jax.ShapeDtypeStruct((tl + tc, d), BF16),
        compiler_params=_params())(x2, ctx2, mod, norm_g)


def _norm_bwd(x2, ctx2, mod, norm_g, du_lat, du_b, gx1, tiles):
    tl, d = x2.shape
    nb, tm = tiles.nb, tiles.tm
    nrow = mod.shape[0]
    n_lat_in = len(du_lat)

    def body(x_ref, c_ref, mod_ref, g_ref, *refs):
        dl_refs = refs[:n_lat_in]
        d3_ref, gx_ref, gxo_ref, dmod_ref, dg_ref = refs[n_lat_in:]
        i = pl.program_id(0)

        @pl.when(i == 0)
        def _():
            dmod_ref[...] = jnp.zeros_like(dmod_ref)
            dg_ref[...] = jnp.zeros_like(dg_ref)

        lat = tiles.is_lat(i)
        xv = jnp.where(lat, x_ref[...], c_ref[...])
        row = jnp.where(lat, i // tiles.per_ex, nb)
        m = _rowsel(mod_ref[...], row, nb + 1)
        scale = m[:, d:2 * d]
        g = g_ref[...]
        dl = dl_refs[0][...]
        for ref in dl_refs[1:]:
            dl = dl + ref[...]
        du = d3_ref[...] + jnp.where(lat, dl, 0.0)
        rstd = lax.rsqrt(jnp.mean(xv * xv, axis=-1, keepdims=True) + EPS)
        xh = xv * rstd
        dshift = jnp.sum(du, axis=0, keepdims=True)
        dscale = jnp.sum(du * xh * g, axis=0, keepdims=True)
        dxn = du * (1.0 + scale)
        dg_ref[...] += jnp.sum(dxn * xh, axis=0, keepdims=True)
        dxh = dxn * g
        dx = rstd * (dxh - xh * jnp.mean(dxh * xh, axis=-1, keepdims=True))

        @pl.when(lat)
        def _():
            gxo_ref[...] = dx + gx_ref[...]

        for r in range(nb + 1):
            dmod_ref[r:r + 1, 0:d] += jnp.where(row == r, dshift, 0.0)
            dmod_ref[r:r + 1, d:2 * d] += jnp.where(row == r, dscale, 0.0)

    lat_map = lambda i: (tiles.lat_of_all(i), 0)
    lat_spec = pl.BlockSpec((tm, d), lat_map)
    return pl.pallas_call(
        body, name="norm_bwd", grid=(tiles.n_all,),
        in_specs=[lat_spec,
                  pl.BlockSpec((tm, d), lambda i: (tiles.ctx_of_all(i), 0)),
                  pl.BlockSpec(mod.shape, lambda i: (0, 0)),
                  pl.BlockSpec((1, d), lambda i: (0, 0))]
                 + [lat_spec] * n_lat_in
                 + [pl.BlockSpec((tm, d), lambda i: (i, 0)), lat_spec],
        out_specs=(lat_spec,
                   pl.BlockSpec((nrow, 3 * d), lambda i: (0, 0)),
                   pl.BlockSpec((1, d), lambda i: (0, 0))),
        out_shape=(jax.ShapeDtypeStruct((tl, d), F32), jax.ShapeDtypeStruct((nrow, 3 * d), F32),
                   jax.ShapeDtypeStruct((1, d), F32)),
        compiler_params=_params())(x2, ctx2, mod, norm_g, *du_lat, du_b, gx1)


def _matmul_bias(name, u, w, b, rows, tm, tn, u_tile=lambda i: i):
    d, n = w.shape

    def body(u_ref, w_ref, b_ref, o_ref):
        o_ref[...] = jnp.dot(u_ref[...], w_ref[...], preferred_element_type=F32) + b_ref[...]

    return pl.pallas_call(
        body, name=name, grid=(n // tn, rows // tm),
        in_specs=[pl.BlockSpec((tm, d), lambda j, i: (u_tile(i), 0)),
                  pl.BlockSpec((d, tn), lambda j, i: (0, j)),
                  pl.BlockSpec((1, tn), lambda j, i: (0, j))],
        out_specs=pl.BlockSpec((tm, tn), lambda j, i: (i, j)),
        out_shape=jax.ShapeDtypeStruct((rows, n), F32),
        compiler_params=_params())(u, w, b)


def _matmul_nt(name, a, w, koff, tm, tk):
    r, kc = a.shape
    d = w.shape[0]
    nk = kc // tk

    def body(a_ref, w_ref, o_ref):
        k = pl.program_id(1)
        p = lax.dot_general(a_ref[...], w_ref[...], (((1,), (1,)), ((), ())), preferred_element_type=F32)

        @pl.when(k == 0)
        def _():
            o_ref[...] = p

        @pl.when(k > 0)
        def _():
            o_ref[...] += p

    return pl.pallas_call(
        body, name=name, grid=(r // tm, nk),
        in_specs=[pl.BlockSpec((tm, tk), lambda i, k: (i, k)),
                  pl.BlockSpec((d, tk), lambda i, k: (0, koff + k))],
        out_specs=pl.BlockSpec((tm, d), lambda i, k: (i, 0)),
        out_shape=jax.ShapeDtypeStruct((r, d), F32),
        compiler_params=_params())(a, w)


def _matmul_tn(name, a, b, rows, tk, tn, a_tile=lambda k: k):
    m = a.shape[1]
    n = b.shape[1]

    def body(a_ref, b_ref, o_ref, s_ref):
        k = pl.program_id(1)
        bv = b_ref[...]
        p = lax.dot_general(a_ref[...], bv, (((0,), (0,)), ((), ())), preferred_element_type=F32)
        cs = jnp.sum(bv.astype(F32), axis=0, keepdims=True)

        @pl.when(k == 0)
        def _():
            o_ref[...] = p
            s_ref[...] = cs

        @pl.when(k > 0)
        def _():
            o_ref[...] += p
            s_ref[...] += cs

    return pl.pallas_call(
        body, name=name, grid=(n // tn, rows // tk),
        in_specs=[pl.BlockSpec((tk, m), lambda j, k: (a_tile(k), 0)),
                  pl.BlockSpec((tk, tn), lambda j, k: (k, j))],
        out_specs=(pl.BlockSpec((m, tn), lambda j, k: (0, j)), pl.BlockSpec((1, tn), lambda j, k: (0, j))),
        out_shape=(jax.ShapeDtypeStruct((m, n), F32), jax.ShapeDtypeStruct((1, n), F32)),
        compiler_params=_params())(a, b)


def _conv_taps(pad_ref, w, ktaps, rows, width, horizontal, flip):
    half = ktaps // 2
    acc = None
    for t in range(ktaps):
        s = (half - t) if flip else (t - half)
        if horizontal:
            win = pad_ref[:, pl.ds(16 + s, width), :]
        else:
            win = pad_ref[pl.ds(half + s, rows), :, :]
        term = win * w[t:t + 1, :]
        acc = term if acc is None else acc + term
    return acc


def _conv_fwd(pa, conv_w, conv_b, nb, s, cb):
    ktaps, d = conv_w.shape
    rows, width = s // GRID_W, GRID_W
    half_k = ktaps // 2
    nblk = d // cb
    nh = nblk // 2

    def body(glu_ref, w_ref, b_ref, o_ref, ph_ref, pv_ref):
        j = pl.program_id(1)
        a0 = (glu_ref[:, 0:cb] * _sigmoid(glu_ref[:, cb:2 * cb])).reshape(rows, width, cb)
        w = w_ref[...]

        @pl.when(j < nh)
        def _():
            ph_ref[:, 0:16, :] = jnp.zeros((rows, 16, cb), F32)
            ph_ref[:, 16 + width:32 + width, :] = jnp.zeros((rows, 16, cb), F32)
            ph_ref[:, 16:16 + width, :] = a0
            acc = _conv_taps(ph_ref, w, ktaps, rows, width, True, False)
            o_ref[...] = acc.reshape(s, cb) + b_ref[...]

        @pl.when(j >= nh)
        def _():
            pv_ref[0:half_k, :, :] = jnp.zeros((half_k, width, cb), F32)
            pv_ref[half_k + rows:2 * half_k + rows, :, :] = jnp.zeros((half_k, width, cb), F32)
            pv_ref[half_k:half_k + rows, :, :] = a0
            acc = _conv_taps(pv_ref, w, ktaps, rows, width, False, False)
            o_ref[...] = acc.reshape(s, cb) + b_ref[...]

    return pl.pallas_call(
        body, name="conv_fwd", grid=(nb, nblk),
        in_specs=[pl.BlockSpec((s, 2 * cb), lambda b, j: (b, j)),
                  pl.BlockSpec((ktaps, cb), lambda b, j: (0, j)),
                  pl.BlockSpec((1, cb), lambda b, j: (0, j))],
        out_specs=pl.BlockSpec((s, cb), lambda b, j: (b, j)),
        out_shape=jax.ShapeDtypeStruct((nb * s, d), F32),
        scratch_shapes=[pltpu.VMEM((rows, width + 32, cb), F32), pltpu.VMEM((rows + 2 * half_k, width, cb), F32)],
        compiler_params=_params())(pa, conv_w, conv_b)


def _conv_bwd(pa, da1, conv_w, nb, s, cb):
    ktaps, d = conv_w.shape
    rows, width = s // GRID_W, GRID_W
    half_k = ktaps // 2
    nblk = d // cb
    nh = nblk // 2

    def body(glu_ref, da_ref, w_ref, dp_ref, dw_ref, db_ref, pha_ref, phd_ref, pva_ref, pvd_ref):
        j = pl.program_id(0)
        b = pl.program_id(1)
        gv = glu_ref[:, 0:cb]
        sg = _sigmoid(glu_ref[:, cb:2 * cb])
        a0 = (gv * sg).reshape(rows, width, cb)
        da1v = da_ref[...]
        d3 = da1v.reshape(rows, width, cb)
        w = w_ref[...]

        @pl.when(b == 0)
        def _():
            dw_ref[...] = jnp.zeros_like(dw_ref)
            db_ref[...] = jnp.zeros_like(db_ref)

        db_ref[...] += jnp.sum(da1v, axis=0, keepdims=True)

        def finish(da0_3):
            da0 = da0_3.reshape(s, cb)
            dp_ref[:, 0:cb] = (da0 * sg).astype(BF16)
            dp_ref[:, cb:2 * cb] = (da0 * gv * sg * (1.0 - sg)).astype(BF16)

        @pl.when(j < nh)
        def _():
            for ref, val in ((pha_ref, a0), (phd_ref, d3)):
                ref[:, 0:16, :] = jnp.zeros((rows, 16, cb), F32)
                ref[:, 16 + width:32 + width, :] = jnp.zeros((rows, 16, cb), F32)
                ref[:, 16:16 + width, :] = val
            finish(_conv_taps(phd_ref, w, ktaps, rows, width, True, True))
            for t in range(ktaps):
                win = pha_ref[:, pl.ds(16 + t - half_k, width), :]
                dw_ref[t:t + 1, :] += jnp.sum(jnp.sum(win * d3, axis=0), axis=0, keepdims=True)

        @pl.when(j >= nh)
        def _():
            for ref, val in ((pva_ref, a0), (pvd_ref, d3)):
                ref[0:half_k, :, :] = jnp.zeros((half_k, width, cb), F32)
                ref[half_k + rows:2 * half_k + rows, :, :] = jnp.zeros((half_k, width, cb), F32)
                ref[half_k:half_k + rows, :, :] = val
            finish(_conv_taps(pvd_ref, w, ktaps, rows, width, False, True))
            for t in range(ktaps):
                win = pva_ref[pl.ds(t, rows), :, :]
                dw_ref[t:t + 1, :] += jnp.sum(jnp.sum(win * d3, axis=0), axis=0, keepdims=True)

    return pl.pallas_call(
        body, name="conv_bwd", grid=(nblk, nb),
        in_specs=[pl.BlockSpec((s, 2 * cb), lambda j, b: (b, j)),
                  pl.BlockSpec((s, cb), lambda j, b: (b, j)),
                  pl.BlockSpec((ktaps, cb), lambda j, b: (0, j))],
        out_specs=(pl.BlockSpec((s, 2 * cb), lambda j, b: (b, j)),
                   pl.BlockSpec((ktaps, cb), lambda j, b: (0, j)),
                   pl.BlockSpec((1, cb), lambda j, b: (0, j))),
        out_shape=(jax.ShapeDtypeStruct((nb * s, 2 * d), BF16),
                   jax.ShapeDtypeStruct((ktaps, d), F32), jax.ShapeDtypeStruct((1, d), F32)),
        scratch_shapes=[pltpu.VMEM((rows, width + 32, cb), F32), pltpu.VMEM((rows, width + 32, cb), F32),
                        pltpu.VMEM((rows + 2 * half_k, width, cb), F32),
                        pltpu.VMEM((rows + 2 * half_k, width, cb), F32)],
        compiler_params=_params())(pa, da1, conv_w)


def _log_sigmoid(x):
    return jnp.minimum(x, 0.0) - jnp.log(1.0 + jnp.exp(-jnp.abs(x)))


def _decay_fwd(pb, up2, bias2, tm, lr_blk):
    t_all = pb.shape[0]
    n2 = up2.shape[1]

    def body(lr_ref, up_ref, b_ref, g_ref):
        logits = _mm_hi(lr_ref[...], up_ref[...]) + b_ref[...]
        g_ref[...] = _log_sigmoid(logits) * (1.0 / GATE_TAU)

    return pl.pallas_call(
        body, name="decay_fwd", grid=(t_all // tm,),
        in_specs=[pl.BlockSpec((tm, LANE), lambda i: (i, lr_blk)),
                  pl.BlockSpec(up2.shape, lambda i: (0, 0)),
                  pl.BlockSpec((1, n2), lambda i: (0, 0))],
        out_specs=pl.BlockSpec((tm, n2), lambda i: (i, 0)),
        out_shape=jax.ShapeDtypeStruct((t_all, n2), F32),
        compiler_params=_params())(pb, up2, bias2)


def _decay_bwd(pb, up2, bias2, grads_f, grads_b, tm, lr_blk, dk_, dv_):
    t_all = pb.shape[0]
    n2 = up2.shape[1]
    nbw = 2 * dk_ + dv_ + LANE

    def body(lr_ref, up_ref, b_ref, dqf, dkf, dvf, dgf, dqb, dkb, dvb, dgb, dp_ref, dup_ref, dbias_ref):
        i = pl.program_id(0)

        @pl.when(i == 0)
        def _():
            dup_ref[...] = jnp.zeros_like(dup_ref)
            dbias_ref[...] = jnp.zeros_like(dbias_ref)

        lr = lr_ref[...]
        up = up_ref[...]
        logits = _mm_hi(lr, up) + b_ref[...]
        dg = jnp.concatenate([dgf[...], dgb[...]], axis=1)
        dlog = dg * (1.0 / GATE_TAU) * _sigmoid(-logits)
        dup_ref[...] += _mm_tn_hi(lr, dlog)
        dbias_ref[...] += jnp.sum(dlog, axis=0, keepdims=True)
        dp_ref[:, 0:dk_] = (dqf[...] + dqb[...]).astype(BF16)
        dp_ref[:, dk_:2 * dk_] = (dkf[...] + dkb[...]).astype(BF16)
        dp_ref[:, 2 * dk_:2 * dk_ + dv_] = (dvf[...] + dvb[...]).astype(BF16)
        dp_ref[:, 2 * dk_ + dv_:nbw] = _mm_nt_hi(dlog, up).astype(BF16)

    row = lambda w: pl.BlockSpec((tm, w), lambda i: (i, 0))
    return pl.pallas_call(
        body, name="decay_bwd", grid=(t_all // tm,),
        in_specs=[pl.BlockSpec((tm, LANE), lambda i: (i, lr_blk)),
                  pl.BlockSpec(up2.shape, lambda i: (0, 0)),
                  pl.BlockSpec((1, n2), lambda i: (0, 0)),
                  row(dk_), row(dk_), row(dv_), row(dk_), row(dk_), row(dk_), row(dv_), row(dk_)],
        out_specs=(row(nbw), pl.BlockSpec(up2.shape, lambda i: (0, 0)), pl.BlockSpec((1, n2), lambda i: (0, 0))),
        out_shape=(jax.ShapeDtypeStruct((t_all, nbw), BF16), jax.ShapeDtypeStruct(up2.shape, F32),
                   jax.ShapeDtypeStruct((1, n2), F32)),
        compiler_params=_params())(pb, up2, bias2, *grads_f, *grads_b)


def _scan_chunk(s, nl, nc, rev):
    if rev:
        return jnp.where(s < nc, nl + (nc - 1 - s), nl - 1 - (s - nc))
    return jnp.where(s < nc, nl + s, s - nc)


def _scan_lat_chunk(s, nl, nc, rev):
    first = nl - 1 if rev else 0
    return jnp.where(s < nc, first, _scan_chunk(s, nl, nc, rev))


def _tri_mm(m_bf, x):
    hi = x.astype(BF16)
    r1 = x - hi.astype(F32)
    mid = r1.astype(BF16)
    lo = (r1 - mid.astype(F32)).astype(BF16)
    dot = lambda p: jnp.dot(m_bf, p, preferred_element_type=F32)
    return dot(hi) + dot(mid) + dot(lo)


def _chunk_masks(c, rev):
    ii = lax.broadcasted_iota(jnp.int32, (c, c), 0)
    jj = lax.broadcasted_iota(jnp.int32, (c, c), 1)
    return ((ii <= jj), (ii >= jj)) if rev else ((ii >= jj), (ii <= jj))


def _chunk_terms(q, k, b, far, mid):
    bf, bm = b[far:far + 1, :], b[mid:mid + 1, :]
    e = jnp.exp(b)
    em = jnp.exp(b - bm)
    eim = jnp.exp(bm - b)
    ed = jnp.exp(bf - b)
    return dict(e=e, em=em, eim=eim, ed=ed, dec=jnp.exp(bf), qe=q * e, qem=q * em, kim=k * eim, kd=k * ed)


def _gla_fwd(pb3, g3, nb, s_len, c_len, dk_, dv_):
    c = CHUNK
    nl, nc = s_len // c, c_len // c
    ns = nl + nc
    hk, hv = dk_ // HEADS, dv_ // HEADS
    l_len = s_len + c_len
    scale = hk ** -0.5
    mid = c // 2

    def body(*refs):
        ins, outs, z_scr = refs[:8], refs[8:14], refs[14]
        s = pl.program_id(0)

        @pl.when(s == 0)
        def _():
            z_scr[...] = jnp.zeros_like(z_scr)

        qs = jnp.where(s >= nc, scale, 0.0)
        for di, rev in enumerate((False, True)):
            q_ref, k_ref, v_ref, g_ref = ins[4 * di:4 * di + 4]
            o_ref, zs_ref, b_ref = outs[3 * di:3 * di + 3]
            mask, _ = _chunk_masks(c, rev)
            m_bf = mask.astype(BF16)
            far = 0 if rev else c - 1
            for b in range(nb):
                bc = _tri_mm(m_bf, g_ref[b])
                b_ref[b] = bc
                for h in range(HEADS):
                    ks, vs = slice(h * hk, (h + 1) * hk), slice(h * hv, (h + 1) * hv)
                    zi = (di * nb + b) * HEADS + h
                    v = v_ref[b, :, vs]
                    t = _chunk_terms(q_ref[b, :, ks] * qs, k_ref[b, :, ks], bc[:, ks], far, mid)
                    a = jnp.where(mask, _mm_nt(t["qem"], t["kim"]), 0.0)
                    z = z_scr[zi]
                    zs_ref[0, b * HEADS + h] = z
                    o_ref[b, :, vs] = _mm(a, v) + _mm_nt(t["qe"], z)
                    z_scr[zi] = z * t["dec"] + _mm_tn(v, t["kd"])

    in_specs, out_specs, out_shape = [], [], []
    for di, rev in enumerate((False, True)):
        ch = functools.partial(_scan_chunk, nl=nl, nc=nc, rev=rev)
        lch = functools.partial(_scan_lat_chunk, nl=nl, nc=nc, rev=rev)
        in_specs += [pl.BlockSpec((nb, c, dk_), lambda s, ch=ch: (0, ch(s), 0)),
                     pl.BlockSpec((nb, c, dk_), lambda s, ch=ch: (0, ch(s), 1)),
                     pl.BlockSpec((nb, c, dv_), lambda s, ch=ch: (0, ch(s), 1)),
                     pl.BlockSpec((nb, c, dk_), lambda s, ch=ch, di=di: (0, ch(s), di))]
        out_specs += [pl.BlockSpec((nb, c, dv_), lambda s, lch=lch: (0, lch(s), 0)),
                      pl.BlockSpec((1, nb * HEADS, hv, hk), lambda s: (s, 0, 0, 0)),
                      pl.BlockSpec((nb, c, dk_), lambda s, ch=ch: (0, ch(s), 0))]
        out_shape += [jax.ShapeDtypeStruct((nb, s_len, dv_), F32),
                      jax.ShapeDtypeStruct((ns, nb * HEADS, hv, hk), F32),
                      jax.ShapeDtypeStruct((nb, l_len, dk_), F32)]
    return pl.pallas_call(
        body, name="gla_fwd", grid=(ns,), in_specs=in_specs, out_specs=tuple(out_specs), out_shape=tuple(out_shape),
        scratch_shapes=[pltpu.VMEM((2 * nb * HEADS, hv, hk), F32)],
        compiler_params=_params())(pb3, pb3, pb3, g3, pb3, pb3, pb3, g3)


def _gla_bwd(pb3, do3, fwd_saved, nb, s_len, c_len, dk_, dv_):
    c = CHUNK
    nl, nc = s_len // c, c_len // c
    ns = nl + nc
    hk, hv = dk_ // HEADS, dv_ // HEADS
    l_len = s_len + c_len
    scale = hk ** -0.5
    mid = c // 2
    zs_f, b_f, zs_b, b_b = fwd_saved

    def body(*refs):
        ins, outs, dz_scr = refs[:12], refs[12:20], refs[20]
        s = pl.program_id(0)
        step = ns - 1 - s

        @pl.when(s == 0)
        def _():
            dz_scr[...] = jnp.zeros_like(dz_scr)

        lat = step >= nc
        qs = jnp.where(lat, scale, 0.0)
        dmul = jnp.where(lat, 1.0, 0.0)
        for di, rev in enumerate((False, True)):
            q_ref, k_ref, v_ref, b_ref, do_ref, zs_ref = ins[6 * di:6 * di + 6]
            dq_ref, dk_ref, dv_ref, dg_ref = outs[4 * di:4 * di + 4]
            mask, mask_t = _chunk_masks(c, rev)
            mt_bf = mask_t.astype(BF16)
            far = 0 if rev else c - 1
            far_row = lax.broadcasted_iota(jnp.int32, (c, hk), 0) == far
            for b in range(nb):
                db_parts = []
                for h in range(HEADS):
                    ks, vs = slice(h * hk, (h + 1) * hk), slice(h * hv, (h + 1) * hv)
                    zi = (di * nb + b) * HEADS + h
                    v = v_ref[b, :, vs]
                    d_o = do_ref[b, :, vs] * dmul
                    t = _chunk_terms(q_ref[b, :, ks] * qs, k_ref[b, :, ks], b_ref[b, :, ks], far, mid)
                    qem, kim, qe, kd = t["qem"], t["kim"], t["qe"], t["kd"]
                    a_t = jnp.where(mask_t, _mm_nt(kim, qem), 0.0)
                    d_a = jnp.where(mask, _mm_nt(d_o, v), 0.0)
                    d_at = jnp.where(mask_t, _mm_nt(v, d_o), 0.0)
                    z = zs_ref[0, b * HEADS + h]
                    dzn = dz_scr[zi]
                    dv_ref[b, :, vs] = _mm(a_t, d_o) + _mm_nt(kd, dzn)
                    dqem = _mm(d_a, kim)
                    dkim = _mm(d_at, qem)
                    dqe = _mm(d_o, z)
                    dkd = _mm(v, dzn)
                    ddec = jnp.sum(z * dzn, axis=0, keepdims=True)
                    dz_scr[zi] = dzn * t["dec"] + _mm_tn(d_o, qe)
                    dq_ref[b, :, ks] = (dqem * t["em"] + dqe * t["e"]) * qs
                    dk_ref[b, :, ks] = dkim * t["eim"] + dkd * t["ed"]
                    db = dqem * qem - dkim * kim + dqe * qe - dkd * kd
                    extra = jnp.sum(dkd * kd, axis=0, keepdims=True) + ddec * t["dec"]
                    db_parts.append(db + jnp.where(far_row, extra, 0.0))
                dg_ref[b] = _tri_mm(mt_bf, jnp.concatenate(db_parts, axis=1))

    in_specs, out_specs, out_shape, args = [], [], [], []
    for di, rev in enumerate((False, True)):
        ch = lambda s, rev=rev: _scan_chunk(ns - 1 - s, nl, nc, rev)
        lch = lambda s, rev=rev: _scan_lat_chunk(ns - 1 - s, nl, nc, rev)
        in_specs += [pl.BlockSpec((nb, c, dk_), lambda s, ch=ch: (0, ch(s), 0)),
                     pl.BlockSpec((nb, c, dk_), lambda s, ch=ch: (0, ch(s), 1)),
                     pl.BlockSpec((nb, c, dv_), lambda s, ch=ch: (0, ch(s), 1)),
                     pl.BlockSpec((nb, c, dk_), lambda s, ch=ch: (0, ch(s), 0)),
                     pl.BlockSpec((nb, c, dv_), lambda s, lch=lch: (0, lch(s), 0)),
                     pl.BlockSpec((1, nb * HEADS, hv, hk), lambda s: (ns - 1 - s, 0, 0, 0))]
        args += [pb3, pb3, pb3, (b_b if rev else b_f), do3, (zs_b if rev else zs_f)]
        for w in (dk_, dk_, dv_, dk_):
            out_specs.append(pl.BlockSpec((nb, c, w), lambda s, ch=ch: (0, ch(s), 0)))
            out_shape.append(jax.ShapeDtypeStruct((nb, l_len, w), F32))
    return pl.pallas_call(
        body, name="gla_bwd", grid=(ns,), in_specs=in_specs, out_specs=tuple(out_specs), out_shape=tuple(out_shape),
        scratch_shapes=[pltpu.VMEM((2 * nb * HEADS, hv, hk), F32)],
        compiler_params=_params())(*args)


def _tail(a1, pa, o_f, o_b, x2, tgt, mod, wc, wg, wo, ln_g, ln_b, gn_t, fg, nb, tm):
    tl, d = x2.shape
    nt = tl // tm
    per_ex = nt // nb
    hv = d // HEADS
    nrow = mod.shape[0]

    def body(a1_ref, z_ref, r_ref, mc_ref, mg_ref, of_ref, ob_ref, x_ref, t_ref, mod_ref, wc_ref, wg_ref, wo_ref,
             lng_ref, lnb_ref, gn_ref, fg_ref,
             dp_ref, da1_ref, do_ref, gx_ref, mrg_ref, dmo_ref, yci_ref, dyc_ref, ogi_ref, dyg_ref, sm_ref):
        i = pl.program_id(0)

        @pl.when(i == 0)
        def _():
            sm_ref[...] = jnp.zeros_like(sm_ref)

        bidx = i // per_ex
        gate = _rowsel(mod_ref[...], bidx, nb)[:, 2 * d:3 * d]
        lng, lnb, gn, fgv = lng_ref[...], lnb_ref[...], gn_ref[...], fg_ref[...]
        wc_, wg_, wo_ = wc_ref[...], wg_ref[...], wo_ref[...]

        a1v = a1_ref[...]
        mu = jnp.mean(a1v, axis=-1, keepdims=True)
        xc = a1v - mu
        rs = lax.rsqrt(jnp.mean(xc * xc, axis=-1, keepdims=True) + EPS)
        xh = xc * rs
        a2 = xh * lng + lnb
        s2 = _sigmoid(a2)
        a3 = a2 * s2
        zv = z_ref[...]
        sz = _sigmoid(zv)
        siluz = zv * sz
        ycin = a3 * siluz
        yconv = _mm(ycin, wc_)

        o = of_ref[...] + ob_ref[...]
        ohat_parts, rn_parts = [], []
        for h in range(HEADS):
            oh = o[:, h * hv:(h + 1) * hv]
            rn = lax.rsqrt(jnp.mean(oh * oh, axis=-1, keepdims=True) + EPS)
            ohat_parts.append(oh * rn)
            rn_parts.append(rn)
        ohat = jnp.concatenate(ohat_parts, axis=1)
        on = ohat * gn
        rv = r_ref[...]
        sr = _sigmoid(rv)
        silur = rv * sr
        ogin = on * silur
        ygla = _mm(ogin, wg_)

        sc = _sigmoid(mc_ref[...])
        sg = _sigmoid(mg_ref[...])
        merged = sc * yconv + sg * ygla
        mo = _mm(merged, wo_)
        hn = x_ref[...] + gate * mo
        rf = lax.rsqrt(jnp.mean(hn * hn, axis=-1, keepdims=True) + EPS)
        yh = hn * rf
        err = yh * fgv - t_ref[...]
        loss_part = 0.5 * jnp.sum(err * err) * (1.0 / d)

        dy = err * (1.0 / d)
        dfg = jnp.sum(dy * yh, axis=0, keepdims=True)
        dyh = dy * fgv
        dhn = rf * (dyh - yh * jnp.mean(dyh * yh, axis=-1, keepdims=True))
        gx_ref[...] = dhn
        dgate = jnp.sum(dhn * mo, axis=0, keepdims=True)
        dmo = gate * dhn
        dmerged = _mm_nt(dmo, wo_)
        dyconv = dmerged * sc
        dygla = dmerged * sg
        dp_ref[:, 2 * d:3 * d] = (dmerged * yconv * sc * (1.0 - sc)).astype(BF16)
        dp_ref[:, 3 * d:4 * d] = (dmerged * ygla * sg * (1.0 - sg)).astype(BF16)
        dycin = _mm_nt(dyconv, wc_)
        dogin = _mm_nt(dygla, wg_)
        mrg_ref[...] = merged.astype(BF16)
        dmo_ref[...] = dmo.astype(BF16)
        yci_ref[...] = ycin.astype(BF16)
        dyc_ref[...] = dyconv.astype(BF16)
        ogi_ref[...] = ogin.astype(BF16)
        dyg_ref[...] = dygla.astype(BF16)

        da3 = dycin * siluz
        dp_ref[:, 0:d] = (dycin * a3 * _dsilu(zv, sz)).astype(BF16)
        da2 = da3 * _dsilu(a2, s2)
        dlng = jnp.sum(da2 * xh, axis=0, keepdims=True)
        dlnb = jnp.sum(da2, axis=0, keepdims=True)
        dxh = da2 * lng
        da1_ref[...] = rs * (dxh - jnp.mean(dxh, axis=-1, keepdims=True)
                             - xh * jnp.mean(dxh * xh, axis=-1, keepdims=True))

        don = dogin * silur
        dp_ref[:, d:2 * d] = (dogin * on * _dsilu(rv, sr)).astype(BF16)
        dgn = jnp.sum(don * ohat, axis=0, keepdims=True)
        dyn = don * gn
        for h in range(HEADS):
            vs = slice(h * hv, (h + 1) * hv)
            oh_hat = ohat_parts[h]
            dh = dyn[:, vs]
            do_ref[:, vs] = rn_parts[h] * (dh - oh_hat * jnp.mean(dh * oh_hat, axis=-1, keepdims=True))

        sm_ref[0:1, :] += dfg
        sm_ref[1:2, :] += dlng
        sm_ref[2:3, :] += dlnb
        sm_ref[3:4, :] += dgn
        sm_ref[4:5, :] += jnp.zeros((1, d), F32) + loss_part
        for b in range(nb):
            sm_ref[8 + b:9 + b, :] += jnp.where(bidx == b, dgate, 0.0)

    row = pl.BlockSpec((tm, d), lambda i: (i, 0))
    pcol = lambda blk: pl.BlockSpec((tm, d), lambda i: (i, blk))
    full = lambda arr: pl.BlockSpec(arr.shape, lambda i: (0,) * arr.ndim)
    bfo = jax.ShapeDtypeStruct((tl, d), BF16)
    f32o = jax.ShapeDtypeStruct((tl, d), F32)
    return pl.pallas_call(
        body, name="tail", grid=(nt,),
        in_specs=[row, pcol(2), pcol(3), pcol(4), pcol(5), row, row, row, row, full(mod), full(wc), full(wg),
                  full(wo), full(ln_g), full(ln_b), full(gn_t), full(fg)],
        out_specs=(pl.BlockSpec((tm, 4 * d), lambda i: (i, 0)), row, row, row, row, row, row, row, row, row,
                   pl.BlockSpec((16, d), lambda i: (0, 0))),
        out_shape=(jax.ShapeDtypeStruct((tl, 4 * d), BF16), f32o, f32o, f32o, bfo, bfo, bfo, bfo, bfo, bfo,
                   jax.ShapeDtypeStruct((16, d), F32)),
        compiler_params=_params())(a1, pa, pa, pa, pa, o_f, o_b, x2, tgt, mod, wc, wg, wo, ln_g, ln_b, gn_t, fg)


def _local_step(x, c, ctx, tgt, c_ctx, ada_w, ada_b, norm_g, w_a, b_a, w_b, b_b, conv_w, conv_b, ln_g, ln_b,
                conv_proj, up2, bias2, gla_norm_g, gla_proj, w_out, final_norm_g):
    nb, s_len, d = x.shape
    c_len = ctx.shape[1]
    dk_, dv_ = d // 2, d
    l_len = s_len + c_len
    tl, tc, t_all = nb * s_len, nb * c_len, nb * l_len
    nbw = 2 * dk_ + dv_ + LANE
    tm = math.gcd(256, c_len)
    tiles = _Tiles(nb, s_len, c_len, tm)
    x2, ctx2, tgt2 = x.reshape(tl, d), ctx.reshape(tc, d), tgt.reshape(tl, d)

    cv = jnp.zeros((8, d), F32).at[0:nb].set(c).at[nb].set(c_ctx.reshape(d))
    mod = _ada_fwd(cv, ada_w, ada_b)
    u = _norm_fwd(x2, ctx2, mod, norm_g, tiles)
    pa = _matmul_bias("inproj_a", u, w_a, b_a, tl, tm, _tile(6 * d, 1536), u_tile=tiles.all_of_lat)
    pb = _matmul_bias("inproj_b", u, w_b, b_b, t_all, tm, nbw)

    cb = min(LANE, d // 2)
    a1 = _conv_fwd(pa, conv_w, conv_b, nb, s_len, cb)
    lr_blk = (2 * dk_ + dv_) // LANE
    g_all = _decay_fwd(pb, up2, bias2, tm, lr_blk)
    pb3 = pb.reshape(nb, l_len, nbw)
    o_f, zs_f, b_f, o_b, zs_b, b_b2 = _gla_fwd(pb3, g_all.reshape(nb, l_len, 2 * dk_), nb, s_len, c_len, dk_, dv_)

    gn_t = jnp.tile(gla_norm_g, (1, HEADS))
    tt = math.gcd(128, s_len)
    (dp_a2, da1, d_o, gx1, merged, dmo, ycin, dyconv, ogin, dygla, small) = _tail(
        a1, pa, o_f.reshape(tl, dv_), o_b.reshape(tl, dv_), x2, tgt2, mod, conv_proj, gla_proj, w_out, ln_g, ln_b,
        gn_t, final_norm_g, nb, tt)

    tk = math.gcd(512, tl)
    d_w_out, _ = _matmul_tn("dw_out", merged, dmo, tl, tk, _tile(d, 1024))
    d_conv_proj, _ = _matmul_tn("dw_conv_proj", ycin, dyconv, tl, tk, _tile(d, 1024))
    d_gla_proj, _ = _matmul_tn("dw_gla_proj", ogin, dygla, tl, tk, _tile(d, 1024))

    dp_a1, d_conv_w, d_conv_b = _conv_bwd(pa, da1, conv_w, nb, s_len, cb)
    gl = _gla_bwd(pb3, d_o.reshape(nb, s_len, dv_), (zs_f, b_f, zs_b, b_b2), nb, s_len, c_len, dk_, dv_)
    gl = [g_.reshape(t_all, g_.shape[-1]) for g_ in gl]
    dp_b, d_up2, d_bias2 = _decay_bwd(pb, up2, bias2, gl[0:4], gl[4:8], tm, lr_blk, dk_, dv_)

    tka = _tile(2 * d, 1024)
    du_a1 = _matmul_nt("du_a1", dp_a1, w_a, 0, tm, tka)
    du_a2 = _matmul_nt("du_a2", dp_a2, w_a, (2 * d) // tka, tm, tka)
    du_b = _matmul_nt("du_b", dp_b, w_b, 0, tm, nbw)
    dw_a1, db_a1 = _matmul_tn("dw_a1", u, dp_a1, tl, tm, _tile(2 * d, 1024), a_tile=tiles.all_of_lat)
    dw_a2, db_a2 = _matmul_tn("dw_a2", u, dp_a2, tl, tm, _tile(4 * d, 1024), a_tile=tiles.all_of_lat)
    dw_b, db_b = _matmul_tn("dw_b", u, dp_b, t_all, tm, nbw)

    grad_x2, dmod_ss, d_norm_g = _norm_bwd(x2, ctx2, mod, norm_g, [du_a1, du_a2], du_b, gx1, tiles)
    dmod = dmod_ss.at[0:nb, 2 * d:3 * d].set(small[8:8 + nb])
    d_ada_w, d_ada_b, d_cv = _ada_bwd(cv, ada_w, dmod)

    return dict(
        loss=small[4, 0], grad_x=grad_x2.reshape(nb, s_len, d), c_ctx=d_cv[nb], ada_w=d_ada_w, ada_b=d_ada_b,
        norm_g=d_norm_g, w_a=jnp.concatenate([dw_a1, dw_a2], axis=1), b_a=jnp.concatenate([db_a1, db_a2], axis=1),
        w_b=dw_b, b_b=db_b, conv_w=d_conv_w, conv_b=d_conv_b, conv_ln_g=small[1:2], conv_ln_b=small[2:3],
        conv_proj=d_conv_proj, up2=d_up2, bias2=d_bias2,
        gla_norm_g=small[3:4].reshape(HEADS, d // HEADS).sum(axis=0, keepdims=True),
        gla_proj=d_gla_proj, w_out=d_w_out, final_norm_g=small[0:1])


def _regroup(o, d, r):
    cb = min(LANE, d // 2)
    glu = []
    for j in range(d // cb):
        glu += [o[..., j * cb:(j + 1) * cb], o[..., d + j * cb:d + (j + 1) * cb]]
    a = jnp.concatenate(glu + [o[..., 2 * d:3 * d], o[..., 5 * d + 2 * r:8 * d + 2 * r]], axis=-1)
    pad = jnp.zeros(o.shape[:-1] + (LANE - 2 * r,), o.dtype)
    b = jnp.concatenate([o[..., 3 * d:5 * d + 2 * r], pad], axis=-1)
    return a, b


def _ungroup(a, b, d, r):
    cb = min(LANE, d // 2)
    n = d // cb
    gv = [a[..., 2 * j * cb:(2 * j + 1) * cb] for j in range(n)]
    gg = [a[..., (2 * j + 1) * cb:(2 * j + 2) * cb] for j in range(n)]
    return jnp.concatenate(gv + gg + [a[..., 2 * d:3 * d], b[..., 0:2 * d + 2 * r], a[..., 3 * d:6 * d]], axis=-1)


def _mesh_pos():
    return lax.axis_index("x"), lax.axis_index("y"), lax.axis_index("c")


_ANY = pl.BlockSpec(memory_space=pl.ANY)


def _all_gather(arrs):
    n = len(arrs)

    def body(*refs):
        ins, outs = refs[:n], refs[n:2 * n]
        send_sems, recv_sems, local_sems = refs[2 * n:]
        x, y, c = _mesh_pos()
        me, sibling = (x, y, c), (x, y, 1 - c)
        chips = [(1 - x, y), (x, 1 - y), (1 - x, 1 - y)]

        def slot(a, pos):
            return outs[a].at[4 * pos[0] + 2 * pos[1] + pos[2]]

        def copy(a, k, block, to, src=None):
            return pltpu.make_async_remote_copy(
                src_ref=slot(a, block) if src is None else src, dst_ref=slot(a, block),
                send_sem=send_sems.at[7 * a + k], recv_sem=recv_sems.at[7 * a + k],
                device_id=to, device_id_type=MESH)

        mine = [pltpu.make_async_copy(ins[a], slot(a, me), local_sems.at[a]) for a in range(n)]
        for cp in mine:
            cp.start()
        first = []
        for a in range(n):
            first.append(copy(a, 0, me, sibling, src=ins[a]))
            first += [copy(a, 1 + j, me, (*chip, c), src=ins[a]) for j, chip in enumerate(chips)]
        for cp in first:
            cp.start()
        passed = []
        for j, chip in enumerate(chips):
            for a in range(n):
                copy(a, 1 + j, (*chip, c), me).wait_recv()
                fwd = copy(a, 4 + j, (*chip, c), sibling)
                fwd.start()
                passed.append(fwd)
        for a in range(n):
            copy(a, 0, sibling, me).wait_recv()
            for j, chip in enumerate(chips):
                copy(a, 4 + j, (*chip, 1 - c), me).wait_recv()
        for cp in first + passed:
            cp.wait_send()
        for cp in mine:
            cp.wait()

    return pl.pallas_call(
        body, name="all_gather",
        out_shape=tuple(jax.ShapeDtypeStruct((N_DEV,) + a.shape, a.dtype) for a in arrs),
        in_specs=[_ANY] * n, out_specs=tuple([_ANY] * n),
        scratch_shapes=[pltpu.SemaphoreType.DMA((7 * n,)), pltpu.SemaphoreType.DMA((7 * n,)),
                        pltpu.SemaphoreType.DMA((n,))],
    )(*arrs)


def _exchange_sibling(arrs):
    n = len(arrs)

    def body(*refs):
        ins, outs = refs[:n], refs[n:2 * n]
        send_sems, recv_sems = refs[2 * n:]
        x, y, c = _mesh_pos()
        copies = [pltpu.make_async_remote_copy(
            src_ref=ins[a].at[2 * k + (1 - c)], dst_ref=outs[a].at[k],
            send_sem=send_sems.at[4 * a + k], recv_sem=recv_sems.at[4 * a + k],
            device_id=(x, y, 1 - c), device_id_type=MESH) for a in range(n) for k in range(4)]
        for cp in copies:
            cp.start()
        for cp in copies:
            cp.wait_recv()
        for cp in copies:
            cp.wait_send()

    return pl.pallas_call(
        body, name="grad_exchange_sibling",
        out_shape=tuple(jax.ShapeDtypeStruct((4,) + a.shape[1:], a.dtype) for a in arrs),
        in_specs=[_ANY] * n, out_specs=tuple([_ANY] * n),
        scratch_shapes=[pltpu.SemaphoreType.DMA((4 * n,)), pltpu.SemaphoreType.DMA((4 * n,))],
    )(*arrs)


def _pair_sum(name, mine, theirs):
    _, r, cdim = mine.shape
    tr = r if (r % 8 or r <= 256) else math.gcd(r, 256)

    def body(m_ref, t_ref, o_ref):
        c = lax.axis_index("c")
        own = jnp.where(c == 0, m_ref[:, 0].astype(F32), m_ref[:, 1].astype(F32))
        o_ref[...] = (own + t_ref[...].astype(F32)).astype(o_ref.dtype)

    return pl.pallas_call(
        body, name=name, grid=(r // tr,),
        in_specs=[pl.BlockSpec((4, 2, tr, cdim), lambda i: (0, 0, i, 0)),
                  pl.BlockSpec((4, tr, cdim), lambda i: (0, i, 0))],
        out_specs=pl.BlockSpec((4, tr, cdim), lambda i: (0, i, 0)),
        out_shape=jax.ShapeDtypeStruct((4, r, cdim), mine.dtype),
        compiler_params=_params())(mine.reshape(4, 2, r, cdim), theirs)


def _exchange_chips(arrs):
    n = len(arrs)

    def body(*refs):
        ins, outs = refs[:n], refs[n:2 * n]
        send_sems, recv_sems, local_sems = refs[2 * n:]
        x, y, c = _mesh_pos()
        my_chip = 2 * x + y
        mine = [pltpu.make_async_copy(ins[a].at[my_chip], outs[a].at[my_chip], local_sems.at[a]) for a in range(n)]
        for cp in mine:
            cp.start()
        copies = []
        for rel in range(1, 4):
            px = 1 - x if rel & 2 else x
            py = 1 - y if rel & 1 else y
            for a in range(n):
                copies.append(pltpu.make_async_remote_copy(
                    src_ref=ins[a].at[2 * px + py], dst_ref=outs[a].at[my_chip],
                    send_sem=send_sems.at[3 * a + rel - 1], recv_sem=recv_sems.at[3 * a + rel - 1],
                    device_id=(px, py, c), device_id_type=MESH))
        for cp in copies:
            cp.start()
        for cp in copies:
            cp.wait_recv()
        for cp in copies:
            cp.wait_send()
        for cp in mine:
            cp.wait()

    return pl.pallas_call(
        body, name="grad_exchange_chips",
        out_shape=tuple(jax.ShapeDtypeStruct(a.shape, a.dtype) for a in arrs),
        in_specs=[_ANY] * n, out_specs=tuple([_ANY] * n),
        scratch_shapes=[pltpu.SemaphoreType.DMA((3 * n,)), pltpu.SemaphoreType.DMA((3 * n,)),
                        pltpu.SemaphoreType.DMA((n,))],
    )(*arrs)


def _sum_adam(name, parts, w, m, v):
    r, cdim = w.shape
    n_parts = parts.shape[0]
    tr = r if (r % 8 or r <= 256) else math.gcd(r, 256)
    bc1 = 1.0 - ADAM_B1 ** ADAM_STEP
    bc2 = 1.0 - ADAM_B2 ** ADAM_STEP

    def body(p_ref, w_ref, m_ref, v_ref, g_ref, d_ref, nm_ref, nv_ref):
        g = p_ref[0].astype(F32)
        for k in range(1, n_parts):
            g = g + p_ref[k].astype(F32)
        mn = ADAM_B1 * m_ref[...] + (1.0 - ADAM_B1) * g
        vn = ADAM_B2 * v_ref[...] + (1.0 - ADAM_B2) * (g * g)
        g_ref[...] = g
        nm_ref[...] = mn
        nv_ref[...] = vn
        d_ref[...] = -ADAM_LR * ((mn / bc1) / (jnp.sqrt(vn / bc2) + ADAM_EPS) + ADAM_WD * w_ref[...])

    blk = pl.BlockSpec((tr, cdim), lambda i: (i, 0))
    o = jax.ShapeDtypeStruct((r, cdim), F32)
    return pl.pallas_call(
        body, name=name, grid=(r // tr,),
        in_specs=[pl.BlockSpec((n_parts, tr, cdim), lambda i: (0, i, 0)), blk, blk, blk],
        out_specs=(blk, blk, blk, blk), out_shape=(o, o, o, o),
        compiler_params=_params())(parts, w, m, v)


_SMALL = ("c_ctx", "ada_b", "norm_g", "b_in", "conv_b", "conv_ln_g", "conv_ln_b", "decay_bias_fwd",
          "decay_bias_bwd", "gla_norm_g", "final_norm_g")
_WEIGHTS = ("c_ctx", "ada_w", "ada_b", "norm_g", "w_in", "b_in", "conv_w", "conv_b", "conv_ln_g", "conv_ln_b",
            "conv_proj", "decay_up_fwd", "decay_bias_fwd", "decay_up_bwd", "decay_bias_bwd", "gla_norm_g",
            "gla_proj", "w_out", "final_norm_g")


def _as2d(a):
    if a.ndim == 1:
        return a.reshape(1, -1)
    return a.reshape(-1, a.shape[-1])


def kernel(x, c, ctx, c_ctx, ada_w, ada_b, norm_g, w_in, b_in, conv_w, conv_b, conv_ln_g, conv_ln_b, conv_proj, decay_up_fwd, decay_bias_fwd, decay_up_bwd, decay_bias_bwd, gla_norm_g, gla_proj, w_out, final_norm_g, loss_target, m_c_ctx, m_ada_w, m_ada_b, m_norm_g, m_w_in, m_b_in, m_conv_w, m_conv_b, m_conv_ln_g, m_conv_ln_b, m_conv_proj, m_decay_up_fwd, m_decay_bias_fwd, m_decay_up_bwd, m_decay_bias_bwd, m_gla_norm_g, m_gla_proj, m_w_out, m_final_norm_g, v_c_ctx, v_ada_w, v_ada_b, v_norm_g, v_w_in, v_b_in, v_conv_w, v_conv_b, v_conv_ln_g, v_conv_ln_b, v_conv_proj, v_decay_up_fwd, v_decay_bias_fwd, v_decay_up_bwd, v_decay_bias_bwd, v_gla_norm_g, v_gla_proj, v_w_out, v_final_norm_g):
    env = dict(locals())
    wts = {k: env[k] for k in _WEIGHTS}
    d = x.shape[-1]
    r = decay_up_fwd.shape[1]
    dk_ = d // 2
    n_in = w_in.shape[-1] * N_DEV

    proj3 = jnp.concatenate([conv_proj[0], gla_proj[0], w_out[0]], axis=0).astype(BF16)
    small_pack = jnp.concatenate([
        jnp.pad(conv_w[0], ((0, 32 - conv_w.shape[1]), (0, 0))),
        jnp.concatenate([decay_up_fwd[0], decay_up_bwd[0]], axis=1)], axis=0)
    g_win, g_ada, g_proj, g_small = _all_gather(
        [w_in[0].astype(BF16), ada_w[0].astype(BF16), proj3, small_pack])

    w_in_full = g_win.transpose(1, 0, 2).reshape(d, n_in)
    w_a, w_b = _regroup(w_in_full, d, r)
    ada_w_full = g_ada.transpose(1, 0, 2).reshape(d, 3 * d)
    ds = d // N_DEV
    conv_proj_full = g_proj[:, 0:ds].reshape(d, d)
    gla_proj_full = g_proj[:, ds:2 * ds].reshape(d, d)
    w_out_full = g_proj[:, 2 * ds:3 * ds].reshape(d, d)
    ktaps = conv_w.shape[1]
    conv_w_full = g_small[:, 0:ktaps].transpose(1, 0, 2).reshape(ktaps, d)
    up_f = g_small[:, 32:32 + r, 0:dk_ // N_DEV].transpose(1, 0, 2).reshape(r, dk_)
    up_b = g_small[:, 32:32 + r, dk_ // N_DEV:].transpose(1, 0, 2).reshape(r, dk_)
    up2 = jnp.zeros((LANE, 2 * dk_), F32).at[0:r, 0:dk_].set(up_f).at[r:2 * r, dk_:].set(up_b)
    bias2 = jnp.concatenate([decay_bias_fwd, decay_bias_bwd], axis=1)
    b_a, b_b = _regroup(b_in, d, r)

    g = _local_step(x, c, ctx, loss_target, c_ctx, ada_w_full, ada_b, norm_g[0:1], w_a, b_a, w_b, b_b,
                    conv_w_full, conv_b, conv_ln_g, conv_ln_b, conv_proj_full, up2, bias2, gla_norm_g,
                    gla_proj_full, w_out_full, final_norm_g.reshape(1, d))

    dw_in = _ungroup(g["w_a"], g["w_b"], d, r)
    e_win = dw_in.reshape(d, N_DEV, n_in // N_DEV).transpose(1, 0, 2).astype(BF16)
    e_ada = g["ada_w"].reshape(d, N_DEV, 3 * d // N_DEV).transpose(1, 0, 2).astype(BF16)
    e_proj = jnp.concatenate([g["conv_proj"].reshape(N_DEV, ds, d), g["gla_proj"].reshape(N_DEV, ds, d),
                              g["w_out"].reshape(N_DEV, ds, d)], axis=1).astype(BF16)
    d_up_f, d_up_b = g["up2"][0:r, 0:dk_], g["up2"][r:2 * r, dk_:]
    e_small = jnp.concatenate([
        jnp.pad(g["conv_w"], ((0, 32 - ktaps), (0, 0))).reshape(32, N_DEV, ds).transpose(1, 0, 2),
        jnp.concatenate([d_up_f.reshape(r, N_DEV, dk_ // N_DEV).transpose(1, 0, 2),
                         d_up_b.reshape(r, N_DEV, dk_ // N_DEV).transpose(1, 0, 2)], axis=2)], axis=1)
    mine = [e_win, e_ada, e_proj, e_small]
    theirs = _exchange_sibling(mine)
    chip_sums = [_pair_sum("pair_sum_" + nm, a, b)
                 for nm, a, b in zip(("w_in", "ada_w", "proj", "small"), mine, theirs)]
    x_win, x_ada, x_proj, x_small = _exchange_chips(chip_sums)

    db_in = _ungroup(g["b_a"], g["b_b"], d, r)
    small_g = dict(c_ctx=g["c_ctx"].reshape(1, d), ada_b=g["ada_b"], norm_g=g["norm_g"], b_in=db_in,
                   conv_b=g["conv_b"], conv_ln_g=g["conv_ln_g"], conv_ln_b=g["conv_ln_b"],
                   decay_bias_fwd=g["bias2"][:, 0:dk_], decay_bias_bwd=g["bias2"][:, dk_:],
                   gla_norm_g=g["gla_norm_g"], final_norm_g=g["final_norm_g"])
    sizes = [wts[k].size for k in _SMALL]
    n_small = sum(sizes) + 1
    n_pad = -n_small % LANE
    pack = lambda parts: jnp.concatenate([p.reshape(1, -1) for p in parts] + [jnp.zeros((1, n_pad + 1), F32)], axis=1)
    gpack = jnp.concatenate([small_g[k].reshape(1, -1) for k in _SMALL]
                            + [g["loss"].reshape(1, 1), jnp.zeros((1, n_pad), F32)], axis=1)
    (gpacks,) = _all_gather([gpack])
    sg, sd, sm, sv = _sum_adam("small_adam", gpacks, pack([wts[k] for k in _SMALL]),
                               pack([env["m_" + k] for k in _SMALL]), pack([env["v_" + k] for k in _SMALL]))

    out = {}
    off = 0
    for k, n in zip(_SMALL, sizes):
        for pre, arr in (("grad_", sg), ("delta_", sd), ("new_m_", sm), ("new_v_", sv)):
            out[pre + k] = arr[0, off:off + n].reshape(wts[k].shape)
        off += n
    loss = sg[0, off]

    def big(name, parts, wname):
        w2 = _as2d(wts[wname])
        res = _sum_adam(name, parts, w2, _as2d(env["m_" + wname]), _as2d(env["v_" + wname]))
        for pre, arr in zip(("grad_", "delta_", "new_m_", "new_v_"), res):
            out[pre + wname] = arr.reshape(wts[wname].shape)

    big("adam_w_in", x_win, "w_in")
    big("adam_ada_w", x_ada, "ada_w")
    big("adam_conv_proj", x_proj[:, 0:ds], "conv_proj")
    big("adam_gla_proj", x_proj[:, ds:2 * ds], "gla_proj")
    big("adam_w_out", x_proj[:, 2 * ds:3 * ds], "w_out")
    big("adam_conv_w", x_small[:, 0:ktaps], "conv_w")
    big("adam_up_f", x_small[:, 32:32 + r, 0:dk_ // N_DEV], "decay_up_fwd")
    big("adam_up_b", x_small[:, 32:32 + r, dk_ // N_DEV:], "decay_up_bwd")

    return (loss, g["grad_x"], *[out["grad_" + k] for k in _WEIGHTS], *[out["delta_" + k] for k in _WEIGHTS],
            *[out["new_m_" + k] for k in _WEIGHTS], *[out["new_v_" + k] for k in _WEIGHTS])
```

```python
import functools
import math

import jax
import jax.numpy as jnp
from jax import lax
from jax.experimental import pallas as pl
from jax.experimental.pallas import tpu as pltpu

F32 = jnp.float32
BF16 = jnp.bfloat16
MESH = pl.DeviceIdType.MESH

N_DEV = 8
GRID_W = 64
CHUNK = 64
HEADS = 4
EPS = 1e-6
GATE_TAU = 16.0
LANE = 128
ADAM_LR, ADAM_B1, ADAM_B2, ADAM_EPS, ADAM_WD, ADAM_STEP = 0.001, 0.9, 0.999, 1e-08, 0.01, 10
VMEM_LIMIT = 56 * 1024 * 1024


def _params(**kw):
    return pltpu.CompilerParams(vmem_limit_bytes=VMEM_LIMIT, **kw)


def _tile(n, pref):
    t = (min(pref, n) // LANE) * LANE
    while t >= LANE:
        if n % t == 0:
            return t
        t -= LANE
    return n


def _mm(a, b):
    return jnp.dot(a.astype(BF16), b.astype(BF16), preferred_element_type=F32)


def _mm_nt(a, b):
    return lax.dot_general(a.astype(BF16), b.astype(BF16), (((1,), (1,)), ((), ())), preferred_element_type=F32)


def _mm_tn(a, b):
    return lax.dot_general(a.astype(BF16), b.astype(BF16), (((0,), (0,)), ((), ())), preferred_element_type=F32)


def _mm_hi(a, b):
    return jnp.dot(a, b, precision=lax.Precision.HIGHEST, preferred_element_type=F32)


def _mm_nt_hi(a, b):
    return lax.dot_general(a, b, (((1,), (1,)), ((), ())), precision=lax.Precision.HIGHEST, preferred_element_type=F32)


def _mm_tn_hi(a, b):
    return lax.dot_general(a, b, (((0,), (0,)), ((), ())), precision=lax.Precision.HIGHEST, preferred_element_type=F32)


def _sigmoid(x):
    return 0.5 * jnp.tanh(0.5 * x) + 0.5


def _dsilu(x, s):
    return s * (1.0 + x * (1.0 - s))


def _rowsel(table, idx, n):
    out = table[0:1, :]
    for r in range(1, n):
        out = jnp.where(idx == r, table[r:r + 1, :], out)
    return out


def _ada_fwd(cv, ada_w, ada_b):
    def body(cv_ref, w_ref, b_ref, o_ref):
        c = cv_ref[...]
        o_ref[...] = _mm(c * _sigmoid(c), w_ref[...]) + b_ref[...]

    return pl.pallas_call(body, name="ada_fwd", out_shape=jax.ShapeDtypeStruct((cv.shape[0], ada_w.shape[1]), F32),
                          compiler_params=_params())(cv, ada_w, ada_b)


def _ada_bwd(cv, ada_w, dmod):
    def body(cv_ref, w_ref, dm_ref, dw_ref, db_ref, dc_ref):
        c = cv_ref[...]
        s = _sigmoid(c)
        dm = dm_ref[...]
        dw_ref[...] = _mm_tn_hi(c * s, dm)
        db_ref[...] = jnp.sum(dm, axis=0, keepdims=True)
        dc_ref[...] = _mm_nt(dm, w_ref[...]) * _dsilu(c, s)

    d, n3 = ada_w.shape
    return pl.pallas_call(
        body, name="ada_bwd",
        out_shape=(jax.ShapeDtypeStruct((d, n3), F32), jax.ShapeDtypeStruct((1, n3), F32),
                   jax.ShapeDtypeStruct(cv.shape, F32)),
        compiler_params=_params())(cv, ada_w, dmod)


class _Tiles:
    def __init__(self, nb, s_len, c_len, tm, big):
        self.nb, self.tm, self.big = nb, tm, big
        self.lat, self.ctx = s_len // tm, c_len // tm
        self.pad = -(self.lat + self.ctx) % big
        self.per_ex = self.lat + self.ctx + self.pad
        self.n_all, self.n_lat = nb * self.per_ex, nb * self.lat
        self.rows_per_ex = self.per_ex * tm

    def is_lat(self, i):
        return i % self.per_ex < self.lat

    def is_pad(self, i):
        return i % self.per_ex >= self.lat + self.ctx

    def lat_of_all(self, i):
        return (i // self.per_ex) * self.lat + jnp.minimum(i % self.per_ex, self.lat - 1)

    def ctx_of_all(self, i):
        return (i // self.per_ex) * self.ctx + jnp.clip(i % self.per_ex - self.lat, 0, self.ctx - 1)

    def big_all_of_lat(self, t):
        lat_big = self.lat // self.big
        return (t // lat_big) * (self.per_ex // self.big) + t % lat_big


def _norm_fwd(x2, ctx2, mod, norm_g, tiles):
    tl, d = x2.shape
    tc = ctx2.shape[0]
    nb, tm = tiles.nb, tiles.tm

    def body(x_ref, c_ref, mod_ref, g_ref, u_ref):
        i = pl.program_id(0)
        lat = tiles.is_lat(i)
        xv = jnp.where(lat, x_ref[...], c_ref[...])
        row = jnp.where(lat, i // tiles.per_ex, nb)
        m = _rowsel(mod_ref[...], row, nb + 1)
        shift, scale = m[:, 0:d], m[:, d:2 * d]
        rstd = lax.rsqrt(jnp.mean(xv * xv, axis=-1, keepdims=True) + EPS)
        u = xv * rstd * g_ref[...] * (1.0 + scale) + shift
        u_ref[...] = jnp.where(tiles.is_pad(i), 0.0, u).astype(BF16)

    return pl.pallas_call(
        body, name="norm_fwd", grid=(tiles.n_all,),
        in_specs=[pl.BlockSpec((tm, d), lambda i: (tiles.lat_of_all(i), 0)),
                  pl.BlockSpec((tm, d), lambda i: (tiles.ctx_of_all(i), 0)),
                  pl.BlockSpec(mod.shape, lambda i: (0, 0)),
                  pl.BlockSpec((1, d), lambda i: (0, 0))],
        out_specs=pl.BlockSpec((tm, d), lambda i: (i, 0)),
        out_shape=jax.ShapeDtypeStruct((tiles.n_all * tm, d), BF16),
        compiler_params=_params())(x2, ctx2, mod, norm_g)


def _norm_bwd(x2, ctx2, mod, norm_g, du_lat, du_b, gx1, tiles):
    tl, d = x2.shape
    nb, tm = tiles.nb, tiles.tm
    nrow = mod.shape[0]
    n_lat_in = len(du_lat)

    def body(x_ref, c_ref, mod_ref, g_ref, *refs):
        dl_refs = refs[:n_lat_in]
        d3_ref, gx_ref, gxo_ref, dmod_ref, dg_ref = refs[n_lat_in:]
        i = pl.program_id(0)

        @pl.when(i == 0)
        def _():
            dmod_ref[...] = jnp.zeros_like(dmod_ref)
            dg_ref[...] = jnp.zeros_like(dg_ref)

        lat = tiles.is_lat(i)
        xv = jnp.where(lat, x_ref[...], c_ref[...])
        row = jnp.where(lat, i // tiles.per_ex, nb)
        m = _rowsel(mod_ref[...], row, nb + 1)
        scale = m[:, d:2 * d]
        g = g_ref[...]
        dl = dl_refs[0][...]
        for ref in dl_refs[1:]:
            dl = dl + ref[...]
        du = jnp.where(tiles.is_pad(i), 0.0, d3_ref[...] + jnp.where(lat, dl, 0.0))
        rstd = lax.rsqrt(jnp.mean(xv * xv, axis=-1, keepdims=True) + EPS)
        xh = xv * rstd
        dshift = jnp.sum(du, axis=0, keepdims=True)
        dscale = jnp.sum(du * xh * g, axis=0, keepdims=True)
        dxn = du * (1.0 + scale)
        dg_ref[...] += jnp.sum(dxn * xh, axis=0, keepdims=True)
        dxh = dxn * g
        dx = rstd * (dxh - xh * jnp.mean(dxh * xh, axis=-1, keepdims=True))

        @pl.when(lat)
        def _():
            gxo_ref[...] = dx + gx_ref[...]

        for r in range(nb + 1):
            dmod_ref[r:r + 1, 0:d] += jnp.where(row == r, dshift, 0.0)
            dmod_ref[r:r + 1, d:2 * d] += jnp.where(row == r, dscale, 0.0)

    lat_map = lambda i: (tiles.lat_of_all(i), 0)
    lat_spec = pl.BlockSpec((tm, d), lat_map)
    return pl.pallas_call(
        body, name="norm_bwd", grid=(tiles.n_all,),
        in_specs=[lat_spec,
                  pl.BlockSpec((tm, d), lambda i: (tiles.ctx_of_all(i), 0)),
                  pl.BlockSpec(mod.shape, lambda i: (0, 0)),
                  pl.BlockSpec((1, d), lambda i: (0, 0))]
                 + [lat_spec] * n_lat_in
                 + [pl.BlockSpec((tm, d), lambda i: (i, 0)), lat_spec],
        out_specs=(lat_spec,
                   pl.BlockSpec((nrow, 3 * d), lambda i: (0, 0)),
                   pl.BlockSpec((1, d), lambda i: (0, 0))),
        out_shape=(jax.ShapeDtypeStruct((tl, d), F32), jax.ShapeDtypeStruct((nrow, 3 * d), F32),
                   jax.ShapeDtypeStruct((1, d), F32)),
        compiler_params=_params())(x2, ctx2, mod, norm_g, *du_lat, du_b, gx1)


def _matmul_bias(name, u, w, b, rows, tm, tn, u_tile=lambda i: i):
    d, n = w.shape

    def body(u_ref, w_ref, b_ref, o_ref):
        o_ref[...] = jnp.dot(u_ref[...], w_ref[...], preferred_element_type=F32) + b_ref[...]

    return pl.pallas_call(
        body, name=name, grid=(n // tn, rows // tm),
        in_specs=[pl.BlockSpec((tm, d), lambda j, i: (u_tile(i), 0)),
                  pl.BlockSpec((d, tn), lambda j, i: (0, j)),
                  pl.BlockSpec((1, tn), lambda j, i: (0, j))],
        out_specs=pl.BlockSpec((tm, tn), lambda j, i: (i, j)),
        out_shape=jax.ShapeDtypeStruct((rows, n), F32),
        compiler_params=_params())(u, w, b)


def _matmul_nt(name, a, w, koff, tm, tk):
    r, kc = a.shape
    d = w.shape[0]
    nk = kc // tk

    def body(a_ref, w_ref, o_ref):
        k = pl.program_id(1)
        p = lax.dot_general(a_ref[...], w_ref[...], (((1,), (1,)), ((), ())), preferred_element_type=F32)

        @pl.when(k == 0)
        def _():
            o_ref[...] = p

        @pl.when(k > 0)
        def _():
            o_ref[...] += p

    return pl.pallas_call(
        body, name=name, grid=(r // tm, nk),
        in_specs=[pl.BlockSpec((tm, tk), lambda i, k: (i, k)),
                  pl.BlockSpec((d, tk), lambda i, k: (0, koff + k))],
        out_specs=pl.BlockSpec((tm, d), lambda i, k: (i, 0)),
        out_shape=jax.ShapeDtypeStruct((r, d), F32),
        compiler_params=_params())(a, w)


def _matmul_tn(name, a, b, rows, tk, tn, a_tile=lambda k: k):
    m = a.shape[1]
    n = b.shape[1]

    def body(a_ref, b_ref, o_ref, s_ref):
        k = pl.program_id(1)
        bv = b_ref[...]
        p = lax.dot_general(a_ref[...], bv, (((0,), (0,)), ((), ())), preferred_element_type=F32)
        cs = jnp.sum(bv.astype(F32), axis=0, keepdims=True)

        @pl.when(k == 0)
        def _():
            o_ref[...] = p
            s_ref[...] = cs

        @pl.when(k > 0)
        def _():
            o_ref[...] += p
            s_ref[...] += cs

    return pl.pallas_call(
        body, name=name, grid=(n // tn, rows // tk),
        in_specs=[pl.BlockSpec((tk, m), lambda j, k: (a_tile(k), 0)),
                  pl.BlockSpec((tk, tn), lambda j, k: (k, j))],
        out_specs=(pl.BlockSpec((m, tn), lambda j, k: (0, j)), pl.BlockSpec((1, tn), lambda j, k: (0, j))),
        out_shape=(jax.ShapeDtypeStruct((m, n), F32), jax.ShapeDtypeStruct((1, n), F32)),
        compiler_params=_params())(a, b)


def _conv_window(pad_ref, r, shift, ktaps, width, horizontal):
    if horizontal:
        return pad_ref[r, pl.ds(16 + shift, width), :]
    return pad_ref[r + ktaps // 2 + shift]


def _conv_row(pad_ref, w, r, ktaps, width, horizontal, flip):
    half = ktaps // 2
    acc = None
    for t in range(ktaps):
        win = _conv_window(pad_ref, r, (half - t) if flip else (t - half), ktaps, width, horizontal)
        term = win * w[t:t + 1, :]
        acc = term if acc is None else acc + term
    return acc


def _fill_padded(ref, val, rows, width, ktaps, horizontal):
    half_k = ktaps // 2
    cb = val.shape[-1]
    if horizontal:
        ref[:, 0:16, :] = jnp.zeros((rows, 16, cb), F32)
        ref[:, 16 + width:32 + width, :] = jnp.zeros((rows, 16, cb), F32)
        ref[:, 16:16 + width, :] = val
    else:
        ref[0:half_k, :, :] = jnp.zeros((half_k, width, cb), F32)
        ref[half_k + rows:2 * half_k + rows, :, :] = jnp.zeros((half_k, width, cb), F32)
        ref[half_k:half_k + rows, :, :] = val


def _conv_fwd(pa, conv_w, conv_b, nb, s, cb):
    ktaps, d = conv_w.shape
    rows, width = s // GRID_W, GRID_W
    half_k = ktaps // 2
    nblk = d // cb
    nh = nblk // 2

    def body(glu_ref, w_ref, b_ref, o_ref, ph_ref, pv_ref):
        j = pl.program_id(1)
        a0 = (glu_ref[:, 0:cb] * _sigmoid(glu_ref[:, cb:2 * cb])).reshape(rows, width, cb)
        w = w_ref[...]

        bias = b_ref[...]

        def run(pad_ref, horizontal):
            _fill_padded(pad_ref, a0, rows, width, ktaps, horizontal)

            def row(r, carry):
                at = pl.ds(pl.multiple_of(r * width, width), width)
                o_ref[at, :] = _conv_row(pad_ref, w, r, ktaps, width, horizontal, False) + bias
                return carry

            lax.fori_loop(0, rows, row, 0)

        @pl.when(j < nh)
        def _():
            run(ph_ref, True)

        @pl.when(j >= nh)
        def _():
            run(pv_ref, False)

    return pl.pallas_call(
        body, name="conv_fwd", grid=(nb, nblk),
        in_specs=[pl.BlockSpec((s, 2 * cb), lambda b, j: (b, j)),
                  pl.BlockSpec((ktaps, cb), lambda b, j: (0, j)),
                  pl.BlockSpec((1, cb), lambda b, j: (0, j))],
        out_specs=pl.BlockSpec((s, cb), lambda b, j: (b, j)),
        out_shape=jax.ShapeDtypeStruct((nb * s, d), F32),
        scratch_shapes=[pltpu.VMEM((rows, width + 32, cb), F32), pltpu.VMEM((rows + 2 * half_k, width, cb), F32)],
        compiler_params=_params())(pa, conv_w, conv_b)


def _conv_bwd(pa, da1, conv_w, nb, s, cb):
    ktaps, d = conv_w.shape
    rows, width = s // GRID_W, GRID_W
    half_k = ktaps // 2
    nblk = d // cb
    nh = nblk // 2

    def body(glu_ref, da_ref, w_ref, dp_ref, dw_ref, db_ref, pha_ref, phd_ref, pva_ref, pvd_ref):
        j = pl.program_id(0)
        b = pl.program_id(1)
        a0 = (glu_ref[:, 0:cb] * _sigmoid(glu_ref[:, cb:2 * cb])).reshape(rows, width, cb)
        da1v = da_ref[...]
        d3 = da1v.reshape(rows, width, cb)
        w = w_ref[...]

        @pl.when(b == 0)
        def _():
            dw_ref[...] = jnp.zeros_like(dw_ref)
            db_ref[...] = jnp.zeros_like(db_ref)

        db_ref[...] += jnp.sum(da1v, axis=0, keepdims=True)

        def run(pa_ref, pd_ref, horizontal):
            _fill_padded(pa_ref, a0, rows, width, ktaps, horizontal)
            _fill_padded(pd_ref, d3, rows, width, ktaps, horizontal)

            def row(r, accs):
                at = pl.ds(pl.multiple_of(r * width, width), width)
                da0 = _conv_row(pd_ref, w, r, ktaps, width, horizontal, True)
                gv = glu_ref[at, 0:cb]
                sg = _sigmoid(glu_ref[at, cb:2 * cb])
                dp_ref[at, 0:cb] = (da0 * sg).astype(BF16)
                dp_ref[at, cb:2 * cb] = (da0 * gv * sg * (1.0 - sg)).astype(BF16)
                d_row = da_ref[at, :]
                out = []
                for t in range(ktaps):
                    prod = _conv_window(pa_ref, r, t - half_k, ktaps, width, horizontal) * d_row
                    out.append(accs[t] + jnp.sum(prod.reshape(width // 8, 8, cb), axis=0))
                return tuple(out)

            accs = lax.fori_loop(0, rows, row, tuple(jnp.zeros((8, cb), F32) for _ in range(ktaps)))
            for t in range(ktaps):
                dw_ref[t:t + 1, :] += jnp.sum(accs[t], axis=0, keepdims=True)

        @pl.when(j < nh)
        def _():
            run(pha_ref, phd_ref, True)

        @pl.when(j >= nh)
        def _():
            run(pva_ref, pvd_ref, False)

    return pl.pallas_call(
        body, name="conv_bwd", grid=(nblk, nb),
        in_specs=[pl.BlockSpec((s, 2 * cb), lambda j, b: (b, j)),
                  pl.BlockSpec((s, cb), lambda j, b: (b, j)),
                  pl.BlockSpec((ktaps, cb), lambda j, b: (0, j))],
        out_specs=(pl.BlockSpec((s, 2 * cb), lambda j, b: (b, j)),
                   pl.BlockSpec((ktaps, cb), lambda j, b: (0, j)),
                   pl.BlockSpec((1, cb), lambda j, b: (0, j))),
        out_shape=(jax.ShapeDtypeStruct((nb * s, 2 * d), BF16),
                   jax.ShapeDtypeStruct((ktaps, d), F32), jax.ShapeDtypeStruct((1, d), F32)),
        scratch_shapes=[pltpu.VMEM((rows, width + 32, cb), F32), pltpu.VMEM((rows, width + 32, cb), F32),
                        pltpu.VMEM((rows + 2 * half_k, width, cb), F32),
                        pltpu.VMEM((rows + 2 * half_k, width, cb), F32)],
        compiler_params=_params())(pa, da1, conv_w)


def _log_sigmoid(x):
    return jnp.minimum(x, 0.0) - jnp.log(1.0 + jnp.exp(-jnp.abs(x)))


def _decay_fwd(pb, up2, bias2, tm, lr_blk):
    t_all = pb.shape[0]
    n2 = up2.shape[1]

    def body(lr_ref, up_ref, b_ref, g_ref):
        logits = _mm_hi(lr_ref[...], up_ref[...]) + b_ref[...]
        g_ref[...] = _log_sigmoid(logits) * (1.0 / GATE_TAU)

    return pl.pallas_call(
        body, name="decay_fwd", grid=(t_all // tm,),
        in_specs=[pl.BlockSpec((tm, LANE), lambda i: (i, lr_blk)),
                  pl.BlockSpec(up2.shape, lambda i: (0, 0)),
                  pl.BlockSpec((1, n2), lambda i: (0, 0))],
        out_specs=pl.BlockSpec((tm, n2), lambda i: (i, 0)),
        out_shape=jax.ShapeDtypeStruct((t_all, n2), F32),
        compiler_params=_params())(pb, up2, bias2)


def _decay_bwd(pb, up2, bias2, grads_f, grads_b, tiles, lr_blk, dk_, dv_):
    t_all = pb.shape[0]
    tm = tiles.tm
    n2 = up2.shape[1]
    nbw = 2 * dk_ + dv_ + LANE

    def body(lr_ref, up_ref, b_ref, dqf, dkf, dvf, dgf, dqb, dkb, dvb, dgb, dp_ref, dup_ref, dbias_ref):
        i = pl.program_id(0)
        pad = tiles.is_pad(i)
        live = lambda v: jnp.where(pad, 0.0, v)

        @pl.when(i == 0)
        def _():
            dup_ref[...] = jnp.zeros_like(dup_ref)
            dbias_ref[...] = jnp.zeros_like(dbias_ref)

        lr = lr_ref[...]
        up = up_ref[...]
        logits = _mm_hi(lr, up) + b_ref[...]
        dg = live(jnp.concatenate([dgf[...], dgb[...]], axis=1))
        dlog = dg * (1.0 / GATE_TAU) * _sigmoid(-logits)
        dup_ref[...] += _mm_tn_hi(lr, dlog)
        dbias_ref[...] += jnp.sum(dlog, axis=0, keepdims=True)
        dp_ref[:, 0:dk_] = live(dqf[...] + dqb[...]).astype(BF16)
        dp_ref[:, dk_:2 * dk_] = live(dkf[...] + dkb[...]).astype(BF16)
        dp_ref[:, 2 * dk_:2 * dk_ + dv_] = live(dvf[...] + dvb[...]).astype(BF16)
        dp_ref[:, 2 * dk_ + dv_:nbw] = _mm_nt_hi(dlog, up).astype(BF16)

    row = lambda w: pl.BlockSpec((tm, w), lambda i: (i, 0))
    return pl.pallas_call(
        body, name="decay_bwd", grid=(t_all // tm,),
        in_specs=[pl.BlockSpec((tm, LANE), lambda i: (i, lr_blk)),
                  pl.BlockSpec(up2.shape, lambda i: (0, 0)),
                  pl.BlockSpec((1, n2), lambda i: (0, 0)),
                  row(dk_), row(dk_), row(dv_), row(dk_), row(dk_), row(dk_), row(dv_), row(dk_)],
        out_specs=(row(nbw), pl.BlockSpec(up2.shape, lambda i: (0, 0)), pl.BlockSpec((1, n2), lambda i: (0, 0))),
        out_shape=(jax.ShapeDtypeStruct((t_all, nbw), BF16), jax.ShapeDtypeStruct(up2.shape, F32),
                   jax.ShapeDtypeStruct((1, n2), F32)),
        compiler_params=_params())(pb, up2, bias2, *grads_f, *grads_b)


def _scan_chunk(s, nl, nc, rev):
    if rev:
        return jnp.where(s < nc, nl + (nc - 1 - s), nl - 1 - (s - nc))
    return jnp.where(s < nc, nl + s, s - nc)


def _scan_lat_chunk(s, nl, nc, rev):
    first = nl - 1 if rev else 0
    return jnp.where(s < nc, first, _scan_chunk(s, nl, nc, rev))


def _tri_mm(m_bf, x):
    hi = x.astype(BF16)
    r1 = x - hi.astype(F32)
    mid = r1.astype(BF16)
    lo = (r1 - mid.astype(F32)).astype(BF16)
    dot = lambda p: jnp.dot(m_bf, p, preferred_element_type=F32)
    return dot(hi) + dot(mid) + dot(lo)


def _chunk_masks(c, rev):
    ii = lax.broadcasted_iota(jnp.int32, (c, c), 0)
    jj = lax.broadcasted_iota(jnp.int32, (c, c), 1)
    return ((ii <= jj), (ii >= jj)) if rev else ((ii >= jj), (ii <= jj))


def _chunk_terms(q, k, b, far, mid):
    bf, bm = b[far:far + 1, :], b[mid:mid + 1, :]
    e = jnp.exp(b)
    em = jnp.exp(b - bm)
    eim = jnp.exp(bm - b)
    ed = jnp.exp(bf - b)
    return dict(e=e, em=em, eim=eim, ed=ed, dec=jnp.exp(bf), qe=q * e, qem=q * em, kim=k * eim, kd=k * ed)


def _gla_fwd(pb3, g3, nb, s_len, c_len, dk_, dv_):
    c = CHUNK
    nl, nc = s_len // c, c_len // c
    ns = nl + nc
    hk, hv = dk_ // HEADS, dv_ // HEADS
    l_len = pb3.shape[1]
    scale = hk ** -0.5
    mid = c // 2

    def body(*refs):
        ins, outs, z_scr = refs[:8], refs[8:14], refs[14]
        s = pl.program_id(0)

        @pl.when(s == 0)
        def _():
            z_scr[...] = jnp.zeros_like(z_scr)

        qs = jnp.where(s >= nc, scale, 0.0)
        for di, rev in enumerate((False, True)):
            q_ref, k_ref, v_ref, g_ref = ins[4 * di:4 * di + 4]
            o_ref, zs_ref, b_ref = outs[3 * di:3 * di + 3]
            mask, _ = _chunk_masks(c, rev)
            m_bf = mask.astype(BF16)
            far = 0 if rev else c - 1
            for b in range(nb):
                bc = _tri_mm(m_bf, g_ref[b])
                b_ref[b] = bc
                for h in range(HEADS):
                    ks, vs = slice(h * hk, (h + 1) * hk), slice(h * hv, (h + 1) * hv)
                    zi = (di * nb + b) * HEADS + h
                    v = v_ref[b, :, vs]
                    t = _chunk_terms(q_ref[b, :, ks] * qs, k_ref[b, :, ks], bc[:, ks], far, mid)
                    a = jnp.where(mask, _mm_nt(t["qem"], t["kim"]), 0.0)
                    z = z_scr[zi]
                    zs_ref[0, b * HEADS + h] = z
                    o_ref[b, :, vs] = _mm(a, v) + _mm_nt(t["qe"], z)
                    z_scr[zi] = z * t["dec"] + _mm_tn(v, t["kd"])

    in_specs, out_specs, out_shape = [], [], []
    for di, rev in enumerate((False, True)):
        ch = functools.partial(_scan_chunk, nl=nl, nc=nc, rev=rev)
        lch = functools.partial(_scan_lat_chunk, nl=nl, nc=nc, rev=rev)
        in_specs += [pl.BlockSpec((nb, c, dk_), lambda s, ch=ch: (0, ch(s), 0)),
                     pl.BlockSpec((nb, c, dk_), lambda s, ch=ch: (0, ch(s), 1)),
                     pl.BlockSpec((nb, c, dv_), lambda s, ch=ch: (0, ch(s), 1)),
                     pl.BlockSpec((nb, c, dk_), lambda s, ch=ch, di=di: (0, ch(s), di))]
        out_specs += [pl.BlockSpec((nb, c, dv_), lambda s, lch=lch: (0, lch(s), 0)),
                      pl.BlockSpec((1, nb * HEADS, hv, hk), lambda s: (s, 0, 0, 0)),
                      pl.BlockSpec((nb, c, dk_), lambda s, ch=ch: (0, ch(s), 0))]
        out_shape += [jax.ShapeDtypeStruct((nb, s_len, dv_), F32),
                      jax.ShapeDtypeStruct((ns, nb * HEADS, hv, hk), F32),
                      jax.ShapeDtypeStruct((nb, l_len, dk_), F32)]
    return pl.pallas_call(
        body, name="gla_fwd", grid=(ns,), in_specs=in_specs, out_specs=tuple(out_specs), out_shape=tuple(out_shape),
        scratch_shapes=[pltpu.VMEM((2 * nb * HEADS, hv, hk), F32)],
        compiler_params=_params())(pb3, pb3, pb3, g3, pb3, pb3, pb3, g3)


def _gla_bwd(pb3, do3, fwd_saved, nb, s_len, c_len, dk_, dv_):
    c = CHUNK
    nl, nc = s_len // c, c_len // c
    ns = nl + nc
    hk, hv = dk_ // HEADS, dv_ // HEADS
    l_len = pb3.shape[1]
    scale = hk ** -0.5
    mid = c // 2
    zs_f, b_f, zs_b, b_b = fwd_saved

    def body(*refs):
        ins, outs, dz_scr = refs[:12], refs[12:20], refs[20]
        s = pl.program_id(0)
        step = ns - 1 - s

        @pl.when(s == 0)
        def _():
            dz_scr[...] = jnp.zeros_like(dz_scr)

        lat = step >= nc
        qs = jnp.where(lat, scale, 0.0)
        dmul = jnp.where(lat, 1.0, 0.0)
        for di, rev in enumerate((False, True)):
            q_ref, k_ref, v_ref, b_ref, do_ref, zs_ref = ins[6 * di:6 * di + 6]
            dq_ref, dk_ref, dv_ref, dg_ref = outs[4 * di:4 * di + 4]
            mask, mask_t = _chunk_masks(c, rev)
            mt_bf = mask_t.astype(BF16)
            far = 0 if rev else c - 1
            far_row = lax.broadcasted_iota(jnp.int32, (c, hk), 0) == far
            for b in range(nb):
                db_parts = []
                for h in range(HEADS):
                    ks, vs = slice(h * hk, (h + 1) * hk), slice(h * hv, (h + 1) * hv)
                    zi = (di * nb + b) * HEADS + h
                    v = v_ref[b, :, vs]
                    d_o = do_ref[b, :, vs] * dmul
                    t = _chunk_terms(q_ref[b, :, ks] * qs, k_ref[b, :, ks], b_ref[b, :, ks], far, mid)
                    qem, kim, qe, kd = t["qem"], t["kim"], t["qe"], t["kd"]
                    a_t = jnp.where(mask_t, _mm_nt(kim, qem), 0.0)
                    d_a = jnp.where(mask, _mm_nt(d_o, v), 0.0)
                    d_at = jnp.where(mask_t, _mm_nt(v, d_o), 0.0)
                    z = zs_ref[0, b * HEADS + h]
                    dzn = dz_scr[zi]
                    dv_ref[b, :, vs] = _mm(a_t, d_o) + _mm_nt(kd, dzn)
                    dqem = _mm(d_a, kim)
                    dkim = _mm(d_at, qem)
                    dqe = _mm(d_o, z)
                    dkd = _mm(v, dzn)
                    ddec = jnp.sum(z * dzn, axis=0, keepdims=True)
                    dz_scr[zi] = dzn * t["dec"] + _mm_tn(d_o, qe)
                    dq_ref[b, :, ks] = (dqem * t["em"] + dqe * t["e"]) * qs
                    dk_ref[b, :, ks] = dkim * t["eim"] + dkd * t["ed"]
                    db = dqem * qem - dkim * kim + dqe * qe - dkd * kd
                    extra = jnp.sum(dkd * kd, axis=0, keepdims=True) + ddec * t["dec"]
                    db_parts.append(db + jnp.where(far_row, extra, 0.0))
                dg_ref[b] = _tri_mm(mt_bf, jnp.concatenate(db_parts, axis=1))

    in_specs, out_specs, out_shape, args = [], [], [], []
    for di, rev in enumerate((False, True)):
        ch = lambda s, rev=rev: _scan_chunk(ns - 1 - s, nl, nc, rev)
        lch = lambda s, rev=rev: _scan_lat_chunk(ns - 1 - s, nl, nc, rev)
        in_specs += [pl.BlockSpec((nb, c, dk_), lambda s, ch=ch: (0, ch(s), 0)),
                     pl.BlockSpec((nb, c, dk_), lambda s, ch=ch: (0, ch(s), 1)),
                     pl.BlockSpec((nb, c, dv_), lambda s, ch=ch: (0, ch(s), 1)),
                     pl.BlockSpec((nb, c, dk_), lambda s, ch=ch: (0, ch(s), 0)),
                     pl.BlockSpec((nb, c, dv_), lambda s, lch=lch: (0, lch(s), 0)),
                     pl.BlockSpec((1, nb * HEADS, hv, hk), lambda s: (ns - 1 - s, 0, 0, 0))]
        args += [pb3, pb3, pb3, (b_b if rev else b_f), do3, (zs_b if rev else zs_f)]
        for w in (dk_, dk_, dv_, dk_):
            out_specs.append(pl.BlockSpec((nb, c, w), lambda s, ch=ch: (0, ch(s), 0)))
            out_shape.append(jax.ShapeDtypeStruct((nb, l_len, w), F32))
    return pl.pallas_call(
        body, name="gla_bwd", grid=(ns,), in_specs=in_specs, out_specs=tuple(out_specs), out_shape=tuple(out_shape),
        scratch_shapes=[pltpu.VMEM((2 * nb * HEADS, hv, hk), F32)],
        compiler_params=_params())(*args)


def _tail(a1, pa, o_f, o_b, x2, tgt, mod, wc, wg, wo, ln_g, ln_b, gn_t, fg, nb, tm):
    tl, d = x2.shape
    nt = tl // tm
    per_ex = nt // nb
    hv = d // HEADS
    nrow = mod.shape[0]

    def body(a1_ref, z_ref, r_ref, mc_ref, mg_ref, of_ref, ob_ref, x_ref, t_ref, mod_ref, wc_ref, wg_ref, wo_ref,
             lng_ref, lnb_ref, gn_ref, fg_ref,
             dp_ref, da1_ref, do_ref, gx_ref, mrg_ref, dmo_ref, yci_ref, dyc_ref, ogi_ref, dyg_ref, sm_ref):
        i = pl.program_id(0)

        @pl.when(i == 0)
        def _():
            sm_ref[...] = jnp.zeros_like(sm_ref)

        bidx = i // per_ex
        gate = _rowsel(mod_ref[...], bidx, nb)[:, 2 * d:3 * d]
        lng, lnb, gn, fgv = lng_ref[...], lnb_ref[...], gn_ref[...], fg_ref[...]
        wc_, wg_, wo_ = wc_ref[...], wg_ref[...], wo_ref[...]

        a1v = a1_ref[...]
        mu = jnp.mean(a1v, axis=-1, keepdims=True)
        xc = a1v - mu
        rs = lax.rsqrt(jnp.mean(xc * xc, axis=-1, keepdims=True) + EPS)
        xh = xc * rs
        a2 = xh * lng + lnb
        s2 = _sigmoid(a2)
        a3 = a2 * s2
        zv = z_ref[...]
        sz = _sigmoid(zv)
        siluz = zv * sz
        ycin = a3 * siluz
        yconv = _mm(ycin, wc_)

        o = of_ref[...] + ob_ref[...]
        ohat_parts, rn_parts = [], []
        for h in range(HEADS):
            oh = o[:, h * hv:(h + 1) * hv]
            rn = lax.rsqrt(jnp.mean(oh * oh, axis=-1, keepdims=True) + EPS)
            ohat_parts.append(oh * rn)
            rn_parts.append(rn)
        ohat = jnp.concatenate(ohat_parts, axis=1)
        on = ohat * gn
        rv = r_ref[...]
        sr = _sigmoid(rv)
        silur = rv * sr
        ogin = on * silur
        ygla = _mm(ogin, wg_)

        sc = _sigmoid(mc_ref[...])
        sg = _sigmoid(mg_ref[...])
        merged = sc * yconv + sg * ygla
        mo = _mm(merged, wo_)
        hn = x_ref[...] + gate * mo
        rf = lax.rsqrt(jnp.mean(hn * hn, axis=-1, keepdims=True) + EPS)
        yh = hn * rf
        err = yh * fgv - t_ref[...]
        loss_part = 0.5 * jnp.sum(err * err) * (1.0 / d)

        dy = err * (1.0 / d)
        dfg = jnp.sum(dy * yh, axis=0, keepdims=True)
        dyh = dy * fgv
        dhn = rf * (dyh - yh * jnp.mean(dyh * yh, axis=-1, keepdims=True))
        gx_ref[...] = dhn
        dgate = jnp.sum(dhn * mo, axis=0, keepdims=True)
        dmo = gate * dhn
        dmerged = _mm_nt(dmo, wo_)
        dyconv = dmerged * sc
        dygla = dmerged * sg
        dp_ref[:, 2 * d:3 * d] = (dmerged * yconv * sc * (1.0 - sc)).astype(BF16)
        dp_ref[:, 3 * d:4 * d] = (dmerged * ygla * sg * (1.0 - sg)).astype(BF16)
        dycin = _mm_nt(dyconv, wc_)
        dogin = _mm_nt(dygla, wg_)
        mrg_ref[...] = merged.astype(BF16)
        dmo_ref[...] = dmo.astype(BF16)
        yci_ref[...] = ycin.astype(BF16)
        dyc_ref[...] = dyconv.astype(BF16)
        ogi_ref[...] = ogin.astype(BF16)
        dyg_ref[...] = dygla.astype(BF16)

        da3 = dycin * siluz
        dp_ref[:, 0:d] = (dycin * a3 * _dsilu(zv, sz)).astype(BF16)
        da2 = da3 * _dsilu(a2, s2)
        dlng = jnp.sum(da2 * xh, axis=0, keepdims=True)
        dlnb = jnp.sum(da2, axis=0, keepdims=True)
        dxh = da2 * lng
        da1_ref[...] = rs * (dxh - jnp.mean(dxh, axis=-1, keepdims=True)
                             - xh * jnp.mean(dxh * xh, axis=-1, keepdims=True))

        don = dogin * silur
        dp_ref[:, d:2 * d] = (dogin * on * _dsilu(rv, sr)).astype(BF16)
        dgn = jnp.sum(don * ohat, axis=0, keepdims=True)
        dyn = don * gn
        for h in range(HEADS):
            vs = slice(h * hv, (h + 1) * hv)
            oh_hat = ohat_parts[h]
            dh = dyn[:, vs]
            do_ref[:, vs] = rn_parts[h] * (dh - oh_hat * jnp.mean(dh * oh_hat, axis=-1, keepdims=True))

        sm_ref[0:1, :] += dfg
        sm_ref[1:2, :] += dlng
        sm_ref[2:3, :] += dlnb
        sm_ref[3:4, :] += dgn
        sm_ref[4:5, :] += jnp.zeros((1, d), F32) + loss_part
        for b in range(nb):
            sm_ref[8 + b:9 + b, :] += jnp.where(bidx == b, dgate, 0.0)

    row = pl.BlockSpec((tm, d), lambda i: (i, 0))
    pcol = lambda blk: pl.BlockSpec((tm, d), lambda i: (i, blk))
    full = lambda arr: pl.BlockSpec(arr.shape, lambda i: (0,) * arr.ndim)
    bfo = jax.ShapeDtypeStruct((tl, d), BF16)
    f32o = jax.ShapeDtypeStruct((tl, d), F32)
    return pl.pallas_call(
        body, name="tail", grid=(nt,),
        in_specs=[row, pcol(2), pcol(3), pcol(4), pcol(5), row, row, row, row, full(mod), full(wc), full(wg),
                  full(wo), full(ln_g), full(ln_b), full(gn_t), full(fg)],
        out_specs=(pl.BlockSpec((tm, 4 * d), lambda i: (i, 0)), row, row, row, row, row, row, row, row, row,
                   pl.BlockSpec((16, d), lambda i: (0, 0))),
        out_shape=(jax.ShapeDtypeStruct((tl, 4 * d), BF16), f32o, f32o, f32o, bfo, bfo, bfo, bfo, bfo, bfo,
                   jax.ShapeDtypeStruct((16, d), F32)),
        compiler_params=_params())(a1, pa, pa, pa, pa, o_f, o_b, x2, tgt, mod, wc, wg, wo, ln_g, ln_b, gn_t, fg)


def _local_step(x, c, ctx, tgt, c_ctx, ada_w, ada_b, norm_g, w_a, b_a, w_b, b_b, conv_w, conv_b, ln_g, ln_b,
                conv_proj, up2, bias2, gla_norm_g, gla_proj, w_out, final_norm_g):
    nb, s_len, d = x.shape
    c_len = ctx.shape[1]
    dk_, dv_ = d // 2, d
    tl, tc = nb * s_len, nb * c_len
    nbw = 2 * dk_ + dv_ + LANE
    tm = math.gcd(256, c_len)
    tiles = _Tiles(nb, s_len, c_len, tm, 2)
    tmm = tiles.big * tm
    l_len = tiles.rows_per_ex
    t_all = nb * l_len
    x2, ctx2, tgt2 = x.reshape(tl, d), ctx.reshape(tc, d), tgt.reshape(tl, d)

    cv = jnp.zeros((8, d), F32).at[0:nb].set(c).at[nb].set(c_ctx.reshape(d))
    mod = _ada_fwd(cv, ada_w, ada_b)
    u = _norm_fwd(x2, ctx2, mod, norm_g, tiles)
    pa = _matmul_bias("inproj_a", u, w_a, b_a, tl, tmm, _tile(6 * d, 1536), u_tile=tiles.big_all_of_lat)
    pb = _matmul_bias("inproj_b", u, w_b, b_b, t_all, tmm, nbw)

    cb = min(LANE, d // 2)
    a1 = _conv_fwd(pa, conv_w, conv_b, nb, s_len, cb)
    lr_blk = (2 * dk_ + dv_) // LANE
    g_all = _decay_fwd(pb, up2, bias2, tm, lr_blk)
    pb3 = pb.reshape(nb, l_len, nbw)
    o_f, zs_f, b_f, o_b, zs_b, b_b2 = _gla_fwd(pb3, g_all.reshape(nb, l_len, 2 * dk_), nb, s_len, c_len, dk_, dv_)

    gn_t = jnp.tile(gla_norm_g, (1, HEADS))
    tt = math.gcd(128, s_len)
    (dp_a2, da1, d_o, gx1, merged, dmo, ycin, dyconv, ogin, dygla, small) = _tail(
        a1, pa, o_f.reshape(tl, dv_), o_b.reshape(tl, dv_), x2, tgt2, mod, conv_proj, gla_proj, w_out, ln_g, ln_b,
        gn_t, final_norm_g, nb, tt)

    tk = math.gcd(512, tl)
    d_w_out, _ = _matmul_tn("dw_out", merged, dmo, tl, tk, _tile(d, 1024))
    d_conv_proj, _ = _matmul_tn("dw_conv_proj", ycin, dyconv, tl, tk, _tile(d, 1024))
    d_gla_proj, _ = _matmul_tn("dw_gla_proj", ogin, dygla, tl, tk, _tile(d, 1024))

    dp_a1, d_conv_w, d_conv_b = _conv_bwd(pa, da1, conv_w, nb, s_len, cb)
    gl = _gla_bwd(pb3, d_o.reshape(nb, s_len, dv_), (zs_f, b_f, zs_b, b_b2), nb, s_len, c_len, dk_, dv_)
    gl = [g_.reshape(t_all, g_.shape[-1]) for g_ in gl]
    dp_b, d_up2, d_bias2 = _decay_bwd(pb, up2, bias2, gl[0:4], gl[4:8], tiles, lr_blk, dk_, dv_)

    tka = _tile(2 * d, 1024)
    du_a1 = _matmul_nt("du_a1", dp_a1, w_a, 0, tmm, tka)
    du_a2 = _matmul_nt("du_a2", dp_a2, w_a, (2 * d) // tka, tmm, tka)
    du_b = _matmul_nt("du_b", dp_b, w_b, 0, tmm, nbw)
    dw_a1, db_a1 = _matmul_tn("dw_a1", u, dp_a1, tl, tmm, _tile(2 * d, 1024), a_tile=tiles.big_all_of_lat)
    dw_a2, db_a2 = _matmul_tn("dw_a2", u, dp_a2, tl, tmm, _tile(4 * d, 1024), a_tile=tiles.big_all_of_lat)
    dw_b, db_b = _matmul_tn("dw_b", u, dp_b, t_all, tmm, nbw)

    grad_x2, dmod_ss, d_norm_g = _norm_bwd(x2, ctx2, mod, norm_g, [du_a1, du_a2], du_b, gx1, tiles)
    dmod = dmod_ss.at[0:nb, 2 * d:3 * d].set(small[8:8 + nb])
    d_ada_w, d_ada_b, d_cv = _ada_bwd(cv, ada_w, dmod)

    return dict(
        loss=small[4, 0], grad_x=grad_x2.reshape(nb, s_len, d), c_ctx=d_cv[nb], ada_w=d_ada_w, ada_b=d_ada_b,
        norm_g=d_norm_g, w_a=jnp.concatenate([dw_a1, dw_a2], axis=1), b_a=jnp.concatenate([db_a1, db_a2], axis=1),
        w_b=dw_b, b_b=db_b, conv_w=d_conv_w, conv_b=d_conv_b, conv_ln_g=small[1:2], conv_ln_b=small[2:3],
        conv_proj=d_conv_proj, up2=d_up2, bias2=d_bias2,
        gla_norm_g=small[3:4].reshape(HEADS, d // HEADS).sum(axis=0, keepdims=True),
        gla_proj=d_gla_proj, w_out=d_w_out, final_norm_g=small[0:1])


def _regroup(o, d, r):
    cb = min(LANE, d // 2)
    glu = []
    for j in range(d // cb):
        glu += [o[..., j * cb:(j + 1) * cb], o[..., d + j * cb:d + (j + 1) * cb]]
    a = jnp.concatenate(glu + [o[..., 2 * d:3 * d], o[..., 5 * d + 2 * r:8 * d + 2 * r]], axis=-1)
    pad = jnp.zeros(o.shape[:-1] + (LANE - 2 * r,), o.dtype)
    b = jnp.concatenate([o[..., 3 * d:5 * d + 2 * r], pad], axis=-1)
    return a, b


def _ungroup(a, b, d, r):
    cb = min(LANE, d // 2)
    n = d // cb
    gv = [a[..., 2 * j * cb:(2 * j + 1) * cb] for j in range(n)]
    gg = [a[..., (2 * j + 1) * cb:(2 * j + 2) * cb] for j in range(n)]
    return jnp.concatenate(gv + gg + [a[..., 2 * d:3 * d], b[..., 0:2 * d + 2 * r], a[..., 3 * d:6 * d]], axis=-1)


def _mesh_pos():
    return lax.axis_index("x"), lax.axis_index("y"), lax.axis_index("c")


_ANY = pl.BlockSpec(memory_space=pl.ANY)


def _all_gather(arrs):
    n = len(arrs)

    def body(*refs):
        ins, outs = refs[:n], refs[n:2 * n]
        send_sems, recv_sems, local_sems = refs[2 * n:]
        x, y, c = _mesh_pos()
        me, sibling = (x, y, c), (x, y, 1 - c)
        chips = [(1 - x, y), (x, 1 - y), (1 - x, 1 - y)]

        def slot(a, pos):
            return outs[a].at[4 * pos[0] + 2 * pos[1] + pos[2]]

        def copy(a, k, block, to, src=None):
            return pltpu.make_async_remote_copy(
                src_ref=slot(a, block) if src is None else src, dst_ref=slot(a, block),
                send_sem=send_sems.at[7 * a + k], recv_sem=recv_sems.at[7 * a + k],
                device_id=to, device_id_type=MESH)

        mine = [pltpu.make_async_copy(ins[a], slot(a, me), local_sems.at[a]) for a in range(n)]
        for cp in mine:
            cp.start()
        first = []
        for a in range(n):
            first.append(copy(a, 0, me, sibling, src=ins[a]))
            first += [copy(a, 1 + j, me, (*chip, c), src=ins[a]) for j, chip in enumerate(chips)]
        for cp in first:
            cp.start()
        passed = []
        for j, chip in enumerate(chips):
            for a in range(n):
                copy(a, 1 + j, (*chip, c), me).wait_recv()
                fwd = copy(a, 4 + j, (*chip, c), sibling)
                fwd.start()
                passed.append(fwd)
        for a in range(n):
            copy(a, 0, sibling, me).wait_recv()
            for j, chip in enumerate(chips):
                copy(a, 4 + j, (*chip, 1 - c), me).wait_recv()
        for cp in first + passed:
            cp.wait_send()
        for cp in mine:
            cp.wait()

    return pl.pallas_call(
        body, name="all_gather",
        out_shape=tuple(jax.ShapeDtypeStruct((N_DEV,) + a.shape, a.dtype) for a in arrs),
        in_specs=[_ANY] * n, out_specs=tuple([_ANY] * n),
        scratch_shapes=[pltpu.SemaphoreType.DMA((7 * n,)), pltpu.SemaphoreType.DMA((7 * n,)),
                        pltpu.SemaphoreType.DMA((n,))],
    )(*arrs)


def _exchange_sibling(arrs):
    n = len(arrs)

    def body(*refs):
        ins, outs = refs[:n], refs[n:2 * n]
        send_sems, recv_sems = refs[2 * n:]
        x, y, c = _mesh_pos()
        copies = [pltpu.make_async_remote_copy(
            src_ref=ins[a].at[2 * k + (1 - c)], dst_ref=outs[a].at[k],
            send_sem=send_sems.at[4 * a + k], recv_sem=recv_sems.at[4 * a + k],
            device_id=(x, y, 1 - c), device_id_type=MESH) for a in range(n) for k in range(4)]
        for cp in copies:
            cp.start()
        for cp in copies:
            cp.wait_recv()
        for cp in copies:
            cp.wait_send()

    return pl.pallas_call(
        body, name="grad_exchange_sibling",
        out_shape=tuple(jax.ShapeDtypeStruct((4,) + a.shape[1:], a.dtype) for a in arrs),
        in_specs=[_ANY] * n, out_specs=tuple([_ANY] * n),
        scratch_shapes=[pltpu.SemaphoreType.DMA((4 * n,)), pltpu.SemaphoreType.DMA((4 * n,))],
    )(*arrs)


def _pair_sum(name, mine, theirs):
    _, r, cdim = mine.shape
    tr = r if (r % 8 or r <= 256) else math.gcd(r, 256)

    def body(m_ref, t_ref, o_ref):
        c = lax.axis_index("c")
        own = jnp.where(c == 0, m_ref[:, 0].astype(F32), m_ref[:, 1].astype(F32))
        o_ref[...] = (own + t_ref[...].astype(F32)).astype(o_ref.dtype)

    return pl.pallas_call(
        body, name=name, grid=(r // tr,),
        in_specs=[pl.BlockSpec((4, 2, tr, cdim), lambda i: (0, 0, i, 0)),
                  pl.BlockSpec((4, tr, cdim), lambda i: (0, i, 0))],
        out_specs=pl.BlockSpec((4, tr, cdim), lambda i: (0, i, 0)),
        out_shape=jax.ShapeDtypeStruct((4, r, cdim), mine.dtype),
        compiler_params=_params())(mine.reshape(4, 2, r, cdim), theirs)


def _exchange_chips(arrs):
    n = len(arrs)

    def body(*refs):
        ins, outs = refs[:n], refs[n:2 * n]
        send_sems, recv_sems, local_sems = refs[2 * n:]
        x, y, c = _mesh_pos()
        my_chip = 2 * x + y
        mine = [pltpu.make_async_copy(ins[a].at[my_chip], outs[a].at[my_chip], local_sems.at[a]) for a in range(n)]
        for cp in mine:
            cp.start()
        copies = []
        for rel in range(1, 4):
            px = 1 - x if rel & 2 else x
            py = 1 - y if rel & 1 else y
            for a in range(n):
                copies.append(pltpu.make_async_remote_copy(
                    src_ref=ins[a].at[2 * px + py], dst_ref=outs[a].at[my_chip],
                    send_sem=send_sems.at[3 * a + rel - 1], recv_sem=recv_sems.at[3 * a + rel - 1],
                    device_id=(px, py, c), device_id_type=MESH))
        for cp in copies:
            cp.start()
        for cp in copies:
            cp.wait_recv()
        for cp in copies:
            cp.wait_send()
        for cp in mine:
            cp.wait()

    return pl.pallas_call(
        body, name="grad_exchange_chips",
        out_shape=tuple(jax.ShapeDtypeStruct(a.shape, a.dtype) for a in arrs),
        in_specs=[_ANY] * n, out_specs=tuple([_ANY] * n),
        scratch_shapes=[pltpu.SemaphoreType.DMA((3 * n,)), pltpu.SemaphoreType.DMA((3 * n,)),
                        pltpu.SemaphoreType.DMA((n,))],
    )(*arrs)


def _sum_adam(name, parts, w, m, v):
    r, cdim = w.shape
    n_parts = parts.shape[0]
    tr = r if (r % 8 or r <= 256) else math.gcd(r, 256)
    bc1 = 1.0 - ADAM_B1 ** ADAM_STEP
    bc2 = 1.0 - ADAM_B2 ** ADAM_STEP

    def body(p_ref, w_ref, m_ref, v_ref, g_ref, d_ref, nm_ref, nv_ref):
        g = p_ref[0].astype(F32)
        for k in range(1, n_parts):
            g = g + p_ref[k].astype(F32)
        mn = ADAM_B1 * m_ref[...] + (1.0 - ADAM_B1) * g
        vn = ADAM_B2 * v_ref[...] + (1.0 - ADAM_B2) * (g * g)
        g_ref[...] = g
        nm_ref[...] = mn
        nv_ref[...] = vn
        d_ref[...] = -ADAM_LR * ((mn / bc1) / (jnp.sqrt(vn / bc2) + ADAM_EPS) + ADAM_WD * w_ref[...])

    blk = pl.BlockSpec((tr, cdim), lambda i: (i, 0))
    o = jax.ShapeDtypeStruct((r, cdim), F32)
    return pl.pallas_call(
        body, name=name, grid=(r // tr,),
        in_specs=[pl.BlockSpec((n_parts, tr, cdim), lambda i: (0, i, 0)), blk, blk, blk],
        out_specs=(blk, blk, blk, blk), out_shape=(o, o, o, o),
        compiler_params=_params())(parts, w, m, v)


_SMALL = ("c_ctx", "ada_b", "norm_g", "b_in", "conv_b", "conv_ln_g", "conv_ln_b", "decay_bias_fwd",
          "decay_bias_bwd", "gla_norm_g", "final_norm_g")
_WEIGHTS = ("c_ctx", "ada_w", "ada_b", "norm_g", "w_in", "b_in", "conv_w", "conv_b", "conv_ln_g", "conv_ln_b",
            "conv_proj", "decay_up_fwd", "decay_bias_fwd", "decay_up_bwd", "decay_bias_bwd", "gla_norm_g",
            "gla_proj", "w_out", "final_norm_g")


def _as2d(a):
    if a.ndim == 1:
        return a.reshape(1, -1)
    return a.reshape(-1, a.shape[-1])


def kernel(x, c, ctx, c_ctx, ada_w, ada_b, norm_g, w_in, b_in, conv_w, conv_b, conv_ln_g, conv_ln_b, conv_proj, decay_up_fwd, decay_bias_fwd, decay_up_bwd, decay_bias_bwd, gla_norm_g, gla_proj, w_out, final_norm_g, loss_target, m_c_ctx, m_ada_w, m_ada_b, m_norm_g, m_w_in, m_b_in, m_conv_w, m_conv_b, m_conv_ln_g, m_conv_ln_b, m_conv_proj, m_decay_up_fwd, m_decay_bias_fwd, m_decay_up_bwd, m_decay_bias_bwd, m_gla_norm_g, m_gla_proj, m_w_out, m_final_norm_g, v_c_ctx, v_ada_w, v_ada_b, v_norm_g, v_w_in, v_b_in, v_conv_w, v_conv_b, v_conv_ln_g, v_conv_ln_b, v_conv_proj, v_decay_up_fwd, v_decay_bias_fwd, v_decay_up_bwd, v_decay_bias_bwd, v_gla_norm_g, v_gla_proj, v_w_out, v_final_norm_g):
    env = dict(locals())
    wts = {k: env[k] for k in _WEIGHTS}
    d = x.shape[-1]
    r = decay_up_fwd.shape[1]
    dk_ = d // 2
    n_in = w_in.shape[-1] * N_DEV

    proj3 = jnp.concatenate([conv_proj[0], gla_proj[0], w_out[0]], axis=0).astype(BF16)
    small_pack = jnp.concatenate([
        jnp.pad(conv_w[0], ((0, 32 - conv_w.shape[1]), (0, 0))),
        jnp.concatenate([decay_up_fwd[0], decay_up_bwd[0]], axis=1)], axis=0)
    g_win, g_ada, g_proj, g_small = _all_gather(
        [w_in[0].astype(BF16), ada_w[0].astype(BF16), proj3, small_pack])

    w_in_full = g_win.transpose(1, 0, 2).reshape(d, n_in)
    w_a, w_b = _regroup(w_in_full, d, r)
    ada_w_full = g_ada.transpose(1, 0, 2).reshape(d, 3 * d)
    ds = d // N_DEV
    conv_proj_full = g_proj[:, 0:ds].reshape(d, d)
    gla_proj_full = g_proj[:, ds:2 * ds].reshape(d, d)
    w_out_full = g_proj[:, 2 * ds:3 * ds].reshape(d, d)
    ktaps = conv_w.shape[1]
    conv_w_full = g_small[:, 0:ktaps].transpose(1, 0, 2).reshape(ktaps, d)
    up_f = g_small[:, 32:32 + r, 0:dk_ // N_DEV].transpose(1, 0, 2).reshape(r, dk_)
    up_b = g_small[:, 32:32 + r, dk_ // N_DEV:].transpose(1, 0, 2).reshape(r, dk_)
    up2 = jnp.zeros((LANE, 2 * dk_), F32).at[0:r, 0:dk_].set(up_f).at[r:2 * r, dk_:].set(up_b)
    bias2 = jnp.concatenate([decay_bias_fwd, decay_bias_bwd], axis=1)
    b_a, b_b = _regroup(b_in, d, r)

    g = _local_step(x, c, ctx, loss_target, c_ctx, ada_w_full, ada_b, norm_g[0:1], w_a, b_a, w_b, b_b,
                    conv_w_full, conv_b, conv_ln_g, conv_ln_b, conv_proj_full, up2, bias2, gla_norm_g,
                    gla_proj_full, w_out_full, final_norm_g.reshape(1, d))

    dw_in = _ungroup(g["w_a"], g["w_b"], d, r)
    e_win = dw_in.reshape(d, N_DEV, n_in // N_DEV).transpose(1, 0, 2).astype(BF16)
    e_ada = g["ada_w"].reshape(d, N_DEV, 3 * d // N_DEV).transpose(1, 0, 2).astype(BF16)
    e_proj = jnp.concatenate([g["conv_proj"].reshape(N_DEV, ds, d), g["gla_proj"].reshape(N_DEV, ds, d),
                              g["w_out"].reshape(N_DEV, ds, d)], axis=1).astype(BF16)
    d_up_f, d_up_b = g["up2"][0:r, 0:dk_], g["up2"][r:2 * r, dk_:]
    e_small = jnp.concatenate([
        jnp.pad(g["conv_w"], ((0, 32 - ktaps), (0, 0))).reshape(32, N_DEV, ds).transpose(1, 0, 2),
        jnp.concatenate([d_up_f.reshape(r, N_DEV, dk_ // N_DEV).transpose(1, 0, 2),
                         d_up_b.reshape(r, N_DEV, dk_ // N_DEV).transpose(1, 0, 2)], axis=2)], axis=1)
    mine = [e_win, e_ada, e_proj, e_small]
    theirs = _exchange_sibling(mine)
    chip_sums = [_pair_sum("pair_sum_" + nm, a, b)
                 for nm, a, b in zip(("w_in", "ada_w", "proj", "small"), mine, theirs)]
    x_win, x_ada, x_proj, x_small = _exchange_chips(chip_sums)

    db_in = _ungroup(g["b_a"], g["b_b"], d, r)
    small_g = dict(c_ctx=g["c_ctx"].reshape(1, d), ada_b=g["ada_b"], norm_g=g["norm_g"], b_in=db_in,
                   conv_b=g["conv_b"], conv_ln_g=g["conv_ln_g"], conv_ln_b=g["conv_ln_b"],
                   decay_bias_fwd=g["bias2"][:, 0:dk_], decay_bias_bwd=g["bias2"][:, dk_:],
                   gla_norm_g=g["gla_norm_g"], final_norm_g=g["final_norm_g"])
    sizes = [wts[k].size for k in _SMALL]
    n_small = sum(sizes) + 1
    n_pad = -n_small % LANE
    pack = lambda parts: jnp.concatenate([p.reshape(1, -1) for p in parts] + [jnp.zeros((1, n_pad + 1), F32)], axis=1)
    gpack = jnp.concatenate([small_g[k].reshape(1, -1) for k in _SMALL]
                            + [g["loss"].reshape(1, 1), jnp.zeros((1, n_pad), F32)], axis=1)
    (gpacks,) = _all_gather([gpack])
    sg, sd, sm, sv = _sum_adam("small_adam", gpacks, pack([wts[k] for k in _SMALL]),
                               pack([env["m_" + k] for k in _SMALL]), pack([env["v_" + k] for k in _SMALL]))

    out = {}
    off = 0
    for k, n in zip(_SMALL, sizes):
        for pre, arr in (("grad_", sg), ("delta_", sd), ("new_m_", sm), ("new_v_", sv)):
            out[pre + k] = arr[0, off:off + n].reshape(wts[k].shape)
        off += n
    loss = sg[0, off]

    def big(name, parts, wname):
        w2 = _as2d(wts[wname])
        res = _sum_adam(name, parts, w2, _as2d(env["m_" + wname]), _as2d(env["v_" + wname]))
        for pre, arr in zip(("grad_", "delta_", "new_m_", "new_v_"), res):
            out[pre + wname] = arr.reshape(wts[wname].shape)

    big("adam_w_in", x_win, "w_in")
    big("adam_ada_w", x_ada, "ada_w")
    big("adam_conv_proj", x_proj[:, 0:ds], "conv_proj")
    big("adam_gla_proj", x_proj[:, ds:2 * ds], "gla_proj")
    big("adam_w_out", x_proj[:, 2 * ds:3 * ds], "w_out")
    big("adam_conv_w", x_small[:, 0:ktaps], "conv_w")
    big("adam_up_f", x_small[:, 32:32 + r, 0:dk_ // N_DEV], "decay_up_fwd")
    big("adam_up_b", x_small[:, 32:32 + r, dk_ // N_DEV:], "decay_up_bwd")

    return (loss, g["grad_x"], *[out["grad_" + k] for k in _WEIGHTS], *[out["delta_" + k] for k in _WEIGHTS],
            *[out["new_m_" + k] for k in _WEIGHTS], *[out["new_v_" + k] for k in _WEIGHTS])
```

```python
import functools
import math

import jax
import jax.numpy as jnp
from jax import lax
from jax.experimental import pallas as pl
from jax.experimental.pallas import tpu as pltpu

F32 = jnp.float32
BF16 = jnp.bfloat16
MESH = pl.DeviceIdType.MESH

N_DEV = 8
GRID_W = 64
CHUNK = 128
HEADS = 4
EPS = 1e-6
GATE_TAU = 16.0
LANE = 128
ADAM_LR, ADAM_B1, ADAM_B2, ADAM_EPS, ADAM_WD, ADAM_STEP = 0.001, 0.9, 0.999, 1e-08, 0.01, 10
VMEM_LIMIT = 56 * 1024 * 1024


def _params(**kw):
    return pltpu.CompilerParams(vmem_limit_bytes=VMEM_LIMIT, **kw)


def _tile(n, pref):
    t = (min(pref, n) // LANE) * LANE
    while t >= LANE:
        if n % t == 0:
            return t
        t -= LANE
    return n


def _mm(a, b):
    return jnp.dot(a.astype(BF16), b.astype(BF16), preferred_element_type=F32)


def _mm_nt(a, b):
    return lax.dot_general(a.astype(BF16), b.astype(BF16), (((1,), (1,)), ((), ())), preferred_element_type=F32)


def _mm_tn(a, b):
    return lax.dot_general(a.astype(BF16), b.astype(BF16), (((0,), (0,)), ((), ())), preferred_element_type=F32)


def _mm_tn_hi(a, b):
    return lax.dot_general(a, b, (((0,), (0,)), ((), ())), precision=lax.Precision.HIGHEST, preferred_element_type=F32)


def _sigmoid(x):
    return 0.5 * jnp.tanh(0.5 * x) + 0.5


def _dsilu(x, s):
    return s * (1.0 + x * (1.0 - s))


def _rowsel(table, idx, n):
    out = table[0:1, :]
    for r in range(1, n):
        out = jnp.where(idx == r, table[r:r + 1, :], out)
    return out


def _ada_fwd(cv, ada_w, ada_b):
    def body(cv_ref, w_ref, b_ref, o_ref):
        c = cv_ref[...]
        o_ref[...] = _mm(c * _sigmoid(c), w_ref[...]) + b_ref[...]

    return pl.pallas_call(body, name="ada_fwd", out_shape=jax.ShapeDtypeStruct((cv.shape[0], ada_w.shape[1]), F32),
                          compiler_params=_params())(cv, ada_w, ada_b)


def _ada_bwd(cv, ada_w, dmod):
    def body(cv_ref, w_ref, dm_ref, dw_ref, db_ref, dc_ref):
        c = cv_ref[...]
        s = _sigmoid(c)
        dm = dm_ref[...]
        dw_ref[...] = _mm_tn_hi(c * s, dm)
        db_ref[...] = jnp.sum(dm, axis=0, keepdims=True)
        dc_ref[...] = _mm_nt(dm, w_ref[...]) * _dsilu(c, s)

    d, n3 = ada_w.shape
    return pl.pallas_call(
        body, name="ada_bwd",
        out_shape=(jax.ShapeDtypeStruct((d, n3), F32), jax.ShapeDtypeStruct((1, n3), F32),
                   jax.ShapeDtypeStruct(cv.shape, F32)),
        compiler_params=_params())(cv, ada_w, dmod)


class _Tiles:
    def __init__(self, nb, s_len, c_len, tm, big):
        self.nb, self.tm, self.big = nb, tm, big
        self.lat, self.ctx = s_len // tm, c_len // tm
        self.pad = -(self.lat + self.ctx) % big
        self.per_ex = self.lat + self.ctx + self.pad
        self.n_all, self.n_lat = nb * self.per_ex, nb * self.lat
        self.rows_per_ex = self.per_ex * tm

    def is_lat(self, i):
        return i % self.per_ex < self.lat

    def is_pad(self, i):
        return i % self.per_ex >= self.lat + self.ctx

    def lat_of_all(self, i):
        return (i // self.per_ex) * self.lat + jnp.minimum(i % self.per_ex, self.lat - 1)

    def ctx_of_all(self, i):
        return (i // self.per_ex) * self.ctx + jnp.clip(i % self.per_ex - self.lat, 0, self.ctx - 1)

    def big_all_of_lat(self, t):
        lat_big = self.lat // self.big
        return (t // lat_big) * (self.per_ex // self.big) + t % lat_big


def _norm_fwd(x2, ctx2, mod, norm_g, tiles):
    tl, d = x2.shape
    tc = ctx2.shape[0]
    nb, tm = tiles.nb, tiles.tm

    def body(x_ref, c_ref, mod_ref, g_ref, u_ref):
        i = pl.program_id(0)
        lat = tiles.is_lat(i)
        xv = jnp.where(lat, x_ref[...], c_ref[...])
        row = jnp.where(lat, i // tiles.per_ex, nb)
        m = _rowsel(mod_ref[...], row, nb + 1)
        shift, scale = m[:, 0:d], m[:, d:2 * d]
        rstd = lax.rsqrt(jnp.mean(xv * xv, axis=-1, keepdims=True) + EPS)
        u = xv * rstd * g_ref[...] * (1.0 + scale) + shift
        u_ref[...] = jnp.where(tiles.is_pad(i), 0.0, u).astype(BF16)

    return pl.pallas_call(
        body, name="norm_fwd", grid=(tiles.n_all,),
        in_specs=[pl.BlockSpec((tm, d), lambda i: (tiles.lat_of_all(i), 0)),
                  pl.BlockSpec((tm, d), lambda i: (tiles.ctx_of_all(i), 0)),
                  pl.BlockSpec(mod.shape, lambda i: (0, 0)),
                  pl.BlockSpec((1, d), lambda i: (0, 0))],
        out_specs=pl.BlockSpec((tm, d), lambda i: (i, 0)),
        out_shape=jax.ShapeDtypeStruct((tiles.n_all * tm, d), BF16),
        compiler_params=_params())(x2, ctx2, mod, norm_g)


def _norm_bwd(x2, ctx2, mod, norm_g, du_lat, du_b, gx1, tiles):
    tl, d = x2.shape
    nb, tm = tiles.nb, tiles.tm
    nrow = mod.shape[0]
    n_lat_in = len(du_lat)

    def body(x_ref, c_ref, mod_ref, g_ref, *refs):
        dl_refs = refs[:n_lat_in]
        d3_ref, gx_ref, gxo_ref, dmod_ref, dg_ref = refs[n_lat_in:]
        i = pl.program_id(0)

        @pl.when(i == 0)
        def _():
            dmod_ref[...] = jnp.zeros_like(dmod_ref)
            dg_ref[...] = jnp.zeros_like(dg_ref)

        lat = tiles.is_lat(i)
        xv = jnp.where(lat, x_ref[...], c_ref[...])
        row = jnp.where(lat, i // tiles.per_ex, nb)
        m = _rowsel(mod_ref[...], row, nb + 1)
        scale = m[:, d:2 * d]
        g = g_ref[...]
        dl = dl_refs[0][...]
        for ref in dl_refs[1:]:
            dl = dl + ref[...]
        du = jnp.where(tiles.is_pad(i), 0.0, d3_ref[...] + jnp.where(lat, dl, 0.0))
        rstd = lax.rsqrt(jnp.mean(xv * xv, axis=-1, keepdims=True) + EPS)
        xh = xv * rstd
        dshift = jnp.sum(du, axis=0, keepdims=True)
        dscale = jnp.sum(du * xh * g, axis=0, keepdims=True)
        dxn = du * (1.0 + scale)
        dg_ref[...] += jnp.sum(dxn * xh, axis=0, keepdims=True)
        dxh = dxn * g
        dx = rstd * (dxh - xh * jnp.mean(dxh * xh, axis=-1, keepdims=True))

        @pl.when(lat)
        def _():
            gxo_ref[...] = dx + gx_ref[...]

        for r in range(nb + 1):
            dmod_ref[r:r + 1, 0:d] += jnp.where(row == r, dshift, 0.0)
            dmod_ref[r:r + 1, d:2 * d] += jnp.where(row == r, dscale, 0.0)

    lat_map = lambda i: (tiles.lat_of_all(i), 0)
    lat_spec = pl.BlockSpec((tm, d), lat_map)
    return pl.pallas_call(
        body, name="norm_bwd", grid=(tiles.n_all,),
        in_specs=[lat_spec,
                  pl.BlockSpec((tm, d), lambda i: (tiles.ctx_of_all(i), 0)),
                  pl.BlockSpec(mod.shape, lambda i: (0, 0)),
                  pl.BlockSpec((1, d), lambda i: (0, 0))]
                 + [lat_spec] * n_lat_in
                 + [pl.BlockSpec((tm, d), lambda i: (i, 0)), lat_spec],
        out_specs=(lat_spec,
                   pl.BlockSpec((nrow, 3 * d), lambda i: (0, 0)),
                   pl.BlockSpec((1, d), lambda i: (0, 0))),
        out_shape=(jax.ShapeDtypeStruct((tl, d), F32), jax.ShapeDtypeStruct((nrow, 3 * d), F32),
                   jax.ShapeDtypeStruct((1, d), F32)),
        compiler_params=_params())(x2, ctx2, mod, norm_g, *du_lat, du_b, gx1)


def _matmul_bias(name, u, w, b, rows, tm, tn, u_tile=lambda i: i):
    d, n = w.shape

    def body(u_ref, w_ref, b_ref, o_ref):
        o_ref[...] = jnp.dot(u_ref[...], w_ref[...], preferred_element_type=F32) + b_ref[...]

    return pl.pallas_call(
        body, name=name, grid=(n // tn, rows // tm),
        in_specs=[pl.BlockSpec((tm, d), lambda j, i: (u_tile(i), 0)),
                  pl.BlockSpec((d, tn), lambda j, i: (0, j)),
                  pl.BlockSpec((1, tn), lambda j, i: (0, j))],
        out_specs=pl.BlockSpec((tm, tn), lambda j, i: (i, j)),
        out_shape=jax.ShapeDtypeStruct((rows, n), F32),
        compiler_params=_params())(u, w, b)


def _matmul_nt(name, a, w, koff, tm, tk):
    r, kc = a.shape
    d = w.shape[0]
    nk = kc // tk

    def body(a_ref, w_ref, o_ref):
        k = pl.program_id(1)
        p = lax.dot_general(a_ref[...], w_ref[...], (((1,), (1,)), ((), ())), preferred_element_type=F32)

        @pl.when(k == 0)
        def _():
            o_ref[...] = p

        @pl.when(k > 0)
        def _():
            o_ref[...] += p

    return pl.pallas_call(
        body, name=name, grid=(r // tm, nk),
        in_specs=[pl.BlockSpec((tm, tk), lambda i, k: (i, k)),
                  pl.BlockSpec((d, tk), lambda i, k: (0, koff + k))],
        out_specs=pl.BlockSpec((tm, d), lambda i, k: (i, 0)),
        out_shape=jax.ShapeDtypeStruct((r, d), F32),
        compiler_params=_params())(a, w)


def _matmul_tn(name, a, b, rows, tk, tn):
    m = a.shape[1]
    n = b.shape[1]

    def body(a_ref, b_ref, o_ref, s_ref):
        k = pl.program_id(1)
        bv = b_ref[...]
        p = lax.dot_general(a_ref[...], bv, (((0,), (0,)), ((), ())), preferred_element_type=F32)
        cs = jnp.sum(bv.astype(F32), axis=0, keepdims=True)

        @pl.when(k == 0)
        def _():
            o_ref[...] = p
            s_ref[...] = cs

        @pl.when(k > 0)
        def _():
            o_ref[...] += p
            s_ref[...] += cs

    return pl.pallas_call(
        body, name=name, grid=(n // tn, rows // tk),
        in_specs=[pl.BlockSpec((tk, m), lambda j, k: (k, 0)),
                  pl.BlockSpec((tk, tn), lambda j, k: (k, j))],
        out_specs=(pl.BlockSpec((m, tn), lambda j, k: (0, j)), pl.BlockSpec((1, tn), lambda j, k: (0, j))),
        out_shape=(jax.ShapeDtypeStruct((m, n), F32), jax.ShapeDtypeStruct((1, n), F32)),
        compiler_params=_params())(a, b)


def _matmul_tn_whole(name, a3, b3, rows, tn):
    nb, _, m = a3.shape
    n = b3.shape[2]

    def body(a_ref, b_ref, o_ref, s_ref):
        p, cs = None, None
        for e in range(nb):
            bv = b_ref[e]
            pe = lax.dot_general(a_ref[e], bv, (((0,), (0,)), ((), ())), preferred_element_type=F32)
            ce = jnp.sum(bv.astype(F32), axis=0, keepdims=True)
            p, cs = (pe, ce) if p is None else (p + pe, cs + ce)
        o_ref[...] = p
        s_ref[...] = cs

    return pl.pallas_call(
        body, name=name, grid=(n // tn,),
        in_specs=[pl.BlockSpec((nb, rows, m), lambda j: (0, 0, 0)),
                  pl.BlockSpec((nb, rows, tn), lambda j: (0, 0, j))],
        out_specs=(pl.BlockSpec((m, tn), lambda j: (0, j)), pl.BlockSpec((1, tn), lambda j: (0, j))),
        out_shape=(jax.ShapeDtypeStruct((m, n), F32), jax.ShapeDtypeStruct((1, n), F32)),
        compiler_params=_params())(a3, b3)


def _conv_window(pad_ref, r, shift, ktaps, width, horizontal):
    if horizontal:
        return pad_ref[r, pl.ds(16 + shift, width), :]
    return pad_ref[r + ktaps // 2 + shift]


def _conv_row(pad_ref, w, r, ktaps, width, horizontal, flip):
    half = ktaps // 2
    acc = None
    for t in range(ktaps):
        win = _conv_window(pad_ref, r, (half - t) if flip else (t - half), ktaps, width, horizontal)
        term = win * w[t:t + 1, :]
        acc = term if acc is None else acc + term
    return acc


def _fill_padded(ref, val, rows, width, ktaps, horizontal):
    half_k = ktaps // 2
    cb = val.shape[-1]
    if horizontal:
        ref[:, 0:16, :] = jnp.zeros((rows, 16, cb), F32)
        ref[:, 16 + width:32 + width, :] = jnp.zeros((rows, 16, cb), F32)
        ref[:, 16:16 + width, :] = val
    else:
        ref[0:half_k, :, :] = jnp.zeros((half_k, width, cb), F32)
        ref[half_k + rows:2 * half_k + rows, :, :] = jnp.zeros((half_k, width, cb), F32)
        ref[half_k:half_k + rows, :, :] = val


def _conv_fwd(pa, conv_w, conv_b, nb, s, cb):
    ktaps, d = conv_w.shape
    rows, width = s // GRID_W, GRID_W
    half_k = ktaps // 2
    nblk = d // cb
    nh = nblk // 2

    def body(glu_ref, w_ref, b_ref, o_ref, ph_ref, pv_ref):
        j = pl.program_id(1)
        a0 = (glu_ref[:, 0:cb] * _sigmoid(glu_ref[:, cb:2 * cb])).reshape(rows, width, cb)
        w = w_ref[...]

        bias = b_ref[...]

        def run(pad_ref, horizontal):
            _fill_padded(pad_ref, a0, rows, width, ktaps, horizontal)

            def row(r, carry):
                at = pl.ds(pl.multiple_of(r * width, width), width)
                o_ref[at, :] = _conv_row(pad_ref, w, r, ktaps, width, horizontal, False) + bias
                return carry

            lax.fori_loop(0, rows, row, 0)

        @pl.when(j < nh)
        def _():
            run(ph_ref, True)

        @pl.when(j >= nh)
        def _():
            run(pv_ref, False)

    return pl.pallas_call(
        body, name="conv_fwd", grid=(nb, nblk),
        in_specs=[pl.BlockSpec((s, 2 * cb), lambda b, j: (b, j)),
                  pl.BlockSpec((ktaps, cb), lambda b, j: (0, j)),
                  pl.BlockSpec((1, cb), lambda b, j: (0, j))],
        out_specs=pl.BlockSpec((s, cb), lambda b, j: (b, j)),
        out_shape=jax.ShapeDtypeStruct((nb * s, d), F32),
        scratch_shapes=[pltpu.VMEM((rows, width + 32, cb), F32), pltpu.VMEM((rows + 2 * half_k, width, cb), F32)],
        compiler_params=_params())(pa, conv_w, conv_b)


def _conv_bwd(pa, da1, conv_w, nb, s, cb):
    ktaps, d = conv_w.shape
    rows, width = s // GRID_W, GRID_W
    half_k = ktaps // 2
    nblk = d // cb
    nh = nblk // 2

    def body(glu_ref, da_ref, w_ref, dp_ref, dw_ref, db_ref, pha_ref, phd_ref, pva_ref, pvd_ref):
        j = pl.program_id(0)
        b = pl.program_id(1)
        a0 = (glu_ref[:, 0:cb] * _sigmoid(glu_ref[:, cb:2 * cb])).reshape(rows, width, cb)
        da1v = da_ref[...]
        d3 = da1v.reshape(rows, width, cb)
        w = w_ref[...]

        @pl.when(b == 0)
        def _():
            dw_ref[...] = jnp.zeros_like(dw_ref)
            db_ref[...] = jnp.zeros_like(db_ref)

        db_ref[...] += jnp.sum(da1v, axis=0, keepdims=True)

        def run(pa_ref, pd_ref, horizontal):
            _fill_padded(pa_ref, a0, rows, width, ktaps, horizontal)
            _fill_padded(pd_ref, d3, rows, width, ktaps, horizontal)

            def row(r, accs):
                at = pl.ds(pl.multiple_of(r * width, width), width)
                da0 = _conv_row(pd_ref, w, r, ktaps, width, horizontal, True)
                gv = glu_ref[at, 0:cb]
                sg = _sigmoid(glu_ref[at, cb:2 * cb])
                dp_ref[at, 0:cb] = (da0 * sg).astype(BF16)
                dp_ref[at, cb:2 * cb] = (da0 * gv * sg * (1.0 - sg)).astype(BF16)
                d_row = da_ref[at, :]
                out = []
                for t in range(ktaps):
                    prod = _conv_window(pa_ref, r, t - half_k, ktaps, width, horizontal) * d_row
                    out.append(accs[t] + jnp.sum(prod.reshape(width // 8, 8, cb), axis=0))
                return tuple(out)

            accs = lax.fori_loop(0, rows, row, tuple(jnp.zeros((8, cb), F32) for _ in range(ktaps)))
            for t in range(ktaps):
                dw_ref[t:t + 1, :] += jnp.sum(accs[t], axis=0, keepdims=True)

        @pl.when(j < nh)
        def _():
            run(pha_ref, phd_ref, True)

        @pl.when(j >= nh)
        def _():
            run(pva_ref, pvd_ref, False)

    return pl.pallas_call(
        body, name="conv_bwd", grid=(nblk, nb),
        in_specs=[pl.BlockSpec((s, 2 * cb), lambda j, b: (b, j)),
                  pl.BlockSpec((s, cb), lambda j, b: (b, j)),
                  pl.BlockSpec((ktaps, cb), lambda j, b: (0, j))],
        out_specs=(pl.BlockSpec((s, 2 * cb), lambda j, b: (b, j)),
                   pl.BlockSpec((ktaps, cb), lambda j, b: (0, j)),
                   pl.BlockSpec((1, cb), lambda j, b: (0, j))),
        out_shape=(jax.ShapeDtypeStruct((nb * s, 2 * d), BF16),
                   jax.ShapeDtypeStruct((ktaps, d), F32), jax.ShapeDtypeStruct((1, d), F32)),
        scratch_shapes=[pltpu.VMEM((rows, width + 32, cb), F32), pltpu.VMEM((rows, width + 32, cb), F32),
                        pltpu.VMEM((rows + 2 * half_k, width, cb), F32),
                        pltpu.VMEM((rows + 2 * half_k, width, cb), F32)],
        compiler_params=_params())(pa, da1, conv_w)


def _log_sigmoid(x):
    return jnp.minimum(x, 0.0) - jnp.log(1.0 + jnp.exp(-jnp.abs(x)))


def _decay_fwd(pb, up2, bias2, tm, lr_blk):
    t_all = pb.shape[0]
    n2 = up2.shape[1]

    def body(lr_ref, up_ref, b_ref, g_ref):
        logits = _mm(lr_ref[...], up_ref[...]) + b_ref[...]
        g_ref[...] = _log_sigmoid(logits) * (1.0 / GATE_TAU)

    return pl.pallas_call(
        body, name="decay_fwd", grid=(t_all // tm,),
        in_specs=[pl.BlockSpec((tm, LANE), lambda i: (i, lr_blk)),
                  pl.BlockSpec(up2.shape, lambda i: (0, 0)),
                  pl.BlockSpec((1, n2), lambda i: (0, 0))],
        out_specs=pl.BlockSpec((tm, n2), lambda i: (i, 0)),
        out_shape=jax.ShapeDtypeStruct((t_all, n2), F32),
        compiler_params=_params())(pb, up2, bias2)


def _decay_bwd(pb, up2, bias2, grads_f, grads_b, tiles, lr_blk, dk_, dv_):
    t_all = pb.shape[0]
    tm = tiles.tm
    n2 = up2.shape[1]
    nbw = 2 * dk_ + dv_ + LANE

    def body(lr_ref, up_ref, b_ref, dqf, dkf, dvf, dgf, dqb, dkb, dvb, dgb, dp_ref, dup_ref, dbias_ref):
        i = pl.program_id(0)
        pad = tiles.is_pad(i)
        live = lambda v: jnp.where(pad, 0.0, v)

        @pl.when(i == 0)
        def _():
            dup_ref[...] = jnp.zeros_like(dup_ref)
            dbias_ref[...] = jnp.zeros_like(dbias_ref)

        lr = lr_ref[...]
        up = up_ref[...]
        logits = _mm(lr, up) + b_ref[...]
        dg = live(jnp.concatenate([dgf[...], dgb[...]], axis=1))
        dlog = dg * (1.0 / GATE_TAU) * _sigmoid(-logits)
        dup_ref[...] += _mm_tn(lr, dlog)
        dbias_ref[...] += jnp.sum(dlog, axis=0, keepdims=True)
        dp_ref[:, 0:dk_] = live(dqf[...] + dqb[...]).astype(BF16)
        dp_ref[:, dk_:2 * dk_] = live(dkf[...] + dkb[...]).astype(BF16)
        dp_ref[:, 2 * dk_:2 * dk_ + dv_] = live(dvf[...] + dvb[...]).astype(BF16)
        dp_ref[:, 2 * dk_ + dv_:nbw] = _mm_nt(dlog, up).astype(BF16)

    row = lambda w: pl.BlockSpec((tm, w), lambda i: (i, 0))
    return pl.pallas_call(
        body, name="decay_bwd", grid=(t_all // tm,),
        in_specs=[pl.BlockSpec((tm, LANE), lambda i: (i, lr_blk)),
                  pl.BlockSpec(up2.shape, lambda i: (0, 0)),
                  pl.BlockSpec((1, n2), lambda i: (0, 0)),
                  row(dk_), row(dk_), row(dv_), row(dk_), row(dk_), row(dk_), row(dv_), row(dk_)],
        out_specs=(row(nbw), pl.BlockSpec(up2.shape, lambda i: (0, 0)), pl.BlockSpec((1, n2), lambda i: (0, 0))),
        out_shape=(jax.ShapeDtypeStruct((t_all, nbw), BF16), jax.ShapeDtypeStruct(up2.shape, F32),
                   jax.ShapeDtypeStruct((1, n2), F32)),
        compiler_params=_params())(pb, up2, bias2, *grads_f, *grads_b)


def _scan_chunk(s, nl, nc, rev):
    if rev:
        return jnp.where(s < nc, nl + (nc - 1 - s), nl - 1 - (s - nc))
    return jnp.where(s < nc, nl + s, s - nc)


def _scan_lat_chunk(s, nl, nc, rev):
    first = nl - 1 if rev else 0
    return jnp.where(s < nc, first, _scan_chunk(s, nl, nc, rev))


def _tri_mm(m_bf, x):
    hi = x.astype(BF16)
    r1 = x - hi.astype(F32)
    mid = r1.astype(BF16)
    lo = (r1 - mid.astype(F32)).astype(BF16)
    dot = lambda p: jnp.dot(m_bf, p, preferred_element_type=F32)
    return dot(hi) + dot(mid) + dot(lo)


def _chunk_masks(c, rev):
    ii = lax.broadcasted_iota(jnp.int32, (c, c), 0)
    jj = lax.broadcasted_iota(jnp.int32, (c, c), 1)
    return ((ii <= jj), (ii >= jj)) if rev else ((ii >= jj), (ii <= jj))


def _chunk_terms(q, k, b, far, mid):
    bf, bm = b[far:far + 1, :], b[mid:mid + 1, :]
    e = jnp.exp(b)
    em = jnp.exp(b - bm)
    eim = jnp.exp(bm - b)
    ed = jnp.exp(bf - b)
    return dict(e=e, em=em, eim=eim, ed=ed, dec=jnp.exp(bf), qe=q * e, qem=q * em, kim=k * eim, kd=k * ed)


def _gla_fwd(pb3, g3, nb, s_len, c_len, dk_, dv_):
    c = CHUNK
    nl, nc = s_len // c, c_len // c
    ns = nl + nc
    hk, hv = dk_ // HEADS, dv_ // HEADS
    l_len = pb3.shape[1]
    scale = hk ** -0.5
    mid = c // 2

    def body(*refs):
        ins, outs, z_scr = refs[:8], refs[8:14], refs[14]
        s = pl.program_id(0)

        @pl.when(s == 0)
        def _():
            z_scr[...] = jnp.zeros_like(z_scr)

        qs = jnp.where(s >= nc, scale, 0.0)
        for di, rev in enumerate((False, True)):
            q_ref, k_ref, v_ref, g_ref = ins[4 * di:4 * di + 4]
            o_ref, zs_ref, b_ref = outs[3 * di:3 * di + 3]
            mask, _ = _chunk_masks(c, rev)
            m_bf = mask.astype(BF16)
            far = 0 if rev else c - 1
            for b in range(nb):
                bc = _tri_mm(m_bf, g_ref[b])
                b_ref[b] = bc
                for h in range(HEADS):
                    ks, vs = slice(h * hk, (h + 1) * hk), slice(h * hv, (h + 1) * hv)
                    zi = (di * nb + b) * HEADS + h
                    v = v_ref[b, :, vs]
                    t = _chunk_terms(q_ref[b, :, ks] * qs, k_ref[b, :, ks], bc[:, ks], far, mid)
                    a = jnp.where(mask, _mm_nt(t["qem"], t["kim"]), 0.0)
                    z = z_scr[zi]
                    zs_ref[0, b * HEADS + h] = z
                    o_ref[b, :, vs] = _mm(a, v) + _mm_nt(t["qe"], z)
                    z_scr[zi] = z * t["dec"] + _mm_tn(v, t["kd"])

    in_specs, out_specs, out_shape = [], [], []
    for di, rev in enumerate((False, True)):
        ch = functools.partial(_scan_chunk, nl=nl, nc=nc, rev=rev)
        lch = functools.partial(_scan_lat_chunk, nl=nl, nc=nc, rev=rev)
        in_specs += [pl.BlockSpec((nb, c, dk_), lambda s, ch=ch: (0, ch(s), 0)),
                     pl.BlockSpec((nb, c, dk_), lambda s, ch=ch: (0, ch(s), 1)),
                     pl.BlockSpec((nb, c, dv_), lambda s, ch=ch: (0, ch(s), 1)),
                     pl.BlockSpec((nb, c, dk_), lambda s, ch=ch, di=di: (0, ch(s), di))]
        out_specs += [pl.BlockSpec((nb, c, dv_), lambda s, lch=lch: (0, lch(s), 0)),
                      pl.BlockSpec((1, nb * HEADS, hv, hk), lambda s: (s, 0, 0, 0)),
                      pl.BlockSpec((nb, c, dk_), lambda s, ch=ch: (0, ch(s), 0))]
        out_shape += [jax.ShapeDtypeStruct((nb, s_len, dv_), F32),
                      jax.ShapeDtypeStruct((ns, nb * HEADS, hv, hk), F32),
                      jax.ShapeDtypeStruct((nb, l_len, dk_), F32)]
    return pl.pallas_call(
        body, name="gla_fwd", grid=(ns,), in_specs=in_specs, out_specs=tuple(out_specs), out_shape=tuple(out_shape),
        scratch_shapes=[pltpu.VMEM((2 * nb * HEADS, hv, hk), F32)],
        compiler_params=_params())(pb3, pb3, pb3, g3, pb3, pb3, pb3, g3)


def _gla_bwd(pb3, do3, fwd_saved, nb, s_len, c_len, dk_, dv_):
    c = CHUNK
    nl, nc = s_len // c, c_len // c
    ns = nl + nc
    hk, hv = dk_ // HEADS, dv_ // HEADS
    l_len = pb3.shape[1]
    scale = hk ** -0.5
    mid = c // 2
    zs_f, b_f, zs_b, b_b = fwd_saved

    def body(*refs):
        ins, outs, dz_scr = refs[:12], refs[12:20], refs[20]
        s = pl.program_id(0)
        step = ns - 1 - s

        @pl.when(s == 0)
        def _():
            dz_scr[...] = jnp.zeros_like(dz_scr)

        lat = step >= nc
        qs = jnp.where(lat, scale, 0.0)
        dmul = jnp.where(lat, 1.0, 0.0)
        for di, rev in enumerate((False, True)):
            q_ref, k_ref, v_ref, b_ref, do_ref, zs_ref = ins[6 * di:6 * di + 6]
            dq_ref, dk_ref, dv_ref, dg_ref = outs[4 * di:4 * di + 4]
            mask, mask_t = _chunk_masks(c, rev)
            mt_bf = mask_t.astype(BF16)
            far = 0 if rev else c - 1
            far_row = lax.broadcasted_iota(jnp.int32, (c, hk), 0) == far
            for b in range(nb):
                db_parts = []
                for h in range(HEADS):
                    ks, vs = slice(h * hk, (h + 1) * hk), slice(h * hv, (h + 1) * hv)
                    zi = (di * nb + b) * HEADS + h
                    v = v_ref[b, :, vs]
                    d_o = do_ref[b, :, vs] * dmul
                    t = _chunk_terms(q_ref[b, :, ks] * qs, k_ref[b, :, ks], b_ref[b, :, ks], far, mid)
                    qem, kim, qe, kd = t["qem"], t["kim"], t["qe"], t["kd"]
                    a_t = jnp.where(mask_t, _mm_nt(kim, qem), 0.0)
                    d_a = jnp.where(mask, _mm_nt(d_o, v), 0.0)
                    d_at = jnp.where(mask_t, _mm_nt(v, d_o), 0.0)
                    z = zs_ref[0, b * HEADS + h]
                    dzn = dz_scr[zi]
                    dv_ref[b, :, vs] = _mm(a_t, d_o) + _mm_nt(kd, dzn)
                    dqem = _mm(d_a, kim)
                    dkim = _mm(d_at, qem)
                    dqe = _mm(d_o, z)
                    dkd = _mm(v, dzn)
                    ddec = jnp.sum(z * dzn, axis=0, keepdims=True)
                    dz_scr[zi] = dzn * t["dec"] + _mm_tn(d_o, qe)
                    dq_ref[b, :, ks] = (dqem * t["em"] + dqe * t["e"]) * qs
                    dk_ref[b, :, ks] = dkim * t["eim"] + dkd * t["ed"]
                    db = dqem * qem - dkim * kim + dqe * qe - dkd * kd
                    extra = jnp.sum(dkd * kd, axis=0, keepdims=True) + ddec * t["dec"]
                    db_parts.append(db + jnp.where(far_row, extra, 0.0))
                dg_ref[b] = _tri_mm(mt_bf, jnp.concatenate(db_parts, axis=1))

    in_specs, out_specs, out_shape, args = [], [], [], []
    for di, rev in enumerate((False, True)):
        ch = lambda s, rev=rev: _scan_chunk(ns - 1 - s, nl, nc, rev)
        lch = lambda s, rev=rev: _scan_lat_chunk(ns - 1 - s, nl, nc, rev)
        in_specs += [pl.BlockSpec((nb, c, dk_), lambda s, ch=ch: (0, ch(s), 0)),
                     pl.BlockSpec((nb, c, dk_), lambda s, ch=ch: (0, ch(s), 1)),
                     pl.BlockSpec((nb, c, dv_), lambda s, ch=ch: (0, ch(s), 1)),
                     pl.BlockSpec((nb, c, dk_), lambda s, ch=ch: (0, ch(s), 0)),
                     pl.BlockSpec((nb, c, dv_), lambda s, lch=lch: (0, lch(s), 0)),
                     pl.BlockSpec((1, nb * HEADS, hv, hk), lambda s: (ns - 1 - s, 0, 0, 0))]
        args += [pb3, pb3, pb3, (b_b if rev else b_f), do3, (zs_b if rev else zs_f)]
        for w in (dk_, dk_, dv_, dk_):
            out_specs.append(pl.BlockSpec((nb, c, w), lambda s, ch=ch: (0, ch(s), 0)))
            out_shape.append(jax.ShapeDtypeStruct((nb, l_len, w), F32))
    return pl.pallas_call(
        body, name="gla_bwd", grid=(ns,), in_specs=in_specs, out_specs=tuple(out_specs), out_shape=tuple(out_shape),
        scratch_shapes=[pltpu.VMEM((2 * nb * HEADS, hv, hk), F32)],
        compiler_params=_params())(*args)


def _tail(a1, pa, o_f, o_b, x2, tgt, mod, wc, wg, wo, ln_g, ln_b, gn_t, fg, nb, tm):
    tl, d = x2.shape
    nt = tl // tm
    per_ex = nt // nb
    hv = d // HEADS
    nrow = mod.shape[0]

    def body(a1_ref, z_ref, r_ref, mc_ref, mg_ref, of_ref, ob_ref, x_ref, t_ref, mod_ref, wc_ref, wg_ref, wo_ref,
             lng_ref, lnb_ref, gn_ref, fg_ref,
             dp_ref, da1_ref, do_ref, gx_ref, mrg_ref, dmo_ref, yci_ref, dyc_ref, ogi_ref, dyg_ref, sm_ref):
        i = pl.program_id(0)

        @pl.when(i == 0)
        def _():
            sm_ref[...] = jnp.zeros_like(sm_ref)

        bidx = i // per_ex
        gate = _rowsel(mod_ref[...], bidx, nb)[:, 2 * d:3 * d]
        lng, lnb, gn, fgv = lng_ref[...], lnb_ref[...], gn_ref[...], fg_ref[...]
        wc_, wg_, wo_ = wc_ref[...], wg_ref[...], wo_ref[...]

        a1v = a1_ref[...]
        mu = jnp.mean(a1v, axis=-1, keepdims=True)
        xc = a1v - mu
        rs = lax.rsqrt(jnp.mean(xc * xc, axis=-1, keepdims=True) + EPS)
        xh = xc * rs
        a2 = xh * lng + lnb
        s2 = _sigmoid(a2)
        a3 = a2 * s2
        zv = z_ref[...]
        sz = _sigmoid(zv)
        siluz = zv * sz
        ycin = a3 * siluz
        yconv = _mm(ycin, wc_)

        o = of_ref[...] + ob_ref[...]
        ohat_parts, rn_parts = [], []
        for h in range(HEADS):
            oh = o[:, h * hv:(h + 1) * hv]
            rn = lax.rsqrt(jnp.mean(oh * oh, axis=-1, keepdims=True) + EPS)
            ohat_parts.append(oh * rn)
            rn_parts.append(rn)
        ohat = jnp.concatenate(ohat_parts, axis=1)
        on = ohat * gn
        rv = r_ref[...]
        sr = _sigmoid(rv)
        silur = rv * sr
        ogin = on * silur
        ygla = _mm(ogin, wg_)

        sc = _sigmoid(mc_ref[...])
        sg = _sigmoid(mg_ref[...])
        merged = sc * yconv + sg * ygla
        mo = _mm(merged, wo_)
        hn = x_ref[...] + gate * mo
        rf = lax.rsqrt(jnp.mean(hn * hn, axis=-1, keepdims=True) + EPS)
        yh = hn * rf
        err = yh * fgv - t_ref[...]
        loss_part = 0.5 * jnp.sum(err * err) * (1.0 / d)

        dy = err * (1.0 / d)
        dfg = jnp.sum(dy * yh, axis=0, keepdims=True)
        dyh = dy * fgv
        dhn = rf * (dyh - yh * jnp.mean(dyh * yh, axis=-1, keepdims=True))
        gx_ref[...] = dhn
        dgate = jnp.sum(dhn * mo, axis=0, keepdims=True)
        dmo = gate * dhn
        dmerged = _mm_nt(dmo, wo_)
        dyconv = dmerged * sc
        dygla = dmerged * sg
        dp_ref[:, 2 * d:3 * d] = (dmerged * yconv * sc * (1.0 - sc)).astype(BF16)
        dp_ref[:, 3 * d:4 * d] = (dmerged * ygla * sg * (1.0 - sg)).astype(BF16)
        dycin = _mm_nt(dyconv, wc_)
        dogin = _mm_nt(dygla, wg_)
        mrg_ref[...] = merged.astype(BF16)
        dmo_ref[...] = dmo.astype(BF16)
        yci_ref[...] = ycin.astype(BF16)
        dyc_ref[...] = dyconv.astype(BF16)
        ogi_ref[...] = ogin.astype(BF16)
        dyg_ref[...] = dygla.astype(BF16)

        da3 = dycin * siluz
        dp_ref[:, 0:d] = (dycin * a3 * _dsilu(zv, sz)).astype(BF16)
        da2 = da3 * _dsilu(a2, s2)
        dlng = jnp.sum(da2 * xh, axis=0, keepdims=True)
        dlnb = jnp.sum(da2, axis=0, keepdims=True)
        dxh = da2 * lng
        da1_ref[...] = rs * (dxh - jnp.mean(dxh, axis=-1, keepdims=True)
                             - xh * jnp.mean(dxh * xh, axis=-1, keepdims=True))

        don = dogin * silur
        dp_ref[:, d:2 * d] = (dogin * on * _dsilu(rv, sr)).astype(BF16)
        dgn = jnp.sum(don * ohat, axis=0, keepdims=True)
        dyn = don * gn
        for h in range(HEADS):
            vs = slice(h * hv, (h + 1) * hv)
            oh_hat = ohat_parts[h]
            dh = dyn[:, vs]
            do_ref[:, vs] = rn_parts[h] * (dh - oh_hat * jnp.mean(dh * oh_hat, axis=-1, keepdims=True))

        sm_ref[0:1, :] += dfg
        sm_ref[1:2, :] += dlng
        sm_ref[2:3, :] += dlnb
        sm_ref[3:4, :] += dgn
        sm_ref[4:5, :] += jnp.zeros((1, d), F32) + loss_part
        for b in range(nb):
            sm_ref[8 + b:9 + b, :] += jnp.where(bidx == b, dgate, 0.0)

    row = pl.BlockSpec((tm, d), lambda i: (i, 0))
    pcol = lambda blk: pl.BlockSpec((tm, d), lambda i: (i, blk))
    full = lambda arr: pl.BlockSpec(arr.shape, lambda i: (0,) * arr.ndim)
    bfo = jax.ShapeDtypeStruct((tl, d), BF16)
    f32o = jax.ShapeDtypeStruct((tl, d), F32)
    return pl.pallas_call(
        body, name="tail", grid=(nt,),
        in_specs=[row, pcol(2), pcol(3), pcol(4), pcol(5), row, row, row, row, full(mod), full(wc), full(wg),
                  full(wo), full(ln_g), full(ln_b), full(gn_t), full(fg)],
        out_specs=(pl.BlockSpec((tm, 4 * d), lambda i: (i, 0)), row, row, row, row, row, row, row, row, row,
                   pl.BlockSpec((16, d), lambda i: (0, 0))),
        out_shape=(jax.ShapeDtypeStruct((tl, 4 * d), BF16), f32o, f32o, f32o, bfo, bfo, bfo, bfo, bfo, bfo,
                   jax.ShapeDtypeStruct((16, d), F32)),
        compiler_params=_params())(a1, pa, pa, pa, pa, o_f, o_b, x2, tgt, mod, wc, wg, wo, ln_g, ln_b, gn_t, fg)


def _local_step(x, c, ctx, tgt, c_ctx, ada_w, ada_b, norm_g, w_a, b_a, w_b, b_b, conv_w, conv_b, ln_g, ln_b,
                conv_proj, up2, bias2, gla_norm_g, gla_proj, w_out, final_norm_g):
    nb, s_len, d = x.shape
    c_len = ctx.shape[1]
    dk_, dv_ = d // 2, d
    tl, tc = nb * s_len, nb * c_len
    nbw = 2 * dk_ + dv_ + LANE
    tm = math.gcd(256, c_len)
    tiles = _Tiles(nb, s_len, c_len, tm, 2)
    tmm = tiles.big * tm
    l_len = tiles.rows_per_ex
    t_all = nb * l_len
    x2, ctx2, tgt2 = x.reshape(tl, d), ctx.reshape(tc, d), tgt.reshape(tl, d)

    cv = jnp.zeros((8, d), F32).at[0:nb].set(c).at[nb].set(c_ctx.reshape(d))
    mod = _ada_fwd(cv, ada_w, ada_b)
    u = _norm_fwd(x2, ctx2, mod, norm_g, tiles)
    pa = _matmul_bias("inproj_a", u, w_a, b_a, tl, tmm, _tile(6 * d, 1536), u_tile=tiles.big_all_of_lat)
    pb = _matmul_bias("inproj_b", u, w_b, b_b, t_all, tmm, nbw)

    cb = min(LANE, d // 2)
    a1 = _conv_fwd(pa, conv_w, conv_b, nb, s_len, cb)
    lr_blk = (2 * dk_ + dv_) // LANE
    g_all = _decay_fwd(pb, up2, bias2, tm, lr_blk)
    pb3 = pb.reshape(nb, l_len, nbw)
    o_f, zs_f, b_f, o_b, zs_b, b_b2 = _gla_fwd(pb3, g_all.reshape(nb, l_len, 2 * dk_), nb, s_len, c_len, dk_, dv_)

    gn_t = jnp.tile(gla_norm_g, (1, HEADS))
    tt = math.gcd(128, s_len)
    (dp_a2, da1, d_o, gx1, merged, dmo, ycin, dyconv, ogin, dygla, small) = _tail(
        a1, pa, o_f.reshape(tl, dv_), o_b.reshape(tl, dv_), x2, tgt2, mod, conv_proj, gla_proj, w_out, ln_g, ln_b,
        gn_t, final_norm_g, nb, tt)

    lat3 = lambda a: a.reshape(nb, s_len, a.shape[-1])
    tnw = _tile(d, 1024)
    d_w_out, _ = _matmul_tn_whole("dw_out", lat3(merged), lat3(dmo), s_len, tnw)
    d_conv_proj, _ = _matmul_tn_whole("dw_conv_proj", lat3(ycin), lat3(dyconv), s_len, tnw)
    d_gla_proj, _ = _matmul_tn_whole("dw_gla_proj", lat3(ogin), lat3(dygla), s_len, tnw)

    dp_a1, d_conv_w, d_conv_b = _conv_bwd(pa, da1, conv_w, nb, s_len, cb)
    gl = _gla_bwd(pb3, d_o.reshape(nb, s_len, dv_), (zs_f, b_f, zs_b, b_b2), nb, s_len, c_len, dk_, dv_)
    gl = [g_.reshape(t_all, g_.shape[-1]) for g_ in gl]
    dp_b, d_up2, d_bias2 = _decay_bwd(pb, up2, bias2, gl[0:4], gl[4:8], tiles, lr_blk, dk_, dv_)

    tka = _tile(2 * d, 2048)
    du_a1 = _matmul_nt("du_a1", dp_a1, w_a, 0, tmm, tka)
    du_a2 = _matmul_nt("du_a2", dp_a2, w_a, (2 * d) // tka, tmm, tka)
    du_b = _matmul_nt("du_b", dp_b, w_b, 0, tmm, nbw)
    u3 = u.reshape(nb, l_len, d)
    dw_a1, db_a1 = _matmul_tn_whole("dw_a1", u3, lat3(dp_a1), s_len, tnw)
    dw_a2, db_a2 = _matmul_tn_whole("dw_a2", u3, lat3(dp_a2), s_len, tnw)
    dw_b, db_b = _matmul_tn("dw_b", u, dp_b, t_all, tmm, nbw)

    grad_x2, dmod_ss, d_norm_g = _norm_bwd(x2, ctx2, mod, norm_g, [du_a1, du_a2], du_b, gx1, tiles)
    dmod = dmod_ss.at[0:nb, 2 * d:3 * d].set(small[8:8 + nb])
    d_ada_w, d_ada_b, d_cv = _ada_bwd(cv, ada_w, dmod)

    return dict(
        loss=small[4, 0], grad_x=grad_x2.reshape(nb, s_len, d), c_ctx=d_cv[nb], ada_w=d_ada_w, ada_b=d_ada_b,
        norm_g=d_norm_g, w_a=jnp.concatenate([dw_a1, dw_a2], axis=1), b_a=jnp.concatenate([db_a1, db_a2], axis=1),
        w_b=dw_b, b_b=db_b, conv_w=d_conv_w, conv_b=d_conv_b, conv_ln_g=small[1:2], conv_ln_b=small[2:3],
        conv_proj=d_conv_proj, up2=d_up2, bias2=d_bias2,
        gla_norm_g=small[3:4].reshape(HEADS, d // HEADS).sum(axis=0, keepdims=True),
        gla_proj=d_gla_proj, w_out=d_w_out, final_norm_g=small[0:1])


def _regroup(o, d, r):
    cb = min(LANE, d // 2)
    glu = []
    for j in range(d // cb):
        glu += [o[..., j * cb:(j + 1) * cb], o[..., d + j * cb:d + (j + 1) * cb]]
    a = jnp.concatenate(glu + [o[..., 2 * d:3 * d], o[..., 5 * d + 2 * r:8 * d + 2 * r]], axis=-1)
    pad = jnp.zeros(o.shape[:-1] + (LANE - 2 * r,), o.dtype)
    b = jnp.concatenate([o[..., 3 * d:5 * d + 2 * r], pad], axis=-1)
    return a, b


def _ungroup(a, b, d, r):
    cb = min(LANE, d // 2)
    n = d // cb
    gv = [a[..., 2 * j * cb:(2 * j + 1) * cb] for j in range(n)]
    gg = [a[..., (2 * j + 1) * cb:(2 * j + 2) * cb] for j in range(n)]
    return jnp.concatenate(gv + gg + [a[..., 2 * d:3 * d], b[..., 0:2 * d + 2 * r], a[..., 3 * d:6 * d]], axis=-1)


def _mesh_pos():
    return lax.axis_index("x"), lax.axis_index("y"), lax.axis_index("c")


_ANY = pl.BlockSpec(memory_space=pl.ANY)


def _all_gather(arrs):
    n = len(arrs)

    def body(*refs):
        ins, outs = refs[:n], refs[n:2 * n]
        send_sems, recv_sems, local_sems = refs[2 * n:]
        x, y, c = _mesh_pos()
        me, sibling = (x, y, c), (x, y, 1 - c)
        chips = [(1 - x, y), (x, 1 - y), (1 - x, 1 - y)]

        def slot(a, pos):
            return outs[a].at[4 * pos[0] + 2 * pos[1] + pos[2]]

        def copy(a, k, block, to, src=None):
            return pltpu.make_async_remote_copy(
                src_ref=slot(a, block) if src is None else src, dst_ref=slot(a, block),
                send_sem=send_sems.at[7 * a + k], recv_sem=recv_sems.at[7 * a + k],
                device_id=to, device_id_type=MESH)

        mine = [pltpu.make_async_copy(ins[a], slot(a, me), local_sems.at[a]) for a in range(n)]
        for cp in mine:
            cp.start()
        first = []
        for a in range(n):
            first.append(copy(a, 0, me, sibling, src=ins[a]))
            first += [copy(a, 1 + j, me, (*chip, c), src=ins[a]) for j, chip in enumerate(chips)]
        for cp in first:
            cp.start()
        passed = []
        for j, chip in enumerate(chips):
            for a in range(n):
                copy(a, 1 + j, (*chip, c), me).wait_recv()
                fwd = copy(a, 4 + j, (*chip, c), sibling)
                fwd.start()
                passed.append(fwd)
        for a in range(n):
            copy(a, 0, sibling, me).wait_recv()
            for j, chip in enumerate(chips):
                copy(a, 4 + j, (*chip, 1 - c), me).wait_recv()
        for cp in first + passed:
            cp.wait_send()
        for cp in mine:
            cp.wait()

    return pl.pallas_call(
        body, name="all_gather",
        out_shape=tuple(jax.ShapeDtypeStruct((N_DEV,) + a.shape, a.dtype) for a in arrs),
        in_specs=[_ANY] * n, out_specs=tuple([_ANY] * n),
        scratch_shapes=[pltpu.SemaphoreType.DMA((7 * n,)), pltpu.SemaphoreType.DMA((7 * n,)),
                        pltpu.SemaphoreType.DMA((n,))],
    )(*arrs)


def _exchange_sibling(arrs):
    n = len(arrs)

    def body(*refs):
        ins, outs = refs[:n], refs[n:2 * n]
        send_sems, recv_sems = refs[2 * n:]
        x, y, c = _mesh_pos()
        copies = [pltpu.make_async_remote_copy(
            src_ref=ins[a].at[2 * k + (1 - c)], dst_ref=outs[a].at[k],
            send_sem=send_sems.at[4 * a + k], recv_sem=recv_sems.at[4 * a + k],
            device_id=(x, y, 1 - c), device_id_type=MESH) for a in range(n) for k in range(4)]
        for cp in copies:
            cp.start()
        for cp in copies:
            cp.wait_recv()
        for cp in copies:
            cp.wait_send()

    return pl.pallas_call(
        body, name="grad_exchange_sibling",
        out_shape=tuple(jax.ShapeDtypeStruct((4,) + a.shape[1:], a.dtype) for a in arrs),
        in_specs=[_ANY] * n, out_specs=tuple([_ANY] * n),
        scratch_shapes=[pltpu.SemaphoreType.DMA((4 * n,)), pltpu.SemaphoreType.DMA((4 * n,))],
    )(*arrs)


def _pair_sum(name, mine, theirs):
    _, r, cdim = mine.shape
    tr = r if (r % 8 or r <= 256) else math.gcd(r, 256)

    def body(m_ref, t_ref, o_ref):
        c = lax.axis_index("c")
        own = jnp.where(c == 0, m_ref[:, 0].astype(F32), m_ref[:, 1].astype(F32))
        o_ref[...] = (own + t_ref[...].astype(F32)).astype(o_ref.dtype)

    return pl.pallas_call(
        body, name=name, grid=(r // tr,),
        in_specs=[pl.BlockSpec((4, 2, tr, cdim), lambda i: (0, 0, i, 0)),
                  pl.BlockSpec((4, tr, cdim), lambda i: (0, i, 0))],
        out_specs=pl.BlockSpec((4, tr, cdim), lambda i: (0, i, 0)),
        out_shape=jax.ShapeDtypeStruct((4, r, cdim), mine.dtype),
        compiler_params=_params())(mine.reshape(4, 2, r, cdim), theirs)


def _exchange_chips(arrs):
    n = len(arrs)

    def body(*refs):
        ins, outs = refs[:n], refs[n:2 * n]
        send_sems, recv_sems, local_sems = refs[2 * n:]
        x, y, c = _mesh_pos()
        my_chip = 2 * x + y
        mine = [pltpu.make_async_copy(ins[a].at[my_chip], outs[a].at[my_chip], local_sems.at[a]) for a in range(n)]
        for cp in mine:
            cp.start()
        copies = []
        for rel in range(1, 4):
            px = 1 - x if rel & 2 else x
            py = 1 - y if rel & 1 else y
            for a in range(n):
                copies.append(pltpu.make_async_remote_copy(
                    src_ref=ins[a].at[2 * px + py], dst_ref=outs[a].at[my_chip],
                    send_sem=send_sems.at[3 * a + rel - 1], recv_sem=recv_sems.at[3 * a + rel - 1],
                    device_id=(px, py, c), device_id_type=MESH))
        for cp in copies:
            cp.start()
        for cp in copies:
            cp.wait_recv()
        for cp in copies:
            cp.wait_send()
        for cp in mine:
            cp.wait()

    return pl.pallas_call(
        body, name="grad_exchange_chips",
        out_shape=tuple(jax.ShapeDtypeStruct(a.shape, a.dtype) for a in arrs),
        in_specs=[_ANY] * n, out_specs=tuple([_ANY] * n),
        scratch_shapes=[pltpu.SemaphoreType.DMA((3 * n,)), pltpu.SemaphoreType.DMA((3 * n,)),
                        pltpu.SemaphoreType.DMA((n,))],
    )(*arrs)


def _sum_adam(name, parts, w, m, v):
    r, cdim = w.shape
    n_parts = parts.shape[0]
    tr = r if (r % 8 or r <= 256) else math.gcd(r, 256)
    bc1 = 1.0 - ADAM_B1 ** ADAM_STEP
    bc2 = 1.0 - ADAM_B2 ** ADAM_STEP

    def body(p_ref, w_ref, m_ref, v_ref, g_ref, d_ref, nm_ref, nv_ref):
        g = p_ref[0].astype(F32)
        for k in range(1, n_parts):
            g = g + p_ref[k].astype(F32)
        mn = ADAM_B1 * m_ref[...] + (1.0 - ADAM_B1) * g
        vn = ADAM_B2 * v_ref[...] + (1.0 - ADAM_B2) * (g * g)
        g_ref[...] = g
        nm_ref[...] = mn
        nv_ref[...] = vn
        d_ref[...] = -ADAM_LR * ((mn / bc1) / (jnp.sqrt(vn / bc2) + ADAM_EPS) + ADAM_WD * w_ref[...])

    blk = pl.BlockSpec((tr, cdim), lambda i: (i, 0))
    o = jax.ShapeDtypeStruct((r, cdim), F32)
    return pl.pallas_call(
        body, name=name, grid=(r // tr,),
        in_specs=[pl.BlockSpec((n_parts, tr, cdim), lambda i: (0, i, 0)), blk, blk, blk],
        out_specs=(blk, blk, blk, blk), out_shape=(o, o, o, o),
        compiler_params=_params())(parts, w, m, v)


_SMALL = ("c_ctx", "ada_b", "norm_g", "b_in", "conv_b", "conv_ln_g", "conv_ln_b", "decay_bias_fwd",
          "decay_bias_bwd", "gla_norm_g", "final_norm_g")
_WEIGHTS = ("c_ctx", "ada_w", "ada_b", "norm_g", "w_in", "b_in", "conv_w", "conv_b", "conv_ln_g", "conv_ln_b",
            "conv_proj", "decay_up_fwd", "decay_bias_fwd", "decay_up_bwd", "decay_bias_bwd", "gla_norm_g",
            "gla_proj", "w_out", "final_norm_g")


def _as2d(a):
    if a.ndim == 1:
        return a.reshape(1, -1)
    return a.reshape(-1, a.shape[-1])


def kernel(x, c, ctx, c_ctx, ada_w, ada_b, norm_g, w_in, b_in, conv_w, conv_b, conv_ln_g, conv_ln_b, conv_proj, decay_up_fwd, decay_bias_fwd, decay_up_bwd, decay_bias_bwd, gla_norm_g, gla_proj, w_out, final_norm_g, loss_target, m_c_ctx, m_ada_w, m_ada_b, m_norm_g, m_w_in, m_b_in, m_conv_w, m_conv_b, m_conv_ln_g, m_conv_ln_b, m_conv_proj, m_decay_up_fwd, m_decay_bias_fwd, m_decay_up_bwd, m_decay_bias_bwd, m_gla_norm_g, m_gla_proj, m_w_out, m_final_norm_g, v_c_ctx, v_ada_w, v_ada_b, v_norm_g, v_w_in, v_b_in, v_conv_w, v_conv_b, v_conv_ln_g, v_conv_ln_b, v_conv_proj, v_decay_up_fwd, v_decay_bias_fwd, v_decay_up_bwd, v_decay_bias_bwd, v_gla_norm_g, v_gla_proj, v_w_out, v_final_norm_g):
    env = dict(locals())
    wts = {k: env[k] for k in _WEIGHTS}
    d = x.shape[-1]
    r = decay_up_fwd.shape[1]
    dk_ = d // 2
    n_in = w_in.shape[-1] * N_DEV

    proj3 = jnp.concatenate([conv_proj[0], gla_proj[0], w_out[0]], axis=0).astype(BF16)
    small_pack = jnp.concatenate([
        jnp.pad(conv_w[0], ((0, 32 - conv_w.shape[1]), (0, 0))),
        jnp.concatenate([decay_up_fwd[0], decay_up_bwd[0]], axis=1)], axis=0)
    g_win, g_ada, g_proj, g_small = _all_gather(
        [w_in[0].astype(BF16), ada_w[0].astype(BF16), proj3, small_pack])

    w_in_full = g_win.transpose(1, 0, 2).reshape(d, n_in)
    w_a, w_b = _regroup(w_in_full, d, r)
    ada_w_full = g_ada.transpose(1, 0, 2).reshape(d, 3 * d)
    ds = d // N_DEV
    conv_proj_full = g_proj[:, 0:ds].reshape(d, d)
    gla_proj_full = g_proj[:, ds:2 * ds].reshape(d, d)
    w_out_full = g_proj[:, 2 * ds:3 * ds].reshape(d, d)
    ktaps = conv_w.shape[1]
    conv_w_full = g_small[:, 0:ktaps].transpose(1, 0, 2).reshape(ktaps, d)
    up_f = g_small[:, 32:32 + r, 0:dk_ // N_DEV].transpose(1, 0, 2).reshape(r, dk_)
    up_b = g_small[:, 32:32 + r, dk_ // N_DEV:].transpose(1, 0, 2).reshape(r, dk_)
    up2 = jnp.zeros((LANE, 2 * dk_), F32).at[0:r, 0:dk_].set(up_f).at[r:2 * r, dk_:].set(up_b)
    bias2 = jnp.concatenate([decay_bias_fwd, decay_bias_bwd], axis=1)
    b_a, b_b = _regroup(b_in, d, r)

    g = _local_step(x, c, ctx, loss_target, c_ctx, ada_w_full, ada_b, norm_g[0:1], w_a, b_a, w_b, b_b,
                    conv_w_full, conv_b, conv_ln_g, conv_ln_b, conv_proj_full, up2, bias2, gla_norm_g,
                    gla_proj_full, w_out_full, final_norm_g.reshape(1, d))

    dw_in = _ungroup(g["w_a"], g["w_b"], d, r)
    e_win = dw_in.reshape(d, N_DEV, n_in // N_DEV).transpose(1, 0, 2).astype(BF16)
    e_ada = g["ada_w"].reshape(d, N_DEV, 3 * d // N_DEV).transpose(1, 0, 2).astype(BF16)
    e_proj = jnp.concatenate([g["conv_proj"].reshape(N_DEV, ds, d), g["gla_proj"].reshape(N_DEV, ds, d),
                              g["w_out"].reshape(N_DEV, ds, d)], axis=1).astype(BF16)
    d_up_f, d_up_b = g["up2"][0:r, 0:dk_], g["up2"][r:2 * r, dk_:]
    e_small = jnp.concatenate([
        jnp.pad(g["conv_w"], ((0, 32 - ktaps), (0, 0))).reshape(32, N_DEV, ds).transpose(1, 0, 2),
        jnp.concatenate([d_up_f.reshape(r, N_DEV, dk_ // N_DEV).transpose(1, 0, 2),
                         d_up_b.reshape(r, N_DEV, dk_ // N_DEV).transpose(1, 0, 2)], axis=2)], axis=1)
    mine = [e_win, e_ada, e_proj, e_small]
    theirs = _exchange_sibling(mine)
    chip_sums = [_pair_sum("pair_sum_" + nm, a, b)
                 for nm, a, b in zip(("w_in", "ada_w", "proj", "small"), mine, theirs)]
    x_win, x_ada, x_proj, x_small = _exchange_chips(chip_sums)

    db_in = _ungroup(g["b_a"], g["b_b"], d, r)
    small_g = dict(c_ctx=g["c_ctx"].reshape(1, d), ada_b=g["ada_b"], norm_g=g["norm_g"], b_in=db_in,
                   conv_b=g["conv_b"], conv_ln_g=g["conv_ln_g"], conv_ln_b=g["conv_ln_b"],
                   decay_bias_fwd=g["bias2"][:, 0:dk_], decay_bias_bwd=g["bias2"][:, dk_:],
                   gla_norm_g=g["gla_norm_g"], final_norm_g=g["final_norm_g"])
    sizes = [wts[k].size for k in _SMALL]
    n_small = sum(sizes) + 1
    n_pad = -n_small % LANE
    pack = lambda parts: jnp.concatenate([p.reshape(1, -1) for p in parts] + [jnp.zeros((1, n_pad + 1), F32)], axis=1)
    gpack = jnp.concatenate([small_g[k].reshape(1, -1) for k in _SMALL]
                            + [g["loss"].reshape(1, 1), jnp.zeros((1, n_pad), F32)], axis=1)
    (gpacks,) = _all_gather([gpack])
    sg, sd, sm, sv = _sum_adam("small_adam", gpacks, pack([wts[k] for k in _SMALL]),
                               pack([env["m_" + k] for k in _SMALL]), pack([env["v_" + k] for k in _SMALL]))

    out = {}
    off = 0
    for k, n in zip(_SMALL, sizes):
        for pre, arr in (("grad_", sg), ("delta_", sd), ("new_m_", sm), ("new_v_", sv)):
            out[pre + k] = arr[0, off:off + n].reshape(wts[k].shape)
        off += n
    loss = sg[0, off]

    def big(name, parts, wname):
        w2 = _as2d(wts[wname])
        res = _sum_adam(name, parts, w2, _as2d(env["m_" + wname]), _as2d(env["v_" + wname]))
        for pre, arr in zip(("grad_", "delta_", "new_m_", "new_v_"), res):
            out[pre + wname] = arr.reshape(wts[wname].shape)

    big("adam_w_in", x_win, "w_in")
    big("adam_ada_w", x_ada, "ada_w")
    big("adam_conv_proj", x_proj[:, 0:ds], "conv_proj")
    big("adam_gla_proj", x_proj[:, ds:2 * ds], "gla_proj")
    big("adam_w_out", x_proj[:, 2 * ds:3 * ds], "w_out")
    big("adam_conv_w", x_small[:, 0:ktaps], "conv_w")
    big("adam_up_f", x_small[:, 32:32 + r, 0:dk_ // N_DEV], "decay_up_fwd")
    big("adam_up_b", x_small[:, 32:32 + r, dk_ // N_DEV:], "decay_up_bwd")

    return (loss, g["grad_x"], *[out["grad_" + k] for k in _WEIGHTS], *[out["delta_" + k] for k in _WEIGHTS],
            *[out["new_m_" + k] for k in _WEIGHTS], *[out["new_v_" + k] for k in _WEIGHTS])
```

```python
import functools
import math

import jax
import jax.numpy as jnp
from jax import lax
from jax.experimental import pallas as pl
from jax.experimental.pallas import tpu as pltpu

F32 = jnp.float32
BF16 = jnp.bfloat16
MESH = pl.DeviceIdType.MESH

N_DEV = 8
GRID_W = 64
CHUNK = 128
HEADS = 4
EPS = 1e-6
GATE_TAU = 16.0
LANE = 128
ADAM_LR, ADAM_B1, ADAM_B2, ADAM_EPS, ADAM_WD, ADAM_STEP = 0.001, 0.9, 0.999, 1e-08, 0.01, 10
VMEM_LIMIT = 56 * 1024 * 1024


def _params(**kw):
    return pltpu.CompilerParams(vmem_limit_bytes=VMEM_LIMIT, **kw)


def _tile(n, pref):
    t = (min(pref, n) // LANE) * LANE
    while t >= LANE:
        if n % t == 0:
            return t
        t -= LANE
    return n


def _mm(a, b):
    return jnp.dot(a.astype(BF16), b.astype(BF16), preferred_element_type=F32)


def _mm_nt(a, b):
    return lax.dot_general(a.astype(BF16), b.astype(BF16), (((1,), (1,)), ((), ())), preferred_element_type=F32)


def _mm_tn(a, b):
    return lax.dot_general(a.astype(BF16), b.astype(BF16), (((0,), (0,)), ((), ())), preferred_element_type=F32)


def _mm_tn_hi(a, b):
    return lax.dot_general(a, b, (((0,), (0,)), ((), ())), precision=lax.Precision.HIGHEST, preferred_element_type=F32)


def _sigmoid(x):
    return 0.5 * jnp.tanh(0.5 * x) + 0.5


def _dsilu(x, s):
    return s * (1.0 + x * (1.0 - s))


def _rowsel(table, idx, n):
    out = table[0:1, :]
    for r in range(1, n):
        out = jnp.where(idx == r, table[r:r + 1, :], out)
    return out


def _ada_fwd(cv, ada_w8, ada_b):
    n_sh, _, ws = ada_w8.shape

    def body(cv_ref, w_ref, b_ref, o_ref):
        c = cv_ref[...]
        sv = c * _sigmoid(c)
        for j in range(n_sh):
            cols = pl.ds(j * ws, ws)
            o_ref[:, cols] = _mm(sv, w_ref[j]) + b_ref[:, cols]

    return pl.pallas_call(body, name="ada_fwd", out_shape=jax.ShapeDtypeStruct((cv.shape[0], n_sh * ws), F32),
                          compiler_params=_params())(cv, ada_w8, ada_b)


def _ada_bwd(cv, ada_w8, dmod_ss, small, nb):
    n_sh, d, ws = ada_w8.shape

    def body(cv_ref, w_ref, dm_ref, sm_ref, dw_ref, db_ref, dc_ref):
        c = cv_ref[...]
        s = _sigmoid(c)
        sv = c * s
        dm = jnp.concatenate([dm_ref[:, 0:2 * d], sm_ref[8:16, :]], axis=1)
        db_ref[...] = jnp.sum(dm, axis=0, keepdims=True)
        dsv = None
        for j in range(n_sh):
            dmj = dm[:, j * ws:(j + 1) * ws]
            dw_ref[j] = _mm_tn_hi(sv, dmj).astype(dw_ref.dtype)
            part = _mm_nt(dmj, w_ref[j])
            dsv = part if dsv is None else dsv + part
        dc_ref[...] = dsv * _dsilu(c, s)

    return pl.pallas_call(
        body, name="ada_bwd",
        out_shape=(jax.ShapeDtypeStruct((n_sh, d, ws), BF16), jax.ShapeDtypeStruct((1, n_sh * ws), F32),
                   jax.ShapeDtypeStruct(cv.shape, F32)),
        compiler_params=_params())(cv, ada_w8, dmod_ss, small)


class _Tiles:
    def __init__(self, nb, s_len, c_len, tm, big):
        self.nb, self.tm, self.big = nb, tm, big
        self.lat, self.ctx = s_len // tm, c_len // tm
        self.pad = -(self.lat + self.ctx) % big
        self.per_ex = self.lat + self.ctx + self.pad
        self.n_all, self.n_lat = nb * self.per_ex, nb * self.lat
        self.rows_per_ex = self.per_ex * tm

    def is_lat(self, i):
        return i % self.per_ex < self.lat

    def is_pad(self, i):
        return i % self.per_ex >= self.lat + self.ctx

    def lat_of_all(self, i):
        return (i // self.per_ex) * self.lat + jnp.minimum(i % self.per_ex, self.lat - 1)

    def ctx_of_all(self, i):
        return (i // self.per_ex) * self.ctx + jnp.clip(i % self.per_ex - self.lat, 0, self.ctx - 1)

    def big_all_of_lat(self, t):
        lat_big = self.lat // self.big
        return (t // lat_big) * (self.per_ex // self.big) + t % lat_big


def _norm_fwd(x2, ctx2, mod, norm_g, tiles):
    tl, d = x2.shape
    tc = ctx2.shape[0]
    nb, tm = tiles.nb, tiles.tm

    def body(x_ref, c_ref, mod_ref, g_ref, u_ref):
        i = pl.program_id(0)
        lat = tiles.is_lat(i)
        xv = jnp.where(lat, x_ref[...], c_ref[...])
        row = jnp.where(lat, i // tiles.per_ex, nb)
        m = _rowsel(mod_ref[...], row, nb + 1)
        shift, scale = m[:, 0:d], m[:, d:2 * d]
        rstd = lax.rsqrt(jnp.mean(xv * xv, axis=-1, keepdims=True) + EPS)
        u = xv * rstd * g_ref[...] * (1.0 + scale) + shift
        u_ref[...] = jnp.where(tiles.is_pad(i), 0.0, u).astype(BF16)

    return pl.pallas_call(
        body, name="norm_fwd", grid=(tiles.n_all,),
        in_specs=[pl.BlockSpec((tm, d), lambda i: (tiles.lat_of_all(i), 0)),
                  pl.BlockSpec((tm, d), lambda i: (tiles.ctx_of_all(i), 0)),
                  pl.BlockSpec(mod.shape, lambda i: (0, 0)),
                  pl.BlockSpec((1, d), lambda i: (0, 0))],
        out_specs=pl.BlockSpec((tm, d), lambda i: (i, 0)),
        out_shape=jax.ShapeDtypeStruct((tiles.n_all * tm, d), BF16),
        compiler_params=_params())(x2, ctx2, mod, norm_g)


def _norm_bwd(x2, ctx2, mod, norm_g, du_lat, du_b, gx1, tiles):
    tl, d = x2.shape
    nb, tm = tiles.nb, tiles.tm
    nrow = mod.shape[0]
    n_lat_in = len(du_lat)

    def body(x_ref, c_ref, mod_ref, g_ref, *refs):
        dl_refs = refs[:n_lat_in]
        d3_ref, gx_ref, gxo_ref, dmod_ref, dg_ref = refs[n_lat_in:]
        i = pl.program_id(0)

        @pl.when(i == 0)
        def _():
            dmod_ref[...] = jnp.zeros_like(dmod_ref)
            dg_ref[...] = jnp.zeros_like(dg_ref)

        lat = tiles.is_lat(i)
        xv = jnp.where(lat, x_ref[...], c_ref[...])
        row = jnp.where(lat, i // tiles.per_ex, nb)
        m = _rowsel(mod_ref[...], row, nb + 1)
        scale = m[:, d:2 * d]
        g = g_ref[...]
        dl = dl_refs[0][...]
        for ref in dl_refs[1:]:
            dl = dl + ref[...]
        du = jnp.where(tiles.is_pad(i), 0.0, d3_ref[...] + jnp.where(lat, dl, 0.0))
        rstd = lax.rsqrt(jnp.mean(xv * xv, axis=-1, keepdims=True) + EPS)
        xh = xv * rstd
        dshift = jnp.sum(du, axis=0, keepdims=True)
        dscale = jnp.sum(du * xh * g, axis=0, keepdims=True)
        dxn = du * (1.0 + scale)
        dg_ref[...] += jnp.sum(dxn * xh, axis=0, keepdims=True)
        dxh = dxn * g
        dx = rstd * (dxh - xh * jnp.mean(dxh * xh, axis=-1, keepdims=True))

        @pl.when(lat)
        def _():
            gxo_ref[...] = dx + gx_ref[...]

        for r in range(nb + 1):
            dmod_ref[r:r + 1, 0:d] += jnp.where(row == r, dshift, 0.0)
            dmod_ref[r:r + 1, d:2 * d] += jnp.where(row == r, dscale, 0.0)

    lat_map = lambda i: (tiles.lat_of_all(i), 0)
    lat_spec = pl.BlockSpec((tm, d), lat_map)
    return pl.pallas_call(
        body, name="norm_bwd", grid=(tiles.n_all,),
        in_specs=[lat_spec,
                  pl.BlockSpec((tm, d), lambda i: (tiles.ctx_of_all(i), 0)),
                  pl.BlockSpec(mod.shape, lambda i: (0, 0)),
                  pl.BlockSpec((1, d), lambda i: (0, 0))]
                 + [lat_spec] * n_lat_in
                 + [pl.BlockSpec((tm, d), lambda i: (i, 0)), lat_spec],
        out_specs=(lat_spec,
                   pl.BlockSpec((nrow, 3 * d), lambda i: (0, 0)),
                   pl.BlockSpec((1, d), lambda i: (0, 0))),
        out_shape=(jax.ShapeDtypeStruct((tl, d), F32), jax.ShapeDtypeStruct((nrow, 3 * d), F32),
                   jax.ShapeDtypeStruct((1, d), F32)),
        compiler_params=_params())(x2, ctx2, mod, norm_g, *du_lat, du_b, gx1)


def _matmul_bias(name, u, w, b, rows, tm, tn, u_tile=lambda i: i):
    d, n = w.shape

    def body(u_ref, w_ref, b_ref, o_ref):
        o_ref[...] = jnp.dot(u_ref[...], w_ref[...], preferred_element_type=F32) + b_ref[...]

    return pl.pallas_call(
        body, name=name, grid=(n // tn, rows // tm),
        in_specs=[pl.BlockSpec((tm, d), lambda j, i: (u_tile(i), 0)),
                  pl.BlockSpec((d, tn), lambda j, i: (0, j)),
                  pl.BlockSpec((1, tn), lambda j, i: (0, j))],
        out_specs=pl.BlockSpec((tm, tn), lambda j, i: (i, j)),
        out_shape=jax.ShapeDtypeStruct((rows, n), F32),
        compiler_params=_params())(u, w, b)


def _matmul_nt(name, a, w, koff, tm, tk):
    r, kc = a.shape
    d = w.shape[0]
    nk = kc // tk

    def body(a_ref, w_ref, o_ref):
        k = pl.program_id(1)
        p = lax.dot_general(a_ref[...], w_ref[...], (((1,), (1,)), ((), ())), preferred_element_type=F32)

        @pl.when(k == 0)
        def _():
            o_ref[...] = p

        @pl.when(k > 0)
        def _():
            o_ref[...] += p

    return pl.pallas_call(
        body, name=name, grid=(r // tm, nk),
        in_specs=[pl.BlockSpec((tm, tk), lambda i, k: (i, k)),
                  pl.BlockSpec((d, tk), lambda i, k: (0, koff + k))],
        out_specs=pl.BlockSpec((tm, d), lambda i, k: (i, 0)),
        out_shape=jax.ShapeDtypeStruct((r, d), F32),
        compiler_params=_params())(a, w)


def _matmul_tn(name, a, b, rows, tk, tn):
    m = a.shape[1]
    n = b.shape[1]
    nk = rows // tk

    def body(a_ref, b_ref, o_ref, s_ref, acc_ref):
        k = pl.program_id(1)
        bv = b_ref[...]
        p = lax.dot_general(a_ref[...], bv, (((0,), (0,)), ((), ())), preferred_element_type=F32)
        cs = jnp.sum(bv.astype(F32), axis=0, keepdims=True)

        @pl.when(k == 0)
        def _():
            acc_ref[...] = p
            s_ref[...] = cs

        @pl.when(k > 0)
        def _():
            acc_ref[...] += p
            s_ref[...] += cs

        @pl.when(k == nk - 1)
        def _():
            o_ref[...] = acc_ref[...].astype(o_ref.dtype)

    return pl.pallas_call(
        body, name=name, grid=(n // tn, nk),
        in_specs=[pl.BlockSpec((tk, m), lambda j, k: (k, 0)),
                  pl.BlockSpec((tk, tn), lambda j, k: (k, j))],
        out_specs=(pl.BlockSpec((m, tn), lambda j, k: (0, j)), pl.BlockSpec((1, tn), lambda j, k: (0, j))),
        out_shape=(jax.ShapeDtypeStruct((m, n), BF16), jax.ShapeDtypeStruct((1, n), F32)),
        scratch_shapes=[pltpu.VMEM((m, tn), F32)],
        compiler_params=_params())(a, b)


def _matmul_tn_whole(name, a3, b3, rows, tn):
    nb, _, m = a3.shape
    n = b3.shape[2]

    def body(a_ref, b_ref, o_ref, s_ref):
        p, cs = None, None
        for e in range(nb):
            bv = b_ref[e]
            pe = lax.dot_general(a_ref[e], bv, (((0,), (0,)), ((), ())), preferred_element_type=F32)
            ce = jnp.sum(bv.astype(F32), axis=0, keepdims=True)
            p, cs = (pe, ce) if p is None else (p + pe, cs + ce)
        o_ref[...] = p.astype(o_ref.dtype)
        s_ref[...] = cs

    return pl.pallas_call(
        body, name=name, grid=(n // tn,),
        in_specs=[pl.BlockSpec((nb, rows, m), lambda j: (0, 0, 0)),
                  pl.BlockSpec((nb, rows, tn), lambda j: (0, 0, j))],
        out_specs=(pl.BlockSpec((m, tn), lambda j: (0, j)), pl.BlockSpec((1, tn), lambda j: (0, j))),
        out_shape=(jax.ShapeDtypeStruct((m, n), BF16), jax.ShapeDtypeStruct((1, n), F32)),
        compiler_params=_params())(a3, b3)


def _conv_window(pad_ref, r, shift, ktaps, width, horizontal):
    if horizontal:
        return pad_ref[r, pl.ds(16 + shift, width), :]
    return pad_ref[r + ktaps // 2 + shift]


def _conv_row(pad_ref, w, r, ktaps, width, horizontal, flip):
    half = ktaps // 2
    acc = None
    for t in range(ktaps):
        win = _conv_window(pad_ref, r, (half - t) if flip else (t - half), ktaps, width, horizontal)
        term = win * w[t:t + 1, :]
        acc = term if acc is None else acc + term
    return acc


def _fill_padded(ref, val, rows, width, ktaps, horizontal):
    half_k = ktaps // 2
    cb = val.shape[-1]
    if horizontal:
        ref[:, 0:16, :] = jnp.zeros((rows, 16, cb), F32)
        ref[:, 16 + width:32 + width, :] = jnp.zeros((rows, 16, cb), F32)
        ref[:, 16:16 + width, :] = val
    else:
        ref[0:half_k, :, :] = jnp.zeros((half_k, width, cb), F32)
        ref[half_k + rows:2 * half_k + rows, :, :] = jnp.zeros((half_k, width, cb), F32)
        ref[half_k:half_k + rows, :, :] = val


def _conv_fwd(pa, conv_w8, conv_b, nb, s):
    nblk, ktaps, cb = conv_w8.shape
    d = nblk * cb
    rows, width = s // GRID_W, GRID_W
    half_k = ktaps // 2
    nh = nblk // 2

    def body(glu_ref, w_ref, b_ref, o_ref, ph_ref, pv_ref):
        j = pl.program_id(1)
        a0 = (glu_ref[:, 0:cb] * _sigmoid(glu_ref[:, cb:2 * cb])).reshape(rows, width, cb)
        w = w_ref[...]

        bias = b_ref[...]

        def run(pad_ref, horizontal):
            _fill_padded(pad_ref, a0, rows, width, ktaps, horizontal)

            def row(r, carry):
                at = pl.ds(pl.multiple_of(r * width, width), width)
                o_ref[at, :] = _conv_row(pad_ref, w, r, ktaps, width, horizontal, False) + bias
                return carry

            lax.fori_loop(0, rows, row, 0)

        @pl.when(j < nh)
        def _():
            run(ph_ref, True)

        @pl.when(j >= nh)
        def _():
            run(pv_ref, False)

    return pl.pallas_call(
        body, name="conv_fwd", grid=(nb, nblk),
        in_specs=[pl.BlockSpec((s, 2 * cb), lambda b, j: (b, j)),
                  pl.BlockSpec((None, ktaps, cb), lambda b, j: (j, 0, 0)),
                  pl.BlockSpec((1, cb), lambda b, j: (0, j))],
        out_specs=pl.BlockSpec((s, cb), lambda b, j: (b, j)),
        out_shape=jax.ShapeDtypeStruct((nb * s, d), F32),
        scratch_shapes=[pltpu.VMEM((rows, width + 32, cb), F32), pltpu.VMEM((rows + 2 * half_k, width, cb), F32)],
        compiler_params=_params())(pa, conv_w8, conv_b)


def _conv_bwd(pa, da1, conv_w8, nb, s):
    nblk, ktaps, cb = conv_w8.shape
    d = nblk * cb
    rows, width = s // GRID_W, GRID_W
    half_k = ktaps // 2
    nh = nblk // 2

    def body(glu_ref, da_ref, w_ref, dp_ref, dw_ref, db_ref, pha_ref, phd_ref, pva_ref, pvd_ref):
        j = pl.program_id(0)
        b = pl.program_id(1)
        a0 = (glu_ref[:, 0:cb] * _sigmoid(glu_ref[:, cb:2 * cb])).reshape(rows, width, cb)
        da1v = da_ref[...]
        d3 = da1v.reshape(rows, width, cb)
        w = w_ref[...]

        @pl.when(b == 0)
        def _():
            dw_ref[...] = jnp.zeros_like(dw_ref)
            db_ref[...] = jnp.zeros_like(db_ref)

        db_ref[...] += jnp.sum(da1v, axis=0, keepdims=True)

        def run(pa_ref, pd_ref, horizontal):
            _fill_padded(pa_ref, a0, rows, width, ktaps, horizontal)
            _fill_padded(pd_ref, d3, rows, width, ktaps, horizontal)

            def row(r, accs):
                at = pl.ds(pl.multiple_of(r * width, width), width)
                da0 = _conv_row(pd_ref, w, r, ktaps, width, horizontal, True)
                gv = glu_ref[at, 0:cb]
                sg = _sigmoid(glu_ref[at, cb:2 * cb])
                dp_ref[at, 0:cb] = (da0 * sg).astype(BF16)
                dp_ref[at, cb:2 * cb] = (da0 * gv * sg * (1.0 - sg)).astype(BF16)
                d_row = da_ref[at, :]
                out = []
                for t in range(ktaps):
                    prod = _conv_window(pa_ref, r, t - half_k, ktaps, width, horizontal) * d_row
                    out.append(accs[t] + jnp.sum(prod.reshape(width // 8, 8, cb), axis=0))
                return tuple(out)

            accs = lax.fori_loop(0, rows, row, tuple(jnp.zeros((8, cb), F32) for _ in range(ktaps)))
            for t in range(ktaps):
                dw_ref[t:t + 1, :] += jnp.sum(accs[t], axis=0, keepdims=True)

        @pl.when(j < nh)
        def _():
            run(pha_ref, phd_ref, True)

        @pl.when(j >= nh)
        def _():
            run(pva_ref, pvd_ref, False)

    return pl.pallas_call(
        body, name="conv_bwd", grid=(nblk, nb),
        in_specs=[pl.BlockSpec((s, 2 * cb), lambda j, b: (b, j)),
                  pl.BlockSpec((s, cb), lambda j, b: (b, j)),
                  pl.BlockSpec((None, ktaps, cb), lambda j, b: (j, 0, 0))],
        out_specs=(pl.BlockSpec((s, 2 * cb), lambda j, b: (b, j)),
                   pl.BlockSpec((None, ktaps, cb), lambda j, b: (j, 0, 0)),
                   pl.BlockSpec((1, cb), lambda j, b: (0, j))),
        out_shape=(jax.ShapeDtypeStruct((nb * s, 2 * d), BF16),
                   jax.ShapeDtypeStruct((nblk, ktaps, cb), F32), jax.ShapeDtypeStruct((1, d), F32)),
        scratch_shapes=[pltpu.VMEM((rows, width + 32, cb), F32), pltpu.VMEM((rows, width + 32, cb), F32),
                        pltpu.VMEM((rows + 2 * half_k, width, cb), F32),
                        pltpu.VMEM((rows + 2 * half_k, width, cb), F32)],
        compiler_params=_params())(pa, da1, conv_w8)


def _log_sigmoid(x):
    return jnp.minimum(x, 0.0) - jnp.log(1.0 + jnp.exp(-jnp.abs(x)))


def _decay_fwd(pb, up2, bias2, tm, lr_blk):
    t_all = pb.shape[0]
    n2 = up2.shape[1]

    def body(lr_ref, up_ref, b_ref, g_ref):
        logits = _mm(lr_ref[...], up_ref[...]) + b_ref[...]
        g_ref[...] = _log_sigmoid(logits) * (1.0 / GATE_TAU)

    return pl.pallas_call(
        body, name="decay_fwd", grid=(t_all // tm,),
        in_specs=[pl.BlockSpec((tm, LANE), lambda i: (i, lr_blk)),
                  pl.BlockSpec(up2.shape, lambda i: (0, 0)),
                  pl.BlockSpec((1, n2), lambda i: (0, 0))],
        out_specs=pl.BlockSpec((tm, n2), lambda i: (i, 0)),
        out_shape=jax.ShapeDtypeStruct((t_all, n2), F32),
        compiler_params=_params())(pb, up2, bias2)


def _decay_bwd(pb, up2, bias2, grads_f, grads_b, tiles, lr_blk, dk_, dv_):
    t_all = pb.shape[0]
    tm = tiles.tm
    n2 = up2.shape[1]
    nbw = 2 * dk_ + dv_ + LANE

    def body(lr_ref, up_ref, b_ref, dqf, dkf, dvf, dgf, dqb, dkb, dvb, dgb, dp_ref, dup_ref, dbias_ref):
        i = pl.program_id(0)
        pad = tiles.is_pad(i)
        live = lambda v: jnp.where(pad, 0.0, v)

        @pl.when(i == 0)
        def _():
            dup_ref[...] = jnp.zeros_like(dup_ref)
            dbias_ref[...] = jnp.zeros_like(dbias_ref)

        lr = lr_ref[...]
        up = up_ref[...]
        logits = _mm(lr, up) + b_ref[...]
        dg = live(jnp.concatenate([dgf[...], dgb[...]], axis=1))
        dlog = dg * (1.0 / GATE_TAU) * _sigmoid(-logits)
        dup_ref[...] += _mm_tn(lr, dlog)
        dbias_ref[...] += jnp.sum(dlog, axis=0, keepdims=True)
        dp_ref[:, 0:dk_] = live(dqf[...] + dqb[...]).astype(BF16)
        dp_ref[:, dk_:2 * dk_] = live(dkf[...] + dkb[...]).astype(BF16)
        dp_ref[:, 2 * dk_:2 * dk_ + dv_] = live(dvf[...] + dvb[...]).astype(BF16)
        dp_ref[:, 2 * dk_ + dv_:nbw] = _mm_nt(dlog, up).astype(BF16)

    row = lambda w: pl.BlockSpec((tm, w), lambda i: (i, 0))
    return pl.pallas_call(
        body, name="decay_bwd", grid=(t_all // tm,),
        in_specs=[pl.BlockSpec((tm, LANE), lambda i: (i, lr_blk)),
                  pl.BlockSpec(up2.shape, lambda i: (0, 0)),
                  pl.BlockSpec((1, n2), lambda i: (0, 0)),
                  row(dk_), row(dk_), row(dv_), row(dk_), row(dk_), row(dk_), row(dv_), row(dk_)],
        out_specs=(row(nbw), pl.BlockSpec(up2.shape, lambda i: (0, 0)), pl.BlockSpec((1, n2), lambda i: (0, 0))),
        out_shape=(jax.ShapeDtypeStruct((t_all, nbw), BF16), jax.ShapeDtypeStruct(up2.shape, F32),
                   jax.ShapeDtypeStruct((1, n2), F32)),
        compiler_params=_params())(pb, up2, bias2, *grads_f, *grads_b)


def _scan_chunk(s, nl, nc, rev):
    if rev:
        return jnp.where(s < nc, nl + (nc - 1 - s), nl - 1 - (s - nc))
    return jnp.where(s < nc, nl + s, s - nc)


def _scan_lat_chunk(s, nl, nc, rev):
    first = nl - 1 if rev else 0
    return jnp.where(s < nc, first, _scan_chunk(s, nl, nc, rev))


def _tri_mm(m_bf, x):
    hi = x.astype(BF16)
    r1 = x - hi.astype(F32)
    mid = r1.astype(BF16)
    lo = (r1 - mid.astype(F32)).astype(BF16)
    dot = lambda p: jnp.dot(m_bf, p, preferred_element_type=F32)
    return dot(hi) + dot(mid) + dot(lo)


def _chunk_masks(c, rev):
    ii = lax.broadcasted_iota(jnp.int32, (c, c), 0)
    jj = lax.broadcasted_iota(jnp.int32, (c, c), 1)
    return ((ii <= jj), (ii >= jj)) if rev else ((ii >= jj), (ii <= jj))


def _chunk_terms(q, k, b, far, mid):
    bf, bm = b[far:far + 1, :], b[mid:mid + 1, :]
    e = jnp.exp(b)
    em = jnp.exp(b - bm)
    eim = jnp.exp(bm - b)
    ed = jnp.exp(bf - b)
    return dict(e=e, em=em, eim=eim, ed=ed, dec=jnp.exp(bf), qe=q * e, qem=q * em, kim=k * eim, kd=k * ed)


def _gla_fwd(pb3, g3, nb, s_len, c_len, dk_, dv_):
    c = CHUNK
    nl, nc = s_len // c, c_len // c
    ns = nl + nc
    hk, hv = dk_ // HEADS, dv_ // HEADS
    l_len = pb3.shape[1]
    scale = hk ** -0.5
    mid = c // 2

    def body(*refs):
        ins, outs, z_scr = refs[:8], refs[8:14], refs[14]
        s = pl.program_id(0)

        @pl.when(s == 0)
        def _():
            z_scr[...] = jnp.zeros_like(z_scr)

        qs = jnp.where(s >= nc, scale, 0.0)
        for di, rev in enumerate((False, True)):
            q_ref, k_ref, v_ref, g_ref = ins[4 * di:4 * di + 4]
            o_ref, zs_ref, b_ref = outs[3 * di:3 * di + 3]
            mask, _ = _chunk_masks(c, rev)
            m_bf = mask.astype(BF16)
            far = 0 if rev else c - 1
            for b in range(nb):
                bc = _tri_mm(m_bf, g_ref[b])
                b_ref[b] = bc
                for h in range(HEADS):
                    ks, vs = slice(h * hk, (h + 1) * hk), slice(h * hv, (h + 1) * hv)
                    zi = (di * nb + b) * HEADS + h
                    v = v_ref[b, :, vs]
                    t = _chunk_terms(q_ref[b, :, ks] * qs, k_ref[b, :, ks], bc[:, ks], far, mid)
                    a = jnp.where(mask, _mm_nt(t["qem"], t["kim"]), 0.0)
                    z = z_scr[zi]
                    zs_ref[0, b * HEADS + h] = z
                    o_ref[b, :, vs] = _mm(a, v) + _mm_nt(t["qe"], z)
                    z_scr[zi] = z * t["dec"] + _mm_tn(v, t["kd"])

    in_specs, out_specs, out_shape = [], [], []
    for di, rev in enumerate((False, True)):
        ch = functools.partial(_scan_chunk, nl=nl, nc=nc, rev=rev)
        lch = functools.partial(_scan_lat_chunk, nl=nl, nc=nc, rev=rev)
        in_specs += [pl.BlockSpec((nb, c, dk_), lambda s, ch=ch: (0, ch(s), 0)),
                     pl.BlockSpec((nb, c, dk_), lambda s, ch=ch: (0, ch(s), 1)),
                     pl.BlockSpec((nb, c, dv_), lambda s, ch=ch: (0, ch(s), 1)),
                     pl.BlockSpec((nb, c, dk_), lambda s, ch=ch, di=di: (0, ch(s), di))]
        out_specs += [pl.BlockSpec((nb, c, dv_), lambda s, lch=lch: (0, lch(s), 0)),
                      pl.BlockSpec((1, nb * HEADS, hv, hk), lambda s: (s, 0, 0, 0)),
                      pl.BlockSpec((nb, c, dk_), lambda s, ch=ch: (0, ch(s), 0))]
        out_shape += [jax.ShapeDtypeStruct((nb, s_len, dv_), F32),
                      jax.ShapeDtypeStruct((ns, nb * HEADS, hv, hk), F32),
                      jax.ShapeDtypeStruct((nb, l_len, dk_), F32)]
    return pl.pallas_call(
        body, name="gla_fwd", grid=(ns,), in_specs=in_specs, out_specs=tuple(out_specs), out_shape=tuple(out_shape),
        scratch_shapes=[pltpu.VMEM((2 * nb * HEADS, hv, hk), F32)],
        compiler_params=_params())(pb3, pb3, pb3, g3, pb3, pb3, pb3, g3)


def _gla_bwd(pb3, do3, fwd_saved, nb, s_len, c_len, dk_, dv_):
    c = CHUNK
    nl, nc = s_len // c, c_len // c
    ns = nl + nc
    hk, hv = dk_ // HEADS, dv_ // HEADS
    l_len = pb3.shape[1]
    scale = hk ** -0.5
    mid = c // 2
    zs_f, b_f, zs_b, b_b = fwd_saved

    def body(*refs):
        ins, outs, dz_scr = refs[:12], refs[12:20], refs[20]
        s = pl.program_id(0)
        step = ns - 1 - s

        @pl.when(s == 0)
        def _():
            dz_scr[...] = jnp.zeros_like(dz_scr)

        lat = step >= nc
        qs = jnp.where(lat, scale, 0.0)
        dmul = jnp.where(lat, 1.0, 0.0)
        for di, rev in enumerate((False, True)):
            q_ref, k_ref, v_ref, b_ref, do_ref, zs_ref = ins[6 * di:6 * di + 6]
            dq_ref, dk_ref, dv_ref, dg_ref = outs[4 * di:4 * di + 4]
            mask, mask_t = _chunk_masks(c, rev)
            mt_bf = mask_t.astype(BF16)
            far = 0 if rev else c - 1
            far_row = lax.broadcasted_iota(jnp.int32, (c, hk), 0) == far
            for b in range(nb):
                db_parts = []
                for h in range(HEADS):
                    ks, vs = slice(h * hk, (h + 1) * hk), slice(h * hv, (h + 1) * hv)
                    zi = (di * nb + b) * HEADS + h
                    v = v_ref[b, :, vs]
                    d_o = do_ref[b, :, vs] * dmul
                    t = _chunk_terms(q_ref[b, :, ks] * qs, k_ref[b, :, ks], b_ref[b, :, ks], far, mid)
                    qem, kim, qe, kd = t["qem"], t["kim"], t["qe"], t["kd"]
                    a_t = jnp.where(mask_t, _mm_nt(kim, qem), 0.0)
                    d_a = jnp.where(mask, _mm_nt(d_o, v), 0.0)
                    d_at = jnp.where(mask_t, _mm_nt(v, d_o), 0.0)
                    z = zs_ref[0, b * HEADS + h]
                    dzn = dz_scr[zi]
                    dv_ref[b, :, vs] = _mm(a_t, d_o) + _mm_nt(kd, dzn)
                    dqem = _mm(d_a, kim)
                    dkim = _mm(d_at, qem)
                    dqe = _mm(d_o, z)
                    dkd = _mm(v, dzn)
                    ddec = jnp.sum(z * dzn, axis=0, keepdims=True)
                    dz_scr[zi] = dzn * t["dec"] + _mm_tn(d_o, qe)
                    dq_ref[b, :, ks] = (dqem * t["em"] + dqe * t["e"]) * qs
                    dk_ref[b, :, ks] = dkim * t["eim"] + dkd * t["ed"]
                    db = dqem * qem - dkim * kim + dqe * qe - dkd * kd
                    extra = jnp.sum(dkd * kd, axis=0, keepdims=True) + ddec * t["dec"]
                    db_parts.append(db + jnp.where(far_row, extra, 0.0))
                dg_ref[b] = _tri_mm(mt_bf, jnp.concatenate(db_parts, axis=1))

    in_specs, out_specs, out_shape, args = [], [], [], []
    for di, rev in enumerate((False, True)):
        ch = lambda s, rev=rev: _scan_chunk(ns - 1 - s, nl, nc, rev)
        lch = lambda s, rev=rev: _scan_lat_chunk(ns - 1 - s, nl, nc, rev)
        in_specs += [pl.BlockSpec((nb, c, dk_), lambda s, ch=ch: (0, ch(s), 0)),
                     pl.BlockSpec((nb, c, dk_), lambda s, ch=ch: (0, ch(s), 1)),
                     pl.BlockSpec((nb, c, dv_), lambda s, ch=ch: (0, ch(s), 1)),
                     pl.BlockSpec((nb, c, dk_), lambda s, ch=ch: (0, ch(s), 0)),
                     pl.BlockSpec((nb, c, dv_), lambda s, lch=lch: (0, lch(s), 0)),
                     pl.BlockSpec((1, nb * HEADS, hv, hk), lambda s: (ns - 1 - s, 0, 0, 0))]
        args += [pb3, pb3, pb3, (b_b if rev else b_f), do3, (zs_b if rev else zs_f)]
        for w in (dk_, dk_, dv_, dk_):
            out_specs.append(pl.BlockSpec((nb, c, w), lambda s, ch=ch: (0, ch(s), 0)))
            out_shape.append(jax.ShapeDtypeStruct((nb, l_len, w), F32))
    return pl.pallas_call(
        body, name="gla_bwd", grid=(ns,), in_specs=in_specs, out_specs=tuple(out_specs), out_shape=tuple(out_shape),
        scratch_shapes=[pltpu.VMEM((2 * nb * HEADS, hv, hk), F32)],
        compiler_params=_params())(*args)


def _tail(a1, pa, o_f, o_b, x2, tgt, mod, wc, wg, wo, ln_g, ln_b, gn_t, fg, nb, tm):
    tl, d = x2.shape
    nt = tl // tm
    per_ex = nt // nb
    hv = d // HEADS
    nrow = mod.shape[0]

    def body(a1_ref, z_ref, r_ref, mc_ref, mg_ref, of_ref, ob_ref, x_ref, t_ref, mod_ref, wc_ref, wg_ref, wo_ref,
             lng_ref, lnb_ref, gn_ref, fg_ref,
             dp_ref, da1_ref, do_ref, gx_ref, mrg_ref, dmo_ref, yci_ref, dyc_ref, ogi_ref, dyg_ref, sm_ref):
        i = pl.program_id(0)

        @pl.when(i == 0)
        def _():
            sm_ref[...] = jnp.zeros_like(sm_ref)

        bidx = i // per_ex
        gate = _rowsel(mod_ref[...], bidx, nb)[:, 2 * d:3 * d]
        lng, lnb, fgv = lng_ref[...], lnb_ref[...], fg_ref[...]
        gn = jnp.concatenate([gn_ref[...]] * HEADS, axis=1)
        wc_, wg_, wo_ = wc_ref[...], wg_ref[...], wo_ref[...]

        a1v = a1_ref[...]
        mu = jnp.mean(a1v, axis=-1, keepdims=True)
        xc = a1v - mu
        rs = lax.rsqrt(jnp.mean(xc * xc, axis=-1, keepdims=True) + EPS)
        xh = xc * rs
        a2 = xh * lng + lnb
        s2 = _sigmoid(a2)
        a3 = a2 * s2
        zv = z_ref[...]
        sz = _sigmoid(zv)
        siluz = zv * sz
        ycin = a3 * siluz
        yconv = _mm(ycin, wc_)

        o = of_ref[...] + ob_ref[...]
        ohat_parts, rn_parts = [], []
        for h in range(HEADS):
            oh = o[:, h * hv:(h + 1) * hv]
            rn = lax.rsqrt(jnp.mean(oh * oh, axis=-1, keepdims=True) + EPS)
            ohat_parts.append(oh * rn)
            rn_parts.append(rn)
        ohat = jnp.concatenate(ohat_parts, axis=1)
        on = ohat * gn
        rv = r_ref[...]
        sr = _sigmoid(rv)
        silur = rv * sr
        ogin = on * silur
        ygla = _mm(ogin, wg_)

        sc = _sigmoid(mc_ref[...])
        sg = _sigmoid(mg_ref[...])
        merged = sc * yconv + sg * ygla
        mo = _mm(merged, wo_)
        hn = x_ref[...] + gate * mo
        rf = lax.rsqrt(jnp.mean(hn * hn, axis=-1, keepdims=True) + EPS)
        yh = hn * rf
        err = yh * fgv - t_ref[...]
        loss_part = 0.5 * jnp.sum(err * err) * (1.0 / d)

        dy = err * (1.0 / d)
        dfg = jnp.sum(dy * yh, axis=0, keepdims=True)
        dyh = dy * fgv
        dhn = rf * (dyh - yh * jnp.mean(dyh * yh, axis=-1, keepdims=True))
        gx_ref[...] = dhn
        dgate = jnp.sum(dhn * mo, axis=0, keepdims=True)
        dmo = gate * dhn
        dmerged = _mm_nt(dmo, wo_)
        dyconv = dmerged * sc
        dygla = dmerged * sg
        dp_ref[:, 2 * d:3 * d] = (dmerged * yconv * sc * (1.0 - sc)).astype(BF16)
        dp_ref[:, 3 * d:4 * d] = (dmerged * ygla * sg * (1.0 - sg)).astype(BF16)
        dycin = _mm_nt(dyconv, wc_)
        dogin = _mm_nt(dygla, wg_)
        mrg_ref[...] = merged.astype(BF16)
        dmo_ref[...] = dmo.astype(BF16)
        yci_ref[...] = ycin.astype(BF16)
        dyc_ref[...] = dyconv.astype(BF16)
        ogi_ref[...] = ogin.astype(BF16)
        dyg_ref[...] = dygla.astype(BF16)

        da3 = dycin * siluz
        dp_ref[:, 0:d] = (dycin * a3 * _dsilu(zv, sz)).astype(BF16)
        da2 = da3 * _dsilu(a2, s2)
        dlng = jnp.sum(da2 * xh, axis=0, keepdims=True)
        dlnb = jnp.sum(da2, axis=0, keepdims=True)
        dxh = da2 * lng
        da1_ref[...] = rs * (dxh - jnp.mean(dxh, axis=-1, keepdims=True)
                             - xh * jnp.mean(dxh * xh, axis=-1, keepdims=True))

        don = dogin * silur
        dp_ref[:, d:2 * d] = (dogin * on * _dsilu(rv, sr)).astype(BF16)
        dgn = jnp.sum(don * ohat, axis=0, keepdims=True)
        dyn = don * gn
        for h in range(HEADS):
            vs = slice(h * hv, (h + 1) * hv)
            oh_hat = ohat_parts[h]
            dh = dyn[:, vs]
            do_ref[:, vs] = rn_parts[h] * (dh - oh_hat * jnp.mean(dh * oh_hat, axis=-1, keepdims=True))

        sm_ref[0:1, :] += dfg
        sm_ref[1:2, :] += dlng
        sm_ref[2:3, :] += dlnb
        sm_ref[3:4, :] += dgn
        sm_ref[4:5, :] += jnp.zeros((1, d), F32) + loss_part
        for b in range(nb):
            sm_ref[8 + b:9 + b, :] += jnp.where(bidx == b, dgate, 0.0)

    row = pl.BlockSpec((tm, d), lambda i: (i, 0))
    pcol = lambda blk: pl.BlockSpec((tm, d), lambda i: (i, blk))
    full = lambda arr: pl.BlockSpec(arr.shape, lambda i: (0,) * arr.ndim)
    bfo = jax.ShapeDtypeStruct((tl, d), BF16)
    f32o = jax.ShapeDtypeStruct((tl, d), F32)
    return pl.pallas_call(
        body, name="tail", grid=(nt,),
        in_specs=[row, pcol(2), pcol(3), pcol(4), pcol(5), row, row, row, row, full(mod), full(wc), full(wg),
                  full(wo), full(ln_g), full(ln_b), full(gn_t), full(fg)],
        out_specs=(pl.BlockSpec((tm, 4 * d), lambda i: (i, 0)), row, row, row, row, row, row, row, row, row,
                   pl.BlockSpec((16, d), lambda i: (0, 0))),
        out_shape=(jax.ShapeDtypeStruct((tl, 4 * d), BF16), f32o, f32o, f32o, bfo, bfo, bfo, bfo, bfo, bfo,
                   jax.ShapeDtypeStruct((16, d), F32)),
        compiler_params=_params())(a1, pa, pa, pa, pa, o_f, o_b, x2, tgt, mod, wc, wg, wo, ln_g, ln_b, gn_t, fg)


def _local_step(x, c, ctx, tgt, c_ctx, ada_w8, ada_b, norm_g, w_a, b_a, w_b, b_b, conv_w8, conv_b, ln_g, ln_b,
                conv_proj, up2, bias2, gla_norm_g, gla_proj, w_out, final_norm_g):
    nb, s_len, d = x.shape
    c_len = ctx.shape[1]
    dk_, dv_ = d // 2, d
    tl, tc = nb * s_len, nb * c_len
    nbw = 2 * dk_ + dv_ + LANE
    tm = math.gcd(256, c_len)
    tiles = _Tiles(nb, s_len, c_len, tm, 2)
    tmm = tiles.big * tm
    l_len = tiles.rows_per_ex
    t_all = nb * l_len
    x2, ctx2, tgt2 = x.reshape(tl, d), ctx.reshape(tc, d), tgt.reshape(tl, d)

    cv = jnp.zeros((8, d), F32).at[0:nb].set(c).at[nb].set(c_ctx.reshape(d))
    mod = _ada_fwd(cv, ada_w8, ada_b)
    u = _norm_fwd(x2, ctx2, mod, norm_g, tiles)
    pa = _matmul_bias("inproj_a", u, w_a, b_a, tl, tmm, _tile(6 * d, 1536), u_tile=tiles.big_all_of_lat)
    pb = _matmul_bias("inproj_b", u, w_b, b_b, t_all, tmm, nbw)

    a1 = _conv_fwd(pa, conv_w8, conv_b, nb, s_len)
    lr_blk = (2 * dk_ + dv_) // LANE
    g_all = _decay_fwd(pb, up2, bias2, tm, lr_blk)
    pb3 = pb.reshape(nb, l_len, nbw)
    o_f, zs_f, b_f, o_b, zs_b, b_b2 = _gla_fwd(pb3, g_all.reshape(nb, l_len, 2 * dk_), nb, s_len, c_len, dk_, dv_)

    tt = math.gcd(128, s_len)
    (dp_a2, da1, d_o, gx1, merged, dmo, ycin, dyconv, ogin, dygla, small) = _tail(
        a1, pa, o_f.reshape(tl, dv_), o_b.reshape(tl, dv_), x2, tgt2, mod, conv_proj, gla_proj, w_out, ln_g, ln_b,
        gla_norm_g, final_norm_g, nb, tt)

    lat3 = lambda a: a.reshape(nb, s_len, a.shape[-1])
    tnw = _tile(d, 1024)
    d_w_out, _ = _matmul_tn_whole("dw_out", lat3(merged), lat3(dmo), s_len, tnw)
    d_conv_proj, _ = _matmul_tn_whole("dw_conv_proj", lat3(ycin), lat3(dyconv), s_len, tnw)
    d_gla_proj, _ = _matmul_tn_whole("dw_gla_proj", lat3(ogin), lat3(dygla), s_len, tnw)

    dp_a1, d_conv_w8, d_conv_b = _conv_bwd(pa, da1, conv_w8, nb, s_len)
    gl = _gla_bwd(pb3, d_o.reshape(nb, s_len, dv_), (zs_f, b_f, zs_b, b_b2), nb, s_len, c_len, dk_, dv_)
    gl = [g_.reshape(t_all, g_.shape[-1]) for g_ in gl]
    dp_b, d_up2, d_bias2 = _decay_bwd(pb, up2, bias2, gl[0:4], gl[4:8], tiles, lr_blk, dk_, dv_)

    tka = _tile(2 * d, 2048)
    du_a1 = _matmul_nt("du_a1", dp_a1, w_a, 0, tmm, tka)
    du_a2 = _matmul_nt("du_a2", dp_a2, w_a, (2 * d) // tka, tmm, tka)
    du_b = _matmul_nt("du_b", dp_b, w_b, 0, tmm, nbw)
    u3 = u.reshape(nb, l_len, d)
    dw_a1, db_a1 = _matmul_tn_whole("dw_a1", u3, lat3(dp_a1), s_len, tnw)
    dw_a2, db_a2 = _matmul_tn_whole("dw_a2", u3, lat3(dp_a2), s_len, tnw)
    dw_b, db_b = _matmul_tn("dw_b", u, dp_b, t_all, tmm, nbw)

    grad_x2, dmod_ss, d_norm_g = _norm_bwd(x2, ctx2, mod, norm_g, [du_a1, du_a2], du_b, gx1, tiles)
    d_ada_w8, d_ada_b, d_cv = _ada_bwd(cv, ada_w8, dmod_ss, small, nb)

    return dict(
        grad_x=grad_x2.reshape(nb, s_len, d), small=small, cv=d_cv, ada_w8=d_ada_w8, ada_b=d_ada_b,
        norm_g=d_norm_g, w_a1=dw_a1, w_a2=dw_a2, w_b=dw_b, b_a1=db_a1, b_a2=db_a2, b_b=db_b,
        conv_w8=d_conv_w8, conv_b=d_conv_b, conv_proj=d_conv_proj, up2=d_up2, bias2=d_bias2,
        gla_proj=d_gla_proj, w_out=d_w_out)


def _regroup_pieces(d, r, wshard):
    cb = d // N_DEV
    segs = []
    for j in range(N_DEV):
        segs.append((j * cb, cb, 0, 2 * j * cb))
    for j in range(N_DEV):
        segs.append((d + j * cb, cb, 0, (2 * j + 1) * cb))
    segs += [(2 * d, d, 0, 2 * d), (3 * d, 2 * d + 2 * r, 1, 0), (5 * d + 2 * r, 3 * d, 0, 3 * d)]
    pieces = []
    for o0, w, dst, d0 in segs:
        lo = o0
        while lo < o0 + w:
            j = lo // wshard
            hi = min(o0 + w, (j + 1) * wshard)
            pieces.append((j, lo - j * wshard, hi - lo, dst, d0 + lo - o0))
            lo = hi
    return pieces


def _regroup(o, d, r):
    n_in = 8 * d + 2 * r
    parts = ([], [])
    for _, s0, n, dst, _ in sorted(_regroup_pieces(d, r, n_in), key=lambda p: (p[3], p[4])):
        parts[dst].append(o[..., s0:s0 + n])
    pad = jnp.zeros(o.shape[:-1] + (LANE - 2 * r,), o.dtype)
    return jnp.concatenate(parts[0], axis=-1), jnp.concatenate(parts[1] + [pad], axis=-1)


def _unshard_w_in(g_win, d, r):
    n_sh, _, ws = g_win.shape
    nbw = 2 * d + LANE
    pieces = _regroup_pieces(d, r, ws)
    tr = math.gcd(d, 256)

    def body(g_ref, a_ref, b_ref):
        dsts = (a_ref, b_ref)
        for j, s0, n, dst, d0 in pieces:
            dsts[dst][:, pl.ds(d0, n)] = g_ref[j, :, pl.ds(s0, n)]
        b_ref[:, pl.ds(2 * d + 2 * r, LANE - 2 * r)] = jnp.zeros((tr, LANE - 2 * r), b_ref.dtype)

    return pl.pallas_call(
        body, name="unshard_w_in", grid=(d // tr,),
        in_specs=[pl.BlockSpec((n_sh, tr, ws), lambda i: (0, i, 0))],
        out_specs=(pl.BlockSpec((tr, 6 * d), lambda i: (i, 0)), pl.BlockSpec((tr, nbw), lambda i: (i, 0))),
        out_shape=(jax.ShapeDtypeStruct((d, 6 * d), g_win.dtype), jax.ShapeDtypeStruct((d, nbw), g_win.dtype)),
        compiler_params=_params())(g_win)


def _reshard_w_in(dw_a1, dw_a2, dw_b, d, r):
    ws = (8 * d + 2 * r) // N_DEV
    pieces = _regroup_pieces(d, r, ws)
    tr = math.gcd(d, 256)

    def body(a1_ref, a2_ref, b_ref, o_ref):
        for j, s0, n, dst, d0 in pieces:
            if dst == 1:
                src = b_ref[:, pl.ds(d0, n)]
            elif d0 < 2 * d:
                src = a1_ref[:, pl.ds(d0, n)]
            else:
                src = a2_ref[:, pl.ds(d0 - 2 * d, n)]
            o_ref[j, :, pl.ds(s0, n)] = src

    row = lambda w: pl.BlockSpec((tr, w), lambda i: (i, 0))
    return pl.pallas_call(
        body, name="reshard_w_in", grid=(d // tr,),
        in_specs=[row(2 * d), row(4 * d), row(2 * d + LANE)],
        out_specs=pl.BlockSpec((N_DEV, tr, ws), lambda i: (0, i, 0)),
        out_shape=jax.ShapeDtypeStruct((N_DEV, d, ws), dw_b.dtype),
        compiler_params=_params())(dw_a1, dw_a2, dw_b)


_SMALL = ("c_ctx", "ada_b", "norm_g", "b_in", "conv_b", "conv_ln_g", "conv_ln_b", "decay_bias_fwd",
          "decay_bias_bwd", "gla_norm_g", "final_norm_g")


def _small_layout(d, r):
    sizes = dict(c_ctx=d, ada_b=3 * d, norm_g=d, b_in=8 * d + 2 * r, conv_b=d, conv_ln_g=d, conv_ln_b=d,
                 decay_bias_fwd=d // 2, decay_bias_bwd=d // 2, gla_norm_g=d // HEADS, final_norm_g=d, loss=1)
    table, off = {}, 0
    for name in _SMALL + ("loss",):
        table[name] = (off, sizes[name])
        off += -(-sizes[name] // LANE) * LANE
    return table, off


def _pack_small(g, nb, d, r):
    table, width = _small_layout(d, r)
    hv = d // HEADS
    pieces = _regroup_pieces(d, r, 8 * d + 2 * r)
    names = ("small", "cv", "ada_b", "norm_g", "b_a1", "b_a2", "b_b", "conv_b", "bias2")

    def body(sm, cv, ab, ng, ba1, ba2, bb, cvb, b2, o_ref):
        o_ref[...] = jnp.zeros_like(o_ref)

        def put(name, val):
            off, n = table[name]
            o_ref[:, pl.ds(off, n)] = val

        put("c_ctx", cv[nb:nb + 1, :])
        put("ada_b", ab[...])
        put("norm_g", ng[...])
        off_b = table["b_in"][0]
        for _, s0, n, dst, d0 in pieces:
            if dst == 1:
                src = bb[:, pl.ds(d0, n)]
            elif d0 < 2 * d:
                src = ba1[:, pl.ds(d0, n)]
            else:
                src = ba2[:, pl.ds(d0 - 2 * d, n)]
            o_ref[:, pl.ds(off_b + s0, n)] = src
        put("conv_b", cvb[...])
        put("conv_ln_g", sm[1:2, :])
        put("conv_ln_b", sm[2:3, :])
        put("decay_bias_fwd", b2[:, 0:d // 2])
        put("decay_bias_bwd", b2[:, d // 2:d])
        gn = sm[3:4, 0:hv]
        for h in range(1, HEADS):
            gn = gn + sm[3:4, h * hv:(h + 1) * hv]
        put("gla_norm_g", gn)
        put("final_norm_g", sm[0:1, :])
        put("loss", sm[4:5, 0:1])

    return pl.pallas_call(body, name="pack_small", out_shape=jax.ShapeDtypeStruct((1, width), F32),
                          compiler_params=_params())(*[g[k] for k in names])


def _small_adam(parts, ws, ms, vs, d, r):
    table, width = _small_layout(d, r)
    n_parts = parts.shape[0]
    k = len(_SMALL)
    bc1 = 1.0 - ADAM_B1 ** ADAM_STEP
    bc2 = 1.0 - ADAM_B2 ** ADAM_STEP

    def body(p_ref, *refs):
        w_refs, m_refs, v_refs = refs[0:k], refs[k:2 * k], refs[2 * k:3 * k]
        outs = refs[3 * k:]
        tot = p_ref[0]
        for i in range(1, n_parts):
            tot = tot + p_ref[i]
        for i, name in enumerate(_SMALL):
            off, n = table[name]
            g = tot[:, off:off + n]
            mn = ADAM_B1 * m_refs[i][...] + (1.0 - ADAM_B1) * g
            vn = ADAM_B2 * v_refs[i][...] + (1.0 - ADAM_B2) * (g * g)
            outs[i][...] = g
            outs[k + i][...] = -ADAM_LR * ((mn / bc1) / (jnp.sqrt(vn / bc2) + ADAM_EPS) + ADAM_WD * w_refs[i][...])
            outs[2 * k + i][...] = mn
            outs[3 * k + i][...] = vn
        off, _ = table["loss"]
        outs[4 * k][...] = tot[:, off:off + 1]

    shapes = [jax.ShapeDtypeStruct(w.shape, F32) for w in ws]
    res = pl.pallas_call(body, name="small_adam", out_shape=tuple(shapes * 4 + [jax.ShapeDtypeStruct((1, 1), F32)]),
                         compiler_params=_params())(parts, *ws, *ms, *vs)
    return res[0:k], res[k:2 * k], res[2 * k:3 * k], res[3 * k:4 * k], res[4 * k]


def _mesh_pos():
    return lax.axis_index("x"), lax.axis_index("y"), lax.axis_index("c")


_ANY = pl.BlockSpec(memory_space=pl.ANY)


def _all_gather(arrs):
    n = len(arrs)

    def body(*refs):
        ins, outs = refs[:n], refs[n:2 * n]
        send_sems, recv_sems, local_sems = refs[2 * n:]
        x, y, c = _mesh_pos()
        me, sibling = (x, y, c), (x, y, 1 - c)
        chips = [(1 - x, y), (x, 1 - y), (1 - x, 1 - y)]

        def slot(a, pos):
            return outs[a].at[4 * pos[0] + 2 * pos[1] + pos[2]]

        def copy(a, k, block, to, src=None):
            return pltpu.make_async_remote_copy(
                src_ref=slot(a, block) if src is None else src, dst_ref=slot(a, block),
                send_sem=send_sems.at[7 * a + k], recv_sem=recv_sems.at[7 * a + k],
                device_id=to, device_id_type=MESH)

        mine = [pltpu.make_async_copy(ins[a], slot(a, me), local_sems.at[a]) for a in range(n)]
        for cp in mine:
            cp.start()
        first = []
        for a in range(n):
            first.append(copy(a, 0, me, sibling, src=ins[a]))
            first += [copy(a, 1 + j, me, (*chip, c), src=ins[a]) for j, chip in enumerate(chips)]
        for cp in first:
            cp.start()
        passed = []
        for j, chip in enumerate(chips):
            for a in range(n):
                copy(a, 1 + j, (*chip, c), me).wait_recv()
                fwd = copy(a, 4 + j, (*chip, c), sibling)
                fwd.start()
                passed.append(fwd)
        for a in range(n):
            copy(a, 0, sibling, me).wait_recv()
            for j, chip in enumerate(chips):
                copy(a, 4 + j, (*chip, 1 - c), me).wait_recv()
        for cp in first + passed:
            cp.wait_send()
        for cp in mine:
            cp.wait()

    return pl.pallas_call(
        body, name="all_gather",
        out_shape=tuple(jax.ShapeDtypeStruct((N_DEV,) + a.shape, a.dtype) for a in arrs),
        in_specs=[_ANY] * n, out_specs=tuple([_ANY] * n),
        scratch_shapes=[pltpu.SemaphoreType.DMA((7 * n,)), pltpu.SemaphoreType.DMA((7 * n,)),
                        pltpu.SemaphoreType.DMA((n,))],
    )(*arrs)


def _exchange_sibling(arrs):
    n = len(arrs)

    def body(*refs):
        ins, outs = refs[:n], refs[n:2 * n]
        send_sems, recv_sems = refs[2 * n:]
        x, y, c = _mesh_pos()
        copies = [pltpu.make_async_remote_copy(
            src_ref=ins[a].at[2 * k + (1 - c)], dst_ref=outs[a].at[k],
            send_sem=send_sems.at[4 * a + k], recv_sem=recv_sems.at[4 * a + k],
            device_id=(x, y, 1 - c), device_id_type=MESH) for a in range(n) for k in range(4)]
        for cp in copies:
            cp.start()
        for cp in copies:
            cp.wait_recv()
        for cp in copies:
            cp.wait_send()

    return pl.pallas_call(
        body, name="grad_exchange_sibling",
        out_shape=tuple(jax.ShapeDtypeStruct((4,) + a.shape[1:], a.dtype) for a in arrs),
        in_specs=[_ANY] * n, out_specs=tuple([_ANY] * n),
        scratch_shapes=[pltpu.SemaphoreType.DMA((4 * n,)), pltpu.SemaphoreType.DMA((4 * n,))],
    )(*arrs)


def _pair_sum(name, mine, theirs):
    _, r, cdim = mine.shape
    tr = r if (r % 8 or r <= 256) else math.gcd(r, 256)

    def body(m_ref, t_ref, o_ref):
        c = lax.axis_index("c")
        own = jnp.where(c == 0, m_ref[:, 0].astype(F32), m_ref[:, 1].astype(F32))
        o_ref[...] = (own + t_ref[...].astype(F32)).astype(o_ref.dtype)

    return pl.pallas_call(
        body, name=name, grid=(r // tr,),
        in_specs=[pl.BlockSpec((4, 2, tr, cdim), lambda i: (0, 0, i, 0)),
                  pl.BlockSpec((4, tr, cdim), lambda i: (0, i, 0))],
        out_specs=pl.BlockSpec((4, tr, cdim), lambda i: (0, i, 0)),
        out_shape=jax.ShapeDtypeStruct((4, r, cdim), mine.dtype),
        compiler_params=_params())(mine.reshape(4, 2, r, cdim), theirs)


def _exchange_chips(arrs):
    n = len(arrs)

    def body(*refs):
        ins, outs = refs[:n], refs[n:2 * n]
        send_sems, recv_sems, local_sems = refs[2 * n:]
        x, y, c = _mesh_pos()
        my_chip = 2 * x + y
        mine = [pltpu.make_async_copy(ins[a].at[my_chip], outs[a].at[my_chip], local_sems.at[a]) for a in range(n)]
        for cp in mine:
            cp.start()
        copies = []
        for rel in range(1, 4):
            px = 1 - x if rel & 2 else x
            py = 1 - y if rel & 1 else y
            for a in range(n):
                copies.append(pltpu.make_async_remote_copy(
                    src_ref=ins[a].at[2 * px + py], dst_ref=outs[a].at[my_chip],
                    send_sem=send_sems.at[3 * a + rel - 1], recv_sem=recv_sems.at[3 * a + rel - 1],
                    device_id=(px, py, c), device_id_type=MESH))
        for cp in copies:
            cp.start()
        for cp in copies:
            cp.wait_recv()
        for cp in copies:
            cp.wait_send()
        for cp in mine:
            cp.wait()

    return pl.pallas_call(
        body, name="grad_exchange_chips",
        out_shape=tuple(jax.ShapeDtypeStruct(a.shape, a.dtype) for a in arrs),
        in_specs=[_ANY] * n, out_specs=tuple([_ANY] * n),
        scratch_shapes=[pltpu.SemaphoreType.DMA((3 * n,)), pltpu.SemaphoreType.DMA((3 * n,)),
                        pltpu.SemaphoreType.DMA((n,))],
    )(*arrs)


def _sum_adam(name, parts, w, m, v):
    r, cdim = w.shape
    n_parts = parts.shape[0]
    tr = r if (r % 8 or r <= 256) else math.gcd(r, 256)
    bc1 = 1.0 - ADAM_B1 ** ADAM_STEP
    bc2 = 1.0 - ADAM_B2 ** ADAM_STEP

    def body(p_ref, w_ref, m_ref, v_ref, g_ref, d_ref, nm_ref, nv_ref):
        g = p_ref[0].astype(F32)
        for k in range(1, n_parts):
            g = g + p_ref[k].astype(F32)
        mn = ADAM_B1 * m_ref[...] + (1.0 - ADAM_B1) * g
        vn = ADAM_B2 * v_ref[...] + (1.0 - ADAM_B2) * (g * g)
        g_ref[...] = g
        nm_ref[...] = mn
        nv_ref[...] = vn
        d_ref[...] = -ADAM_LR * ((mn / bc1) / (jnp.sqrt(vn / bc2) + ADAM_EPS) + ADAM_WD * w_ref[...])

    blk = pl.BlockSpec((tr, cdim), lambda i: (i, 0))
    o = jax.ShapeDtypeStruct((r, cdim), F32)
    return pl.pallas_call(
        body, name=name, grid=(r // tr,),
        in_specs=[pl.BlockSpec((n_parts, tr, cdim), lambda i: (0, i, 0)), blk, blk, blk],
        out_specs=(blk, blk, blk, blk), out_shape=(o, o, o, o),
        compiler_params=_params())(parts, w, m, v)


_WEIGHTS = ("c_ctx", "ada_w", "ada_b", "norm_g", "w_in", "b_in", "conv_w", "conv_b", "conv_ln_g", "conv_ln_b",
            "conv_proj", "decay_up_fwd", "decay_bias_fwd", "decay_up_bwd", "decay_bias_bwd", "gla_norm_g",
            "gla_proj", "w_out", "final_norm_g")


def _as2d(a):
    if a.ndim == 1:
        return a.reshape(1, -1)
    return a.reshape(-1, a.shape[-1])


def kernel(x, c, ctx, c_ctx, ada_w, ada_b, norm_g, w_in, b_in, conv_w, conv_b, conv_ln_g, conv_ln_b, conv_proj, decay_up_fwd, decay_bias_fwd, decay_up_bwd, decay_bias_bwd, gla_norm_g, gla_proj, w_out, final_norm_g, loss_target, m_c_ctx, m_ada_w, m_ada_b, m_norm_g, m_w_in, m_b_in, m_conv_w, m_conv_b, m_conv_ln_g, m_conv_ln_b, m_conv_proj, m_decay_up_fwd, m_decay_bias_fwd, m_decay_up_bwd, m_decay_bias_bwd, m_gla_norm_g, m_gla_proj, m_w_out, m_final_norm_g, v_c_ctx, v_ada_w, v_ada_b, v_norm_g, v_w_in, v_b_in, v_conv_w, v_conv_b, v_conv_ln_g, v_conv_ln_b, v_conv_proj, v_decay_up_fwd, v_decay_bias_fwd, v_decay_up_bwd, v_decay_bias_bwd, v_gla_norm_g, v_gla_proj, v_w_out, v_final_norm_g):
    env = dict(locals())
    wts = {k: env[k] for k in _WEIGHTS}
    d = x.shape[-1]
    r = decay_up_fwd.shape[1]
    dk_ = d // 2
    n_in = w_in.shape[-1] * N_DEV

    ds, dks = d // N_DEV, dk_ // N_DEV
    g_win, g_ada, g_cp, g_gp, g_wo, conv_w8, g_up = _all_gather(
        [w_in[0].astype(BF16), ada_w[0].astype(BF16), conv_proj[0].astype(BF16), gla_proj[0].astype(BF16),
         w_out[0].astype(BF16), conv_w[0], jnp.concatenate([decay_up_fwd[0], decay_up_bwd[0]], axis=1)])

    w_a, w_b = _unshard_w_in(g_win, d, r)
    up_f = g_up[:, :, 0:dks].transpose(1, 0, 2).reshape(r, dk_)
    up_b = g_up[:, :, dks:].transpose(1, 0, 2).reshape(r, dk_)
    up2 = jnp.zeros((LANE, 2 * dk_), F32).at[0:r, 0:dk_].set(up_f).at[r:2 * r, dk_:].set(up_b)
    bias2 = jnp.concatenate([decay_bias_fwd, decay_bias_bwd], axis=1)
    b_a, b_b = _regroup(b_in, d, r)

    g = _local_step(x, c, ctx, loss_target, c_ctx, g_ada, ada_b, norm_g[0:1], w_a, b_a, w_b, b_b,
                    conv_w8, conv_b, conv_ln_g, conv_ln_b, g_cp.reshape(d, d), up2, bias2, gla_norm_g,
                    g_gp.reshape(d, d), g_wo.reshape(d, d), final_norm_g.reshape(1, d))

    d_up = jnp.concatenate([g["up2"][0:r, 0:dk_].reshape(r, N_DEV, dks).transpose(1, 0, 2),
                            g["up2"][r:2 * r, dk_:].reshape(r, N_DEV, dks).transpose(1, 0, 2)], axis=2)
    mine = [_reshard_w_in(g["w_a1"], g["w_a2"], g["w_b"], d, r), g["ada_w8"], g["conv_proj"].reshape(N_DEV, ds, d),
            g["gla_proj"].reshape(N_DEV, ds, d), g["w_out"].reshape(N_DEV, ds, d), g["conv_w8"], d_up]
    names = ("w_in", "ada_w", "conv_proj", "gla_proj", "w_out", "conv_w", "decay_up")
    theirs = _exchange_sibling(mine)
    chip_sums = [_pair_sum("pair_sum_" + nm, a, b) for nm, a, b in zip(names, mine, theirs)]
    x_win, x_ada, x_cp, x_gp, x_wo, x_cw, x_up = _exchange_chips(chip_sums)

    (packs,) = _all_gather([_pack_small(g, x.shape[0], d, r)])
    row = lambda a: a.reshape(1, -1)
    sg, sd, sm, sv, loss = _small_adam(packs, [row(wts[k]) for k in _SMALL], [row(env["m_" + k]) for k in _SMALL],
                                       [row(env["v_" + k]) for k in _SMALL], d, r)
    out = {}
    for i, k in enumerate(_SMALL):
        for pre, arrs in (("grad_", sg), ("delta_", sd), ("new_m_", sm), ("new_v_", sv)):
            out[pre + k] = arrs[i].reshape(wts[k].shape)
    loss = loss.reshape(())

    def big(name, parts, wname):
        w2 = _as2d(wts[wname])
        res = _sum_adam(name, parts, w2, _as2d(env["m_" + wname]), _as2d(env["v_" + wname]))
        for pre, arr in zip(("grad_", "delta_", "new_m_", "new_v_"), res):
            out[pre + wname] = arr.reshape(wts[wname].shape)

    big("adam_w_in", x_win, "w_in")
    big("adam_ada_w", x_ada, "ada_w")
    big("adam_conv_proj", x_cp, "conv_proj")
    big("adam_gla_proj", x_gp, "gla_proj")
    big("adam_w_out", x_wo, "w_out")
    big("adam_conv_w", x_cw, "conv_w")
    big("adam_up_f", x_up[:, :, 0:dks], "decay_up_fwd")
    big("adam_up_b", x_up[:, :, dks:], "decay_up_bwd")

    return (loss, g["grad_x"], *[out["grad_" + k] for k in _WEIGHTS], *[out["delta_" + k] for k in _WEIGHTS],
            *[out["new_m_" + k] for k in _WEIGHTS], *[out["new_v_" + k] for k in _WEIGHTS])
```

```python
import functools
import math

import jax
import jax.numpy as jnp
from jax import lax
from jax.experimental import pallas as pl
from jax.experimental.pallas import tpu as pltpu

F32 = jnp.float32
BF16 = jnp.bfloat16
MESH = pl.DeviceIdType.MESH

N_DEV = 8
GRID_W = 64
CHUNK = 128
HEADS = 4
EPS = 1e-6
GATE_TAU = 16.0
LANE = 128
ADAM_LR, ADAM_B1, ADAM_B2, ADAM_EPS, ADAM_WD, ADAM_STEP = 0.001, 0.9, 0.999, 1e-08, 0.01, 10
VMEM_LIMIT = 56 * 1024 * 1024
_ANY = pl.BlockSpec(memory_space=pl.ANY)


def _params(**kw):
    return pltpu.CompilerParams(vmem_limit_bytes=VMEM_LIMIT, **kw)


def _tile(n, pref):
    t = (min(pref, n) // LANE) * LANE
    while t >= LANE:
        if n % t == 0:
            return t
        t -= LANE
    return n


def _mm(a, b):
    return jnp.dot(a.astype(BF16), b.astype(BF16), preferred_element_type=F32)


def _mm_nt(a, b):
    return lax.dot_general(a.astype(BF16), b.astype(BF16), (((1,), (1,)), ((), ())), preferred_element_type=F32)


def _mm_tn(a, b):
    return lax.dot_general(a.astype(BF16), b.astype(BF16), (((0,), (0,)), ((), ())), preferred_element_type=F32)


def _mm_tn_hi(a, b):
    return lax.dot_general(a, b, (((0,), (0,)), ((), ())), precision=lax.Precision.HIGHEST, preferred_element_type=F32)


def _sigmoid(x):
    return 0.5 * jnp.tanh(0.5 * x) + 0.5


def _dsilu(x, s):
    return s * (1.0 + x * (1.0 - s))


def _rowsel(table, idx, n):
    out = table[0:1, :]
    for r in range(1, n):
        out = jnp.where(idx == r, table[r:r + 1, :], out)
    return out


def _ada_fwd(cv, ada_w8, ada_b):
    n_sh, _, ws = ada_w8.shape

    def body(cv_ref, w_ref, b_ref, o_ref):
        c = cv_ref[...]
        sv = c * _sigmoid(c)
        for j in range(n_sh):
            cols = pl.ds(j * ws, ws)
            o_ref[:, cols] = _mm(sv, w_ref[j]) + b_ref[:, cols]

    return pl.pallas_call(body, name="ada_fwd", out_shape=jax.ShapeDtypeStruct((cv.shape[0], n_sh * ws), F32),
                          compiler_params=_params())(cv, ada_w8, ada_b)


def _ada_bwd(cv, ada_w8, dmod_ss, small, nb):
    n_sh, d, ws = ada_w8.shape

    def body(cv_ref, w_ref, dm_ref, sm_ref, dw_ref, db_ref, dc_ref):
        c = cv_ref[...]
        s = _sigmoid(c)
        sv = c * s
        dm = jnp.concatenate([dm_ref[:, 0:2 * d], sm_ref[8:16, :]], axis=1)
        db_ref[...] = jnp.sum(dm, axis=0, keepdims=True)
        dsv = None
        for j in range(n_sh):
            dmj = dm[:, j * ws:(j + 1) * ws]
            dw_ref[j] = _mm_tn_hi(sv, dmj).astype(dw_ref.dtype)
            part = _mm_nt(dmj, w_ref[j])
            dsv = part if dsv is None else dsv + part
        dc_ref[...] = dsv * _dsilu(c, s)

    return pl.pallas_call(
        body, name="ada_bwd",
        out_shape=(jax.ShapeDtypeStruct((n_sh, d, ws), BF16), jax.ShapeDtypeStruct((1, n_sh * ws), F32),
                   jax.ShapeDtypeStruct(cv.shape, F32)),
        compiler_params=_params())(cv, ada_w8, dmod_ss, small)


class _Tiles:
    def __init__(self, nb, s_len, c_len, tm, big):
        self.nb, self.tm, self.big = nb, tm, big
        self.lat, self.ctx = s_len // tm, c_len // tm
        self.pad = -(self.lat + self.ctx) % big
        self.per_ex = self.lat + self.ctx + self.pad
        self.n_all, self.n_lat = nb * self.per_ex, nb * self.lat
        self.rows_per_ex = self.per_ex * tm

    def is_lat(self, i):
        return i % self.per_ex < self.lat

    def is_pad(self, i):
        return i % self.per_ex >= self.lat + self.ctx

    def lat_of_all(self, i):
        return (i // self.per_ex) * self.lat + jnp.minimum(i % self.per_ex, self.lat - 1)

    def ctx_of_all(self, i):
        return (i // self.per_ex) * self.ctx + jnp.clip(i % self.per_ex - self.lat, 0, self.ctx - 1)

    def big_all_of_lat(self, t):
        lat_big = self.lat // self.big
        return (t // lat_big) * (self.per_ex // self.big) + t % lat_big


def _norm_fwd(x2, ctx2, mod, norm_g, tiles):
    tl, d = x2.shape
    tc = ctx2.shape[0]
    nb, tm = tiles.nb, tiles.tm

    def body(x_ref, c_ref, mod_ref, g_ref, u_ref):
        i = pl.program_id(0)
        lat = tiles.is_lat(i)
        xv = jnp.where(lat, x_ref[...], c_ref[...])
        row = jnp.where(lat, i // tiles.per_ex, nb)
        m = _rowsel(mod_ref[...], row, nb + 1)
        shift, scale = m[:, 0:d], m[:, d:2 * d]
        rstd = lax.rsqrt(jnp.mean(xv * xv, axis=-1, keepdims=True) + EPS)
        u = xv * rstd * g_ref[...] * (1.0 + scale) + shift
        u_ref[...] = jnp.where(tiles.is_pad(i), 0.0, u).astype(BF16)

    return pl.pallas_call(
        body, name="norm_fwd", grid=(tiles.n_all,),
        in_specs=[pl.BlockSpec((tm, d), lambda i: (tiles.lat_of_all(i), 0)),
                  pl.BlockSpec((tm, d), lambda i: (tiles.ctx_of_all(i), 0)),
                  pl.BlockSpec(mod.shape, lambda i: (0, 0)),
                  pl.BlockSpec((1, d), lambda i: (0, 0))],
        out_specs=pl.BlockSpec((tm, d), lambda i: (i, 0)),
        out_shape=jax.ShapeDtypeStruct((tiles.n_all * tm, d), BF16),
        compiler_params=_params())(x2, ctx2, mod, norm_g)


def _norm_bwd(x2, ctx2, mod, norm_g, du_lat, du_b, gx1, tiles):
    tl, d = x2.shape
    nb, tm = tiles.nb, tiles.tm
    nrow = mod.shape[0]
    n_lat_in = len(du_lat)

    def body(x_ref, c_ref, mod_ref, g_ref, *refs):
        dl_refs = refs[:n_lat_in]
        d3_ref, gx_ref, gxo_ref, dmod_ref, dg_ref = refs[n_lat_in:]
        i = pl.program_id(0)

        @pl.when(i == 0)
        def _():
            dmod_ref[...] = jnp.zeros_like(dmod_ref)
            dg_ref[...] = jnp.zeros_like(dg_ref)

        lat = tiles.is_lat(i)
        xv = jnp.where(lat, x_ref[...], c_ref[...])
        row = jnp.where(lat, i // tiles.per_ex, nb)
        m = _rowsel(mod_ref[...], row, nb + 1)
        scale = m[:, d:2 * d]
        g = g_ref[...]
        dl = dl_refs[0][...]
        for ref in dl_refs[1:]:
            dl = dl + ref[...]
        du = jnp.where(tiles.is_pad(i), 0.0, d3_ref[...] + jnp.where(lat, dl, 0.0))
        rstd = lax.rsqrt(jnp.mean(xv * xv, axis=-1, keepdims=True) + EPS)
        xh = xv * rstd
        dshift = jnp.sum(du, axis=0, keepdims=True)
        dscale = jnp.sum(du * xh * g, axis=0, keepdims=True)
        dxn = du * (1.0 + scale)
        dg_ref[...] += jnp.sum(dxn * xh, axis=0, keepdims=True)
        dxh = dxn * g
        dx = rstd * (dxh - xh * jnp.mean(dxh * xh, axis=-1, keepdims=True))

        @pl.when(lat)
        def _():
            gxo_ref[...] = dx + gx_ref[...]

        for r in range(nb + 1):
            dmod_ref[r:r + 1, 0:d] += jnp.where(row == r, dshift, 0.0)
            dmod_ref[r:r + 1, d:2 * d] += jnp.where(row == r, dscale, 0.0)

    lat_map = lambda i: (tiles.lat_of_all(i), 0)
    lat_spec = pl.BlockSpec((tm, d), lat_map)
    return pl.pallas_call(
        body, name="norm_bwd", grid=(tiles.n_all,),
        in_specs=[lat_spec,
                  pl.BlockSpec((tm, d), lambda i: (tiles.ctx_of_all(i), 0)),
                  pl.BlockSpec(mod.shape, lambda i: (0, 0)),
                  pl.BlockSpec((1, d), lambda i: (0, 0))]
                 + [lat_spec] * n_lat_in
                 + [pl.BlockSpec((tm, d), lambda i: (i, 0)), lat_spec],
        out_specs=(lat_spec,
                   pl.BlockSpec((nrow, 3 * d), lambda i: (0, 0)),
                   pl.BlockSpec((1, d), lambda i: (0, 0))),
        out_shape=(jax.ShapeDtypeStruct((tl, d), F32), jax.ShapeDtypeStruct((nrow, 3 * d), F32),
                   jax.ShapeDtypeStruct((1, d), F32)),
        compiler_params=_params())(x2, ctx2, mod, norm_g, *du_lat, du_b, gx1)


def _matmul_bias(name, u, w, b, rows, tm, tn, u_tile=lambda i: i):
    d, n = w.shape

    def body(u_ref, w_ref, b_ref, o_ref):
        o_ref[...] = jnp.dot(u_ref[...], w_ref[...], preferred_element_type=F32) + b_ref[...]

    return pl.pallas_call(
        body, name=name, grid=(n // tn, rows // tm),
        in_specs=[pl.BlockSpec((tm, d), lambda j, i: (u_tile(i), 0)),
                  pl.BlockSpec((d, tn), lambda j, i: (0, j)),
                  pl.BlockSpec((1, tn), lambda j, i: (0, j))],
        out_specs=pl.BlockSpec((tm, tn), lambda j, i: (i, j)),
        out_shape=jax.ShapeDtypeStruct((rows, n), F32),
        compiler_params=_params())(u, w, b)


def _matmul_nt(name, a, w, koff, tm, tk, after=()):
    r, kc = a.shape
    d = w.shape[0]
    nk = kc // tk

    def body(a_ref, w_ref, *rest):
        o_ref = rest[len(after)]
        k = pl.program_id(1)
        p = lax.dot_general(a_ref[...], w_ref[...], (((1,), (1,)), ((), ())), preferred_element_type=F32)

        @pl.when(k == 0)
        def _():
            o_ref[...] = p

        @pl.when(k > 0)
        def _():
            o_ref[...] += p

    return pl.pallas_call(
        body, name=name, grid=(r // tm, nk),
        in_specs=[pl.BlockSpec((tm, tk), lambda i, k: (i, k)),
                  pl.BlockSpec((d, tk), lambda i, k: (0, koff + k))] + [_ANY] * len(after),
        out_specs=pl.BlockSpec((tm, d), lambda i, k: (i, 0)),
        out_shape=jax.ShapeDtypeStruct((r, d), F32),
        compiler_params=_params())(a, w, *after)


def _matmul_tn(name, a, b, rows, tk, tn):
    m = a.shape[1]
    n = b.shape[1]
    nk = rows // tk

    def body(a_ref, b_ref, o_ref, s_ref, acc_ref):
        k = pl.program_id(1)
        bv = b_ref[...]
        p = lax.dot_general(a_ref[...], bv, (((0,), (0,)), ((), ())), preferred_element_type=F32)
        cs = jnp.sum(bv.astype(F32), axis=0, keepdims=True)

        @pl.when(k == 0)
        def _():
            acc_ref[...] = p
            s_ref[...] = cs

        @pl.when(k > 0)
        def _():
            acc_ref[...] += p
            s_ref[...] += cs

        @pl.when(k == nk - 1)
        def _():
            o_ref[...] = acc_ref[...].astype(o_ref.dtype)

    return pl.pallas_call(
        body, name=name, grid=(n // tn, nk),
        in_specs=[pl.BlockSpec((tk, m), lambda j, k: (k, 0)),
                  pl.BlockSpec((tk, tn), lambda j, k: (k, j))],
        out_specs=(pl.BlockSpec((m, tn), lambda j, k: (0, j)), pl.BlockSpec((1, tn), lambda j, k: (0, j))),
        out_shape=(jax.ShapeDtypeStruct((m, n), BF16), jax.ShapeDtypeStruct((1, n), F32)),
        scratch_shapes=[pltpu.VMEM((m, tn), F32)],
        compiler_params=_params())(a, b)


def _matmul_tn_whole(name, a3, b3, rows, tn):
    nb, _, m = a3.shape
    n = b3.shape[2]

    def body(a_ref, b_ref, o_ref, s_ref):
        p, cs = None, None
        for e in range(nb):
            bv = b_ref[e]
            pe = lax.dot_general(a_ref[e], bv, (((0,), (0,)), ((), ())), preferred_element_type=F32)
            ce = jnp.sum(bv.astype(F32), axis=0, keepdims=True)
            p, cs = (pe, ce) if p is None else (p + pe, cs + ce)
        o_ref[...] = p.astype(o_ref.dtype)
        s_ref[...] = cs

    return pl.pallas_call(
        body, name=name, grid=(n // tn,),
        in_specs=[pl.BlockSpec((nb, rows, m), lambda j: (0, 0, 0)),
                  pl.BlockSpec((nb, rows, tn), lambda j: (0, 0, j))],
        out_specs=(pl.BlockSpec((m, tn), lambda j: (0, j)), pl.BlockSpec((1, tn), lambda j: (0, j))),
        out_shape=(jax.ShapeDtypeStruct((m, n), BF16), jax.ShapeDtypeStruct((1, n), F32)),
        compiler_params=_params())(a3, b3)


def _conv_window(pad_ref, r, shift, ktaps, width, horizontal):
    if horizontal:
        return pad_ref[r, pl.ds(16 + shift, width), :]
    return pad_ref[r + ktaps // 2 + shift]


def _conv_row(pad_ref, w, r, ktaps, width, horizontal, flip):
    half = ktaps // 2
    acc = None
    for t in range(ktaps):
        win = _conv_window(pad_ref, r, (half - t) if flip else (t - half), ktaps, width, horizontal)
        term = win * w[t:t + 1, :]
        acc = term if acc is None else acc + term
    return acc


def _fill_padded(ref, val, rows, width, ktaps, horizontal):
    half_k = ktaps // 2
    cb = val.shape[-1]
    if horizontal:
        ref[:, 0:16, :] = jnp.zeros((rows, 16, cb), F32)
        ref[:, 16 + width:32 + width, :] = jnp.zeros((rows, 16, cb), F32)
        ref[:, 16:16 + width, :] = val
    else:
        ref[0:half_k, :, :] = jnp.zeros((half_k, width, cb), F32)
        ref[half_k + rows:2 * half_k + rows, :, :] = jnp.zeros((half_k, width, cb), F32)
        ref[half_k:half_k + rows, :, :] = val


def _conv_fwd(pa, conv_w8, conv_b, nb, s):
    nblk, ktaps, cb = conv_w8.shape
    d = nblk * cb
    rows, width = s // GRID_W, GRID_W
    half_k = ktaps // 2
    nh = nblk // 2

    def body(glu_ref, w_ref, b_ref, o_ref, ph_ref, pv_ref):
        j = pl.program_id(1)
        a0 = (glu_ref[:, 0:cb] * _sigmoid(glu_ref[:, cb:2 * cb])).reshape(rows, width, cb)
        w = w_ref[...]

        bias = b_ref[...]

        def run(pad_ref, horizontal):
            _fill_padded(pad_ref, a0, rows, width, ktaps, horizontal)

            def row(r, carry):
                at = pl.ds(pl.multiple_of(r * width, width), width)
                o_ref[at, :] = _conv_row(pad_ref, w, r, ktaps, width, horizontal, False) + bias
                return carry

            lax.fori_loop(0, rows, row, 0)

        @pl.when(j < nh)
        def _():
            run(ph_ref, True)

        @pl.when(j >= nh)
        def _():
            run(pv_ref, False)

    return pl.pallas_call(
        body, name="conv_fwd", grid=(nb, nblk),
        in_specs=[pl.BlockSpec((s, 2 * cb), lambda b, j: (b, j)),
                  pl.BlockSpec((None, ktaps, cb), lambda b, j: (j, 0, 0)),
                  pl.BlockSpec((1, cb), lambda b, j: (0, j))],
        out_specs=pl.BlockSpec((s, cb), lambda b, j: (b, j)),
        out_shape=jax.ShapeDtypeStruct((nb * s, d), F32),
        scratch_shapes=[pltpu.VMEM((rows, width + 32, cb), F32), pltpu.VMEM((rows + 2 * half_k, width, cb), F32)],
        compiler_params=_params())(pa, conv_w8, conv_b)


def _conv_bwd(pa, da1, conv_w8, nb, s):
    nblk, ktaps, cb = conv_w8.shape
    d = nblk * cb
    rows, width = s // GRID_W, GRID_W
    half_k = ktaps // 2
    nh = nblk // 2

    def body(glu_ref, da_ref, w_ref, dp_ref, dw_ref, db_ref, pha_ref, phd_ref, pva_ref, pvd_ref):
        j = pl.program_id(0)
        b = pl.program_id(1)
        a0 = (glu_ref[:, 0:cb] * _sigmoid(glu_ref[:, cb:2 * cb])).reshape(rows, width, cb)
        da1v = da_ref[...]
        d3 = da1v.reshape(rows, width, cb)
        w = w_ref[...]

        @pl.when(b == 0)
        def _():
            dw_ref[...] = jnp.zeros_like(dw_ref)
            db_ref[...] = jnp.zeros_like(db_ref)

        db_ref[...] += jnp.sum(da1v, axis=0, keepdims=True)

        def run(pa_ref, pd_ref, horizontal):
            _fill_padded(pa_ref, a0, rows, width, ktaps, horizontal)
            _fill_padded(pd_ref, d3, rows, width, ktaps, horizontal)

            def row(r, accs):
                at = pl.ds(pl.multiple_of(r * width, width), width)
                da0 = _conv_row(pd_ref, w, r, ktaps, width, horizontal, True)
                gv = glu_ref[at, 0:cb]
                sg = _sigmoid(glu_ref[at, cb:2 * cb])
                dp_ref[at, 0:cb] = (da0 * sg).astype(BF16)
                dp_ref[at, cb:2 * cb] = (da0 * gv * sg * (1.0 - sg)).astype(BF16)
                d_row = da_ref[at, :]
                out = []
                for t in range(ktaps):
                    prod = _conv_window(pa_ref, r, t - half_k, ktaps, width, horizontal) * d_row
                    out.append(accs[t] + jnp.sum(prod.reshape(width // 8, 8, cb), axis=0))
                return tuple(out)

            accs = lax.fori_loop(0, rows, row, tuple(jnp.zeros((8, cb), F32) for _ in range(ktaps)))
            for t in range(ktaps):
                dw_ref[t:t + 1, :] += jnp.sum(accs[t], axis=0, keepdims=True)

        @pl.when(j < nh)
        def _():
            run(pha_ref, phd_ref, True)

        @pl.when(j >= nh)
        def _():
            run(pva_ref, pvd_ref, False)

    return pl.pallas_call(
        body, name="conv_bwd", grid=(nblk, nb),
        in_specs=[pl.BlockSpec((s, 2 * cb), lambda j, b: (b, j)),
                  pl.BlockSpec((s, cb), lambda j, b: (b, j)),
                  pl.BlockSpec((None, ktaps, cb), lambda j, b: (j, 0, 0))],
        out_specs=(pl.BlockSpec((s, 2 * cb), lambda j, b: (b, j)),
                   pl.BlockSpec((None, ktaps, cb), lambda j, b: (j, 0, 0)),
                   pl.BlockSpec((1, cb), lambda j, b: (0, j))),
        out_shape=(jax.ShapeDtypeStruct((nb * s, 2 * d), BF16),
                   jax.ShapeDtypeStruct((nblk, ktaps, cb), F32), jax.ShapeDtypeStruct((1, d), F32)),
        scratch_shapes=[pltpu.VMEM((rows, width + 32, cb), F32), pltpu.VMEM((rows, width + 32, cb), F32),
                        pltpu.VMEM((rows + 2 * half_k, width, cb), F32),
                        pltpu.VMEM((rows + 2 * half_k, width, cb), F32)],
        compiler_params=_params())(pa, da1, conv_w8)


def _log_sigmoid(x):
    return jnp.minimum(x, 0.0) - jnp.log(1.0 + jnp.exp(-jnp.abs(x)))


def _decay_fwd(pb, up2, bias2, tm, lr_blk):
    t_all = pb.shape[0]
    n2 = up2.shape[1]

    def body(lr_ref, up_ref, b_ref, g_ref):
        logits = _mm(lr_ref[...], up_ref[...]) + b_ref[...]
        g_ref[...] = _log_sigmoid(logits) * (1.0 / GATE_TAU)

    return pl.pallas_call(
        body, name="decay_fwd", grid=(t_all // tm,),
        in_specs=[pl.BlockSpec((tm, LANE), lambda i: (i, lr_blk)),
                  pl.BlockSpec(up2.shape, lambda i: (0, 0)),
                  pl.BlockSpec((1, n2), lambda i: (0, 0))],
        out_specs=pl.BlockSpec((tm, n2), lambda i: (i, 0)),
        out_shape=jax.ShapeDtypeStruct((t_all, n2), F32),
        compiler_params=_params())(pb, up2, bias2)


def _decay_bwd(pb, up2, bias2, grads_f, grads_b, tiles, lr_blk, dk_, dv_):
    t_all = pb.shape[0]
    tm = tiles.tm
    n2 = up2.shape[1]
    nbw = 2 * dk_ + dv_ + LANE

    def body(lr_ref, up_ref, b_ref, dqf, dkf, dvf, dgf, dqb, dkb, dvb, dgb, dp_ref, dup_ref, dbias_ref):
        i = pl.program_id(0)
        pad = tiles.is_pad(i)
        live = lambda v: jnp.where(pad, 0.0, v)

        @pl.when(i == 0)
        def _():
            dup_ref[...] = jnp.zeros_like(dup_ref)
            dbias_ref[...] = jnp.zeros_like(dbias_ref)

        lr = lr_ref[...]
        up = up_ref[...]
        logits = _mm(lr, up) + b_ref[...]
        dg = live(jnp.concatenate([dgf[...], dgb[...]], axis=1))
        dlog = dg * (1.0 / GATE_TAU) * _sigmoid(-logits)
        dup_ref[...] += _mm_tn(lr, dlog)
        dbias_ref[...] += jnp.sum(dlog, axis=0, keepdims=True)
        dp_ref[:, 0:dk_] = live(dqf[...] + dqb[...]).astype(BF16)
        dp_ref[:, dk_:2 * dk_] = live(dkf[...] + dkb[...]).astype(BF16)
        dp_ref[:, 2 * dk_:2 * dk_ + dv_] = live(dvf[...] + dvb[...]).astype(BF16)
        dp_ref[:, 2 * dk_ + dv_:nbw] = _mm_nt(dlog, up).astype(BF16)

    row = lambda w: pl.BlockSpec((tm, w), lambda i: (i, 0))
    return pl.pallas_call(
        body, name="decay_bwd", grid=(t_all // tm,),
        in_specs=[pl.BlockSpec((tm, LANE), lambda i: (i, lr_blk)),
                  pl.BlockSpec(up2.shape, lambda i: (0, 0)),
                  pl.BlockSpec((1, n2), lambda i: (0, 0)),
                  row(dk_), row(dk_), row(dv_), row(dk_), row(dk_), row(dk_), row(dv_), row(dk_)],
        out_specs=(row(nbw), pl.BlockSpec(up2.shape, lambda i: (0, 0)), pl.BlockSpec((1, n2), lambda i: (0, 0))),
        out_shape=(jax.ShapeDtypeStruct((t_all, nbw), BF16), jax.ShapeDtypeStruct(up2.shape, F32),
                   jax.ShapeDtypeStruct((1, n2), F32)),
        compiler_params=_params())(pb, up2, bias2, *grads_f, *grads_b)


def _scan_chunk(s, nl, nc, rev):
    if rev:
        return jnp.where(s < nc, nl + (nc - 1 - s), nl - 1 - (s - nc))
    return jnp.where(s < nc, nl + s, s - nc)


def _scan_lat_chunk(s, nl, nc, rev):
    first = nl - 1 if rev else 0
    return jnp.where(s < nc, first, _scan_chunk(s, nl, nc, rev))


def _tri_mm(m_bf, x):
    hi = x.astype(BF16)
    r1 = x - hi.astype(F32)
    mid = r1.astype(BF16)
    lo = (r1 - mid.astype(F32)).astype(BF16)
    dot = lambda p: jnp.dot(m_bf, p, preferred_element_type=F32)
    return dot(hi) + dot(mid) + dot(lo)


def _chunk_masks(c, rev):
    ii = lax.broadcasted_iota(jnp.int32, (c, c), 0)
    jj = lax.broadcasted_iota(jnp.int32, (c, c), 1)
    return ((ii <= jj), (ii >= jj)) if rev else ((ii >= jj), (ii <= jj))


def _chunk_terms(q, k, b, far, mid):
    bf, bm = b[far:far + 1, :], b[mid:mid + 1, :]
    e = jnp.exp(b)
    em = jnp.exp(b - bm)
    eim = jnp.exp(bm - b)
    ed = jnp.exp(bf - b)
    return dict(e=e, em=em, eim=eim, ed=ed, dec=jnp.exp(bf), qe=q * e, qem=q * em, kim=k * eim, kd=k * ed)


def _gla_fwd(pb3, g3, nb, s_len, c_len, dk_, dv_):
    c = CHUNK
    nl, nc = s_len // c, c_len // c
    ns = nl + nc
    hk, hv = dk_ // HEADS, dv_ // HEADS
    l_len = pb3.shape[1]
    scale = hk ** -0.5
    mid = c // 2

    def body(*refs):
        ins, outs, z_scr = refs[:8], refs[8:14], refs[14]
        s = pl.program_id(0)

        @pl.when(s == 0)
        def _():
            z_scr[...] = jnp.zeros_like(z_scr)

        qs = jnp.where(s >= nc, scale, 0.0)
        for di, rev in enumerate((False, True)):
            q_ref, k_ref, v_ref, g_ref = ins[4 * di:4 * di + 4]
            o_ref, zs_ref, b_ref = outs[3 * di:3 * di + 3]
            mask, _ = _chunk_masks(c, rev)
            m_bf = mask.astype(BF16)
            far = 0 if rev else c - 1
            for b in range(nb):
                bc = _tri_mm(m_bf, g_ref[b])
                b_ref[b] = bc
                for h in range(HEADS):
                    ks, vs = slice(h * hk, (h + 1) * hk), slice(h * hv, (h + 1) * hv)
                    zi = (di * nb + b) * HEADS + h
                    v = v_ref[b, :, vs]
                    t = _chunk_terms(q_ref[b, :, ks] * qs, k_ref[b, :, ks], bc[:, ks], far, mid)
                    a = jnp.where(mask, _mm_nt(t["qem"], t["kim"]), 0.0)
                    z = z_scr[zi]
                    zs_ref[0, b * HEADS + h] = z
                    o_ref[b, :, vs] = _mm(a, v) + _mm_nt(t["qe"], z)
                    z_scr[zi] = z * t["dec"] + _mm_tn(v, t["kd"])

    in_specs, out_specs, out_shape = [], [], []
    for di, rev in enumerate((False, True)):
        ch = functools.partial(_scan_chunk, nl=nl, nc=nc, rev=rev)
        lch = functools.partial(_scan_lat_chunk, nl=nl, nc=nc, rev=rev)
        in_specs += [pl.BlockSpec((nb, c, dk_), lambda s, ch=ch: (0, ch(s), 0)),
                     pl.BlockSpec((nb, c, dk_), lambda s, ch=ch: (0, ch(s), 1)),
                     pl.BlockSpec((nb, c, dv_), lambda s, ch=ch: (0, ch(s), 1)),
                     pl.BlockSpec((nb, c, dk_), lambda s, ch=ch, di=di: (0, ch(s), di))]
        out_specs += [pl.BlockSpec((nb, c, dv_), lambda s, lch=lch: (0, lch(s), 0)),
                      pl.BlockSpec((1, nb * HEADS, hv, hk), lambda s: (s, 0, 0, 0)),
                      pl.BlockSpec((nb, c, dk_), lambda s, ch=ch: (0, ch(s), 0))]
        out_shape += [jax.ShapeDtypeStruct((nb, s_len, dv_), F32),
                      jax.ShapeDtypeStruct((ns, nb * HEADS, hv, hk), F32),
                      jax.ShapeDtypeStruct((nb, l_len, dk_), F32)]
    return pl.pallas_call(
        body, name="gla_fwd", grid=(ns,), in_specs=in_specs, out_specs=tuple(out_specs), out_shape=tuple(out_shape),
        scratch_shapes=[pltpu.VMEM((2 * nb * HEADS, hv, hk), F32)],
        compiler_params=_params())(pb3, pb3, pb3, g3, pb3, pb3, pb3, g3)


def _gla_bwd(pb3, do3, fwd_saved, nb, s_len, c_len, dk_, dv_):
    c = CHUNK
    nl, nc = s_len // c, c_len // c
    ns = nl + nc
    hk, hv = dk_ // HEADS, dv_ // HEADS
    l_len = pb3.shape[1]
    scale = hk ** -0.5
    mid = c // 2
    zs_f, b_f, zs_b, b_b = fwd_saved

    def body(*refs):
        ins, outs, dz_scr = refs[:12], refs[12:20], refs[20]
        s = pl.program_id(0)
        step = ns - 1 - s

        @pl.when(s == 0)
        def _():
            dz_scr[...] = jnp.zeros_like(dz_scr)

        lat = step >= nc
        qs = jnp.where(lat, scale, 0.0)
        dmul = jnp.where(lat, 1.0, 0.0)
        for di, rev in enumerate((False, True)):
            q_ref, k_ref, v_ref, b_ref, do_ref, zs_ref = ins[6 * di:6 * di + 6]
            dq_ref, dk_ref, dv_ref, dg_ref = outs[4 * di:4 * di + 4]
            mask, mask_t = _chunk_masks(c, rev)
            mt_bf = mask_t.astype(BF16)
            far = 0 if rev else c - 1
            far_row = lax.broadcasted_iota(jnp.int32, (c, hk), 0) == far
            for b in range(nb):
                db_parts = []
                for h in range(HEADS):
                    ks, vs = slice(h * hk, (h + 1) * hk), slice(h * hv, (h + 1) * hv)
                    zi = (di * nb + b) * HEADS + h
                    v = v_ref[b, :, vs]
                    d_o = do_ref[b, :, vs] * dmul
                    t = _chunk_terms(q_ref[b, :, ks] * qs, k_ref[b, :, ks], b_ref[b, :, ks], far, mid)
                    qem, kim, qe, kd = t["qem"], t["kim"], t["qe"], t["kd"]
                    a_t = jnp.where(mask_t, _mm_nt(kim, qem), 0.0)
                    d_a = jnp.where(mask, _mm_nt(d_o, v), 0.0)
                    d_at = jnp.where(mask_t, _mm_nt(v, d_o), 0.0)
                    z = zs_ref[0, b * HEADS + h]
                    dzn = dz_scr[zi]
                    dv_ref[b, :, vs] = _mm(a_t, d_o) + _mm_nt(kd, dzn)
                    dqem = _mm(d_a, kim)
                    dkim = _mm(d_at, qem)
                    dqe = _mm(d_o, z)
                    dkd = _mm(v, dzn)
                    ddec = jnp.sum(z * dzn, axis=0, keepdims=True)
                    dz_scr[zi] = dzn * t["dec"] + _mm_tn(d_o, qe)
                    dq_ref[b, :, ks] = (dqem * t["em"] + dqe * t["e"]) * qs
                    dk_ref[b, :, ks] = dkim * t["eim"] + dkd * t["ed"]
                    db = dqem * qem - dkim * kim + dqe * qe - dkd * kd
                    extra = jnp.sum(dkd * kd, axis=0, keepdims=True) + ddec * t["dec"]
                    db_parts.append(db + jnp.where(far_row, extra, 0.0))
                dg_ref[b] = _tri_mm(mt_bf, jnp.concatenate(db_parts, axis=1))

    in_specs, out_specs, out_shape, args = [], [], [], []
    for di, rev in enumerate((False, True)):
        ch = lambda s, rev=rev: _scan_chunk(ns - 1 - s, nl, nc, rev)
        lch = lambda s, rev=rev: _scan_lat_chunk(ns - 1 - s, nl, nc, rev)
        in_specs += [pl.BlockSpec((nb, c, dk_), lambda s, ch=ch: (0, ch(s), 0)),
                     pl.BlockSpec((nb, c, dk_), lambda s, ch=ch: (0, ch(s), 1)),
                     pl.BlockSpec((nb, c, dv_), lambda s, ch=ch: (0, ch(s), 1)),
                     pl.BlockSpec((nb, c, dk_), lambda s, ch=ch: (0, ch(s), 0)),
                     pl.BlockSpec((nb, c, dv_), lambda s, lch=lch: (0, lch(s), 0)),
                     pl.BlockSpec((1, nb * HEADS, hv, hk), lambda s: (ns - 1 - s, 0, 0, 0))]
        args += [pb3, pb3, pb3, (b_b if rev else b_f), do3, (zs_b if rev else zs_f)]
        for w in (dk_, dk_, dv_, dk_):
            out_specs.append(pl.BlockSpec((nb, c, w), lambda s, ch=ch: (0, ch(s), 0)))
            out_shape.append(jax.ShapeDtypeStruct((nb, l_len, w), F32))
    return pl.pallas_call(
        body, name="gla_bwd", grid=(ns,), in_specs=in_specs, out_specs=tuple(out_specs), out_shape=tuple(out_shape),
        scratch_shapes=[pltpu.VMEM((2 * nb * HEADS, hv, hk), F32)],
        compiler_params=_params())(*args)


def _tail(a1, pa, o_f, o_b, x2, tgt, mod, wc, wg, wo, ln_g, ln_b, gn_t, fg, nb, tm):
    tl, d = x2.shape
    nt = tl // tm
    per_ex = nt // nb
    hv = d // HEADS
    nrow = mod.shape[0]

    def body(a1_ref, z_ref, r_ref, mc_ref, mg_ref, of_ref, ob_ref, x_ref, t_ref, mod_ref, wc_ref, wg_ref, wo_ref,
             lng_ref, lnb_ref, gn_ref, fg_ref,
             dp_ref, da1_ref, do_ref, gx_ref, mrg_ref, dmo_ref, yci_ref, dyc_ref, ogi_ref, dyg_ref, sm_ref):
        i = pl.program_id(0)

        @pl.when(i == 0)
        def _():
            sm_ref[...] = jnp.zeros_like(sm_ref)

        bidx = i // per_ex
        gate = _rowsel(mod_ref[...], bidx, nb)[:, 2 * d:3 * d]
        lng, lnb, fgv = lng_ref[...], lnb_ref[...], fg_ref[...]
        gn = jnp.concatenate([gn_ref[...]] * HEADS, axis=1)
        wc_, wg_, wo_ = wc_ref[...], wg_ref[...], wo_ref[...]

        a1v = a1_ref[...]
        mu = jnp.mean(a1v, axis=-1, keepdims=True)
        xc = a1v - mu
        rs = lax.rsqrt(jnp.mean(xc * xc, axis=-1, keepdims=True) + EPS)
        xh = xc * rs
        a2 = xh * lng + lnb
        s2 = _sigmoid(a2)
        a3 = a2 * s2
        zv = z_ref[...]
        sz = _sigmoid(zv)
        siluz = zv * sz
        ycin = a3 * siluz
        yconv = _mm(ycin, wc_)

        o = of_ref[...] + ob_ref[...]
        ohat_parts, rn_parts = [], []
        for h in range(HEADS):
            oh = o[:, h * hv:(h + 1) * hv]
            rn = lax.rsqrt(jnp.mean(oh * oh, axis=-1, keepdims=True) + EPS)
            ohat_parts.append(oh * rn)
            rn_parts.append(rn)
        ohat = jnp.concatenate(ohat_parts, axis=1)
        on = ohat * gn
        rv = r_ref[...]
        sr = _sigmoid(rv)
        silur = rv * sr
        ogin = on * silur
        ygla = _mm(ogin, wg_)

        sc = _sigmoid(mc_ref[...])
        sg = _sigmoid(mg_ref[...])
        merged = sc * yconv + sg * ygla
        mo = _mm(merged, wo_)
        hn = x_ref[...] + gate * mo
        rf = lax.rsqrt(jnp.mean(hn * hn, axis=-1, keepdims=True) + EPS)
        yh = hn * rf
        err = yh * fgv - t_ref[...]
        loss_part = 0.5 * jnp.sum(err * err) * (1.0 / d)

        dy = err * (1.0 / d)
        dfg = jnp.sum(dy * yh, axis=0, keepdims=True)
        dyh = dy * fgv
        dhn = rf * (dyh - yh * jnp.mean(dyh * yh, axis=-1, keepdims=True))
        gx_ref[...] = dhn
        dgate = jnp.sum(dhn * mo, axis=0, keepdims=True)
        dmo = gate * dhn
        dmerged = _mm_nt(dmo, wo_)
        dyconv = dmerged * sc
        dygla = dmerged * sg
        dp_ref[:, 2 * d:3 * d] = (dmerged * yconv * sc * (1.0 - sc)).astype(BF16)
        dp_ref[:, 3 * d:4 * d] = (dmerged * ygla * sg * (1.0 - sg)).astype(BF16)
        dycin = _mm_nt(dyconv, wc_)
        dogin = _mm_nt(dygla, wg_)
        mrg_ref[...] = merged.astype(BF16)
        dmo_ref[...] = dmo.astype(BF16)
        yci_ref[...] = ycin.astype(BF16)
        dyc_ref[...] = dyconv.astype(BF16)
        ogi_ref[...] = ogin.astype(BF16)
        dyg_ref[...] = dygla.astype(BF16)

        da3 = dycin * siluz
        dp_ref[:, 0:d] = (dycin * a3 * _dsilu(zv, sz)).astype(BF16)
        da2 = da3 * _dsilu(a2, s2)
        dlng = jnp.sum(da2 * xh, axis=0, keepdims=True)
        dlnb = jnp.sum(da2, axis=0, keepdims=True)
        dxh = da2 * lng
        da1_ref[...] = rs * (dxh - jnp.mean(dxh, axis=-1, keepdims=True)
                             - xh * jnp.mean(dxh * xh, axis=-1, keepdims=True))

        don = dogin * silur
        dp_ref[:, d:2 * d] = (dogin * on * _dsilu(rv, sr)).astype(BF16)
        dgn = jnp.sum(don * ohat, axis=0, keepdims=True)
        dyn = don * gn
        for h in range(HEADS):
            vs = slice(h * hv, (h + 1) * hv)
            oh_hat = ohat_parts[h]
            dh = dyn[:, vs]
            do_ref[:, vs] = rn_parts[h] * (dh - oh_hat * jnp.mean(dh * oh_hat, axis=-1, keepdims=True))

        sm_ref[0:1, :] += dfg
        sm_ref[1:2, :] += dlng
        sm_ref[2:3, :] += dlnb
        sm_ref[3:4, :] += dgn
        sm_ref[4:5, :] += jnp.zeros((1, d), F32) + loss_part
        for b in range(nb):
            sm_ref[8 + b:9 + b, :] += jnp.where(bidx == b, dgate, 0.0)

    row = pl.BlockSpec((tm, d), lambda i: (i, 0))
    pcol = lambda blk: pl.BlockSpec((tm, d), lambda i: (i, blk))
    full = lambda arr: pl.BlockSpec(arr.shape, lambda i: (0,) * arr.ndim)
    bfo = jax.ShapeDtypeStruct((tl, d), BF16)
    f32o = jax.ShapeDtypeStruct((tl, d), F32)
    return pl.pallas_call(
        body, name="tail", grid=(nt,),
        in_specs=[row, pcol(2), pcol(3), pcol(4), pcol(5), row, row, row, row, full(mod), full(wc), full(wg),
                  full(wo), full(ln_g), full(ln_b), full(gn_t), full(fg)],
        out_specs=(pl.BlockSpec((tm, 4 * d), lambda i: (i, 0)), row, row, row, row, row, row, row, row, row,
                   pl.BlockSpec((16, d), lambda i: (0, 0))),
        out_shape=(jax.ShapeDtypeStruct((tl, 4 * d), BF16), f32o, f32o, f32o, bfo, bfo, bfo, bfo, bfo, bfo,
                   jax.ShapeDtypeStruct((16, d), F32)),
        compiler_params=_params())(a1, pa, pa, pa, pa, o_f, o_b, x2, tgt, mod, wc, wg, wo, ln_g, ln_b, gn_t, fg)


def _local_step(x, c, ctx, tgt, c_ctx, ada_w8, ada_b, norm_g, w_a, b_a, w_b, b_b, conv_w8, conv_b, ln_g, ln_b,
                up2, bias2, gla_norm_g, final_norm_g, proj, on_grads=None, on_du_a1=None):
    nb, s_len, d = x.shape
    c_len = ctx.shape[1]
    dk_, dv_ = d // 2, d
    tl, tc = nb * s_len, nb * c_len
    nbw = 2 * dk_ + dv_ + LANE
    tm = math.gcd(256, c_len)
    tiles = _Tiles(nb, s_len, c_len, tm, 2)
    tmm = tiles.big * tm
    l_len = tiles.rows_per_ex
    t_all = nb * l_len
    x2, ctx2, tgt2 = x.reshape(tl, d), ctx.reshape(tc, d), tgt.reshape(tl, d)

    cv = jnp.zeros((8, d), F32).at[0:nb].set(c).at[nb].set(c_ctx.reshape(d))
    mod = _ada_fwd(cv, ada_w8, ada_b)
    u = _norm_fwd(x2, ctx2, mod, norm_g, tiles)
    pa = _matmul_bias("inproj_a", u, w_a, b_a, tl, tmm, _tile(6 * d, 1536), u_tile=tiles.big_all_of_lat)
    pb = _matmul_bias("inproj_b", u, w_b, b_b, t_all, tmm, nbw)

    a1 = _conv_fwd(pa, conv_w8, conv_b, nb, s_len)
    lr_blk = (2 * dk_ + dv_) // LANE
    g_all = _decay_fwd(pb, up2, bias2, tm, lr_blk)
    pb3 = pb.reshape(nb, l_len, nbw)
    o_f, zs_f, b_f, o_b, zs_b, b_b2 = _gla_fwd(pb3, g_all.reshape(nb, l_len, 2 * dk_), nb, s_len, c_len, dk_, dv_)

    conv_proj, gla_proj, w_out = proj(a1) if callable(proj) else proj
    tt = math.gcd(128, s_len)
    (dp_a2, da1, d_o, gx1, merged, dmo, ycin, dyconv, ogin, dygla, small) = _tail(
        a1, pa, o_f.reshape(tl, dv_), o_b.reshape(tl, dv_), x2, tgt2, mod, conv_proj, gla_proj, w_out, ln_g, ln_b,
        gla_norm_g, final_norm_g, nb, tt)

    lat3 = lambda a: a.reshape(nb, s_len, a.shape[-1])
    tnw = _tile(d, 1024)
    d_w_out, _ = _matmul_tn_whole("dw_out", lat3(merged), lat3(dmo), s_len, tnw)
    d_conv_proj, _ = _matmul_tn_whole("dw_conv_proj", lat3(ycin), lat3(dyconv), s_len, tnw)
    d_gla_proj, _ = _matmul_tn_whole("dw_gla_proj", lat3(ogin), lat3(dygla), s_len, tnw)

    dp_a1, d_conv_w8, d_conv_b = _conv_bwd(pa, da1, conv_w8, nb, s_len)
    gl = _gla_bwd(pb3, d_o.reshape(nb, s_len, dv_), (zs_f, b_f, zs_b, b_b2), nb, s_len, c_len, dk_, dv_)
    gl = [g_.reshape(t_all, g_.shape[-1]) for g_ in gl]
    dp_b, d_up2, d_bias2 = _decay_bwd(pb, up2, bias2, gl[0:4], gl[4:8], tiles, lr_blk, dk_, dv_)

    u3 = u.reshape(nb, l_len, d)
    dw_a1, db_a1 = _matmul_tn_whole("dw_a1", u3, lat3(dp_a1), s_len, tnw)
    dw_a2, db_a2 = _matmul_tn_whole("dw_a2", u3, lat3(dp_a2), s_len, tnw)
    dw_b, db_b = _matmul_tn("dw_b", u, dp_b, t_all, tmm, nbw)
    grads = dict(w_a1=dw_a1, w_a2=dw_a2, w_b=dw_b, conv_w8=d_conv_w8, conv_proj=d_conv_proj, up2=d_up2,
                 gla_proj=d_gla_proj, w_out=d_w_out)

    tka = _tile(2 * d, 2048)
    du_a1 = _matmul_nt("du_a1", dp_a1, w_a, 0, tmm, tka, after=on_grads(grads) if on_grads else ())
    du_a2 = _matmul_nt("du_a2", dp_a2, w_a, (2 * d) // tka, tmm, tka, after=on_du_a1(du_a1) if on_du_a1 else ())
    du_b = _matmul_nt("du_b", dp_b, w_b, 0, tmm, nbw)
    grad_x2, dmod_ss, d_norm_g = _norm_bwd(x2, ctx2, mod, norm_g, [du_a1, du_a2], du_b, gx1, tiles)
    d_ada_w8, d_ada_b, d_cv = _ada_bwd(cv, ada_w8, dmod_ss, small, nb)

    return dict(
        grads, grad_x=grad_x2.reshape(nb, s_len, d), small=small, cv=d_cv, ada_w8=d_ada_w8, ada_b=d_ada_b,
        norm_g=d_norm_g, b_a1=db_a1, b_a2=db_a2, b_b=db_b, conv_b=d_conv_b, bias2=d_bias2)


def _regroup_pieces(d, r, wshard):
    cb = d // N_DEV
    segs = []
    for j in range(N_DEV):
        segs.append((j * cb, cb, 0, 2 * j * cb))
    for j in range(N_DEV):
        segs.append((d + j * cb, cb, 0, (2 * j + 1) * cb))
    segs += [(2 * d, d, 0, 2 * d), (3 * d, 2 * d + 2 * r, 1, 0), (5 * d + 2 * r, 3 * d, 0, 3 * d)]
    pieces = []
    for o0, w, dst, d0 in segs:
        lo = o0
        while lo < o0 + w:
            j = lo // wshard
            hi = min(o0 + w, (j + 1) * wshard)
            pieces.append((j, lo - j * wshard, hi - lo, dst, d0 + lo - o0))
            lo = hi
    return pieces


def _regroup(o, d, r):
    n_in = 8 * d + 2 * r
    parts = ([], [])
    for _, s0, n, dst, _ in sorted(_regroup_pieces(d, r, n_in), key=lambda p: (p[3], p[4])):
        parts[dst].append(o[..., s0:s0 + n])
    pad = jnp.zeros(o.shape[:-1] + (LANE - 2 * r,), o.dtype)
    return jnp.concatenate(parts[0], axis=-1), jnp.concatenate(parts[1] + [pad], axis=-1)


def _unshard_w_in(g_win, d, r, after=()):
    n_sh, _, ws = g_win.shape
    nbw = 2 * d + LANE
    pieces = _regroup_pieces(d, r, ws)
    tr = math.gcd(d, 256)

    def body(g_ref, *rest):
        a_ref, b_ref = rest[len(after):]
        dsts = (a_ref, b_ref)
        for j, s0, n, dst, d0 in pieces:
            dsts[dst][:, pl.ds(d0, n)] = g_ref[j, :, pl.ds(s0, n)]
        b_ref[:, pl.ds(2 * d + 2 * r, LANE - 2 * r)] = jnp.zeros((tr, LANE - 2 * r), b_ref.dtype)

    return pl.pallas_call(
        body, name="unshard_w_in", grid=(d // tr,),
        in_specs=[pl.BlockSpec((n_sh, tr, ws), lambda i: (0, i, 0))] + [_ANY] * len(after),
        out_specs=(pl.BlockSpec((tr, 6 * d), lambda i: (i, 0)), pl.BlockSpec((tr, nbw), lambda i: (i, 0))),
        out_shape=(jax.ShapeDtypeStruct((d, 6 * d), g_win.dtype), jax.ShapeDtypeStruct((d, nbw), g_win.dtype)),
        compiler_params=_params())(g_win, *after)


def _reshard_w_in(dw_a1, dw_a2, dw_b, d, r):
    ws = (8 * d + 2 * r) // N_DEV
    pieces = _regroup_pieces(d, r, ws)
    tr = math.gcd(d, 256)

    def body(a1_ref, a2_ref, b_ref, o_ref):
        for j, s0, n, dst, d0 in pieces:
            if dst == 1:
                src = b_ref[:, pl.ds(d0, n)]
            elif d0 < 2 * d:
                src = a1_ref[:, pl.ds(d0, n)]
            else:
                src = a2_ref[:, pl.ds(d0 - 2 * d, n)]
            o_ref[j, :, pl.ds(s0, n)] = src

    row = lambda w: pl.BlockSpec((tr, w), lambda i: (i, 0))
    return pl.pallas_call(
        body, name="reshard_w_in", grid=(d // tr,),
        in_specs=[row(2 * d), row(4 * d), row(2 * d + LANE)],
        out_specs=pl.BlockSpec((N_DEV, tr, ws), lambda i: (0, i, 0)),
        out_shape=jax.ShapeDtypeStruct((N_DEV, d, ws), dw_b.dtype),
        compiler_params=_params())(dw_a1, dw_a2, dw_b)


_SMALL = ("c_ctx", "ada_b", "norm_g", "b_in", "conv_b", "conv_ln_g", "conv_ln_b", "decay_bias_fwd",
          "decay_bias_bwd", "gla_norm_g", "final_norm_g")


def _small_layout(d, r):
    sizes = dict(c_ctx=d, ada_b=3 * d, norm_g=d, b_in=8 * d + 2 * r, conv_b=d, conv_ln_g=d, conv_ln_b=d,
                 decay_bias_fwd=d // 2, decay_bias_bwd=d // 2, gla_norm_g=d // HEADS, final_norm_g=d, loss=1)
    table, off = {}, 0
    for name in _SMALL + ("loss",):
        table[name] = (off, sizes[name])
        off += -(-sizes[name] // LANE) * LANE
    return table, off


def _pack_small(g, nb, d, r):
    table, width = _small_layout(d, r)
    hv = d // HEADS
    pieces = _regroup_pieces(d, r, 8 * d + 2 * r)
    names = ("small", "cv", "ada_b", "norm_g", "b_a1", "b_a2", "b_b", "conv_b", "bias2")

    def body(sm, cv, ab, ng, ba1, ba2, bb, cvb, b2, o_ref):
        o_ref[...] = jnp.zeros_like(o_ref)

        def put(name, val):
            off, n = table[name]
            o_ref[:, pl.ds(off, n)] = val

        put("c_ctx", cv[nb:nb + 1, :])
        put("ada_b", ab[...])
        put("norm_g", ng[...])
        off_b = table["b_in"][0]
        for _, s0, n, dst, d0 in pieces:
            if dst == 1:
                src = bb[:, pl.ds(d0, n)]
            elif d0 < 2 * d:
                src = ba1[:, pl.ds(d0, n)]
            else:
                src = ba2[:, pl.ds(d0 - 2 * d, n)]
            o_ref[:, pl.ds(off_b + s0, n)] = src
        put("conv_b", cvb[...])
        put("conv_ln_g", sm[1:2, :])
        put("conv_ln_b", sm[2:3, :])
        put("decay_bias_fwd", b2[:, 0:d // 2])
        put("decay_bias_bwd", b2[:, d // 2:d])
        gn = sm[3:4, 0:hv]
        for h in range(1, HEADS):
            gn = gn + sm[3:4, h * hv:(h + 1) * hv]
        put("gla_norm_g", gn)
        put("final_norm_g", sm[0:1, :])
        put("loss", sm[4:5, 0:1])

    return pl.pallas_call(body, name="pack_small", out_shape=jax.ShapeDtypeStruct((1, width), F32),
                          compiler_params=_params())(*[g[k] for k in names])


def _small_adam(parts, ws, ms, vs, d, r):
    table, width = _small_layout(d, r)
    n_parts = parts.shape[0]
    k = len(_SMALL)
    bc1 = 1.0 - ADAM_B1 ** ADAM_STEP
    bc2 = 1.0 - ADAM_B2 ** ADAM_STEP

    def body(p_ref, *refs):
        w_refs, m_refs, v_refs = refs[0:k], refs[k:2 * k], refs[2 * k:3 * k]
        outs = refs[3 * k:]
        tot = p_ref[0]
        for i in range(1, n_parts):
            tot = tot + p_ref[i]
        for i, name in enumerate(_SMALL):
            off, n = table[name]
            g = tot[:, off:off + n]
            mn = ADAM_B1 * m_refs[i][...] + (1.0 - ADAM_B1) * g
            vn = ADAM_B2 * v_refs[i][...] + (1.0 - ADAM_B2) * (g * g)
            outs[i][...] = g
            outs[k + i][...] = -ADAM_LR * ((mn / bc1) / (jnp.sqrt(vn / bc2) + ADAM_EPS) + ADAM_WD * w_refs[i][...])
            outs[2 * k + i][...] = mn
            outs[3 * k + i][...] = vn
        off, _ = table["loss"]
        outs[4 * k][...] = tot[:, off:off + 1]

    shapes = [jax.ShapeDtypeStruct(w.shape, F32) for w in ws]
    res = pl.pallas_call(body, name="small_adam", out_shape=tuple(shapes * 4 + [jax.ShapeDtypeStruct((1, 1), F32)]),
                         compiler_params=_params())(parts, *ws, *ms, *vs)
    return res[0:k], res[k:2 * k], res[2 * k:3 * k], res[3 * k:4 * k], res[4 * k]


def _mesh_pos():
    return lax.axis_index("x"), lax.axis_index("y"), lax.axis_index("c")


def _all_gather(arrs):
    n = len(arrs)

    def body(*refs):
        ins, outs = refs[:n], refs[n:2 * n]
        send_sems, recv_sems, local_sems = refs[2 * n:]
        x, y, c = _mesh_pos()
        me, sibling = (x, y, c), (x, y, 1 - c)
        chips = [(1 - x, y), (x, 1 - y), (1 - x, 1 - y)]

        def slot(a, pos):
            return outs[a].at[4 * pos[0] + 2 * pos[1] + pos[2]]

        def copy(a, k, block, to, src=None):
            return pltpu.make_async_remote_copy(
                src_ref=slot(a, block) if src is None else src, dst_ref=slot(a, block),
                send_sem=send_sems.at[7 * a + k], recv_sem=recv_sems.at[7 * a + k],
                device_id=to, device_id_type=MESH)

        mine = [pltpu.make_async_copy(ins[a], slot(a, me), local_sems.at[a]) for a in range(n)]
        for cp in mine:
            cp.start()
        first = []
        for a in range(n):
            first.append(copy(a, 0, me, sibling, src=ins[a]))
            first += [copy(a, 1 + j, me, (*chip, c), src=ins[a]) for j, chip in enumerate(chips)]
        for cp in first:
            cp.start()
        passed = []
        for j, chip in enumerate(chips):
            for a in range(n):
                copy(a, 1 + j, (*chip, c), me).wait_recv()
                fwd = copy(a, 4 + j, (*chip, c), sibling)
                fwd.start()
                passed.append(fwd)
        for a in range(n):
            copy(a, 0, sibling, me).wait_recv()
            for j, chip in enumerate(chips):
                copy(a, 4 + j, (*chip, 1 - c), me).wait_recv()
        for cp in first + passed:
            cp.wait_send()
        for cp in mine:
            cp.wait()

    return pl.pallas_call(
        body, name="all_gather",
        out_shape=tuple(jax.ShapeDtypeStruct((N_DEV,) + a.shape, a.dtype) for a in arrs),
        in_specs=[_ANY] * n, out_specs=tuple([_ANY] * n),
        scratch_shapes=[pltpu.SemaphoreType.DMA((7 * n,)), pltpu.SemaphoreType.DMA((7 * n,)),
                        pltpu.SemaphoreType.DMA((n,))],
    )(*arrs)


def _exchange_sibling(arrs):
    n = len(arrs)

    def body(*refs):
        ins, outs = refs[:n], refs[n:2 * n]
        send_sems, recv_sems = refs[2 * n:]
        x, y, c = _mesh_pos()
        copies = [pltpu.make_async_remote_copy(
            src_ref=ins[a].at[2 * k + (1 - c)], dst_ref=outs[a].at[k],
            send_sem=send_sems.at[4 * a + k], recv_sem=recv_sems.at[4 * a + k],
            device_id=(x, y, 1 - c), device_id_type=MESH) for a in range(n) for k in range(4)]
        for cp in copies:
            cp.start()
        for cp in copies:
            cp.wait_recv()
        for cp in copies:
            cp.wait_send()

    return pl.pallas_call(
        body, name="grad_exchange_sibling",
        out_shape=tuple(jax.ShapeDtypeStruct((4,) + a.shape[1:], a.dtype) for a in arrs),
        in_specs=[_ANY] * n, out_specs=tuple([_ANY] * n),
        scratch_shapes=[pltpu.SemaphoreType.DMA((4 * n,)), pltpu.SemaphoreType.DMA((4 * n,))],
    )(*arrs)


def _pair_sum(name, mine, theirs):
    _, r, cdim = mine.shape
    tr = r if (r % 8 or r <= 256) else math.gcd(r, 256)

    def body(m_ref, t_ref, o_ref):
        c = lax.axis_index("c")
        own = jnp.where(c == 0, m_ref[:, 0].astype(F32), m_ref[:, 1].astype(F32))
        o_ref[...] = (own + t_ref[...].astype(F32)).astype(o_ref.dtype)

    return pl.pallas_call(
        body, name=name, grid=(r // tr,),
        in_specs=[pl.BlockSpec((4, 2, tr, cdim), lambda i: (0, 0, i, 0)),
                  pl.BlockSpec((4, tr, cdim), lambda i: (0, i, 0))],
        out_specs=pl.BlockSpec((4, tr, cdim), lambda i: (0, i, 0)),
        out_shape=jax.ShapeDtypeStruct((4, r, cdim), mine.dtype),
        compiler_params=_params())(mine.reshape(4, 2, r, cdim), theirs)


def _exchange_chips(arrs):
    n = len(arrs)

    def body(*refs):
        ins, outs = refs[:n], refs[n:2 * n]
        send_sems, recv_sems, local_sems = refs[2 * n:]
        x, y, c = _mesh_pos()
        my_chip = 2 * x + y
        mine = [pltpu.make_async_copy(ins[a].at[my_chip], outs[a].at[my_chip], local_sems.at[a]) for a in range(n)]
        for cp in mine:
            cp.start()
        copies = []
        for rel in range(1, 4):
            px = 1 - x if rel & 2 else x
            py = 1 - y if rel & 1 else y
            for a in range(n):
                copies.append(pltpu.make_async_remote_copy(
                    src_ref=ins[a].at[2 * px + py], dst_ref=outs[a].at[my_chip],
                    send_sem=send_sems.at[3 * a + rel - 1], recv_sem=recv_sems.at[3 * a + rel - 1],
                    device_id=(px, py, c), device_id_type=MESH))
        for cp in copies:
            cp.start()
        for cp in copies:
            cp.wait_recv()
        for cp in copies:
            cp.wait_send()
        for cp in mine:
            cp.wait()

    return pl.pallas_call(
        body, name="grad_exchange_chips",
        out_shape=tuple(jax.ShapeDtypeStruct(a.shape, a.dtype) for a in arrs),
        in_specs=[_ANY] * n, out_specs=tuple([_ANY] * n),
        scratch_shapes=[pltpu.SemaphoreType.DMA((3 * n,)), pltpu.SemaphoreType.DMA((3 * n,)),
                        pltpu.SemaphoreType.DMA((n,))],
    )(*arrs)


_HBM = pl.BlockSpec(memory_space=pltpu.HBM)
_SEM = pl.BlockSpec(memory_space=pltpu.SEMAPHORE)


def _copies_start(name, srcs, lands, make_copies, n_sems):
    n, m = len(srcs), len(lands)

    def body(*refs):
        ins = refs[:n + m]
        send_sems, recv_sems = refs[n + m], refs[n + m + 1]
        for cp in make_copies(ins[:n], ins[n:], send_sems, recv_sems):
            cp.start()
        refs[-1][...] = jnp.zeros_like(refs[-1])

    res = pl.pallas_call(
        body, name=name,
        out_shape=(pltpu.SemaphoreType.DMA((n_sems,)), pltpu.SemaphoreType.DMA((n_sems,)),
                   *[pltpu.HBM(a.shape, a.dtype) for a in (*srcs, *lands)], jax.ShapeDtypeStruct((8, LANE), F32)),
        in_specs=[_HBM] * (n + m),
        out_specs=(_SEM, _SEM, *[_HBM] * (n + m), pl.BlockSpec(memory_space=pltpu.VMEM)),
        input_output_aliases={i: 2 + i for i in range(n + m)},
        compiler_params=pltpu.CompilerParams(has_side_effects=pltpu.SideEffectType.DATAFLOW_SIDE_EFFECTING),
    )(*[pltpu.with_memory_space_constraint(a, pltpu.HBM) for a in (*srcs, *lands)])
    return res[0], res[1], res[2:2 + n], res[2 + n:2 + n + m], res[-1]


def _copies_wait(name, started, after, make_copies):
    send_sems, recv_sems, srcs, lands, _ = started
    n, m = len(srcs), len(lands)

    def body(*refs):
        ins = refs[:n + m]
        for cp in make_copies(ins[:n], ins[n:], refs[n + m], refs[n + m + 1]):
            cp.wait_send()
            cp.wait_recv()

    res = pl.pallas_call(
        body, name=name,
        out_shape=tuple(pltpu.HBM(a.shape, a.dtype) for a in (*srcs, *lands)),
        in_specs=[_HBM] * (n + m) + [_SEM, _SEM] + [_ANY] * len(after),
        out_specs=tuple([_HBM] * (n + m)),
        input_output_aliases={i: i for i in range(n + m)},
        compiler_params=pltpu.CompilerParams(has_side_effects=pltpu.SideEffectType.DATAFLOW_SIDE_EFFECTING),
    )(*srcs, *lands, send_sems, recv_sems, *after)
    return res[:n], res[n:]


def _gather_copies(srcs, lands, send_sems, recv_sems):
    x, y, c = _mesh_pos()
    me_i = 4 * x + 2 * y + c
    copies = []
    for rel in range(1, N_DEV):
        peer = (1 - x if rel & 4 else x, 1 - y if rel & 2 else y, 1 - c if rel & 1 else c)
        for a in range(len(srcs)):
            copies.append(pltpu.make_async_remote_copy(
                src_ref=srcs[a], dst_ref=lands[a].at[me_i], send_sem=send_sems.at[7 * a + rel - 1],
                recv_sem=recv_sems.at[7 * a + rel - 1], device_id=peer, device_id_type=MESH))
    return copies


def _sibling_copies(srcs, lands, send_sems, recv_sems):
    x, y, c = _mesh_pos()
    return [pltpu.make_async_remote_copy(
        src_ref=srcs[a].at[2 * k + (1 - c)], dst_ref=lands[a].at[k], send_sem=send_sems.at[4 * a + k],
        recv_sem=recv_sems.at[4 * a + k], device_id=(x, y, 1 - c), device_id_type=MESH)
        for a in range(len(srcs)) for k in range(4)]


def _chip_copies(srcs, lands, send_sems, recv_sems):
    x, y, c = _mesh_pos()
    my_chip = 2 * x + y
    copies = []
    for rel in range(1, 4):
        px = 1 - x if rel & 2 else x
        py = 1 - y if rel & 1 else y
        for a in range(len(srcs)):
            copies.append(pltpu.make_async_remote_copy(
                src_ref=srcs[a].at[2 * px + py], dst_ref=lands[a].at[my_chip], send_sem=send_sems.at[3 * a + rel - 1],
                recv_sem=recv_sems.at[3 * a + rel - 1], device_id=(px, py, c), device_id_type=MESH))
    return copies


def _place_own(arrs):
    n = len(arrs)

    def body(*refs):
        x, y, c = _mesh_pos()
        sems = refs[2 * n]
        copies = [pltpu.make_async_copy(refs[a], refs[n + a].at[4 * x + 2 * y + c], sems.at[a]) for a in range(n)]
        for cp in copies:
            cp.start()
        for cp in copies:
            cp.wait()

    return pl.pallas_call(
        body, name="place_own", out_shape=tuple(jax.ShapeDtypeStruct((N_DEV,) + a.shape, a.dtype) for a in arrs),
        in_specs=[_ANY] * n, out_specs=tuple([_ANY] * n), scratch_shapes=[pltpu.SemaphoreType.DMA((n,))])(*arrs)


def _sum_adam(name, parts, w, m, v, own=None):
    r, cdim = w.shape
    n_parts = parts.shape[0]
    tr = r if (r % 8 or r <= 256) else math.gcd(r, 256)
    bc1 = 1.0 - ADAM_B1 ** ADAM_STEP
    bc2 = 1.0 - ADAM_B2 ** ADAM_STEP
    extra = [] if own is None else [own]

    def body(p_ref, *refs):
        w_ref, m_ref, v_ref, g_ref, d_ref, nm_ref, nv_ref = refs[len(extra):]
        if own is None:
            part = lambda k: p_ref[k].astype(F32)
        else:
            my_chip = 2 * lax.axis_index("x") + lax.axis_index("y")
            part = lambda k: jnp.where(my_chip == k, refs[0][k], p_ref[k]).astype(F32)
        g = part(0)
        for k in range(1, n_parts):
            g = g + part(k)
        mn = ADAM_B1 * m_ref[...] + (1.0 - ADAM_B1) * g
        vn = ADAM_B2 * v_ref[...] + (1.0 - ADAM_B2) * (g * g)
        g_ref[...] = g
        nm_ref[...] = mn
        nv_ref[...] = vn
        d_ref[...] = -ADAM_LR * ((mn / bc1) / (jnp.sqrt(vn / bc2) + ADAM_EPS) + ADAM_WD * w_ref[...])

    blk = pl.BlockSpec((tr, cdim), lambda i: (i, 0))
    o = jax.ShapeDtypeStruct((r, cdim), F32)
    return pl.pallas_call(
        body, name=name, grid=(r // tr,),
        in_specs=[pl.BlockSpec((n_parts, tr, cdim), lambda i: (0, i, 0))] * (1 + len(extra)) + [blk, blk, blk],
        out_specs=(blk, blk, blk, blk), out_shape=(o, o, o, o),
        compiler_params=_params())(parts, *extra, w, m, v)


_WEIGHTS = ("c_ctx", "ada_w", "ada_b", "norm_g", "w_in", "b_in", "conv_w", "conv_b", "conv_ln_g", "conv_ln_b",
            "conv_proj", "decay_up_fwd", "decay_bias_fwd", "decay_up_bwd", "decay_bias_bwd", "gla_norm_g",
            "gla_proj", "w_out", "final_norm_g")


def _as2d(a):
    if a.ndim == 1:
        return a.reshape(1, -1)
    return a.reshape(-1, a.shape[-1])


def kernel(x, c, ctx, c_ctx, ada_w, ada_b, norm_g, w_in, b_in, conv_w, conv_b, conv_ln_g, conv_ln_b, conv_proj, decay_up_fwd, decay_bias_fwd, decay_up_bwd, decay_bias_bwd, gla_norm_g, gla_proj, w_out, final_norm_g, loss_target, m_c_ctx, m_ada_w, m_ada_b, m_norm_g, m_w_in, m_b_in, m_conv_w, m_conv_b, m_conv_ln_g, m_conv_ln_b, m_conv_proj, m_decay_up_fwd, m_decay_bias_fwd, m_decay_up_bwd, m_decay_bias_bwd, m_gla_norm_g, m_gla_proj, m_w_out, m_final_norm_g, v_c_ctx, v_ada_w, v_ada_b, v_norm_g, v_w_in, v_b_in, v_conv_w, v_conv_b, v_conv_ln_g, v_conv_ln_b, v_conv_proj, v_decay_up_fwd, v_decay_bias_fwd, v_decay_up_bwd, v_decay_bias_bwd, v_gla_norm_g, v_gla_proj, v_w_out, v_final_norm_g):
    env = dict(locals())
    wts = {k: env[k] for k in _WEIGHTS}
    d = x.shape[-1]
    r = decay_up_fwd.shape[1]
    dk_ = d // 2
    n_in = w_in.shape[-1] * N_DEV

    ds, dks = d // N_DEV, dk_ // N_DEV
    g_win, g_ada, conv_w8, g_up = _all_gather(
        [w_in[0].astype(BF16), ada_w[0].astype(BF16), conv_w[0],
         jnp.concatenate([decay_up_fwd[0], decay_up_bwd[0]], axis=1)])
    proj_own = [conv_proj[0].astype(BF16), gla_proj[0].astype(BF16), w_out[0].astype(BF16)]
    proj_start = _copies_start("proj_gather_start", proj_own, _place_own(proj_own), _gather_copies, 7 * 3)

    def proj(after):
        _, lands = _copies_wait("proj_gather_wait", proj_start, (after,), _gather_copies)
        return [w.reshape(d, d) for w in lands]

    w_a, w_b = _unshard_w_in(g_win, d, r, after=(proj_start[4],))
    up_f = g_up[:, :, 0:dks].transpose(1, 0, 2).reshape(r, dk_)
    up_b = g_up[:, :, dks:].transpose(1, 0, 2).reshape(r, dk_)
    up2 = jnp.zeros((LANE, 2 * dk_), F32).at[0:r, 0:dk_].set(up_f).at[r:2 * r, dk_:].set(up_b)
    bias2 = jnp.concatenate([decay_bias_fwd, decay_bias_bwd], axis=1)
    b_a, b_b = _regroup(b_in, d, r)

    names = ("w_in", "conv_proj", "gla_proj", "w_out", "conv_w", "decay_up")
    comm = {}

    def on_grads(gr):
        d_up = jnp.concatenate([gr["up2"][0:r, 0:dk_].reshape(r, N_DEV, dks).transpose(1, 0, 2),
                                gr["up2"][r:2 * r, dk_:].reshape(r, N_DEV, dks).transpose(1, 0, 2)], axis=2)
        mine = [_reshard_w_in(gr["w_a1"], gr["w_a2"], gr["w_b"], d, r), gr["conv_proj"].reshape(N_DEV, ds, d),
                gr["gla_proj"].reshape(N_DEV, ds, d), gr["w_out"].reshape(N_DEV, ds, d), gr["conv_w8"], d_up]
        lands = [lax.empty((4,) + a.shape[1:], a.dtype) for a in mine]
        comm["sibling"] = _copies_start("grad_sibling_start", mine, lands, _sibling_copies, 4 * len(mine))
        return (comm["sibling"][4],)

    def on_du_a1(du_a1):
        mine, theirs = _copies_wait("grad_sibling_wait", comm["sibling"], (du_a1,), _sibling_copies)
        sums = [_pair_sum("pair_sum_" + nm, a, b) for nm, a, b in zip(names, mine, theirs)]
        lands = [lax.empty(a.shape, a.dtype) for a in sums]
        comm["chips"] = _copies_start("grad_chips_start", sums, lands, _chip_copies, 3 * len(sums))
        return (comm["chips"][4],)

    g = _local_step(x, c, ctx, loss_target, c_ctx, g_ada, ada_b, norm_g[0:1], w_a, b_a, w_b, b_b,
                    conv_w8, conv_b, conv_ln_g, conv_ln_b, up2, bias2, gla_norm_g, final_norm_g.reshape(1, d),
                    proj, on_grads, on_du_a1)

    (their_ada,) = _exchange_sibling([g["ada_w8"]])
    (x_ada,) = _exchange_chips([_pair_sum("pair_sum_ada_w", g["ada_w8"], their_ada)])
    own, landed = _copies_wait("grad_chips_wait", comm["chips"], (x_ada,), _chip_copies)
    o_win, o_cp, o_gp, o_wo, o_cw, o_up = own
    x_win, x_cp, x_gp, x_wo, x_cw, x_up = landed

    (packs,) = _all_gather([_pack_small(g, x.shape[0], d, r)])
    row = lambda a: a.reshape(1, -1)
    sg, sd, sm, sv, loss = _small_adam(packs, [row(wts[k]) for k in _SMALL], [row(env["m_" + k]) for k in _SMALL],
                                       [row(env["v_" + k]) for k in _SMALL], d, r)
    out = {}
    for i, k in enumerate(_SMALL):
        for pre, arrs in (("grad_", sg), ("delta_", sd), ("new_m_", sm), ("new_v_", sv)):
            out[pre + k] = arrs[i].reshape(wts[k].shape)
    loss = loss.reshape(())

    def big(name, parts, wname, own=None):
        w2 = _as2d(wts[wname])
        res = _sum_adam(name, parts, w2, _as2d(env["m_" + wname]), _as2d(env["v_" + wname]), own)
        for pre, arr in zip(("grad_", "delta_", "new_m_", "new_v_"), res):
            out[pre + wname] = arr.reshape(wts[wname].shape)

    big("adam_w_in", x_win, "w_in", o_win)
    big("adam_ada_w", x_ada, "ada_w")
    big("adam_conv_proj", x_cp, "conv_proj", o_cp)
    big("adam_gla_proj", x_gp, "gla_proj", o_gp)
    big("adam_w_out", x_wo, "w_out", o_wo)
    big("adam_conv_w", x_cw, "conv_w", o_cw)
    big("adam_up_f", x_up[:, :, 0:dks], "decay_up_fwd", o_up[:, :, 0:dks])
    big("adam_up_b", x_up[:, :, dks:], "decay_up_bwd", o_up[:, :, dks:])

    return (loss, g["grad_x"], *[out["grad_" + k] for k in _WEIGHTS], *[out["delta_" + k] for k in _WEIGHTS],
            *[out["new_m_" + k] for k in _WEIGHTS], *[out["new_v_" + k] for k in _WEIGHTS])
```

```python
import functools
import math

import jax
import jax.numpy as jnp
from jax import lax
from jax.experimental import pallas as pl
from jax.experimental.pallas import tpu as pltpu

F32 = jnp.float32
BF16 = jnp.bfloat16
MESH = pl.DeviceIdType.MESH

N_DEV = 8
GRID_W = 64
CHUNK = 128
HEADS = 4
EPS = 1e-6
GATE_TAU = 16.0
LANE = 128
ADAM_LR, ADAM_B1, ADAM_B2, ADAM_EPS, ADAM_WD, ADAM_STEP = 0.001, 0.9, 0.999, 1e-08, 0.01, 10
VMEM_LIMIT = 56 * 1024 * 1024
_ANY = pl.BlockSpec(memory_space=pl.ANY)


def _params(**kw):
    return pltpu.CompilerParams(vmem_limit_bytes=VMEM_LIMIT, **kw)


def _tile(n, pref):
    t = (min(pref, n) // LANE) * LANE
    while t >= LANE:
        if n % t == 0:
            return t
        t -= LANE
    return n


def _mm(a, b):
    return jnp.dot(a.astype(BF16), b.astype(BF16), preferred_element_type=F32)


def _mm_nt(a, b):
    return lax.dot_general(a.astype(BF16), b.astype(BF16), (((1,), (1,)), ((), ())), preferred_element_type=F32)


def _mm_tn(a, b):
    return lax.dot_general(a.astype(BF16), b.astype(BF16), (((0,), (0,)), ((), ())), preferred_element_type=F32)


def _mm_tn_hi(a, b):
    return lax.dot_general(a, b, (((0,), (0,)), ((), ())), precision=lax.Precision.HIGHEST, preferred_element_type=F32)


def _sigmoid(x):
    return 0.5 * jnp.tanh(0.5 * x) + 0.5


def _dsilu(x, s):
    return s * (1.0 + x * (1.0 - s))


def _rowsel(table, idx, n):
    out = table[0:1, :]
    for r in range(1, n):
        out = jnp.where(idx == r, table[r:r + 1, :], out)
    return out


def _ada_fwd(cv, ada_w8, ada_b):
    n_sh, _, ws = ada_w8.shape

    def body(cv_ref, w_ref, b_ref, o_ref):
        c = cv_ref[...]
        sv = c * _sigmoid(c)
        for j in range(n_sh):
            cols = pl.ds(j * ws, ws)
            o_ref[:, cols] = _mm(sv, w_ref[j]) + b_ref[:, cols]

    return pl.pallas_call(body, name="ada_fwd", out_shape=jax.ShapeDtypeStruct((cv.shape[0], n_sh * ws), F32),
                          compiler_params=_params())(cv, ada_w8, ada_b)


def _ada_bwd(cv, ada_w8, dmod_ss, small, nb):
    n_sh, d, ws = ada_w8.shape

    def body(cv_ref, w_ref, dm_ref, sm_ref, dw_ref, db_ref, dc_ref):
        c = cv_ref[...]
        s = _sigmoid(c)
        sv = c * s
        dm = jnp.concatenate([dm_ref[:, 0:2 * d], sm_ref[8:16, :]], axis=1)
        db_ref[...] = jnp.sum(dm, axis=0, keepdims=True)
        dsv = None
        for j in range(n_sh):
            dmj = dm[:, j * ws:(j + 1) * ws]
            dw_ref[j] = _mm_tn_hi(sv, dmj).astype(dw_ref.dtype)
            part = _mm_nt(dmj, w_ref[j])
            dsv = part if dsv is None else dsv + part
        dc_ref[...] = dsv * _dsilu(c, s)

    return pl.pallas_call(
        body, name="ada_bwd",
        out_shape=(jax.ShapeDtypeStruct((n_sh, d, ws), BF16), jax.ShapeDtypeStruct((1, n_sh * ws), F32),
                   jax.ShapeDtypeStruct(cv.shape, F32)),
        compiler_params=_params())(cv, ada_w8, dmod_ss, small)


class _Tiles:
    def __init__(self, nb, s_len, c_len, tm, big):
        self.nb, self.tm, self.big = nb, tm, big
        self.lat, self.ctx = s_len // tm, c_len // tm
        self.pad = -(self.lat + self.ctx) % big
        self.per_ex = self.lat + self.ctx + self.pad
        self.n_all, self.n_lat = nb * self.per_ex, nb * self.lat
        self.rows_per_ex = self.per_ex * tm

    def is_lat(self, i):
        return i % self.per_ex < self.lat

    def is_pad(self, i):
        return i % self.per_ex >= self.lat + self.ctx

    def lat_of_all(self, i):
        return (i // self.per_ex) * self.lat + jnp.minimum(i % self.per_ex, self.lat - 1)

    def ctx_of_all(self, i):
        return (i // self.per_ex) * self.ctx + jnp.clip(i % self.per_ex - self.lat, 0, self.ctx - 1)

    def big_all_of_lat(self, t):
        lat_big = self.lat // self.big
        return (t // lat_big) * (self.per_ex // self.big) + t % lat_big


def _norm_fwd(x2, ctx2, mod, norm_g, tiles):
    tl, d = x2.shape
    tc = ctx2.shape[0]
    nb, tm = tiles.nb, tiles.tm

    def body(x_ref, c_ref, mod_ref, g_ref, u_ref):
        i = pl.program_id(0)
        lat = tiles.is_lat(i)
        xv = jnp.where(lat, x_ref[...], c_ref[...])
        row = jnp.where(lat, i // tiles.per_ex, nb)
        m = _rowsel(mod_ref[...], row, nb + 1)
        shift, scale = m[:, 0:d], m[:, d:2 * d]
        rstd = lax.rsqrt(jnp.mean(xv * xv, axis=-1, keepdims=True) + EPS)
        u = xv * rstd * g_ref[...] * (1.0 + scale) + shift
        u_ref[...] = jnp.where(tiles.is_pad(i), 0.0, u).astype(BF16)

    return pl.pallas_call(
        body, name="norm_fwd", grid=(tiles.n_all,),
        in_specs=[pl.BlockSpec((tm, d), lambda i: (tiles.lat_of_all(i), 0)),
                  pl.BlockSpec((tm, d), lambda i: (tiles.ctx_of_all(i), 0)),
                  pl.BlockSpec(mod.shape, lambda i: (0, 0)),
                  pl.BlockSpec((1, d), lambda i: (0, 0))],
        out_specs=pl.BlockSpec((tm, d), lambda i: (i, 0)),
        out_shape=jax.ShapeDtypeStruct((tiles.n_all * tm, d), BF16),
        compiler_params=_params())(x2, ctx2, mod, norm_g)


def _norm_bwd(x2, ctx2, mod, norm_g, du_lat, du_b, gx1, tiles):
    tl, d = x2.shape
    nb, tm = tiles.nb, tiles.tm
    nrow = mod.shape[0]
    n_lat_in = len(du_lat)

    def body(x_ref, c_ref, mod_ref, g_ref, *refs):
        dl_refs = refs[:n_lat_in]
        d3_ref, gx_ref, gxo_ref, dmod_ref, dg_ref = refs[n_lat_in:]
        i = pl.program_id(0)

        @pl.when(i == 0)
        def _():
            dmod_ref[...] = jnp.zeros_like(dmod_ref)
            dg_ref[...] = jnp.zeros_like(dg_ref)

        lat = tiles.is_lat(i)
        xv = jnp.where(lat, x_ref[...], c_ref[...])
        row = jnp.where(lat, i // tiles.per_ex, nb)
        m = _rowsel(mod_ref[...], row, nb + 1)
        scale = m[:, d:2 * d]
        g = g_ref[...]
        dl = dl_refs[0][...]
        for ref in dl_refs[1:]:
            dl = dl + ref[...]
        du = jnp.where(tiles.is_pad(i), 0.0, d3_ref[...] + jnp.where(lat, dl, 0.0))
        rstd = lax.rsqrt(jnp.mean(xv * xv, axis=-1, keepdims=True) + EPS)
        xh = xv * rstd
        dshift = jnp.sum(du, axis=0, keepdims=True)
        dscale = jnp.sum(du * xh * g, axis=0, keepdims=True)
        dxn = du * (1.0 + scale)
        dg_ref[...] += jnp.sum(dxn * xh, axis=0, keepdims=True)
        dxh = dxn * g
        dx = rstd * (dxh - xh * jnp.mean(dxh * xh, axis=-1, keepdims=True))

        @pl.when(lat)
        def _():
            gxo_ref[...] = dx + gx_ref[...]

        for r in range(nb + 1):
            dmod_ref[r:r + 1, 0:d] += jnp.where(row == r, dshift, 0.0)
            dmod_ref[r:r + 1, d:2 * d] += jnp.where(row == r, dscale, 0.0)

    lat_map = lambda i: (tiles.lat_of_all(i), 0)
    lat_spec = pl.BlockSpec((tm, d), lat_map)
    return pl.pallas_call(
        body, name="norm_bwd", grid=(tiles.n_all,),
        in_specs=[lat_spec,
                  pl.BlockSpec((tm, d), lambda i: (tiles.ctx_of_all(i), 0)),
                  pl.BlockSpec(mod.shape, lambda i: (0, 0)),
                  pl.BlockSpec((1, d), lambda i: (0, 0))]
                 + [lat_spec] * n_lat_in
                 + [pl.BlockSpec((tm, d), lambda i: (i, 0)), lat_spec],
        out_specs=(lat_spec,
                   pl.BlockSpec((nrow, 3 * d), lambda i: (0, 0)),
                   pl.BlockSpec((1, d), lambda i: (0, 0))),
        out_shape=(jax.ShapeDtypeStruct((tl, d), F32), jax.ShapeDtypeStruct((nrow, 3 * d), F32),
                   jax.ShapeDtypeStruct((1, d), F32)),
        compiler_params=_params())(x2, ctx2, mod, norm_g, *du_lat, du_b, gx1)


def _matmul_bias(name, u, w, b, rows, tm, tn, u_tile=lambda i: i):
    d, n = w.shape

    def body(u_ref, w_ref, b_ref, o_ref):
        o_ref[...] = jnp.dot(u_ref[...], w_ref[...], preferred_element_type=F32) + b_ref[...]

    return pl.pallas_call(
        body, name=name, grid=(n // tn, rows // tm),
        in_specs=[pl.BlockSpec((tm, d), lambda j, i: (u_tile(i), 0)),
                  pl.BlockSpec((d, tn), lambda j, i: (0, j)),
                  pl.BlockSpec((1, tn), lambda j, i: (0, j))],
        out_specs=pl.BlockSpec((tm, tn), lambda j, i: (i, j)),
        out_shape=jax.ShapeDtypeStruct((rows, n), F32),
        compiler_params=_params())(u, w, b)


def _matmul_nt(name, a, w, koff, tm, tk, after=()):
    r, kc = a.shape
    d = w.shape[0]
    nk = kc // tk

    def body(a_ref, w_ref, *rest):
        o_ref = rest[len(after)]
        k = pl.program_id(1)
        p = lax.dot_general(a_ref[...], w_ref[...], (((1,), (1,)), ((), ())), preferred_element_type=F32)

        @pl.when(k == 0)
        def _():
            o_ref[...] = p

        @pl.when(k > 0)
        def _():
            o_ref[...] += p

    return pl.pallas_call(
        body, name=name, grid=(r // tm, nk),
        in_specs=[pl.BlockSpec((tm, tk), lambda i, k: (i, k)),
                  pl.BlockSpec((d, tk), lambda i, k: (0, koff + k))] + [_ANY] * len(after),
        out_specs=pl.BlockSpec((tm, d), lambda i, k: (i, 0)),
        out_shape=jax.ShapeDtypeStruct((r, d), F32),
        compiler_params=_params())(a, w, *after)


def _matmul_tn(name, a, b, rows, tk, tn):
    m = a.shape[1]
    n = b.shape[1]
    nk = rows // tk

    def body(a_ref, b_ref, o_ref, s_ref, acc_ref):
        k = pl.program_id(1)
        bv = b_ref[...]
        p = lax.dot_general(a_ref[...], bv, (((0,), (0,)), ((), ())), preferred_element_type=F32)
        cs = jnp.sum(bv.astype(F32), axis=0, keepdims=True)

        @pl.when(k == 0)
        def _():
            acc_ref[...] = p
            s_ref[...] = cs

        @pl.when(k > 0)
        def _():
            acc_ref[...] += p
            s_ref[...] += cs

        @pl.when(k == nk - 1)
        def _():
            o_ref[...] = acc_ref[...].astype(o_ref.dtype)

    return pl.pallas_call(
        body, name=name, grid=(n // tn, nk),
        in_specs=[pl.BlockSpec((tk, m), lambda j, k: (k, 0)),
                  pl.BlockSpec((tk, tn), lambda j, k: (k, j))],
        out_specs=(pl.BlockSpec((m, tn), lambda j, k: (0, j)), pl.BlockSpec((1, tn), lambda j, k: (0, j))),
        out_shape=(jax.ShapeDtypeStruct((m, n), BF16), jax.ShapeDtypeStruct((1, n), F32)),
        scratch_shapes=[pltpu.VMEM((m, tn), F32)],
        compiler_params=_params())(a, b)


def _matmul_tn_whole(name, a3, b3, rows, tn):
    nb, _, m = a3.shape
    n = b3.shape[2]

    def body(a_ref, b_ref, o_ref, s_ref):
        p, cs = None, None
        for e in range(nb):
            bv = b_ref[e]
            pe = lax.dot_general(a_ref[e], bv, (((0,), (0,)), ((), ())), preferred_element_type=F32)
            ce = jnp.sum(bv.astype(F32), axis=0, keepdims=True)
            p, cs = (pe, ce) if p is None else (p + pe, cs + ce)
        o_ref[...] = p.astype(o_ref.dtype)
        s_ref[...] = cs

    return pl.pallas_call(
        body, name=name, grid=(n // tn,),
        in_specs=[pl.BlockSpec((nb, rows, m), lambda j: (0, 0, 0)),
                  pl.BlockSpec((nb, rows, tn), lambda j: (0, 0, j))],
        out_specs=(pl.BlockSpec((m, tn), lambda j: (0, j)), pl.BlockSpec((1, tn), lambda j: (0, j))),
        out_shape=(jax.ShapeDtypeStruct((m, n), BF16), jax.ShapeDtypeStruct((1, n), F32)),
        compiler_params=_params())(a3, b3)


def _conv_window(pad_ref, r, shift, ktaps, width, horizontal):
    if horizontal:
        return pad_ref[r, pl.ds(16 + shift, width), :]
    return pad_ref[r + ktaps // 2 + shift]


def _conv_row(pad_ref, w, r, ktaps, width, horizontal, flip):
    half = ktaps // 2
    acc = None
    for t in range(ktaps):
        win = _conv_window(pad_ref, r, (half - t) if flip else (t - half), ktaps, width, horizontal)
        term = win * w[t:t + 1, :]
        acc = term if acc is None else acc + term
    return acc


def _fill_padded(ref, val, rows, width, ktaps, horizontal):
    half_k = ktaps // 2
    cb = val.shape[-1]
    if horizontal:
        ref[:, 0:16, :] = jnp.zeros((rows, 16, cb), F32)
        ref[:, 16 + width:32 + width, :] = jnp.zeros((rows, 16, cb), F32)
        ref[:, 16:16 + width, :] = val
    else:
        ref[0:half_k, :, :] = jnp.zeros((half_k, width, cb), F32)
        ref[half_k + rows:2 * half_k + rows, :, :] = jnp.zeros((half_k, width, cb), F32)
        ref[half_k:half_k + rows, :, :] = val


def _conv_fwd(pa, conv_w8, conv_b, nb, s):
    nblk, ktaps, cb = conv_w8.shape
    d = nblk * cb
    rows, width = s // GRID_W, GRID_W
    half_k = ktaps // 2
    nh = nblk // 2

    def body(glu_ref, w_ref, b_ref, o_ref, ph_ref, pv_ref):
        j = pl.program_id(1)
        a0 = (glu_ref[:, 0:cb] * _sigmoid(glu_ref[:, cb:2 * cb])).reshape(rows, width, cb)
        w = w_ref[...]

        bias = b_ref[...]

        def run(pad_ref, horizontal):
            _fill_padded(pad_ref, a0, rows, width, ktaps, horizontal)

            def row(r, carry):
                at = pl.ds(pl.multiple_of(r * width, width), width)
                o_ref[at, :] = _conv_row(pad_ref, w, r, ktaps, width, horizontal, False) + bias
                return carry

            lax.fori_loop(0, rows, row, 0)

        @pl.when(j < nh)
        def _():
            run(ph_ref, True)

        @pl.when(j >= nh)
        def _():
            run(pv_ref, False)

    return pl.pallas_call(
        body, name="conv_fwd", grid=(nb, nblk),
        in_specs=[pl.BlockSpec((s, 2 * cb), lambda b, j: (b, j)),
                  pl.BlockSpec((None, ktaps, cb), lambda b, j: (j, 0, 0)),
                  pl.BlockSpec((1, cb), lambda b, j: (0, j))],
        out_specs=pl.BlockSpec((s, cb), lambda b, j: (b, j)),
        out_shape=jax.ShapeDtypeStruct((nb * s, d), F32),
        scratch_shapes=[pltpu.VMEM((rows, width + 32, cb), F32), pltpu.VMEM((rows + 2 * half_k, width, cb), F32)],
        compiler_params=_params())(pa, conv_w8, conv_b)


def _conv_bwd(pa, da1, conv_w8, nb, s):
    nblk, ktaps, cb = conv_w8.shape
    d = nblk * cb
    rows, width = s // GRID_W, GRID_W
    half_k = ktaps // 2
    nh = nblk // 2

    def body(glu_ref, da_ref, w_ref, dp_ref, dw_ref, db_ref, pha_ref, phd_ref, pva_ref, pvd_ref):
        j = pl.program_id(0)
        b = pl.program_id(1)
        a0 = (glu_ref[:, 0:cb] * _sigmoid(glu_ref[:, cb:2 * cb])).reshape(rows, width, cb)
        da1v = da_ref[...]
        d3 = da1v.reshape(rows, width, cb)
        w = w_ref[...]

        @pl.when(b == 0)
        def _():
            dw_ref[...] = jnp.zeros_like(dw_ref)
            db_ref[...] = jnp.zeros_like(db_ref)

        db_ref[...] += jnp.sum(da1v, axis=0, keepdims=True)

        def run(pa_ref, pd_ref, horizontal):
            _fill_padded(pa_ref, a0, rows, width, ktaps, horizontal)
            _fill_padded(pd_ref, d3, rows, width, ktaps, horizontal)

            def row(r, accs):
                at = pl.ds(pl.multiple_of(r * width, width), width)
                da0 = _conv_row(pd_ref, w, r, ktaps, width, horizontal, True)
                gv = glu_ref[at, 0:cb]
                sg = _sigmoid(glu_ref[at, cb:2 * cb])
                dp_ref[at, 0:cb] = (da0 * sg).astype(BF16)
                dp_ref[at, cb:2 * cb] = (da0 * gv * sg * (1.0 - sg)).astype(BF16)
                d_row = da_ref[at, :]
                out = []
                for t in range(ktaps):
                    prod = _conv_window(pa_ref, r, t - half_k, ktaps, width, horizontal) * d_row
                    out.append(accs[t] + jnp.sum(prod.reshape(width // 8, 8, cb), axis=0))
                return tuple(out)

            accs = lax.fori_loop(0, rows, row, tuple(jnp.zeros((8, cb), F32) for _ in range(ktaps)))
            for t in range(ktaps):
                dw_ref[t:t + 1, :] += jnp.sum(accs[t], axis=0, keepdims=True)

        @pl.when(j < nh)
        def _():
            run(pha_ref, phd_ref, True)

        @pl.when(j >= nh)
        def _():
            run(pva_ref, pvd_ref, False)

    return pl.pallas_call(
        body, name="conv_bwd", grid=(nblk, nb),
        in_specs=[pl.BlockSpec((s, 2 * cb), lambda j, b: (b, j)),
                  pl.BlockSpec((s, cb), lambda j, b: (b, j)),
                  pl.BlockSpec((None, ktaps, cb), lambda j, b: (j, 0, 0))],
        out_specs=(pl.BlockSpec((s, 2 * cb), lambda j, b: (b, j)),
                   pl.BlockSpec((None, ktaps, cb), lambda j, b: (j, 0, 0)),
                   pl.BlockSpec((1, cb), lambda j, b: (0, j))),
        out_shape=(jax.ShapeDtypeStruct((nb * s, 2 * d), BF16),
                   jax.ShapeDtypeStruct((nblk, ktaps, cb), F32), jax.ShapeDtypeStruct((1, d), F32)),
        scratch_shapes=[pltpu.VMEM((rows, width + 32, cb), F32), pltpu.VMEM((rows, width + 32, cb), F32),
                        pltpu.VMEM((rows + 2 * half_k, width, cb), F32),
                        pltpu.VMEM((rows + 2 * half_k, width, cb), F32)],
        compiler_params=_params())(pa, da1, conv_w8)


def _log_sigmoid(x):
    return jnp.minimum(x, 0.0) - jnp.log(1.0 + jnp.exp(-jnp.abs(x)))


def _decay_fwd(pb, up2, bias2, tm, lr_blk):
    t_all = pb.shape[0]
    n2 = up2.shape[1]

    def body(lr_ref, up_ref, b_ref, g_ref):
        logits = _mm(lr_ref[...], up_ref[...]) + b_ref[...]
        g_ref[...] = _log_sigmoid(logits) * (1.0 / GATE_TAU)

    return pl.pallas_call(
        body, name="decay_fwd", grid=(t_all // tm,),
        in_specs=[pl.BlockSpec((tm, LANE), lambda i: (i, lr_blk)),
                  pl.BlockSpec(up2.shape, lambda i: (0, 0)),
                  pl.BlockSpec((1, n2), lambda i: (0, 0))],
        out_specs=pl.BlockSpec((tm, n2), lambda i: (i, 0)),
        out_shape=jax.ShapeDtypeStruct((t_all, n2), F32),
        compiler_params=_params())(pb, up2, bias2)


def _decay_bwd(pb, up2, bias2, grads_f, grads_b, tiles, lr_blk, dk_, dv_):
    t_all = pb.shape[0]
    tm = tiles.tm
    n2 = up2.shape[1]
    nbw = 2 * dk_ + dv_ + LANE

    def body(lr_ref, up_ref, b_ref, dqf, dkf, dvf, dgf, dqb, dkb, dvb, dgb, dp_ref, dup_ref, dbias_ref):
        i = pl.program_id(0)
        pad = tiles.is_pad(i)
        live = lambda v: jnp.where(pad, 0.0, v)

        @pl.when(i == 0)
        def _():
            dup_ref[...] = jnp.zeros_like(dup_ref)
            dbias_ref[...] = jnp.zeros_like(dbias_ref)

        lr = lr_ref[...]
        up = up_ref[...]
        logits = _mm(lr, up) + b_ref[...]
        dg = live(jnp.concatenate([dgf[...], dgb[...]], axis=1))
        dlog = dg * (1.0 / GATE_TAU) * _sigmoid(-logits)
        dup_ref[...] += _mm_tn(lr, dlog)
        dbias_ref[...] += jnp.sum(dlog, axis=0, keepdims=True)
        dp_ref[:, 0:dk_] = live(dqf[...] + dqb[...]).astype(BF16)
        dp_ref[:, dk_:2 * dk_] = live(dkf[...] + dkb[...]).astype(BF16)
        dp_ref[:, 2 * dk_:2 * dk_ + dv_] = live(dvf[...] + dvb[...]).astype(BF16)
        dp_ref[:, 2 * dk_ + dv_:nbw] = _mm_nt(dlog, up).astype(BF16)

    row = lambda w: pl.BlockSpec((tm, w), lambda i: (i, 0))
    return pl.pallas_call(
        body, name="decay_bwd", grid=(t_all // tm,),
        in_specs=[pl.BlockSpec((tm, LANE), lambda i: (i, lr_blk)),
                  pl.BlockSpec(up2.shape, lambda i: (0, 0)),
                  pl.BlockSpec((1, n2), lambda i: (0, 0)),
                  row(dk_), row(dk_), row(dv_), row(dk_), row(dk_), row(dk_), row(dv_), row(dk_)],
        out_specs=(row(nbw), pl.BlockSpec(up2.shape, lambda i: (0, 0)), pl.BlockSpec((1, n2), lambda i: (0, 0))),
        out_shape=(jax.ShapeDtypeStruct((t_all, nbw), BF16), jax.ShapeDtypeStruct(up2.shape, F32),
                   jax.ShapeDtypeStruct((1, n2), F32)),
        compiler_params=_params())(pb, up2, bias2, *grads_f, *grads_b)


def _scan_chunk(s, nl, nc, rev):
    if rev:
        return jnp.where(s < nc, nl + (nc - 1 - s), nl - 1 - (s - nc))
    return jnp.where(s < nc, nl + s, s - nc)


def _scan_lat_chunk(s, nl, nc, rev):
    first = nl - 1 if rev else 0
    return jnp.where(s < nc, first, _scan_chunk(s, nl, nc, rev))


def _tri_mm(m_bf, x):
    hi = x.astype(BF16)
    r1 = x - hi.astype(F32)
    mid = r1.astype(BF16)
    lo = (r1 - mid.astype(F32)).astype(BF16)
    dot = lambda p: jnp.dot(m_bf, p, preferred_element_type=F32)
    return dot(hi) + dot(mid) + dot(lo)


def _chunk_masks(c, rev):
    ii = lax.broadcasted_iota(jnp.int32, (c, c), 0)
    jj = lax.broadcasted_iota(jnp.int32, (c, c), 1)
    return ((ii <= jj), (ii >= jj)) if rev else ((ii >= jj), (ii <= jj))


def _chunk_terms(q, k, b, far, mid):
    bf, bm = b[far:far + 1, :], b[mid:mid + 1, :]
    e = jnp.exp(b)
    em = jnp.exp(b - bm)
    eim = jnp.exp(bm - b)
    ed = jnp.exp(bf - b)
    return dict(e=e, em=em, eim=eim, ed=ed, dec=jnp.exp(bf), qe=q * e, qem=q * em, kim=k * eim, kd=k * ed)


def _gla_fwd(pb3, g3, nb, s_len, c_len, dk_, dv_):
    c = CHUNK
    nl, nc = s_len // c, c_len // c
    ns = nl + nc
    hk, hv = dk_ // HEADS, dv_ // HEADS
    l_len = pb3.shape[1]
    scale = hk ** -0.5
    mid = c // 2

    def body(*refs):
        ins, outs, z_scr = refs[:8], refs[8:14], refs[14]
        s = pl.program_id(0)

        @pl.when(s == 0)
        def _():
            z_scr[...] = jnp.zeros_like(z_scr)

        qs = jnp.where(s >= nc, scale, 0.0)
        for di, rev in enumerate((False, True)):
            q_ref, k_ref, v_ref, g_ref = ins[4 * di:4 * di + 4]
            o_ref, zs_ref, b_ref = outs[3 * di:3 * di + 3]
            mask, _ = _chunk_masks(c, rev)
            m_bf = mask.astype(BF16)
            far = 0 if rev else c - 1
            for b in range(nb):
                bc = _tri_mm(m_bf, g_ref[b])
                b_ref[b] = bc
                for h in range(HEADS):
                    ks, vs = slice(h * hk, (h + 1) * hk), slice(h * hv, (h + 1) * hv)
                    zi = (di * nb + b) * HEADS + h
                    v = v_ref[b, :, vs]
                    t = _chunk_terms(q_ref[b, :, ks] * qs, k_ref[b, :, ks], bc[:, ks], far, mid)
                    a = jnp.where(mask, _mm_nt(t["qem"], t["kim"]), 0.0)
                    z = z_scr[zi]
                    zs_ref[0, b * HEADS + h] = z
                    o_ref[b, :, vs] = _mm(a, v) + _mm_nt(t["qe"], z)
                    z_scr[zi] = z * t["dec"] + _mm_tn(v, t["kd"])

    in_specs, out_specs, out_shape = [], [], []
    for di, rev in enumerate((False, True)):
        ch = functools.partial(_scan_chunk, nl=nl, nc=nc, rev=rev)
        lch = functools.partial(_scan_lat_chunk, nl=nl, nc=nc, rev=rev)
        in_specs += [pl.BlockSpec((nb, c, dk_), lambda s, ch=ch: (0, ch(s), 0)),
                     pl.BlockSpec((nb, c, dk_), lambda s, ch=ch: (0, ch(s), 1)),
                     pl.BlockSpec((nb, c, dv_), lambda s, ch=ch: (0, ch(s), 1)),
                     pl.BlockSpec((nb, c, dk_), lambda s, ch=ch, di=di: (0, ch(s), di))]
        out_specs += [pl.BlockSpec((nb, c, dv_), lambda s, lch=lch: (0, lch(s), 0)),
                      pl.BlockSpec((1, nb * HEADS, hv, hk), lambda s: (s, 0, 0, 0)),
                      pl.BlockSpec((nb, c, dk_), lambda s, ch=ch: (0, ch(s), 0))]
        out_shape += [jax.ShapeDtypeStruct((nb, s_len, dv_), F32),
                      jax.ShapeDtypeStruct((ns, nb * HEADS, hv, hk), F32),
                      jax.ShapeDtypeStruct((nb, l_len, dk_), F32)]
    return pl.pallas_call(
        body, name="gla_fwd", grid=(ns,), in_specs=in_specs, out_specs=tuple(out_specs), out_shape=tuple(out_shape),
        scratch_shapes=[pltpu.VMEM((2 * nb * HEADS, hv, hk), F32)],
        compiler_params=_params())(pb3, pb3, pb3, g3, pb3, pb3, pb3, g3)


def _gla_bwd(pb3, do3, fwd_saved, nb, s_len, c_len, dk_, dv_):
    c = CHUNK
    nl, nc = s_len // c, c_len // c
    ns = nl + nc
    hk, hv = dk_ // HEADS, dv_ // HEADS
    l_len = pb3.shape[1]
    scale = hk ** -0.5
    mid = c // 2
    zs_f, b_f, zs_b, b_b = fwd_saved

    def body(*refs):
        ins, outs, dz_scr = refs[:12], refs[12:20], refs[20]
        s = pl.program_id(0)
        step = ns - 1 - s

        @pl.when(s == 0)
        def _():
            dz_scr[...] = jnp.zeros_like(dz_scr)

        lat = step >= nc
        qs = jnp.where(lat, scale, 0.0)
        dmul = jnp.where(lat, 1.0, 0.0)
        for di, rev in enumerate((False, True)):
            q_ref, k_ref, v_ref, b_ref, do_ref, zs_ref = ins[6 * di:6 * di + 6]
            dq_ref, dk_ref, dv_ref, dg_ref = outs[4 * di:4 * di + 4]
            mask, mask_t = _chunk_masks(c, rev)
            mt_bf = mask_t.astype(BF16)
            far = 0 if rev else c - 1
            far_row = lax.broadcasted_iota(jnp.int32, (c, hk), 0) == far
            for b in range(nb):
                db_parts = []
                for h in range(HEADS):
                    ks, vs = slice(h * hk, (h + 1) * hk), slice(h * hv, (h + 1) * hv)
                    zi = (di * nb + b) * HEADS + h
                    v = v_ref[b, :, vs]
                    d_o = do_ref[b, :, vs] * dmul
                    t = _chunk_terms(q_ref[b, :, ks] * qs, k_ref[b, :, ks], b_ref[b, :, ks], far, mid)
                    qem, kim, qe, kd = t["qem"], t["kim"], t["qe"], t["kd"]
                    a_t = jnp.where(mask_t, _mm_nt(kim, qem), 0.0)
                    d_a = jnp.where(mask, _mm_nt(d_o, v), 0.0)
                    d_at = jnp.where(mask_t, _mm_nt(v, d_o), 0.0)
                    z = zs_ref[0, b * HEADS + h]
                    dzn = dz_scr[zi]
                    dv_ref[b, :, vs] = _mm(a_t, d_o) + _mm_nt(kd, dzn)
                    dqem = _mm(d_a, kim)
                    dkim = _mm(d_at, qem)
                    dqe = _mm(d_o, z)
                    dkd = _mm(v, dzn)
                    ddec = jnp.sum(z * dzn, axis=0, keepdims=True)
                    dz_scr[zi] = dzn * t["dec"] + _mm_tn(d_o, qe)
                    dq_ref[b, :, ks] = (dqem * t["em"] + dqe * t["e"]) * qs
                    dk_ref[b, :, ks] = dkim * t["eim"] + dkd * t["ed"]
                    db = dqem * qem - dkim * kim + dqe * qe - dkd * kd
                    extra = jnp.sum(dkd * kd, axis=0, keepdims=True) + ddec * t["dec"]
                    db_parts.append(db + jnp.where(far_row, extra, 0.0))
                dg_ref[b] = _tri_mm(mt_bf, jnp.concatenate(db_parts, axis=1))

    in_specs, out_specs, out_shape, args = [], [], [], []
    for di, rev in enumerate((False, True)):
        ch = lambda s, rev=rev: _scan_chunk(ns - 1 - s, nl, nc, rev)
        lch = lambda s, rev=rev: _scan_lat_chunk(ns - 1 - s, nl, nc, rev)
        in_specs += [pl.BlockSpec((nb, c, dk_), lambda s, ch=ch: (0, ch(s), 0)),
                     pl.BlockSpec((nb, c, dk_), lambda s, ch=ch: (0, ch(s), 1)),
                     pl.BlockSpec((nb, c, dv_), lambda s, ch=ch: (0, ch(s), 1)),
                     pl.BlockSpec((nb, c, dk_), lambda s, ch=ch: (0, ch(s), 0)),
                     pl.BlockSpec((nb, c, dv_), lambda s, lch=lch: (0, lch(s), 0)),
                     pl.BlockSpec((1, nb * HEADS, hv, hk), lambda s: (ns - 1 - s, 0, 0, 0))]
        args += [pb3, pb3, pb3, (b_b if rev else b_f), do3, (zs_b if rev else zs_f)]
        for w in (dk_, dk_, dv_, dk_):
            out_specs.append(pl.BlockSpec((nb, c, w), lambda s, ch=ch: (0, ch(s), 0)))
            out_shape.append(jax.ShapeDtypeStruct((nb, l_len, w), F32))
    return pl.pallas_call(
        body, name="gla_bwd", grid=(ns,), in_specs=in_specs, out_specs=tuple(out_specs), out_shape=tuple(out_shape),
        scratch_shapes=[pltpu.VMEM((2 * nb * HEADS, hv, hk), F32)],
        compiler_params=_params())(*args)


def _tail(a1, pa, o_f, o_b, x2, tgt, mod, wc, wg, wo, ln_g, ln_b, gn_t, fg, nb, tm):
    tl, d = x2.shape
    nt = tl // tm
    per_ex = nt // nb
    hv = d // HEADS
    nrow = mod.shape[0]

    def body(a1_ref, z_ref, r_ref, mc_ref, mg_ref, of_ref, ob_ref, x_ref, t_ref, mod_ref, wc_ref, wg_ref, wo_ref,
             lng_ref, lnb_ref, gn_ref, fg_ref,
             dp_ref, da1_ref, do_ref, gx_ref, mrg_ref, dmo_ref, yci_ref, dyc_ref, ogi_ref, dyg_ref, sm_ref):
        i = pl.program_id(0)

        @pl.when(i == 0)
        def _():
            sm_ref[...] = jnp.zeros_like(sm_ref)

        bidx = i // per_ex
        gate = _rowsel(mod_ref[...], bidx, nb)[:, 2 * d:3 * d]
        lng, lnb, fgv = lng_ref[...], lnb_ref[...], fg_ref[...]
        gn = jnp.concatenate([gn_ref[...]] * HEADS, axis=1)
        wc_, wg_, wo_ = wc_ref[...], wg_ref[...], wo_ref[...]

        a1v = a1_ref[...]
        mu = jnp.mean(a1v, axis=-1, keepdims=True)
        xc = a1v - mu
        rs = lax.rsqrt(jnp.mean(xc * xc, axis=-1, keepdims=True) + EPS)
        xh = xc * rs
        a2 = xh * lng + lnb
        s2 = _sigmoid(a2)
        a3 = a2 * s2
        zv = z_ref[...]
        sz = _sigmoid(zv)
        siluz = zv * sz
        ycin = a3 * siluz
        yconv = _mm(ycin, wc_)

        o = of_ref[...] + ob_ref[...]
        ohat_parts, rn_parts = [], []
        for h in range(HEADS):
            oh = o[:, h * hv:(h + 1) * hv]
            rn = lax.rsqrt(jnp.mean(oh * oh, axis=-1, keepdims=True) + EPS)
            ohat_parts.append(oh * rn)
            rn_parts.append(rn)
        ohat = jnp.concatenate(ohat_parts, axis=1)
        on = ohat * gn
        rv = r_ref[...]
        sr = _sigmoid(rv)
        silur = rv * sr
        ogin = on * silur
        ygla = _mm(ogin, wg_)

        sc = _sigmoid(mc_ref[...])
        sg = _sigmoid(mg_ref[...])
        merged = sc * yconv + sg * ygla
        mo = _mm(merged, wo_)
        hn = x_ref[...] + gate * mo
        rf = lax.rsqrt(jnp.mean(hn * hn, axis=-1, keepdims=True) + EPS)
        yh = hn * rf
        err = yh * fgv - t_ref[...]
        loss_part = 0.5 * jnp.sum(err * err) * (1.0 / d)

        dy = err * (1.0 / d)
        dfg = jnp.sum(dy * yh, axis=0, keepdims=True)
        dyh = dy * fgv
        dhn = rf * (dyh - yh * jnp.mean(dyh * yh, axis=-1, keepdims=True))
        gx_ref[...] = dhn
        dgate = jnp.sum(dhn * mo, axis=0, keepdims=True)
        dmo = gate * dhn
        dmerged = _mm_nt(dmo, wo_)
        dyconv = dmerged * sc
        dygla = dmerged * sg
        dp_ref[:, 2 * d:3 * d] = (dmerged * yconv * sc * (1.0 - sc)).astype(BF16)
        dp_ref[:, 3 * d:4 * d] = (dmerged * ygla * sg * (1.0 - sg)).astype(BF16)
        dycin = _mm_nt(dyconv, wc_)
        dogin = _mm_nt(dygla, wg_)
        mrg_ref[...] = merged.astype(BF16)
        dmo_ref[...] = dmo.astype(BF16)
        yci_ref[...] = ycin.astype(BF16)
        dyc_ref[...] = dyconv.astype(BF16)
        ogi_ref[...] = ogin.astype(BF16)
        dyg_ref[...] = dygla.astype(BF16)

        da3 = dycin * siluz
        dp_ref[:, 0:d] = (dycin * a3 * _dsilu(zv, sz)).astype(BF16)
        da2 = da3 * _dsilu(a2, s2)
        dlng = jnp.sum(da2 * xh, axis=0, keepdims=True)
        dlnb = jnp.sum(da2, axis=0, keepdims=True)
        dxh = da2 * lng
        da1_ref[...] = rs * (dxh - jnp.mean(dxh, axis=-1, keepdims=True)
                             - xh * jnp.mean(dxh * xh, axis=-1, keepdims=True))

        don = dogin * silur
        dp_ref[:, d:2 * d] = (dogin * on * _dsilu(rv, sr)).astype(BF16)
        dgn = jnp.sum(don * ohat, axis=0, keepdims=True)
        dyn = don * gn
        for h in range(HEADS):
            vs = slice(h * hv, (h + 1) * hv)
            oh_hat = ohat_parts[h]
            dh = dyn[:, vs]
            do_ref[:, vs] = rn_parts[h] * (dh - oh_hat * jnp.mean(dh * oh_hat, axis=-1, keepdims=True))

        sm_ref[0:1, :] += dfg
        sm_ref[1:2, :] += dlng
        sm_ref[2:3, :] += dlnb
        sm_ref[3:4, :] += dgn
        sm_ref[4:5, :] += jnp.zeros((1, d), F32) + loss_part
        for b in range(nb):
            sm_ref[8 + b:9 + b, :] += jnp.where(bidx == b, dgate, 0.0)

    row = pl.BlockSpec((tm, d), lambda i: (i, 0))
    pcol = lambda blk: pl.BlockSpec((tm, d), lambda i: (i, blk))
    full = lambda arr: pl.BlockSpec(arr.shape, lambda i: (0,) * arr.ndim)
    bfo = jax.ShapeDtypeStruct((tl, d), BF16)
    f32o = jax.ShapeDtypeStruct((tl, d), F32)
    return pl.pallas_call(
        body, name="tail", grid=(nt,),
        in_specs=[row, pcol(2), pcol(3), pcol(4), pcol(5), row, row, row, row, full(mod), full(wc), full(wg),
                  full(wo), full(ln_g), full(ln_b), full(gn_t), full(fg)],
        out_specs=(pl.BlockSpec((tm, 4 * d), lambda i: (i, 0)), row, row, row, row, row, row, row, row, row,
                   pl.BlockSpec((16, d), lambda i: (0, 0))),
        out_shape=(jax.ShapeDtypeStruct((tl, 4 * d), BF16), f32o, f32o, f32o, bfo, bfo, bfo, bfo, bfo, bfo,
                   jax.ShapeDtypeStruct((16, d), F32)),
        compiler_params=_params())(a1, pa, pa, pa, pa, o_f, o_b, x2, tgt, mod, wc, wg, wo, ln_g, ln_b, gn_t, fg)


def _local_step(x, c, ctx, tgt, c_ctx, ada_w8, ada_b, norm_g, w_a, b_a, w_b, b_b, conv_w8, conv_b, ln_g, ln_b,
                up2, bias2, gla_norm_g, final_norm_g, proj, on_grads=None, on_du_a1=None):
    nb, s_len, d = x.shape
    c_len = ctx.shape[1]
    dk_, dv_ = d // 2, d
    tl, tc = nb * s_len, nb * c_len
    nbw = 2 * dk_ + dv_ + LANE
    tm = math.gcd(256, c_len)
    tiles = _Tiles(nb, s_len, c_len, tm, 2)
    tmm = tiles.big * tm
    l_len = tiles.rows_per_ex
    t_all = nb * l_len
    x2, ctx2, tgt2 = x.reshape(tl, d), ctx.reshape(tc, d), tgt.reshape(tl, d)

    cv = jnp.zeros((8, d), F32).at[0:nb].set(c).at[nb].set(c_ctx.reshape(d))
    mod = _ada_fwd(cv, ada_w8, ada_b)
    u = _norm_fwd(x2, ctx2, mod, norm_g, tiles)
    pa = _matmul_bias("inproj_a", u, w_a, b_a, tl, tmm, _tile(6 * d, 1536), u_tile=tiles.big_all_of_lat)
    pb = _matmul_bias("inproj_b", u, w_b, b_b, t_all, tmm, nbw)

    a1 = _conv_fwd(pa, conv_w8, conv_b, nb, s_len)
    lr_blk = (2 * dk_ + dv_) // LANE
    g_all = _decay_fwd(pb, up2, bias2, tm, lr_blk)
    pb3 = pb.reshape(nb, l_len, nbw)
    o_f, zs_f, b_f, o_b, zs_b, b_b2 = _gla_fwd(pb3, g_all.reshape(nb, l_len, 2 * dk_), nb, s_len, c_len, dk_, dv_)

    conv_proj, gla_proj, w_out = proj(a1) if callable(proj) else proj
    tt = math.gcd(128, s_len)
    (dp_a2, da1, d_o, gx1, merged, dmo, ycin, dyconv, ogin, dygla, small) = _tail(
        a1, pa, o_f.reshape(tl, dv_), o_b.reshape(tl, dv_), x2, tgt2, mod, conv_proj, gla_proj, w_out, ln_g, ln_b,
        gla_norm_g, final_norm_g, nb, tt)

    lat3 = lambda a: a.reshape(nb, s_len, a.shape[-1])
    tnw = _tile(d, 1024)
    d_w_out, _ = _matmul_tn_whole("dw_out", lat3(merged), lat3(dmo), s_len, tnw)
    d_conv_proj, _ = _matmul_tn_whole("dw_conv_proj", lat3(ycin), lat3(dyconv), s_len, tnw)
    d_gla_proj, _ = _matmul_tn_whole("dw_gla_proj", lat3(ogin), lat3(dygla), s_len, tnw)

    dp_a1, d_conv_w8, d_conv_b = _conv_bwd(pa, da1, conv_w8, nb, s_len)
    gl = _gla_bwd(pb3, d_o.reshape(nb, s_len, dv_), (zs_f, b_f, zs_b, b_b2), nb, s_len, c_len, dk_, dv_)
    gl = [g_.reshape(t_all, g_.shape[-1]) for g_ in gl]
    dp_b, d_up2, d_bias2 = _decay_bwd(pb, up2, bias2, gl[0:4], gl[4:8], tiles, lr_blk, dk_, dv_)

    u3 = u.reshape(nb, l_len, d)
    dw_a1, db_a1 = _matmul_tn_whole("dw_a1", u3, lat3(dp_a1), s_len, tnw)
    dw_a2, db_a2 = _matmul_tn_whole("dw_a2", u3, lat3(dp_a2), s_len, tnw)
    dw_b, db_b = _matmul_tn("dw_b", u, dp_b, t_all, tmm, nbw)
    grads = dict(w_a1=dw_a1, w_a2=dw_a2, w_b=dw_b, conv_w8=d_conv_w8, conv_proj=d_conv_proj, up2=d_up2,
                 gla_proj=d_gla_proj, w_out=d_w_out)

    tka = _tile(2 * d, 2048)
    du_a1 = _matmul_nt("du_a1", dp_a1, w_a, 0, tmm, tka, after=on_grads(grads) if on_grads else ())
    du_a2 = _matmul_nt("du_a2", dp_a2, w_a, (2 * d) // tka, tmm, tka, after=on_du_a1(du_a1) if on_du_a1 else ())
    du_b = _matmul_nt("du_b", dp_b, w_b, 0, tmm, nbw)
    grad_x2, dmod_ss, d_norm_g = _norm_bwd(x2, ctx2, mod, norm_g, [du_a1, du_a2], du_b, gx1, tiles)
    d_ada_w8, d_ada_b, d_cv = _ada_bwd(cv, ada_w8, dmod_ss, small, nb)

    return dict(
        grads, grad_x=grad_x2.reshape(nb, s_len, d), small=small, cv=d_cv, ada_w8=d_ada_w8, ada_b=d_ada_b,
        norm_g=d_norm_g, b_a1=db_a1, b_a2=db_a2, b_b=db_b, conv_b=d_conv_b, bias2=d_bias2)


def _regroup_pieces(d, r, wshard):
    cb = d // N_DEV
    segs = []
    for j in range(N_DEV):
        segs.append((j * cb, cb, 0, 2 * j * cb))
    for j in range(N_DEV):
        segs.append((d + j * cb, cb, 0, (2 * j + 1) * cb))
    segs += [(2 * d, d, 0, 2 * d), (3 * d, 2 * d + 2 * r, 1, 0), (5 * d + 2 * r, 3 * d, 0, 3 * d)]
    pieces = []
    for o0, w, dst, d0 in segs:
        lo = o0
        while lo < o0 + w:
            j = lo // wshard
            hi = min(o0 + w, (j + 1) * wshard)
            pieces.append((j, lo - j * wshard, hi - lo, dst, d0 + lo - o0))
            lo = hi
    return pieces


def _regroup(o, d, r):
    n_in = 8 * d + 2 * r
    parts = ([], [])
    for _, s0, n, dst, _ in sorted(_regroup_pieces(d, r, n_in), key=lambda p: (p[3], p[4])):
        parts[dst].append(o[..., s0:s0 + n])
    pad = jnp.zeros(o.shape[:-1] + (LANE - 2 * r,), o.dtype)
    return jnp.concatenate(parts[0], axis=-1), jnp.concatenate(parts[1] + [pad], axis=-1)


def _unshard_w_in(g_win, d, r, after=()):
    n_sh, _, ws = g_win.shape
    nbw = 2 * d + LANE
    pieces = _regroup_pieces(d, r, ws)
    tr = math.gcd(d, 256)

    def body(g_ref, *rest):
        a_ref, b_ref = rest[len(after):]
        dsts = (a_ref, b_ref)
        for j, s0, n, dst, d0 in pieces:
            dsts[dst][:, pl.ds(d0, n)] = g_ref[j, :, pl.ds(s0, n)]
        b_ref[:, pl.ds(2 * d + 2 * r, LANE - 2 * r)] = jnp.zeros((tr, LANE - 2 * r), b_ref.dtype)

    return pl.pallas_call(
        body, name="unshard_w_in", grid=(d // tr,),
        in_specs=[pl.BlockSpec((n_sh, tr, ws), lambda i: (0, i, 0))] + [_ANY] * len(after),
        out_specs=(pl.BlockSpec((tr, 6 * d), lambda i: (i, 0)), pl.BlockSpec((tr, nbw), lambda i: (i, 0))),
        out_shape=(jax.ShapeDtypeStruct((d, 6 * d), g_win.dtype), jax.ShapeDtypeStruct((d, nbw), g_win.dtype)),
        compiler_params=_params())(g_win, *after)


def _reshard_w_in(dw_a1, dw_a2, dw_b, d, r):
    ws = (8 * d + 2 * r) // N_DEV
    pieces = _regroup_pieces(d, r, ws)
    tr = math.gcd(d, 256)

    def body(a1_ref, a2_ref, b_ref, o_ref):
        for j, s0, n, dst, d0 in pieces:
            if dst == 1:
                src = b_ref[:, pl.ds(d0, n)]
            elif d0 < 2 * d:
                src = a1_ref[:, pl.ds(d0, n)]
            else:
                src = a2_ref[:, pl.ds(d0 - 2 * d, n)]
            o_ref[j, :, pl.ds(s0, n)] = src

    row = lambda w: pl.BlockSpec((tr, w), lambda i: (i, 0))
    return pl.pallas_call(
        body, name="reshard_w_in", grid=(d // tr,),
        in_specs=[row(2 * d), row(4 * d), row(2 * d + LANE)],
        out_specs=pl.BlockSpec((N_DEV, tr, ws), lambda i: (0, i, 0)),
        out_shape=jax.ShapeDtypeStruct((N_DEV, d, ws), dw_b.dtype),
        compiler_params=_params())(dw_a1, dw_a2, dw_b)


_SMALL = ("c_ctx", "ada_b", "norm_g", "b_in", "conv_b", "conv_ln_g", "conv_ln_b", "decay_bias_fwd",
          "decay_bias_bwd", "gla_norm_g", "final_norm_g")


def _small_layout(d, r):
    sizes = dict(c_ctx=d, ada_b=3 * d, norm_g=d, b_in=8 * d + 2 * r, conv_b=d, conv_ln_g=d, conv_ln_b=d,
                 decay_bias_fwd=d // 2, decay_bias_bwd=d // 2, gla_norm_g=d // HEADS, final_norm_g=d, loss=1)
    table, off = {}, 0
    for name in _SMALL + ("loss",):
        table[name] = (off, sizes[name])
        off += -(-sizes[name] // LANE) * LANE
    return table, off


def _pack_small(g, nb, d, r):
    table, width = _small_layout(d, r)
    hv = d // HEADS
    pieces = _regroup_pieces(d, r, 8 * d + 2 * r)
    names = ("small", "cv", "ada_b", "norm_g", "b_a1", "b_a2", "b_b", "conv_b", "bias2")

    def body(sm, cv, ab, ng, ba1, ba2, bb, cvb, b2, o_ref):
        o_ref[...] = jnp.zeros_like(o_ref)

        def put(name, val):
            off, n = table[name]
            o_ref[:, pl.ds(off, n)] = val

        put("c_ctx", cv[nb:nb + 1, :])
        put("ada_b", ab[...])
        put("norm_g", ng[...])
        off_b = table["b_in"][0]
        for _, s0, n, dst, d0 in pieces:
            if dst == 1:
                src = bb[:, pl.ds(d0, n)]
            elif d0 < 2 * d:
                src = ba1[:, pl.ds(d0, n)]
            else:
                src = ba2[:, pl.ds(d0 - 2 * d, n)]
            o_ref[:, pl.ds(off_b + s0, n)] = src
        put("conv_b", cvb[...])
        put("conv_ln_g", sm[1:2, :])
        put("conv_ln_b", sm[2:3, :])
        put("decay_bias_fwd", b2[:, 0:d // 2])
        put("decay_bias_bwd", b2[:, d // 2:d])
        gn = sm[3:4, 0:hv]
        for h in range(1, HEADS):
            gn = gn + sm[3:4, h * hv:(h + 1) * hv]
        put("gla_norm_g", gn)
        put("final_norm_g", sm[0:1, :])
        put("loss", sm[4:5, 0:1])

    return pl.pallas_call(body, name="pack_small", out_shape=jax.ShapeDtypeStruct((1, width), F32),
                          compiler_params=_params())(*[g[k] for k in names])


def _small_adam(parts, ws, ms, vs, d, r):
    table, width = _small_layout(d, r)
    n_parts = parts.shape[0]
    k = len(_SMALL)
    bc1 = 1.0 - ADAM_B1 ** ADAM_STEP
    bc2 = 1.0 - ADAM_B2 ** ADAM_STEP

    def body(p_ref, *refs):
        w_refs, m_refs, v_refs = refs[0:k], refs[k:2 * k], refs[2 * k:3 * k]
        outs = refs[3 * k:]
        tot = p_ref[0]
        for i in range(1, n_parts):
            tot = tot + p_ref[i]
        for i, name in enumerate(_SMALL):
            off, n = table[name]
            g = tot[:, off:off + n]
            mn = ADAM_B1 * m_refs[i][...] + (1.0 - ADAM_B1) * g
            vn = ADAM_B2 * v_refs[i][...] + (1.0 - ADAM_B2) * (g * g)
            outs[i][...] = g
            outs[k + i][...] = -ADAM_LR * ((mn / bc1) / (jnp.sqrt(vn / bc2) + ADAM_EPS) + ADAM_WD * w_refs[i][...])
            outs[2 * k + i][...] = mn
            outs[3 * k + i][...] = vn
        off, _ = table["loss"]
        outs[4 * k][...] = tot[:, off:off + 1]

    shapes = [jax.ShapeDtypeStruct(w.shape, F32) for w in ws]
    res = pl.pallas_call(body, name="small_adam", out_shape=tuple(shapes * 4 + [jax.ShapeDtypeStruct((1, 1), F32)]),
                         compiler_params=_params())(parts, *ws, *ms, *vs)
    return res[0:k], res[k:2 * k], res[2 * k:3 * k], res[3 * k:4 * k], res[4 * k]


def _mesh_pos():
    return lax.axis_index("x"), lax.axis_index("y"), lax.axis_index("c")


def _all_gather(arrs):
    n = len(arrs)

    def body(*refs):
        ins, outs = refs[:n], refs[n:2 * n]
        send_sems, recv_sems, local_sems = refs[2 * n:]
        x, y, c = _mesh_pos()
        me, sibling = (x, y, c), (x, y, 1 - c)
        chips = [(1 - x, y), (x, 1 - y), (1 - x, 1 - y)]

        def slot(a, pos):
            return outs[a].at[4 * pos[0] + 2 * pos[1] + pos[2]]

        def copy(a, k, block, to, src=None):
            return pltpu.make_async_remote_copy(
                src_ref=slot(a, block) if src is None else src, dst_ref=slot(a, block),
                send_sem=send_sems.at[7 * a + k], recv_sem=recv_sems.at[7 * a + k],
                device_id=to, device_id_type=MESH)

        mine = [pltpu.make_async_copy(ins[a], slot(a, me), local_sems.at[a]) for a in range(n)]
        for cp in mine:
            cp.start()
        first = []
        for a in range(n):
            first.append(copy(a, 0, me, sibling, src=ins[a]))
            first += [copy(a, 1 + j, me, (*chip, c), src=ins[a]) for j, chip in enumerate(chips)]
        for cp in first:
            cp.start()
        passed = []
        for j, chip in enumerate(chips):
            for a in range(n):
                copy(a, 1 + j, (*chip, c), me).wait_recv()
                fwd = copy(a, 4 + j, (*chip, c), sibling)
                fwd.start()
                passed.append(fwd)
        for a in range(n):
            copy(a, 0, sibling, me).wait_recv()
            for j, chip in enumerate(chips):
                copy(a, 4 + j, (*chip, 1 - c), me).wait_recv()
        for cp in first + passed:
            cp.wait_send()
        for cp in mine:
            cp.wait()

    return pl.pallas_call(
        body, name="all_gather",
        out_shape=tuple(jax.ShapeDtypeStruct((N_DEV,) + a.shape, a.dtype) for a in arrs),
        in_specs=[_ANY] * n, out_specs=tuple([_ANY] * n),
        scratch_shapes=[pltpu.SemaphoreType.DMA((7 * n,)), pltpu.SemaphoreType.DMA((7 * n,)),
                        pltpu.SemaphoreType.DMA((n,))],
    )(*arrs)


def _exchange_sibling(arrs):
    n = len(arrs)

    def body(*refs):
        ins, outs = refs[:n], refs[n:2 * n]
        send_sems, recv_sems = refs[2 * n:]
        x, y, c = _mesh_pos()
        copies = [pltpu.make_async_remote_copy(
            src_ref=ins[a].at[2 * k + (1 - c)], dst_ref=outs[a].at[k],
            send_sem=send_sems.at[4 * a + k], recv_sem=recv_sems.at[4 * a + k],
            device_id=(x, y, 1 - c), device_id_type=MESH) for a in range(n) for k in range(4)]
        for cp in copies:
            cp.start()
        for cp in copies:
            cp.wait_recv()
        for cp in copies:
            cp.wait_send()

    return pl.pallas_call(
        body, name="grad_exchange_sibling",
        out_shape=tuple(jax.ShapeDtypeStruct((4,) + a.shape[1:], a.dtype) for a in arrs),
        in_specs=[_ANY] * n, out_specs=tuple([_ANY] * n),
        scratch_shapes=[pltpu.SemaphoreType.DMA((4 * n,)), pltpu.SemaphoreType.DMA((4 * n,))],
    )(*arrs)


def _pair_sum(name, mine, theirs):
    _, r, cdim = mine.shape
    tr = r if (r % 8 or r <= 256) else math.gcd(r, 256)

    def body(m_ref, t_ref, o_ref):
        c = lax.axis_index("c")
        own = jnp.where(c == 0, m_ref[:, 0].astype(F32), m_ref[:, 1].astype(F32))
        o_ref[...] = (own + t_ref[...].astype(F32)).astype(o_ref.dtype)

    return pl.pallas_call(
        body, name=name, grid=(r // tr,),
        in_specs=[pl.BlockSpec((4, 2, tr, cdim), lambda i: (0, 0, i, 0)),
                  pl.BlockSpec((4, tr, cdim), lambda i: (0, i, 0))],
        out_specs=pl.BlockSpec((4, tr, cdim), lambda i: (0, i, 0)),
        out_shape=jax.ShapeDtypeStruct((4, r, cdim), mine.dtype),
        compiler_params=_params())(mine.reshape(4, 2, r, cdim), theirs)


def _exchange_chips(arrs):
    n = len(arrs)

    def body(*refs):
        ins, outs = refs[:n], refs[n:2 * n]
        send_sems, recv_sems, local_sems = refs[2 * n:]
        x, y, c = _mesh_pos()
        my_chip = 2 * x + y
        mine = [pltpu.make_async_copy(ins[a].at[my_chip], outs[a].at[my_chip], local_sems.at[a]) for a in range(n)]
        for cp in mine:
            cp.start()
        copies = []
        for rel in range(1, 4):
            px = 1 - x if rel & 2 else x
            py = 1 - y if rel & 1 else y
            for a in range(n):
                copies.append(pltpu.make_async_remote_copy(
                    src_ref=ins[a].at[2 * px + py], dst_ref=outs[a].at[my_chip],
                    send_sem=send_sems.at[3 * a + rel - 1], recv_sem=recv_sems.at[3 * a + rel - 1],
                    device_id=(px, py, c), device_id_type=MESH))
        for cp in copies:
            cp.start()
        for cp in copies:
            cp.wait_recv()
        for cp in copies:
            cp.wait_send()
        for cp in mine:
            cp.wait()

    return pl.pallas_call(
        body, name="grad_exchange_chips",
        out_shape=tuple(jax.ShapeDtypeStruct(a.shape, a.dtype) for a in arrs),
        in_specs=[_ANY] * n, out_specs=tuple([_ANY] * n),
        scratch_shapes=[pltpu.SemaphoreType.DMA((3 * n,)), pltpu.SemaphoreType.DMA((3 * n,)),
                        pltpu.SemaphoreType.DMA((n,))],
    )(*arrs)


_HBM = pl.BlockSpec(memory_space=pltpu.HBM)
_SEM = pl.BlockSpec(memory_space=pltpu.SEMAPHORE)


def _copies_start(name, srcs, lands, make_copies, n_sems):
    n, m = len(srcs), len(lands)

    def body(*refs):
        ins = refs[:n + m]
        send_sems, recv_sems = refs[n + m], refs[n + m + 1]
        for cp in make_copies(ins[:n], ins[n:], send_sems, recv_sems):
            cp.start()
        refs[-1][...] = jnp.zeros_like(refs[-1])

    res = pl.pallas_call(
        body, name=name,
        out_shape=(pltpu.SemaphoreType.DMA((n_sems,)), pltpu.SemaphoreType.DMA((n_sems,)),
                   *[pltpu.HBM(a.shape, a.dtype) for a in (*srcs, *lands)], jax.ShapeDtypeStruct((8, LANE), F32)),
        in_specs=[_HBM] * (n + m),
        out_specs=(_SEM, _SEM, *[_HBM] * (n + m), pl.BlockSpec(memory_space=pltpu.VMEM)),
        input_output_aliases={i: 2 + i for i in range(n + m)},
        compiler_params=pltpu.CompilerParams(has_side_effects=pltpu.SideEffectType.DATAFLOW_SIDE_EFFECTING),
    )(*[pltpu.with_memory_space_constraint(a, pltpu.HBM) for a in (*srcs, *lands)])
    return res[0], res[1], res[2:2 + n], res[2 + n:2 + n + m], res[-1]


def _copies_wait(name, started, after, make_copies):
    send_sems, recv_sems, srcs, lands, _ = started
    n, m = len(srcs), len(lands)

    def body(*refs):
        ins = refs[:n + m]
        for cp in make_copies(ins[:n], ins[n:], refs[n + m], refs[n + m + 1]):
            cp.wait_send()
            cp.wait_recv()

    res = pl.pallas_call(
        body, name=name,
        out_shape=tuple(pltpu.HBM(a.shape, a.dtype) for a in (*srcs, *lands)),
        in_specs=[_HBM] * (n + m) + [_SEM, _SEM] + [_ANY] * len(after),
        out_specs=tuple([_HBM] * (n + m)),
        input_output_aliases={i: i for i in range(n + m)},
        compiler_params=pltpu.CompilerParams(has_side_effects=pltpu.SideEffectType.DATAFLOW_SIDE_EFFECTING),
    )(*srcs, *lands, send_sems, recv_sems, *after)
    return res[:n], res[n:]


def _gather_copies(srcs, lands, send_sems, recv_sems):
    x, y, c = _mesh_pos()
    me_i = 4 * x + 2 * y + c
    copies = []
    for rel in range(1, N_DEV):
        peer = (1 - x if rel & 4 else x, 1 - y if rel & 2 else y, 1 - c if rel & 1 else c)
        for a in range(len(srcs)):
            copies.append(pltpu.make_async_remote_copy(
                src_ref=srcs[a], dst_ref=lands[a].at[me_i], send_sem=send_sems.at[7 * a + rel - 1],
                recv_sem=recv_sems.at[7 * a + rel - 1], device_id=peer, device_id_type=MESH))
    return copies


def _sibling_copies(srcs, lands, send_sems, recv_sems):
    x, y, c = _mesh_pos()
    return [pltpu.make_async_remote_copy(
        src_ref=srcs[a].at[2 * k + (1 - c)], dst_ref=lands[a].at[k], send_sem=send_sems.at[4 * a + k],
        recv_sem=recv_sems.at[4 * a + k], device_id=(x, y, 1 - c), device_id_type=MESH)
        for a in range(len(srcs)) for k in range(4)]


def _chip_copies(srcs, lands, send_sems, recv_sems):
    x, y, c = _mesh_pos()
    my_chip = 2 * x + y
    copies = []
    for rel in range(1, 4):
        px = 1 - x if rel & 2 else x
        py = 1 - y if rel & 1 else y
        for a in range(len(srcs)):
            copies.append(pltpu.make_async_remote_copy(
                src_ref=srcs[a].at[2 * px + py], dst_ref=lands[a].at[my_chip], send_sem=send_sems.at[3 * a + rel - 1],
                recv_sem=recv_sems.at[3 * a + rel - 1], device_id=(px, py, c), device_id_type=MESH))
    return copies


def _sum_adam(name, parts, w, m, v, own=None):
    r, cdim = w.shape
    n_parts = parts.shape[0]
    tr = r if (r % 8 or r <= 256) else math.gcd(r, 256)
    bc1 = 1.0 - ADAM_B1 ** ADAM_STEP
    bc2 = 1.0 - ADAM_B2 ** ADAM_STEP
    extra = [] if own is None else [own]

    def body(p_ref, *refs):
        w_ref, m_ref, v_ref, g_ref, d_ref, nm_ref, nv_ref = refs[len(extra):]
        if own is None:
            part = lambda k: p_ref[k].astype(F32)
        else:
            my_chip = 2 * lax.axis_index("x") + lax.axis_index("y")
            part = lambda k: jnp.where(my_chip == k, refs[0][k], p_ref[k]).astype(F32)
        g = part(0)
        for k in range(1, n_parts):
            g = g + part(k)
        mn = ADAM_B1 * m_ref[...] + (1.0 - ADAM_B1) * g
        vn = ADAM_B2 * v_ref[...] + (1.0 - ADAM_B2) * (g * g)
        g_ref[...] = g
        nm_ref[...] = mn
        nv_ref[...] = vn
        d_ref[...] = -ADAM_LR * ((mn / bc1) / (jnp.sqrt(vn / bc2) + ADAM_EPS) + ADAM_WD * w_ref[...])

    blk = pl.BlockSpec((tr, cdim), lambda i: (i, 0))
    o = jax.ShapeDtypeStruct((r, cdim), F32)
    return pl.pallas_call(
        body, name=name, grid=(r // tr,),
        in_specs=[pl.BlockSpec((n_parts, tr, cdim), lambda i: (0, i, 0))] * (1 + len(extra)) + [blk, blk, blk],
        out_specs=(blk, blk, blk, blk), out_shape=(o, o, o, o),
        compiler_params=_params())(parts, *extra, w, m, v)


_WEIGHTS = ("c_ctx", "ada_w", "ada_b", "norm_g", "w_in", "b_in", "conv_w", "conv_b", "conv_ln_g", "conv_ln_b",
            "conv_proj", "decay_up_fwd", "decay_bias_fwd", "decay_up_bwd", "decay_bias_bwd", "gla_norm_g",
            "gla_proj", "w_out", "final_norm_g")


def _as2d(a):
    if a.ndim == 1:
        return a.reshape(1, -1)
    return a.reshape(-1, a.shape[-1])


def kernel(x, c, ctx, c_ctx, ada_w, ada_b, norm_g, w_in, b_in, conv_w, conv_b, conv_ln_g, conv_ln_b, conv_proj, decay_up_fwd, decay_bias_fwd, decay_up_bwd, decay_bias_bwd, gla_norm_g, gla_proj, w_out, final_norm_g, loss_target, m_c_ctx, m_ada_w, m_ada_b, m_norm_g, m_w_in, m_b_in, m_conv_w, m_conv_b, m_conv_ln_g, m_conv_ln_b, m_conv_proj, m_decay_up_fwd, m_decay_bias_fwd, m_decay_up_bwd, m_decay_bias_bwd, m_gla_norm_g, m_gla_proj, m_w_out, m_final_norm_g, v_c_ctx, v_ada_w, v_ada_b, v_norm_g, v_w_in, v_b_in, v_conv_w, v_conv_b, v_conv_ln_g, v_conv_ln_b, v_conv_proj, v_decay_up_fwd, v_decay_bias_fwd, v_decay_up_bwd, v_decay_bias_bwd, v_gla_norm_g, v_gla_proj, v_w_out, v_final_norm_g):
    env = dict(locals())
    wts = {k: env[k] for k in _WEIGHTS}
    d = x.shape[-1]
    r = decay_up_fwd.shape[1]
    dk_ = d // 2
    n_in = w_in.shape[-1] * N_DEV

    ds, dks = d // N_DEV, dk_ // N_DEV
    g_win, g_ada, conv_w8, g_up = _all_gather(
        [w_in[0].astype(BF16), ada_w[0].astype(BF16), conv_w[0],
         jnp.concatenate([decay_up_fwd[0], decay_up_bwd[0]], axis=1)])
    proj_own = [conv_proj[0].astype(BF16), gla_proj[0].astype(BF16), w_out[0].astype(BF16)]
    me_i = 4 * lax.axis_index("x") + 2 * lax.axis_index("y") + lax.axis_index("c")
    proj_lands = [lax.dynamic_update_slice(lax.empty((N_DEV,) + a.shape, a.dtype), a[None], (me_i, 0, 0))
                  for a in proj_own]
    proj_start = _copies_start("proj_gather_start", proj_own, proj_lands, _gather_copies, 7 * 3)

    def proj(after):
        _, lands = _copies_wait("proj_gather_wait", proj_start, (after,), _gather_copies)
        return [w.reshape(d, d) for w in lands]

    w_a, w_b = _unshard_w_in(g_win, d, r, after=(proj_start[4],))
    up_f = g_up[:, :, 0:dks].transpose(1, 0, 2).reshape(r, dk_)
    up_b = g_up[:, :, dks:].transpose(1, 0, 2).reshape(r, dk_)
    up2 = jnp.zeros((LANE, 2 * dk_), F32).at[0:r, 0:dk_].set(up_f).at[r:2 * r, dk_:].set(up_b)
    bias2 = jnp.concatenate([decay_bias_fwd, decay_bias_bwd], axis=1)
    b_a, b_b = _regroup(b_in, d, r)

    names = ("w_in", "conv_proj", "gla_proj", "w_out", "conv_w", "decay_up")
    comm = {}

    def on_grads(gr):
        d_up = jnp.concatenate([gr["up2"][0:r, 0:dk_].reshape(r, N_DEV, dks).transpose(1, 0, 2),
                                gr["up2"][r:2 * r, dk_:].reshape(r, N_DEV, dks).transpose(1, 0, 2)], axis=2)
        mine = [_reshard_w_in(gr["w_a1"], gr["w_a2"], gr["w_b"], d, r), gr["conv_proj"].reshape(N_DEV, ds, d),
                gr["gla_proj"].reshape(N_DEV, ds, d), gr["w_out"].reshape(N_DEV, ds, d), gr["conv_w8"], d_up]
        lands = [lax.empty((4,) + a.shape[1:], a.dtype) for a in mine]
        comm["sibling"] = _copies_start("grad_sibling_start", mine, lands, _sibling_copies, 4 * len(mine))
        return (comm["sibling"][4],)

    def on_du_a1(du_a1):
        mine, theirs = _copies_wait("grad_sibling_wait", comm["sibling"], (du_a1,), _sibling_copies)
        sums = [_pair_sum("pair_sum_" + nm, a, b) for nm, a, b in zip(names, mine, theirs)]
        lands = [lax.empty(a.shape, a.dtype) for a in sums]
        comm["chips"] = _copies_start("grad_chips_start", sums, lands, _chip_copies, 3 * len(sums))
        return (comm["chips"][4],)

    g = _local_step(x, c, ctx, loss_target, c_ctx, g_ada, ada_b, norm_g[0:1], w_a, b_a, w_b, b_b,
                    conv_w8, conv_b, conv_ln_g, conv_ln_b, up2, bias2, gla_norm_g, final_norm_g.reshape(1, d),
                    proj, on_grads, on_du_a1)

    (their_ada,) = _exchange_sibling([g["ada_w8"]])
    ada_sum = _pair_sum("pair_sum_ada_w", g["ada_w8"], their_ada)
    ada_start = _copies_start("ada_chips_start", [ada_sum], [lax.empty(ada_sum.shape, ada_sum.dtype)],
                              _chip_copies, 3)
    own, landed = _copies_wait("grad_chips_wait", comm["chips"], (ada_start[4],), _chip_copies)
    o_win, o_cp, o_gp, o_wo, o_cw, o_up = own
    x_win, x_cp, x_gp, x_wo, x_cw, x_up = landed

    (packs,) = _all_gather([_pack_small(g, x.shape[0], d, r)])
    row = lambda a: a.reshape(1, -1)
    sg, sd, sm, sv, loss = _small_adam(packs, [row(wts[k]) for k in _SMALL], [row(env["m_" + k]) for k in _SMALL],
                                       [row(env["v_" + k]) for k in _SMALL], d, r)
    out = {}
    for i, k in enumerate(_SMALL):
        for pre, arrs in (("grad_", sg), ("delta_", sd), ("new_m_", sm), ("new_v_", sv)):
            out[pre + k] = arrs[i].reshape(wts[k].shape)
    loss = loss.reshape(())

    def big(name, parts, wname, own=None):
        w2 = _as2d(wts[wname])
        res = _sum_adam(name, parts, w2, _as2d(env["m_" + wname]), _as2d(env["v_" + wname]), own)
        for pre, arr in zip(("grad_", "delta_", "new_m_", "new_v_"), res):
            out[pre + wname] = arr.reshape(wts[wname].shape)

    big("adam_w_in", x_win, "w_in", o_win)
    big("adam_conv_proj", x_cp, "conv_proj", o_cp)
    big("adam_gla_proj", x_gp, "gla_proj", o_gp)
    big("adam_w_out", x_wo, "w_out", o_wo)
    big("adam_conv_w", x_cw, "conv_w", o_cw)
    big("adam_up_f", x_up[:, :, 0:dks], "decay_up_fwd", o_up[:, :, 0:dks])
    big("adam_up_b", x_up[:, :, dks:], "decay_up_bwd", o_up[:, :, dks:])
    (o_ada,), (x_ada,) = _copies_wait("ada_chips_wait", ada_start, (out["grad_w_in"], out["grad_w_out"], out["grad_b_in"]),
                                      _chip_copies)
    big("adam_ada_w", x_ada, "ada_w", o_ada)

    return (loss, g["grad_x"], *[out["grad_" + k] for k in _WEIGHTS], *[out["delta_" + k] for k in _WEIGHTS],
            *[out["new_m_" + k] for k in _WEIGHTS], *[out["new_v_" + k] for k in _WEIGHTS])
```

```python
import functools
import math

import jax
import jax.numpy as jnp
from jax import lax
from jax.experimental import pallas as pl
from jax.experimental.pallas import tpu as pltpu

F32 = jnp.float32
BF16 = jnp.bfloat16
MESH = pl.DeviceIdType.MESH

N_DEV = 8
GRID_W = 64
CHUNK = 128
HEADS = 4
EPS = 1e-6
GATE_TAU = 16.0
LANE = 128
ADAM_LR, ADAM_B1, ADAM_B2, ADAM_EPS, ADAM_WD, ADAM_STEP = 0.001, 0.9, 0.999, 1e-08, 0.01, 10
VMEM_LIMIT = 56 * 1024 * 1024
_ANY = pl.BlockSpec(memory_space=pl.ANY)


def _params(**kw):
    return pltpu.CompilerParams(vmem_limit_bytes=VMEM_LIMIT, **kw)


def _tile(n, pref):
    t = (min(pref, n) // LANE) * LANE
    while t >= LANE:
        if n % t == 0:
            return t
        t -= LANE
    return n


def _mm(a, b):
    return jnp.dot(a.astype(BF16), b.astype(BF16), preferred_element_type=F32)


def _mm_nt(a, b):
    return lax.dot_general(a.astype(BF16), b.astype(BF16), (((1,), (1,)), ((), ())), preferred_element_type=F32)


def _mm_tn(a, b):
    return lax.dot_general(a.astype(BF16), b.astype(BF16), (((0,), (0,)), ((), ())), preferred_element_type=F32)


def _mm_tn_hi(a, b):
    return lax.dot_general(a, b, (((0,), (0,)), ((), ())), precision=lax.Precision.HIGHEST, preferred_element_type=F32)


def _sigmoid(x):
    return 0.5 * jnp.tanh(0.5 * x) + 0.5


def _dsilu(x, s):
    return s * (1.0 + x * (1.0 - s))


def _rowsel(table, idx, n):
    out = table[0:1, :]
    for r in range(1, n):
        out = jnp.where(idx == r, table[r:r + 1, :], out)
    return out


def _ada_fwd(cv, ada_w8, ada_b):
    n_sh, _, ws = ada_w8.shape

    def body(cv_ref, w_ref, b_ref, o_ref):
        c = cv_ref[...]
        sv = c * _sigmoid(c)
        for j in range(n_sh):
            cols = pl.ds(j * ws, ws)
            o_ref[:, cols] = _mm(sv, w_ref[j]) + b_ref[:, cols]

    return pl.pallas_call(body, name="ada_fwd", out_shape=jax.ShapeDtypeStruct((cv.shape[0], n_sh * ws), F32),
                          compiler_params=_params())(cv, ada_w8, ada_b)


def _ada_bwd(cv, ada_w8, dmod_ss, small, nb):
    n_sh, d, ws = ada_w8.shape

    def body(cv_ref, w_ref, dm_ref, sm_ref, dw_ref, db_ref, dc_ref):
        c = cv_ref[...]
        s = _sigmoid(c)
        sv = c * s
        dm = jnp.concatenate([dm_ref[:, 0:2 * d], sm_ref[8:16, :]], axis=1)
        db_ref[...] = jnp.sum(dm, axis=0, keepdims=True)
        dsv = None
        for j in range(n_sh):
            dmj = dm[:, j * ws:(j + 1) * ws]
            dw_ref[j] = _mm_tn_hi(sv, dmj).astype(dw_ref.dtype)
            part = _mm_nt(dmj, w_ref[j])
            dsv = part if dsv is None else dsv + part
        dc_ref[...] = dsv * _dsilu(c, s)

    return pl.pallas_call(
        body, name="ada_bwd",
        out_shape=(jax.ShapeDtypeStruct((n_sh, d, ws), BF16), jax.ShapeDtypeStruct((1, n_sh * ws), F32),
                   jax.ShapeDtypeStruct(cv.shape, F32)),
        compiler_params=_params())(cv, ada_w8, dmod_ss, small)


class _Tiles:
    def __init__(self, nb, s_len, c_len, tm, big):
        self.nb, self.tm, self.big = nb, tm, big
        self.lat, self.ctx = s_len // tm, c_len // tm
        self.pad = -(self.lat + self.ctx) % big
        self.per_ex = self.lat + self.ctx + self.pad
        self.n_all, self.n_lat = nb * self.per_ex, nb * self.lat
        self.rows_per_ex = self.per_ex * tm

    def is_lat(self, i):
        return i % self.per_ex < self.lat

    def is_pad(self, i):
        return i % self.per_ex >= self.lat + self.ctx

    def lat_of_all(self, i):
        return (i // self.per_ex) * self.lat + jnp.minimum(i % self.per_ex, self.lat - 1)

    def ctx_of_all(self, i):
        return (i // self.per_ex) * self.ctx + jnp.clip(i % self.per_ex - self.lat, 0, self.ctx - 1)

    def big_all_of_lat(self, t):
        lat_big = self.lat // self.big
        return (t // lat_big) * (self.per_ex // self.big) + t % lat_big


def _norm_fwd(x2, ctx2, mod, norm_g, tiles):
    tl, d = x2.shape
    tc = ctx2.shape[0]
    nb, tm = tiles.nb, tiles.tm

    def body(x_ref, c_ref, mod_ref, g_ref, u_ref):
        i = pl.program_id(0)
        lat = tiles.is_lat(i)
        xv = jnp.where(lat, x_ref[...], c_ref[...])
        row = jnp.where(lat, i // tiles.per_ex, nb)
        m = _rowsel(mod_ref[...], row, nb + 1)
        shift, scale = m[:, 0:d], m[:, d:2 * d]
        rstd = lax.rsqrt(jnp.mean(xv * xv, axis=-1, keepdims=True) + EPS)
        u = xv * rstd * g_ref[...] * (1.0 + scale) + shift
        u_ref[...] = jnp.where(tiles.is_pad(i), 0.0, u).astype(BF16)

    return pl.pallas_call(
        body, name="norm_fwd", grid=(tiles.n_all,),
        in_specs=[pl.BlockSpec((tm, d), lambda i: (tiles.lat_of_all(i), 0)),
                  pl.BlockSpec((tm, d), lambda i: (tiles.ctx_of_all(i), 0)),
                  pl.BlockSpec(mod.shape, lambda i: (0, 0)),
                  pl.BlockSpec((1, d), lambda i: (0, 0))],
        out_specs=pl.BlockSpec((tm, d), lambda i: (i, 0)),
        out_shape=jax.ShapeDtypeStruct((tiles.n_all * tm, d), BF16),
        compiler_params=_params())(x2, ctx2, mod, norm_g)


def _norm_bwd(x2, ctx2, mod, norm_g, du_lat, du_b, gx1, tiles):
    tl, d = x2.shape
    nb, tm = tiles.nb, tiles.tm
    nrow = mod.shape[0]
    n_lat_in = len(du_lat)

    def body(x_ref, c_ref, mod_ref, g_ref, *refs):
        dl_refs = refs[:n_lat_in]
        d3_ref, gx_ref, gxo_ref, dmod_ref, dg_ref = refs[n_lat_in:]
        i = pl.program_id(0)

        @pl.when(i == 0)
        def _():
            dmod_ref[...] = jnp.zeros_like(dmod_ref)
            dg_ref[...] = jnp.zeros_like(dg_ref)

        lat = tiles.is_lat(i)
        xv = jnp.where(lat, x_ref[...], c_ref[...])
        row = jnp.where(lat, i // tiles.per_ex, nb)
        m = _rowsel(mod_ref[...], row, nb + 1)
        scale = m[:, d:2 * d]
        g = g_ref[...]
        dl = dl_refs[0][...]
        for ref in dl_refs[1:]:
            dl = dl + ref[...]
        du = jnp.where(tiles.is_pad(i), 0.0, d3_ref[...] + jnp.where(lat, dl, 0.0))
        rstd = lax.rsqrt(jnp.mean(xv * xv, axis=-1, keepdims=True) + EPS)
        xh = xv * rstd
        dshift = jnp.sum(du, axis=0, keepdims=True)
        dscale = jnp.sum(du * xh * g, axis=0, keepdims=True)
        dxn = du * (1.0 + scale)
        dg_ref[...] += jnp.sum(dxn * xh, axis=0, keepdims=True)
        dxh = dxn * g
        dx = rstd * (dxh - xh * jnp.mean(dxh * xh, axis=-1, keepdims=True))

        @pl.when(lat)
        def _():
            gxo_ref[...] = dx + gx_ref[...]

        for r in range(nb + 1):
            dmod_ref[r:r + 1, 0:d] += jnp.where(row == r, dshift, 0.0)
            dmod_ref[r:r + 1, d:2 * d] += jnp.where(row == r, dscale, 0.0)

    lat_map = lambda i: (tiles.lat_of_all(i), 0)
    lat_spec = pl.BlockSpec((tm, d), lat_map)
    return pl.pallas_call(
        body, name="norm_bwd", grid=(tiles.n_all,),
        in_specs=[lat_spec,
                  pl.BlockSpec((tm, d), lambda i: (tiles.ctx_of_all(i), 0)),
                  pl.BlockSpec(mod.shape, lambda i: (0, 0)),
                  pl.BlockSpec((1, d), lambda i: (0, 0))]
                 + [lat_spec] * n_lat_in
                 + [pl.BlockSpec((tm, d), lambda i: (i, 0)), lat_spec],
        out_specs=(lat_spec,
                   pl.BlockSpec((nrow, 3 * d), lambda i: (0, 0)),
                   pl.BlockSpec((1, d), lambda i: (0, 0))),
        out_shape=(jax.ShapeDtypeStruct((tl, d), F32), jax.ShapeDtypeStruct((nrow, 3 * d), F32),
                   jax.ShapeDtypeStruct((1, d), F32)),
        compiler_params=_params())(x2, ctx2, mod, norm_g, *du_lat, du_b, gx1)


def _matmul_bias(name, u, w, b, rows, tm, tn, u_tile=lambda i: i):
    d, n = w.shape

    def body(u_ref, w_ref, b_ref, o_ref):
        o_ref[...] = jnp.dot(u_ref[...], w_ref[...], preferred_element_type=F32) + b_ref[...]

    return pl.pallas_call(
        body, name=name, grid=(n // tn, rows // tm),
        in_specs=[pl.BlockSpec((tm, d), lambda j, i: (u_tile(i), 0)),
                  pl.BlockSpec((d, tn), lambda j, i: (0, j)),
                  pl.BlockSpec((1, tn), lambda j, i: (0, j))],
        out_specs=pl.BlockSpec((tm, tn), lambda j, i: (i, j)),
        out_shape=jax.ShapeDtypeStruct((rows, n), F32),
        compiler_params=_params())(u, w, b)


def _matmul_nt(name, a, w, koff, tm, tk, after=()):
    r, kc = a.shape
    d = w.shape[0]
    nk = kc // tk

    def body(a_ref, w_ref, *rest):
        o_ref = rest[len(after)]
        k = pl.program_id(1)
        p = lax.dot_general(a_ref[...], w_ref[...], (((1,), (1,)), ((), ())), preferred_element_type=F32)

        @pl.when(k == 0)
        def _():
            o_ref[...] = p

        @pl.when(k > 0)
        def _():
            o_ref[...] += p

    return pl.pallas_call(
        body, name=name, grid=(r // tm, nk),
        in_specs=[pl.BlockSpec((tm, tk), lambda i, k: (i, k)),
                  pl.BlockSpec((d, tk), lambda i, k: (0, koff + k))] + [_ANY] * len(after),
        out_specs=pl.BlockSpec((tm, d), lambda i, k: (i, 0)),
        out_shape=jax.ShapeDtypeStruct((r, d), F32),
        compiler_params=_params())(a, w, *after)


def _matmul_tn(name, a, b, rows, tk, tn):
    m = a.shape[1]
    n = b.shape[1]
    nk = rows // tk

    def body(a_ref, b_ref, o_ref, s_ref, acc_ref):
        k = pl.program_id(1)
        bv = b_ref[...]
        p = lax.dot_general(a_ref[...], bv, (((0,), (0,)), ((), ())), preferred_element_type=F32)
        cs = jnp.sum(bv.astype(F32), axis=0, keepdims=True)

        @pl.when(k == 0)
        def _():
            acc_ref[...] = p
            s_ref[...] = cs

        @pl.when(k > 0)
        def _():
            acc_ref[...] += p
            s_ref[...] += cs

        @pl.when(k == nk - 1)
        def _():
            o_ref[...] = acc_ref[...].astype(o_ref.dtype)

    return pl.pallas_call(
        body, name=name, grid=(n // tn, nk),
        in_specs=[pl.BlockSpec((tk, m), lambda j, k: (k, 0)),
                  pl.BlockSpec((tk, tn), lambda j, k: (k, j))],
        out_specs=(pl.BlockSpec((m, tn), lambda j, k: (0, j)), pl.BlockSpec((1, tn), lambda j, k: (0, j))),
        out_shape=(jax.ShapeDtypeStruct((m, n), BF16), jax.ShapeDtypeStruct((1, n), F32)),
        scratch_shapes=[pltpu.VMEM((m, tn), F32)],
        compiler_params=_params())(a, b)


def _matmul_tn_whole(name, a3, b3, rows, tn):
    nb, _, m = a3.shape
    n = b3.shape[2]

    def body(a_ref, b_ref, o_ref, s_ref):
        p, cs = None, None
        for e in range(nb):
            bv = b_ref[e]
            pe = lax.dot_general(a_ref[e], bv, (((0,), (0,)), ((), ())), preferred_element_type=F32)
            ce = jnp.sum(bv.astype(F32), axis=0, keepdims=True)
            p, cs = (pe, ce) if p is None else (p + pe, cs + ce)
        o_ref[...] = p.astype(o_ref.dtype)
        s_ref[...] = cs

    return pl.pallas_call(
        body, name=name, grid=(n // tn,),
        in_specs=[pl.BlockSpec((nb, rows, m), lambda j: (0, 0, 0)),
                  pl.BlockSpec((nb, rows, tn), lambda j: (0, 0, j))],
        out_specs=(pl.BlockSpec((m, tn), lambda j: (0, j)), pl.BlockSpec((1, tn), lambda j: (0, j))),
        out_shape=(jax.ShapeDtypeStruct((m, n), BF16), jax.ShapeDtypeStruct((1, n), F32)),
        compiler_params=_params())(a3, b3)


def _conv_window(pad_ref, r, shift, ktaps, width, horizontal):
    if horizontal:
        return pad_ref[r, pl.ds(16 + shift, width), :]
    return pad_ref[r + ktaps // 2 + shift]


def _conv_row(pad_ref, w, r, ktaps, width, horizontal, flip):
    half = ktaps // 2
    acc = None
    for t in range(ktaps):
        win = _conv_window(pad_ref, r, (half - t) if flip else (t - half), ktaps, width, horizontal)
        term = win * w[t:t + 1, :]
        acc = term if acc is None else acc + term
    return acc


def _fill_padded(ref, val, rows, width, ktaps, horizontal):
    half_k = ktaps // 2
    cb = val.shape[-1]
    if horizontal:
        ref[:, 0:16, :] = jnp.zeros((rows, 16, cb), F32)
        ref[:, 16 + width:32 + width, :] = jnp.zeros((rows, 16, cb), F32)
        ref[:, 16:16 + width, :] = val
    else:
        ref[0:half_k, :, :] = jnp.zeros((half_k, width, cb), F32)
        ref[half_k + rows:2 * half_k + rows, :, :] = jnp.zeros((half_k, width, cb), F32)
        ref[half_k:half_k + rows, :, :] = val


def _conv_fwd(pa, conv_w8, conv_b, nb, s):
    nblk, ktaps, cb = conv_w8.shape
    d = nblk * cb
    rows, width = s // GRID_W, GRID_W
    half_k = ktaps // 2
    nh = nblk // 2

    def body(glu_ref, w_ref, b_ref, o_ref, ph_ref, pv_ref):
        j = pl.program_id(1)
        a0 = (glu_ref[:, 0:cb] * _sigmoid(glu_ref[:, cb:2 * cb])).reshape(rows, width, cb)
        w = w_ref[...]

        bias = b_ref[...]

        def run(pad_ref, horizontal):
            _fill_padded(pad_ref, a0, rows, width, ktaps, horizontal)

            def row(r, carry):
                at = pl.ds(pl.multiple_of(r * width, width), width)
                o_ref[at, :] = _conv_row(pad_ref, w, r, ktaps, width, horizontal, False) + bias
                return carry

            lax.fori_loop(0, rows, row, 0)

        @pl.when(j < nh)
        def _():
            run(ph_ref, True)

        @pl.when(j >= nh)
        def _():
            run(pv_ref, False)

    return pl.pallas_call(
        body, name="conv_fwd", grid=(nb, nblk),
        in_specs=[pl.BlockSpec((s, 2 * cb), lambda b, j: (b, j)),
                  pl.BlockSpec((None, ktaps, cb), lambda b, j: (j, 0, 0)),
                  pl.BlockSpec((1, cb), lambda b, j: (0, j))],
        out_specs=pl.BlockSpec((s, cb), lambda b, j: (b, j)),
        out_shape=jax.ShapeDtypeStruct((nb * s, d), F32),
        scratch_shapes=[pltpu.VMEM((rows, width + 32, cb), F32), pltpu.VMEM((rows + 2 * half_k, width, cb), F32)],
        compiler_params=_params())(pa, conv_w8, conv_b)


def _conv_bwd(pa, da1, conv_w8, nb, s):
    nblk, ktaps, cb = conv_w8.shape
    d = nblk * cb
    rows, width = s // GRID_W, GRID_W
    half_k = ktaps // 2
    nh = nblk // 2

    def body(glu_ref, da_ref, w_ref, dp_ref, dw_ref, db_ref, pha_ref, phd_ref, pva_ref, pvd_ref):
        j = pl.program_id(0)
        b = pl.program_id(1)
        a0 = (glu_ref[:, 0:cb] * _sigmoid(glu_ref[:, cb:2 * cb])).reshape(rows, width, cb)
        da1v = da_ref[...]
        d3 = da1v.reshape(rows, width, cb)
        w = w_ref[...]

        @pl.when(b == 0)
        def _():
            dw_ref[...] = jnp.zeros_like(dw_ref)
            db_ref[...] = jnp.zeros_like(db_ref)

        db_ref[...] += jnp.sum(da1v, axis=0, keepdims=True)

        def run(pa_ref, pd_ref, horizontal):
            _fill_padded(pa_ref, a0, rows, width, ktaps, horizontal)
            _fill_padded(pd_ref, d3, rows, width, ktaps, horizontal)

            def row(r, accs):
                at = pl.ds(pl.multiple_of(r * width, width), width)
                da0 = _conv_row(pd_ref, w, r, ktaps, width, horizontal, True)
                gv = glu_ref[at, 0:cb]
                sg = _sigmoid(glu_ref[at, cb:2 * cb])
                dp_ref[at, 0:cb] = (da0 * sg).astype(BF16)
                dp_ref[at, cb:2 * cb] = (da0 * gv * sg * (1.0 - sg)).astype(BF16)
                d_row = da_ref[at, :]
                out = []
                for t in range(ktaps):
                    prod = _conv_window(pa_ref, r, t - half_k, ktaps, width, horizontal) * d_row
                    out.append(accs[t] + jnp.sum(prod.reshape(width // 8, 8, cb), axis=0))
                return tuple(out)

            accs = lax.fori_loop(0, rows, row, tuple(jnp.zeros((8, cb), F32) for _ in range(ktaps)))
            for t in range(ktaps):
                dw_ref[t:t + 1, :] += jnp.sum(accs[t], axis=0, keepdims=True)

        @pl.when(j < nh)
        def _():
            run(pha_ref, phd_ref, True)

        @pl.when(j >= nh)
        def _():
            run(pva_ref, pvd_ref, False)

    return pl.pallas_call(
        body, name="conv_bwd", grid=(nblk, nb),
        in_specs=[pl.BlockSpec((s, 2 * cb), lambda j, b: (b, j)),
                  pl.BlockSpec((s, cb), lambda j, b: (b, j)),
                  pl.BlockSpec((None, ktaps, cb), lambda j, b: (j, 0, 0))],
        out_specs=(pl.BlockSpec((s, 2 * cb), lambda j, b: (b, j)),
                   pl.BlockSpec((None, ktaps, cb), lambda j, b: (j, 0, 0)),
                   pl.BlockSpec((1, cb), lambda j, b: (0, j))),
        out_shape=(jax.ShapeDtypeStruct((nb * s, 2 * d), BF16),
                   jax.ShapeDtypeStruct((nblk, ktaps, cb), F32), jax.ShapeDtypeStruct((1, d), F32)),
        scratch_shapes=[pltpu.VMEM((rows, width + 32, cb), F32), pltpu.VMEM((rows, width + 32, cb), F32),
                        pltpu.VMEM((rows + 2 * half_k, width, cb), F32),
                        pltpu.VMEM((rows + 2 * half_k, width, cb), F32)],
        compiler_params=_params())(pa, da1, conv_w8)


def _log_sigmoid(x):
    return jnp.minimum(x, 0.0) - jnp.log(1.0 + jnp.exp(-jnp.abs(x)))


def _decay_fwd(pb, up2, bias2, tm, lr_blk):
    t_all = pb.shape[0]
    n2 = up2.shape[1]

    def body(lr_ref, up_ref, b_ref, g_ref):
        logits = _mm(lr_ref[...], up_ref[...]) + b_ref[...]
        g_ref[...] = _log_sigmoid(logits) * (1.0 / GATE_TAU)

    return pl.pallas_call(
        body, name="decay_fwd", grid=(t_all // tm,),
        in_specs=[pl.BlockSpec((tm, LANE), lambda i: (i, lr_blk)),
                  pl.BlockSpec(up2.shape, lambda i: (0, 0)),
                  pl.BlockSpec((1, n2), lambda i: (0, 0))],
        out_specs=pl.BlockSpec((tm, n2), lambda i: (i, 0)),
        out_shape=jax.ShapeDtypeStruct((t_all, n2), F32),
        compiler_params=_params())(pb, up2, bias2)


def _decay_bwd(pb, up2, bias2, grads_f, grads_b, tiles, lr_blk, dk_, dv_):
    t_all = pb.shape[0]
    tm = tiles.tm
    n2 = up2.shape[1]
    nbw = 2 * dk_ + dv_ + LANE

    def body(lr_ref, up_ref, b_ref, dqf, dkf, dvf, dgf, dqb, dkb, dvb, dgb, dp_ref, dup_ref, dbias_ref):
        i = pl.program_id(0)
        pad = tiles.is_pad(i)
        live = lambda v: jnp.where(pad, 0.0, v)

        @pl.when(i == 0)
        def _():
            dup_ref[...] = jnp.zeros_like(dup_ref)
            dbias_ref[...] = jnp.zeros_like(dbias_ref)

        lr = lr_ref[...]
        up = up_ref[...]
        logits = _mm(lr, up) + b_ref[...]
        dg = live(jnp.concatenate([dgf[...], dgb[...]], axis=1))
        dlog = dg * (1.0 / GATE_TAU) * _sigmoid(-logits)
        dup_ref[...] += _mm_tn(lr, dlog)
        dbias_ref[...] += jnp.sum(dlog, axis=0, keepdims=True)
        both = lambda f, b: live(f[...].astype(F32) + b[...].astype(F32)).astype(BF16)
        dp_ref[:, 0:dk_] = both(dqf, dqb)
        dp_ref[:, dk_:2 * dk_] = both(dkf, dkb)
        dp_ref[:, 2 * dk_:2 * dk_ + dv_] = both(dvf, dvb)
        dp_ref[:, 2 * dk_ + dv_:nbw] = _mm_nt(dlog, up).astype(BF16)

    row = lambda w: pl.BlockSpec((tm, w), lambda i: (i, 0))
    return pl.pallas_call(
        body, name="decay_bwd", grid=(t_all // tm,),
        in_specs=[pl.BlockSpec((tm, LANE), lambda i: (i, lr_blk)),
                  pl.BlockSpec(up2.shape, lambda i: (0, 0)),
                  pl.BlockSpec((1, n2), lambda i: (0, 0)),
                  row(dk_), row(dk_), row(dv_), row(dk_), row(dk_), row(dk_), row(dv_), row(dk_)],
        out_specs=(row(nbw), pl.BlockSpec(up2.shape, lambda i: (0, 0)), pl.BlockSpec((1, n2), lambda i: (0, 0))),
        out_shape=(jax.ShapeDtypeStruct((t_all, nbw), BF16), jax.ShapeDtypeStruct(up2.shape, F32),
                   jax.ShapeDtypeStruct((1, n2), F32)),
        compiler_params=_params())(pb, up2, bias2, *grads_f, *grads_b)


def _scan_chunk(s, nl, nc, rev):
    if rev:
        return jnp.where(s < nc, nl + (nc - 1 - s), nl - 1 - (s - nc))
    return jnp.where(s < nc, nl + s, s - nc)


def _scan_lat_chunk(s, nl, nc, rev):
    first = nl - 1 if rev else 0
    return jnp.where(s < nc, first, _scan_chunk(s, nl, nc, rev))


def _tri_mm(m_bf, x):
    hi = x.astype(BF16)
    r1 = x - hi.astype(F32)
    mid = r1.astype(BF16)
    lo = (r1 - mid.astype(F32)).astype(BF16)
    dot = lambda p: jnp.dot(m_bf, p, preferred_element_type=F32)
    return dot(hi) + dot(mid) + dot(lo)


def _chunk_masks(c, rev):
    ii = lax.broadcasted_iota(jnp.int32, (c, c), 0)
    jj = lax.broadcasted_iota(jnp.int32, (c, c), 1)
    return ((ii <= jj), (ii >= jj)) if rev else ((ii >= jj), (ii <= jj))


def _chunk_terms(q, k, b, far, mid):
    bf, bm = b[far:far + 1, :], b[mid:mid + 1, :]
    e = jnp.exp(b)
    em = jnp.exp(b - bm)
    eim = jnp.exp(bm - b)
    ed = jnp.exp(bf - b)
    return dict(e=e, em=em, eim=eim, ed=ed, dec=jnp.exp(bf), qe=q * e, qem=q * em, kim=k * eim, kd=k * ed)


def _gla_fwd(pb3, g3, nb, s_len, c_len, dk_, dv_):
    c = CHUNK
    nl, nc = s_len // c, c_len // c
    ns = nl + nc
    hk, hv = dk_ // HEADS, dv_ // HEADS
    l_len = pb3.shape[1]
    scale = hk ** -0.5
    mid = c // 2

    def body(*refs):
        ins, outs, z_scr = refs[:8], refs[8:14], refs[14]
        s = pl.program_id(0)

        @pl.when(s == 0)
        def _():
            z_scr[...] = jnp.zeros_like(z_scr)

        qs = jnp.where(s >= nc, scale, 0.0)
        for di, rev in enumerate((False, True)):
            q_ref, k_ref, v_ref, g_ref = ins[4 * di:4 * di + 4]
            o_ref, zs_ref, b_ref = outs[3 * di:3 * di + 3]
            mask, _ = _chunk_masks(c, rev)
            m_bf = mask.astype(BF16)
            far = 0 if rev else c - 1
            for b in range(nb):
                bc = _tri_mm(m_bf, g_ref[b])
                b_ref[b] = bc
                for h in range(HEADS):
                    ks, vs = slice(h * hk, (h + 1) * hk), slice(h * hv, (h + 1) * hv)
                    zi = (di * nb + b) * HEADS + h
                    v = v_ref[b, :, vs]
                    t = _chunk_terms(q_ref[b, :, ks] * qs, k_ref[b, :, ks], bc[:, ks], far, mid)
                    a = jnp.where(mask, _mm_nt(t["qem"], t["kim"]), 0.0)
                    z = z_scr[zi]
                    zs_ref[0, b * HEADS + h] = z
                    o_ref[b, :, vs] = _mm(a, v) + _mm_nt(t["qe"], z)
                    z_scr[zi] = z * t["dec"] + _mm_tn(v, t["kd"])

    in_specs, out_specs, out_shape = [], [], []
    for di, rev in enumerate((False, True)):
        ch = functools.partial(_scan_chunk, nl=nl, nc=nc, rev=rev)
        lch = functools.partial(_scan_lat_chunk, nl=nl, nc=nc, rev=rev)
        in_specs += [pl.BlockSpec((nb, c, dk_), lambda s, ch=ch: (0, ch(s), 0)),
                     pl.BlockSpec((nb, c, dk_), lambda s, ch=ch: (0, ch(s), 1)),
                     pl.BlockSpec((nb, c, dv_), lambda s, ch=ch: (0, ch(s), 1)),
                     pl.BlockSpec((nb, c, dk_), lambda s, ch=ch, di=di: (0, ch(s), di))]
        out_specs += [pl.BlockSpec((nb, c, dv_), lambda s, lch=lch: (0, lch(s), 0)),
                      pl.BlockSpec((1, nb * HEADS, hv, hk), lambda s: (s, 0, 0, 0)),
                      pl.BlockSpec((nb, c, dk_), lambda s, ch=ch: (0, ch(s), 0))]
        out_shape += [jax.ShapeDtypeStruct((nb, s_len, dv_), F32),
                      jax.ShapeDtypeStruct((ns, nb * HEADS, hv, hk), F32),
                      jax.ShapeDtypeStruct((nb, l_len, dk_), F32)]
    return pl.pallas_call(
        body, name="gla_fwd", grid=(ns,), in_specs=in_specs, out_specs=tuple(out_specs), out_shape=tuple(out_shape),
        scratch_shapes=[pltpu.VMEM((2 * nb * HEADS, hv, hk), F32)],
        compiler_params=_params())(pb3, pb3, pb3, g3, pb3, pb3, pb3, g3)


def _gla_bwd(pb3, do3, fwd_saved, nb, s_len, c_len, dk_, dv_):
    c = CHUNK
    nl, nc = s_len // c, c_len // c
    ns = nl + nc
    hk, hv = dk_ // HEADS, dv_ // HEADS
    l_len = pb3.shape[1]
    scale = hk ** -0.5
    mid = c // 2
    zs_f, b_f, zs_b, b_b = fwd_saved

    def body(*refs):
        ins, outs, dz_scr = refs[:12], refs[12:20], refs[20]
        s = pl.program_id(0)
        step = ns - 1 - s

        @pl.when(s == 0)
        def _():
            dz_scr[...] = jnp.zeros_like(dz_scr)

        lat = step >= nc
        qs = jnp.where(lat, scale, 0.0)
        dmul = jnp.where(lat, 1.0, 0.0)
        for di, rev in enumerate((False, True)):
            q_ref, k_ref, v_ref, b_ref, do_ref, zs_ref = ins[6 * di:6 * di + 6]
            dq_ref, dk_ref, dv_ref, dg_ref = outs[4 * di:4 * di + 4]
            mask, mask_t = _chunk_masks(c, rev)
            mt_bf = mask_t.astype(BF16)
            far = 0 if rev else c - 1
            far_row = lax.broadcasted_iota(jnp.int32, (c, hk), 0) == far
            for b in range(nb):
                db_parts = []
                for h in range(HEADS):
                    ks, vs = slice(h * hk, (h + 1) * hk), slice(h * hv, (h + 1) * hv)
                    zi = (di * nb + b) * HEADS + h
                    v = v_ref[b, :, vs]
                    d_o = do_ref[b, :, vs] * dmul
                    t = _chunk_terms(q_ref[b, :, ks] * qs, k_ref[b, :, ks], b_ref[b, :, ks], far, mid)
                    qem, kim, qe, kd = t["qem"], t["kim"], t["qe"], t["kd"]
                    a_t = jnp.where(mask_t, _mm_nt(kim, qem), 0.0)
                    d_a = jnp.where(mask, _mm_nt(d_o, v), 0.0)
                    d_at = jnp.where(mask_t, _mm_nt(v, d_o), 0.0)
                    z = zs_ref[0, b * HEADS + h]
                    dzn = dz_scr[zi]
                    dv_ref[b, :, vs] = (_mm(a_t, d_o) + _mm_nt(kd, dzn)).astype(dv_ref.dtype)
                    dqem = _mm(d_a, kim)
                    dkim = _mm(d_at, qem)
                    dqe = _mm(d_o, z)
                    dkd = _mm(v, dzn)
                    ddec = jnp.sum(z * dzn, axis=0, keepdims=True)
                    dz_scr[zi] = dzn * t["dec"] + _mm_tn(d_o, qe)
                    dq_ref[b, :, ks] = ((dqem * t["em"] + dqe * t["e"]) * qs).astype(dq_ref.dtype)
                    dk_ref[b, :, ks] = (dkim * t["eim"] + dkd * t["ed"]).astype(dk_ref.dtype)
                    db = dqem * qem - dkim * kim + dqe * qe - dkd * kd
                    extra = jnp.sum(dkd * kd, axis=0, keepdims=True) + ddec * t["dec"]
                    db_parts.append(db + jnp.where(far_row, extra, 0.0))
                dg_ref[b] = _tri_mm(mt_bf, jnp.concatenate(db_parts, axis=1))

    in_specs, out_specs, out_shape, args = [], [], [], []
    for di, rev in enumerate((False, True)):
        ch = lambda s, rev=rev: _scan_chunk(ns - 1 - s, nl, nc, rev)
        lch = lambda s, rev=rev: _scan_lat_chunk(ns - 1 - s, nl, nc, rev)
        in_specs += [pl.BlockSpec((nb, c, dk_), lambda s, ch=ch: (0, ch(s), 0)),
                     pl.BlockSpec((nb, c, dk_), lambda s, ch=ch: (0, ch(s), 1)),
                     pl.BlockSpec((nb, c, dv_), lambda s, ch=ch: (0, ch(s), 1)),
                     pl.BlockSpec((nb, c, dk_), lambda s, ch=ch: (0, ch(s), 0)),
                     pl.BlockSpec((nb, c, dv_), lambda s, lch=lch: (0, lch(s), 0)),
                     pl.BlockSpec((1, nb * HEADS, hv, hk), lambda s: (ns - 1 - s, 0, 0, 0))]
        args += [pb3, pb3, pb3, (b_b if rev else b_f), do3, (zs_b if rev else zs_f)]
        for w, dt in ((dk_, BF16), (dk_, BF16), (dv_, BF16), (dk_, F32)):
            out_specs.append(pl.BlockSpec((nb, c, w), lambda s, ch=ch: (0, ch(s), 0)))
            out_shape.append(jax.ShapeDtypeStruct((nb, l_len, w), dt))
    return pl.pallas_call(
        body, name="gla_bwd", grid=(ns,), in_specs=in_specs, out_specs=tuple(out_specs), out_shape=tuple(out_shape),
        scratch_shapes=[pltpu.VMEM((2 * nb * HEADS, hv, hk), F32)],
        compiler_params=_params())(*args)


def _tail(a1, pa, o_f, o_b, x2, tgt, mod, wc, wg, wo, ln_g, ln_b, gn_t, fg, nb, tm):
    tl, d = x2.shape
    nt = tl // tm
    per_ex = nt // nb
    hv = d // HEADS
    nrow = mod.shape[0]

    def body(a1_ref, z_ref, r_ref, mc_ref, mg_ref, of_ref, ob_ref, x_ref, t_ref, mod_ref, wc_ref, wg_ref, wo_ref,
             lng_ref, lnb_ref, gn_ref, fg_ref,
             dp_ref, da1_ref, do_ref, gx_ref, mrg_ref, dmo_ref, yci_ref, dyc_ref, ogi_ref, dyg_ref, sm_ref):
        i = pl.program_id(0)

        @pl.when(i == 0)
        def _():
            sm_ref[...] = jnp.zeros_like(sm_ref)

        bidx = i // per_ex
        gate = _rowsel(mod_ref[...], bidx, nb)[:, 2 * d:3 * d]
        lng, lnb, fgv = lng_ref[...], lnb_ref[...], fg_ref[...]
        gn = jnp.concatenate([gn_ref[...]] * HEADS, axis=1)
        wc_, wg_, wo_ = wc_ref[...], wg_ref[...], wo_ref[...]

        a1v = a1_ref[...]
        mu = jnp.mean(a1v, axis=-1, keepdims=True)
        xc = a1v - mu
        rs = lax.rsqrt(jnp.mean(xc * xc, axis=-1, keepdims=True) + EPS)
        xh = xc * rs
        a2 = xh * lng + lnb
        s2 = _sigmoid(a2)
        a3 = a2 * s2
        zv = z_ref[...]
        sz = _sigmoid(zv)
        siluz = zv * sz
        ycin = a3 * siluz
        yconv = _mm(ycin, wc_)

        o = of_ref[...] + ob_ref[...]
        ohat_parts, rn_parts = [], []
        for h in range(HEADS):
            oh = o[:, h * hv:(h + 1) * hv]
            rn = lax.rsqrt(jnp.mean(oh * oh, axis=-1, keepdims=True) + EPS)
            ohat_parts.append(oh * rn)
            rn_parts.append(rn)
        ohat = jnp.concatenate(ohat_parts, axis=1)
        on = ohat * gn
        rv = r_ref[...]
        sr = _sigmoid(rv)
        silur = rv * sr
        ogin = on * silur
        ygla = _mm(ogin, wg_)

        sc = _sigmoid(mc_ref[...])
        sg = _sigmoid(mg_ref[...])
        merged = sc * yconv + sg * ygla
        mo = _mm(merged, wo_)
        hn = x_ref[...] + gate * mo
        rf = lax.rsqrt(jnp.mean(hn * hn, axis=-1, keepdims=True) + EPS)
        yh = hn * rf
        err = yh * fgv - t_ref[...]
        loss_part = 0.5 * jnp.sum(err * err) * (1.0 / d)

        dy = err * (1.0 / d)
        dfg = jnp.sum(dy * yh, axis=0, keepdims=True)
        dyh = dy * fgv
        dhn = rf * (dyh - yh * jnp.mean(dyh * yh, axis=-1, keepdims=True))
        gx_ref[...] = dhn
        dgate = jnp.sum(dhn * mo, axis=0, keepdims=True)
        dmo = gate * dhn
        dmerged = _mm_nt(dmo, wo_)
        dyconv = dmerged * sc
        dygla = dmerged * sg
        dp_ref[:, 2 * d:3 * d] = (dmerged * yconv * sc * (1.0 - sc)).astype(BF16)
        dp_ref[:, 3 * d:4 * d] = (dmerged * ygla * sg * (1.0 - sg)).astype(BF16)
        dycin = _mm_nt(dyconv, wc_)
        dogin = _mm_nt(dygla, wg_)
        mrg_ref[...] = merged.astype(BF16)
        dmo_ref[...] = dmo.astype(BF16)
        yci_ref[...] = ycin.astype(BF16)
        dyc_ref[...] = dyconv.astype(BF16)
        ogi_ref[...] = ogin.astype(BF16)
        dyg_ref[...] = dygla.astype(BF16)

        da3 = dycin * siluz
        dp_ref[:, 0:d] = (dycin * a3 * _dsilu(zv, sz)).astype(BF16)
        da2 = da3 * _dsilu(a2, s2)
        dlng = jnp.sum(da2 * xh, axis=0, keepdims=True)
        dlnb = jnp.sum(da2, axis=0, keepdims=True)
        dxh = da2 * lng
        da1_ref[...] = rs * (dxh - jnp.mean(dxh, axis=-1, keepdims=True)
                             - xh * jnp.mean(dxh * xh, axis=-1, keepdims=True))

        don = dogin * silur
        dp_ref[:, d:2 * d] = (dogin * on * _dsilu(rv, sr)).astype(BF16)
        dgn = jnp.sum(don * ohat, axis=0, keepdims=True)
        dyn = don * gn
        for h in range(HEADS):
            vs = slice(h * hv, (h + 1) * hv)
            oh_hat = ohat_parts[h]
            dh = dyn[:, vs]
            do_ref[:, vs] = (rn_parts[h] * (dh - oh_hat * jnp.mean(dh * oh_hat, axis=-1, keepdims=True))
                             ).astype(BF16)

        sm_ref[0:1, :] += dfg
        sm_ref[1:2, :] += dlng
        sm_ref[2:3, :] += dlnb
        sm_ref[3:4, :] += dgn
        sm_ref[4:5, :] += jnp.zeros((1, d), F32) + loss_part
        for b in range(nb):
            sm_ref[8 + b:9 + b, :] += jnp.where(bidx == b, dgate, 0.0)

    row = pl.BlockSpec((tm, d), lambda i: (i, 0))
    pcol = lambda blk: pl.BlockSpec((tm, d), lambda i: (i, blk))
    full = lambda arr: pl.BlockSpec(arr.shape, lambda i: (0,) * arr.ndim)
    bfo = jax.ShapeDtypeStruct((tl, d), BF16)
    f32o = jax.ShapeDtypeStruct((tl, d), F32)
    return pl.pallas_call(
        body, name="tail", grid=(nt,),
        in_specs=[row, pcol(2), pcol(3), pcol(4), pcol(5), row, row, row, row, full(mod), full(wc), full(wg),
                  full(wo), full(ln_g), full(ln_b), full(gn_t), full(fg)],
        out_specs=(pl.BlockSpec((tm, 4 * d), lambda i: (i, 0)), row, row, row, row, row, row, row, row, row,
                   pl.BlockSpec((16, d), lambda i: (0, 0))),
        out_shape=(jax.ShapeDtypeStruct((tl, 4 * d), BF16), f32o, bfo, f32o, bfo, bfo, bfo, bfo, bfo, bfo,
                   jax.ShapeDtypeStruct((16, d), F32)),
        compiler_params=_params())(a1, pa, pa, pa, pa, o_f, o_b, x2, tgt, mod, wc, wg, wo, ln_g, ln_b, gn_t, fg)


def _local_step(x, c, ctx, tgt, c_ctx, ada_w8, ada_b, norm_g, w_a, b_a, w_b, b_b, conv_w8, conv_b, ln_g, ln_b,
                up2, bias2, gla_norm_g, final_norm_g, proj, on_grads=None, on_du_a1=None):
    nb, s_len, d = x.shape
    c_len = ctx.shape[1]
    dk_, dv_ = d // 2, d
    tl, tc = nb * s_len, nb * c_len
    nbw = 2 * dk_ + dv_ + LANE
    tm = math.gcd(256, c_len)
    tiles = _Tiles(nb, s_len, c_len, tm, 2)
    tmm = tiles.big * tm
    l_len = tiles.rows_per_ex
    t_all = nb * l_len
    x2, ctx2, tgt2 = x.reshape(tl, d), ctx.reshape(tc, d), tgt.reshape(tl, d)

    cv = jnp.zeros((8, d), F32).at[0:nb].set(c).at[nb].set(c_ctx.reshape(d))
    mod = _ada_fwd(cv, ada_w8, ada_b)
    u = _norm_fwd(x2, ctx2, mod, norm_g, tiles)
    pa = _matmul_bias("inproj_a", u, w_a, b_a, tl, tmm, _tile(6 * d, 3072), u_tile=tiles.big_all_of_lat)
    pb = _matmul_bias("inproj_b", u, w_b, b_b, t_all, tmm, nbw)

    a1 = _conv_fwd(pa, conv_w8, conv_b, nb, s_len)
    lr_blk = (2 * dk_ + dv_) // LANE
    g_all = _decay_fwd(pb, up2, bias2, tm, lr_blk)
    pb3 = pb.reshape(nb, l_len, nbw)
    o_f, zs_f, b_f, o_b, zs_b, b_b2 = _gla_fwd(pb3, g_all.reshape(nb, l_len, 2 * dk_), nb, s_len, c_len, dk_, dv_)

    conv_proj, gla_proj, w_out = proj(a1) if callable(proj) else proj
    tt = math.gcd(128, s_len)
    (dp_a2, da1, d_o, gx1, merged, dmo, ycin, dyconv, ogin, dygla, small) = _tail(
        a1, pa, o_f.reshape(tl, dv_), o_b.reshape(tl, dv_), x2, tgt2, mod, conv_proj, gla_proj, w_out, ln_g, ln_b,
        gla_norm_g, final_norm_g, nb, tt)

    lat3 = lambda a: a.reshape(nb, s_len, a.shape[-1])
    tnw = _tile(d, 1024)
    tnp = _tile(d, 512)
    d_w_out, _ = _matmul_tn_whole("dw_out", lat3(merged), lat3(dmo), s_len, tnp)
    d_conv_proj, _ = _matmul_tn_whole("dw_conv_proj", lat3(ycin), lat3(dyconv), s_len, tnp)
    d_gla_proj, _ = _matmul_tn_whole("dw_gla_proj", lat3(ogin), lat3(dygla), s_len, tnp)

    dp_a1, d_conv_w8, d_conv_b = _conv_bwd(pa, da1, conv_w8, nb, s_len)
    gl = _gla_bwd(pb3, d_o.reshape(nb, s_len, dv_), (zs_f, b_f, zs_b, b_b2), nb, s_len, c_len, dk_, dv_)
    gl = [g_.reshape(t_all, g_.shape[-1]) for g_ in gl]
    dp_b, d_up2, d_bias2 = _decay_bwd(pb, up2, bias2, gl[0:4], gl[4:8], tiles, lr_blk, dk_, dv_)

    u3 = u.reshape(nb, l_len, d)
    dw_a1, db_a1 = _matmul_tn_whole("dw_a1", u3, lat3(dp_a1), s_len, tnw)
    dw_a2, db_a2 = _matmul_tn_whole("dw_a2", u3, lat3(dp_a2), s_len, tnw)
    dw_b, db_b = _matmul_tn("dw_b", u, dp_b, t_all, tmm, nbw)
    grads = dict(w_a1=dw_a1, w_a2=dw_a2, w_b=dw_b, conv_w8=d_conv_w8, conv_proj=d_conv_proj, up2=d_up2,
                 gla_proj=d_gla_proj, w_out=d_w_out)

    tka = _tile(2 * d, 2048)
    du_a1 = _matmul_nt("du_a1", dp_a1, w_a, 0, tmm, tka, after=on_grads(grads) if on_grads else ())
    du_a2 = _matmul_nt("du_a2", dp_a2, w_a, (2 * d) // tka, tmm, tka, after=on_du_a1(du_a1) if on_du_a1 else ())
    du_b = _matmul_nt("du_b", dp_b, w_b, 0, tmm, nbw)
    grad_x2, dmod_ss, d_norm_g = _norm_bwd(x2, ctx2, mod, norm_g, [du_a1, du_a2], du_b, gx1, tiles)
    d_ada_w8, d_ada_b, d_cv = _ada_bwd(cv, ada_w8, dmod_ss, small, nb)

    return dict(
        grads, grad_x=grad_x2.reshape(nb, s_len, d), small=small, cv=d_cv, ada_w8=d_ada_w8, ada_b=d_ada_b,
        norm_g=d_norm_g, b_a1=db_a1, b_a2=db_a2, b_b=db_b, conv_b=d_conv_b, bias2=d_bias2)


def _regroup_pieces(d, r, wshard):
    cb = d // N_DEV
    segs = []
    for j in range(N_DEV):
        segs.append((j * cb, cb, 0, 2 * j * cb))
    for j in range(N_DEV):
        segs.append((d + j * cb, cb, 0, (2 * j + 1) * cb))
    segs += [(2 * d, d, 0, 2 * d), (3 * d, 2 * d + 2 * r, 1, 0), (5 * d + 2 * r, 3 * d, 0, 3 * d)]
    pieces = []
    for o0, w, dst, d0 in segs:
        lo = o0
        while lo < o0 + w:
            j = lo // wshard
            hi = min(o0 + w, (j + 1) * wshard)
            pieces.append((j, lo - j * wshard, hi - lo, dst, d0 + lo - o0))
            lo = hi
    return pieces


def _regroup(o, d, r):
    n_in = 8 * d + 2 * r
    parts = ([], [])
    for _, s0, n, dst, _ in sorted(_regroup_pieces(d, r, n_in), key=lambda p: (p[3], p[4])):
        parts[dst].append(o[..., s0:s0 + n])
    pad = jnp.zeros(o.shape[:-1] + (LANE - 2 * r,), o.dtype)
    return jnp.concatenate(parts[0], axis=-1), jnp.concatenate(parts[1] + [pad], axis=-1)


def _unshard_w_in(g_win, d, r, after=()):
    n_sh, _, ws = g_win.shape
    nbw = 2 * d + LANE
    pieces = _regroup_pieces(d, r, ws)
    tr = math.gcd(d, 256)

    def body(g_ref, *rest):
        a_ref, b_ref = rest[len(after):]
        dsts = (a_ref, b_ref)
        for j, s0, n, dst, d0 in pieces:
            dsts[dst][:, pl.ds(d0, n)] = g_ref[j, :, pl.ds(s0, n)]
        b_ref[:, pl.ds(2 * d + 2 * r, LANE - 2 * r)] = jnp.zeros((tr, LANE - 2 * r), b_ref.dtype)

    return pl.pallas_call(
        body, name="unshard_w_in", grid=(d // tr,),
        in_specs=[pl.BlockSpec((n_sh, tr, ws), lambda i: (0, i, 0))] + [_ANY] * len(after),
        out_specs=(pl.BlockSpec((tr, 6 * d), lambda i: (i, 0)), pl.BlockSpec((tr, nbw), lambda i: (i, 0))),
        out_shape=(jax.ShapeDtypeStruct((d, 6 * d), g_win.dtype), jax.ShapeDtypeStruct((d, nbw), g_win.dtype)),
        compiler_params=_params())(g_win, *after)


def _reshard_w_in(dw_a1, dw_a2, dw_b, d, r):
    ws = (8 * d + 2 * r) // N_DEV
    pieces = _regroup_pieces(d, r, ws)
    tr = math.gcd(d, 256)

    def body(a1_ref, a2_ref, b_ref, o_ref):
        for j, s0, n, dst, d0 in pieces:
            if dst == 1:
                src = b_ref[:, pl.ds(d0, n)]
            elif d0 < 2 * d:
                src = a1_ref[:, pl.ds(d0, n)]
            else:
                src = a2_ref[:, pl.ds(d0 - 2 * d, n)]
            o_ref[j, :, pl.ds(s0, n)] = src

    row = lambda w: pl.BlockSpec((tr, w), lambda i: (i, 0))
    return pl.pallas_call(
        body, name="reshard_w_in", grid=(d // tr,),
        in_specs=[row(2 * d), row(4 * d), row(2 * d + LANE)],
        out_specs=pl.BlockSpec((N_DEV, tr, ws), lambda i: (0, i, 0)),
        out_shape=jax.ShapeDtypeStruct((N_DEV, d, ws), dw_b.dtype),
        compiler_params=_params())(dw_a1, dw_a2, dw_b)


_SMALL = ("c_ctx", "ada_b", "norm_g", "b_in", "conv_b", "conv_ln_g", "conv_ln_b", "decay_bias_fwd",
          "decay_bias_bwd", "gla_norm_g", "final_norm_g")


def _small_layout(d, r):
    sizes = dict(c_ctx=d, ada_b=3 * d, norm_g=d, b_in=8 * d + 2 * r, conv_b=d, conv_ln_g=d, conv_ln_b=d,
                 decay_bias_fwd=d // 2, decay_bias_bwd=d // 2, gla_norm_g=d // HEADS, final_norm_g=d, loss=1)
    table, off = {}, 0
    for name in _SMALL + ("loss",):
        table[name] = (off, sizes[name])
        off += -(-sizes[name] // LANE) * LANE
    return table, off


def _pack_small(g, nb, d, r):
    table, width = _small_layout(d, r)
    hv = d // HEADS
    pieces = _regroup_pieces(d, r, 8 * d + 2 * r)
    names = ("small", "cv", "ada_b", "norm_g", "b_a1", "b_a2", "b_b", "conv_b", "bias2")

    def body(sm, cv, ab, ng, ba1, ba2, bb, cvb, b2, o_ref):
        o_ref[...] = jnp.zeros_like(o_ref)

        def put(name, val):
            off, n = table[name]
            o_ref[:, pl.ds(off, n)] = val

        put("c_ctx", cv[nb:nb + 1, :])
        put("ada_b", ab[...])
        put("norm_g", ng[...])
        off_b = table["b_in"][0]
        for _, s0, n, dst, d0 in pieces:
            if dst == 1:
                src = bb[:, pl.ds(d0, n)]
            elif d0 < 2 * d:
                src = ba1[:, pl.ds(d0, n)]
            else:
                src = ba2[:, pl.ds(d0 - 2 * d, n)]
            o_ref[:, pl.ds(off_b + s0, n)] = src
        put("conv_b", cvb[...])
        put("conv_ln_g", sm[1:2, :])
        put("conv_ln_b", sm[2:3, :])
        put("decay_bias_fwd", b2[:, 0:d // 2])
        put("decay_bias_bwd", b2[:, d // 2:d])
        gn = sm[3:4, 0:hv]
        for h in range(1, HEADS):
            gn = gn + sm[3:4, h * hv:(h + 1) * hv]
        put("gla_norm_g", gn)
        put("final_norm_g", sm[0:1, :])
        put("loss", sm[4:5, 0:1])

    return pl.pallas_call(body, name="pack_small", out_shape=jax.ShapeDtypeStruct((1, width), F32),
                          compiler_params=_params())(*[g[k] for k in names])


def _small_adam(parts, ws, ms, vs, d, r):
    table, width = _small_layout(d, r)
    n_parts = parts.shape[0]
    k = len(_SMALL)
    bc1 = 1.0 - ADAM_B1 ** ADAM_STEP
    bc2 = 1.0 - ADAM_B2 ** ADAM_STEP

    def body(p_ref, *refs):
        w_refs, m_refs, v_refs = refs[0:k], refs[k:2 * k], refs[2 * k:3 * k]
        outs = refs[3 * k:]
        tot = p_ref[0]
        for i in range(1, n_parts):
            tot = tot + p_ref[i]
        for i, name in enumerate(_SMALL):
            off, n = table[name]
            g = tot[:, off:off + n]
            mn = ADAM_B1 * m_refs[i][...] + (1.0 - ADAM_B1) * g
            vn = ADAM_B2 * v_refs[i][...] + (1.0 - ADAM_B2) * (g * g)
            outs[i][...] = g
            outs[k + i][...] = -ADAM_LR * ((mn / bc1) / (jnp.sqrt(vn / bc2) + ADAM_EPS) + ADAM_WD * w_refs[i][...])
            outs[2 * k + i][...] = mn
            outs[3 * k + i][...] = vn
        off, _ = table["loss"]
        outs[4 * k][...] = tot[:, off:off + 1]

    shapes = [jax.ShapeDtypeStruct(w.shape, F32) for w in ws]
    res = pl.pallas_call(body, name="small_adam", out_shape=tuple(shapes * 4 + [jax.ShapeDtypeStruct((1, 1), F32)]),
                         compiler_params=_params())(parts, *ws, *ms, *vs)
    return res[0:k], res[k:2 * k], res[2 * k:3 * k], res[3 * k:4 * k], res[4 * k]


def _mesh_pos():
    return lax.axis_index("x"), lax.axis_index("y"), lax.axis_index("c")


def _all_gather(arrs):
    n = len(arrs)

    def body(*refs):
        ins, outs = refs[:n], refs[n:2 * n]
        send_sems, recv_sems, local_sems = refs[2 * n:]
        x, y, c = _mesh_pos()
        me, sibling = (x, y, c), (x, y, 1 - c)
        chips = [(1 - x, y), (x, 1 - y), (1 - x, 1 - y)]

        def slot(a, pos):
            return outs[a].at[4 * pos[0] + 2 * pos[1] + pos[2]]

        def copy(a, k, block, to, src=None):
            return pltpu.make_async_remote_copy(
                src_ref=slot(a, block) if src is None else src, dst_ref=slot(a, block),
                send_sem=send_sems.at[7 * a + k], recv_sem=recv_sems.at[7 * a + k],
                device_id=to, device_id_type=MESH)

        mine = [pltpu.make_async_copy(ins[a], slot(a, me), local_sems.at[a]) for a in range(n)]
        for cp in mine:
            cp.start()
        first = []
        for a in range(n):
            first.append(copy(a, 0, me, sibling, src=ins[a]))
            first += [copy(a, 1 + j, me, (*chip, c), src=ins[a]) for j, chip in enumerate(chips)]
        for cp in first:
            cp.start()
        passed = []
        for j, chip in enumerate(chips):
            for a in range(n):
                copy(a, 1 + j, (*chip, c), me).wait_recv()
                fwd = copy(a, 4 + j, (*chip, c), sibling)
                fwd.start()
                passed.append(fwd)
        for a in range(n):
            copy(a, 0, sibling, me).wait_recv()
            for j, chip in enumerate(chips):
                copy(a, 4 + j, (*chip, 1 - c), me).wait_recv()
        for cp in first + passed:
            cp.wait_send()
        for cp in mine:
            cp.wait()

    return pl.pallas_call(
        body, name="all_gather",
        out_shape=tuple(jax.ShapeDtypeStruct((N_DEV,) + a.shape, a.dtype) for a in arrs),
        in_specs=[_ANY] * n, out_specs=tuple([_ANY] * n),
        scratch_shapes=[pltpu.SemaphoreType.DMA((7 * n,)), pltpu.SemaphoreType.DMA((7 * n,)),
                        pltpu.SemaphoreType.DMA((n,))],
    )(*arrs)


def _exchange_sibling(arrs):
    n = len(arrs)

    def body(*refs):
        ins, outs = refs[:n], refs[n:2 * n]
        send_sems, recv_sems = refs[2 * n:]
        x, y, c = _mesh_pos()
        copies = [pltpu.make_async_remote_copy(
            src_ref=ins[a].at[2 * k + (1 - c)], dst_ref=outs[a].at[k],
            send_sem=send_sems.at[4 * a + k], recv_sem=recv_sems.at[4 * a + k],
            device_id=(x, y, 1 - c), device_id_type=MESH) for a in range(n) for k in range(4)]
        for cp in copies:
            cp.start()
        for cp in copies:
            cp.wait_recv()
        for cp in copies:
            cp.wait_send()

    return pl.pallas_call(
        body, name="grad_exchange_sibling",
        out_shape=tuple(jax.ShapeDtypeStruct((4,) + a.shape[1:], a.dtype) for a in arrs),
        in_specs=[_ANY] * n, out_specs=tuple([_ANY] * n),
        scratch_shapes=[pltpu.SemaphoreType.DMA((4 * n,)), pltpu.SemaphoreType.DMA((4 * n,))],
    )(*arrs)


def _pair_sum(name, mine, theirs):
    _, r, cdim = mine.shape
    tr = r if (r % 8 or r <= 256) else math.gcd(r, 256)

    def body(m_ref, t_ref, o_ref):
        c = lax.axis_index("c")
        own = jnp.where(c == 0, m_ref[:, 0].astype(F32), m_ref[:, 1].astype(F32))
        o_ref[...] = (own + t_ref[...].astype(F32)).astype(o_ref.dtype)

    return pl.pallas_call(
        body, name=name, grid=(r // tr,),
        in_specs=[pl.BlockSpec((4, 2, tr, cdim), lambda i: (0, 0, i, 0)),
                  pl.BlockSpec((4, tr, cdim), lambda i: (0, i, 0))],
        out_specs=pl.BlockSpec((4, tr, cdim), lambda i: (0, i, 0)),
        out_shape=jax.ShapeDtypeStruct((4, r, cdim), mine.dtype),
        compiler_params=_params())(mine.reshape(4, 2, r, cdim), theirs)


def _exchange_chips(arrs):
    n = len(arrs)

    def body(*refs):
        ins, outs = refs[:n], refs[n:2 * n]
        send_sems, recv_sems, local_sems = refs[2 * n:]
        x, y, c = _mesh_pos()
        my_chip = 2 * x + y
        mine = [pltpu.make_async_copy(ins[a].at[my_chip], outs[a].at[my_chip], local_sems.at[a]) for a in range(n)]
        for cp in mine:
            cp.start()
        copies = []
        for rel in range(1, 4):
            px = 1 - x if rel & 2 else x
            py = 1 - y if rel & 1 else y
            for a in range(n):
                copies.append(pltpu.make_async_remote_copy(
                    src_ref=ins[a].at[2 * px + py], dst_ref=outs[a].at[my_chip],
                    send_sem=send_sems.at[3 * a + rel - 1], recv_sem=recv_sems.at[3 * a + rel - 1],
                    device_id=(px, py, c), device_id_type=MESH))
        for cp in copies:
            cp.start()
        for cp in copies:
            cp.wait_recv()
        for cp in copies:
            cp.wait_send()
        for cp in mine:
            cp.wait()

    return pl.pallas_call(
        body, name="grad_exchange_chips",
        out_shape=tuple(jax.ShapeDtypeStruct(a.shape, a.dtype) for a in arrs),
        in_specs=[_ANY] * n, out_specs=tuple([_ANY] * n),
        scratch_shapes=[pltpu.SemaphoreType.DMA((3 * n,)), pltpu.SemaphoreType.DMA((3 * n,)),
                        pltpu.SemaphoreType.DMA((n,))],
    )(*arrs)


_HBM = pl.BlockSpec(memory_space=pltpu.HBM)
_SEM = pl.BlockSpec(memory_space=pltpu.SEMAPHORE)


def _copies_start(name, srcs, lands, make_copies, n_sems):
    n, m = len(srcs), len(lands)

    def body(*refs):
        ins = refs[:n + m]
        send_sems, recv_sems = refs[n + m], refs[n + m + 1]
        for cp in make_copies(ins[:n], ins[n:], send_sems, recv_sems):
            cp.start()
        refs[-1][...] = jnp.zeros_like(refs[-1])

    res = pl.pallas_call(
        body, name=name,
        out_shape=(pltpu.SemaphoreType.DMA((n_sems,)), pltpu.SemaphoreType.DMA((n_sems,)),
                   *[pltpu.HBM(a.shape, a.dtype) for a in (*srcs, *lands)], jax.ShapeDtypeStruct((8, LANE), F32)),
        in_specs=[_HBM] * (n + m),
        out_specs=(_SEM, _SEM, *[_HBM] * (n + m), pl.BlockSpec(memory_space=pltpu.VMEM)),
        input_output_aliases={i: 2 + i for i in range(n + m)},
        compiler_params=pltpu.CompilerParams(has_side_effects=pltpu.SideEffectType.DATAFLOW_SIDE_EFFECTING),
    )(*[pltpu.with_memory_space_constraint(a, pltpu.HBM) for a in (*srcs, *lands)])
    return res[0], res[1], res[2:2 + n], res[2 + n:2 + n + m], res[-1]


def _copies_wait(name, started, after, make_copies):
    send_sems, recv_sems, srcs, lands, _ = started
    n, m = len(srcs), len(lands)

    def body(*refs):
        ins = refs[:n + m]
        for cp in make_copies(ins[:n], ins[n:], refs[n + m], refs[n + m + 1]):
            cp.wait_send()
            cp.wait_recv()

    res = pl.pallas_call(
        body, name=name,
        out_shape=tuple(pltpu.HBM(a.shape, a.dtype) for a in (*srcs, *lands)),
        in_specs=[_HBM] * (n + m) + [_SEM, _SEM] + [_ANY] * len(after),
        out_specs=tuple([_HBM] * (n + m)),
        input_output_aliases={i: i for i in range(n + m)},
        compiler_params=pltpu.CompilerParams(has_side_effects=pltpu.SideEffectType.DATAFLOW_SIDE_EFFECTING),
    )(*srcs, *lands, send_sems, recv_sems, *after)
    return res[:n], res[n:]


def _gather_copies(srcs, lands, send_sems, recv_sems):
    x, y, c = _mesh_pos()
    me_i = 4 * x + 2 * y + c
    copies = []
    for rel in range(1, N_DEV):
        peer = (1 - x if rel & 4 else x, 1 - y if rel & 2 else y, 1 - c if rel & 1 else c)
        for a in range(len(srcs)):
            copies.append(pltpu.make_async_remote_copy(
                src_ref=srcs[a], dst_ref=lands[a].at[me_i], send_sem=send_sems.at[7 * a + rel - 1],
                recv_sem=recv_sems.at[7 * a + rel - 1], device_id=peer, device_id_type=MESH))
    return copies


def _sibling_copies(srcs, lands, send_sems, recv_sems):
    x, y, c = _mesh_pos()
    return [pltpu.make_async_remote_copy(
        src_ref=srcs[a].at[2 * k + (1 - c)], dst_ref=lands[a].at[k], send_sem=send_sems.at[4 * a + k],
        recv_sem=recv_sems.at[4 * a + k], device_id=(x, y, 1 - c), device_id_type=MESH)
        for a in range(len(srcs)) for k in range(4)]


def _chip_copies(srcs, lands, send_sems, recv_sems):
    x, y, c = _mesh_pos()
    my_chip = 2 * x + y
    copies = []
    for rel in range(1, 4):
        px = 1 - x if rel & 2 else x
        py = 1 - y if rel & 1 else y
        for a in range(len(srcs)):
            copies.append(pltpu.make_async_remote_copy(
                src_ref=srcs[a].at[2 * px + py], dst_ref=lands[a].at[my_chip], send_sem=send_sems.at[3 * a + rel - 1],
                recv_sem=recv_sems.at[3 * a + rel - 1], device_id=(px, py, c), device_id_type=MESH))
    return copies


def _sum_adam(name, parts, w, m, v, own=None):
    r, cdim = w.shape
    n_parts = parts.shape[0]
    tr = r if (r % 8 or r <= 256) else math.gcd(r, 256)
    bc1 = 1.0 - ADAM_B1 ** ADAM_STEP
    bc2 = 1.0 - ADAM_B2 ** ADAM_STEP
    extra = [] if own is None else [own]

    def body(p_ref, *refs):
        w_ref, m_ref, v_ref, g_ref, d_ref, nm_ref, nv_ref = refs[len(extra):]
        if own is None:
            part = lambda k: p_ref[k].astype(F32)
        else:
            my_chip = 2 * lax.axis_index("x") + lax.axis_index("y")
            part = lambda k: jnp.where(my_chip == k, refs[0][k], p_ref[k]).astype(F32)
        g = part(0)
        for k in range(1, n_parts):
            g = g + part(k)
        mn = ADAM_B1 * m_ref[...] + (1.0 - ADAM_B1) * g
        vn = ADAM_B2 * v_ref[...] + (1.0 - ADAM_B2) * (g * g)
        g_ref[...] = g
        nm_ref[...] = mn
        nv_ref[...] = vn
        d_ref[...] = -ADAM_LR * ((mn / bc1) / (jnp.sqrt(vn / bc2) + ADAM_EPS) + ADAM_WD * w_ref[...])

    blk = pl.BlockSpec((tr, cdim), lambda i: (i, 0))
    o = jax.ShapeDtypeStruct((r, cdim), F32)
    return pl.pallas_call(
        body, name=name, grid=(r // tr,),
        in_specs=[pl.BlockSpec((n_parts, tr, cdim), lambda i: (0, i, 0))] * (1 + len(extra)) + [blk, blk, blk],
        out_specs=(blk, blk, blk, blk), out_shape=(o, o, o, o),
        compiler_params=_params())(parts, *extra, w, m, v)


_WEIGHTS = ("c_ctx", "ada_w", "ada_b", "norm_g", "w_in", "b_in", "conv_w", "conv_b", "conv_ln_g", "conv_ln_b",
            "conv_proj", "decay_up_fwd", "decay_bias_fwd", "decay_up_bwd", "decay_bias_bwd", "gla_norm_g",
            "gla_proj", "w_out", "final_norm_g")


def _as2d(a):
    if a.ndim == 1:
        return a.reshape(1, -1)
    return a.reshape(-1, a.shape[-1])


def kernel(x, c, ctx, c_ctx, ada_w, ada_b, norm_g, w_in, b_in, conv_w, conv_b, conv_ln_g, conv_ln_b, conv_proj, decay_up_fwd, decay_bias_fwd, decay_up_bwd, decay_bias_bwd, gla_norm_g, gla_proj, w_out, final_norm_g, loss_target, m_c_ctx, m_ada_w, m_ada_b, m_norm_g, m_w_in, m_b_in, m_conv_w, m_conv_b, m_conv_ln_g, m_conv_ln_b, m_conv_proj, m_decay_up_fwd, m_decay_bias_fwd, m_decay_up_bwd, m_decay_bias_bwd, m_gla_norm_g, m_gla_proj, m_w_out, m_final_norm_g, v_c_ctx, v_ada_w, v_ada_b, v_norm_g, v_w_in, v_b_in, v_conv_w, v_conv_b, v_conv_ln_g, v_conv_ln_b, v_conv_proj, v_decay_up_fwd, v_decay_bias_fwd, v_decay_up_bwd, v_decay_bias_bwd, v_gla_norm_g, v_gla_proj, v_w_out, v_final_norm_g):
    env = dict(locals())
    wts = {k: env[k] for k in _WEIGHTS}
    d = x.shape[-1]
    r = decay_up_fwd.shape[1]
    dk_ = d // 2
    n_in = w_in.shape[-1] * N_DEV

    ds, dks = d // N_DEV, dk_ // N_DEV
    g_win, g_ada, conv_w8, g_up = _all_gather(
        [w_in[0].astype(BF16), ada_w[0].astype(BF16), conv_w[0],
         jnp.concatenate([decay_up_fwd[0], decay_up_bwd[0]], axis=1)])
    proj_own = [conv_proj[0].astype(BF16), gla_proj[0].astype(BF16), w_out[0].astype(BF16)]
    me_i = 4 * lax.axis_index("x") + 2 * lax.axis_index("y") + lax.axis_index("c")
    proj_lands = [lax.dynamic_update_slice(lax.empty((N_DEV,) + a.shape, a.dtype), a[None], (me_i, 0, 0))
                  for a in proj_own]
    proj_start = _copies_start("proj_gather_start", proj_own, proj_lands, _gather_copies, 7 * 3)

    def proj(after):
        _, lands = _copies_wait("proj_gather_wait", proj_start, (after,), _gather_copies)
        return [w.reshape(d, d) for w in lands]

    w_a, w_b = _unshard_w_in(g_win, d, r, after=(proj_start[4],))
    up_f = g_up[:, :, 0:dks].transpose(1, 0, 2).reshape(r, dk_)
    up_b = g_up[:, :, dks:].transpose(1, 0, 2).reshape(r, dk_)
    up2 = jnp.zeros((LANE, 2 * dk_), F32).at[0:r, 0:dk_].set(up_f).at[r:2 * r, dk_:].set(up_b)
    bias2 = jnp.concatenate([decay_bias_fwd, decay_bias_bwd], axis=1)
    b_a, b_b = _regroup(b_in, d, r)

    names = ("w_in", "conv_proj", "gla_proj", "w_out", "conv_w", "decay_up")
    comm = {}

    def on_grads(gr):
        d_up = jnp.concatenate([gr["up2"][0:r, 0:dk_].reshape(r, N_DEV, dks).transpose(1, 0, 2),
                                gr["up2"][r:2 * r, dk_:].reshape(r, N_DEV, dks).transpose(1, 0, 2)], axis=2)
        mine = [_reshard_w_in(gr["w_a1"], gr["w_a2"], gr["w_b"], d, r), gr["conv_proj"].reshape(N_DEV, ds, d),
                gr["gla_proj"].reshape(N_DEV, ds, d), gr["w_out"].reshape(N_DEV, ds, d), gr["conv_w8"], d_up]
        lands = [lax.empty((4,) + a.shape[1:], a.dtype) for a in mine]
        comm["sibling"] = _copies_start("grad_sibling_start", mine, lands, _sibling_copies, 4 * len(mine))
        return (comm["sibling"][4],)

    def on_du_a1(du_a1):
        mine, theirs = _copies_wait("grad_sibling_wait", comm["sibling"], (du_a1,), _sibling_copies)
        sums = [_pair_sum("pair_sum_" + nm, a, b) for nm, a, b in zip(names, mine, theirs)]
        lands = [lax.empty(a.shape, a.dtype) for a in sums]
        comm["chips"] = _copies_start("grad_chips_start", sums, lands, _chip_copies, 3 * len(sums))
        return (comm["chips"][4],)

    g = _local_step(x, c, ctx, loss_target, c_ctx, g_ada, ada_b, norm_g[0:1], w_a, b_a, w_b, b_b,
                    conv_w8, conv_b, conv_ln_g, conv_ln_b, up2, bias2, gla_norm_g, final_norm_g.reshape(1, d),
                    proj, on_grads, on_du_a1)

    (their_ada,) = _exchange_sibling([g["ada_w8"]])
    ada_sum = _pair_sum("pair_sum_ada_w", g["ada_w8"], their_ada)
    ada_start = _copies_start("ada_chips_start", [ada_sum], [lax.empty(ada_sum.shape, ada_sum.dtype)],
                              _chip_copies, 3)
    own, landed = _copies_wait("grad_chips_wait", comm["chips"], (ada_start[4],), _chip_copies)
    o_win, o_cp, o_gp, o_wo, o_cw, o_up = own
    x_win, x_cp, x_gp, x_wo, x_cw, x_up = landed

    (packs,) = _all_gather([_pack_small(g, x.shape[0], d, r)])
    row = lambda a: a.reshape(1, -1)
    sg, sd, sm, sv, loss = _small_adam(packs, [row(wts[k]) for k in _SMALL], [row(env["m_" + k]) for k in _SMALL],
                                       [row(env["v_" + k]) for k in _SMALL], d, r)
    out = {}
    for i, k in enumerate(_SMALL):
        for pre, arrs in (("grad_", sg), ("delta_", sd), ("new_m_", sm), ("new_v_", sv)):
            out[pre + k] = arrs[i].reshape(wts[k].shape)
    loss = loss.reshape(())

    def big(name, parts, wname, own=None):
        w2 = _as2d(wts[wname])
        res = _sum_adam(name, parts, w2, _as2d(env["m_" + wname]), _as2d(env["v_" + wname]), own)
        for pre, arr in zip(("grad_", "delta_", "new_m_", "new_v_"), res):
            out[pre + wname] = arr.reshape(wts[wname].shape)

    big("adam_w_in", x_win, "w_in", o_win)
    big("adam_conv_proj", x_cp, "conv_proj", o_cp)
    big("adam_gla_proj", x_gp, "gla_proj", o_gp)
    big("adam_w_out", x_wo, "w_out", o_wo)
    big("adam_conv_w", x_cw, "conv_w", o_cw)
    big("adam_up_f", x_up[:, :, 0:dks], "decay_up_fwd", o_up[:, :, 0:dks])
    big("adam_up_b", x_up[:, :, dks:], "decay_up_bwd", o_up[:, :, dks:])
    (o_ada,), (x_ada,) = _copies_wait("ada_chips_wait", ada_start, (out["grad_w_in"], out["grad_w_out"], out["grad_b_in"]),
                                      _chip_copies)
    big("adam_ada_w", x_ada, "ada_w", o_ada)

    return (loss, g["grad_x"], *[out["grad_" + k] for k in _WEIGHTS], *[out["delta_" + k] for k in _WEIGHTS],
            *[out["new_m_" + k] for k in _WEIGHTS], *[out["new_v_" + k] for k in _WEIGHTS])
```

```python
import functools
import math

import jax
import jax.numpy as jnp
from jax import lax
from jax.experimental import pallas as pl
from jax.experimental.pallas import tpu as pltpu

F32 = jnp.float32
BF16 = jnp.bfloat16
MESH = pl.DeviceIdType.MESH

N_DEV = 8
GRID_W = 64
CHUNK = 128
HEADS = 4
EPS = 1e-6
GATE_TAU = 16.0
LANE = 128
ADAM_LR, ADAM_B1, ADAM_B2, ADAM_EPS, ADAM_WD, ADAM_STEP = 0.001, 0.9, 0.999, 1e-08, 0.01, 10
VMEM_LIMIT = 56 * 1024 * 1024
_ANY = pl.BlockSpec(memory_space=pl.ANY)


def _params(**kw):
    return pltpu.CompilerParams(vmem_limit_bytes=VMEM_LIMIT, **kw)


def _tile(n, pref):
    t = (min(pref, n) // LANE) * LANE
    while t >= LANE:
        if n % t == 0:
            return t
        t -= LANE
    return n


def _mm(a, b):
    return jnp.dot(a.astype(BF16), b.astype(BF16), preferred_element_type=F32)


def _mm_nt(a, b):
    return lax.dot_general(a.astype(BF16), b.astype(BF16), (((1,), (1,)), ((), ())), preferred_element_type=F32)


def _mm_tn(a, b):
    return lax.dot_general(a.astype(BF16), b.astype(BF16), (((0,), (0,)), ((), ())), preferred_element_type=F32)


def _mm_tn_hi(a, b):
    return lax.dot_general(a, b, (((0,), (0,)), ((), ())), precision=lax.Precision.HIGHEST, preferred_element_type=F32)


def _sigmoid(x):
    return 0.5 * jnp.tanh(0.5 * x) + 0.5


def _dsilu(x, s):
    return s * (1.0 + x * (1.0 - s))


def _rowsel(table, idx, n):
    out = table[0:1, :]
    for r in range(1, n):
        out = jnp.where(idx == r, table[r:r + 1, :], out)
    return out


def _ada_fwd(cv, ada_w8, ada_b):
    n_sh, _, ws = ada_w8.shape

    def body(cv_ref, w_ref, b_ref, o_ref):
        c = cv_ref[...]
        sv = c * _sigmoid(c)
        for j in range(n_sh):
            cols = pl.ds(j * ws, ws)
            o_ref[:, cols] = _mm(sv, w_ref[j]) + b_ref[:, cols]

    return pl.pallas_call(body, name="ada_fwd", out_shape=jax.ShapeDtypeStruct((cv.shape[0], n_sh * ws), F32),
                          compiler_params=_params())(cv, ada_w8, ada_b)


def _ada_bwd(cv, ada_w8, dmod_ss, small, nb):
    n_sh, d, ws = ada_w8.shape

    def body(cv_ref, w_ref, dm_ref, sm_ref, dw_ref, db_ref, dc_ref):
        c = cv_ref[...]
        s = _sigmoid(c)
        sv = c * s
        dm = jnp.concatenate([dm_ref[:, 0:2 * d], sm_ref[8:16, :]], axis=1)
        db_ref[...] = jnp.sum(dm, axis=0, keepdims=True)
        dsv = None
        for j in range(n_sh):
            dmj = dm[:, j * ws:(j + 1) * ws]
            dw_ref[j] = _mm_tn_hi(sv, dmj).astype(dw_ref.dtype)
            part = _mm_nt(dmj, w_ref[j])
            dsv = part if dsv is None else dsv + part
        dc_ref[...] = dsv * _dsilu(c, s)

    return pl.pallas_call(
        body, name="ada_bwd",
        out_shape=(jax.ShapeDtypeStruct((n_sh, d, ws), BF16), jax.ShapeDtypeStruct((1, n_sh * ws), F32),
                   jax.ShapeDtypeStruct(cv.shape, F32)),
        compiler_params=_params())(cv, ada_w8, dmod_ss, small)


class _Tiles:
    def __init__(self, nb, s_len, c_len, tm, big):
        self.nb, self.tm, self.big = nb, tm, big
        self.lat, self.ctx = s_len // tm, c_len // tm
        self.pad = -(self.lat + self.ctx) % big
        self.per_ex = self.lat + self.ctx + self.pad
        self.n_all, self.n_lat = nb * self.per_ex, nb * self.lat
        self.rows_per_ex = self.per_ex * tm

    def is_lat(self, i):
        return i % self.per_ex < self.lat

    def is_pad(self, i):
        return i % self.per_ex >= self.lat + self.ctx

    def lat_of_all(self, i):
        return (i // self.per_ex) * self.lat + jnp.minimum(i % self.per_ex, self.lat - 1)

    def ctx_of_all(self, i):
        return (i // self.per_ex) * self.ctx + jnp.clip(i % self.per_ex - self.lat, 0, self.ctx - 1)

    def big_all_of_lat(self, t):
        lat_big = self.lat // self.big
        return (t // lat_big) * (self.per_ex // self.big) + t % lat_big


def _norm_fwd(x2, ctx2, mod, norm_g, tiles):
    tl, d = x2.shape
    tc = ctx2.shape[0]
    nb, tm = tiles.nb, tiles.tm

    def body(x_ref, c_ref, mod_ref, g_ref, u_ref):
        i = pl.program_id(0)
        lat = tiles.is_lat(i)
        xv = jnp.where(lat, x_ref[...], c_ref[...])
        row = jnp.where(lat, i // tiles.per_ex, nb)
        m = _rowsel(mod_ref[...], row, nb + 1)
        shift, scale = m[:, 0:d], m[:, d:2 * d]
        rstd = lax.rsqrt(jnp.mean(xv * xv, axis=-1, keepdims=True) + EPS)
        u = xv * rstd * g_ref[...] * (1.0 + scale) + shift
        u_ref[...] = jnp.where(tiles.is_pad(i), 0.0, u).astype(BF16)

    return pl.pallas_call(
        body, name="norm_fwd", grid=(tiles.n_all,),
        in_specs=[pl.BlockSpec((tm, d), lambda i: (tiles.lat_of_all(i), 0)),
                  pl.BlockSpec((tm, d), lambda i: (tiles.ctx_of_all(i), 0)),
                  pl.BlockSpec(mod.shape, lambda i: (0, 0)),
                  pl.BlockSpec((1, d), lambda i: (0, 0))],
        out_specs=pl.BlockSpec((tm, d), lambda i: (i, 0)),
        out_shape=jax.ShapeDtypeStruct((tiles.n_all * tm, d), BF16),
        compiler_params=_params())(x2, ctx2, mod, norm_g)


def _norm_bwd(x2, ctx2, mod, norm_g, du_lat, du_b, gx1, tiles):
    tl, d = x2.shape
    nb, tm = tiles.nb, tiles.tm
    nrow = mod.shape[0]
    n_lat_in = len(du_lat)

    def body(x_ref, c_ref, mod_ref, g_ref, *refs):
        dl_refs = refs[:n_lat_in]
        d3_ref, gx_ref, gxo_ref, dmod_ref, dg_ref = refs[n_lat_in:]
        i = pl.program_id(0)

        @pl.when(i == 0)
        def _():
            dmod_ref[...] = jnp.zeros_like(dmod_ref)
            dg_ref[...] = jnp.zeros_like(dg_ref)

        lat = tiles.is_lat(i)
        xv = jnp.where(lat, x_ref[...], c_ref[...])
        row = jnp.where(lat, i // tiles.per_ex, nb)
        m = _rowsel(mod_ref[...], row, nb + 1)
        scale = m[:, d:2 * d]
        g = g_ref[...]
        dl = dl_refs[0][...]
        for ref in dl_refs[1:]:
            dl = dl + ref[...]
        du = jnp.where(tiles.is_pad(i), 0.0, d3_ref[...] + jnp.where(lat, dl, 0.0))
        rstd = lax.rsqrt(jnp.mean(xv * xv, axis=-1, keepdims=True) + EPS)
        xh = xv * rstd
        dshift = jnp.sum(du, axis=0, keepdims=True)
        dscale = jnp.sum(du * xh * g, axis=0, keepdims=True)
        dxn = du * (1.0 + scale)
        dg_ref[...] += jnp.sum(dxn * xh, axis=0, keepdims=True)
        dxh = dxn * g
        dx = rstd * (dxh - xh * jnp.mean(dxh * xh, axis=-1, keepdims=True))

        @pl.when(lat)
        def _():
            gxo_ref[...] = dx + gx_ref[...]

        for r in range(nb + 1):
            dmod_ref[r:r + 1, 0:d] += jnp.where(row == r, dshift, 0.0)
            dmod_ref[r:r + 1, d:2 * d] += jnp.where(row == r, dscale, 0.0)

    lat_map = lambda i: (tiles.lat_of_all(i), 0)
    lat_spec = pl.BlockSpec((tm, d), lat_map)
    return pl.pallas_call(
        body, name="norm_bwd", grid=(tiles.n_all,),
        in_specs=[lat_spec,
                  pl.BlockSpec((tm, d), lambda i: (tiles.ctx_of_all(i), 0)),
                  pl.BlockSpec(mod.shape, lambda i: (0, 0)),
                  pl.BlockSpec((1, d), lambda i: (0, 0))]
                 + [lat_spec] * n_lat_in
                 + [pl.BlockSpec((tm, d), lambda i: (i, 0)), lat_spec],
        out_specs=(lat_spec,
                   pl.BlockSpec((nrow, 3 * d), lambda i: (0, 0)),
                   pl.BlockSpec((1, d), lambda i: (0, 0))),
        out_shape=(jax.ShapeDtypeStruct((tl, d), F32), jax.ShapeDtypeStruct((nrow, 3 * d), F32),
                   jax.ShapeDtypeStruct((1, d), F32)),
        compiler_params=_params())(x2, ctx2, mod, norm_g, *du_lat, du_b, gx1)


def _matmul_bias(name, u, w, b, rows, tm, tn, u_tile=lambda i: i):
    d, n = w.shape

    def body(u_ref, w_ref, b_ref, o_ref):
        o_ref[...] = jnp.dot(u_ref[...], w_ref[...], preferred_element_type=F32) + b_ref[...]

    return pl.pallas_call(
        body, name=name, grid=(n // tn, rows // tm),
        in_specs=[pl.BlockSpec((tm, d), lambda j, i: (u_tile(i), 0)),
                  pl.BlockSpec((d, tn), lambda j, i: (0, j)),
                  pl.BlockSpec((1, tn), lambda j, i: (0, j))],
        out_specs=pl.BlockSpec((tm, tn), lambda j, i: (i, j)),
        out_shape=jax.ShapeDtypeStruct((rows, n), F32),
        compiler_params=_params())(u, w, b)


def _matmul_nt(name, a, w, koff, tm, tk, after=()):
    r, kc = a.shape
    d = w.shape[0]
    nk = kc // tk

    def body(a_ref, w_ref, *rest):
        o_ref = rest[len(after)]
        k = pl.program_id(1)
        p = lax.dot_general(a_ref[...], w_ref[...], (((1,), (1,)), ((), ())), preferred_element_type=F32)

        @pl.when(k == 0)
        def _():
            o_ref[...] = p

        @pl.when(k > 0)
        def _():
            o_ref[...] += p

    return pl.pallas_call(
        body, name=name, grid=(r // tm, nk),
        in_specs=[pl.BlockSpec((tm, tk), lambda i, k: (i, k)),
                  pl.BlockSpec((d, tk), lambda i, k: (0, koff + k))] + [_ANY] * len(after),
        out_specs=pl.BlockSpec((tm, d), lambda i, k: (i, 0)),
        out_shape=jax.ShapeDtypeStruct((r, d), F32),
        compiler_params=_params())(a, w, *after)


def _matmul_tn(name, a, b, rows, tk, tn):
    m = a.shape[1]
    n = b.shape[1]
    nk = rows // tk

    def body(a_ref, b_ref, o_ref, s_ref, acc_ref):
        k = pl.program_id(1)
        bv = b_ref[...]
        p = lax.dot_general(a_ref[...], bv, (((0,), (0,)), ((), ())), preferred_element_type=F32)
        cs = jnp.sum(bv.astype(F32), axis=0, keepdims=True)

        @pl.when(k == 0)
        def _():
            acc_ref[...] = p
            s_ref[...] = cs

        @pl.when(k > 0)
        def _():
            acc_ref[...] += p
            s_ref[...] += cs

        @pl.when(k == nk - 1)
        def _():
            o_ref[...] = acc_ref[...].astype(o_ref.dtype)

    return pl.pallas_call(
        body, name=name, grid=(n // tn, nk),
        in_specs=[pl.BlockSpec((tk, m), lambda j, k: (k, 0)),
                  pl.BlockSpec((tk, tn), lambda j, k: (k, j))],
        out_specs=(pl.BlockSpec((m, tn), lambda j, k: (0, j)), pl.BlockSpec((1, tn), lambda j, k: (0, j))),
        out_shape=(jax.ShapeDtypeStruct((m, n), BF16), jax.ShapeDtypeStruct((1, n), F32)),
        scratch_shapes=[pltpu.VMEM((m, tn), F32)],
        compiler_params=_params())(a, b)


def _matmul_tn_whole(name, a3, b3, rows, tn):
    nb, _, m = a3.shape
    n = b3.shape[2]

    def body(a_ref, b_ref, o_ref, s_ref):
        p, cs = None, None
        for e in range(nb):
            bv = b_ref[e]
            pe = lax.dot_general(a_ref[e], bv, (((0,), (0,)), ((), ())), preferred_element_type=F32)
            ce = jnp.sum(bv.astype(F32), axis=0, keepdims=True)
            p, cs = (pe, ce) if p is None else (p + pe, cs + ce)
        o_ref[...] = p.astype(o_ref.dtype)
        s_ref[...] = cs

    return pl.pallas_call(
        body, name=name, grid=(n // tn,),
        in_specs=[pl.BlockSpec((nb, rows, m), lambda j: (0, 0, 0)),
                  pl.BlockSpec((nb, rows, tn), lambda j: (0, 0, j))],
        out_specs=(pl.BlockSpec((m, tn), lambda j: (0, j)), pl.BlockSpec((1, tn), lambda j: (0, j))),
        out_shape=(jax.ShapeDtypeStruct((m, n), BF16), jax.ShapeDtypeStruct((1, n), F32)),
        compiler_params=_params())(a3, b3)


def _conv_window(pad_ref, r, shift, ktaps, width, horizontal):
    if horizontal:
        return pad_ref[r, pl.ds(16 + shift, width), :]
    return pad_ref[r + ktaps // 2 + shift]


def _conv_row(pad_ref, w, r, ktaps, width, horizontal, flip):
    half = ktaps // 2
    acc = None
    for t in range(ktaps):
        win = _conv_window(pad_ref, r, (half - t) if flip else (t - half), ktaps, width, horizontal)
        term = win * w[t:t + 1, :]
        acc = term if acc is None else acc + term
    return acc


def _fill_padded(ref, val, rows, width, ktaps, horizontal):
    half_k = ktaps // 2
    cb = val.shape[-1]
    if horizontal:
        ref[:, 0:16, :] = jnp.zeros((rows, 16, cb), F32)
        ref[:, 16 + width:32 + width, :] = jnp.zeros((rows, 16, cb), F32)
        ref[:, 16:16 + width, :] = val
    else:
        ref[0:half_k, :, :] = jnp.zeros((half_k, width, cb), F32)
        ref[half_k + rows:2 * half_k + rows, :, :] = jnp.zeros((half_k, width, cb), F32)
        ref[half_k:half_k + rows, :, :] = val


def _conv_fwd(pa, conv_w8, conv_b, nb, s):
    nblk, ktaps, cb = conv_w8.shape
    d = nblk * cb
    rows, width = s // GRID_W, GRID_W
    half_k = ktaps // 2
    nh = nblk // 2

    def body(glu_ref, w_ref, b_ref, o_ref, ph_ref, pv_ref):
        j = pl.program_id(1)
        a0 = (glu_ref[:, 0:cb] * _sigmoid(glu_ref[:, cb:2 * cb])).reshape(rows, width, cb)
        w = w_ref[...]

        bias = b_ref[...]

        def run(pad_ref, horizontal):
            _fill_padded(pad_ref, a0, rows, width, ktaps, horizontal)

            def row(r, carry):
                at = pl.ds(pl.multiple_of(r * width, width), width)
                o_ref[at, :] = _conv_row(pad_ref, w, r, ktaps, width, horizontal, False) + bias
                return carry

            lax.fori_loop(0, rows, row, 0)

        @pl.when(j < nh)
        def _():
            run(ph_ref, True)

        @pl.when(j >= nh)
        def _():
            run(pv_ref, False)

    return pl.pallas_call(
        body, name="conv_fwd", grid=(nb, nblk),
        in_specs=[pl.BlockSpec((s, 2 * cb), lambda b, j: (b, j)),
                  pl.BlockSpec((None, ktaps, cb), lambda b, j: (j, 0, 0)),
                  pl.BlockSpec((1, cb), lambda b, j: (0, j))],
        out_specs=pl.BlockSpec((s, cb), lambda b, j: (b, j)),
        out_shape=jax.ShapeDtypeStruct((nb * s, d), F32),
        scratch_shapes=[pltpu.VMEM((rows, width + 32, cb), F32), pltpu.VMEM((rows + 2 * half_k, width, cb), F32)],
        compiler_params=_params())(pa, conv_w8, conv_b)


def _conv_bwd(pa, da1, conv_w8, nb, s):
    nblk, ktaps, cb = conv_w8.shape
    d = nblk * cb
    rows, width = s // GRID_W, GRID_W
    half_k = ktaps // 2
    nh = nblk // 2

    def body(glu_ref, da_ref, w_ref, dp_ref, dw_ref, db_ref, pha_ref, phd_ref, pva_ref, pvd_ref):
        j = pl.program_id(0)
        b = pl.program_id(1)
        a0 = (glu_ref[:, 0:cb] * _sigmoid(glu_ref[:, cb:2 * cb])).reshape(rows, width, cb)
        da1v = da_ref[...]
        d3 = da1v.reshape(rows, width, cb)
        w = w_ref[...]

        @pl.when(b == 0)
        def _():
            dw_ref[...] = jnp.zeros_like(dw_ref)
            db_ref[...] = jnp.zeros_like(db_ref)

        db_ref[...] += jnp.sum(da1v, axis=0, keepdims=True)

        def run(pa_ref, pd_ref, horizontal):
            _fill_padded(pa_ref, a0, rows, width, ktaps, horizontal)
            _fill_padded(pd_ref, d3, rows, width, ktaps, horizontal)

            def row(r, accs):
                at = pl.ds(pl.multiple_of(r * width, width), width)
                da0 = _conv_row(pd_ref, w, r, ktaps, width, horizontal, True)
                gv = glu_ref[at, 0:cb]
                sg = _sigmoid(glu_ref[at, cb:2 * cb])
                dp_ref[at, 0:cb] = (da0 * sg).astype(BF16)
                dp_ref[at, cb:2 * cb] = (da0 * gv * sg * (1.0 - sg)).astype(BF16)
                d_row = da_ref[at, :]
                out = []
                for t in range(ktaps):
                    prod = _conv_window(pa_ref, r, t - half_k, ktaps, width, horizontal) * d_row
                    out.append(accs[t] + jnp.sum(prod.reshape(width // 8, 8, cb), axis=0))
                return tuple(out)

            accs = lax.fori_loop(0, rows, row, tuple(jnp.zeros((8, cb), F32) for _ in range(ktaps)))
            for t in range(ktaps):
                dw_ref[t:t + 1, :] += jnp.sum(accs[t], axis=0, keepdims=True)

        @pl.when(j < nh)
        def _():
            run(pha_ref, phd_ref, True)

        @pl.when(j >= nh)
        def _():
            run(pva_ref, pvd_ref, False)

    return pl.pallas_call(
        body, name="conv_bwd", grid=(nblk, nb),
        in_specs=[pl.BlockSpec((s, 2 * cb), lambda j, b: (b, j)),
                  pl.BlockSpec((s, cb), lambda j, b: (b, j)),
                  pl.BlockSpec((None, ktaps, cb), lambda j, b: (j, 0, 0))],
        out_specs=(pl.BlockSpec((s, 2 * cb), lambda j, b: (b, j)),
                   pl.BlockSpec((None, ktaps, cb), lambda j, b: (j, 0, 0)),
                   pl.BlockSpec((1, cb), lambda j, b: (0, j))),
        out_shape=(jax.ShapeDtypeStruct((nb * s, 2 * d), BF16),
                   jax.ShapeDtypeStruct((nblk, ktaps, cb), F32), jax.ShapeDtypeStruct((1, d), F32)),
        scratch_shapes=[pltpu.VMEM((rows, width + 32, cb), F32), pltpu.VMEM((rows, width + 32, cb), F32),
                        pltpu.VMEM((rows + 2 * half_k, width, cb), F32),
                        pltpu.VMEM((rows + 2 * half_k, width, cb), F32)],
        compiler_params=_params())(pa, da1, conv_w8)


def _log_sigmoid(x):
    return jnp.minimum(x, 0.0) - jnp.log(1.0 + jnp.exp(-jnp.abs(x)))


def _decay_fwd(pb, up2, bias2, tm, lr_blk):
    t_all = pb.shape[0]
    n2 = up2.shape[1]

    def body(lr_ref, up_ref, b_ref, g_ref):
        logits = _mm(lr_ref[...], up_ref[...]) + b_ref[...]
        g_ref[...] = _log_sigmoid(logits) * (1.0 / GATE_TAU)

    return pl.pallas_call(
        body, name="decay_fwd", grid=(t_all // tm,),
        in_specs=[pl.BlockSpec((tm, LANE), lambda i: (i, lr_blk)),
                  pl.BlockSpec(up2.shape, lambda i: (0, 0)),
                  pl.BlockSpec((1, n2), lambda i: (0, 0))],
        out_specs=pl.BlockSpec((tm, n2), lambda i: (i, 0)),
        out_shape=jax.ShapeDtypeStruct((t_all, n2), F32),
        compiler_params=_params())(pb, up2, bias2)


def _decay_bwd(pb, up2, bias2, grads_f, grads_b, tiles, lr_blk, dk_, dv_):
    t_all = pb.shape[0]
    tm = tiles.tm
    n2 = up2.shape[1]
    nbw = 2 * dk_ + dv_ + LANE

    def body(lr_ref, up_ref, b_ref, dqf, dkf, dvf, dgf, dqb, dkb, dvb, dgb, dp_ref, dup_ref, dbias_ref):
        i = pl.program_id(0)
        pad = tiles.is_pad(i)
        live = lambda v: jnp.where(pad, 0.0, v)

        @pl.when(i == 0)
        def _():
            dup_ref[...] = jnp.zeros_like(dup_ref)
            dbias_ref[...] = jnp.zeros_like(dbias_ref)

        lr = lr_ref[...]
        up = up_ref[...]
        logits = _mm(lr, up) + b_ref[...]
        dg = live(jnp.concatenate([dgf[...], dgb[...]], axis=1))
        dlog = dg * (1.0 / GATE_TAU) * _sigmoid(-logits)
        dup_ref[...] += _mm_tn(lr, dlog)
        dbias_ref[...] += jnp.sum(dlog, axis=0, keepdims=True)
        both = lambda f, b: live(f[...].astype(F32) + b[...].astype(F32)).astype(BF16)
        dp_ref[:, 0:dk_] = both(dqf, dqb)
        dp_ref[:, dk_:2 * dk_] = both(dkf, dkb)
        dp_ref[:, 2 * dk_:2 * dk_ + dv_] = both(dvf, dvb)
        dp_ref[:, 2 * dk_ + dv_:nbw] = _mm_nt(dlog, up).astype(BF16)

    row = lambda w: pl.BlockSpec((tm, w), lambda i: (i, 0))
    return pl.pallas_call(
        body, name="decay_bwd", grid=(t_all // tm,),
        in_specs=[pl.BlockSpec((tm, LANE), lambda i: (i, lr_blk)),
                  pl.BlockSpec(up2.shape, lambda i: (0, 0)),
                  pl.BlockSpec((1, n2), lambda i: (0, 0)),
                  row(dk_), row(dk_), row(dv_), row(dk_), row(dk_), row(dk_), row(dv_), row(dk_)],
        out_specs=(row(nbw), pl.BlockSpec(up2.shape, lambda i: (0, 0)), pl.BlockSpec((1, n2), lambda i: (0, 0))),
        out_shape=(jax.ShapeDtypeStruct((t_all, nbw), BF16), jax.ShapeDtypeStruct(up2.shape, F32),
                   jax.ShapeDtypeStruct((1, n2), F32)),
        compiler_params=_params())(pb, up2, bias2, *grads_f, *grads_b)


def _scan_chunk(s, nl, nc, rev):
    if rev:
        return jnp.where(s < nc, nl + (nc - 1 - s), nl - 1 - (s - nc))
    return jnp.where(s < nc, nl + s, s - nc)


def _scan_lat_chunk(s, nl, nc, rev):
    first = nl - 1 if rev else 0
    return jnp.where(s < nc, first, _scan_chunk(s, nl, nc, rev))


def _tri_mm(m_bf, x):
    hi = x.astype(BF16)
    r1 = x - hi.astype(F32)
    mid = r1.astype(BF16)
    lo = (r1 - mid.astype(F32)).astype(BF16)
    dot = lambda p: jnp.dot(m_bf, p, preferred_element_type=F32)
    return dot(hi) + dot(mid) + dot(lo)


def _chunk_masks(c, rev):
    ii = lax.broadcasted_iota(jnp.int32, (c, c), 0)
    jj = lax.broadcasted_iota(jnp.int32, (c, c), 1)
    return ((ii <= jj), (ii >= jj)) if rev else ((ii >= jj), (ii <= jj))


def _chunk_terms(q, k, b, far, mid):
    bf, bm = b[far:far + 1, :], b[mid:mid + 1, :]
    e = jnp.exp(b)
    em = jnp.exp(b - bm)
    eim = jnp.exp(bm - b)
    ed = jnp.exp(bf - b)
    return dict(e=e, em=em, eim=eim, ed=ed, dec=jnp.exp(bf), qe=q * e, qem=q * em, kim=k * eim, kd=k * ed)


def _gla_fwd(pb3, g3, nb, s_len, c_len, dk_, dv_):
    c = CHUNK
    nl, nc = s_len // c, c_len // c
    ns = nl + nc
    hk, hv = dk_ // HEADS, dv_ // HEADS
    l_len = pb3.shape[1]
    scale = hk ** -0.5
    mid = c // 2

    def body(*refs):
        ins, outs, z_scr = refs[:8], refs[8:14], refs[14]
        s = pl.program_id(0)

        @pl.when(s == 0)
        def _():
            z_scr[...] = jnp.zeros_like(z_scr)

        qs = jnp.where(s >= nc, scale, 0.0)
        for di, rev in enumerate((False, True)):
            q_ref, k_ref, v_ref, g_ref = ins[4 * di:4 * di + 4]
            o_ref, zs_ref, b_ref = outs[3 * di:3 * di + 3]
            mask, _ = _chunk_masks(c, rev)
            m_bf = mask.astype(BF16)
            far = 0 if rev else c - 1
            for b in range(nb):
                bc = _tri_mm(m_bf, g_ref[b])
                b_ref[b] = bc
                for h in range(HEADS):
                    ks, vs = slice(h * hk, (h + 1) * hk), slice(h * hv, (h + 1) * hv)
                    zi = (di * nb + b) * HEADS + h
                    v = v_ref[b, :, vs]
                    t = _chunk_terms(q_ref[b, :, ks] * qs, k_ref[b, :, ks], bc[:, ks], far, mid)
                    a = jnp.where(mask, _mm_nt(t["qem"], t["kim"]), 0.0)
                    z = z_scr[zi]
                    zs_ref[0, b * HEADS + h] = z
                    o_ref[b, :, vs] = _mm(a, v) + _mm_nt(t["qe"], z)
                    z_scr[zi] = z * t["dec"] + _mm_tn(v, t["kd"])

    in_specs, out_specs, out_shape = [], [], []
    for di, rev in enumerate((False, True)):
        ch = functools.partial(_scan_chunk, nl=nl, nc=nc, rev=rev)
        lch = functools.partial(_scan_lat_chunk, nl=nl, nc=nc, rev=rev)
        in_specs += [pl.BlockSpec((nb, c, dk_), lambda s, ch=ch: (0, ch(s), 0)),
                     pl.BlockSpec((nb, c, dk_), lambda s, ch=ch: (0, ch(s), 1)),
                     pl.BlockSpec((nb, c, dv_), lambda s, ch=ch: (0, ch(s), 1)),
                     pl.BlockSpec((nb, c, dk_), lambda s, ch=ch, di=di: (0, ch(s), di))]
        out_specs += [pl.BlockSpec((nb, c, dv_), lambda s, lch=lch: (0, lch(s), 0)),
                      pl.BlockSpec((1, nb * HEADS, hv, hk), lambda s: (s, 0, 0, 0)),
                      pl.BlockSpec((nb, c, dk_), lambda s, ch=ch: (0, ch(s), 0))]
        out_shape += [jax.ShapeDtypeStruct((nb, s_len, dv_), F32),
                      jax.ShapeDtypeStruct((ns, nb * HEADS, hv, hk), F32),
                      jax.ShapeDtypeStruct((nb, l_len, dk_), F32)]
    return pl.pallas_call(
        body, name="gla_fwd", grid=(ns,), in_specs=in_specs, out_specs=tuple(out_specs), out_shape=tuple(out_shape),
        scratch_shapes=[pltpu.VMEM((2 * nb * HEADS, hv, hk), F32)],
        compiler_params=_params())(pb3, pb3, pb3, g3, pb3, pb3, pb3, g3)


def _gla_bwd(pb3, do3, fwd_saved, nb, s_len, c_len, dk_, dv_):
    c = CHUNK
    nl, nc = s_len // c, c_len // c
    ns = nl + nc
    hk, hv = dk_ // HEADS, dv_ // HEADS
    l_len = pb3.shape[1]
    scale = hk ** -0.5
    mid = c // 2
    zs_f, b_f, zs_b, b_b = fwd_saved

    def body(*refs):
        ins, outs, dz_scr = refs[:12], refs[12:20], refs[20]
        s = pl.program_id(0)
        step = ns - 1 - s

        @pl.when(s == 0)
        def _():
            dz_scr[...] = jnp.zeros_like(dz_scr)

        lat = step >= nc
        qs = jnp.where(lat, scale, 0.0)
        dmul = jnp.where(lat, 1.0, 0.0)
        for di, rev in enumerate((False, True)):
            q_ref, k_ref, v_ref, b_ref, do_ref, zs_ref = ins[6 * di:6 * di + 6]
            dq_ref, dk_ref, dv_ref, dg_ref = outs[4 * di:4 * di + 4]
            mask, mask_t = _chunk_masks(c, rev)
            mt_bf = mask_t.astype(BF16)
            far = 0 if rev else c - 1
            far_row = lax.broadcasted_iota(jnp.int32, (c, hk), 0) == far
            for b in range(nb):
                db_parts = []
                for h in range(HEADS):
                    ks, vs = slice(h * hk, (h + 1) * hk), slice(h * hv, (h + 1) * hv)
                    zi = (di * nb + b) * HEADS + h
                    v = v_ref[b, :, vs]
                    d_o = do_ref[b, :, vs] * dmul
                    t = _chunk_terms(q_ref[b, :, ks] * qs, k_ref[b, :, ks], b_ref[b, :, ks], far, mid)
                    qem, kim, qe, kd = t["qem"], t["kim"], t["qe"], t["kd"]
                    a_t = jnp.where(mask_t, _mm_nt(kim, qem), 0.0)
                    d_a = jnp.where(mask, _mm_nt(d_o, v), 0.0)
                    d_at = jnp.where(mask_t, _mm_nt(v, d_o), 0.0)
                    z = zs_ref[0, b * HEADS + h]
                    dzn = dz_scr[zi]
                    dv_ref[b, :, vs] = (_mm(a_t, d_o) + _mm_nt(kd, dzn)).astype(dv_ref.dtype)
                    dqem = _mm(d_a, kim)
                    dkim = _mm(d_at, qem)
                    dqe = _mm(d_o, z)
                    dkd = _mm(v, dzn)
                    ddec = jnp.sum(z * dzn, axis=0, keepdims=True)
                    dz_scr[zi] = dzn * t["dec"] + _mm_tn(d_o, qe)
                    dq_ref[b, :, ks] = ((dqem * t["em"] + dqe * t["e"]) * qs).astype(dq_ref.dtype)
                    dk_ref[b, :, ks] = (dkim * t["eim"] + dkd * t["ed"]).astype(dk_ref.dtype)
                    db = dqem * qem - dkim * kim + dqe * qe - dkd * kd
                    extra = jnp.sum(dkd * kd, axis=0, keepdims=True) + ddec * t["dec"]
                    db_parts.append(db + jnp.where(far_row, extra, 0.0))
                dg_ref[b] = _tri_mm(mt_bf, jnp.concatenate(db_parts, axis=1))

    in_specs, out_specs, out_shape, args = [], [], [], []
    for di, rev in enumerate((False, True)):
        ch = lambda s, rev=rev: _scan_chunk(ns - 1 - s, nl, nc, rev)
        lch = lambda s, rev=rev: _scan_lat_chunk(ns - 1 - s, nl, nc, rev)
        in_specs += [pl.BlockSpec((nb, c, dk_), lambda s, ch=ch: (0, ch(s), 0)),
                     pl.BlockSpec((nb, c, dk_), lambda s, ch=ch: (0, ch(s), 1)),
                     pl.BlockSpec((nb, c, dv_), lambda s, ch=ch: (0, ch(s), 1)),
                     pl.BlockSpec((nb, c, dk_), lambda s, ch=ch: (0, ch(s), 0)),
                     pl.BlockSpec((nb, c, dv_), lambda s, lch=lch: (0, lch(s), 0)),
                     pl.BlockSpec((1, nb * HEADS, hv, hk), lambda s: (ns - 1 - s, 0, 0, 0))]
        args += [pb3, pb3, pb3, (b_b if rev else b_f), do3, (zs_b if rev else zs_f)]
        for w, dt in ((dk_, BF16), (dk_, BF16), (dv_, BF16), (dk_, F32)):
            out_specs.append(pl.BlockSpec((nb, c, w), lambda s, ch=ch: (0, ch(s), 0)))
            out_shape.append(jax.ShapeDtypeStruct((nb, l_len, w), dt))
    return pl.pallas_call(
        body, name="gla_bwd", grid=(ns,), in_specs=in_specs, out_specs=tuple(out_specs), out_shape=tuple(out_shape),
        scratch_shapes=[pltpu.VMEM((2 * nb * HEADS, hv, hk), F32)],
        compiler_params=_params())(*args)


def _tail(a1, pa, o_f, o_b, x2, tgt, mod, wc, wg, wo, ln_g, ln_b, gn_t, fg, nb, tm):
    tl, d = x2.shape
    nt = tl // tm
    per_ex = nt // nb
    hv = d // HEADS
    nrow = mod.shape[0]

    def body(a1_ref, z_ref, r_ref, mc_ref, mg_ref, of_ref, ob_ref, x_ref, t_ref, mod_ref, wc_ref, wg_ref, wo_ref,
             lng_ref, lnb_ref, gn_ref, fg_ref,
             dp_ref, da1_ref, do_ref, gx_ref, mrg_ref, dmo_ref, yci_ref, dyc_ref, ogi_ref, dyg_ref, sm_ref):
        i = pl.program_id(0)

        @pl.when(i == 0)
        def _():
            sm_ref[...] = jnp.zeros_like(sm_ref)

        bidx = i // per_ex
        gate = _rowsel(mod_ref[...], bidx, nb)[:, 2 * d:3 * d]
        lng, lnb, fgv = lng_ref[...], lnb_ref[...], fg_ref[...]
        gn = jnp.concatenate([gn_ref[...]] * HEADS, axis=1)
        wc_, wg_, wo_ = wc_ref[...], wg_ref[...], wo_ref[...]

        a1v = a1_ref[...]
        mu = jnp.mean(a1v, axis=-1, keepdims=True)
        xc = a1v - mu
        rs = lax.rsqrt(jnp.mean(xc * xc, axis=-1, keepdims=True) + EPS)
        xh = xc * rs
        a2 = xh * lng + lnb
        s2 = _sigmoid(a2)
        a3 = a2 * s2
        zv = z_ref[...]
        sz = _sigmoid(zv)
        siluz = zv * sz
        ycin = a3 * siluz
        yconv = _mm(ycin, wc_)

        o = of_ref[...] + ob_ref[...]
        ohat_parts, rn_parts = [], []
        for h in range(HEADS):
            oh = o[:, h * hv:(h + 1) * hv]
            rn = lax.rsqrt(jnp.mean(oh * oh, axis=-1, keepdims=True) + EPS)
            ohat_parts.append(oh * rn)
            rn_parts.append(rn)
        ohat = jnp.concatenate(ohat_parts, axis=1)
        on = ohat * gn
        rv = r_ref[...]
        sr = _sigmoid(rv)
        silur = rv * sr
        ogin = on * silur
        ygla = _mm(ogin, wg_)

        sc = _sigmoid(mc_ref[...])
        sg = _sigmoid(mg_ref[...])
        merged = sc * yconv + sg * ygla
        mo = _mm(merged, wo_)
        hn = x_ref[...] + gate * mo
        rf = lax.rsqrt(jnp.mean(hn * hn, axis=-1, keepdims=True) + EPS)
        yh = hn * rf
        err = yh * fgv - t_ref[...]
        loss_part = 0.5 * jnp.sum(err * err) * (1.0 / d)

        dy = err * (1.0 / d)
        dfg = jnp.sum(dy * yh, axis=0, keepdims=True)
        dyh = dy * fgv
        dhn = rf * (dyh - yh * jnp.mean(dyh * yh, axis=-1, keepdims=True))
        gx_ref[...] = dhn
        dgate = jnp.sum(dhn * mo, axis=0, keepdims=True)
        dmo = gate * dhn
        dmerged = _mm_nt(dmo, wo_)
        dyconv = dmerged * sc
        dygla = dmerged * sg
        dp_ref[:, 2 * d:3 * d] = (dmerged * yconv * sc * (1.0 - sc)).astype(BF16)
        dp_ref[:, 3 * d:4 * d] = (dmerged * ygla * sg * (1.0 - sg)).astype(BF16)
        dycin = _mm_nt(dyconv, wc_)
        dogin = _mm_nt(dygla, wg_)
        mrg_ref[...] = merged.astype(BF16)
        dmo_ref[...] = dmo.astype(BF16)
        yci_ref[...] = ycin.astype(BF16)
        dyc_ref[...] = dyconv.astype(BF16)
        ogi_ref[...] = ogin.astype(BF16)
        dyg_ref[...] = dygla.astype(BF16)

        da3 = dycin * siluz
        dp_ref[:, 0:d] = (dycin * a3 * _dsilu(zv, sz)).astype(BF16)
        da2 = da3 * _dsilu(a2, s2)
        dlng = jnp.sum(da2 * xh, axis=0, keepdims=True)
        dlnb = jnp.sum(da2, axis=0, keepdims=True)
        dxh = da2 * lng
        da1_ref[...] = rs * (dxh - jnp.mean(dxh, axis=-1, keepdims=True)
                             - xh * jnp.mean(dxh * xh, axis=-1, keepdims=True))

        don = dogin * silur
        dp_ref[:, d:2 * d] = (dogin * on * _dsilu(rv, sr)).astype(BF16)
        dgn = jnp.sum(don * ohat, axis=0, keepdims=True)
        dyn = don * gn
        for h in range(HEADS):
            vs = slice(h * hv, (h + 1) * hv)
            oh_hat = ohat_parts[h]
            dh = dyn[:, vs]
            do_ref[:, vs] = (rn_parts[h] * (dh - oh_hat * jnp.mean(dh * oh_hat, axis=-1, keepdims=True))
                             ).astype(BF16)

        sm_ref[0:1, :] += dfg
        sm_ref[1:2, :] += dlng
        sm_ref[2:3, :] += dlnb
        sm_ref[3:4, :] += dgn
        sm_ref[4:5, :] += jnp.zeros((1, d), F32) + loss_part
        for b in range(nb):
            sm_ref[8 + b:9 + b, :] += jnp.where(bidx == b, dgate, 0.0)

    row = pl.BlockSpec((tm, d), lambda i: (i, 0))
    pcol = lambda blk: pl.BlockSpec((tm, d), lambda i: (i, blk))
    full = lambda arr: pl.BlockSpec(arr.shape, lambda i: (0,) * arr.ndim)
    bfo = jax.ShapeDtypeStruct((tl, d), BF16)
    f32o = jax.ShapeDtypeStruct((tl, d), F32)
    return pl.pallas_call(
        body, name="tail", grid=(nt,),
        in_specs=[row, pcol(2), pcol(3), pcol(4), pcol(5), row, row, row, row, full(mod), full(wc), full(wg),
                  full(wo), full(ln_g), full(ln_b), full(gn_t), full(fg)],
        out_specs=(pl.BlockSpec((tm, 4 * d), lambda i: (i, 0)), row, row, row, row, row, row, row, row, row,
                   pl.BlockSpec((16, d), lambda i: (0, 0))),
        out_shape=(jax.ShapeDtypeStruct((tl, 4 * d), BF16), f32o, bfo, f32o, bfo, bfo, bfo, bfo, bfo, bfo,
                   jax.ShapeDtypeStruct((16, d), F32)),
        compiler_params=_params())(a1, pa, pa, pa, pa, o_f, o_b, x2, tgt, mod, wc, wg, wo, ln_g, ln_b, gn_t, fg)


def _local_step(x, c, ctx, tgt, c_ctx, ada_w8, ada_b, norm_g, w_a, b_a, w_b, b_b, conv_w8, conv_b, ln_g, ln_b,
                up2, bias2, gla_norm_g, final_norm_g, proj, on_grads=None, on_du_a1=None):
    nb, s_len, d = x.shape
    c_len = ctx.shape[1]
    dk_, dv_ = d // 2, d
    tl, tc = nb * s_len, nb * c_len
    nbw = 2 * dk_ + dv_ + LANE
    tm = math.gcd(256, c_len)
    tiles = _Tiles(nb, s_len, c_len, tm, 2)
    tmm = tiles.big * tm
    l_len = tiles.rows_per_ex
    t_all = nb * l_len
    x2, ctx2, tgt2 = x.reshape(tl, d), ctx.reshape(tc, d), tgt.reshape(tl, d)

    cv = jnp.zeros((8, d), F32).at[0:nb].set(c).at[nb].set(c_ctx.reshape(d))
    mod = _ada_fwd(cv, ada_w8, ada_b)
    u = _norm_fwd(x2, ctx2, mod, norm_g, tiles)
    pa = _matmul_bias("inproj_a", u, w_a, b_a, tl, tmm, _tile(6 * d, 3072), u_tile=tiles.big_all_of_lat)
    pb = _matmul_bias("inproj_b", u, w_b, b_b, t_all, tmm, nbw)

    a1 = _conv_fwd(pa, conv_w8, conv_b, nb, s_len)
    lr_blk = (2 * dk_ + dv_) // LANE
    g_all = _decay_fwd(pb, up2, bias2, tm, lr_blk)
    pb3 = pb.reshape(nb, l_len, nbw)
    o_f, zs_f, b_f, o_b, zs_b, b_b2 = _gla_fwd(pb3, g_all.reshape(nb, l_len, 2 * dk_), nb, s_len, c_len, dk_, dv_)

    conv_proj, gla_proj, w_out = proj(a1) if callable(proj) else proj
    tt = math.gcd(128, s_len)
    (dp_a2, da1, d_o, gx1, merged, dmo, ycin, dyconv, ogin, dygla, small) = _tail(
        a1, pa, o_f.reshape(tl, dv_), o_b.reshape(tl, dv_), x2, tgt2, mod, conv_proj, gla_proj, w_out, ln_g, ln_b,
        gla_norm_g, final_norm_g, nb, tt)

    lat3 = lambda a: a.reshape(nb, s_len, a.shape[-1])
    tnw = _tile(d, 1024)
    tnp = _tile(d, 512)
    d_w_out, _ = _matmul_tn_whole("dw_out", lat3(merged), lat3(dmo), s_len, tnp)
    d_conv_proj, _ = _matmul_tn_whole("dw_conv_proj", lat3(ycin), lat3(dyconv), s_len, tnp)
    d_gla_proj, _ = _matmul_tn_whole("dw_gla_proj", lat3(ogin), lat3(dygla), s_len, tnp)

    dp_a1, d_conv_w8, d_conv_b = _conv_bwd(pa, da1, conv_w8, nb, s_len)
    gl = _gla_bwd(pb3, d_o.reshape(nb, s_len, dv_), (zs_f, b_f, zs_b, b_b2), nb, s_len, c_len, dk_, dv_)
    gl = [g_.reshape(t_all, g_.shape[-1]) for g_ in gl]
    dp_b, d_up2, d_bias2 = _decay_bwd(pb, up2, bias2, gl[0:4], gl[4:8], tiles, lr_blk, dk_, dv_)

    u3 = u.reshape(nb, l_len, d)
    dw_a1, db_a1 = _matmul_tn_whole("dw_a1", u3, lat3(dp_a1), s_len, tnw)
    dw_a2, db_a2 = _matmul_tn_whole("dw_a2", u3, lat3(dp_a2), s_len, tnw)
    dw_b, db_b = _matmul_tn("dw_b", u, dp_b, t_all, tmm, nbw)
    grads = dict(w_a1=dw_a1, w_a2=dw_a2, w_b=dw_b, conv_w8=d_conv_w8, conv_proj=d_conv_proj, up2=d_up2,
                 gla_proj=d_gla_proj, w_out=d_w_out)

    tka = _tile(2 * d, 2048)
    du_a1 = _matmul_nt("du_a1", dp_a1, w_a, 0, tmm, tka, after=on_grads(grads) if on_grads else ())
    du_a2 = _matmul_nt("du_a2", dp_a2, w_a, (2 * d) // tka, tmm, tka, after=on_du_a1(du_a1) if on_du_a1 else ())
    du_b = _matmul_nt("du_b", dp_b, w_b, 0, tmm, nbw)
    grad_x2, dmod_ss, d_norm_g = _norm_bwd(x2, ctx2, mod, norm_g, [du_a1, du_a2], du_b, gx1, tiles)
    d_ada_w8, d_ada_b, d_cv = _ada_bwd(cv, ada_w8, dmod_ss, small, nb)

    return dict(
        grads, grad_x=grad_x2.reshape(nb, s_len, d), small=small, cv=d_cv, ada_w8=d_ada_w8, ada_b=d_ada_b,
        norm_g=d_norm_g, b_a1=db_a1, b_a2=db_a2, b_b=db_b, conv_b=d_conv_b, bias2=d_bias2)


def _regroup_pieces(d, r, wshard):
    cb = d // N_DEV
    segs = []
    for j in range(N_DEV):
        segs.append((j * cb, cb, 0, 2 * j * cb))
    for j in range(N_DEV):
        segs.append((d + j * cb, cb, 0, (2 * j + 1) * cb))
    segs += [(2 * d, d, 0, 2 * d), (3 * d, 2 * d + 2 * r, 1, 0), (5 * d + 2 * r, 3 * d, 0, 3 * d)]
    pieces = []
    for o0, w, dst, d0 in segs:
        lo = o0
        while lo < o0 + w:
            j = lo // wshard
            hi = min(o0 + w, (j + 1) * wshard)
            pieces.append((j, lo - j * wshard, hi - lo, dst, d0 + lo - o0))
            lo = hi
    return pieces


def _regroup(o, d, r):
    n_in = 8 * d + 2 * r
    parts = ([], [])
    for _, s0, n, dst, _ in sorted(_regroup_pieces(d, r, n_in), key=lambda p: (p[3], p[4])):
        parts[dst].append(o[..., s0:s0 + n])
    pad = jnp.zeros(o.shape[:-1] + (LANE - 2 * r,), o.dtype)
    return jnp.concatenate(parts[0], axis=-1), jnp.concatenate(parts[1] + [pad], axis=-1)


def _unshard_w_in(g_win, d, r, after=()):
    n_sh, _, ws = g_win.shape
    nbw = 2 * d + LANE
    pieces = _regroup_pieces(d, r, ws)
    tr = math.gcd(d, 256)

    def body(g_ref, *rest):
        a_ref, b_ref = rest[len(after):]
        dsts = (a_ref, b_ref)
        for j, s0, n, dst, d0 in pieces:
            dsts[dst][:, pl.ds(d0, n)] = g_ref[j, :, pl.ds(s0, n)]
        b_ref[:, pl.ds(2 * d + 2 * r, LANE - 2 * r)] = jnp.zeros((tr, LANE - 2 * r), b_ref.dtype)

    return pl.pallas_call(
        body, name="unshard_w_in", grid=(d // tr,),
        in_specs=[pl.BlockSpec((n_sh, tr, ws), lambda i: (0, i, 0))] + [_ANY] * len(after),
        out_specs=(pl.BlockSpec((tr, 6 * d), lambda i: (i, 0)), pl.BlockSpec((tr, nbw), lambda i: (i, 0))),
        out_shape=(jax.ShapeDtypeStruct((d, 6 * d), g_win.dtype), jax.ShapeDtypeStruct((d, nbw), g_win.dtype)),
        compiler_params=_params())(g_win, *after)


def _reshard_w_in(dw_a1, dw_a2, dw_b, d, r):
    ws = (8 * d + 2 * r) // N_DEV
    pieces = _regroup_pieces(d, r, ws)
    tr = math.gcd(d, 256)

    def body(a1_ref, a2_ref, b_ref, o_ref):
        for j, s0, n, dst, d0 in pieces:
            if dst == 1:
                src = b_ref[:, pl.ds(d0, n)]
            elif d0 < 2 * d:
                src = a1_ref[:, pl.ds(d0, n)]
            else:
                src = a2_ref[:, pl.ds(d0 - 2 * d, n)]
            o_ref[j, :, pl.ds(s0, n)] = src

    row = lambda w: pl.BlockSpec((tr, w), lambda i: (i, 0))
    return pl.pallas_call(
        body, name="reshard_w_in", grid=(d // tr,),
        in_specs=[row(2 * d), row(4 * d), row(2 * d + LANE)],
        out_specs=pl.BlockSpec((N_DEV, tr, ws), lambda i: (0, i, 0)),
        out_shape=jax.ShapeDtypeStruct((N_DEV, d, ws), dw_b.dtype),
        compiler_params=_params())(dw_a1, dw_a2, dw_b)


_SMALL = ("c_ctx", "ada_b", "norm_g", "b_in", "conv_b", "conv_ln_g", "conv_ln_b", "decay_bias_fwd",
          "decay_bias_bwd", "gla_norm_g", "final_norm_g")


def _small_layout(d, r):
    sizes = dict(c_ctx=d, ada_b=3 * d, norm_g=d, b_in=8 * d + 2 * r, conv_b=d, conv_ln_g=d, conv_ln_b=d,
                 decay_bias_fwd=d // 2, decay_bias_bwd=d // 2, gla_norm_g=d // HEADS, final_norm_g=d, loss=1)
    table, off = {}, 0
    for name in _SMALL + ("loss",):
        table[name] = (off, sizes[name])
        off += -(-sizes[name] // LANE) * LANE
    return table, off


def _pack_small(g, nb, d, r):
    table, width = _small_layout(d, r)
    hv = d // HEADS
    pieces = _regroup_pieces(d, r, 8 * d + 2 * r)
    names = ("small", "cv", "ada_b", "norm_g", "b_a1", "b_a2", "b_b", "conv_b", "bias2")

    def body(sm, cv, ab, ng, ba1, ba2, bb, cvb, b2, o_ref):
        o_ref[...] = jnp.zeros_like(o_ref)

        def put(name, val):
            off, n = table[name]
            o_ref[:, pl.ds(off, n)] = val

        put("c_ctx", cv[nb:nb + 1, :])
        put("ada_b", ab[...])
        put("norm_g", ng[...])
        off_b = table["b_in"][0]
        for _, s0, n, dst, d0 in pieces:
            if dst == 1:
                src = bb[:, pl.ds(d0, n)]
            elif d0 < 2 * d:
                src = ba1[:, pl.ds(d0, n)]
            else:
                src = ba2[:, pl.ds(d0 - 2 * d, n)]
            o_ref[:, pl.ds(off_b + s0, n)] = src
        put("conv_b", cvb[...])
        put("conv_ln_g", sm[1:2, :])
        put("conv_ln_b", sm[2:3, :])
        put("decay_bias_fwd", b2[:, 0:d // 2])
        put("decay_bias_bwd", b2[:, d // 2:d])
        gn = sm[3:4, 0:hv]
        for h in range(1, HEADS):
            gn = gn + sm[3:4, h * hv:(h + 1) * hv]
        put("gla_norm_g", gn)
        put("final_norm_g", sm[0:1, :])
        put("loss", sm[4:5, 0:1])

    return pl.pallas_call(body, name="pack_small", out_shape=jax.ShapeDtypeStruct((1, width), F32),
                          compiler_params=_params())(*[g[k] for k in names])


def _small_adam(parts, ws, ms, vs, d, r):
    table, width = _small_layout(d, r)
    n_parts = parts.shape[0]
    k = len(_SMALL)
    bc1 = 1.0 - ADAM_B1 ** ADAM_STEP
    bc2 = 1.0 - ADAM_B2 ** ADAM_STEP

    def body(p_ref, *refs):
        w_refs, m_refs, v_refs = refs[0:k], refs[k:2 * k], refs[2 * k:3 * k]
        outs = refs[3 * k:]
        tot = p_ref[0]
        for i in range(1, n_parts):
            tot = tot + p_ref[i]
        for i, name in enumerate(_SMALL):
            off, n = table[name]
            g = tot[:, off:off + n]
            mn = ADAM_B1 * m_refs[i][...] + (1.0 - ADAM_B1) * g
            vn = ADAM_B2 * v_refs[i][...] + (1.0 - ADAM_B2) * (g * g)
            outs[i][...] = g
            outs[k + i][...] = -ADAM_LR * ((mn / bc1) / (jnp.sqrt(vn / bc2) + ADAM_EPS) + ADAM_WD * w_refs[i][...])
            outs[2 * k + i][...] = mn
            outs[3 * k + i][...] = vn
        off, _ = table["loss"]
        outs[4 * k][...] = tot[:, off:off + 1]

    shapes = [jax.ShapeDtypeStruct(w.shape, F32) for w in ws]
    res = pl.pallas_call(body, name="small_adam", out_shape=tuple(shapes * 4 + [jax.ShapeDtypeStruct((1, 1), F32)]),
                         compiler_params=_params())(parts, *ws, *ms, *vs)
    return res[0:k], res[k:2 * k], res[2 * k:3 * k], res[3 * k:4 * k], res[4 * k]


def _mesh_pos():
    return lax.axis_index("x"), lax.axis_index("y"), lax.axis_index("c")


def _all_gather(arrs):
    n = len(arrs)
    ns = 9
    split = [a.ndim == 2 and a.shape[0] % 32 == 0 for a in arrs]

    def body(*refs):
        ins, outs = refs[:n], refs[n:2 * n]
        send_sems, recv_sems, local_sems = refs[2 * n:]
        x, y, c = _mesh_pos()
        me, sibling = (x, y, c), (x, y, 1 - c)
        xn, yn, dg = (1 - x, y, c), (x, 1 - y, c), (1 - x, 1 - y, c)
        other = lambda pos: (pos[0], pos[1], 1 - c)

        def slot(a, pos, half):
            ref = outs[a].at[4 * pos[0] + 2 * pos[1] + pos[2]]
            if half is None:
                return ref
            rows = arrs[a].shape[0] // 2
            return ref.at[pl.ds(half * rows, rows)]

        def copy(a, k, block, to, src=None, half=None):
            dst = slot(a, block, half)
            return pltpu.make_async_remote_copy(
                src_ref=dst if src is None else src, dst_ref=dst,
                send_sem=send_sems.at[ns * a + k], recv_sem=recv_sems.at[ns * a + k],
                device_id=to, device_id_type=MESH)

        h0 = lambda a: 0 if split[a] else None
        mine = [pltpu.make_async_copy(ins[a], slot(a, me, None), local_sems.at[a]) for a in range(n)]
        for cp in mine:
            cp.start()
        sent = []
        for a in range(n):
            sent += [copy(a, 0, me, sibling, src=ins[a]), copy(a, 1, me, xn, src=ins[a]),
                     copy(a, 2, me, yn, src=ins[a])]
        for cp in sent:
            cp.start()

        def pass_on(cp):
            cp.start()
            sent.append(cp)

        for a in range(n):
            copy(a, 1, xn, me).wait_recv()
            pass_on(copy(a, 3, xn, sibling))
            pass_on(copy(a, 4, xn, yn, half=h0(a)))
        for a in range(n):
            copy(a, 2, yn, me).wait_recv()
            pass_on(copy(a, 5, yn, sibling))
            if split[a]:
                pass_on(copy(a, 6, yn, xn, half=1))
        for a in range(n):
            copy(a, 4, dg, me, half=h0(a)).wait_recv()
            pass_on(copy(a, 7, dg, sibling, half=h0(a)))
            if split[a]:
                copy(a, 6, dg, me, half=1).wait_recv()
                pass_on(copy(a, 8, dg, sibling, half=1))
        for a in range(n):
            copy(a, 0, sibling, me).wait_recv()
            copy(a, 3, other(xn), me).wait_recv()
            copy(a, 5, other(yn), me).wait_recv()
            copy(a, 7, other(dg), me, half=h0(a)).wait_recv()
            if split[a]:
                copy(a, 8, other(dg), me, half=1).wait_recv()
        for cp in sent:
            cp.wait_send()
        for cp in mine:
            cp.wait()

    return pl.pallas_call(
        body, name="all_gather",
        out_shape=tuple(jax.ShapeDtypeStruct((N_DEV,) + a.shape, a.dtype) for a in arrs),
        in_specs=[_ANY] * n, out_specs=tuple([_ANY] * n),
        scratch_shapes=[pltpu.SemaphoreType.DMA((ns * n,)), pltpu.SemaphoreType.DMA((ns * n,)),
                        pltpu.SemaphoreType.DMA((n,))],
    )(*arrs)


def _exchange_sibling(arrs):
    n = len(arrs)

    def body(*refs):
        ins, outs = refs[:n], refs[n:2 * n]
        send_sems, recv_sems = refs[2 * n:]
        x, y, c = _mesh_pos()
        copies = [pltpu.make_async_remote_copy(
            src_ref=ins[a].at[2 * k + (1 - c)], dst_ref=outs[a].at[k],
            send_sem=send_sems.at[4 * a + k], recv_sem=recv_sems.at[4 * a + k],
            device_id=(x, y, 1 - c), device_id_type=MESH) for a in range(n) for k in range(4)]
        for cp in copies:
            cp.start()
        for cp in copies:
            cp.wait_recv()
        for cp in copies:
            cp.wait_send()

    return pl.pallas_call(
        body, name="grad_exchange_sibling",
        out_shape=tuple(jax.ShapeDtypeStruct((4,) + a.shape[1:], a.dtype) for a in arrs),
        in_specs=[_ANY] * n, out_specs=tuple([_ANY] * n),
        scratch_shapes=[pltpu.SemaphoreType.DMA((4 * n,)), pltpu.SemaphoreType.DMA((4 * n,))],
    )(*arrs)


def _pair_sum(name, mine, theirs):
    _, r, cdim = mine.shape
    tr = r if (r % 8 or r <= 256) else math.gcd(r, 256)

    def body(m_ref, t_ref, o_ref):
        c = lax.axis_index("c")
        own = jnp.where(c == 0, m_ref[:, 0].astype(F32), m_ref[:, 1].astype(F32))
        o_ref[...] = (own + t_ref[...].astype(F32)).astype(o_ref.dtype)

    return pl.pallas_call(
        body, name=name, grid=(r // tr,),
        in_specs=[pl.BlockSpec((4, 2, tr, cdim), lambda i: (0, 0, i, 0)),
                  pl.BlockSpec((4, tr, cdim), lambda i: (0, i, 0))],
        out_specs=pl.BlockSpec((4, tr, cdim), lambda i: (0, i, 0)),
        out_shape=jax.ShapeDtypeStruct((4, r, cdim), mine.dtype),
        compiler_params=_params())(mine.reshape(4, 2, r, cdim), theirs)


def _exchange_chips(arrs):
    n = len(arrs)

    def body(*refs):
        ins, outs = refs[:n], refs[n:2 * n]
        send_sems, recv_sems, local_sems = refs[2 * n:]
        x, y, c = _mesh_pos()
        my_chip = 2 * x + y
        mine = [pltpu.make_async_copy(ins[a].at[my_chip], outs[a].at[my_chip], local_sems.at[a]) for a in range(n)]
        for cp in mine:
            cp.start()
        copies = []
        for rel in range(1, 4):
            px = 1 - x if rel & 2 else x
            py = 1 - y if rel & 1 else y
            for a in range(n):
                copies.append(pltpu.make_async_remote_copy(
                    src_ref=ins[a].at[2 * px + py], dst_ref=outs[a].at[my_chip],
                    send_sem=send_sems.at[3 * a + rel - 1], recv_sem=recv_sems.at[3 * a + rel - 1],
                    device_id=(px, py, c), device_id_type=MESH))
        for cp in copies:
            cp.start()
        for cp in copies:
            cp.wait_recv()
        for cp in copies:
            cp.wait_send()
        for cp in mine:
            cp.wait()

    return pl.pallas_call(
        body, name="grad_exchange_chips",
        out_shape=tuple(jax.ShapeDtypeStruct(a.shape, a.dtype) for a in arrs),
        in_specs=[_ANY] * n, out_specs=tuple([_ANY] * n),
        scratch_shapes=[pltpu.SemaphoreType.DMA((3 * n,)), pltpu.SemaphoreType.DMA((3 * n,)),
                        pltpu.SemaphoreType.DMA((n,))],
    )(*arrs)


_HBM = pl.BlockSpec(memory_space=pltpu.HBM)
_SEM = pl.BlockSpec(memory_space=pltpu.SEMAPHORE)


def _copies_start(name, srcs, lands, make_copies, n_sems):
    n, m = len(srcs), len(lands)

    def body(*refs):
        ins = refs[:n + m]
        send_sems, recv_sems = refs[n + m], refs[n + m + 1]
        for cp in make_copies(ins[:n], ins[n:], send_sems, recv_sems):
            cp.start()
        refs[-1][...] = jnp.zeros_like(refs[-1])

    res = pl.pallas_call(
        body, name=name,
        out_shape=(pltpu.SemaphoreType.DMA((n_sems,)), pltpu.SemaphoreType.DMA((n_sems,)),
                   *[pltpu.HBM(a.shape, a.dtype) for a in (*srcs, *lands)], jax.ShapeDtypeStruct((8, LANE), F32)),
        in_specs=[_HBM] * (n + m),
        out_specs=(_SEM, _SEM, *[_HBM] * (n + m), pl.BlockSpec(memory_space=pltpu.VMEM)),
        input_output_aliases={i: 2 + i for i in range(n + m)},
        compiler_params=pltpu.CompilerParams(has_side_effects=pltpu.SideEffectType.DATAFLOW_SIDE_EFFECTING),
    )(*[pltpu.with_memory_space_constraint(a, pltpu.HBM) for a in (*srcs, *lands)])
    return res[0], res[1], res[2:2 + n], res[2 + n:2 + n + m], res[-1]


def _copies_wait(name, started, after, make_copies):
    send_sems, recv_sems, srcs, lands, _ = started
    n, m = len(srcs), len(lands)

    def body(*refs):
        ins = refs[:n + m]
        for cp in make_copies(ins[:n], ins[n:], refs[n + m], refs[n + m + 1]):
            cp.wait_send()
            cp.wait_recv()

    res = pl.pallas_call(
        body, name=name,
        out_shape=tuple(pltpu.HBM(a.shape, a.dtype) for a in (*srcs, *lands)),
        in_specs=[_HBM] * (n + m) + [_SEM, _SEM] + [_ANY] * len(after),
        out_specs=tuple([_HBM] * (n + m)),
        input_output_aliases={i: i for i in range(n + m)},
        compiler_params=pltpu.CompilerParams(has_side_effects=pltpu.SideEffectType.DATAFLOW_SIDE_EFFECTING),
    )(*srcs, *lands, send_sems, recv_sems, *after)
    return res[:n], res[n:]


def _gather_copies(srcs, lands, send_sems, recv_sems):
    x, y, c = _mesh_pos()
    me_i = 4 * x + 2 * y + c
    copies = []
    for rel in range(1, N_DEV):
        peer = (1 - x if rel & 4 else x, 1 - y if rel & 2 else y, 1 - c if rel & 1 else c)
        for a in range(len(srcs)):
            copies.append(pltpu.make_async_remote_copy(
                src_ref=srcs[a], dst_ref=lands[a].at[me_i], send_sem=send_sems.at[7 * a + rel - 1],
                recv_sem=recv_sems.at[7 * a + rel - 1], device_id=peer, device_id_type=MESH))
    return copies


def _sibling_copies(srcs, lands, send_sems, recv_sems):
    x, y, c = _mesh_pos()
    return [pltpu.make_async_remote_copy(
        src_ref=srcs[a].at[2 * k + (1 - c)], dst_ref=lands[a].at[k], send_sem=send_sems.at[4 * a + k],
        recv_sem=recv_sems.at[4 * a + k], device_id=(x, y, 1 - c), device_id_type=MESH)
        for a in range(len(srcs)) for k in range(4)]


def _chip_copies(srcs, lands, send_sems, recv_sems):
    x, y, c = _mesh_pos()
    my_chip = 2 * x + y
    copies = []
    for rel in range(1, 4):
        px = 1 - x if rel & 2 else x
        py = 1 - y if rel & 1 else y
        for a in range(len(srcs)):
            copies.append(pltpu.make_async_remote_copy(
                src_ref=srcs[a].at[2 * px + py], dst_ref=lands[a].at[my_chip], send_sem=send_sems.at[3 * a + rel - 1],
                recv_sem=recv_sems.at[3 * a + rel - 1], device_id=(px, py, c), device_id_type=MESH))
    return copies


def _sum_adam(name, parts, w, m, v, own=None):
    r, cdim = w.shape
    n_parts = parts.shape[0]
    tr = r if (r % 8 or r <= 256) else math.gcd(r, 256)
    bc1 = 1.0 - ADAM_B1 ** ADAM_STEP
    bc2 = 1.0 - ADAM_B2 ** ADAM_STEP
    extra = [] if own is None else [own]

    def body(p_ref, *refs):
        w_ref, m_ref, v_ref, g_ref, d_ref, nm_ref, nv_ref = refs[len(extra):]
        if own is None:
            part = lambda k: p_ref[k].astype(F32)
        else:
            my_chip = 2 * lax.axis_index("x") + lax.axis_index("y")
            part = lambda k: jnp.where(my_chip == k, refs[0][k], p_ref[k]).astype(F32)
        g = part(0)
        for k in range(1, n_parts):
            g = g + part(k)
        mn = ADAM_B1 * m_ref[...] + (1.0 - ADAM_B1) * g
        vn = ADAM_B2 * v_ref[...] + (1.0 - ADAM_B2) * (g * g)
        g_ref[...] = g
        nm_ref[...] = mn
        nv_ref[...] = vn
        d_ref[...] = -ADAM_LR * ((mn / bc1) / (jnp.sqrt(vn / bc2) + ADAM_EPS) + ADAM_WD * w_ref[...])

    blk = pl.BlockSpec((tr, cdim), lambda i: (i, 0))
    o = jax.ShapeDtypeStruct((r, cdim), F32)
    return pl.pallas_call(
        body, name=name, grid=(r // tr,),
        in_specs=[pl.BlockSpec((n_parts, tr, cdim), lambda i: (0, i, 0))] * (1 + len(extra)) + [blk, blk, blk],
        out_specs=(blk, blk, blk, blk), out_shape=(o, o, o, o),
        compiler_params=_params())(parts, *extra, w, m, v)


_WEIGHTS = ("c_ctx", "ada_w", "ada_b", "norm_g", "w_in", "b_in", "conv_w", "conv_b", "conv_ln_g", "conv_ln_b",
            "conv_proj", "decay_up_fwd", "decay_bias_fwd", "decay_up_bwd", "decay_bias_bwd", "gla_norm_g",
            "gla_proj", "w_out", "final_norm_g")


def _as2d(a):
    if a.ndim == 1:
        return a.reshape(1, -1)
    return a.reshape(-1, a.shape[-1])


def kernel(x, c, ctx, c_ctx, ada_w, ada_b, norm_g, w_in, b_in, conv_w, conv_b, conv_ln_g, conv_ln_b, conv_proj, decay_up_fwd, decay_bias_fwd, decay_up_bwd, decay_bias_bwd, gla_norm_g, gla_proj, w_out, final_norm_g, loss_target, m_c_ctx, m_ada_w, m_ada_b, m_norm_g, m_w_in, m_b_in, m_conv_w, m_conv_b, m_conv_ln_g, m_conv_ln_b, m_conv_proj, m_decay_up_fwd, m_decay_bias_fwd, m_decay_up_bwd, m_decay_bias_bwd, m_gla_norm_g, m_gla_proj, m_w_out, m_final_norm_g, v_c_ctx, v_ada_w, v_ada_b, v_norm_g, v_w_in, v_b_in, v_conv_w, v_conv_b, v_conv_ln_g, v_conv_ln_b, v_conv_proj, v_decay_up_fwd, v_decay_bias_fwd, v_decay_up_bwd, v_decay_bias_bwd, v_gla_norm_g, v_gla_proj, v_w_out, v_final_norm_g):
    env = dict(locals())
    wts = {k: env[k] for k in _WEIGHTS}
    d = x.shape[-1]
    r = decay_up_fwd.shape[1]
    dk_ = d // 2
    n_in = w_in.shape[-1] * N_DEV

    ds, dks = d // N_DEV, dk_ // N_DEV
    g_win, g_ada, conv_w8, g_up = _all_gather(
        [w_in[0].astype(BF16), ada_w[0].astype(BF16), conv_w[0],
         jnp.concatenate([decay_up_fwd[0], decay_up_bwd[0]], axis=1)])
    proj_own = [conv_proj[0].astype(BF16), gla_proj[0].astype(BF16), w_out[0].astype(BF16)]
    me_i = 4 * lax.axis_index("x") + 2 * lax.axis_index("y") + lax.axis_index("c")
    proj_lands = [lax.dynamic_update_slice(lax.empty((N_DEV,) + a.shape, a.dtype), a[None], (me_i, 0, 0))
                  for a in proj_own]
    proj_start = _copies_start("proj_gather_start", proj_own, proj_lands, _gather_copies, 7 * 3)

    def proj(after):
        _, lands = _copies_wait("proj_gather_wait", proj_start, (after,), _gather_copies)
        return [w.reshape(d, d) for w in lands]

    w_a, w_b = _unshard_w_in(g_win, d, r, after=(proj_start[4],))
    up_f = g_up[:, :, 0:dks].transpose(1, 0, 2).reshape(r, dk_)
    up_b = g_up[:, :, dks:].transpose(1, 0, 2).reshape(r, dk_)
    up2 = jnp.zeros((LANE, 2 * dk_), F32).at[0:r, 0:dk_].set(up_f).at[r:2 * r, dk_:].set(up_b)
    bias2 = jnp.concatenate([decay_bias_fwd, decay_bias_bwd], axis=1)
    b_a, b_b = _regroup(b_in, d, r)

    names = ("w_in", "conv_proj", "gla_proj", "w_out", "conv_w", "decay_up")
    comm = {}

    def on_grads(gr):
        d_up = jnp.concatenate([gr["up2"][0:r, 0:dk_].reshape(r, N_DEV, dks).transpose(1, 0, 2),
                                gr["up2"][r:2 * r, dk_:].reshape(r, N_DEV, dks).transpose(1, 0, 2)], axis=2)
        mine = [_reshard_w_in(gr["w_a1"], gr["w_a2"], gr["w_b"], d, r), gr["conv_proj"].reshape(N_DEV, ds, d),
                gr["gla_proj"].reshape(N_DEV, ds, d), gr["w_out"].reshape(N_DEV, ds, d), gr["conv_w8"], d_up]
        lands = [lax.empty((4,) + a.shape[1:], a.dtype) for a in mine]
        comm["sibling"] = _copies_start("grad_sibling_start", mine, lands, _sibling_copies, 4 * len(mine))
        return (comm["sibling"][4],)

    def on_du_a1(du_a1):
        mine, theirs = _copies_wait("grad_sibling_wait", comm["sibling"], (du_a1,), _sibling_copies)
        sums = [_pair_sum("pair_sum_" + nm, a, b) for nm, a, b in zip(names, mine, theirs)]
        lands = [lax.empty(a.shape, a.dtype) for a in sums]
        comm["chips"] = _copies_start("grad_chips_start", sums, lands, _chip_copies, 3 * len(sums))
        return (comm["chips"][4],)

    g = _local_step(x, c, ctx, loss_target, c_ctx, g_ada, ada_b, norm_g[0:1], w_a, b_a, w_b, b_b,
                    conv_w8, conv_b, conv_ln_g, conv_ln_b, up2, bias2, gla_norm_g, final_norm_g.reshape(1, d),
                    proj, on_grads, on_du_a1)

    (their_ada,) = _exchange_sibling([g["ada_w8"]])
    ada_sum = _pair_sum("pair_sum_ada_w", g["ada_w8"], their_ada)
    ada_start = _copies_start("ada_chips_start", [ada_sum], [lax.empty(ada_sum.shape, ada_sum.dtype)],
                              _chip_copies, 3)
    own, landed = _copies_wait("grad_chips_wait", comm["chips"], (ada_start[4],), _chip_copies)
    o_win, o_cp, o_gp, o_wo, o_cw, o_up = own
    x_win, x_cp, x_gp, x_wo, x_cw, x_up = landed

    (packs,) = _all_gather([_pack_small(g, x.shape[0], d, r)])
    row = lambda a: a.reshape(1, -1)
    sg, sd, sm, sv, loss = _small_adam(packs, [row(wts[k]) for k in _SMALL], [row(env["m_" + k]) for k in _SMALL],
                                       [row(env["v_" + k]) for k in _SMALL], d, r)
    out = {}
    for i, k in enumerate(_SMALL):
        for pre, arrs in (("grad_", sg), ("delta_", sd), ("new_m_", sm), ("new_v_", sv)):
            out[pre + k] = arrs[i].reshape(wts[k].shape)
    loss = loss.reshape(())

    def big(name, parts, wname, own=None):
        w2 = _as2d(wts[wname])
        res = _sum_adam(name, parts, w2, _as2d(env["m_" + wname]), _as2d(env["v_" + wname]), own)
        for pre, arr in zip(("grad_", "delta_", "new_m_", "new_v_"), res):
            out[pre + wname] = arr.reshape(wts[wname].shape)

    big("adam_w_in", x_win, "w_in", o_win)
    big("adam_conv_proj", x_cp, "conv_proj", o_cp)
    big("adam_gla_proj", x_gp, "gla_proj", o_gp)
    big("adam_w_out", x_wo, "w_out", o_wo)
    big("adam_conv_w", x_cw, "conv_w", o_cw)
    big("adam_up_f", x_up[:, :, 0:dks], "decay_up_fwd", o_up[:, :, 0:dks])
    big("adam_up_b", x_up[:, :, dks:], "decay_up_bwd", o_up[:, :, dks:])
    (o_ada,), (x_ada,) = _copies_wait("ada_chips_wait", ada_start, (out["grad_w_in"], out["grad_w_out"], out["grad_b_in"]),
                                      _chip_copies)
    big("adam_ada_w", x_ada, "ada_w", o_ada)

    return (loss, g["grad_x"], *[out["grad_" + k] for k in _WEIGHTS], *[out["delta_" + k] for k in _WEIGHTS],
            *[out["new_m_" + k] for k in _WEIGHTS], *[out["new_v_" + k] for k in _WEIGHTS])
```

```python
import functools
import math

import jax
import jax.numpy as jnp
from jax import lax
from jax.experimental import pallas as pl
from jax.experimental.pallas import tpu as pltpu

F32 = jnp.float32
BF16 = jnp.bfloat16
MESH = pl.DeviceIdType.MESH

N_DEV = 8
GRID_W = 64
CHUNK = 128
HEADS = 4
EPS = 1e-6
GATE_TAU = 16.0
LANE = 128
ADAM_LR, ADAM_B1, ADAM_B2, ADAM_EPS, ADAM_WD, ADAM_STEP = 0.001, 0.9, 0.999, 1e-08, 0.01, 10
VMEM_LIMIT = 56 * 1024 * 1024
_ANY = pl.BlockSpec(memory_space=pl.ANY)


def _params(**kw):
    return pltpu.CompilerParams(vmem_limit_bytes=VMEM_LIMIT, **kw)


def _tile(n, pref):
    t = (min(pref, n) // LANE) * LANE
    while t >= LANE:
        if n % t == 0:
            return t
        t -= LANE
    return n


def _mm(a, b):
    return jnp.dot(a.astype(BF16), b.astype(BF16), preferred_element_type=F32)


def _mm_nt(a, b):
    return lax.dot_general(a.astype(BF16), b.astype(BF16), (((1,), (1,)), ((), ())), preferred_element_type=F32)


def _mm_tn(a, b):
    return lax.dot_general(a.astype(BF16), b.astype(BF16), (((0,), (0,)), ((), ())), preferred_element_type=F32)


def _mm_tn_hi(a, b):
    return lax.dot_general(a, b, (((0,), (0,)), ((), ())), precision=lax.Precision.HIGHEST, preferred_element_type=F32)


def _sigmoid(x):
    return 0.5 * jnp.tanh(0.5 * x) + 0.5


def _dsilu(x, s):
    return s * (1.0 + x * (1.0 - s))


def _rowsel(table, idx, n):
    out = table[0:1, :]
    for r in range(1, n):
        out = jnp.where(idx == r, table[r:r + 1, :], out)
    return out


def _ada_fwd(cv, ada_w8, ada_b):
    n_sh, _, ws = ada_w8.shape

    def body(cv_ref, w_ref, b_ref, o_ref):
        c = cv_ref[...]
        sv = c * _sigmoid(c)
        for j in range(n_sh):
            cols = pl.ds(j * ws, ws)
            o_ref[:, cols] = _mm(sv, w_ref[j]) + b_ref[:, cols]

    return pl.pallas_call(body, name="ada_fwd", out_shape=jax.ShapeDtypeStruct((cv.shape[0], n_sh * ws), F32),
                          compiler_params=_params())(cv, ada_w8, ada_b)


def _ada_bwd(cv, ada_w8, dmod_ss, small, nb):
    n_sh, d, ws = ada_w8.shape

    def body(cv_ref, w_ref, dm_ref, sm_ref, dw_ref, db_ref, dc_ref):
        c = cv_ref[...]
        s = _sigmoid(c)
        sv = c * s
        dm = jnp.concatenate([dm_ref[:, 0:2 * d], sm_ref[8:16, :]], axis=1)
        db_ref[...] = jnp.sum(dm, axis=0, keepdims=True)
        dsv = None
        for j in range(n_sh):
            dmj = dm[:, j * ws:(j + 1) * ws]
            dw_ref[j] = _mm_tn_hi(sv, dmj).astype(dw_ref.dtype)
            part = _mm_nt(dmj, w_ref[j])
            dsv = part if dsv is None else dsv + part
        dc_ref[...] = dsv * _dsilu(c, s)

    return pl.pallas_call(
        body, name="ada_bwd",
        out_shape=(jax.ShapeDtypeStruct((n_sh, d, ws), BF16), jax.ShapeDtypeStruct((1, n_sh * ws), F32),
                   jax.ShapeDtypeStruct(cv.shape, F32)),
        compiler_params=_params())(cv, ada_w8, dmod_ss, small)


class _Tiles:
    def __init__(self, nb, s_len, c_len, tm, big):
        self.nb, self.tm, self.big = nb, tm, big
        self.lat, self.ctx = s_len // tm, c_len // tm
        self.pad = -(self.lat + self.ctx) % big
        self.per_ex = self.lat + self.ctx + self.pad
        self.n_all, self.n_lat = nb * self.per_ex, nb * self.lat
        self.rows_per_ex = self.per_ex * tm

    def is_lat(self, i):
        return i % self.per_ex < self.lat

    def is_pad(self, i):
        return i % self.per_ex >= self.lat + self.ctx

    def lat_of_all(self, i):
        return (i // self.per_ex) * self.lat + jnp.minimum(i % self.per_ex, self.lat - 1)

    def ctx_of_all(self, i):
        return (i // self.per_ex) * self.ctx + jnp.clip(i % self.per_ex - self.lat, 0, self.ctx - 1)

    def big_all_of_lat(self, t):
        lat_big = self.lat // self.big
        return (t // lat_big) * (self.per_ex // self.big) + t % lat_big


def _norm_fwd(x2, ctx2, mod, norm_g, tiles):
    tl, d = x2.shape
    tc = ctx2.shape[0]
    nb, tm = tiles.nb, tiles.tm

    def body(x_ref, c_ref, mod_ref, g_ref, u_ref):
        i = pl.program_id(0)
        lat = tiles.is_lat(i)
        xv = jnp.where(lat, x_ref[...], c_ref[...])
        row = jnp.where(lat, i // tiles.per_ex, nb)
        m = _rowsel(mod_ref[...], row, nb + 1)
        shift, scale = m[:, 0:d], m[:, d:2 * d]
        rstd = lax.rsqrt(jnp.mean(xv * xv, axis=-1, keepdims=True) + EPS)
        u = xv * rstd * g_ref[...] * (1.0 + scale) + shift
        u_ref[...] = jnp.where(tiles.is_pad(i), 0.0, u).astype(BF16)

    return pl.pallas_call(
        body, name="norm_fwd", grid=(tiles.n_all,),
        in_specs=[pl.BlockSpec((tm, d), lambda i: (tiles.lat_of_all(i), 0)),
                  pl.BlockSpec((tm, d), lambda i: (tiles.ctx_of_all(i), 0)),
                  pl.BlockSpec(mod.shape, lambda i: (0, 0)),
                  pl.BlockSpec((1, d), lambda i: (0, 0))],
        out_specs=pl.BlockSpec((tm, d), lambda i: (i, 0)),
        out_shape=jax.ShapeDtypeStruct((tiles.n_all * tm, d), BF16),
        compiler_params=_params())(x2, ctx2, mod, norm_g)


def _norm_bwd(x2, ctx2, mod, norm_g, du_lat, du_b, gx1, tiles):
    tl, d = x2.shape
    nb, tm = tiles.nb, tiles.tm
    nrow = mod.shape[0]
    n_lat_in = len(du_lat)

    def body(x_ref, c_ref, mod_ref, g_ref, *refs):
        dl_refs = refs[:n_lat_in]
        d3_ref, gx_ref, gxo_ref, dmod_ref, dg_ref = refs[n_lat_in:]
        i = pl.program_id(0)

        @pl.when(i == 0)
        def _():
            dmod_ref[...] = jnp.zeros_like(dmod_ref)
            dg_ref[...] = jnp.zeros_like(dg_ref)

        lat = tiles.is_lat(i)
        xv = jnp.where(lat, x_ref[...], c_ref[...])
        row = jnp.where(lat, i // tiles.per_ex, nb)
        m = _rowsel(mod_ref[...], row, nb + 1)
        scale = m[:, d:2 * d]
        g = g_ref[...]
        dl = dl_refs[0][...]
        for ref in dl_refs[1:]:
            dl = dl + ref[...]
        du = jnp.where(tiles.is_pad(i), 0.0, d3_ref[...] + jnp.where(lat, dl, 0.0))
        rstd = lax.rsqrt(jnp.mean(xv * xv, axis=-1, keepdims=True) + EPS)
        xh = xv * rstd
        dshift = jnp.sum(du, axis=0, keepdims=True)
        dscale = jnp.sum(du * xh * g, axis=0, keepdims=True)
        dxn = du * (1.0 + scale)
        dg_ref[...] += jnp.sum(dxn * xh, axis=0, keepdims=True)
        dxh = dxn * g
        dx = rstd * (dxh - xh * jnp.mean(dxh * xh, axis=-1, keepdims=True))

        @pl.when(lat)
        def _():
            gxo_ref[...] = dx + gx_ref[...]

        for r in range(nb + 1):
            dmod_ref[r:r + 1, 0:d] += jnp.where(row == r, dshift, 0.0)
            dmod_ref[r:r + 1, d:2 * d] += jnp.where(row == r, dscale, 0.0)

    lat_map = lambda i: (tiles.lat_of_all(i), 0)
    lat_spec = pl.BlockSpec((tm, d), lat_map)
    return pl.pallas_call(
        body, name="norm_bwd", grid=(tiles.n_all,),
        in_specs=[lat_spec,
                  pl.BlockSpec((tm, d), lambda i: (tiles.ctx_of_all(i), 0)),
                  pl.BlockSpec(mod.shape, lambda i: (0, 0)),
                  pl.BlockSpec((1, d), lambda i: (0, 0))]
                 + [lat_spec] * n_lat_in
                 + [pl.BlockSpec((tm, d), lambda i: (i, 0)), lat_spec],
        out_specs=(lat_spec,
                   pl.BlockSpec((nrow, 3 * d), lambda i: (0, 0)),
                   pl.BlockSpec((1, d), lambda i: (0, 0))),
        out_shape=(jax.ShapeDtypeStruct((tl, d), F32), jax.ShapeDtypeStruct((nrow, 3 * d), F32),
                   jax.ShapeDtypeStruct((1, d), F32)),
        compiler_params=_params())(x2, ctx2, mod, norm_g, *du_lat, du_b, gx1)


def _matmul_bias(name, u, w, b, rows, tm, tn, u_tile, col0, ncols, out_dtype):
    d = w.shape[0]
    j0 = col0 // tn

    def body(u_ref, w_ref, b_ref, o_ref):
        o_ref[...] = (jnp.dot(u_ref[...], w_ref[...], preferred_element_type=F32) + b_ref[...]).astype(o_ref.dtype)

    return pl.pallas_call(
        body, name=name, grid=(ncols // tn, rows // tm),
        in_specs=[pl.BlockSpec((tm, d), lambda j, i: (u_tile(i), 0)),
                  pl.BlockSpec((d, tn), lambda j, i: (0, j0 + j)),
                  pl.BlockSpec((1, tn), lambda j, i: (0, j0 + j))],
        out_specs=pl.BlockSpec((tm, tn), lambda j, i: (i, j)),
        out_shape=jax.ShapeDtypeStruct((rows, ncols), out_dtype),
        compiler_params=_params())(u, w, b)


def _inproj_b(u, w_b, b_b, tm, dk_, dv_):
    t_all, d = u.shape
    nbw = w_b.shape[1]

    def body(u_ref, w_ref, b_ref, qk_ref, v_ref):
        full = jnp.dot(u_ref[...], w_ref[...], preferred_element_type=F32) + b_ref[...]
        qk_ref[:, 0:2 * dk_] = full[:, 0:2 * dk_]
        qk_ref[:, 2 * dk_:2 * dk_ + LANE] = full[:, 2 * dk_ + dv_:nbw]
        v_ref[...] = full[:, 2 * dk_:2 * dk_ + dv_].astype(BF16)

    return pl.pallas_call(
        body, name="inproj_b", grid=(t_all // tm,),
        in_specs=[pl.BlockSpec((tm, d), lambda i: (i, 0)), pl.BlockSpec((d, nbw), lambda i: (0, 0)),
                  pl.BlockSpec((1, nbw), lambda i: (0, 0))],
        out_specs=(pl.BlockSpec((tm, 2 * dk_ + LANE), lambda i: (i, 0)), pl.BlockSpec((tm, dv_), lambda i: (i, 0))),
        out_shape=(jax.ShapeDtypeStruct((t_all, 2 * dk_ + LANE), F32), jax.ShapeDtypeStruct((t_all, dv_), BF16)),
        compiler_params=_params())(u, w_b, b_b)


def _matmul_nt(name, a, w, koff, tm, tk, after=()):
    r, kc = a.shape
    d = w.shape[0]
    nk = kc // tk

    def body(a_ref, w_ref, *rest):
        o_ref = rest[len(after)]
        k = pl.program_id(1)
        p = lax.dot_general(a_ref[...], w_ref[...], (((1,), (1,)), ((), ())), preferred_element_type=F32)

        @pl.when(k == 0)
        def _():
            o_ref[...] = p

        @pl.when(k > 0)
        def _():
            o_ref[...] += p

    return pl.pallas_call(
        body, name=name, grid=(r // tm, nk),
        in_specs=[pl.BlockSpec((tm, tk), lambda i, k: (i, k)),
                  pl.BlockSpec((d, tk), lambda i, k: (0, koff + k))] + [_ANY] * len(after),
        out_specs=pl.BlockSpec((tm, d), lambda i, k: (i, 0)),
        out_shape=jax.ShapeDtypeStruct((r, d), F32),
        compiler_params=_params())(a, w, *after)


def _matmul_tn(name, a, b, rows, tk, tn):
    m = a.shape[1]
    n = b.shape[1]
    nk = rows // tk

    def body(a_ref, b_ref, o_ref, s_ref, acc_ref):
        k = pl.program_id(1)
        bv = b_ref[...]
        p = lax.dot_general(a_ref[...], bv, (((0,), (0,)), ((), ())), preferred_element_type=F32)
        cs = jnp.sum(bv.astype(F32), axis=0, keepdims=True)

        @pl.when(k == 0)
        def _():
            acc_ref[...] = p
            s_ref[...] = cs

        @pl.when(k > 0)
        def _():
            acc_ref[...] += p
            s_ref[...] += cs

        @pl.when(k == nk - 1)
        def _():
            o_ref[...] = acc_ref[...].astype(o_ref.dtype)

    return pl.pallas_call(
        body, name=name, grid=(n // tn, nk),
        in_specs=[pl.BlockSpec((tk, m), lambda j, k: (k, 0)),
                  pl.BlockSpec((tk, tn), lambda j, k: (k, j))],
        out_specs=(pl.BlockSpec((m, tn), lambda j, k: (0, j)), pl.BlockSpec((1, tn), lambda j, k: (0, j))),
        out_shape=(jax.ShapeDtypeStruct((m, n), BF16), jax.ShapeDtypeStruct((1, n), F32)),
        scratch_shapes=[pltpu.VMEM((m, tn), F32)],
        compiler_params=_params())(a, b)


def _matmul_tn_whole(name, a3, b3, rows, tn):
    nb, _, m = a3.shape
    n = b3.shape[2]

    def body(a_ref, b_ref, o_ref, s_ref):
        p, cs = None, None
        for e in range(nb):
            bv = b_ref[e]
            pe = lax.dot_general(a_ref[e], bv, (((0,), (0,)), ((), ())), preferred_element_type=F32)
            ce = jnp.sum(bv.astype(F32), axis=0, keepdims=True)
            p, cs = (pe, ce) if p is None else (p + pe, cs + ce)
        o_ref[...] = p.astype(o_ref.dtype)
        s_ref[...] = cs

    return pl.pallas_call(
        body, name=name, grid=(n // tn,),
        in_specs=[pl.BlockSpec((nb, rows, m), lambda j: (0, 0, 0)),
                  pl.BlockSpec((nb, rows, tn), lambda j: (0, 0, j))],
        out_specs=(pl.BlockSpec((m, tn), lambda j: (0, j)), pl.BlockSpec((1, tn), lambda j: (0, j))),
        out_shape=(jax.ShapeDtypeStruct((m, n), BF16), jax.ShapeDtypeStruct((1, n), F32)),
        compiler_params=_params())(a3, b3)


def _conv_window(pad_ref, r, shift, ktaps, width, horizontal):
    if horizontal:
        return pad_ref[r, pl.ds(16 + shift, width), :]
    return pad_ref[r + ktaps // 2 + shift]


def _conv_row(pad_ref, w, r, ktaps, width, horizontal, flip):
    half = ktaps // 2
    acc = None
    for t in range(ktaps):
        win = _conv_window(pad_ref, r, (half - t) if flip else (t - half), ktaps, width, horizontal)
        term = win * w[t:t + 1, :]
        acc = term if acc is None else acc + term
    return acc


def _fill_padded(ref, val, rows, width, ktaps, horizontal):
    half_k = ktaps // 2
    cb = val.shape[-1]
    if horizontal:
        ref[:, 0:16, :] = jnp.zeros((rows, 16, cb), F32)
        ref[:, 16 + width:32 + width, :] = jnp.zeros((rows, 16, cb), F32)
        ref[:, 16:16 + width, :] = val
    else:
        ref[0:half_k, :, :] = jnp.zeros((half_k, width, cb), F32)
        ref[half_k + rows:2 * half_k + rows, :, :] = jnp.zeros((half_k, width, cb), F32)
        ref[half_k:half_k + rows, :, :] = val


def _conv_fwd(pa, conv_w8, conv_b, nb, s):
    nblk, ktaps, cb = conv_w8.shape
    d = nblk * cb
    rows, width = s // GRID_W, GRID_W
    half_k = ktaps // 2
    nh = nblk // 2

    def body(glu_ref, w_ref, b_ref, o_ref, ph_ref, pv_ref):
        j = pl.program_id(1)
        a0 = (glu_ref[:, 0:cb] * _sigmoid(glu_ref[:, cb:2 * cb])).reshape(rows, width, cb)
        w = w_ref[...]

        bias = b_ref[...]

        def run(pad_ref, horizontal):
            _fill_padded(pad_ref, a0, rows, width, ktaps, horizontal)

            def row(r, carry):
                at = pl.ds(pl.multiple_of(r * width, width), width)
                o_ref[at, :] = _conv_row(pad_ref, w, r, ktaps, width, horizontal, False) + bias
                return carry

            lax.fori_loop(0, rows, row, 0)

        @pl.when(j < nh)
        def _():
            run(ph_ref, True)

        @pl.when(j >= nh)
        def _():
            run(pv_ref, False)

    return pl.pallas_call(
        body, name="conv_fwd", grid=(nb, nblk),
        in_specs=[pl.BlockSpec((s, 2 * cb), lambda b, j: (b, j)),
                  pl.BlockSpec((None, ktaps, cb), lambda b, j: (j, 0, 0)),
                  pl.BlockSpec((1, cb), lambda b, j: (0, j))],
        out_specs=pl.BlockSpec((s, cb), lambda b, j: (b, j)),
        out_shape=jax.ShapeDtypeStruct((nb * s, d), F32),
        scratch_shapes=[pltpu.VMEM((rows, width + 32, cb), F32), pltpu.VMEM((rows + 2 * half_k, width, cb), F32)],
        compiler_params=_params())(pa, conv_w8, conv_b)


def _conv_bwd(pa, da1, conv_w8, nb, s):
    nblk, ktaps, cb = conv_w8.shape
    d = nblk * cb
    rows, width = s // GRID_W, GRID_W
    half_k = ktaps // 2
    nh = nblk // 2

    def body(glu_ref, da_ref, w_ref, dp_ref, dw_ref, db_ref, pha_ref, phd_ref, pva_ref, pvd_ref):
        j = pl.program_id(0)
        b = pl.program_id(1)
        a0 = (glu_ref[:, 0:cb] * _sigmoid(glu_ref[:, cb:2 * cb])).reshape(rows, width, cb)
        da1v = da_ref[...]
        d3 = da1v.reshape(rows, width, cb)
        w = w_ref[...]

        @pl.when(b == 0)
        def _():
            dw_ref[...] = jnp.zeros_like(dw_ref)
            db_ref[...] = jnp.zeros_like(db_ref)

        db_ref[...] += jnp.sum(da1v, axis=0, keepdims=True)

        def run(pa_ref, pd_ref, horizontal):
            _fill_padded(pa_ref, a0, rows, width, ktaps, horizontal)
            _fill_padded(pd_ref, d3, rows, width, ktaps, horizontal)

            def row(r, accs):
                at = pl.ds(pl.multiple_of(r * width, width), width)
                da0 = _conv_row(pd_ref, w, r, ktaps, width, horizontal, True)
                gv = glu_ref[at, 0:cb]
                sg = _sigmoid(glu_ref[at, cb:2 * cb])
                dp_ref[at, 0:cb] = (da0 * sg).astype(BF16)
                dp_ref[at, cb:2 * cb] = (da0 * gv * sg * (1.0 - sg)).astype(BF16)
                d_row = da_ref[at, :]
                out = []
                for t in range(ktaps):
                    prod = _conv_window(pa_ref, r, t - half_k, ktaps, width, horizontal) * d_row
                    out.append(accs[t] + jnp.sum(prod.reshape(width // 8, 8, cb), axis=0))
                return tuple(out)

            accs = lax.fori_loop(0, rows, row, tuple(jnp.zeros((8, cb), F32) for _ in range(ktaps)))
            for t in range(ktaps):
                dw_ref[t:t + 1, :] += jnp.sum(accs[t], axis=0, keepdims=True)

        @pl.when(j < nh)
        def _():
            run(pha_ref, phd_ref, True)

        @pl.when(j >= nh)
        def _():
            run(pva_ref, pvd_ref, False)

    return pl.pallas_call(
        body, name="conv_bwd", grid=(nblk, nb),
        in_specs=[pl.BlockSpec((s, 2 * cb), lambda j, b: (b, j)),
                  pl.BlockSpec((s, cb), lambda j, b: (b, j)),
                  pl.BlockSpec((None, ktaps, cb), lambda j, b: (j, 0, 0))],
        out_specs=(pl.BlockSpec((s, 2 * cb), lambda j, b: (b, j)),
                   pl.BlockSpec((None, ktaps, cb), lambda j, b: (j, 0, 0)),
                   pl.BlockSpec((1, cb), lambda j, b: (0, j))),
        out_shape=(jax.ShapeDtypeStruct((nb * s, 2 * d), BF16),
                   jax.ShapeDtypeStruct((nblk, ktaps, cb), F32), jax.ShapeDtypeStruct((1, d), F32)),
        scratch_shapes=[pltpu.VMEM((rows, width + 32, cb), F32), pltpu.VMEM((rows, width + 32, cb), F32),
                        pltpu.VMEM((rows + 2 * half_k, width, cb), F32),
                        pltpu.VMEM((rows + 2 * half_k, width, cb), F32)],
        compiler_params=_params())(pa, da1, conv_w8)


def _log_sigmoid(x):
    return jnp.minimum(x, 0.0) - jnp.log(1.0 + jnp.exp(-jnp.abs(x)))


def _decay_fwd(pb, up2, bias2, tm, lr_blk):
    t_all = pb.shape[0]
    n2 = up2.shape[1]

    def body(lr_ref, up_ref, b_ref, g_ref):
        logits = _mm(lr_ref[...], up_ref[...]) + b_ref[...]
        g_ref[...] = _log_sigmoid(logits) * (1.0 / GATE_TAU)

    return pl.pallas_call(
        body, name="decay_fwd", grid=(t_all // tm,),
        in_specs=[pl.BlockSpec((tm, LANE), lambda i: (i, lr_blk)),
                  pl.BlockSpec(up2.shape, lambda i: (0, 0)),
                  pl.BlockSpec((1, n2), lambda i: (0, 0))],
        out_specs=pl.BlockSpec((tm, n2), lambda i: (i, 0)),
        out_shape=jax.ShapeDtypeStruct((t_all, n2), F32),
        compiler_params=_params())(pb, up2, bias2)


def _decay_bwd(pb, up2, bias2, grads_f, grads_b, tiles, lr_blk, dk_, dv_):
    t_all = pb.shape[0]
    tm = tiles.tm
    n2 = up2.shape[1]
    nbw = 2 * dk_ + dv_ + LANE

    def body(lr_ref, up_ref, b_ref, dqf, dkf, dvf, dgf, dqb, dkb, dvb, dgb, dp_ref, dup_ref, dbias_ref):
        i = pl.program_id(0)
        pad = tiles.is_pad(i)
        live = lambda v: jnp.where(pad, 0.0, v)

        @pl.when(i == 0)
        def _():
            dup_ref[...] = jnp.zeros_like(dup_ref)
            dbias_ref[...] = jnp.zeros_like(dbias_ref)

        lr = lr_ref[...]
        up = up_ref[...]
        logits = _mm(lr, up) + b_ref[...]
        dg = live(jnp.concatenate([dgf[...], dgb[...]], axis=1))
        dlog = dg * (1.0 / GATE_TAU) * _sigmoid(-logits)
        dup_ref[...] += _mm_tn(lr, dlog)
        dbias_ref[...] += jnp.sum(dlog, axis=0, keepdims=True)
        both = lambda f, b: live(f[...].astype(F32) + b[...].astype(F32)).astype(BF16)
        dp_ref[:, 0:dk_] = both(dqf, dqb)
        dp_ref[:, dk_:2 * dk_] = both(dkf, dkb)
        dp_ref[:, 2 * dk_:2 * dk_ + dv_] = both(dvf, dvb)
        dp_ref[:, 2 * dk_ + dv_:nbw] = _mm_nt(dlog, up).astype(BF16)

    row = lambda w: pl.BlockSpec((tm, w), lambda i: (i, 0))
    return pl.pallas_call(
        body, name="decay_bwd", grid=(t_all // tm,),
        in_specs=[pl.BlockSpec((tm, LANE), lambda i: (i, lr_blk)),
                  pl.BlockSpec(up2.shape, lambda i: (0, 0)),
                  pl.BlockSpec((1, n2), lambda i: (0, 0)),
                  row(dk_), row(dk_), row(dv_), row(dk_), row(dk_), row(dk_), row(dv_), row(dk_)],
        out_specs=(row(nbw), pl.BlockSpec(up2.shape, lambda i: (0, 0)), pl.BlockSpec((1, n2), lambda i: (0, 0))),
        out_shape=(jax.ShapeDtypeStruct((t_all, nbw), BF16), jax.ShapeDtypeStruct(up2.shape, F32),
                   jax.ShapeDtypeStruct((1, n2), F32)),
        compiler_params=_params())(pb, up2, bias2, *grads_f, *grads_b)


def _scan_chunk(s, nl, nc, rev):
    if rev:
        return jnp.where(s < nc, nl + (nc - 1 - s), nl - 1 - (s - nc))
    return jnp.where(s < nc, nl + s, s - nc)


def _scan_lat_chunk(s, nl, nc, rev):
    first = nl - 1 if rev else 0
    return jnp.where(s < nc, first, _scan_chunk(s, nl, nc, rev))


def _tri_mm(m_bf, x):
    hi = x.astype(BF16)
    r1 = x - hi.astype(F32)
    mid = r1.astype(BF16)
    lo = (r1 - mid.astype(F32)).astype(BF16)
    dot = lambda p: jnp.dot(m_bf, p, preferred_element_type=F32)
    return dot(hi) + dot(mid) + dot(lo)


def _chunk_masks(c, rev):
    ii = lax.broadcasted_iota(jnp.int32, (c, c), 0)
    jj = lax.broadcasted_iota(jnp.int32, (c, c), 1)
    return ((ii <= jj), (ii >= jj)) if rev else ((ii >= jj), (ii <= jj))


def _chunk_terms(q, k, b, far, mid):
    bf, bm = b[far:far + 1, :], b[mid:mid + 1, :]
    e = jnp.exp(b)
    em = jnp.exp(b - bm)
    eim = jnp.exp(bm - b)
    ed = jnp.exp(bf - b)
    return dict(e=e, em=em, eim=eim, ed=ed, dec=jnp.exp(bf), qe=q * e, qem=q * em, kim=k * eim, kd=k * ed)


def _gla_fwd(pb3, pv3, g3, nb, s_len, c_len, dk_, dv_):
    c = CHUNK
    nl, nc = s_len // c, c_len // c
    ns = nl + nc
    hk, hv = dk_ // HEADS, dv_ // HEADS
    l_len = pb3.shape[1]
    scale = hk ** -0.5
    mid = c // 2

    def body(*refs):
        ins, outs, z_scr = refs[:8], refs[8:14], refs[14]
        s = pl.program_id(0)

        @pl.when(s == 0)
        def _():
            z_scr[...] = jnp.zeros_like(z_scr)

        qs = jnp.where(s >= nc, scale, 0.0)
        for di, rev in enumerate((False, True)):
            q_ref, k_ref, v_ref, g_ref = ins[4 * di:4 * di + 4]
            o_ref, zs_ref, b_ref = outs[3 * di:3 * di + 3]
            mask, _ = _chunk_masks(c, rev)
            m_bf = mask.astype(BF16)
            far = 0 if rev else c - 1
            for b in range(nb):
                bc = _tri_mm(m_bf, g_ref[b])
                b_ref[b] = bc
                for h in range(HEADS):
                    ks, vs = slice(h * hk, (h + 1) * hk), slice(h * hv, (h + 1) * hv)
                    zi = (di * nb + b) * HEADS + h
                    v = v_ref[b, :, vs]
                    t = _chunk_terms(q_ref[b, :, ks] * qs, k_ref[b, :, ks], bc[:, ks], far, mid)
                    a = jnp.where(mask, _mm_nt(t["qem"], t["kim"]), 0.0)
                    z = z_scr[zi]
                    zs_ref[0, b * HEADS + h] = z
                    o_ref[b, :, vs] = _mm(a, v) + _mm_nt(t["qe"], z)
                    z_scr[zi] = z * t["dec"] + _mm_tn(v, t["kd"])

    in_specs, out_specs, out_shape = [], [], []
    for di, rev in enumerate((False, True)):
        ch = functools.partial(_scan_chunk, nl=nl, nc=nc, rev=rev)
        lch = functools.partial(_scan_lat_chunk, nl=nl, nc=nc, rev=rev)
        in_specs += [pl.BlockSpec((nb, c, dk_), lambda s, ch=ch: (0, ch(s), 0)),
                     pl.BlockSpec((nb, c, dk_), lambda s, ch=ch: (0, ch(s), 1)),
                     pl.BlockSpec((nb, c, dv_), lambda s, ch=ch: (0, ch(s), 0)),
                     pl.BlockSpec((nb, c, dk_), lambda s, ch=ch, di=di: (0, ch(s), di))]
        out_specs += [pl.BlockSpec((nb, c, dv_), lambda s, lch=lch: (0, lch(s), 0)),
                      pl.BlockSpec((1, nb * HEADS, hv, hk), lambda s: (s, 0, 0, 0)),
                      pl.BlockSpec((nb, c, dk_), lambda s, ch=ch: (0, ch(s), 0))]
        out_shape += [jax.ShapeDtypeStruct((nb, s_len, dv_), F32),
                      jax.ShapeDtypeStruct((ns, nb * HEADS, hv, hk), F32),
                      jax.ShapeDtypeStruct((nb, l_len, dk_), F32)]
    return pl.pallas_call(
        body, name="gla_fwd", grid=(ns,), in_specs=in_specs, out_specs=tuple(out_specs), out_shape=tuple(out_shape),
        scratch_shapes=[pltpu.VMEM((2 * nb * HEADS, hv, hk), F32)],
        compiler_params=_params())(pb3, pb3, pv3, g3, pb3, pb3, pv3, g3)


def _gla_bwd(pb3, pv3, do3, fwd_saved, nb, s_len, c_len, dk_, dv_):
    c = CHUNK
    nl, nc = s_len // c, c_len // c
    ns = nl + nc
    hk, hv = dk_ // HEADS, dv_ // HEADS
    l_len = pb3.shape[1]
    scale = hk ** -0.5
    mid = c // 2
    zs_f, b_f, zs_b, b_b = fwd_saved

    def body(*refs):
        ins, outs, dz_scr = refs[:12], refs[12:20], refs[20]
        s = pl.program_id(0)
        step = ns - 1 - s

        @pl.when(s == 0)
        def _():
            dz_scr[...] = jnp.zeros_like(dz_scr)

        lat = step >= nc
        qs = jnp.where(lat, scale, 0.0)
        dmul = jnp.where(lat, 1.0, 0.0)
        for di, rev in enumerate((False, True)):
            q_ref, k_ref, v_ref, b_ref, do_ref, zs_ref = ins[6 * di:6 * di + 6]
            dq_ref, dk_ref, dv_ref, dg_ref = outs[4 * di:4 * di + 4]
            mask, mask_t = _chunk_masks(c, rev)
            mt_bf = mask_t.astype(BF16)
            far = 0 if rev else c - 1
            far_row = lax.broadcasted_iota(jnp.int32, (c, hk), 0) == far
            for b in range(nb):
                db_parts = []
                for h in range(HEADS):
                    ks, vs = slice(h * hk, (h + 1) * hk), slice(h * hv, (h + 1) * hv)
                    zi = (di * nb + b) * HEADS + h
                    v = v_ref[b, :, vs]
                    d_o = do_ref[b, :, vs] * dmul
                    t = _chunk_terms(q_ref[b, :, ks] * qs, k_ref[b, :, ks], b_ref[b, :, ks], far, mid)
                    qem, kim, qe, kd = t["qem"], t["kim"], t["qe"], t["kd"]
                    a_t = jnp.where(mask_t, _mm_nt(kim, qem), 0.0)
                    d_a = jnp.where(mask, _mm_nt(d_o, v), 0.0)
                    d_at = jnp.where(mask_t, _mm_nt(v, d_o), 0.0)
                    z = zs_ref[0, b * HEADS + h]
                    dzn = dz_scr[zi]
                    dv_ref[b, :, vs] = (_mm(a_t, d_o) + _mm_nt(kd, dzn)).astype(dv_ref.dtype)
                    dqem = _mm(d_a, kim)
                    dkim = _mm(d_at, qem)
                    dqe = _mm(d_o, z)
                    dkd = _mm(v, dzn)
                    ddec = jnp.sum(z * dzn, axis=0, keepdims=True)
                    dz_scr[zi] = dzn * t["dec"] + _mm_tn(d_o, qe)
                    dq_ref[b, :, ks] = ((dqem * t["em"] + dqe * t["e"]) * qs).astype(dq_ref.dtype)
                    dk_ref[b, :, ks] = (dkim * t["eim"] + dkd * t["ed"]).astype(dk_ref.dtype)
                    db = dqem * qem - dkim * kim + dqe * qe - dkd * kd
                    extra = jnp.sum(dkd * kd, axis=0, keepdims=True) + ddec * t["dec"]
                    db_parts.append(db + jnp.where(far_row, extra, 0.0))
                dg_ref[b] = _tri_mm(mt_bf, jnp.concatenate(db_parts, axis=1))

    in_specs, out_specs, out_shape, args = [], [], [], []
    for di, rev in enumerate((False, True)):
        ch = lambda s, rev=rev: _scan_chunk(ns - 1 - s, nl, nc, rev)
        lch = lambda s, rev=rev: _scan_lat_chunk(ns - 1 - s, nl, nc, rev)
        in_specs += [pl.BlockSpec((nb, c, dk_), lambda s, ch=ch: (0, ch(s), 0)),
                     pl.BlockSpec((nb, c, dk_), lambda s, ch=ch: (0, ch(s), 1)),
                     pl.BlockSpec((nb, c, dv_), lambda s, ch=ch: (0, ch(s), 0)),
                     pl.BlockSpec((nb, c, dk_), lambda s, ch=ch: (0, ch(s), 0)),
                     pl.BlockSpec((nb, c, dv_), lambda s, lch=lch: (0, lch(s), 0)),
                     pl.BlockSpec((1, nb * HEADS, hv, hk), lambda s: (ns - 1 - s, 0, 0, 0))]
        args += [pb3, pb3, pv3, (b_b if rev else b_f), do3, (zs_b if rev else zs_f)]
        for w, dt in ((dk_, BF16), (dk_, BF16), (dv_, BF16), (dk_, F32)):
            out_specs.append(pl.BlockSpec((nb, c, w), lambda s, ch=ch: (0, ch(s), 0)))
            out_shape.append(jax.ShapeDtypeStruct((nb, l_len, w), dt))
    return pl.pallas_call(
        body, name="gla_bwd", grid=(ns,), in_specs=in_specs, out_specs=tuple(out_specs), out_shape=tuple(out_shape),
        scratch_shapes=[pltpu.VMEM((2 * nb * HEADS, hv, hk), F32)],
        compiler_params=_params())(*args)


def _tail(a1, pa2, o_f, o_b, x2, tgt, mod, wc, wg, wo, ln_g, ln_b, gn_t, fg, nb, tm):
    tl, d = x2.shape
    nt = tl // tm
    per_ex = nt // nb
    hv = d // HEADS
    nrow = mod.shape[0]

    def body(a1_ref, z_ref, r_ref, mc_ref, mg_ref, of_ref, ob_ref, x_ref, t_ref, mod_ref, wc_ref, wg_ref, wo_ref,
             lng_ref, lnb_ref, gn_ref, fg_ref,
             dp_ref, da1_ref, do_ref, gx_ref, mrg_ref, dmo_ref, yci_ref, dyc_ref, ogi_ref, dyg_ref, sm_ref):
        i = pl.program_id(0)

        @pl.when(i == 0)
        def _():
            sm_ref[...] = jnp.zeros_like(sm_ref)

        bidx = i // per_ex
        gate = _rowsel(mod_ref[...], bidx, nb)[:, 2 * d:3 * d]
        lng, lnb, fgv = lng_ref[...], lnb_ref[...], fg_ref[...]
        gn = jnp.concatenate([gn_ref[...]] * HEADS, axis=1)
        wc_, wg_, wo_ = wc_ref[...], wg_ref[...], wo_ref[...]

        a1v = a1_ref[...]
        mu = jnp.mean(a1v, axis=-1, keepdims=True)
        xc = a1v - mu
        rs = lax.rsqrt(jnp.mean(xc * xc, axis=-1, keepdims=True) + EPS)
        xh = xc * rs
        a2 = xh * lng + lnb
        s2 = _sigmoid(a2)
        a3 = a2 * s2
        zv = z_ref[...].astype(F32)
        sz = _sigmoid(zv)
        siluz = zv * sz
        ycin = a3 * siluz
        yconv = _mm(ycin, wc_)

        o = of_ref[...] + ob_ref[...]
        ohat_parts, rn_parts = [], []
        for h in range(HEADS):
            oh = o[:, h * hv:(h + 1) * hv]
            rn = lax.rsqrt(jnp.mean(oh * oh, axis=-1, keepdims=True) + EPS)
            ohat_parts.append(oh * rn)
            rn_parts.append(rn)
        ohat = jnp.concatenate(ohat_parts, axis=1)
        on = ohat * gn
        rv = r_ref[...].astype(F32)
        sr = _sigmoid(rv)
        silur = rv * sr
        ogin = on * silur
        ygla = _mm(ogin, wg_)

        sc = _sigmoid(mc_ref[...].astype(F32))
        sg = _sigmoid(mg_ref[...].astype(F32))
        merged = sc * yconv + sg * ygla
        mo = _mm(merged, wo_)
        hn = x_ref[...] + gate * mo
        rf = lax.rsqrt(jnp.mean(hn * hn, axis=-1, keepdims=True) + EPS)
        yh = hn * rf
        err = yh * fgv - t_ref[...]
        loss_part = 0.5 * jnp.sum(err * err) * (1.0 / d)

        dy = err * (1.0 / d)
        dfg = jnp.sum(dy * yh, axis=0, keepdims=True)
        dyh = dy * fgv
        dhn = rf * (dyh - yh * jnp.mean(dyh * yh, axis=-1, keepdims=True))
        gx_ref[...] = dhn
        dgate = jnp.sum(dhn * mo, axis=0, keepdims=True)
        dmo = gate * dhn
        dmerged = _mm_nt(dmo, wo_)
        dyconv = dmerged * sc
        dygla = dmerged * sg
        dp_ref[:, 2 * d:3 * d] = (dmerged * yconv * sc * (1.0 - sc)).astype(BF16)
        dp_ref[:, 3 * d:4 * d] = (dmerged * ygla * sg * (1.0 - sg)).astype(BF16)
        dycin = _mm_nt(dyconv, wc_)
        dogin = _mm_nt(dygla, wg_)
        mrg_ref[...] = merged.astype(BF16)
        dmo_ref[...] = dmo.astype(BF16)
        yci_ref[...] = ycin.astype(BF16)
        dyc_ref[...] = dyconv.astype(BF16)
        ogi_ref[...] = ogin.astype(BF16)
        dyg_ref[...] = dygla.astype(BF16)

        da3 = dycin * siluz
        dp_ref[:, 0:d] = (dycin * a3 * _dsilu(zv, sz)).astype(BF16)
        da2 = da3 * _dsilu(a2, s2)
        dlng = jnp.sum(da2 * xh, axis=0, keepdims=True)
        dlnb = jnp.sum(da2, axis=0, keepdims=True)
        dxh = da2 * lng
        da1_ref[...] = rs * (dxh - jnp.mean(dxh, axis=-1, keepdims=True)
                             - xh * jnp.mean(dxh * xh, axis=-1, keepdims=True))

        don = dogin * silur
        dp_ref[:, d:2 * d] = (dogin * on * _dsilu(rv, sr)).astype(BF16)
        dgn = jnp.sum(don * ohat, axis=0, keepdims=True)
        dyn = don * gn
        for h in range(HEADS):
            vs = slice(h * hv, (h + 1) * hv)
            oh_hat = ohat_parts[h]
            dh = dyn[:, vs]
            do_ref[:, vs] = (rn_parts[h] * (dh - oh_hat * jnp.mean(dh * oh_hat, axis=-1, keepdims=True))
                             ).astype(BF16)

        sm_ref[0:1, :] += dfg
        sm_ref[1:2, :] += dlng
        sm_ref[2:3, :] += dlnb
        sm_ref[3:4, :] += dgn
        sm_ref[4:5, :] += jnp.zeros((1, d), F32) + loss_part
        for b in range(nb):
            sm_ref[8 + b:9 + b, :] += jnp.where(bidx == b, dgate, 0.0)

    row = pl.BlockSpec((tm, d), lambda i: (i, 0))
    pcol = lambda blk: pl.BlockSpec((tm, d), lambda i: (i, blk))
    full = lambda arr: pl.BlockSpec(arr.shape, lambda i: (0,) * arr.ndim)
    bfo = jax.ShapeDtypeStruct((tl, d), BF16)
    f32o = jax.ShapeDtypeStruct((tl, d), F32)
    return pl.pallas_call(
        body, name="tail", grid=(nt,),
        in_specs=[row, pcol(0), pcol(1), pcol(2), pcol(3), row, row, row, row, full(mod), full(wc), full(wg),
                  full(wo), full(ln_g), full(ln_b), full(gn_t), full(fg)],
        out_specs=(pl.BlockSpec((tm, 4 * d), lambda i: (i, 0)), row, row, row, row, row, row, row, row, row,
                   pl.BlockSpec((16, d), lambda i: (0, 0))),
        out_shape=(jax.ShapeDtypeStruct((tl, 4 * d), BF16), f32o, bfo, f32o, bfo, bfo, bfo, bfo, bfo, bfo,
                   jax.ShapeDtypeStruct((16, d), F32)),
        compiler_params=_params())(a1, pa2, pa2, pa2, pa2, o_f, o_b, x2, tgt, mod, wc, wg, wo, ln_g, ln_b, gn_t, fg)


def _local_step(x, c, ctx, tgt, c_ctx, ada_w8, ada_b, norm_g, w_a, b_a, w_b, b_b, conv_w8, conv_b, ln_g, ln_b,
                up2, bias2, gla_norm_g, final_norm_g, proj, on_grads=None, on_du_a1=None):
    nb, s_len, d = x.shape
    c_len = ctx.shape[1]
    dk_, dv_ = d // 2, d
    tl, tc = nb * s_len, nb * c_len
    nbw = 2 * dk_ + dv_ + LANE
    tm = math.gcd(256, c_len)
    tiles = _Tiles(nb, s_len, c_len, tm, 2)
    tmm = tiles.big * tm
    l_len = tiles.rows_per_ex
    t_all = nb * l_len
    x2, ctx2, tgt2 = x.reshape(tl, d), ctx.reshape(tc, d), tgt.reshape(tl, d)

    cv = jnp.zeros((8, d), F32).at[0:nb].set(c).at[nb].set(c_ctx.reshape(d))
    mod = _ada_fwd(cv, ada_w8, ada_b)
    u = _norm_fwd(x2, ctx2, mod, norm_g, tiles)
    tna = _tile(2 * d, 2048)
    pa1 = _matmul_bias("inproj_a1", u, w_a, b_a, tl, tmm, tna, tiles.big_all_of_lat, 0, 2 * d, F32)
    pa2 = _matmul_bias("inproj_a2", u, w_a, b_a, tl, tmm, tna, tiles.big_all_of_lat, 2 * d, 4 * d, BF16)
    pb, pv = _inproj_b(u, w_b, b_b, tmm, dk_, dv_)

    a1 = _conv_fwd(pa1, conv_w8, conv_b, nb, s_len)
    lr_blk = (2 * dk_) // LANE
    g_all = _decay_fwd(pb, up2, bias2, tm, lr_blk)
    pb3, pv3 = pb.reshape(nb, l_len, 2 * dk_ + LANE), pv.reshape(nb, l_len, dv_)
    o_f, zs_f, b_f, o_b, zs_b, b_b2 = _gla_fwd(pb3, pv3, g_all.reshape(nb, l_len, 2 * dk_), nb, s_len, c_len,
                                               dk_, dv_)

    conv_proj, gla_proj, w_out = proj(a1) if callable(proj) else proj
    tt = math.gcd(128, s_len)
    (dp_a2, da1, d_o, gx1, merged, dmo, ycin, dyconv, ogin, dygla, small) = _tail(
        a1, pa2, o_f.reshape(tl, dv_), o_b.reshape(tl, dv_), x2, tgt2, mod, conv_proj, gla_proj, w_out, ln_g, ln_b,
        gla_norm_g, final_norm_g, nb, tt)

    lat3 = lambda a: a.reshape(nb, s_len, a.shape[-1])
    tnw = _tile(d, 1024)
    tnp = _tile(d, 512)
    d_w_out, _ = _matmul_tn_whole("dw_out", lat3(merged), lat3(dmo), s_len, tnp)
    d_conv_proj, _ = _matmul_tn_whole("dw_conv_proj", lat3(ycin), lat3(dyconv), s_len, tnp)
    d_gla_proj, _ = _matmul_tn_whole("dw_gla_proj", lat3(ogin), lat3(dygla), s_len, tnp)

    dp_a1, d_conv_w8, d_conv_b = _conv_bwd(pa1, da1, conv_w8, nb, s_len)
    gl = _gla_bwd(pb3, pv3, d_o.reshape(nb, s_len, dv_), (zs_f, b_f, zs_b, b_b2), nb, s_len, c_len, dk_, dv_)
    gl = [g_.reshape(t_all, g_.shape[-1]) for g_ in gl]
    dp_b, d_up2, d_bias2 = _decay_bwd(pb, up2, bias2, gl[0:4], gl[4:8], tiles, lr_blk, dk_, dv_)

    u3 = u.reshape(nb, l_len, d)
    dw_a1, db_a1 = _matmul_tn_whole("dw_a1", u3, lat3(dp_a1), s_len, tnw)
    dw_a2, db_a2 = _matmul_tn_whole("dw_a2", u3, lat3(dp_a2), s_len, tnw)
    dw_b, db_b = _matmul_tn("dw_b", u, dp_b, t_all, tmm, nbw)
    grads = dict(w_a1=dw_a1, w_a2=dw_a2, w_b=dw_b, conv_w8=d_conv_w8, conv_proj=d_conv_proj, up2=d_up2,
                 gla_proj=d_gla_proj, w_out=d_w_out)

    tka = _tile(2 * d, 2048)
    du_a1 = _matmul_nt("du_a1", dp_a1, w_a, 0, tmm, tka, after=on_grads(grads) if on_grads else ())
    du_a2 = _matmul_nt("du_a2", dp_a2, w_a, (2 * d) // tka, tmm, tka, after=on_du_a1(du_a1) if on_du_a1 else ())
    du_b = _matmul_nt("du_b", dp_b, w_b, 0, tmm, nbw)
    grad_x2, dmod_ss, d_norm_g = _norm_bwd(x2, ctx2, mod, norm_g, [du_a1, du_a2], du_b, gx1, tiles)
    d_ada_w8, d_ada_b, d_cv = _ada_bwd(cv, ada_w8, dmod_ss, small, nb)

    return dict(
        grads, grad_x=grad_x2.reshape(nb, s_len, d), small=small, cv=d_cv, ada_w8=d_ada_w8, ada_b=d_ada_b,
        norm_g=d_norm_g, b_a1=db_a1, b_a2=db_a2, b_b=db_b, conv_b=d_conv_b, bias2=d_bias2)


def _regroup_pieces(d, r, wshard):
    cb = d // N_DEV
    segs = []
    for j in range(N_DEV):
        segs.append((j * cb, cb, 0, 2 * j * cb))
    for j in range(N_DEV):
        segs.append((d + j * cb, cb, 0, (2 * j + 1) * cb))
    segs += [(2 * d, d, 0, 2 * d), (3 * d, 2 * d + 2 * r, 1, 0), (5 * d + 2 * r, 3 * d, 0, 3 * d)]
    pieces = []
    for o0, w, dst, d0 in segs:
        lo = o0
        while lo < o0 + w:
            j = lo // wshard
            hi = min(o0 + w, (j + 1) * wshard)
            pieces.append((j, lo - j * wshard, hi - lo, dst, d0 + lo - o0))
            lo = hi
    return pieces


def _regroup(o, d, r):
    n_in = 8 * d + 2 * r
    parts = ([], [])
    for _, s0, n, dst, _ in sorted(_regroup_pieces(d, r, n_in), key=lambda p: (p[3], p[4])):
        parts[dst].append(o[..., s0:s0 + n])
    pad = jnp.zeros(o.shape[:-1] + (LANE - 2 * r,), o.dtype)
    return jnp.concatenate(parts[0], axis=-1), jnp.concatenate(parts[1] + [pad], axis=-1)


def _unshard_w_in(g_win, d, r, after=()):
    n_sh, _, ws = g_win.shape
    nbw = 2 * d + LANE
    pieces = _regroup_pieces(d, r, ws)
    tr = math.gcd(d, 256)

    def body(g_ref, *rest):
        a_ref, b_ref = rest[len(after):]
        dsts = (a_ref, b_ref)
        for j, s0, n, dst, d0 in pieces:
            dsts[dst][:, pl.ds(d0, n)] = g_ref[j, :, pl.ds(s0, n)]
        b_ref[:, pl.ds(2 * d + 2 * r, LANE - 2 * r)] = jnp.zeros((tr, LANE - 2 * r), b_ref.dtype)

    return pl.pallas_call(
        body, name="unshard_w_in", grid=(d // tr,),
        in_specs=[pl.BlockSpec((n_sh, tr, ws), lambda i: (0, i, 0))] + [_ANY] * len(after),
        out_specs=(pl.BlockSpec((tr, 6 * d), lambda i: (i, 0)), pl.BlockSpec((tr, nbw), lambda i: (i, 0))),
        out_shape=(jax.ShapeDtypeStruct((d, 6 * d), g_win.dtype), jax.ShapeDtypeStruct((d, nbw), g_win.dtype)),
        compiler_params=_params())(g_win, *after)


def _reshard_w_in(dw_a1, dw_a2, dw_b, d, r):
    ws = (8 * d + 2 * r) // N_DEV
    pieces = _regroup_pieces(d, r, ws)
    tr = math.gcd(d, 256)

    def body(a1_ref, a2_ref, b_ref, o_ref):
        for j, s0, n, dst, d0 in pieces:
            if dst == 1:
                src = b_ref[:, pl.ds(d0, n)]
            elif d0 < 2 * d:
                src = a1_ref[:, pl.ds(d0, n)]
            else:
                src = a2_ref[:, pl.ds(d0 - 2 * d, n)]
            o_ref[j, :, pl.ds(s0, n)] = src

    row = lambda w: pl.BlockSpec((tr, w), lambda i: (i, 0))
    return pl.pallas_call(
        body, name="reshard_w_in", grid=(d // tr,),
        in_specs=[row(2 * d), row(4 * d), row(2 * d + LANE)],
        out_specs=pl.BlockSpec((N_DEV, tr, ws), lambda i: (0, i, 0)),
        out_shape=jax.ShapeDtypeStruct((N_DEV, d, ws), dw_b.dtype),
        compiler_params=_params())(dw_a1, dw_a2, dw_b)


_SMALL = ("c_ctx", "ada_b", "norm_g", "b_in", "conv_b", "conv_ln_g", "conv_ln_b", "decay_bias_fwd",
          "decay_bias_bwd", "gla_norm_g", "final_norm_g")


def _small_layout(d, r):
    sizes = dict(c_ctx=d, ada_b=3 * d, norm_g=d, b_in=8 * d + 2 * r, conv_b=d, conv_ln_g=d, conv_ln_b=d,
                 decay_bias_fwd=d // 2, decay_bias_bwd=d // 2, gla_norm_g=d // HEADS, final_norm_g=d, loss=1)
    table, off = {}, 0
    for name in _SMALL + ("loss",):
        table[name] = (off, sizes[name])
        off += -(-sizes[name] // LANE) * LANE
    return table, off


def _pack_small(g, nb, d, r):
    table, width = _small_layout(d, r)
    hv = d // HEADS
    pieces = _regroup_pieces(d, r, 8 * d + 2 * r)
    names = ("small", "cv", "ada_b", "norm_g", "b_a1", "b_a2", "b_b", "conv_b", "bias2")

    def body(sm, cv, ab, ng, ba1, ba2, bb, cvb, b2, o_ref):
        o_ref[...] = jnp.zeros_like(o_ref)

        def put(name, val):
            off, n = table[name]
            o_ref[:, pl.ds(off, n)] = val

        put("c_ctx", cv[nb:nb + 1, :])
        put("ada_b", ab[...])
        put("norm_g", ng[...])
        off_b = table["b_in"][0]
        for _, s0, n, dst, d0 in pieces:
            if dst == 1:
                src = bb[:, pl.ds(d0, n)]
            elif d0 < 2 * d:
                src = ba1[:, pl.ds(d0, n)]
            else:
                src = ba2[:, pl.ds(d0 - 2 * d, n)]
            o_ref[:, pl.ds(off_b + s0, n)] = src
        put("conv_b", cvb[...])
        put("conv_ln_g", sm[1:2, :])
        put("conv_ln_b", sm[2:3, :])
        put("decay_bias_fwd", b2[:, 0:d // 2])
        put("decay_bias_bwd", b2[:, d // 2:d])
        gn = sm[3:4, 0:hv]
        for h in range(1, HEADS):
            gn = gn + sm[3:4, h * hv:(h + 1) * hv]
        put("gla_norm_g", gn)
        put("final_norm_g", sm[0:1, :])
        put("loss", sm[4:5, 0:1])

    return pl.pallas_call(body, name="pack_small", out_shape=jax.ShapeDtypeStruct((1, width), F32),
                          compiler_params=_params())(*[g[k] for k in names])


def _small_adam(parts, ws, ms, vs, d, r):
    table, width = _small_layout(d, r)
    n_parts = parts.shape[0]
    k = len(_SMALL)
    bc1 = 1.0 - ADAM_B1 ** ADAM_STEP
    bc2 = 1.0 - ADAM_B2 ** ADAM_STEP

    def body(p_ref, *refs):
        w_refs, m_refs, v_refs = refs[0:k], refs[k:2 * k], refs[2 * k:3 * k]
        outs = refs[3 * k:]
        tot = p_ref[0]
        for i in range(1, n_parts):
            tot = tot + p_ref[i]
        for i, name in enumerate(_SMALL):
            off, n = table[name]
            g = tot[:, off:off + n]
            mn = ADAM_B1 * m_refs[i][...] + (1.0 - ADAM_B1) * g
            vn = ADAM_B2 * v_refs[i][...] + (1.0 - ADAM_B2) * (g * g)
            outs[i][...] = g
            outs[k + i][...] = -ADAM_LR * ((mn / bc1) / (jnp.sqrt(vn / bc2) + ADAM_EPS) + ADAM_WD * w_refs[i][...])
            outs[2 * k + i][...] = mn
            outs[3 * k + i][...] = vn
        off, _ = table["loss"]
        outs[4 * k][...] = tot[:, off:off + 1]

    shapes = [jax.ShapeDtypeStruct(w.shape, F32) for w in ws]
    res = pl.pallas_call(body, name="small_adam", out_shape=tuple(shapes * 4 + [jax.ShapeDtypeStruct((1, 1), F32)]),
                         compiler_params=_params())(parts, *ws, *ms, *vs)
    return res[0:k], res[k:2 * k], res[2 * k:3 * k], res[3 * k:4 * k], res[4 * k]


def _mesh_pos():
    return lax.axis_index("x"), lax.axis_index("y"), lax.axis_index("c")


def _all_gather(arrs):
    n = len(arrs)
    ns = 9
    split = [a.ndim == 2 and a.shape[0] % 32 == 0 for a in arrs]

    def body(*refs):
        ins, outs = refs[:n], refs[n:2 * n]
        send_sems, recv_sems, local_sems = refs[2 * n:]
        x, y, c = _mesh_pos()
        me, sibling = (x, y, c), (x, y, 1 - c)
        xn, yn, dg = (1 - x, y, c), (x, 1 - y, c), (1 - x, 1 - y, c)
        other = lambda pos: (pos[0], pos[1], 1 - c)

        def slot(a, pos, half):
            ref = outs[a].at[4 * pos[0] + 2 * pos[1] + pos[2]]
            if half is None:
                return ref
            rows = arrs[a].shape[0] // 2
            return ref.at[pl.ds(half * rows, rows)]

        def copy(a, k, block, to, src=None, half=None):
            dst = slot(a, block, half)
            return pltpu.make_async_remote_copy(
                src_ref=dst if src is None else src, dst_ref=dst,
                send_sem=send_sems.at[ns * a + k], recv_sem=recv_sems.at[ns * a + k],
                device_id=to, device_id_type=MESH)

        h0 = lambda a: 0 if split[a] else None
        mine = [pltpu.make_async_copy(ins[a], slot(a, me, None), local_sems.at[a]) for a in range(n)]
        for cp in mine:
            cp.start()
        sent = []
        for a in range(n):
            sent += [copy(a, 0, me, sibling, src=ins[a]), copy(a, 1, me, xn, src=ins[a]),
                     copy(a, 2, me, yn, src=ins[a])]
        for cp in sent:
            cp.start()

        def pass_on(cp):
            cp.start()
            sent.append(cp)

        for a in range(n):
            copy(a, 1, xn, me).wait_recv()
            pass_on(copy(a, 3, xn, sibling))
            pass_on(copy(a, 4, xn, yn, half=h0(a)))
        for a in range(n):
            copy(a, 2, yn, me).wait_recv()
            pass_on(copy(a, 5, yn, sibling))
            if split[a]:
                pass_on(copy(a, 6, yn, xn, half=1))
        for a in range(n):
            copy(a, 4, dg, me, half=h0(a)).wait_recv()
            pass_on(copy(a, 7, dg, sibling, half=h0(a)))
            if split[a]:
                copy(a, 6, dg, me, half=1).wait_recv()
                pass_on(copy(a, 8, dg, sibling, half=1))
        for a in range(n):
            copy(a, 0, sibling, me).wait_recv()
            copy(a, 3, other(xn), me).wait_recv()
            copy(a, 5, other(yn), me).wait_recv()
            copy(a, 7, other(dg), me, half=h0(a)).wait_recv()
            if split[a]:
                copy(a, 8, other(dg), me, half=1).wait_recv()
        for cp in sent:
            cp.wait_send()
        for cp in mine:
            cp.wait()

    return pl.pallas_call(
        body, name="all_gather",
        out_shape=tuple(jax.ShapeDtypeStruct((N_DEV,) + a.shape, a.dtype) for a in arrs),
        in_specs=[_ANY] * n, out_specs=tuple([_ANY] * n),
        scratch_shapes=[pltpu.SemaphoreType.DMA((ns * n,)), pltpu.SemaphoreType.DMA((ns * n,)),
                        pltpu.SemaphoreType.DMA((n,))],
    )(*arrs)


def _exchange_sibling(arrs):
    n = len(arrs)

    def body(*refs):
        ins, outs = refs[:n], refs[n:2 * n]
        send_sems, recv_sems = refs[2 * n:]
        x, y, c = _mesh_pos()
        copies = [pltpu.make_async_remote_copy(
            src_ref=ins[a].at[2 * k + (1 - c)], dst_ref=outs[a].at[k],
            send_sem=send_sems.at[4 * a + k], recv_sem=recv_sems.at[4 * a + k],
            device_id=(x, y, 1 - c), device_id_type=MESH) for a in range(n) for k in range(4)]
        for cp in copies:
            cp.start()
        for cp in copies:
            cp.wait_recv()
        for cp in copies:
            cp.wait_send()

    return pl.pallas_call(
        body, name="grad_exchange_sibling",
        out_shape=tuple(jax.ShapeDtypeStruct((4,) + a.shape[1:], a.dtype) for a in arrs),
        in_specs=[_ANY] * n, out_specs=tuple([_ANY] * n),
        scratch_shapes=[pltpu.SemaphoreType.DMA((4 * n,)), pltpu.SemaphoreType.DMA((4 * n,))],
    )(*arrs)


def _pair_sum(name, mine, theirs):
    _, r, cdim = mine.shape
    tr = r if (r % 8 or r <= 256) else math.gcd(r, 256)

    def body(m_ref, t_ref, o_ref):
        c = lax.axis_index("c")
        own = jnp.where(c == 0, m_ref[:, 0].astype(F32), m_ref[:, 1].astype(F32))
        o_ref[...] = (own + t_ref[...].astype(F32)).astype(o_ref.dtype)

    return pl.pallas_call(
        body, name=name, grid=(r // tr,),
        in_specs=[pl.BlockSpec((4, 2, tr, cdim), lambda i: (0, 0, i, 0)),
                  pl.BlockSpec((4, tr, cdim), lambda i: (0, i, 0))],
        out_specs=pl.BlockSpec((4, tr, cdim), lambda i: (0, i, 0)),
        out_shape=jax.ShapeDtypeStruct((4, r, cdim), mine.dtype),
        compiler_params=_params())(mine.reshape(4, 2, r, cdim), theirs)


def _exchange_chips(arrs):
    n = len(arrs)

    def body(*refs):
        ins, outs = refs[:n], refs[n:2 * n]
        send_sems, recv_sems, local_sems = refs[2 * n:]
        x, y, c = _mesh_pos()
        my_chip = 2 * x + y
        mine = [pltpu.make_async_copy(ins[a].at[my_chip], outs[a].at[my_chip], local_sems.at[a]) for a in range(n)]
        for cp in mine:
            cp.start()
        copies = []
        for rel in range(1, 4):
            px = 1 - x if rel & 2 else x
            py = 1 - y if rel & 1 else y
            for a in range(n):
                copies.append(pltpu.make_async_remote_copy(
                    src_ref=ins[a].at[2 * px + py], dst_ref=outs[a].at[my_chip],
                    send_sem=send_sems.at[3 * a + rel - 1], recv_sem=recv_sems.at[3 * a + rel - 1],
                    device_id=(px, py, c), device_id_type=MESH))
        for cp in copies:
            cp.start()
        for cp in copies:
            cp.wait_recv()
        for cp in copies:
            cp.wait_send()
        for cp in mine:
            cp.wait()

    return pl.pallas_call(
        body, name="grad_exchange_chips",
        out_shape=tuple(jax.ShapeDtypeStruct(a.shape, a.dtype) for a in arrs),
        in_specs=[_ANY] * n, out_specs=tuple([_ANY] * n),
        scratch_shapes=[pltpu.SemaphoreType.DMA((3 * n,)), pltpu.SemaphoreType.DMA((3 * n,)),
                        pltpu.SemaphoreType.DMA((n,))],
    )(*arrs)


_HBM = pl.BlockSpec(memory_space=pltpu.HBM)
_SEM = pl.BlockSpec(memory_space=pltpu.SEMAPHORE)


def _copies_start(name, srcs, lands, make_copies, n_sems):
    n, m = len(srcs), len(lands)

    def body(*refs):
        ins = refs[:n + m]
        send_sems, recv_sems = refs[n + m], refs[n + m + 1]
        for cp in make_copies(ins[:n], ins[n:], send_sems, recv_sems):
            cp.start()
        refs[-1][...] = jnp.zeros_like(refs[-1])

    res = pl.pallas_call(
        body, name=name,
        out_shape=(pltpu.SemaphoreType.DMA((n_sems,)), pltpu.SemaphoreType.DMA((n_sems,)),
                   *[pltpu.HBM(a.shape, a.dtype) for a in (*srcs, *lands)], jax.ShapeDtypeStruct((8, LANE), F32)),
        in_specs=[_HBM] * (n + m),
        out_specs=(_SEM, _SEM, *[_HBM] * (n + m), pl.BlockSpec(memory_space=pltpu.VMEM)),
        input_output_aliases={i: 2 + i for i in range(n + m)},
        compiler_params=pltpu.CompilerParams(has_side_effects=pltpu.SideEffectType.DATAFLOW_SIDE_EFFECTING),
    )(*[pltpu.with_memory_space_constraint(a, pltpu.HBM) for a in (*srcs, *lands)])
    return res[0], res[1], res[2:2 + n], res[2 + n:2 + n + m], res[-1]


def _copies_wait(name, started, after, make_copies):
    send_sems, recv_sems, srcs, lands, _ = started
    n, m = len(srcs), len(lands)

    def body(*refs):
        ins = refs[:n + m]
        for cp in make_copies(ins[:n], ins[n:], refs[n + m], refs[n + m + 1]):
            cp.wait_send()
            cp.wait_recv()

    res = pl.pallas_call(
        body, name=name,
        out_shape=tuple(pltpu.HBM(a.shape, a.dtype) for a in (*srcs, *lands)),
        in_specs=[_HBM] * (n + m) + [_SEM, _SEM] + [_ANY] * len(after),
        out_specs=tuple([_HBM] * (n + m)),
        input_output_aliases={i: i for i in range(n + m)},
        compiler_params=pltpu.CompilerParams(has_side_effects=pltpu.SideEffectType.DATAFLOW_SIDE_EFFECTING),
    )(*srcs, *lands, send_sems, recv_sems, *after)
    return res[:n], res[n:]


def _gather_copies(srcs, lands, send_sems, recv_sems):
    x, y, c = _mesh_pos()
    me_i = 4 * x + 2 * y + c
    copies = []
    for rel in range(1, N_DEV):
        peer = (1 - x if rel & 4 else x, 1 - y if rel & 2 else y, 1 - c if rel & 1 else c)
        for a in range(len(srcs)):
            copies.append(pltpu.make_async_remote_copy(
                src_ref=srcs[a], dst_ref=lands[a].at[me_i], send_sem=send_sems.at[7 * a + rel - 1],
                recv_sem=recv_sems.at[7 * a + rel - 1], device_id=peer, device_id_type=MESH))
    return copies


def _sibling_copies(srcs, lands, send_sems, recv_sems):
    x, y, c = _mesh_pos()
    return [pltpu.make_async_remote_copy(
        src_ref=srcs[a].at[2 * k + (1 - c)], dst_ref=lands[a].at[k], send_sem=send_sems.at[4 * a + k],
        recv_sem=recv_sems.at[4 * a + k], device_id=(x, y, 1 - c), device_id_type=MESH)
        for a in range(len(srcs)) for k in range(4)]


def _chip_copies(srcs, lands, send_sems, recv_sems):
    x, y, c = _mesh_pos()
    my_chip = 2 * x + y
    copies = []
    for rel in range(1, 4):
        px = 1 - x if rel & 2 else x
        py = 1 - y if rel & 1 else y
        for a in range(len(srcs)):
            copies.append(pltpu.make_async_remote_copy(
                src_ref=srcs[a].at[2 * px + py], dst_ref=lands[a].at[my_chip], send_sem=send_sems.at[3 * a + rel - 1],
                recv_sem=recv_sems.at[3 * a + rel - 1], device_id=(px, py, c), device_id_type=MESH))
    return copies


def _sum_adam(name, parts, w, m, v, own=None):
    r, cdim = w.shape
    n_parts = parts.shape[0]
    tr = r if (r % 8 or r <= 256) else math.gcd(r, 256)
    bc1 = 1.0 - ADAM_B1 ** ADAM_STEP
    bc2 = 1.0 - ADAM_B2 ** ADAM_STEP
    extra = [] if own is None else [own]

    def body(p_ref, *refs):
        w_ref, m_ref, v_ref, g_ref, d_ref, nm_ref, nv_ref = refs[len(extra):]
        if own is None:
            part = lambda k: p_ref[k].astype(F32)
        else:
            my_chip = 2 * lax.axis_index("x") + lax.axis_index("y")
            part = lambda k: jnp.where(my_chip == k, refs[0][k], p_ref[k]).astype(F32)
        g = part(0)
        for k in range(1, n_parts):
            g = g + part(k)
        mn = ADAM_B1 * m_ref[...] + (1.0 - ADAM_B1) * g
        vn = ADAM_B2 * v_ref[...] + (1.0 - ADAM_B2) * (g * g)
        g_ref[...] = g
        nm_ref[...] = mn
        nv_ref[...] = vn
        d_ref[...] = -ADAM_LR * ((mn / bc1) / (jnp.sqrt(vn / bc2) + ADAM_EPS) + ADAM_WD * w_ref[...])

    blk = pl.BlockSpec((tr, cdim), lambda i: (i, 0))
    o = jax.ShapeDtypeStruct((r, cdim), F32)
    return pl.pallas_call(
        body, name=name, grid=(r // tr,),
        in_specs=[pl.BlockSpec((n_parts, tr, cdim), lambda i: (0, i, 0))] * (1 + len(extra)) + [blk, blk, blk],
        out_specs=(blk, blk, blk, blk), out_shape=(o, o, o, o),
        compiler_params=_params())(parts, *extra, w, m, v)


_WEIGHTS = ("c_ctx", "ada_w", "ada_b", "norm_g", "w_in", "b_in", "conv_w", "conv_b", "conv_ln_g", "conv_ln_b",
            "conv_proj", "decay_up_fwd", "decay_bias_fwd", "decay_up_bwd", "decay_bias_bwd", "gla_norm_g",
            "gla_proj", "w_out", "final_norm_g")


def _as2d(a):
    if a.ndim == 1:
        return a.reshape(1, -1)
    return a.reshape(-1, a.shape[-1])


def kernel(x, c, ctx, c_ctx, ada_w, ada_b, norm_g, w_in, b_in, conv_w, conv_b, conv_ln_g, conv_ln_b, conv_proj, decay_up_fwd, decay_bias_fwd, decay_up_bwd, decay_bias_bwd, gla_norm_g, gla_proj, w_out, final_norm_g, loss_target, m_c_ctx, m_ada_w, m_ada_b, m_norm_g, m_w_in, m_b_in, m_conv_w, m_conv_b, m_conv_ln_g, m_conv_ln_b, m_conv_proj, m_decay_up_fwd, m_decay_bias_fwd, m_decay_up_bwd, m_decay_bias_bwd, m_gla_norm_g, m_gla_proj, m_w_out, m_final_norm_g, v_c_ctx, v_ada_w, v_ada_b, v_norm_g, v_w_in, v_b_in, v_conv_w, v_conv_b, v_conv_ln_g, v_conv_ln_b, v_conv_proj, v_decay_up_fwd, v_decay_bias_fwd, v_decay_up_bwd, v_decay_bias_bwd, v_gla_norm_g, v_gla_proj, v_w_out, v_final_norm_g):
    env = dict(locals())
    wts = {k: env[k] for k in _WEIGHTS}
    d = x.shape[-1]
    r = decay_up_fwd.shape[1]
    dk_ = d // 2
    n_in = w_in.shape[-1] * N_DEV

    ds, dks = d // N_DEV, dk_ // N_DEV
    g_win, g_ada, conv_w8, g_up = _all_gather(
        [w_in[0].astype(BF16), ada_w[0].astype(BF16), conv_w[0],
         jnp.concatenate([decay_up_fwd[0], decay_up_bwd[0]], axis=1)])
    proj_own = [conv_proj[0].astype(BF16), gla_proj[0].astype(BF16), w_out[0].astype(BF16)]
    me_i = 4 * lax.axis_index("x") + 2 * lax.axis_index("y") + lax.axis_index("c")
    proj_lands = [lax.dynamic_update_slice(lax.empty((N_DEV,) + a.shape, a.dtype), a[None], (me_i, 0, 0))
                  for a in proj_own]
    proj_start = _copies_start("proj_gather_start", proj_own, proj_lands, _gather_copies, 7 * 3)

    def proj(after):
        _, lands = _copies_wait("proj_gather_wait", proj_start, (after,), _gather_copies)
        return [w.reshape(d, d) for w in lands]

    w_a, w_b = _unshard_w_in(g_win, d, r, after=(proj_start[4],))
    up_f = g_up[:, :, 0:dks].transpose(1, 0, 2).reshape(r, dk_)
    up_b = g_up[:, :, dks:].transpose(1, 0, 2).reshape(r, dk_)
    up2 = jnp.zeros((LANE, 2 * dk_), F32).at[0:r, 0:dk_].set(up_f).at[r:2 * r, dk_:].set(up_b)
    bias2 = jnp.concatenate([decay_bias_fwd, decay_bias_bwd], axis=1)
    b_a, b_b = _regroup(b_in, d, r)

    names = ("w_in", "conv_proj", "gla_proj", "w_out", "conv_w", "decay_up")
    comm = {}

    def on_grads(gr):
        d_up = jnp.concatenate([gr["up2"][0:r, 0:dk_].reshape(r, N_DEV, dks).transpose(1, 0, 2),
                                gr["up2"][r:2 * r, dk_:].reshape(r, N_DEV, dks).transpose(1, 0, 2)], axis=2)
        mine = [_reshard_w_in(gr["w_a1"], gr["w_a2"], gr["w_b"], d, r), gr["conv_proj"].reshape(N_DEV, ds, d),
                gr["gla_proj"].reshape(N_DEV, ds, d), gr["w_out"].reshape(N_DEV, ds, d), gr["conv_w8"], d_up]
        lands = [lax.empty((4,) + a.shape[1:], a.dtype) for a in mine]
        comm["sibling"] = _copies_start("grad_sibling_start", mine, lands, _sibling_copies, 4 * len(mine))
        return (comm["sibling"][4],)

    def on_du_a1(du_a1):
        mine, theirs = _copies_wait("grad_sibling_wait", comm["sibling"], (du_a1,), _sibling_copies)
        sums = [_pair_sum("pair_sum_" + nm, a, b) for nm, a, b in zip(names, mine, theirs)]
        lands = [lax.empty(a.shape, a.dtype) for a in sums]
        comm["chips"] = _copies_start("grad_chips_start", sums, lands, _chip_copies, 3 * len(sums))
        return (comm["chips"][4],)

    g = _local_step(x, c, ctx, loss_target, c_ctx, g_ada, ada_b, norm_g[0:1], w_a, b_a, w_b, b_b,
                    conv_w8, conv_b, conv_ln_g, conv_ln_b, up2, bias2, gla_norm_g, final_norm_g.reshape(1, d),
                    proj, on_grads, on_du_a1)

    (their_ada,) = _exchange_sibling([g["ada_w8"]])
    ada_sum = _pair_sum("pair_sum_ada_w", g["ada_w8"], their_ada)
    ada_start = _copies_start("ada_chips_start", [ada_sum], [lax.empty(ada_sum.shape, ada_sum.dtype)],
                              _chip_copies, 3)
    own, landed = _copies_wait("grad_chips_wait", comm["chips"], (ada_start[4],), _chip_copies)
    o_win, o_cp, o_gp, o_wo, o_cw, o_up = own
    x_win, x_cp, x_gp, x_wo, x_cw, x_up = landed

    (packs,) = _all_gather([_pack_small(g, x.shape[0], d, r)])
    row = lambda a: a.reshape(1, -1)
    sg, sd, sm, sv, loss = _small_adam(packs, [row(wts[k]) for k in _SMALL], [row(env["m_" + k]) for k in _SMALL],
                                       [row(env["v_" + k]) for k in _SMALL], d, r)
    out = {}
    for i, k in enumerate(_SMALL):
        for pre, arrs in (("grad_", sg), ("delta_", sd), ("new_m_", sm), ("new_v_", sv)):
            out[pre + k] = arrs[i].reshape(wts[k].shape)
    loss = loss.reshape(())

    def big(name, parts, wname, own=None):
        w2 = _as2d(wts[wname])
        res = _sum_adam(name, parts, w2, _as2d(env["m_" + wname]), _as2d(env["v_" + wname]), own)
        for pre, arr in zip(("grad_", "delta_", "new_m_", "new_v_"), res):
            out[pre + wname] = arr.reshape(wts[wname].shape)

    big("adam_w_in", x_win, "w_in", o_win)
    big("adam_conv_proj", x_cp, "conv_proj", o_cp)
    big("adam_gla_proj", x_gp, "gla_proj", o_gp)
    big("adam_w_out", x_wo, "w_out", o_wo)
    big("adam_conv_w", x_cw, "conv_w", o_cw)
    big("adam_up_f", x_up[:, :, 0:dks], "decay_up_fwd", o_up[:, :, 0:dks])
    big("adam_up_b", x_up[:, :, dks:], "decay_up_bwd", o_up[:, :, dks:])
    (o_ada,), (x_ada,) = _copies_wait("ada_chips_wait", ada_start, (out["grad_w_in"], out["grad_w_out"], out["grad_b_in"]),
                                      _chip_copies)
    big("adam_ada_w", x_ada, "ada_w", o_ada)

    return (loss, g["grad_x"], *[out["grad_" + k] for k in _WEIGHTS], *[out["delta_" + k] for k in _WEIGHTS],
            *[out["new_m_" + k] for k in _WEIGHTS], *[out["new_v_" + k] for k in _WEIGHTS])
```

```python
import functools
import math

import jax
import jax.numpy as jnp
from jax import lax
from jax.experimental import pallas as pl
from jax.experimental.pallas import tpu as pltpu

F32 = jnp.float32
BF16 = jnp.bfloat16
MESH = pl.DeviceIdType.MESH

N_DEV = 8
GRID_W = 64
CHUNK = 128
HEADS = 4
EPS = 1e-6
GATE_TAU = 16.0
LANE = 128
ADAM_LR, ADAM_B1, ADAM_B2, ADAM_EPS, ADAM_WD, ADAM_STEP = 0.001, 0.9, 0.999, 1e-08, 0.01, 10
VMEM_LIMIT = 60 * 1024 * 1024
_ANY = pl.BlockSpec(memory_space=pl.ANY)


def _params(**kw):
    return pltpu.CompilerParams(vmem_limit_bytes=VMEM_LIMIT, **kw)


def _tile(n, pref):
    t = (min(pref, n) // LANE) * LANE
    while t >= LANE:
        if n % t == 0:
            return t
        t -= LANE
    return n


def _mm(a, b):
    return jnp.dot(a.astype(BF16), b.astype(BF16), preferred_element_type=F32)


def _mm_nt(a, b):
    return lax.dot_general(a.astype(BF16), b.astype(BF16), (((1,), (1,)), ((), ())), preferred_element_type=F32)


def _mm_tn(a, b):
    return lax.dot_general(a.astype(BF16), b.astype(BF16), (((0,), (0,)), ((), ())), preferred_element_type=F32)


def _mm_tn_hi(a, b):
    return lax.dot_general(a, b, (((0,), (0,)), ((), ())), precision=lax.Precision.HIGHEST, preferred_element_type=F32)


def _sigmoid(x):
    return 0.5 * jnp.tanh(0.5 * x) + 0.5


def _dsilu(x, s):
    return s * (1.0 + x * (1.0 - s))


def _rowsel(table, idx, n):
    out = table[0:1, :]
    for r in range(1, n):
        out = jnp.where(idx == r, table[r:r + 1, :], out)
    return out


def _ada_fwd(cv, ada_w8, ada_b):
    n_sh, _, ws = ada_w8.shape

    def body(cv_ref, w_ref, b_ref, o_ref):
        c = cv_ref[...]
        sv = c * _sigmoid(c)
        for j in range(n_sh):
            cols = pl.ds(j * ws, ws)
            o_ref[:, cols] = _mm(sv, w_ref[j]) + b_ref[:, cols]

    return pl.pallas_call(body, name="ada_fwd", out_shape=jax.ShapeDtypeStruct((cv.shape[0], n_sh * ws), F32),
                          compiler_params=_params())(cv, ada_w8, ada_b)


def _ada_bwd(cv, ada_w8, dmod_ss, small, nb):
    n_sh, d, ws = ada_w8.shape

    def body(cv_ref, w_ref, dm_ref, sm_ref, dw_ref, db_ref, dc_ref):
        c = cv_ref[...]
        s = _sigmoid(c)
        sv = c * s
        dm = jnp.concatenate([dm_ref[:, 0:2 * d], sm_ref[8:16, :]], axis=1)
        db_ref[...] = jnp.sum(dm, axis=0, keepdims=True)
        dsv = None
        for j in range(n_sh):
            dmj = dm[:, j * ws:(j + 1) * ws]
            dw_ref[j] = _mm_tn_hi(sv, dmj).astype(dw_ref.dtype)
            part = _mm_nt(dmj, w_ref[j])
            dsv = part if dsv is None else dsv + part
        dc_ref[...] = dsv * _dsilu(c, s)

    return pl.pallas_call(
        body, name="ada_bwd",
        out_shape=(jax.ShapeDtypeStruct((n_sh, d, ws), BF16), jax.ShapeDtypeStruct((1, n_sh * ws), F32),
                   jax.ShapeDtypeStruct(cv.shape, F32)),
        compiler_params=_params())(cv, ada_w8, dmod_ss, small)


class _Tiles:
    def __init__(self, nb, s_len, c_len, tm, big):
        self.nb, self.tm, self.big = nb, tm, big
        self.lat, self.ctx = s_len // tm, c_len // tm
        self.pad = -(self.lat + self.ctx) % big
        self.per_ex = self.lat + self.ctx + self.pad
        self.n_all, self.n_lat = nb * self.per_ex, nb * self.lat
        self.rows_per_ex = self.per_ex * tm

    def is_lat(self, i):
        return i % self.per_ex < self.lat

    def is_pad(self, i):
        return i % self.per_ex >= self.lat + self.ctx

    def lat_of_all(self, i):
        return (i // self.per_ex) * self.lat + jnp.minimum(i % self.per_ex, self.lat - 1)

    def ctx_of_all(self, i):
        return (i // self.per_ex) * self.ctx + jnp.clip(i % self.per_ex - self.lat, 0, self.ctx - 1)

    def big_all_of_lat(self, t):
        lat_big = self.lat // self.big
        return (t // lat_big) * (self.per_ex // self.big) + t % lat_big


def _norm_fwd(x2, ctx2, mod, norm_g, tiles):
    tl, d = x2.shape
    tc = ctx2.shape[0]
    nb, tm = tiles.nb, tiles.tm

    def body(x_ref, c_ref, mod_ref, g_ref, u_ref):
        i = pl.program_id(0)
        lat = tiles.is_lat(i)
        xv = jnp.where(lat, x_ref[...], c_ref[...])
        row = jnp.where(lat, i // tiles.per_ex, nb)
        m = _rowsel(mod_ref[...], row, nb + 1)
        shift, scale = m[:, 0:d], m[:, d:2 * d]
        rstd = lax.rsqrt(jnp.mean(xv * xv, axis=-1, keepdims=True) + EPS)
        u = xv * rstd * g_ref[...] * (1.0 + scale) + shift
        u_ref[...] = jnp.where(tiles.is_pad(i), 0.0, u).astype(BF16)

    return pl.pallas_call(
        body, name="norm_fwd", grid=(tiles.n_all,),
        in_specs=[pl.BlockSpec((tm, d), lambda i: (tiles.lat_of_all(i), 0)),
                  pl.BlockSpec((tm, d), lambda i: (tiles.ctx_of_all(i), 0)),
                  pl.BlockSpec(mod.shape, lambda i: (0, 0)),
                  pl.BlockSpec((1, d), lambda i: (0, 0))],
        out_specs=pl.BlockSpec((tm, d), lambda i: (i, 0)),
        out_shape=jax.ShapeDtypeStruct((tiles.n_all * tm, d), BF16),
        compiler_params=_params())(x2, ctx2, mod, norm_g)


def _norm_bwd(x2, ctx2, mod, norm_g, du_lat, du_b, gx1, tiles):
    tl, d = x2.shape
    nb, tm = tiles.nb, tiles.tm
    nrow = mod.shape[0]
    n_lat_in = len(du_lat)

    def body(x_ref, c_ref, mod_ref, g_ref, *refs):
        dl_refs = refs[:n_lat_in]
        d3_ref, gx_ref, gxo_ref, dmod_ref, dg_ref = refs[n_lat_in:]
        i = pl.program_id(0)

        @pl.when(i == 0)
        def _():
            dmod_ref[...] = jnp.zeros_like(dmod_ref)
            dg_ref[...] = jnp.zeros_like(dg_ref)

        lat = tiles.is_lat(i)
        xv = jnp.where(lat, x_ref[...], c_ref[...])
        row = jnp.where(lat, i // tiles.per_ex, nb)
        m = _rowsel(mod_ref[...], row, nb + 1)
        scale = m[:, d:2 * d]
        g = g_ref[...]
        dl = dl_refs[0][...]
        for ref in dl_refs[1:]:
            dl = dl + ref[...]
        du = jnp.where(tiles.is_pad(i), 0.0, d3_ref[...] + jnp.where(lat, dl, 0.0))
        rstd = lax.rsqrt(jnp.mean(xv * xv, axis=-1, keepdims=True) + EPS)
        xh = xv * rstd
        dshift = jnp.sum(du, axis=0, keepdims=True)
        dscale = jnp.sum(du * xh * g, axis=0, keepdims=True)
        dxn = du * (1.0 + scale)
        dg_ref[...] += jnp.sum(dxn * xh, axis=0, keepdims=True)
        dxh = dxn * g
        dx = rstd * (dxh - xh * jnp.mean(dxh * xh, axis=-1, keepdims=True))

        @pl.when(lat)
        def _():
            gxo_ref[...] = dx + gx_ref[...]

        for r in range(nb + 1):
            dmod_ref[r:r + 1, 0:d] += jnp.where(row == r, dshift, 0.0)
            dmod_ref[r:r + 1, d:2 * d] += jnp.where(row == r, dscale, 0.0)

    lat_map = lambda i: (tiles.lat_of_all(i), 0)
    lat_spec = pl.BlockSpec((tm, d), lat_map)
    return pl.pallas_call(
        body, name="norm_bwd", grid=(tiles.n_all,),
        in_specs=[lat_spec,
                  pl.BlockSpec((tm, d), lambda i: (tiles.ctx_of_all(i), 0)),
                  pl.BlockSpec(mod.shape, lambda i: (0, 0)),
                  pl.BlockSpec((1, d), lambda i: (0, 0))]
                 + [lat_spec] * n_lat_in
                 + [pl.BlockSpec((tm, d), lambda i: (i, 0)), lat_spec],
        out_specs=(lat_spec,
                   pl.BlockSpec((nrow, 3 * d), lambda i: (0, 0)),
                   pl.BlockSpec((1, d), lambda i: (0, 0))),
        out_shape=(jax.ShapeDtypeStruct((tl, d), F32), jax.ShapeDtypeStruct((nrow, 3 * d), F32),
                   jax.ShapeDtypeStruct((1, d), F32)),
        compiler_params=_params())(x2, ctx2, mod, norm_g, *du_lat, du_b, gx1)


def _matmul_bias(name, u, w, b, rows, tm, tn, u_tile):
    d, n = w.shape

    def body(u_ref, w_ref, b_ref, o_ref):
        o_ref[...] = jnp.dot(u_ref[...], w_ref[...], preferred_element_type=F32) + b_ref[...]

    return pl.pallas_call(
        body, name=name, grid=(n // tn, rows // tm),
        in_specs=[pl.BlockSpec((tm, d), lambda j, i: (u_tile(i), 0)),
                  pl.BlockSpec((d, tn), lambda j, i: (0, j)),
                  pl.BlockSpec((1, tn), lambda j, i: (0, j))],
        out_specs=pl.BlockSpec((tm, tn), lambda j, i: (i, j)),
        out_shape=jax.ShapeDtypeStruct((rows, n), F32),
        compiler_params=_params())(u, w, b)


def _inproj_b(u, w_b, b_b, tm, dk_, dv_):
    t_all, d = u.shape
    nbw = w_b.shape[1]

    def body(u_ref, w_ref, b_ref, qk_ref, v_ref):
        full = jnp.dot(u_ref[...], w_ref[...], preferred_element_type=F32) + b_ref[...]
        qk_ref[:, 0:2 * dk_] = full[:, 0:2 * dk_]
        qk_ref[:, 2 * dk_:2 * dk_ + LANE] = full[:, 2 * dk_ + dv_:nbw]
        v_ref[...] = full[:, 2 * dk_:2 * dk_ + dv_].astype(BF16)

    return pl.pallas_call(
        body, name="inproj_b", grid=(t_all // tm,),
        in_specs=[pl.BlockSpec((tm, d), lambda i: (i, 0)), pl.BlockSpec((d, nbw), lambda i: (0, 0)),
                  pl.BlockSpec((1, nbw), lambda i: (0, 0))],
        out_specs=(pl.BlockSpec((tm, 2 * dk_ + LANE), lambda i: (i, 0)), pl.BlockSpec((tm, dv_), lambda i: (i, 0))),
        out_shape=(jax.ShapeDtypeStruct((t_all, 2 * dk_ + LANE), F32), jax.ShapeDtypeStruct((t_all, dv_), BF16)),
        compiler_params=_params())(u, w_b, b_b)


def _matmul_nt(name, a, w, koff, tm, tk, after=()):
    r, kc = a.shape
    d = w.shape[0]
    nk = kc // tk

    def body(a_ref, w_ref, *rest):
        o_ref = rest[len(after)]
        k = pl.program_id(1)
        p = lax.dot_general(a_ref[...], w_ref[...], (((1,), (1,)), ((), ())), preferred_element_type=F32)

        @pl.when(k == 0)
        def _():
            o_ref[...] = p

        @pl.when(k > 0)
        def _():
            o_ref[...] += p

    return pl.pallas_call(
        body, name=name, grid=(r // tm, nk),
        in_specs=[pl.BlockSpec((tm, tk), lambda i, k: (i, k)),
                  pl.BlockSpec((d, tk), lambda i, k: (0, koff + k))] + [_ANY] * len(after),
        out_specs=pl.BlockSpec((tm, d), lambda i, k: (i, 0)),
        out_shape=jax.ShapeDtypeStruct((r, d), F32),
        compiler_params=_params())(a, w, *after)


def _matmul_tn(name, a, b, rows, tk, tn):
    m = a.shape[1]
    n = b.shape[1]
    nk = rows // tk

    def body(a_ref, b_ref, o_ref, s_ref, acc_ref):
        k = pl.program_id(1)
        bv = b_ref[...]
        p = lax.dot_general(a_ref[...], bv, (((0,), (0,)), ((), ())), preferred_element_type=F32)
        cs = jnp.sum(bv.astype(F32), axis=0, keepdims=True)

        @pl.when(k == 0)
        def _():
            acc_ref[...] = p
            s_ref[...] = cs

        @pl.when(k > 0)
        def _():
            acc_ref[...] += p
            s_ref[...] += cs

        @pl.when(k == nk - 1)
        def _():
            o_ref[...] = acc_ref[...].astype(o_ref.dtype)

    return pl.pallas_call(
        body, name=name, grid=(n // tn, nk),
        in_specs=[pl.BlockSpec((tk, m), lambda j, k: (k, 0)),
                  pl.BlockSpec((tk, tn), lambda j, k: (k, j))],
        out_specs=(pl.BlockSpec((m, tn), lambda j, k: (0, j)), pl.BlockSpec((1, tn), lambda j, k: (0, j))),
        out_shape=(jax.ShapeDtypeStruct((m, n), BF16), jax.ShapeDtypeStruct((1, n), F32)),
        scratch_shapes=[pltpu.VMEM((m, tn), F32)],
        compiler_params=_params())(a, b)


def _matmul_tn_whole(name, a3, b3, rows, tn):
    nb, _, m = a3.shape
    n = b3.shape[2]

    def body(a_ref, b_ref, o_ref, s_ref):
        p, cs = None, None
        for e in range(nb):
            bv = b_ref[e]
            pe = lax.dot_general(a_ref[e], bv, (((0,), (0,)), ((), ())), preferred_element_type=F32)
            ce = jnp.sum(bv.astype(F32), axis=0, keepdims=True)
            p, cs = (pe, ce) if p is None else (p + pe, cs + ce)
        o_ref[...] = p.astype(o_ref.dtype)
        s_ref[...] = cs

    return pl.pallas_call(
        body, name=name, grid=(n // tn,),
        in_specs=[pl.BlockSpec((nb, rows, m), lambda j: (0, 0, 0)),
                  pl.BlockSpec((nb, rows, tn), lambda j: (0, 0, j))],
        out_specs=(pl.BlockSpec((m, tn), lambda j: (0, j)), pl.BlockSpec((1, tn), lambda j: (0, j))),
        out_shape=(jax.ShapeDtypeStruct((m, n), BF16), jax.ShapeDtypeStruct((1, n), F32)),
        compiler_params=_params())(a3, b3)


def _conv_window(pad_ref, r, shift, ktaps, width, horizontal):
    if horizontal:
        return pad_ref[r, pl.ds(16 + shift, width), :]
    return pad_ref[r + ktaps // 2 + shift]


def _conv_row(pad_ref, w, r, ktaps, width, horizontal, flip):
    half = ktaps // 2
    acc = None
    for t in range(ktaps):
        win = _conv_window(pad_ref, r, (half - t) if flip else (t - half), ktaps, width, horizontal)
        term = win * w[t:t + 1, :]
        acc = term if acc is None else acc + term
    return acc


def _fill_padded(ref, val, rows, width, ktaps, horizontal):
    half_k = ktaps // 2
    cb = val.shape[-1]
    if horizontal:
        ref[:, 0:16, :] = jnp.zeros((rows, 16, cb), F32)
        ref[:, 16 + width:32 + width, :] = jnp.zeros((rows, 16, cb), F32)
        ref[:, 16:16 + width, :] = val
    else:
        ref[0:half_k, :, :] = jnp.zeros((half_k, width, cb), F32)
        ref[half_k + rows:2 * half_k + rows, :, :] = jnp.zeros((half_k, width, cb), F32)
        ref[half_k:half_k + rows, :, :] = val


def _conv_fwd(pa, conv_w8, conv_b, nb, s):
    nblk, ktaps, cb = conv_w8.shape
    d = nblk * cb
    rows, width = s // GRID_W, GRID_W
    half_k = ktaps // 2
    nh = nblk // 2

    def body(glu_ref, w_ref, b_ref, o_ref, ph_ref, pv_ref):
        j = pl.program_id(1)
        a0 = (glu_ref[:, 0:cb] * _sigmoid(glu_ref[:, cb:2 * cb])).reshape(rows, width, cb)
        w = w_ref[...]

        bias = b_ref[...]

        def run(pad_ref, horizontal):
            _fill_padded(pad_ref, a0, rows, width, ktaps, horizontal)

            def row(r, carry):
                at = pl.ds(pl.multiple_of(r * width, width), width)
                o_ref[at, :] = _conv_row(pad_ref, w, r, ktaps, width, horizontal, False) + bias
                return carry

            lax.fori_loop(0, rows, row, 0)

        @pl.when(j < nh)
        def _():
            run(ph_ref, True)

        @pl.when(j >= nh)
        def _():
            run(pv_ref, False)

    return pl.pallas_call(
        body, name="conv_fwd", grid=(nb, nblk),
        in_specs=[pl.BlockSpec((s, 2 * cb), lambda b, j: (b, j)),
                  pl.BlockSpec((None, ktaps, cb), lambda b, j: (j, 0, 0)),
                  pl.BlockSpec((1, cb), lambda b, j: (0, j))],
        out_specs=pl.BlockSpec((s, cb), lambda b, j: (b, j)),
        out_shape=jax.ShapeDtypeStruct((nb * s, d), F32),
        scratch_shapes=[pltpu.VMEM((rows, width + 32, cb), F32), pltpu.VMEM((rows + 2 * half_k, width, cb), F32)],
        compiler_params=_params())(pa, conv_w8, conv_b)


def _conv_bwd(pa, da1, conv_w8, nb, s):
    nblk, ktaps, cb = conv_w8.shape
    d = nblk * cb
    rows, width = s // GRID_W, GRID_W
    half_k = ktaps // 2
    nh = nblk // 2

    def body(glu_ref, da_ref, w_ref, dp_ref, dw_ref, db_ref, pha_ref, phd_ref, pva_ref, pvd_ref):
        j = pl.program_id(0)
        b = pl.program_id(1)
        a0 = (glu_ref[:, 0:cb] * _sigmoid(glu_ref[:, cb:2 * cb])).reshape(rows, width, cb)
        da1v = da_ref[...]
        d3 = da1v.reshape(rows, width, cb)
        w = w_ref[...]

        @pl.when(b == 0)
        def _():
            dw_ref[...] = jnp.zeros_like(dw_ref)
            db_ref[...] = jnp.zeros_like(db_ref)

        db_ref[...] += jnp.sum(da1v, axis=0, keepdims=True)

        def run(pa_ref, pd_ref, horizontal):
            _fill_padded(pa_ref, a0, rows, width, ktaps, horizontal)
            _fill_padded(pd_ref, d3, rows, width, ktaps, horizontal)

            def row(r, accs):
                at = pl.ds(pl.multiple_of(r * width, width), width)
                da0 = _conv_row(pd_ref, w, r, ktaps, width, horizontal, True)
                gv = glu_ref[at, 0:cb]
                sg = _sigmoid(glu_ref[at, cb:2 * cb])
                dp_ref[at, 0:cb] = (da0 * sg).astype(BF16)
                dp_ref[at, cb:2 * cb] = (da0 * gv * sg * (1.0 - sg)).astype(BF16)
                d_row = da_ref[at, :]
                out = []
                for t in range(ktaps):
                    prod = _conv_window(pa_ref, r, t - half_k, ktaps, width, horizontal) * d_row
                    out.append(accs[t] + jnp.sum(prod.reshape(width // 8, 8, cb), axis=0))
                return tuple(out)

            accs = lax.fori_loop(0, rows, row, tuple(jnp.zeros((8, cb), F32) for _ in range(ktaps)))
            for t in range(ktaps):
                dw_ref[t:t + 1, :] += jnp.sum(accs[t], axis=0, keepdims=True)

        @pl.when(j < nh)
        def _():
            run(pha_ref, phd_ref, True)

        @pl.when(j >= nh)
        def _():
            run(pva_ref, pvd_ref, False)

    return pl.pallas_call(
        body, name="conv_bwd", grid=(nblk, nb),
        in_specs=[pl.BlockSpec((s, 2 * cb), lambda j, b: (b, j)),
                  pl.BlockSpec((s, cb), lambda j, b: (b, j)),
                  pl.BlockSpec((None, ktaps, cb), lambda j, b: (j, 0, 0))],
        out_specs=(pl.BlockSpec((s, 2 * cb), lambda j, b: (b, j)),
                   pl.BlockSpec((None, ktaps, cb), lambda j, b: (j, 0, 0)),
                   pl.BlockSpec((1, cb), lambda j, b: (0, j))),
        out_shape=(jax.ShapeDtypeStruct((nb * s, 2 * d), BF16),
                   jax.ShapeDtypeStruct((nblk, ktaps, cb), F32), jax.ShapeDtypeStruct((1, d), F32)),
        scratch_shapes=[pltpu.VMEM((rows, width + 32, cb), F32), pltpu.VMEM((rows, width + 32, cb), F32),
                        pltpu.VMEM((rows + 2 * half_k, width, cb), F32),
                        pltpu.VMEM((rows + 2 * half_k, width, cb), F32)],
        compiler_params=_params())(pa, da1, conv_w8)


def _log_sigmoid(x):
    return jnp.minimum(x, 0.0) - jnp.log(1.0 + jnp.exp(-jnp.abs(x)))


def _decay_fwd(pb, up2, bias2, tm, lr_blk):
    t_all = pb.shape[0]
    n2 = up2.shape[1]

    def body(lr_ref, up_ref, b_ref, g_ref):
        logits = _mm(lr_ref[...], up_ref[...]) + b_ref[...]
        g_ref[...] = _log_sigmoid(logits) * (1.0 / GATE_TAU)

    return pl.pallas_call(
        body, name="decay_fwd", grid=(t_all // tm,),
        in_specs=[pl.BlockSpec((tm, LANE), lambda i: (i, lr_blk)),
                  pl.BlockSpec(up2.shape, lambda i: (0, 0)),
                  pl.BlockSpec((1, n2), lambda i: (0, 0))],
        out_specs=pl.BlockSpec((tm, n2), lambda i: (i, 0)),
        out_shape=jax.ShapeDtypeStruct((t_all, n2), F32),
        compiler_params=_params())(pb, up2, bias2)


def _decay_bwd(pb, up2, bias2, grads_f, grads_b, tiles, lr_blk, dk_, dv_):
    t_all = pb.shape[0]
    tm = tiles.tm
    n2 = up2.shape[1]
    nbw = 2 * dk_ + dv_ + LANE

    def body(lr_ref, up_ref, b_ref, dqf, dkf, dvf, dgf, dqb, dkb, dvb, dgb, dp_ref, dup_ref, dbias_ref):
        i = pl.program_id(0)
        pad = tiles.is_pad(i)
        live = lambda v: jnp.where(pad, 0.0, v)

        @pl.when(i == 0)
        def _():
            dup_ref[...] = jnp.zeros_like(dup_ref)
            dbias_ref[...] = jnp.zeros_like(dbias_ref)

        lr = lr_ref[...]
        up = up_ref[...]
        logits = _mm(lr, up) + b_ref[...]
        dg = live(jnp.concatenate([dgf[...], dgb[...]], axis=1))
        dlog = dg * (1.0 / GATE_TAU) * _sigmoid(-logits)
        dup_ref[...] += _mm_tn(lr, dlog)
        dbias_ref[...] += jnp.sum(dlog, axis=0, keepdims=True)
        both = lambda f, b: live(f[...].astype(F32) + b[...].astype(F32)).astype(BF16)
        dp_ref[:, 0:dk_] = both(dqf, dqb)
        dp_ref[:, dk_:2 * dk_] = both(dkf, dkb)
        dp_ref[:, 2 * dk_:2 * dk_ + dv_] = both(dvf, dvb)
        dp_ref[:, 2 * dk_ + dv_:nbw] = _mm_nt(dlog, up).astype(BF16)

    row = lambda w: pl.BlockSpec((tm, w), lambda i: (i, 0))
    return pl.pallas_call(
        body, name="decay_bwd", grid=(t_all // tm,),
        in_specs=[pl.BlockSpec((tm, LANE), lambda i: (i, lr_blk)),
                  pl.BlockSpec(up2.shape, lambda i: (0, 0)),
                  pl.BlockSpec((1, n2), lambda i: (0, 0)),
                  row(dk_), row(dk_), row(dv_), row(dk_), row(dk_), row(dk_), row(dv_), row(dk_)],
        out_specs=(row(nbw), pl.BlockSpec(up2.shape, lambda i: (0, 0)), pl.BlockSpec((1, n2), lambda i: (0, 0))),
        out_shape=(jax.ShapeDtypeStruct((t_all, nbw), BF16), jax.ShapeDtypeStruct(up2.shape, F32),
                   jax.ShapeDtypeStruct((1, n2), F32)),
        compiler_params=_params())(pb, up2, bias2, *grads_f, *grads_b)


def _scan_chunk(s, nl, nc, rev):
    if rev:
        return jnp.where(s < nc, nl + (nc - 1 - s), nl - 1 - (s - nc))
    return jnp.where(s < nc, nl + s, s - nc)


def _scan_lat_chunk(s, nl, nc, rev):
    first = nl - 1 if rev else 0
    return jnp.where(s < nc, first, _scan_chunk(s, nl, nc, rev))


def _tri_mm(m_bf, x):
    hi = x.astype(BF16)
    r1 = x - hi.astype(F32)
    mid = r1.astype(BF16)
    lo = (r1 - mid.astype(F32)).astype(BF16)
    dot = lambda p: jnp.dot(m_bf, p, preferred_element_type=F32)
    return dot(hi) + dot(mid) + dot(lo)


def _chunk_masks(c, rev):
    ii = lax.broadcasted_iota(jnp.int32, (c, c), 0)
    jj = lax.broadcasted_iota(jnp.int32, (c, c), 1)
    return ((ii <= jj), (ii >= jj)) if rev else ((ii >= jj), (ii <= jj))


def _chunk_terms(q, k, b, far, mid):
    bf, bm = b[far:far + 1, :], b[mid:mid + 1, :]
    e = jnp.exp(b)
    em = jnp.exp(b - bm)
    eim = jnp.exp(bm - b)
    ed = jnp.exp(bf - b)
    return dict(e=e, em=em, eim=eim, ed=ed, dec=jnp.exp(bf), qe=q * e, qem=q * em, kim=k * eim, kd=k * ed)


def _gla_fwd(pb3, pv3, g3, nb, s_len, c_len, dk_, dv_):
    c = CHUNK
    nl, nc = s_len // c, c_len // c
    ns = nl + nc
    hk, hv = dk_ // HEADS, dv_ // HEADS
    l_len = pb3.shape[1]
    scale = hk ** -0.5
    mid = c // 2

    def body(*refs):
        ins, outs, z_scr = refs[:8], refs[8:14], refs[14]
        s = pl.program_id(0)

        @pl.when(s == 0)
        def _():
            z_scr[...] = jnp.zeros_like(z_scr)

        qs = jnp.where(s >= nc, scale, 0.0)
        for di, rev in enumerate((False, True)):
            q_ref, k_ref, v_ref, g_ref = ins[4 * di:4 * di + 4]
            o_ref, zs_ref, b_ref = outs[3 * di:3 * di + 3]
            mask, _ = _chunk_masks(c, rev)
            m_bf = mask.astype(BF16)
            far = 0 if rev else c - 1
            for b in range(nb):
                bc = _tri_mm(m_bf, g_ref[b])
                b_ref[b] = bc
                for h in range(HEADS):
                    ks, vs = slice(h * hk, (h + 1) * hk), slice(h * hv, (h + 1) * hv)
                    zi = (di * nb + b) * HEADS + h
                    v = v_ref[b, :, vs]
                    t = _chunk_terms(q_ref[b, :, ks] * qs, k_ref[b, :, ks], bc[:, ks], far, mid)
                    a = jnp.where(mask, _mm_nt(t["qem"], t["kim"]), 0.0)
                    z = z_scr[zi]
                    zs_ref[0, b * HEADS + h] = z
                    o_ref[b, :, vs] = _mm(a, v) + _mm_nt(t["qe"], z)
                    z_scr[zi] = z * t["dec"] + _mm_tn(v, t["kd"])

    in_specs, out_specs, out_shape = [], [], []
    for di, rev in enumerate((False, True)):
        ch = functools.partial(_scan_chunk, nl=nl, nc=nc, rev=rev)
        lch = functools.partial(_scan_lat_chunk, nl=nl, nc=nc, rev=rev)
        in_specs += [pl.BlockSpec((nb, c, dk_), lambda s, ch=ch: (0, ch(s), 0)),
                     pl.BlockSpec((nb, c, dk_), lambda s, ch=ch: (0, ch(s), 1)),
                     pl.BlockSpec((nb, c, dv_), lambda s, ch=ch: (0, ch(s), 0)),
                     pl.BlockSpec((nb, c, dk_), lambda s, ch=ch, di=di: (0, ch(s), di))]
        out_specs += [pl.BlockSpec((nb, c, dv_), lambda s, lch=lch: (0, lch(s), 0)),
                      pl.BlockSpec((1, nb * HEADS, hv, hk), lambda s: (s, 0, 0, 0)),
                      pl.BlockSpec((nb, c, dk_), lambda s, ch=ch: (0, ch(s), 0))]
        out_shape += [jax.ShapeDtypeStruct((nb, s_len, dv_), F32),
                      jax.ShapeDtypeStruct((ns, nb * HEADS, hv, hk), F32),
                      jax.ShapeDtypeStruct((nb, l_len, dk_), F32)]
    return pl.pallas_call(
        body, name="gla_fwd", grid=(ns,), in_specs=in_specs, out_specs=tuple(out_specs), out_shape=tuple(out_shape),
        scratch_shapes=[pltpu.VMEM((2 * nb * HEADS, hv, hk), F32)],
        compiler_params=_params())(pb3, pb3, pv3, g3, pb3, pb3, pv3, g3)


def _gla_bwd(pb3, pv3, do3, fwd_saved, nb, s_len, c_len, dk_, dv_):
    c = CHUNK
    nl, nc = s_len // c, c_len // c
    ns = nl + nc
    hk, hv = dk_ // HEADS, dv_ // HEADS
    l_len = pb3.shape[1]
    scale = hk ** -0.5
    mid = c // 2
    zs_f, b_f, zs_b, b_b = fwd_saved

    def body(*refs):
        ins, outs, dz_scr = refs[:12], refs[12:20], refs[20]
        s = pl.program_id(0)
        step = ns - 1 - s

        @pl.when(s == 0)
        def _():
            dz_scr[...] = jnp.zeros_like(dz_scr)

        lat = step >= nc
        qs = jnp.where(lat, scale, 0.0)
        dmul = jnp.where(lat, 1.0, 0.0)
        for di, rev in enumerate((False, True)):
            q_ref, k_ref, v_ref, b_ref, do_ref, zs_ref = ins[6 * di:6 * di + 6]
            dq_ref, dk_ref, dv_ref, dg_ref = outs[4 * di:4 * di + 4]
            mask, mask_t = _chunk_masks(c, rev)
            mt_bf = mask_t.astype(BF16)
            far = 0 if rev else c - 1
            far_row = lax.broadcasted_iota(jnp.int32, (c, hk), 0) == far
            for b in range(nb):
                db_parts = []
                for h in range(HEADS):
                    ks, vs = slice(h * hk, (h + 1) * hk), slice(h * hv, (h + 1) * hv)
                    zi = (di * nb + b) * HEADS + h
                    v = v_ref[b, :, vs]
                    d_o = do_ref[b, :, vs] * dmul
                    t = _chunk_terms(q_ref[b, :, ks] * qs, k_ref[b, :, ks], b_ref[b, :, ks], far, mid)
                    qem, kim, qe, kd = t["qem"], t["kim"], t["qe"], t["kd"]
                    a_t = jnp.where(mask_t, _mm_nt(kim, qem), 0.0)
                    d_a = jnp.where(mask, _mm_nt(d_o, v), 0.0)
                    d_at = jnp.where(mask_t, _mm_nt(v, d_o), 0.0)
                    z = zs_ref[0, b * HEADS + h]
                    dzn = dz_scr[zi]
                    dv_ref[b, :, vs] = (_mm(a_t, d_o) + _mm_nt(kd, dzn)).astype(dv_ref.dtype)
                    dqem = _mm(d_a, kim)
                    dkim = _mm(d_at, qem)
                    dqe = _mm(d_o, z)
                    dkd = _mm(v, dzn)
                    ddec = jnp.sum(z * dzn, axis=0, keepdims=True)
                    dz_scr[zi] = dzn * t["dec"] + _mm_tn(d_o, qe)
                    dq_ref[b, :, ks] = ((dqem * t["em"] + dqe * t["e"]) * qs).astype(dq_ref.dtype)
                    dk_ref[b, :, ks] = (dkim * t["eim"] + dkd * t["ed"]).astype(dk_ref.dtype)
                    db = dqem * qem - dkim * kim + dqe * qe - dkd * kd
                    extra = jnp.sum(dkd * kd, axis=0, keepdims=True) + ddec * t["dec"]
                    db_parts.append(db + jnp.where(far_row, extra, 0.0))
                dg_ref[b] = _tri_mm(mt_bf, jnp.concatenate(db_parts, axis=1))

    in_specs, out_specs, out_shape, args = [], [], [], []
    for di, rev in enumerate((False, True)):
        ch = lambda s, rev=rev: _scan_chunk(ns - 1 - s, nl, nc, rev)
        lch = lambda s, rev=rev: _scan_lat_chunk(ns - 1 - s, nl, nc, rev)
        in_specs += [pl.BlockSpec((nb, c, dk_), lambda s, ch=ch: (0, ch(s), 0)),
                     pl.BlockSpec((nb, c, dk_), lambda s, ch=ch: (0, ch(s), 1)),
                     pl.BlockSpec((nb, c, dv_), lambda s, ch=ch: (0, ch(s), 0)),
                     pl.BlockSpec((nb, c, dk_), lambda s, ch=ch: (0, ch(s), 0)),
                     pl.BlockSpec((nb, c, dv_), lambda s, lch=lch: (0, lch(s), 0)),
                     pl.BlockSpec((1, nb * HEADS, hv, hk), lambda s: (ns - 1 - s, 0, 0, 0))]
        args += [pb3, pb3, pv3, (b_b if rev else b_f), do3, (zs_b if rev else zs_f)]
        for w, dt in ((dk_, BF16), (dk_, BF16), (dv_, BF16), (dk_, F32)):
            out_specs.append(pl.BlockSpec((nb, c, w), lambda s, ch=ch: (0, ch(s), 0)))
            out_shape.append(jax.ShapeDtypeStruct((nb, l_len, w), dt))
    return pl.pallas_call(
        body, name="gla_bwd", grid=(ns,), in_specs=in_specs, out_specs=tuple(out_specs), out_shape=tuple(out_shape),
        scratch_shapes=[pltpu.VMEM((2 * nb * HEADS, hv, hk), F32)],
        compiler_params=_params())(*args)


def _tail(a1, pa, o_f, o_b, x2, tgt, mod, wc, wg, wo, ln_g, ln_b, gn_t, fg, nb, tm, n_split):
    tl, d = x2.shape
    nt = tl // tm
    per_ex = nt // nb
    hv = d // HEADS
    nrow = mod.shape[0]

    def part(shared, a1_ref, z_ref, r_ref, mc_ref, mg_ref, of_ref, ob_ref, x_ref, t_ref,
             dp_ref, da1_ref, do_ref, gx_ref, mrg_ref, dmo_ref, yci_ref, dyc_ref, ogi_ref, dyg_ref, sm_ref):
        bidx, gate, lng, lnb, fgv, gn, wc_, wg_, wo_ = shared

        a1v = a1_ref[...]
        mu = jnp.mean(a1v, axis=-1, keepdims=True)
        xc = a1v - mu
        rs = lax.rsqrt(jnp.mean(xc * xc, axis=-1, keepdims=True) + EPS)
        xh = xc * rs
        a2 = xh * lng + lnb
        s2 = _sigmoid(a2)
        a3 = a2 * s2
        zv = z_ref[...]
        sz = _sigmoid(zv)
        siluz = zv * sz
        ycin = a3 * siluz
        yconv = _mm(ycin, wc_)

        o = of_ref[...] + ob_ref[...]
        ohat_parts, rn_parts = [], []
        for h in range(HEADS):
            oh = o[:, h * hv:(h + 1) * hv]
            rn = lax.rsqrt(jnp.mean(oh * oh, axis=-1, keepdims=True) + EPS)
            ohat_parts.append(oh * rn)
            rn_parts.append(rn)
        ohat = jnp.concatenate(ohat_parts, axis=1)
        on = ohat * gn
        rv = r_ref[...]
        sr = _sigmoid(rv)
        silur = rv * sr
        ogin = on * silur
        ygla = _mm(ogin, wg_)

        sc = _sigmoid(mc_ref[...])
        sg = _sigmoid(mg_ref[...])
        merged = sc * yconv + sg * ygla
        mo = _mm(merged, wo_)
        hn = x_ref[...] + gate * mo
        rf = lax.rsqrt(jnp.mean(hn * hn, axis=-1, keepdims=True) + EPS)
        yh = hn * rf
        err = yh * fgv - t_ref[...]
        loss_part = 0.5 * jnp.sum(err * err) * (1.0 / d)

        dy = err * (1.0 / d)
        dfg = jnp.sum(dy * yh, axis=0, keepdims=True)
        dyh = dy * fgv
        dhn = rf * (dyh - yh * jnp.mean(dyh * yh, axis=-1, keepdims=True))
        gx_ref[...] = dhn
        dgate = jnp.sum(dhn * mo, axis=0, keepdims=True)
        dmo = gate * dhn
        dmerged = _mm_nt(dmo, wo_)
        dyconv = dmerged * sc
        dygla = dmerged * sg
        dp_ref[:, 2 * d:3 * d] = (dmerged * yconv * sc * (1.0 - sc)).astype(BF16)
        dp_ref[:, 3 * d:4 * d] = (dmerged * ygla * sg * (1.0 - sg)).astype(BF16)
        dycin = _mm_nt(dyconv, wc_)
        dogin = _mm_nt(dygla, wg_)
        mrg_ref[...] = merged.astype(BF16)
        dmo_ref[...] = dmo.astype(BF16)
        yci_ref[...] = ycin.astype(BF16)
        dyc_ref[...] = dyconv.astype(BF16)
        ogi_ref[...] = ogin.astype(BF16)
        dyg_ref[...] = dygla.astype(BF16)

        da3 = dycin * siluz
        dp_ref[:, 0:d] = (dycin * a3 * _dsilu(zv, sz)).astype(BF16)
        da2 = da3 * _dsilu(a2, s2)
        dlng = jnp.sum(da2 * xh, axis=0, keepdims=True)
        dlnb = jnp.sum(da2, axis=0, keepdims=True)
        dxh = da2 * lng
        da1_ref[...] = rs * (dxh - jnp.mean(dxh, axis=-1, keepdims=True)
                             - xh * jnp.mean(dxh * xh, axis=-1, keepdims=True))

        don = dogin * silur
        dp_ref[:, d:2 * d] = (dogin * on * _dsilu(rv, sr)).astype(BF16)
        dgn = jnp.sum(don * ohat, axis=0, keepdims=True)
        dyn = don * gn
        for h in range(HEADS):
            vs = slice(h * hv, (h + 1) * hv)
            oh_hat = ohat_parts[h]
            dh = dyn[:, vs]
            do_ref[:, vs] = (rn_parts[h] * (dh - oh_hat * jnp.mean(dh * oh_hat, axis=-1, keepdims=True))
                             ).astype(BF16)

        sm_ref[0:1, :] += dfg
        sm_ref[1:2, :] += dlng
        sm_ref[2:3, :] += dlnb
        sm_ref[3:4, :] += dgn
        sm_ref[4:5, :] += jnp.zeros((1, d), F32) + loss_part
        for b in range(nb):
            sm_ref[8 + b:9 + b, :] += jnp.where(bidx == b, dgate, 0.0)

    def body(*refs):
        mod_ref, wc_ref, wg_ref, wo_ref, lng_ref, lnb_ref, gn_ref, fg_ref = refs[9:17]
        sm_ref = refs[27]
        i = pl.program_id(0)

        @pl.when(i == 0)
        def _():
            sm_ref[...] = jnp.zeros_like(sm_ref)

        bidx = i // per_ex
        shared = (bidx, _rowsel(mod_ref[...], bidx, nb)[:, 2 * d:3 * d], lng_ref[...], lnb_ref[...], fg_ref[...],
                  jnp.concatenate([gn_ref[...]] * HEADS, axis=1), wc_ref[...], wg_ref[...], wo_ref[...])
        rows_per = tm // n_split
        for p in range(n_split):
            rows = pl.ds(p * rows_per, rows_per)
            part(shared, *[r.at[rows] for r in refs[0:9]], *[r.at[rows] for r in refs[17:27]], sm_ref)

    row = pl.BlockSpec((tm, d), lambda i: (i, 0))
    pcol = lambda blk: pl.BlockSpec((tm, d), lambda i: (i, blk))
    full = lambda arr: pl.BlockSpec(arr.shape, lambda i: (0,) * arr.ndim)
    bfo = jax.ShapeDtypeStruct((tl, d), BF16)
    f32o = jax.ShapeDtypeStruct((tl, d), F32)
    return pl.pallas_call(
        body, name="tail", grid=(nt,),
        in_specs=[row, pcol(2), pcol(3), pcol(4), pcol(5), row, row, row, row, full(mod), full(wc), full(wg),
                  full(wo), full(ln_g), full(ln_b), full(gn_t), full(fg)],
        out_specs=(pl.BlockSpec((tm, 4 * d), lambda i: (i, 0)), row, row, row, row, row, row, row, row, row,
                   pl.BlockSpec((16, d), lambda i: (0, 0))),
        out_shape=(jax.ShapeDtypeStruct((tl, 4 * d), BF16), f32o, bfo, f32o, bfo, bfo, bfo, bfo, bfo, bfo,
                   jax.ShapeDtypeStruct((16, d), F32)),
        compiler_params=_params())(a1, pa, pa, pa, pa, o_f, o_b, x2, tgt, mod, wc, wg, wo, ln_g, ln_b, gn_t, fg)


def _local_step(x, c, ctx, tgt, c_ctx, ada_w8, ada_b, norm_g, w_a, b_a, w_b, b_b, conv_w8, conv_b, ln_g, ln_b,
                up2, bias2, gla_norm_g, final_norm_g, proj, on_grads=None, on_du_a1=None):
    nb, s_len, d = x.shape
    c_len = ctx.shape[1]
    dk_, dv_ = d // 2, d
    tl, tc = nb * s_len, nb * c_len
    nbw = 2 * dk_ + dv_ + LANE
    tm = math.gcd(256, c_len)
    tiles = _Tiles(nb, s_len, c_len, tm, 2)
    tmm = tiles.big * tm
    l_len = tiles.rows_per_ex
    t_all = nb * l_len
    x2, ctx2, tgt2 = x.reshape(tl, d), ctx.reshape(tc, d), tgt.reshape(tl, d)

    cv = jnp.zeros((8, d), F32).at[0:nb].set(c).at[nb].set(c_ctx.reshape(d))
    mod = _ada_fwd(cv, ada_w8, ada_b)
    u = _norm_fwd(x2, ctx2, mod, norm_g, tiles)
    pa = _matmul_bias("inproj_a", u, w_a, b_a, tl, tmm, _tile(6 * d, 3072), tiles.big_all_of_lat)
    pb, pv = _inproj_b(u, w_b, b_b, tmm, dk_, dv_)

    a1 = _conv_fwd(pa, conv_w8, conv_b, nb, s_len)
    lr_blk = (2 * dk_) // LANE
    g_all = _decay_fwd(pb, up2, bias2, tm, lr_blk)
    pb3, pv3 = pb.reshape(nb, l_len, 2 * dk_ + LANE), pv.reshape(nb, l_len, dv_)
    o_f, zs_f, b_f, o_b, zs_b, b_b2 = _gla_fwd(pb3, pv3, g_all.reshape(nb, l_len, 2 * dk_), nb, s_len, c_len,
                                               dk_, dv_)

    conv_proj, gla_proj, w_out = proj(a1) if callable(proj) else proj
    tt = math.gcd(256, s_len)
    (dp_a2, da1, d_o, gx1, merged, dmo, ycin, dyconv, ogin, dygla, small) = _tail(
        a1, pa, o_f.reshape(tl, dv_), o_b.reshape(tl, dv_), x2, tgt2, mod, conv_proj, gla_proj, w_out, ln_g, ln_b,
        gla_norm_g, final_norm_g, nb, tt, 2)

    lat3 = lambda a: a.reshape(nb, s_len, a.shape[-1])
    tnw = _tile(d, 1024)
    tnp = _tile(d, 512)
    d_w_out, _ = _matmul_tn_whole("dw_out", lat3(merged), lat3(dmo), s_len, tnp)
    d_conv_proj, _ = _matmul_tn_whole("dw_conv_proj", lat3(ycin), lat3(dyconv), s_len, tnp)
    d_gla_proj, _ = _matmul_tn_whole("dw_gla_proj", lat3(ogin), lat3(dygla), s_len, tnp)

    dp_a1, d_conv_w8, d_conv_b = _conv_bwd(pa, da1, conv_w8, nb, s_len)
    gl = _gla_bwd(pb3, pv3, d_o.reshape(nb, s_len, dv_), (zs_f, b_f, zs_b, b_b2), nb, s_len, c_len, dk_, dv_)
    gl = [g_.reshape(t_all, g_.shape[-1]) for g_ in gl]
    dp_b, d_up2, d_bias2 = _decay_bwd(pb, up2, bias2, gl[0:4], gl[4:8], tiles, lr_blk, dk_, dv_)

    u3 = u.reshape(nb, l_len, d)
    dw_a1, db_a1 = _matmul_tn_whole("dw_a1", u3, lat3(dp_a1), s_len, tnw)
    dw_a2, db_a2 = _matmul_tn_whole("dw_a2", u3, lat3(dp_a2), s_len, tnw)
    dw_b, db_b = _matmul_tn("dw_b", u, dp_b, t_all, tmm, nbw)
    grads = dict(w_a1=dw_a1, w_a2=dw_a2, w_b=dw_b, conv_w8=d_conv_w8, conv_proj=d_conv_proj, up2=d_up2,
                 gla_proj=d_gla_proj, w_out=d_w_out)

    tka = _tile(2 * d, 2048)
    du_a1 = _matmul_nt("du_a1", dp_a1, w_a, 0, tmm, tka, after=on_grads(grads) if on_grads else ())
    du_a2 = _matmul_nt("du_a2", dp_a2, w_a, (2 * d) // tka, tmm, tka, after=on_du_a1(du_a1) if on_du_a1 else ())
    du_b = _matmul_nt("du_b", dp_b, w_b, 0, tmm, nbw)
    grad_x2, dmod_ss, d_norm_g = _norm_bwd(x2, ctx2, mod, norm_g, [du_a1, du_a2], du_b, gx1, tiles)
    d_ada_w8, d_ada_b, d_cv = _ada_bwd(cv, ada_w8, dmod_ss, small, nb)

    return dict(
        grads, grad_x=grad_x2.reshape(nb, s_len, d), small=small, cv=d_cv, ada_w8=d_ada_w8, ada_b=d_ada_b,
        norm_g=d_norm_g, b_a1=db_a1, b_a2=db_a2, b_b=db_b, conv_b=d_conv_b, bias2=d_bias2)


def _regroup_pieces(d, r, wshard):
    cb = d // N_DEV
    segs = []
    for j in range(N_DEV):
        segs.append((j * cb, cb, 0, 2 * j * cb))
    for j in range(N_DEV):
        segs.append((d + j * cb, cb, 0, (2 * j + 1) * cb))
    segs += [(2 * d, d, 0, 2 * d), (3 * d, 2 * d + 2 * r, 1, 0), (5 * d + 2 * r, 3 * d, 0, 3 * d)]
    pieces = []
    for o0, w, dst, d0 in segs:
        lo = o0
        while lo < o0 + w:
            j = lo // wshard
            hi = min(o0 + w, (j + 1) * wshard)
            pieces.append((j, lo - j * wshard, hi - lo, dst, d0 + lo - o0))
            lo = hi
    return pieces


def _regroup(o, d, r):
    n_in = 8 * d + 2 * r
    parts = ([], [])
    for _, s0, n, dst, _ in sorted(_regroup_pieces(d, r, n_in), key=lambda p: (p[3], p[4])):
        parts[dst].append(o[..., s0:s0 + n])
    pad = jnp.zeros(o.shape[:-1] + (LANE - 2 * r,), o.dtype)
    return jnp.concatenate(parts[0], axis=-1), jnp.concatenate(parts[1] + [pad], axis=-1)


def _unshard_w_in(g_win, d, r, after=()):
    n_sh, _, ws = g_win.shape
    nbw = 2 * d + LANE
    pieces = _regroup_pieces(d, r, ws)
    tr = math.gcd(d, 256)

    def body(g_ref, *rest):
        a_ref, b_ref = rest[len(after):]
        dsts = (a_ref, b_ref)
        for j, s0, n, dst, d0 in pieces:
            dsts[dst][:, pl.ds(d0, n)] = g_ref[j, :, pl.ds(s0, n)]
        b_ref[:, pl.ds(2 * d + 2 * r, LANE - 2 * r)] = jnp.zeros((tr, LANE - 2 * r), b_ref.dtype)

    return pl.pallas_call(
        body, name="unshard_w_in", grid=(d // tr,),
        in_specs=[pl.BlockSpec((n_sh, tr, ws), lambda i: (0, i, 0))] + [_ANY] * len(after),
        out_specs=(pl.BlockSpec((tr, 6 * d), lambda i: (i, 0)), pl.BlockSpec((tr, nbw), lambda i: (i, 0))),
        out_shape=(jax.ShapeDtypeStruct((d, 6 * d), g_win.dtype), jax.ShapeDtypeStruct((d, nbw), g_win.dtype)),
        compiler_params=_params())(g_win, *after)


def _reshard_w_in(dw_a1, dw_a2, dw_b, d, r):
    ws = (8 * d + 2 * r) // N_DEV
    pieces = _regroup_pieces(d, r, ws)
    tr = math.gcd(d, 256)

    def body(a1_ref, a2_ref, b_ref, o_ref):
        for j, s0, n, dst, d0 in pieces:
            if dst == 1:
                src = b_ref[:, pl.ds(d0, n)]
            elif d0 < 2 * d:
                src = a1_ref[:, pl.ds(d0, n)]
            else:
                src = a2_ref[:, pl.ds(d0 - 2 * d, n)]
            o_ref[j, :, pl.ds(s0, n)] = src

    row = lambda w: pl.BlockSpec((tr, w), lambda i: (i, 0))
    return pl.pallas_call(
        body, name="reshard_w_in", grid=(d // tr,),
        in_specs=[row(2 * d), row(4 * d), row(2 * d + LANE)],
        out_specs=pl.BlockSpec((N_DEV, tr, ws), lambda i: (0, i, 0)),
        out_shape=jax.ShapeDtypeStruct((N_DEV, d, ws), dw_b.dtype),
        compiler_params=_params())(dw_a1, dw_a2, dw_b)


_SMALL = ("c_ctx", "ada_b", "norm_g", "b_in", "conv_b", "conv_ln_g", "conv_ln_b", "decay_bias_fwd",
          "decay_bias_bwd", "gla_norm_g", "final_norm_g")


def _small_layout(d, r):
    sizes = dict(c_ctx=d, ada_b=3 * d, norm_g=d, b_in=8 * d + 2 * r, conv_b=d, conv_ln_g=d, conv_ln_b=d,
                 decay_bias_fwd=d // 2, decay_bias_bwd=d // 2, gla_norm_g=d // HEADS, final_norm_g=d, loss=1)
    table, off = {}, 0
    for name in _SMALL + ("loss",):
        table[name] = (off, sizes[name])
        off += -(-sizes[name] // LANE) * LANE
    return table, off


def _pack_small(g, nb, d, r):
    table, width = _small_layout(d, r)
    hv = d // HEADS
    pieces = _regroup_pieces(d, r, 8 * d + 2 * r)
    names = ("small", "cv", "ada_b", "norm_g", "b_a1", "b_a2", "b_b", "conv_b", "bias2")

    def body(sm, cv, ab, ng, ba1, ba2, bb, cvb, b2, o_ref):
        o_ref[...] = jnp.zeros_like(o_ref)

        def put(name, val):
            off, n = table[name]
            o_ref[:, pl.ds(off, n)] = val

        put("c_ctx", cv[nb:nb + 1, :])
        put("ada_b", ab[...])
        put("norm_g", ng[...])
        off_b = table["b_in"][0]
        for _, s0, n, dst, d0 in pieces:
            if dst == 1:
                src = bb[:, pl.ds(d0, n)]
            elif d0 < 2 * d:
                src = ba1[:, pl.ds(d0, n)]
            else:
                src = ba2[:, pl.ds(d0 - 2 * d, n)]
            o_ref[:, pl.ds(off_b + s0, n)] = src
        put("conv_b", cvb[...])
        put("conv_ln_g", sm[1:2, :])
        put("conv_ln_b", sm[2:3, :])
        put("decay_bias_fwd", b2[:, 0:d // 2])
        put("decay_bias_bwd", b2[:, d // 2:d])
        gn = sm[3:4, 0:hv]
        for h in range(1, HEADS):
            gn = gn + sm[3:4, h * hv:(h + 1) * hv]
        put("gla_norm_g", gn)
        put("final_norm_g", sm[0:1, :])
        put("loss", sm[4:5, 0:1])

    return pl.pallas_call(body, name="pack_small", out_shape=jax.ShapeDtypeStruct((1, width), F32),
                          compiler_params=_params())(*[g[k] for k in names])


def _small_adam(parts, ws, ms, vs, d, r):
    table, width = _small_layout(d, r)
    n_parts = parts.shape[0]
    k = len(_SMALL)
    bc1 = 1.0 - ADAM_B1 ** ADAM_STEP
    bc2 = 1.0 - ADAM_B2 ** ADAM_STEP

    def body(p_ref, *refs):
        w_refs, m_refs, v_refs = refs[0:k], refs[k:2 * k], refs[2 * k:3 * k]
        outs = refs[3 * k:]
        tot = p_ref[0]
        for i in range(1, n_parts):
            tot = tot + p_ref[i]
        for i, name in enumerate(_SMALL):
            off, n = table[name]
            g = tot[:, off:off + n]
            mn = ADAM_B1 * m_refs[i][...] + (1.0 - ADAM_B1) * g
            vn = ADAM_B2 * v_refs[i][...] + (1.0 - ADAM_B2) * (g * g)
            outs[i][...] = g
            outs[k + i][...] = -ADAM_LR * ((mn / bc1) / (jnp.sqrt(vn / bc2) + ADAM_EPS) + ADAM_WD * w_refs[i][...])
            outs[2 * k + i][...] = mn
            outs[3 * k + i][...] = vn
        off, _ = table["loss"]
        outs[4 * k][...] = tot[:, off:off + 1]

    shapes = [jax.ShapeDtypeStruct(w.shape, F32) for w in ws]
    res = pl.pallas_call(body, name="small_adam", out_shape=tuple(shapes * 4 + [jax.ShapeDtypeStruct((1, 1), F32)]),
                         compiler_params=_params())(parts, *ws, *ms, *vs)
    return res[0:k], res[k:2 * k], res[2 * k:3 * k], res[3 * k:4 * k], res[4 * k]


def _mesh_pos():
    return lax.axis_index("x"), lax.axis_index("y"), lax.axis_index("c")


def _all_gather(arrs):
    n = len(arrs)
    ns = 9
    split = [a.ndim == 2 and a.shape[0] % 32 == 0 for a in arrs]

    def body(*refs):
        ins, outs = refs[:n], refs[n:2 * n]
        send_sems, recv_sems, local_sems = refs[2 * n:]
        x, y, c = _mesh_pos()
        me, sibling = (x, y, c), (x, y, 1 - c)
        xn, yn, dg = (1 - x, y, c), (x, 1 - y, c), (1 - x, 1 - y, c)
        other = lambda pos: (pos[0], pos[1], 1 - c)

        def slot(a, pos, half):
            ref = outs[a].at[4 * pos[0] + 2 * pos[1] + pos[2]]
            if half is None:
                return ref
            rows = arrs[a].shape[0] // 2
            return ref.at[pl.ds(half * rows, rows)]

        def copy(a, k, block, to, src=None, half=None):
            dst = slot(a, block, half)
            return pltpu.make_async_remote_copy(
                src_ref=dst if src is None else src, dst_ref=dst,
                send_sem=send_sems.at[ns * a + k], recv_sem=recv_sems.at[ns * a + k],
                device_id=to, device_id_type=MESH)

        h0 = lambda a: 0 if split[a] else None
        mine = [pltpu.make_async_copy(ins[a], slot(a, me, None), local_sems.at[a]) for a in range(n)]
        for cp in mine:
            cp.start()
        sent = []
        for a in range(n):
            sent += [copy(a, 0, me, sibling, src=ins[a]), copy(a, 1, me, xn, src=ins[a]),
                     copy(a, 2, me, yn, src=ins[a])]
        for cp in sent:
            cp.start()

        def pass_on(cp):
            cp.start()
            sent.append(cp)

        for a in range(n):
            copy(a, 1, xn, me).wait_recv()
            pass_on(copy(a, 3, xn, sibling))
            pass_on(copy(a, 4, xn, yn, half=h0(a)))
        for a in range(n):
            copy(a, 2, yn, me).wait_recv()
            pass_on(copy(a, 5, yn, sibling))
            if split[a]:
                pass_on(copy(a, 6, yn, xn, half=1))
        for a in range(n):
            copy(a, 4, dg, me, half=h0(a)).wait_recv()
            pass_on(copy(a, 7, dg, sibling, half=h0(a)))
            if split[a]:
                copy(a, 6, dg, me, half=1).wait_recv()
                pass_on(copy(a, 8, dg, sibling, half=1))
        for a in range(n):
            copy(a, 0, sibling, me).wait_recv()
            copy(a, 3, other(xn), me).wait_recv()
            copy(a, 5, other(yn), me).wait_recv()
            copy(a, 7, other(dg), me, half=h0(a)).wait_recv()
            if split[a]:
                copy(a, 8, other(dg), me, half=1).wait_recv()
        for cp in sent:
            cp.wait_send()
        for cp in mine:
            cp.wait()

    return pl.pallas_call(
        body, name="all_gather",
        out_shape=tuple(jax.ShapeDtypeStruct((N_DEV,) + a.shape, a.dtype) for a in arrs),
        in_specs=[_ANY] * n, out_specs=tuple([_ANY] * n),
        scratch_shapes=[pltpu.SemaphoreType.DMA((ns * n,)), pltpu.SemaphoreType.DMA((ns * n,)),
                        pltpu.SemaphoreType.DMA((n,))],
    )(*arrs)


def _exchange_sibling(arrs):
    n = len(arrs)

    def body(*refs):
        ins, outs = refs[:n], refs[n:2 * n]
        send_sems, recv_sems = refs[2 * n:]
        x, y, c = _mesh_pos()
        copies = [pltpu.make_async_remote_copy(
            src_ref=ins[a].at[2 * k + (1 - c)], dst_ref=outs[a].at[k],
            send_sem=send_sems.at[4 * a + k], recv_sem=recv_sems.at[4 * a + k],
            device_id=(x, y, 1 - c), device_id_type=MESH) for a in range(n) for k in range(4)]
        for cp in copies:
            cp.start()
        for cp in copies:
            cp.wait_recv()
        for cp in copies:
            cp.wait_send()

    return pl.pallas_call(
        body, name="grad_exchange_sibling",
        out_shape=tuple(jax.ShapeDtypeStruct((4,) + a.shape[1:], a.dtype) for a in arrs),
        in_specs=[_ANY] * n, out_specs=tuple([_ANY] * n),
        scratch_shapes=[pltpu.SemaphoreType.DMA((4 * n,)), pltpu.SemaphoreType.DMA((4 * n,))],
    )(*arrs)


def _pair_sum(name, mine, theirs):
    _, r, cdim = mine.shape
    tr = r if (r % 8 or r <= 256) else math.gcd(r, 256)

    def body(m_ref, t_ref, o_ref):
        c = lax.axis_index("c")
        own = jnp.where(c == 0, m_ref[:, 0].astype(F32), m_ref[:, 1].astype(F32))
        o_ref[...] = (own + t_ref[...].astype(F32)).astype(o_ref.dtype)

    return pl.pallas_call(
        body, name=name, grid=(r // tr,),
        in_specs=[pl.BlockSpec((4, 2, tr, cdim), lambda i: (0, 0, i, 0)),
                  pl.BlockSpec((4, tr, cdim), lambda i: (0, i, 0))],
        out_specs=pl.BlockSpec((4, tr, cdim), lambda i: (0, i, 0)),
        out_shape=jax.ShapeDtypeStruct((4, r, cdim), mine.dtype),
        compiler_params=_params())(mine.reshape(4, 2, r, cdim), theirs)


def _exchange_chips(arrs):
    n = len(arrs)

    def body(*refs):
        ins, outs = refs[:n], refs[n:2 * n]
        send_sems, recv_sems, local_sems = refs[2 * n:]
        x, y, c = _mesh_pos()
        my_chip = 2 * x + y
        mine = [pltpu.make_async_copy(ins[a].at[my_chip], outs[a].at[my_chip], local_sems.at[a]) for a in range(n)]
        for cp in mine:
            cp.start()
        copies = []
        for rel in range(1, 4):
            px = 1 - x if rel & 2 else x
            py = 1 - y if rel & 1 else y
            for a in range(n):
                copies.append(pltpu.make_async_remote_copy(
                    src_ref=ins[a].at[2 * px + py], dst_ref=outs[a].at[my_chip],
                    send_sem=send_sems.at[3 * a + rel - 1], recv_sem=recv_sems.at[3 * a + rel - 1],
                    device_id=(px, py, c), device_id_type=MESH))
        for cp in copies:
            cp.start()
        for cp in copies:
            cp.wait_recv()
        for cp in copies:
            cp.wait_send()
        for cp in mine:
            cp.wait()

    return pl.pallas_call(
        body, name="grad_exchange_chips",
        out_shape=tuple(jax.ShapeDtypeStruct(a.shape, a.dtype) for a in arrs),
        in_specs=[_ANY] * n, out_specs=tuple([_ANY] * n),
        scratch_shapes=[pltpu.SemaphoreType.DMA((3 * n,)), pltpu.SemaphoreType.DMA((3 * n,)),
                        pltpu.SemaphoreType.DMA((n,))],
    )(*arrs)


_HBM = pl.BlockSpec(memory_space=pltpu.HBM)
_SEM = pl.BlockSpec(memory_space=pltpu.SEMAPHORE)


def _copies_start(name, srcs, lands, make_copies, n_sems):
    n, m = len(srcs), len(lands)

    def body(*refs):
        ins = refs[:n + m]
        send_sems, recv_sems = refs[n + m], refs[n + m + 1]
        for cp in make_copies(ins[:n], ins[n:], send_sems, recv_sems):
            cp.start()
        refs[-1][...] = jnp.zeros_like(refs[-1])

    res = pl.pallas_call(
        body, name=name,
        out_shape=(pltpu.SemaphoreType.DMA((n_sems,)), pltpu.SemaphoreType.DMA((n_sems,)),
                   *[pltpu.HBM(a.shape, a.dtype) for a in (*srcs, *lands)], jax.ShapeDtypeStruct((8, LANE), F32)),
        in_specs=[_HBM] * (n + m),
        out_specs=(_SEM, _SEM, *[_HBM] * (n + m), pl.BlockSpec(memory_space=pltpu.VMEM)),
        input_output_aliases={i: 2 + i for i in range(n + m)},
        compiler_params=pltpu.CompilerParams(has_side_effects=pltpu.SideEffectType.DATAFLOW_SIDE_EFFECTING),
    )(*[pltpu.with_memory_space_constraint(a, pltpu.HBM) for a in (*srcs, *lands)])
    return res[0], res[1], res[2:2 + n], res[2 + n:2 + n + m], res[-1]


def _copies_wait(name, started, after, make_copies):
    send_sems, recv_sems, srcs, lands, _ = started
    n, m = len(srcs), len(lands)

    def body(*refs):
        ins = refs[:n + m]
        for cp in make_copies(ins[:n], ins[n:], refs[n + m], refs[n + m + 1]):
            cp.wait_send()
            cp.wait_recv()

    res = pl.pallas_call(
        body, name=name,
        out_shape=tuple(pltpu.HBM(a.shape, a.dtype) for a in (*srcs, *lands)),
        in_specs=[_HBM] * (n + m) + [_SEM, _SEM] + [_ANY] * len(after),
        out_specs=tuple([_HBM] * (n + m)),
        input_output_aliases={i: i for i in range(n + m)},
        compiler_params=pltpu.CompilerParams(has_side_effects=pltpu.SideEffectType.DATAFLOW_SIDE_EFFECTING),
    )(*srcs, *lands, send_sems, recv_sems, *after)
    return res[:n], res[n:]


def _gather_copies(srcs, lands, send_sems, recv_sems):
    x, y, c = _mesh_pos()
    me_i = 4 * x + 2 * y + c
    copies = []
    for rel in range(1, N_DEV):
        peer = (1 - x if rel & 4 else x, 1 - y if rel & 2 else y, 1 - c if rel & 1 else c)
        for a in range(len(srcs)):
            copies.append(pltpu.make_async_remote_copy(
                src_ref=srcs[a], dst_ref=lands[a].at[me_i], send_sem=send_sems.at[7 * a + rel - 1],
                recv_sem=recv_sems.at[7 * a + rel - 1], device_id=peer, device_id_type=MESH))
    return copies


def _sibling_copies(srcs, lands, send_sems, recv_sems):
    x, y, c = _mesh_pos()
    return [pltpu.make_async_remote_copy(
        src_ref=srcs[a].at[2 * k + (1 - c)], dst_ref=lands[a].at[k], send_sem=send_sems.at[4 * a + k],
        recv_sem=recv_sems.at[4 * a + k], device_id=(x, y, 1 - c), device_id_type=MESH)
        for a in range(len(srcs)) for k in range(4)]


def _chip_copies(srcs, lands, send_sems, recv_sems):
    x, y, c = _mesh_pos()
    my_chip = 2 * x + y
    copies = []
    for rel in range(1, 4):
        px = 1 - x if rel & 2 else x
        py = 1 - y if rel & 1 else y
        for a in range(len(srcs)):
            copies.append(pltpu.make_async_remote_copy(
                src_ref=srcs[a].at[2 * px + py], dst_ref=lands[a].at[my_chip], send_sem=send_sems.at[3 * a + rel - 1],
                recv_sem=recv_sems.at[3 * a + rel - 1], device_id=(px, py, c), device_id_type=MESH))
    return copies


def _sum_adam(name, parts, w, m, v, own=None):
    r, cdim = w.shape
    n_parts = parts.shape[0]
    tr = r if (r % 8 or r <= 256) else math.gcd(r, 256)
    bc1 = 1.0 - ADAM_B1 ** ADAM_STEP
    bc2 = 1.0 - ADAM_B2 ** ADAM_STEP
    extra = [] if own is None else [own]

    def body(p_ref, *refs):
        w_ref, m_ref, v_ref, g_ref, d_ref, nm_ref, nv_ref = refs[len(extra):]
        if own is None:
            part = lambda k: p_ref[k].astype(F32)
        else:
            my_chip = 2 * lax.axis_index("x") + lax.axis_index("y")
            part = lambda k: jnp.where(my_chip == k, refs[0][k], p_ref[k]).astype(F32)
        g = part(0)
        for k in range(1, n_parts):
            g = g + part(k)
        mn = ADAM_B1 * m_ref[...] + (1.0 - ADAM_B1) * g
        vn = ADAM_B2 * v_ref[...] + (1.0 - ADAM_B2) * (g * g)
        g_ref[...] = g
        nm_ref[...] = mn
        nv_ref[...] = vn
        d_ref[...] = -ADAM_LR * ((mn / bc1) / (jnp.sqrt(vn / bc2) + ADAM_EPS) + ADAM_WD * w_ref[...])

    blk = pl.BlockSpec((tr, cdim), lambda i: (i, 0))
    o = jax.ShapeDtypeStruct((r, cdim), F32)
    return pl.pallas_call(
        body, name=name, grid=(r // tr,),
        in_specs=[pl.BlockSpec((n_parts, tr, cdim), lambda i: (0, i, 0))] * (1 + len(extra)) + [blk, blk, blk],
        out_specs=(blk, blk, blk, blk), out_shape=(o, o, o, o),
        compiler_params=_params())(parts, *extra, w, m, v)


_WEIGHTS = ("c_ctx", "ada_w", "ada_b", "norm_g", "w_in", "b_in", "conv_w", "conv_b", "conv_ln_g", "conv_ln_b",
            "conv_proj", "decay_up_fwd", "decay_bias_fwd", "decay_up_bwd", "decay_bias_bwd", "gla_norm_g",
            "gla_proj", "w_out", "final_norm_g")


def _as2d(a):
    if a.ndim == 1:
        return a.reshape(1, -1)
    return a.reshape(-1, a.shape[-1])


def kernel(x, c, ctx, c_ctx, ada_w, ada_b, norm_g, w_in, b_in, conv_w, conv_b, conv_ln_g, conv_ln_b, conv_proj, decay_up_fwd, decay_bias_fwd, decay_up_bwd, decay_bias_bwd, gla_norm_g, gla_proj, w_out, final_norm_g, loss_target, m_c_ctx, m_ada_w, m_ada_b, m_norm_g, m_w_in, m_b_in, m_conv_w, m_conv_b, m_conv_ln_g, m_conv_ln_b, m_conv_proj, m_decay_up_fwd, m_decay_bias_fwd, m_decay_up_bwd, m_decay_bias_bwd, m_gla_norm_g, m_gla_proj, m_w_out, m_final_norm_g, v_c_ctx, v_ada_w, v_ada_b, v_norm_g, v_w_in, v_b_in, v_conv_w, v_conv_b, v_conv_ln_g, v_conv_ln_b, v_conv_proj, v_decay_up_fwd, v_decay_bias_fwd, v_decay_up_bwd, v_decay_bias_bwd, v_gla_norm_g, v_gla_proj, v_w_out, v_final_norm_g):
    env = dict(locals())
    wts = {k: env[k] for k in _WEIGHTS}
    d = x.shape[-1]
    r = decay_up_fwd.shape[1]
    dk_ = d // 2
    n_in = w_in.shape[-1] * N_DEV

    ds, dks = d // N_DEV, dk_ // N_DEV
    g_win, g_ada, conv_w8, g_up = _all_gather(
        [w_in[0].astype(BF16), ada_w[0].astype(BF16), conv_w[0],
         jnp.concatenate([decay_up_fwd[0], decay_up_bwd[0]], axis=1)])
    proj_own = [conv_proj[0].astype(BF16), gla_proj[0].astype(BF16), w_out[0].astype(BF16)]
    me_i = 4 * lax.axis_index("x") + 2 * lax.axis_index("y") + lax.axis_index("c")
    proj_lands = [lax.dynamic_update_slice(lax.empty((N_DEV,) + a.shape, a.dtype), a[None], (me_i, 0, 0))
                  for a in proj_own]
    proj_start = _copies_start("proj_gather_start", proj_own, proj_lands, _gather_copies, 7 * 3)

    def proj(after):
        _, lands = _copies_wait("proj_gather_wait", proj_start, (after,), _gather_copies)
        return [w.reshape(d, d) for w in lands]

    w_a, w_b = _unshard_w_in(g_win, d, r, after=(proj_start[4],))
    up_f = g_up[:, :, 0:dks].transpose(1, 0, 2).reshape(r, dk_)
    up_b = g_up[:, :, dks:].transpose(1, 0, 2).reshape(r, dk_)
    up2 = jnp.zeros((LANE, 2 * dk_), F32).at[0:r, 0:dk_].set(up_f).at[r:2 * r, dk_:].set(up_b)
    bias2 = jnp.concatenate([decay_bias_fwd, decay_bias_bwd], axis=1)
    b_a, b_b = _regroup(b_in, d, r)

    names = ("w_in", "conv_proj", "gla_proj", "w_out", "conv_w", "decay_up")
    comm = {}

    def on_grads(gr):
        d_up = jnp.concatenate([gr["up2"][0:r, 0:dk_].reshape(r, N_DEV, dks).transpose(1, 0, 2),
                                gr["up2"][r:2 * r, dk_:].reshape(r, N_DEV, dks).transpose(1, 0, 2)], axis=2)
        mine = [_reshard_w_in(gr["w_a1"], gr["w_a2"], gr["w_b"], d, r), gr["conv_proj"].reshape(N_DEV, ds, d),
                gr["gla_proj"].reshape(N_DEV, ds, d), gr["w_out"].reshape(N_DEV, ds, d), gr["conv_w8"], d_up]
        lands = [lax.empty((4,) + a.shape[1:], a.dtype) for a in mine]
        comm["sibling"] = _copies_start("grad_sibling_start", mine, lands, _sibling_copies, 4 * len(mine))
        return (comm["sibling"][4],)

    def on_du_a1(du_a1):
        mine, theirs = _copies_wait("grad_sibling_wait", comm["sibling"], (du_a1,), _sibling_copies)
        sums = [_pair_sum("pair_sum_" + nm, a, b) for nm, a, b in zip(names, mine, theirs)]
        lands = [lax.empty(a.shape, a.dtype) for a in sums]
        comm["chips"] = _copies_start("grad_chips_start", sums, lands, _chip_copies, 3 * len(sums))
        return (comm["chips"][4],)

    g = _local_step(x, c, ctx, loss_target, c_ctx, g_ada, ada_b, norm_g[0:1], w_a, b_a, w_b, b_b,
                    conv_w8, conv_b, conv_ln_g, conv_ln_b, up2, bias2, gla_norm_g, final_norm_g.reshape(1, d),
                    proj, on_grads, on_du_a1)

    (their_ada,) = _exchange_sibling([g["ada_w8"]])
    ada_sum = _pair_sum("pair_sum_ada_w", g["ada_w8"], their_ada)
    ada_start = _copies_start("ada_chips_start", [ada_sum], [lax.empty(ada_sum.shape, ada_sum.dtype)],
                              _chip_copies, 3)
    own, landed = _copies_wait("grad_chips_wait", comm["chips"], (ada_start[4],), _chip_copies)
    o_win, o_cp, o_gp, o_wo, o_cw, o_up = own
    x_win, x_cp, x_gp, x_wo, x_cw, x_up = landed

    (packs,) = _all_gather([_pack_small(g, x.shape[0], d, r)])
    row = lambda a: a.reshape(1, -1)
    sg, sd, sm, sv, loss = _small_adam(packs, [row(wts[k]) for k in _SMALL], [row(env["m_" + k]) for k in _SMALL],
                                       [row(env["v_" + k]) for k in _SMALL], d, r)
    out = {}
    for i, k in enumerate(_SMALL):
        for pre, arrs in (("grad_", sg), ("delta_", sd), ("new_m_", sm), ("new_v_", sv)):
            out[pre + k] = arrs[i].reshape(wts[k].shape)
    loss = loss.reshape(())

    def big(name, parts, wname, own=None):
        w2 = _as2d(wts[wname])
        res = _sum_adam(name, parts, w2, _as2d(env["m_" + wname]), _as2d(env["v_" + wname]), own)
        for pre, arr in zip(("grad_", "delta_", "new_m_", "new_v_"), res):
            out[pre + wname] = arr.reshape(wts[wname].shape)

    big("adam_w_in", x_win, "w_in", o_win)
    big("adam_conv_proj", x_cp, "conv_proj", o_cp)
    big("adam_gla_proj", x_gp, "gla_proj", o_gp)
    big("adam_w_out", x_wo, "w_out", o_wo)
    big("adam_conv_w", x_cw, "conv_w", o_cw)
    big("adam_up_f", x_up[:, :, 0:dks], "decay_up_fwd", o_up[:, :, 0:dks])
    big("adam_up_b", x_up[:, :, dks:], "decay_up_bwd", o_up[:, :, dks:])
    (o_ada,), (x_ada,) = _copies_wait("ada_chips_wait", ada_start, (out["grad_w_in"], out["grad_w_out"], out["grad_b_in"]),
                                      _chip_copies)
    big("adam_ada_w", x_ada, "ada_w", o_ada)

    return (loss, g["grad_x"], *[out["grad_" + k] for k in _WEIGHTS], *[out["delta_" + k] for k in _WEIGHTS],
            *[out["new_m_" + k] for k in _WEIGHTS], *[out["new_v_" + k] for k in _WEIGHTS])
```

```python
import functools
import math

import jax
import jax.numpy as jnp
from jax import lax
from jax.experimental import pallas as pl
from jax.experimental.pallas import tpu as pltpu

F32 = jnp.float32
BF16 = jnp.bfloat16
MESH = pl.DeviceIdType.MESH

N_DEV = 8
GRID_W = 64
CHUNK = 128
HEADS = 4
EPS = 1e-6
GATE_TAU = 16.0
LANE = 128
ADAM_LR, ADAM_B1, ADAM_B2, ADAM_EPS, ADAM_WD, ADAM_STEP = 0.001, 0.9, 0.999, 1e-08, 0.01, 10
VMEM_LIMIT = 60 * 1024 * 1024
_ANY = pl.BlockSpec(memory_space=pl.ANY)


def _params(**kw):
    return pltpu.CompilerParams(vmem_limit_bytes=VMEM_LIMIT, **kw)


def _tile(n, pref):
    t = (min(pref, n) // LANE) * LANE
    while t >= LANE:
        if n % t == 0:
            return t
        t -= LANE
    return n


def _mm(a, b):
    return jnp.dot(a.astype(BF16), b.astype(BF16), preferred_element_type=F32)


def _mm_nt(a, b):
    return lax.dot_general(a.astype(BF16), b.astype(BF16), (((1,), (1,)), ((), ())), preferred_element_type=F32)


def _mm_tn(a, b):
    return lax.dot_general(a.astype(BF16), b.astype(BF16), (((0,), (0,)), ((), ())), preferred_element_type=F32)


def _mm_tn_hi(a, b):
    return lax.dot_general(a, b, (((0,), (0,)), ((), ())), precision=lax.Precision.HIGHEST, preferred_element_type=F32)


def _sigmoid(x):
    return 0.5 * jnp.tanh(0.5 * x) + 0.5


def _dsilu(x, s):
    return s * (1.0 + x * (1.0 - s))


def _rowsel(table, idx, n):
    out = table[0:1, :]
    for r in range(1, n):
        out = jnp.where(idx == r, table[r:r + 1, :], out)
    return out


def _ada_fwd(cv, ada_w8, ada_b):
    n_sh, _, ws = ada_w8.shape

    def body(cv_ref, w_ref, b_ref, o_ref):
        c = cv_ref[...]
        sv = c * _sigmoid(c)
        for j in range(n_sh):
            cols = pl.ds(j * ws, ws)
            o_ref[:, cols] = _mm(sv, w_ref[j]) + b_ref[:, cols]

    return pl.pallas_call(body, name="ada_fwd", out_shape=jax.ShapeDtypeStruct((cv.shape[0], n_sh * ws), F32),
                          compiler_params=_params())(cv, ada_w8, ada_b)


def _ada_bwd(cv, ada_w8, dmod_ss, small, nb):
    n_sh, d, ws = ada_w8.shape

    def body(cv_ref, w_ref, dm_ref, sm_ref, dw_ref, db_ref, dc_ref):
        c = cv_ref[...]
        s = _sigmoid(c)
        sv = c * s
        dm = jnp.concatenate([dm_ref[:, 0:2 * d], sm_ref[8:16, :]], axis=1)
        db_ref[...] = jnp.sum(dm, axis=0, keepdims=True)
        dsv = None
        for j in range(n_sh):
            dmj = dm[:, j * ws:(j + 1) * ws]
            dw_ref[j] = _mm_tn_hi(sv, dmj).astype(dw_ref.dtype)
            part = _mm_nt(dmj, w_ref[j])
            dsv = part if dsv is None else dsv + part
        dc_ref[...] = dsv * _dsilu(c, s)

    return pl.pallas_call(
        body, name="ada_bwd",
        out_shape=(jax.ShapeDtypeStruct((n_sh, d, ws), BF16), jax.ShapeDtypeStruct((1, n_sh * ws), F32),
                   jax.ShapeDtypeStruct(cv.shape, F32)),
        compiler_params=_params())(cv, ada_w8, dmod_ss, small)


class _Tiles:
    def __init__(self, nb, s_len, c_len, tm, big):
        self.nb, self.tm, self.big = nb, tm, big
        self.lat, self.ctx = s_len // tm, c_len // tm
        self.pad = -(self.lat + self.ctx) % big
        self.per_ex = self.lat + self.ctx + self.pad
        self.n_all, self.n_lat = nb * self.per_ex, nb * self.lat
        self.rows_per_ex = self.per_ex * tm

    def is_lat(self, i):
        return i % self.per_ex < self.lat

    def is_pad(self, i):
        return i % self.per_ex >= self.lat + self.ctx

    def lat_of_all(self, i):
        return (i // self.per_ex) * self.lat + jnp.minimum(i % self.per_ex, self.lat - 1)

    def ctx_of_all(self, i):
        return (i // self.per_ex) * self.ctx + jnp.clip(i % self.per_ex - self.lat, 0, self.ctx - 1)

    def big_all_of_lat(self, t):
        lat_big = self.lat // self.big
        return (t // lat_big) * (self.per_ex // self.big) + t % lat_big


def _norm_fwd(x2, ctx2, mod, norm_g, tiles):
    tl, d = x2.shape
    tc = ctx2.shape[0]
    nb, tm = tiles.nb, tiles.tm

    def body(x_ref, c_ref, mod_ref, g_ref, u_ref):
        i = pl.program_id(0)
        lat = tiles.is_lat(i)
        xv = jnp.where(lat, x_ref[...], c_ref[...])
        row = jnp.where(lat, i // tiles.per_ex, nb)
        m = _rowsel(mod_ref[...], row, nb + 1)
        shift, scale = m[:, 0:d], m[:, d:2 * d]
        rstd = lax.rsqrt(jnp.mean(xv * xv, axis=-1, keepdims=True) + EPS)
        u = xv * rstd * g_ref[...] * (1.0 + scale) + shift
        u_ref[...] = jnp.where(tiles.is_pad(i), 0.0, u).astype(BF16)

    return pl.pallas_call(
        body, name="norm_fwd", grid=(tiles.n_all,),
        in_specs=[pl.BlockSpec((tm, d), lambda i: (tiles.lat_of_all(i), 0)),
                  pl.BlockSpec((tm, d), lambda i: (tiles.ctx_of_all(i), 0)),
                  pl.BlockSpec(mod.shape, lambda i: (0, 0)),
                  pl.BlockSpec((1, d), lambda i: (0, 0))],
        out_specs=pl.BlockSpec((tm, d), lambda i: (i, 0)),
        out_shape=jax.ShapeDtypeStruct((tiles.n_all * tm, d), BF16),
        compiler_params=_params())(x2, ctx2, mod, norm_g)


def _norm_bwd(x2, ctx2, mod, norm_g, du_lat, du_b, gx1, tiles):
    tl, d = x2.shape
    nb, tm = tiles.nb, tiles.tm
    nrow = mod.shape[0]
    n_lat_in = len(du_lat)

    def body(x_ref, c_ref, mod_ref, g_ref, *refs):
        dl_refs = refs[:n_lat_in]
        d3_ref, gx_ref, gxo_ref, dmod_ref, dg_ref = refs[n_lat_in:]
        i = pl.program_id(0)

        @pl.when(i == 0)
        def _():
            dmod_ref[...] = jnp.zeros_like(dmod_ref)
            dg_ref[...] = jnp.zeros_like(dg_ref)

        lat = tiles.is_lat(i)
        xv = jnp.where(lat, x_ref[...], c_ref[...])
        row = jnp.where(lat, i // tiles.per_ex, nb)
        m = _rowsel(mod_ref[...], row, nb + 1)
        scale = m[:, d:2 * d]
        g = g_ref[...]
        dl = dl_refs[0][...]
        for ref in dl_refs[1:]:
            dl = dl + ref[...]
        du = jnp.where(tiles.is_pad(i), 0.0, d3_ref[...] + jnp.where(lat, dl, 0.0))
        rstd = lax.rsqrt(jnp.mean(xv * xv, axis=-1, keepdims=True) + EPS)
        xh = xv * rstd
        dshift = jnp.sum(du, axis=0, keepdims=True)
        dscale = jnp.sum(du * xh * g, axis=0, keepdims=True)
        dxn = du * (1.0 + scale)
        dg_ref[...] += jnp.sum(dxn * xh, axis=0, keepdims=True)
        dxh = dxn * g
        dx = rstd * (dxh - xh * jnp.mean(dxh * xh, axis=-1, keepdims=True))

        @pl.when(lat)
        def _():
            gxo_ref[...] = dx + gx_ref[...]

        for r in range(nb + 1):
            dmod_ref[r:r + 1, 0:d] += jnp.where(row == r, dshift, 0.0)
            dmod_ref[r:r + 1, d:2 * d] += jnp.where(row == r, dscale, 0.0)

    lat_map = lambda i: (tiles.lat_of_all(i), 0)
    lat_spec = pl.BlockSpec((tm, d), lat_map)
    return pl.pallas_call(
        body, name="norm_bwd", grid=(tiles.n_all,),
        in_specs=[lat_spec,
                  pl.BlockSpec((tm, d), lambda i: (tiles.ctx_of_all(i), 0)),
                  pl.BlockSpec(mod.shape, lambda i: (0, 0)),
                  pl.BlockSpec((1, d), lambda i: (0, 0))]
                 + [lat_spec] * n_lat_in
                 + [pl.BlockSpec((tm, d), lambda i: (i, 0)), lat_spec],
        out_specs=(lat_spec,
                   pl.BlockSpec((nrow, 3 * d), lambda i: (0, 0)),
                   pl.BlockSpec((1, d), lambda i: (0, 0))),
        out_shape=(jax.ShapeDtypeStruct((tl, d), F32), jax.ShapeDtypeStruct((nrow, 3 * d), F32),
                   jax.ShapeDtypeStruct((1, d), F32)),
        compiler_params=_params())(x2, ctx2, mod, norm_g, *du_lat, du_b, gx1)


def _matmul_bias(name, u, w, b, rows, tm, tn, u_tile):
    d, n = w.shape

    def body(u_ref, w_ref, b_ref, o_ref):
        o_ref[...] = jnp.dot(u_ref[...], w_ref[...], preferred_element_type=F32) + b_ref[...]

    return pl.pallas_call(
        body, name=name, grid=(n // tn, rows // tm),
        in_specs=[pl.BlockSpec((tm, d), lambda j, i: (u_tile(i), 0)),
                  pl.BlockSpec((d, tn), lambda j, i: (0, j)),
                  pl.BlockSpec((1, tn), lambda j, i: (0, j))],
        out_specs=pl.BlockSpec((tm, tn), lambda j, i: (i, j)),
        out_shape=jax.ShapeDtypeStruct((rows, n), F32),
        compiler_params=_params())(u, w, b)


def _inproj_b(u, w_b, b_b, tm, dk_, dv_):
    t_all, d = u.shape
    nbw = w_b.shape[1]

    def body(u_ref, w_ref, b_ref, qk_ref, v_ref):
        full = jnp.dot(u_ref[...], w_ref[...], preferred_element_type=F32) + b_ref[...]
        qk_ref[:, 0:2 * dk_] = full[:, 0:2 * dk_]
        qk_ref[:, 2 * dk_:2 * dk_ + LANE] = full[:, 2 * dk_ + dv_:nbw]
        v_ref[...] = full[:, 2 * dk_:2 * dk_ + dv_].astype(BF16)

    return pl.pallas_call(
        body, name="inproj_b", grid=(t_all // tm,),
        in_specs=[pl.BlockSpec((tm, d), lambda i: (i, 0)), pl.BlockSpec((d, nbw), lambda i: (0, 0)),
                  pl.BlockSpec((1, nbw), lambda i: (0, 0))],
        out_specs=(pl.BlockSpec((tm, 2 * dk_ + LANE), lambda i: (i, 0)), pl.BlockSpec((tm, dv_), lambda i: (i, 0))),
        out_shape=(jax.ShapeDtypeStruct((t_all, 2 * dk_ + LANE), F32), jax.ShapeDtypeStruct((t_all, dv_), BF16)),
        compiler_params=_params())(u, w_b, b_b)


def _matmul_nt(name, a, w, koff, tm, tk, after=()):
    r, kc = a.shape
    d = w.shape[0]
    nk = kc // tk

    def body(a_ref, w_ref, *rest):
        o_ref = rest[len(after)]
        k = pl.program_id(1)
        p = lax.dot_general(a_ref[...], w_ref[...], (((1,), (1,)), ((), ())), preferred_element_type=F32)

        @pl.when(k == 0)
        def _():
            o_ref[...] = p

        @pl.when(k > 0)
        def _():
            o_ref[...] += p

    return pl.pallas_call(
        body, name=name, grid=(r // tm, nk),
        in_specs=[pl.BlockSpec((tm, tk), lambda i, k: (i, k)),
                  pl.BlockSpec((d, tk), lambda i, k: (0, koff + k))] + [_ANY] * len(after),
        out_specs=pl.BlockSpec((tm, d), lambda i, k: (i, 0)),
        out_shape=jax.ShapeDtypeStruct((r, d), F32),
        compiler_params=_params())(a, w, *after)


def _matmul_tn(name, a, b, rows, tk, tn):
    m = a.shape[1]
    n = b.shape[1]
    nk = rows // tk

    def body(a_ref, b_ref, o_ref, s_ref, acc_ref):
        k = pl.program_id(1)
        bv = b_ref[...]
        p = lax.dot_general(bv, a_ref[...], (((0,), (0,)), ((), ())), preferred_element_type=F32)
        cs = jnp.sum(bv.astype(F32), axis=0, keepdims=True)

        @pl.when(k == 0)
        def _():
            acc_ref[...] = p
            s_ref[...] = cs

        @pl.when(k > 0)
        def _():
            acc_ref[...] += p
            s_ref[...] += cs

        @pl.when(k == nk - 1)
        def _():
            o_ref[...] = acc_ref[...].astype(o_ref.dtype)

    return pl.pallas_call(
        body, name=name, grid=(n // tn, nk),
        in_specs=[pl.BlockSpec((tk, m), lambda j, k: (k, 0)),
                  pl.BlockSpec((tk, tn), lambda j, k: (k, j))],
        out_specs=(pl.BlockSpec((tn, m), lambda j, k: (j, 0)), pl.BlockSpec((1, tn), lambda j, k: (0, j))),
        out_shape=(jax.ShapeDtypeStruct((n, m), BF16), jax.ShapeDtypeStruct((1, n), F32)),
        scratch_shapes=[pltpu.VMEM((tn, m), F32)],
        compiler_params=_params())(a, b)


def _matmul_tn_whole(name, a3, b3, rows, tn, transposed):
    nb, _, m = a3.shape
    n = b3.shape[2]

    def body(a_ref, b_ref, o_ref, s_ref):
        p, cs = None, None
        for e in range(nb):
            bv = b_ref[e]
            lhs, rhs = (bv, a_ref[e]) if transposed else (a_ref[e], bv)
            pe = lax.dot_general(lhs, rhs, (((0,), (0,)), ((), ())), preferred_element_type=F32)
            ce = jnp.sum(bv.astype(F32), axis=0, keepdims=True)
            p, cs = (pe, ce) if p is None else (p + pe, cs + ce)
        o_ref[...] = p.astype(o_ref.dtype)
        s_ref[...] = cs

    o_spec, o_shape = ((pl.BlockSpec((tn, m), lambda j: (j, 0)), (n, m)) if transposed
                       else (pl.BlockSpec((m, tn), lambda j: (0, j)), (m, n)))
    return pl.pallas_call(
        body, name=name, grid=(n // tn,),
        in_specs=[pl.BlockSpec((nb, rows, m), lambda j: (0, 0, 0)),
                  pl.BlockSpec((nb, rows, tn), lambda j: (0, 0, j))],
        out_specs=(o_spec, pl.BlockSpec((1, tn), lambda j: (0, j))),
        out_shape=(jax.ShapeDtypeStruct(o_shape, BF16), jax.ShapeDtypeStruct((1, n), F32)),
        compiler_params=_params())(a3, b3)


def _conv_window(pad_ref, r, shift, ktaps, width, horizontal):
    if horizontal:
        return pad_ref[r, pl.ds(16 + shift, width), :]
    return pad_ref[r + ktaps // 2 + shift]


def _conv_row(pad_ref, w, r, ktaps, width, horizontal, flip):
    half = ktaps // 2
    acc = None
    for t in range(ktaps):
        win = _conv_window(pad_ref, r, (half - t) if flip else (t - half), ktaps, width, horizontal)
        term = win * w[t:t + 1, :]
        acc = term if acc is None else acc + term
    return acc


def _fill_padded(ref, val, rows, width, ktaps, horizontal):
    half_k = ktaps // 2
    cb = val.shape[-1]
    if horizontal:
        ref[:, 0:16, :] = jnp.zeros((rows, 16, cb), F32)
        ref[:, 16 + width:32 + width, :] = jnp.zeros((rows, 16, cb), F32)
        ref[:, 16:16 + width, :] = val
    else:
        ref[0:half_k, :, :] = jnp.zeros((half_k, width, cb), F32)
        ref[half_k + rows:2 * half_k + rows, :, :] = jnp.zeros((half_k, width, cb), F32)
        ref[half_k:half_k + rows, :, :] = val


def _conv_fwd(pa, conv_w8, conv_b, nb, s):
    nblk, ktaps, cb = conv_w8.shape
    d = nblk * cb
    rows, width = s // GRID_W, GRID_W
    half_k = ktaps // 2
    nh = nblk // 2

    def body(glu_ref, w_ref, b_ref, o_ref, ph_ref, pv_ref):
        j = pl.program_id(1)
        a0 = (glu_ref[:, 0:cb] * _sigmoid(glu_ref[:, cb:2 * cb])).reshape(rows, width, cb)
        w = w_ref[...]

        bias = b_ref[...]

        def run(pad_ref, horizontal):
            _fill_padded(pad_ref, a0, rows, width, ktaps, horizontal)

            def row(r, carry):
                at = pl.ds(pl.multiple_of(r * width, width), width)
                o_ref[at, :] = _conv_row(pad_ref, w, r, ktaps, width, horizontal, False) + bias
                return carry

            lax.fori_loop(0, rows, row, 0)

        @pl.when(j < nh)
        def _():
            run(ph_ref, True)

        @pl.when(j >= nh)
        def _():
            run(pv_ref, False)

    return pl.pallas_call(
        body, name="conv_fwd", grid=(nb, nblk),
        in_specs=[pl.BlockSpec((s, 2 * cb), lambda b, j: (b, j)),
                  pl.BlockSpec((None, ktaps, cb), lambda b, j: (j, 0, 0)),
                  pl.BlockSpec((1, cb), lambda b, j: (0, j))],
        out_specs=pl.BlockSpec((s, cb), lambda b, j: (b, j)),
        out_shape=jax.ShapeDtypeStruct((nb * s, d), F32),
        scratch_shapes=[pltpu.VMEM((rows, width + 32, cb), F32), pltpu.VMEM((rows + 2 * half_k, width, cb), F32)],
        compiler_params=_params())(pa, conv_w8, conv_b)


def _conv_bwd(pa, da1, conv_w8, nb, s):
    nblk, ktaps, cb = conv_w8.shape
    d = nblk * cb
    rows, width = s // GRID_W, GRID_W
    half_k = ktaps // 2
    nh = nblk // 2

    def body(glu_ref, da_ref, w_ref, dp_ref, dw_ref, db_ref, pha_ref, phd_ref, pva_ref, pvd_ref):
        j = pl.program_id(0)
        b = pl.program_id(1)
        a0 = (glu_ref[:, 0:cb] * _sigmoid(glu_ref[:, cb:2 * cb])).reshape(rows, width, cb)
        da1v = da_ref[...]
        d3 = da1v.reshape(rows, width, cb)
        w = w_ref[...]

        @pl.when(b == 0)
        def _():
            dw_ref[...] = jnp.zeros_like(dw_ref)
            db_ref[...] = jnp.zeros_like(db_ref)

        db_ref[...] += jnp.sum(da1v, axis=0, keepdims=True)

        def run(pa_ref, pd_ref, horizontal):
            _fill_padded(pa_ref, a0, rows, width, ktaps, horizontal)
            _fill_padded(pd_ref, d3, rows, width, ktaps, horizontal)

            def row(r, accs):
                at = pl.ds(pl.multiple_of(r * width, width), width)
                da0 = _conv_row(pd_ref, w, r, ktaps, width, horizontal, True)
                gv = glu_ref[at, 0:cb]
                sg = _sigmoid(glu_ref[at, cb:2 * cb])
                dp_ref[at, 0:cb] = (da0 * sg).astype(BF16)
                dp_ref[at, cb:2 * cb] = (da0 * gv * sg * (1.0 - sg)).astype(BF16)
                d_row = da_ref[at, :]
                out = []
                for t in range(ktaps):
                    prod = _conv_window(pa_ref, r, t - half_k, ktaps, width, horizontal) * d_row
                    out.append(accs[t] + jnp.sum(prod.reshape(width // 8, 8, cb), axis=0))
                return tuple(out)

            accs = lax.fori_loop(0, rows, row, tuple(jnp.zeros((8, cb), F32) for _ in range(ktaps)))
            for t in range(ktaps):
                dw_ref[t:t + 1, :] += jnp.sum(accs[t], axis=0, keepdims=True)

        @pl.when(j < nh)
        def _():
            run(pha_ref, phd_ref, True)

        @pl.when(j >= nh)
        def _():
            run(pva_ref, pvd_ref, False)

    return pl.pallas_call(
        body, name="conv_bwd", grid=(nblk, nb),
        in_specs=[pl.BlockSpec((s, 2 * cb), lambda j, b: (b, j)),
                  pl.BlockSpec((s, cb), lambda j, b: (b, j)),
                  pl.BlockSpec((None, ktaps, cb), lambda j, b: (j, 0, 0))],
        out_specs=(pl.BlockSpec((s, 2 * cb), lambda j, b: (b, j)),
                   pl.BlockSpec((None, ktaps, cb), lambda j, b: (j, 0, 0)),
                   pl.BlockSpec((1, cb), lambda j, b: (0, j))),
        out_shape=(jax.ShapeDtypeStruct((nb * s, 2 * d), BF16),
                   jax.ShapeDtypeStruct((nblk, ktaps, cb), F32), jax.ShapeDtypeStruct((1, d), F32)),
        scratch_shapes=[pltpu.VMEM((rows, width + 32, cb), F32), pltpu.VMEM((rows, width + 32, cb), F32),
                        pltpu.VMEM((rows + 2 * half_k, width, cb), F32),
                        pltpu.VMEM((rows + 2 * half_k, width, cb), F32)],
        compiler_params=_params())(pa, da1, conv_w8)


def _log_sigmoid(x):
    return jnp.minimum(x, 0.0) - jnp.log(1.0 + jnp.exp(-jnp.abs(x)))


def _decay_fwd(pb, up2, bias2, tm, lr_blk):
    t_all = pb.shape[0]
    n2 = up2.shape[1]

    def body(lr_ref, up_ref, b_ref, g_ref):
        logits = _mm(lr_ref[...], up_ref[...]) + b_ref[...]
        g_ref[...] = _log_sigmoid(logits) * (1.0 / GATE_TAU)

    return pl.pallas_call(
        body, name="decay_fwd", grid=(t_all // tm,),
        in_specs=[pl.BlockSpec((tm, LANE), lambda i: (i, lr_blk)),
                  pl.BlockSpec(up2.shape, lambda i: (0, 0)),
                  pl.BlockSpec((1, n2), lambda i: (0, 0))],
        out_specs=pl.BlockSpec((tm, n2), lambda i: (i, 0)),
        out_shape=jax.ShapeDtypeStruct((t_all, n2), F32),
        compiler_params=_params())(pb, up2, bias2)


def _decay_bwd(pb, up2, bias2, grads_f, grads_b, tiles, lr_blk, dk_, dv_):
    t_all = pb.shape[0]
    tm = tiles.tm
    n2 = up2.shape[1]
    nbw = 2 * dk_ + dv_ + LANE

    def body(lr_ref, up_ref, b_ref, dqf, dkf, dvf, dgf, dqb, dkb, dvb, dgb, dp_ref, dup_ref, dbias_ref):
        i = pl.program_id(0)
        pad = tiles.is_pad(i)
        live = lambda v: jnp.where(pad, 0.0, v)

        @pl.when(i == 0)
        def _():
            dup_ref[...] = jnp.zeros_like(dup_ref)
            dbias_ref[...] = jnp.zeros_like(dbias_ref)

        lr = lr_ref[...]
        up = up_ref[...]
        logits = _mm(lr, up) + b_ref[...]
        dg = live(jnp.concatenate([dgf[...], dgb[...]], axis=1))
        dlog = dg * (1.0 / GATE_TAU) * _sigmoid(-logits)
        dup_ref[...] += _mm_tn(lr, dlog)
        dbias_ref[...] += jnp.sum(dlog, axis=0, keepdims=True)
        both = lambda f, b: live(f[...].astype(F32) + b[...].astype(F32)).astype(BF16)
        dp_ref[:, 0:dk_] = both(dqf, dqb)
        dp_ref[:, dk_:2 * dk_] = both(dkf, dkb)
        dp_ref[:, 2 * dk_:2 * dk_ + dv_] = both(dvf, dvb)
        dp_ref[:, 2 * dk_ + dv_:nbw] = _mm_nt(dlog, up).astype(BF16)

    row = lambda w: pl.BlockSpec((tm, w), lambda i: (i, 0))
    return pl.pallas_call(
        body, name="decay_bwd", grid=(t_all // tm,),
        in_specs=[pl.BlockSpec((tm, LANE), lambda i: (i, lr_blk)),
                  pl.BlockSpec(up2.shape, lambda i: (0, 0)),
                  pl.BlockSpec((1, n2), lambda i: (0, 0)),
                  row(dk_), row(dk_), row(dv_), row(dk_), row(dk_), row(dk_), row(dv_), row(dk_)],
        out_specs=(row(nbw), pl.BlockSpec(up2.shape, lambda i: (0, 0)), pl.BlockSpec((1, n2), lambda i: (0, 0))),
        out_shape=(jax.ShapeDtypeStruct((t_all, nbw), BF16), jax.ShapeDtypeStruct(up2.shape, F32),
                   jax.ShapeDtypeStruct((1, n2), F32)),
        compiler_params=_params())(pb, up2, bias2, *grads_f, *grads_b)


def _scan_chunk(s, nl, nc, rev):
    if rev:
        return jnp.where(s < nc, nl + (nc - 1 - s), nl - 1 - (s - nc))
    return jnp.where(s < nc, nl + s, s - nc)


def _scan_lat_chunk(s, nl, nc, rev):
    first = nl - 1 if rev else 0
    return jnp.where(s < nc, first, _scan_chunk(s, nl, nc, rev))


def _tri_mm(m_bf, x):
    hi = x.astype(BF16)
    r1 = x - hi.astype(F32)
    mid = r1.astype(BF16)
    lo = (r1 - mid.astype(F32)).astype(BF16)
    dot = lambda p: jnp.dot(m_bf, p, preferred_element_type=F32)
    return dot(hi) + dot(mid) + dot(lo)


def _chunk_masks(c, rev):
    ii = lax.broadcasted_iota(jnp.int32, (c, c), 0)
    jj = lax.broadcasted_iota(jnp.int32, (c, c), 1)
    return ((ii <= jj), (ii >= jj)) if rev else ((ii >= jj), (ii <= jj))


def _chunk_terms(q, k, b, far, mid):
    bf, bm = b[far:far + 1, :], b[mid:mid + 1, :]
    e = jnp.exp(b)
    em = jnp.exp(b - bm)
    eim = jnp.exp(bm - b)
    ed = jnp.exp(bf - b)
    return dict(e=e, em=em, eim=eim, ed=ed, dec=jnp.exp(bf), qe=q * e, qem=q * em, kim=k * eim, kd=k * ed)


def _gla_fwd(pb3, pv3, g3, nb, s_len, c_len, dk_, dv_):
    c = CHUNK
    nl, nc = s_len // c, c_len // c
    ns = nl + nc
    hk, hv = dk_ // HEADS, dv_ // HEADS
    l_len = pb3.shape[1]
    scale = hk ** -0.5
    mid = c // 2

    def body(*refs):
        ins, outs, z_scr = refs[:8], refs[8:14], refs[14]
        s = pl.program_id(0)

        @pl.when(s == 0)
        def _():
            z_scr[...] = jnp.zeros_like(z_scr)

        qs = jnp.where(s >= nc, scale, 0.0)
        for di, rev in enumerate((False, True)):
            q_ref, k_ref, v_ref, g_ref = ins[4 * di:4 * di + 4]
            o_ref, zs_ref, b_ref = outs[3 * di:3 * di + 3]
            mask, _ = _chunk_masks(c, rev)
            m_bf = mask.astype(BF16)
            far = 0 if rev else c - 1
            for b in range(nb):
                bc = _tri_mm(m_bf, g_ref[b])
                b_ref[b] = bc
                for h in range(HEADS):
                    ks, vs = slice(h * hk, (h + 1) * hk), slice(h * hv, (h + 1) * hv)
                    zi = (di * nb + b) * HEADS + h
                    v = v_ref[b, :, vs]
                    t = _chunk_terms(q_ref[b, :, ks] * qs, k_ref[b, :, ks], bc[:, ks], far, mid)
                    a = jnp.where(mask, _mm_nt(t["qem"], t["kim"]), 0.0)
                    z = z_scr[zi]
                    zs_ref[0, b * HEADS + h] = z
                    o_ref[b, :, vs] = _mm(a, v) + _mm_nt(t["qe"], z)
                    z_scr[zi] = z * t["dec"] + _mm_tn(v, t["kd"])

    in_specs, out_specs, out_shape = [], [], []
    for di, rev in enumerate((False, True)):
        ch = functools.partial(_scan_chunk, nl=nl, nc=nc, rev=rev)
        lch = functools.partial(_scan_lat_chunk, nl=nl, nc=nc, rev=rev)
        in_specs += [pl.BlockSpec((nb, c, dk_), lambda s, ch=ch: (0, ch(s), 0)),
                     pl.BlockSpec((nb, c, dk_), lambda s, ch=ch: (0, ch(s), 1)),
                     pl.BlockSpec((nb, c, dv_), lambda s, ch=ch: (0, ch(s), 0)),
                     pl.BlockSpec((nb, c, dk_), lambda s, ch=ch, di=di: (0, ch(s), di))]
        out_specs += [pl.BlockSpec((nb, c, dv_), lambda s, lch=lch: (0, lch(s), 0)),
                      pl.BlockSpec((1, nb * HEADS, hv, hk), lambda s: (s, 0, 0, 0)),
                      pl.BlockSpec((nb, c, dk_), lambda s, ch=ch: (0, ch(s), 0))]
        out_shape += [jax.ShapeDtypeStruct((nb, s_len, dv_), F32),
                      jax.ShapeDtypeStruct((ns, nb * HEADS, hv, hk), F32),
                      jax.ShapeDtypeStruct((nb, l_len, dk_), F32)]
    return pl.pallas_call(
        body, name="gla_fwd", grid=(ns,), in_specs=in_specs, out_specs=tuple(out_specs), out_shape=tuple(out_shape),
        scratch_shapes=[pltpu.VMEM((2 * nb * HEADS, hv, hk), F32)],
        compiler_params=_params())(pb3, pb3, pv3, g3, pb3, pb3, pv3, g3)


def _gla_bwd(pb3, pv3, do3, fwd_saved, nb, s_len, c_len, dk_, dv_):
    c = CHUNK
    nl, nc = s_len // c, c_len // c
    ns = nl + nc
    hk, hv = dk_ // HEADS, dv_ // HEADS
    l_len = pb3.shape[1]
    scale = hk ** -0.5
    mid = c // 2
    zs_f, b_f, zs_b, b_b = fwd_saved

    def body(*refs):
        ins, outs, dz_scr = refs[:12], refs[12:20], refs[20]
        s = pl.program_id(0)
        step = ns - 1 - s

        @pl.when(s == 0)
        def _():
            dz_scr[...] = jnp.zeros_like(dz_scr)

        lat = step >= nc
        qs = jnp.where(lat, scale, 0.0)
        dmul = jnp.where(lat, 1.0, 0.0)
        for di, rev in enumerate((False, True)):
            q_ref, k_ref, v_ref, b_ref, do_ref, zs_ref = ins[6 * di:6 * di + 6]
            dq_ref, dk_ref, dv_ref, dg_ref = outs[4 * di:4 * di + 4]
            mask, mask_t = _chunk_masks(c, rev)
            mt_bf = mask_t.astype(BF16)
            far = 0 if rev else c - 1
            far_row = lax.broadcasted_iota(jnp.int32, (c, hk), 0) == far
            for b in range(nb):
                db_parts = []
                for h in range(HEADS):
                    ks, vs = slice(h * hk, (h + 1) * hk), slice(h * hv, (h + 1) * hv)
                    zi = (di * nb + b) * HEADS + h
                    v = v_ref[b, :, vs]
                    d_o = do_ref[b, :, vs] * dmul
                    t = _chunk_terms(q_ref[b, :, ks] * qs, k_ref[b, :, ks], b_ref[b, :, ks], far, mid)
                    qem, kim, qe, kd = t["qem"], t["kim"], t["qe"], t["kd"]
                    a_t = jnp.where(mask_t, _mm_nt(kim, qem), 0.0)
                    d_a = jnp.where(mask, _mm_nt(d_o, v), 0.0)
                    d_at = jnp.where(mask_t, _mm_nt(v, d_o), 0.0)
                    z = zs_ref[0, b * HEADS + h]
                    dzn = dz_scr[zi]
                    dv_ref[b, :, vs] = (_mm(a_t, d_o) + _mm_nt(kd, dzn)).astype(dv_ref.dtype)
                    dqem = _mm(d_a, kim)
                    dkim = _mm(d_at, qem)
                    dqe = _mm(d_o, z)
                    dkd = _mm(v, dzn)
                    ddec = jnp.sum(z * dzn, axis=0, keepdims=True)
                    dz_scr[zi] = dzn * t["dec"] + _mm_tn(d_o, qe)
                    dq_ref[b, :, ks] = ((dqem * t["em"] + dqe * t["e"]) * qs).astype(dq_ref.dtype)
                    dk_ref[b, :, ks] = (dkim * t["eim"] + dkd * t["ed"]).astype(dk_ref.dtype)
                    db = dqem * qem - dkim * kim + dqe * qe - dkd * kd
                    extra = jnp.sum(dkd * kd, axis=0, keepdims=True) + ddec * t["dec"]
                    db_parts.append(db + jnp.where(far_row, extra, 0.0))
                dg_ref[b] = _tri_mm(mt_bf, jnp.concatenate(db_parts, axis=1))

    in_specs, out_specs, out_shape, args = [], [], [], []
    for di, rev in enumerate((False, True)):
        ch = lambda s, rev=rev: _scan_chunk(ns - 1 - s, nl, nc, rev)
        lch = lambda s, rev=rev: _scan_lat_chunk(ns - 1 - s, nl, nc, rev)
        in_specs += [pl.BlockSpec((nb, c, dk_), lambda s, ch=ch: (0, ch(s), 0)),
                     pl.BlockSpec((nb, c, dk_), lambda s, ch=ch: (0, ch(s), 1)),
                     pl.BlockSpec((nb, c, dv_), lambda s, ch=ch: (0, ch(s), 0)),
                     pl.BlockSpec((nb, c, dk_), lambda s, ch=ch: (0, ch(s), 0)),
                     pl.BlockSpec((nb, c, dv_), lambda s, lch=lch: (0, lch(s), 0)),
                     pl.BlockSpec((1, nb * HEADS, hv, hk), lambda s: (ns - 1 - s, 0, 0, 0))]
        args += [pb3, pb3, pv3, (b_b if rev else b_f), do3, (zs_b if rev else zs_f)]
        for w, dt in ((dk_, BF16), (dk_, BF16), (dv_, BF16), (dk_, F32)):
            out_specs.append(pl.BlockSpec((nb, c, w), lambda s, ch=ch: (0, ch(s), 0)))
            out_shape.append(jax.ShapeDtypeStruct((nb, l_len, w), dt))
    return pl.pallas_call(
        body, name="gla_bwd", grid=(ns,), in_specs=in_specs, out_specs=tuple(out_specs), out_shape=tuple(out_shape),
        scratch_shapes=[pltpu.VMEM((2 * nb * HEADS, hv, hk), F32)],
        compiler_params=_params())(*args)


def _tail(a1, pa, o_f, o_b, x2, tgt, mod, wc, wg, wo, ln_g, ln_b, gn_t, fg, nb, tm, n_split):
    tl, d = x2.shape
    nt = tl // tm
    per_ex = nt // nb
    hv = d // HEADS
    nrow = mod.shape[0]

    def part(shared, a1_ref, z_ref, r_ref, mc_ref, mg_ref, of_ref, ob_ref, x_ref, t_ref,
             dp_ref, da1_ref, do_ref, gx_ref, mrg_ref, dmo_ref, yci_ref, dyc_ref, ogi_ref, dyg_ref, sm_ref):
        bidx, gate, lng, lnb, fgv, gn, wc_, wg_, wo_ = shared

        a1v = a1_ref[...]
        mu = jnp.mean(a1v, axis=-1, keepdims=True)
        xc = a1v - mu
        rs = lax.rsqrt(jnp.mean(xc * xc, axis=-1, keepdims=True) + EPS)
        xh = xc * rs
        a2 = xh * lng + lnb
        s2 = _sigmoid(a2)
        a3 = a2 * s2
        zv = z_ref[...]
        sz = _sigmoid(zv)
        siluz = zv * sz
        ycin = a3 * siluz
        yconv = _mm(ycin, wc_)

        o = of_ref[...] + ob_ref[...]
        ohat_parts, rn_parts = [], []
        for h in range(HEADS):
            oh = o[:, h * hv:(h + 1) * hv]
            rn = lax.rsqrt(jnp.mean(oh * oh, axis=-1, keepdims=True) + EPS)
            ohat_parts.append(oh * rn)
            rn_parts.append(rn)
        ohat = jnp.concatenate(ohat_parts, axis=1)
        on = ohat * gn
        rv = r_ref[...]
        sr = _sigmoid(rv)
        silur = rv * sr
        ogin = on * silur
        ygla = _mm(ogin, wg_)

        sc = _sigmoid(mc_ref[...])
        sg = _sigmoid(mg_ref[...])
        merged = sc * yconv + sg * ygla
        mo = _mm(merged, wo_)
        hn = x_ref[...] + gate * mo
        rf = lax.rsqrt(jnp.mean(hn * hn, axis=-1, keepdims=True) + EPS)
        yh = hn * rf
        err = yh * fgv - t_ref[...]
        loss_part = 0.5 * jnp.sum(err * err) * (1.0 / d)

        dy = err * (1.0 / d)
        dfg = jnp.sum(dy * yh, axis=0, keepdims=True)
        dyh = dy * fgv
        dhn = rf * (dyh - yh * jnp.mean(dyh * yh, axis=-1, keepdims=True))
        gx_ref[...] = dhn
        dgate = jnp.sum(dhn * mo, axis=0, keepdims=True)
        dmo = gate * dhn
        dmerged = _mm_nt(dmo, wo_)
        dyconv = dmerged * sc
        dygla = dmerged * sg
        dp_ref[:, 2 * d:3 * d] = (dmerged * yconv * sc * (1.0 - sc)).astype(BF16)
        dp_ref[:, 3 * d:4 * d] = (dmerged * ygla * sg * (1.0 - sg)).astype(BF16)
        dycin = _mm_nt(dyconv, wc_)
        dogin = _mm_nt(dygla, wg_)
        mrg_ref[...] = merged.astype(BF16)
        dmo_ref[...] = dmo.astype(BF16)
        yci_ref[...] = ycin.astype(BF16)
        dyc_ref[...] = dyconv.astype(BF16)
        ogi_ref[...] = ogin.astype(BF16)
        dyg_ref[...] = dygla.astype(BF16)

        da3 = dycin * siluz
        dp_ref[:, 0:d] = (dycin * a3 * _dsilu(zv, sz)).astype(BF16)
        da2 = da3 * _dsilu(a2, s2)
        dlng = jnp.sum(da2 * xh, axis=0, keepdims=True)
        dlnb = jnp.sum(da2, axis=0, keepdims=True)
        dxh = da2 * lng
        da1_ref[...] = rs * (dxh - jnp.mean(dxh, axis=-1, keepdims=True)
                             - xh * jnp.mean(dxh * xh, axis=-1, keepdims=True))

        don = dogin * silur
        dp_ref[:, d:2 * d] = (dogin * on * _dsilu(rv, sr)).astype(BF16)
        dgn = jnp.sum(don * ohat, axis=0, keepdims=True)
        dyn = don * gn
        for h in range(HEADS):
            vs = slice(h * hv, (h + 1) * hv)
            oh_hat = ohat_parts[h]
            dh = dyn[:, vs]
            do_ref[:, vs] = (rn_parts[h] * (dh - oh_hat * jnp.mean(dh * oh_hat, axis=-1, keepdims=True))
                             ).astype(BF16)

        sm_ref[0:1, :] += dfg
        sm_ref[1:2, :] += dlng
        sm_ref[2:3, :] += dlnb
        sm_ref[3:4, :] += dgn
        sm_ref[4:5, :] += jnp.zeros((1, d), F32) + loss_part
        for b in range(nb):
            sm_ref[8 + b:9 + b, :] += jnp.where(bidx == b, dgate, 0.0)

    def body(*refs):
        mod_ref, wc_ref, wg_ref, wo_ref, lng_ref, lnb_ref, gn_ref, fg_ref = refs[9:17]
        sm_ref = refs[27]
        i = pl.program_id(0)

        @pl.when(i == 0)
        def _():
            sm_ref[...] = jnp.zeros_like(sm_ref)

        bidx = i // per_ex
        shared = (bidx, _rowsel(mod_ref[...], bidx, nb)[:, 2 * d:3 * d], lng_ref[...], lnb_ref[...], fg_ref[...],
                  jnp.concatenate([gn_ref[...]] * HEADS, axis=1), wc_ref[...], wg_ref[...], wo_ref[...])
        rows_per = tm // n_split
        for p in range(n_split):
            rows = pl.ds(p * rows_per, rows_per)
            part(shared, *[r.at[rows] for r in refs[0:9]], *[r.at[rows] for r in refs[17:27]], sm_ref)

    row = pl.BlockSpec((tm, d), lambda i: (i, 0))
    pcol = lambda blk: pl.BlockSpec((tm, d), lambda i: (i, blk))
    full = lambda arr: pl.BlockSpec(arr.shape, lambda i: (0,) * arr.ndim)
    bfo = jax.ShapeDtypeStruct((tl, d), BF16)
    f32o = jax.ShapeDtypeStruct((tl, d), F32)
    return pl.pallas_call(
        body, name="tail", grid=(nt,),
        in_specs=[row, pcol(2), pcol(3), pcol(4), pcol(5), row, row, row, row, full(mod), full(wc), full(wg),
                  full(wo), full(ln_g), full(ln_b), full(gn_t), full(fg)],
        out_specs=(pl.BlockSpec((tm, 4 * d), lambda i: (i, 0)), row, row, row, row, row, row, row, row, row,
                   pl.BlockSpec((16, d), lambda i: (0, 0))),
        out_shape=(jax.ShapeDtypeStruct((tl, 4 * d), BF16), f32o, bfo, f32o, bfo, bfo, bfo, bfo, bfo, bfo,
                   jax.ShapeDtypeStruct((16, d), F32)),
        compiler_params=_params())(a1, pa, pa, pa, pa, o_f, o_b, x2, tgt, mod, wc, wg, wo, ln_g, ln_b, gn_t, fg)


def _local_step(x, c, ctx, tgt, c_ctx, ada_w8, ada_b, norm_g, w_a, b_a, w_b, b_b, conv_w8, conv_b, ln_g, ln_b,
                up2, bias2, gla_norm_g, final_norm_g, proj, on_grads=None, on_du_a1=None):
    nb, s_len, d = x.shape
    c_len = ctx.shape[1]
    dk_, dv_ = d // 2, d
    tl, tc = nb * s_len, nb * c_len
    nbw = 2 * dk_ + dv_ + LANE
    tm = math.gcd(256, c_len)
    tiles = _Tiles(nb, s_len, c_len, tm, 2)
    tmm = tiles.big * tm
    l_len = tiles.rows_per_ex
    t_all = nb * l_len
    x2, ctx2, tgt2 = x.reshape(tl, d), ctx.reshape(tc, d), tgt.reshape(tl, d)

    cv = jnp.zeros((8, d), F32).at[0:nb].set(c).at[nb].set(c_ctx.reshape(d))
    mod = _ada_fwd(cv, ada_w8, ada_b)
    u = _norm_fwd(x2, ctx2, mod, norm_g, tiles)
    pa = _matmul_bias("inproj_a", u, w_a, b_a, tl, tmm, _tile(6 * d, 3072), tiles.big_all_of_lat)
    pb, pv = _inproj_b(u, w_b, b_b, tmm, dk_, dv_)

    a1 = _conv_fwd(pa, conv_w8, conv_b, nb, s_len)
    lr_blk = (2 * dk_) // LANE
    g_all = _decay_fwd(pb, up2, bias2, tm, lr_blk)
    pb3, pv3 = pb.reshape(nb, l_len, 2 * dk_ + LANE), pv.reshape(nb, l_len, dv_)
    o_f, zs_f, b_f, o_b, zs_b, b_b2 = _gla_fwd(pb3, pv3, g_all.reshape(nb, l_len, 2 * dk_), nb, s_len, c_len,
                                               dk_, dv_)

    conv_proj, gla_proj, w_out = proj(a1) if callable(proj) else proj
    tt = math.gcd(256, s_len)
    (dp_a2, da1, d_o, gx1, merged, dmo, ycin, dyconv, ogin, dygla, small) = _tail(
        a1, pa, o_f.reshape(tl, dv_), o_b.reshape(tl, dv_), x2, tgt2, mod, conv_proj, gla_proj, w_out, ln_g, ln_b,
        gla_norm_g, final_norm_g, nb, tt, 2)

    lat3 = lambda a: a.reshape(nb, s_len, a.shape[-1])
    tnw = _tile(d, 1024)
    tnp = _tile(d, 512)
    d_w_out, _ = _matmul_tn_whole("dw_out", lat3(merged), lat3(dmo), s_len, tnp, False)
    d_conv_proj, _ = _matmul_tn_whole("dw_conv_proj", lat3(ycin), lat3(dyconv), s_len, tnp, False)
    d_gla_proj, _ = _matmul_tn_whole("dw_gla_proj", lat3(ogin), lat3(dygla), s_len, tnp, False)

    dp_a1, d_conv_w8, d_conv_b = _conv_bwd(pa, da1, conv_w8, nb, s_len)
    gl = _gla_bwd(pb3, pv3, d_o.reshape(nb, s_len, dv_), (zs_f, b_f, zs_b, b_b2), nb, s_len, c_len, dk_, dv_)
    gl = [g_.reshape(t_all, g_.shape[-1]) for g_ in gl]
    dp_b, d_up2, d_bias2 = _decay_bwd(pb, up2, bias2, gl[0:4], gl[4:8], tiles, lr_blk, dk_, dv_)

    u3 = u.reshape(nb, l_len, d)
    dw_a1, db_a1 = _matmul_tn_whole("dw_a1", u3, lat3(dp_a1), s_len, tnw, True)
    dw_a2, db_a2 = _matmul_tn_whole("dw_a2", u3, lat3(dp_a2), s_len, tnw, True)
    dw_b, db_b = _matmul_tn("dw_b", u, dp_b, t_all, tmm, nbw)
    grads = dict(w_a1=dw_a1, w_a2=dw_a2, w_b=dw_b, conv_w8=d_conv_w8, conv_proj=d_conv_proj, up2=d_up2,
                 gla_proj=d_gla_proj, w_out=d_w_out)

    tka = _tile(2 * d, 2048)
    du_a1 = _matmul_nt("du_a1", dp_a1, w_a, 0, tmm, tka, after=on_grads(grads) if on_grads else ())
    du_a2 = _matmul_nt("du_a2", dp_a2, w_a, (2 * d) // tka, tmm, tka, after=on_du_a1(du_a1) if on_du_a1 else ())
    du_b = _matmul_nt("du_b", dp_b, w_b, 0, tmm, nbw)
    grad_x2, dmod_ss, d_norm_g = _norm_bwd(x2, ctx2, mod, norm_g, [du_a1, du_a2], du_b, gx1, tiles)
    d_ada_w8, d_ada_b, d_cv = _ada_bwd(cv, ada_w8, dmod_ss, small, nb)

    return dict(
        grads, grad_x=grad_x2.reshape(nb, s_len, d), small=small, cv=d_cv, ada_w8=d_ada_w8, ada_b=d_ada_b,
        norm_g=d_norm_g, b_a1=db_a1, b_a2=db_a2, b_b=db_b, conv_b=d_conv_b, bias2=d_bias2)


def _regroup_pieces(d, r, wshard):
    cb = d // N_DEV
    segs = []
    for j in range(N_DEV):
        segs.append((j * cb, cb, 0, 2 * j * cb))
    for j in range(N_DEV):
        segs.append((d + j * cb, cb, 0, (2 * j + 1) * cb))
    segs += [(2 * d, d, 0, 2 * d), (3 * d, 2 * d + 2 * r, 1, 0), (5 * d + 2 * r, 3 * d, 0, 3 * d)]
    pieces = []
    for o0, w, dst, d0 in segs:
        lo = o0
        while lo < o0 + w:
            j = lo // wshard
            hi = min(o0 + w, (j + 1) * wshard)
            pieces.append((j, lo - j * wshard, hi - lo, dst, d0 + lo - o0))
            lo = hi
    return pieces


def _regroup(o, d, r):
    n_in = 8 * d + 2 * r
    parts = ([], [])
    for _, s0, n, dst, _ in sorted(_regroup_pieces(d, r, n_in), key=lambda p: (p[3], p[4])):
        parts[dst].append(o[..., s0:s0 + n])
    pad = jnp.zeros(o.shape[:-1] + (LANE - 2 * r,), o.dtype)
    return jnp.concatenate(parts[0], axis=-1), jnp.concatenate(parts[1] + [pad], axis=-1)


def _unshard_w_in(g_win, d, r, after=()):
    n_sh, _, ws = g_win.shape
    nbw = 2 * d + LANE
    pieces = _regroup_pieces(d, r, ws)
    tr = math.gcd(d, 256)

    def body(g_ref, *rest):
        a_ref, b_ref = rest[len(after):]
        dsts = (a_ref, b_ref)
        for j, s0, n, dst, d0 in pieces:
            dsts[dst][:, pl.ds(d0, n)] = g_ref[j, :, pl.ds(s0, n)]
        b_ref[:, pl.ds(2 * d + 2 * r, LANE - 2 * r)] = jnp.zeros((tr, LANE - 2 * r), b_ref.dtype)

    return pl.pallas_call(
        body, name="unshard_w_in", grid=(d // tr,),
        in_specs=[pl.BlockSpec((n_sh, tr, ws), lambda i: (0, i, 0))] + [_ANY] * len(after),
        out_specs=(pl.BlockSpec((tr, 6 * d), lambda i: (i, 0)), pl.BlockSpec((tr, nbw), lambda i: (i, 0))),
        out_shape=(jax.ShapeDtypeStruct((d, 6 * d), g_win.dtype), jax.ShapeDtypeStruct((d, nbw), g_win.dtype)),
        compiler_params=_params())(g_win, *after)


def _reshard_w_in(dwt_a1, dwt_a2, dwt_b, d, r):
    ws = (8 * d + 2 * r) // N_DEV
    pieces = _regroup_pieces(d, r, ws)
    tc = math.gcd(d, 256)

    def body(a1_ref, a2_ref, b_ref, o_ref):
        for j, s0, n, dst, d0 in pieces:
            if dst == 1:
                src = b_ref[pl.ds(d0, n), :]
            elif d0 < 2 * d:
                src = a1_ref[pl.ds(d0, n), :]
            else:
                src = a2_ref[pl.ds(d0 - 2 * d, n), :]
            o_ref[j, pl.ds(s0, n), :] = src

    col = lambda h: pl.BlockSpec((h, tc), lambda i: (0, i))
    return pl.pallas_call(
        body, name="reshard_w_in", grid=(d // tc,),
        in_specs=[col(2 * d), col(4 * d), col(2 * d + LANE)],
        out_specs=pl.BlockSpec((N_DEV, ws, tc), lambda i: (0, 0, i)),
        out_shape=jax.ShapeDtypeStruct((N_DEV, ws, d), dwt_b.dtype),
        compiler_params=_params())(dwt_a1, dwt_a2, dwt_b)


_SMALL = ("c_ctx", "ada_b", "norm_g", "b_in", "conv_b", "conv_ln_g", "conv_ln_b", "decay_bias_fwd",
          "decay_bias_bwd", "gla_norm_g", "final_norm_g")


def _small_layout(d, r):
    sizes = dict(c_ctx=d, ada_b=3 * d, norm_g=d, b_in=8 * d + 2 * r, conv_b=d, conv_ln_g=d, conv_ln_b=d,
                 decay_bias_fwd=d // 2, decay_bias_bwd=d // 2, gla_norm_g=d // HEADS, final_norm_g=d, loss=1)
    table, off = {}, 0
    for name in _SMALL + ("loss",):
        table[name] = (off, sizes[name])
        off += -(-sizes[name] // LANE) * LANE
    return table, off


def _pack_small(g, nb, d, r):
    table, width = _small_layout(d, r)
    hv = d // HEADS
    pieces = _regroup_pieces(d, r, 8 * d + 2 * r)
    names = ("small", "cv", "ada_b", "norm_g", "b_a1", "b_a2", "b_b", "conv_b", "bias2")

    def body(sm, cv, ab, ng, ba1, ba2, bb, cvb, b2, o_ref):
        o_ref[...] = jnp.zeros_like(o_ref)

        def put(name, val):
            off, n = table[name]
            o_ref[:, pl.ds(off, n)] = val

        put("c_ctx", cv[nb:nb + 1, :])
        put("ada_b", ab[...])
        put("norm_g", ng[...])
        off_b = table["b_in"][0]
        for _, s0, n, dst, d0 in pieces:
            if dst == 1:
                src = bb[:, pl.ds(d0, n)]
            elif d0 < 2 * d:
                src = ba1[:, pl.ds(d0, n)]
            else:
                src = ba2[:, pl.ds(d0 - 2 * d, n)]
            o_ref[:, pl.ds(off_b + s0, n)] = src
        put("conv_b", cvb[...])
        put("conv_ln_g", sm[1:2, :])
        put("conv_ln_b", sm[2:3, :])
        put("decay_bias_fwd", b2[:, 0:d // 2])
        put("decay_bias_bwd", b2[:, d // 2:d])
        gn = sm[3:4, 0:hv]
        for h in range(1, HEADS):
            gn = gn + sm[3:4, h * hv:(h + 1) * hv]
        put("gla_norm_g", gn)
        put("final_norm_g", sm[0:1, :])
        put("loss", sm[4:5, 0:1])

    return pl.pallas_call(body, name="pack_small", out_shape=jax.ShapeDtypeStruct((1, width), F32),
                          compiler_params=_params())(*[g[k] for k in names])


def _small_adam(parts, ws, ms, vs, d, r):
    table, width = _small_layout(d, r)
    n_parts = parts.shape[0]
    k = len(_SMALL)
    bc1 = 1.0 - ADAM_B1 ** ADAM_STEP
    bc2 = 1.0 - ADAM_B2 ** ADAM_STEP

    def body(p_ref, *refs):
        w_refs, m_refs, v_refs = refs[0:k], refs[k:2 * k], refs[2 * k:3 * k]
        outs = refs[3 * k:]
        tot = p_ref[0]
        for i in range(1, n_parts):
            tot = tot + p_ref[i]
        for i, name in enumerate(_SMALL):
            off, n = table[name]
            g = tot[:, off:off + n]
            mn = ADAM_B1 * m_refs[i][...] + (1.0 - ADAM_B1) * g
            vn = ADAM_B2 * v_refs[i][...] + (1.0 - ADAM_B2) * (g * g)
            outs[i][...] = g
            outs[k + i][...] = -ADAM_LR * ((mn / bc1) / (jnp.sqrt(vn / bc2) + ADAM_EPS) + ADAM_WD * w_refs[i][...])
            outs[2 * k + i][...] = mn
            outs[3 * k + i][...] = vn
        off, _ = table["loss"]
        outs[4 * k][...] = tot[:, off:off + 1]

    shapes = [jax.ShapeDtypeStruct(w.shape, F32) for w in ws]
    res = pl.pallas_call(body, name="small_adam", out_shape=tuple(shapes * 4 + [jax.ShapeDtypeStruct((1, 1), F32)]),
                         compiler_params=_params())(parts, *ws, *ms, *vs)
    return res[0:k], res[k:2 * k], res[2 * k:3 * k], res[3 * k:4 * k], res[4 * k]


def _mesh_pos():
    return lax.axis_index("x"), lax.axis_index("y"), lax.axis_index("c")


def _all_gather(arrs):
    n = len(arrs)
    ns = 9
    split = [a.ndim == 2 and a.shape[0] % 32 == 0 for a in arrs]

    def body(*refs):
        ins, outs = refs[:n], refs[n:2 * n]
        send_sems, recv_sems, local_sems = refs[2 * n:]
        x, y, c = _mesh_pos()
        me, sibling = (x, y, c), (x, y, 1 - c)
        xn, yn, dg = (1 - x, y, c), (x, 1 - y, c), (1 - x, 1 - y, c)
        other = lambda pos: (pos[0], pos[1], 1 - c)

        def slot(a, pos, half):
            ref = outs[a].at[4 * pos[0] + 2 * pos[1] + pos[2]]
            if half is None:
                return ref
            rows = arrs[a].shape[0] // 2
            return ref.at[pl.ds(half * rows, rows)]

        def copy(a, k, block, to, src=None, half=None):
            dst = slot(a, block, half)
            return pltpu.make_async_remote_copy(
                src_ref=dst if src is None else src, dst_ref=dst,
                send_sem=send_sems.at[ns * a + k], recv_sem=recv_sems.at[ns * a + k],
                device_id=to, device_id_type=MESH)

        h0 = lambda a: 0 if split[a] else None
        mine = [pltpu.make_async_copy(ins[a], slot(a, me, None), local_sems.at[a]) for a in range(n)]
        for cp in mine:
            cp.start()
        sent = []
        for a in range(n):
            sent += [copy(a, 0, me, sibling, src=ins[a]), copy(a, 1, me, xn, src=ins[a]),
                     copy(a, 2, me, yn, src=ins[a])]
        for cp in sent:
            cp.start()

        def pass_on(cp):
            cp.start()
            sent.append(cp)

        for a in range(n):
            copy(a, 1, xn, me).wait_recv()
            pass_on(copy(a, 3, xn, sibling))
            pass_on(copy(a, 4, xn, yn, half=h0(a)))
        for a in range(n):
            copy(a, 2, yn, me).wait_recv()
            pass_on(copy(a, 5, yn, sibling))
            if split[a]:
                pass_on(copy(a, 6, yn, xn, half=1))
        for a in range(n):
            copy(a, 4, dg, me, half=h0(a)).wait_recv()
            pass_on(copy(a, 7, dg, sibling, half=h0(a)))
            if split[a]:
                copy(a, 6, dg, me, half=1).wait_recv()
                pass_on(copy(a, 8, dg, sibling, half=1))
        for a in range(n):
            copy(a, 0, sibling, me).wait_recv()
            copy(a, 3, other(xn), me).wait_recv()
            copy(a, 5, other(yn), me).wait_recv()
            copy(a, 7, other(dg), me, half=h0(a)).wait_recv()
            if split[a]:
                copy(a, 8, other(dg), me, half=1).wait_recv()
        for cp in sent:
            cp.wait_send()
        for cp in mine:
            cp.wait()

    return pl.pallas_call(
        body, name="all_gather",
        out_shape=tuple(jax.ShapeDtypeStruct((N_DEV,) + a.shape, a.dtype) for a in arrs),
        in_specs=[_ANY] * n, out_specs=tuple([_ANY] * n),
        scratch_shapes=[pltpu.SemaphoreType.DMA((ns * n,)), pltpu.SemaphoreType.DMA((ns * n,)),
                        pltpu.SemaphoreType.DMA((n,))],
    )(*arrs)


def _exchange_sibling(arrs):
    n = len(arrs)

    def body(*refs):
        ins, outs = refs[:n], refs[n:2 * n]
        send_sems, recv_sems = refs[2 * n:]
        x, y, c = _mesh_pos()
        copies = [pltpu.make_async_remote_copy(
            src_ref=ins[a].at[2 * k + (1 - c)], dst_ref=outs[a].at[k],
            send_sem=send_sems.at[4 * a + k], recv_sem=recv_sems.at[4 * a + k],
            device_id=(x, y, 1 - c), device_id_type=MESH) for a in range(n) for k in range(4)]
        for cp in copies:
            cp.start()
        for cp in copies:
            cp.wait_recv()
        for cp in copies:
            cp.wait_send()

    return pl.pallas_call(
        body, name="grad_exchange_sibling",
        out_shape=tuple(jax.ShapeDtypeStruct((4,) + a.shape[1:], a.dtype) for a in arrs),
        in_specs=[_ANY] * n, out_specs=tuple([_ANY] * n),
        scratch_shapes=[pltpu.SemaphoreType.DMA((4 * n,)), pltpu.SemaphoreType.DMA((4 * n,))],
    )(*arrs)


def _elementwise_tile(r, cdim):
    if r % 8 == 0 and r > 256:
        return math.gcd(r, 256), cdim
    if r > 256 and cdim % 256 == 0:
        return r, 256
    return r, cdim


def _pair_sum(name, mine, theirs):
    _, r, cdim = mine.shape
    tr, tc = _elementwise_tile(r, cdim)

    def body(m_ref, t_ref, o_ref):
        c = lax.axis_index("c")
        own = jnp.where(c == 0, m_ref[:, 0].astype(F32), m_ref[:, 1].astype(F32))
        o_ref[...] = (own + t_ref[...].astype(F32)).astype(o_ref.dtype)

    return pl.pallas_call(
        body, name=name, grid=(r // tr, cdim // tc),
        in_specs=[pl.BlockSpec((4, 2, tr, tc), lambda i, j: (0, 0, i, j)),
                  pl.BlockSpec((4, tr, tc), lambda i, j: (0, i, j))],
        out_specs=pl.BlockSpec((4, tr, tc), lambda i, j: (0, i, j)),
        out_shape=jax.ShapeDtypeStruct((4, r, cdim), mine.dtype),
        compiler_params=_params())(mine.reshape(4, 2, r, cdim), theirs)


def _exchange_chips(arrs):
    n = len(arrs)

    def body(*refs):
        ins, outs = refs[:n], refs[n:2 * n]
        send_sems, recv_sems, local_sems = refs[2 * n:]
        x, y, c = _mesh_pos()
        my_chip = 2 * x + y
        mine = [pltpu.make_async_copy(ins[a].at[my_chip], outs[a].at[my_chip], local_sems.at[a]) for a in range(n)]
        for cp in mine:
            cp.start()
        copies = []
        for rel in range(1, 4):
            px = 1 - x if rel & 2 else x
            py = 1 - y if rel & 1 else y
            for a in range(n):
                copies.append(pltpu.make_async_remote_copy(
                    src_ref=ins[a].at[2 * px + py], dst_ref=outs[a].at[my_chip],
                    send_sem=send_sems.at[3 * a + rel - 1], recv_sem=recv_sems.at[3 * a + rel - 1],
                    device_id=(px, py, c), device_id_type=MESH))
        for cp in copies:
            cp.start()
        for cp in copies:
            cp.wait_recv()
        for cp in copies:
            cp.wait_send()
        for cp in mine:
            cp.wait()

    return pl.pallas_call(
        body, name="grad_exchange_chips",
        out_shape=tuple(jax.ShapeDtypeStruct(a.shape, a.dtype) for a in arrs),
        in_specs=[_ANY] * n, out_specs=tuple([_ANY] * n),
        scratch_shapes=[pltpu.SemaphoreType.DMA((3 * n,)), pltpu.SemaphoreType.DMA((3 * n,)),
                        pltpu.SemaphoreType.DMA((n,))],
    )(*arrs)


_HBM = pl.BlockSpec(memory_space=pltpu.HBM)
_SEM = pl.BlockSpec(memory_space=pltpu.SEMAPHORE)


def _copies_start(name, srcs, lands, make_copies, n_sems):
    n, m = len(srcs), len(lands)

    def body(*refs):
        ins = refs[:n + m]
        send_sems, recv_sems = refs[n + m], refs[n + m + 1]
        for cp in make_copies(ins[:n], ins[n:], send_sems, recv_sems):
            cp.start()
        refs[-1][...] = jnp.zeros_like(refs[-1])

    res = pl.pallas_call(
        body, name=name,
        out_shape=(pltpu.SemaphoreType.DMA((n_sems,)), pltpu.SemaphoreType.DMA((n_sems,)),
                   *[pltpu.HBM(a.shape, a.dtype) for a in (*srcs, *lands)], jax.ShapeDtypeStruct((8, LANE), F32)),
        in_specs=[_HBM] * (n + m),
        out_specs=(_SEM, _SEM, *[_HBM] * (n + m), pl.BlockSpec(memory_space=pltpu.VMEM)),
        input_output_aliases={i: 2 + i for i in range(n + m)},
        compiler_params=pltpu.CompilerParams(has_side_effects=pltpu.SideEffectType.DATAFLOW_SIDE_EFFECTING),
    )(*[pltpu.with_memory_space_constraint(a, pltpu.HBM) for a in (*srcs, *lands)])
    return res[0], res[1], res[2:2 + n], res[2 + n:2 + n + m], res[-1]


def _copies_wait(name, started, after, make_copies):
    send_sems, recv_sems, srcs, lands, _ = started
    n, m = len(srcs), len(lands)

    def body(*refs):
        ins = refs[:n + m]
        for cp in make_copies(ins[:n], ins[n:], refs[n + m], refs[n + m + 1]):
            cp.wait_send()
            cp.wait_recv()

    res = pl.pallas_call(
        body, name=name,
        out_shape=tuple(pltpu.HBM(a.shape, a.dtype) for a in (*srcs, *lands)),
        in_specs=[_HBM] * (n + m) + [_SEM, _SEM] + [_ANY] * len(after),
        out_specs=tuple([_HBM] * (n + m)),
        input_output_aliases={i: i for i in range(n + m)},
        compiler_params=pltpu.CompilerParams(has_side_effects=pltpu.SideEffectType.DATAFLOW_SIDE_EFFECTING),
    )(*srcs, *lands, send_sems, recv_sems, *after)
    return res[:n], res[n:]


def _gather_copies(srcs, lands, send_sems, recv_sems):
    x, y, c = _mesh_pos()
    me_i = 4 * x + 2 * y + c
    copies = []
    for rel in range(1, N_DEV):
        peer = (1 - x if rel & 4 else x, 1 - y if rel & 2 else y, 1 - c if rel & 1 else c)
        for a in range(len(srcs)):
            copies.append(pltpu.make_async_remote_copy(
                src_ref=srcs[a], dst_ref=lands[a].at[me_i], send_sem=send_sems.at[7 * a + rel - 1],
                recv_sem=recv_sems.at[7 * a + rel - 1], device_id=peer, device_id_type=MESH))
    return copies


def _sibling_copies(srcs, lands, send_sems, recv_sems):
    x, y, c = _mesh_pos()
    return [pltpu.make_async_remote_copy(
        src_ref=srcs[a].at[2 * k + (1 - c)], dst_ref=lands[a].at[k], send_sem=send_sems.at[4 * a + k],
        recv_sem=recv_sems.at[4 * a + k], device_id=(x, y, 1 - c), device_id_type=MESH)
        for a in range(len(srcs)) for k in range(4)]


def _chip_copies(srcs, lands, send_sems, recv_sems):
    x, y, c = _mesh_pos()
    my_chip = 2 * x + y
    copies = []
    for rel in range(1, 4):
        px = 1 - x if rel & 2 else x
        py = 1 - y if rel & 1 else y
        for a in range(len(srcs)):
            copies.append(pltpu.make_async_remote_copy(
                src_ref=srcs[a].at[2 * px + py], dst_ref=lands[a].at[my_chip], send_sem=send_sems.at[3 * a + rel - 1],
                recv_sem=recv_sems.at[3 * a + rel - 1], device_id=(px, py, c), device_id_type=MESH))
    return copies


def _sum_adam(name, parts, w, m, v, own=None):
    unit_mid = w.ndim == 3
    _, r, cdim = parts.shape
    n_parts = parts.shape[0]
    tr, tc = _elementwise_tile(r, cdim)
    bc1 = 1.0 - ADAM_B1 ** ADAM_STEP
    bc2 = 1.0 - ADAM_B2 ** ADAM_STEP
    extra = [] if own is None else [own]

    def body(p_ref, *refs):
        w_ref, m_ref, v_ref, g_ref, d_ref, nm_ref, nv_ref = refs[len(extra):]
        if own is None:
            part = lambda k: p_ref[k].astype(F32)
        else:
            my_chip = 2 * lax.axis_index("x") + lax.axis_index("y")
            part = lambda k: jnp.where(my_chip == k, refs[0][k], p_ref[k]).astype(F32)
        g = part(0)
        for k in range(1, n_parts):
            g = g + part(k)
        if unit_mid:
            g = g.reshape(tr, 1, tc)
        mn = ADAM_B1 * m_ref[...] + (1.0 - ADAM_B1) * g
        vn = ADAM_B2 * v_ref[...] + (1.0 - ADAM_B2) * (g * g)
        g_ref[...] = g
        nm_ref[...] = mn
        nv_ref[...] = vn
        d_ref[...] = -ADAM_LR * ((mn / bc1) / (jnp.sqrt(vn / bc2) + ADAM_EPS) + ADAM_WD * w_ref[...])

    blk = (pl.BlockSpec((tr, 1, tc), lambda i, j: (i, 0, j)) if unit_mid
           else pl.BlockSpec((tr, tc), lambda i, j: (i, j)))
    o = jax.ShapeDtypeStruct(w.shape, F32)
    return pl.pallas_call(
        body, name=name, grid=(r // tr, cdim // tc),
        in_specs=[pl.BlockSpec((n_parts, tr, tc), lambda i, j: (0, i, j))] * (1 + len(extra)) + [blk, blk, blk],
        out_specs=(blk, blk, blk, blk), out_shape=(o, o, o, o),
        compiler_params=_params())(parts, *extra, w, m, v)


_WEIGHTS = ("c_ctx", "ada_w", "ada_b", "norm_g", "w_in", "b_in", "conv_w", "conv_b", "conv_ln_g", "conv_ln_b",
            "conv_proj", "decay_up_fwd", "decay_bias_fwd", "decay_up_bwd", "decay_bias_bwd", "gla_norm_g",
            "gla_proj", "w_out", "final_norm_g")


def _as2d(a):
    if a.ndim == 1:
        return a.reshape(1, -1)
    return a.reshape(-1, a.shape[-1])


def kernel(x, c, ctx, c_ctx, ada_w, ada_b, norm_g, w_in, b_in, conv_w, conv_b, conv_ln_g, conv_ln_b, conv_proj, decay_up_fwd, decay_bias_fwd, decay_up_bwd, decay_bias_bwd, gla_norm_g, gla_proj, w_out, final_norm_g, loss_target, m_c_ctx, m_ada_w, m_ada_b, m_norm_g, m_w_in, m_b_in, m_conv_w, m_conv_b, m_conv_ln_g, m_conv_ln_b, m_conv_proj, m_decay_up_fwd, m_decay_bias_fwd, m_decay_up_bwd, m_decay_bias_bwd, m_gla_norm_g, m_gla_proj, m_w_out, m_final_norm_g, v_c_ctx, v_ada_w, v_ada_b, v_norm_g, v_w_in, v_b_in, v_conv_w, v_conv_b, v_conv_ln_g, v_conv_ln_b, v_conv_proj, v_decay_up_fwd, v_decay_bias_fwd, v_decay_up_bwd, v_decay_bias_bwd, v_gla_norm_g, v_gla_proj, v_w_out, v_final_norm_g):
    env = dict(locals())
    wts = {k: env[k] for k in _WEIGHTS}
    d = x.shape[-1]
    r = decay_up_fwd.shape[1]
    dk_ = d // 2
    n_in = w_in.shape[-1] * N_DEV

    ds, dks = d // N_DEV, dk_ // N_DEV
    g_win, g_ada, conv_w8, g_up = _all_gather(
        [w_in[0].astype(BF16), ada_w[0].astype(BF16), conv_w[0],
         jnp.concatenate([decay_up_fwd[0], decay_up_bwd[0]], axis=1)])
    proj_own = [conv_proj[0].astype(BF16), gla_proj[0].astype(BF16), w_out[0].astype(BF16)]
    me_i = 4 * lax.axis_index("x") + 2 * lax.axis_index("y") + lax.axis_index("c")
    proj_lands = [lax.dynamic_update_slice(lax.empty((N_DEV,) + a.shape, a.dtype), a[None], (me_i, 0, 0))
                  for a in proj_own]
    proj_start = _copies_start("proj_gather_start", proj_own, proj_lands, _gather_copies, 7 * 3)

    def proj(after):
        _, lands = _copies_wait("proj_gather_wait", proj_start, (after,), _gather_copies)
        return [w.reshape(d, d) for w in lands]

    w_a, w_b = _unshard_w_in(g_win, d, r, after=(proj_start[4],))
    up_f = g_up[:, :, 0:dks].transpose(1, 0, 2).reshape(r, dk_)
    up_b = g_up[:, :, dks:].transpose(1, 0, 2).reshape(r, dk_)
    up2 = jnp.zeros((LANE, 2 * dk_), F32).at[0:r, 0:dk_].set(up_f).at[r:2 * r, dk_:].set(up_b)
    bias2 = jnp.concatenate([decay_bias_fwd, decay_bias_bwd], axis=1)
    b_a, b_b = _regroup(b_in, d, r)

    names = ("w_in", "conv_proj", "gla_proj", "w_out", "conv_w", "decay_up")
    comm = {}

    def on_grads(gr):
        d_up = jnp.concatenate([gr["up2"][0:r, 0:dk_].reshape(r, N_DEV, dks).transpose(1, 0, 2),
                                gr["up2"][r:2 * r, dk_:].reshape(r, N_DEV, dks).transpose(1, 0, 2)], axis=2)
        mine = [_reshard_w_in(gr["w_a1"], gr["w_a2"], gr["w_b"], d, r), gr["conv_proj"].reshape(N_DEV, ds, d),
                gr["gla_proj"].reshape(N_DEV, ds, d), gr["w_out"].reshape(N_DEV, ds, d), gr["conv_w8"], d_up]
        lands = [lax.empty((4,) + a.shape[1:], a.dtype) for a in mine]
        comm["sibling"] = _copies_start("grad_sibling_start", mine, lands, _sibling_copies, 4 * len(mine))
        return (comm["sibling"][4],)

    def on_du_a1(du_a1):
        mine, theirs = _copies_wait("grad_sibling_wait", comm["sibling"], (du_a1,), _sibling_copies)
        sums = [_pair_sum("pair_sum_" + nm, a, b) for nm, a, b in zip(names, mine, theirs)]
        lands = [lax.empty(a.shape, a.dtype) for a in sums]
        comm["chips"] = _copies_start("grad_chips_start", sums, lands, _chip_copies, 3 * len(sums))
        return (comm["chips"][4],)

    g = _local_step(x, c, ctx, loss_target, c_ctx, g_ada, ada_b, norm_g[0:1], w_a, b_a, w_b, b_b,
                    conv_w8, conv_b, conv_ln_g, conv_ln_b, up2, bias2, gla_norm_g, final_norm_g.reshape(1, d),
                    proj, on_grads, on_du_a1)

    (their_ada,) = _exchange_sibling([g["ada_w8"]])
    ada_sum = _pair_sum("pair_sum_ada_w", g["ada_w8"], their_ada)
    ada_start = _copies_start("ada_chips_start", [ada_sum], [lax.empty(ada_sum.shape, ada_sum.dtype)],
                              _chip_copies, 3)
    own, landed = _copies_wait("grad_chips_wait", comm["chips"], (ada_start[4],), _chip_copies)
    o_win, o_cp, o_gp, o_wo, o_cw, o_up = own
    x_win, x_cp, x_gp, x_wo, x_cw, x_up = landed

    (packs,) = _all_gather([_pack_small(g, x.shape[0], d, r)])
    row = lambda a: a.reshape(1, -1)
    sg, sd, sm, sv, loss = _small_adam(packs, [row(wts[k]) for k in _SMALL], [row(env["m_" + k]) for k in _SMALL],
                                       [row(env["v_" + k]) for k in _SMALL], d, r)
    out = {}
    for i, k in enumerate(_SMALL):
        for pre, arrs in (("grad_", sg), ("delta_", sd), ("new_m_", sm), ("new_v_", sv)):
            out[pre + k] = arrs[i].reshape(wts[k].shape)
    loss = loss.reshape(())

    def big(name, parts, wname, own=None):
        w2 = _as2d(wts[wname])
        res = _sum_adam(name, parts, w2, _as2d(env["m_" + wname]), _as2d(env["v_" + wname]), own)
        for pre, arr in zip(("grad_", "delta_", "new_m_", "new_v_"), res):
            out[pre + wname] = arr.reshape(wts[wname].shape)

    as_rows = lambda a: jnp.transpose(a, (2, 0, 1))
    res = _sum_adam("adam_w_in", x_win, as_rows(w_in), as_rows(m_w_in), as_rows(v_w_in), o_win)
    for pre, arr in zip(("grad_", "delta_", "new_m_", "new_v_"), res):
        out[pre + "w_in"] = jnp.transpose(arr, (1, 2, 0))
    big("adam_conv_proj", x_cp, "conv_proj", o_cp)
    big("adam_gla_proj", x_gp, "gla_proj", o_gp)
    big("adam_w_out", x_wo, "w_out", o_wo)
    big("adam_conv_w", x_cw, "conv_w", o_cw)
    big("adam_up_f", x_up[:, :, 0:dks], "decay_up_fwd", o_up[:, :, 0:dks])
    big("adam_up_b", x_up[:, :, dks:], "decay_up_bwd", o_up[:, :, dks:])
    (o_ada,), (x_ada,) = _copies_wait("ada_chips_wait", ada_start, (out["grad_w_in"], out["grad_w_out"], out["grad_b_in"]),
                                      _chip_copies)
    big("adam_ada_w", x_ada, "ada_w", o_ada)

    return (loss, g["grad_x"], *[out["grad_" + k] for k in _WEIGHTS], *[out["delta_" + k] for k in _WEIGHTS],
            *[out["new_m_" + k] for k in _WEIGHTS], *[out["new_v_" + k] for k in _WEIGHTS])
```

```python
import functools
import math

import jax
import jax.numpy as jnp
from jax import lax
from jax.experimental import pallas as pl
from jax.experimental.pallas import tpu as pltpu

F32 = jnp.float32
BF16 = jnp.bfloat16
MESH = pl.DeviceIdType.MESH

N_DEV = 8
GRID_W = 64
CHUNK = 128
HEADS = 4
EPS = 1e-6
GATE_TAU = 16.0
LANE = 128
ADAM_LR, ADAM_B1, ADAM_B2, ADAM_EPS, ADAM_WD, ADAM_STEP = 0.001, 0.9, 0.999, 1e-08, 0.01, 10
VMEM_LIMIT = 60 * 1024 * 1024
_ANY = pl.BlockSpec(memory_space=pl.ANY)


def _params(**kw):
    return pltpu.CompilerParams(vmem_limit_bytes=VMEM_LIMIT, **kw)


def _tile(n, pref):
    t = (min(pref, n) // LANE) * LANE
    while t >= LANE:
        if n % t == 0:
            return t
        t -= LANE
    return n


def _mm(a, b):
    return jnp.dot(a.astype(BF16), b.astype(BF16), preferred_element_type=F32)


def _mm_nt(a, b):
    return lax.dot_general(a.astype(BF16), b.astype(BF16), (((1,), (1,)), ((), ())), preferred_element_type=F32)


def _mm_tn(a, b):
    return lax.dot_general(a.astype(BF16), b.astype(BF16), (((0,), (0,)), ((), ())), preferred_element_type=F32)


def _mm_tn_hi(a, b):
    return lax.dot_general(a, b, (((0,), (0,)), ((), ())), precision=lax.Precision.HIGHEST, preferred_element_type=F32)


def _sigmoid(x):
    return 0.5 * jnp.tanh(0.5 * x) + 0.5


def _dsilu(x, s):
    return s * (1.0 + x * (1.0 - s))


def _rowsel(table, idx, n):
    out = table[0:1, :]
    for r in range(1, n):
        out = jnp.where(idx == r, table[r:r + 1, :], out)
    return out


def _ada_fwd(cv, ada_w8, ada_b):
    n_sh, _, ws = ada_w8.shape

    def body(cv_ref, w_ref, b_ref, o_ref):
        c = cv_ref[...]
        sv = c * _sigmoid(c)
        for j in range(n_sh):
            cols = pl.ds(j * ws, ws)
            o_ref[:, cols] = _mm(sv, w_ref[j]) + b_ref[:, cols]

    return pl.pallas_call(body, name="ada_fwd", out_shape=jax.ShapeDtypeStruct((cv.shape[0], n_sh * ws), F32),
                          compiler_params=_params())(cv, ada_w8, ada_b)


def _ada_bwd(cv, ada_w8, dmod_ss, small, nb):
    n_sh, d, ws = ada_w8.shape

    def body(cv_ref, w_ref, dm_ref, sm_ref, dw_ref, db_ref, dc_ref):
        c = cv_ref[...]
        s = _sigmoid(c)
        sv = c * s
        dm = jnp.concatenate([dm_ref[:, 0:2 * d], sm_ref[8:16, :]], axis=1)
        db_ref[...] = jnp.sum(dm, axis=0, keepdims=True)
        dsv = None
        for j in range(n_sh):
            dmj = dm[:, j * ws:(j + 1) * ws]
            dw_ref[j] = _mm_tn_hi(sv, dmj).astype(dw_ref.dtype)
            part = _mm_nt(dmj, w_ref[j])
            dsv = part if dsv is None else dsv + part
        dc_ref[...] = dsv * _dsilu(c, s)

    return pl.pallas_call(
        body, name="ada_bwd",
        out_shape=(jax.ShapeDtypeStruct((n_sh, d, ws), BF16), jax.ShapeDtypeStruct((1, n_sh * ws), F32),
                   jax.ShapeDtypeStruct(cv.shape, F32)),
        compiler_params=_params())(cv, ada_w8, dmod_ss, small)


class _Tiles:
    def __init__(self, nb, s_len, c_len, tm, big):
        self.nb, self.tm, self.big = nb, tm, big
        self.lat, self.ctx = s_len // tm, c_len // tm
        self.pad = -(self.lat + self.ctx) % big
        self.per_ex = self.lat + self.ctx + self.pad
        self.n_all, self.n_lat = nb * self.per_ex, nb * self.lat
        self.rows_per_ex = self.per_ex * tm

    def is_lat(self, i):
        return i % self.per_ex < self.lat

    def is_pad(self, i):
        return i % self.per_ex >= self.lat + self.ctx

    def lat_of_all(self, i):
        return (i // self.per_ex) * self.lat + jnp.minimum(i % self.per_ex, self.lat - 1)

    def ctx_of_all(self, i):
        return (i // self.per_ex) * self.ctx + jnp.clip(i % self.per_ex - self.lat, 0, self.ctx - 1)

    def big_all_of_lat(self, t):
        lat_big = self.lat // self.big
        return (t // lat_big) * (self.per_ex // self.big) + t % lat_big


def _norm_fwd(x2, ctx2, mod, norm_g, tiles):
    tl, d = x2.shape
    tc = ctx2.shape[0]
    nb, tm = tiles.nb, tiles.tm

    def body(x_ref, c_ref, mod_ref, g_ref, u_ref):
        i = pl.program_id(0)
        lat = tiles.is_lat(i)
        xv = jnp.where(lat, x_ref[...], c_ref[...])
        row = jnp.where(lat, i // tiles.per_ex, nb)
        m = _rowsel(mod_ref[...], row, nb + 1)
        shift, scale = m[:, 0:d], m[:, d:2 * d]
        rstd = lax.rsqrt(jnp.mean(xv * xv, axis=-1, keepdims=True) + EPS)
        u = xv * rstd * g_ref[...] * (1.0 + scale) + shift
        u_ref[...] = jnp.where(tiles.is_pad(i), 0.0, u).astype(BF16)

    return pl.pallas_call(
        body, name="norm_fwd", grid=(tiles.n_all,),
        in_specs=[pl.BlockSpec((tm, d), lambda i: (tiles.lat_of_all(i), 0)),
                  pl.BlockSpec((tm, d), lambda i: (tiles.ctx_of_all(i), 0)),
                  pl.BlockSpec(mod.shape, lambda i: (0, 0)),
                  pl.BlockSpec((1, d), lambda i: (0, 0))],
        out_specs=pl.BlockSpec((tm, d), lambda i: (i, 0)),
        out_shape=jax.ShapeDtypeStruct((tiles.n_all * tm, d), BF16),
        compiler_params=_params())(x2, ctx2, mod, norm_g)


def _norm_bwd(x2, ctx2, mod, norm_g, du_lat, du_b, gx1, tiles):
    tl, d = x2.shape
    nb, tm = tiles.nb, tiles.tm
    nrow = mod.shape[0]
    n_lat_in = len(du_lat)

    def body(x_ref, c_ref, mod_ref, g_ref, *refs):
        dl_refs = refs[:n_lat_in]
        d3_ref, gx_ref, gxo_ref, dmod_ref, dg_ref = refs[n_lat_in:]
        i = pl.program_id(0)

        @pl.when(i == 0)
        def _():
            dmod_ref[...] = jnp.zeros_like(dmod_ref)
            dg_ref[...] = jnp.zeros_like(dg_ref)

        lat = tiles.is_lat(i)
        xv = jnp.where(lat, x_ref[...], c_ref[...])
        row = jnp.where(lat, i // tiles.per_ex, nb)
        m = _rowsel(mod_ref[...], row, nb + 1)
        scale = m[:, d:2 * d]
        g = g_ref[...]
        dl = dl_refs[0][...]
        for ref in dl_refs[1:]:
            dl = dl + ref[...]
        du = jnp.where(tiles.is_pad(i), 0.0, d3_ref[...] + jnp.where(lat, dl, 0.0))
        rstd = lax.rsqrt(jnp.mean(xv * xv, axis=-1, keepdims=True) + EPS)
        xh = xv * rstd
        dshift = jnp.sum(du, axis=0, keepdims=True)
        dscale = jnp.sum(du * xh * g, axis=0, keepdims=True)
        dxn = du * (1.0 + scale)
        dg_ref[...] += jnp.sum(dxn * xh, axis=0, keepdims=True)
        dxh = dxn * g
        dx = rstd * (dxh - xh * jnp.mean(dxh * xh, axis=-1, keepdims=True))

        @pl.when(lat)
        def _():
            gxo_ref[...] = dx + gx_ref[...]

        for r in range(nb + 1):
            dmod_ref[r:r + 1, 0:d] += jnp.where(row == r, dshift, 0.0)
            dmod_ref[r:r + 1, d:2 * d] += jnp.where(row == r, dscale, 0.0)

    lat_map = lambda i: (tiles.lat_of_all(i), 0)
    lat_spec = pl.BlockSpec((tm, d), lat_map)
    return pl.pallas_call(
        body, name="norm_bwd", grid=(tiles.n_all,),
        in_specs=[lat_spec,
                  pl.BlockSpec((tm, d), lambda i: (tiles.ctx_of_all(i), 0)),
                  pl.BlockSpec(mod.shape, lambda i: (0, 0)),
                  pl.BlockSpec((1, d), lambda i: (0, 0))]
                 + [lat_spec] * n_lat_in
                 + [pl.BlockSpec((tm, d), lambda i: (i, 0)), lat_spec],
        out_specs=(lat_spec,
                   pl.BlockSpec((nrow, 3 * d), lambda i: (0, 0)),
                   pl.BlockSpec((1, d), lambda i: (0, 0))),
        out_shape=(jax.ShapeDtypeStruct((tl, d), F32), jax.ShapeDtypeStruct((nrow, 3 * d), F32),
                   jax.ShapeDtypeStruct((1, d), F32)),
        compiler_params=_params())(x2, ctx2, mod, norm_g, *du_lat, du_b, gx1)


def _matmul_bias(name, u, w, b, rows, tm, tn, u_tile):
    d, n = w.shape

    def body(u_ref, w_ref, b_ref, o_ref):
        o_ref[...] = jnp.dot(u_ref[...], w_ref[...], preferred_element_type=F32) + b_ref[...]

    return pl.pallas_call(
        body, name=name, grid=(n // tn, rows // tm),
        in_specs=[pl.BlockSpec((tm, d), lambda j, i: (u_tile(i), 0)),
                  pl.BlockSpec((d, tn), lambda j, i: (0, j)),
                  pl.BlockSpec((1, tn), lambda j, i: (0, j))],
        out_specs=pl.BlockSpec((tm, tn), lambda j, i: (i, j)),
        out_shape=jax.ShapeDtypeStruct((rows, n), F32),
        compiler_params=_params())(u, w, b)


def _inproj_b(u, w_b, b_b, tm, dk_, dv_):
    t_all, d = u.shape
    nbw = w_b.shape[1]

    def body(u_ref, w_ref, b_ref, qk_ref, v_ref):
        full = jnp.dot(u_ref[...], w_ref[...], preferred_element_type=F32) + b_ref[...]
        qk_ref[:, 0:2 * dk_] = full[:, 0:2 * dk_]
        qk_ref[:, 2 * dk_:2 * dk_ + LANE] = full[:, 2 * dk_ + dv_:nbw]
        v_ref[...] = full[:, 2 * dk_:2 * dk_ + dv_].astype(BF16)

    return pl.pallas_call(
        body, name="inproj_b", grid=(t_all // tm,),
        in_specs=[pl.BlockSpec((tm, d), lambda i: (i, 0)), pl.BlockSpec((d, nbw), lambda i: (0, 0)),
                  pl.BlockSpec((1, nbw), lambda i: (0, 0))],
        out_specs=(pl.BlockSpec((tm, 2 * dk_ + LANE), lambda i: (i, 0)), pl.BlockSpec((tm, dv_), lambda i: (i, 0))),
        out_shape=(jax.ShapeDtypeStruct((t_all, 2 * dk_ + LANE), F32), jax.ShapeDtypeStruct((t_all, dv_), BF16)),
        compiler_params=_params())(u, w_b, b_b)


def _matmul_nt(name, a, w, koff, tm, tk, after=()):
    r, kc = a.shape
    d = w.shape[0]
    nk = kc // tk

    def body(a_ref, w_ref, *rest):
        o_ref = rest[len(after)]
        k = pl.program_id(1)
        p = lax.dot_general(a_ref[...], w_ref[...], (((1,), (1,)), ((), ())), preferred_element_type=F32)

        @pl.when(k == 0)
        def _():
            o_ref[...] = p

        @pl.when(k > 0)
        def _():
            o_ref[...] += p

    return pl.pallas_call(
        body, name=name, grid=(r // tm, nk),
        in_specs=[pl.BlockSpec((tm, tk), lambda i, k: (i, k)),
                  pl.BlockSpec((d, tk), lambda i, k: (0, koff + k))] + [_ANY] * len(after),
        out_specs=pl.BlockSpec((tm, d), lambda i, k: (i, 0)),
        out_shape=jax.ShapeDtypeStruct((r, d), F32),
        compiler_params=_params())(a, w, *after)


def _matmul_tn(name, a, b, rows, tk, tn):
    m = a.shape[1]
    n = b.shape[1]
    nk = rows // tk

    def body(a_ref, b_ref, o_ref, s_ref, acc_ref):
        k = pl.program_id(1)
        bv = b_ref[...]
        p = lax.dot_general(bv, a_ref[...], (((0,), (0,)), ((), ())), preferred_element_type=F32)
        cs = jnp.sum(bv.astype(F32), axis=0, keepdims=True)

        @pl.when(k == 0)
        def _():
            acc_ref[...] = p
            s_ref[...] = cs

        @pl.when(k > 0)
        def _():
            acc_ref[...] += p
            s_ref[...] += cs

        @pl.when(k == nk - 1)
        def _():
            o_ref[...] = acc_ref[...].astype(o_ref.dtype)

    return pl.pallas_call(
        body, name=name, grid=(n // tn, nk),
        in_specs=[pl.BlockSpec((tk, m), lambda j, k: (k, 0)),
                  pl.BlockSpec((tk, tn), lambda j, k: (k, j))],
        out_specs=(pl.BlockSpec((tn, m), lambda j, k: (j, 0)), pl.BlockSpec((1, tn), lambda j, k: (0, j))),
        out_shape=(jax.ShapeDtypeStruct((n, m), BF16), jax.ShapeDtypeStruct((1, n), F32)),
        scratch_shapes=[pltpu.VMEM((tn, m), F32)],
        compiler_params=_params())(a, b)


def _matmul_tn_whole(name, a3, b3, rows, tn, transposed):
    nb, _, m = a3.shape
    n = b3.shape[2]

    def body(a_ref, b_ref, o_ref, s_ref):
        p, cs = None, None
        for e in range(nb):
            bv = b_ref[e]
            lhs, rhs = (bv, a_ref[e]) if transposed else (a_ref[e], bv)
            pe = lax.dot_general(lhs, rhs, (((0,), (0,)), ((), ())), preferred_element_type=F32)
            ce = jnp.sum(bv.astype(F32), axis=0, keepdims=True)
            p, cs = (pe, ce) if p is None else (p + pe, cs + ce)
        o_ref[...] = p.astype(o_ref.dtype)
        s_ref[...] = cs

    o_spec, o_shape = ((pl.BlockSpec((tn, m), lambda j: (j, 0)), (n, m)) if transposed
                       else (pl.BlockSpec((m, tn), lambda j: (0, j)), (m, n)))
    return pl.pallas_call(
        body, name=name, grid=(n // tn,),
        in_specs=[pl.BlockSpec((nb, rows, m), lambda j: (0, 0, 0)),
                  pl.BlockSpec((nb, rows, tn), lambda j: (0, 0, j))],
        out_specs=(o_spec, pl.BlockSpec((1, tn), lambda j: (0, j))),
        out_shape=(jax.ShapeDtypeStruct(o_shape, BF16), jax.ShapeDtypeStruct((1, n), F32)),
        compiler_params=_params())(a3, b3)


def _conv_window(pad_ref, r, shift, ktaps, width, horizontal):
    if horizontal:
        return pad_ref[r, pl.ds(16 + shift, width), :]
    return pad_ref[r + ktaps // 2 + shift]


def _conv_row(pad_ref, w, r, ktaps, width, horizontal, flip):
    half = ktaps // 2
    acc = None
    for t in range(ktaps):
        win = _conv_window(pad_ref, r, (half - t) if flip else (t - half), ktaps, width, horizontal)
        term = win * w[t:t + 1, :]
        acc = term if acc is None else acc + term
    return acc


def _fill_padded(ref, val, rows, width, ktaps, horizontal):
    half_k = ktaps // 2
    cb = val.shape[-1]
    if horizontal:
        ref[:, 0:16, :] = jnp.zeros((rows, 16, cb), F32)
        ref[:, 16 + width:32 + width, :] = jnp.zeros((rows, 16, cb), F32)
        ref[:, 16:16 + width, :] = val
    else:
        ref[0:half_k, :, :] = jnp.zeros((half_k, width, cb), F32)
        ref[half_k + rows:2 * half_k + rows, :, :] = jnp.zeros((half_k, width, cb), F32)
        ref[half_k:half_k + rows, :, :] = val


def _conv_fwd(pa, conv_w8, conv_b, nb, s):
    nblk, ktaps, cb = conv_w8.shape
    d = nblk * cb
    rows, width = s // GRID_W, GRID_W
    half_k = ktaps // 2
    nh = nblk // 2

    def body(glu_ref, w_ref, b_ref, o_ref, ph_ref, pv_ref):
        j = pl.program_id(1)
        a0 = (glu_ref[:, 0:cb] * _sigmoid(glu_ref[:, cb:2 * cb])).reshape(rows, width, cb)
        w = w_ref[...]

        bias = b_ref[...]

        def run(pad_ref, horizontal):
            _fill_padded(pad_ref, a0, rows, width, ktaps, horizontal)

            def row(r, carry):
                at = pl.ds(pl.multiple_of(r * width, width), width)
                o_ref[at, :] = _conv_row(pad_ref, w, r, ktaps, width, horizontal, False) + bias
                return carry

            lax.fori_loop(0, rows, row, 0)

        @pl.when(j < nh)
        def _():
            run(ph_ref, True)

        @pl.when(j >= nh)
        def _():
            run(pv_ref, False)

    return pl.pallas_call(
        body, name="conv_fwd", grid=(nb, nblk),
        in_specs=[pl.BlockSpec((s, 2 * cb), lambda b, j: (b, j)),
                  pl.BlockSpec((None, ktaps, cb), lambda b, j: (j, 0, 0)),
                  pl.BlockSpec((1, cb), lambda b, j: (0, j))],
        out_specs=pl.BlockSpec((s, cb), lambda b, j: (b, j)),
        out_shape=jax.ShapeDtypeStruct((nb * s, d), F32),
        scratch_shapes=[pltpu.VMEM((rows, width + 32, cb), F32), pltpu.VMEM((rows + 2 * half_k, width, cb), F32)],
        compiler_params=_params())(pa, conv_w8, conv_b)


def _conv_bwd(pa, da1, conv_w8, nb, s):
    nblk, ktaps, cb = conv_w8.shape
    d = nblk * cb
    rows, width = s // GRID_W, GRID_W
    half_k = ktaps // 2
    nh = nblk // 2

    def body(glu_ref, da_ref, w_ref, dp_ref, dw_ref, db_ref, pha_ref, phd_ref, pva_ref, pvd_ref):
        j = pl.program_id(0)
        b = pl.program_id(1)
        a0 = (glu_ref[:, 0:cb] * _sigmoid(glu_ref[:, cb:2 * cb])).reshape(rows, width, cb)
        da1v = da_ref[...]
        d3 = da1v.reshape(rows, width, cb)
        w = w_ref[...]

        @pl.when(b == 0)
        def _():
            dw_ref[...] = jnp.zeros_like(dw_ref)
            db_ref[...] = jnp.zeros_like(db_ref)

        db_ref[...] += jnp.sum(da1v, axis=0, keepdims=True)

        def run(pa_ref, pd_ref, horizontal):
            _fill_padded(pa_ref, a0, rows, width, ktaps, horizontal)
            _fill_padded(pd_ref, d3, rows, width, ktaps, horizontal)

            def row(r, accs):
                at = pl.ds(pl.multiple_of(r * width, width), width)
                da0 = _conv_row(pd_ref, w, r, ktaps, width, horizontal, True)
                gv = glu_ref[at, 0:cb]
                sg = _sigmoid(glu_ref[at, cb:2 * cb])
                dp_ref[at, 0:cb] = (da0 * sg).astype(BF16)
                dp_ref[at, cb:2 * cb] = (da0 * gv * sg * (1.0 - sg)).astype(BF16)
                d_row = da_ref[at, :]
                out = []
                for t in range(ktaps):
                    prod = _conv_window(pa_ref, r, t - half_k, ktaps, width, horizontal) * d_row
                    out.append(accs[t] + jnp.sum(prod.reshape(width // 8, 8, cb), axis=0))
                return tuple(out)

            accs = lax.fori_loop(0, rows, row, tuple(jnp.zeros((8, cb), F32) for _ in range(ktaps)))
            for t in range(ktaps):
                dw_ref[t:t + 1, :] += jnp.sum(accs[t], axis=0, keepdims=True)

        @pl.when(j < nh)
        def _():
            run(pha_ref, phd_ref, True)

        @pl.when(j >= nh)
        def _():
            run(pva_ref, pvd_ref, False)

    return pl.pallas_call(
        body, name="conv_bwd", grid=(nblk, nb),
        in_specs=[pl.BlockSpec((s, 2 * cb), lambda j, b: (b, j)),
                  pl.BlockSpec((s, cb), lambda j, b: (b, j)),
                  pl.BlockSpec((None, ktaps, cb), lambda j, b: (j, 0, 0))],
        out_specs=(pl.BlockSpec((s, 2 * cb), lambda j, b: (b, j)),
                   pl.BlockSpec((None, ktaps, cb), lambda j, b: (j, 0, 0)),
                   pl.BlockSpec((1, cb), lambda j, b: (0, j))),
        out_shape=(jax.ShapeDtypeStruct((nb * s, 2 * d), BF16),
                   jax.ShapeDtypeStruct((nblk, ktaps, cb), F32), jax.ShapeDtypeStruct((1, d), F32)),
        scratch_shapes=[pltpu.VMEM((rows, width + 32, cb), F32), pltpu.VMEM((rows, width + 32, cb), F32),
                        pltpu.VMEM((rows + 2 * half_k, width, cb), F32),
                        pltpu.VMEM((rows + 2 * half_k, width, cb), F32)],
        compiler_params=_params())(pa, da1, conv_w8)


def _log_sigmoid(x):
    return jnp.minimum(x, 0.0) - jnp.log(1.0 + jnp.exp(-jnp.abs(x)))


def _decay_fwd(pb, up2, bias2, tm, lr_blk):
    t_all = pb.shape[0]
    n2 = up2.shape[1]

    def body(lr_ref, up_ref, b_ref, g_ref):
        logits = _mm(lr_ref[...], up_ref[...]) + b_ref[...]
        g_ref[...] = _log_sigmoid(logits) * (1.0 / GATE_TAU)

    return pl.pallas_call(
        body, name="decay_fwd", grid=(t_all // tm,),
        in_specs=[pl.BlockSpec((tm, LANE), lambda i: (i, lr_blk)),
                  pl.BlockSpec(up2.shape, lambda i: (0, 0)),
                  pl.BlockSpec((1, n2), lambda i: (0, 0))],
        out_specs=pl.BlockSpec((tm, n2), lambda i: (i, 0)),
        out_shape=jax.ShapeDtypeStruct((t_all, n2), F32),
        compiler_params=_params())(pb, up2, bias2)


def _decay_bwd(pb, up2, bias2, grads_f, grads_b, tiles, lr_blk, dk_, dv_):
    t_all = pb.shape[0]
    tm = tiles.tm
    n2 = up2.shape[1]
    nbw = 2 * dk_ + dv_ + LANE

    def body(lr_ref, up_ref, b_ref, dqf, dkf, dvf, dgf, dqb, dkb, dvb, dgb, dp_ref, dup_ref, dbias_ref):
        i = pl.program_id(0)
        pad = tiles.is_pad(i)
        live = lambda v: jnp.where(pad, 0.0, v)

        @pl.when(i == 0)
        def _():
            dup_ref[...] = jnp.zeros_like(dup_ref)
            dbias_ref[...] = jnp.zeros_like(dbias_ref)

        lr = lr_ref[...]
        up = up_ref[...]
        logits = _mm(lr, up) + b_ref[...]
        dg = live(jnp.concatenate([dgf[...], dgb[...]], axis=1))
        dlog = dg * (1.0 / GATE_TAU) * _sigmoid(-logits)
        dup_ref[...] += _mm_tn(lr, dlog)
        dbias_ref[...] += jnp.sum(dlog, axis=0, keepdims=True)
        both = lambda f, b: live(f[...].astype(F32) + b[...].astype(F32)).astype(BF16)
        dp_ref[:, 0:dk_] = both(dqf, dqb)
        dp_ref[:, dk_:2 * dk_] = both(dkf, dkb)
        dp_ref[:, 2 * dk_:2 * dk_ + dv_] = both(dvf, dvb)
        dp_ref[:, 2 * dk_ + dv_:nbw] = _mm_nt(dlog, up).astype(BF16)

    row = lambda w: pl.BlockSpec((tm, w), lambda i: (i, 0))
    return pl.pallas_call(
        body, name="decay_bwd", grid=(t_all // tm,),
        in_specs=[pl.BlockSpec((tm, LANE), lambda i: (i, lr_blk)),
                  pl.BlockSpec(up2.shape, lambda i: (0, 0)),
                  pl.BlockSpec((1, n2), lambda i: (0, 0)),
                  row(dk_), row(dk_), row(dv_), row(dk_), row(dk_), row(dk_), row(dv_), row(dk_)],
        out_specs=(row(nbw), pl.BlockSpec(up2.shape, lambda i: (0, 0)), pl.BlockSpec((1, n2), lambda i: (0, 0))),
        out_shape=(jax.ShapeDtypeStruct((t_all, nbw), BF16), jax.ShapeDtypeStruct(up2.shape, F32),
                   jax.ShapeDtypeStruct((1, n2), F32)),
        compiler_params=_params())(pb, up2, bias2, *grads_f, *grads_b)


def _scan_chunk(s, nl, nc, rev):
    if rev:
        return jnp.where(s < nc, nl + (nc - 1 - s), nl - 1 - (s - nc))
    return jnp.where(s < nc, nl + s, s - nc)


def _scan_lat_chunk(s, nl, nc, rev):
    first = nl - 1 if rev else 0
    return jnp.where(s < nc, first, _scan_chunk(s, nl, nc, rev))


def _tri_mm(m_bf, x):
    hi = x.astype(BF16)
    r1 = x - hi.astype(F32)
    mid = r1.astype(BF16)
    lo = (r1 - mid.astype(F32)).astype(BF16)
    dot = lambda p: jnp.dot(m_bf, p, preferred_element_type=F32)
    return dot(hi) + dot(mid) + dot(lo)


def _chunk_masks(c, rev):
    ii = lax.broadcasted_iota(jnp.int32, (c, c), 0)
    jj = lax.broadcasted_iota(jnp.int32, (c, c), 1)
    return ((ii <= jj), (ii >= jj)) if rev else ((ii >= jj), (ii <= jj))


def _chunk_terms(q, k, b, far, mid):
    bf, bm = b[far:far + 1, :], b[mid:mid + 1, :]
    e = jnp.exp(b)
    em = jnp.exp(b - bm)
    eim = jnp.exp(bm - b)
    ed = jnp.exp(bf - b)
    return dict(e=e, em=em, eim=eim, ed=ed, dec=jnp.exp(bf), qe=q * e, qem=q * em, kim=k * eim, kd=k * ed)


def _gla_fwd(pb3, pv3, g3, nb, s_len, c_len, dk_, dv_):
    c = CHUNK
    nl, nc = s_len // c, c_len // c
    ns = nl + nc
    hk, hv = dk_ // HEADS, dv_ // HEADS
    l_len = pb3.shape[1]
    scale = hk ** -0.5
    mid = c // 2

    def body(*refs):
        ins, outs, z_scr = refs[:8], refs[8:14], refs[14]
        s = pl.program_id(0)

        @pl.when(s == 0)
        def _():
            z_scr[...] = jnp.zeros_like(z_scr)

        qs = jnp.where(s >= nc, scale, 0.0)
        for di, rev in enumerate((False, True)):
            q_ref, k_ref, v_ref, g_ref = ins[4 * di:4 * di + 4]
            o_ref, zs_ref, b_ref = outs[3 * di:3 * di + 3]
            mask, _ = _chunk_masks(c, rev)
            m_bf = mask.astype(BF16)
            far = 0 if rev else c - 1
            for b in range(nb):
                bc = _tri_mm(m_bf, g_ref[b])
                b_ref[b] = bc
                for h in range(HEADS):
                    ks, vs = slice(h * hk, (h + 1) * hk), slice(h * hv, (h + 1) * hv)
                    zi = (di * nb + b) * HEADS + h
                    v = v_ref[b, :, vs]
                    t = _chunk_terms(q_ref[b, :, ks] * qs, k_ref[b, :, ks], bc[:, ks], far, mid)
                    a = jnp.where(mask, _mm_nt(t["qem"], t["kim"]), 0.0)
                    z = z_scr[zi]
                    zs_ref[0, b * HEADS + h] = z
                    o_ref[b, :, vs] = _mm(a, v) + _mm_nt(t["qe"], z)
                    z_scr[zi] = z * t["dec"] + _mm_tn(v, t["kd"])

    in_specs, out_specs, out_shape = [], [], []
    for di, rev in enumerate((False, True)):
        ch = functools.partial(_scan_chunk, nl=nl, nc=nc, rev=rev)
        lch = functools.partial(_scan_lat_chunk, nl=nl, nc=nc, rev=rev)
        in_specs += [pl.BlockSpec((nb, c, dk_), lambda s, ch=ch: (0, ch(s), 0)),
                     pl.BlockSpec((nb, c, dk_), lambda s, ch=ch: (0, ch(s), 1)),
                     pl.BlockSpec((nb, c, dv_), lambda s, ch=ch: (0, ch(s), 0)),
                     pl.BlockSpec((nb, c, dk_), lambda s, ch=ch, di=di: (0, ch(s), di))]
        out_specs += [pl.BlockSpec((nb, c, dv_), lambda s, lch=lch: (0, lch(s), 0)),
                      pl.BlockSpec((1, nb * HEADS, hv, hk), lambda s: (s, 0, 0, 0)),
                      pl.BlockSpec((nb, c, dk_), lambda s, ch=ch: (0, ch(s), 0))]
        out_shape += [jax.ShapeDtypeStruct((nb, s_len, dv_), F32),
                      jax.ShapeDtypeStruct((ns, nb * HEADS, hv, hk), F32),
                      jax.ShapeDtypeStruct((nb, l_len, dk_), F32)]
    return pl.pallas_call(
        body, name="gla_fwd", grid=(ns,), in_specs=in_specs, out_specs=tuple(out_specs), out_shape=tuple(out_shape),
        scratch_shapes=[pltpu.VMEM((2 * nb * HEADS, hv, hk), F32)],
        compiler_params=_params())(pb3, pb3, pv3, g3, pb3, pb3, pv3, g3)


def _gla_bwd(pb3, pv3, do3, fwd_saved, nb, s_len, c_len, dk_, dv_):
    c = CHUNK
    nl, nc = s_len // c, c_len // c
    ns = nl + nc
    hk, hv = dk_ // HEADS, dv_ // HEADS
    l_len = pb3.shape[1]
    scale = hk ** -0.5
    mid = c // 2
    zs_f, b_f, zs_b, b_b = fwd_saved

    def body(*refs):
        ins, outs, dz_scr = refs[:12], refs[12:20], refs[20]
        s = pl.program_id(0)
        step = ns - 1 - s

        @pl.when(s == 0)
        def _():
            dz_scr[...] = jnp.zeros_like(dz_scr)

        lat = step >= nc
        qs = jnp.where(lat, scale, 0.0)
        dmul = jnp.where(lat, 1.0, 0.0)
        for di, rev in enumerate((False, True)):
            q_ref, k_ref, v_ref, b_ref, do_ref, zs_ref = ins[6 * di:6 * di + 6]
            dq_ref, dk_ref, dv_ref, dg_ref = outs[4 * di:4 * di + 4]
            mask, mask_t = _chunk_masks(c, rev)
            mt_bf = mask_t.astype(BF16)
            far = 0 if rev else c - 1
            far_row = lax.broadcasted_iota(jnp.int32, (c, hk), 0) == far
            for b in range(nb):
                db_parts = []
                for h in range(HEADS):
                    ks, vs = slice(h * hk, (h + 1) * hk), slice(h * hv, (h + 1) * hv)
                    zi = (di * nb + b) * HEADS + h
                    v = v_ref[b, :, vs]
                    d_o = do_ref[b, :, vs] * dmul
                    t = _chunk_terms(q_ref[b, :, ks] * qs, k_ref[b, :, ks], b_ref[b, :, ks], far, mid)
                    qem, kim, qe, kd = t["qem"], t["kim"], t["qe"], t["kd"]
                    a_t = jnp.where(mask_t, _mm_nt(kim, qem), 0.0)
                    d_a = jnp.where(mask, _mm_nt(d_o, v), 0.0)
                    d_at = jnp.where(mask_t, _mm_nt(v, d_o), 0.0)
                    z = zs_ref[0, b * HEADS + h]
                    dzn = dz_scr[zi]
                    dv_ref[b, :, vs] = (_mm(a_t, d_o) + _mm_nt(kd, dzn)).astype(dv_ref.dtype)
                    dqem = _mm(d_a, kim)
                    dkim = _mm(d_at, qem)
                    dqe = _mm(d_o, z)
                    dkd = _mm(v, dzn)
                    ddec = jnp.sum(z * dzn, axis=0, keepdims=True)
                    dz_scr[zi] = dzn * t["dec"] + _mm_tn(d_o, qe)
                    dq_ref[b, :, ks] = ((dqem * t["em"] + dqe * t["e"]) * qs).astype(dq_ref.dtype)
                    dk_ref[b, :, ks] = (dkim * t["eim"] + dkd * t["ed"]).astype(dk_ref.dtype)
                    db = dqem * qem - dkim * kim + dqe * qe - dkd * kd
                    extra = jnp.sum(dkd * kd, axis=0, keepdims=True) + ddec * t["dec"]
                    db_parts.append(db + jnp.where(far_row, extra, 0.0))
                dg_ref[b] = _tri_mm(mt_bf, jnp.concatenate(db_parts, axis=1))

    in_specs, out_specs, out_shape, args = [], [], [], []
    for di, rev in enumerate((False, True)):
        ch = lambda s, rev=rev: _scan_chunk(ns - 1 - s, nl, nc, rev)
        lch = lambda s, rev=rev: _scan_lat_chunk(ns - 1 - s, nl, nc, rev)
        in_specs += [pl.BlockSpec((nb, c, dk_), lambda s, ch=ch: (0, ch(s), 0)),
                     pl.BlockSpec((nb, c, dk_), lambda s, ch=ch: (0, ch(s), 1)),
                     pl.BlockSpec((nb, c, dv_), lambda s, ch=ch: (0, ch(s), 0)),
                     pl.BlockSpec((nb, c, dk_), lambda s, ch=ch: (0, ch(s), 0)),
                     pl.BlockSpec((nb, c, dv_), lambda s, lch=lch: (0, lch(s), 0)),
                     pl.BlockSpec((1, nb * HEADS, hv, hk), lambda s: (ns - 1 - s, 0, 0, 0))]
        args += [pb3, pb3, pv3, (b_b if rev else b_f), do3, (zs_b if rev else zs_f)]
        for w, dt in ((dk_, BF16), (dk_, BF16), (dv_, BF16), (dk_, F32)):
            out_specs.append(pl.BlockSpec((nb, c, w), lambda s, ch=ch: (0, ch(s), 0)))
            out_shape.append(jax.ShapeDtypeStruct((nb, l_len, w), dt))
    return pl.pallas_call(
        body, name="gla_bwd", grid=(ns,), in_specs=in_specs, out_specs=tuple(out_specs), out_shape=tuple(out_shape),
        scratch_shapes=[pltpu.VMEM((2 * nb * HEADS, hv, hk), F32)],
        compiler_params=_params())(*args)


def _tail(a1, pa, o_f, o_b, x2, tgt, mod, wc, wg, wo, ln_g, ln_b, gn_t, fg, nb, tm, n_split):
    tl, d = x2.shape
    nt = tl // tm
    per_ex = nt // nb
    hv = d // HEADS
    nrow = mod.shape[0]

    def part(shared, a1_ref, z_ref, r_ref, mc_ref, mg_ref, of_ref, ob_ref, x_ref, t_ref,
             dp_ref, da1_ref, do_ref, gx_ref, mrg_ref, dmo_ref, yci_ref, dyc_ref, ogi_ref, dyg_ref, sm_ref):
        bidx, gate, lng, lnb, fgv, gn, wc_, wg_, wo_ = shared

        a1v = a1_ref[...]
        mu = jnp.mean(a1v, axis=-1, keepdims=True)
        xc = a1v - mu
        rs = lax.rsqrt(jnp.mean(xc * xc, axis=-1, keepdims=True) + EPS)
        xh = xc * rs
        a2 = xh * lng + lnb
        s2 = _sigmoid(a2)
        a3 = a2 * s2
        zv = z_ref[...]
        sz = _sigmoid(zv)
        siluz = zv * sz
        ycin = a3 * siluz
        yconv = _mm(ycin, wc_)

        o = of_ref[...] + ob_ref[...]
        ohat_parts, rn_parts = [], []
        for h in range(HEADS):
            oh = o[:, h * hv:(h + 1) * hv]
            rn = lax.rsqrt(jnp.mean(oh * oh, axis=-1, keepdims=True) + EPS)
            ohat_parts.append(oh * rn)
            rn_parts.append(rn)
        ohat = jnp.concatenate(ohat_parts, axis=1)
        on = ohat * gn
        rv = r_ref[...]
        sr = _sigmoid(rv)
        silur = rv * sr
        ogin = on * silur
        ygla = _mm(ogin, wg_)

        sc = _sigmoid(mc_ref[...])
        sg = _sigmoid(mg_ref[...])
        merged = sc * yconv + sg * ygla
        mo = _mm(merged, wo_)
        hn = x_ref[...] + gate * mo
        rf = lax.rsqrt(jnp.mean(hn * hn, axis=-1, keepdims=True) + EPS)
        yh = hn * rf
        err = yh * fgv - t_ref[...]
        loss_part = 0.5 * jnp.sum(err * err) * (1.0 / d)

        dy = err * (1.0 / d)
        dfg = jnp.sum(dy * yh, axis=0, keepdims=True)
        dyh = dy * fgv
        dhn = rf * (dyh - yh * jnp.mean(dyh * yh, axis=-1, keepdims=True))
        gx_ref[...] = dhn
        dgate = jnp.sum(dhn * mo, axis=0, keepdims=True)
        dmo = gate * dhn
        dmerged = _mm_nt(dmo, wo_)
        dyconv = dmerged * sc
        dygla = dmerged * sg
        dp_ref[:, 2 * d:3 * d] = (dmerged * yconv * sc * (1.0 - sc)).astype(BF16)
        dp_ref[:, 3 * d:4 * d] = (dmerged * ygla * sg * (1.0 - sg)).astype(BF16)
        dycin = _mm_nt(dyconv, wc_)
        dogin = _mm_nt(dygla, wg_)
        mrg_ref[...] = merged.astype(BF16)
        dmo_ref[...] = dmo.astype(BF16)
        yci_ref[...] = ycin.astype(BF16)
        dyc_ref[...] = dyconv.astype(BF16)
        ogi_ref[...] = ogin.astype(BF16)
        dyg_ref[...] = dygla.astype(BF16)

        da3 = dycin * siluz
        dp_ref[:, 0:d] = (dycin * a3 * _dsilu(zv, sz)).astype(BF16)
        da2 = da3 * _dsilu(a2, s2)
        dlng = jnp.sum(da2 * xh, axis=0, keepdims=True)
        dlnb = jnp.sum(da2, axis=0, keepdims=True)
        dxh = da2 * lng
        da1_ref[...] = rs * (dxh - jnp.mean(dxh, axis=-1, keepdims=True)
                             - xh * jnp.mean(dxh * xh, axis=-1, keepdims=True))

        don = dogin * silur
        dp_ref[:, d:2 * d] = (dogin * on * _dsilu(rv, sr)).astype(BF16)
        dgn = jnp.sum(don * ohat, axis=0, keepdims=True)
        dyn = don * gn
        for h in range(HEADS):
            vs = slice(h * hv, (h + 1) * hv)
            oh_hat = ohat_parts[h]
            dh = dyn[:, vs]
            do_ref[:, vs] = (rn_parts[h] * (dh - oh_hat * jnp.mean(dh * oh_hat, axis=-1, keepdims=True))
                             ).astype(BF16)

        sm_ref[0:1, :] += dfg
        sm_ref[1:2, :] += dlng
        sm_ref[2:3, :] += dlnb
        sm_ref[3:4, :] += dgn
        sm_ref[4:5, :] += jnp.zeros((1, d), F32) + loss_part
        for b in range(nb):
            sm_ref[8 + b:9 + b, :] += jnp.where(bidx == b, dgate, 0.0)

    def body(*refs):
        mod_ref, wc_ref, wg_ref, wo_ref, lng_ref, lnb_ref, gn_ref, fg_ref = refs[9:17]
        sm_ref = refs[27]
        i = pl.program_id(0)

        @pl.when(i == 0)
        def _():
            sm_ref[...] = jnp.zeros_like(sm_ref)

        bidx = i // per_ex
        shared = (bidx, _rowsel(mod_ref[...], bidx, nb)[:, 2 * d:3 * d], lng_ref[...], lnb_ref[...], fg_ref[...],
                  jnp.concatenate([gn_ref[...]] * HEADS, axis=1), wc_ref[...], wg_ref[...], wo_ref[...])
        rows_per = tm // n_split
        for p in range(n_split):
            rows = pl.ds(p * rows_per, rows_per)
            part(shared, *[r.at[rows] for r in refs[0:9]], *[r.at[rows] for r in refs[17:27]], sm_ref)

    row = pl.BlockSpec((tm, d), lambda i: (i, 0))
    pcol = lambda blk: pl.BlockSpec((tm, d), lambda i: (i, blk))
    full = lambda arr: pl.BlockSpec(arr.shape, lambda i: (0,) * arr.ndim)
    bfo = jax.ShapeDtypeStruct((tl, d), BF16)
    f32o = jax.ShapeDtypeStruct((tl, d), F32)
    return pl.pallas_call(
        body, name="tail", grid=(nt,),
        in_specs=[row, pcol(2), pcol(3), pcol(4), pcol(5), row, row, row, row, full(mod), full(wc), full(wg),
                  full(wo), full(ln_g), full(ln_b), full(gn_t), full(fg)],
        out_specs=(pl.BlockSpec((tm, 4 * d), lambda i: (i, 0)), row, row, row, row, row, row, row, row, row,
                   pl.BlockSpec((16, d), lambda i: (0, 0))),
        out_shape=(jax.ShapeDtypeStruct((tl, 4 * d), BF16), f32o, bfo, f32o, bfo, bfo, bfo, bfo, bfo, bfo,
                   jax.ShapeDtypeStruct((16, d), F32)),
        compiler_params=_params())(a1, pa, pa, pa, pa, o_f, o_b, x2, tgt, mod, wc, wg, wo, ln_g, ln_b, gn_t, fg)


def _local_step(x, c, ctx, tgt, c_ctx, ada_w8, ada_b, norm_g, w_a, b_a, w_b, b_b, conv_w8, conv_b, ln_g, ln_b,
                up2, bias2, gla_norm_g, final_norm_g, proj, on_grads=None, on_du_a1=None):
    nb, s_len, d = x.shape
    c_len = ctx.shape[1]
    dk_, dv_ = d // 2, d
    tl, tc = nb * s_len, nb * c_len
    nbw = 2 * dk_ + dv_ + LANE
    tm = math.gcd(256, c_len)
    tiles = _Tiles(nb, s_len, c_len, tm, 2)
    tmm = tiles.big * tm
    l_len = tiles.rows_per_ex
    t_all = nb * l_len
    x2, ctx2, tgt2 = x.reshape(tl, d), ctx.reshape(tc, d), tgt.reshape(tl, d)

    cv = jnp.zeros((8, d), F32).at[0:nb].set(c).at[nb].set(c_ctx.reshape(d))
    mod = _ada_fwd(cv, ada_w8, ada_b)
    u = _norm_fwd(x2, ctx2, mod, norm_g, tiles)
    pa = _matmul_bias("inproj_a", u, w_a, b_a, tl, tmm, _tile(6 * d, 3072), tiles.big_all_of_lat)
    pb, pv = _inproj_b(u, w_b, b_b, tmm, dk_, dv_)

    a1 = _conv_fwd(pa, conv_w8, conv_b, nb, s_len)
    lr_blk = (2 * dk_) // LANE
    g_all = _decay_fwd(pb, up2, bias2, tm, lr_blk)
    pb3, pv3 = pb.reshape(nb, l_len, 2 * dk_ + LANE), pv.reshape(nb, l_len, dv_)
    o_f, zs_f, b_f, o_b, zs_b, b_b2 = _gla_fwd(pb3, pv3, g_all.reshape(nb, l_len, 2 * dk_), nb, s_len, c_len,
                                               dk_, dv_)

    conv_proj, gla_proj, w_out = proj(a1) if callable(proj) else proj
    tt = math.gcd(256, s_len)
    (dp_a2, da1, d_o, gx1, merged, dmo, ycin, dyconv, ogin, dygla, small) = _tail(
        a1, pa, o_f.reshape(tl, dv_), o_b.reshape(tl, dv_), x2, tgt2, mod, conv_proj, gla_proj, w_out, ln_g, ln_b,
        gla_norm_g, final_norm_g, nb, tt, 2)

    lat3 = lambda a: a.reshape(nb, s_len, a.shape[-1])
    tnw = _tile(d, 1024)
    tnp = _tile(d, 512)
    d_w_out, _ = _matmul_tn_whole("dw_out", lat3(merged), lat3(dmo), s_len, tnp, False)
    d_conv_proj, _ = _matmul_tn_whole("dw_conv_proj", lat3(ycin), lat3(dyconv), s_len, tnp, False)
    d_gla_proj, _ = _matmul_tn_whole("dw_gla_proj", lat3(ogin), lat3(dygla), s_len, tnp, False)

    dp_a1, d_conv_w8, d_conv_b = _conv_bwd(pa, da1, conv_w8, nb, s_len)
    gl = _gla_bwd(pb3, pv3, d_o.reshape(nb, s_len, dv_), (zs_f, b_f, zs_b, b_b2), nb, s_len, c_len, dk_, dv_)
    gl = [g_.reshape(t_all, g_.shape[-1]) for g_ in gl]
    dp_b, d_up2, d_bias2 = _decay_bwd(pb, up2, bias2, gl[0:4], gl[4:8], tiles, lr_blk, dk_, dv_)

    u3 = u.reshape(nb, l_len, d)
    dw_a1, db_a1 = _matmul_tn_whole("dw_a1", u3, lat3(dp_a1), s_len, tnw, True)
    dw_a2, db_a2 = _matmul_tn_whole("dw_a2", u3, lat3(dp_a2), s_len, tnw, True)
    dw_b, db_b = _matmul_tn("dw_b", u, dp_b, t_all, tmm, nbw)
    grads = dict(w_a1=dw_a1, w_a2=dw_a2, w_b=dw_b, conv_w8=d_conv_w8, conv_proj=d_conv_proj, up2=d_up2,
                 gla_proj=d_gla_proj, w_out=d_w_out)

    tka = _tile(2 * d, 2048)
    du_a1 = _matmul_nt("du_a1", dp_a1, w_a, 0, tmm, tka, after=on_grads(grads) if on_grads else ())
    du_a2 = _matmul_nt("du_a2", dp_a2, w_a, (2 * d) // tka, tmm, tka, after=on_du_a1(du_a1) if on_du_a1 else ())
    du_b = _matmul_nt("du_b", dp_b, w_b, 0, tmm, nbw)
    grad_x2, dmod_ss, d_norm_g = _norm_bwd(x2, ctx2, mod, norm_g, [du_a1, du_a2], du_b, gx1, tiles)
    d_ada_w8, d_ada_b, d_cv = _ada_bwd(cv, ada_w8, dmod_ss, small, nb)

    return dict(
        grads, grad_x=grad_x2.reshape(nb, s_len, d), small=small, cv=d_cv, ada_w8=d_ada_w8, ada_b=d_ada_b,
        norm_g=d_norm_g, b_a1=db_a1, b_a2=db_a2, b_b=db_b, conv_b=d_conv_b, bias2=d_bias2)


def _regroup_pieces(d, r, wshard):
    cb = d // N_DEV
    segs = []
    for j in range(N_DEV):
        segs.append((j * cb, cb, 0, 2 * j * cb))
    for j in range(N_DEV):
        segs.append((d + j * cb, cb, 0, (2 * j + 1) * cb))
    segs += [(2 * d, d, 0, 2 * d), (3 * d, 2 * d + 2 * r, 1, 0), (5 * d + 2 * r, 3 * d, 0, 3 * d)]
    pieces = []
    for o0, w, dst, d0 in segs:
        lo = o0
        while lo < o0 + w:
            j = lo // wshard
            hi = min(o0 + w, (j + 1) * wshard)
            pieces.append((j, lo - j * wshard, hi - lo, dst, d0 + lo - o0))
            lo = hi
    return pieces


def _regroup(o, d, r):
    n_in = 8 * d + 2 * r
    parts = ([], [])
    for _, s0, n, dst, _ in sorted(_regroup_pieces(d, r, n_in), key=lambda p: (p[3], p[4])):
        parts[dst].append(o[..., s0:s0 + n])
    pad = jnp.zeros(o.shape[:-1] + (LANE - 2 * r,), o.dtype)
    return jnp.concatenate(parts[0], axis=-1), jnp.concatenate(parts[1] + [pad], axis=-1)


def _unshard_w_in(g_win, d, r, after=()):
    n_sh, _, ws = g_win.shape
    nbw = 2 * d + LANE
    pieces = _regroup_pieces(d, r, ws)
    tr = math.gcd(d, 256)

    def body(g_ref, *rest):
        a_ref, b_ref = rest[len(after):]
        dsts = (a_ref, b_ref)
        for j, s0, n, dst, d0 in pieces:
            dsts[dst][:, pl.ds(d0, n)] = g_ref[j, :, pl.ds(s0, n)]
        b_ref[:, pl.ds(2 * d + 2 * r, LANE - 2 * r)] = jnp.zeros((tr, LANE - 2 * r), b_ref.dtype)

    return pl.pallas_call(
        body, name="unshard_w_in", grid=(d // tr,),
        in_specs=[pl.BlockSpec((n_sh, tr, ws), lambda i: (0, i, 0))] + [_ANY] * len(after),
        out_specs=(pl.BlockSpec((tr, 6 * d), lambda i: (i, 0)), pl.BlockSpec((tr, nbw), lambda i: (i, 0))),
        out_shape=(jax.ShapeDtypeStruct((d, 6 * d), g_win.dtype), jax.ShapeDtypeStruct((d, nbw), g_win.dtype)),
        compiler_params=_params())(g_win, *after)


def _reshard_w_in(dwt_a1, dwt_a2, dwt_b, d, r):
    ws = (8 * d + 2 * r) // N_DEV
    pieces = _regroup_pieces(d, r, ws)
    tc = math.gcd(d, 256)

    def body(a1_ref, a2_ref, b_ref, o_ref):
        for j, s0, n, dst, d0 in pieces:
            if dst == 1:
                src = b_ref[pl.ds(d0, n), :]
            elif d0 < 2 * d:
                src = a1_ref[pl.ds(d0, n), :]
            else:
                src = a2_ref[pl.ds(d0 - 2 * d, n), :]
            o_ref[j, pl.ds(s0, n), :] = src

    col = lambda h: pl.BlockSpec((h, tc), lambda i: (0, i))
    return pl.pallas_call(
        body, name="reshard_w_in", grid=(d // tc,),
        in_specs=[col(2 * d), col(4 * d), col(2 * d + LANE)],
        out_specs=pl.BlockSpec((N_DEV, ws, tc), lambda i: (0, 0, i)),
        out_shape=jax.ShapeDtypeStruct((N_DEV, ws, d), dwt_b.dtype),
        compiler_params=_params())(dwt_a1, dwt_a2, dwt_b)


_SMALL = ("c_ctx", "ada_b", "norm_g", "b_in", "conv_b", "conv_ln_g", "conv_ln_b", "decay_bias_fwd",
          "decay_bias_bwd", "gla_norm_g", "final_norm_g")


def _small_layout(d, r):
    sizes = dict(c_ctx=d, ada_b=3 * d, norm_g=d, b_in=8 * d + 2 * r, conv_b=d, conv_ln_g=d, conv_ln_b=d,
                 decay_bias_fwd=d // 2, decay_bias_bwd=d // 2, gla_norm_g=d // HEADS, final_norm_g=d, loss=1)
    table, off = {}, 0
    for name in _SMALL + ("loss",):
        table[name] = (off, sizes[name])
        off += -(-sizes[name] // LANE) * LANE
    return table, off


def _pack_small(g, nb, d, r):
    table, width = _small_layout(d, r)
    hv = d // HEADS
    pieces = _regroup_pieces(d, r, 8 * d + 2 * r)
    names = ("small", "cv", "ada_b", "norm_g", "b_a1", "b_a2", "b_b", "conv_b", "bias2")

    def body(sm, cv, ab, ng, ba1, ba2, bb, cvb, b2, o_ref):
        o_ref[...] = jnp.zeros_like(o_ref)

        def put(name, val):
            off, n = table[name]
            o_ref[:, pl.ds(off, n)] = val

        put("c_ctx", cv[nb:nb + 1, :])
        put("ada_b", ab[...])
        put("norm_g", ng[...])
        off_b = table["b_in"][0]
        for _, s0, n, dst, d0 in pieces:
            if dst == 1:
                src = bb[:, pl.ds(d0, n)]
            elif d0 < 2 * d:
                src = ba1[:, pl.ds(d0, n)]
            else:
                src = ba2[:, pl.ds(d0 - 2 * d, n)]
            o_ref[:, pl.ds(off_b + s0, n)] = src
        put("conv_b", cvb[...])
        put("conv_ln_g", sm[1:2, :])
        put("conv_ln_b", sm[2:3, :])
        put("decay_bias_fwd", b2[:, 0:d // 2])
        put("decay_bias_bwd", b2[:, d // 2:d])
        gn = sm[3:4, 0:hv]
        for h in range(1, HEADS):
            gn = gn + sm[3:4, h * hv:(h + 1) * hv]
        put("gla_norm_g", gn)
        put("final_norm_g", sm[0:1, :])
        put("loss", sm[4:5, 0:1])

    return pl.pallas_call(body, name="pack_small", out_shape=jax.ShapeDtypeStruct((1, width), F32),
                          compiler_params=_params())(*[g[k] for k in names])


def _small_adam(parts, ws, ms, vs, d, r):
    table, width = _small_layout(d, r)
    n_parts = parts.shape[0]
    k = len(_SMALL)
    bc1 = 1.0 - ADAM_B1 ** ADAM_STEP
    bc2 = 1.0 - ADAM_B2 ** ADAM_STEP

    def body(p_ref, *refs):
        w_refs, m_refs, v_refs = refs[0:k], refs[k:2 * k], refs[2 * k:3 * k]
        outs = refs[3 * k:]
        tot = p_ref[0]
        for i in range(1, n_parts):
            tot = tot + p_ref[i]
        for i, name in enumerate(_SMALL):
            off, n = table[name]
            g = tot[:, off:off + n]
            mn = ADAM_B1 * m_refs[i][...] + (1.0 - ADAM_B1) * g
            vn = ADAM_B2 * v_refs[i][...] + (1.0 - ADAM_B2) * (g * g)
            outs[i][...] = g
            outs[k + i][...] = -ADAM_LR * ((mn / bc1) / (jnp.sqrt(vn / bc2) + ADAM_EPS) + ADAM_WD * w_refs[i][...])
            outs[2 * k + i][...] = mn
            outs[3 * k + i][...] = vn
        off, _ = table["loss"]
        outs[4 * k][...] = tot[:, off:off + 1]

    shapes = [jax.ShapeDtypeStruct(w.shape, F32) for w in ws]
    res = pl.pallas_call(body, name="small_adam", out_shape=tuple(shapes * 4 + [jax.ShapeDtypeStruct((1, 1), F32)]),
                         compiler_params=_params())(parts, *ws, *ms, *vs)
    return res[0:k], res[k:2 * k], res[2 * k:3 * k], res[3 * k:4 * k], res[4 * k]


def _mesh_pos():
    return lax.axis_index("x"), lax.axis_index("y"), lax.axis_index("c")


def _all_gather(arrs):
    n = len(arrs)
    ns = 9
    split = [a.ndim == 2 and a.shape[0] % 32 == 0 for a in arrs]

    def body(*refs):
        ins, outs = refs[:n], refs[n:2 * n]
        send_sems, recv_sems, local_sems = refs[2 * n:]
        x, y, c = _mesh_pos()
        me, sibling = (x, y, c), (x, y, 1 - c)
        xn, yn, dg = (1 - x, y, c), (x, 1 - y, c), (1 - x, 1 - y, c)
        other = lambda pos: (pos[0], pos[1], 1 - c)

        def slot(a, pos, half):
            ref = outs[a].at[4 * pos[0] + 2 * pos[1] + pos[2]]
            if half is None:
                return ref
            rows = arrs[a].shape[0] // 2
            return ref.at[pl.ds(half * rows, rows)]

        def copy(a, k, block, to, src=None, half=None):
            dst = slot(a, block, half)
            return pltpu.make_async_remote_copy(
                src_ref=dst if src is None else src, dst_ref=dst,
                send_sem=send_sems.at[ns * a + k], recv_sem=recv_sems.at[ns * a + k],
                device_id=to, device_id_type=MESH)

        h0 = lambda a: 0 if split[a] else None
        mine = [pltpu.make_async_copy(ins[a], slot(a, me, None), local_sems.at[a]) for a in range(n)]
        for cp in mine:
            cp.start()
        sent = []
        for a in range(n):
            sent += [copy(a, 0, me, sibling, src=ins[a]), copy(a, 1, me, xn, src=ins[a]),
                     copy(a, 2, me, yn, src=ins[a])]
        for cp in sent:
            cp.start()

        def pass_on(cp):
            cp.start()
            sent.append(cp)

        for a in range(n):
            copy(a, 1, xn, me).wait_recv()
            pass_on(copy(a, 3, xn, sibling))
            pass_on(copy(a, 4, xn, yn, half=h0(a)))
        for a in range(n):
            copy(a, 2, yn, me).wait_recv()
            pass_on(copy(a, 5, yn, sibling))
            if split[a]:
                pass_on(copy(a, 6, yn, xn, half=1))
        for a in range(n):
            copy(a, 4, dg, me, half=h0(a)).wait_recv()
            pass_on(copy(a, 7, dg, sibling, half=h0(a)))
            if split[a]:
                copy(a, 6, dg, me, half=1).wait_recv()
                pass_on(copy(a, 8, dg, sibling, half=1))
        for a in range(n):
            copy(a, 0, sibling, me).wait_recv()
            copy(a, 3, other(xn), me).wait_recv()
            copy(a, 5, other(yn), me).wait_recv()
            copy(a, 7, other(dg), me, half=h0(a)).wait_recv()
            if split[a]:
                copy(a, 8, other(dg), me, half=1).wait_recv()
        for cp in sent:
            cp.wait_send()
        for cp in mine:
            cp.wait()

    return pl.pallas_call(
        body, name="all_gather",
        out_shape=tuple(jax.ShapeDtypeStruct((N_DEV,) + a.shape, a.dtype) for a in arrs),
        in_specs=[_ANY] * n, out_specs=tuple([_ANY] * n),
        scratch_shapes=[pltpu.SemaphoreType.DMA((ns * n,)), pltpu.SemaphoreType.DMA((ns * n,)),
                        pltpu.SemaphoreType.DMA((n,))],
    )(*arrs)


def _exchange_sibling(arrs):
    n = len(arrs)

    def body(*refs):
        ins, outs = refs[:n], refs[n:2 * n]
        send_sems, recv_sems = refs[2 * n:]
        x, y, c = _mesh_pos()
        copies = [pltpu.make_async_remote_copy(
            src_ref=ins[a].at[2 * k + (1 - c)], dst_ref=outs[a].at[k],
            send_sem=send_sems.at[4 * a + k], recv_sem=recv_sems.at[4 * a + k],
            device_id=(x, y, 1 - c), device_id_type=MESH) for a in range(n) for k in range(4)]
        for cp in copies:
            cp.start()
        for cp in copies:
            cp.wait_recv()
        for cp in copies:
            cp.wait_send()

    return pl.pallas_call(
        body, name="grad_exchange_sibling",
        out_shape=tuple(jax.ShapeDtypeStruct((4,) + a.shape[1:], a.dtype) for a in arrs),
        in_specs=[_ANY] * n, out_specs=tuple([_ANY] * n),
        scratch_shapes=[pltpu.SemaphoreType.DMA((4 * n,)), pltpu.SemaphoreType.DMA((4 * n,))],
    )(*arrs)


def _elementwise_tile(r, cdim):
    if r % 8 == 0 and r > 256:
        return math.gcd(r, 256), cdim
    if r > 256 and cdim % 256 == 0:
        return r, 256
    return r, cdim


def _pair_sum(name, mine, theirs):
    _, r, cdim = mine.shape
    tr, tc = _elementwise_tile(r, cdim)

    def body(m_ref, t_ref, o_ref):
        c = lax.axis_index("c")
        own = jnp.where(c == 0, m_ref[:, 0].astype(F32), m_ref[:, 1].astype(F32))
        o_ref[...] = (own + t_ref[...].astype(F32)).astype(o_ref.dtype)

    return pl.pallas_call(
        body, name=name, grid=(r // tr, cdim // tc),
        in_specs=[pl.BlockSpec((4, 2, tr, tc), lambda i, j: (0, 0, i, j)),
                  pl.BlockSpec((4, tr, tc), lambda i, j: (0, i, j))],
        out_specs=pl.BlockSpec((4, tr, tc), lambda i, j: (0, i, j)),
        out_shape=jax.ShapeDtypeStruct((4, r, cdim), mine.dtype),
        compiler_params=_params())(mine.reshape(4, 2, r, cdim), theirs)


def _exchange_chips(arrs):
    n = len(arrs)

    def body(*refs):
        ins, outs = refs[:n], refs[n:2 * n]
        send_sems, recv_sems, local_sems = refs[2 * n:]
        x, y, c = _mesh_pos()
        my_chip = 2 * x + y
        mine = [pltpu.make_async_copy(ins[a].at[my_chip], outs[a].at[my_chip], local_sems.at[a]) for a in range(n)]
        for cp in mine:
            cp.start()
        copies = []
        for rel in range(1, 4):
            px = 1 - x if rel & 2 else x
            py = 1 - y if rel & 1 else y
            for a in range(n):
                copies.append(pltpu.make_async_remote_copy(
                    src_ref=ins[a].at[2 * px + py], dst_ref=outs[a].at[my_chip],
                    send_sem=send_sems.at[3 * a + rel - 1], recv_sem=recv_sems.at[3 * a + rel - 1],
                    device_id=(px, py, c), device_id_type=MESH))
        for cp in copies:
            cp.start()
        for cp in copies:
            cp.wait_recv()
        for cp in copies:
            cp.wait_send()
        for cp in mine:
            cp.wait()

    return pl.pallas_call(
        body, name="grad_exchange_chips",
        out_shape=tuple(jax.ShapeDtypeStruct(a.shape, a.dtype) for a in arrs),
        in_specs=[_ANY] * n, out_specs=tuple([_ANY] * n),
        scratch_shapes=[pltpu.SemaphoreType.DMA((3 * n,)), pltpu.SemaphoreType.DMA((3 * n,)),
                        pltpu.SemaphoreType.DMA((n,))],
    )(*arrs)


_HBM = pl.BlockSpec(memory_space=pltpu.HBM)
_SEM = pl.BlockSpec(memory_space=pltpu.SEMAPHORE)


def _copies_start(name, srcs, lands, make_copies, n_sems):
    n, m = len(srcs), len(lands)

    def body(*refs):
        ins = refs[:n + m]
        send_sems, recv_sems = refs[n + m], refs[n + m + 1]
        for cp in make_copies(ins[:n], ins[n:], send_sems, recv_sems):
            cp.start()
        refs[-1][...] = jnp.zeros_like(refs[-1])

    res = pl.pallas_call(
        body, name=name,
        out_shape=(pltpu.SemaphoreType.DMA((n_sems,)), pltpu.SemaphoreType.DMA((n_sems,)),
                   *[pltpu.HBM(a.shape, a.dtype) for a in (*srcs, *lands)], jax.ShapeDtypeStruct((8, LANE), F32)),
        in_specs=[_HBM] * (n + m),
        out_specs=(_SEM, _SEM, *[_HBM] * (n + m), pl.BlockSpec(memory_space=pltpu.VMEM)),
        input_output_aliases={i: 2 + i for i in range(n + m)},
        compiler_params=pltpu.CompilerParams(has_side_effects=pltpu.SideEffectType.DATAFLOW_SIDE_EFFECTING),
    )(*[pltpu.with_memory_space_constraint(a, pltpu.HBM) for a in (*srcs, *lands)])
    return res[0], res[1], res[2:2 + n], res[2 + n:2 + n + m], res[-1]


def _copies_wait(name, started, after, make_copies):
    send_sems, recv_sems, srcs, lands, _ = started
    n, m = len(srcs), len(lands)

    def body(*refs):
        ins = refs[:n + m]
        for cp in make_copies(ins[:n], ins[n:], refs[n + m], refs[n + m + 1]):
            cp.wait_send()
            cp.wait_recv()

    res = pl.pallas_call(
        body, name=name,
        out_shape=tuple(pltpu.HBM(a.shape, a.dtype) for a in (*srcs, *lands)),
        in_specs=[_HBM] * (n + m) + [_SEM, _SEM] + [_ANY] * len(after),
        out_specs=tuple([_HBM] * (n + m)),
        input_output_aliases={i: i for i in range(n + m)},
        compiler_params=pltpu.CompilerParams(has_side_effects=pltpu.SideEffectType.DATAFLOW_SIDE_EFFECTING),
    )(*srcs, *lands, send_sems, recv_sems, *after)
    return res[:n], res[n:]


def _gather_copies(srcs, lands, send_sems, recv_sems):
    x, y, c = _mesh_pos()
    me_i = 4 * x + 2 * y + c
    copies = []
    for rel in range(1, N_DEV):
        peer = (1 - x if rel & 4 else x, 1 - y if rel & 2 else y, 1 - c if rel & 1 else c)
        for a in range(len(srcs)):
            copies.append(pltpu.make_async_remote_copy(
                src_ref=srcs[a], dst_ref=lands[a].at[me_i], send_sem=send_sems.at[7 * a + rel - 1],
                recv_sem=recv_sems.at[7 * a + rel - 1], device_id=peer, device_id_type=MESH))
    return copies


def _sibling_copies(srcs, lands, send_sems, recv_sems):
    x, y, c = _mesh_pos()
    return [pltpu.make_async_remote_copy(
        src_ref=srcs[a].at[2 * k + (1 - c)], dst_ref=lands[a].at[k], send_sem=send_sems.at[4 * a + k],
        recv_sem=recv_sems.at[4 * a + k], device_id=(x, y, 1 - c), device_id_type=MESH)
        for a in range(len(srcs)) for k in range(4)]


def _chip_copies(srcs, lands, send_sems, recv_sems):
    x, y, c = _mesh_pos()
    my_chip = 2 * x + y
    copies = []
    for rel in range(1, 4):
        px = 1 - x if rel & 2 else x
        py = 1 - y if rel & 1 else y
        for a in range(len(srcs)):
            copies.append(pltpu.make_async_remote_copy(
                src_ref=srcs[a].at[2 * px + py], dst_ref=lands[a].at[my_chip], send_sem=send_sems.at[3 * a + rel - 1],
                recv_sem=recv_sems.at[3 * a + rel - 1], device_id=(px, py, c), device_id_type=MESH))
    return copies


def _sum_adam(name, parts, w, m, v, own=None):
    unit_mid = w.ndim == 3
    _, r, cdim = parts.shape
    n_parts = parts.shape[0]
    tr, tc = _elementwise_tile(r, cdim)
    bc1 = 1.0 - ADAM_B1 ** ADAM_STEP
    bc2 = 1.0 - ADAM_B2 ** ADAM_STEP
    extra = [] if own is None else [own]

    def body(p_ref, *refs):
        w_ref, m_ref, v_ref, g_ref, d_ref, nm_ref, nv_ref = refs[len(extra):]
        if own is None:
            part = lambda k: p_ref[k].astype(F32)
        else:
            my_chip = 2 * lax.axis_index("x") + lax.axis_index("y")
            part = lambda k: jnp.where(my_chip == k, refs[0][k], p_ref[k]).astype(F32)
        g = part(0)
        for k in range(1, n_parts):
            g = g + part(k)
        if unit_mid:
            g = g.reshape(tr, 1, tc)
        mn = ADAM_B1 * m_ref[...] + (1.0 - ADAM_B1) * g
        vn = ADAM_B2 * v_ref[...] + (1.0 - ADAM_B2) * (g * g)
        g_ref[...] = g
        nm_ref[...] = mn
        nv_ref[...] = vn
        d_ref[...] = -ADAM_LR * ((mn / bc1) / (jnp.sqrt(vn / bc2) + ADAM_EPS) + ADAM_WD * w_ref[...])

    blk = (pl.BlockSpec((tr, 1, tc), lambda i, j: (i, 0, j)) if unit_mid
           else pl.BlockSpec((tr, tc), lambda i, j: (i, j)))
    o = jax.ShapeDtypeStruct(w.shape, F32)
    return pl.pallas_call(
        body, name=name, grid=(r // tr, cdim // tc),
        in_specs=[pl.BlockSpec((n_parts, tr, tc), lambda i, j: (0, i, j))] * (1 + len(extra)) + [blk, blk, blk],
        out_specs=(blk, blk, blk, blk), out_shape=(o, o, o, o),
        compiler_params=_params())(parts, *extra, w, m, v)


_WEIGHTS = ("c_ctx", "ada_w", "ada_b", "norm_g", "w_in", "b_in", "conv_w", "conv_b", "conv_ln_g", "conv_ln_b",
            "conv_proj", "decay_up_fwd", "decay_bias_fwd", "decay_up_bwd", "decay_bias_bwd", "gla_norm_g",
            "gla_proj", "w_out", "final_norm_g")


def _as2d(a):
    if a.ndim == 1:
        return a.reshape(1, -1)
    return a.reshape(-1, a.shape[-1])


def kernel(x, c, ctx, c_ctx, ada_w, ada_b, norm_g, w_in, b_in, conv_w, conv_b, conv_ln_g, conv_ln_b, conv_proj, decay_up_fwd, decay_bias_fwd, decay_up_bwd, decay_bias_bwd, gla_norm_g, gla_proj, w_out, final_norm_g, loss_target, m_c_ctx, m_ada_w, m_ada_b, m_norm_g, m_w_in, m_b_in, m_conv_w, m_conv_b, m_conv_ln_g, m_conv_ln_b, m_conv_proj, m_decay_up_fwd, m_decay_bias_fwd, m_decay_up_bwd, m_decay_bias_bwd, m_gla_norm_g, m_gla_proj, m_w_out, m_final_norm_g, v_c_ctx, v_ada_w, v_ada_b, v_norm_g, v_w_in, v_b_in, v_conv_w, v_conv_b, v_conv_ln_g, v_conv_ln_b, v_conv_proj, v_decay_up_fwd, v_decay_bias_fwd, v_decay_up_bwd, v_decay_bias_bwd, v_gla_norm_g, v_gla_proj, v_w_out, v_final_norm_g):
    env = dict(locals())
    wts = {k: env[k] for k in _WEIGHTS}
    d = x.shape[-1]
    r = decay_up_fwd.shape[1]
    dk_ = d // 2
    n_in = w_in.shape[-1] * N_DEV

    ds, dks = d // N_DEV, dk_ // N_DEV
    g_win, g_ada, conv_w8, g_up = _all_gather(
        [w_in[0].astype(BF16), ada_w[0].astype(BF16), conv_w[0],
         jnp.concatenate([decay_up_fwd[0], decay_up_bwd[0]], axis=1)])
    proj_own = [conv_proj[0].astype(BF16), gla_proj[0].astype(BF16), w_out[0].astype(BF16)]
    me_i = 4 * lax.axis_index("x") + 2 * lax.axis_index("y") + lax.axis_index("c")
    proj_lands = [lax.dynamic_update_slice(lax.empty((N_DEV,) + a.shape, a.dtype), a[None], (me_i, 0, 0))
                  for a in proj_own]
    proj_start = _copies_start("proj_gather_start", proj_own, proj_lands, _gather_copies, 7 * 3)

    def proj(after):
        _, lands = _copies_wait("proj_gather_wait", proj_start, (after,), _gather_copies)
        return [w.reshape(d, d) for w in lands]

    w_a, w_b = _unshard_w_in(g_win, d, r, after=(proj_start[4],))
    up_f = g_up[:, :, 0:dks].transpose(1, 0, 2).reshape(r, dk_)
    up_b = g_up[:, :, dks:].transpose(1, 0, 2).reshape(r, dk_)
    up2 = jnp.zeros((LANE, 2 * dk_), F32).at[0:r, 0:dk_].set(up_f).at[r:2 * r, dk_:].set(up_b)
    bias2 = jnp.concatenate([decay_bias_fwd, decay_bias_bwd], axis=1)
    b_a, b_b = _regroup(b_in, d, r)

    names = ("w_in", "conv_proj", "gla_proj", "w_out", "conv_w", "decay_up")
    comm = {}

    def on_grads(gr):
        d_up = jnp.concatenate([gr["up2"][0:r, 0:dk_].reshape(r, N_DEV, dks).transpose(1, 0, 2),
                                gr["up2"][r:2 * r, dk_:].reshape(r, N_DEV, dks).transpose(1, 0, 2)], axis=2)
        mine = [_reshard_w_in(gr["w_a1"], gr["w_a2"], gr["w_b"], d, r), gr["conv_proj"].reshape(N_DEV, ds, d),
                gr["gla_proj"].reshape(N_DEV, ds, d), gr["w_out"].reshape(N_DEV, ds, d), gr["conv_w8"], d_up]
        lands = [lax.empty((4,) + a.shape[1:], a.dtype) for a in mine]
        comm["sibling"] = _copies_start("grad_sibling_start", mine, lands, _sibling_copies, 4 * len(mine))
        return (comm["sibling"][4],)

    def on_du_a1(du_a1):
        mine, theirs = _copies_wait("grad_sibling_wait", comm["sibling"], (du_a1,), _sibling_copies)
        sums = [_pair_sum("pair_sum_" + nm, a, b) for nm, a, b in zip(names, mine, theirs)]
        lands = [lax.empty(a.shape, a.dtype) for a in sums]
        comm["chips"] = _copies_start("grad_chips_start", sums, lands, _chip_copies, 3 * len(sums))
        return (comm["chips"][4],)

    g = _local_step(x, c, ctx, loss_target, c_ctx, g_ada, ada_b, norm_g[0:1], w_a, b_a, w_b, b_b,
                    conv_w8, conv_b, conv_ln_g, conv_ln_b, up2, bias2, gla_norm_g, final_norm_g.reshape(1, d),
                    proj, on_grads, on_du_a1)

    (their_ada,) = _exchange_sibling([g["ada_w8"]])
    ada_sum = _pair_sum("pair_sum_ada_w", g["ada_w8"], their_ada)
    ada_start = _copies_start("ada_chips_start", [ada_sum], [lax.empty(ada_sum.shape, ada_sum.dtype)],
                              _chip_copies, 3)
    own, landed = _copies_wait("grad_chips_wait", comm["chips"], (ada_start[4],), _chip_copies)
    o_win, o_cp, o_gp, o_wo, o_cw, o_up = own
    x_win, x_cp, x_gp, x_wo, x_cw, x_up = landed

    (packs,) = _all_gather([_pack_small(g, x.shape[0], d, r)])
    row = lambda a: a.reshape(1, -1)
    sg, sd, sm, sv, loss = _small_adam(packs, [row(wts[k]) for k in _SMALL], [row(env["m_" + k]) for k in _SMALL],
                                       [row(env["v_" + k]) for k in _SMALL], d, r)
    out = {}
    for i, k in enumerate(_SMALL):
        for pre, arrs in (("grad_", sg), ("delta_", sd), ("new_m_", sm), ("new_v_", sv)):
            out[pre + k] = arrs[i].reshape(wts[k].shape)
    loss = loss.reshape(())

    def big(name, parts, wname, own=None):
        w2 = _as2d(wts[wname])
        res = _sum_adam(name, parts, w2, _as2d(env["m_" + wname]), _as2d(env["v_" + wname]), own)
        for pre, arr in zip(("grad_", "delta_", "new_m_", "new_v_"), res):
            out[pre + wname] = arr.reshape(wts[wname].shape)

    as_rows = lambda a: jnp.transpose(a, (2, 0, 1))
    res = _sum_adam("adam_w_in", x_win, as_rows(w_in), as_rows(m_w_in), as_rows(v_w_in), o_win)
    for pre, arr in zip(("grad_", "delta_", "new_m_", "new_v_"), res):
        out[pre + "w_in"] = jnp.transpose(arr, (1, 2, 0))
    big("adam_conv_proj", x_cp, "conv_proj", o_cp)
    big("adam_gla_proj", x_gp, "gla_proj", o_gp)
    big("adam_w_out", x_wo, "w_out", o_wo)
    big("adam_conv_w", x_cw, "conv_w", o_cw)
    big("adam_up_f", x_up[:, :, 0:dks], "decay_up_fwd", o_up[:, :, 0:dks])
    big("adam_up_b", x_up[:, :, dks:], "decay_up_bwd", o_up[:, :, dks:])
    (o_ada,), (x_ada,) = _copies_wait("ada_chips_wait", ada_start, (res[0], out["grad_w_out"], out["grad_b_in"]),
                                      _chip_copies)
    big("adam_ada_w", x_ada, "ada_w", o_ada)

    return (loss, g["grad_x"], *[out["grad_" + k] for k in _WEIGHTS], *[out["delta_" + k] for k in _WEIGHTS],
            *[out["new_m_" + k] for k in _WEIGHTS], *[out["new_v_" + k] for k in _WEIGHTS])
```

```python
import functools
import math

import jax
import jax.numpy as jnp
from jax import lax
from jax.experimental import pallas as pl
from jax.experimental.pallas import tpu as pltpu

F32 = jnp.float32
BF16 = jnp.bfloat16
MESH = pl.DeviceIdType.MESH

N_DEV = 8
GRID_W = 64
CHUNK = 128
HEADS = 4
EPS = 1e-6
GATE_TAU = 16.0
LANE = 128
ADAM_LR, ADAM_B1, ADAM_B2, ADAM_EPS, ADAM_WD, ADAM_STEP = 0.001, 0.9, 0.999, 1e-08, 0.01, 10
VMEM_LIMIT = 60 * 1024 * 1024
_ANY = pl.BlockSpec(memory_space=pl.ANY)


def _params(**kw):
    return pltpu.CompilerParams(vmem_limit_bytes=VMEM_LIMIT, **kw)


def _tile(n, pref):
    t = (min(pref, n) // LANE) * LANE
    while t >= LANE:
        if n % t == 0:
            return t
        t -= LANE
    return n


def _mm(a, b):
    return jnp.dot(a.astype(BF16), b.astype(BF16), preferred_element_type=F32)


def _mm_nt(a, b):
    return lax.dot_general(a.astype(BF16), b.astype(BF16), (((1,), (1,)), ((), ())), preferred_element_type=F32)


def _mm_tn(a, b):
    return lax.dot_general(a.astype(BF16), b.astype(BF16), (((0,), (0,)), ((), ())), preferred_element_type=F32)


def _mm_tn_hi(a, b):
    return lax.dot_general(a, b, (((0,), (0,)), ((), ())), precision=lax.Precision.HIGHEST, preferred_element_type=F32)


def _sigmoid(x):
    return 0.5 * jnp.tanh(0.5 * x) + 0.5


def _dsilu(x, s):
    return s * (1.0 + x * (1.0 - s))


def _rowsel(table, idx, n):
    out = table[0:1, :]
    for r in range(1, n):
        out = jnp.where(idx == r, table[r:r + 1, :], out)
    return out


def _ada_fwd(cv, ada_w8, ada_b):
    n_sh, _, ws = ada_w8.shape

    def body(cv_ref, w_ref, b_ref, o_ref):
        c = cv_ref[...]
        sv = c * _sigmoid(c)
        for j in range(n_sh):
            cols = pl.ds(j * ws, ws)
            o_ref[:, cols] = _mm(sv, w_ref[j]) + b_ref[:, cols]

    return pl.pallas_call(body, name="ada_fwd", out_shape=jax.ShapeDtypeStruct((cv.shape[0], n_sh * ws), F32),
                          compiler_params=_params())(cv, ada_w8, ada_b)


def _ada_bwd(cv, ada_w8, dmod_ss, small, nb):
    n_sh, d, ws = ada_w8.shape

    def body(cv_ref, w_ref, dm_ref, sm_ref, dw_ref, db_ref, dc_ref):
        c = cv_ref[...]
        s = _sigmoid(c)
        sv = c * s
        dm = jnp.concatenate([dm_ref[:, 0:2 * d], sm_ref[8:16, :]], axis=1)
        db_ref[...] = jnp.sum(dm, axis=0, keepdims=True)
        dsv = None
        for j in range(n_sh):
            dmj = dm[:, j * ws:(j + 1) * ws]
            dw_ref[j] = _mm_tn_hi(sv, dmj).astype(dw_ref.dtype)
            part = _mm_nt(dmj, w_ref[j])
            dsv = part if dsv is None else dsv + part
        dc_ref[...] = dsv * _dsilu(c, s)

    return pl.pallas_call(
        body, name="ada_bwd",
        out_shape=(jax.ShapeDtypeStruct((n_sh, d, ws), BF16), jax.ShapeDtypeStruct((1, n_sh * ws), F32),
                   jax.ShapeDtypeStruct(cv.shape, F32)),
        compiler_params=_params())(cv, ada_w8, dmod_ss, small)


class _Tiles:
    def __init__(self, nb, s_len, c_len, tm, big):
        self.nb, self.tm, self.big = nb, tm, big
        self.lat, self.ctx = s_len // tm, c_len // tm
        self.pad = -(self.lat + self.ctx) % big
        self.per_ex = self.lat + self.ctx + self.pad
        self.n_all = nb * self.per_ex
        self.rows_per_ex = self.per_ex * tm

    def is_lat(self, i):
        return i % self.per_ex < self.lat

    def is_pad(self, i):
        return i % self.per_ex >= self.lat + self.ctx

    def lat_of_all(self, i):
        return (i // self.per_ex) * self.lat + jnp.minimum(i % self.per_ex, self.lat - 1)

    def ctx_of_all(self, i):
        return (i // self.per_ex) * self.ctx + jnp.clip(i % self.per_ex - self.lat, 0, self.ctx - 1)

    def big_all_of_lat(self, t):
        lat_big = self.lat // self.big
        return (t // lat_big) * (self.per_ex // self.big) + t % lat_big


def _norm_fwd(x2, ctx2, mod, norm_g, tiles):
    tl, d = x2.shape
    tc = ctx2.shape[0]
    nb, tm = tiles.nb, tiles.tm

    def body(x_ref, c_ref, mod_ref, g_ref, u_ref):
        i = pl.program_id(0)
        lat = tiles.is_lat(i)
        xv = jnp.where(lat, x_ref[...], c_ref[...])
        row = jnp.where(lat, i // tiles.per_ex, nb)
        m = _rowsel(mod_ref[...], row, nb + 1)
        shift, scale = m[:, 0:d], m[:, d:2 * d]
        rstd = lax.rsqrt(jnp.mean(xv * xv, axis=-1, keepdims=True) + EPS)
        u = xv * rstd * g_ref[...] * (1.0 + scale) + shift
        u_ref[...] = jnp.where(tiles.is_pad(i), 0.0, u).astype(BF16)

    return pl.pallas_call(
        body, name="norm_fwd", grid=(tiles.n_all,),
        in_specs=[pl.BlockSpec((tm, d), lambda i: (tiles.lat_of_all(i), 0)),
                  pl.BlockSpec((tm, d), lambda i: (tiles.ctx_of_all(i), 0)),
                  pl.BlockSpec(mod.shape, lambda i: (0, 0)),
                  pl.BlockSpec((1, d), lambda i: (0, 0))],
        out_specs=pl.BlockSpec((tm, d), lambda i: (i, 0)),
        out_shape=jax.ShapeDtypeStruct((tiles.n_all * tm, d), BF16),
        compiler_params=_params())(x2, ctx2, mod, norm_g)


def _norm_bwd(x2, ctx2, mod, norm_g, du_lat, du_b, gx1, tiles):
    tl, d = x2.shape
    nb, tm = tiles.nb, tiles.tm
    nrow = mod.shape[0]
    n_lat_in = len(du_lat)

    def body(x_ref, c_ref, mod_ref, g_ref, *refs):
        dl_refs = refs[:n_lat_in]
        d3_ref, gx_ref, gxo_ref, dmod_ref, dg_ref = refs[n_lat_in:]
        i = pl.program_id(0)

        @pl.when(i == 0)
        def _():
            dmod_ref[...] = jnp.zeros_like(dmod_ref)
            dg_ref[...] = jnp.zeros_like(dg_ref)

        lat = tiles.is_lat(i)
        xv = jnp.where(lat, x_ref[...], c_ref[...])
        row = jnp.where(lat, i // tiles.per_ex, nb)
        m = _rowsel(mod_ref[...], row, nb + 1)
        scale = m[:, d:2 * d]
        g = g_ref[...]
        dl = dl_refs[0][...].astype(F32)
        for ref in dl_refs[1:]:
            dl = dl + ref[...].astype(F32)
        du = jnp.where(tiles.is_pad(i), 0.0, d3_ref[...].astype(F32) + jnp.where(lat, dl, 0.0))
        rstd = lax.rsqrt(jnp.mean(xv * xv, axis=-1, keepdims=True) + EPS)
        xh = xv * rstd
        dshift = jnp.sum(du, axis=0, keepdims=True)
        dscale = jnp.sum(du * xh * g, axis=0, keepdims=True)
        dxn = du * (1.0 + scale)
        dg_ref[...] += jnp.sum(dxn * xh, axis=0, keepdims=True)
        dxh = dxn * g
        dx = rstd * (dxh - xh * jnp.mean(dxh * xh, axis=-1, keepdims=True))

        @pl.when(lat)
        def _():
            gxo_ref[...] = dx + gx_ref[...]

        for r in range(nb + 1):
            dmod_ref[r:r + 1, 0:d] += jnp.where(row == r, dshift, 0.0)
            dmod_ref[r:r + 1, d:2 * d] += jnp.where(row == r, dscale, 0.0)

    lat_map = lambda i: (tiles.lat_of_all(i), 0)
    lat_spec = pl.BlockSpec((tm, d), lat_map)
    return pl.pallas_call(
        body, name="norm_bwd", grid=(tiles.n_all,),
        in_specs=[lat_spec,
                  pl.BlockSpec((tm, d), lambda i: (tiles.ctx_of_all(i), 0)),
                  pl.BlockSpec(mod.shape, lambda i: (0, 0)),
                  pl.BlockSpec((1, d), lambda i: (0, 0))]
                 + [lat_spec] * n_lat_in
                 + [pl.BlockSpec((tm, d), lambda i: (i, 0)), lat_spec],
        out_specs=(lat_spec,
                   pl.BlockSpec((nrow, 3 * d), lambda i: (0, 0)),
                   pl.BlockSpec((1, d), lambda i: (0, 0))),
        out_shape=(jax.ShapeDtypeStruct((tl, d), F32), jax.ShapeDtypeStruct((nrow, 3 * d), F32),
                   jax.ShapeDtypeStruct((1, d), F32)),
        compiler_params=_params())(x2, ctx2, mod, norm_g, *du_lat, du_b, gx1)


def _matmul_bias(name, u, w, b, rows, tm, tn, u_tile):
    d, n = w.shape

    def body(u_ref, w_ref, b_ref, o_ref):
        o_ref[...] = jnp.dot(u_ref[...], w_ref[...], preferred_element_type=F32) + b_ref[...]

    return pl.pallas_call(
        body, name=name, grid=(n // tn, rows // tm),
        in_specs=[pl.BlockSpec((tm, d), lambda j, i: (u_tile(i), 0)),
                  pl.BlockSpec((d, tn), lambda j, i: (0, j)),
                  pl.BlockSpec((1, tn), lambda j, i: (0, j))],
        out_specs=pl.BlockSpec((tm, tn), lambda j, i: (i, j)),
        out_shape=jax.ShapeDtypeStruct((rows, n), F32),
        compiler_params=_params())(u, w, b)


def _inproj_b(u, w_b, b_b, tm, dk_, dv_):
    t_all, d = u.shape
    nbw = w_b.shape[1]

    def body(u_ref, w_ref, b_ref, qk_ref, v_ref):
        full = jnp.dot(u_ref[...], w_ref[...], preferred_element_type=F32) + b_ref[...]
        qk_ref[:, 0:2 * dk_] = full[:, 0:2 * dk_]
        qk_ref[:, 2 * dk_:2 * dk_ + LANE] = full[:, 2 * dk_ + dv_:nbw]
        v_ref[...] = full[:, 2 * dk_:2 * dk_ + dv_].astype(BF16)

    return pl.pallas_call(
        body, name="inproj_b", grid=(t_all // tm,),
        in_specs=[pl.BlockSpec((tm, d), lambda i: (i, 0)), pl.BlockSpec((d, nbw), lambda i: (0, 0)),
                  pl.BlockSpec((1, nbw), lambda i: (0, 0))],
        out_specs=(pl.BlockSpec((tm, 2 * dk_ + LANE), lambda i: (i, 0)), pl.BlockSpec((tm, dv_), lambda i: (i, 0))),
        out_shape=(jax.ShapeDtypeStruct((t_all, 2 * dk_ + LANE), F32), jax.ShapeDtypeStruct((t_all, dv_), BF16)),
        compiler_params=_params())(u, w_b, b_b)


def _matmul_nt(name, a, w, koff, tm, tk, after=()):
    r, kc = a.shape
    d = w.shape[0]
    nk = kc // tk

    def body(a_ref, w_ref, *rest):
        o_ref = rest[len(after)]
        k = pl.program_id(1)
        p = lax.dot_general(a_ref[...], w_ref[...], (((1,), (1,)), ((), ())), preferred_element_type=F32)
        if nk == 1:
            o_ref[...] = p.astype(o_ref.dtype)
            return
        acc_ref = rest[len(after) + 1]

        @pl.when(k == 0)
        def _():
            acc_ref[...] = p

        @pl.when(k > 0)
        def _():
            acc_ref[...] += p

        @pl.when(k == nk - 1)
        def _():
            o_ref[...] = acc_ref[...].astype(o_ref.dtype)

    return pl.pallas_call(
        body, name=name, grid=(r // tm, nk),
        in_specs=[pl.BlockSpec((tm, tk), lambda i, k: (i, k)),
                  pl.BlockSpec((d, tk), lambda i, k: (0, koff + k))] + [_ANY] * len(after),
        out_specs=pl.BlockSpec((tm, d), lambda i, k: (i, 0)),
        out_shape=jax.ShapeDtypeStruct((r, d), BF16),
        scratch_shapes=[pltpu.VMEM((tm, d), F32)] if nk > 1 else [],
        compiler_params=_params())(a, w, *after)


def _matmul_tn(name, a, b, rows, tk, tn):
    m = a.shape[1]
    n = b.shape[1]
    nk = rows // tk

    def body(a_ref, b_ref, o_ref, s_ref, acc_ref):
        k = pl.program_id(1)
        bv = b_ref[...]
        p = lax.dot_general(bv, a_ref[...], (((0,), (0,)), ((), ())), preferred_element_type=F32)
        cs = jnp.sum(bv.astype(F32), axis=0, keepdims=True)

        @pl.when(k == 0)
        def _():
            acc_ref[...] = p
            s_ref[...] = cs

        @pl.when(k > 0)
        def _():
            acc_ref[...] += p
            s_ref[...] += cs

        @pl.when(k == nk - 1)
        def _():
            o_ref[...] = acc_ref[...].astype(o_ref.dtype)

    return pl.pallas_call(
        body, name=name, grid=(n // tn, nk),
        in_specs=[pl.BlockSpec((tk, m), lambda j, k: (k, 0)),
                  pl.BlockSpec((tk, tn), lambda j, k: (k, j))],
        out_specs=(pl.BlockSpec((tn, m), lambda j, k: (j, 0)), pl.BlockSpec((1, tn), lambda j, k: (0, j))),
        out_shape=(jax.ShapeDtypeStruct((n, m), BF16), jax.ShapeDtypeStruct((1, n), F32)),
        scratch_shapes=[pltpu.VMEM((tn, m), F32)],
        compiler_params=_params())(a, b)


def _matmul_tn_whole(name, a3, b3, rows, tn, transposed):
    nb, _, m = a3.shape
    n = b3.shape[2]

    def body(a_ref, b_ref, o_ref, s_ref):
        p, cs = None, None
        for e in range(nb):
            bv = b_ref[e]
            lhs, rhs = (bv, a_ref[e]) if transposed else (a_ref[e], bv)
            pe = lax.dot_general(lhs, rhs, (((0,), (0,)), ((), ())), preferred_element_type=F32)
            ce = jnp.sum(bv.astype(F32), axis=0, keepdims=True)
            p, cs = (pe, ce) if p is None else (p + pe, cs + ce)
        o_ref[...] = p.astype(o_ref.dtype)
        s_ref[...] = cs

    o_spec, o_shape = ((pl.BlockSpec((tn, m), lambda j: (j, 0)), (n, m)) if transposed
                       else (pl.BlockSpec((m, tn), lambda j: (0, j)), (m, n)))
    return pl.pallas_call(
        body, name=name, grid=(n // tn,),
        in_specs=[pl.BlockSpec((nb, rows, m), lambda j: (0, 0, 0)),
                  pl.BlockSpec((nb, rows, tn), lambda j: (0, 0, j))],
        out_specs=(o_spec, pl.BlockSpec((1, tn), lambda j: (0, j))),
        out_shape=(jax.ShapeDtypeStruct(o_shape, BF16), jax.ShapeDtypeStruct((1, n), F32)),
        compiler_params=_params())(a3, b3)


def _conv_window(pad_ref, r, shift, ktaps, width, horizontal):
    if horizontal:
        return pad_ref[r, pl.ds(16 + shift, width), :]
    return pad_ref[r + ktaps // 2 + shift]


def _conv_row(pad_ref, w, r, ktaps, width, horizontal, flip):
    half = ktaps // 2
    acc = None
    for t in range(ktaps):
        win = _conv_window(pad_ref, r, (half - t) if flip else (t - half), ktaps, width, horizontal)
        term = win * w[t:t + 1, :]
        acc = term if acc is None else acc + term
    return acc


def _fill_padded(ref, val, rows, width, ktaps, horizontal):
    half_k = ktaps // 2
    cb = val.shape[-1]
    if horizontal:
        ref[:, 0:16, :] = jnp.zeros((rows, 16, cb), F32)
        ref[:, 16 + width:32 + width, :] = jnp.zeros((rows, 16, cb), F32)
        ref[:, 16:16 + width, :] = val
    else:
        ref[0:half_k, :, :] = jnp.zeros((half_k, width, cb), F32)
        ref[half_k + rows:2 * half_k + rows, :, :] = jnp.zeros((half_k, width, cb), F32)
        ref[half_k:half_k + rows, :, :] = val


def _conv_fwd(pa, conv_w8, conv_b, nb, s):
    nblk, ktaps, cb = conv_w8.shape
    d = nblk * cb
    rows, width = s // GRID_W, GRID_W
    half_k = ktaps // 2
    nh = nblk // 2

    def body(glu_ref, w_ref, b_ref, o_ref, ph_ref, pv_ref):
        j = pl.program_id(1)
        a0 = (glu_ref[:, 0:cb] * _sigmoid(glu_ref[:, cb:2 * cb])).reshape(rows, width, cb)
        w = w_ref[...]

        bias = b_ref[...]

        def run(pad_ref, horizontal):
            _fill_padded(pad_ref, a0, rows, width, ktaps, horizontal)

            def row(r, carry):
                at = pl.ds(pl.multiple_of(r * width, width), width)
                o_ref[at, :] = _conv_row(pad_ref, w, r, ktaps, width, horizontal, False) + bias
                return carry

            lax.fori_loop(0, rows, row, 0)

        @pl.when(j < nh)
        def _():
            run(ph_ref, True)

        @pl.when(j >= nh)
        def _():
            run(pv_ref, False)

    return pl.pallas_call(
        body, name="conv_fwd", grid=(nb, nblk),
        in_specs=[pl.BlockSpec((s, 2 * cb), lambda b, j: (b, j)),
                  pl.BlockSpec((None, ktaps, cb), lambda b, j: (j, 0, 0)),
                  pl.BlockSpec((1, cb), lambda b, j: (0, j))],
        out_specs=pl.BlockSpec((s, cb), lambda b, j: (b, j)),
        out_shape=jax.ShapeDtypeStruct((nb * s, d), F32),
        scratch_shapes=[pltpu.VMEM((rows, width + 32, cb), F32), pltpu.VMEM((rows + 2 * half_k, width, cb), F32)],
        compiler_params=_params())(pa, conv_w8, conv_b)


def _conv_bwd(pa, da1, conv_w8, nb, s):
    nblk, ktaps, cb = conv_w8.shape
    d = nblk * cb
    rows, width = s // GRID_W, GRID_W
    half_k = ktaps // 2
    nh = nblk // 2

    def body(glu_ref, da_ref, w_ref, dp_ref, dw_ref, db_ref, pha_ref, phd_ref, pva_ref, pvd_ref):
        j = pl.program_id(0)
        b = pl.program_id(1)
        a0 = (glu_ref[:, 0:cb] * _sigmoid(glu_ref[:, cb:2 * cb])).reshape(rows, width, cb)
        da1v = da_ref[...]
        d3 = da1v.reshape(rows, width, cb)
        w = w_ref[...]

        @pl.when(b == 0)
        def _():
            dw_ref[...] = jnp.zeros_like(dw_ref)
            db_ref[...] = jnp.zeros_like(db_ref)

        db_ref[...] += jnp.sum(da1v, axis=0, keepdims=True)

        def run(pa_ref, pd_ref, horizontal):
            _fill_padded(pa_ref, a0, rows, width, ktaps, horizontal)
            _fill_padded(pd_ref, d3, rows, width, ktaps, horizontal)

            def row(r, accs):
                at = pl.ds(pl.multiple_of(r * width, width), width)
                da0 = _conv_row(pd_ref, w, r, ktaps, width, horizontal, True)
                gv = glu_ref[at, 0:cb]
                sg = _sigmoid(glu_ref[at, cb:2 * cb])
                dp_ref[at, 0:cb] = (da0 * sg).astype(BF16)
                dp_ref[at, cb:2 * cb] = (da0 * gv * sg * (1.0 - sg)).astype(BF16)
                d_row = da_ref[at, :]
                out = []
                for t in range(ktaps):
                    prod = _conv_window(pa_ref, r, t - half_k, ktaps, width, horizontal) * d_row
                    out.append(accs[t] + jnp.sum(prod.reshape(width // 8, 8, cb), axis=0))
                return tuple(out)

            accs = lax.fori_loop(0, rows, row, tuple(jnp.zeros((8, cb), F32) for _ in range(ktaps)))
            for t in range(ktaps):
                dw_ref[t:t + 1, :] += jnp.sum(accs[t], axis=0, keepdims=True)

        @pl.when(j < nh)
        def _():
            run(pha_ref, phd_ref, True)

        @pl.when(j >= nh)
        def _():
            run(pva_ref, pvd_ref, False)

    return pl.pallas_call(
        body, name="conv_bwd", grid=(nblk, nb),
        in_specs=[pl.BlockSpec((s, 2 * cb), lambda j, b: (b, j)),
                  pl.BlockSpec((s, cb), lambda j, b: (b, j)),
                  pl.BlockSpec((None, ktaps, cb), lambda j, b: (j, 0, 0))],
        out_specs=(pl.BlockSpec((s, 2 * cb), lambda j, b: (b, j)),
                   pl.BlockSpec((None, ktaps, cb), lambda j, b: (j, 0, 0)),
                   pl.BlockSpec((1, cb), lambda j, b: (0, j))),
        out_shape=(jax.ShapeDtypeStruct((nb * s, 2 * d), BF16),
                   jax.ShapeDtypeStruct((nblk, ktaps, cb), F32), jax.ShapeDtypeStruct((1, d), F32)),
        scratch_shapes=[pltpu.VMEM((rows, width + 32, cb), F32), pltpu.VMEM((rows, width + 32, cb), F32),
                        pltpu.VMEM((rows + 2 * half_k, width, cb), F32),
                        pltpu.VMEM((rows + 2 * half_k, width, cb), F32)],
        compiler_params=_params())(pa, da1, conv_w8)


def _log_sigmoid(x):
    return jnp.minimum(x, 0.0) - jnp.log(1.0 + jnp.exp(-jnp.abs(x)))


def _decay_fwd(pb, up2, bias2, tm, lr_blk):
    t_all = pb.shape[0]
    n2 = up2.shape[1]

    def body(lr_ref, up_ref, b_ref, g_ref):
        logits = _mm(lr_ref[...], up_ref[...]) + b_ref[...]
        g_ref[...] = _log_sigmoid(logits) * (1.0 / GATE_TAU)

    return pl.pallas_call(
        body, name="decay_fwd", grid=(t_all // tm,),
        in_specs=[pl.BlockSpec((tm, LANE), lambda i: (i, lr_blk)),
                  pl.BlockSpec(up2.shape, lambda i: (0, 0)),
                  pl.BlockSpec((1, n2), lambda i: (0, 0))],
        out_specs=pl.BlockSpec((tm, n2), lambda i: (i, 0)),
        out_shape=jax.ShapeDtypeStruct((t_all, n2), F32),
        compiler_params=_params())(pb, up2, bias2)


def _decay_bwd(pb, up2, bias2, grads_f, grads_b, tiles, lr_blk, dk_, dv_):
    t_all = pb.shape[0]
    tm = tiles.tm
    n2 = up2.shape[1]
    nbw = 2 * dk_ + dv_ + LANE

    def body(lr_ref, up_ref, b_ref, dqf, dkf, dvf, dgf, dqb, dkb, dvb, dgb, dp_ref, dup_ref, dbias_ref):
        i = pl.program_id(0)
        pad = tiles.is_pad(i)
        live = lambda v: jnp.where(pad, 0.0, v)

        @pl.when(i == 0)
        def _():
            dup_ref[...] = jnp.zeros_like(dup_ref)
            dbias_ref[...] = jnp.zeros_like(dbias_ref)

        lr = lr_ref[...]
        up = up_ref[...]
        logits = _mm(lr, up) + b_ref[...]
        dg = live(jnp.concatenate([dgf[...], dgb[...]], axis=1))
        dlog = dg * (1.0 / GATE_TAU) * _sigmoid(-logits)
        dup_ref[...] += _mm_tn(lr, dlog)
        dbias_ref[...] += jnp.sum(dlog, axis=0, keepdims=True)
        both = lambda f, b: live(f[...].astype(F32) + b[...].astype(F32)).astype(BF16)
        dp_ref[:, 0:dk_] = both(dqf, dqb)
        dp_ref[:, dk_:2 * dk_] = both(dkf, dkb)
        dp_ref[:, 2 * dk_:2 * dk_ + dv_] = both(dvf, dvb)
        dp_ref[:, 2 * dk_ + dv_:nbw] = _mm_nt(dlog, up).astype(BF16)

    row = lambda w: pl.BlockSpec((tm, w), lambda i: (i, 0))
    return pl.pallas_call(
        body, name="decay_bwd", grid=(t_all // tm,),
        in_specs=[pl.BlockSpec((tm, LANE), lambda i: (i, lr_blk)),
                  pl.BlockSpec(up2.shape, lambda i: (0, 0)),
                  pl.BlockSpec((1, n2), lambda i: (0, 0)),
                  row(dk_), row(dk_), row(dv_), row(dk_), row(dk_), row(dk_), row(dv_), row(dk_)],
        out_specs=(row(nbw), pl.BlockSpec(up2.shape, lambda i: (0, 0)), pl.BlockSpec((1, n2), lambda i: (0, 0))),
        out_shape=(jax.ShapeDtypeStruct((t_all, nbw), BF16), jax.ShapeDtypeStruct(up2.shape, F32),
                   jax.ShapeDtypeStruct((1, n2), F32)),
        compiler_params=_params())(pb, up2, bias2, *grads_f, *grads_b)


def _scan_chunk(s, nl, nc, rev):
    if rev:
        return jnp.where(s < nc, nl + (nc - 1 - s), nl - 1 - (s - nc))
    return jnp.where(s < nc, nl + s, s - nc)


def _scan_lat_chunk(s, nl, nc, rev):
    first = nl - 1 if rev else 0
    return jnp.where(s < nc, first, _scan_chunk(s, nl, nc, rev))


def _tri_mm(m_bf, x):
    hi = x.astype(BF16)
    r1 = x - hi.astype(F32)
    mid = r1.astype(BF16)
    lo = (r1 - mid.astype(F32)).astype(BF16)
    dot = lambda p: jnp.dot(m_bf, p, preferred_element_type=F32)
    return dot(hi) + dot(mid) + dot(lo)


def _chunk_masks(c, rev):
    ii = lax.broadcasted_iota(jnp.int32, (c, c), 0)
    jj = lax.broadcasted_iota(jnp.int32, (c, c), 1)
    return ((ii <= jj), (ii >= jj)) if rev else ((ii >= jj), (ii <= jj))


def _chunk_terms(q, k, b, far, mid):
    bf, bm = b[far:far + 1, :], b[mid:mid + 1, :]
    e = jnp.exp(b)
    em = jnp.exp(b - bm)
    eim = jnp.exp(bm - b)
    ed = jnp.exp(bf - b)
    return dict(e=e, em=em, eim=eim, ed=ed, dec=jnp.exp(bf), qe=q * e, qem=q * em, kim=k * eim, kd=k * ed)


def _gla_fwd(pb3, pv3, g3, nb, s_len, c_len, dk_, dv_):
    c = CHUNK
    nl, nc = s_len // c, c_len // c
    ns = nl + nc
    hk, hv = dk_ // HEADS, dv_ // HEADS
    l_len = pb3.shape[1]
    scale = hk ** -0.5
    mid = c // 2

    def body(*refs):
        ins, outs, z_scr = refs[:8], refs[8:14], refs[14]
        s = pl.program_id(0)

        @pl.when(s == 0)
        def _():
            z_scr[...] = jnp.zeros_like(z_scr)

        qs = jnp.where(s >= nc, scale, 0.0)
        for di, rev in enumerate((False, True)):
            q_ref, k_ref, v_ref, g_ref = ins[4 * di:4 * di + 4]
            o_ref, zs_ref, b_ref = outs[3 * di:3 * di + 3]
            mask, _ = _chunk_masks(c, rev)
            m_bf = mask.astype(BF16)
            far = 0 if rev else c - 1
            for b in range(nb):
                bc = _tri_mm(m_bf, g_ref[b])
                b_ref[b] = bc
                for h in range(HEADS):
                    ks, vs = slice(h * hk, (h + 1) * hk), slice(h * hv, (h + 1) * hv)
                    zi = (di * nb + b) * HEADS + h
                    v = v_ref[b, :, vs]
                    t = _chunk_terms(q_ref[b, :, ks] * qs, k_ref[b, :, ks], bc[:, ks], far, mid)
                    a = jnp.where(mask, _mm_nt(t["qem"], t["kim"]), 0.0)
                    z = z_scr[zi]
                    zs_ref[0, b * HEADS + h] = z
                    o_ref[b, :, vs] = _mm(a, v) + _mm_nt(t["qe"], z)
                    z_scr[zi] = z * t["dec"] + _mm_tn(v, t["kd"])

    in_specs, out_specs, out_shape = [], [], []
    for di, rev in enumerate((False, True)):
        ch = functools.partial(_scan_chunk, nl=nl, nc=nc, rev=rev)
        lch = functools.partial(_scan_lat_chunk, nl=nl, nc=nc, rev=rev)
        in_specs += [pl.BlockSpec((nb, c, dk_), lambda s, ch=ch: (0, ch(s), 0)),
                     pl.BlockSpec((nb, c, dk_), lambda s, ch=ch: (0, ch(s), 1)),
                     pl.BlockSpec((nb, c, dv_), lambda s, ch=ch: (0, ch(s), 0)),
                     pl.BlockSpec((nb, c, dk_), lambda s, ch=ch, di=di: (0, ch(s), di))]
        out_specs += [pl.BlockSpec((nb, c, dv_), lambda s, lch=lch: (0, lch(s), 0)),
                      pl.BlockSpec((1, nb * HEADS, hv, hk), lambda s: (s, 0, 0, 0)),
                      pl.BlockSpec((nb, c, dk_), lambda s, ch=ch: (0, ch(s), 0))]
        out_shape += [jax.ShapeDtypeStruct((nb, s_len, dv_), F32),
                      jax.ShapeDtypeStruct((ns, nb * HEADS, hv, hk), F32),
                      jax.ShapeDtypeStruct((nb, l_len, dk_), F32)]
    return pl.pallas_call(
        body, name="gla_fwd", grid=(ns,), in_specs=in_specs, out_specs=tuple(out_specs), out_shape=tuple(out_shape),
        scratch_shapes=[pltpu.VMEM((2 * nb * HEADS, hv, hk), F32)],
        compiler_params=_params())(pb3, pb3, pv3, g3, pb3, pb3, pv3, g3)


def _gla_bwd(pb3, pv3, do3, fwd_saved, nb, s_len, c_len, dk_, dv_):
    c = CHUNK
    nl, nc = s_len // c, c_len // c
    ns = nl + nc
    hk, hv = dk_ // HEADS, dv_ // HEADS
    l_len = pb3.shape[1]
    scale = hk ** -0.5
    mid = c // 2
    zs_f, b_f, zs_b, b_b = fwd_saved

    def body(*refs):
        ins, outs, dz_scr = refs[:12], refs[12:20], refs[20]
        s = pl.program_id(0)
        step = ns - 1 - s

        @pl.when(s == 0)
        def _():
            dz_scr[...] = jnp.zeros_like(dz_scr)

        lat = step >= nc
        qs = jnp.where(lat, scale, 0.0)
        dmul = jnp.where(lat, 1.0, 0.0)
        for di, rev in enumerate((False, True)):
            q_ref, k_ref, v_ref, b_ref, do_ref, zs_ref = ins[6 * di:6 * di + 6]
            dq_ref, dk_ref, dv_ref, dg_ref = outs[4 * di:4 * di + 4]
            mask, mask_t = _chunk_masks(c, rev)
            mt_bf = mask_t.astype(BF16)
            far = 0 if rev else c - 1
            far_row = lax.broadcasted_iota(jnp.int32, (c, hk), 0) == far
            for b in range(nb):
                db_parts = []
                for h in range(HEADS):
                    ks, vs = slice(h * hk, (h + 1) * hk), slice(h * hv, (h + 1) * hv)
                    zi = (di * nb + b) * HEADS + h
                    v = v_ref[b, :, vs]
                    d_o = do_ref[b, :, vs] * dmul
                    t = _chunk_terms(q_ref[b, :, ks] * qs, k_ref[b, :, ks], b_ref[b, :, ks], far, mid)
                    qem, kim, qe, kd = t["qem"], t["kim"], t["qe"], t["kd"]
                    a_t = jnp.where(mask_t, _mm_nt(kim, qem), 0.0)
                    d_a = jnp.where(mask, _mm_nt(d_o, v), 0.0)
                    d_at = jnp.where(mask_t, _mm_nt(v, d_o), 0.0)
                    z = zs_ref[0, b * HEADS + h]
                    dzn = dz_scr[zi]
                    dv_ref[b, :, vs] = (_mm(a_t, d_o) + _mm_nt(kd, dzn)).astype(dv_ref.dtype)
                    dqem = _mm(d_a, kim)
                    dkim = _mm(d_at, qem)
                    dqe = _mm(d_o, z)
                    dkd = _mm(v, dzn)
                    ddec = jnp.sum(z * dzn, axis=0, keepdims=True)
                    dz_scr[zi] = dzn * t["dec"] + _mm_tn(d_o, qe)
                    dq_ref[b, :, ks] = ((dqem * t["em"] + dqe * t["e"]) * qs).astype(dq_ref.dtype)
                    dk_ref[b, :, ks] = (dkim * t["eim"] + dkd * t["ed"]).astype(dk_ref.dtype)
                    db = dqem * qem - dkim * kim + dqe * qe - dkd * kd
                    extra = jnp.sum(dkd * kd, axis=0, keepdims=True) + ddec * t["dec"]
                    db_parts.append(db + jnp.where(far_row, extra, 0.0))
                dg_ref[b] = _tri_mm(mt_bf, jnp.concatenate(db_parts, axis=1))

    in_specs, out_specs, out_shape, args = [], [], [], []
    for di, rev in enumerate((False, True)):
        ch = lambda s, rev=rev: _scan_chunk(ns - 1 - s, nl, nc, rev)
        lch = lambda s, rev=rev: _scan_lat_chunk(ns - 1 - s, nl, nc, rev)
        in_specs += [pl.BlockSpec((nb, c, dk_), lambda s, ch=ch: (0, ch(s), 0)),
                     pl.BlockSpec((nb, c, dk_), lambda s, ch=ch: (0, ch(s), 1)),
                     pl.BlockSpec((nb, c, dv_), lambda s, ch=ch: (0, ch(s), 0)),
                     pl.BlockSpec((nb, c, dk_), lambda s, ch=ch: (0, ch(s), 0)),
                     pl.BlockSpec((nb, c, dv_), lambda s, lch=lch: (0, lch(s), 0)),
                     pl.BlockSpec((1, nb * HEADS, hv, hk), lambda s: (ns - 1 - s, 0, 0, 0))]
        args += [pb3, pb3, pv3, (b_b if rev else b_f), do3, (zs_b if rev else zs_f)]
        for w, dt in ((dk_, BF16), (dk_, BF16), (dv_, BF16), (dk_, F32)):
            out_specs.append(pl.BlockSpec((nb, c, w), lambda s, ch=ch: (0, ch(s), 0)))
            out_shape.append(jax.ShapeDtypeStruct((nb, l_len, w), dt))
    return pl.pallas_call(
        body, name="gla_bwd", grid=(ns,), in_specs=in_specs, out_specs=tuple(out_specs), out_shape=tuple(out_shape),
        scratch_shapes=[pltpu.VMEM((2 * nb * HEADS, hv, hk), F32)],
        compiler_params=_params())(*args)


def _tail(a1, pa, o_f, o_b, x2, tgt, mod, wc, wg, wo, ln_g, ln_b, gn_t, fg, nb, tm, n_split):
    tl, d = x2.shape
    nt = tl // tm
    per_ex = nt // nb
    hv = d // HEADS
    nrow = mod.shape[0]

    def part(shared, a1_ref, z_ref, r_ref, mc_ref, mg_ref, of_ref, ob_ref, x_ref, t_ref,
             dp_ref, da1_ref, do_ref, gx_ref, mrg_ref, dmo_ref, yci_ref, dyc_ref, ogi_ref, dyg_ref, sm_ref):
        bidx, gate, lng, lnb, fgv, gn, wc_, wg_, wo_ = shared

        a1v = a1_ref[...]
        mu = jnp.mean(a1v, axis=-1, keepdims=True)
        xc = a1v - mu
        rs = lax.rsqrt(jnp.mean(xc * xc, axis=-1, keepdims=True) + EPS)
        xh = xc * rs
        a2 = xh * lng + lnb
        s2 = _sigmoid(a2)
        a3 = a2 * s2
        zv = z_ref[...]
        sz = _sigmoid(zv)
        siluz = zv * sz
        ycin = a3 * siluz
        yconv = _mm(ycin, wc_)

        o = of_ref[...] + ob_ref[...]
        ohat_parts, rn_parts = [], []
        for h in range(HEADS):
            oh = o[:, h * hv:(h + 1) * hv]
            rn = lax.rsqrt(jnp.mean(oh * oh, axis=-1, keepdims=True) + EPS)
            ohat_parts.append(oh * rn)
            rn_parts.append(rn)
        ohat = jnp.concatenate(ohat_parts, axis=1)
        on = ohat * gn
        rv = r_ref[...]
        sr = _sigmoid(rv)
        silur = rv * sr
        ogin = on * silur
        ygla = _mm(ogin, wg_)

        sc = _sigmoid(mc_ref[...])
        sg = _sigmoid(mg_ref[...])
        merged = sc * yconv + sg * ygla
        mo = _mm(merged, wo_)
        hn = x_ref[...] + gate * mo
        rf = lax.rsqrt(jnp.mean(hn * hn, axis=-1, keepdims=True) + EPS)
        yh = hn * rf
        err = yh * fgv - t_ref[...]
        loss_part = 0.5 * jnp.sum(err * err) * (1.0 / d)

        dy = err * (1.0 / d)
        dfg = jnp.sum(dy * yh, axis=0, keepdims=True)
        dyh = dy * fgv
        dhn = rf * (dyh - yh * jnp.mean(dyh * yh, axis=-1, keepdims=True))
        gx_ref[...] = dhn
        dgate = jnp.sum(dhn * mo, axis=0, keepdims=True)
        dmo = gate * dhn
        dmerged = _mm_nt(dmo, wo_)
        dyconv = dmerged * sc
        dygla = dmerged * sg
        dp_ref[:, 2 * d:3 * d] = (dmerged * yconv * sc * (1.0 - sc)).astype(BF16)
        dp_ref[:, 3 * d:4 * d] = (dmerged * ygla * sg * (1.0 - sg)).astype(BF16)
        dycin = _mm_nt(dyconv, wc_)
        dogin = _mm_nt(dygla, wg_)
        mrg_ref[...] = merged.astype(BF16)
        dmo_ref[...] = dmo.astype(BF16)
        yci_ref[...] = ycin.astype(BF16)
        dyc_ref[...] = dyconv.astype(BF16)
        ogi_ref[...] = ogin.astype(BF16)
        dyg_ref[...] = dygla.astype(BF16)

        da3 = dycin * siluz
        dp_ref[:, 0:d] = (dycin * a3 * _dsilu(zv, sz)).astype(BF16)
        da2 = da3 * _dsilu(a2, s2)
        dlng = jnp.sum(da2 * xh, axis=0, keepdims=True)
        dlnb = jnp.sum(da2, axis=0, keepdims=True)
        dxh = da2 * lng
        da1_ref[...] = rs * (dxh - jnp.mean(dxh, axis=-1, keepdims=True)
                             - xh * jnp.mean(dxh * xh, axis=-1, keepdims=True))

        don = dogin * silur
        dp_ref[:, d:2 * d] = (dogin * on * _dsilu(rv, sr)).astype(BF16)
        dgn = jnp.sum(don * ohat, axis=0, keepdims=True)
        dyn = don * gn
        for h in range(HEADS):
            vs = slice(h * hv, (h + 1) * hv)
            oh_hat = ohat_parts[h]
            dh = dyn[:, vs]
            do_ref[:, vs] = (rn_parts[h] * (dh - oh_hat * jnp.mean(dh * oh_hat, axis=-1, keepdims=True))
                             ).astype(BF16)

        sm_ref[0:1, :] += dfg
        sm_ref[1:2, :] += dlng
        sm_ref[2:3, :] += dlnb
        sm_ref[3:4, :] += dgn
        sm_ref[4:5, :] += jnp.zeros((1, d), F32) + loss_part
        for b in range(nb):
            sm_ref[8 + b:9 + b, :] += jnp.where(bidx == b, dgate, 0.0)

    def body(*refs):
        mod_ref, wc_ref, wg_ref, wo_ref, lng_ref, lnb_ref, gn_ref, fg_ref = refs[9:17]
        sm_ref = refs[27]
        i = pl.program_id(0)

        @pl.when(i == 0)
        def _():
            sm_ref[...] = jnp.zeros_like(sm_ref)

        bidx = i // per_ex
        shared = (bidx, _rowsel(mod_ref[...], bidx, nb)[:, 2 * d:3 * d], lng_ref[...], lnb_ref[...], fg_ref[...],
                  jnp.concatenate([gn_ref[...]] * HEADS, axis=1), wc_ref[...], wg_ref[...], wo_ref[...])
        rows_per = tm // n_split
        for p in range(n_split):
            rows = pl.ds(p * rows_per, rows_per)
            part(shared, *[r.at[rows] for r in refs[0:9]], *[r.at[rows] for r in refs[17:27]], sm_ref)

    row = pl.BlockSpec((tm, d), lambda i: (i, 0))
    pcol = lambda blk: pl.BlockSpec((tm, d), lambda i: (i, blk))
    full = lambda arr: pl.BlockSpec(arr.shape, lambda i: (0,) * arr.ndim)
    bfo = jax.ShapeDtypeStruct((tl, d), BF16)
    f32o = jax.ShapeDtypeStruct((tl, d), F32)
    return pl.pallas_call(
        body, name="tail", grid=(nt,),
        in_specs=[row, pcol(2), pcol(3), pcol(4), pcol(5), row, row, row, row, full(mod), full(wc), full(wg),
                  full(wo), full(ln_g), full(ln_b), full(gn_t), full(fg)],
        out_specs=(pl.BlockSpec((tm, 4 * d), lambda i: (i, 0)), row, row, row, row, row, row, row, row, row,
                   pl.BlockSpec((16, d), lambda i: (0, 0))),
        out_shape=(jax.ShapeDtypeStruct((tl, 4 * d), BF16), f32o, bfo, f32o, bfo, bfo, bfo, bfo, bfo, bfo,
                   jax.ShapeDtypeStruct((16, d), F32)),
        compiler_params=_params())(a1, pa, pa, pa, pa, o_f, o_b, x2, tgt, mod, wc, wg, wo, ln_g, ln_b, gn_t, fg)


def _local_step(x, c, ctx, tgt, c_ctx, ada_w8, ada_b, norm_g, w_a, b_a, w_b, b_b, conv_w8, conv_b, ln_g, ln_b,
                up2, bias2, gla_norm_g, final_norm_g, proj, on_grads=None, on_du_a1=None):
    nb, s_len, d = x.shape
    c_len = ctx.shape[1]
    dk_, dv_ = d // 2, d
    tl, tc = nb * s_len, nb * c_len
    nbw = 2 * dk_ + dv_ + LANE
    tm = math.gcd(256, c_len)
    tiles = _Tiles(nb, s_len, c_len, tm, 2)
    tmm = tiles.big * tm
    l_len = tiles.rows_per_ex
    t_all = nb * l_len
    x2, ctx2, tgt2 = x.reshape(tl, d), ctx.reshape(tc, d), tgt.reshape(tl, d)

    cv = jnp.zeros((8, d), F32).at[0:nb].set(c).at[nb].set(c_ctx.reshape(d))
    mod = _ada_fwd(cv, ada_w8, ada_b)
    u = _norm_fwd(x2, ctx2, mod, norm_g, tiles)
    pa = _matmul_bias("inproj_a", u, w_a, b_a, tl, tmm, _tile(6 * d, 3072), tiles.big_all_of_lat)
    pb, pv = _inproj_b(u, w_b, b_b, tmm, dk_, dv_)

    a1 = _conv_fwd(pa, conv_w8, conv_b, nb, s_len)
    lr_blk = (2 * dk_) // LANE
    g_all = _decay_fwd(pb, up2, bias2, tm, lr_blk)
    pb3, pv3 = pb.reshape(nb, l_len, 2 * dk_ + LANE), pv.reshape(nb, l_len, dv_)
    o_f, zs_f, b_f, o_b, zs_b, b_b2 = _gla_fwd(pb3, pv3, g_all.reshape(nb, l_len, 2 * dk_), nb, s_len, c_len,
                                               dk_, dv_)

    conv_proj, gla_proj, w_out = proj(a1) if callable(proj) else proj
    tt = math.gcd(256, s_len)
    (dp_a2, da1, d_o, gx1, merged, dmo, ycin, dyconv, ogin, dygla, small) = _tail(
        a1, pa, o_f.reshape(tl, dv_), o_b.reshape(tl, dv_), x2, tgt2, mod, conv_proj, gla_proj, w_out, ln_g, ln_b,
        gla_norm_g, final_norm_g, nb, tt, 2)

    lat3 = lambda a: a.reshape(nb, s_len, a.shape[-1])
    tnw = _tile(d, 1024)
    tnp = _tile(d, 512)
    d_w_out, _ = _matmul_tn_whole("dw_out", lat3(merged), lat3(dmo), s_len, tnp, False)
    d_conv_proj, _ = _matmul_tn_whole("dw_conv_proj", lat3(ycin), lat3(dyconv), s_len, tnp, False)
    d_gla_proj, _ = _matmul_tn_whole("dw_gla_proj", lat3(ogin), lat3(dygla), s_len, tnp, False)

    dp_a1, d_conv_w8, d_conv_b = _conv_bwd(pa, da1, conv_w8, nb, s_len)
    gl = _gla_bwd(pb3, pv3, d_o.reshape(nb, s_len, dv_), (zs_f, b_f, zs_b, b_b2), nb, s_len, c_len, dk_, dv_)
    gl = [g_.reshape(t_all, g_.shape[-1]) for g_ in gl]
    dp_b, d_up2, d_bias2 = _decay_bwd(pb, up2, bias2, gl[0:4], gl[4:8], tiles, lr_blk, dk_, dv_)

    u3 = u.reshape(nb, l_len, d)
    dw_a1, db_a1 = _matmul_tn_whole("dw_a1", u3, lat3(dp_a1), s_len, tnw, True)
    dw_a2, db_a2 = _matmul_tn_whole("dw_a2", u3, lat3(dp_a2), s_len, tnw, True)
    dw_b, db_b = _matmul_tn("dw_b", u, dp_b, t_all, tmm, nbw)
    grads = dict(w_a1=dw_a1, w_a2=dw_a2, w_b=dw_b, conv_w8=d_conv_w8, conv_proj=d_conv_proj, up2=d_up2,
                 gla_proj=d_gla_proj, w_out=d_w_out)

    tka = _tile(2 * d, 2048)
    du_a1 = _matmul_nt("du_a1", dp_a1, w_a, 0, tmm, tka, after=on_grads(grads) if on_grads else ())
    du_a2 = _matmul_nt("du_a2", dp_a2, w_a, (2 * d) // tka, tmm, tka, after=on_du_a1(du_a1) if on_du_a1 else ())
    du_b = _matmul_nt("du_b", dp_b, w_b, 0, tmm, nbw)
    grad_x2, dmod_ss, d_norm_g = _norm_bwd(x2, ctx2, mod, norm_g, [du_a1, du_a2], du_b, gx1, tiles)
    d_ada_w8, d_ada_b, d_cv = _ada_bwd(cv, ada_w8, dmod_ss, small, nb)

    return dict(
        grads, grad_x=grad_x2.reshape(nb, s_len, d), small=small, cv=d_cv, ada_w8=d_ada_w8, ada_b=d_ada_b,
        norm_g=d_norm_g, b_a1=db_a1, b_a2=db_a2, b_b=db_b, conv_b=d_conv_b, bias2=d_bias2)


def _regroup_pieces(d, r, wshard):
    cb = d // N_DEV
    segs = []
    for j in range(N_DEV):
        segs.append((j * cb, cb, 0, 2 * j * cb))
    for j in range(N_DEV):
        segs.append((d + j * cb, cb, 0, (2 * j + 1) * cb))
    segs += [(2 * d, d, 0, 2 * d), (3 * d, 2 * d + 2 * r, 1, 0), (5 * d + 2 * r, 3 * d, 0, 3 * d)]
    pieces = []
    for o0, w, dst, d0 in segs:
        lo = o0
        while lo < o0 + w:
            j = lo // wshard
            hi = min(o0 + w, (j + 1) * wshard)
            pieces.append((j, lo - j * wshard, hi - lo, dst, d0 + lo - o0))
            lo = hi
    return pieces


def _regroup(o, d, r):
    n_in = 8 * d + 2 * r
    parts = ([], [])
    for _, s0, n, dst, _ in sorted(_regroup_pieces(d, r, n_in), key=lambda p: (p[3], p[4])):
        parts[dst].append(o[..., s0:s0 + n])
    pad = jnp.zeros(o.shape[:-1] + (LANE - 2 * r,), o.dtype)
    return jnp.concatenate(parts[0], axis=-1), jnp.concatenate(parts[1] + [pad], axis=-1)


def _unshard_w_in(g_win, d, r, after=()):
    n_sh, _, ws = g_win.shape
    nbw = 2 * d + LANE
    pieces = _regroup_pieces(d, r, ws)
    tr = math.gcd(d, 256)

    def body(g_ref, *rest):
        a_ref, b_ref = rest[len(after):]
        dsts = (a_ref, b_ref)
        for j, s0, n, dst, d0 in pieces:
            dsts[dst][:, pl.ds(d0, n)] = g_ref[j, :, pl.ds(s0, n)]
        b_ref[:, pl.ds(2 * d + 2 * r, LANE - 2 * r)] = jnp.zeros((tr, LANE - 2 * r), b_ref.dtype)

    return pl.pallas_call(
        body, name="unshard_w_in", grid=(d // tr,),
        in_specs=[pl.BlockSpec((n_sh, tr, ws), lambda i: (0, i, 0))] + [_ANY] * len(after),
        out_specs=(pl.BlockSpec((tr, 6 * d), lambda i: (i, 0)), pl.BlockSpec((tr, nbw), lambda i: (i, 0))),
        out_shape=(jax.ShapeDtypeStruct((d, 6 * d), g_win.dtype), jax.ShapeDtypeStruct((d, nbw), g_win.dtype)),
        compiler_params=_params())(g_win, *after)


def _reshard_w_in(dwt_a1, dwt_a2, dwt_b, d, r):
    ws = (8 * d + 2 * r) // N_DEV
    pieces = _regroup_pieces(d, r, ws)
    tc = math.gcd(d, 256)

    def body(a1_ref, a2_ref, b_ref, o_ref):
        for j, s0, n, dst, d0 in pieces:
            if dst == 1:
                src = b_ref[pl.ds(d0, n), :]
            elif d0 < 2 * d:
                src = a1_ref[pl.ds(d0, n), :]
            else:
                src = a2_ref[pl.ds(d0 - 2 * d, n), :]
            o_ref[j, pl.ds(s0, n), :] = src

    col = lambda h: pl.BlockSpec((h, tc), lambda i: (0, i))
    return pl.pallas_call(
        body, name="reshard_w_in", grid=(d // tc,),
        in_specs=[col(2 * d), col(4 * d), col(2 * d + LANE)],
        out_specs=pl.BlockSpec((N_DEV, ws, tc), lambda i: (0, 0, i)),
        out_shape=jax.ShapeDtypeStruct((N_DEV, ws, d), dwt_b.dtype),
        compiler_params=_params())(dwt_a1, dwt_a2, dwt_b)


_SMALL = ("c_ctx", "ada_b", "norm_g", "b_in", "conv_b", "conv_ln_g", "conv_ln_b", "decay_bias_fwd",
          "decay_bias_bwd", "gla_norm_g", "final_norm_g")


def _small_layout(d, r):
    sizes = dict(c_ctx=d, ada_b=3 * d, norm_g=d, b_in=8 * d + 2 * r, conv_b=d, conv_ln_g=d, conv_ln_b=d,
                 decay_bias_fwd=d // 2, decay_bias_bwd=d // 2, gla_norm_g=d // HEADS, final_norm_g=d, loss=1)
    table, off = {}, 0
    for name in _SMALL + ("loss",):
        table[name] = (off, sizes[name])
        off += -(-sizes[name] // LANE) * LANE
    return table, off


def _pack_small(g, nb, d, r):
    table, width = _small_layout(d, r)
    hv = d // HEADS
    pieces = _regroup_pieces(d, r, 8 * d + 2 * r)
    names = ("small", "cv", "ada_b", "norm_g", "b_a1", "b_a2", "b_b", "conv_b", "bias2")

    def body(sm, cv, ab, ng, ba1, ba2, bb, cvb, b2, o_ref):
        o_ref[...] = jnp.zeros_like(o_ref)

        def put(name, val):
            off, n = table[name]
            o_ref[:, pl.ds(off, n)] = val

        put("c_ctx", cv[nb:nb + 1, :])
        put("ada_b", ab[...])
        put("norm_g", ng[...])
        off_b = table["b_in"][0]
        for _, s0, n, dst, d0 in pieces:
            if dst == 1:
                src = bb[:, pl.ds(d0, n)]
            elif d0 < 2 * d:
                src = ba1[:, pl.ds(d0, n)]
            else:
                src = ba2[:, pl.ds(d0 - 2 * d, n)]
            o_ref[:, pl.ds(off_b + s0, n)] = src
        put("conv_b", cvb[...])
        put("conv_ln_g", sm[1:2, :])
        put("conv_ln_b", sm[2:3, :])
        put("decay_bias_fwd", b2[:, 0:d // 2])
        put("decay_bias_bwd", b2[:, d // 2:d])
        gn = sm[3:4, 0:hv]
        for h in range(1, HEADS):
            gn = gn + sm[3:4, h * hv:(h + 1) * hv]
        put("gla_norm_g", gn)
        put("final_norm_g", sm[0:1, :])
        put("loss", sm[4:5, 0:1])

    return pl.pallas_call(body, name="pack_small", out_shape=jax.ShapeDtypeStruct((1, width), F32),
                          compiler_params=_params())(*[g[k] for k in names])


def _small_adam(parts, ws, ms, vs, d, r):
    table, width = _small_layout(d, r)
    n_parts = parts.shape[0]
    k = len(_SMALL)
    bc1 = 1.0 - ADAM_B1 ** ADAM_STEP
    bc2 = 1.0 - ADAM_B2 ** ADAM_STEP

    def body(p_ref, *refs):
        w_refs, m_refs, v_refs = refs[0:k], refs[k:2 * k], refs[2 * k:3 * k]
        outs = refs[3 * k:]
        tot = p_ref[0]
        for i in range(1, n_parts):
            tot = tot + p_ref[i]
        for i, name in enumerate(_SMALL):
            off, n = table[name]
            g = tot[:, off:off + n]
            mn = ADAM_B1 * m_refs[i][...] + (1.0 - ADAM_B1) * g
            vn = ADAM_B2 * v_refs[i][...] + (1.0 - ADAM_B2) * (g * g)
            outs[i][...] = g
            outs[k + i][...] = -ADAM_LR * ((mn / bc1) / (jnp.sqrt(vn / bc2) + ADAM_EPS) + ADAM_WD * w_refs[i][...])
            outs[2 * k + i][...] = mn
            outs[3 * k + i][...] = vn
        off, _ = table["loss"]
        outs[4 * k][...] = tot[:, off:off + 1]

    shapes = [jax.ShapeDtypeStruct(w.shape, F32) for w in ws]
    res = pl.pallas_call(body, name="small_adam", out_shape=tuple(shapes * 4 + [jax.ShapeDtypeStruct((1, 1), F32)]),
                         compiler_params=_params())(parts, *ws, *ms, *vs)
    return res[0:k], res[k:2 * k], res[2 * k:3 * k], res[3 * k:4 * k], res[4 * k]


def _mesh_pos():
    return lax.axis_index("x"), lax.axis_index("y"), lax.axis_index("c")


def _all_gather(arrs):
    n = len(arrs)
    ns = 9
    split = [a.ndim == 2 and a.shape[0] % 32 == 0 for a in arrs]

    def body(*refs):
        ins, outs = refs[:n], refs[n:2 * n]
        send_sems, recv_sems, local_sems = refs[2 * n:]
        x, y, c = _mesh_pos()
        me, sibling = (x, y, c), (x, y, 1 - c)
        xn, yn, dg = (1 - x, y, c), (x, 1 - y, c), (1 - x, 1 - y, c)
        other = lambda pos: (pos[0], pos[1], 1 - c)

        def slot(a, pos, half):
            ref = outs[a].at[4 * pos[0] + 2 * pos[1] + pos[2]]
            if half is None:
                return ref
            rows = arrs[a].shape[0] // 2
            return ref.at[pl.ds(half * rows, rows)]

        def copy(a, k, block, to, src=None, half=None):
            dst = slot(a, block, half)
            return pltpu.make_async_remote_copy(
                src_ref=dst if src is None else src, dst_ref=dst,
                send_sem=send_sems.at[ns * a + k], recv_sem=recv_sems.at[ns * a + k],
                device_id=to, device_id_type=MESH)

        h0 = lambda a: 0 if split[a] else None
        mine = [pltpu.make_async_copy(ins[a], slot(a, me, None), local_sems.at[a]) for a in range(n)]
        for cp in mine:
            cp.start()
        sent = []
        for a in range(n):
            sent += [copy(a, 0, me, sibling, src=ins[a]), copy(a, 1, me, xn, src=ins[a]),
                     copy(a, 2, me, yn, src=ins[a])]
        for cp in sent:
            cp.start()

        def pass_on(cp):
            cp.start()
            sent.append(cp)

        for a in range(n):
            copy(a, 1, xn, me).wait_recv()
            pass_on(copy(a, 3, xn, sibling))
            pass_on(copy(a, 4, xn, yn, half=h0(a)))
        for a in range(n):
            copy(a, 2, yn, me).wait_recv()
            pass_on(copy(a, 5, yn, sibling))
            if split[a]:
                pass_on(copy(a, 6, yn, xn, half=1))
        for a in range(n):
            copy(a, 4, dg, me, half=h0(a)).wait_recv()
            pass_on(copy(a, 7, dg, sibling, half=h0(a)))
            if split[a]:
                copy(a, 6, dg, me, half=1).wait_recv()
                pass_on(copy(a, 8, dg, sibling, half=1))
        for a in range(n):
            copy(a, 0, sibling, me).wait_recv()
            copy(a, 3, other(xn), me).wait_recv()
            copy(a, 5, other(yn), me).wait_recv()
            copy(a, 7, other(dg), me, half=h0(a)).wait_recv()
            if split[a]:
                copy(a, 8, other(dg), me, half=1).wait_recv()
        for cp in sent:
            cp.wait_send()
        for cp in mine:
            cp.wait()

    return pl.pallas_call(
        body, name="all_gather",
        out_shape=tuple(jax.ShapeDtypeStruct((N_DEV,) + a.shape, a.dtype) for a in arrs),
        in_specs=[_ANY] * n, out_specs=tuple([_ANY] * n),
        scratch_shapes=[pltpu.SemaphoreType.DMA((ns * n,)), pltpu.SemaphoreType.DMA((ns * n,)),
                        pltpu.SemaphoreType.DMA((n,))],
    )(*arrs)


def _exchange_sibling(arrs):
    n = len(arrs)

    def body(*refs):
        ins, outs = refs[:n], refs[n:2 * n]
        send_sems, recv_sems = refs[2 * n:]
        x, y, c = _mesh_pos()
        copies = [pltpu.make_async_remote_copy(
            src_ref=ins[a].at[2 * k + (1 - c)], dst_ref=outs[a].at[k],
            send_sem=send_sems.at[4 * a + k], recv_sem=recv_sems.at[4 * a + k],
            device_id=(x, y, 1 - c), device_id_type=MESH) for a in range(n) for k in range(4)]
        for cp in copies:
            cp.start()
        for cp in copies:
            cp.wait_recv()
        for cp in copies:
            cp.wait_send()

    return pl.pallas_call(
        body, name="grad_exchange_sibling",
        out_shape=tuple(jax.ShapeDtypeStruct((4,) + a.shape[1:], a.dtype) for a in arrs),
        in_specs=[_ANY] * n, out_specs=tuple([_ANY] * n),
        scratch_shapes=[pltpu.SemaphoreType.DMA((4 * n,)), pltpu.SemaphoreType.DMA((4 * n,))],
    )(*arrs)


def _elementwise_tile(r, cdim):
    if r % 8 == 0 and r > 256:
        return math.gcd(r, 256), cdim
    if r > 256 and cdim % 256 == 0:
        return r, 256
    return r, cdim


def _pair_sum(name, mine, theirs):
    _, r, cdim = mine.shape
    tr, tc = _elementwise_tile(r, cdim)

    def body(m_ref, t_ref, o_ref):
        c = lax.axis_index("c")
        own = jnp.where(c == 0, m_ref[:, 0].astype(F32), m_ref[:, 1].astype(F32))
        o_ref[...] = (own + t_ref[...].astype(F32)).astype(o_ref.dtype)

    return pl.pallas_call(
        body, name=name, grid=(r // tr, cdim // tc),
        in_specs=[pl.BlockSpec((4, 2, tr, tc), lambda i, j: (0, 0, i, j)),
                  pl.BlockSpec((4, tr, tc), lambda i, j: (0, i, j))],
        out_specs=pl.BlockSpec((4, tr, tc), lambda i, j: (0, i, j)),
        out_shape=jax.ShapeDtypeStruct((4, r, cdim), mine.dtype),
        compiler_params=_params())(mine.reshape(4, 2, r, cdim), theirs)


_HBM = pl.BlockSpec(memory_space=pltpu.HBM)
_SEM = pl.BlockSpec(memory_space=pltpu.SEMAPHORE)


def _copies_start(name, srcs, lands, make_copies, n_sems):
    n, m = len(srcs), len(lands)

    def body(*refs):
        ins = refs[:n + m]
        send_sems, recv_sems = refs[n + m], refs[n + m + 1]
        for cp in make_copies(ins[:n], ins[n:], send_sems, recv_sems):
            cp.start()
        refs[-1][...] = jnp.zeros_like(refs[-1])

    res = pl.pallas_call(
        body, name=name,
        out_shape=(pltpu.SemaphoreType.DMA((n_sems,)), pltpu.SemaphoreType.DMA((n_sems,)),
                   *[pltpu.HBM(a.shape, a.dtype) for a in (*srcs, *lands)], jax.ShapeDtypeStruct((8, LANE), F32)),
        in_specs=[_HBM] * (n + m),
        out_specs=(_SEM, _SEM, *[_HBM] * (n + m), pl.BlockSpec(memory_space=pltpu.VMEM)),
        input_output_aliases={i: 2 + i for i in range(n + m)},
        compiler_params=pltpu.CompilerParams(has_side_effects=pltpu.SideEffectType.DATAFLOW_SIDE_EFFECTING),
    )(*[pltpu.with_memory_space_constraint(a, pltpu.HBM) for a in (*srcs, *lands)])
    return res[0], res[1], res[2:2 + n], res[2 + n:2 + n + m], res[-1]


def _copies_wait(name, started, after, make_copies):
    send_sems, recv_sems, srcs, lands, _ = started
    n, m = len(srcs), len(lands)

    def body(*refs):
        ins = refs[:n + m]
        for cp in make_copies(ins[:n], ins[n:], refs[n + m], refs[n + m + 1]):
            cp.wait_send()
            cp.wait_recv()

    res = pl.pallas_call(
        body, name=name,
        out_shape=tuple(pltpu.HBM(a.shape, a.dtype) for a in (*srcs, *lands)),
        in_specs=[_HBM] * (n + m) + [_SEM, _SEM] + [_ANY] * len(after),
        out_specs=tuple([_HBM] * (n + m)),
        input_output_aliases={i: i for i in range(n + m)},
        compiler_params=pltpu.CompilerParams(has_side_effects=pltpu.SideEffectType.DATAFLOW_SIDE_EFFECTING),
    )(*srcs, *lands, send_sems, recv_sems, *after)
    return res[:n], res[n:]


def _gather_copies(srcs, lands, send_sems, recv_sems):
    x, y, c = _mesh_pos()
    me_i = 4 * x + 2 * y + c
    copies = []
    for rel in range(1, N_DEV):
        peer = (1 - x if rel & 4 else x, 1 - y if rel & 2 else y, 1 - c if rel & 1 else c)
        for a in range(len(srcs)):
            copies.append(pltpu.make_async_remote_copy(
                src_ref=srcs[a], dst_ref=lands[a].at[me_i], send_sem=send_sems.at[7 * a + rel - 1],
                recv_sem=recv_sems.at[7 * a + rel - 1], device_id=peer, device_id_type=MESH))
    return copies


def _sibling_copies(srcs, lands, send_sems, recv_sems):
    x, y, c = _mesh_pos()
    return [pltpu.make_async_remote_copy(
        src_ref=srcs[a].at[2 * k + (1 - c)], dst_ref=lands[a].at[k], send_sem=send_sems.at[4 * a + k],
        recv_sem=recv_sems.at[4 * a + k], device_id=(x, y, 1 - c), device_id_type=MESH)
        for a in range(len(srcs)) for k in range(4)]


def _chip_copies(srcs, lands, send_sems, recv_sems):
    x, y, c = _mesh_pos()
    my_chip = 2 * x + y
    copies = []
    for rel in range(1, 4):
        px = 1 - x if rel & 2 else x
        py = 1 - y if rel & 1 else y
        for a in range(len(srcs)):
            copies.append(pltpu.make_async_remote_copy(
                src_ref=srcs[a].at[2 * px + py], dst_ref=lands[a].at[my_chip], send_sem=send_sems.at[3 * a + rel - 1],
                recv_sem=recv_sems.at[3 * a + rel - 1], device_id=(px, py, c), device_id_type=MESH))
    return copies


def _sum_adam(name, parts, w, m, v, own=None):
    unit_mid = w.ndim == 3
    _, r, cdim = parts.shape
    n_parts = parts.shape[0]
    tr, tc = _elementwise_tile(r, cdim)
    bc1 = 1.0 - ADAM_B1 ** ADAM_STEP
    bc2 = 1.0 - ADAM_B2 ** ADAM_STEP
    extra = [] if own is None else [own]

    def body(p_ref, *refs):
        w_ref, m_ref, v_ref, g_ref, d_ref, nm_ref, nv_ref = refs[len(extra):]
        if own is None:
            part = lambda k: p_ref[k].astype(F32)
        else:
            my_chip = 2 * lax.axis_index("x") + lax.axis_index("y")
            part = lambda k: jnp.where(my_chip == k, refs[0][k], p_ref[k]).astype(F32)
        g = part(0)
        for k in range(1, n_parts):
            g = g + part(k)
        if unit_mid:
            g = g.reshape(tr, 1, tc)
        mn = ADAM_B1 * m_ref[...] + (1.0 - ADAM_B1) * g
        vn = ADAM_B2 * v_ref[...] + (1.0 - ADAM_B2) * (g * g)
        g_ref[...] = g
        nm_ref[...] = mn
        nv_ref[...] = vn
        d_ref[...] = -ADAM_LR * ((mn / bc1) / (jnp.sqrt(vn / bc2) + ADAM_EPS) + ADAM_WD * w_ref[...])

    blk = (pl.BlockSpec((tr, 1, tc), lambda i, j: (i, 0, j)) if unit_mid
           else pl.BlockSpec((tr, tc), lambda i, j: (i, j)))
    o = jax.ShapeDtypeStruct(w.shape, F32)
    return pl.pallas_call(
        body, name=name, grid=(r // tr, cdim // tc),
        in_specs=[pl.BlockSpec((n_parts, tr, tc), lambda i, j: (0, i, j))] * (1 + len(extra)) + [blk, blk, blk],
        out_specs=(blk, blk, blk, blk), out_shape=(o, o, o, o),
        compiler_params=_params())(parts, *extra, w, m, v)


_WEIGHTS = ("c_ctx", "ada_w", "ada_b", "norm_g", "w_in", "b_in", "conv_w", "conv_b", "conv_ln_g", "conv_ln_b",
            "conv_proj", "decay_up_fwd", "decay_bias_fwd", "decay_up_bwd", "decay_bias_bwd", "gla_norm_g",
            "gla_proj", "w_out", "final_norm_g")


def _as2d(a):
    if a.ndim == 1:
        return a.reshape(1, -1)
    return a.reshape(-1, a.shape[-1])


def kernel(x, c, ctx, c_ctx, ada_w, ada_b, norm_g, w_in, b_in, conv_w, conv_b, conv_ln_g, conv_ln_b, conv_proj, decay_up_fwd, decay_bias_fwd, decay_up_bwd, decay_bias_bwd, gla_norm_g, gla_proj, w_out, final_norm_g, loss_target, m_c_ctx, m_ada_w, m_ada_b, m_norm_g, m_w_in, m_b_in, m_conv_w, m_conv_b, m_conv_ln_g, m_conv_ln_b, m_conv_proj, m_decay_up_fwd, m_decay_bias_fwd, m_decay_up_bwd, m_decay_bias_bwd, m_gla_norm_g, m_gla_proj, m_w_out, m_final_norm_g, v_c_ctx, v_ada_w, v_ada_b, v_norm_g, v_w_in, v_b_in, v_conv_w, v_conv_b, v_conv_ln_g, v_conv_ln_b, v_conv_proj, v_decay_up_fwd, v_decay_bias_fwd, v_decay_up_bwd, v_decay_bias_bwd, v_gla_norm_g, v_gla_proj, v_w_out, v_final_norm_g):
    env = dict(locals())
    wts = {k: env[k] for k in _WEIGHTS}
    d = x.shape[-1]
    r = decay_up_fwd.shape[1]
    dk_ = d // 2

    ds, dks = d // N_DEV, dk_ // N_DEV
    g_win, g_ada, conv_w8, g_up = _all_gather(
        [w_in[0].astype(BF16), ada_w[0].astype(BF16), conv_w[0],
         jnp.concatenate([decay_up_fwd[0], decay_up_bwd[0]], axis=1)])
    proj_own = [conv_proj[0].astype(BF16), gla_proj[0].astype(BF16), w_out[0].astype(BF16)]
    me_i = 4 * lax.axis_index("x") + 2 * lax.axis_index("y") + lax.axis_index("c")
    proj_lands = [lax.dynamic_update_slice(lax.empty((N_DEV,) + a.shape, a.dtype), a[None], (me_i, 0, 0))
                  for a in proj_own]
    proj_start = _copies_start("proj_gather_start", proj_own, proj_lands, _gather_copies, 7 * 3)

    def proj(after):
        _, lands = _copies_wait("proj_gather_wait", proj_start, (after,), _gather_copies)
        return [w.reshape(d, d) for w in lands]

    w_a, w_b = _unshard_w_in(g_win, d, r, after=(proj_start[4],))
    up_f = g_up[:, :, 0:dks].transpose(1, 0, 2).reshape(r, dk_)
    up_b = g_up[:, :, dks:].transpose(1, 0, 2).reshape(r, dk_)
    up2 = jnp.zeros((LANE, 2 * dk_), F32).at[0:r, 0:dk_].set(up_f).at[r:2 * r, dk_:].set(up_b)
    bias2 = jnp.concatenate([decay_bias_fwd, decay_bias_bwd], axis=1)
    b_a, b_b = _regroup(b_in, d, r)

    names = ("w_in", "conv_proj", "gla_proj", "w_out", "conv_w", "decay_up")
    comm = {}

    def on_grads(gr):
        d_up = jnp.concatenate([gr["up2"][0:r, 0:dk_].reshape(r, N_DEV, dks).transpose(1, 0, 2),
                                gr["up2"][r:2 * r, dk_:].reshape(r, N_DEV, dks).transpose(1, 0, 2)], axis=2)
        mine = [_reshard_w_in(gr["w_a1"], gr["w_a2"], gr["w_b"], d, r), gr["conv_proj"].reshape(N_DEV, ds, d),
                gr["gla_proj"].reshape(N_DEV, ds, d), gr["w_out"].reshape(N_DEV, ds, d), gr["conv_w8"], d_up]
        lands = [lax.empty((4,) + a.shape[1:], a.dtype) for a in mine]
        comm["sibling"] = _copies_start("grad_sibling_start", mine, lands, _sibling_copies, 4 * len(mine))
        return (comm["sibling"][4],)

    def on_du_a1(du_a1):
        mine, theirs = _copies_wait("grad_sibling_wait", comm["sibling"], (du_a1,), _sibling_copies)
        sums = [_pair_sum("pair_sum_" + nm, a, b) for nm, a, b in zip(names, mine, theirs)]
        lands = [lax.empty(a.shape, a.dtype) for a in sums]
        comm["chips"] = _copies_start("grad_chips_start", sums, lands, _chip_copies, 3 * len(sums))
        return (comm["chips"][4],)

    g = _local_step(x, c, ctx, loss_target, c_ctx, g_ada, ada_b, norm_g[0:1], w_a, b_a, w_b, b_b,
                    conv_w8, conv_b, conv_ln_g, conv_ln_b, up2, bias2, gla_norm_g, final_norm_g.reshape(1, d),
                    proj, on_grads, on_du_a1)

    pack = _pack_small(g, x.shape[0], d, r)
    pack_lands = [lax.dynamic_update_slice(lax.empty((N_DEV,) + pack.shape, F32), pack[None], (me_i, 0, 0))]
    small_start = _copies_start("small_gather_start", [pack], pack_lands, _gather_copies, 7)

    (their_ada,) = _exchange_sibling([g["ada_w8"]])
    ada_sum = _pair_sum("pair_sum_ada_w", g["ada_w8"], their_ada)
    ada_start = _copies_start("ada_chips_start", [ada_sum], [lax.empty(ada_sum.shape, ada_sum.dtype)],
                              _chip_copies, 3)
    own, landed = _copies_wait("grad_chips_wait", comm["chips"], (ada_start[4],), _chip_copies)
    o_win, o_cp, o_gp, o_wo, o_cw, o_up = own
    x_win, x_cp, x_gp, x_wo, x_cw, x_up = landed

    out = {}

    def big(name, parts, wname, own=None):
        w2 = _as2d(wts[wname])
        res = _sum_adam(name, parts, w2, _as2d(env["m_" + wname]), _as2d(env["v_" + wname]), own)
        for pre, arr in zip(("grad_", "delta_", "new_m_", "new_v_"), res):
            out[pre + wname] = arr.reshape(wts[wname].shape)

    as_rows = lambda a: jnp.transpose(a, (2, 0, 1))
    res = _sum_adam("adam_w_in", x_win, as_rows(w_in), as_rows(m_w_in), as_rows(v_w_in), o_win)
    for pre, arr in zip(("grad_", "delta_", "new_m_", "new_v_"), res):
        out[pre + "w_in"] = jnp.transpose(arr, (1, 2, 0))
    big("adam_conv_proj", x_cp, "conv_proj", o_cp)
    big("adam_gla_proj", x_gp, "gla_proj", o_gp)
    big("adam_w_out", x_wo, "w_out", o_wo)
    big("adam_conv_w", x_cw, "conv_w", o_cw)
    big("adam_up_f", x_up[:, :, 0:dks], "decay_up_fwd", o_up[:, :, 0:dks])
    big("adam_up_b", x_up[:, :, dks:], "decay_up_bwd", o_up[:, :, dks:])

    _, (packs,) = _copies_wait("small_gather_wait", small_start, (res[0], out["grad_w_out"]), _gather_copies)
    row = lambda a: a.reshape(1, -1)
    sg, sd, sm, sv, loss = _small_adam(packs, [row(wts[k]) for k in _SMALL], [row(env["m_" + k]) for k in _SMALL],
                                       [row(env["v_" + k]) for k in _SMALL], d, r)
    for i, k in enumerate(_SMALL):
        for pre, arrs in (("grad_", sg), ("delta_", sd), ("new_m_", sm), ("new_v_", sv)):
            out[pre + k] = arrs[i].reshape(wts[k].shape)
    loss = loss.reshape(())

    (o_ada,), (x_ada,) = _copies_wait("ada_chips_wait", ada_start, (res[0], out["grad_w_out"], out["grad_b_in"]),
                                      _chip_copies)
    big("adam_ada_w", x_ada, "ada_w", o_ada)

    return (loss, g["grad_x"], *[out["grad_" + k] for k in _WEIGHTS], *[out["delta_" + k] for k in _WEIGHTS],
            *[out["new_m_" + k] for k in _WEIGHTS], *[out["new_v_" + k] for k in _WEIGHTS])
```

```python
import functools
import math

import jax
import jax.numpy as jnp
from jax import lax
from jax.experimental import pallas as pl
from jax.experimental.pallas import tpu as pltpu

F32 = jnp.float32
BF16 = jnp.bfloat16
MESH = pl.DeviceIdType.MESH

N_DEV = 8
GRID_W = 64
CHUNK = 128
HEADS = 4
EPS = 1e-6
GATE_TAU = 16.0
LANE = 128
ADAM_LR, ADAM_B1, ADAM_B2, ADAM_EPS, ADAM_WD, ADAM_STEP = 0.001, 0.9, 0.999, 1e-08, 0.01, 10
VMEM_LIMIT = 60 * 1024 * 1024
_ANY = pl.BlockSpec(memory_space=pl.ANY)


def _params(**kw):
    return pltpu.CompilerParams(vmem_limit_bytes=VMEM_LIMIT, **kw)


def _tile(n, pref):
    t = (min(pref, n) // LANE) * LANE
    while t >= LANE:
        if n % t == 0:
            return t
        t -= LANE
    return n


def _mm(a, b):
    return jnp.dot(a.astype(BF16), b.astype(BF16), preferred_element_type=F32)


def _mm_nt(a, b):
    return lax.dot_general(a.astype(BF16), b.astype(BF16), (((1,), (1,)), ((), ())), preferred_element_type=F32)


def _mm_tn(a, b):
    return lax.dot_general(a.astype(BF16), b.astype(BF16), (((0,), (0,)), ((), ())), preferred_element_type=F32)


def _mm_tn_hi(a, b):
    return lax.dot_general(a, b, (((0,), (0,)), ((), ())), precision=lax.Precision.HIGHEST, preferred_element_type=F32)


def _sigmoid(x):
    return 0.5 * jnp.tanh(0.5 * x) + 0.5


def _dsilu(x, s):
    return s * (1.0 + x * (1.0 - s))


def _rowsel(table, idx, n):
    out = table[0:1, :]
    for r in range(1, n):
        out = jnp.where(idx == r, table[r:r + 1, :], out)
    return out


def _ada_fwd(cv, ada_w8, ada_b):
    n_sh, _, ws = ada_w8.shape

    def body(cv_ref, w_ref, b_ref, o_ref):
        c = cv_ref[...]
        sv = c * _sigmoid(c)
        for j in range(n_sh):
            cols = pl.ds(j * ws, ws)
            o_ref[:, cols] = _mm(sv, w_ref[j]) + b_ref[:, cols]

    return pl.pallas_call(body, name="ada_fwd", out_shape=jax.ShapeDtypeStruct((cv.shape[0], n_sh * ws), F32),
                          compiler_params=_params())(cv, ada_w8, ada_b)


def _ada_bwd(cv, ada_w8, dmod_ss, small, nb):
    n_sh, d, ws = ada_w8.shape

    def body(cv_ref, w_ref, dm_ref, sm_ref, dw_ref, db_ref, dc_ref):
        c = cv_ref[...]
        s = _sigmoid(c)
        sv = c * s
        dm = jnp.concatenate([dm_ref[:, 0:2 * d], sm_ref[8:16, :]], axis=1)
        db_ref[...] = jnp.sum(dm, axis=0, keepdims=True)
        dsv = None
        for j in range(n_sh):
            dmj = dm[:, j * ws:(j + 1) * ws]
            dw_ref[j] = _mm_tn_hi(sv, dmj).astype(dw_ref.dtype)
            part = _mm_nt(dmj, w_ref[j])
            dsv = part if dsv is None else dsv + part
        dc_ref[...] = dsv * _dsilu(c, s)

    return pl.pallas_call(
        body, name="ada_bwd",
        out_shape=(jax.ShapeDtypeStruct((n_sh, d, ws), BF16), jax.ShapeDtypeStruct((1, n_sh * ws), F32),
                   jax.ShapeDtypeStruct(cv.shape, F32)),
        compiler_params=_params())(cv, ada_w8, dmod_ss, small)


class _Tiles:
    def __init__(self, nb, s_len, c_len, tm, big):
        self.nb, self.tm, self.big = nb, tm, big
        self.lat, self.ctx = s_len // tm, c_len // tm
        self.pad = -(self.lat + self.ctx) % big
        self.per_ex = self.lat + self.ctx + self.pad
        self.n_all = nb * self.per_ex
        self.rows_per_ex = self.per_ex * tm

    def is_lat(self, i):
        return i % self.per_ex < self.lat

    def is_pad(self, i):
        return i % self.per_ex >= self.lat + self.ctx

    def lat_of_all(self, i):
        return (i // self.per_ex) * self.lat + jnp.minimum(i % self.per_ex, self.lat - 1)

    def ctx_of_all(self, i):
        return (i // self.per_ex) * self.ctx + jnp.clip(i % self.per_ex - self.lat, 0, self.ctx - 1)


def _norm_fwd(x2, ctx2, mod, norm_g, tiles):
    tl, d = x2.shape
    tc = ctx2.shape[0]
    nb, tm = tiles.nb, tiles.tm

    def body(x_ref, c_ref, mod_ref, g_ref, u_ref):
        i = pl.program_id(0)
        lat = tiles.is_lat(i)
        xv = jnp.where(lat, x_ref[...], c_ref[...])
        row = jnp.where(lat, i // tiles.per_ex, nb)
        m = _rowsel(mod_ref[...], row, nb + 1)
        shift, scale = m[:, 0:d], m[:, d:2 * d]
        rstd = lax.rsqrt(jnp.mean(xv * xv, axis=-1, keepdims=True) + EPS)
        u = xv * rstd * g_ref[...] * (1.0 + scale) + shift
        u_ref[...] = jnp.where(tiles.is_pad(i), 0.0, u).astype(BF16)

    return pl.pallas_call(
        body, name="norm_fwd", grid=(tiles.n_all,),
        in_specs=[pl.BlockSpec((tm, d), lambda i: (tiles.lat_of_all(i), 0)),
                  pl.BlockSpec((tm, d), lambda i: (tiles.ctx_of_all(i), 0)),
                  pl.BlockSpec(mod.shape, lambda i: (0, 0)),
                  pl.BlockSpec((1, d), lambda i: (0, 0))],
        out_specs=pl.BlockSpec((tm, d), lambda i: (i, 0)),
        out_shape=jax.ShapeDtypeStruct((tiles.n_all * tm, d), BF16),
        compiler_params=_params())(x2, ctx2, mod, norm_g)


def _norm_bwd(x2, ctx2, mod, norm_g, du_lat, du_b, gx1, tiles):
    tl, d = x2.shape
    nb, tm = tiles.nb, tiles.tm
    nrow = mod.shape[0]
    n_lat_in = len(du_lat)

    def body(x_ref, c_ref, mod_ref, g_ref, *refs):
        dl_refs = refs[:n_lat_in]
        d3_ref, gx_ref, gxo_ref, dmod_ref, dg_ref = refs[n_lat_in:]
        i = pl.program_id(0)

        @pl.when(i == 0)
        def _():
            dmod_ref[...] = jnp.zeros_like(dmod_ref)
            dg_ref[...] = jnp.zeros_like(dg_ref)

        lat = tiles.is_lat(i)
        xv = jnp.where(lat, x_ref[...], c_ref[...])
        row = jnp.where(lat, i // tiles.per_ex, nb)
        m = _rowsel(mod_ref[...], row, nb + 1)
        scale = m[:, d:2 * d]
        g = g_ref[...]
        dl = dl_refs[0][...].astype(F32)
        for ref in dl_refs[1:]:
            dl = dl + ref[...].astype(F32)
        du = jnp.where(tiles.is_pad(i), 0.0, d3_ref[...].astype(F32) + jnp.where(lat, dl, 0.0))
        rstd = lax.rsqrt(jnp.mean(xv * xv, axis=-1, keepdims=True) + EPS)
        xh = xv * rstd
        dshift = jnp.sum(du, axis=0, keepdims=True)
        dscale = jnp.sum(du * xh * g, axis=0, keepdims=True)
        dxn = du * (1.0 + scale)
        dg_ref[...] += jnp.sum(dxn * xh, axis=0, keepdims=True)
        dxh = dxn * g
        dx = rstd * (dxh - xh * jnp.mean(dxh * xh, axis=-1, keepdims=True))

        @pl.when(lat)
        def _():
            gxo_ref[...] = dx + gx_ref[...]

        for r in range(nb + 1):
            dmod_ref[r:r + 1, 0:d] += jnp.where(row == r, dshift, 0.0)
            dmod_ref[r:r + 1, d:2 * d] += jnp.where(row == r, dscale, 0.0)

    lat_map = lambda i: (tiles.lat_of_all(i), 0)
    lat_spec = pl.BlockSpec((tm, d), lat_map)
    return pl.pallas_call(
        body, name="norm_bwd", grid=(tiles.n_all,),
        in_specs=[lat_spec,
                  pl.BlockSpec((tm, d), lambda i: (tiles.ctx_of_all(i), 0)),
                  pl.BlockSpec(mod.shape, lambda i: (0, 0)),
                  pl.BlockSpec((1, d), lambda i: (0, 0))]
                 + [lat_spec] * n_lat_in
                 + [pl.BlockSpec((tm, d), lambda i: (i, 0)), lat_spec],
        out_specs=(lat_spec,
                   pl.BlockSpec((nrow, 3 * d), lambda i: (0, 0)),
                   pl.BlockSpec((1, d), lambda i: (0, 0))),
        out_shape=(jax.ShapeDtypeStruct((tl, d), F32), jax.ShapeDtypeStruct((nrow, 3 * d), F32),
                   jax.ShapeDtypeStruct((1, d), F32)),
        compiler_params=_params())(x2, ctx2, mod, norm_g, *du_lat, du_b, gx1)


def _matmul_bias(name, u3, w, b, s_len, tm, tn):
    nb = u3.shape[0]
    d, n = w.shape
    per = s_len // tm
    rows = nb * s_len

    def body(u_ref, w_ref, b_ref, o_ref):
        o_ref[...] = jnp.dot(u_ref[...], w_ref[...], preferred_element_type=F32) + b_ref[...]

    return pl.pallas_call(
        body, name=name, grid=(n // tn, rows // tm),
        in_specs=[pl.BlockSpec((None, tm, d), lambda j, i: (i // per, i % per, 0)),
                  pl.BlockSpec((d, tn), lambda j, i: (0, j)),
                  pl.BlockSpec((1, tn), lambda j, i: (0, j))],
        out_specs=pl.BlockSpec((tm, tn), lambda j, i: (i, j)),
        out_shape=jax.ShapeDtypeStruct((rows, n), F32),
        compiler_params=_params())(u3, w, b)


def _inproj_b(u, w_b, b_b, tm, dk_, dv_):
    t_all, d = u.shape
    nbw = w_b.shape[1]

    def body(u_ref, w_ref, b_ref, qk_ref, v_ref):
        full = jnp.dot(u_ref[...], w_ref[...], preferred_element_type=F32) + b_ref[...]
        qk_ref[:, 0:2 * dk_] = full[:, 0:2 * dk_]
        qk_ref[:, 2 * dk_:2 * dk_ + LANE] = full[:, 2 * dk_ + dv_:nbw]
        v_ref[...] = full[:, 2 * dk_:2 * dk_ + dv_].astype(BF16)

    return pl.pallas_call(
        body, name="inproj_b", grid=(t_all // tm,),
        in_specs=[pl.BlockSpec((tm, d), lambda i: (i, 0)), pl.BlockSpec((d, nbw), lambda i: (0, 0)),
                  pl.BlockSpec((1, nbw), lambda i: (0, 0))],
        out_specs=(pl.BlockSpec((tm, 2 * dk_ + LANE), lambda i: (i, 0)), pl.BlockSpec((tm, dv_), lambda i: (i, 0))),
        out_shape=(jax.ShapeDtypeStruct((t_all, 2 * dk_ + LANE), F32), jax.ShapeDtypeStruct((t_all, dv_), BF16)),
        compiler_params=_params())(u, w_b, b_b)


def _matmul_nt(name, a, w, koff, tm, tk, after=()):
    r, kc = a.shape
    d = w.shape[0]
    nk = kc // tk

    def body(a_ref, w_ref, *rest):
        o_ref = rest[len(after)]
        k = pl.program_id(1)
        p = lax.dot_general(a_ref[...], w_ref[...], (((1,), (1,)), ((), ())), preferred_element_type=F32)
        if nk == 1:
            o_ref[...] = p.astype(o_ref.dtype)
            return
        acc_ref = rest[len(after) + 1]

        @pl.when(k == 0)
        def _():
            acc_ref[...] = p

        @pl.when(k > 0)
        def _():
            acc_ref[...] += p

        @pl.when(k == nk - 1)
        def _():
            o_ref[...] = acc_ref[...].astype(o_ref.dtype)

    return pl.pallas_call(
        body, name=name, grid=(r // tm, nk),
        in_specs=[pl.BlockSpec((tm, tk), lambda i, k: (i, k)),
                  pl.BlockSpec((d, tk), lambda i, k: (0, koff + k))] + [_ANY] * len(after),
        out_specs=pl.BlockSpec((tm, d), lambda i, k: (i, 0)),
        out_shape=jax.ShapeDtypeStruct((r, d), BF16),
        scratch_shapes=[pltpu.VMEM((tm, d), F32)] if nk > 1 else [],
        compiler_params=_params())(a, w, *after)


def _matmul_tn(name, a, b, rows, tk, tn):
    m = a.shape[1]
    n = b.shape[1]
    nk = rows // tk

    def body(a_ref, b_ref, o_ref, s_ref, acc_ref):
        k = pl.program_id(1)
        bv = b_ref[...]
        p = lax.dot_general(bv, a_ref[...], (((0,), (0,)), ((), ())), preferred_element_type=F32)
        cs = jnp.sum(bv.astype(F32), axis=0, keepdims=True)

        @pl.when(k == 0)
        def _():
            acc_ref[...] = p
            s_ref[...] = cs

        @pl.when(k > 0)
        def _():
            acc_ref[...] += p
            s_ref[...] += cs

        @pl.when(k == nk - 1)
        def _():
            o_ref[...] = acc_ref[...].astype(o_ref.dtype)

    return pl.pallas_call(
        body, name=name, grid=(n // tn, nk),
        in_specs=[pl.BlockSpec((tk, m), lambda j, k: (k, 0)),
                  pl.BlockSpec((tk, tn), lambda j, k: (k, j))],
        out_specs=(pl.BlockSpec((tn, m), lambda j, k: (j, 0)), pl.BlockSpec((1, tn), lambda j, k: (0, j))),
        out_shape=(jax.ShapeDtypeStruct((n, m), BF16), jax.ShapeDtypeStruct((1, n), F32)),
        scratch_shapes=[pltpu.VMEM((tn, m), F32)],
        compiler_params=_params())(a, b)


def _matmul_tn_whole(name, a3, b3, rows, tn, transposed):
    nb, _, m = a3.shape
    n = b3.shape[2]

    def body(a_ref, b_ref, o_ref, s_ref):
        p, cs = None, None
        for e in range(nb):
            bv = b_ref[e]
            lhs, rhs = (bv, a_ref[e]) if transposed else (a_ref[e], bv)
            pe = lax.dot_general(lhs, rhs, (((0,), (0,)), ((), ())), preferred_element_type=F32)
            ce = jnp.sum(bv.astype(F32), axis=0, keepdims=True)
            p, cs = (pe, ce) if p is None else (p + pe, cs + ce)
        o_ref[...] = p.astype(o_ref.dtype)
        s_ref[...] = cs

    o_spec, o_shape = ((pl.BlockSpec((tn, m), lambda j: (j, 0)), (n, m)) if transposed
                       else (pl.BlockSpec((m, tn), lambda j: (0, j)), (m, n)))
    return pl.pallas_call(
        body, name=name, grid=(n // tn,),
        in_specs=[pl.BlockSpec((nb, rows, m), lambda j: (0, 0, 0)),
                  pl.BlockSpec((nb, rows, tn), lambda j: (0, 0, j))],
        out_specs=(o_spec, pl.BlockSpec((1, tn), lambda j: (0, j))),
        out_shape=(jax.ShapeDtypeStruct(o_shape, BF16), jax.ShapeDtypeStruct((1, n), F32)),
        compiler_params=_params())(a3, b3)


def _conv_window(pad_ref, r, shift, ktaps, width, horizontal):
    if horizontal:
        return pad_ref[r, pl.ds(16 + shift, width), :]
    return pad_ref[r + ktaps // 2 + shift]


def _conv_row(pad_ref, w, r, ktaps, width, horizontal, flip):
    half = ktaps // 2
    acc = None
    for t in range(ktaps):
        win = _conv_window(pad_ref, r, (half - t) if flip else (t - half), ktaps, width, horizontal)
        term = win * w[t:t + 1, :]
        acc = term if acc is None else acc + term
    return acc


def _fill_padded(ref, val, rows, width, ktaps, horizontal):
    half_k = ktaps // 2
    cb = val.shape[-1]
    if horizontal:
        ref[:, 0:16, :] = jnp.zeros((rows, 16, cb), F32)
        ref[:, 16 + width:32 + width, :] = jnp.zeros((rows, 16, cb), F32)
        ref[:, 16:16 + width, :] = val
    else:
        ref[0:half_k, :, :] = jnp.zeros((half_k, width, cb), F32)
        ref[half_k + rows:2 * half_k + rows, :, :] = jnp.zeros((half_k, width, cb), F32)
        ref[half_k:half_k + rows, :, :] = val


def _conv_fwd(pa, conv_w8, conv_b, nb, s):
    nblk, ktaps, cb = conv_w8.shape
    d = nblk * cb
    rows, width = s // GRID_W, GRID_W
    half_k = ktaps // 2
    nh = nblk // 2

    def body(glu_ref, w_ref, b_ref, o_ref, ph_ref, pv_ref):
        j = pl.program_id(1)
        a0 = (glu_ref[:, 0:cb] * _sigmoid(glu_ref[:, cb:2 * cb])).reshape(rows, width, cb)
        w = w_ref[...]

        bias = b_ref[...]

        def run(pad_ref, horizontal):
            _fill_padded(pad_ref, a0, rows, width, ktaps, horizontal)

            def row(r, carry):
                at = pl.ds(pl.multiple_of(r * width, width), width)
                o_ref[at, :] = _conv_row(pad_ref, w, r, ktaps, width, horizontal, False) + bias
                return carry

            lax.fori_loop(0, rows, row, 0)

        @pl.when(j < nh)
        def _():
            run(ph_ref, True)

        @pl.when(j >= nh)
        def _():
            run(pv_ref, False)

    return pl.pallas_call(
        body, name="conv_fwd", grid=(nb, nblk),
        in_specs=[pl.BlockSpec((s, 2 * cb), lambda b, j: (b, j)),
                  pl.BlockSpec((None, ktaps, cb), lambda b, j: (j, 0, 0)),
                  pl.BlockSpec((1, cb), lambda b, j: (0, j))],
        out_specs=pl.BlockSpec((s, cb), lambda b, j: (b, j)),
        out_shape=jax.ShapeDtypeStruct((nb * s, d), F32),
        scratch_shapes=[pltpu.VMEM((rows, width + 32, cb), F32), pltpu.VMEM((rows + 2 * half_k, width, cb), F32)],
        compiler_params=_params())(pa, conv_w8, conv_b)


def _conv_bwd(pa, da1, conv_w8, nb, s):
    nblk, ktaps, cb = conv_w8.shape
    d = nblk * cb
    rows, width = s // GRID_W, GRID_W
    half_k = ktaps // 2
    nh = nblk // 2

    def body(glu_ref, da_ref, w_ref, dp_ref, dw_ref, db_ref, pha_ref, phd_ref, pva_ref, pvd_ref):
        j = pl.program_id(0)
        b = pl.program_id(1)
        a0 = (glu_ref[:, 0:cb] * _sigmoid(glu_ref[:, cb:2 * cb])).reshape(rows, width, cb)
        da1v = da_ref[...]
        d3 = da1v.reshape(rows, width, cb)
        w = w_ref[...]

        @pl.when(b == 0)
        def _():
            dw_ref[...] = jnp.zeros_like(dw_ref)
            db_ref[...] = jnp.zeros_like(db_ref)

        db_ref[...] += jnp.sum(da1v, axis=0, keepdims=True)

        def run(pa_ref, pd_ref, horizontal):
            _fill_padded(pa_ref, a0, rows, width, ktaps, horizontal)
            _fill_padded(pd_ref, d3, rows, width, ktaps, horizontal)

            def row(r, accs):
                at = pl.ds(pl.multiple_of(r * width, width), width)
                da0 = _conv_row(pd_ref, w, r, ktaps, width, horizontal, True)
                gv = glu_ref[at, 0:cb]
                sg = _sigmoid(glu_ref[at, cb:2 * cb])
                dp_ref[at, 0:cb] = (da0 * sg).astype(BF16)
                dp_ref[at, cb:2 * cb] = (da0 * gv * sg * (1.0 - sg)).astype(BF16)
                d_row = da_ref[at, :]
                out = []
                for t in range(ktaps):
                    prod = _conv_window(pa_ref, r, t - half_k, ktaps, width, horizontal) * d_row
                    out.append(accs[t] + jnp.sum(prod.reshape(width // 8, 8, cb), axis=0))
                return tuple(out)

            accs = lax.fori_loop(0, rows, row, tuple(jnp.zeros((8, cb), F32) for _ in range(ktaps)))
            for t in range(ktaps):
                dw_ref[t:t + 1, :] += jnp.sum(accs[t], axis=0, keepdims=True)

        @pl.when(j < nh)
        def _():
            run(pha_ref, phd_ref, True)

        @pl.when(j >= nh)
        def _():
            run(pva_ref, pvd_ref, False)

    return pl.pallas_call(
        body, name="conv_bwd", grid=(nblk, nb),
        in_specs=[pl.BlockSpec((s, 2 * cb), lambda j, b: (b, j)),
                  pl.BlockSpec((s, cb), lambda j, b: (b, j)),
                  pl.BlockSpec((None, ktaps, cb), lambda j, b: (j, 0, 0))],
        out_specs=(pl.BlockSpec((s, 2 * cb), lambda j, b: (b, j)),
                   pl.BlockSpec((None, ktaps, cb), lambda j, b: (j, 0, 0)),
                   pl.BlockSpec((1, cb), lambda j, b: (0, j))),
        out_shape=(jax.ShapeDtypeStruct((nb * s, 2 * d), BF16),
                   jax.ShapeDtypeStruct((nblk, ktaps, cb), F32), jax.ShapeDtypeStruct((1, d), F32)),
        scratch_shapes=[pltpu.VMEM((rows, width + 32, cb), F32), pltpu.VMEM((rows, width + 32, cb), F32),
                        pltpu.VMEM((rows + 2 * half_k, width, cb), F32),
                        pltpu.VMEM((rows + 2 * half_k, width, cb), F32)],
        compiler_params=_params())(pa, da1, conv_w8)


def _log_sigmoid(x):
    return jnp.minimum(x, 0.0) - jnp.log(1.0 + jnp.exp(-jnp.abs(x)))


def _decay_fwd(pb, up2, bias2, tm, lr_blk):
    t_all = pb.shape[0]
    n2 = up2.shape[1]

    def body(lr_ref, up_ref, b_ref, g_ref):
        logits = _mm(lr_ref[...], up_ref[...]) + b_ref[...]
        g_ref[...] = _log_sigmoid(logits) * (1.0 / GATE_TAU)

    return pl.pallas_call(
        body, name="decay_fwd", grid=(t_all // tm,),
        in_specs=[pl.BlockSpec((tm, LANE), lambda i: (i, lr_blk)),
                  pl.BlockSpec(up2.shape, lambda i: (0, 0)),
                  pl.BlockSpec((1, n2), lambda i: (0, 0))],
        out_specs=pl.BlockSpec((tm, n2), lambda i: (i, 0)),
        out_shape=jax.ShapeDtypeStruct((t_all, n2), F32),
        compiler_params=_params())(pb, up2, bias2)


def _decay_bwd(pb, up2, bias2, grads_f, grads_b, tiles, lr_blk, dk_, dv_):
    t_all = pb.shape[0]
    tm = tiles.tm
    n2 = up2.shape[1]
    nbw = 2 * dk_ + dv_ + LANE

    def body(lr_ref, up_ref, b_ref, dqf, dkf, dvf, dgf, dqb, dkb, dvb, dgb, dp_ref, dup_ref, dbias_ref):
        i = pl.program_id(0)
        pad = tiles.is_pad(i)
        live = lambda v: jnp.where(pad, 0.0, v)

        @pl.when(i == 0)
        def _():
            dup_ref[...] = jnp.zeros_like(dup_ref)
            dbias_ref[...] = jnp.zeros_like(dbias_ref)

        lr = lr_ref[...]
        up = up_ref[...]
        logits = _mm(lr, up) + b_ref[...]
        dg = live(jnp.concatenate([dgf[...], dgb[...]], axis=1))
        dlog = dg * (1.0 / GATE_TAU) * _sigmoid(-logits)
        dup_ref[...] += _mm_tn(lr, dlog)
        dbias_ref[...] += jnp.sum(dlog, axis=0, keepdims=True)
        both = lambda f, b: live(f[...].astype(F32) + b[...].astype(F32)).astype(BF16)
        dp_ref[:, 0:dk_] = both(dqf, dqb)
        dp_ref[:, dk_:2 * dk_] = both(dkf, dkb)
        dp_ref[:, 2 * dk_:2 * dk_ + dv_] = both(dvf, dvb)
        dp_ref[:, 2 * dk_ + dv_:nbw] = _mm_nt(dlog, up).astype(BF16)

    row = lambda w: pl.BlockSpec((tm, w), lambda i: (i, 0))
    return pl.pallas_call(
        body, name="decay_bwd", grid=(t_all // tm,),
        in_specs=[pl.BlockSpec((tm, LANE), lambda i: (i, lr_blk)),
                  pl.BlockSpec(up2.shape, lambda i: (0, 0)),
                  pl.BlockSpec((1, n2), lambda i: (0, 0)),
                  row(dk_), row(dk_), row(dv_), row(dk_), row(dk_), row(dk_), row(dv_), row(dk_)],
        out_specs=(row(nbw), pl.BlockSpec(up2.shape, lambda i: (0, 0)), pl.BlockSpec((1, n2), lambda i: (0, 0))),
        out_shape=(jax.ShapeDtypeStruct((t_all, nbw), BF16), jax.ShapeDtypeStruct(up2.shape, F32),
                   jax.ShapeDtypeStruct((1, n2), F32)),
        compiler_params=_params())(pb, up2, bias2, *grads_f, *grads_b)


def _scan_chunk(s, nl, nc, rev):
    if rev:
        return jnp.where(s < nc, nl + (nc - 1 - s), nl - 1 - (s - nc))
    return jnp.where(s < nc, nl + s, s - nc)


def _scan_lat_chunk(s, nl, nc, rev):
    first = nl - 1 if rev else 0
    return jnp.where(s < nc, first, _scan_chunk(s, nl, nc, rev))


def _tri_mm(m_bf, x):
    hi = x.astype(BF16)
    r1 = x - hi.astype(F32)
    mid = r1.astype(BF16)
    lo = (r1 - mid.astype(F32)).astype(BF16)
    dot = lambda p: jnp.dot(m_bf, p, preferred_element_type=F32)
    return dot(hi) + dot(mid) + dot(lo)


def _chunk_masks(c, rev):
    ii = lax.broadcasted_iota(jnp.int32, (c, c), 0)
    jj = lax.broadcasted_iota(jnp.int32, (c, c), 1)
    return ((ii <= jj), (ii >= jj)) if rev else ((ii >= jj), (ii <= jj))


def _chunk_terms(q, k, b, far, mid):
    bf, bm = b[far:far + 1, :], b[mid:mid + 1, :]
    e = jnp.exp(b)
    em = jnp.exp(b - bm)
    eim = jnp.exp(bm - b)
    ed = jnp.exp(bf - b)
    return dict(e=e, em=em, eim=eim, ed=ed, dec=jnp.exp(bf), qe=q * e, qem=q * em, kim=k * eim, kd=k * ed)


def _gla_fwd(pb3, pv3, g3, nb, s_len, c_len, dk_, dv_):
    c = CHUNK
    nl, nc = s_len // c, c_len // c
    ns = nl + nc
    hk, hv = dk_ // HEADS, dv_ // HEADS
    l_len = pb3.shape[1]
    scale = hk ** -0.5
    mid = c // 2

    def body(*refs):
        ins, outs, z_scr = refs[:8], refs[8:14], refs[14]
        s = pl.program_id(0)

        @pl.when(s == 0)
        def _():
            z_scr[...] = jnp.zeros_like(z_scr)

        qs = jnp.where(s >= nc, scale, 0.0)
        for di, rev in enumerate((False, True)):
            q_ref, k_ref, v_ref, g_ref = ins[4 * di:4 * di + 4]
            o_ref, zs_ref, b_ref = outs[3 * di:3 * di + 3]
            mask, _ = _chunk_masks(c, rev)
            m_bf = mask.astype(BF16)
            far = 0 if rev else c - 1
            for b in range(nb):
                bc = _tri_mm(m_bf, g_ref[b])
                b_ref[b] = bc
                for h in range(HEADS):
                    ks, vs = slice(h * hk, (h + 1) * hk), slice(h * hv, (h + 1) * hv)
                    zi = (di * nb + b) * HEADS + h
                    v = v_ref[b, :, vs]
                    t = _chunk_terms(q_ref[b, :, ks] * qs, k_ref[b, :, ks], bc[:, ks], far, mid)
                    a = jnp.where(mask, _mm_nt(t["qem"], t["kim"]), 0.0)
                    z = z_scr[zi]
                    zs_ref[0, b * HEADS + h] = z
                    o_ref[b, :, vs] = _mm(a, v) + _mm_nt(t["qe"], z)
                    z_scr[zi] = z * t["dec"] + _mm_tn(v, t["kd"])

    in_specs, out_specs, out_shape = [], [], []
    for di, rev in enumerate((False, True)):
        ch = functools.partial(_scan_chunk, nl=nl, nc=nc, rev=rev)
        lch = functools.partial(_scan_lat_chunk, nl=nl, nc=nc, rev=rev)
        in_specs += [pl.BlockSpec((nb, c, dk_), lambda s, ch=ch: (0, ch(s), 0)),
                     pl.BlockSpec((nb, c, dk_), lambda s, ch=ch: (0, ch(s), 1)),
                     pl.BlockSpec((nb, c, dv_), lambda s, ch=ch: (0, ch(s), 0)),
                     pl.BlockSpec((nb, c, dk_), lambda s, ch=ch, di=di: (0, ch(s), di))]
        out_specs += [pl.BlockSpec((nb, c, dv_), lambda s, lch=lch: (0, lch(s), 0)),
                      pl.BlockSpec((1, nb * HEADS, hv, hk), lambda s: (s, 0, 0, 0)),
                      pl.BlockSpec((nb, c, dk_), lambda s, ch=ch: (0, ch(s), 0))]
        out_shape += [jax.ShapeDtypeStruct((nb, s_len, dv_), F32),
                      jax.ShapeDtypeStruct((ns, nb * HEADS, hv, hk), F32),
                      jax.ShapeDtypeStruct((nb, l_len, dk_), F32)]
    return pl.pallas_call(
        body, name="gla_fwd", grid=(ns,), in_specs=in_specs, out_specs=tuple(out_specs), out_shape=tuple(out_shape),
        scratch_shapes=[pltpu.VMEM((2 * nb * HEADS, hv, hk), F32)],
        compiler_params=_params())(pb3, pb3, pv3, g3, pb3, pb3, pv3, g3)


def _gla_bwd(pb3, pv3, do3, fwd_saved, nb, s_len, c_len, dk_, dv_):
    c = CHUNK
    nl, nc = s_len // c, c_len // c
    ns = nl + nc
    hk, hv = dk_ // HEADS, dv_ // HEADS
    l_len = pb3.shape[1]
    scale = hk ** -0.5
    mid = c // 2
    zs_f, b_f, zs_b, b_b = fwd_saved

    def body(*refs):
        ins, outs, dz_scr = refs[:12], refs[12:20], refs[20]
        s = pl.program_id(0)
        step = ns - 1 - s

        @pl.when(s == 0)
        def _():
            dz_scr[...] = jnp.zeros_like(dz_scr)

        lat = step >= nc
        qs = jnp.where(lat, scale, 0.0)
        dmul = jnp.where(lat, 1.0, 0.0)
        for di, rev in enumerate((False, True)):
            q_ref, k_ref, v_ref, b_ref, do_ref, zs_ref = ins[6 * di:6 * di + 6]
            dq_ref, dk_ref, dv_ref, dg_ref = outs[4 * di:4 * di + 4]
            mask, mask_t = _chunk_masks(c, rev)
            mt_bf = mask_t.astype(BF16)
            far = 0 if rev else c - 1
            far_row = lax.broadcasted_iota(jnp.int32, (c, hk), 0) == far
            for b in range(nb):
                db_parts = []
                for h in range(HEADS):
                    ks, vs = slice(h * hk, (h + 1) * hk), slice(h * hv, (h + 1) * hv)
                    zi = (di * nb + b) * HEADS + h
                    v = v_ref[b, :, vs]
                    d_o = do_ref[b, :, vs] * dmul
                    t = _chunk_terms(q_ref[b, :, ks] * qs, k_ref[b, :, ks], b_ref[b, :, ks], far, mid)
                    qem, kim, qe, kd = t["qem"], t["kim"], t["qe"], t["kd"]
                    a_t = jnp.where(mask_t, _mm_nt(kim, qem), 0.0)
                    d_a = jnp.where(mask, _mm_nt(d_o, v), 0.0)
                    d_at = jnp.where(mask_t, _mm_nt(v, d_o), 0.0)
                    z = zs_ref[0, b * HEADS + h]
                    dzn = dz_scr[zi]
                    dv_ref[b, :, vs] = (_mm(a_t, d_o) + _mm_nt(kd, dzn)).astype(dv_ref.dtype)
                    dqem = _mm(d_a, kim)
                    dkim = _mm(d_at, qem)
                    dqe = _mm(d_o, z)
                    dkd = _mm(v, dzn)
                    ddec = jnp.sum(z * dzn, axis=0, keepdims=True)
                    dz_scr[zi] = dzn * t["dec"] + _mm_tn(d_o, qe)
                    dq_ref[b, :, ks] = ((dqem * t["em"] + dqe * t["e"]) * qs).astype(dq_ref.dtype)
                    dk_ref[b, :, ks] = (dkim * t["eim"] + dkd * t["ed"]).astype(dk_ref.dtype)
                    db = dqem * qem - dkim * kim + dqe * qe - dkd * kd
                    extra = jnp.sum(dkd * kd, axis=0, keepdims=True) + ddec * t["dec"]
                    db_parts.append(db + jnp.where(far_row, extra, 0.0))
                dg_ref[b] = _tri_mm(mt_bf, jnp.concatenate(db_parts, axis=1))

    in_specs, out_specs, out_shape, args = [], [], [], []
    for di, rev in enumerate((False, True)):
        ch = lambda s, rev=rev: _scan_chunk(ns - 1 - s, nl, nc, rev)
        lch = lambda s, rev=rev: _scan_lat_chunk(ns - 1 - s, nl, nc, rev)
        in_specs += [pl.BlockSpec((nb, c, dk_), lambda s, ch=ch: (0, ch(s), 0)),
                     pl.BlockSpec((nb, c, dk_), lambda s, ch=ch: (0, ch(s), 1)),
                     pl.BlockSpec((nb, c, dv_), lambda s, ch=ch: (0, ch(s), 0)),
                     pl.BlockSpec((nb, c, dk_), lambda s, ch=ch: (0, ch(s), 0)),
                     pl.BlockSpec((nb, c, dv_), lambda s, lch=lch: (0, lch(s), 0)),
                     pl.BlockSpec((1, nb * HEADS, hv, hk), lambda s: (ns - 1 - s, 0, 0, 0))]
        args += [pb3, pb3, pv3, (b_b if rev else b_f), do3, (zs_b if rev else zs_f)]
        for w, dt in ((dk_, BF16), (dk_, BF16), (dv_, BF16), (dk_, F32)):
            out_specs.append(pl.BlockSpec((nb, c, w), lambda s, ch=ch: (0, ch(s), 0)))
            out_shape.append(jax.ShapeDtypeStruct((nb, l_len, w), dt))
    return pl.pallas_call(
        body, name="gla_bwd", grid=(ns,), in_specs=in_specs, out_specs=tuple(out_specs), out_shape=tuple(out_shape),
        scratch_shapes=[pltpu.VMEM((2 * nb * HEADS, hv, hk), F32)],
        compiler_params=_params())(*args)


def _tail(a1, pa, o_f, o_b, x2, tgt, mod, wc, wg, wo, ln_g, ln_b, gn_t, fg, nb, tm, n_split):
    tl, d = x2.shape
    nt = tl // tm
    per_ex = nt // nb
    hv = d // HEADS
    nrow = mod.shape[0]

    def part(shared, a1_ref, z_ref, r_ref, mc_ref, mg_ref, of_ref, ob_ref, x_ref, t_ref,
             dp_ref, da1_ref, do_ref, gx_ref, mrg_ref, dmo_ref, yci_ref, dyc_ref, ogi_ref, dyg_ref, sm_ref):
        bidx, gate, lng, lnb, fgv, gn, wc_, wg_, wo_ = shared

        a1v = a1_ref[...]
        mu = jnp.mean(a1v, axis=-1, keepdims=True)
        xc = a1v - mu
        rs = lax.rsqrt(jnp.mean(xc * xc, axis=-1, keepdims=True) + EPS)
        xh = xc * rs
        a2 = xh * lng + lnb
        s2 = _sigmoid(a2)
        a3 = a2 * s2
        zv = z_ref[...]
        sz = _sigmoid(zv)
        siluz = zv * sz
        ycin = a3 * siluz
        yconv = _mm(ycin, wc_)

        o = of_ref[...] + ob_ref[...]
        ohat_parts, rn_parts = [], []
        for h in range(HEADS):
            oh = o[:, h * hv:(h + 1) * hv]
            rn = lax.rsqrt(jnp.mean(oh * oh, axis=-1, keepdims=True) + EPS)
            ohat_parts.append(oh * rn)
            rn_parts.append(rn)
        ohat = jnp.concatenate(ohat_parts, axis=1)
        on = ohat * gn
        rv = r_ref[...]
        sr = _sigmoid(rv)
        silur = rv * sr
        ogin = on * silur
        ygla = _mm(ogin, wg_)

        sc = _sigmoid(mc_ref[...])
        sg = _sigmoid(mg_ref[...])
        merged = sc * yconv + sg * ygla
        mo = _mm(merged, wo_)
        hn = x_ref[...] + gate * mo
        rf = lax.rsqrt(jnp.mean(hn * hn, axis=-1, keepdims=True) + EPS)
        yh = hn * rf
        err = yh * fgv - t_ref[...]
        loss_part = 0.5 * jnp.sum(err * err) * (1.0 / d)

        dy = err * (1.0 / d)
        dfg = jnp.sum(dy * yh, axis=0, keepdims=True)
        dyh = dy * fgv
        dhn = rf * (dyh - yh * jnp.mean(dyh * yh, axis=-1, keepdims=True))
        gx_ref[...] = dhn
        dgate = jnp.sum(dhn * mo, axis=0, keepdims=True)
        dmo = gate * dhn
        dmerged = _mm_nt(dmo, wo_)
        dyconv = dmerged * sc
        dygla = dmerged * sg
        dp_ref[:, 2 * d:3 * d] = (dmerged * yconv * sc * (1.0 - sc)).astype(BF16)
        dp_ref[:, 3 * d:4 * d] = (dmerged * ygla * sg * (1.0 - sg)).astype(BF16)
        dycin = _mm_nt(dyconv, wc_)
        dogin = _mm_nt(dygla, wg_)
        mrg_ref[...] = merged.astype(BF16)
        dmo_ref[...] = dmo.astype(BF16)
        yci_ref[...] = ycin.astype(BF16)
        dyc_ref[...] = dyconv.astype(BF16)
        ogi_ref[...] = ogin.astype(BF16)
        dyg_ref[...] = dygla.astype(BF16)

        da3 = dycin * siluz
        dp_ref[:, 0:d] = (dycin * a3 * _dsilu(zv, sz)).astype(BF16)
        da2 = da3 * _dsilu(a2, s2)
        dlng = jnp.sum(da2 * xh, axis=0, keepdims=True)
        dlnb = jnp.sum(da2, axis=0, keepdims=True)
        dxh = da2 * lng
        da1_ref[...] = rs * (dxh - jnp.mean(dxh, axis=-1, keepdims=True)
                             - xh * jnp.mean(dxh * xh, axis=-1, keepdims=True))

        don = dogin * silur
        dp_ref[:, d:2 * d] = (dogin * on * _dsilu(rv, sr)).astype(BF16)
        dgn = jnp.sum(don * ohat, axis=0, keepdims=True)
        dyn = don * gn
        for h in range(HEADS):
            vs = slice(h * hv, (h + 1) * hv)
            oh_hat = ohat_parts[h]
            dh = dyn[:, vs]
            do_ref[:, vs] = (rn_parts[h] * (dh - oh_hat * jnp.mean(dh * oh_hat, axis=-1, keepdims=True))
                             ).astype(BF16)

        sm_ref[0:1, :] += dfg
        sm_ref[1:2, :] += dlng
        sm_ref[2:3, :] += dlnb
        sm_ref[3:4, :] += dgn
        sm_ref[4:5, :] += jnp.zeros((1, d), F32) + loss_part
        for b in range(nb):
            sm_ref[8 + b:9 + b, :] += jnp.where(bidx == b, dgate, 0.0)

    def body(*refs):
        mod_ref, wc_ref, wg_ref, wo_ref, lng_ref, lnb_ref, gn_ref, fg_ref = refs[9:17]
        sm_ref = refs[27]
        i = pl.program_id(0)

        @pl.when(i == 0)
        def _():
            sm_ref[...] = jnp.zeros_like(sm_ref)

        bidx = i // per_ex
        shared = (bidx, _rowsel(mod_ref[...], bidx, nb)[:, 2 * d:3 * d], lng_ref[...], lnb_ref[...], fg_ref[...],
                  jnp.concatenate([gn_ref[...]] * HEADS, axis=1), wc_ref[...], wg_ref[...], wo_ref[...])
        rows_per = tm // n_split
        for p in range(n_split):
            rows = pl.ds(p * rows_per, rows_per)
            part(shared, *[r.at[rows] for r in refs[0:9]], *[r.at[rows] for r in refs[17:27]], sm_ref)

    row = pl.BlockSpec((tm, d), lambda i: (i, 0))
    pcol = lambda blk: pl.BlockSpec((tm, d), lambda i: (i, blk))
    full = lambda arr: pl.BlockSpec(arr.shape, lambda i: (0,) * arr.ndim)
    bfo = jax.ShapeDtypeStruct((tl, d), BF16)
    f32o = jax.ShapeDtypeStruct((tl, d), F32)
    return pl.pallas_call(
        body, name="tail", grid=(nt,),
        in_specs=[row, pcol(2), pcol(3), pcol(4), pcol(5), row, row, row, row, full(mod), full(wc), full(wg),
                  full(wo), full(ln_g), full(ln_b), full(gn_t), full(fg)],
        out_specs=(pl.BlockSpec((tm, 4 * d), lambda i: (i, 0)), row, row, row, row, row, row, row, row, row,
                   pl.BlockSpec((16, d), lambda i: (0, 0))),
        out_shape=(jax.ShapeDtypeStruct((tl, 4 * d), BF16), f32o, bfo, f32o, bfo, bfo, bfo, bfo, bfo, bfo,
                   jax.ShapeDtypeStruct((16, d), F32)),
        compiler_params=_params())(a1, pa, pa, pa, pa, o_f, o_b, x2, tgt, mod, wc, wg, wo, ln_g, ln_b, gn_t, fg)


def _local_step(x, c, ctx, tgt, c_ctx, ada_w8, ada_b, norm_g, w_a, b_a, w_b, b_b, conv_w8, conv_b, ln_g, ln_b,
                up2, bias2, gla_norm_g, final_norm_g, proj, on_grads=None, on_du_a1=None):
    nb, s_len, d = x.shape
    c_len = ctx.shape[1]
    dk_, dv_ = d // 2, d
    tl, tc = nb * s_len, nb * c_len
    nbw = 2 * dk_ + dv_ + LANE
    tm = math.gcd(256, c_len)
    tiles = _Tiles(nb, s_len, c_len, tm, 2)
    l_len = tiles.rows_per_ex
    t_all = nb * l_len
    x2, ctx2, tgt2 = x.reshape(tl, d), ctx.reshape(tc, d), tgt.reshape(tl, d)

    cv = jnp.zeros((8, d), F32).at[0:nb].set(c).at[nb].set(c_ctx.reshape(d))
    mod = _ada_fwd(cv, ada_w8, ada_b)
    u = _norm_fwd(x2, ctx2, mod, norm_g, tiles)
    u3 = u.reshape(nb, l_len, d)
    tma = math.gcd(1024, s_len)
    pa = _matmul_bias("inproj_a", u3, w_a, b_a, s_len, tma, _tile(6 * d, 2048))
    tmb = math.gcd(1024, t_all)
    pb, pv = _inproj_b(u, w_b, b_b, tmb, dk_, dv_)

    a1 = _conv_fwd(pa, conv_w8, conv_b, nb, s_len)
    lr_blk = (2 * dk_) // LANE
    g_all = _decay_fwd(pb, up2, bias2, tm, lr_blk)
    pb3, pv3 = pb.reshape(nb, l_len, 2 * dk_ + LANE), pv.reshape(nb, l_len, dv_)
    o_f, zs_f, b_f, o_b, zs_b, b_b2 = _gla_fwd(pb3, pv3, g_all.reshape(nb, l_len, 2 * dk_), nb, s_len, c_len,
                                               dk_, dv_)

    conv_proj, gla_proj, w_out = proj(a1) if callable(proj) else proj
    tt = math.gcd(256, s_len)
    (dp_a2, da1, d_o, gx1, merged, dmo, ycin, dyconv, ogin, dygla, small) = _tail(
        a1, pa, o_f.reshape(tl, dv_), o_b.reshape(tl, dv_), x2, tgt2, mod, conv_proj, gla_proj, w_out, ln_g, ln_b,
        gla_norm_g, final_norm_g, nb, tt, 2)

    lat3 = lambda a: a.reshape(nb, s_len, a.shape[-1])
    tnw = _tile(d, 1024)
    tnp = _tile(d, 512)
    d_w_out, _ = _matmul_tn_whole("dw_out", lat3(merged), lat3(dmo), s_len, tnp, False)
    d_conv_proj, _ = _matmul_tn_whole("dw_conv_proj", lat3(ycin), lat3(dyconv), s_len, tnp, False)
    d_gla_proj, _ = _matmul_tn_whole("dw_gla_proj", lat3(ogin), lat3(dygla), s_len, tnp, False)

    dp_a1, d_conv_w8, d_conv_b = _conv_bwd(pa, da1, conv_w8, nb, s_len)
    gl = _gla_bwd(pb3, pv3, d_o.reshape(nb, s_len, dv_), (zs_f, b_f, zs_b, b_b2), nb, s_len, c_len, dk_, dv_)
    gl = [g_.reshape(t_all, g_.shape[-1]) for g_ in gl]
    dp_b, d_up2, d_bias2 = _decay_bwd(pb, up2, bias2, gl[0:4], gl[4:8], tiles, lr_blk, dk_, dv_)

    dw_a1, db_a1 = _matmul_tn_whole("dw_a1", u3, lat3(dp_a1), s_len, tnw, True)
    dw_a2, db_a2 = _matmul_tn_whole("dw_a2", u3, lat3(dp_a2), s_len, tnw, True)
    dw_b, db_b = _matmul_tn("dw_b", u, dp_b, t_all, tmb, nbw)
    grads = dict(w_a1=dw_a1, w_a2=dw_a2, w_b=dw_b, conv_w8=d_conv_w8, conv_proj=d_conv_proj, up2=d_up2,
                 gla_proj=d_gla_proj, w_out=d_w_out)

    tka = _tile(2 * d, 2048)
    du_a1 = _matmul_nt("du_a1", dp_a1, w_a, 0, tma, tka, after=on_grads(grads) if on_grads else ())
    du_a2 = _matmul_nt("du_a2", dp_a2, w_a, (2 * d) // tka, tma, tka, after=on_du_a1(du_a1) if on_du_a1 else ())
    du_b = _matmul_nt("du_b", dp_b, w_b, 0, tmb, nbw)
    grad_x2, dmod_ss, d_norm_g = _norm_bwd(x2, ctx2, mod, norm_g, [du_a1, du_a2], du_b, gx1, tiles)
    d_ada_w8, d_ada_b, d_cv = _ada_bwd(cv, ada_w8, dmod_ss, small, nb)

    return dict(
        grads, grad_x=grad_x2.reshape(nb, s_len, d), small=small, cv=d_cv, ada_w8=d_ada_w8, ada_b=d_ada_b,
        norm_g=d_norm_g, b_a1=db_a1, b_a2=db_a2, b_b=db_b, conv_b=d_conv_b, bias2=d_bias2)


def _regroup_pieces(d, r, wshard):
    cb = d // N_DEV
    segs = []
    for j in range(N_DEV):
        segs.append((j * cb, cb, 0, 2 * j * cb))
    for j in range(N_DEV):
        segs.append((d + j * cb, cb, 0, (2 * j + 1) * cb))
    segs += [(2 * d, d, 0, 2 * d), (3 * d, 2 * d + 2 * r, 1, 0), (5 * d + 2 * r, 3 * d, 0, 3 * d)]
    pieces = []
    for o0, w, dst, d0 in segs:
        lo = o0
        while lo < o0 + w:
            j = lo // wshard
            hi = min(o0 + w, (j + 1) * wshard)
            pieces.append((j, lo - j * wshard, hi - lo, dst, d0 + lo - o0))
            lo = hi
    return pieces


def _regroup(o, d, r):
    n_in = 8 * d + 2 * r
    parts = ([], [])
    for _, s0, n, dst, _ in sorted(_regroup_pieces(d, r, n_in), key=lambda p: (p[3], p[4])):
        parts[dst].append(o[..., s0:s0 + n])
    pad = jnp.zeros(o.shape[:-1] + (LANE - 2 * r,), o.dtype)
    return jnp.concatenate(parts[0], axis=-1), jnp.concatenate(parts[1] + [pad], axis=-1)


def _unshard_w_in(g_win, d, r, after=()):
    n_sh, _, ws = g_win.shape
    nbw = 2 * d + LANE
    pieces = _regroup_pieces(d, r, ws)
    tr = math.gcd(d, 256)

    def body(g_ref, *rest):
        a_ref, b_ref = rest[len(after):]
        dsts = (a_ref, b_ref)
        for j, s0, n, dst, d0 in pieces:
            dsts[dst][:, pl.ds(d0, n)] = g_ref[j, :, pl.ds(s0, n)]
        b_ref[:, pl.ds(2 * d + 2 * r, LANE - 2 * r)] = jnp.zeros((tr, LANE - 2 * r), b_ref.dtype)

    return pl.pallas_call(
        body, name="unshard_w_in", grid=(d // tr,),
        in_specs=[pl.BlockSpec((n_sh, tr, ws), lambda i: (0, i, 0))] + [_ANY] * len(after),
        out_specs=(pl.BlockSpec((tr, 6 * d), lambda i: (i, 0)), pl.BlockSpec((tr, nbw), lambda i: (i, 0))),
        out_shape=(jax.ShapeDtypeStruct((d, 6 * d), g_win.dtype), jax.ShapeDtypeStruct((d, nbw), g_win.dtype)),
        compiler_params=_params())(g_win, *after)


def _reshard_w_in(dwt_a1, dwt_a2, dwt_b, d, r):
    ws = (8 * d + 2 * r) // N_DEV
    pieces = _regroup_pieces(d, r, ws)
    tc = math.gcd(d, 256)

    def body(a1_ref, a2_ref, b_ref, o_ref):
        for j, s0, n, dst, d0 in pieces:
            if dst == 1:
                src = b_ref[pl.ds(d0, n), :]
            elif d0 < 2 * d:
                src = a1_ref[pl.ds(d0, n), :]
            else:
                src = a2_ref[pl.ds(d0 - 2 * d, n), :]
            o_ref[j, pl.ds(s0, n), :] = src

    col = lambda h: pl.BlockSpec((h, tc), lambda i: (0, i))
    return pl.pallas_call(
        body, name="reshard_w_in", grid=(d // tc,),
        in_specs=[col(2 * d), col(4 * d), col(2 * d + LANE)],
        out_specs=pl.BlockSpec((N_DEV, ws, tc), lambda i: (0, 0, i)),
        out_shape=jax.ShapeDtypeStruct((N_DEV, ws, d), dwt_b.dtype),
        compiler_params=_params())(dwt_a1, dwt_a2, dwt_b)


_SMALL = ("c_ctx", "ada_b", "norm_g", "b_in", "conv_b", "conv_ln_g", "conv_ln_b", "decay_bias_fwd",
          "decay_bias_bwd", "gla_norm_g", "final_norm_g")


def _small_layout(d, r):
    sizes = dict(c_ctx=d, ada_b=3 * d, norm_g=d, b_in=8 * d + 2 * r, conv_b=d, conv_ln_g=d, conv_ln_b=d,
                 decay_bias_fwd=d // 2, decay_bias_bwd=d // 2, gla_norm_g=d // HEADS, final_norm_g=d, loss=1)
    table, off = {}, 0
    for name in _SMALL + ("loss",):
        table[name] = (off, sizes[name])
        off += -(-sizes[name] // LANE) * LANE
    return table, off


def _pack_small(g, nb, d, r):
    table, width = _small_layout(d, r)
    hv = d // HEADS
    pieces = _regroup_pieces(d, r, 8 * d + 2 * r)
    names = ("small", "cv", "ada_b", "norm_g", "b_a1", "b_a2", "b_b", "conv_b", "bias2")

    def body(sm, cv, ab, ng, ba1, ba2, bb, cvb, b2, o_ref):
        o_ref[...] = jnp.zeros_like(o_ref)

        def put(name, val):
            off, n = table[name]
            o_ref[:, pl.ds(off, n)] = val

        put("c_ctx", cv[nb:nb + 1, :])
        put("ada_b", ab[...])
        put("norm_g", ng[...])
        off_b = table["b_in"][0]
        for _, s0, n, dst, d0 in pieces:
            if dst == 1:
                src = bb[:, pl.ds(d0, n)]
            elif d0 < 2 * d:
                src = ba1[:, pl.ds(d0, n)]
            else:
                src = ba2[:, pl.ds(d0 - 2 * d, n)]
            o_ref[:, pl.ds(off_b + s0, n)] = src
        put("conv_b", cvb[...])
        put("conv_ln_g", sm[1:2, :])
        put("conv_ln_b", sm[2:3, :])
        put("decay_bias_fwd", b2[:, 0:d // 2])
        put("decay_bias_bwd", b2[:, d // 2:d])
        gn = sm[3:4, 0:hv]
        for h in range(1, HEADS):
            gn = gn + sm[3:4, h * hv:(h + 1) * hv]
        put("gla_norm_g", gn)
        put("final_norm_g", sm[0:1, :])
        put("loss", sm[4:5, 0:1])

    return pl.pallas_call(body, name="pack_small", out_shape=jax.ShapeDtypeStruct((1, width), F32),
                          compiler_params=_params())(*[g[k] for k in names])


def _small_adam(parts, ws, ms, vs, d, r):
    table, width = _small_layout(d, r)
    n_parts = parts.shape[0]
    k = len(_SMALL)
    bc1 = 1.0 - ADAM_B1 ** ADAM_STEP
    bc2 = 1.0 - ADAM_B2 ** ADAM_STEP

    def body(p_ref, *refs):
        w_refs, m_refs, v_refs = refs[0:k], refs[k:2 * k], refs[2 * k:3 * k]
        outs = refs[3 * k:]
        tot = p_ref[0]
        for i in range(1, n_parts):
            tot = tot + p_ref[i]
        for i, name in enumerate(_SMALL):
            off, n = table[name]
            g = tot[:, off:off + n]
            mn = ADAM_B1 * m_refs[i][...] + (1.0 - ADAM_B1) * g
            vn = ADAM_B2 * v_refs[i][...] + (1.0 - ADAM_B2) * (g * g)
            outs[i][...] = g
            outs[k + i][...] = -ADAM_LR * ((mn / bc1) / (jnp.sqrt(vn / bc2) + ADAM_EPS) + ADAM_WD * w_refs[i][...])
            outs[2 * k + i][...] = mn
            outs[3 * k + i][...] = vn
        off, _ = table["loss"]
        outs[4 * k][...] = tot[:, off:off + 1]

    shapes = [jax.ShapeDtypeStruct(w.shape, F32) for w in ws]
    res = pl.pallas_call(body, name="small_adam", out_shape=tuple(shapes * 4 + [jax.ShapeDtypeStruct((1, 1), F32)]),
                         compiler_params=_params())(parts, *ws, *ms, *vs)
    return res[0:k], res[k:2 * k], res[2 * k:3 * k], res[3 * k:4 * k], res[4 * k]


def _mesh_pos():
    return lax.axis_index("x"), lax.axis_index("y"), lax.axis_index("c")


def _all_gather(arrs):
    n = len(arrs)
    ns = 9
    split = [a.ndim == 2 and a.shape[0] % 32 == 0 for a in arrs]

    def body(*refs):
        ins, outs = refs[:n], refs[n:2 * n]
        send_sems, recv_sems, local_sems = refs[2 * n:]
        x, y, c = _mesh_pos()
        me, sibling = (x, y, c), (x, y, 1 - c)
        xn, yn, dg = (1 - x, y, c), (x, 1 - y, c), (1 - x, 1 - y, c)
        other = lambda pos: (pos[0], pos[1], 1 - c)

        def slot(a, pos, half):
            ref = outs[a].at[4 * pos[0] + 2 * pos[1] + pos[2]]
            if half is None:
                return ref
            rows = arrs[a].shape[0] // 2
            return ref.at[pl.ds(half * rows, rows)]

        def copy(a, k, block, to, src=None, half=None):
            dst = slot(a, block, half)
            return pltpu.make_async_remote_copy(
                src_ref=dst if src is None else src, dst_ref=dst,
                send_sem=send_sems.at[ns * a + k], recv_sem=recv_sems.at[ns * a + k],
                device_id=to, device_id_type=MESH)

        h0 = lambda a: 0 if split[a] else None
        mine = [pltpu.make_async_copy(ins[a], slot(a, me, None), local_sems.at[a]) for a in range(n)]
        for cp in mine:
            cp.start()
        sent = []
        for a in range(n):
            sent += [copy(a, 0, me, sibling, src=ins[a]), copy(a, 1, me, xn, src=ins[a]),
                     copy(a, 2, me, yn, src=ins[a])]
        for cp in sent:
            cp.start()

        def pass_on(cp):
            cp.start()
            sent.append(cp)

        for a in range(n):
            copy(a, 1, xn, me).wait_recv()
            pass_on(copy(a, 3, xn, sibling))
            pass_on(copy(a, 4, xn, yn, half=h0(a)))
        for a in range(n):
            copy(a, 2, yn, me).wait_recv()
            pass_on(copy(a, 5, yn, sibling))
            if split[a]:
                pass_on(copy(a, 6, yn, xn, half=1))
        for a in range(n):
            copy(a, 4, dg, me, half=h0(a)).wait_recv()
            pass_on(copy(a, 7, dg, sibling, half=h0(a)))
            if split[a]:
                copy(a, 6, dg, me, half=1).wait_recv()
                pass_on(copy(a, 8, dg, sibling, half=1))
        for a in range(n):
            copy(a, 0, sibling, me).wait_recv()
            copy(a, 3, other(xn), me).wait_recv()
            copy(a, 5, other(yn), me).wait_recv()
            copy(a, 7, other(dg), me, half=h0(a)).wait_recv()
            if split[a]:
                copy(a, 8, other(dg), me, half=1).wait_recv()
        for cp in sent:
            cp.wait_send()
        for cp in mine:
            cp.wait()

    return pl.pallas_call(
        body, name="all_gather",
        out_shape=tuple(jax.ShapeDtypeStruct((N_DEV,) + a.shape, a.dtype) for a in arrs),
        in_specs=[_ANY] * n, out_specs=tuple([_ANY] * n),
        scratch_shapes=[pltpu.SemaphoreType.DMA((ns * n,)), pltpu.SemaphoreType.DMA((ns * n,)),
                        pltpu.SemaphoreType.DMA((n,))],
    )(*arrs)


def _exchange_sibling(arrs):
    n = len(arrs)

    def body(*refs):
        ins, outs = refs[:n], refs[n:2 * n]
        send_sems, recv_sems = refs[2 * n:]
        x, y, c = _mesh_pos()
        copies = [pltpu.make_async_remote_copy(
            src_ref=ins[a].at[2 * k + (1 - c)], dst_ref=outs[a].at[k],
            send_sem=send_sems.at[4 * a + k], recv_sem=recv_sems.at[4 * a + k],
            device_id=(x, y, 1 - c), device_id_type=MESH) for a in range(n) for k in range(4)]
        for cp in copies:
            cp.start()
        for cp in copies:
            cp.wait_recv()
        for cp in copies:
            cp.wait_send()

    return pl.pallas_call(
        body, name="grad_exchange_sibling",
        out_shape=tuple(jax.ShapeDtypeStruct((4,) + a.shape[1:], a.dtype) for a in arrs),
        in_specs=[_ANY] * n, out_specs=tuple([_ANY] * n),
        scratch_shapes=[pltpu.SemaphoreType.DMA((4 * n,)), pltpu.SemaphoreType.DMA((4 * n,))],
    )(*arrs)


def _elementwise_tile(r, cdim):
    if r % 8 == 0 and r > 256:
        return math.gcd(r, 256), cdim
    if r > 256 and cdim % 256 == 0:
        return r, 256
    return r, cdim


def _pair_sum(name, mine, theirs):
    _, r, cdim = mine.shape
    tr, tc = _elementwise_tile(r, cdim)

    def body(m_ref, t_ref, o_ref):
        c = lax.axis_index("c")
        own = jnp.where(c == 0, m_ref[:, 0].astype(F32), m_ref[:, 1].astype(F32))
        o_ref[...] = (own + t_ref[...].astype(F32)).astype(o_ref.dtype)

    return pl.pallas_call(
        body, name=name, grid=(r // tr, cdim // tc),
        in_specs=[pl.BlockSpec((4, 2, tr, tc), lambda i, j: (0, 0, i, j)),
                  pl.BlockSpec((4, tr, tc), lambda i, j: (0, i, j))],
        out_specs=pl.BlockSpec((4, tr, tc), lambda i, j: (0, i, j)),
        out_shape=jax.ShapeDtypeStruct((4, r, cdim), mine.dtype),
        compiler_params=_params())(mine.reshape(4, 2, r, cdim), theirs)


_HBM = pl.BlockSpec(memory_space=pltpu.HBM)
_SEM = pl.BlockSpec(memory_space=pltpu.SEMAPHORE)


def _copies_start(name, srcs, lands, make_copies, n_sems):
    n, m = len(srcs), len(lands)

    def body(*refs):
        ins = refs[:n + m]
        send_sems, recv_sems = refs[n + m], refs[n + m + 1]
        for cp in make_copies(ins[:n], ins[n:], send_sems, recv_sems):
            cp.start()
        refs[-1][...] = jnp.zeros_like(refs[-1])

    res = pl.pallas_call(
        body, name=name,
        out_shape=(pltpu.SemaphoreType.DMA((n_sems,)), pltpu.SemaphoreType.DMA((n_sems,)),
                   *[pltpu.HBM(a.shape, a.dtype) for a in (*srcs, *lands)], jax.ShapeDtypeStruct((8, LANE), F32)),
        in_specs=[_HBM] * (n + m),
        out_specs=(_SEM, _SEM, *[_HBM] * (n + m), pl.BlockSpec(memory_space=pltpu.VMEM)),
        input_output_aliases={i: 2 + i for i in range(n + m)},
        compiler_params=pltpu.CompilerParams(has_side_effects=pltpu.SideEffectType.DATAFLOW_SIDE_EFFECTING),
    )(*[pltpu.with_memory_space_constraint(a, pltpu.HBM) for a in (*srcs, *lands)])
    return res[0], res[1], res[2:2 + n], res[2 + n:2 + n + m], res[-1]


def _copies_wait(name, started, after, make_copies):
    send_sems, recv_sems, srcs, lands, _ = started
    n, m = len(srcs), len(lands)

    def body(*refs):
        ins = refs[:n + m]
        for cp in make_copies(ins[:n], ins[n:], refs[n + m], refs[n + m + 1]):
            cp.wait_send()
            cp.wait_recv()

    res = pl.pallas_call(
        body, name=name,
        out_shape=tuple(pltpu.HBM(a.shape, a.dtype) for a in (*srcs, *lands)),
        in_specs=[_HBM] * (n + m) + [_SEM, _SEM] + [_ANY] * len(after),
        out_specs=tuple([_HBM] * (n + m)),
        input_output_aliases={i: i for i in range(n + m)},
        compiler_params=pltpu.CompilerParams(has_side_effects=pltpu.SideEffectType.DATAFLOW_SIDE_EFFECTING),
    )(*srcs, *lands, send_sems, recv_sems, *after)
    return res[:n], res[n:]


def _gather_copies(srcs, lands, send_sems, recv_sems):
    x, y, c = _mesh_pos()
    me_i = 4 * x + 2 * y + c
    copies = []
    for rel in range(1, N_DEV):
        peer = (1 - x if rel & 4 else x, 1 - y if rel & 2 else y, 1 - c if rel & 1 else c)
        for a in range(len(srcs)):
            copies.append(pltpu.make_async_remote_copy(
                src_ref=srcs[a], dst_ref=lands[a].at[me_i], send_sem=send_sems.at[7 * a + rel - 1],
                recv_sem=recv_sems.at[7 * a + rel - 1], device_id=peer, device_id_type=MESH))
    return copies


def _sibling_copies(srcs, lands, send_sems, recv_sems):
    x, y, c = _mesh_pos()
    return [pltpu.make_async_remote_copy(
        src_ref=srcs[a].at[2 * k + (1 - c)], dst_ref=lands[a].at[k], send_sem=send_sems.at[4 * a + k],
        recv_sem=recv_sems.at[4 * a + k], device_id=(x, y, 1 - c), device_id_type=MESH)
        for a in range(len(srcs)) for k in range(4)]


def _chip_copies(srcs, lands, send_sems, recv_sems):
    x, y, c = _mesh_pos()
    my_chip = 2 * x + y
    copies = []
    for rel in range(1, 4):
        px = 1 - x if rel & 2 else x
        py = 1 - y if rel & 1 else y
        for a in range(len(srcs)):
            copies.append(pltpu.make_async_remote_copy(
                src_ref=srcs[a].at[2 * px + py], dst_ref=lands[a].at[my_chip], send_sem=send_sems.at[3 * a + rel - 1],
                recv_sem=recv_sems.at[3 * a + rel - 1], device_id=(px, py, c), device_id_type=MESH))
    return copies


def _sum_adam(name, parts, w, m, v, own=None):
    unit_mid = w.ndim == 3
    _, r, cdim = parts.shape
    n_parts = parts.shape[0]
    tr, tc = _elementwise_tile(r, cdim)
    bc1 = 1.0 - ADAM_B1 ** ADAM_STEP
    bc2 = 1.0 - ADAM_B2 ** ADAM_STEP
    extra = [] if own is None else [own]

    def body(p_ref, *refs):
        w_ref, m_ref, v_ref, g_ref, d_ref, nm_ref, nv_ref = refs[len(extra):]
        if own is None:
            part = lambda k: p_ref[k].astype(F32)
        else:
            my_chip = 2 * lax.axis_index("x") + lax.axis_index("y")
            part = lambda k: jnp.where(my_chip == k, refs[0][k], p_ref[k]).astype(F32)
        g = part(0)
        for k in range(1, n_parts):
            g = g + part(k)
        if unit_mid:
            g = g.reshape(tr, 1, tc)
        mn = ADAM_B1 * m_ref[...] + (1.0 - ADAM_B1) * g
        vn = ADAM_B2 * v_ref[...] + (1.0 - ADAM_B2) * (g * g)
        g_ref[...] = g
        nm_ref[...] = mn
        nv_ref[...] = vn
        d_ref[...] = -ADAM_LR * ((mn / bc1) / (jnp.sqrt(vn / bc2) + ADAM_EPS) + ADAM_WD * w_ref[...])

    blk = (pl.BlockSpec((tr, 1, tc), lambda i, j: (i, 0, j)) if unit_mid
           else pl.BlockSpec((tr, tc), lambda i, j: (i, j)))
    o = jax.ShapeDtypeStruct(w.shape, F32)
    return pl.pallas_call(
        body, name=name, grid=(r // tr, cdim // tc),
        in_specs=[pl.BlockSpec((n_parts, tr, tc), lambda i, j: (0, i, j))] * (1 + len(extra)) + [blk, blk, blk],
        out_specs=(blk, blk, blk, blk), out_shape=(o, o, o, o),
        compiler_params=_params())(parts, *extra, w, m, v)


_WEIGHTS = ("c_ctx", "ada_w", "ada_b", "norm_g", "w_in", "b_in", "conv_w", "conv_b", "conv_ln_g", "conv_ln_b",
            "conv_proj", "decay_up_fwd", "decay_bias_fwd", "decay_up_bwd", "decay_bias_bwd", "gla_norm_g",
            "gla_proj", "w_out", "final_norm_g")


def _as2d(a):
    if a.ndim == 1:
        return a.reshape(1, -1)
    return a.reshape(-1, a.shape[-1])


def kernel(x, c, ctx, c_ctx, ada_w, ada_b, norm_g, w_in, b_in, conv_w, conv_b, conv_ln_g, conv_ln_b, conv_proj, decay_up_fwd, decay_bias_fwd, decay_up_bwd, decay_bias_bwd, gla_norm_g, gla_proj, w_out, final_norm_g, loss_target, m_c_ctx, m_ada_w, m_ada_b, m_norm_g, m_w_in, m_b_in, m_conv_w, m_conv_b, m_conv_ln_g, m_conv_ln_b, m_conv_proj, m_decay_up_fwd, m_decay_bias_fwd, m_decay_up_bwd, m_decay_bias_bwd, m_gla_norm_g, m_gla_proj, m_w_out, m_final_norm_g, v_c_ctx, v_ada_w, v_ada_b, v_norm_g, v_w_in, v_b_in, v_conv_w, v_conv_b, v_conv_ln_g, v_conv_ln_b, v_conv_proj, v_decay_up_fwd, v_decay_bias_fwd, v_decay_up_bwd, v_decay_bias_bwd, v_gla_norm_g, v_gla_proj, v_w_out, v_final_norm_g):
    env = dict(locals())
    wts = {k: env[k] for k in _WEIGHTS}
    d = x.shape[-1]
    r = decay_up_fwd.shape[1]
    dk_ = d // 2

    ds, dks = d // N_DEV, dk_ // N_DEV
    g_win, g_ada, conv_w8, g_up = _all_gather(
        [w_in[0].astype(BF16), ada_w[0].astype(BF16), conv_w[0],
         jnp.concatenate([decay_up_fwd[0], decay_up_bwd[0]], axis=1)])
    proj_own = [conv_proj[0].astype(BF16), gla_proj[0].astype(BF16), w_out[0].astype(BF16)]
    me_i = 4 * lax.axis_index("x") + 2 * lax.axis_index("y") + lax.axis_index("c")
    proj_lands = [lax.dynamic_update_slice(lax.empty((N_DEV,) + a.shape, a.dtype), a[None], (me_i, 0, 0))
                  for a in proj_own]
    proj_start = _copies_start("proj_gather_start", proj_own, proj_lands, _gather_copies, 7 * 3)

    def proj(after):
        _, lands = _copies_wait("proj_gather_wait", proj_start, (after,), _gather_copies)
        return [w.reshape(d, d) for w in lands]

    w_a, w_b = _unshard_w_in(g_win, d, r, after=(proj_start[4],))
    up_f = g_up[:, :, 0:dks].transpose(1, 0, 2).reshape(r, dk_)
    up_b = g_up[:, :, dks:].transpose(1, 0, 2).reshape(r, dk_)
    up2 = jnp.zeros((LANE, 2 * dk_), F32).at[0:r, 0:dk_].set(up_f).at[r:2 * r, dk_:].set(up_b)
    bias2 = jnp.concatenate([decay_bias_fwd, decay_bias_bwd], axis=1)
    b_a, b_b = _regroup(b_in, d, r)

    names = ("w_in", "conv_proj", "gla_proj", "w_out", "conv_w", "decay_up")
    comm = {}

    def on_grads(gr):
        d_up = jnp.concatenate([gr["up2"][0:r, 0:dk_].reshape(r, N_DEV, dks).transpose(1, 0, 2),
                                gr["up2"][r:2 * r, dk_:].reshape(r, N_DEV, dks).transpose(1, 0, 2)], axis=2)
        mine = [_reshard_w_in(gr["w_a1"], gr["w_a2"], gr["w_b"], d, r), gr["conv_proj"].reshape(N_DEV, ds, d),
                gr["gla_proj"].reshape(N_DEV, ds, d), gr["w_out"].reshape(N_DEV, ds, d), gr["conv_w8"], d_up]
        lands = [lax.empty((4,) + a.shape[1:], a.dtype) for a in mine]
        comm["sibling"] = _copies_start("grad_sibling_start", mine, lands, _sibling_copies, 4 * len(mine))
        return (comm["sibling"][4],)

    def on_du_a1(du_a1):
        mine, theirs = _copies_wait("grad_sibling_wait", comm["sibling"], (du_a1,), _sibling_copies)
        sums = [_pair_sum("pair_sum_" + nm, a, b) for nm, a, b in zip(names, mine, theirs)]
        lands = [lax.empty(a.shape, a.dtype) for a in sums]
        comm["chips"] = _copies_start("grad_chips_start", sums, lands, _chip_copies, 3 * len(sums))
        return (comm["chips"][4],)

    g = _local_step(x, c, ctx, loss_target, c_ctx, g_ada, ada_b, norm_g[0:1], w_a, b_a, w_b, b_b,
                    conv_w8, conv_b, conv_ln_g, conv_ln_b, up2, bias2, gla_norm_g, final_norm_g.reshape(1, d),
                    proj, on_grads, on_du_a1)

    pack = _pack_small(g, x.shape[0], d, r)
    pack_lands = [lax.dynamic_update_slice(lax.empty((N_DEV,) + pack.shape, F32), pack[None], (me_i, 0, 0))]
    small_start = _copies_start("small_gather_start", [pack], pack_lands, _gather_copies, 7)

    (their_ada,) = _exchange_sibling([g["ada_w8"]])
    ada_sum = _pair_sum("pair_sum_ada_w", g["ada_w8"], their_ada)
    ada_start = _copies_start("ada_chips_start", [ada_sum], [lax.empty(ada_sum.shape, ada_sum.dtype)],
                              _chip_copies, 3)
    own, landed = _copies_wait("grad_chips_wait", comm["chips"], (ada_start[4],), _chip_copies)
    o_win, o_cp, o_gp, o_wo, o_cw, o_up = own
    x_win, x_cp, x_gp, x_wo, x_cw, x_up = landed

    out = {}

    def big(name, parts, wname, own=None):
        w2 = _as2d(wts[wname])
        res = _sum_adam(name, parts, w2, _as2d(env["m_" + wname]), _as2d(env["v_" + wname]), own)
        for pre, arr in zip(("grad_", "delta_", "new_m_", "new_v_"), res):
            out[pre + wname] = arr.reshape(wts[wname].shape)

    as_rows = lambda a: jnp.transpose(a, (2, 0, 1))
    res = _sum_adam("adam_w_in", x_win, as_rows(w_in), as_rows(m_w_in), as_rows(v_w_in), o_win)
    for pre, arr in zip(("grad_", "delta_", "new_m_", "new_v_"), res):
        out[pre + "w_in"] = jnp.transpose(arr, (1, 2, 0))
    big("adam_conv_proj", x_cp, "conv_proj", o_cp)
    big("adam_gla_proj", x_gp, "gla_proj", o_gp)
    big("adam_w_out", x_wo, "w_out", o_wo)
    big("adam_conv_w", x_cw, "conv_w", o_cw)
    big("adam_up_f", x_up[:, :, 0:dks], "decay_up_fwd", o_up[:, :, 0:dks])
    big("adam_up_b", x_up[:, :, dks:], "decay_up_bwd", o_up[:, :, dks:])

    _, (packs,) = _copies_wait("small_gather_wait", small_start, (res[0], out["grad_w_out"]), _gather_copies)
    row = lambda a: a.reshape(1, -1)
    sg, sd, sm, sv, loss = _small_adam(packs, [row(wts[k]) for k in _SMALL], [row(env["m_" + k]) for k in _SMALL],
                                       [row(env["v_" + k]) for k in _SMALL], d, r)
    for i, k in enumerate(_SMALL):
        for pre, arrs in (("grad_", sg), ("delta_", sd), ("new_m_", sm), ("new_v_", sv)):
            out[pre + k] = arrs[i].reshape(wts[k].shape)
    loss = loss.reshape(())

    (o_ada,), (x_ada,) = _copies_wait("ada_chips_wait", ada_start, (res[0], out["grad_w_out"], out["grad_b_in"]),
                                      _chip_copies)
    big("adam_ada_w", x_ada, "ada_w", o_ada)

    return (loss, g["grad_x"], *[out["grad_" + k] for k in _WEIGHTS], *[out["delta_" + k] for k in _WEIGHTS],
            *[out["new_m_" + k] for k in _WEIGHTS], *[out["new_v_" + k] for k in _WEIGHTS])
```

```python
import functools
import math

import jax
import jax.numpy as jnp
from jax import lax
from jax.experimental import pallas as pl
from jax.experimental.pallas import tpu as pltpu

F32 = jnp.float32
BF16 = jnp.bfloat16
MESH = pl.DeviceIdType.MESH

N_DEV = 8
GRID_W = 64
CHUNK = 128
HEADS = 4
EPS = 1e-6
GATE_TAU = 16.0
LANE = 128
ADAM_LR, ADAM_B1, ADAM_B2, ADAM_EPS, ADAM_WD, ADAM_STEP = 0.001, 0.9, 0.999, 1e-08, 0.01, 10
VMEM_LIMIT = 60 * 1024 * 1024
_ANY = pl.BlockSpec(memory_space=pl.ANY)


def _params(**kw):
    return pltpu.CompilerParams(vmem_limit_bytes=VMEM_LIMIT, **kw)


def _tile(n, pref):
    t = (min(pref, n) // LANE) * LANE
    while t >= LANE:
        if n % t == 0:
            return t
        t -= LANE
    return n


def _mm(a, b):
    return jnp.dot(a.astype(BF16), b.astype(BF16), preferred_element_type=F32)


def _mm_nt(a, b):
    return lax.dot_general(a.astype(BF16), b.astype(BF16), (((1,), (1,)), ((), ())), preferred_element_type=F32)


def _mm_tn(a, b):
    return lax.dot_general(a.astype(BF16), b.astype(BF16), (((0,), (0,)), ((), ())), preferred_element_type=F32)


def _sigmoid(x):
    return 0.5 * jnp.tanh(0.5 * x) + 0.5


def _dsilu(x, s):
    return s * (1.0 + x * (1.0 - s))


def _rowsel(table, idx, n):
    out = table[0:1, :]
    for r in range(1, n):
        out = jnp.where(idx == r, table[r:r + 1, :], out)
    return out


def _ada_fwd(cv, ada_w8, ada_b):
    n_sh, _, ws = ada_w8.shape

    def body(cv_ref, w_ref, b_ref, o_ref):
        c = cv_ref[...]
        sv = c * _sigmoid(c)
        for j in range(n_sh):
            cols = pl.ds(j * ws, ws)
            o_ref[:, cols] = _mm(sv, w_ref[j]) + b_ref[:, cols]

    return pl.pallas_call(body, name="ada_fwd", out_shape=jax.ShapeDtypeStruct((cv.shape[0], n_sh * ws), F32),
                          compiler_params=_params())(cv, ada_w8, ada_b)


def _ada_bwd(cv, ada_w8, dmod_ss, small, nb):
    n_sh, d, ws = ada_w8.shape

    def body(cv_ref, w_ref, dm_ref, sm_ref, dw_ref, db_ref, dc_ref):
        c = cv_ref[...]
        s = _sigmoid(c)
        sv = c * s
        dm = jnp.concatenate([dm_ref[:, 0:2 * d], sm_ref[8:16, :]], axis=1)
        db_ref[...] = jnp.sum(dm, axis=0, keepdims=True)
        sv_t = jnp.transpose(sv)
        dsv = None
        for j in range(n_sh):
            dmj = dm[:, j * ws:(j + 1) * ws]
            dw = sv_t[:, 0:1] * dmj[0:1, :]
            for row in range(1, nb + 1):
                dw = dw + sv_t[:, row:row + 1] * dmj[row:row + 1, :]
            dw_ref[j] = dw.astype(dw_ref.dtype)
            part = _mm_nt(dmj, w_ref[j])
            dsv = part if dsv is None else dsv + part
        dc_ref[...] = dsv * _dsilu(c, s)

    return pl.pallas_call(
        body, name="ada_bwd",
        out_shape=(jax.ShapeDtypeStruct((n_sh, d, ws), BF16), jax.ShapeDtypeStruct((1, n_sh * ws), F32),
                   jax.ShapeDtypeStruct(cv.shape, F32)),
        compiler_params=_params())(cv, ada_w8, dmod_ss, small)


class _Tiles:
    def __init__(self, nb, s_len, c_len, tm, big):
        self.nb, self.tm, self.big = nb, tm, big
        self.lat, self.ctx = s_len // tm, c_len // tm
        self.pad = -(self.lat + self.ctx) % big
        self.per_ex = self.lat + self.ctx + self.pad
        self.n_all = nb * self.per_ex
        self.rows_per_ex = self.per_ex * tm

    def is_lat(self, i):
        return i % self.per_ex < self.lat

    def is_pad(self, i):
        return i % self.per_ex >= self.lat + self.ctx

    def lat_of_all(self, i):
        return (i // self.per_ex) * self.lat + jnp.minimum(i % self.per_ex, self.lat - 1)

    def ctx_of_all(self, i):
        return (i // self.per_ex) * self.ctx + jnp.clip(i % self.per_ex - self.lat, 0, self.ctx - 1)


def _norm_fwd(x2, ctx2, mod, norm_g, tiles):
    tl, d = x2.shape
    tc = ctx2.shape[0]
    nb, tm = tiles.nb, tiles.tm

    def body(x_ref, c_ref, mod_ref, g_ref, u_ref):
        i = pl.program_id(0)
        lat = tiles.is_lat(i)
        xv = jnp.where(lat, x_ref[...], c_ref[...])
        row = jnp.where(lat, i // tiles.per_ex, nb)
        m = _rowsel(mod_ref[...], row, nb + 1)
        shift, scale = m[:, 0:d], m[:, d:2 * d]
        rstd = lax.rsqrt(jnp.mean(xv * xv, axis=-1, keepdims=True) + EPS)
        u = xv * rstd * g_ref[...] * (1.0 + scale) + shift
        u_ref[...] = jnp.where(tiles.is_pad(i), 0.0, u).astype(BF16)

    return pl.pallas_call(
        body, name="norm_fwd", grid=(tiles.n_all,),
        in_specs=[pl.BlockSpec((tm, d), lambda i: (tiles.lat_of_all(i), 0)),
                  pl.BlockSpec((tm, d), lambda i: (tiles.ctx_of_all(i), 0)),
                  pl.BlockSpec(mod.shape, lambda i: (0, 0)),
                  pl.BlockSpec((1, d), lambda i: (0, 0))],
        out_specs=pl.BlockSpec((tm, d), lambda i: (i, 0)),
        out_shape=jax.ShapeDtypeStruct((tiles.n_all * tm, d), BF16),
        compiler_params=_params())(x2, ctx2, mod, norm_g)


def _norm_bwd(x2, ctx2, mod, norm_g, du_lat, du_b, gx1, tiles):
    tl, d = x2.shape
    nb, tm = tiles.nb, tiles.tm
    nrow = mod.shape[0]
    n_lat_in = len(du_lat)

    def body(x_ref, c_ref, mod_ref, g_ref, *refs):
        dl_refs = refs[:n_lat_in]
        d3_ref, gx_ref, gxo_ref, dmod_ref, dg_ref = refs[n_lat_in:]
        i = pl.program_id(0)

        @pl.when(i == 0)
        def _():
            dmod_ref[...] = jnp.zeros_like(dmod_ref)
            dg_ref[...] = jnp.zeros_like(dg_ref)

        lat = tiles.is_lat(i)
        xv = jnp.where(lat, x_ref[...], c_ref[...])
        row = jnp.where(lat, i // tiles.per_ex, nb)
        m = _rowsel(mod_ref[...], row, nb + 1)
        scale = m[:, d:2 * d]
        g = g_ref[...]
        dl = dl_refs[0][...].astype(F32)
        for ref in dl_refs[1:]:
            dl = dl + ref[...].astype(F32)
        du = jnp.where(tiles.is_pad(i), 0.0, d3_ref[...].astype(F32) + jnp.where(lat, dl, 0.0))
        rstd = lax.rsqrt(jnp.mean(xv * xv, axis=-1, keepdims=True) + EPS)
        xh = xv * rstd
        dshift = jnp.sum(du, axis=0, keepdims=True)
        dscale = jnp.sum(du * xh * g, axis=0, keepdims=True)
        dxn = du * (1.0 + scale)
        dg_ref[...] += jnp.sum(dxn * xh, axis=0, keepdims=True)
        dxh = dxn * g
        dx = rstd * (dxh - xh * jnp.mean(dxh * xh, axis=-1, keepdims=True))

        @pl.when(lat)
        def _():
            gxo_ref[...] = dx + gx_ref[...]

        for r in range(nb + 1):
            dmod_ref[r:r + 1, 0:d] += jnp.where(row == r, dshift, 0.0)
            dmod_ref[r:r + 1, d:2 * d] += jnp.where(row == r, dscale, 0.0)

    lat_map = lambda i: (tiles.lat_of_all(i), 0)
    lat_spec = pl.BlockSpec((tm, d), lat_map)
    return pl.pallas_call(
        body, name="norm_bwd", grid=(tiles.n_all,),
        in_specs=[lat_spec,
                  pl.BlockSpec((tm, d), lambda i: (tiles.ctx_of_all(i), 0)),
                  pl.BlockSpec(mod.shape, lambda i: (0, 0)),
                  pl.BlockSpec((1, d), lambda i: (0, 0))]
                 + [lat_spec] * n_lat_in
                 + [pl.BlockSpec((tm, d), lambda i: (i, 0)), lat_spec],
        out_specs=(lat_spec,
                   pl.BlockSpec((nrow, 3 * d), lambda i: (0, 0)),
                   pl.BlockSpec((1, d), lambda i: (0, 0))),
        out_shape=(jax.ShapeDtypeStruct((tl, d), F32), jax.ShapeDtypeStruct((nrow, 3 * d), F32),
                   jax.ShapeDtypeStruct((1, d), F32)),
        compiler_params=_params())(x2, ctx2, mod, norm_g, *du_lat, du_b, gx1)


def _matmul_bias(name, u3, w, b, s_len, tm, tn):
    nb = u3.shape[0]
    d, n = w.shape
    per = s_len // tm
    rows = nb * s_len

    def body(u_ref, w_ref, b_ref, o_ref):
        o_ref[...] = jnp.dot(u_ref[...], w_ref[...], preferred_element_type=F32) + b_ref[...]

    return pl.pallas_call(
        body, name=name, grid=(n // tn, rows // tm),
        in_specs=[pl.BlockSpec((None, tm, d), lambda j, i: (i // per, i % per, 0)),
                  pl.BlockSpec((d, tn), lambda j, i: (0, j)),
                  pl.BlockSpec((1, tn), lambda j, i: (0, j))],
        out_specs=pl.BlockSpec((tm, tn), lambda j, i: (i, j)),
        out_shape=jax.ShapeDtypeStruct((rows, n), F32),
        compiler_params=_params())(u3, w, b)


def _inproj_b(u, w_b, b_b, tm, dk_, dv_):
    t_all, d = u.shape
    nbw = w_b.shape[1]

    def body(u_ref, w_ref, b_ref, qk_ref, v_ref):
        full = jnp.dot(u_ref[...], w_ref[...], preferred_element_type=F32) + b_ref[...]
        qk_ref[:, 0:2 * dk_] = full[:, 0:2 * dk_]
        qk_ref[:, 2 * dk_:2 * dk_ + LANE] = full[:, 2 * dk_ + dv_:nbw]
        v_ref[...] = full[:, 2 * dk_:2 * dk_ + dv_].astype(BF16)

    return pl.pallas_call(
        body, name="inproj_b", grid=(t_all // tm,),
        in_specs=[pl.BlockSpec((tm, d), lambda i: (i, 0)), pl.BlockSpec((d, nbw), lambda i: (0, 0)),
                  pl.BlockSpec((1, nbw), lambda i: (0, 0))],
        out_specs=(pl.BlockSpec((tm, 2 * dk_ + LANE), lambda i: (i, 0)), pl.BlockSpec((tm, dv_), lambda i: (i, 0))),
        out_shape=(jax.ShapeDtypeStruct((t_all, 2 * dk_ + LANE), F32), jax.ShapeDtypeStruct((t_all, dv_), BF16)),
        compiler_params=_params())(u, w_b, b_b)


def _matmul_nt(name, a, w, koff, tm, tk, after=()):
    r, kc = a.shape
    d = w.shape[0]
    nk = kc // tk

    def body(a_ref, w_ref, *rest):
        o_ref = rest[len(after)]
        k = pl.program_id(1)
        p = lax.dot_general(a_ref[...], w_ref[...], (((1,), (1,)), ((), ())), preferred_element_type=F32)
        if nk == 1:
            o_ref[...] = p.astype(o_ref.dtype)
            return
        acc_ref = rest[len(after) + 1]

        @pl.when(k == 0)
        def _():
            acc_ref[...] = p

        @pl.when(k > 0)
        def _():
            acc_ref[...] += p

        @pl.when(k == nk - 1)
        def _():
            o_ref[...] = acc_ref[...].astype(o_ref.dtype)

    return pl.pallas_call(
        body, name=name, grid=(r // tm, nk),
        in_specs=[pl.BlockSpec((tm, tk), lambda i, k: (i, k)),
                  pl.BlockSpec((d, tk), lambda i, k: (0, koff + k))] + [_ANY] * len(after),
        out_specs=pl.BlockSpec((tm, d), lambda i, k: (i, 0)),
        out_shape=jax.ShapeDtypeStruct((r, d), BF16),
        scratch_shapes=[pltpu.VMEM((tm, d), F32)] if nk > 1 else [],
        compiler_params=_params())(a, w, *after)


def _matmul_tn(name, a, b, rows, tk, tn):
    m = a.shape[1]
    n = b.shape[1]
    nk = rows // tk

    def body(a_ref, b_ref, o_ref, s_ref, acc_ref):
        k = pl.program_id(1)
        bv = b_ref[...]
        p = lax.dot_general(bv, a_ref[...], (((0,), (0,)), ((), ())), preferred_element_type=F32)
        cs = jnp.sum(bv.astype(F32), axis=0, keepdims=True)

        @pl.when(k == 0)
        def _():
            acc_ref[...] = p
            s_ref[...] = cs

        @pl.when(k > 0)
        def _():
            acc_ref[...] += p
            s_ref[...] += cs

        @pl.when(k == nk - 1)
        def _():
            o_ref[...] = acc_ref[...].astype(o_ref.dtype)

    return pl.pallas_call(
        body, name=name, grid=(n // tn, nk),
        in_specs=[pl.BlockSpec((tk, m), lambda j, k: (k, 0)),
                  pl.BlockSpec((tk, tn), lambda j, k: (k, j))],
        out_specs=(pl.BlockSpec((tn, m), lambda j, k: (j, 0)), pl.BlockSpec((1, tn), lambda j, k: (0, j))),
        out_shape=(jax.ShapeDtypeStruct((n, m), BF16), jax.ShapeDtypeStruct((1, n), F32)),
        scratch_shapes=[pltpu.VMEM((tn, m), F32)],
        compiler_params=_params())(a, b)


def _matmul_tn_whole(name, a3, b3, rows, tn, transposed):
    nb, _, m = a3.shape
    n = b3.shape[2]

    def body(a_ref, b_ref, o_ref, s_ref):
        p, cs = None, None
        for e in range(nb):
            bv = b_ref[e]
            lhs, rhs = (bv, a_ref[e]) if transposed else (a_ref[e], bv)
            pe = lax.dot_general(lhs, rhs, (((0,), (0,)), ((), ())), preferred_element_type=F32)
            ce = jnp.sum(bv.astype(F32), axis=0, keepdims=True)
            p, cs = (pe, ce) if p is None else (p + pe, cs + ce)
        o_ref[...] = p.astype(o_ref.dtype)
        s_ref[...] = cs

    o_spec, o_shape = ((pl.BlockSpec((tn, m), lambda j: (j, 0)), (n, m)) if transposed
                       else (pl.BlockSpec((m, tn), lambda j: (0, j)), (m, n)))
    return pl.pallas_call(
        body, name=name, grid=(n // tn,),
        in_specs=[pl.BlockSpec((nb, rows, m), lambda j: (0, 0, 0)),
                  pl.BlockSpec((nb, rows, tn), lambda j: (0, 0, j))],
        out_specs=(o_spec, pl.BlockSpec((1, tn), lambda j: (0, j))),
        out_shape=(jax.ShapeDtypeStruct(o_shape, BF16), jax.ShapeDtypeStruct((1, n), F32)),
        compiler_params=_params())(a3, b3)


def _conv_window(pad_ref, r, shift, ktaps, width, horizontal):
    if horizontal:
        return pad_ref[r, pl.ds(16 + shift, width), :]
    return pad_ref[r + ktaps // 2 + shift]


def _conv_row(pad_ref, w, r, ktaps, width, horizontal, flip):
    half = ktaps // 2
    acc = None
    for t in range(ktaps):
        win = _conv_window(pad_ref, r, (half - t) if flip else (t - half), ktaps, width, horizontal)
        term = win * w[t:t + 1, :]
        acc = term if acc is None else acc + term
    return acc


def _fill_padded(ref, val, rows, width, ktaps, horizontal):
    half_k = ktaps // 2
    cb = val.shape[-1]
    if horizontal:
        ref[:, 0:16, :] = jnp.zeros((rows, 16, cb), F32)
        ref[:, 16 + width:32 + width, :] = jnp.zeros((rows, 16, cb), F32)
        ref[:, 16:16 + width, :] = val
    else:
        ref[0:half_k, :, :] = jnp.zeros((half_k, width, cb), F32)
        ref[half_k + rows:2 * half_k + rows, :, :] = jnp.zeros((half_k, width, cb), F32)
        ref[half_k:half_k + rows, :, :] = val


def _conv_fwd(pa, conv_w8, conv_b, nb, s):
    nblk, ktaps, cb = conv_w8.shape
    d = nblk * cb
    rows, width = s // GRID_W, GRID_W
    half_k = ktaps // 2
    nh = nblk // 2

    def body(glu_ref, w_ref, b_ref, o_ref, ph_ref, pv_ref):
        j = pl.program_id(1)
        a0 = (glu_ref[:, 0:cb] * _sigmoid(glu_ref[:, cb:2 * cb])).reshape(rows, width, cb)
        w = w_ref[...]

        bias = b_ref[...]

        def run(pad_ref, horizontal):
            _fill_padded(pad_ref, a0, rows, width, ktaps, horizontal)

            def row(r, carry):
                at = pl.ds(pl.multiple_of(r * width, width), width)
                o_ref[at, :] = _conv_row(pad_ref, w, r, ktaps, width, horizontal, False) + bias
                return carry

            lax.fori_loop(0, rows, row, 0)

        @pl.when(j < nh)
        def _():
            run(ph_ref, True)

        @pl.when(j >= nh)
        def _():
            run(pv_ref, False)

    return pl.pallas_call(
        body, name="conv_fwd", grid=(nb, nblk),
        in_specs=[pl.BlockSpec((s, 2 * cb), lambda b, j: (b, j)),
                  pl.BlockSpec((None, ktaps, cb), lambda b, j: (j, 0, 0)),
                  pl.BlockSpec((1, cb), lambda b, j: (0, j))],
        out_specs=pl.BlockSpec((s, cb), lambda b, j: (b, j)),
        out_shape=jax.ShapeDtypeStruct((nb * s, d), F32),
        scratch_shapes=[pltpu.VMEM((rows, width + 32, cb), F32), pltpu.VMEM((rows + 2 * half_k, width, cb), F32)],
        compiler_params=_params())(pa, conv_w8, conv_b)


def _conv_bwd(pa, da1, conv_w8, nb, s):
    nblk, ktaps, cb = conv_w8.shape
    d = nblk * cb
    rows, width = s // GRID_W, GRID_W
    half_k = ktaps // 2
    nh = nblk // 2

    def body(glu_ref, da_ref, w_ref, dp_ref, dw_ref, db_ref, pha_ref, phd_ref, pva_ref, pvd_ref):
        j = pl.program_id(0)
        b = pl.program_id(1)
        a0 = (glu_ref[:, 0:cb] * _sigmoid(glu_ref[:, cb:2 * cb])).reshape(rows, width, cb)
        da1v = da_ref[...]
        d3 = da1v.reshape(rows, width, cb)
        w = w_ref[...]

        @pl.when(b == 0)
        def _():
            dw_ref[...] = jnp.zeros_like(dw_ref)
            db_ref[...] = jnp.zeros_like(db_ref)

        db_ref[...] += jnp.sum(da1v, axis=0, keepdims=True)

        def run(pa_ref, pd_ref, horizontal):
            _fill_padded(pa_ref, a0, rows, width, ktaps, horizontal)
            _fill_padded(pd_ref, d3, rows, width, ktaps, horizontal)

            def row(r, accs):
                at = pl.ds(pl.multiple_of(r * width, width), width)
                da0 = _conv_row(pd_ref, w, r, ktaps, width, horizontal, True)
                gv = glu_ref[at, 0:cb]
                sg = _sigmoid(glu_ref[at, cb:2 * cb])
                dp_ref[at, 0:cb] = (da0 * sg).astype(BF16)
                dp_ref[at, cb:2 * cb] = (da0 * gv * sg * (1.0 - sg)).astype(BF16)
                d_row = da_ref[at, :]
                out = []
                for t in range(ktaps):
                    prod = _conv_window(pa_ref, r, t - half_k, ktaps, width, horizontal) * d_row
                    out.append(accs[t] + jnp.sum(prod.reshape(width // 8, 8, cb), axis=0))
                return tuple(out)

            accs = lax.fori_loop(0, rows, row, tuple(jnp.zeros((8, cb), F32) for _ in range(ktaps)))
            for t in range(ktaps):
                dw_ref[t:t + 1, :] += jnp.sum(accs[t], axis=0, keepdims=True)

        @pl.when(j < nh)
        def _():
            run(pha_ref, phd_ref, True)

        @pl.when(j >= nh)
        def _():
            run(pva_ref, pvd_ref, False)

    return pl.pallas_call(
        body, name="conv_bwd", grid=(nblk, nb),
        in_specs=[pl.BlockSpec((s, 2 * cb), lambda j, b: (b, j)),
                  pl.BlockSpec((s, cb), lambda j, b: (b, j)),
                  pl.BlockSpec((None, ktaps, cb), lambda j, b: (j, 0, 0))],
        out_specs=(pl.BlockSpec((s, 2 * cb), lambda j, b: (b, j)),
                   pl.BlockSpec((None, ktaps, cb), lambda j, b: (j, 0, 0)),
                   pl.BlockSpec((1, cb), lambda j, b: (0, j))),
        out_shape=(jax.ShapeDtypeStruct((nb * s, 2 * d), BF16),
                   jax.ShapeDtypeStruct((nblk, ktaps, cb), F32), jax.ShapeDtypeStruct((1, d), F32)),
        scratch_shapes=[pltpu.VMEM((rows, width + 32, cb), F32), pltpu.VMEM((rows, width + 32, cb), F32),
                        pltpu.VMEM((rows + 2 * half_k, width, cb), F32),
                        pltpu.VMEM((rows + 2 * half_k, width, cb), F32)],
        compiler_params=_params())(pa, da1, conv_w8)


def _log_sigmoid(x):
    return jnp.minimum(x, 0.0) - jnp.log(1.0 + jnp.exp(-jnp.abs(x)))


def _decay_fwd(pb, up2, bias2, tm, lr_blk):
    t_all = pb.shape[0]
    n2 = up2.shape[1]

    def body(lr_ref, up_ref, b_ref, g_ref):
        logits = _mm(lr_ref[...], up_ref[...]) + b_ref[...]
        g_ref[...] = _log_sigmoid(logits) * (1.0 / GATE_TAU)

    return pl.pallas_call(
        body, name="decay_fwd", grid=(t_all // tm,),
        in_specs=[pl.BlockSpec((tm, LANE), lambda i: (i, lr_blk)),
                  pl.BlockSpec(up2.shape, lambda i: (0, 0)),
                  pl.BlockSpec((1, n2), lambda i: (0, 0))],
        out_specs=pl.BlockSpec((tm, n2), lambda i: (i, 0)),
        out_shape=jax.ShapeDtypeStruct((t_all, n2), F32),
        compiler_params=_params())(pb, up2, bias2)


def _decay_bwd(pb, up2, bias2, grads_f, grads_b, tiles, lr_blk, dk_, dv_):
    t_all = pb.shape[0]
    tm = tiles.tm
    n2 = up2.shape[1]
    nbw = 2 * dk_ + dv_ + LANE

    def body(lr_ref, up_ref, b_ref, dqf, dkf, dvf, dgf, dqb, dkb, dvb, dgb, dp_ref, dup_ref, dbias_ref):
        i = pl.program_id(0)
        pad = tiles.is_pad(i)
        live = lambda v: jnp.where(pad, 0.0, v)

        @pl.when(i == 0)
        def _():
            dup_ref[...] = jnp.zeros_like(dup_ref)
            dbias_ref[...] = jnp.zeros_like(dbias_ref)

        lr = lr_ref[...]
        up = up_ref[...]
        logits = _mm(lr, up) + b_ref[...]
        dg = live(jnp.concatenate([dgf[...], dgb[...]], axis=1))
        dlog = dg * (1.0 / GATE_TAU) * _sigmoid(-logits)
        dup_ref[...] += _mm_tn(lr, dlog)
        dbias_ref[...] += jnp.sum(dlog, axis=0, keepdims=True)
        both = lambda f, b: live(f[...].astype(F32) + b[...].astype(F32)).astype(BF16)
        dp_ref[:, 0:dk_] = both(dqf, dqb)
        dp_ref[:, dk_:2 * dk_] = both(dkf, dkb)
        dp_ref[:, 2 * dk_:2 * dk_ + dv_] = both(dvf, dvb)
        dp_ref[:, 2 * dk_ + dv_:nbw] = _mm_nt(dlog, up).astype(BF16)

    row = lambda w: pl.BlockSpec((tm, w), lambda i: (i, 0))
    return pl.pallas_call(
        body, name="decay_bwd", grid=(t_all // tm,),
        in_specs=[pl.BlockSpec((tm, LANE), lambda i: (i, lr_blk)),
                  pl.BlockSpec(up2.shape, lambda i: (0, 0)),
                  pl.BlockSpec((1, n2), lambda i: (0, 0)),
                  row(dk_), row(dk_), row(dv_), row(dk_), row(dk_), row(dk_), row(dv_), row(dk_)],
        out_specs=(row(nbw), pl.BlockSpec(up2.shape, lambda i: (0, 0)), pl.BlockSpec((1, n2), lambda i: (0, 0))),
        out_shape=(jax.ShapeDtypeStruct((t_all, nbw), BF16), jax.ShapeDtypeStruct(up2.shape, F32),
                   jax.ShapeDtypeStruct((1, n2), F32)),
        compiler_params=_params())(pb, up2, bias2, *grads_f, *grads_b)


def _scan_chunk(s, nl, nc, rev):
    if rev:
        return jnp.where(s < nc, nl + (nc - 1 - s), nl - 1 - (s - nc))
    return jnp.where(s < nc, nl + s, s - nc)


def _scan_lat_chunk(s, nl, nc, rev):
    first = nl - 1 if rev else 0
    return jnp.where(s < nc, first, _scan_chunk(s, nl, nc, rev))


def _tri_mm(m_bf, x):
    hi = x.astype(BF16)
    r1 = x - hi.astype(F32)
    mid = r1.astype(BF16)
    lo = (r1 - mid.astype(F32)).astype(BF16)
    dot = lambda p: jnp.dot(m_bf, p, preferred_element_type=F32)
    return dot(hi) + dot(mid) + dot(lo)


def _chunk_masks(c, rev):
    ii = lax.broadcasted_iota(jnp.int32, (c, c), 0)
    jj = lax.broadcasted_iota(jnp.int32, (c, c), 1)
    return ((ii <= jj), (ii >= jj)) if rev else ((ii >= jj), (ii <= jj))


def _chunk_terms(q, k, b, far, mid):
    bf, bm = b[far:far + 1, :], b[mid:mid + 1, :]
    e = jnp.exp(b)
    em = jnp.exp(b - bm)
    eim = jnp.exp(bm - b)
    ed = jnp.exp(bf - b)
    return dict(e=e, em=em, eim=eim, ed=ed, dec=jnp.exp(bf), qe=q * e, qem=q * em, kim=k * eim, kd=k * ed)


def _gla_fwd(pb3, pv3, g3, nb, s_len, c_len, dk_, dv_):
    c = CHUNK
    nl, nc = s_len // c, c_len // c
    ns = nl + nc
    hk, hv = dk_ // HEADS, dv_ // HEADS
    l_len = pb3.shape[1]
    scale = hk ** -0.5
    mid = c // 2

    def body(*refs):
        ins, outs, z_scr = refs[:8], refs[8:14], refs[14]
        s = pl.program_id(0)

        @pl.when(s == 0)
        def _():
            z_scr[...] = jnp.zeros_like(z_scr)

        qs = jnp.where(s >= nc, scale, 0.0)
        for di, rev in enumerate((False, True)):
            q_ref, k_ref, v_ref, g_ref = ins[4 * di:4 * di + 4]
            o_ref, zs_ref, b_ref = outs[3 * di:3 * di + 3]
            mask, _ = _chunk_masks(c, rev)
            m_bf = mask.astype(BF16)
            far = 0 if rev else c - 1
            for b in range(nb):
                bc = _tri_mm(m_bf, g_ref[b])
                b_ref[b] = bc
                for h in range(HEADS):
                    ks, vs = slice(h * hk, (h + 1) * hk), slice(h * hv, (h + 1) * hv)
                    zi = (di * nb + b) * HEADS + h
                    v = v_ref[b, :, vs]
                    t = _chunk_terms(q_ref[b, :, ks] * qs, k_ref[b, :, ks], bc[:, ks], far, mid)
                    a = jnp.where(mask, _mm_nt(t["qem"], t["kim"]), 0.0)
                    z = z_scr[zi]
                    zs_ref[0, b * HEADS + h] = z
                    o_ref[b, :, vs] = _mm(a, v) + _mm_nt(t["qe"], z)
                    z_scr[zi] = z * t["dec"] + _mm_tn(v, t["kd"])

    in_specs, out_specs, out_shape = [], [], []
    for di, rev in enumerate((False, True)):
        ch = functools.partial(_scan_chunk, nl=nl, nc=nc, rev=rev)
        lch = functools.partial(_scan_lat_chunk, nl=nl, nc=nc, rev=rev)
        in_specs += [pl.BlockSpec((nb, c, dk_), lambda s, ch=ch: (0, ch(s), 0)),
                     pl.BlockSpec((nb, c, dk_), lambda s, ch=ch: (0, ch(s), 1)),
                     pl.BlockSpec((nb, c, dv_), lambda s, ch=ch: (0, ch(s), 0)),
                     pl.BlockSpec((nb, c, dk_), lambda s, ch=ch, di=di: (0, ch(s), di))]
        out_specs += [pl.BlockSpec((nb, c, dv_), lambda s, lch=lch: (0, lch(s), 0)),
                      pl.BlockSpec((1, nb * HEADS, hv, hk), lambda s: (s, 0, 0, 0)),
                      pl.BlockSpec((nb, c, dk_), lambda s, ch=ch: (0, ch(s), 0))]
        out_shape += [jax.ShapeDtypeStruct((nb, s_len, dv_), F32),
                      jax.ShapeDtypeStruct((ns, nb * HEADS, hv, hk), F32),
                      jax.ShapeDtypeStruct((nb, l_len, dk_), F32)]
    return pl.pallas_call(
        body, name="gla_fwd", grid=(ns,), in_specs=in_specs, out_specs=tuple(out_specs), out_shape=tuple(out_shape),
        scratch_shapes=[pltpu.VMEM((2 * nb * HEADS, hv, hk), F32)],
        compiler_params=_params())(pb3, pb3, pv3, g3, pb3, pb3, pv3, g3)


def _gla_bwd(pb3, pv3, do3, fwd_saved, nb, s_len, c_len, dk_, dv_):
    c = CHUNK
    nl, nc = s_len // c, c_len // c
    ns = nl + nc
    hk, hv = dk_ // HEADS, dv_ // HEADS
    l_len = pb3.shape[1]
    scale = hk ** -0.5
    mid = c // 2
    zs_f, b_f, zs_b, b_b = fwd_saved

    def body(*refs):
        ins, outs, dz_scr = refs[:12], refs[12:20], refs[20]
        s = pl.program_id(0)
        step = ns - 1 - s

        @pl.when(s == 0)
        def _():
            dz_scr[...] = jnp.zeros_like(dz_scr)

        lat = step >= nc
        qs = jnp.where(lat, scale, 0.0)
        dmul = jnp.where(lat, 1.0, 0.0)
        for di, rev in enumerate((False, True)):
            q_ref, k_ref, v_ref, b_ref, do_ref, zs_ref = ins[6 * di:6 * di + 6]
            dq_ref, dk_ref, dv_ref, dg_ref = outs[4 * di:4 * di + 4]
            mask, mask_t = _chunk_masks(c, rev)
            mt_bf = mask_t.astype(BF16)
            far = 0 if rev else c - 1
            far_row = lax.broadcasted_iota(jnp.int32, (c, hk), 0) == far
            for b in range(nb):
                db_parts = []
                for h in range(HEADS):
                    ks, vs = slice(h * hk, (h + 1) * hk), slice(h * hv, (h + 1) * hv)
                    zi = (di * nb + b) * HEADS + h
                    v = v_ref[b, :, vs]
                    d_o = do_ref[b, :, vs] * dmul
                    t = _chunk_terms(q_ref[b, :, ks] * qs, k_ref[b, :, ks], b_ref[b, :, ks], far, mid)
                    qem, kim, qe, kd = t["qem"], t["kim"], t["qe"], t["kd"]
                    a_t = jnp.where(mask_t, _mm_nt(kim, qem), 0.0)
                    d_a = jnp.where(mask, _mm_nt(d_o, v), 0.0)
                    d_at = jnp.where(mask_t, _mm_nt(v, d_o), 0.0)
                    z = zs_ref[0, b * HEADS + h]
                    dzn = dz_scr[zi]
                    dv_ref[b, :, vs] = (_mm(a_t, d_o) + _mm_nt(kd, dzn)).astype(dv_ref.dtype)
                    dqem = _mm(d_a, kim)
                    dkim = _mm(d_at, qem)
                    dqe = _mm(d_o, z)
                    dkd = _mm(v, dzn)
                    ddec = jnp.sum(z * dzn, axis=0, keepdims=True)
                    dz_scr[zi] = dzn * t["dec"] + _mm_tn(d_o, qe)
                    dq_ref[b, :, ks] = ((dqem * t["em"] + dqe * t["e"]) * qs).astype(dq_ref.dtype)
                    dk_ref[b, :, ks] = (dkim * t["eim"] + dkd * t["ed"]).astype(dk_ref.dtype)
                    db = dqem * qem - dkim * kim + dqe * qe - dkd * kd
                    extra = jnp.sum(dkd * kd, axis=0, keepdims=True) + ddec * t["dec"]
                    db_parts.append(db + jnp.where(far_row, extra, 0.0))
                dg_ref[b] = _tri_mm(mt_bf, jnp.concatenate(db_parts, axis=1))

    in_specs, out_specs, out_shape, args = [], [], [], []
    for di, rev in enumerate((False, True)):
        ch = lambda s, rev=rev: _scan_chunk(ns - 1 - s, nl, nc, rev)
        lch = lambda s, rev=rev: _scan_lat_chunk(ns - 1 - s, nl, nc, rev)
        in_specs += [pl.BlockSpec((nb, c, dk_), lambda s, ch=ch: (0, ch(s), 0)),
                     pl.BlockSpec((nb, c, dk_), lambda s, ch=ch: (0, ch(s), 1)),
                     pl.BlockSpec((nb, c, dv_), lambda s, ch=ch: (0, ch(s), 0)),
                     pl.BlockSpec((nb, c, dk_), lambda s, ch=ch: (0, ch(s), 0)),
                     pl.BlockSpec((nb, c, dv_), lambda s, lch=lch: (0, lch(s), 0)),
                     pl.BlockSpec((1, nb * HEADS, hv, hk), lambda s: (ns - 1 - s, 0, 0, 0))]
        args += [pb3, pb3, pv3, (b_b if rev else b_f), do3, (zs_b if rev else zs_f)]
        for w, dt in ((dk_, BF16), (dk_, BF16), (dv_, BF16), (dk_, F32)):
            out_specs.append(pl.BlockSpec((nb, c, w), lambda s, ch=ch: (0, ch(s), 0)))
            out_shape.append(jax.ShapeDtypeStruct((nb, l_len, w), dt))
    return pl.pallas_call(
        body, name="gla_bwd", grid=(ns,), in_specs=in_specs, out_specs=tuple(out_specs), out_shape=tuple(out_shape),
        scratch_shapes=[pltpu.VMEM((2 * nb * HEADS, hv, hk), F32)],
        compiler_params=_params())(*args)


def _tail(a1, pa, o_f, o_b, x2, tgt, mod, wc, wg, wo, ln_g, ln_b, gn_t, fg, nb, tm, n_split):
    tl, d = x2.shape
    nt = tl // tm
    per_ex = nt // nb
    hv = d // HEADS
    nrow = mod.shape[0]

    def part(shared, a1_ref, z_ref, r_ref, mc_ref, mg_ref, of_ref, ob_ref, x_ref, t_ref,
             dp_ref, da1_ref, do_ref, gx_ref, mrg_ref, dmo_ref, yci_ref, dyc_ref, ogi_ref, dyg_ref, sm_ref):
        bidx, gate, lng, lnb, fgv, gn, wc_, wg_, wo_ = shared

        a1v = a1_ref[...]
        mu = jnp.mean(a1v, axis=-1, keepdims=True)
        xc = a1v - mu
        rs = lax.rsqrt(jnp.mean(xc * xc, axis=-1, keepdims=True) + EPS)
        xh = xc * rs
        a2 = xh * lng + lnb
        s2 = _sigmoid(a2)
        a3 = a2 * s2
        zv = z_ref[...]
        sz = _sigmoid(zv)
        siluz = zv * sz
        ycin = a3 * siluz
        yconv = _mm(ycin, wc_)

        o = of_ref[...] + ob_ref[...]
        ohat_parts, rn_parts = [], []
        for h in range(HEADS):
            oh = o[:, h * hv:(h + 1) * hv]
            rn = lax.rsqrt(jnp.mean(oh * oh, axis=-1, keepdims=True) + EPS)
            ohat_parts.append(oh * rn)
            rn_parts.append(rn)
        ohat = jnp.concatenate(ohat_parts, axis=1)
        on = ohat * gn
        rv = r_ref[...]
        sr = _sigmoid(rv)
        silur = rv * sr
        ogin = on * silur
        ygla = _mm(ogin, wg_)

        sc = _sigmoid(mc_ref[...])
        sg = _sigmoid(mg_ref[...])
        merged = sc * yconv + sg * ygla
        mo = _mm(merged, wo_)
        hn = x_ref[...] + gate * mo
        rf = lax.rsqrt(jnp.mean(hn * hn, axis=-1, keepdims=True) + EPS)
        yh = hn * rf
        err = yh * fgv - t_ref[...]
        loss_part = 0.5 * jnp.sum(err * err) * (1.0 / d)

        dy = err * (1.0 / d)
        dfg = jnp.sum(dy * yh, axis=0, keepdims=True)
        dyh = dy * fgv
        dhn = rf * (dyh - yh * jnp.mean(dyh * yh, axis=-1, keepdims=True))
        gx_ref[...] = dhn
        dgate = jnp.sum(dhn * mo, axis=0, keepdims=True)
        dmo = gate * dhn
        dmerged = _mm_nt(dmo, wo_)
        dyconv = dmerged * sc
        dygla = dmerged * sg
        dp_ref[:, 2 * d:3 * d] = (dmerged * yconv * sc * (1.0 - sc)).astype(BF16)
        dp_ref[:, 3 * d:4 * d] = (dmerged * ygla * sg * (1.0 - sg)).astype(BF16)
        dycin = _mm_nt(dyconv, wc_)
        dogin = _mm_nt(dygla, wg_)
        mrg_ref[...] = merged.astype(BF16)
        dmo_ref[...] = dmo.astype(BF16)
        yci_ref[...] = ycin.astype(BF16)
        dyc_ref[...] = dyconv.astype(BF16)
        ogi_ref[...] = ogin.astype(BF16)
        dyg_ref[...] = dygla.astype(BF16)

        da3 = dycin * siluz
        dp_ref[:, 0:d] = (dycin * a3 * _dsilu(zv, sz)).astype(BF16)
        da2 = da3 * _dsilu(a2, s2)
        dlng = jnp.sum(da2 * xh, axis=0, keepdims=True)
        dlnb = jnp.sum(da2, axis=0, keepdims=True)
        dxh = da2 * lng
        da1_ref[...] = rs * (dxh - jnp.mean(dxh, axis=-1, keepdims=True)
                             - xh * jnp.mean(dxh * xh, axis=-1, keepdims=True))

        don = dogin * silur
        dp_ref[:, d:2 * d] = (dogin * on * _dsilu(rv, sr)).astype(BF16)
        dgn = jnp.sum(don * ohat, axis=0, keepdims=True)
        dyn = don * gn
        for h in range(HEADS):
            vs = slice(h * hv, (h + 1) * hv)
            oh_hat = ohat_parts[h]
            dh = dyn[:, vs]
            do_ref[:, vs] = (rn_parts[h] * (dh - oh_hat * jnp.mean(dh * oh_hat, axis=-1, keepdims=True))
                             ).astype(BF16)

        sm_ref[0:1, :] += dfg
        sm_ref[1:2, :] += dlng
        sm_ref[2:3, :] += dlnb
        sm_ref[3:4, :] += dgn
        sm_ref[4:5, :] += jnp.zeros((1, d), F32) + loss_part
        for b in range(nb):
            sm_ref[8 + b:9 + b, :] += jnp.where(bidx == b, dgate, 0.0)

    def body(*refs):
        mod_ref, wc_ref, wg_ref, wo_ref, lng_ref, lnb_ref, gn_ref, fg_ref = refs[9:17]
        sm_ref = refs[27]
        i = pl.program_id(0)

        @pl.when(i == 0)
        def _():
            sm_ref[...] = jnp.zeros_like(sm_ref)

        bidx = i // per_ex
        shared = (bidx, _rowsel(mod_ref[...], bidx, nb)[:, 2 * d:3 * d], lng_ref[...], lnb_ref[...], fg_ref[...],
                  jnp.concatenate([gn_ref[...]] * HEADS, axis=1), wc_ref[...], wg_ref[...], wo_ref[...])
        rows_per = tm // n_split
        for p in range(n_split):
            rows = pl.ds(p * rows_per, rows_per)
            part(shared, *[r.at[rows] for r in refs[0:9]], *[r.at[rows] for r in refs[17:27]], sm_ref)

    row = pl.BlockSpec((tm, d), lambda i: (i, 0))
    pcol = lambda blk: pl.BlockSpec((tm, d), lambda i: (i, blk))
    full = lambda arr: pl.BlockSpec(arr.shape, lambda i: (0,) * arr.ndim)
    bfo = jax.ShapeDtypeStruct((tl, d), BF16)
    f32o = jax.ShapeDtypeStruct((tl, d), F32)
    return pl.pallas_call(
        body, name="tail", grid=(nt,),
        in_specs=[row, pcol(2), pcol(3), pcol(4), pcol(5), row, row, row, row, full(mod), full(wc), full(wg),
                  full(wo), full(ln_g), full(ln_b), full(gn_t), full(fg)],
        out_specs=(pl.BlockSpec((tm, 4 * d), lambda i: (i, 0)), row, row, row, row, row, row, row, row, row,
                   pl.BlockSpec((16, d), lambda i: (0, 0))),
        out_shape=(jax.ShapeDtypeStruct((tl, 4 * d), BF16), f32o, bfo, f32o, bfo, bfo, bfo, bfo, bfo, bfo,
                   jax.ShapeDtypeStruct((16, d), F32)),
        compiler_params=_params())(a1, pa, pa, pa, pa, o_f, o_b, x2, tgt, mod, wc, wg, wo, ln_g, ln_b, gn_t, fg)


def _local_step(x, c, ctx, tgt, c_ctx, ada_w8, ada_b, norm_g, w_a, b_a, w_b, b_b, conv_w8, conv_b, ln_g, ln_b,
                up2, bias2, gla_norm_g, final_norm_g, proj, on_grads=None, on_du_a1=None):
    nb, s_len, d = x.shape
    c_len = ctx.shape[1]
    dk_, dv_ = d // 2, d
    tl, tc = nb * s_len, nb * c_len
    nbw = 2 * dk_ + dv_ + LANE
    tm = math.gcd(256, c_len)
    tiles = _Tiles(nb, s_len, c_len, tm, 2)
    l_len = tiles.rows_per_ex
    t_all = nb * l_len
    x2, ctx2, tgt2 = x.reshape(tl, d), ctx.reshape(tc, d), tgt.reshape(tl, d)

    cv = jnp.zeros((8, d), F32).at[0:nb].set(c).at[nb].set(c_ctx.reshape(d))
    mod = _ada_fwd(cv, ada_w8, ada_b)
    u = _norm_fwd(x2, ctx2, mod, norm_g, tiles)
    u3 = u.reshape(nb, l_len, d)
    tma = math.gcd(1024, s_len)
    pa = _matmul_bias("inproj_a", u3, w_a, b_a, s_len, tma, _tile(6 * d, 2048))
    tmb = math.gcd(1024, t_all)
    pb, pv = _inproj_b(u, w_b, b_b, tmb, dk_, dv_)

    a1 = _conv_fwd(pa, conv_w8, conv_b, nb, s_len)
    lr_blk = (2 * dk_) // LANE
    g_all = _decay_fwd(pb, up2, bias2, tm, lr_blk)
    pb3, pv3 = pb.reshape(nb, l_len, 2 * dk_ + LANE), pv.reshape(nb, l_len, dv_)
    o_f, zs_f, b_f, o_b, zs_b, b_b2 = _gla_fwd(pb3, pv3, g_all.reshape(nb, l_len, 2 * dk_), nb, s_len, c_len,
                                               dk_, dv_)

    conv_proj, gla_proj, w_out = proj(a1) if callable(proj) else proj
    tt = math.gcd(256, s_len)
    (dp_a2, da1, d_o, gx1, merged, dmo, ycin, dyconv, ogin, dygla, small) = _tail(
        a1, pa, o_f.reshape(tl, dv_), o_b.reshape(tl, dv_), x2, tgt2, mod, conv_proj, gla_proj, w_out, ln_g, ln_b,
        gla_norm_g, final_norm_g, nb, tt, 2)

    lat3 = lambda a: a.reshape(nb, s_len, a.shape[-1])
    tnw = _tile(d, 512)
    d_w_out, _ = _matmul_tn_whole("dw_out", lat3(merged), lat3(dmo), s_len, tnw, False)
    d_conv_proj, _ = _matmul_tn_whole("dw_conv_proj", lat3(ycin), lat3(dyconv), s_len, tnw, False)
    d_gla_proj, _ = _matmul_tn_whole("dw_gla_proj", lat3(ogin), lat3(dygla), s_len, tnw, False)

    dp_a1, d_conv_w8, d_conv_b = _conv_bwd(pa, da1, conv_w8, nb, s_len)
    gl = _gla_bwd(pb3, pv3, d_o.reshape(nb, s_len, dv_), (zs_f, b_f, zs_b, b_b2), nb, s_len, c_len, dk_, dv_)
    gl = [g_.reshape(t_all, g_.shape[-1]) for g_ in gl]
    dp_b, d_up2, d_bias2 = _decay_bwd(pb, up2, bias2, gl[0:4], gl[4:8], tiles, lr_blk, dk_, dv_)

    dw_a1, db_a1 = _matmul_tn_whole("dw_a1", u3, lat3(dp_a1), s_len, tnw, True)
    dw_a2, db_a2 = _matmul_tn_whole("dw_a2", u3, lat3(dp_a2), s_len, tnw, True)
    dw_b, db_b = _matmul_tn("dw_b", u, dp_b, t_all, tmb, nbw)
    grads = dict(w_a1=dw_a1, w_a2=dw_a2, w_b=dw_b, conv_w8=d_conv_w8, conv_proj=d_conv_proj, up2=d_up2,
                 gla_proj=d_gla_proj, w_out=d_w_out)

    tka = _tile(2 * d, 2048)
    du_a1 = _matmul_nt("du_a1", dp_a1, w_a, 0, tma, tka, after=on_grads(grads) if on_grads else ())
    du_a2 = _matmul_nt("du_a2", dp_a2, w_a, (2 * d) // tka, tma, tka, after=on_du_a1(du_a1) if on_du_a1 else ())
    du_b = _matmul_nt("du_b", dp_b, w_b, 0, tmb, nbw)
    grad_x2, dmod_ss, d_norm_g = _norm_bwd(x2, ctx2, mod, norm_g, [du_a1, du_a2], du_b, gx1, tiles)
    d_ada_w8, d_ada_b, d_cv = _ada_bwd(cv, ada_w8, dmod_ss, small, nb)

    return dict(
        grads, grad_x=grad_x2.reshape(nb, s_len, d), small=small, cv=d_cv, ada_w8=d_ada_w8, ada_b=d_ada_b,
        norm_g=d_norm_g, b_a1=db_a1, b_a2=db_a2, b_b=db_b, conv_b=d_conv_b, bias2=d_bias2)


def _regroup_pieces(d, r, wshard):
    cb = d // N_DEV
    segs = []
    for j in range(N_DEV):
        segs.append((j * cb, cb, 0, 2 * j * cb))
    for j in range(N_DEV):
        segs.append((d + j * cb, cb, 0, (2 * j + 1) * cb))
    segs += [(2 * d, d, 0, 2 * d), (3 * d, 2 * d + 2 * r, 1, 0), (5 * d + 2 * r, 3 * d, 0, 3 * d)]
    pieces = []
    for o0, w, dst, d0 in segs:
        lo = o0
        while lo < o0 + w:
            j = lo // wshard
            hi = min(o0 + w, (j + 1) * wshard)
            pieces.append((j, lo - j * wshard, hi - lo, dst, d0 + lo - o0))
            lo = hi
    return pieces


def _regroup(o, d, r):
    n_in = 8 * d + 2 * r
    parts = ([], [])
    for _, s0, n, dst, _ in sorted(_regroup_pieces(d, r, n_in), key=lambda p: (p[3], p[4])):
        parts[dst].append(o[..., s0:s0 + n])
    pad = jnp.zeros(o.shape[:-1] + (LANE - 2 * r,), o.dtype)
    return jnp.concatenate(parts[0], axis=-1), jnp.concatenate(parts[1] + [pad], axis=-1)


def _unshard_w_in(g_win, d, r, after=()):
    n_sh, _, ws = g_win.shape
    nbw = 2 * d + LANE
    pieces = _regroup_pieces(d, r, ws)
    tr = math.gcd(d, 256)

    def body(g_ref, *rest):
        a_ref, b_ref = rest[len(after):]
        dsts = (a_ref, b_ref)
        for j, s0, n, dst, d0 in pieces:
            dsts[dst][:, pl.ds(d0, n)] = g_ref[j, :, pl.ds(s0, n)]
        b_ref[:, pl.ds(2 * d + 2 * r, LANE - 2 * r)] = jnp.zeros((tr, LANE - 2 * r), b_ref.dtype)

    return pl.pallas_call(
        body, name="unshard_w_in", grid=(d // tr,),
        in_specs=[pl.BlockSpec((n_sh, tr, ws), lambda i: (0, i, 0))] + [_ANY] * len(after),
        out_specs=(pl.BlockSpec((tr, 6 * d), lambda i: (i, 0)), pl.BlockSpec((tr, nbw), lambda i: (i, 0))),
        out_shape=(jax.ShapeDtypeStruct((d, 6 * d), g_win.dtype), jax.ShapeDtypeStruct((d, nbw), g_win.dtype)),
        compiler_params=_params())(g_win, *after)


def _reshard_w_in(dwt_a1, dwt_a2, dwt_b, d, r):
    ws = (8 * d + 2 * r) // N_DEV
    pieces = _regroup_pieces(d, r, ws)
    tc = math.gcd(d, 256)

    def body(a1_ref, a2_ref, b_ref, o_ref):
        for j, s0, n, dst, d0 in pieces:
            if dst == 1:
                src = b_ref[pl.ds(d0, n), :]
            elif d0 < 2 * d:
                src = a1_ref[pl.ds(d0, n), :]
            else:
                src = a2_ref[pl.ds(d0 - 2 * d, n), :]
            o_ref[j, pl.ds(s0, n), :] = src

    col = lambda h: pl.BlockSpec((h, tc), lambda i: (0, i))
    return pl.pallas_call(
        body, name="reshard_w_in", grid=(d // tc,),
        in_specs=[col(2 * d), col(4 * d), col(2 * d + LANE)],
        out_specs=pl.BlockSpec((N_DEV, ws, tc), lambda i: (0, 0, i)),
        out_shape=jax.ShapeDtypeStruct((N_DEV, ws, d), dwt_b.dtype),
        compiler_params=_params())(dwt_a1, dwt_a2, dwt_b)


_SMALL = ("c_ctx", "ada_b", "norm_g", "b_in", "conv_b", "conv_ln_g", "conv_ln_b", "decay_bias_fwd",
          "decay_bias_bwd", "gla_norm_g", "final_norm_g")


def _small_layout(d, r):
    sizes = dict(c_ctx=d, ada_b=3 * d, norm_g=d, b_in=8 * d + 2 * r, conv_b=d, conv_ln_g=d, conv_ln_b=d,
                 decay_bias_fwd=d // 2, decay_bias_bwd=d // 2, gla_norm_g=d // HEADS, final_norm_g=d, loss=1)
    table, off = {}, 0
    for name in _SMALL + ("loss",):
        table[name] = (off, sizes[name])
        off += -(-sizes[name] // LANE) * LANE
    return table, off


def _pack_small(g, nb, d, r):
    table, width = _small_layout(d, r)
    hv = d // HEADS
    pieces = _regroup_pieces(d, r, 8 * d + 2 * r)
    names = ("small", "cv", "ada_b", "norm_g", "b_a1", "b_a2", "b_b", "conv_b", "bias2")

    def body(sm, cv, ab, ng, ba1, ba2, bb, cvb, b2, o_ref):
        o_ref[...] = jnp.zeros_like(o_ref)

        def put(name, val):
            off, n = table[name]
            o_ref[:, pl.ds(off, n)] = val

        put("c_ctx", cv[nb:nb + 1, :])
        put("ada_b", ab[...])
        put("norm_g", ng[...])
        off_b = table["b_in"][0]
        for _, s0, n, dst, d0 in pieces:
            if dst == 1:
                src = bb[:, pl.ds(d0, n)]
            elif d0 < 2 * d:
                src = ba1[:, pl.ds(d0, n)]
            else:
                src = ba2[:, pl.ds(d0 - 2 * d, n)]
            o_ref[:, pl.ds(off_b + s0, n)] = src
        put("conv_b", cvb[...])
        put("conv_ln_g", sm[1:2, :])
        put("conv_ln_b", sm[2:3, :])
        put("decay_bias_fwd", b2[:, 0:d // 2])
        put("decay_bias_bwd", b2[:, d // 2:d])
        gn = sm[3:4, 0:hv]
        for h in range(1, HEADS):
            gn = gn + sm[3:4, h * hv:(h + 1) * hv]
        put("gla_norm_g", gn)
        put("final_norm_g", sm[0:1, :])
        put("loss", sm[4:5, 0:1])

    return pl.pallas_call(body, name="pack_small", out_shape=jax.ShapeDtypeStruct((1, width), F32),
                          compiler_params=_params())(*[g[k] for k in names])


def _small_adam(parts, ws, ms, vs, d, r):
    table, width = _small_layout(d, r)
    n_parts = parts.shape[0]
    k = len(_SMALL)
    bc1 = 1.0 - ADAM_B1 ** ADAM_STEP
    bc2 = 1.0 - ADAM_B2 ** ADAM_STEP

    def body(p_ref, *refs):
        w_refs, m_refs, v_refs = refs[0:k], refs[k:2 * k], refs[2 * k:3 * k]
        outs = refs[3 * k:]
        tot = p_ref[0]
        for i in range(1, n_parts):
            tot = tot + p_ref[i]
        for i, name in enumerate(_SMALL):
            off, n = table[name]
            g = tot[:, off:off + n]
            mn = ADAM_B1 * m_refs[i][...] + (1.0 - ADAM_B1) * g
            vn = ADAM_B2 * v_refs[i][...] + (1.0 - ADAM_B2) * (g * g)
            outs[i][...] = g
            outs[k + i][...] = -ADAM_LR * ((mn / bc1) / (jnp.sqrt(vn / bc2) + ADAM_EPS) + ADAM_WD * w_refs[i][...])
            outs[2 * k + i][...] = mn
            outs[3 * k + i][...] = vn
        off, _ = table["loss"]
        outs[4 * k][...] = tot[:, off:off + 1]

    shapes = [jax.ShapeDtypeStruct(w.shape, F32) for w in ws]
    res = pl.pallas_call(body, name="small_adam", out_shape=tuple(shapes * 4 + [jax.ShapeDtypeStruct((1, 1), F32)]),
                         compiler_params=_params())(parts, *ws, *ms, *vs)
    return res[0:k], res[k:2 * k], res[2 * k:3 * k], res[3 * k:4 * k], res[4 * k]


def _mesh_pos():
    return lax.axis_index("x"), lax.axis_index("y"), lax.axis_index("c")


def _all_gather(arrs):
    n = len(arrs)
    ns = 9
    split = [a.ndim == 2 and a.shape[0] % 32 == 0 for a in arrs]

    def body(*refs):
        ins, outs = refs[:n], refs[n:2 * n]
        send_sems, recv_sems, local_sems = refs[2 * n:]
        x, y, c = _mesh_pos()
        me, sibling = (x, y, c), (x, y, 1 - c)
        xn, yn, dg = (1 - x, y, c), (x, 1 - y, c), (1 - x, 1 - y, c)
        other = lambda pos: (pos[0], pos[1], 1 - c)

        def slot(a, pos, half):
            ref = outs[a].at[4 * pos[0] + 2 * pos[1] + pos[2]]
            if half is None:
                return ref
            rows = arrs[a].shape[0] // 2
            return ref.at[pl.ds(half * rows, rows)]

        def copy(a, k, block, to, src=None, half=None):
            dst = slot(a, block, half)
            return pltpu.make_async_remote_copy(
                src_ref=dst if src is None else src, dst_ref=dst,
                send_sem=send_sems.at[ns * a + k], recv_sem=recv_sems.at[ns * a + k],
                device_id=to, device_id_type=MESH)

        h0 = lambda a: 0 if split[a] else None
        mine = [pltpu.make_async_copy(ins[a], slot(a, me, None), local_sems.at[a]) for a in range(n)]
        for cp in mine:
            cp.start()
        sent = []
        for a in range(n):
            sent += [copy(a, 0, me, sibling, src=ins[a]), copy(a, 1, me, xn, src=ins[a]),
                     copy(a, 2, me, yn, src=ins[a])]
        for cp in sent:
            cp.start()

        def pass_on(cp):
            cp.start()
            sent.append(cp)

        for a in range(n):
            copy(a, 1, xn, me).wait_recv()
            pass_on(copy(a, 3, xn, sibling))
            pass_on(copy(a, 4, xn, yn, half=h0(a)))
        for a in range(n):
            copy(a, 2, yn, me).wait_recv()
            pass_on(copy(a, 5, yn, sibling))
            if split[a]:
                pass_on(copy(a, 6, yn, xn, half=1))
        for a in range(n):
            copy(a, 4, dg, me, half=h0(a)).wait_recv()
            pass_on(copy(a, 7, dg, sibling, half=h0(a)))
            if split[a]:
                copy(a, 6, dg, me, half=1).wait_recv()
                pass_on(copy(a, 8, dg, sibling, half=1))
        for a in range(n):
            copy(a, 0, sibling, me).wait_recv()
            copy(a, 3, other(xn), me).wait_recv()
            copy(a, 5, other(yn), me).wait_recv()
            copy(a, 7, other(dg), me, half=h0(a)).wait_recv()
            if split[a]:
                copy(a, 8, other(dg), me, half=1).wait_recv()
        for cp in sent:
            cp.wait_send()
        for cp in mine:
            cp.wait()

    return pl.pallas_call(
        body, name="all_gather",
        out_shape=tuple(jax.ShapeDtypeStruct((N_DEV,) + a.shape, a.dtype) for a in arrs),
        in_specs=[_ANY] * n, out_specs=tuple([_ANY] * n),
        scratch_shapes=[pltpu.SemaphoreType.DMA((ns * n,)), pltpu.SemaphoreType.DMA((ns * n,)),
                        pltpu.SemaphoreType.DMA((n,))],
    )(*arrs)


def _exchange_sibling(arrs):
    n = len(arrs)

    def body(*refs):
        ins, outs = refs[:n], refs[n:2 * n]
        send_sems, recv_sems = refs[2 * n:]
        x, y, c = _mesh_pos()
        copies = [pltpu.make_async_remote_copy(
            src_ref=ins[a].at[2 * k + (1 - c)], dst_ref=outs[a].at[k],
            send_sem=send_sems.at[4 * a + k], recv_sem=recv_sems.at[4 * a + k],
            device_id=(x, y, 1 - c), device_id_type=MESH) for a in range(n) for k in range(4)]
        for cp in copies:
            cp.start()
        for cp in copies:
            cp.wait_recv()
        for cp in copies:
            cp.wait_send()

    return pl.pallas_call(
        body, name="grad_exchange_sibling",
        out_shape=tuple(jax.ShapeDtypeStruct((4,) + a.shape[1:], a.dtype) for a in arrs),
        in_specs=[_ANY] * n, out_specs=tuple([_ANY] * n),
        scratch_shapes=[pltpu.SemaphoreType.DMA((4 * n,)), pltpu.SemaphoreType.DMA((4 * n,))],
    )(*arrs)


def _elementwise_tile(r, cdim):
    if r % 8 == 0 and r > 256:
        return math.gcd(r, 256), cdim
    if r > 256 and cdim % 256 == 0:
        return r, 256
    return r, cdim


def _pair_sum(name, mine, theirs):
    _, r, cdim = mine.shape
    tr, tc = _elementwise_tile(r, cdim)

    def body(m_ref, t_ref, o_ref):
        c = lax.axis_index("c")
        own = jnp.where(c == 0, m_ref[:, 0].astype(F32), m_ref[:, 1].astype(F32))
        o_ref[...] = (own + t_ref[...].astype(F32)).astype(o_ref.dtype)

    return pl.pallas_call(
        body, name=name, grid=(r // tr, cdim // tc),
        in_specs=[pl.BlockSpec((4, 2, tr, tc), lambda i, j: (0, 0, i, j)),
                  pl.BlockSpec((4, tr, tc), lambda i, j: (0, i, j))],
        out_specs=pl.BlockSpec((4, tr, tc), lambda i, j: (0, i, j)),
        out_shape=jax.ShapeDtypeStruct((4, r, cdim), mine.dtype),
        compiler_params=_params())(mine.reshape(4, 2, r, cdim), theirs)


_HBM = pl.BlockSpec(memory_space=pltpu.HBM)
_SEM = pl.BlockSpec(memory_space=pltpu.SEMAPHORE)


def _copies_start(name, srcs, lands, make_copies, n_sems):
    n, m = len(srcs), len(lands)

    def body(*refs):
        ins = refs[:n + m]
        send_sems, recv_sems = refs[n + m], refs[n + m + 1]
        for cp in make_copies(ins[:n], ins[n:], send_sems, recv_sems):
            cp.start()
        refs[-1][...] = jnp.zeros_like(refs[-1])

    res = pl.pallas_call(
        body, name=name,
        out_shape=(pltpu.SemaphoreType.DMA((n_sems,)), pltpu.SemaphoreType.DMA((n_sems,)),
                   *[pltpu.HBM(a.shape, a.dtype) for a in (*srcs, *lands)], jax.ShapeDtypeStruct((8, LANE), F32)),
        in_specs=[_HBM] * (n + m),
        out_specs=(_SEM, _SEM, *[_HBM] * (n + m), pl.BlockSpec(memory_space=pltpu.VMEM)),
        input_output_aliases={i: 2 + i for i in range(n + m)},
        compiler_params=pltpu.CompilerParams(has_side_effects=pltpu.SideEffectType.DATAFLOW_SIDE_EFFECTING),
    )(*[pltpu.with_memory_space_constraint(a, pltpu.HBM) for a in (*srcs, *lands)])
    return res[0], res[1], res[2:2 + n], res[2 + n:2 + n + m], res[-1]


def _copies_wait(name, started, after, make_copies):
    send_sems, recv_sems, srcs, lands, _ = started
    n, m = len(srcs), len(lands)

    def body(*refs):
        ins = refs[:n + m]
        for cp in make_copies(ins[:n], ins[n:], refs[n + m], refs[n + m + 1]):
            cp.wait_send()
            cp.wait_recv()

    res = pl.pallas_call(
        body, name=name,
        out_shape=tuple(pltpu.HBM(a.shape, a.dtype) for a in (*srcs, *lands)),
        in_specs=[_HBM] * (n + m) + [_SEM, _SEM] + [_ANY] * len(after),
        out_specs=tuple([_HBM] * (n + m)),
        input_output_aliases={i: i for i in range(n + m)},
        compiler_params=pltpu.CompilerParams(has_side_effects=pltpu.SideEffectType.DATAFLOW_SIDE_EFFECTING),
    )(*srcs, *lands, send_sems, recv_sems, *after)
    return res[:n], res[n:]


def _gather_copies(srcs, lands, send_sems, recv_sems):
    x, y, c = _mesh_pos()
    me_i = 4 * x + 2 * y + c
    copies = []
    for rel in range(1, N_DEV):
        peer = (1 - x if rel & 4 else x, 1 - y if rel & 2 else y, 1 - c if rel & 1 else c)
        for a in range(len(srcs)):
            copies.append(pltpu.make_async_remote_copy(
                src_ref=srcs[a], dst_ref=lands[a].at[me_i], send_sem=send_sems.at[7 * a + rel - 1],
                recv_sem=recv_sems.at[7 * a + rel - 1], device_id=peer, device_id_type=MESH))
    return copies


def _sibling_copies(srcs, lands, send_sems, recv_sems):
    x, y, c = _mesh_pos()
    return [pltpu.make_async_remote_copy(
        src_ref=srcs[a].at[2 * k + (1 - c)], dst_ref=lands[a].at[k], send_sem=send_sems.at[4 * a + k],
        recv_sem=recv_sems.at[4 * a + k], device_id=(x, y, 1 - c), device_id_type=MESH)
        for a in range(len(srcs)) for k in range(4)]


def _chip_copies(srcs, lands, send_sems, recv_sems):
    x, y, c = _mesh_pos()
    my_chip = 2 * x + y
    copies = []
    for rel in range(1, 4):
        px = 1 - x if rel & 2 else x
        py = 1 - y if rel & 1 else y
        for a in range(len(srcs)):
            copies.append(pltpu.make_async_remote_copy(
                src_ref=srcs[a].at[2 * px + py], dst_ref=lands[a].at[my_chip], send_sem=send_sems.at[3 * a + rel - 1],
                recv_sem=recv_sems.at[3 * a + rel - 1], device_id=(px, py, c), device_id_type=MESH))
    return copies


def _sum_adam(name, parts, w, m, v, own=None):
    unit_mid = w.ndim == 3
    _, r, cdim = parts.shape
    n_parts = parts.shape[0]
    tr, tc = _elementwise_tile(r, cdim)
    bc1 = 1.0 - ADAM_B1 ** ADAM_STEP
    bc2 = 1.0 - ADAM_B2 ** ADAM_STEP
    extra = [] if own is None else [own]

    def body(p_ref, *refs):
        w_ref, m_ref, v_ref, g_ref, d_ref, nm_ref, nv_ref = refs[len(extra):]
        if own is None:
            part = lambda k: p_ref[k].astype(F32)
        else:
            my_chip = 2 * lax.axis_index("x") + lax.axis_index("y")
            part = lambda k: jnp.where(my_chip == k, refs[0][k], p_ref[k]).astype(F32)
        g = part(0)
        for k in range(1, n_parts):
            g = g + part(k)
        if unit_mid:
            g = g.reshape(tr, 1, tc)
        mn = ADAM_B1 * m_ref[...] + (1.0 - ADAM_B1) * g
        vn = ADAM_B2 * v_ref[...] + (1.0 - ADAM_B2) * (g * g)
        g_ref[...] = g
        nm_ref[...] = mn
        nv_ref[...] = vn
        d_ref[...] = -ADAM_LR * ((mn / bc1) / (jnp.sqrt(vn / bc2) + ADAM_EPS) + ADAM_WD * w_ref[...])

    blk = (pl.BlockSpec((tr, 1, tc), lambda i, j: (i, 0, j)) if unit_mid
           else pl.BlockSpec((tr, tc), lambda i, j: (i, j)))
    o = jax.ShapeDtypeStruct(w.shape, F32)
    return pl.pallas_call(
        body, name=name, grid=(r // tr, cdim // tc),
        in_specs=[pl.BlockSpec((n_parts, tr, tc), lambda i, j: (0, i, j))] * (1 + len(extra)) + [blk, blk, blk],
        out_specs=(blk, blk, blk, blk), out_shape=(o, o, o, o),
        compiler_params=_params())(parts, *extra, w, m, v)


_WEIGHTS = ("c_ctx", "ada_w", "ada_b", "norm_g", "w_in", "b_in", "conv_w", "conv_b", "conv_ln_g", "conv_ln_b",
            "conv_proj", "decay_up_fwd", "decay_bias_fwd", "decay_up_bwd", "decay_bias_bwd", "gla_norm_g",
            "gla_proj", "w_out", "final_norm_g")


def _as2d(a):
    if a.ndim == 1:
        return a.reshape(1, -1)
    return a.reshape(-1, a.shape[-1])


def kernel(x, c, ctx, c_ctx, ada_w, ada_b, norm_g, w_in, b_in, conv_w, conv_b, conv_ln_g, conv_ln_b, conv_proj, decay_up_fwd, decay_bias_fwd, decay_up_bwd, decay_bias_bwd, gla_norm_g, gla_proj, w_out, final_norm_g, loss_target, m_c_ctx, m_ada_w, m_ada_b, m_norm_g, m_w_in, m_b_in, m_conv_w, m_conv_b, m_conv_ln_g, m_conv_ln_b, m_conv_proj, m_decay_up_fwd, m_decay_bias_fwd, m_decay_up_bwd, m_decay_bias_bwd, m_gla_norm_g, m_gla_proj, m_w_out, m_final_norm_g, v_c_ctx, v_ada_w, v_ada_b, v_norm_g, v_w_in, v_b_in, v_conv_w, v_conv_b, v_conv_ln_g, v_conv_ln_b, v_conv_proj, v_decay_up_fwd, v_decay_bias_fwd, v_decay_up_bwd, v_decay_bias_bwd, v_gla_norm_g, v_gla_proj, v_w_out, v_final_norm_g):
    env = dict(locals())
    wts = {k: env[k] for k in _WEIGHTS}
    d = x.shape[-1]
    r = decay_up_fwd.shape[1]
    dk_ = d // 2

    ds, dks = d // N_DEV, dk_ // N_DEV
    g_win, g_ada, conv_w8, g_up = _all_gather(
        [w_in[0].astype(BF16), ada_w[0].astype(BF16), conv_w[0],
         jnp.concatenate([decay_up_fwd[0], decay_up_bwd[0]], axis=1)])
    proj_own = [conv_proj[0].astype(BF16), gla_proj[0].astype(BF16), w_out[0].astype(BF16)]
    me_i = 4 * lax.axis_index("x") + 2 * lax.axis_index("y") + lax.axis_index("c")
    proj_lands = [lax.dynamic_update_slice(lax.empty((N_DEV,) + a.shape, a.dtype), a[None], (me_i, 0, 0))
                  for a in proj_own]
    proj_start = _copies_start("proj_gather_start", proj_own, proj_lands, _gather_copies, 7 * 3)

    def proj(after):
        _, lands = _copies_wait("proj_gather_wait", proj_start, (after,), _gather_copies)
        return [w.reshape(d, d) for w in lands]

    w_a, w_b = _unshard_w_in(g_win, d, r, after=(proj_start[4],))
    up_f = g_up[:, :, 0:dks].transpose(1, 0, 2).reshape(r, dk_)
    up_b = g_up[:, :, dks:].transpose(1, 0, 2).reshape(r, dk_)
    up2 = jnp.zeros((LANE, 2 * dk_), F32).at[0:r, 0:dk_].set(up_f).at[r:2 * r, dk_:].set(up_b)
    bias2 = jnp.concatenate([decay_bias_fwd, decay_bias_bwd], axis=1)
    b_a, b_b = _regroup(b_in, d, r)

    names = ("w_in", "conv_proj", "gla_proj", "w_out", "conv_w", "decay_up")
    comm = {}

    def on_grads(gr):
        d_up = jnp.concatenate([gr["up2"][0:r, 0:dk_].reshape(r, N_DEV, dks).transpose(1, 0, 2),
                                gr["up2"][r:2 * r, dk_:].reshape(r, N_DEV, dks).transpose(1, 0, 2)], axis=2)
        mine = [_reshard_w_in(gr["w_a1"], gr["w_a2"], gr["w_b"], d, r), gr["conv_proj"].reshape(N_DEV, ds, d),
                gr["gla_proj"].reshape(N_DEV, ds, d), gr["w_out"].reshape(N_DEV, ds, d), gr["conv_w8"], d_up]
        lands = [lax.empty((4,) + a.shape[1:], a.dtype) for a in mine]
        comm["sibling"] = _copies_start("grad_sibling_start", mine, lands, _sibling_copies, 4 * len(mine))
        return (comm["sibling"][4],)

    def on_du_a1(du_a1):
        mine, theirs = _copies_wait("grad_sibling_wait", comm["sibling"], (du_a1,), _sibling_copies)
        sums = [_pair_sum("pair_sum_" + nm, a, b) for nm, a, b in zip(names, mine, theirs)]
        lands = [lax.empty(a.shape, a.dtype) for a in sums]
        comm["chips"] = _copies_start("grad_chips_start", sums, lands, _chip_copies, 3 * len(sums))
        return (comm["chips"][4],)

    g = _local_step(x, c, ctx, loss_target, c_ctx, g_ada, ada_b, norm_g[0:1], w_a, b_a, w_b, b_b,
                    conv_w8, conv_b, conv_ln_g, conv_ln_b, up2, bias2, gla_norm_g, final_norm_g.reshape(1, d),
                    proj, on_grads, on_du_a1)

    pack = _pack_small(g, x.shape[0], d, r)
    pack_lands = [lax.dynamic_update_slice(lax.empty((N_DEV,) + pack.shape, F32), pack[None], (me_i, 0, 0))]
    small_start = _copies_start("small_gather_start", [pack], pack_lands, _gather_copies, 7)

    (their_ada,) = _exchange_sibling([g["ada_w8"]])
    ada_sum = _pair_sum("pair_sum_ada_w", g["ada_w8"], their_ada)
    ada_start = _copies_start("ada_chips_start", [ada_sum], [lax.empty(ada_sum.shape, ada_sum.dtype)],
                              _chip_copies, 3)
    own, landed = _copies_wait("grad_chips_wait", comm["chips"], (ada_start[4],), _chip_copies)
    o_win, o_cp, o_gp, o_wo, o_cw, o_up = own
    x_win, x_cp, x_gp, x_wo, x_cw, x_up = landed

    out = {}

    def big(name, parts, wname, own=None):
        w2 = _as2d(wts[wname])
        res = _sum_adam(name, parts, w2, _as2d(env["m_" + wname]), _as2d(env["v_" + wname]), own)
        for pre, arr in zip(("grad_", "delta_", "new_m_", "new_v_"), res):
            out[pre + wname] = arr.reshape(wts[wname].shape)

    as_rows = lambda a: jnp.transpose(a, (2, 0, 1))
    res = _sum_adam("adam_w_in", x_win, as_rows(w_in), as_rows(m_w_in), as_rows(v_w_in), o_win)
    for pre, arr in zip(("grad_", "delta_", "new_m_", "new_v_"), res):
        out[pre + "w_in"] = jnp.transpose(arr, (1, 2, 0))
    big("adam_conv_proj", x_cp, "conv_proj", o_cp)
    big("adam_gla_proj", x_gp, "gla_proj", o_gp)
    big("adam_w_out", x_wo, "w_out", o_wo)
    big("adam_conv_w", x_cw, "conv_w", o_cw)
    big("adam_up_f", x_up[:, :, 0:dks], "decay_up_fwd", o_up[:, :, 0:dks])
    big("adam_up_b", x_up[:, :, dks:], "decay_up_bwd", o_up[:, :, dks:])

    _, (packs,) = _copies_wait("small_gather_wait", small_start, (res[0], out["grad_w_out"]), _gather_copies)
    row = lambda a: a.reshape(1, -1)
    sg, sd, sm, sv, loss = _small_adam(packs, [row(wts[k]) for k in _SMALL], [row(env["m_" + k]) for k in _SMALL],
                                       [row(env["v_" + k]) for k in _SMALL], d, r)
    for i, k in enumerate(_SMALL):
        for pre, arrs in (("grad_", sg), ("delta_", sd), ("new_m_", sm), ("new_v_", sv)):
            out[pre + k] = arrs[i].reshape(wts[k].shape)
    loss = loss.reshape(())

    (o_ada,), (x_ada,) = _copies_wait("ada_chips_wait", ada_start, (res[0], out["grad_w_out"], out["grad_b_in"]),
                                      _chip_copies)
    big("adam_ada_w", x_ada, "ada_w", o_ada)

    return (loss, g["grad_x"], *[out["grad_" + k] for k in _WEIGHTS], *[out["delta_" + k] for k in _WEIGHTS],
            *[out["new_m_" + k] for k in _WEIGHTS], *[out["new_v_" + k] for k in _WEIGHTS])
```

```python
import functools
import math

import jax
import jax.numpy as jnp
from jax import lax
from jax.experimental import pallas as pl
from jax.experimental.pallas import tpu as pltpu

F32 = jnp.float32
BF16 = jnp.bfloat16
MESH = pl.DeviceIdType.MESH

N_DEV = 8
GRID_W = 64
CHUNK = 128
HEADS = 4
EPS = 1e-6
GATE_TAU = 16.0
LANE = 128
ADAM_LR, ADAM_B1, ADAM_B2, ADAM_EPS, ADAM_WD, ADAM_STEP = 0.001, 0.9, 0.999, 1e-08, 0.01, 10
VMEM_LIMIT = 60 * 1024 * 1024
_ANY = pl.BlockSpec(memory_space=pl.ANY)


def _params(**kw):
    return pltpu.CompilerParams(vmem_limit_bytes=VMEM_LIMIT, **kw)


def _tile(n, pref):
    t = (min(pref, n) // LANE) * LANE
    while t >= LANE:
        if n % t == 0:
            return t
        t -= LANE
    return n


def _mm(a, b):
    return jnp.dot(a.astype(BF16), b.astype(BF16), preferred_element_type=F32)


def _mm_nt(a, b):
    return lax.dot_general(a.astype(BF16), b.astype(BF16), (((1,), (1,)), ((), ())), preferred_element_type=F32)


def _mm_tn(a, b):
    return lax.dot_general(a.astype(BF16), b.astype(BF16), (((0,), (0,)), ((), ())), preferred_element_type=F32)


def _sigmoid(x):
    return 0.5 * jnp.tanh(0.5 * x) + 0.5


def _dsilu(x, s):
    return s * (1.0 + x * (1.0 - s))


def _rowsel(table, idx, n):
    out = table[0:1, :]
    for r in range(1, n):
        out = jnp.where(idx == r, table[r:r + 1, :], out)
    return out


def _ada_fwd(cv, ada_w8, ada_b):
    n_sh, _, ws = ada_w8.shape

    def body(cv_ref, w_ref, b_ref, o_ref):
        c = cv_ref[...]
        sv = c * _sigmoid(c)
        for j in range(n_sh):
            cols = pl.ds(j * ws, ws)
            o_ref[:, cols] = _mm(sv, w_ref[j]) + b_ref[:, cols]

    return pl.pallas_call(body, name="ada_fwd", out_shape=jax.ShapeDtypeStruct((cv.shape[0], n_sh * ws), F32),
                          compiler_params=_params())(cv, ada_w8, ada_b)


def _ada_bwd(cv, ada_w8, dmod_ss, small, nb):
    n_sh, d, ws = ada_w8.shape

    def body(cv_ref, w_ref, dm_ref, sm_ref, dw_ref, db_ref, dc_ref):
        c = cv_ref[...]
        s = _sigmoid(c)
        sv = c * s
        dm = jnp.concatenate([dm_ref[:, 0:2 * d], sm_ref[8:16, :]], axis=1)
        db_ref[...] = jnp.sum(dm, axis=0, keepdims=True)
        sv_t = jnp.transpose(sv)
        dsv = None
        for j in range(n_sh):
            dmj = dm[:, j * ws:(j + 1) * ws]
            dw = sv_t[:, 0:1] * dmj[0:1, :]
            for row in range(1, nb + 1):
                dw = dw + sv_t[:, row:row + 1] * dmj[row:row + 1, :]
            dw_ref[j] = dw.astype(dw_ref.dtype)
            part = _mm_nt(dmj, w_ref[j])
            dsv = part if dsv is None else dsv + part
        dc_ref[...] = dsv * _dsilu(c, s)

    return pl.pallas_call(
        body, name="ada_bwd",
        out_shape=(jax.ShapeDtypeStruct((n_sh, d, ws), BF16), jax.ShapeDtypeStruct((1, n_sh * ws), F32),
                   jax.ShapeDtypeStruct(cv.shape, F32)),
        compiler_params=_params())(cv, ada_w8, dmod_ss, small)


class _Tiles:
    def __init__(self, nb, s_len, c_len, tm, big):
        self.nb, self.tm, self.big = nb, tm, big
        self.lat, self.ctx = s_len // tm, c_len // tm
        self.pad = -(self.lat + self.ctx) % big
        self.per_ex = self.lat + self.ctx + self.pad
        self.n_all = nb * self.per_ex
        self.rows_per_ex = self.per_ex * tm

    def is_lat(self, i):
        return i % self.per_ex < self.lat

    def is_pad(self, i):
        return i % self.per_ex >= self.lat + self.ctx

    def lat_of_all(self, i):
        return (i // self.per_ex) * self.lat + jnp.minimum(i % self.per_ex, self.lat - 1)

    def ctx_of_all(self, i):
        return (i // self.per_ex) * self.ctx + jnp.clip(i % self.per_ex - self.lat, 0, self.ctx - 1)


def _norm_fwd(x2, ctx2, mod, norm_g, tiles):
    tl, d = x2.shape
    tc = ctx2.shape[0]
    nb, tm = tiles.nb, tiles.tm

    def body(x_ref, c_ref, mod_ref, g_ref, u_ref):
        i = pl.program_id(0)
        lat = tiles.is_lat(i)
        xv = jnp.where(lat, x_ref[...], c_ref[...])
        row = jnp.where(lat, i // tiles.per_ex, nb)
        m = _rowsel(mod_ref[...], row, nb + 1)
        shift, scale = m[:, 0:d], m[:, d:2 * d]
        rstd = lax.rsqrt(jnp.mean(xv * xv, axis=-1, keepdims=True) + EPS)
        u = xv * rstd * g_ref[...] * (1.0 + scale) + shift
        u_ref[...] = jnp.where(tiles.is_pad(i), 0.0, u).astype(BF16)

    return pl.pallas_call(
        body, name="norm_fwd", grid=(tiles.n_all,),
        in_specs=[pl.BlockSpec((tm, d), lambda i: (tiles.lat_of_all(i), 0)),
                  pl.BlockSpec((tm, d), lambda i: (tiles.ctx_of_all(i), 0)),
                  pl.BlockSpec(mod.shape, lambda i: (0, 0)),
                  pl.BlockSpec((1, d), lambda i: (0, 0))],
        out_specs=pl.BlockSpec((tm, d), lambda i: (i, 0)),
        out_shape=jax.ShapeDtypeStruct((tiles.n_all * tm, d), BF16),
        compiler_params=_params())(x2, ctx2, mod, norm_g)


def _norm_bwd(x2, ctx2, mod, norm_g, du_lat, du_b, gx1, tiles):
    tl, d = x2.shape
    nb, tm = tiles.nb, tiles.tm
    nrow = mod.shape[0]
    n_lat_in = len(du_lat)

    def body(x_ref, c_ref, mod_ref, g_ref, *refs):
        dl_refs = refs[:n_lat_in]
        d3_ref, gx_ref, gxo_ref, dmod_ref, dg_ref = refs[n_lat_in:]
        i = pl.program_id(0)

        @pl.when(i == 0)
        def _():
            dmod_ref[...] = jnp.zeros_like(dmod_ref)
            dg_ref[...] = jnp.zeros_like(dg_ref)

        lat = tiles.is_lat(i)
        xv = jnp.where(lat, x_ref[...], c_ref[...])
        row = jnp.where(lat, i // tiles.per_ex, nb)
        m = _rowsel(mod_ref[...], row, nb + 1)
        scale = m[:, d:2 * d]
        g = g_ref[...]
        dl = dl_refs[0][...].astype(F32)
        for ref in dl_refs[1:]:
            dl = dl + ref[...].astype(F32)
        du = jnp.where(tiles.is_pad(i), 0.0, d3_ref[...].astype(F32) + jnp.where(lat, dl, 0.0))
        rstd = lax.rsqrt(jnp.mean(xv * xv, axis=-1, keepdims=True) + EPS)
        xh = xv * rstd
        dshift = jnp.sum(du, axis=0, keepdims=True)
        dscale = jnp.sum(du * xh * g, axis=0, keepdims=True)
        dxn = du * (1.0 + scale)
        dg_ref[...] += jnp.sum(dxn * xh, axis=0, keepdims=True)
        dxh = dxn * g
        dx = rstd * (dxh - xh * jnp.mean(dxh * xh, axis=-1, keepdims=True))

        @pl.when(lat)
        def _():
            gxo_ref[...] = dx + gx_ref[...]

        for r in range(nb + 1):
            dmod_ref[r:r + 1, 0:d] += jnp.where(row == r, dshift, 0.0)
            dmod_ref[r:r + 1, d:2 * d] += jnp.where(row == r, dscale, 0.0)

    lat_map = lambda i: (tiles.lat_of_all(i), 0)
    lat_spec = pl.BlockSpec((tm, d), lat_map)
    return pl.pallas_call(
        body, name="norm_bwd", grid=(tiles.n_all,),
        in_specs=[lat_spec,
                  pl.BlockSpec((tm, d), lambda i: (tiles.ctx_of_all(i), 0)),
                  pl.BlockSpec(mod.shape, lambda i: (0, 0)),
                  pl.BlockSpec((1, d), lambda i: (0, 0))]
                 + [lat_spec] * n_lat_in
                 + [pl.BlockSpec((tm, d), lambda i: (i, 0)), lat_spec],
        out_specs=(lat_spec,
                   pl.BlockSpec((nrow, 3 * d), lambda i: (0, 0)),
                   pl.BlockSpec((1, d), lambda i: (0, 0))),
        out_shape=(jax.ShapeDtypeStruct((tl, d), F32), jax.ShapeDtypeStruct((nrow, 3 * d), F32),
                   jax.ShapeDtypeStruct((1, d), F32)),
        compiler_params=_params())(x2, ctx2, mod, norm_g, *du_lat, du_b, gx1)


def _matmul_bias(name, u3, w, b, s_len, tm, tn):
    nb = u3.shape[0]
    d, n = w.shape
    per = s_len // tm
    rows = nb * s_len

    def body(u_ref, w_ref, b_ref, o_ref):
        o_ref[...] = jnp.dot(u_ref[...], w_ref[...], preferred_element_type=F32) + b_ref[...]

    return pl.pallas_call(
        body, name=name, grid=(n // tn, rows // tm),
        in_specs=[pl.BlockSpec((None, tm, d), lambda j, i: (i // per, i % per, 0)),
                  pl.BlockSpec((d, tn), lambda j, i: (0, j)),
                  pl.BlockSpec((1, tn), lambda j, i: (0, j))],
        out_specs=pl.BlockSpec((tm, tn), lambda j, i: (i, j)),
        out_shape=jax.ShapeDtypeStruct((rows, n), F32),
        compiler_params=_params())(u3, w, b)


def _log_sigmoid(x):
    return jnp.minimum(x, 0.0) - jnp.log(1.0 + jnp.exp(-jnp.abs(x)))


def _inproj_b(u, w_b, b_b, up2, bias2, tm, dk_, dv_):
    t_all, d = u.shape
    nbw = w_b.shape[1]
    n2 = up2.shape[1]

    def body(u_ref, w_ref, b_ref, up_ref, bias_ref, qk_ref, v_ref, g_ref):
        full = jnp.dot(u_ref[...], w_ref[...], preferred_element_type=F32) + b_ref[...]
        lr = full[:, 2 * dk_ + dv_:nbw]
        qk_ref[:, 0:2 * dk_] = full[:, 0:2 * dk_]
        qk_ref[:, 2 * dk_:2 * dk_ + LANE] = lr
        v_ref[...] = full[:, 2 * dk_:2 * dk_ + dv_].astype(BF16)
        g_ref[...] = _log_sigmoid(_mm(lr, up_ref[...]) + bias_ref[...]) * (1.0 / GATE_TAU)

    whole = lambda a: pl.BlockSpec(a.shape, lambda i: (0, 0))
    return pl.pallas_call(
        body, name="inproj_b", grid=(t_all // tm,),
        in_specs=[pl.BlockSpec((tm, d), lambda i: (i, 0)), whole(w_b), whole(b_b), whole(up2), whole(bias2)],
        out_specs=(pl.BlockSpec((tm, 2 * dk_ + LANE), lambda i: (i, 0)), pl.BlockSpec((tm, dv_), lambda i: (i, 0)),
                   pl.BlockSpec((tm, n2), lambda i: (i, 0))),
        out_shape=(jax.ShapeDtypeStruct((t_all, 2 * dk_ + LANE), F32), jax.ShapeDtypeStruct((t_all, dv_), BF16),
                   jax.ShapeDtypeStruct((t_all, n2), F32)),
        compiler_params=_params())(u, w_b, b_b, up2, bias2)


def _matmul_nt(name, a, w, koff, tm, tk, after=()):
    r, kc = a.shape
    d = w.shape[0]
    nk = kc // tk

    def body(a_ref, w_ref, *rest):
        o_ref = rest[len(after)]
        k = pl.program_id(1)
        p = lax.dot_general(a_ref[...], w_ref[...], (((1,), (1,)), ((), ())), preferred_element_type=F32)
        if nk == 1:
            o_ref[...] = p.astype(o_ref.dtype)
            return
        acc_ref = rest[len(after) + 1]

        @pl.when(k == 0)
        def _():
            acc_ref[...] = p

        @pl.when(k > 0)
        def _():
            acc_ref[...] += p

        @pl.when(k == nk - 1)
        def _():
            o_ref[...] = acc_ref[...].astype(o_ref.dtype)

    return pl.pallas_call(
        body, name=name, grid=(r // tm, nk),
        in_specs=[pl.BlockSpec((tm, tk), lambda i, k: (i, k)),
                  pl.BlockSpec((d, tk), lambda i, k: (0, koff + k))] + [_ANY] * len(after),
        out_specs=pl.BlockSpec((tm, d), lambda i, k: (i, 0)),
        out_shape=jax.ShapeDtypeStruct((r, d), BF16),
        scratch_shapes=[pltpu.VMEM((tm, d), F32)] if nk > 1 else [],
        compiler_params=_params())(a, w, *after)


def _matmul_tn(name, a, b, rows, tk, tn):
    m = a.shape[1]
    n = b.shape[1]
    nk = rows // tk

    def body(a_ref, b_ref, o_ref, s_ref, acc_ref):
        k = pl.program_id(1)
        bv = b_ref[...]
        p = lax.dot_general(bv, a_ref[...], (((0,), (0,)), ((), ())), preferred_element_type=F32)
        cs = jnp.sum(bv.astype(F32), axis=0, keepdims=True)

        @pl.when(k == 0)
        def _():
            acc_ref[...] = p
            s_ref[...] = cs

        @pl.when(k > 0)
        def _():
            acc_ref[...] += p
            s_ref[...] += cs

        @pl.when(k == nk - 1)
        def _():
            o_ref[...] = acc_ref[...].astype(o_ref.dtype)

    return pl.pallas_call(
        body, name=name, grid=(n // tn, nk),
        in_specs=[pl.BlockSpec((tk, m), lambda j, k: (k, 0)),
                  pl.BlockSpec((tk, tn), lambda j, k: (k, j))],
        out_specs=(pl.BlockSpec((tn, m), lambda j, k: (j, 0)), pl.BlockSpec((1, tn), lambda j, k: (0, j))),
        out_shape=(jax.ShapeDtypeStruct((n, m), BF16), jax.ShapeDtypeStruct((1, n), F32)),
        scratch_shapes=[pltpu.VMEM((tn, m), F32)],
        compiler_params=_params())(a, b)


def _matmul_tn_whole(name, a3, b3, rows, tn, transposed):
    nb, _, m = a3.shape
    n = b3.shape[2]

    def body(a_ref, b_ref, o_ref, s_ref):
        p, cs = None, None
        for e in range(nb):
            bv = b_ref[e]
            lhs, rhs = (bv, a_ref[e]) if transposed else (a_ref[e], bv)
            pe = lax.dot_general(lhs, rhs, (((0,), (0,)), ((), ())), preferred_element_type=F32)
            ce = jnp.sum(bv.astype(F32), axis=0, keepdims=True)
            p, cs = (pe, ce) if p is None else (p + pe, cs + ce)
        o_ref[...] = p.astype(o_ref.dtype)
        s_ref[...] = cs

    o_spec, o_shape = ((pl.BlockSpec((tn, m), lambda j: (j, 0)), (n, m)) if transposed
                       else (pl.BlockSpec((m, tn), lambda j: (0, j)), (m, n)))
    return pl.pallas_call(
        body, name=name, grid=(n // tn,),
        in_specs=[pl.BlockSpec((nb, rows, m), lambda j: (0, 0, 0)),
                  pl.BlockSpec((nb, rows, tn), lambda j: (0, 0, j))],
        out_specs=(o_spec, pl.BlockSpec((1, tn), lambda j: (0, j))),
        out_shape=(jax.ShapeDtypeStruct(o_shape, BF16), jax.ShapeDtypeStruct((1, n), F32)),
        compiler_params=_params())(a3, b3)


def _conv_window(pad_ref, r, shift, ktaps, width, horizontal):
    if horizontal:
        return pad_ref[r, pl.ds(16 + shift, width), :]
    return pad_ref[r + ktaps // 2 + shift]


def _conv_row(pad_ref, w, r, ktaps, width, horizontal, flip):
    half = ktaps // 2
    acc = None
    for t in range(ktaps):
        win = _conv_window(pad_ref, r, (half - t) if flip else (t - half), ktaps, width, horizontal)
        term = win * w[t:t + 1, :]
        acc = term if acc is None else acc + term
    return acc


def _fill_padded(ref, val, rows, width, ktaps, horizontal):
    half_k = ktaps // 2
    cb = val.shape[-1]
    if horizontal:
        ref[:, 0:16, :] = jnp.zeros((rows, 16, cb), F32)
        ref[:, 16 + width:32 + width, :] = jnp.zeros((rows, 16, cb), F32)
        ref[:, 16:16 + width, :] = val
    else:
        ref[0:half_k, :, :] = jnp.zeros((half_k, width, cb), F32)
        ref[half_k + rows:2 * half_k + rows, :, :] = jnp.zeros((half_k, width, cb), F32)
        ref[half_k:half_k + rows, :, :] = val


def _conv_fwd(pa, conv_w8, conv_b, nb, s):
    nblk, ktaps, cb = conv_w8.shape
    d = nblk * cb
    rows, width = s // GRID_W, GRID_W
    half_k = ktaps // 2
    nh = nblk // 2

    def body(glu_ref, w_ref, b_ref, o_ref, ph_ref, pv_ref):
        j = pl.program_id(1)
        a0 = (glu_ref[:, 0:cb] * _sigmoid(glu_ref[:, cb:2 * cb])).reshape(rows, width, cb)
        w = w_ref[...]

        bias = b_ref[...]

        def run(pad_ref, horizontal):
            _fill_padded(pad_ref, a0, rows, width, ktaps, horizontal)

            def row(r, carry):
                at = pl.ds(pl.multiple_of(r * width, width), width)
                o_ref[at, :] = _conv_row(pad_ref, w, r, ktaps, width, horizontal, False) + bias
                return carry

            lax.fori_loop(0, rows, row, 0)

        @pl.when(j < nh)
        def _():
            run(ph_ref, True)

        @pl.when(j >= nh)
        def _():
            run(pv_ref, False)

    return pl.pallas_call(
        body, name="conv_fwd", grid=(nb, nblk),
        in_specs=[pl.BlockSpec((s, 2 * cb), lambda b, j: (b, j)),
                  pl.BlockSpec((None, ktaps, cb), lambda b, j: (j, 0, 0)),
                  pl.BlockSpec((1, cb), lambda b, j: (0, j))],
        out_specs=pl.BlockSpec((s, cb), lambda b, j: (b, j)),
        out_shape=jax.ShapeDtypeStruct((nb * s, d), F32),
        scratch_shapes=[pltpu.VMEM((rows, width + 32, cb), F32), pltpu.VMEM((rows + 2 * half_k, width, cb), F32)],
        compiler_params=_params())(pa, conv_w8, conv_b)


def _conv_bwd(pa, da1, conv_w8, nb, s):
    nblk, ktaps, cb = conv_w8.shape
    d = nblk * cb
    rows, width = s // GRID_W, GRID_W
    half_k = ktaps // 2
    nh = nblk // 2

    def body(glu_ref, da_ref, w_ref, dp_ref, dw_ref, db_ref, pha_ref, phd_ref, pva_ref, pvd_ref):
        j = pl.program_id(0)
        b = pl.program_id(1)
        a0 = (glu_ref[:, 0:cb] * _sigmoid(glu_ref[:, cb:2 * cb])).reshape(rows, width, cb)
        da1v = da_ref[...]
        d3 = da1v.reshape(rows, width, cb)
        w = w_ref[...]

        @pl.when(b == 0)
        def _():
            dw_ref[...] = jnp.zeros_like(dw_ref)
            db_ref[...] = jnp.zeros_like(db_ref)

        db_ref[...] += jnp.sum(da1v, axis=0, keepdims=True)

        def run(pa_ref, pd_ref, horizontal):
            _fill_padded(pa_ref, a0, rows, width, ktaps, horizontal)
            _fill_padded(pd_ref, d3, rows, width, ktaps, horizontal)

            def row(r, accs):
                at = pl.ds(pl.multiple_of(r * width, width), width)
                da0 = _conv_row(pd_ref, w, r, ktaps, width, horizontal, True)
                gv = glu_ref[at, 0:cb]
                sg = _sigmoid(glu_ref[at, cb:2 * cb])
                dp_ref[at, 0:cb] = (da0 * sg).astype(BF16)
                dp_ref[at, cb:2 * cb] = (da0 * gv * sg * (1.0 - sg)).astype(BF16)
                d_row = da_ref[at, :]
                out = []
                for t in range(ktaps):
                    prod = _conv_window(pa_ref, r, t - half_k, ktaps, width, horizontal) * d_row
                    out.append(accs[t] + jnp.sum(prod.reshape(width // 8, 8, cb), axis=0))
                return tuple(out)

            accs = lax.fori_loop(0, rows, row, tuple(jnp.zeros((8, cb), F32) for _ in range(ktaps)))
            for t in range(ktaps):
                dw_ref[t:t + 1, :] += jnp.sum(accs[t], axis=0, keepdims=True)

        @pl.when(j < nh)
        def _():
            run(pha_ref, phd_ref, True)

        @pl.when(j >= nh)
        def _():
            run(pva_ref, pvd_ref, False)

    return pl.pallas_call(
        body, name="conv_bwd", grid=(nblk, nb),
        in_specs=[pl.BlockSpec((s, 2 * cb), lambda j, b: (b, j)),
                  pl.BlockSpec((s, cb), lambda j, b: (b, j)),
                  pl.BlockSpec((None, ktaps, cb), lambda j, b: (j, 0, 0))],
        out_specs=(pl.BlockSpec((s, 2 * cb), lambda j, b: (b, j)),
                   pl.BlockSpec((None, ktaps, cb), lambda j, b: (j, 0, 0)),
                   pl.BlockSpec((1, cb), lambda j, b: (0, j))),
        out_shape=(jax.ShapeDtypeStruct((nb * s, 2 * d), BF16),
                   jax.ShapeDtypeStruct((nblk, ktaps, cb), F32), jax.ShapeDtypeStruct((1, d), F32)),
        scratch_shapes=[pltpu.VMEM((rows, width + 32, cb), F32), pltpu.VMEM((rows, width + 32, cb), F32),
                        pltpu.VMEM((rows + 2 * half_k, width, cb), F32),
                        pltpu.VMEM((rows + 2 * half_k, width, cb), F32)],
        compiler_params=_params())(pa, da1, conv_w8)


def _decay_bwd(pb, up2, bias2, grads_f, grads_b, tiles, lr_blk, dk_, dv_):
    t_all = pb.shape[0]
    tm = tiles.tm
    n2 = up2.shape[1]
    nbw = 2 * dk_ + dv_ + LANE

    def body(lr_ref, up_ref, b_ref, dqf, dkf, dvf, dgf, dqb, dkb, dvb, dgb, dp_ref, dup_ref, dbias_ref):
        i = pl.program_id(0)
        pad = tiles.is_pad(i)
        live = lambda v: jnp.where(pad, 0.0, v)

        @pl.when(i == 0)
        def _():
            dup_ref[...] = jnp.zeros_like(dup_ref)
            dbias_ref[...] = jnp.zeros_like(dbias_ref)

        lr = lr_ref[...]
        up = up_ref[...]
        logits = _mm(lr, up) + b_ref[...]
        dg = live(jnp.concatenate([dgf[...], dgb[...]], axis=1))
        dlog = dg * (1.0 / GATE_TAU) * _sigmoid(-logits)
        dup_ref[...] += _mm_tn(lr, dlog)
        dbias_ref[...] += jnp.sum(dlog, axis=0, keepdims=True)
        both = lambda f, b: live(f[...].astype(F32) + b[...].astype(F32)).astype(BF16)
        dp_ref[:, 0:dk_] = both(dqf, dqb)
        dp_ref[:, dk_:2 * dk_] = both(dkf, dkb)
        dp_ref[:, 2 * dk_:2 * dk_ + dv_] = both(dvf, dvb)
        dp_ref[:, 2 * dk_ + dv_:nbw] = _mm_nt(dlog, up).astype(BF16)

    row = lambda w: pl.BlockSpec((tm, w), lambda i: (i, 0))
    return pl.pallas_call(
        body, name="decay_bwd", grid=(t_all // tm,),
        in_specs=[pl.BlockSpec((tm, LANE), lambda i: (i, lr_blk)),
                  pl.BlockSpec(up2.shape, lambda i: (0, 0)),
                  pl.BlockSpec((1, n2), lambda i: (0, 0)),
                  row(dk_), row(dk_), row(dv_), row(dk_), row(dk_), row(dk_), row(dv_), row(dk_)],
        out_specs=(row(nbw), pl.BlockSpec(up2.shape, lambda i: (0, 0)), pl.BlockSpec((1, n2), lambda i: (0, 0))),
        out_shape=(jax.ShapeDtypeStruct((t_all, nbw), BF16), jax.ShapeDtypeStruct(up2.shape, F32),
                   jax.ShapeDtypeStruct((1, n2), F32)),
        compiler_params=_params())(pb, up2, bias2, *grads_f, *grads_b)


def _scan_chunk(s, nl, nc, rev):
    if rev:
        return jnp.where(s < nc, nl + (nc - 1 - s), nl - 1 - (s - nc))
    return jnp.where(s < nc, nl + s, s - nc)


def _scan_lat_chunk(s, nl, nc, rev):
    first = nl - 1 if rev else 0
    return jnp.where(s < nc, first, _scan_chunk(s, nl, nc, rev))


def _tri_mm(m_bf, x):
    hi = x.astype(BF16)
    r1 = x - hi.astype(F32)
    mid = r1.astype(BF16)
    lo = (r1 - mid.astype(F32)).astype(BF16)
    dot = lambda p: jnp.dot(m_bf, p, preferred_element_type=F32)
    return dot(hi) + dot(mid) + dot(lo)


def _chunk_masks(c, rev):
    ii = lax.broadcasted_iota(jnp.int32, (c, c), 0)
    jj = lax.broadcasted_iota(jnp.int32, (c, c), 1)
    return ((ii <= jj), (ii >= jj)) if rev else ((ii >= jj), (ii <= jj))


def _chunk_terms(q, k, b, far, mid):
    bf, bm = b[far:far + 1, :], b[mid:mid + 1, :]
    e = jnp.exp(b)
    em = jnp.exp(b - bm)
    eim = jnp.exp(bm - b)
    ed = jnp.exp(bf - b)
    return dict(e=e, em=em, eim=eim, ed=ed, dec=jnp.exp(bf), qe=q * e, qem=q * em, kim=k * eim, kd=k * ed)


def _gla_fwd(pb3, pv3, g3, nb, s_len, c_len, dk_, dv_):
    c = CHUNK
    nl, nc = s_len // c, c_len // c
    ns = nl + nc
    hk, hv = dk_ // HEADS, dv_ // HEADS
    l_len = pb3.shape[1]
    scale = hk ** -0.5
    mid = c // 2

    def body(*refs):
        ins, outs, z_scr = refs[:8], refs[8:14], refs[14]
        s = pl.program_id(0)

        @pl.when(s == 0)
        def _():
            z_scr[...] = jnp.zeros_like(z_scr)

        qs = jnp.where(s >= nc, scale, 0.0)
        for di, rev in enumerate((False, True)):
            q_ref, k_ref, v_ref, g_ref = ins[4 * di:4 * di + 4]
            o_ref, zs_ref, b_ref = outs[3 * di:3 * di + 3]
            mask, _ = _chunk_masks(c, rev)
            m_bf = mask.astype(BF16)
            far = 0 if rev else c - 1
            for b in range(nb):
                bc = _tri_mm(m_bf, g_ref[b])
                b_ref[b] = bc
                for h in range(HEADS):
                    ks, vs = slice(h * hk, (h + 1) * hk), slice(h * hv, (h + 1) * hv)
                    zi = (di * nb + b) * HEADS + h
                    v = v_ref[b, :, vs]
                    t = _chunk_terms(q_ref[b, :, ks] * qs, k_ref[b, :, ks], bc[:, ks], far, mid)
                    a = jnp.where(mask, _mm_nt(t["qem"], t["kim"]), 0.0)
                    z = z_scr[zi]
                    zs_ref[0, b * HEADS + h] = z
                    o_ref[b, :, vs] = _mm(a, v) + _mm_nt(t["qe"], z)
                    z_scr[zi] = z * t["dec"] + _mm_tn(v, t["kd"])

    in_specs, out_specs, out_shape = [], [], []
    for di, rev in enumerate((False, True)):
        ch = functools.partial(_scan_chunk, nl=nl, nc=nc, rev=rev)
        lch = functools.partial(_scan_lat_chunk, nl=nl, nc=nc, rev=rev)
        in_specs += [pl.BlockSpec((nb, c, dk_), lambda s, ch=ch: (0, ch(s), 0)),
                     pl.BlockSpec((nb, c, dk_), lambda s, ch=ch: (0, ch(s), 1)),
                     pl.BlockSpec((nb, c, dv_), lambda s, ch=ch: (0, ch(s), 0)),
                     pl.BlockSpec((nb, c, dk_), lambda s, ch=ch, di=di: (0, ch(s), di))]
        out_specs += [pl.BlockSpec((nb, c, dv_), lambda s, lch=lch: (0, lch(s), 0)),
                      pl.BlockSpec((1, nb * HEADS, hv, hk), lambda s: (s, 0, 0, 0)),
                      pl.BlockSpec((nb, c, dk_), lambda s, ch=ch: (0, ch(s), 0))]
        out_shape += [jax.ShapeDtypeStruct((nb, s_len, dv_), F32),
                      jax.ShapeDtypeStruct((ns, nb * HEADS, hv, hk), F32),
                      jax.ShapeDtypeStruct((nb, l_len, dk_), F32)]
    return pl.pallas_call(
        body, name="gla_fwd", grid=(ns,), in_specs=in_specs, out_specs=tuple(out_specs), out_shape=tuple(out_shape),
        scratch_shapes=[pltpu.VMEM((2 * nb * HEADS, hv, hk), F32)],
        compiler_params=_params())(pb3, pb3, pv3, g3, pb3, pb3, pv3, g3)


def _gla_bwd(pb3, pv3, do3, fwd_saved, nb, s_len, c_len, dk_, dv_):
    c = CHUNK
    nl, nc = s_len // c, c_len // c
    ns = nl + nc
    hk, hv = dk_ // HEADS, dv_ // HEADS
    l_len = pb3.shape[1]
    scale = hk ** -0.5
    mid = c // 2
    zs_f, b_f, zs_b, b_b = fwd_saved

    def body(*refs):
        ins, outs, dz_scr = refs[:12], refs[12:20], refs[20]
        s = pl.program_id(0)
        step = ns - 1 - s

        @pl.when(s == 0)
        def _():
            dz_scr[...] = jnp.zeros_like(dz_scr)

        lat = step >= nc
        qs = jnp.where(lat, scale, 0.0)
        dmul = jnp.where(lat, 1.0, 0.0)
        for di, rev in enumerate((False, True)):
            q_ref, k_ref, v_ref, b_ref, do_ref, zs_ref = ins[6 * di:6 * di + 6]
            dq_ref, dk_ref, dv_ref, dg_ref = outs[4 * di:4 * di + 4]
            mask, mask_t = _chunk_masks(c, rev)
            mt_bf = mask_t.astype(BF16)
            far = 0 if rev else c - 1
            far_row = lax.broadcasted_iota(jnp.int32, (c, hk), 0) == far
            for b in range(nb):
                db_parts = []
                for h in range(HEADS):
                    ks, vs = slice(h * hk, (h + 1) * hk), slice(h * hv, (h + 1) * hv)
                    zi = (di * nb + b) * HEADS + h
                    v = v_ref[b, :, vs]
                    d_o = do_ref[b, :, vs] * dmul
                    t = _chunk_terms(q_ref[b, :, ks] * qs, k_ref[b, :, ks], b_ref[b, :, ks], far, mid)
                    qem, kim, qe, kd = t["qem"], t["kim"], t["qe"], t["kd"]
                    a_t = jnp.where(mask_t, _mm_nt(kim, qem), 0.0)
                    d_a = jnp.where(mask, _mm_nt(d_o, v), 0.0)
                    d_at = jnp.where(mask_t, _mm_nt(v, d_o), 0.0)
                    z = zs_ref[0, b * HEADS + h]
                    dzn = dz_scr[zi]
                    dv_ref[b, :, vs] = (_mm(a_t, d_o) + _mm_nt(kd, dzn)).astype(dv_ref.dtype)
                    dqem = _mm(d_a, kim)
                    dkim = _mm(d_at, qem)
                    dqe = _mm(d_o, z)
                    dkd = _mm(v, dzn)
                    ddec = jnp.sum(z * dzn, axis=0, keepdims=True)
                    dz_scr[zi] = dzn * t["dec"] + _mm_tn(d_o, qe)
                    dq_ref[b, :, ks] = ((dqem * t["em"] + dqe * t["e"]) * qs).astype(dq_ref.dtype)
                    dk_ref[b, :, ks] = (dkim * t["eim"] + dkd * t["ed"]).astype(dk_ref.dtype)
                    db = dqem * qem - dkim * kim + dqe * qe - dkd * kd
                    extra = jnp.sum(dkd * kd, axis=0, keepdims=True) + ddec * t["dec"]
                    db_parts.append(db + jnp.where(far_row, extra, 0.0))
                dg_ref[b] = _tri_mm(mt_bf, jnp.concatenate(db_parts, axis=1))

    in_specs, out_specs, out_shape, args = [], [], [], []
    for di, rev in enumerate((False, True)):
        ch = lambda s, rev=rev: _scan_chunk(ns - 1 - s, nl, nc, rev)
        lch = lambda s, rev=rev: _scan_lat_chunk(ns - 1 - s, nl, nc, rev)
        in_specs += [pl.BlockSpec((nb, c, dk_), lambda s, ch=ch: (0, ch(s), 0)),
                     pl.BlockSpec((nb, c, dk_), lambda s, ch=ch: (0, ch(s), 1)),
                     pl.BlockSpec((nb, c, dv_), lambda s, ch=ch: (0, ch(s), 0)),
                     pl.BlockSpec((nb, c, dk_), lambda s, ch=ch: (0, ch(s), 0)),
                     pl.BlockSpec((nb, c, dv_), lambda s, lch=lch: (0, lch(s), 0)),
                     pl.BlockSpec((1, nb * HEADS, hv, hk), lambda s: (ns - 1 - s, 0, 0, 0))]
        args += [pb3, pb3, pv3, (b_b if rev else b_f), do3, (zs_b if rev else zs_f)]
        for w, dt in ((dk_, BF16), (dk_, BF16), (dv_, BF16), (dk_, F32)):
            out_specs.append(pl.BlockSpec((nb, c, w), lambda s, ch=ch: (0, ch(s), 0)))
            out_shape.append(jax.ShapeDtypeStruct((nb, l_len, w), dt))
    return pl.pallas_call(
        body, name="gla_bwd", grid=(ns,), in_specs=in_specs, out_specs=tuple(out_specs), out_shape=tuple(out_shape),
        scratch_shapes=[pltpu.VMEM((2 * nb * HEADS, hv, hk), F32)],
        compiler_params=_params())(*args)


def _tail(a1, pa, o_f, o_b, x2, tgt, mod, wc, wg, wo, ln_g, ln_b, gn_t, fg, nb, tm, n_split):
    tl, d = x2.shape
    nt = tl // tm
    per_ex = nt // nb
    hv = d // HEADS
    nrow = mod.shape[0]

    def part(shared, a1_ref, z_ref, r_ref, mc_ref, mg_ref, of_ref, ob_ref, x_ref, t_ref,
             dp_ref, da1_ref, do_ref, gx_ref, mrg_ref, dmo_ref, yci_ref, dyc_ref, ogi_ref, dyg_ref, sm_ref):
        bidx, gate, lng, lnb, fgv, gn, wc_, wg_, wo_ = shared

        a1v = a1_ref[...]
        mu = jnp.mean(a1v, axis=-1, keepdims=True)
        xc = a1v - mu
        rs = lax.rsqrt(jnp.mean(xc * xc, axis=-1, keepdims=True) + EPS)
        xh = xc * rs
        a2 = xh * lng + lnb
        s2 = _sigmoid(a2)
        a3 = a2 * s2
        zv = z_ref[...]
        sz = _sigmoid(zv)
        siluz = zv * sz
        ycin = a3 * siluz
        yconv = _mm(ycin, wc_)

        o = of_ref[...] + ob_ref[...]
        ohat_parts, rn_parts = [], []
        for h in range(HEADS):
            oh = o[:, h * hv:(h + 1) * hv]
            rn = lax.rsqrt(jnp.mean(oh * oh, axis=-1, keepdims=True) + EPS)
            ohat_parts.append(oh * rn)
            rn_parts.append(rn)
        ohat = jnp.concatenate(ohat_parts, axis=1)
        on = ohat * gn
        rv = r_ref[...]
        sr = _sigmoid(rv)
        silur = rv * sr
        ogin = on * silur
        ygla = _mm(ogin, wg_)

        sc = _sigmoid(mc_ref[...])
        sg = _sigmoid(mg_ref[...])
        merged = sc * yconv + sg * ygla
        mo = _mm(merged, wo_)
        hn = x_ref[...] + gate * mo
        rf = lax.rsqrt(jnp.mean(hn * hn, axis=-1, keepdims=True) + EPS)
        yh = hn * rf
        err = yh * fgv - t_ref[...]
        loss_part = 0.5 * jnp.sum(err * err) * (1.0 / d)

        dy = err * (1.0 / d)
        dfg = jnp.sum(dy * yh, axis=0, keepdims=True)
        dyh = dy * fgv
        dhn = rf * (dyh - yh * jnp.mean(dyh * yh, axis=-1, keepdims=True))
        gx_ref[...] = dhn
        dgate = jnp.sum(dhn * mo, axis=0, keepdims=True)
        dmo = gate * dhn
        dmerged = _mm_nt(dmo, wo_)
        dyconv = dmerged * sc
        dygla = dmerged * sg
        dp_ref[:, 2 * d:3 * d] = (dmerged * yconv * sc * (1.0 - sc)).astype(BF16)
        dp_ref[:, 3 * d:4 * d] = (dmerged * ygla * sg * (1.0 - sg)).astype(BF16)
        dycin = _mm_nt(dyconv, wc_)
        dogin = _mm_nt(dygla, wg_)
        mrg_ref[...] = merged.astype(BF16)
        dmo_ref[...] = dmo.astype(BF16)
        yci_ref[...] = ycin.astype(BF16)
        dyc_ref[...] = dyconv.astype(BF16)
        ogi_ref[...] = ogin.astype(BF16)
        dyg_ref[...] = dygla.astype(BF16)

        da3 = dycin * siluz
        dp_ref[:, 0:d] = (dycin * a3 * _dsilu(zv, sz)).astype(BF16)
        da2 = da3 * _dsilu(a2, s2)
        dlng = jnp.sum(da2 * xh, axis=0, keepdims=True)
        dlnb = jnp.sum(da2, axis=0, keepdims=True)
        dxh = da2 * lng
        da1_ref[...] = rs * (dxh - jnp.mean(dxh, axis=-1, keepdims=True)
                             - xh * jnp.mean(dxh * xh, axis=-1, keepdims=True))

        don = dogin * silur
        dp_ref[:, d:2 * d] = (dogin * on * _dsilu(rv, sr)).astype(BF16)
        dgn = jnp.sum(don * ohat, axis=0, keepdims=True)
        dyn = don * gn
        for h in range(HEADS):
            vs = slice(h * hv, (h + 1) * hv)
            oh_hat = ohat_parts[h]
            dh = dyn[:, vs]
            do_ref[:, vs] = (rn_parts[h] * (dh - oh_hat * jnp.mean(dh * oh_hat, axis=-1, keepdims=True))
                             ).astype(BF16)

        sm_ref[0:1, :] += dfg
        sm_ref[1:2, :] += dlng
        sm_ref[2:3, :] += dlnb
        sm_ref[3:4, :] += dgn
        sm_ref[4:5, :] += jnp.zeros((1, d), F32) + loss_part
        for b in range(nb):
            sm_ref[8 + b:9 + b, :] += jnp.where(bidx == b, dgate, 0.0)

    def body(*refs):
        mod_ref, wc_ref, wg_ref, wo_ref, lng_ref, lnb_ref, gn_ref, fg_ref = refs[9:17]
        sm_ref = refs[27]
        i = pl.program_id(0)

        @pl.when(i == 0)
        def _():
            sm_ref[...] = jnp.zeros_like(sm_ref)

        bidx = i // per_ex
        shared = (bidx, _rowsel(mod_ref[...], bidx, nb)[:, 2 * d:3 * d], lng_ref[...], lnb_ref[...], fg_ref[...],
                  jnp.concatenate([gn_ref[...]] * HEADS, axis=1), wc_ref[...], wg_ref[...], wo_ref[...])
        rows_per = tm // n_split
        for p in range(n_split):
            rows = pl.ds(p * rows_per, rows_per)
            part(shared, *[r.at[rows] for r in refs[0:9]], *[r.at[rows] for r in refs[17:27]], sm_ref)

    row = pl.BlockSpec((tm, d), lambda i: (i, 0))
    pcol = lambda blk: pl.BlockSpec((tm, d), lambda i: (i, blk))
    full = lambda arr: pl.BlockSpec(arr.shape, lambda i: (0,) * arr.ndim)
    bfo = jax.ShapeDtypeStruct((tl, d), BF16)
    f32o = jax.ShapeDtypeStruct((tl, d), F32)
    return pl.pallas_call(
        body, name="tail", grid=(nt,),
        in_specs=[row, pcol(2), pcol(3), pcol(4), pcol(5), row, row, row, row, full(mod), full(wc), full(wg),
                  full(wo), full(ln_g), full(ln_b), full(gn_t), full(fg)],
        out_specs=(pl.BlockSpec((tm, 4 * d), lambda i: (i, 0)), row, row, row, row, row, row, row, row, row,
                   pl.BlockSpec((16, d), lambda i: (0, 0))),
        out_shape=(jax.ShapeDtypeStruct((tl, 4 * d), BF16), f32o, bfo, f32o, bfo, bfo, bfo, bfo, bfo, bfo,
                   jax.ShapeDtypeStruct((16, d), F32)),
        compiler_params=_params())(a1, pa, pa, pa, pa, o_f, o_b, x2, tgt, mod, wc, wg, wo, ln_g, ln_b, gn_t, fg)


def _local_step(x, c, ctx, tgt, c_ctx, ada_w8, ada_b, norm_g, w_a, b_a, w_b, b_b, conv_w8, conv_b, ln_g, ln_b,
                up2, bias2, gla_norm_g, final_norm_g, proj, on_grads=None, on_du_a1=None):
    nb, s_len, d = x.shape
    c_len = ctx.shape[1]
    dk_, dv_ = d // 2, d
    tl, tc = nb * s_len, nb * c_len
    nbw = 2 * dk_ + dv_ + LANE
    tm = math.gcd(256, c_len)
    tiles = _Tiles(nb, s_len, c_len, tm, 2)
    l_len = tiles.rows_per_ex
    t_all = nb * l_len
    x2, ctx2, tgt2 = x.reshape(tl, d), ctx.reshape(tc, d), tgt.reshape(tl, d)

    cv = jnp.zeros((8, d), F32).at[0:nb].set(c).at[nb].set(c_ctx.reshape(d))
    mod = _ada_fwd(cv, ada_w8, ada_b)
    u = _norm_fwd(x2, ctx2, mod, norm_g, tiles)
    u3 = u.reshape(nb, l_len, d)
    tma = math.gcd(1024, s_len)
    pa = _matmul_bias("inproj_a", u3, w_a, b_a, s_len, tma, _tile(6 * d, 2048))
    tmb = math.gcd(1024, t_all)
    pb, pv, g_all = _inproj_b(u, w_b, b_b, up2, bias2, tmb, dk_, dv_)

    a1 = _conv_fwd(pa, conv_w8, conv_b, nb, s_len)
    lr_blk = (2 * dk_) // LANE
    pb3, pv3 = pb.reshape(nb, l_len, 2 * dk_ + LANE), pv.reshape(nb, l_len, dv_)
    o_f, zs_f, b_f, o_b, zs_b, b_b2 = _gla_fwd(pb3, pv3, g_all.reshape(nb, l_len, 2 * dk_), nb, s_len, c_len,
                                               dk_, dv_)

    conv_proj, gla_proj, w_out = proj(a1) if callable(proj) else proj
    tt = math.gcd(256, s_len)
    (dp_a2, da1, d_o, gx1, merged, dmo, ycin, dyconv, ogin, dygla, small) = _tail(
        a1, pa, o_f.reshape(tl, dv_), o_b.reshape(tl, dv_), x2, tgt2, mod, conv_proj, gla_proj, w_out, ln_g, ln_b,
        gla_norm_g, final_norm_g, nb, tt, 2)

    lat3 = lambda a: a.reshape(nb, s_len, a.shape[-1])
    tnw = _tile(d, 512)
    d_w_out, _ = _matmul_tn_whole("dw_out", lat3(merged), lat3(dmo), s_len, tnw, False)
    d_conv_proj, _ = _matmul_tn_whole("dw_conv_proj", lat3(ycin), lat3(dyconv), s_len, tnw, False)
    d_gla_proj, _ = _matmul_tn_whole("dw_gla_proj", lat3(ogin), lat3(dygla), s_len, tnw, False)

    dp_a1, d_conv_w8, d_conv_b = _conv_bwd(pa, da1, conv_w8, nb, s_len)
    gl = _gla_bwd(pb3, pv3, d_o.reshape(nb, s_len, dv_), (zs_f, b_f, zs_b, b_b2), nb, s_len, c_len, dk_, dv_)
    gl = [g_.reshape(t_all, g_.shape[-1]) for g_ in gl]
    dp_b, d_up2, d_bias2 = _decay_bwd(pb, up2, bias2, gl[0:4], gl[4:8], tiles, lr_blk, dk_, dv_)

    dw_a1, db_a1 = _matmul_tn_whole("dw_a1", u3, lat3(dp_a1), s_len, tnw, True)
    dw_a2, db_a2 = _matmul_tn_whole("dw_a2", u3, lat3(dp_a2), s_len, tnw, True)
    dw_b, db_b = _matmul_tn("dw_b", u, dp_b, t_all, tmb, nbw)
    grads = dict(w_a1=dw_a1, w_a2=dw_a2, w_b=dw_b, conv_w8=d_conv_w8, conv_proj=d_conv_proj, up2=d_up2,
                 gla_proj=d_gla_proj, w_out=d_w_out)

    tka = _tile(2 * d, 2048)
    du_a1 = _matmul_nt("du_a1", dp_a1, w_a, 0, tma, tka, after=on_grads(grads) if on_grads else ())
    du_a2 = _matmul_nt("du_a2", dp_a2, w_a, (2 * d) // tka, tma, tka, after=on_du_a1(du_a1) if on_du_a1 else ())
    du_b = _matmul_nt("du_b", dp_b, w_b, 0, tmb, nbw)
    grad_x2, dmod_ss, d_norm_g = _norm_bwd(x2, ctx2, mod, norm_g, [du_a1, du_a2], du_b, gx1, tiles)
    d_ada_w8, d_ada_b, d_cv = _ada_bwd(cv, ada_w8, dmod_ss, small, nb)

    return dict(
        grads, grad_x=grad_x2.reshape(nb, s_len, d), small=small, cv=d_cv, ada_w8=d_ada_w8, ada_b=d_ada_b,
        norm_g=d_norm_g, b_a1=db_a1, b_a2=db_a2, b_b=db_b, conv_b=d_conv_b, bias2=d_bias2)


def _regroup_pieces(d, r, wshard):
    cb = d // N_DEV
    segs = []
    for j in range(N_DEV):
        segs.append((j * cb, cb, 0, 2 * j * cb))
    for j in range(N_DEV):
        segs.append((d + j * cb, cb, 0, (2 * j + 1) * cb))
    segs += [(2 * d, d, 0, 2 * d), (3 * d, 2 * d + 2 * r, 1, 0), (5 * d + 2 * r, 3 * d, 0, 3 * d)]
    pieces = []
    for o0, w, dst, d0 in segs:
        lo = o0
        while lo < o0 + w:
            j = lo // wshard
            hi = min(o0 + w, (j + 1) * wshard)
            pieces.append((j, lo - j * wshard, hi - lo, dst, d0 + lo - o0))
            lo = hi
    return pieces


def _regroup(o, d, r):
    n_in = 8 * d + 2 * r
    parts = ([], [])
    for _, s0, n, dst, _ in sorted(_regroup_pieces(d, r, n_in), key=lambda p: (p[3], p[4])):
        parts[dst].append(o[..., s0:s0 + n])
    pad = jnp.zeros(o.shape[:-1] + (LANE - 2 * r,), o.dtype)
    return jnp.concatenate(parts[0], axis=-1), jnp.concatenate(parts[1] + [pad], axis=-1)


def _unshard_w_in(g_win, d, r, after=()):
    n_sh, _, ws = g_win.shape
    nbw = 2 * d + LANE
    pieces = _regroup_pieces(d, r, ws)
    tr = math.gcd(d, 256)

    def body(g_ref, *rest):
        a_ref, b_ref = rest[len(after):]
        dsts = (a_ref, b_ref)
        for j, s0, n, dst, d0 in pieces:
            dsts[dst][:, pl.ds(d0, n)] = g_ref[j, :, pl.ds(s0, n)]
        b_ref[:, pl.ds(2 * d + 2 * r, LANE - 2 * r)] = jnp.zeros((tr, LANE - 2 * r), b_ref.dtype)

    return pl.pallas_call(
        body, name="unshard_w_in", grid=(d // tr,),
        in_specs=[pl.BlockSpec((n_sh, tr, ws), lambda i: (0, i, 0))] + [_ANY] * len(after),
        out_specs=(pl.BlockSpec((tr, 6 * d), lambda i: (i, 0)), pl.BlockSpec((tr, nbw), lambda i: (i, 0))),
        out_shape=(jax.ShapeDtypeStruct((d, 6 * d), g_win.dtype), jax.ShapeDtypeStruct((d, nbw), g_win.dtype)),
        compiler_params=_params())(g_win, *after)


def _reshard_w_in(dwt_a1, dwt_a2, dwt_b, d, r):
    ws = (8 * d + 2 * r) // N_DEV
    pieces = _regroup_pieces(d, r, ws)
    tc = math.gcd(d, 256)

    def body(a1_ref, a2_ref, b_ref, o_ref):
        for j, s0, n, dst, d0 in pieces:
            if dst == 1:
                src = b_ref[pl.ds(d0, n), :]
            elif d0 < 2 * d:
                src = a1_ref[pl.ds(d0, n), :]
            else:
                src = a2_ref[pl.ds(d0 - 2 * d, n), :]
            o_ref[j, pl.ds(s0, n), :] = src

    col = lambda h: pl.BlockSpec((h, tc), lambda i: (0, i))
    return pl.pallas_call(
        body, name="reshard_w_in", grid=(d // tc,),
        in_specs=[col(2 * d), col(4 * d), col(2 * d + LANE)],
        out_specs=pl.BlockSpec((N_DEV, ws, tc), lambda i: (0, 0, i)),
        out_shape=jax.ShapeDtypeStruct((N_DEV, ws, d), dwt_b.dtype),
        compiler_params=_params())(dwt_a1, dwt_a2, dwt_b)


_SMALL = ("c_ctx", "ada_b", "norm_g", "b_in", "conv_b", "conv_ln_g", "conv_ln_b", "decay_bias_fwd",
          "decay_bias_bwd", "gla_norm_g", "final_norm_g")


def _small_layout(d, r):
    sizes = dict(c_ctx=d, ada_b=3 * d, norm_g=d, b_in=8 * d + 2 * r, conv_b=d, conv_ln_g=d, conv_ln_b=d,
                 decay_bias_fwd=d // 2, decay_bias_bwd=d // 2, gla_norm_g=d // HEADS, final_norm_g=d, loss=1)
    table, off = {}, 0
    for name in _SMALL + ("loss",):
        table[name] = (off, sizes[name])
        off += -(-sizes[name] // LANE) * LANE
    return table, off


def _pack_small(g, nb, d, r):
    table, width = _small_layout(d, r)
    hv = d // HEADS
    pieces = _regroup_pieces(d, r, 8 * d + 2 * r)
    names = ("small", "cv", "ada_b", "norm_g", "b_a1", "b_a2", "b_b", "conv_b", "bias2")

    def body(sm, cv, ab, ng, ba1, ba2, bb, cvb, b2, o_ref):
        o_ref[...] = jnp.zeros_like(o_ref)

        def put(name, val):
            off, n = table[name]
            o_ref[:, pl.ds(off, n)] = val

        put("c_ctx", cv[nb:nb + 1, :])
        put("ada_b", ab[...])
        put("norm_g", ng[...])
        off_b = table["b_in"][0]
        for _, s0, n, dst, d0 in pieces:
            if dst == 1:
                src = bb[:, pl.ds(d0, n)]
            elif d0 < 2 * d:
                src = ba1[:, pl.ds(d0, n)]
            else:
                src = ba2[:, pl.ds(d0 - 2 * d, n)]
            o_ref[:, pl.ds(off_b + s0, n)] = src
        put("conv_b", cvb[...])
        put("conv_ln_g", sm[1:2, :])
        put("conv_ln_b", sm[2:3, :])
        put("decay_bias_fwd", b2[:, 0:d // 2])
        put("decay_bias_bwd", b2[:, d // 2:d])
        gn = sm[3:4, 0:hv]
        for h in range(1, HEADS):
            gn = gn + sm[3:4, h * hv:(h + 1) * hv]
        put("gla_norm_g", gn)
        put("final_norm_g", sm[0:1, :])
        put("loss", sm[4:5, 0:1])

    return pl.pallas_call(body, name="pack_small", out_shape=jax.ShapeDtypeStruct((1, width), F32),
                          compiler_params=_params())(*[g[k] for k in names])


def _small_adam(parts, ws, ms, vs, d, r):
    table, width = _small_layout(d, r)
    n_parts = parts.shape[0]
    k = len(_SMALL)
    bc1 = 1.0 - ADAM_B1 ** ADAM_STEP
    bc2 = 1.0 - ADAM_B2 ** ADAM_STEP

    def body(p_ref, *refs):
        w_refs, m_refs, v_refs = refs[0:k], refs[k:2 * k], refs[2 * k:3 * k]
        outs = refs[3 * k:]
        tot = p_ref[0]
        for i in range(1, n_parts):
            tot = tot + p_ref[i]
        for i, name in enumerate(_SMALL):
            off, n = table[name]
            g = tot[:, off:off + n]
            mn = ADAM_B1 * m_refs[i][...] + (1.0 - ADAM_B1) * g
            vn = ADAM_B2 * v_refs[i][...] + (1.0 - ADAM_B2) * (g * g)
            outs[i][...] = g
            outs[k + i][...] = -ADAM_LR * ((mn / bc1) / (jnp.sqrt(vn / bc2) + ADAM_EPS) + ADAM_WD * w_refs[i][...])
            outs[2 * k + i][...] = mn
            outs[3 * k + i][...] = vn
        off, _ = table["loss"]
        outs[4 * k][...] = tot[:, off:off + 1]

    shapes = [jax.ShapeDtypeStruct(w.shape, F32) for w in ws]
    res = pl.pallas_call(body, name="small_adam", out_shape=tuple(shapes * 4 + [jax.ShapeDtypeStruct((1, 1), F32)]),
                         compiler_params=_params())(parts, *ws, *ms, *vs)
    return res[0:k], res[k:2 * k], res[2 * k:3 * k], res[3 * k:4 * k], res[4 * k]


def _mesh_pos():
    return lax.axis_index("x"), lax.axis_index("y"), lax.axis_index("c")


def _all_gather(arrs):
    n = len(arrs)
    ns = 9
    split = [a.ndim == 2 and a.shape[0] % 32 == 0 for a in arrs]

    def body(*refs):
        ins, outs = refs[:n], refs[n:2 * n]
        send_sems, recv_sems, local_sems = refs[2 * n:]
        x, y, c = _mesh_pos()
        me, sibling = (x, y, c), (x, y, 1 - c)
        xn, yn, dg = (1 - x, y, c), (x, 1 - y, c), (1 - x, 1 - y, c)
        other = lambda pos: (pos[0], pos[1], 1 - c)

        def slot(a, pos, half):
            ref = outs[a].at[4 * pos[0] + 2 * pos[1] + pos[2]]
            if half is None:
                return ref
            rows = arrs[a].shape[0] // 2
            return ref.at[pl.ds(half * rows, rows)]

        def copy(a, k, block, to, src=None, half=None):
            dst = slot(a, block, half)
            return pltpu.make_async_remote_copy(
                src_ref=dst if src is None else src, dst_ref=dst,
                send_sem=send_sems.at[ns * a + k], recv_sem=recv_sems.at[ns * a + k],
                device_id=to, device_id_type=MESH)

        h0 = lambda a: 0 if split[a] else None
        mine = [pltpu.make_async_copy(ins[a], slot(a, me, None), local_sems.at[a]) for a in range(n)]
        for cp in mine:
            cp.start()
        sent = []
        for a in range(n):
            sent += [copy(a, 0, me, sibling, src=ins[a]), copy(a, 1, me, xn, src=ins[a]),
                     copy(a, 2, me, yn, src=ins[a])]
        for cp in sent:
            cp.start()

        def pass_on(cp):
            cp.start()
            sent.append(cp)

        for a in range(n):
            copy(a, 1, xn, me).wait_recv()
            pass_on(copy(a, 3, xn, sibling))
            pass_on(copy(a, 4, xn, yn, half=h0(a)))
        for a in range(n):
            copy(a, 2, yn, me).wait_recv()
            pass_on(copy(a, 5, yn, sibling))
            if split[a]:
                pass_on(copy(a, 6, yn, xn, half=1))
        for a in range(n):
            copy(a, 4, dg, me, half=h0(a)).wait_recv()
            pass_on(copy(a, 7, dg, sibling, half=h0(a)))
            if split[a]:
                copy(a, 6, dg, me, half=1).wait_recv()
                pass_on(copy(a, 8, dg, sibling, half=1))
        for a in range(n):
            copy(a, 0, sibling, me).wait_recv()
            copy(a, 3, other(xn), me).wait_recv()
            copy(a, 5, other(yn), me).wait_recv()
            copy(a, 7, other(dg), me, half=h0(a)).wait_recv()
            if split[a]:
                copy(a, 8, other(dg), me, half=1).wait_recv()
        for cp in sent:
            cp.wait_send()
        for cp in mine:
            cp.wait()

    return pl.pallas_call(
        body, name="all_gather",
        out_shape=tuple(jax.ShapeDtypeStruct((N_DEV,) + a.shape, a.dtype) for a in arrs),
        in_specs=[_ANY] * n, out_specs=tuple([_ANY] * n),
        scratch_shapes=[pltpu.SemaphoreType.DMA((ns * n,)), pltpu.SemaphoreType.DMA((ns * n,)),
                        pltpu.SemaphoreType.DMA((n,))],
    )(*arrs)


def _exchange_sibling(arrs):
    n = len(arrs)

    def body(*refs):
        ins, outs = refs[:n], refs[n:2 * n]
        send_sems, recv_sems = refs[2 * n:]
        x, y, c = _mesh_pos()
        copies = [pltpu.make_async_remote_copy(
            src_ref=ins[a].at[2 * k + (1 - c)], dst_ref=outs[a].at[k],
            send_sem=send_sems.at[4 * a + k], recv_sem=recv_sems.at[4 * a + k],
            device_id=(x, y, 1 - c), device_id_type=MESH) for a in range(n) for k in range(4)]
        for cp in copies:
            cp.start()
        for cp in copies:
            cp.wait_recv()
        for cp in copies:
            cp.wait_send()

    return pl.pallas_call(
        body, name="grad_exchange_sibling",
        out_shape=tuple(jax.ShapeDtypeStruct((4,) + a.shape[1:], a.dtype) for a in arrs),
        in_specs=[_ANY] * n, out_specs=tuple([_ANY] * n),
        scratch_shapes=[pltpu.SemaphoreType.DMA((4 * n,)), pltpu.SemaphoreType.DMA((4 * n,))],
    )(*arrs)


def _elementwise_tile(r, cdim):
    if r % 8 == 0 and r > 256:
        return math.gcd(r, 256), cdim
    if r > 256 and cdim % 256 == 0:
        return r, 256
    return r, cdim


def _pair_sum(name, mine, theirs):
    _, r, cdim = mine.shape
    tr, tc = _elementwise_tile(r, cdim)

    def body(m_ref, t_ref, o_ref):
        c = lax.axis_index("c")
        own = jnp.where(c == 0, m_ref[:, 0].astype(F32), m_ref[:, 1].astype(F32))
        o_ref[...] = (own + t_ref[...].astype(F32)).astype(o_ref.dtype)

    return pl.pallas_call(
        body, name=name, grid=(r // tr, cdim // tc),
        in_specs=[pl.BlockSpec((4, 2, tr, tc), lambda i, j: (0, 0, i, j)),
                  pl.BlockSpec((4, tr, tc), lambda i, j: (0, i, j))],
        out_specs=pl.BlockSpec((4, tr, tc), lambda i, j: (0, i, j)),
        out_shape=jax.ShapeDtypeStruct((4, r, cdim), mine.dtype),
        compiler_params=_params())(mine.reshape(4, 2, r, cdim), theirs)


_HBM = pl.BlockSpec(memory_space=pltpu.HBM)
_SEM = pl.BlockSpec(memory_space=pltpu.SEMAPHORE)


def _copies_start(name, srcs, lands, make_copies, n_sems):
    n, m = len(srcs), len(lands)

    def body(*refs):
        ins = refs[:n + m]
        send_sems, recv_sems = refs[n + m], refs[n + m + 1]
        for cp in make_copies(ins[:n], ins[n:], send_sems, recv_sems):
            cp.start()
        refs[-1][...] = jnp.zeros_like(refs[-1])

    res = pl.pallas_call(
        body, name=name,
        out_shape=(pltpu.SemaphoreType.DMA((n_sems,)), pltpu.SemaphoreType.DMA((n_sems,)),
                   *[pltpu.HBM(a.shape, a.dtype) for a in (*srcs, *lands)], jax.ShapeDtypeStruct((8, LANE), F32)),
        in_specs=[_HBM] * (n + m),
        out_specs=(_SEM, _SEM, *[_HBM] * (n + m), pl.BlockSpec(memory_space=pltpu.VMEM)),
        input_output_aliases={i: 2 + i for i in range(n + m)},
        compiler_params=pltpu.CompilerParams(has_side_effects=pltpu.SideEffectType.DATAFLOW_SIDE_EFFECTING),
    )(*[pltpu.with_memory_space_constraint(a, pltpu.HBM) for a in (*srcs, *lands)])
    return res[0], res[1], res[2:2 + n], res[2 + n:2 + n + m], res[-1]


def _copies_wait(name, started, after, make_copies):
    send_sems, recv_sems, srcs, lands, _ = started
    n, m = len(srcs), len(lands)

    def body(*refs):
        ins = refs[:n + m]
        for cp in make_copies(ins[:n], ins[n:], refs[n + m], refs[n + m + 1]):
            cp.wait_send()
            cp.wait_recv()

    res = pl.pallas_call(
        body, name=name,
        out_shape=tuple(pltpu.HBM(a.shape, a.dtype) for a in (*srcs, *lands)),
        in_specs=[_HBM] * (n + m) + [_SEM, _SEM] + [_ANY] * len(after),
        out_specs=tuple([_HBM] * (n + m)),
        input_output_aliases={i: i for i in range(n + m)},
        compiler_params=pltpu.CompilerParams(has_side_effects=pltpu.SideEffectType.DATAFLOW_SIDE_EFFECTING),
    )(*srcs, *lands, send_sems, recv_sems, *after)
    return res[:n], res[n:]


def _gather_copies(srcs, lands, send_sems, recv_sems):
    x, y, c = _mesh_pos()
    me_i = 4 * x + 2 * y + c
    copies = []
    for rel in range(1, N_DEV):
        peer = (1 - x if rel & 4 else x, 1 - y if rel & 2 else y, 1 - c if rel & 1 else c)
        for a in range(len(srcs)):
            copies.append(pltpu.make_async_remote_copy(
                src_ref=srcs[a], dst_ref=lands[a].at[me_i], send_sem=send_sems.at[7 * a + rel - 1],
                recv_sem=recv_sems.at[7 * a + rel - 1], device_id=peer, device_id_type=MESH))
    return copies


def _sibling_copies(srcs, lands, send_sems, recv_sems):
    x, y, c = _mesh_pos()
    return [pltpu.make_async_remote_copy(
        src_ref=srcs[a].at[2 * k + (1 - c)], dst_ref=lands[a].at[k], send_sem=send_sems.at[4 * a + k],
        recv_sem=recv_sems.at[4 * a + k], device_id=(x, y, 1 - c), device_id_type=MESH)
        for a in range(len(srcs)) for k in range(4)]


def _chip_copies(srcs, lands, send_sems, recv_sems):
    x, y, c = _mesh_pos()
    my_chip = 2 * x + y
    copies = []
    for rel in range(1, 4):
        px = 1 - x if rel & 2 else x
        py = 1 - y if rel & 1 else y
        for a in range(len(srcs)):
            copies.append(pltpu.make_async_remote_copy(
                src_ref=srcs[a].at[2 * px + py], dst_ref=lands[a].at[my_chip], send_sem=send_sems.at[3 * a + rel - 1],
                recv_sem=recv_sems.at[3 * a + rel - 1], device_id=(px, py, c), device_id_type=MESH))
    return copies


def _sum_adam(name, parts, w, m, v, own=None):
    unit_mid = w.ndim == 3
    _, r, cdim = parts.shape
    n_parts = parts.shape[0]
    tr, tc = _elementwise_tile(r, cdim)
    bc1 = 1.0 - ADAM_B1 ** ADAM_STEP
    bc2 = 1.0 - ADAM_B2 ** ADAM_STEP
    extra = [] if own is None else [own]

    def body(p_ref, *refs):
        w_ref, m_ref, v_ref, g_ref, d_ref, nm_ref, nv_ref = refs[len(extra):]
        if own is None:
            part = lambda k: p_ref[k].astype(F32)
        else:
            my_chip = 2 * lax.axis_index("x") + lax.axis_index("y")
            part = lambda k: jnp.where(my_chip == k, refs[0][k], p_ref[k]).astype(F32)
        g = part(0)
        for k in range(1, n_parts):
            g = g + part(k)
        if unit_mid:
            g = g.reshape(tr, 1, tc)
        mn = ADAM_B1 * m_ref[...] + (1.0 - ADAM_B1) * g
        vn = ADAM_B2 * v_ref[...] + (1.0 - ADAM_B2) * (g * g)
        g_ref[...] = g
        nm_ref[...] = mn
        nv_ref[...] = vn
        d_ref[...] = -ADAM_LR * ((mn / bc1) / (jnp.sqrt(vn / bc2) + ADAM_EPS) + ADAM_WD * w_ref[...])

    blk = (pl.BlockSpec((tr, 1, tc), lambda i, j: (i, 0, j)) if unit_mid
           else pl.BlockSpec((tr, tc), lambda i, j: (i, j)))
    o = jax.ShapeDtypeStruct(w.shape, F32)
    return pl.pallas_call(
        body, name=name, grid=(r // tr, cdim // tc),
        in_specs=[pl.BlockSpec((n_parts, tr, tc), lambda i, j: (0, i, j))] * (1 + len(extra)) + [blk, blk, blk],
        out_specs=(blk, blk, blk, blk), out_shape=(o, o, o, o),
        compiler_params=_params())(parts, *extra, w, m, v)


def _sum_adam_small(items):
    n = len(items)
    bc1 = 1.0 - ADAM_B1 ** ADAM_STEP
    bc2 = 1.0 - ADAM_B2 ** ADAM_STEP

    def body(*refs):
        my_chip = 2 * lax.axis_index("x") + lax.axis_index("y")
        for i in range(n):
            p_ref, own_ref, w_ref, m_ref, v_ref = refs[5 * i:5 * i + 5]
            g_ref, d_ref, nm_ref, nv_ref = refs[5 * n + 4 * i:5 * n + 4 * i + 4]
            g = None
            for k in range(p_ref.shape[0]):
                part = jnp.where(my_chip == k, own_ref[k], p_ref[k]).astype(F32)
                g = part if g is None else g + part
            mn = ADAM_B1 * m_ref[...] + (1.0 - ADAM_B1) * g
            vn = ADAM_B2 * v_ref[...] + (1.0 - ADAM_B2) * (g * g)
            g_ref[...] = g
            nm_ref[...] = mn
            nv_ref[...] = vn
            d_ref[...] = -ADAM_LR * ((mn / bc1) / (jnp.sqrt(vn / bc2) + ADAM_EPS) + ADAM_WD * w_ref[...])

    out_shape = tuple(jax.ShapeDtypeStruct(it[2].shape, F32) for it in items for _ in range(4))
    res = pl.pallas_call(body, name="adam_small_weights", out_shape=out_shape,
                         compiler_params=_params())(*[a for it in items for a in it])
    return [res[4 * i:4 * i + 4] for i in range(n)]


_WEIGHTS = ("c_ctx", "ada_w", "ada_b", "norm_g", "w_in", "b_in", "conv_w", "conv_b", "conv_ln_g", "conv_ln_b",
            "conv_proj", "decay_up_fwd", "decay_bias_fwd", "decay_up_bwd", "decay_bias_bwd", "gla_norm_g",
            "gla_proj", "w_out", "final_norm_g")


def _as2d(a):
    if a.ndim == 1:
        return a.reshape(1, -1)
    return a.reshape(-1, a.shape[-1])


def kernel(x, c, ctx, c_ctx, ada_w, ada_b, norm_g, w_in, b_in, conv_w, conv_b, conv_ln_g, conv_ln_b, conv_proj, decay_up_fwd, decay_bias_fwd, decay_up_bwd, decay_bias_bwd, gla_norm_g, gla_proj, w_out, final_norm_g, loss_target, m_c_ctx, m_ada_w, m_ada_b, m_norm_g, m_w_in, m_b_in, m_conv_w, m_conv_b, m_conv_ln_g, m_conv_ln_b, m_conv_proj, m_decay_up_fwd, m_decay_bias_fwd, m_decay_up_bwd, m_decay_bias_bwd, m_gla_norm_g, m_gla_proj, m_w_out, m_final_norm_g, v_c_ctx, v_ada_w, v_ada_b, v_norm_g, v_w_in, v_b_in, v_conv_w, v_conv_b, v_conv_ln_g, v_conv_ln_b, v_conv_proj, v_decay_up_fwd, v_decay_bias_fwd, v_decay_up_bwd, v_decay_bias_bwd, v_gla_norm_g, v_gla_proj, v_w_out, v_final_norm_g):
    env = dict(locals())
    wts = {k: env[k] for k in _WEIGHTS}
    d = x.shape[-1]
    r = decay_up_fwd.shape[1]
    dk_ = d // 2

    ds, dks = d // N_DEV, dk_ // N_DEV
    g_win, g_ada, conv_w8, g_up = _all_gather(
        [w_in[0].astype(BF16), ada_w[0].astype(BF16), conv_w[0],
         jnp.concatenate([decay_up_fwd[0], decay_up_bwd[0]], axis=1)])
    proj_own = [conv_proj[0].astype(BF16), gla_proj[0].astype(BF16), w_out[0].astype(BF16)]
    me_i = 4 * lax.axis_index("x") + 2 * lax.axis_index("y") + lax.axis_index("c")
    proj_lands = [lax.dynamic_update_slice(lax.empty((N_DEV,) + a.shape, a.dtype), a[None], (me_i, 0, 0))
                  for a in proj_own]
    proj_start = _copies_start("proj_gather_start", proj_own, proj_lands, _gather_copies, 7 * 3)

    def proj(after):
        _, lands = _copies_wait("proj_gather_wait", proj_start, (after,), _gather_copies)
        return [w.reshape(d, d) for w in lands]

    w_a, w_b = _unshard_w_in(g_win, d, r, after=(proj_start[4],))
    up_f = g_up[:, :, 0:dks].transpose(1, 0, 2).reshape(r, dk_)
    up_b = g_up[:, :, dks:].transpose(1, 0, 2).reshape(r, dk_)
    up2 = jnp.zeros((LANE, 2 * dk_), F32).at[0:r, 0:dk_].set(up_f).at[r:2 * r, dk_:].set(up_b)
    bias2 = jnp.concatenate([decay_bias_fwd, decay_bias_bwd], axis=1)
    b_a, b_b = _regroup(b_in, d, r)

    names = ("w_in", "conv_proj", "gla_proj", "w_out", "conv_w", "decay_up")
    comm = {}

    def on_grads(gr):
        d_up = jnp.concatenate([gr["up2"][0:r, 0:dk_].reshape(r, N_DEV, dks).transpose(1, 0, 2),
                                gr["up2"][r:2 * r, dk_:].reshape(r, N_DEV, dks).transpose(1, 0, 2)], axis=2)
        mine = [_reshard_w_in(gr["w_a1"], gr["w_a2"], gr["w_b"], d, r), gr["conv_proj"].reshape(N_DEV, ds, d),
                gr["gla_proj"].reshape(N_DEV, ds, d), gr["w_out"].reshape(N_DEV, ds, d), gr["conv_w8"], d_up]
        lands = [lax.empty((4,) + a.shape[1:], a.dtype) for a in mine]
        comm["sibling"] = _copies_start("grad_sibling_start", mine, lands, _sibling_copies, 4 * len(mine))
        return (comm["sibling"][4],)

    def on_du_a1(du_a1):
        mine, theirs = _copies_wait("grad_sibling_wait", comm["sibling"], (du_a1,), _sibling_copies)
        sums = [_pair_sum("pair_sum_" + nm, a, b) for nm, a, b in zip(names, mine, theirs)]
        lands = [lax.empty(a.shape, a.dtype) for a in sums]
        comm["chips"] = _copies_start("grad_chips_start", sums, lands, _chip_copies, 3 * len(sums))
        return (comm["chips"][4],)

    g = _local_step(x, c, ctx, loss_target, c_ctx, g_ada, ada_b, norm_g[0:1], w_a, b_a, w_b, b_b,
                    conv_w8, conv_b, conv_ln_g, conv_ln_b, up2, bias2, gla_norm_g, final_norm_g.reshape(1, d),
                    proj, on_grads, on_du_a1)

    pack = _pack_small(g, x.shape[0], d, r)
    pack_lands = [lax.dynamic_update_slice(lax.empty((N_DEV,) + pack.shape, F32), pack[None], (me_i, 0, 0))]
    small_start = _copies_start("small_gather_start", [pack], pack_lands, _gather_copies, 7)

    (their_ada,) = _exchange_sibling([g["ada_w8"]])
    ada_sum = _pair_sum("pair_sum_ada_w", g["ada_w8"], their_ada)
    ada_start = _copies_start("ada_chips_start", [ada_sum], [lax.empty(ada_sum.shape, ada_sum.dtype)],
                              _chip_copies, 3)
    own, landed = _copies_wait("grad_chips_wait", comm["chips"], (ada_start[4],), _chip_copies)
    o_win, o_cp, o_gp, o_wo, o_cw, o_up = own
    x_win, x_cp, x_gp, x_wo, x_cw, x_up = landed

    out = {}

    def big(name, parts, wname, own=None):
        w2 = _as2d(wts[wname])
        res = _sum_adam(name, parts, w2, _as2d(env["m_" + wname]), _as2d(env["v_" + wname]), own)
        for pre, arr in zip(("grad_", "delta_", "new_m_", "new_v_"), res):
            out[pre + wname] = arr.reshape(wts[wname].shape)

    as_rows = lambda a: jnp.transpose(a, (2, 0, 1))
    res = _sum_adam("adam_w_in", x_win, as_rows(w_in), as_rows(m_w_in), as_rows(v_w_in), o_win)
    for pre, arr in zip(("grad_", "delta_", "new_m_", "new_v_"), res):
        out[pre + "w_in"] = jnp.transpose(arr, (1, 2, 0))
    small_w = (("conv_proj", x_cp, o_cp), ("gla_proj", x_gp, o_gp), ("w_out", x_wo, o_wo), ("conv_w", x_cw, o_cw),
               ("decay_up_fwd", x_up[:, :, 0:dks], o_up[:, :, 0:dks]),
               ("decay_up_bwd", x_up[:, :, dks:], o_up[:, :, dks:]))
    small_res = _sum_adam_small([(p, o, _as2d(wts[k]), _as2d(env["m_" + k]), _as2d(env["v_" + k]))
                                 for k, p, o in small_w])
    for (k, _, _), arrs in zip(small_w, small_res):
        for pre, arr in zip(("grad_", "delta_", "new_m_", "new_v_"), arrs):
            out[pre + k] = arr.reshape(wts[k].shape)

    _, (packs,) = _copies_wait("small_gather_wait", small_start, (res[0], out["grad_w_out"]), _gather_copies)
    row = lambda a: a.reshape(1, -1)
    sg, sd, sm, sv, loss = _small_adam(packs, [row(wts[k]) for k in _SMALL], [row(env["m_" + k]) for k in _SMALL],
                                       [row(env["v_" + k]) for k in _SMALL], d, r)
    for i, k in enumerate(_SMALL):
        for pre, arrs in (("grad_", sg), ("delta_", sd), ("new_m_", sm), ("new_v_", sv)):
            out[pre + k] = arrs[i].reshape(wts[k].shape)
    loss = loss.reshape(())

    (o_ada,), (x_ada,) = _copies_wait("ada_chips_wait", ada_start, (res[0], out["grad_w_out"], out["grad_b_in"]),
                                      _chip_copies)
    big("adam_ada_w", x_ada, "ada_w", o_ada)

    return (loss, g["grad_x"], *[out["grad_" + k] for k in _WEIGHTS], *[out["delta_" + k] for k in _WEIGHTS],
            *[out["new_m_" + k] for k in _WEIGHTS], *[out["new_v_" + k] for k in _WEIGHTS])
```

```python
import functools
import math

import jax
import jax.numpy as jnp
from jax import lax
from jax.experimental import pallas as pl
from jax.experimental.pallas import tpu as pltpu

F32 = jnp.float32
BF16 = jnp.bfloat16
MESH = pl.DeviceIdType.MESH

N_DEV = 8
GRID_W = 64
CHUNK = 128
HEADS = 4
EPS = 1e-6
GATE_TAU = 16.0
LANE = 128
ADAM_LR, ADAM_B1, ADAM_B2, ADAM_EPS, ADAM_WD, ADAM_STEP = 0.001, 0.9, 0.999, 1e-08, 0.01, 10
VMEM_LIMIT = 60 * 1024 * 1024
_ANY = pl.BlockSpec(memory_space=pl.ANY)


def _params(**kw):
    return pltpu.CompilerParams(vmem_limit_bytes=VMEM_LIMIT, **kw)


def _tile(n, pref):
    t = (min(pref, n) // LANE) * LANE
    while t >= LANE:
        if n % t == 0:
            return t
        t -= LANE
    return n


def _mm(a, b):
    return jnp.dot(a.astype(BF16), b.astype(BF16), preferred_element_type=F32)


def _mm_nt(a, b):
    return lax.dot_general(a.astype(BF16), b.astype(BF16), (((1,), (1,)), ((), ())), preferred_element_type=F32)


def _mm_tn(a, b):
    return lax.dot_general(a.astype(BF16), b.astype(BF16), (((0,), (0,)), ((), ())), preferred_element_type=F32)


def _sigmoid(x):
    return 0.5 * jnp.tanh(0.5 * x) + 0.5


def _dsilu(x, s):
    return s * (1.0 + x * (1.0 - s))


def _rowsel(table, idx, n):
    out = table[0:1, :]
    for r in range(1, n):
        out = jnp.where(idx == r, table[r:r + 1, :], out)
    return out


def _ada_fwd(cv, ada_w8, ada_b):
    n_sh, _, ws = ada_w8.shape

    def body(cv_ref, w_ref, b_ref, o_ref):
        c = cv_ref[...]
        sv = c * _sigmoid(c)
        for j in range(n_sh):
            cols = pl.ds(j * ws, ws)
            o_ref[:, cols] = _mm(sv, w_ref[j]) + b_ref[:, cols]

    return pl.pallas_call(body, name="ada_fwd", out_shape=jax.ShapeDtypeStruct((cv.shape[0], n_sh * ws), F32),
                          compiler_params=_params())(cv, ada_w8, ada_b)


def _ada_bwd(cv, ada_w8, dmod_ss, small, nb):
    n_sh, d, ws = ada_w8.shape

    def body(cv_ref, w_ref, dm_ref, sm_ref, dw_ref, db_ref, dc_ref):
        c = cv_ref[...]
        s = _sigmoid(c)
        sv = c * s
        dm = jnp.concatenate([dm_ref[:, 0:2 * d], sm_ref[8:16, :]], axis=1)
        db_ref[...] = jnp.sum(dm, axis=0, keepdims=True)
        sv_t = jnp.transpose(sv)
        dsv = None
        for j in range(n_sh):
            dmj = dm[:, j * ws:(j + 1) * ws]
            dw = sv_t[:, 0:1] * dmj[0:1, :]
            for row in range(1, nb + 1):
                dw = dw + sv_t[:, row:row + 1] * dmj[row:row + 1, :]
            dw_ref[j] = dw.astype(dw_ref.dtype)
            part = _mm_nt(dmj, w_ref[j])
            dsv = part if dsv is None else dsv + part
        dc_ref[...] = dsv * _dsilu(c, s)

    return pl.pallas_call(
        body, name="ada_bwd",
        out_shape=(jax.ShapeDtypeStruct((n_sh, d, ws), BF16), jax.ShapeDtypeStruct((1, n_sh * ws), F32),
                   jax.ShapeDtypeStruct(cv.shape, F32)),
        compiler_params=_params())(cv, ada_w8, dmod_ss, small)


class _Tiles:
    def __init__(self, nb, s_len, c_len, tm, big):
        self.nb, self.tm, self.big = nb, tm, big
        self.lat, self.ctx = s_len // tm, c_len // tm
        self.pad = -(self.lat + self.ctx) % big
        self.per_ex = self.lat + self.ctx + self.pad
        self.n_all = nb * self.per_ex
        self.rows_per_ex = self.per_ex * tm

    def is_lat(self, i):
        return i % self.per_ex < self.lat

    def is_pad(self, i):
        return i % self.per_ex >= self.lat + self.ctx

    def lat_of_all(self, i):
        return (i // self.per_ex) * self.lat + jnp.minimum(i % self.per_ex, self.lat - 1)

    def ctx_of_all(self, i):
        return (i // self.per_ex) * self.ctx + jnp.clip(i % self.per_ex - self.lat, 0, self.ctx - 1)


def _norm_fwd(x2, ctx2, mod, norm_g, tiles):
    tl, d = x2.shape
    tc = ctx2.shape[0]
    nb, tm = tiles.nb, tiles.tm

    def body(x_ref, c_ref, mod_ref, g_ref, u_ref):
        i = pl.program_id(0)
        lat = tiles.is_lat(i)
        xv = jnp.where(lat, x_ref[...], c_ref[...])
        row = jnp.where(lat, i // tiles.per_ex, nb)
        m = _rowsel(mod_ref[...], row, nb + 1)
        shift, scale = m[:, 0:d], m[:, d:2 * d]
        rstd = lax.rsqrt(jnp.mean(xv * xv, axis=-1, keepdims=True) + EPS)
        u = xv * rstd * g_ref[...] * (1.0 + scale) + shift
        u_ref[...] = jnp.where(tiles.is_pad(i), 0.0, u).astype(BF16)

    return pl.pallas_call(
        body, name="norm_fwd", grid=(tiles.n_all,),
        in_specs=[pl.BlockSpec((tm, d), lambda i: (tiles.lat_of_all(i), 0)),
                  pl.BlockSpec((tm, d), lambda i: (tiles.ctx_of_all(i), 0)),
                  pl.BlockSpec(mod.shape, lambda i: (0, 0)),
                  pl.BlockSpec((1, d), lambda i: (0, 0))],
        out_specs=pl.BlockSpec((tm, d), lambda i: (i, 0)),
        out_shape=jax.ShapeDtypeStruct((tiles.n_all * tm, d), BF16),
        compiler_params=_params())(x2, ctx2, mod, norm_g)


def _norm_bwd(x2, ctx2, mod, norm_g, du_lat, du_b, gx1, tiles):
    tl, d = x2.shape
    nb, tm = tiles.nb, tiles.tm
    nrow = mod.shape[0]
    n_lat_in = len(du_lat)

    def body(x_ref, c_ref, mod_ref, g_ref, *refs):
        dl_refs = refs[:n_lat_in]
        d3_ref, gx_ref, gxo_ref, dmod_ref, dg_ref = refs[n_lat_in:]
        i = pl.program_id(0)

        @pl.when(i == 0)
        def _():
            dmod_ref[...] = jnp.zeros_like(dmod_ref)
            dg_ref[...] = jnp.zeros_like(dg_ref)

        lat = tiles.is_lat(i)
        xv = jnp.where(lat, x_ref[...], c_ref[...])
        row = jnp.where(lat, i // tiles.per_ex, nb)
        m = _rowsel(mod_ref[...], row, nb + 1)
        scale = m[:, d:2 * d]
        g = g_ref[...]
        dl = dl_refs[0][...].astype(F32)
        for ref in dl_refs[1:]:
            dl = dl + ref[...].astype(F32)
        du = jnp.where(tiles.is_pad(i), 0.0, d3_ref[...].astype(F32) + jnp.where(lat, dl, 0.0))
        rstd = lax.rsqrt(jnp.mean(xv * xv, axis=-1, keepdims=True) + EPS)
        xh = xv * rstd
        dshift = jnp.sum(du, axis=0, keepdims=True)
        dscale = jnp.sum(du * xh * g, axis=0, keepdims=True)
        dxn = du * (1.0 + scale)
        dg_ref[...] += jnp.sum(dxn * xh, axis=0, keepdims=True)
        dxh = dxn * g
        dx = rstd * (dxh - xh * jnp.mean(dxh * xh, axis=-1, keepdims=True))

        @pl.when(lat)
        def _():
            gxo_ref[...] = dx + gx_ref[...]

        for r in range(nb + 1):
            dmod_ref[r:r + 1, 0:d] += jnp.where(row == r, dshift, 0.0)
            dmod_ref[r:r + 1, d:2 * d] += jnp.where(row == r, dscale, 0.0)

    lat_map = lambda i: (tiles.lat_of_all(i), 0)
    lat_spec = pl.BlockSpec((tm, d), lat_map)
    return pl.pallas_call(
        body, name="norm_bwd", grid=(tiles.n_all,),
        in_specs=[lat_spec,
                  pl.BlockSpec((tm, d), lambda i: (tiles.ctx_of_all(i), 0)),
                  pl.BlockSpec(mod.shape, lambda i: (0, 0)),
                  pl.BlockSpec((1, d), lambda i: (0, 0))]
                 + [lat_spec] * n_lat_in
                 + [pl.BlockSpec((tm, d), lambda i: (i, 0)), lat_spec],
        out_specs=(lat_spec,
                   pl.BlockSpec((nrow, 3 * d), lambda i: (0, 0)),
                   pl.BlockSpec((1, d), lambda i: (0, 0))),
        out_shape=(jax.ShapeDtypeStruct((tl, d), F32), jax.ShapeDtypeStruct((nrow, 3 * d), F32),
                   jax.ShapeDtypeStruct((1, d), F32)),
        compiler_params=_params())(x2, ctx2, mod, norm_g, *du_lat, du_b, gx1)


def _matmul_bias(name, u3, w, b, s_len, tm, tn):
    nb = u3.shape[0]
    d, n = w.shape
    per = s_len // tm
    rows = nb * s_len

    def body(u_ref, w_ref, b_ref, o_ref):
        o_ref[...] = jnp.dot(u_ref[...], w_ref[...], preferred_element_type=F32) + b_ref[...]

    return pl.pallas_call(
        body, name=name, grid=(n // tn, rows // tm),
        in_specs=[pl.BlockSpec((None, tm, d), lambda j, i: (i // per, i % per, 0)),
                  pl.BlockSpec((d, tn), lambda j, i: (0, j)),
                  pl.BlockSpec((1, tn), lambda j, i: (0, j))],
        out_specs=pl.BlockSpec((tm, tn), lambda j, i: (i, j)),
        out_shape=jax.ShapeDtypeStruct((rows, n), F32),
        compiler_params=_params())(u3, w, b)


def _log_sigmoid(x):
    return jnp.minimum(x, 0.0) - jnp.log(1.0 + jnp.exp(-jnp.abs(x)))


def _inproj_b(u, w_b, b_b, up2, bias2, tm, dk_, dv_):
    t_all, d = u.shape
    nbw = w_b.shape[1]
    n2 = up2.shape[1]

    def body(u_ref, w_ref, b_ref, up_ref, bias_ref, qk_ref, v_ref, g_ref):
        full = jnp.dot(u_ref[...], w_ref[...], preferred_element_type=F32) + b_ref[...]
        lr = full[:, 2 * dk_ + dv_:nbw]
        qk_ref[:, 0:2 * dk_] = full[:, 0:2 * dk_]
        qk_ref[:, 2 * dk_:2 * dk_ + LANE] = lr
        v_ref[...] = full[:, 2 * dk_:2 * dk_ + dv_].astype(BF16)
        g_ref[...] = _log_sigmoid(_mm(lr, up_ref[...]) + bias_ref[...]) * (1.0 / GATE_TAU)

    whole = lambda a: pl.BlockSpec(a.shape, lambda i: (0, 0))
    return pl.pallas_call(
        body, name="inproj_b", grid=(t_all // tm,),
        in_specs=[pl.BlockSpec((tm, d), lambda i: (i, 0)), whole(w_b), whole(b_b), whole(up2), whole(bias2)],
        out_specs=(pl.BlockSpec((tm, 2 * dk_ + LANE), lambda i: (i, 0)), pl.BlockSpec((tm, dv_), lambda i: (i, 0)),
                   pl.BlockSpec((tm, n2), lambda i: (i, 0))),
        out_shape=(jax.ShapeDtypeStruct((t_all, 2 * dk_ + LANE), F32), jax.ShapeDtypeStruct((t_all, dv_), BF16),
                   jax.ShapeDtypeStruct((t_all, n2), F32)),
        compiler_params=_params())(u, w_b, b_b, up2, bias2)


def _matmul_nt(name, a, w, koff, tm, tk, after=()):
    r, kc = a.shape
    d = w.shape[0]
    nk = kc // tk

    def body(a_ref, w_ref, *rest):
        o_ref = rest[len(after)]
        k = pl.program_id(1)
        p = lax.dot_general(a_ref[...], w_ref[...], (((1,), (1,)), ((), ())), preferred_element_type=F32)
        if nk == 1:
            o_ref[...] = p.astype(o_ref.dtype)
            return
        acc_ref = rest[len(after) + 1]

        @pl.when(k == 0)
        def _():
            acc_ref[...] = p

        @pl.when(k > 0)
        def _():
            acc_ref[...] += p

        @pl.when(k == nk - 1)
        def _():
            o_ref[...] = acc_ref[...].astype(o_ref.dtype)

    return pl.pallas_call(
        body, name=name, grid=(r // tm, nk),
        in_specs=[pl.BlockSpec((tm, tk), lambda i, k: (i, k)),
                  pl.BlockSpec((d, tk), lambda i, k: (0, koff + k))] + [_ANY] * len(after),
        out_specs=pl.BlockSpec((tm, d), lambda i, k: (i, 0)),
        out_shape=jax.ShapeDtypeStruct((r, d), BF16),
        scratch_shapes=[pltpu.VMEM((tm, d), F32)] if nk > 1 else [],
        compiler_params=_params())(a, w, *after)


def _matmul_tn(name, a, b, rows, tk, tn):
    m = a.shape[1]
    n = b.shape[1]
    nk = rows // tk

    def body(a_ref, b_ref, o_ref, s_ref, acc_ref):
        k = pl.program_id(1)
        bv = b_ref[...]
        p = lax.dot_general(bv, a_ref[...], (((0,), (0,)), ((), ())), preferred_element_type=F32)
        cs = jnp.sum(bv.astype(F32), axis=0, keepdims=True)

        @pl.when(k == 0)
        def _():
            acc_ref[...] = p
            s_ref[...] = cs

        @pl.when(k > 0)
        def _():
            acc_ref[...] += p
            s_ref[...] += cs

        @pl.when(k == nk - 1)
        def _():
            o_ref[...] = acc_ref[...].astype(o_ref.dtype)

    return pl.pallas_call(
        body, name=name, grid=(n // tn, nk),
        in_specs=[pl.BlockSpec((tk, m), lambda j, k: (k, 0)),
                  pl.BlockSpec((tk, tn), lambda j, k: (k, j))],
        out_specs=(pl.BlockSpec((tn, m), lambda j, k: (j, 0)), pl.BlockSpec((1, tn), lambda j, k: (0, j))),
        out_shape=(jax.ShapeDtypeStruct((n, m), BF16), jax.ShapeDtypeStruct((1, n), F32)),
        scratch_shapes=[pltpu.VMEM((tn, m), F32)],
        compiler_params=_params())(a, b)


def _matmul_tn_whole(name, a3, b3, rows, tn, transposed):
    nb, _, m = a3.shape
    n = b3.shape[2]

    def body(a_ref, b_ref, o_ref, s_ref):
        p, cs = None, None
        for e in range(nb):
            bv = b_ref[e]
            lhs, rhs = (bv, a_ref[e]) if transposed else (a_ref[e], bv)
            pe = lax.dot_general(lhs, rhs, (((0,), (0,)), ((), ())), preferred_element_type=F32)
            ce = jnp.sum(bv.astype(F32), axis=0, keepdims=True)
            p, cs = (pe, ce) if p is None else (p + pe, cs + ce)
        o_ref[...] = p.astype(o_ref.dtype)
        s_ref[...] = cs

    o_spec, o_shape = ((pl.BlockSpec((tn, m), lambda j: (j, 0)), (n, m)) if transposed
                       else (pl.BlockSpec((m, tn), lambda j: (0, j)), (m, n)))
    return pl.pallas_call(
        body, name=name, grid=(n // tn,),
        in_specs=[pl.BlockSpec((nb, rows, m), lambda j: (0, 0, 0)),
                  pl.BlockSpec((nb, rows, tn), lambda j: (0, 0, j))],
        out_specs=(o_spec, pl.BlockSpec((1, tn), lambda j: (0, j))),
        out_shape=(jax.ShapeDtypeStruct(o_shape, BF16), jax.ShapeDtypeStruct((1, n), F32)),
        compiler_params=_params())(a3, b3)


def _conv_window(pad_ref, r, shift, ktaps, width, horizontal):
    if horizontal:
        return pad_ref[r, pl.ds(16 + shift, width), :]
    return pad_ref[r + ktaps // 2 + shift]


def _conv_row(pad_ref, w, r, ktaps, width, horizontal, flip):
    half = ktaps // 2
    acc = None
    for t in range(ktaps):
        win = _conv_window(pad_ref, r, (half - t) if flip else (t - half), ktaps, width, horizontal)
        term = win * w[t:t + 1, :]
        acc = term if acc is None else acc + term
    return acc


def _fill_padded(ref, val, rows, width, ktaps, horizontal):
    half_k = ktaps // 2
    cb = val.shape[-1]
    if horizontal:
        ref[:, 0:16, :] = jnp.zeros((rows, 16, cb), F32)
        ref[:, 16 + width:32 + width, :] = jnp.zeros((rows, 16, cb), F32)
        ref[:, 16:16 + width, :] = val
    else:
        ref[0:half_k, :, :] = jnp.zeros((half_k, width, cb), F32)
        ref[half_k + rows:2 * half_k + rows, :, :] = jnp.zeros((half_k, width, cb), F32)
        ref[half_k:half_k + rows, :, :] = val


def _conv_fwd(pa, conv_w8, conv_b, nb, s):
    nblk, ktaps, cb = conv_w8.shape
    d = nblk * cb
    rows, width = s // GRID_W, GRID_W
    half_k = ktaps // 2
    nh = nblk // 2

    def body(glu_ref, w_ref, b_ref, o_ref, ph_ref, pv_ref):
        j = pl.program_id(1)
        a0 = (glu_ref[:, 0:cb] * _sigmoid(glu_ref[:, cb:2 * cb])).reshape(rows, width, cb)
        w = w_ref[...]

        bias = b_ref[...]

        def run(pad_ref, horizontal):
            _fill_padded(pad_ref, a0, rows, width, ktaps, horizontal)

            def row(r, carry):
                at = pl.ds(pl.multiple_of(r * width, width), width)
                o_ref[at, :] = _conv_row(pad_ref, w, r, ktaps, width, horizontal, False) + bias
                return carry

            lax.fori_loop(0, rows, row, 0)

        @pl.when(j < nh)
        def _():
            run(ph_ref, True)

        @pl.when(j >= nh)
        def _():
            run(pv_ref, False)

    return pl.pallas_call(
        body, name="conv_fwd", grid=(nb, nblk),
        in_specs=[pl.BlockSpec((s, 2 * cb), lambda b, j: (b, j)),
                  pl.BlockSpec((None, ktaps, cb), lambda b, j: (j, 0, 0)),
                  pl.BlockSpec((1, cb), lambda b, j: (0, j))],
        out_specs=pl.BlockSpec((s, cb), lambda b, j: (b, j)),
        out_shape=jax.ShapeDtypeStruct((nb * s, d), F32),
        scratch_shapes=[pltpu.VMEM((rows, width + 32, cb), F32), pltpu.VMEM((rows + 2 * half_k, width, cb), F32)],
        compiler_params=_params())(pa, conv_w8, conv_b)


def _conv_bwd(pa, da1, conv_w8, nb, s):
    nblk, ktaps, cb = conv_w8.shape
    d = nblk * cb
    rows, width = s // GRID_W, GRID_W
    half_k = ktaps // 2
    nh = nblk // 2

    def body(glu_ref, da_ref, w_ref, dp_ref, dw_ref, db_ref, pha_ref, phd_ref, pva_ref, pvd_ref):
        j = pl.program_id(0)
        b = pl.program_id(1)
        a0 = (glu_ref[:, 0:cb] * _sigmoid(glu_ref[:, cb:2 * cb])).reshape(rows, width, cb)
        da1v = da_ref[...]
        d3 = da1v.reshape(rows, width, cb)
        w = w_ref[...]

        @pl.when(b == 0)
        def _():
            dw_ref[...] = jnp.zeros_like(dw_ref)
            db_ref[...] = jnp.zeros_like(db_ref)

        db_ref[...] += jnp.sum(da1v, axis=0, keepdims=True)

        def run(pa_ref, pd_ref, horizontal):
            _fill_padded(pa_ref, a0, rows, width, ktaps, horizontal)
            _fill_padded(pd_ref, d3, rows, width, ktaps, horizontal)

            def row(r, accs):
                at = pl.ds(pl.multiple_of(r * width, width), width)
                da0 = _conv_row(pd_ref, w, r, ktaps, width, horizontal, True)
                gv = glu_ref[at, 0:cb]
                sg = _sigmoid(glu_ref[at, cb:2 * cb])
                dp_ref[at, 0:cb] = (da0 * sg).astype(BF16)
                dp_ref[at, cb:2 * cb] = (da0 * gv * sg * (1.0 - sg)).astype(BF16)
                d_row = da_ref[at, :]
                out = []
                for t in range(ktaps):
                    prod = _conv_window(pa_ref, r, t - half_k, ktaps, width, horizontal) * d_row
                    out.append(accs[t] + jnp.sum(prod.reshape(width // 8, 8, cb), axis=0))
                return tuple(out)

            accs = lax.fori_loop(0, rows, row, tuple(jnp.zeros((8, cb), F32) for _ in range(ktaps)))
            for t in range(ktaps):
                dw_ref[t:t + 1, :] += jnp.sum(accs[t], axis=0, keepdims=True)

        @pl.when(j < nh)
        def _():
            run(pha_ref, phd_ref, True)

        @pl.when(j >= nh)
        def _():
            run(pva_ref, pvd_ref, False)

    return pl.pallas_call(
        body, name="conv_bwd", grid=(nblk, nb),
        in_specs=[pl.BlockSpec((s, 2 * cb), lambda j, b: (b, j)),
                  pl.BlockSpec((s, cb), lambda j, b: (b, j)),
                  pl.BlockSpec((None, ktaps, cb), lambda j, b: (j, 0, 0))],
        out_specs=(pl.BlockSpec((s, 2 * cb), lambda j, b: (b, j)),
                   pl.BlockSpec((None, ktaps, cb), lambda j, b: (j, 0, 0)),
                   pl.BlockSpec((1, cb), lambda j, b: (0, j))),
        out_shape=(jax.ShapeDtypeStruct((nb * s, 2 * d), BF16),
                   jax.ShapeDtypeStruct((nblk, ktaps, cb), F32), jax.ShapeDtypeStruct((1, d), F32)),
        scratch_shapes=[pltpu.VMEM((rows, width + 32, cb), F32), pltpu.VMEM((rows, width + 32, cb), F32),
                        pltpu.VMEM((rows + 2 * half_k, width, cb), F32),
                        pltpu.VMEM((rows + 2 * half_k, width, cb), F32)],
        compiler_params=_params())(pa, da1, conv_w8)


def _decay_bwd(pb, up2, bias2, grads_f, grads_b, tiles, lr_blk, dk_, dv_):
    t_all = pb.shape[0]
    tm = tiles.tm
    n2 = up2.shape[1]
    nbw = 2 * dk_ + dv_ + LANE

    def body(lr_ref, up_ref, b_ref, dqf, dkf, dvf, dgf, dqb, dkb, dvb, dgb, dp_ref, dup_ref, dbias_ref):
        i = pl.program_id(0)
        pad = tiles.is_pad(i)
        live = lambda v: jnp.where(pad, 0.0, v)

        @pl.when(i == 0)
        def _():
            dup_ref[...] = jnp.zeros_like(dup_ref)
            dbias_ref[...] = jnp.zeros_like(dbias_ref)

        lr = lr_ref[...]
        up = up_ref[...]
        logits = _mm(lr, up) + b_ref[...]
        dg = live(jnp.concatenate([dgf[...], dgb[...]], axis=1))
        dlog = dg * (1.0 / GATE_TAU) * _sigmoid(-logits)
        dup_ref[...] += _mm_tn(lr, dlog)
        dbias_ref[...] += jnp.sum(dlog, axis=0, keepdims=True)
        both = lambda f, b: live(f[...].astype(F32) + b[...].astype(F32)).astype(BF16)
        dp_ref[:, 0:dk_] = both(dqf, dqb)
        dp_ref[:, dk_:2 * dk_] = both(dkf, dkb)
        dp_ref[:, 2 * dk_:2 * dk_ + dv_] = both(dvf, dvb)
        dp_ref[:, 2 * dk_ + dv_:nbw] = _mm_nt(dlog, up).astype(BF16)

    row = lambda w: pl.BlockSpec((tm, w), lambda i: (i, 0))
    return pl.pallas_call(
        body, name="decay_bwd", grid=(t_all // tm,),
        in_specs=[pl.BlockSpec((tm, LANE), lambda i: (i, lr_blk)),
                  pl.BlockSpec(up2.shape, lambda i: (0, 0)),
                  pl.BlockSpec((1, n2), lambda i: (0, 0)),
                  row(dk_), row(dk_), row(dv_), row(dk_), row(dk_), row(dk_), row(dv_), row(dk_)],
        out_specs=(row(nbw), pl.BlockSpec(up2.shape, lambda i: (0, 0)), pl.BlockSpec((1, n2), lambda i: (0, 0))),
        out_shape=(jax.ShapeDtypeStruct((t_all, nbw), BF16), jax.ShapeDtypeStruct(up2.shape, F32),
                   jax.ShapeDtypeStruct((1, n2), F32)),
        compiler_params=_params())(pb, up2, bias2, *grads_f, *grads_b)


def _scan_chunk(s, nl, nc, rev):
    if rev:
        return jnp.where(s < nc, nl + (nc - 1 - s), nl - 1 - (s - nc))
    return jnp.where(s < nc, nl + s, s - nc)


def _scan_lat_chunk(s, nl, nc, rev):
    first = nl - 1 if rev else 0
    return jnp.where(s < nc, first, _scan_chunk(s, nl, nc, rev))


def _tri_mm(m_bf, x):
    hi = x.astype(BF16)
    r1 = x - hi.astype(F32)
    mid = r1.astype(BF16)
    lo = (r1 - mid.astype(F32)).astype(BF16)
    dot = lambda p: jnp.dot(m_bf, p, preferred_element_type=F32)
    return dot(hi) + dot(mid) + dot(lo)


def _chunk_masks(c, rev):
    ii = lax.broadcasted_iota(jnp.int32, (c, c), 0)
    jj = lax.broadcasted_iota(jnp.int32, (c, c), 1)
    return ((ii <= jj), (ii >= jj)) if rev else ((ii >= jj), (ii <= jj))


def _chunk_terms(q, k, b, far, mid):
    bf, bm = b[far:far + 1, :], b[mid:mid + 1, :]
    e = jnp.exp(b)
    em = jnp.exp(b - bm)
    eim = jnp.exp(bm - b)
    ed = jnp.exp(bf - b)
    return dict(e=e, em=em, eim=eim, ed=ed, dec=jnp.exp(bf), qe=q * e, qem=q * em, kim=k * eim, kd=k * ed)


def _gla_fwd(pb3, pv3, g3, nb, s_len, c_len, dk_, dv_):
    c = CHUNK
    nl, nc = s_len // c, c_len // c
    ns = nl + nc
    hk, hv = dk_ // HEADS, dv_ // HEADS
    l_len = pb3.shape[1]
    scale = hk ** -0.5
    mid = c // 2

    def body(*refs):
        ins, outs, z_scr = refs[:8], refs[8:14], refs[14]
        s = pl.program_id(0)

        @pl.when(s == 0)
        def _():
            z_scr[...] = jnp.zeros_like(z_scr)

        qs = jnp.where(s >= nc, scale, 0.0)
        for di, rev in enumerate((False, True)):
            q_ref, k_ref, v_ref, g_ref = ins[4 * di:4 * di + 4]
            o_ref, zs_ref, b_ref = outs[3 * di:3 * di + 3]
            mask, _ = _chunk_masks(c, rev)
            m_bf = mask.astype(BF16)
            far = 0 if rev else c - 1
            for b in range(nb):
                bc = _tri_mm(m_bf, g_ref[b])
                b_ref[b] = bc
                for h in range(HEADS):
                    ks, vs = slice(h * hk, (h + 1) * hk), slice(h * hv, (h + 1) * hv)
                    zi = (di * nb + b) * HEADS + h
                    v = v_ref[b, :, vs]
                    t = _chunk_terms(q_ref[b, :, ks] * qs, k_ref[b, :, ks], bc[:, ks], far, mid)
                    a = jnp.where(mask, _mm_nt(t["qem"], t["kim"]), 0.0)
                    z = z_scr[zi]
                    zs_ref[0, b * HEADS + h] = z
                    o_ref[b, :, vs] = _mm(a, v) + _mm_nt(t["qe"], z)
                    z_scr[zi] = z * t["dec"] + _mm_tn(v, t["kd"])

    in_specs, out_specs, out_shape = [], [], []
    for di, rev in enumerate((False, True)):
        ch = functools.partial(_scan_chunk, nl=nl, nc=nc, rev=rev)
        lch = functools.partial(_scan_lat_chunk, nl=nl, nc=nc, rev=rev)
        in_specs += [pl.BlockSpec((nb, c, dk_), lambda s, ch=ch: (0, ch(s), 0)),
                     pl.BlockSpec((nb, c, dk_), lambda s, ch=ch: (0, ch(s), 1)),
                     pl.BlockSpec((nb, c, dv_), lambda s, ch=ch: (0, ch(s), 0)),
                     pl.BlockSpec((nb, c, dk_), lambda s, ch=ch, di=di: (0, ch(s), di))]
        out_specs += [pl.BlockSpec((nb, c, dv_), lambda s, lch=lch: (0, lch(s), 0)),
                      pl.BlockSpec((1, nb * HEADS, hv, hk), lambda s: (s, 0, 0, 0)),
                      pl.BlockSpec((nb, c, dk_), lambda s, ch=ch: (0, ch(s), 0))]
        out_shape += [jax.ShapeDtypeStruct((nb, s_len, dv_), F32),
                      jax.ShapeDtypeStruct((ns, nb * HEADS, hv, hk), F32),
                      jax.ShapeDtypeStruct((nb, l_len, dk_), F32)]
    return pl.pallas_call(
        body, name="gla_fwd", grid=(ns,), in_specs=in_specs, out_specs=tuple(out_specs), out_shape=tuple(out_shape),
        scratch_shapes=[pltpu.VMEM((2 * nb * HEADS, hv, hk), F32)],
        compiler_params=_params())(pb3, pb3, pv3, g3, pb3, pb3, pv3, g3)


def _gla_bwd(pb3, pv3, do3, fwd_saved, nb, s_len, c_len, dk_, dv_):
    c = CHUNK
    nl, nc = s_len // c, c_len // c
    ns = nl + nc
    hk, hv = dk_ // HEADS, dv_ // HEADS
    l_len = pb3.shape[1]
    scale = hk ** -0.5
    mid = c // 2
    zs_f, b_f, zs_b, b_b = fwd_saved

    def body(*refs):
        ins, outs, dz_scr = refs[:12], refs[12:20], refs[20]
        s = pl.program_id(0)
        step = ns - 1 - s

        @pl.when(s == 0)
        def _():
            dz_scr[...] = jnp.zeros_like(dz_scr)

        lat = step >= nc
        qs = jnp.where(lat, scale, 0.0)
        dmul = jnp.where(lat, 1.0, 0.0)
        for di, rev in enumerate((False, True)):
            q_ref, k_ref, v_ref, b_ref, do_ref, zs_ref = ins[6 * di:6 * di + 6]
            dq_ref, dk_ref, dv_ref, dg_ref = outs[4 * di:4 * di + 4]
            mask, mask_t = _chunk_masks(c, rev)
            mt_bf = mask_t.astype(BF16)
            far = 0 if rev else c - 1
            far_row = lax.broadcasted_iota(jnp.int32, (c, hk), 0) == far
            for b in range(nb):
                db_parts = []
                for h in range(HEADS):
                    ks, vs = slice(h * hk, (h + 1) * hk), slice(h * hv, (h + 1) * hv)
                    zi = (di * nb + b) * HEADS + h
                    v = v_ref[b, :, vs]
                    d_o = do_ref[b, :, vs] * dmul
                    t = _chunk_terms(q_ref[b, :, ks] * qs, k_ref[b, :, ks], b_ref[b, :, ks], far, mid)
                    qem, kim, qe, kd = t["qem"], t["kim"], t["qe"], t["kd"]
                    a_t = jnp.where(mask_t, _mm_nt(kim, qem), 0.0)
                    d_a = jnp.where(mask, _mm_nt(d_o, v), 0.0)
                    d_at = jnp.where(mask_t, _mm_nt(v, d_o), 0.0)
                    z = zs_ref[0, b * HEADS + h]
                    dzn = dz_scr[zi]
                    dv_ref[b, :, vs] = (_mm(a_t, d_o) + _mm_nt(kd, dzn)).astype(dv_ref.dtype)
                    dqem = _mm(d_a, kim)
                    dkim = _mm(d_at, qem)
                    dqe = _mm(d_o, z)
                    dkd = _mm(v, dzn)
                    ddec = jnp.sum(z * dzn, axis=0, keepdims=True)
                    dz_scr[zi] = dzn * t["dec"] + _mm_tn(d_o, qe)
                    dq_ref[b, :, ks] = ((dqem * t["em"] + dqe * t["e"]) * qs).astype(dq_ref.dtype)
                    dk_ref[b, :, ks] = (dkim * t["eim"] + dkd * t["ed"]).astype(dk_ref.dtype)
                    db = dqem * qem - dkim * kim + dqe * qe - dkd * kd
                    extra = jnp.sum(dkd * kd, axis=0, keepdims=True) + ddec * t["dec"]
                    db_parts.append(db + jnp.where(far_row, extra, 0.0))
                dg_ref[b] = _tri_mm(mt_bf, jnp.concatenate(db_parts, axis=1))

    in_specs, out_specs, out_shape, args = [], [], [], []
    for di, rev in enumerate((False, True)):
        ch = lambda s, rev=rev: _scan_chunk(ns - 1 - s, nl, nc, rev)
        lch = lambda s, rev=rev: _scan_lat_chunk(ns - 1 - s, nl, nc, rev)
        in_specs += [pl.BlockSpec((nb, c, dk_), lambda s, ch=ch: (0, ch(s), 0)),
                     pl.BlockSpec((nb, c, dk_), lambda s, ch=ch: (0, ch(s), 1)),
                     pl.BlockSpec((nb, c, dv_), lambda s, ch=ch: (0, ch(s), 0)),
                     pl.BlockSpec((nb, c, dk_), lambda s, ch=ch: (0, ch(s), 0)),
                     pl.BlockSpec((nb, c, dv_), lambda s, lch=lch: (0, lch(s), 0)),
                     pl.BlockSpec((1, nb * HEADS, hv, hk), lambda s: (ns - 1 - s, 0, 0, 0))]
        args += [pb3, pb3, pv3, (b_b if rev else b_f), do3, (zs_b if rev else zs_f)]
        for w, dt in ((dk_, BF16), (dk_, BF16), (dv_, BF16), (dk_, F32)):
            out_specs.append(pl.BlockSpec((nb, c, w), lambda s, ch=ch: (0, ch(s), 0)))
            out_shape.append(jax.ShapeDtypeStruct((nb, l_len, w), dt))
    return pl.pallas_call(
        body, name="gla_bwd", grid=(ns,), in_specs=in_specs, out_specs=tuple(out_specs), out_shape=tuple(out_shape),
        scratch_shapes=[pltpu.VMEM((2 * nb * HEADS, hv, hk), F32)],
        compiler_params=_params())(*args)


def _tail(a1, pa, o_f, o_b, x2, tgt, mod, wc, wg, wo, ln_g, ln_b, gn_t, fg, nb, tm, n_split):
    tl, d = x2.shape
    nt = tl // tm
    per_ex = nt // nb
    hv = d // HEADS
    nrow = mod.shape[0]

    def part(shared, a1_ref, z_ref, r_ref, mc_ref, mg_ref, of_ref, ob_ref, x_ref, t_ref,
             dp_ref, da1_ref, do_ref, gx_ref, mrg_ref, dmo_ref, yci_ref, dyc_ref, ogi_ref, dyg_ref, sm_ref):
        bidx, gate, lng, lnb, fgv, gn, wc_, wg_, wo_ = shared

        a1v = a1_ref[...]
        mu = jnp.mean(a1v, axis=-1, keepdims=True)
        xc = a1v - mu
        rs = lax.rsqrt(jnp.mean(xc * xc, axis=-1, keepdims=True) + EPS)
        xh = xc * rs
        a2 = xh * lng + lnb
        s2 = _sigmoid(a2)
        a3 = a2 * s2
        zv = z_ref[...]
        sz = _sigmoid(zv)
        siluz = zv * sz
        ycin = a3 * siluz
        yconv = _mm(ycin, wc_)

        o = of_ref[...] + ob_ref[...]
        ohat_parts, rn_parts = [], []
        for h in range(HEADS):
            oh = o[:, h * hv:(h + 1) * hv]
            rn = lax.rsqrt(jnp.mean(oh * oh, axis=-1, keepdims=True) + EPS)
            ohat_parts.append(oh * rn)
            rn_parts.append(rn)
        ohat = jnp.concatenate(ohat_parts, axis=1)
        on = ohat * gn
        rv = r_ref[...]
        sr = _sigmoid(rv)
        silur = rv * sr
        ogin = on * silur
        ygla = _mm(ogin, wg_)

        sc = _sigmoid(mc_ref[...])
        sg = _sigmoid(mg_ref[...])
        merged = sc * yconv + sg * ygla
        mo = _mm(merged, wo_)
        hn = x_ref[...] + gate * mo
        rf = lax.rsqrt(jnp.mean(hn * hn, axis=-1, keepdims=True) + EPS)
        yh = hn * rf
        err = yh * fgv - t_ref[...]
        loss_part = 0.5 * jnp.sum(err * err) * (1.0 / d)

        dy = err * (1.0 / d)
        dfg = jnp.sum(dy * yh, axis=0, keepdims=True)
        dyh = dy * fgv
        dhn = rf * (dyh - yh * jnp.mean(dyh * yh, axis=-1, keepdims=True))
        gx_ref[...] = dhn
        dgate = jnp.sum(dhn * mo, axis=0, keepdims=True)
        dmo = gate * dhn
        dmerged = _mm_nt(dmo, wo_)
        dyconv = dmerged * sc
        dygla = dmerged * sg
        dp_ref[:, 2 * d:3 * d] = (dmerged * yconv * sc * (1.0 - sc)).astype(BF16)
        dp_ref[:, 3 * d:4 * d] = (dmerged * ygla * sg * (1.0 - sg)).astype(BF16)
        dycin = _mm_nt(dyconv, wc_)
        dogin = _mm_nt(dygla, wg_)
        mrg_ref[...] = merged.astype(BF16)
        dmo_ref[...] = dmo.astype(BF16)
        yci_ref[...] = ycin.astype(BF16)
        dyc_ref[...] = dyconv.astype(BF16)
        ogi_ref[...] = ogin.astype(BF16)
        dyg_ref[...] = dygla.astype(BF16)

        da3 = dycin * siluz
        dp_ref[:, 0:d] = (dycin * a3 * _dsilu(zv, sz)).astype(BF16)
        da2 = da3 * _dsilu(a2, s2)
        dlng = jnp.sum(da2 * xh, axis=0, keepdims=True)
        dlnb = jnp.sum(da2, axis=0, keepdims=True)
        dxh = da2 * lng
        da1_ref[...] = rs * (dxh - jnp.mean(dxh, axis=-1, keepdims=True)
                             - xh * jnp.mean(dxh * xh, axis=-1, keepdims=True))

        don = dogin * silur
        dp_ref[:, d:2 * d] = (dogin * on * _dsilu(rv, sr)).astype(BF16)
        dgn = jnp.sum(don * ohat, axis=0, keepdims=True)
        dyn = don * gn
        for h in range(HEADS):
            vs = slice(h * hv, (h + 1) * hv)
            oh_hat = ohat_parts[h]
            dh = dyn[:, vs]
            do_ref[:, vs] = (rn_parts[h] * (dh - oh_hat * jnp.mean(dh * oh_hat, axis=-1, keepdims=True))
                             ).astype(BF16)

        sm_ref[0:1, :] += dfg
        sm_ref[1:2, :] += dlng
        sm_ref[2:3, :] += dlnb
        sm_ref[3:4, :] += dgn
        sm_ref[4:5, :] += jnp.zeros((1, d), F32) + loss_part
        for b in range(nb):
            sm_ref[8 + b:9 + b, :] += jnp.where(bidx == b, dgate, 0.0)

    def body(*refs):
        mod_ref, wc_ref, wg_ref, wo_ref, lng_ref, lnb_ref, gn_ref, fg_ref = refs[9:17]
        sm_ref = refs[27]
        i = pl.program_id(0)

        @pl.when(i == 0)
        def _():
            sm_ref[...] = jnp.zeros_like(sm_ref)

        bidx = i // per_ex
        shared = (bidx, _rowsel(mod_ref[...], bidx, nb)[:, 2 * d:3 * d], lng_ref[...], lnb_ref[...], fg_ref[...],
                  jnp.concatenate([gn_ref[...]] * HEADS, axis=1), wc_ref[...], wg_ref[...], wo_ref[...])
        rows_per = tm // n_split
        for p in range(n_split):
            rows = pl.ds(p * rows_per, rows_per)
            part(shared, *[r.at[rows] for r in refs[0:9]], *[r.at[rows] for r in refs[17:27]], sm_ref)

    row = pl.BlockSpec((tm, d), lambda i: (i, 0))
    pcol = lambda blk: pl.BlockSpec((tm, d), lambda i: (i, blk))
    full = lambda arr: pl.BlockSpec(arr.shape, lambda i: (0,) * arr.ndim)
    bfo = jax.ShapeDtypeStruct((tl, d), BF16)
    f32o = jax.ShapeDtypeStruct((tl, d), F32)
    return pl.pallas_call(
        body, name="tail", grid=(nt,),
        in_specs=[row, pcol(2), pcol(3), pcol(4), pcol(5), row, row, row, row, full(mod), full(wc), full(wg),
                  full(wo), full(ln_g), full(ln_b), full(gn_t), full(fg)],
        out_specs=(pl.BlockSpec((tm, 4 * d), lambda i: (i, 0)), row, row, row, row, row, row, row, row, row,
                   pl.BlockSpec((16, d), lambda i: (0, 0))),
        out_shape=(jax.ShapeDtypeStruct((tl, 4 * d), BF16), f32o, bfo, f32o, bfo, bfo, bfo, bfo, bfo, bfo,
                   jax.ShapeDtypeStruct((16, d), F32)),
        compiler_params=_params())(a1, pa, pa, pa, pa, o_f, o_b, x2, tgt, mod, wc, wg, wo, ln_g, ln_b, gn_t, fg)


def _local_step(x, c, ctx, tgt, c_ctx, ada_w8, ada_b, norm_g, w_a, b_a, w_b, b_b, conv_w8, conv_b, ln_g, ln_b,
                up2, bias2, gla_norm_g, final_norm_g, proj, on_grads=None, on_du_a1=None):
    nb, s_len, d = x.shape
    c_len = ctx.shape[1]
    dk_, dv_ = d // 2, d
    tl, tc = nb * s_len, nb * c_len
    nbw = 2 * dk_ + dv_ + LANE
    tm = math.gcd(256, c_len)
    tiles = _Tiles(nb, s_len, c_len, tm, 2)
    l_len = tiles.rows_per_ex
    t_all = nb * l_len
    x2, ctx2, tgt2 = x.reshape(tl, d), ctx.reshape(tc, d), tgt.reshape(tl, d)

    cv = jnp.zeros((8, d), F32).at[0:nb].set(c).at[nb].set(c_ctx.reshape(d))
    mod = _ada_fwd(cv, ada_w8, ada_b)
    u = _norm_fwd(x2, ctx2, mod, norm_g, tiles)
    u3 = u.reshape(nb, l_len, d)
    tma = math.gcd(1024, s_len)
    pa = _matmul_bias("inproj_a", u3, w_a, b_a, s_len, tma, _tile(6 * d, 2048))
    tmb = math.gcd(1024, t_all)
    pb, pv, g_all = _inproj_b(u, w_b, b_b, up2, bias2, tmb, dk_, dv_)

    a1 = _conv_fwd(pa, conv_w8, conv_b, nb, s_len)
    lr_blk = (2 * dk_) // LANE
    pb3, pv3 = pb.reshape(nb, l_len, 2 * dk_ + LANE), pv.reshape(nb, l_len, dv_)
    o_f, zs_f, b_f, o_b, zs_b, b_b2 = _gla_fwd(pb3, pv3, g_all.reshape(nb, l_len, 2 * dk_), nb, s_len, c_len,
                                               dk_, dv_)

    conv_proj, gla_proj, w_out = proj(a1) if callable(proj) else proj
    tt = math.gcd(256, s_len)
    (dp_a2, da1, d_o, gx1, merged, dmo, ycin, dyconv, ogin, dygla, small) = _tail(
        a1, pa, o_f.reshape(tl, dv_), o_b.reshape(tl, dv_), x2, tgt2, mod, conv_proj, gla_proj, w_out, ln_g, ln_b,
        gla_norm_g, final_norm_g, nb, tt, 2)

    lat3 = lambda a: a.reshape(nb, s_len, a.shape[-1])
    tnw = _tile(d, 512)
    d_w_out, _ = _matmul_tn_whole("dw_out", lat3(merged), lat3(dmo), s_len, tnw, False)
    d_conv_proj, _ = _matmul_tn_whole("dw_conv_proj", lat3(ycin), lat3(dyconv), s_len, tnw, False)
    d_gla_proj, _ = _matmul_tn_whole("dw_gla_proj", lat3(ogin), lat3(dygla), s_len, tnw, False)

    dp_a1, d_conv_w8, d_conv_b = _conv_bwd(pa, da1, conv_w8, nb, s_len)
    gl = _gla_bwd(pb3, pv3, d_o.reshape(nb, s_len, dv_), (zs_f, b_f, zs_b, b_b2), nb, s_len, c_len, dk_, dv_)
    gl = [g_.reshape(t_all, g_.shape[-1]) for g_ in gl]
    dp_b, d_up2, d_bias2 = _decay_bwd(pb, up2, bias2, gl[0:4], gl[4:8], tiles, lr_blk, dk_, dv_)

    dw_a1, db_a1 = _matmul_tn_whole("dw_a1", u3, lat3(dp_a1), s_len, tnw, True)
    dw_a2, db_a2 = _matmul_tn_whole("dw_a2", u3, lat3(dp_a2), s_len, tnw, True)
    dw_b, db_b = _matmul_tn("dw_b", u, dp_b, t_all, tmb, nbw)
    grads = dict(w_a1=dw_a1, w_a2=dw_a2, w_b=dw_b, conv_w8=d_conv_w8, conv_proj=d_conv_proj, up2=d_up2,
                 gla_proj=d_gla_proj, w_out=d_w_out)

    tka = _tile(2 * d, 2048)
    du_a1 = _matmul_nt("du_a1", dp_a1, w_a, 0, tma, tka, after=on_grads(grads) if on_grads else ())
    du_a2 = _matmul_nt("du_a2", dp_a2, w_a, (2 * d) // tka, tma, tka, after=on_du_a1(du_a1) if on_du_a1 else ())
    du_b = _matmul_nt("du_b", dp_b, w_b, 0, tmb, nbw)
    grad_x2, dmod_ss, d_norm_g = _norm_bwd(x2, ctx2, mod, norm_g, [du_a1, du_a2], du_b, gx1, tiles)
    d_ada_w8, d_ada_b, d_cv = _ada_bwd(cv, ada_w8, dmod_ss, small, nb)

    return dict(
        grads, grad_x=grad_x2.reshape(nb, s_len, d), small=small, cv=d_cv, ada_w8=d_ada_w8, ada_b=d_ada_b,
        norm_g=d_norm_g, b_a1=db_a1, b_a2=db_a2, b_b=db_b, conv_b=d_conv_b, bias2=d_bias2)


def _regroup_pieces(d, r, wshard):
    cb = d // N_DEV
    segs = []
    for j in range(N_DEV):
        segs.append((j * cb, cb, 0, 2 * j * cb))
    for j in range(N_DEV):
        segs.append((d + j * cb, cb, 0, (2 * j + 1) * cb))
    segs += [(2 * d, d, 0, 2 * d), (3 * d, 2 * d + 2 * r, 1, 0), (5 * d + 2 * r, 3 * d, 0, 3 * d)]
    pieces = []
    for o0, w, dst, d0 in segs:
        lo = o0
        while lo < o0 + w:
            j = lo // wshard
            hi = min(o0 + w, (j + 1) * wshard)
            pieces.append((j, lo - j * wshard, hi - lo, dst, d0 + lo - o0))
            lo = hi
    return pieces


def _regroup(o, d, r):
    n_in = 8 * d + 2 * r
    parts = ([], [])
    for _, s0, n, dst, _ in sorted(_regroup_pieces(d, r, n_in), key=lambda p: (p[3], p[4])):
        parts[dst].append(o[..., s0:s0 + n])
    pad = jnp.zeros(o.shape[:-1] + (LANE - 2 * r,), o.dtype)
    return jnp.concatenate(parts[0], axis=-1), jnp.concatenate(parts[1] + [pad], axis=-1)


def _unshard_w_in(g_win, d, r, after=()):
    n_sh, _, ws = g_win.shape
    nbw = 2 * d + LANE
    pieces = _regroup_pieces(d, r, ws)
    tr = math.gcd(d, 256)

    def body(g_ref, *rest):
        a_ref, b_ref = rest[len(after):]
        dsts = (a_ref, b_ref)
        for j, s0, n, dst, d0 in pieces:
            dsts[dst][:, pl.ds(d0, n)] = g_ref[j, :, pl.ds(s0, n)]
        b_ref[:, pl.ds(2 * d + 2 * r, LANE - 2 * r)] = jnp.zeros((tr, LANE - 2 * r), b_ref.dtype)

    return pl.pallas_call(
        body, name="unshard_w_in", grid=(d // tr,),
        in_specs=[pl.BlockSpec((n_sh, tr, ws), lambda i: (0, i, 0))] + [_ANY] * len(after),
        out_specs=(pl.BlockSpec((tr, 6 * d), lambda i: (i, 0)), pl.BlockSpec((tr, nbw), lambda i: (i, 0))),
        out_shape=(jax.ShapeDtypeStruct((d, 6 * d), g_win.dtype), jax.ShapeDtypeStruct((d, nbw), g_win.dtype)),
        compiler_params=_params())(g_win, *after)


def _reshard_w_in(dwt_a1, dwt_a2, dwt_b, d, r):
    ws = (8 * d + 2 * r) // N_DEV
    pieces = _regroup_pieces(d, r, ws)
    tc = math.gcd(d, 256)

    def body(a1_ref, a2_ref, b_ref, o_ref):
        for j, s0, n, dst, d0 in pieces:
            if dst == 1:
                src = b_ref[pl.ds(d0, n), :]
            elif d0 < 2 * d:
                src = a1_ref[pl.ds(d0, n), :]
            else:
                src = a2_ref[pl.ds(d0 - 2 * d, n), :]
            o_ref[j, pl.ds(s0, n), :] = src

    col = lambda h: pl.BlockSpec((h, tc), lambda i: (0, i))
    return pl.pallas_call(
        body, name="reshard_w_in", grid=(d // tc,),
        in_specs=[col(2 * d), col(4 * d), col(2 * d + LANE)],
        out_specs=pl.BlockSpec((N_DEV, ws, tc), lambda i: (0, 0, i)),
        out_shape=jax.ShapeDtypeStruct((N_DEV, ws, d), dwt_b.dtype),
        compiler_params=_params())(dwt_a1, dwt_a2, dwt_b)


_SMALL = ("c_ctx", "ada_b", "norm_g", "b_in", "conv_b", "conv_ln_g", "conv_ln_b", "decay_bias_fwd",
          "decay_bias_bwd", "gla_norm_g", "final_norm_g")


def _small_layout(d, r):
    sizes = dict(c_ctx=d, ada_b=3 * d, norm_g=d, b_in=8 * d + 2 * r, conv_b=d, conv_ln_g=d, conv_ln_b=d,
                 decay_bias_fwd=d // 2, decay_bias_bwd=d // 2, gla_norm_g=d // HEADS, final_norm_g=d, loss=1)
    table, off = {}, 0
    for name in _SMALL + ("loss",):
        table[name] = (off, sizes[name])
        off += -(-sizes[name] // LANE) * LANE
    return table, off


def _pack_small(g, nb, d, r):
    table, width = _small_layout(d, r)
    hv = d // HEADS
    pieces = _regroup_pieces(d, r, 8 * d + 2 * r)
    names = ("small", "cv", "ada_b", "norm_g", "b_a1", "b_a2", "b_b", "conv_b", "bias2")

    def body(sm, cv, ab, ng, ba1, ba2, bb, cvb, b2, o_ref):
        o_ref[...] = jnp.zeros_like(o_ref)

        def put(name, val):
            off, n = table[name]
            o_ref[:, pl.ds(off, n)] = val

        put("c_ctx", cv[nb:nb + 1, :])
        put("ada_b", ab[...])
        put("norm_g", ng[...])
        off_b = table["b_in"][0]
        for _, s0, n, dst, d0 in pieces:
            if dst == 1:
                src = bb[:, pl.ds(d0, n)]
            elif d0 < 2 * d:
                src = ba1[:, pl.ds(d0, n)]
            else:
                src = ba2[:, pl.ds(d0 - 2 * d, n)]
            o_ref[:, pl.ds(off_b + s0, n)] = src
        put("conv_b", cvb[...])
        put("conv_ln_g", sm[1:2, :])
        put("conv_ln_b", sm[2:3, :])
        put("decay_bias_fwd", b2[:, 0:d // 2])
        put("decay_bias_bwd", b2[:, d // 2:d])
        gn = sm[3:4, 0:hv]
        for h in range(1, HEADS):
            gn = gn + sm[3:4, h * hv:(h + 1) * hv]
        put("gla_norm_g", gn)
        put("final_norm_g", sm[0:1, :])
        put("loss", sm[4:5, 0:1])

    return pl.pallas_call(body, name="pack_small", out_shape=jax.ShapeDtypeStruct((1, width), F32),
                          compiler_params=_params())(*[g[k] for k in names])


def _small_adam(parts, ws, ms, vs, d, r):
    table, width = _small_layout(d, r)
    n_parts = parts.shape[0]
    k = len(_SMALL)
    bc1 = 1.0 - ADAM_B1 ** ADAM_STEP
    bc2 = 1.0 - ADAM_B2 ** ADAM_STEP

    def body(p_ref, *refs):
        w_refs, m_refs, v_refs = refs[0:k], refs[k:2 * k], refs[2 * k:3 * k]
        outs = refs[3 * k:]
        tot = p_ref[0]
        for i in range(1, n_parts):
            tot = tot + p_ref[i]
        for i, name in enumerate(_SMALL):
            off, n = table[name]
            g = tot[:, off:off + n]
            mn = ADAM_B1 * m_refs[i][...] + (1.0 - ADAM_B1) * g
            vn = ADAM_B2 * v_refs[i][...] + (1.0 - ADAM_B2) * (g * g)
            outs[i][...] = g
            outs[k + i][...] = -ADAM_LR * ((mn / bc1) / (jnp.sqrt(vn / bc2) + ADAM_EPS) + ADAM_WD * w_refs[i][...])
            outs[2 * k + i][...] = mn
            outs[3 * k + i][...] = vn
        off, _ = table["loss"]
        outs[4 * k][...] = tot[:, off:off + 1]

    shapes = [jax.ShapeDtypeStruct(w.shape, F32) for w in ws]
    res = pl.pallas_call(body, name="small_adam", out_shape=tuple(shapes * 4 + [jax.ShapeDtypeStruct((1, 1), F32)]),
                         compiler_params=_params())(parts, *ws, *ms, *vs)
    return res[0:k], res[k:2 * k], res[2 * k:3 * k], res[3 * k:4 * k], res[4 * k]


def _mesh_pos():
    return lax.axis_index("x"), lax.axis_index("y"), lax.axis_index("c")


def _all_gather(arrs):
    n = len(arrs)
    ns = 9
    split = [a.ndim == 2 and a.shape[0] % 32 == 0 for a in arrs]

    def body(*refs):
        ins, outs = refs[:n], refs[n:2 * n]
        send_sems, recv_sems, local_sems = refs[2 * n:]
        x, y, c = _mesh_pos()
        me, sibling = (x, y, c), (x, y, 1 - c)
        xn, yn, dg = (1 - x, y, c), (x, 1 - y, c), (1 - x, 1 - y, c)
        other = lambda pos: (pos[0], pos[1], 1 - c)

        def slot(a, pos, half):
            ref = outs[a].at[4 * pos[0] + 2 * pos[1] + pos[2]]
            if half is None:
                return ref
            rows = arrs[a].shape[0] // 2
            return ref.at[pl.ds(half * rows, rows)]

        def copy(a, k, block, to, src=None, half=None):
            dst = slot(a, block, half)
            return pltpu.make_async_remote_copy(
                src_ref=dst if src is None else src, dst_ref=dst,
                send_sem=send_sems.at[ns * a + k], recv_sem=recv_sems.at[ns * a + k],
                device_id=to, device_id_type=MESH)

        h0 = lambda a: 0 if split[a] else None
        mine = [pltpu.make_async_copy(ins[a], slot(a, me, None), local_sems.at[a]) for a in range(n)]
        for cp in mine:
            cp.start()
        sent = []
        for a in range(n):
            sent += [copy(a, 0, me, sibling, src=ins[a]), copy(a, 1, me, xn, src=ins[a]),
                     copy(a, 2, me, yn, src=ins[a])]
        for cp in sent:
            cp.start()

        def pass_on(cp):
            cp.start()
            sent.append(cp)

        for a in range(n):
            copy(a, 1, xn, me).wait_recv()
            pass_on(copy(a, 3, xn, sibling))
            pass_on(copy(a, 4, xn, yn, half=h0(a)))
        for a in range(n):
            copy(a, 2, yn, me).wait_recv()
            pass_on(copy(a, 5, yn, sibling))
            if split[a]:
                pass_on(copy(a, 6, yn, xn, half=1))
        for a in range(n):
            copy(a, 4, dg, me, half=h0(a)).wait_recv()
            pass_on(copy(a, 7, dg, sibling, half=h0(a)))
            if split[a]:
                copy(a, 6, dg, me, half=1).wait_recv()
                pass_on(copy(a, 8, dg, sibling, half=1))
        for a in range(n):
            copy(a, 0, sibling, me).wait_recv()
            copy(a, 3, other(xn), me).wait_recv()
            copy(a, 5, other(yn), me).wait_recv()
            copy(a, 7, other(dg), me, half=h0(a)).wait_recv()
            if split[a]:
                copy(a, 8, other(dg), me, half=1).wait_recv()
        for cp in sent:
            cp.wait_send()
        for cp in mine:
            cp.wait()

    return pl.pallas_call(
        body, name="all_gather",
        out_shape=tuple(jax.ShapeDtypeStruct((N_DEV,) + a.shape, a.dtype) for a in arrs),
        in_specs=[_ANY] * n, out_specs=tuple([_ANY] * n),
        scratch_shapes=[pltpu.SemaphoreType.DMA((ns * n,)), pltpu.SemaphoreType.DMA((ns * n,)),
                        pltpu.SemaphoreType.DMA((n,))],
    )(*arrs)


def _exchange_sibling(arrs):
    n = len(arrs)

    def body(*refs):
        ins, outs = refs[:n], refs[n:2 * n]
        send_sems, recv_sems = refs[2 * n:]
        x, y, c = _mesh_pos()
        copies = [pltpu.make_async_remote_copy(
            src_ref=ins[a].at[2 * k + (1 - c)], dst_ref=outs[a].at[k],
            send_sem=send_sems.at[4 * a + k], recv_sem=recv_sems.at[4 * a + k],
            device_id=(x, y, 1 - c), device_id_type=MESH) for a in range(n) for k in range(4)]
        for cp in copies:
            cp.start()
        for cp in copies:
            cp.wait_recv()
        for cp in copies:
            cp.wait_send()

    return pl.pallas_call(
        body, name="grad_exchange_sibling",
        out_shape=tuple(jax.ShapeDtypeStruct((4,) + a.shape[1:], a.dtype) for a in arrs),
        in_specs=[_ANY] * n, out_specs=tuple([_ANY] * n),
        scratch_shapes=[pltpu.SemaphoreType.DMA((4 * n,)), pltpu.SemaphoreType.DMA((4 * n,))],
    )(*arrs)


def _elementwise_tile(r, cdim):
    if r % 8 == 0 and r > 256:
        return math.gcd(r, 256), cdim
    if r > 256 and cdim % 256 == 0:
        return r, 256
    return r, cdim


def _pair_sum(name, mine, theirs):
    _, r, cdim = mine.shape
    tr, tc = _elementwise_tile(r, cdim)

    def body(m_ref, t_ref, o_ref):
        c = lax.axis_index("c")
        own = jnp.where(c == 0, m_ref[:, 0].astype(F32), m_ref[:, 1].astype(F32))
        o_ref[...] = (own + t_ref[...].astype(F32)).astype(o_ref.dtype)

    return pl.pallas_call(
        body, name=name, grid=(r // tr, cdim // tc),
        in_specs=[pl.BlockSpec((4, 2, tr, tc), lambda i, j: (0, 0, i, j)),
                  pl.BlockSpec((4, tr, tc), lambda i, j: (0, i, j))],
        out_specs=pl.BlockSpec((4, tr, tc), lambda i, j: (0, i, j)),
        out_shape=jax.ShapeDtypeStruct((4, r, cdim), mine.dtype),
        compiler_params=_params())(mine.reshape(4, 2, r, cdim), theirs)


def _pair_sum_small(mines, theirs):
    n = len(mines)

    def body(*refs):
        c = lax.axis_index("c")
        for i in range(n):
            m_ref, t_ref, o_ref = refs[i], refs[n + i], refs[2 * n + i]
            own = jnp.where(c == 0, m_ref[:, 0].astype(F32), m_ref[:, 1].astype(F32))
            o_ref[...] = (own + t_ref[...].astype(F32)).astype(o_ref.dtype)

    return pl.pallas_call(
        body, name="pair_sum_small_weights",
        out_shape=tuple(jax.ShapeDtypeStruct(t.shape, m.dtype) for m, t in zip(mines, theirs)),
        compiler_params=_params())(*[m.reshape((4, 2) + m.shape[1:]) for m in mines], *theirs)


_HBM = pl.BlockSpec(memory_space=pltpu.HBM)
_SEM = pl.BlockSpec(memory_space=pltpu.SEMAPHORE)


def _copies_start(name, srcs, lands, make_copies, n_sems):
    n, m = len(srcs), len(lands)

    def body(*refs):
        ins = refs[:n + m]
        send_sems, recv_sems = refs[n + m], refs[n + m + 1]
        for cp in make_copies(ins[:n], ins[n:], send_sems, recv_sems):
            cp.start()
        refs[-1][...] = jnp.zeros_like(refs[-1])

    res = pl.pallas_call(
        body, name=name,
        out_shape=(pltpu.SemaphoreType.DMA((n_sems,)), pltpu.SemaphoreType.DMA((n_sems,)),
                   *[pltpu.HBM(a.shape, a.dtype) for a in (*srcs, *lands)], jax.ShapeDtypeStruct((8, LANE), F32)),
        in_specs=[_HBM] * (n + m),
        out_specs=(_SEM, _SEM, *[_HBM] * (n + m), pl.BlockSpec(memory_space=pltpu.VMEM)),
        input_output_aliases={i: 2 + i for i in range(n + m)},
        compiler_params=pltpu.CompilerParams(has_side_effects=pltpu.SideEffectType.DATAFLOW_SIDE_EFFECTING),
    )(*[pltpu.with_memory_space_constraint(a, pltpu.HBM) for a in (*srcs, *lands)])
    return res[0], res[1], res[2:2 + n], res[2 + n:2 + n + m], res[-1]


def _copies_wait(name, started, after, make_copies):
    send_sems, recv_sems, srcs, lands, _ = started
    n, m = len(srcs), len(lands)

    def body(*refs):
        ins = refs[:n + m]
        for cp in make_copies(ins[:n], ins[n:], refs[n + m], refs[n + m + 1]):
            cp.wait_send()
            cp.wait_recv()

    res = pl.pallas_call(
        body, name=name,
        out_shape=tuple(pltpu.HBM(a.shape, a.dtype) for a in (*srcs, *lands)),
        in_specs=[_HBM] * (n + m) + [_SEM, _SEM] + [_ANY] * len(after),
        out_specs=tuple([_HBM] * (n + m)),
        input_output_aliases={i: i for i in range(n + m)},
        compiler_params=pltpu.CompilerParams(has_side_effects=pltpu.SideEffectType.DATAFLOW_SIDE_EFFECTING),
    )(*srcs, *lands, send_sems, recv_sems, *after)
    return res[:n], res[n:]


def _gather_copies(srcs, lands, send_sems, recv_sems):
    x, y, c = _mesh_pos()
    me_i = 4 * x + 2 * y + c
    copies = []
    for rel in range(1, N_DEV):
        peer = (1 - x if rel & 4 else x, 1 - y if rel & 2 else y, 1 - c if rel & 1 else c)
        for a in range(len(srcs)):
            copies.append(pltpu.make_async_remote_copy(
                src_ref=srcs[a], dst_ref=lands[a].at[me_i], send_sem=send_sems.at[7 * a + rel - 1],
                recv_sem=recv_sems.at[7 * a + rel - 1], device_id=peer, device_id_type=MESH))
    return copies


def _sibling_copies(srcs, lands, send_sems, recv_sems):
    x, y, c = _mesh_pos()
    return [pltpu.make_async_remote_copy(
        src_ref=srcs[a].at[2 * k + (1 - c)], dst_ref=lands[a].at[k], send_sem=send_sems.at[4 * a + k],
        recv_sem=recv_sems.at[4 * a + k], device_id=(x, y, 1 - c), device_id_type=MESH)
        for a in range(len(srcs)) for k in range(4)]


def _chip_copies(srcs, lands, send_sems, recv_sems):
    x, y, c = _mesh_pos()
    my_chip = 2 * x + y
    copies = []
    for rel in range(1, 4):
        px = 1 - x if rel & 2 else x
        py = 1 - y if rel & 1 else y
        for a in range(len(srcs)):
            copies.append(pltpu.make_async_remote_copy(
                src_ref=srcs[a].at[2 * px + py], dst_ref=lands[a].at[my_chip], send_sem=send_sems.at[3 * a + rel - 1],
                recv_sem=recv_sems.at[3 * a + rel - 1], device_id=(px, py, c), device_id_type=MESH))
    return copies


def _sum_adam(name, parts, w, m, v, own=None):
    unit_mid = w.ndim == 3
    _, r, cdim = parts.shape
    n_parts = parts.shape[0]
    tr, tc = _elementwise_tile(r, cdim)
    bc1 = 1.0 - ADAM_B1 ** ADAM_STEP
    bc2 = 1.0 - ADAM_B2 ** ADAM_STEP
    extra = [] if own is None else [own]

    def body(p_ref, *refs):
        w_ref, m_ref, v_ref, g_ref, d_ref, nm_ref, nv_ref = refs[len(extra):]
        if own is None:
            part = lambda k: p_ref[k].astype(F32)
        else:
            my_chip = 2 * lax.axis_index("x") + lax.axis_index("y")
            part = lambda k: jnp.where(my_chip == k, refs[0][k], p_ref[k]).astype(F32)
        g = part(0)
        for k in range(1, n_parts):
            g = g + part(k)
        if unit_mid:
            g = g.reshape(tr, 1, tc)
        mn = ADAM_B1 * m_ref[...] + (1.0 - ADAM_B1) * g
        vn = ADAM_B2 * v_ref[...] + (1.0 - ADAM_B2) * (g * g)
        g_ref[...] = g
        nm_ref[...] = mn
        nv_ref[...] = vn
        d_ref[...] = -ADAM_LR * ((mn / bc1) / (jnp.sqrt(vn / bc2) + ADAM_EPS) + ADAM_WD * w_ref[...])

    blk = (pl.BlockSpec((tr, 1, tc), lambda i, j: (i, 0, j)) if unit_mid
           else pl.BlockSpec((tr, tc), lambda i, j: (i, j)))
    o = jax.ShapeDtypeStruct(w.shape, F32)
    return pl.pallas_call(
        body, name=name, grid=(r // tr, cdim // tc),
        in_specs=[pl.BlockSpec((n_parts, tr, tc), lambda i, j: (0, i, j))] * (1 + len(extra)) + [blk, blk, blk],
        out_specs=(blk, blk, blk, blk), out_shape=(o, o, o, o),
        compiler_params=_params())(parts, *extra, w, m, v)


def _sum_adam_small(items):
    n = len(items)
    bc1 = 1.0 - ADAM_B1 ** ADAM_STEP
    bc2 = 1.0 - ADAM_B2 ** ADAM_STEP

    def body(*refs):
        my_chip = 2 * lax.axis_index("x") + lax.axis_index("y")
        for i in range(n):
            p_ref, own_ref, w_ref, m_ref, v_ref = refs[5 * i:5 * i + 5]
            g_ref, d_ref, nm_ref, nv_ref = refs[5 * n + 4 * i:5 * n + 4 * i + 4]
            g = None
            for k in range(p_ref.shape[0]):
                part = jnp.where(my_chip == k, own_ref[k], p_ref[k]).astype(F32)
                g = part if g is None else g + part
            mn = ADAM_B1 * m_ref[...] + (1.0 - ADAM_B1) * g
            vn = ADAM_B2 * v_ref[...] + (1.0 - ADAM_B2) * (g * g)
            g_ref[...] = g
            nm_ref[...] = mn
            nv_ref[...] = vn
            d_ref[...] = -ADAM_LR * ((mn / bc1) / (jnp.sqrt(vn / bc2) + ADAM_EPS) + ADAM_WD * w_ref[...])

    out_shape = tuple(jax.ShapeDtypeStruct(it[2].shape, F32) for it in items for _ in range(4))
    res = pl.pallas_call(body, name="adam_small_weights", out_shape=out_shape,
                         compiler_params=_params())(*[a for it in items for a in it])
    return [res[4 * i:4 * i + 4] for i in range(n)]


_WEIGHTS = ("c_ctx", "ada_w", "ada_b", "norm_g", "w_in", "b_in", "conv_w", "conv_b", "conv_ln_g", "conv_ln_b",
            "conv_proj", "decay_up_fwd", "decay_bias_fwd", "decay_up_bwd", "decay_bias_bwd", "gla_norm_g",
            "gla_proj", "w_out", "final_norm_g")


def _as2d(a):
    if a.ndim == 1:
        return a.reshape(1, -1)
    return a.reshape(-1, a.shape[-1])


def kernel(x, c, ctx, c_ctx, ada_w, ada_b, norm_g, w_in, b_in, conv_w, conv_b, conv_ln_g, conv_ln_b, conv_proj, decay_up_fwd, decay_bias_fwd, decay_up_bwd, decay_bias_bwd, gla_norm_g, gla_proj, w_out, final_norm_g, loss_target, m_c_ctx, m_ada_w, m_ada_b, m_norm_g, m_w_in, m_b_in, m_conv_w, m_conv_b, m_conv_ln_g, m_conv_ln_b, m_conv_proj, m_decay_up_fwd, m_decay_bias_fwd, m_decay_up_bwd, m_decay_bias_bwd, m_gla_norm_g, m_gla_proj, m_w_out, m_final_norm_g, v_c_ctx, v_ada_w, v_ada_b, v_norm_g, v_w_in, v_b_in, v_conv_w, v_conv_b, v_conv_ln_g, v_conv_ln_b, v_conv_proj, v_decay_up_fwd, v_decay_bias_fwd, v_decay_up_bwd, v_decay_bias_bwd, v_gla_norm_g, v_gla_proj, v_w_out, v_final_norm_g):
    env = dict(locals())
    wts = {k: env[k] for k in _WEIGHTS}
    d = x.shape[-1]
    r = decay_up_fwd.shape[1]
    dk_ = d // 2

    ds, dks = d // N_DEV, dk_ // N_DEV
    g_win, g_ada, conv_w8, g_up = _all_gather(
        [w_in[0].astype(BF16), ada_w[0].astype(BF16), conv_w[0],
         jnp.concatenate([decay_up_fwd[0], decay_up_bwd[0]], axis=1)])
    proj_own = [conv_proj[0].astype(BF16), gla_proj[0].astype(BF16), w_out[0].astype(BF16)]
    me_i = 4 * lax.axis_index("x") + 2 * lax.axis_index("y") + lax.axis_index("c")
    proj_lands = [lax.dynamic_update_slice(lax.empty((N_DEV,) + a.shape, a.dtype), a[None], (me_i, 0, 0))
                  for a in proj_own]
    proj_start = _copies_start("proj_gather_start", proj_own, proj_lands, _gather_copies, 7 * 3)

    def proj(after):
        _, lands = _copies_wait("proj_gather_wait", proj_start, (after,), _gather_copies)
        return [w.reshape(d, d) for w in lands]

    w_a, w_b = _unshard_w_in(g_win, d, r, after=(proj_start[4],))
    up_f = g_up[:, :, 0:dks].transpose(1, 0, 2).reshape(r, dk_)
    up_b = g_up[:, :, dks:].transpose(1, 0, 2).reshape(r, dk_)
    up2 = jnp.zeros((LANE, 2 * dk_), F32).at[0:r, 0:dk_].set(up_f).at[r:2 * r, dk_:].set(up_b)
    bias2 = jnp.concatenate([decay_bias_fwd, decay_bias_bwd], axis=1)
    b_a, b_b = _regroup(b_in, d, r)

    comm = {}

    def on_grads(gr):
        d_up = jnp.concatenate([gr["up2"][0:r, 0:dk_].reshape(r, N_DEV, dks).transpose(1, 0, 2),
                                gr["up2"][r:2 * r, dk_:].reshape(r, N_DEV, dks).transpose(1, 0, 2)], axis=2)
        mine = [_reshard_w_in(gr["w_a1"], gr["w_a2"], gr["w_b"], d, r), gr["conv_proj"].reshape(N_DEV, ds, d),
                gr["gla_proj"].reshape(N_DEV, ds, d), gr["w_out"].reshape(N_DEV, ds, d), gr["conv_w8"], d_up]
        lands = [lax.empty((4,) + a.shape[1:], a.dtype) for a in mine]
        comm["sibling"] = _copies_start("grad_sibling_start", mine, lands, _sibling_copies, 4 * len(mine))
        return (comm["sibling"][4],)

    def on_du_a1(du_a1):
        mine, theirs = _copies_wait("grad_sibling_wait", comm["sibling"], (du_a1,), _sibling_copies)
        sums = [_pair_sum("pair_sum_w_in", mine[0], theirs[0])] + list(_pair_sum_small(mine[1:], theirs[1:]))
        lands = [lax.empty(a.shape, a.dtype) for a in sums]
        comm["chips"] = _copies_start("grad_chips_start", sums, lands, _chip_copies, 3 * len(sums))
        return (comm["chips"][4],)

    g = _local_step(x, c, ctx, loss_target, c_ctx, g_ada, ada_b, norm_g[0:1], w_a, b_a, w_b, b_b,
                    conv_w8, conv_b, conv_ln_g, conv_ln_b, up2, bias2, gla_norm_g, final_norm_g.reshape(1, d),
                    proj, on_grads, on_du_a1)

    pack = _pack_small(g, x.shape[0], d, r)
    pack_lands = [lax.dynamic_update_slice(lax.empty((N_DEV,) + pack.shape, F32), pack[None], (me_i, 0, 0))]
    small_start = _copies_start("small_gather_start", [pack], pack_lands, _gather_copies, 7)

    (their_ada,) = _exchange_sibling([g["ada_w8"]])
    ada_sum = _pair_sum("pair_sum_ada_w", g["ada_w8"], their_ada)
    ada_start = _copies_start("ada_chips_start", [ada_sum], [lax.empty(ada_sum.shape, ada_sum.dtype)],
                              _chip_copies, 3)
    own, landed = _copies_wait("grad_chips_wait", comm["chips"], (ada_start[4],), _chip_copies)
    o_win, o_cp, o_gp, o_wo, o_cw, o_up = own
    x_win, x_cp, x_gp, x_wo, x_cw, x_up = landed

    out = {}

    def big(name, parts, wname, own=None):
        w2 = _as2d(wts[wname])
        res = _sum_adam(name, parts, w2, _as2d(env["m_" + wname]), _as2d(env["v_" + wname]), own)
        for pre, arr in zip(("grad_", "delta_", "new_m_", "new_v_"), res):
            out[pre + wname] = arr.reshape(wts[wname].shape)

    as_rows = lambda a: jnp.transpose(a, (2, 0, 1))
    res = _sum_adam("adam_w_in", x_win, as_rows(w_in), as_rows(m_w_in), as_rows(v_w_in), o_win)
    for pre, arr in zip(("grad_", "delta_", "new_m_", "new_v_"), res):
        out[pre + "w_in"] = jnp.transpose(arr, (1, 2, 0))
    small_w = (("conv_proj", x_cp, o_cp), ("gla_proj", x_gp, o_gp), ("w_out", x_wo, o_wo), ("conv_w", x_cw, o_cw),
               ("decay_up_fwd", x_up[:, :, 0:dks], o_up[:, :, 0:dks]),
               ("decay_up_bwd", x_up[:, :, dks:], o_up[:, :, dks:]))
    small_res = _sum_adam_small([(p, o, _as2d(wts[k]), _as2d(env["m_" + k]), _as2d(env["v_" + k]))
                                 for k, p, o in small_w])
    for (k, _, _), arrs in zip(small_w, small_res):
        for pre, arr in zip(("grad_", "delta_", "new_m_", "new_v_"), arrs):
            out[pre + k] = arr.reshape(wts[k].shape)

    _, (packs,) = _copies_wait("small_gather_wait", small_start, (res[0], out["grad_w_out"]), _gather_copies)
    row = lambda a: a.reshape(1, -1)
    sg, sd, sm, sv, loss = _small_adam(packs, [row(wts[k]) for k in _SMALL], [row(env["m_" + k]) for k in _SMALL],
                                       [row(env["v_" + k]) for k in _SMALL], d, r)
    for i, k in enumerate(_SMALL):
        for pre, arrs in (("grad_", sg), ("delta_", sd), ("new_m_", sm), ("new_v_", sv)):
            out[pre + k] = arrs[i].reshape(wts[k].shape)
    loss = loss.reshape(())

    (o_ada,), (x_ada,) = _copies_wait("ada_chips_wait", ada_start, (res[0], out["grad_w_out"], out["grad_b_in"]),
                                      _chip_copies)
    big("adam_ada_w", x_ada, "ada_w", o_ada)

    return (loss, g["grad_x"], *[out["grad_" + k] for k in _WEIGHTS], *[out["delta_" + k] for k in _WEIGHTS],
            *[out["new_m_" + k] for k in _WEIGHTS], *[out["new_v_" + k] for k in _WEIGHTS])
```

```python
import functools
import math

import jax
import jax.numpy as jnp
from jax import lax
from jax.experimental import pallas as pl
from jax.experimental.pallas import tpu as pltpu

F32 = jnp.float32
BF16 = jnp.bfloat16
MESH = pl.DeviceIdType.MESH

N_DEV = 8
GRID_W = 64
CHUNK = 128
HEADS = 4
EPS = 1e-6
GATE_TAU = 16.0
LANE = 128
ADAM_LR, ADAM_B1, ADAM_B2, ADAM_EPS, ADAM_WD, ADAM_STEP = 0.001, 0.9, 0.999, 1e-08, 0.01, 10
VMEM_LIMIT = 60 * 1024 * 1024
_ANY = pl.BlockSpec(memory_space=pl.ANY)


def _params(**kw):
    return pltpu.CompilerParams(vmem_limit_bytes=VMEM_LIMIT, **kw)


def _tile(n, pref):
    t = (min(pref, n) // LANE) * LANE
    while t >= LANE:
        if n % t == 0:
            return t
        t -= LANE
    return n


def _mm(a, b):
    return jnp.dot(a.astype(BF16), b.astype(BF16), preferred_element_type=F32)


def _mm_nt(a, b):
    return lax.dot_general(a.astype(BF16), b.astype(BF16), (((1,), (1,)), ((), ())), preferred_element_type=F32)


def _mm_tn(a, b):
    return lax.dot_general(a.astype(BF16), b.astype(BF16), (((0,), (0,)), ((), ())), preferred_element_type=F32)


def _sigmoid(x):
    return 0.5 * jnp.tanh(0.5 * x) + 0.5


def _dsilu(x, s):
    return s * (1.0 + x * (1.0 - s))


def _adamw(g, w, m, v):
    bc1 = 1.0 - ADAM_B1 ** ADAM_STEP
    bc2 = 1.0 - ADAM_B2 ** ADAM_STEP
    mn = ADAM_B1 * m + (1.0 - ADAM_B1) * g
    vn = ADAM_B2 * v + (1.0 - ADAM_B2) * (g * g)
    delta = -ADAM_LR * ((mn / bc1) / (jnp.sqrt(vn / bc2) + ADAM_EPS) + ADAM_WD * w)
    return delta, mn, vn


def _rowsel(table, idx, n):
    out = table[0:1, :]
    for r in range(1, n):
        out = jnp.where(idx == r, table[r:r + 1, :], out)
    return out


def _ada_fwd(cv, ada_w8, ada_b):
    n_sh, _, ws = ada_w8.shape

    def body(cv_ref, w_ref, b_ref, o_ref):
        c = cv_ref[...]
        sv = c * _sigmoid(c)
        for j in range(n_sh):
            cols = pl.ds(j * ws, ws)
            o_ref[:, cols] = _mm(sv, w_ref[j]) + b_ref[:, cols]

    return pl.pallas_call(body, name="ada_fwd", out_shape=jax.ShapeDtypeStruct((cv.shape[0], n_sh * ws), F32),
                          compiler_params=_params())(cv, ada_w8, ada_b)


def _ada_bwd(cv, ada_w8, dmod_ss, small, nb):
    n_sh, d, ws = ada_w8.shape

    def body(cv_ref, w_ref, dm_ref, sm_ref, dw_ref, db_ref, dc_ref):
        c = cv_ref[...]
        s = _sigmoid(c)
        sv = c * s
        dm = jnp.concatenate([dm_ref[:, 0:2 * d], sm_ref[8:16, :]], axis=1)
        db_ref[...] = jnp.sum(dm, axis=0, keepdims=True)
        sv_t = jnp.transpose(sv)
        dsv = None
        for j in range(n_sh):
            dmj = dm[:, j * ws:(j + 1) * ws]
            dw = sv_t[:, 0:1] * dmj[0:1, :]
            for row in range(1, nb + 1):
                dw = dw + sv_t[:, row:row + 1] * dmj[row:row + 1, :]
            dw_ref[j] = dw.astype(dw_ref.dtype)
            part = _mm_nt(dmj, w_ref[j])
            dsv = part if dsv is None else dsv + part
        dc_ref[...] = dsv * _dsilu(c, s)

    return pl.pallas_call(
        body, name="ada_bwd",
        out_shape=(jax.ShapeDtypeStruct((n_sh, d, ws), BF16), jax.ShapeDtypeStruct((1, n_sh * ws), F32),
                   jax.ShapeDtypeStruct(cv.shape, F32)),
        compiler_params=_params())(cv, ada_w8, dmod_ss, small)


class _Tiles:
    def __init__(self, nb, s_len, c_len, tm, big):
        self.nb, self.tm, self.big = nb, tm, big
        self.lat, self.ctx = s_len // tm, c_len // tm
        self.pad = -(self.lat + self.ctx) % big
        self.per_ex = self.lat + self.ctx + self.pad
        self.n_all = nb * self.per_ex
        self.rows_per_ex = self.per_ex * tm

    def is_lat(self, i):
        return i % self.per_ex < self.lat

    def is_pad(self, i):
        return i % self.per_ex >= self.lat + self.ctx

    def lat_of_all(self, i):
        return (i // self.per_ex) * self.lat + jnp.minimum(i % self.per_ex, self.lat - 1)

    def ctx_of_all(self, i):
        return (i // self.per_ex) * self.ctx + jnp.clip(i % self.per_ex - self.lat, 0, self.ctx - 1)


def _norm_fwd(x2, ctx2, mod, norm_g, tiles):
    tl, d = x2.shape
    tc = ctx2.shape[0]
    nb, tm = tiles.nb, tiles.tm

    def body(x_ref, c_ref, mod_ref, g_ref, u_ref):
        i = pl.program_id(0)
        lat = tiles.is_lat(i)
        xv = jnp.where(lat, x_ref[...], c_ref[...])
        row = jnp.where(lat, i // tiles.per_ex, nb)
        m = _rowsel(mod_ref[...], row, nb + 1)
        shift, scale = m[:, 0:d], m[:, d:2 * d]
        rstd = lax.rsqrt(jnp.mean(xv * xv, axis=-1, keepdims=True) + EPS)
        u = xv * rstd * g_ref[...] * (1.0 + scale) + shift
        u_ref[...] = jnp.where(tiles.is_pad(i), 0.0, u).astype(BF16)

    return pl.pallas_call(
        body, name="norm_fwd", grid=(tiles.n_all,),
        in_specs=[pl.BlockSpec((tm, d), lambda i: (tiles.lat_of_all(i), 0)),
                  pl.BlockSpec((tm, d), lambda i: (tiles.ctx_of_all(i), 0)),
                  pl.BlockSpec(mod.shape, lambda i: (0, 0)),
                  pl.BlockSpec((1, d), lambda i: (0, 0))],
        out_specs=pl.BlockSpec((tm, d), lambda i: (i, 0)),
        out_shape=jax.ShapeDtypeStruct((tiles.n_all * tm, d), BF16),
        compiler_params=_params())(x2, ctx2, mod, norm_g)


def _norm_bwd(x2, ctx2, mod, norm_g, du_lat, du_b, gx1, tiles):
    tl, d = x2.shape
    nb, tm = tiles.nb, tiles.tm
    nrow = mod.shape[0]
    n_lat_in = len(du_lat)

    def body(x_ref, c_ref, mod_ref, g_ref, *refs):
        dl_refs = refs[:n_lat_in]
        d3_ref, gx_ref, gxo_ref, dmod_ref, dg_ref = refs[n_lat_in:]
        i = pl.program_id(0)

        @pl.when(i == 0)
        def _():
            dmod_ref[...] = jnp.zeros_like(dmod_ref)
            dg_ref[...] = jnp.zeros_like(dg_ref)

        lat = tiles.is_lat(i)
        xv = jnp.where(lat, x_ref[...], c_ref[...])
        row = jnp.where(lat, i // tiles.per_ex, nb)
        m = _rowsel(mod_ref[...], row, nb + 1)
        scale = m[:, d:2 * d]
        g = g_ref[...]
        dl = dl_refs[0][...].astype(F32)
        for ref in dl_refs[1:]:
            dl = dl + ref[...].astype(F32)
        du = jnp.where(tiles.is_pad(i), 0.0, d3_ref[...].astype(F32) + jnp.where(lat, dl, 0.0))
        rstd = lax.rsqrt(jnp.mean(xv * xv, axis=-1, keepdims=True) + EPS)
        xh = xv * rstd
        dshift = jnp.sum(du, axis=0, keepdims=True)
        dscale = jnp.sum(du * xh * g, axis=0, keepdims=True)
        dxn = du * (1.0 + scale)
        dg_ref[...] += jnp.sum(dxn * xh, axis=0, keepdims=True)
        dxh = dxn * g
        dx = rstd * (dxh - xh * jnp.mean(dxh * xh, axis=-1, keepdims=True))

        @pl.when(lat)
        def _():
            gxo_ref[...] = dx + gx_ref[...]

        for r in range(nb + 1):
            dmod_ref[r:r + 1, 0:d] += jnp.where(row == r, dshift, 0.0)
            dmod_ref[r:r + 1, d:2 * d] += jnp.where(row == r, dscale, 0.0)

    lat_map = lambda i: (tiles.lat_of_all(i), 0)
    lat_spec = pl.BlockSpec((tm, d), lat_map)
    return pl.pallas_call(
        body, name="norm_bwd", grid=(tiles.n_all,),
        in_specs=[lat_spec,
                  pl.BlockSpec((tm, d), lambda i: (tiles.ctx_of_all(i), 0)),
                  pl.BlockSpec(mod.shape, lambda i: (0, 0)),
                  pl.BlockSpec((1, d), lambda i: (0, 0))]
                 + [lat_spec] * n_lat_in
                 + [pl.BlockSpec((tm, d), lambda i: (i, 0)), lat_spec],
        out_specs=(lat_spec,
                   pl.BlockSpec((nrow, 3 * d), lambda i: (0, 0)),
                   pl.BlockSpec((1, d), lambda i: (0, 0))),
        out_shape=(jax.ShapeDtypeStruct((tl, d), F32), jax.ShapeDtypeStruct((nrow, 3 * d), F32),
                   jax.ShapeDtypeStruct((1, d), F32)),
        compiler_params=_params())(x2, ctx2, mod, norm_g, *du_lat, du_b, gx1)


def _matmul_bias(name, u3, w, b, s_len, tm, tn):
    nb = u3.shape[0]
    d, n = w.shape
    per = s_len // tm
    rows = nb * s_len

    def body(u_ref, w_ref, b_ref, o_ref):
        o_ref[...] = jnp.dot(u_ref[...], w_ref[...], preferred_element_type=F32) + b_ref[...]

    return pl.pallas_call(
        body, name=name, grid=(n // tn, rows // tm),
        in_specs=[pl.BlockSpec((None, tm, d), lambda j, i: (i // per, i % per, 0)),
                  pl.BlockSpec((d, tn), lambda j, i: (0, j)),
                  pl.BlockSpec((1, tn), lambda j, i: (0, j))],
        out_specs=pl.BlockSpec((tm, tn), lambda j, i: (i, j)),
        out_shape=jax.ShapeDtypeStruct((rows, n), F32),
        compiler_params=_params())(u3, w, b)


def _log_sigmoid(x):
    return jnp.minimum(x, 0.0) - jnp.log(1.0 + jnp.exp(-jnp.abs(x)))


def _inproj_b(u, w_b, b_b, up2, bias2, tm, dk_, dv_):
    t_all, d = u.shape
    nbw = w_b.shape[1]
    n2 = up2.shape[1]

    def body(u_ref, w_ref, b_ref, up_ref, bias_ref, qk_ref, v_ref, g_ref):
        full = jnp.dot(u_ref[...], w_ref[...], preferred_element_type=F32) + b_ref[...]
        lr = full[:, 2 * dk_ + dv_:nbw]
        qk_ref[:, 0:2 * dk_] = full[:, 0:2 * dk_]
        qk_ref[:, 2 * dk_:2 * dk_ + LANE] = lr
        v_ref[...] = full[:, 2 * dk_:2 * dk_ + dv_].astype(BF16)
        g_ref[...] = _log_sigmoid(_mm(lr, up_ref[...]) + bias_ref[...]) * (1.0 / GATE_TAU)

    whole = lambda a: pl.BlockSpec(a.shape, lambda i: (0, 0))
    return pl.pallas_call(
        body, name="inproj_b", grid=(t_all // tm,),
        in_specs=[pl.BlockSpec((tm, d), lambda i: (i, 0)), whole(w_b), whole(b_b), whole(up2), whole(bias2)],
        out_specs=(pl.BlockSpec((tm, 2 * dk_ + LANE), lambda i: (i, 0)), pl.BlockSpec((tm, dv_), lambda i: (i, 0)),
                   pl.BlockSpec((tm, n2), lambda i: (i, 0))),
        out_shape=(jax.ShapeDtypeStruct((t_all, 2 * dk_ + LANE), F32), jax.ShapeDtypeStruct((t_all, dv_), BF16),
                   jax.ShapeDtypeStruct((t_all, n2), F32)),
        compiler_params=_params())(u, w_b, b_b, up2, bias2)


def _matmul_nt(name, a, w, koff, tm, tk, after=()):
    r, kc = a.shape
    d = w.shape[0]
    nk = kc // tk

    def body(a_ref, w_ref, *rest):
        o_ref = rest[len(after)]
        k = pl.program_id(1)
        p = lax.dot_general(a_ref[...], w_ref[...], (((1,), (1,)), ((), ())), preferred_element_type=F32)
        if nk == 1:
            o_ref[...] = p.astype(o_ref.dtype)
            return
        acc_ref = rest[len(after) + 1]

        @pl.when(k == 0)
        def _():
            acc_ref[...] = p

        @pl.when(k > 0)
        def _():
            acc_ref[...] += p

        @pl.when(k == nk - 1)
        def _():
            o_ref[...] = acc_ref[...].astype(o_ref.dtype)

    return pl.pallas_call(
        body, name=name, grid=(r // tm, nk),
        in_specs=[pl.BlockSpec((tm, tk), lambda i, k: (i, k)),
                  pl.BlockSpec((d, tk), lambda i, k: (0, koff + k))] + [_ANY] * len(after),
        out_specs=pl.BlockSpec((tm, d), lambda i, k: (i, 0)),
        out_shape=jax.ShapeDtypeStruct((r, d), BF16),
        scratch_shapes=[pltpu.VMEM((tm, d), F32)] if nk > 1 else [],
        compiler_params=_params())(a, w, *after)


def _matmul_tn(name, a, b, rows, tk, tn):
    m = a.shape[1]
    n = b.shape[1]
    nk = rows // tk

    def body(a_ref, b_ref, o_ref, s_ref, acc_ref):
        k = pl.program_id(1)
        bv = b_ref[...]
        p = lax.dot_general(bv, a_ref[...], (((0,), (0,)), ((), ())), preferred_element_type=F32)
        cs = jnp.sum(bv.astype(F32), axis=0, keepdims=True)

        @pl.when(k == 0)
        def _():
            acc_ref[...] = p
            s_ref[...] = cs

        @pl.when(k > 0)
        def _():
            acc_ref[...] += p
            s_ref[...] += cs

        @pl.when(k == nk - 1)
        def _():
            o_ref[...] = acc_ref[...].astype(o_ref.dtype)

    return pl.pallas_call(
        body, name=name, grid=(n // tn, nk),
        in_specs=[pl.BlockSpec((tk, m), lambda j, k: (k, 0)),
                  pl.BlockSpec((tk, tn), lambda j, k: (k, j))],
        out_specs=(pl.BlockSpec((tn, m), lambda j, k: (j, 0)), pl.BlockSpec((1, tn), lambda j, k: (0, j))),
        out_shape=(jax.ShapeDtypeStruct((n, m), BF16), jax.ShapeDtypeStruct((1, n), F32)),
        scratch_shapes=[pltpu.VMEM((tn, m), F32)],
        compiler_params=_params())(a, b)


def _matmul_tn_whole(name, a3, b3, rows, tn, transposed):
    nb, _, m = a3.shape
    n = b3.shape[2]

    def body(a_ref, b_ref, o_ref, s_ref):
        p, cs = None, None
        for e in range(nb):
            bv = b_ref[e]
            lhs, rhs = (bv, a_ref[e]) if transposed else (a_ref[e], bv)
            pe = lax.dot_general(lhs, rhs, (((0,), (0,)), ((), ())), preferred_element_type=F32)
            ce = jnp.sum(bv.astype(F32), axis=0, keepdims=True)
            p, cs = (pe, ce) if p is None else (p + pe, cs + ce)
        o_ref[...] = p.astype(o_ref.dtype)
        s_ref[...] = cs

    o_spec, o_shape = ((pl.BlockSpec((tn, m), lambda j: (j, 0)), (n, m)) if transposed
                       else (pl.BlockSpec((m, tn), lambda j: (0, j)), (m, n)))
    return pl.pallas_call(
        body, name=name, grid=(n // tn,),
        in_specs=[pl.BlockSpec((nb, rows, m), lambda j: (0, 0, 0)),
                  pl.BlockSpec((nb, rows, tn), lambda j: (0, 0, j))],
        out_specs=(o_spec, pl.BlockSpec((1, tn), lambda j: (0, j))),
        out_shape=(jax.ShapeDtypeStruct(o_shape, BF16), jax.ShapeDtypeStruct((1, n), F32)),
        compiler_params=_params())(a3, b3)


def _conv_window(pad_ref, r, shift, ktaps, width, horizontal):
    if horizontal:
        return pad_ref[r, pl.ds(16 + shift, width), :]
    return pad_ref[r + ktaps // 2 + shift]


def _conv_row(pad_ref, w, r, ktaps, width, horizontal, flip):
    half = ktaps // 2
    acc = None
    for t in range(ktaps):
        win = _conv_window(pad_ref, r, (half - t) if flip else (t - half), ktaps, width, horizontal)
        term = win * w[t:t + 1, :]
        acc = term if acc is None else acc + term
    return acc


def _fill_padded(ref, val, rows, width, ktaps, horizontal):
    half_k = ktaps // 2
    cb = val.shape[-1]
    if horizontal:
        ref[:, 0:16, :] = jnp.zeros((rows, 16, cb), F32)
        ref[:, 16 + width:32 + width, :] = jnp.zeros((rows, 16, cb), F32)
        ref[:, 16:16 + width, :] = val
    else:
        ref[0:half_k, :, :] = jnp.zeros((half_k, width, cb), F32)
        ref[half_k + rows:2 * half_k + rows, :, :] = jnp.zeros((half_k, width, cb), F32)
        ref[half_k:half_k + rows, :, :] = val


def _conv_fwd(pa, conv_w8, conv_b, nb, s):
    nblk, ktaps, cb = conv_w8.shape
    d = nblk * cb
    rows, width = s // GRID_W, GRID_W
    half_k = ktaps // 2
    nh = nblk // 2

    def body(glu_ref, w_ref, b_ref, o_ref, ph_ref, pv_ref):
        j = pl.program_id(1)
        a0 = (glu_ref[:, 0:cb] * _sigmoid(glu_ref[:, cb:2 * cb])).reshape(rows, width, cb)
        w = w_ref[...]

        bias = b_ref[...]

        def run(pad_ref, horizontal):
            _fill_padded(pad_ref, a0, rows, width, ktaps, horizontal)

            def row(r, carry):
                at = pl.ds(pl.multiple_of(r * width, width), width)
                o_ref[at, :] = _conv_row(pad_ref, w, r, ktaps, width, horizontal, False) + bias
                return carry

            lax.fori_loop(0, rows, row, 0)

        @pl.when(j < nh)
        def _():
            run(ph_ref, True)

        @pl.when(j >= nh)
        def _():
            run(pv_ref, False)

    return pl.pallas_call(
        body, name="conv_fwd", grid=(nb, nblk),
        in_specs=[pl.BlockSpec((s, 2 * cb), lambda b, j: (b, j)),
                  pl.BlockSpec((None, ktaps, cb), lambda b, j: (j, 0, 0)),
                  pl.BlockSpec((1, cb), lambda b, j: (0, j))],
        out_specs=pl.BlockSpec((s, cb), lambda b, j: (b, j)),
        out_shape=jax.ShapeDtypeStruct((nb * s, d), F32),
        scratch_shapes=[pltpu.VMEM((rows, width + 32, cb), F32), pltpu.VMEM((rows + 2 * half_k, width, cb), F32)],
        compiler_params=_params())(pa, conv_w8, conv_b)


def _conv_bwd(pa, da1, conv_w8, nb, s):
    nblk, ktaps, cb = conv_w8.shape
    d = nblk * cb
    rows, width = s // GRID_W, GRID_W
    half_k = ktaps // 2
    nh = nblk // 2

    def body(glu_ref, da_ref, w_ref, dp_ref, dw_ref, db_ref, pha_ref, phd_ref, pva_ref, pvd_ref):
        j = pl.program_id(0)
        b = pl.program_id(1)
        a0 = (glu_ref[:, 0:cb] * _sigmoid(glu_ref[:, cb:2 * cb])).reshape(rows, width, cb)
        da1v = da_ref[...]
        d3 = da1v.reshape(rows, width, cb)
        w = w_ref[...]

        @pl.when(b == 0)
        def _():
            dw_ref[...] = jnp.zeros_like(dw_ref)
            db_ref[...] = jnp.zeros_like(db_ref)

        db_ref[...] += jnp.sum(da1v, axis=0, keepdims=True)

        def run(pa_ref, pd_ref, horizontal):
            _fill_padded(pa_ref, a0, rows, width, ktaps, horizontal)
            _fill_padded(pd_ref, d3, rows, width, ktaps, horizontal)

            def row(r, accs):
                at = pl.ds(pl.multiple_of(r * width, width), width)
                da0 = _conv_row(pd_ref, w, r, ktaps, width, horizontal, True)
                gv = glu_ref[at, 0:cb]
                sg = _sigmoid(glu_ref[at, cb:2 * cb])
                dp_ref[at, 0:cb] = (da0 * sg).astype(BF16)
                dp_ref[at, cb:2 * cb] = (da0 * gv * sg * (1.0 - sg)).astype(BF16)
                d_row = da_ref[at, :]
                out = []
                for t in range(ktaps):
                    prod = _conv_window(pa_ref, r, t - half_k, ktaps, width, horizontal) * d_row
                    out.append(accs[t] + jnp.sum(prod.reshape(width // 8, 8, cb), axis=0))
                return tuple(out)

            accs = lax.fori_loop(0, rows, row, tuple(jnp.zeros((8, cb), F32) for _ in range(ktaps)))
            for t in range(ktaps):
                dw_ref[t:t + 1, :] += jnp.sum(accs[t], axis=0, keepdims=True)

        @pl.when(j < nh)
        def _():
            run(pha_ref, phd_ref, True)

        @pl.when(j >= nh)
        def _():
            run(pva_ref, pvd_ref, False)

    return pl.pallas_call(
        body, name="conv_bwd", grid=(nblk, nb),
        in_specs=[pl.BlockSpec((s, 2 * cb), lambda j, b: (b, j)),
                  pl.BlockSpec((s, cb), lambda j, b: (b, j)),
                  pl.BlockSpec((None, ktaps, cb), lambda j, b: (j, 0, 0))],
        out_specs=(pl.BlockSpec((s, 2 * cb), lambda j, b: (b, j)),
                   pl.BlockSpec((None, ktaps, cb), lambda j, b: (j, 0, 0)),
                   pl.BlockSpec((1, cb), lambda j, b: (0, j))),
        out_shape=(jax.ShapeDtypeStruct((nb * s, 2 * d), BF16),
                   jax.ShapeDtypeStruct((nblk, ktaps, cb), F32), jax.ShapeDtypeStruct((1, d), F32)),
        scratch_shapes=[pltpu.VMEM((rows, width + 32, cb), F32), pltpu.VMEM((rows, width + 32, cb), F32),
                        pltpu.VMEM((rows + 2 * half_k, width, cb), F32),
                        pltpu.VMEM((rows + 2 * half_k, width, cb), F32)],
        compiler_params=_params())(pa, da1, conv_w8)


def _decay_bwd(pb, up2, bias2, grads_f, grads_b, tiles, lr_blk, dk_, dv_):
    t_all = pb.shape[0]
    tm = tiles.tm
    n2 = up2.shape[1]
    nbw = 2 * dk_ + dv_ + LANE

    def body(lr_ref, up_ref, b_ref, dqf, dkf, dvf, dgf, dqb, dkb, dvb, dgb, dp_ref, dup_ref, dbias_ref):
        i = pl.program_id(0)
        pad = tiles.is_pad(i)
        live = lambda v: jnp.where(pad, 0.0, v)

        @pl.when(i == 0)
        def _():
            dup_ref[...] = jnp.zeros_like(dup_ref)
            dbias_ref[...] = jnp.zeros_like(dbias_ref)

        lr = lr_ref[...]
        up = up_ref[...]
        logits = _mm(lr, up) + b_ref[...]
        dg = live(jnp.concatenate([dgf[...], dgb[...]], axis=1))
        dlog = dg * (1.0 / GATE_TAU) * _sigmoid(-logits)
        dup_ref[...] += _mm_tn(lr, dlog)
        dbias_ref[...] += jnp.sum(dlog, axis=0, keepdims=True)
        both = lambda f, b: live(f[...].astype(F32) + b[...].astype(F32)).astype(BF16)
        dp_ref[:, 0:dk_] = both(dqf, dqb)
        dp_ref[:, dk_:2 * dk_] = both(dkf, dkb)
        dp_ref[:, 2 * dk_:2 * dk_ + dv_] = both(dvf, dvb)
        dp_ref[:, 2 * dk_ + dv_:nbw] = _mm_nt(dlog, up).astype(BF16)

    row = lambda w: pl.BlockSpec((tm, w), lambda i: (i, 0))
    return pl.pallas_call(
        body, name="decay_bwd", grid=(t_all // tm,),
        in_specs=[pl.BlockSpec((tm, LANE), lambda i: (i, lr_blk)),
                  pl.BlockSpec(up2.shape, lambda i: (0, 0)),
                  pl.BlockSpec((1, n2), lambda i: (0, 0)),
                  row(dk_), row(dk_), row(dv_), row(dk_), row(dk_), row(dk_), row(dv_), row(dk_)],
        out_specs=(row(nbw), pl.BlockSpec(up2.shape, lambda i: (0, 0)), pl.BlockSpec((1, n2), lambda i: (0, 0))),
        out_shape=(jax.ShapeDtypeStruct((t_all, nbw), BF16), jax.ShapeDtypeStruct(up2.shape, F32),
                   jax.ShapeDtypeStruct((1, n2), F32)),
        compiler_params=_params())(pb, up2, bias2, *grads_f, *grads_b)


def _scan_chunk(s, nl, nc, rev):
    if rev:
        return jnp.where(s < nc, nl + (nc - 1 - s), nl - 1 - (s - nc))
    return jnp.where(s < nc, nl + s, s - nc)


def _scan_lat_chunk(s, nl, nc, rev):
    first = nl - 1 if rev else 0
    return jnp.where(s < nc, first, _scan_chunk(s, nl, nc, rev))


def _tri_mm(m_bf, x):
    hi = x.astype(BF16)
    r1 = x - hi.astype(F32)
    mid = r1.astype(BF16)
    lo = (r1 - mid.astype(F32)).astype(BF16)
    dot = lambda p: jnp.dot(m_bf, p, preferred_element_type=F32)
    return dot(hi) + dot(mid) + dot(lo)


def _chunk_masks(c, rev):
    ii = lax.broadcasted_iota(jnp.int32, (c, c), 0)
    jj = lax.broadcasted_iota(jnp.int32, (c, c), 1)
    return ((ii <= jj), (ii >= jj)) if rev else ((ii >= jj), (ii <= jj))


def _chunk_terms(q, k, b, far, mid):
    bf, bm = b[far:far + 1, :], b[mid:mid + 1, :]
    e = jnp.exp(b)
    em = jnp.exp(b - bm)
    eim = jnp.exp(bm - b)
    ed = jnp.exp(bf - b)
    return dict(e=e, em=em, eim=eim, ed=ed, dec=jnp.exp(bf), qe=q * e, qem=q * em, kim=k * eim, kd=k * ed)


def _gla_fwd(pb3, pv3, g3, nb, s_len, c_len, dk_, dv_):
    c = CHUNK
    nl, nc = s_len // c, c_len // c
    ns = nl + nc
    hk, hv = dk_ // HEADS, dv_ // HEADS
    l_len = pb3.shape[1]
    scale = hk ** -0.5
    mid = c // 2

    def body(*refs):
        ins, outs, z_scr = refs[:8], refs[8:14], refs[14]
        s = pl.program_id(0)

        @pl.when(s == 0)
        def _():
            z_scr[...] = jnp.zeros_like(z_scr)

        qs = jnp.where(s >= nc, scale, 0.0)
        for di, rev in enumerate((False, True)):
            q_ref, k_ref, v_ref, g_ref = ins[4 * di:4 * di + 4]
            o_ref, zs_ref, b_ref = outs[3 * di:3 * di + 3]
            mask, _ = _chunk_masks(c, rev)
            m_bf = mask.astype(BF16)
            far = 0 if rev else c - 1
            for b in range(nb):
                bc = _tri_mm(m_bf, g_ref[b])
                b_ref[b] = bc
                for h in range(HEADS):
                    ks, vs = slice(h * hk, (h + 1) * hk), slice(h * hv, (h + 1) * hv)
                    zi = (di * nb + b) * HEADS + h
                    v = v_ref[b, :, vs]
                    t = _chunk_terms(q_ref[b, :, ks] * qs, k_ref[b, :, ks], bc[:, ks], far, mid)
                    a = jnp.where(mask, _mm_nt(t["qem"], t["kim"]), 0.0)
                    z = z_scr[zi]
                    zs_ref[0, b * HEADS + h] = z
                    o_ref[b, :, vs] = _mm(a, v) + _mm_nt(t["qe"], z)
                    z_scr[zi] = z * t["dec"] + _mm_tn(v, t["kd"])

    in_specs, out_specs, out_shape = [], [], []
    for di, rev in enumerate((False, True)):
        ch = functools.partial(_scan_chunk, nl=nl, nc=nc, rev=rev)
        lch = functools.partial(_scan_lat_chunk, nl=nl, nc=nc, rev=rev)
        in_specs += [pl.BlockSpec((nb, c, dk_), lambda s, ch=ch: (0, ch(s), 0)),
                     pl.BlockSpec((nb, c, dk_), lambda s, ch=ch: (0, ch(s), 1)),
                     pl.BlockSpec((nb, c, dv_), lambda s, ch=ch: (0, ch(s), 0)),
                     pl.BlockSpec((nb, c, dk_), lambda s, ch=ch, di=di: (0, ch(s), di))]
        out_specs += [pl.BlockSpec((nb, c, dv_), lambda s, lch=lch: (0, lch(s), 0)),
                      pl.BlockSpec((1, nb * HEADS, hv, hk), lambda s: (s, 0, 0, 0)),
                      pl.BlockSpec((nb, c, dk_), lambda s, ch=ch: (0, ch(s), 0))]
        out_shape += [jax.ShapeDtypeStruct((nb, s_len, dv_), F32),
                      jax.ShapeDtypeStruct((ns, nb * HEADS, hv, hk), F32),
                      jax.ShapeDtypeStruct((nb, l_len, dk_), F32)]
    return pl.pallas_call(
        body, name="gla_fwd", grid=(ns,), in_specs=in_specs, out_specs=tuple(out_specs), out_shape=tuple(out_shape),
        scratch_shapes=[pltpu.VMEM((2 * nb * HEADS, hv, hk), F32)],
        compiler_params=_params())(pb3, pb3, pv3, g3, pb3, pb3, pv3, g3)


def _gla_bwd(pb3, pv3, do3, fwd_saved, nb, s_len, c_len, dk_, dv_):
    c = CHUNK
    nl, nc = s_len // c, c_len // c
    ns = nl + nc
    hk, hv = dk_ // HEADS, dv_ // HEADS
    l_len = pb3.shape[1]
    scale = hk ** -0.5
    mid = c // 2
    zs_f, b_f, zs_b, b_b = fwd_saved

    def body(*refs):
        ins, outs, dz_scr = refs[:12], refs[12:20], refs[20]
        s = pl.program_id(0)
        step = ns - 1 - s

        @pl.when(s == 0)
        def _():
            dz_scr[...] = jnp.zeros_like(dz_scr)

        lat = step >= nc
        qs = jnp.where(lat, scale, 0.0)
        dmul = jnp.where(lat, 1.0, 0.0)
        for di, rev in enumerate((False, True)):
            q_ref, k_ref, v_ref, b_ref, do_ref, zs_ref = ins[6 * di:6 * di + 6]
            dq_ref, dk_ref, dv_ref, dg_ref = outs[4 * di:4 * di + 4]
            mask, mask_t = _chunk_masks(c, rev)
            mt_bf = mask_t.astype(BF16)
            far = 0 if rev else c - 1
            far_row = lax.broadcasted_iota(jnp.int32, (c, hk), 0) == far
            for b in range(nb):
                db_parts = []
                for h in range(HEADS):
                    ks, vs = slice(h * hk, (h + 1) * hk), slice(h * hv, (h + 1) * hv)
                    zi = (di * nb + b) * HEADS + h
                    v = v_ref[b, :, vs]
                    d_o = do_ref[b, :, vs] * dmul
                    t = _chunk_terms(q_ref[b, :, ks] * qs, k_ref[b, :, ks], b_ref[b, :, ks], far, mid)
                    qem, kim, qe, kd = t["qem"], t["kim"], t["qe"], t["kd"]
                    a_t = jnp.where(mask_t, _mm_nt(kim, qem), 0.0)
                    d_a = jnp.where(mask, _mm_nt(d_o, v), 0.0)
                    d_at = jnp.where(mask_t, _mm_nt(v, d_o), 0.0)
                    z = zs_ref[0, b * HEADS + h]
                    dzn = dz_scr[zi]
                    dv_ref[b, :, vs] = (_mm(a_t, d_o) + _mm_nt(kd, dzn)).astype(dv_ref.dtype)
                    dqem = _mm(d_a, kim)
                    dkim = _mm(d_at, qem)
                    dqe = _mm(d_o, z)
                    dkd = _mm(v, dzn)
                    ddec = jnp.sum(z * dzn, axis=0, keepdims=True)
                    dz_scr[zi] = dzn * t["dec"] + _mm_tn(d_o, qe)
                    dq_ref[b, :, ks] = ((dqem * t["em"] + dqe * t["e"]) * qs).astype(dq_ref.dtype)
                    dk_ref[b, :, ks] = (dkim * t["eim"] + dkd * t["ed"]).astype(dk_ref.dtype)
                    db = dqem * qem - dkim * kim + dqe * qe - dkd * kd
                    extra = jnp.sum(dkd * kd, axis=0, keepdims=True) + ddec * t["dec"]
                    db_parts.append(db + jnp.where(far_row, extra, 0.0))
                dg_ref[b] = _tri_mm(mt_bf, jnp.concatenate(db_parts, axis=1))

    in_specs, out_specs, out_shape, args = [], [], [], []
    for di, rev in enumerate((False, True)):
        ch = lambda s, rev=rev: _scan_chunk(ns - 1 - s, nl, nc, rev)
        lch = lambda s, rev=rev: _scan_lat_chunk(ns - 1 - s, nl, nc, rev)
        in_specs += [pl.BlockSpec((nb, c, dk_), lambda s, ch=ch: (0, ch(s), 0)),
                     pl.BlockSpec((nb, c, dk_), lambda s, ch=ch: (0, ch(s), 1)),
                     pl.BlockSpec((nb, c, dv_), lambda s, ch=ch: (0, ch(s), 0)),
                     pl.BlockSpec((nb, c, dk_), lambda s, ch=ch: (0, ch(s), 0)),
                     pl.BlockSpec((nb, c, dv_), lambda s, lch=lch: (0, lch(s), 0)),
                     pl.BlockSpec((1, nb * HEADS, hv, hk), lambda s: (ns - 1 - s, 0, 0, 0))]
        args += [pb3, pb3, pv3, (b_b if rev else b_f), do3, (zs_b if rev else zs_f)]
        for w, dt in ((dk_, BF16), (dk_, BF16), (dv_, BF16), (dk_, F32)):
            out_specs.append(pl.BlockSpec((nb, c, w), lambda s, ch=ch: (0, ch(s), 0)))
            out_shape.append(jax.ShapeDtypeStruct((nb, l_len, w), dt))
    return pl.pallas_call(
        body, name="gla_bwd", grid=(ns,), in_specs=in_specs, out_specs=tuple(out_specs), out_shape=tuple(out_shape),
        scratch_shapes=[pltpu.VMEM((2 * nb * HEADS, hv, hk), F32)],
        compiler_params=_params())(*args)


def _tail(a1, pa, o_f, o_b, x2, tgt, mod, wc, wg, wo, ln_g, ln_b, gn_t, fg, nb, tm, n_split):
    tl, d = x2.shape
    nt = tl // tm
    per_ex = nt // nb
    hv = d // HEADS
    nrow = mod.shape[0]

    def part(shared, a1_ref, z_ref, r_ref, mc_ref, mg_ref, of_ref, ob_ref, x_ref, t_ref,
             dp_ref, da1_ref, do_ref, gx_ref, mrg_ref, dmo_ref, yci_ref, dyc_ref, ogi_ref, dyg_ref, sm_ref):
        bidx, gate, lng, lnb, fgv, gn, wc_, wg_, wo_ = shared

        a1v = a1_ref[...]
        mu = jnp.mean(a1v, axis=-1, keepdims=True)
        xc = a1v - mu
        rs = lax.rsqrt(jnp.mean(xc * xc, axis=-1, keepdims=True) + EPS)
        xh = xc * rs
        a2 = xh * lng + lnb
        s2 = _sigmoid(a2)
        a3 = a2 * s2
        zv = z_ref[...]
        sz = _sigmoid(zv)
        siluz = zv * sz
        ycin = a3 * siluz
        yconv = _mm(ycin, wc_)

        o = of_ref[...] + ob_ref[...]
        ohat_parts, rn_parts = [], []
        for h in range(HEADS):
            oh = o[:, h * hv:(h + 1) * hv]
            rn = lax.rsqrt(jnp.mean(oh * oh, axis=-1, keepdims=True) + EPS)
            ohat_parts.append(oh * rn)
            rn_parts.append(rn)
        ohat = jnp.concatenate(ohat_parts, axis=1)
        on = ohat * gn
        rv = r_ref[...]
        sr = _sigmoid(rv)
        silur = rv * sr
        ogin = on * silur
        ygla = _mm(ogin, wg_)

        sc = _sigmoid(mc_ref[...])
        sg = _sigmoid(mg_ref[...])
        merged = sc * yconv + sg * ygla
        mo = _mm(merged, wo_)
        hn = x_ref[...] + gate * mo
        rf = lax.rsqrt(jnp.mean(hn * hn, axis=-1, keepdims=True) + EPS)
        yh = hn * rf
        err = yh * fgv - t_ref[...]
        loss_part = 0.5 * jnp.sum(err * err) * (1.0 / d)

        dy = err * (1.0 / d)
        dfg = jnp.sum(dy * yh, axis=0, keepdims=True)
        dyh = dy * fgv
        dhn = rf * (dyh - yh * jnp.mean(dyh * yh, axis=-1, keepdims=True))
        gx_ref[...] = dhn
        dgate = jnp.sum(dhn * mo, axis=0, keepdims=True)
        dmo = gate * dhn
        dmerged = _mm_nt(dmo, wo_)
        dyconv = dmerged * sc
        dygla = dmerged * sg
        dp_ref[:, 2 * d:3 * d] = (dmerged * yconv * sc * (1.0 - sc)).astype(BF16)
        dp_ref[:, 3 * d:4 * d] = (dmerged * ygla * sg * (1.0 - sg)).astype(BF16)
        dycin = _mm_nt(dyconv, wc_)
        dogin = _mm_nt(dygla, wg_)
        mrg_ref[...] = merged.astype(BF16)
        dmo_ref[...] = dmo.astype(BF16)
        yci_ref[...] = ycin.astype(BF16)
        dyc_ref[...] = dyconv.astype(BF16)
        ogi_ref[...] = ogin.astype(BF16)
        dyg_ref[...] = dygla.astype(BF16)

        da3 = dycin * siluz
        dp_ref[:, 0:d] = (dycin * a3 * _dsilu(zv, sz)).astype(BF16)
        da2 = da3 * _dsilu(a2, s2)
        dlng = jnp.sum(da2 * xh, axis=0, keepdims=True)
        dlnb = jnp.sum(da2, axis=0, keepdims=True)
        dxh = da2 * lng
        da1_ref[...] = rs * (dxh - jnp.mean(dxh, axis=-1, keepdims=True)
                             - xh * jnp.mean(dxh * xh, axis=-1, keepdims=True))

        don = dogin * silur
        dp_ref[:, d:2 * d] = (dogin * on * _dsilu(rv, sr)).astype(BF16)
        dgn = jnp.sum(don * ohat, axis=0, keepdims=True)
        dyn = don * gn
        for h in range(HEADS):
            vs = slice(h * hv, (h + 1) * hv)
            oh_hat = ohat_parts[h]
            dh = dyn[:, vs]
            do_ref[:, vs] = (rn_parts[h] * (dh - oh_hat * jnp.mean(dh * oh_hat, axis=-1, keepdims=True))
                             ).astype(BF16)

        sm_ref[0:1, :] += dfg
        sm_ref[1:2, :] += dlng
        sm_ref[2:3, :] += dlnb
        sm_ref[3:4, :] += dgn
        sm_ref[4:5, :] += jnp.zeros((1, d), F32) + loss_part
        for b in range(nb):
            sm_ref[8 + b:9 + b, :] += jnp.where(bidx == b, dgate, 0.0)

    def body(*refs):
        mod_ref, wc_ref, wg_ref, wo_ref, lng_ref, lnb_ref, gn_ref, fg_ref = refs[9:17]
        sm_ref = refs[27]
        i = pl.program_id(0)

        @pl.when(i == 0)
        def _():
            sm_ref[...] = jnp.zeros_like(sm_ref)

        bidx = i // per_ex
        shared = (bidx, _rowsel(mod_ref[...], bidx, nb)[:, 2 * d:3 * d], lng_ref[...], lnb_ref[...], fg_ref[...],
                  jnp.concatenate([gn_ref[...]] * HEADS, axis=1), wc_ref[...], wg_ref[...], wo_ref[...])
        rows_per = tm // n_split
        for p in range(n_split):
            rows = pl.ds(p * rows_per, rows_per)
            part(shared, *[r.at[rows] for r in refs[0:9]], *[r.at[rows] for r in refs[17:27]], sm_ref)

    row = pl.BlockSpec((tm, d), lambda i: (i, 0))
    pcol = lambda blk: pl.BlockSpec((tm, d), lambda i: (i, blk))
    full = lambda arr: pl.BlockSpec(arr.shape, lambda i: (0,) * arr.ndim)
    bfo = jax.ShapeDtypeStruct((tl, d), BF16)
    f32o = jax.ShapeDtypeStruct((tl, d), F32)
    return pl.pallas_call(
        body, name="tail", grid=(nt,),
        in_specs=[row, pcol(2), pcol(3), pcol(4), pcol(5), row, row, row, row, full(mod), full(wc), full(wg),
                  full(wo), full(ln_g), full(ln_b), full(gn_t), full(fg)],
        out_specs=(pl.BlockSpec((tm, 4 * d), lambda i: (i, 0)), row, row, row, row, row, row, row, row, row,
                   pl.BlockSpec((16, d), lambda i: (0, 0))),
        out_shape=(jax.ShapeDtypeStruct((tl, 4 * d), BF16), f32o, bfo, f32o, bfo, bfo, bfo, bfo, bfo, bfo,
                   jax.ShapeDtypeStruct((16, d), F32)),
        compiler_params=_params())(a1, pa, pa, pa, pa, o_f, o_b, x2, tgt, mod, wc, wg, wo, ln_g, ln_b, gn_t, fg)


def _local_step(x, c, ctx, tgt, c_ctx, ada_w8, ada_b, norm_g, w_a, b_a, w_b, b_b, conv_w8, conv_b, ln_g, ln_b,
                up2, bias2, gla_norm_g, final_norm_g, proj, on_grads=None, on_du_a1=None):
    nb, s_len, d = x.shape
    c_len = ctx.shape[1]
    dk_, dv_ = d // 2, d
    tl, tc = nb * s_len, nb * c_len
    nbw = 2 * dk_ + dv_ + LANE
    tm = math.gcd(256, c_len)
    tiles = _Tiles(nb, s_len, c_len, tm, 2)
    l_len = tiles.rows_per_ex
    t_all = nb * l_len
    x2, ctx2, tgt2 = x.reshape(tl, d), ctx.reshape(tc, d), tgt.reshape(tl, d)

    cv = jnp.zeros((8, d), F32).at[0:nb].set(c).at[nb].set(c_ctx.reshape(d))
    mod = _ada_fwd(cv, ada_w8, ada_b)
    u = _norm_fwd(x2, ctx2, mod, norm_g, tiles)
    u3 = u.reshape(nb, l_len, d)
    tma = math.gcd(1024, s_len)
    pa = _matmul_bias("inproj_a", u3, w_a, b_a, s_len, tma, _tile(6 * d, 2048))
    tmb = math.gcd(1024, t_all)
    pb, pv, g_all = _inproj_b(u, w_b, b_b, up2, bias2, tmb, dk_, dv_)

    a1 = _conv_fwd(pa, conv_w8, conv_b, nb, s_len)
    lr_blk = (2 * dk_) // LANE
    pb3, pv3 = pb.reshape(nb, l_len, 2 * dk_ + LANE), pv.reshape(nb, l_len, dv_)
    o_f, zs_f, b_f, o_b, zs_b, b_b2 = _gla_fwd(pb3, pv3, g_all.reshape(nb, l_len, 2 * dk_), nb, s_len, c_len,
                                               dk_, dv_)

    conv_proj, gla_proj, w_out = proj(a1) if callable(proj) else proj
    tt = math.gcd(256, s_len)
    (dp_a2, da1, d_o, gx1, merged, dmo, ycin, dyconv, ogin, dygla, small) = _tail(
        a1, pa, o_f.reshape(tl, dv_), o_b.reshape(tl, dv_), x2, tgt2, mod, conv_proj, gla_proj, w_out, ln_g, ln_b,
        gla_norm_g, final_norm_g, nb, tt, 2)

    lat3 = lambda a: a.reshape(nb, s_len, a.shape[-1])
    tnw = _tile(d, 512)
    d_w_out, _ = _matmul_tn_whole("dw_out", lat3(merged), lat3(dmo), s_len, tnw, False)
    d_conv_proj, _ = _matmul_tn_whole("dw_conv_proj", lat3(ycin), lat3(dyconv), s_len, tnw, False)
    d_gla_proj, _ = _matmul_tn_whole("dw_gla_proj", lat3(ogin), lat3(dygla), s_len, tnw, False)

    dp_a1, d_conv_w8, d_conv_b = _conv_bwd(pa, da1, conv_w8, nb, s_len)
    gl = _gla_bwd(pb3, pv3, d_o.reshape(nb, s_len, dv_), (zs_f, b_f, zs_b, b_b2), nb, s_len, c_len, dk_, dv_)
    gl = [g_.reshape(t_all, g_.shape[-1]) for g_ in gl]
    dp_b, d_up2, d_bias2 = _decay_bwd(pb, up2, bias2, gl[0:4], gl[4:8], tiles, lr_blk, dk_, dv_)

    dw_a1, db_a1 = _matmul_tn_whole("dw_a1", u3, lat3(dp_a1), s_len, tnw, True)
    dw_a2, db_a2 = _matmul_tn_whole("dw_a2", u3, lat3(dp_a2), s_len, tnw, True)
    dw_b, db_b = _matmul_tn("dw_b", u, dp_b, t_all, tmb, nbw)
    grads = dict(w_a1=dw_a1, w_a2=dw_a2, w_b=dw_b, conv_w8=d_conv_w8, conv_proj=d_conv_proj, up2=d_up2,
                 gla_proj=d_gla_proj, w_out=d_w_out)

    tka = _tile(2 * d, 2048)
    du_a1 = _matmul_nt("du_a1", dp_a1, w_a, 0, tma, tka, after=on_grads(grads) if on_grads else ())
    du_a2 = _matmul_nt("du_a2", dp_a2, w_a, (2 * d) // tka, tma, tka, after=on_du_a1(du_a1) if on_du_a1 else ())
    du_b = _matmul_nt("du_b", dp_b, w_b, 0, tmb, nbw)
    grad_x2, dmod_ss, d_norm_g = _norm_bwd(x2, ctx2, mod, norm_g, [du_a1, du_a2], du_b, gx1, tiles)
    d_ada_w8, d_ada_b, d_cv = _ada_bwd(cv, ada_w8, dmod_ss, small, nb)

    return dict(
        grads, grad_x=grad_x2.reshape(nb, s_len, d), small=small, cv=d_cv, ada_w8=d_ada_w8, ada_b=d_ada_b,
        norm_g=d_norm_g, b_a1=db_a1, b_a2=db_a2, b_b=db_b, conv_b=d_conv_b, bias2=d_bias2)


def _regroup_pieces(d, r, wshard):
    cb = d // N_DEV
    segs = []
    for j in range(N_DEV):
        segs.append((j * cb, cb, 0, 2 * j * cb))
    for j in range(N_DEV):
        segs.append((d + j * cb, cb, 0, (2 * j + 1) * cb))
    segs += [(2 * d, d, 0, 2 * d), (3 * d, 2 * d + 2 * r, 1, 0), (5 * d + 2 * r, 3 * d, 0, 3 * d)]
    pieces = []
    for o0, w, dst, d0 in segs:
        lo = o0
        while lo < o0 + w:
            j = lo // wshard
            hi = min(o0 + w, (j + 1) * wshard)
            pieces.append((j, lo - j * wshard, hi - lo, dst, d0 + lo - o0))
            lo = hi
    return pieces


def _regroup(o, d, r):
    n_in = 8 * d + 2 * r
    parts = ([], [])
    for _, s0, n, dst, _ in sorted(_regroup_pieces(d, r, n_in), key=lambda p: (p[3], p[4])):
        parts[dst].append(o[..., s0:s0 + n])
    pad = jnp.zeros(o.shape[:-1] + (LANE - 2 * r,), o.dtype)
    return jnp.concatenate(parts[0], axis=-1), jnp.concatenate(parts[1] + [pad], axis=-1)


def _unshard_w_in(g_win, d, r, after=()):
    n_sh, _, ws = g_win.shape
    nbw = 2 * d + LANE
    pieces = _regroup_pieces(d, r, ws)
    tr = math.gcd(d, 256)

    def body(g_ref, *rest):
        a_ref, b_ref = rest[len(after):]
        dsts = (a_ref, b_ref)
        for j, s0, n, dst, d0 in pieces:
            dsts[dst][:, pl.ds(d0, n)] = g_ref[j, :, pl.ds(s0, n)]
        b_ref[:, pl.ds(2 * d + 2 * r, LANE - 2 * r)] = jnp.zeros((tr, LANE - 2 * r), b_ref.dtype)

    return pl.pallas_call(
        body, name="unshard_w_in", grid=(d // tr,),
        in_specs=[pl.BlockSpec((n_sh, tr, ws), lambda i: (0, i, 0))] + [_ANY] * len(after),
        out_specs=(pl.BlockSpec((tr, 6 * d), lambda i: (i, 0)), pl.BlockSpec((tr, nbw), lambda i: (i, 0))),
        out_shape=(jax.ShapeDtypeStruct((d, 6 * d), g_win.dtype), jax.ShapeDtypeStruct((d, nbw), g_win.dtype)),
        compiler_params=_params())(g_win, *after)


def _reshard_w_in(dwt_a1, dwt_a2, dwt_b, d, r):
    ws = (8 * d + 2 * r) // N_DEV
    pieces = _regroup_pieces(d, r, ws)
    tc = math.gcd(d, 256)

    def body(a1_ref, a2_ref, b_ref, o_ref):
        for j, s0, n, dst, d0 in pieces:
            if dst == 1:
                src = b_ref[pl.ds(d0, n), :]
            elif d0 < 2 * d:
                src = a1_ref[pl.ds(d0, n), :]
            else:
                src = a2_ref[pl.ds(d0 - 2 * d, n), :]
            o_ref[j, pl.ds(s0, n), :] = src

    col = lambda h: pl.BlockSpec((h, tc), lambda i: (0, i))
    return pl.pallas_call(
        body, name="reshard_w_in", grid=(d // tc,),
        in_specs=[col(2 * d), col(4 * d), col(2 * d + LANE)],
        out_specs=pl.BlockSpec((N_DEV, ws, tc), lambda i: (0, 0, i)),
        out_shape=jax.ShapeDtypeStruct((N_DEV, ws, d), dwt_b.dtype),
        compiler_params=_params())(dwt_a1, dwt_a2, dwt_b)


_SMALL = ("c_ctx", "ada_b", "norm_g", "b_in", "conv_b", "conv_ln_g", "conv_ln_b", "decay_bias_fwd",
          "decay_bias_bwd", "gla_norm_g", "final_norm_g")


def _small_layout(d, r):
    sizes = dict(c_ctx=d, ada_b=3 * d, norm_g=d, b_in=8 * d + 2 * r, conv_b=d, conv_ln_g=d, conv_ln_b=d,
                 decay_bias_fwd=d // 2, decay_bias_bwd=d // 2, gla_norm_g=d // HEADS, final_norm_g=d, loss=1)
    table, off = {}, 0
    for name in _SMALL + ("loss",):
        table[name] = (off, sizes[name])
        off += -(-sizes[name] // LANE) * LANE
    return table, off


def _pack_small(g, nb, d, r):
    table, width = _small_layout(d, r)
    hv = d // HEADS
    pieces = _regroup_pieces(d, r, 8 * d + 2 * r)
    names = ("small", "cv", "ada_b", "norm_g", "b_a1", "b_a2", "b_b", "conv_b", "bias2")

    def body(sm, cv, ab, ng, ba1, ba2, bb, cvb, b2, o_ref):
        o_ref[...] = jnp.zeros_like(o_ref)

        def put(name, val):
            off, n = table[name]
            o_ref[:, pl.ds(off, n)] = val

        put("c_ctx", cv[nb:nb + 1, :])
        put("ada_b", ab[...])
        put("norm_g", ng[...])
        off_b = table["b_in"][0]
        for _, s0, n, dst, d0 in pieces:
            if dst == 1:
                src = bb[:, pl.ds(d0, n)]
            elif d0 < 2 * d:
                src = ba1[:, pl.ds(d0, n)]
            else:
                src = ba2[:, pl.ds(d0 - 2 * d, n)]
            o_ref[:, pl.ds(off_b + s0, n)] = src
        put("conv_b", cvb[...])
        put("conv_ln_g", sm[1:2, :])
        put("conv_ln_b", sm[2:3, :])
        put("decay_bias_fwd", b2[:, 0:d // 2])
        put("decay_bias_bwd", b2[:, d // 2:d])
        gn = sm[3:4, 0:hv]
        for h in range(1, HEADS):
            gn = gn + sm[3:4, h * hv:(h + 1) * hv]
        put("gla_norm_g", gn)
        put("final_norm_g", sm[0:1, :])
        put("loss", sm[4:5, 0:1])

    return pl.pallas_call(body, name="pack_small", out_shape=jax.ShapeDtypeStruct((1, width), F32),
                          compiler_params=_params())(*[g[k] for k in names])


def _small_adam(parts, ws, ms, vs, d, r):
    table, width = _small_layout(d, r)
    n_parts = parts.shape[0]
    k = len(_SMALL)

    def body(p_ref, *refs):
        w_refs, m_refs, v_refs = refs[0:k], refs[k:2 * k], refs[2 * k:3 * k]
        outs = refs[3 * k:]
        tot = p_ref[0]
        for i in range(1, n_parts):
            tot = tot + p_ref[i]
        for i, name in enumerate(_SMALL):
            off, n = table[name]
            g = tot[:, off:off + n]
            outs[i][...] = g
            outs[k + i][...], outs[2 * k + i][...], outs[3 * k + i][...] = _adamw(
                g, w_refs[i][...], m_refs[i][...], v_refs[i][...])
        off, _ = table["loss"]
        outs[4 * k][...] = tot[:, off:off + 1]

    shapes = [jax.ShapeDtypeStruct(w.shape, F32) for w in ws]
    res = pl.pallas_call(body, name="small_adam", out_shape=tuple(shapes * 4 + [jax.ShapeDtypeStruct((1, 1), F32)]),
                         compiler_params=_params())(parts, *ws, *ms, *vs)
    return res[0:k], res[k:2 * k], res[2 * k:3 * k], res[3 * k:4 * k], res[4 * k]


def _mesh_pos():
    return lax.axis_index("x"), lax.axis_index("y"), lax.axis_index("c")


def _all_gather(arrs):
    n = len(arrs)
    ns = 9
    split = [a.ndim == 2 and a.shape[0] % 32 == 0 for a in arrs]

    def body(*refs):
        ins, outs = refs[:n], refs[n:2 * n]
        send_sems, recv_sems, local_sems = refs[2 * n:]
        x, y, c = _mesh_pos()
        me, sibling = (x, y, c), (x, y, 1 - c)
        xn, yn, dg = (1 - x, y, c), (x, 1 - y, c), (1 - x, 1 - y, c)
        other = lambda pos: (pos[0], pos[1], 1 - c)

        def slot(a, pos, half):
            ref = outs[a].at[4 * pos[0] + 2 * pos[1] + pos[2]]
            if half is None:
                return ref
            rows = arrs[a].shape[0] // 2
            return ref.at[pl.ds(half * rows, rows)]

        def copy(a, k, block, to, src=None, half=None):
            dst = slot(a, block, half)
            return pltpu.make_async_remote_copy(
                src_ref=dst if src is None else src, dst_ref=dst,
                send_sem=send_sems.at[ns * a + k], recv_sem=recv_sems.at[ns * a + k],
                device_id=to, device_id_type=MESH)

        h0 = lambda a: 0 if split[a] else None
        mine = [pltpu.make_async_copy(ins[a], slot(a, me, None), local_sems.at[a]) for a in range(n)]
        for cp in mine:
            cp.start()
        sent = []
        for a in range(n):
            sent += [copy(a, 0, me, sibling, src=ins[a]), copy(a, 1, me, xn, src=ins[a]),
                     copy(a, 2, me, yn, src=ins[a])]
        for cp in sent:
            cp.start()

        def pass_on(cp):
            cp.start()
            sent.append(cp)

        for a in range(n):
            copy(a, 1, xn, me).wait_recv()
            pass_on(copy(a, 3, xn, sibling))
            pass_on(copy(a, 4, xn, yn, half=h0(a)))
        for a in range(n):
            copy(a, 2, yn, me).wait_recv()
            pass_on(copy(a, 5, yn, sibling))
            if split[a]:
                pass_on(copy(a, 6, yn, xn, half=1))
        for a in range(n):
            copy(a, 4, dg, me, half=h0(a)).wait_recv()
            pass_on(copy(a, 7, dg, sibling, half=h0(a)))
            if split[a]:
                copy(a, 6, dg, me, half=1).wait_recv()
                pass_on(copy(a, 8, dg, sibling, half=1))
        for a in range(n):
            copy(a, 0, sibling, me).wait_recv()
            copy(a, 3, other(xn), me).wait_recv()
            copy(a, 5, other(yn), me).wait_recv()
            copy(a, 7, other(dg), me, half=h0(a)).wait_recv()
            if split[a]:
                copy(a, 8, other(dg), me, half=1).wait_recv()
        for cp in sent:
            cp.wait_send()
        for cp in mine:
            cp.wait()

    return pl.pallas_call(
        body, name="all_gather",
        out_shape=tuple(jax.ShapeDtypeStruct((N_DEV,) + a.shape, a.dtype) for a in arrs),
        in_specs=[_ANY] * n, out_specs=tuple([_ANY] * n),
        scratch_shapes=[pltpu.SemaphoreType.DMA((ns * n,)), pltpu.SemaphoreType.DMA((ns * n,)),
                        pltpu.SemaphoreType.DMA((n,))],
    )(*arrs)


def _exchange_sibling(arrs):
    n = len(arrs)

    def body(*refs):
        ins, outs = refs[:n], refs[n:2 * n]
        send_sems, recv_sems = refs[2 * n:]
        x, y, c = _mesh_pos()
        copies = [pltpu.make_async_remote_copy(
            src_ref=ins[a].at[2 * k + (1 - c)], dst_ref=outs[a].at[k],
            send_sem=send_sems.at[4 * a + k], recv_sem=recv_sems.at[4 * a + k],
            device_id=(x, y, 1 - c), device_id_type=MESH) for a in range(n) for k in range(4)]
        for cp in copies:
            cp.start()
        for cp in copies:
            cp.wait_recv()
        for cp in copies:
            cp.wait_send()

    return pl.pallas_call(
        body, name="grad_exchange_sibling",
        out_shape=tuple(jax.ShapeDtypeStruct((4,) + a.shape[1:], a.dtype) for a in arrs),
        in_specs=[_ANY] * n, out_specs=tuple([_ANY] * n),
        scratch_shapes=[pltpu.SemaphoreType.DMA((4 * n,)), pltpu.SemaphoreType.DMA((4 * n,))],
    )(*arrs)


def _elementwise_tile(r, cdim, cols=2 * LANE):
    if r % 8 == 0 and r > 256:
        return math.gcd(r, 256), cdim
    if r > 256 and cdim % cols == 0:
        return r, cols
    return r, cdim


def _pair_sum(name, mine, theirs):
    _, r, cdim = mine.shape
    tr, tc = _elementwise_tile(r, cdim)

    def body(m_ref, t_ref, o_ref):
        c = lax.axis_index("c")
        own = jnp.where(c == 0, m_ref[:, 0].astype(F32), m_ref[:, 1].astype(F32))
        o_ref[...] = (own + t_ref[...].astype(F32)).astype(o_ref.dtype)

    return pl.pallas_call(
        body, name=name, grid=(r // tr, cdim // tc),
        in_specs=[pl.BlockSpec((4, 2, tr, tc), lambda i, j: (0, 0, i, j)),
                  pl.BlockSpec((4, tr, tc), lambda i, j: (0, i, j))],
        out_specs=pl.BlockSpec((4, tr, tc), lambda i, j: (0, i, j)),
        out_shape=jax.ShapeDtypeStruct((4, r, cdim), mine.dtype),
        compiler_params=_params())(mine.reshape(4, 2, r, cdim), theirs)


def _pair_sum_small(mines, theirs):
    n = len(mines)

    def body(*refs):
        c = lax.axis_index("c")
        for i in range(n):
            m_ref, t_ref, o_ref = refs[i], refs[n + i], refs[2 * n + i]
            own = jnp.where(c == 0, m_ref[:, 0].astype(F32), m_ref[:, 1].astype(F32))
            o_ref[...] = (own + t_ref[...].astype(F32)).astype(o_ref.dtype)

    return pl.pallas_call(
        body, name="pair_sum_small_weights",
        out_shape=tuple(jax.ShapeDtypeStruct(t.shape, m.dtype) for m, t in zip(mines, theirs)),
        compiler_params=_params())(*[m.reshape((4, 2) + m.shape[1:]) for m in mines], *theirs)


_HBM = pl.BlockSpec(memory_space=pltpu.HBM)
_SEM = pl.BlockSpec(memory_space=pltpu.SEMAPHORE)


def _copies_start(name, srcs, lands, make_copies, n_sems):
    n, m = len(srcs), len(lands)

    def body(*refs):
        ins = refs[:n + m]
        send_sems, recv_sems = refs[n + m], refs[n + m + 1]
        for cp in make_copies(ins[:n], ins[n:], send_sems, recv_sems):
            cp.start()
        refs[-1][...] = jnp.zeros_like(refs[-1])

    res = pl.pallas_call(
        body, name=name,
        out_shape=(pltpu.SemaphoreType.DMA((n_sems,)), pltpu.SemaphoreType.DMA((n_sems,)),
                   *[pltpu.HBM(a.shape, a.dtype) for a in (*srcs, *lands)], jax.ShapeDtypeStruct((8, LANE), F32)),
        in_specs=[_HBM] * (n + m),
        out_specs=(_SEM, _SEM, *[_HBM] * (n + m), pl.BlockSpec(memory_space=pltpu.VMEM)),
        input_output_aliases={i: 2 + i for i in range(n + m)},
        compiler_params=pltpu.CompilerParams(has_side_effects=pltpu.SideEffectType.DATAFLOW_SIDE_EFFECTING),
    )(*[pltpu.with_memory_space_constraint(a, pltpu.HBM) for a in (*srcs, *lands)])
    return res[0], res[1], res[2:2 + n], res[2 + n:2 + n + m], res[-1]


def _copies_wait(name, started, after, make_copies):
    send_sems, recv_sems, srcs, lands, _ = started
    n, m = len(srcs), len(lands)

    def body(*refs):
        ins = refs[:n + m]
        for cp in make_copies(ins[:n], ins[n:], refs[n + m], refs[n + m + 1]):
            cp.wait_send()
            cp.wait_recv()

    res = pl.pallas_call(
        body, name=name,
        out_shape=tuple(pltpu.HBM(a.shape, a.dtype) for a in (*srcs, *lands)),
        in_specs=[_HBM] * (n + m) + [_SEM, _SEM] + [_ANY] * len(after),
        out_specs=tuple([_HBM] * (n + m)),
        input_output_aliases={i: i for i in range(n + m)},
        compiler_params=pltpu.CompilerParams(has_side_effects=pltpu.SideEffectType.DATAFLOW_SIDE_EFFECTING),
    )(*srcs, *lands, send_sems, recv_sems, *after)
    return res[:n], res[n:]


def _gather_copies(srcs, lands, send_sems, recv_sems):
    x, y, c = _mesh_pos()
    me_i = 4 * x + 2 * y + c
    copies = []
    for rel in range(1, N_DEV):
        peer = (1 - x if rel & 4 else x, 1 - y if rel & 2 else y, 1 - c if rel & 1 else c)
        for a in range(len(srcs)):
            copies.append(pltpu.make_async_remote_copy(
                src_ref=srcs[a], dst_ref=lands[a].at[me_i], send_sem=send_sems.at[7 * a + rel - 1],
                recv_sem=recv_sems.at[7 * a + rel - 1], device_id=peer, device_id_type=MESH))
    return copies


def _sibling_copies(srcs, lands, send_sems, recv_sems):
    x, y, c = _mesh_pos()
    return [pltpu.make_async_remote_copy(
        src_ref=srcs[a].at[2 * k + (1 - c)], dst_ref=lands[a].at[k], send_sem=send_sems.at[4 * a + k],
        recv_sem=recv_sems.at[4 * a + k], device_id=(x, y, 1 - c), device_id_type=MESH)
        for a in range(len(srcs)) for k in range(4)]


def _chip_copies(srcs, lands, send_sems, recv_sems):
    x, y, c = _mesh_pos()
    my_chip = 2 * x + y
    copies = []
    for rel in range(1, 4):
        px = 1 - x if rel & 2 else x
        py = 1 - y if rel & 1 else y
        for a in range(len(srcs)):
            copies.append(pltpu.make_async_remote_copy(
                src_ref=srcs[a].at[2 * px + py], dst_ref=lands[a].at[my_chip], send_sem=send_sems.at[3 * a + rel - 1],
                recv_sem=recv_sems.at[3 * a + rel - 1], device_id=(px, py, c), device_id_type=MESH))
    return copies


def _sum_adam(name, parts, w, m, v, own=None):
    unit_mid = w.ndim == 3
    _, r, cdim = parts.shape
    n_parts = parts.shape[0]
    tr, tc = _elementwise_tile(r, cdim, (4 if unit_mid else 2) * LANE)
    extra = [] if own is None else [own]

    def body(p_ref, *refs):
        w_ref, m_ref, v_ref, g_ref, d_ref, nm_ref, nv_ref = refs[len(extra):]
        if own is None:
            part = lambda k: p_ref[k].astype(F32)
        else:
            my_chip = 2 * lax.axis_index("x") + lax.axis_index("y")
            part = lambda k: jnp.where(my_chip == k, refs[0][k], p_ref[k]).astype(F32)
        g = part(0)
        for k in range(1, n_parts):
            g = g + part(k)
        if unit_mid:
            g = g.reshape(tr, 1, tc)
        g_ref[...] = g
        d_ref[...], nm_ref[...], nv_ref[...] = _adamw(g, w_ref[...], m_ref[...], v_ref[...])

    blk = (pl.BlockSpec((tr, 1, tc), lambda i, j: (i, 0, j)) if unit_mid
           else pl.BlockSpec((tr, tc), lambda i, j: (i, j)))
    o = jax.ShapeDtypeStruct(w.shape, F32)
    return pl.pallas_call(
        body, name=name, grid=(r // tr, cdim // tc),
        in_specs=[pl.BlockSpec((n_parts, tr, tc), lambda i, j: (0, i, j))] * (1 + len(extra)) + [blk, blk, blk],
        out_specs=(blk, blk, blk, blk), out_shape=(o, o, o, o),
        compiler_params=_params())(parts, *extra, w, m, v)


def _sum_adam_small(items):
    n = len(items)

    def body(*refs):
        my_chip = 2 * lax.axis_index("x") + lax.axis_index("y")
        for i in range(n):
            p_ref, own_ref, w_ref, m_ref, v_ref = refs[5 * i:5 * i + 5]
            g_ref, d_ref, nm_ref, nv_ref = refs[5 * n + 4 * i:5 * n + 4 * i + 4]
            g = None
            for k in range(p_ref.shape[0]):
                part = jnp.where(my_chip == k, own_ref[k], p_ref[k]).astype(F32)
                g = part if g is None else g + part
            g_ref[...] = g
            d_ref[...], nm_ref[...], nv_ref[...] = _adamw(g, w_ref[...], m_ref[...], v_ref[...])

    out_shape = tuple(jax.ShapeDtypeStruct(it[2].shape, F32) for it in items for _ in range(4))
    res = pl.pallas_call(body, name="adam_small_weights", out_shape=out_shape,
                         compiler_params=_params())(*[a for it in items for a in it])
    return [res[4 * i:4 * i + 4] for i in range(n)]


_WEIGHTS = ("c_ctx", "ada_w", "ada_b", "norm_g", "w_in", "b_in", "conv_w", "conv_b", "conv_ln_g", "conv_ln_b",
            "conv_proj", "decay_up_fwd", "decay_bias_fwd", "decay_up_bwd", "decay_bias_bwd", "gla_norm_g",
            "gla_proj", "w_out", "final_norm_g")


def _as2d(a):
    if a.ndim == 1:
        return a.reshape(1, -1)
    return a.reshape(-1, a.shape[-1])


def kernel(x, c, ctx, c_ctx, ada_w, ada_b, norm_g, w_in, b_in, conv_w, conv_b, conv_ln_g, conv_ln_b, conv_proj, decay_up_fwd, decay_bias_fwd, decay_up_bwd, decay_bias_bwd, gla_norm_g, gla_proj, w_out, final_norm_g, loss_target, m_c_ctx, m_ada_w, m_ada_b, m_norm_g, m_w_in, m_b_in, m_conv_w, m_conv_b, m_conv_ln_g, m_conv_ln_b, m_conv_proj, m_decay_up_fwd, m_decay_bias_fwd, m_decay_up_bwd, m_decay_bias_bwd, m_gla_norm_g, m_gla_proj, m_w_out, m_final_norm_g, v_c_ctx, v_ada_w, v_ada_b, v_norm_g, v_w_in, v_b_in, v_conv_w, v_conv_b, v_conv_ln_g, v_conv_ln_b, v_conv_proj, v_decay_up_fwd, v_decay_bias_fwd, v_decay_up_bwd, v_decay_bias_bwd, v_gla_norm_g, v_gla_proj, v_w_out, v_final_norm_g):
    env = dict(locals())
    wts = {k: env[k] for k in _WEIGHTS}
    d = x.shape[-1]
    r = decay_up_fwd.shape[1]
    dk_ = d // 2

    ds, dks = d // N_DEV, dk_ // N_DEV
    g_win, g_ada, conv_w8, g_up = _all_gather(
        [w_in[0].astype(BF16), ada_w[0].astype(BF16), conv_w[0],
         jnp.concatenate([decay_up_fwd[0], decay_up_bwd[0]], axis=1)])
    proj_own = [conv_proj[0].astype(BF16), gla_proj[0].astype(BF16), w_out[0].astype(BF16)]
    me_i = 4 * lax.axis_index("x") + 2 * lax.axis_index("y") + lax.axis_index("c")
    proj_lands = [lax.dynamic_update_slice(lax.empty((N_DEV,) + a.shape, a.dtype), a[None], (me_i, 0, 0))
                  for a in proj_own]
    proj_start = _copies_start("proj_gather_start", proj_own, proj_lands, _gather_copies, 7 * 3)

    def proj(after):
        _, lands = _copies_wait("proj_gather_wait", proj_start, (after,), _gather_copies)
        return [w.reshape(d, d) for w in lands]

    w_a, w_b = _unshard_w_in(g_win, d, r, after=(proj_start[4],))
    up_f = g_up[:, :, 0:dks].transpose(1, 0, 2).reshape(r, dk_)
    up_b = g_up[:, :, dks:].transpose(1, 0, 2).reshape(r, dk_)
    up2 = jnp.zeros((LANE, 2 * dk_), F32).at[0:r, 0:dk_].set(up_f).at[r:2 * r, dk_:].set(up_b)
    bias2 = jnp.concatenate([decay_bias_fwd, decay_bias_bwd], axis=1)
    b_a, b_b = _regroup(b_in, d, r)

    comm = {}

    def on_grads(gr):
        d_up = jnp.concatenate([gr["up2"][0:r, 0:dk_].reshape(r, N_DEV, dks).transpose(1, 0, 2),
                                gr["up2"][r:2 * r, dk_:].reshape(r, N_DEV, dks).transpose(1, 0, 2)], axis=2)
        mine = [_reshard_w_in(gr["w_a1"], gr["w_a2"], gr["w_b"], d, r), gr["conv_proj"].reshape(N_DEV, ds, d),
                gr["gla_proj"].reshape(N_DEV, ds, d), gr["w_out"].reshape(N_DEV, ds, d), gr["conv_w8"], d_up]
        lands = [lax.empty((4,) + a.shape[1:], a.dtype) for a in mine]
        comm["sibling"] = _copies_start("grad_sibling_start", mine, lands, _sibling_copies, 4 * len(mine))
        return (comm["sibling"][4],)

    def on_du_a1(du_a1):
        mine, theirs = _copies_wait("grad_sibling_wait", comm["sibling"], (du_a1,), _sibling_copies)
        sums = [_pair_sum("pair_sum_w_in", mine[0], theirs[0])] + list(_pair_sum_small(mine[1:], theirs[1:]))
        lands = [lax.empty(a.shape, a.dtype) for a in sums]
        comm["chips"] = _copies_start("grad_chips_start", sums, lands, _chip_copies, 3 * len(sums))
        return (comm["chips"][4],)

    g = _local_step(x, c, ctx, loss_target, c_ctx, g_ada, ada_b, norm_g[0:1], w_a, b_a, w_b, b_b,
                    conv_w8, conv_b, conv_ln_g, conv_ln_b, up2, bias2, gla_norm_g, final_norm_g.reshape(1, d),
                    proj, on_grads, on_du_a1)

    pack = _pack_small(g, x.shape[0], d, r)
    pack_lands = [lax.dynamic_update_slice(lax.empty((N_DEV,) + pack.shape, F32), pack[None], (me_i, 0, 0))]
    small_start = _copies_start("small_gather_start", [pack], pack_lands, _gather_copies, 7)

    (their_ada,) = _exchange_sibling([g["ada_w8"]])
    ada_sum = _pair_sum("pair_sum_ada_w", g["ada_w8"], their_ada)
    ada_start = _copies_start("ada_chips_start", [ada_sum], [lax.empty(ada_sum.shape, ada_sum.dtype)],
                              _chip_copies, 3)
    own, landed = _copies_wait("grad_chips_wait", comm["chips"], (ada_start[4],), _chip_copies)
    o_win, o_cp, o_gp, o_wo, o_cw, o_up = own
    x_win, x_cp, x_gp, x_wo, x_cw, x_up = landed

    out = {}

    def big(name, parts, wname, own=None):
        w2 = _as2d(wts[wname])
        res = _sum_adam(name, parts, w2, _as2d(env["m_" + wname]), _as2d(env["v_" + wname]), own)
        for pre, arr in zip(("grad_", "delta_", "new_m_", "new_v_"), res):
            out[pre + wname] = arr.reshape(wts[wname].shape)

    as_rows = lambda a: jnp.transpose(a, (2, 0, 1))
    res = _sum_adam("adam_w_in", x_win, as_rows(w_in), as_rows(m_w_in), as_rows(v_w_in), o_win)
    for pre, arr in zip(("grad_", "delta_", "new_m_", "new_v_"), res):
        out[pre + "w_in"] = jnp.transpose(arr, (1, 2, 0))
    small_w = (("conv_proj", x_cp, o_cp), ("gla_proj", x_gp, o_gp), ("w_out", x_wo, o_wo), ("conv_w", x_cw, o_cw),
               ("decay_up_fwd", x_up[:, :, 0:dks], o_up[:, :, 0:dks]),
               ("decay_up_bwd", x_up[:, :, dks:], o_up[:, :, dks:]))
    small_res = _sum_adam_small([(p, o, _as2d(wts[k]), _as2d(env["m_" + k]), _as2d(env["v_" + k]))
                                 for k, p, o in small_w])
    for (k, _, _), arrs in zip(small_w, small_res):
        for pre, arr in zip(("grad_", "delta_", "new_m_", "new_v_"), arrs):
            out[pre + k] = arr.reshape(wts[k].shape)

    _, (packs,) = _copies_wait("small_gather_wait", small_start, (res[0], out["grad_w_out"]), _gather_copies)
    row = lambda a: a.reshape(1, -1)
    sg, sd, sm, sv, loss = _small_adam(packs, [row(wts[k]) for k in _SMALL], [row(env["m_" + k]) for k in _SMALL],
                                       [row(env["v_" + k]) for k in _SMALL], d, r)
    for i, k in enumerate(_SMALL):
        for pre, arrs in (("grad_", sg), ("delta_", sd), ("new_m_", sm), ("new_v_", sv)):
            out[pre + k] = arrs[i].reshape(wts[k].shape)
    loss = loss.reshape(())

    (o_ada,), (x_ada,) = _copies_wait("ada_chips_wait", ada_start, (res[0], out["grad_w_out"], out["grad_b_in"]),
                                      _chip_copies)
    big("adam_ada_w", x_ada, "ada_w", o_ada)

    return (loss, g["grad_x"], *[out["grad_" + k] for k in _WEIGHTS], *[out["delta_" + k] for k in _WEIGHTS],
            *[out["new_m_" + k] for k in _WEIGHTS], *[out["new_v_" + k] for k in _WEIGHTS])
```

```python
import functools
import math

import jax
import jax.numpy as jnp
from jax import lax
from jax.experimental import pallas as pl
from jax.experimental.pallas import tpu as pltpu

F32 = jnp.float32
BF16 = jnp.bfloat16
MESH = pl.DeviceIdType.MESH

N_DEV = 8
GRID_W = 64
CHUNK = 128
HEADS = 4
EPS = 1e-6
GATE_TAU = 16.0
LANE = 128
ADAM_LR, ADAM_B1, ADAM_B2, ADAM_EPS, ADAM_WD, ADAM_STEP = 0.001, 0.9, 0.999, 1e-08, 0.01, 10
VMEM_LIMIT = 60 * 1024 * 1024
_ANY = pl.BlockSpec(memory_space=pl.ANY)


def _params(**kw):
    return pltpu.CompilerParams(vmem_limit_bytes=VMEM_LIMIT, **kw)


def _tile(n, pref):
    t = (min(pref, n) // LANE) * LANE
    while t >= LANE:
        if n % t == 0:
            return t
        t -= LANE
    return n


def _mm(a, b):
    return jnp.dot(a.astype(BF16), b.astype(BF16), preferred_element_type=F32)


def _mm_nt(a, b):
    return lax.dot_general(a.astype(BF16), b.astype(BF16), (((1,), (1,)), ((), ())), preferred_element_type=F32)


def _mm_tn(a, b):
    return lax.dot_general(a.astype(BF16), b.astype(BF16), (((0,), (0,)), ((), ())), preferred_element_type=F32)


def _sigmoid(x):
    return 0.5 * jnp.tanh(0.5 * x) + 0.5


def _dsilu(x, s):
    return s * (1.0 + x * (1.0 - s))


def _adamw(g, w, m, v):
    bc1 = 1.0 - ADAM_B1 ** ADAM_STEP
    bc2 = 1.0 - ADAM_B2 ** ADAM_STEP
    mn = ADAM_B1 * m + (1.0 - ADAM_B1) * g
    vn = ADAM_B2 * v + (1.0 - ADAM_B2) * (g * g)
    delta = -ADAM_LR * ((mn / bc1) / (jnp.sqrt(vn / bc2) + ADAM_EPS) + ADAM_WD * w)
    return delta, mn, vn


def _rowsel(table, idx, n):
    out = table[0:1, :]
    for r in range(1, n):
        out = jnp.where(idx == r, table[r:r + 1, :], out)
    return out


def _ada_fwd(cv, ada_w8, ada_b):
    n_sh, _, ws = ada_w8.shape

    def body(cv_ref, w_ref, b_ref, o_ref):
        c = cv_ref[...]
        sv = c * _sigmoid(c)
        for j in range(n_sh):
            cols = pl.ds(j * ws, ws)
            o_ref[:, cols] = _mm(sv, w_ref[j]) + b_ref[:, cols]

    return pl.pallas_call(body, name="ada_fwd", out_shape=jax.ShapeDtypeStruct((cv.shape[0], n_sh * ws), F32),
                          compiler_params=_params())(cv, ada_w8, ada_b)


def _ada_bwd(cv, ada_w8, dmod_ss, small, nb):
    n_sh, d, ws = ada_w8.shape

    def body(cv_ref, w_ref, dm_ref, sm_ref, dw_ref, db_ref, dc_ref):
        c = cv_ref[...]
        s = _sigmoid(c)
        sv = c * s
        dm = jnp.concatenate([dm_ref[:, 0:2 * d], sm_ref[8:16, :]], axis=1)
        db_ref[...] = jnp.sum(dm, axis=0, keepdims=True)
        sv_t = jnp.transpose(sv)
        dsv = None
        for j in range(n_sh):
            dmj = dm[:, j * ws:(j + 1) * ws]
            dw = sv_t[:, 0:1] * dmj[0:1, :]
            for row in range(1, nb + 1):
                dw = dw + sv_t[:, row:row + 1] * dmj[row:row + 1, :]
            dw_ref[j] = dw.astype(dw_ref.dtype)
            part = _mm_nt(dmj, w_ref[j])
            dsv = part if dsv is None else dsv + part
        dc_ref[...] = dsv * _dsilu(c, s)

    return pl.pallas_call(
        body, name="ada_bwd",
        out_shape=(jax.ShapeDtypeStruct((n_sh, d, ws), BF16), jax.ShapeDtypeStruct((1, n_sh * ws), F32),
                   jax.ShapeDtypeStruct(cv.shape, F32)),
        compiler_params=_params())(cv, ada_w8, dmod_ss, small)


class _Tiles:
    def __init__(self, nb, s_len, c_len, tm, big):
        self.nb, self.tm, self.big = nb, tm, big
        self.lat, self.ctx = s_len // tm, c_len // tm
        self.pad = -(self.lat + self.ctx) % big
        self.per_ex = self.lat + self.ctx + self.pad
        self.n_all = nb * self.per_ex
        self.rows_per_ex = self.per_ex * tm

    def is_lat(self, i):
        return i % self.per_ex < self.lat

    def is_pad(self, i):
        return i % self.per_ex >= self.lat + self.ctx

    def lat_of_all(self, i):
        return (i // self.per_ex) * self.lat + jnp.minimum(i % self.per_ex, self.lat - 1)

    def ctx_of_all(self, i):
        return (i // self.per_ex) * self.ctx + jnp.clip(i % self.per_ex - self.lat, 0, self.ctx - 1)


def _norm_fwd(x2, ctx2, mod, norm_g, tiles):
    tl, d = x2.shape
    tc = ctx2.shape[0]
    nb, tm = tiles.nb, tiles.tm

    def body(x_ref, c_ref, mod_ref, g_ref, u_ref):
        i = pl.program_id(0)
        lat = tiles.is_lat(i)
        xv = jnp.where(lat, x_ref[...], c_ref[...])
        row = jnp.where(lat, i // tiles.per_ex, nb)
        m = _rowsel(mod_ref[...], row, nb + 1)
        shift, scale = m[:, 0:d], m[:, d:2 * d]
        rstd = lax.rsqrt(jnp.mean(xv * xv, axis=-1, keepdims=True) + EPS)
        u = xv * rstd * g_ref[...] * (1.0 + scale) + shift
        u_ref[...] = jnp.where(tiles.is_pad(i), 0.0, u).astype(BF16)

    return pl.pallas_call(
        body, name="norm_fwd", grid=(tiles.n_all,),
        in_specs=[pl.BlockSpec((tm, d), lambda i: (tiles.lat_of_all(i), 0)),
                  pl.BlockSpec((tm, d), lambda i: (tiles.ctx_of_all(i), 0)),
                  pl.BlockSpec(mod.shape, lambda i: (0, 0)),
                  pl.BlockSpec((1, d), lambda i: (0, 0))],
        out_specs=pl.BlockSpec((tm, d), lambda i: (i, 0)),
        out_shape=jax.ShapeDtypeStruct((tiles.n_all * tm, d), BF16),
        compiler_params=_params())(x2, ctx2, mod, norm_g)


def _norm_bwd(x2, ctx2, mod, norm_g, du_lat, du_b, gx1, tiles):
    tl, d = x2.shape
    nb, tm = tiles.nb, tiles.tm
    nrow = mod.shape[0]
    n_lat_in = len(du_lat)

    def body(x_ref, c_ref, mod_ref, g_ref, *refs):
        dl_refs = refs[:n_lat_in]
        d3_ref, gx_ref, gxo_ref, dmod_ref, dg_ref = refs[n_lat_in:]
        i = pl.program_id(0)

        @pl.when(i == 0)
        def _():
            dmod_ref[...] = jnp.zeros_like(dmod_ref)
            dg_ref[...] = jnp.zeros_like(dg_ref)

        lat = tiles.is_lat(i)
        xv = jnp.where(lat, x_ref[...], c_ref[...])
        row = jnp.where(lat, i // tiles.per_ex, nb)
        m = _rowsel(mod_ref[...], row, nb + 1)
        scale = m[:, d:2 * d]
        g = g_ref[...]
        dl = dl_refs[0][...].astype(F32)
        for ref in dl_refs[1:]:
            dl = dl + ref[...].astype(F32)
        du = jnp.where(tiles.is_pad(i), 0.0, d3_ref[...].astype(F32) + jnp.where(lat, dl, 0.0))
        rstd = lax.rsqrt(jnp.mean(xv * xv, axis=-1, keepdims=True) + EPS)
        xh = xv * rstd
        dshift = jnp.sum(du, axis=0, keepdims=True)
        dscale = jnp.sum(du * xh * g, axis=0, keepdims=True)
        dxn = du * (1.0 + scale)
        dg_ref[...] += jnp.sum(dxn * xh, axis=0, keepdims=True)
        dxh = dxn * g
        dx = rstd * (dxh - xh * jnp.mean(dxh * xh, axis=-1, keepdims=True))

        @pl.when(lat)
        def _():
            gxo_ref[...] = dx + gx_ref[...]

        for r in range(nb + 1):
            dmod_ref[r:r + 1, 0:d] += jnp.where(row == r, dshift, 0.0)
            dmod_ref[r:r + 1, d:2 * d] += jnp.where(row == r, dscale, 0.0)

    lat_map = lambda i: (tiles.lat_of_all(i), 0)
    lat_spec = pl.BlockSpec((tm, d), lat_map)
    return pl.pallas_call(
        body, name="norm_bwd", grid=(tiles.n_all,),
        in_specs=[lat_spec,
                  pl.BlockSpec((tm, d), lambda i: (tiles.ctx_of_all(i), 0)),
                  pl.BlockSpec(mod.shape, lambda i: (0, 0)),
                  pl.BlockSpec((1, d), lambda i: (0, 0))]
                 + [lat_spec] * n_lat_in
                 + [pl.BlockSpec((tm, d), lambda i: (i, 0)), lat_spec],
        out_specs=(lat_spec,
                   pl.BlockSpec((nrow, 3 * d), lambda i: (0, 0)),
                   pl.BlockSpec((1, d), lambda i: (0, 0))),
        out_shape=(jax.ShapeDtypeStruct((tl, d), F32), jax.ShapeDtypeStruct((nrow, 3 * d), F32),
                   jax.ShapeDtypeStruct((1, d), F32)),
        compiler_params=_params())(x2, ctx2, mod, norm_g, *du_lat, du_b, gx1)


def _matmul_bias(name, u3, w, b, s_len, tm, tn):
    nb = u3.shape[0]
    d, n = w.shape
    per = s_len // tm
    rows = nb * s_len

    def body(u_ref, w_ref, b_ref, o_ref):
        o_ref[...] = jnp.dot(u_ref[...], w_ref[...], preferred_element_type=F32) + b_ref[...]

    return pl.pallas_call(
        body, name=name, grid=(n // tn, rows // tm),
        in_specs=[pl.BlockSpec((None, tm, d), lambda j, i: (i // per, i % per, 0)),
                  pl.BlockSpec((d, tn), lambda j, i: (0, j)),
                  pl.BlockSpec((1, tn), lambda j, i: (0, j))],
        out_specs=pl.BlockSpec((tm, tn), lambda j, i: (i, j)),
        out_shape=jax.ShapeDtypeStruct((rows, n), F32),
        compiler_params=_params())(u3, w, b)


def _log_sigmoid(x):
    return jnp.minimum(x, 0.0) - jnp.log(1.0 + jnp.exp(-jnp.abs(x)))


def _inproj_b(u, w_b, b_b, up2, bias2, tm, dk_, dv_):
    t_all, d = u.shape
    nbw = w_b.shape[1]
    n2 = up2.shape[1]

    def body(u_ref, w_ref, b_ref, up_ref, bias_ref, qk_ref, v_ref, g_ref):
        full = jnp.dot(u_ref[...], w_ref[...], preferred_element_type=F32) + b_ref[...]
        lr = full[:, 2 * dk_ + dv_:nbw]
        qk_ref[:, 0:2 * dk_] = full[:, 0:2 * dk_]
        qk_ref[:, 2 * dk_:2 * dk_ + LANE] = lr
        v_ref[...] = full[:, 2 * dk_:2 * dk_ + dv_].astype(BF16)
        g_ref[...] = _log_sigmoid(_mm(lr, up_ref[...]) + bias_ref[...]) * (1.0 / GATE_TAU)

    whole = lambda a: pl.BlockSpec(a.shape, lambda i: (0, 0))
    return pl.pallas_call(
        body, name="inproj_b", grid=(t_all // tm,),
        in_specs=[pl.BlockSpec((tm, d), lambda i: (i, 0)), whole(w_b), whole(b_b), whole(up2), whole(bias2)],
        out_specs=(pl.BlockSpec((tm, 2 * dk_ + LANE), lambda i: (i, 0)), pl.BlockSpec((tm, dv_), lambda i: (i, 0)),
                   pl.BlockSpec((tm, n2), lambda i: (i, 0))),
        out_shape=(jax.ShapeDtypeStruct((t_all, 2 * dk_ + LANE), F32), jax.ShapeDtypeStruct((t_all, dv_), BF16),
                   jax.ShapeDtypeStruct((t_all, n2), F32)),
        compiler_params=_params())(u, w_b, b_b, up2, bias2)


def _matmul_nt(name, a, w, koff, tm, tk, after=()):
    r, kc = a.shape
    d = w.shape[0]
    nk = kc // tk

    def body(a_ref, w_ref, *rest):
        o_ref = rest[len(after)]
        k = pl.program_id(1)
        p = lax.dot_general(a_ref[...], w_ref[...], (((1,), (1,)), ((), ())), preferred_element_type=F32)
        if nk == 1:
            o_ref[...] = p.astype(o_ref.dtype)
            return
        acc_ref = rest[len(after) + 1]

        @pl.when(k == 0)
        def _():
            acc_ref[...] = p

        @pl.when(k > 0)
        def _():
            acc_ref[...] += p

        @pl.when(k == nk - 1)
        def _():
            o_ref[...] = acc_ref[...].astype(o_ref.dtype)

    return pl.pallas_call(
        body, name=name, grid=(r // tm, nk),
        in_specs=[pl.BlockSpec((tm, tk), lambda i, k: (i, k)),
                  pl.BlockSpec((d, tk), lambda i, k: (0, koff + k))] + [_ANY] * len(after),
        out_specs=pl.BlockSpec((tm, d), lambda i, k: (i, 0)),
        out_shape=jax.ShapeDtypeStruct((r, d), BF16),
        scratch_shapes=[pltpu.VMEM((tm, d), F32)] if nk > 1 else [],
        compiler_params=_params())(a, w, *after)


def _matmul_tn(name, a, b, rows, tk, tn):
    m = a.shape[1]
    n = b.shape[1]
    nk = rows // tk

    def body(a_ref, b_ref, o_ref, s_ref, acc_ref):
        k = pl.program_id(1)
        bv = b_ref[...]
        p = lax.dot_general(bv, a_ref[...], (((0,), (0,)), ((), ())), preferred_element_type=F32)
        cs = jnp.sum(bv.astype(F32), axis=0, keepdims=True)

        @pl.when(k == 0)
        def _():
            acc_ref[...] = p
            s_ref[...] = cs

        @pl.when(k > 0)
        def _():
            acc_ref[...] += p
            s_ref[...] += cs

        @pl.when(k == nk - 1)
        def _():
            o_ref[...] = acc_ref[...].astype(o_ref.dtype)

    return pl.pallas_call(
        body, name=name, grid=(n // tn, nk),
        in_specs=[pl.BlockSpec((tk, m), lambda j, k: (k, 0)),
                  pl.BlockSpec((tk, tn), lambda j, k: (k, j))],
        out_specs=(pl.BlockSpec((tn, m), lambda j, k: (j, 0)), pl.BlockSpec((1, tn), lambda j, k: (0, j))),
        out_shape=(jax.ShapeDtypeStruct((n, m), BF16), jax.ShapeDtypeStruct((1, n), F32)),
        scratch_shapes=[pltpu.VMEM((tn, m), F32)],
        compiler_params=_params())(a, b)


def _matmul_tn_whole(name, a3, b3, rows, tn, transposed):
    nb, _, m = a3.shape
    n = b3.shape[2]

    def body(a_ref, b_ref, o_ref, s_ref):
        p, cs = None, None
        for e in range(nb):
            bv = b_ref[e]
            lhs, rhs = (bv, a_ref[e]) if transposed else (a_ref[e], bv)
            pe = lax.dot_general(lhs, rhs, (((0,), (0,)), ((), ())), preferred_element_type=F32)
            ce = jnp.sum(bv.astype(F32), axis=0, keepdims=True)
            p, cs = (pe, ce) if p is None else (p + pe, cs + ce)
        o_ref[...] = p.astype(o_ref.dtype)
        s_ref[...] = cs

    o_spec, o_shape = ((pl.BlockSpec((tn, m), lambda j: (j, 0)), (n, m)) if transposed
                       else (pl.BlockSpec((m, tn), lambda j: (0, j)), (m, n)))
    return pl.pallas_call(
        body, name=name, grid=(n // tn,),
        in_specs=[pl.BlockSpec((nb, rows, m), lambda j: (0, 0, 0)),
                  pl.BlockSpec((nb, rows, tn), lambda j: (0, 0, j))],
        out_specs=(o_spec, pl.BlockSpec((1, tn), lambda j: (0, j))),
        out_shape=(jax.ShapeDtypeStruct(o_shape, BF16), jax.ShapeDtypeStruct((1, n), F32)),
        compiler_params=_params())(a3, b3)


def _conv_window(pad_ref, r, shift, ktaps, width, horizontal):
    if horizontal:
        return pad_ref[r, pl.ds(16 + shift, width), :]
    return pad_ref[r + ktaps // 2 + shift]


def _conv_row(pad_ref, w, r, ktaps, width, horizontal, flip):
    half = ktaps // 2
    acc = None
    for t in range(ktaps):
        win = _conv_window(pad_ref, r, (half - t) if flip else (t - half), ktaps, width, horizontal)
        term = win * w[t:t + 1, :]
        acc = term if acc is None else acc + term
    return acc


def _fill_padded(ref, val, rows, width, ktaps, horizontal):
    half_k = ktaps // 2
    cb = val.shape[-1]
    if horizontal:
        ref[:, 0:16, :] = jnp.zeros((rows, 16, cb), F32)
        ref[:, 16 + width:32 + width, :] = jnp.zeros((rows, 16, cb), F32)
        ref[:, 16:16 + width, :] = val
    else:
        ref[0:half_k, :, :] = jnp.zeros((half_k, width, cb), F32)
        ref[half_k + rows:2 * half_k + rows, :, :] = jnp.zeros((half_k, width, cb), F32)
        ref[half_k:half_k + rows, :, :] = val


def _conv_fwd(pa, conv_w8, conv_b, nb, s):
    nblk, ktaps, cb = conv_w8.shape
    d = nblk * cb
    rows, width = s // GRID_W, GRID_W
    half_k = ktaps // 2
    nh = nblk // 2

    def body(glu_ref, w_ref, b_ref, o_ref, ph_ref, pv_ref):
        j = pl.program_id(1)
        a0 = (glu_ref[:, 0:cb] * _sigmoid(glu_ref[:, cb:2 * cb])).reshape(rows, width, cb)
        w = w_ref[...]

        bias = b_ref[...]

        def run(pad_ref, horizontal):
            _fill_padded(pad_ref, a0, rows, width, ktaps, horizontal)

            def row(r, carry):
                at = pl.ds(pl.multiple_of(r * width, width), width)
                o_ref[at, :] = _conv_row(pad_ref, w, r, ktaps, width, horizontal, False) + bias
                return carry

            lax.fori_loop(0, rows, row, 0)

        @pl.when(j < nh)
        def _():
            run(ph_ref, True)

        @pl.when(j >= nh)
        def _():
            run(pv_ref, False)

    return pl.pallas_call(
        body, name="conv_fwd", grid=(nb, nblk),
        in_specs=[pl.BlockSpec((s, 2 * cb), lambda b, j: (b, j)),
                  pl.BlockSpec((None, ktaps, cb), lambda b, j: (j, 0, 0)),
                  pl.BlockSpec((1, cb), lambda b, j: (0, j))],
        out_specs=pl.BlockSpec((s, cb), lambda b, j: (b, j)),
        out_shape=jax.ShapeDtypeStruct((nb * s, d), F32),
        scratch_shapes=[pltpu.VMEM((rows, width + 32, cb), F32), pltpu.VMEM((rows + 2 * half_k, width, cb), F32)],
        compiler_params=_params())(pa, conv_w8, conv_b)


def _conv_bwd(pa, da1, conv_w8, nb, s):
    nblk, ktaps, cb = conv_w8.shape
    d = nblk * cb
    rows, width = s // GRID_W, GRID_W
    half_k = ktaps // 2
    nh = nblk // 2

    def body(glu_ref, da_ref, w_ref, dp_ref, dw_ref, db_ref, pha_ref, phd_ref, pva_ref, pvd_ref):
        j = pl.program_id(0)
        b = pl.program_id(1)
        a0 = (glu_ref[:, 0:cb] * _sigmoid(glu_ref[:, cb:2 * cb])).reshape(rows, width, cb)
        da1v = da_ref[...]
        d3 = da1v.reshape(rows, width, cb)
        w = w_ref[...]

        @pl.when(b == 0)
        def _():
            dw_ref[...] = jnp.zeros_like(dw_ref)
            db_ref[...] = jnp.zeros_like(db_ref)

        db_ref[...] += jnp.sum(da1v, axis=0, keepdims=True)

        def run(pa_ref, pd_ref, horizontal):
            _fill_padded(pa_ref, a0, rows, width, ktaps, horizontal)
            _fill_padded(pd_ref, d3, rows, width, ktaps, horizontal)

            def row(r, accs):
                at = pl.ds(pl.multiple_of(r * width, width), width)
                da0 = _conv_row(pd_ref, w, r, ktaps, width, horizontal, True)
                gv = glu_ref[at, 0:cb]
                sg = _sigmoid(glu_ref[at, cb:2 * cb])
                dp_ref[at, 0:cb] = (da0 * sg).astype(BF16)
                dp_ref[at, cb:2 * cb] = (da0 * gv * sg * (1.0 - sg)).astype(BF16)
                d_row = da_ref[at, :]
                out = []
                for t in range(ktaps):
                    prod = _conv_window(pa_ref, r, t - half_k, ktaps, width, horizontal) * d_row
                    out.append(accs[t] + jnp.sum(prod.reshape(width // 8, 8, cb), axis=0))
                return tuple(out)

            accs = lax.fori_loop(0, rows, row, tuple(jnp.zeros((8, cb), F32) for _ in range(ktaps)))
            for t in range(ktaps):
                dw_ref[t:t + 1, :] += jnp.sum(accs[t], axis=0, keepdims=True)

        @pl.when(j < nh)
        def _():
            run(pha_ref, phd_ref, True)

        @pl.when(j >= nh)
        def _():
            run(pva_ref, pvd_ref, False)

    return pl.pallas_call(
        body, name="conv_bwd", grid=(nblk, nb),
        in_specs=[pl.BlockSpec((s, 2 * cb), lambda j, b: (b, j)),
                  pl.BlockSpec((s, cb), lambda j, b: (b, j)),
                  pl.BlockSpec((None, ktaps, cb), lambda j, b: (j, 0, 0))],
        out_specs=(pl.BlockSpec((s, 2 * cb), lambda j, b: (b, j)),
                   pl.BlockSpec((None, ktaps, cb), lambda j, b: (j, 0, 0)),
                   pl.BlockSpec((1, cb), lambda j, b: (0, j))),
        out_shape=(jax.ShapeDtypeStruct((nb * s, 2 * d), BF16),
                   jax.ShapeDtypeStruct((nblk, ktaps, cb), F32), jax.ShapeDtypeStruct((1, d), F32)),
        scratch_shapes=[pltpu.VMEM((rows, width + 32, cb), F32), pltpu.VMEM((rows, width + 32, cb), F32),
                        pltpu.VMEM((rows + 2 * half_k, width, cb), F32),
                        pltpu.VMEM((rows + 2 * half_k, width, cb), F32)],
        compiler_params=_params())(pa, da1, conv_w8)


def _decay_bwd(pb, up2, bias2, grads_f, grads_b, tiles, lr_blk, dk_, dv_):
    t_all = pb.shape[0]
    tm = tiles.tm
    n2 = up2.shape[1]
    nbw = 2 * dk_ + dv_ + LANE

    def body(lr_ref, up_ref, b_ref, dqf, dkf, dvf, dgf, dqb, dkb, dvb, dgb, dp_ref, dup_ref, dbias_ref):
        i = pl.program_id(0)
        pad = tiles.is_pad(i)
        live = lambda v: jnp.where(pad, 0.0, v)

        @pl.when(i == 0)
        def _():
            dup_ref[...] = jnp.zeros_like(dup_ref)
            dbias_ref[...] = jnp.zeros_like(dbias_ref)

        lr = lr_ref[...]
        up = up_ref[...]
        logits = _mm(lr, up) + b_ref[...]
        dg = live(jnp.concatenate([dgf[...], dgb[...]], axis=1))
        dlog = dg * (1.0 / GATE_TAU) * _sigmoid(-logits)
        dup_ref[...] += _mm_tn(lr, dlog)
        dbias_ref[...] += jnp.sum(dlog, axis=0, keepdims=True)
        both = lambda f, b: live(f[...].astype(F32) + b[...].astype(F32)).astype(BF16)
        dp_ref[:, 0:dk_] = both(dqf, dqb)
        dp_ref[:, dk_:2 * dk_] = both(dkf, dkb)
        dp_ref[:, 2 * dk_:2 * dk_ + dv_] = both(dvf, dvb)
        dp_ref[:, 2 * dk_ + dv_:nbw] = _mm_nt(dlog, up).astype(BF16)

    row = lambda w: pl.BlockSpec((tm, w), lambda i: (i, 0))
    return pl.pallas_call(
        body, name="decay_bwd", grid=(t_all // tm,),
        in_specs=[pl.BlockSpec((tm, LANE), lambda i: (i, lr_blk)),
                  pl.BlockSpec(up2.shape, lambda i: (0, 0)),
                  pl.BlockSpec((1, n2), lambda i: (0, 0)),
                  row(dk_), row(dk_), row(dv_), row(dk_), row(dk_), row(dk_), row(dv_), row(dk_)],
        out_specs=(row(nbw), pl.BlockSpec(up2.shape, lambda i: (0, 0)), pl.BlockSpec((1, n2), lambda i: (0, 0))),
        out_shape=(jax.ShapeDtypeStruct((t_all, nbw), BF16), jax.ShapeDtypeStruct(up2.shape, F32),
                   jax.ShapeDtypeStruct((1, n2), F32)),
        compiler_params=_params())(pb, up2, bias2, *grads_f, *grads_b)


def _scan_chunk(s, nl, nc, rev):
    if rev:
        return jnp.where(s < nc, nl + (nc - 1 - s), nl - 1 - (s - nc))
    return jnp.where(s < nc, nl + s, s - nc)


def _scan_lat_chunk(s, nl, nc, rev):
    first = nl - 1 if rev else 0
    return jnp.where(s < nc, first, _scan_chunk(s, nl, nc, rev))


def _tri_mm(m_bf, x):
    hi = x.astype(BF16)
    r1 = x - hi.astype(F32)
    mid = r1.astype(BF16)
    lo = (r1 - mid.astype(F32)).astype(BF16)
    dot = lambda p: jnp.dot(m_bf, p, preferred_element_type=F32)
    return dot(hi) + dot(mid) + dot(lo)


def _chunk_masks(c, rev):
    ii = lax.broadcasted_iota(jnp.int32, (c, c), 0)
    jj = lax.broadcasted_iota(jnp.int32, (c, c), 1)
    return ((ii <= jj), (ii >= jj)) if rev else ((ii >= jj), (ii <= jj))


def _chunk_terms(q, k, b, far, mid):
    bf, bm = b[far:far + 1, :], b[mid:mid + 1, :]
    e = jnp.exp(b)
    em = jnp.exp(b - bm)
    eim = jnp.exp(bm - b)
    ed = jnp.exp(bf - b)
    return dict(e=e, em=em, eim=eim, ed=ed, dec=jnp.exp(bf), qe=q * e, qem=q * em, kim=k * eim, kd=k * ed)


def _gla_fwd(pb3, pv3, g3, nb, s_len, c_len, dk_, dv_):
    c = CHUNK
    nl, nc = s_len // c, c_len // c
    ns = nl + nc
    hk, hv = dk_ // HEADS, dv_ // HEADS
    l_len = pb3.shape[1]
    scale = hk ** -0.5
    mid = c // 2

    def body(*refs):
        ins, outs, z_scr = refs[:8], refs[8:14], refs[14]
        s = pl.program_id(0)

        @pl.when(s == 0)
        def _():
            z_scr[...] = jnp.zeros_like(z_scr)

        qs = jnp.where(s >= nc, scale, 0.0)
        for di, rev in enumerate((False, True)):
            q_ref, k_ref, v_ref, g_ref = ins[4 * di:4 * di + 4]
            o_ref, zs_ref, b_ref = outs[3 * di:3 * di + 3]
            mask, _ = _chunk_masks(c, rev)
            m_bf = mask.astype(BF16)
            far = 0 if rev else c - 1
            for b in range(nb):
                bc = _tri_mm(m_bf, g_ref[b])
                b_ref[b] = bc
                for h in range(HEADS):
                    ks, vs = slice(h * hk, (h + 1) * hk), slice(h * hv, (h + 1) * hv)
                    zi = (di * nb + b) * HEADS + h
                    v = v_ref[b, :, vs]
                    t = _chunk_terms(q_ref[b, :, ks] * qs, k_ref[b, :, ks], bc[:, ks], far, mid)
                    a = jnp.where(mask, _mm_nt(t["qem"], t["kim"]), 0.0)
                    z = z_scr[zi]
                    zs_ref[0, b * HEADS + h] = z
                    o_ref[b, :, vs] = _mm(a, v) + _mm_nt(t["qe"], z)
                    z_scr[zi] = z * t["dec"] + _mm_tn(v, t["kd"])

    in_specs, out_specs, out_shape = [], [], []
    for di, rev in enumerate((False, True)):
        ch = functools.partial(_scan_chunk, nl=nl, nc=nc, rev=rev)
        lch = functools.partial(_scan_lat_chunk, nl=nl, nc=nc, rev=rev)
        in_specs += [pl.BlockSpec((nb, c, dk_), lambda s, ch=ch: (0, ch(s), 0)),
                     pl.BlockSpec((nb, c, dk_), lambda s, ch=ch: (0, ch(s), 1)),
                     pl.BlockSpec((nb, c, dv_), lambda s, ch=ch: (0, ch(s), 0)),
                     pl.BlockSpec((nb, c, dk_), lambda s, ch=ch, di=di: (0, ch(s), di))]
        out_specs += [pl.BlockSpec((nb, c, dv_), lambda s, lch=lch: (0, lch(s), 0)),
                      pl.BlockSpec((1, nb * HEADS, hv, hk), lambda s: (s, 0, 0, 0)),
                      pl.BlockSpec((nb, c, dk_), lambda s, ch=ch: (0, ch(s), 0))]
        out_shape += [jax.ShapeDtypeStruct((nb, s_len, dv_), F32),
                      jax.ShapeDtypeStruct((ns, nb * HEADS, hv, hk), F32),
                      jax.ShapeDtypeStruct((nb, l_len, dk_), F32)]
    return pl.pallas_call(
        body, name="gla_fwd", grid=(ns,), in_specs=in_specs, out_specs=tuple(out_specs), out_shape=tuple(out_shape),
        scratch_shapes=[pltpu.VMEM((2 * nb * HEADS, hv, hk), F32)],
        compiler_params=_params())(pb3, pb3, pv3, g3, pb3, pb3, pv3, g3)


def _gla_bwd(pb3, pv3, do3, fwd_saved, nb, s_len, c_len, dk_, dv_):
    c = CHUNK
    nl, nc = s_len // c, c_len // c
    ns = nl + nc
    hk, hv = dk_ // HEADS, dv_ // HEADS
    l_len = pb3.shape[1]
    scale = hk ** -0.5
    mid = c // 2
    zs_f, b_f, zs_b, b_b = fwd_saved

    def body(*refs):
        ins, outs, dz_scr = refs[:12], refs[12:20], refs[20]
        s = pl.program_id(0)
        step = ns - 1 - s

        @pl.when(s == 0)
        def _():
            dz_scr[...] = jnp.zeros_like(dz_scr)

        lat = step >= nc
        qs = jnp.where(lat, scale, 0.0)
        dmul = jnp.where(lat, 1.0, 0.0)
        for di, rev in enumerate((False, True)):
            q_ref, k_ref, v_ref, b_ref, do_ref, zs_ref = ins[6 * di:6 * di + 6]
            dq_ref, dk_ref, dv_ref, dg_ref = outs[4 * di:4 * di + 4]
            mask, mask_t = _chunk_masks(c, rev)
            mt_bf = mask_t.astype(BF16)
            far = 0 if rev else c - 1
            far_row = lax.broadcasted_iota(jnp.int32, (c, hk), 0) == far
            for b in range(nb):
                db_parts = []
                for h in range(HEADS):
                    ks, vs = slice(h * hk, (h + 1) * hk), slice(h * hv, (h + 1) * hv)
                    zi = (di * nb + b) * HEADS + h
                    v = v_ref[b, :, vs]
                    d_o = do_ref[b, :, vs] * dmul
                    t = _chunk_terms(q_ref[b, :, ks] * qs, k_ref[b, :, ks], b_ref[b, :, ks], far, mid)
                    qem, kim, qe, kd = t["qem"], t["kim"], t["qe"], t["kd"]
                    a_t = jnp.where(mask_t, _mm_nt(kim, qem), 0.0)
                    d_a = jnp.where(mask, _mm_nt(d_o, v), 0.0)
                    d_at = jnp.where(mask_t, _mm_nt(v, d_o), 0.0)
                    z = zs_ref[0, b * HEADS + h]
                    dzn = dz_scr[zi]
                    dv_ref[b, :, vs] = (_mm(a_t, d_o) + _mm_nt(kd, dzn)).astype(dv_ref.dtype)
                    dqem = _mm(d_a, kim)
                    dkim = _mm(d_at, qem)
                    dqe = _mm(d_o, z)
                    dkd = _mm(v, dzn)
                    ddec = jnp.sum(z * dzn, axis=0, keepdims=True)
                    dz_scr[zi] = dzn * t["dec"] + _mm_tn(d_o, qe)
                    dq_ref[b, :, ks] = ((dqem * t["em"] + dqe * t["e"]) * qs).astype(dq_ref.dtype)
                    dk_ref[b, :, ks] = (dkim * t["eim"] + dkd * t["ed"]).astype(dk_ref.dtype)
                    db = dqem * qem - dkim * kim + dqe * qe - dkd * kd
                    extra = jnp.sum(dkd * kd, axis=0, keepdims=True) + ddec * t["dec"]
                    db_parts.append(db + jnp.where(far_row, extra, 0.0))
                dg_ref[b] = _tri_mm(mt_bf, jnp.concatenate(db_parts, axis=1))

    in_specs, out_specs, out_shape, args = [], [], [], []
    for di, rev in enumerate((False, True)):
        ch = lambda s, rev=rev: _scan_chunk(ns - 1 - s, nl, nc, rev)
        lch = lambda s, rev=rev: _scan_lat_chunk(ns - 1 - s, nl, nc, rev)
        in_specs += [pl.BlockSpec((nb, c, dk_), lambda s, ch=ch: (0, ch(s), 0)),
                     pl.BlockSpec((nb, c, dk_), lambda s, ch=ch: (0, ch(s), 1)),
                     pl.BlockSpec((nb, c, dv_), lambda s, ch=ch: (0, ch(s), 0)),
                     pl.BlockSpec((nb, c, dk_), lambda s, ch=ch: (0, ch(s), 0)),
                     pl.BlockSpec((nb, c, dv_), lambda s, lch=lch: (0, lch(s), 0)),
                     pl.BlockSpec((1, nb * HEADS, hv, hk), lambda s: (ns - 1 - s, 0, 0, 0))]
        args += [pb3, pb3, pv3, (b_b if rev else b_f), do3, (zs_b if rev else zs_f)]
        for w, dt in ((dk_, BF16), (dk_, BF16), (dv_, BF16), (dk_, F32)):
            out_specs.append(pl.BlockSpec((nb, c, w), lambda s, ch=ch: (0, ch(s), 0)))
            out_shape.append(jax.ShapeDtypeStruct((nb, l_len, w), dt))
    return pl.pallas_call(
        body, name="gla_bwd", grid=(ns,), in_specs=in_specs, out_specs=tuple(out_specs), out_shape=tuple(out_shape),
        scratch_shapes=[pltpu.VMEM((2 * nb * HEADS, hv, hk), F32)],
        compiler_params=_params())(*args)


def _tail(a1, pa, o_f, o_b, x2, tgt, mod, wc, wg, wo, ln_g, ln_b, gn_t, fg, nb, tm, n_split):
    tl, d = x2.shape
    nt = tl // tm
    per_ex = nt // nb
    hv = d // HEADS
    nrow = mod.shape[0]

    def part(shared, a1_ref, z_ref, r_ref, mc_ref, mg_ref, of_ref, ob_ref, x_ref, t_ref,
             dp_ref, da1_ref, do_ref, gx_ref, mrg_ref, dmo_ref, yci_ref, dyc_ref, ogi_ref, dyg_ref, sm_ref):
        bidx, gate, lng, lnb, fgv, gn, wc_, wg_, wo_ = shared

        a1v = a1_ref[...]
        mu = jnp.mean(a1v, axis=-1, keepdims=True)
        xc = a1v - mu
        rs = lax.rsqrt(jnp.mean(xc * xc, axis=-1, keepdims=True) + EPS)
        xh = xc * rs
        a2 = xh * lng + lnb
        s2 = _sigmoid(a2)
        a3 = a2 * s2
        zv = z_ref[...]
        sz = _sigmoid(zv)
        siluz = zv * sz
        ycin = a3 * siluz
        yconv = _mm(ycin, wc_)

        o = of_ref[...] + ob_ref[...]
        ohat_parts, rn_parts = [], []
        for h in range(HEADS):
            oh = o[:, h * hv:(h + 1) * hv]
            rn = lax.rsqrt(jnp.mean(oh * oh, axis=-1, keepdims=True) + EPS)
            ohat_parts.append(oh * rn)
            rn_parts.append(rn)
        ohat = jnp.concatenate(ohat_parts, axis=1)
        on = ohat * gn
        rv = r_ref[...]
        sr = _sigmoid(rv)
        silur = rv * sr
        ogin = on * silur
        ygla = _mm(ogin, wg_)

        sc = _sigmoid(mc_ref[...])
        sg = _sigmoid(mg_ref[...])
        merged = sc * yconv + sg * ygla
        mo = _mm(merged, wo_)
        hn = x_ref[...] + gate * mo
        rf = lax.rsqrt(jnp.mean(hn * hn, axis=-1, keepdims=True) + EPS)
        yh = hn * rf
        err = yh * fgv - t_ref[...]
        loss_part = 0.5 * jnp.sum(err * err) * (1.0 / d)

        dy = err * (1.0 / d)
        dfg = jnp.sum(dy * yh, axis=0, keepdims=True)
        dyh = dy * fgv
        dhn = rf * (dyh - yh * jnp.mean(dyh * yh, axis=-1, keepdims=True))
        gx_ref[...] = dhn
        dgate = jnp.sum(dhn * mo, axis=0, keepdims=True)
        dmo = gate * dhn
        dmerged = _mm_nt(dmo, wo_)
        dyconv = dmerged * sc
        dygla = dmerged * sg
        dp_ref[:, 2 * d:3 * d] = (dmerged * yconv * sc * (1.0 - sc)).astype(BF16)
        dp_ref[:, 3 * d:4 * d] = (dmerged * ygla * sg * (1.0 - sg)).astype(BF16)
        dycin = _mm_nt(dyconv, wc_)
        dogin = _mm_nt(dygla, wg_)
        mrg_ref[...] = merged.astype(BF16)
        dmo_ref[...] = dmo.astype(BF16)
        yci_ref[...] = ycin.astype(BF16)
        dyc_ref[...] = dyconv.astype(BF16)
        ogi_ref[...] = ogin.astype(BF16)
        dyg_ref[...] = dygla.astype(BF16)

        da3 = dycin * siluz
        dp_ref[:, 0:d] = (dycin * a3 * _dsilu(zv, sz)).astype(BF16)
        da2 = da3 * _dsilu(a2, s2)
        dlng = jnp.sum(da2 * xh, axis=0, keepdims=True)
        dlnb = jnp.sum(da2, axis=0, keepdims=True)
        dxh = da2 * lng
        da1_ref[...] = rs * (dxh - jnp.mean(dxh, axis=-1, keepdims=True)
                             - xh * jnp.mean(dxh * xh, axis=-1, keepdims=True))

        don = dogin * silur
        dp_ref[:, d:2 * d] = (dogin * on * _dsilu(rv, sr)).astype(BF16)
        dgn = jnp.sum(don * ohat, axis=0, keepdims=True)
        dyn = don * gn
        for h in range(HEADS):
            vs = slice(h * hv, (h + 1) * hv)
            oh_hat = ohat_parts[h]
            dh = dyn[:, vs]
            do_ref[:, vs] = (rn_parts[h] * (dh - oh_hat * jnp.mean(dh * oh_hat, axis=-1, keepdims=True))
                             ).astype(BF16)

        sm_ref[0:1, :] += dfg
        sm_ref[1:2, :] += dlng
        sm_ref[2:3, :] += dlnb
        sm_ref[3:4, :] += dgn
        sm_ref[4:5, :] += jnp.zeros((1, d), F32) + loss_part
        for b in range(nb):
            sm_ref[8 + b:9 + b, :] += jnp.where(bidx == b, dgate, 0.0)

    def body(*refs):
        mod_ref, wc_ref, wg_ref, wo_ref, lng_ref, lnb_ref, gn_ref, fg_ref = refs[9:17]
        sm_ref = refs[27]
        i = pl.program_id(0)

        @pl.when(i == 0)
        def _():
            sm_ref[...] = jnp.zeros_like(sm_ref)

        bidx = i // per_ex
        shared = (bidx, _rowsel(mod_ref[...], bidx, nb)[:, 2 * d:3 * d], lng_ref[...], lnb_ref[...], fg_ref[...],
                  jnp.concatenate([gn_ref[...]] * HEADS, axis=1), wc_ref[...], wg_ref[...], wo_ref[...])
        rows_per = tm // n_split
        for p in range(n_split):
            rows = pl.ds(p * rows_per, rows_per)
            part(shared, *[r.at[rows] for r in refs[0:9]], *[r.at[rows] for r in refs[17:27]], sm_ref)

    row = pl.BlockSpec((tm, d), lambda i: (i, 0))
    pcol = lambda blk: pl.BlockSpec((tm, d), lambda i: (i, blk))
    full = lambda arr: pl.BlockSpec(arr.shape, lambda i: (0,) * arr.ndim)
    bfo = jax.ShapeDtypeStruct((tl, d), BF16)
    f32o = jax.ShapeDtypeStruct((tl, d), F32)
    return pl.pallas_call(
        body, name="tail", grid=(nt,),
        in_specs=[row, pcol(2), pcol(3), pcol(4), pcol(5), row, row, row, row, full(mod), full(wc), full(wg),
                  full(wo), full(ln_g), full(ln_b), full(gn_t), full(fg)],
        out_specs=(pl.BlockSpec((tm, 4 * d), lambda i: (i, 0)), row, row, row, row, row, row, row, row, row,
                   pl.BlockSpec((16, d), lambda i: (0, 0))),
        out_shape=(jax.ShapeDtypeStruct((tl, 4 * d), BF16), f32o, bfo, f32o, bfo, bfo, bfo, bfo, bfo, bfo,
                   jax.ShapeDtypeStruct((16, d), F32)),
        compiler_params=_params())(a1, pa, pa, pa, pa, o_f, o_b, x2, tgt, mod, wc, wg, wo, ln_g, ln_b, gn_t, fg)


def _local_step(x, c, ctx, tgt, c_ctx, ada_w8, ada_b, norm_g, w_a, b_a, w_b, b_b, conv_w8, conv_b, ln_g, ln_b,
                up2, bias2, gla_norm_g, final_norm_g, proj, on_grads=None, on_du_a1=None):
    nb, s_len, d = x.shape
    c_len = ctx.shape[1]
    dk_, dv_ = d // 2, d
    tl, tc = nb * s_len, nb * c_len
    nbw = 2 * dk_ + dv_ + LANE
    tm = math.gcd(256, c_len)
    tiles = _Tiles(nb, s_len, c_len, tm, 2)
    l_len = tiles.rows_per_ex
    t_all = nb * l_len
    x2, ctx2, tgt2 = x.reshape(tl, d), ctx.reshape(tc, d), tgt.reshape(tl, d)

    cv = jnp.zeros((8, d), F32).at[0:nb].set(c).at[nb].set(c_ctx.reshape(d))
    mod = _ada_fwd(cv, ada_w8, ada_b)
    u = _norm_fwd(x2, ctx2, mod, norm_g, tiles)
    u3 = u.reshape(nb, l_len, d)
    tma = math.gcd(1024, s_len)
    pa = _matmul_bias("inproj_a", u3, w_a, b_a, s_len, tma, _tile(6 * d, 2048))
    tmb = math.gcd(1024, t_all)
    pb, pv, g_all = _inproj_b(u, w_b, b_b, up2, bias2, tmb, dk_, dv_)

    a1 = _conv_fwd(pa, conv_w8, conv_b, nb, s_len)
    lr_blk = (2 * dk_) // LANE
    pb3, pv3 = pb.reshape(nb, l_len, 2 * dk_ + LANE), pv.reshape(nb, l_len, dv_)
    o_f, zs_f, b_f, o_b, zs_b, b_b2 = _gla_fwd(pb3, pv3, g_all.reshape(nb, l_len, 2 * dk_), nb, s_len, c_len,
                                               dk_, dv_)

    conv_proj, gla_proj, w_out = proj(a1) if callable(proj) else proj
    tt = math.gcd(256, s_len)
    (dp_a2, da1, d_o, gx1, merged, dmo, ycin, dyconv, ogin, dygla, small) = _tail(
        a1, pa, o_f.reshape(tl, dv_), o_b.reshape(tl, dv_), x2, tgt2, mod, conv_proj, gla_proj, w_out, ln_g, ln_b,
        gla_norm_g, final_norm_g, nb, tt, 2)

    lat3 = lambda a: a.reshape(nb, s_len, a.shape[-1])
    tnw = _tile(d, 512)
    d_w_out, _ = _matmul_tn_whole("dw_out", lat3(merged), lat3(dmo), s_len, tnw, False)
    d_conv_proj, _ = _matmul_tn_whole("dw_conv_proj", lat3(ycin), lat3(dyconv), s_len, tnw, False)
    d_gla_proj, _ = _matmul_tn_whole("dw_gla_proj", lat3(ogin), lat3(dygla), s_len, tnw, False)

    dp_a1, d_conv_w8, d_conv_b = _conv_bwd(pa, da1, conv_w8, nb, s_len)
    gl = _gla_bwd(pb3, pv3, d_o.reshape(nb, s_len, dv_), (zs_f, b_f, zs_b, b_b2), nb, s_len, c_len, dk_, dv_)
    gl = [g_.reshape(t_all, g_.shape[-1]) for g_ in gl]
    dp_b, d_up2, d_bias2 = _decay_bwd(pb, up2, bias2, gl[0:4], gl[4:8], tiles, lr_blk, dk_, dv_)

    dw_a1, db_a1 = _matmul_tn_whole("dw_a1", u3, lat3(dp_a1), s_len, tnw, True)
    dw_a2, db_a2 = _matmul_tn_whole("dw_a2", u3, lat3(dp_a2), s_len, tnw, True)
    dw_b, db_b = _matmul_tn("dw_b", u, dp_b, t_all, tmb, nbw)
    grads = dict(w_a1=dw_a1, w_a2=dw_a2, w_b=dw_b, conv_w8=d_conv_w8, conv_proj=d_conv_proj, up2=d_up2,
                 gla_proj=d_gla_proj, w_out=d_w_out)

    tka = _tile(2 * d, 2048)
    du_a1 = _matmul_nt("du_a1", dp_a1, w_a, 0, tma, tka, after=on_grads(grads) if on_grads else ())
    du_a2 = _matmul_nt("du_a2", dp_a2, w_a, (2 * d) // tka, tma, tka, after=on_du_a1(du_a1) if on_du_a1 else ())
    du_b = _matmul_nt("du_b", dp_b, w_b, 0, tmb, nbw)
    grad_x2, dmod_ss, d_norm_g = _norm_bwd(x2, ctx2, mod, norm_g, [du_a1, du_a2], du_b, gx1, tiles)
    d_ada_w8, d_ada_b, d_cv = _ada_bwd(cv, ada_w8, dmod_ss, small, nb)

    return dict(
        grads, grad_x=grad_x2.reshape(nb, s_len, d), small=small, cv=d_cv, ada_w8=d_ada_w8, ada_b=d_ada_b,
        norm_g=d_norm_g, b_a1=db_a1, b_a2=db_a2, b_b=db_b, conv_b=d_conv_b, bias2=d_bias2)


def _regroup_pieces(d, r, wshard):
    cb = d // N_DEV
    segs = []
    for j in range(N_DEV):
        segs.append((j * cb, cb, 0, 2 * j * cb))
    for j in range(N_DEV):
        segs.append((d + j * cb, cb, 0, (2 * j + 1) * cb))
    segs += [(2 * d, d, 0, 2 * d), (3 * d, 2 * d + 2 * r, 1, 0), (5 * d + 2 * r, 3 * d, 0, 3 * d)]
    pieces = []
    for o0, w, dst, d0 in segs:
        lo = o0
        while lo < o0 + w:
            j = lo // wshard
            hi = min(o0 + w, (j + 1) * wshard)
            pieces.append((j, lo - j * wshard, hi - lo, dst, d0 + lo - o0))
            lo = hi
    return pieces


def _regroup(o, d, r):
    n_in = 8 * d + 2 * r
    parts = ([], [])
    for _, s0, n, dst, _ in sorted(_regroup_pieces(d, r, n_in), key=lambda p: (p[3], p[4])):
        parts[dst].append(o[..., s0:s0 + n])
    pad = jnp.zeros(o.shape[:-1] + (LANE - 2 * r,), o.dtype)
    return jnp.concatenate(parts[0], axis=-1), jnp.concatenate(parts[1] + [pad], axis=-1)


def _unshard_w_in(g_win, d, r, after=()):
    n_sh, _, ws = g_win.shape
    nbw = 2 * d + LANE
    pieces = _regroup_pieces(d, r, ws)
    tr = math.gcd(d, 256)

    def body(g_ref, *rest):
        a_ref, b_ref = rest[len(after):]
        dsts = (a_ref, b_ref)
        for j, s0, n, dst, d0 in pieces:
            dsts[dst][:, pl.ds(d0, n)] = g_ref[j, :, pl.ds(s0, n)]
        b_ref[:, pl.ds(2 * d + 2 * r, LANE - 2 * r)] = jnp.zeros((tr, LANE - 2 * r), b_ref.dtype)

    return pl.pallas_call(
        body, name="unshard_w_in", grid=(d // tr,),
        in_specs=[pl.BlockSpec((n_sh, tr, ws), lambda i: (0, i, 0))] + [_ANY] * len(after),
        out_specs=(pl.BlockSpec((tr, 6 * d), lambda i: (i, 0)), pl.BlockSpec((tr, nbw), lambda i: (i, 0))),
        out_shape=(jax.ShapeDtypeStruct((d, 6 * d), g_win.dtype), jax.ShapeDtypeStruct((d, nbw), g_win.dtype)),
        compiler_params=_params())(g_win, *after)


def _reshard_w_in(dwt_a1, dwt_a2, dwt_b, d, r):
    ws = (8 * d + 2 * r) // N_DEV
    pieces = _regroup_pieces(d, r, ws)
    tc = math.gcd(d, 256)

    def body(a1_ref, a2_ref, b_ref, o_ref):
        for j, s0, n, dst, d0 in pieces:
            if dst == 1:
                src = b_ref[pl.ds(d0, n), :]
            elif d0 < 2 * d:
                src = a1_ref[pl.ds(d0, n), :]
            else:
                src = a2_ref[pl.ds(d0 - 2 * d, n), :]
            o_ref[j, pl.ds(s0, n), :] = src

    col = lambda h: pl.BlockSpec((h, tc), lambda i: (0, i))
    return pl.pallas_call(
        body, name="reshard_w_in", grid=(d // tc,),
        in_specs=[col(2 * d), col(4 * d), col(2 * d + LANE)],
        out_specs=pl.BlockSpec((N_DEV, ws, tc), lambda i: (0, 0, i)),
        out_shape=jax.ShapeDtypeStruct((N_DEV, ws, d), dwt_b.dtype),
        compiler_params=_params())(dwt_a1, dwt_a2, dwt_b)


_SMALL = ("c_ctx", "ada_b", "norm_g", "b_in", "conv_b", "conv_ln_g", "conv_ln_b", "decay_bias_fwd",
          "decay_bias_bwd", "gla_norm_g", "final_norm_g")


def _small_layout(d, r):
    sizes = dict(c_ctx=d, ada_b=3 * d, norm_g=d, b_in=8 * d + 2 * r, conv_b=d, conv_ln_g=d, conv_ln_b=d,
                 decay_bias_fwd=d // 2, decay_bias_bwd=d // 2, gla_norm_g=d // HEADS, final_norm_g=d, loss=1)
    table, off = {}, 0
    for name in _SMALL + ("loss",):
        table[name] = (off, sizes[name])
        off += -(-sizes[name] // LANE) * LANE
    return table, off


def _pack_small(g, nb, d, r):
    table, width = _small_layout(d, r)
    hv = d // HEADS
    pieces = _regroup_pieces(d, r, 8 * d + 2 * r)
    names = ("small", "cv", "ada_b", "norm_g", "b_a1", "b_a2", "b_b", "conv_b", "bias2")

    def body(sm, cv, ab, ng, ba1, ba2, bb, cvb, b2, o_ref):
        o_ref[...] = jnp.zeros_like(o_ref)

        def put(name, val):
            off, n = table[name]
            o_ref[:, pl.ds(off, n)] = val

        put("c_ctx", cv[nb:nb + 1, :])
        put("ada_b", ab[...])
        put("norm_g", ng[...])
        off_b = table["b_in"][0]
        for _, s0, n, dst, d0 in pieces:
            if dst == 1:
                src = bb[:, pl.ds(d0, n)]
            elif d0 < 2 * d:
                src = ba1[:, pl.ds(d0, n)]
            else:
                src = ba2[:, pl.ds(d0 - 2 * d, n)]
            o_ref[:, pl.ds(off_b + s0, n)] = src
        put("conv_b", cvb[...])
        put("conv_ln_g", sm[1:2, :])
        put("conv_ln_b", sm[2:3, :])
        put("decay_bias_fwd", b2[:, 0:d // 2])
        put("decay_bias_bwd", b2[:, d // 2:d])
        gn = sm[3:4, 0:hv]
        for h in range(1, HEADS):
            gn = gn + sm[3:4, h * hv:(h + 1) * hv]
        put("gla_norm_g", gn)
        put("final_norm_g", sm[0:1, :])
        put("loss", sm[4:5, 0:1])

    return pl.pallas_call(body, name="pack_small", out_shape=jax.ShapeDtypeStruct((1, width), F32),
                          compiler_params=_params())(*[g[k] for k in names])


def _small_adam(parts, ws, ms, vs, d, r):
    table, width = _small_layout(d, r)
    n_parts = parts.shape[0]
    k = len(_SMALL)

    def body(p_ref, *refs):
        w_refs, m_refs, v_refs = refs[0:k], refs[k:2 * k], refs[2 * k:3 * k]
        outs = refs[3 * k:]
        tot = p_ref[0]
        for i in range(1, n_parts):
            tot = tot + p_ref[i]
        for i, name in enumerate(_SMALL):
            off, n = table[name]
            g = tot[:, off:off + n]
            outs[i][...] = g
            outs[k + i][...], outs[2 * k + i][...], outs[3 * k + i][...] = _adamw(
                g, w_refs[i][...], m_refs[i][...], v_refs[i][...])
        off, _ = table["loss"]
        outs[4 * k][...] = tot[:, off:off + 1]

    shapes = [jax.ShapeDtypeStruct(w.shape, F32) for w in ws]
    res = pl.pallas_call(body, name="small_adam", out_shape=tuple(shapes * 4 + [jax.ShapeDtypeStruct((1, 1), F32)]),
                         compiler_params=_params())(parts, *ws, *ms, *vs)
    return res[0:k], res[k:2 * k], res[2 * k:3 * k], res[3 * k:4 * k], res[4 * k]


def _mesh_pos():
    return lax.axis_index("x"), lax.axis_index("y"), lax.axis_index("c")


def _all_gather(arrs):
    n = len(arrs)
    ns = 9
    split = [a.ndim == 2 and a.shape[0] % 32 == 0 for a in arrs]

    def body(*refs):
        ins, outs = refs[:n], refs[n:2 * n]
        send_sems, recv_sems, local_sems = refs[2 * n:]
        x, y, c = _mesh_pos()
        me, sibling = (x, y, c), (x, y, 1 - c)
        xn, yn, dg = (1 - x, y, c), (x, 1 - y, c), (1 - x, 1 - y, c)
        other = lambda pos: (pos[0], pos[1], 1 - c)

        def slot(a, pos, half):
            ref = outs[a].at[4 * pos[0] + 2 * pos[1] + pos[2]]
            if half is None:
                return ref
            rows = arrs[a].shape[0] // 2
            return ref.at[pl.ds(half * rows, rows)]

        def copy(a, k, block, to, src=None, half=None):
            dst = slot(a, block, half)
            return pltpu.make_async_remote_copy(
                src_ref=dst if src is None else src, dst_ref=dst,
                send_sem=send_sems.at[ns * a + k], recv_sem=recv_sems.at[ns * a + k],
                device_id=to, device_id_type=MESH)

        h0 = lambda a: 0 if split[a] else None
        mine = [pltpu.make_async_copy(ins[a], slot(a, me, None), local_sems.at[a]) for a in range(n)]
        for cp in mine:
            cp.start()
        sent = []
        for a in range(n):
            sent += [copy(a, 0, me, sibling, src=ins[a]), copy(a, 1, me, xn, src=ins[a]),
                     copy(a, 2, me, yn, src=ins[a])]
        for cp in sent:
            cp.start()

        def pass_on(cp):
            cp.start()
            sent.append(cp)

        for a in range(n):
            copy(a, 1, xn, me).wait_recv()
            pass_on(copy(a, 3, xn, sibling))
            pass_on(copy(a, 4, xn, yn, half=h0(a)))
        for a in range(n):
            copy(a, 2, yn, me).wait_recv()
            pass_on(copy(a, 5, yn, sibling))
            if split[a]:
                pass_on(copy(a, 6, yn, xn, half=1))
        for a in range(n):
            copy(a, 4, dg, me, half=h0(a)).wait_recv()
            pass_on(copy(a, 7, dg, sibling, half=h0(a)))
            if split[a]:
                copy(a, 6, dg, me, half=1).wait_recv()
                pass_on(copy(a, 8, dg, sibling, half=1))
        for a in range(n):
            copy(a, 0, sibling, me).wait_recv()
            copy(a, 3, other(xn), me).wait_recv()
            copy(a, 5, other(yn), me).wait_recv()
            copy(a, 7, other(dg), me, half=h0(a)).wait_recv()
            if split[a]:
                copy(a, 8, other(dg), me, half=1).wait_recv()
        for cp in sent:
            cp.wait_send()
        for cp in mine:
            cp.wait()

    return pl.pallas_call(
        body, name="all_gather",
        out_shape=tuple(jax.ShapeDtypeStruct((N_DEV,) + a.shape, a.dtype) for a in arrs),
        in_specs=[_ANY] * n, out_specs=tuple([_ANY] * n),
        scratch_shapes=[pltpu.SemaphoreType.DMA((ns * n,)), pltpu.SemaphoreType.DMA((ns * n,)),
                        pltpu.SemaphoreType.DMA((n,))],
    )(*arrs)


def _exchange_sibling(arrs):
    n = len(arrs)

    def body(*refs):
        ins, outs = refs[:n], refs[n:2 * n]
        send_sems, recv_sems = refs[2 * n:]
        x, y, c = _mesh_pos()
        copies = [pltpu.make_async_remote_copy(
            src_ref=ins[a].at[2 * k + (1 - c)], dst_ref=outs[a].at[k],
            send_sem=send_sems.at[4 * a + k], recv_sem=recv_sems.at[4 * a + k],
            device_id=(x, y, 1 - c), device_id_type=MESH) for a in range(n) for k in range(4)]
        for cp in copies:
            cp.start()
        for cp in copies:
            cp.wait_recv()
        for cp in copies:
            cp.wait_send()

    return pl.pallas_call(
        body, name="grad_exchange_sibling",
        out_shape=tuple(jax.ShapeDtypeStruct((4,) + a.shape[1:], a.dtype) for a in arrs),
        in_specs=[_ANY] * n, out_specs=tuple([_ANY] * n),
        scratch_shapes=[pltpu.SemaphoreType.DMA((4 * n,)), pltpu.SemaphoreType.DMA((4 * n,))],
    )(*arrs)


def _elementwise_tile(r, cdim, cols=2 * LANE):
    if r % 8 == 0 and r > 256:
        return math.gcd(r, 256), cdim
    if r > 256 and cdim % cols == 0:
        return r, cols
    return r, cdim


def _pair_sum(name, mine, theirs):
    _, r, cdim = mine.shape
    tr, tc = _elementwise_tile(r, cdim)

    def body(c_ref, m_ref, t_ref, o_ref):
        o_ref[...] = (m_ref[...].astype(F32) + t_ref[...].astype(F32)).astype(o_ref.dtype)

    return pl.pallas_call(
        body, name=name,
        grid_spec=pltpu.PrefetchScalarGridSpec(
            num_scalar_prefetch=1, grid=(r // tr, cdim // tc),
            in_specs=[pl.BlockSpec((4, None, tr, tc), lambda i, j, c_ref: (0, c_ref[0], i, j)),
                      pl.BlockSpec((4, tr, tc), lambda i, j, c_ref: (0, i, j))],
            out_specs=pl.BlockSpec((4, tr, tc), lambda i, j, c_ref: (0, i, j))),
        out_shape=jax.ShapeDtypeStruct((4, r, cdim), mine.dtype),
        compiler_params=_params())(lax.axis_index("c").reshape(1), mine.reshape(4, 2, r, cdim), theirs)


def _pair_sum_small(mines, theirs):
    n = len(mines)

    def body(*refs):
        c = lax.axis_index("c")
        for i in range(n):
            m_ref, t_ref, o_ref = refs[i], refs[n + i], refs[2 * n + i]
            own = jnp.where(c == 0, m_ref[:, 0].astype(F32), m_ref[:, 1].astype(F32))
            o_ref[...] = (own + t_ref[...].astype(F32)).astype(o_ref.dtype)

    return pl.pallas_call(
        body, name="pair_sum_small_weights",
        out_shape=tuple(jax.ShapeDtypeStruct(t.shape, m.dtype) for m, t in zip(mines, theirs)),
        compiler_params=_params())(*[m.reshape((4, 2) + m.shape[1:]) for m in mines], *theirs)


_HBM = pl.BlockSpec(memory_space=pltpu.HBM)
_SEM = pl.BlockSpec(memory_space=pltpu.SEMAPHORE)


def _copies_start(name, srcs, lands, make_copies, n_sems):
    n, m = len(srcs), len(lands)

    def body(*refs):
        ins = refs[:n + m]
        send_sems, recv_sems = refs[n + m], refs[n + m + 1]
        for cp in make_copies(ins[:n], ins[n:], send_sems, recv_sems):
            cp.start()
        refs[-1][...] = jnp.zeros_like(refs[-1])

    res = pl.pallas_call(
        body, name=name,
        out_shape=(pltpu.SemaphoreType.DMA((n_sems,)), pltpu.SemaphoreType.DMA((n_sems,)),
                   *[pltpu.HBM(a.shape, a.dtype) for a in (*srcs, *lands)], jax.ShapeDtypeStruct((8, LANE), F32)),
        in_specs=[_HBM] * (n + m),
        out_specs=(_SEM, _SEM, *[_HBM] * (n + m), pl.BlockSpec(memory_space=pltpu.VMEM)),
        input_output_aliases={i: 2 + i for i in range(n + m)},
        compiler_params=pltpu.CompilerParams(has_side_effects=pltpu.SideEffectType.DATAFLOW_SIDE_EFFECTING),
    )(*[pltpu.with_memory_space_constraint(a, pltpu.HBM) for a in (*srcs, *lands)])
    return res[0], res[1], res[2:2 + n], res[2 + n:2 + n + m], res[-1]


def _copies_wait(name, started, after, make_copies):
    send_sems, recv_sems, srcs, lands, _ = started
    n, m = len(srcs), len(lands)

    def body(*refs):
        ins = refs[:n + m]
        for cp in make_copies(ins[:n], ins[n:], refs[n + m], refs[n + m + 1]):
            cp.wait_send()
            cp.wait_recv()

    res = pl.pallas_call(
        body, name=name,
        out_shape=tuple(pltpu.HBM(a.shape, a.dtype) for a in (*srcs, *lands)),
        in_specs=[_HBM] * (n + m) + [_SEM, _SEM] + [_ANY] * len(after),
        out_specs=tuple([_HBM] * (n + m)),
        input_output_aliases={i: i for i in range(n + m)},
        compiler_params=pltpu.CompilerParams(has_side_effects=pltpu.SideEffectType.DATAFLOW_SIDE_EFFECTING),
    )(*srcs, *lands, send_sems, recv_sems, *after)
    return res[:n], res[n:]


def _gather_copies(srcs, lands, send_sems, recv_sems):
    x, y, c = _mesh_pos()
    me_i = 4 * x + 2 * y + c
    copies = []
    for rel in range(1, N_DEV):
        peer = (1 - x if rel & 4 else x, 1 - y if rel & 2 else y, 1 - c if rel & 1 else c)
        for a in range(len(srcs)):
            copies.append(pltpu.make_async_remote_copy(
                src_ref=srcs[a], dst_ref=lands[a].at[me_i], send_sem=send_sems.at[7 * a + rel - 1],
                recv_sem=recv_sems.at[7 * a + rel - 1], device_id=peer, device_id_type=MESH))
    return copies


def _sibling_copies(srcs, lands, send_sems, recv_sems):
    x, y, c = _mesh_pos()
    return [pltpu.make_async_remote_copy(
        src_ref=srcs[a].at[2 * k + (1 - c)], dst_ref=lands[a].at[k], send_sem=send_sems.at[4 * a + k],
        recv_sem=recv_sems.at[4 * a + k], device_id=(x, y, 1 - c), device_id_type=MESH)
        for a in range(len(srcs)) for k in range(4)]


def _chip_copies(srcs, lands, send_sems, recv_sems):
    x, y, c = _mesh_pos()
    my_chip = 2 * x + y
    copies = []
    for rel in range(1, 4):
        px = 1 - x if rel & 2 else x
        py = 1 - y if rel & 1 else y
        for a in range(len(srcs)):
            copies.append(pltpu.make_async_remote_copy(
                src_ref=srcs[a].at[2 * px + py], dst_ref=lands[a].at[my_chip], send_sem=send_sems.at[3 * a + rel - 1],
                recv_sem=recv_sems.at[3 * a + rel - 1], device_id=(px, py, c), device_id_type=MESH))
    return copies


def _sum_adam(name, parts, w, m, v, own=None):
    unit_mid = w.ndim == 3
    _, r, cdim = parts.shape
    n_parts = parts.shape[0]
    tr, tc = _elementwise_tile(r, cdim, (4 if unit_mid else 2) * LANE)
    extra = [] if own is None else [own]

    def body(p_ref, *refs):
        w_ref, m_ref, v_ref, g_ref, d_ref, nm_ref, nv_ref = refs[len(extra):]
        if own is None:
            part = lambda k: p_ref[k].astype(F32)
        else:
            my_chip = 2 * lax.axis_index("x") + lax.axis_index("y")
            part = lambda k: jnp.where(my_chip == k, refs[0][...], p_ref[k]).astype(F32)
        g = part(0)
        for k in range(1, n_parts):
            g = g + part(k)
        if unit_mid:
            g = g.reshape(tr, 1, tc)
        g_ref[...] = g
        d_ref[...], nm_ref[...], nv_ref[...] = _adamw(g, w_ref[...], m_ref[...], v_ref[...])

    blk = (pl.BlockSpec((tr, 1, tc), lambda i, j: (i, 0, j)) if unit_mid
           else pl.BlockSpec((tr, tc), lambda i, j: (i, j)))
    o = jax.ShapeDtypeStruct(w.shape, F32)
    return pl.pallas_call(
        body, name=name, grid=(r // tr, cdim // tc),
        in_specs=[pl.BlockSpec((n_parts, tr, tc), lambda i, j: (0, i, j))]
        + [pl.BlockSpec((None, tr, tc), lambda i, j: (2 * lax.axis_index("x") + lax.axis_index("y"), i, j))] * len(extra)
        + [blk, blk, blk],
        out_specs=(blk, blk, blk, blk), out_shape=(o, o, o, o),
        compiler_params=_params())(parts, *extra, w, m, v)


def _sum_adam_small(items):
    n = len(items)

    def body(*refs):
        my_chip = 2 * lax.axis_index("x") + lax.axis_index("y")
        for i in range(n):
            p_ref, own_ref, w_ref, m_ref, v_ref = refs[5 * i:5 * i + 5]
            g_ref, d_ref, nm_ref, nv_ref = refs[5 * n + 4 * i:5 * n + 4 * i + 4]
            g = None
            for k in range(p_ref.shape[0]):
                part = jnp.where(my_chip == k, own_ref[k], p_ref[k]).astype(F32)
                g = part if g is None else g + part
            g_ref[...] = g
            d_ref[...], nm_ref[...], nv_ref[...] = _adamw(g, w_ref[...], m_ref[...], v_ref[...])

    out_shape = tuple(jax.ShapeDtypeStruct(it[2].shape, F32) for it in items for _ in range(4))
    res = pl.pallas_call(body, name="adam_small_weights", out_shape=out_shape,
                         compiler_params=_params())(*[a for it in items for a in it])
    return [res[4 * i:4 * i + 4] for i in range(n)]


_WEIGHTS = ("c_ctx", "ada_w", "ada_b", "norm_g", "w_in", "b_in", "conv_w", "conv_b", "conv_ln_g", "conv_ln_b",
            "conv_proj", "decay_up_fwd", "decay_bias_fwd", "decay_up_bwd", "decay_bias_bwd", "gla_norm_g",
            "gla_proj", "w_out", "final_norm_g")


def _as2d(a):
    if a.ndim == 1:
        return a.reshape(1, -1)
    return a.reshape(-1, a.shape[-1])


def kernel(x, c, ctx, c_ctx, ada_w, ada_b, norm_g, w_in, b_in, conv_w, conv_b, conv_ln_g, conv_ln_b, conv_proj, decay_up_fwd, decay_bias_fwd, decay_up_bwd, decay_bias_bwd, gla_norm_g, gla_proj, w_out, final_norm_g, loss_target, m_c_ctx, m_ada_w, m_ada_b, m_norm_g, m_w_in, m_b_in, m_conv_w, m_conv_b, m_conv_ln_g, m_conv_ln_b, m_conv_proj, m_decay_up_fwd, m_decay_bias_fwd, m_decay_up_bwd, m_decay_bias_bwd, m_gla_norm_g, m_gla_proj, m_w_out, m_final_norm_g, v_c_ctx, v_ada_w, v_ada_b, v_norm_g, v_w_in, v_b_in, v_conv_w, v_conv_b, v_conv_ln_g, v_conv_ln_b, v_conv_proj, v_decay_up_fwd, v_decay_bias_fwd, v_decay_up_bwd, v_decay_bias_bwd, v_gla_norm_g, v_gla_proj, v_w_out, v_final_norm_g):
    env = dict(locals())
    wts = {k: env[k] for k in _WEIGHTS}
    d = x.shape[-1]
    r = decay_up_fwd.shape[1]
    dk_ = d // 2

    ds, dks = d // N_DEV, dk_ // N_DEV
    g_win, g_ada, conv_w8, g_up = _all_gather(
        [w_in[0].astype(BF16), ada_w[0].astype(BF16), conv_w[0],
         jnp.concatenate([decay_up_fwd[0], decay_up_bwd[0]], axis=1)])
    proj_own = [conv_proj[0].astype(BF16), gla_proj[0].astype(BF16), w_out[0].astype(BF16)]
    me_i = 4 * lax.axis_index("x") + 2 * lax.axis_index("y") + lax.axis_index("c")
    proj_lands = [lax.dynamic_update_slice(lax.empty((N_DEV,) + a.shape, a.dtype), a[None], (me_i, 0, 0))
                  for a in proj_own]
    proj_start = _copies_start("proj_gather_start", proj_own, proj_lands, _gather_copies, 7 * 3)

    def proj(after):
        _, lands = _copies_wait("proj_gather_wait", proj_start, (after,), _gather_copies)
        return [w.reshape(d, d) for w in lands]

    w_a, w_b = _unshard_w_in(g_win, d, r, after=(proj_start[4],))
    up_f = g_up[:, :, 0:dks].transpose(1, 0, 2).reshape(r, dk_)
    up_b = g_up[:, :, dks:].transpose(1, 0, 2).reshape(r, dk_)
    up2 = jnp.zeros((LANE, 2 * dk_), F32).at[0:r, 0:dk_].set(up_f).at[r:2 * r, dk_:].set(up_b)
    bias2 = jnp.concatenate([decay_bias_fwd, decay_bias_bwd], axis=1)
    b_a, b_b = _regroup(b_in, d, r)

    comm = {}

    def on_grads(gr):
        d_up = jnp.concatenate([gr["up2"][0:r, 0:dk_].reshape(r, N_DEV, dks).transpose(1, 0, 2),
                                gr["up2"][r:2 * r, dk_:].reshape(r, N_DEV, dks).transpose(1, 0, 2)], axis=2)
        mine = [_reshard_w_in(gr["w_a1"], gr["w_a2"], gr["w_b"], d, r), gr["conv_proj"].reshape(N_DEV, ds, d),
                gr["gla_proj"].reshape(N_DEV, ds, d), gr["w_out"].reshape(N_DEV, ds, d), gr["conv_w8"], d_up]
        lands = [lax.empty((4,) + a.shape[1:], a.dtype) for a in mine]
        comm["sibling"] = _copies_start("grad_sibling_start", mine, lands, _sibling_copies, 4 * len(mine))
        return (comm["sibling"][4],)

    def on_du_a1(du_a1):
        mine, theirs = _copies_wait("grad_sibling_wait", comm["sibling"], (du_a1,), _sibling_copies)
        sums = [_pair_sum("pair_sum_w_in", mine[0], theirs[0])] + list(_pair_sum_small(mine[1:], theirs[1:]))
        lands = [lax.empty(a.shape, a.dtype) for a in sums]
        comm["chips"] = _copies_start("grad_chips_start", sums, lands, _chip_copies, 3 * len(sums))
        return (comm["chips"][4],)

    g = _local_step(x, c, ctx, loss_target, c_ctx, g_ada, ada_b, norm_g[0:1], w_a, b_a, w_b, b_b,
                    conv_w8, conv_b, conv_ln_g, conv_ln_b, up2, bias2, gla_norm_g, final_norm_g.reshape(1, d),
                    proj, on_grads, on_du_a1)

    pack = _pack_small(g, x.shape[0], d, r)
    pack_lands = [lax.dynamic_update_slice(lax.empty((N_DEV,) + pack.shape, F32), pack[None], (me_i, 0, 0))]
    small_start = _copies_start("small_gather_start", [pack], pack_lands, _gather_copies, 7)

    (their_ada,) = _exchange_sibling([g["ada_w8"]])
    ada_sum = _pair_sum("pair_sum_ada_w", g["ada_w8"], their_ada)
    ada_start = _copies_start("ada_chips_start", [ada_sum], [lax.empty(ada_sum.shape, ada_sum.dtype)],
                              _chip_copies, 3)
    own, landed = _copies_wait("grad_chips_wait", comm["chips"], (ada_start[4],), _chip_copies)
    o_win, o_cp, o_gp, o_wo, o_cw, o_up = own
    x_win, x_cp, x_gp, x_wo, x_cw, x_up = landed

    out = {}

    def big(name, parts, wname, own=None):
        w2 = _as2d(wts[wname])
        res = _sum_adam(name, parts, w2, _as2d(env["m_" + wname]), _as2d(env["v_" + wname]), own)
        for pre, arr in zip(("grad_", "delta_", "new_m_", "new_v_"), res):
            out[pre + wname] = arr.reshape(wts[wname].shape)

    as_rows = lambda a: jnp.transpose(a, (2, 0, 1))
    res = _sum_adam("adam_w_in", x_win, as_rows(w_in), as_rows(m_w_in), as_rows(v_w_in), o_win)
    for pre, arr in zip(("grad_", "delta_", "new_m_", "new_v_"), res):
        out[pre + "w_in"] = jnp.transpose(arr, (1, 2, 0))
    small_w = (("conv_proj", x_cp, o_cp), ("gla_proj", x_gp, o_gp), ("w_out", x_wo, o_wo), ("conv_w", x_cw, o_cw),
               ("decay_up_fwd", x_up[:, :, 0:dks], o_up[:, :, 0:dks]),
               ("decay_up_bwd", x_up[:, :, dks:], o_up[:, :, dks:]))
    small_res = _sum_adam_small([(p, o, _as2d(wts[k]), _as2d(env["m_" + k]), _as2d(env["v_" + k]))
                                 for k, p, o in small_w])
    for (k, _, _), arrs in zip(small_w, small_res):
        for pre, arr in zip(("grad_", "delta_", "new_m_", "new_v_"), arrs):
            out[pre + k] = arr.reshape(wts[k].shape)

    _, (packs,) = _copies_wait("small_gather_wait", small_start, (res[0], out["grad_w_out"]), _gather_copies)
    row = lambda a: a.reshape(1, -1)
    sg, sd, sm, sv, loss = _small_adam(packs, [row(wts[k]) for k in _SMALL], [row(env["m_" + k]) for k in _SMALL],
                                       [row(env["v_" + k]) for k in _SMALL], d, r)
    for i, k in enumerate(_SMALL):
        for pre, arrs in (("grad_", sg), ("delta_", sd), ("new_m_", sm), ("new_v_", sv)):
            out[pre + k] = arrs[i].reshape(wts[k].shape)
    loss = loss.reshape(())

    (o_ada,), (x_ada,) = _copies_wait("ada_chips_wait", ada_start, (res[0], out["grad_w_out"], out["grad_b_in"]),
                                      _chip_copies)
    big("adam_ada_w", x_ada, "ada_w", o_ada)

    return (loss, g["grad_x"], *[out["grad_" + k] for k in _WEIGHTS], *[out["delta_" + k] for k in _WEIGHTS],
            *[out["new_m_" + k] for k in _WEIGHTS], *[out["new_v_" + k] for k in _WEIGHTS])
```

```python
import functools
import math

import jax
import jax.numpy as jnp
from jax import lax
from jax.experimental import pallas as pl
from jax.experimental.pallas import tpu as pltpu

F32 = jnp.float32
BF16 = jnp.bfloat16
MESH = pl.DeviceIdType.MESH

N_DEV = 8
GRID_W = 64
CHUNK = 128
HEADS = 4
EPS = 1e-6
GATE_TAU = 16.0
LANE = 128
ADAM_LR, ADAM_B1, ADAM_B2, ADAM_EPS, ADAM_WD, ADAM_STEP = 0.001, 0.9, 0.999, 1e-08, 0.01, 10
VMEM_LIMIT = 60 * 1024 * 1024
_ANY = pl.BlockSpec(memory_space=pl.ANY)


def _params(**kw):
    return pltpu.CompilerParams(vmem_limit_bytes=VMEM_LIMIT, **kw)


def _tile(n, pref):
    t = (min(pref, n) // LANE) * LANE
    while t >= LANE:
        if n % t == 0:
            return t
        t -= LANE
    return n


def _mm(a, b):
    return jnp.dot(a.astype(BF16), b.astype(BF16), preferred_element_type=F32)


def _mm_nt(a, b):
    return lax.dot_general(a.astype(BF16), b.astype(BF16), (((1,), (1,)), ((), ())), preferred_element_type=F32)


def _mm_tn(a, b):
    return lax.dot_general(a.astype(BF16), b.astype(BF16), (((0,), (0,)), ((), ())), preferred_element_type=F32)


def _sigmoid(x):
    return 0.5 * jnp.tanh(0.5 * x) + 0.5


def _dsilu(x, s):
    return s * (1.0 + x * (1.0 - s))


def _adamw(g, w, m, v):
    bc1 = 1.0 - ADAM_B1 ** ADAM_STEP
    bc2 = 1.0 - ADAM_B2 ** ADAM_STEP
    mn = ADAM_B1 * m + (1.0 - ADAM_B1) * g
    vn = ADAM_B2 * v + (1.0 - ADAM_B2) * (g * g)
    delta = -ADAM_LR * ((mn / bc1) / (jnp.sqrt(vn / bc2) + ADAM_EPS) + ADAM_WD * w)
    return delta, mn, vn


def _rowsel(table, idx, n):
    out = table[0:1, :]
    for r in range(1, n):
        out = jnp.where(idx == r, table[r:r + 1, :], out)
    return out


def _ada_fwd(cv, ada_w8, ada_b):
    n_sh, _, ws = ada_w8.shape

    def body(cv_ref, w_ref, b_ref, o_ref):
        c = cv_ref[...]
        sv = c * _sigmoid(c)
        for j in range(n_sh):
            cols = pl.ds(j * ws, ws)
            o_ref[:, cols] = _mm(sv, w_ref[j]) + b_ref[:, cols]

    return pl.pallas_call(body, name="ada_fwd", out_shape=jax.ShapeDtypeStruct((cv.shape[0], n_sh * ws), F32),
                          compiler_params=_params())(cv, ada_w8, ada_b)


def _ada_bwd(cv, ada_w8, dmod_ss, small, nb):
    n_sh, d, ws = ada_w8.shape

    def body(cv_ref, w_ref, dm_ref, sm_ref, dw_ref, db_ref, dc_ref):
        c = cv_ref[...]
        s = _sigmoid(c)
        sv = c * s
        dm = jnp.concatenate([dm_ref[:, 0:2 * d], sm_ref[8:16, :]], axis=1)
        db_ref[...] = jnp.sum(dm, axis=0, keepdims=True)
        sv_t = jnp.transpose(sv)
        dsv = None
        for j in range(n_sh):
            dmj = dm[:, j * ws:(j + 1) * ws]
            dw = sv_t[:, 0:1] * dmj[0:1, :]
            for row in range(1, nb + 1):
                dw = dw + sv_t[:, row:row + 1] * dmj[row:row + 1, :]
            dw_ref[j] = dw.astype(dw_ref.dtype)
            part = _mm_nt(dmj, w_ref[j])
            dsv = part if dsv is None else dsv + part
        dc_ref[...] = dsv * _dsilu(c, s)

    return pl.pallas_call(
        body, name="ada_bwd",
        out_shape=(jax.ShapeDtypeStruct((n_sh, d, ws), BF16), jax.ShapeDtypeStruct((1, n_sh * ws), F32),
                   jax.ShapeDtypeStruct(cv.shape, F32)),
        compiler_params=_params())(cv, ada_w8, dmod_ss, small)


class _Tiles:
    def __init__(self, nb, s_len, c_len, tm, big):
        self.nb, self.tm, self.big = nb, tm, big
        self.lat, self.ctx = s_len // tm, c_len // tm
        self.pad = -(self.lat + self.ctx) % big
        self.per_ex = self.lat + self.ctx + self.pad
        self.n_all = nb * self.per_ex
        self.rows_per_ex = self.per_ex * tm

    def is_lat(self, i):
        return i % self.per_ex < self.lat

    def is_pad(self, i):
        return i % self.per_ex >= self.lat + self.ctx

    def lat_of_all(self, i):
        return (i // self.per_ex) * self.lat + jnp.minimum(i % self.per_ex, self.lat - 1)

    def ctx_of_all(self, i):
        return (i // self.per_ex) * self.ctx + jnp.clip(i % self.per_ex - self.lat, 0, self.ctx - 1)


def _norm_fwd(x2, ctx2, mod, norm_g, tiles):
    tl, d = x2.shape
    tc = ctx2.shape[0]
    nb, tm = tiles.nb, tiles.tm

    def body(x_ref, c_ref, mod_ref, g_ref, u_ref):
        i = pl.program_id(0)
        lat = tiles.is_lat(i)
        xv = jnp.where(lat, x_ref[...], c_ref[...])
        row = jnp.where(lat, i // tiles.per_ex, nb)
        m = _rowsel(mod_ref[...], row, nb + 1)
        shift, scale = m[:, 0:d], m[:, d:2 * d]
        rstd = lax.rsqrt(jnp.mean(xv * xv, axis=-1, keepdims=True) + EPS)
        u = xv * rstd * g_ref[...] * (1.0 + scale) + shift
        u_ref[...] = jnp.where(tiles.is_pad(i), 0.0, u).astype(BF16)

    return pl.pallas_call(
        body, name="norm_fwd", grid=(tiles.n_all,),
        in_specs=[pl.BlockSpec((tm, d), lambda i: (tiles.lat_of_all(i), 0)),
                  pl.BlockSpec((tm, d), lambda i: (tiles.ctx_of_all(i), 0)),
                  pl.BlockSpec(mod.shape, lambda i: (0, 0)),
                  pl.BlockSpec((1, d), lambda i: (0, 0))],
        out_specs=pl.BlockSpec((tm, d), lambda i: (i, 0)),
        out_shape=jax.ShapeDtypeStruct((tiles.n_all * tm, d), BF16),
        compiler_params=_params())(x2, ctx2, mod, norm_g)


def _norm_bwd(x2, ctx2, mod, norm_g, du_lat, du_b, gx1, tiles):
    tl, d = x2.shape
    nb, tm = tiles.nb, tiles.tm
    nrow = mod.shape[0]
    n_lat_in = len(du_lat)

    def body(x_ref, c_ref, mod_ref, g_ref, *refs):
        dl_refs = refs[:n_lat_in]
        d3_ref, gx_ref, gxo_ref, dmod_ref, dg_ref = refs[n_lat_in:]
        i = pl.program_id(0)

        @pl.when(i == 0)
        def _():
            dmod_ref[...] = jnp.zeros_like(dmod_ref)
            dg_ref[...] = jnp.zeros_like(dg_ref)

        lat = tiles.is_lat(i)
        xv = jnp.where(lat, x_ref[...], c_ref[...])
        row = jnp.where(lat, i // tiles.per_ex, nb)
        m = _rowsel(mod_ref[...], row, nb + 1)
        scale = m[:, d:2 * d]
        g = g_ref[...]
        dl = dl_refs[0][...].astype(F32)
        for ref in dl_refs[1:]:
            dl = dl + ref[...].astype(F32)
        du = jnp.where(tiles.is_pad(i), 0.0, d3_ref[...].astype(F32) + jnp.where(lat, dl, 0.0))
        rstd = lax.rsqrt(jnp.mean(xv * xv, axis=-1, keepdims=True) + EPS)
        xh = xv * rstd
        dshift = jnp.sum(du, axis=0, keepdims=True)
        dscale = jnp.sum(du * xh * g, axis=0, keepdims=True)
        dxn = du * (1.0 + scale)
        dg_ref[...] += jnp.sum(dxn * xh, axis=0, keepdims=True)
        dxh = dxn * g
        dx = rstd * (dxh - xh * jnp.mean(dxh * xh, axis=-1, keepdims=True))

        @pl.when(lat)
        def _():
            gxo_ref[...] = dx + gx_ref[...]

        for r in range(nb + 1):
            dmod_ref[r:r + 1, 0:d] += jnp.where(row == r, dshift, 0.0)
            dmod_ref[r:r + 1, d:2 * d] += jnp.where(row == r, dscale, 0.0)

    lat_map = lambda i: (tiles.lat_of_all(i), 0)
    lat_spec = pl.BlockSpec((tm, d), lat_map)
    return pl.pallas_call(
        body, name="norm_bwd", grid=(tiles.n_all,),
        in_specs=[lat_spec,
                  pl.BlockSpec((tm, d), lambda i: (tiles.ctx_of_all(i), 0)),
                  pl.BlockSpec(mod.shape, lambda i: (0, 0)),
                  pl.BlockSpec((1, d), lambda i: (0, 0))]
                 + [lat_spec] * n_lat_in
                 + [pl.BlockSpec((tm, d), lambda i: (i, 0)), lat_spec],
        out_specs=(lat_spec,
                   pl.BlockSpec((nrow, 3 * d), lambda i: (0, 0)),
                   pl.BlockSpec((1, d), lambda i: (0, 0))),
        out_shape=(jax.ShapeDtypeStruct((tl, d), F32), jax.ShapeDtypeStruct((nrow, 3 * d), F32),
                   jax.ShapeDtypeStruct((1, d), F32)),
        compiler_params=_params())(x2, ctx2, mod, norm_g, *du_lat, du_b, gx1)


def _matmul_bias(name, u3, w, b, s_len, tm, tn):
    nb = u3.shape[0]
    d, n = w.shape
    per = s_len // tm
    rows = nb * s_len

    def body(u_ref, w_ref, b_ref, o_ref):
        o_ref[...] = jnp.dot(u_ref[...], w_ref[...], preferred_element_type=F32) + b_ref[...]

    return pl.pallas_call(
        body, name=name, grid=(n // tn, rows // tm),
        in_specs=[pl.BlockSpec((None, tm, d), lambda j, i: (i // per, i % per, 0)),
                  pl.BlockSpec((d, tn), lambda j, i: (0, j)),
                  pl.BlockSpec((1, tn), lambda j, i: (0, j))],
        out_specs=pl.BlockSpec((tm, tn), lambda j, i: (i, j)),
        out_shape=jax.ShapeDtypeStruct((rows, n), F32),
        compiler_params=_params())(u3, w, b)


def _log_sigmoid(x):
    return jnp.minimum(x, 0.0) - jnp.log(1.0 + jnp.exp(-jnp.abs(x)))


def _inproj_b(u, w_b, b_b, up2, bias2, tm, dk_, dv_):
    t_all, d = u.shape
    nbw = w_b.shape[1]
    n2 = up2.shape[1]

    def body(u_ref, w_ref, b_ref, up_ref, bias_ref, qk_ref, v_ref, g_ref):
        full = jnp.dot(u_ref[...], w_ref[...], preferred_element_type=F32) + b_ref[...]
        lr = full[:, 2 * dk_ + dv_:nbw]
        qk_ref[:, 0:2 * dk_] = full[:, 0:2 * dk_]
        qk_ref[:, 2 * dk_:2 * dk_ + LANE] = lr
        v_ref[...] = full[:, 2 * dk_:2 * dk_ + dv_].astype(BF16)
        g_ref[...] = _log_sigmoid(_mm(lr, up_ref[...]) + bias_ref[...]) * (1.0 / GATE_TAU)

    whole = lambda a: pl.BlockSpec(a.shape, lambda i: (0, 0))
    return pl.pallas_call(
        body, name="inproj_b", grid=(t_all // tm,),
        in_specs=[pl.BlockSpec((tm, d), lambda i: (i, 0)), whole(w_b), whole(b_b), whole(up2), whole(bias2)],
        out_specs=(pl.BlockSpec((tm, 2 * dk_ + LANE), lambda i: (i, 0)), pl.BlockSpec((tm, dv_), lambda i: (i, 0)),
                   pl.BlockSpec((tm, n2), lambda i: (i, 0))),
        out_shape=(jax.ShapeDtypeStruct((t_all, 2 * dk_ + LANE), F32), jax.ShapeDtypeStruct((t_all, dv_), BF16),
                   jax.ShapeDtypeStruct((t_all, n2), F32)),
        compiler_params=_params())(u, w_b, b_b, up2, bias2)


def _matmul_nt(name, a, w, koff, tm, tk, after=()):
    r, kc = a.shape
    d = w.shape[0]
    nk = kc // tk

    def body(a_ref, w_ref, *rest):
        o_ref = rest[len(after)]
        k = pl.program_id(1)
        p = lax.dot_general(a_ref[...], w_ref[...], (((1,), (1,)), ((), ())), preferred_element_type=F32)
        if nk == 1:
            o_ref[...] = p.astype(o_ref.dtype)
            return
        acc_ref = rest[len(after) + 1]

        @pl.when(k == 0)
        def _():
            acc_ref[...] = p

        @pl.when(k > 0)
        def _():
            acc_ref[...] += p

        @pl.when(k == nk - 1)
        def _():
            o_ref[...] = acc_ref[...].astype(o_ref.dtype)

    return pl.pallas_call(
        body, name=name, grid=(r // tm, nk),
        in_specs=[pl.BlockSpec((tm, tk), lambda i, k: (i, k)),
                  pl.BlockSpec((d, tk), lambda i, k: (0, koff + k))] + [_ANY] * len(after),
        out_specs=pl.BlockSpec((tm, d), lambda i, k: (i, 0)),
        out_shape=jax.ShapeDtypeStruct((r, d), BF16),
        scratch_shapes=[pltpu.VMEM((tm, d), F32)] if nk > 1 else [],
        compiler_params=_params())(a, w, *after)


def _matmul_tn(name, a, b, rows, tk, tn):
    m = a.shape[1]
    n = b.shape[1]
    nk = rows // tk

    def body(a_ref, b_ref, o_ref, s_ref, acc_ref):
        k = pl.program_id(1)
        bv = b_ref[...]
        p = lax.dot_general(bv, a_ref[...], (((0,), (0,)), ((), ())), preferred_element_type=F32)
        cs = jnp.sum(bv.astype(F32), axis=0, keepdims=True)

        @pl.when(k == 0)
        def _():
            acc_ref[...] = p
            s_ref[...] = cs

        @pl.when(k > 0)
        def _():
            acc_ref[...] += p
            s_ref[...] += cs

        @pl.when(k == nk - 1)
        def _():
            o_ref[...] = acc_ref[...].astype(o_ref.dtype)

    return pl.pallas_call(
        body, name=name, grid=(n // tn, nk),
        in_specs=[pl.BlockSpec((tk, m), lambda j, k: (k, 0)),
                  pl.BlockSpec((tk, tn), lambda j, k: (k, j))],
        out_specs=(pl.BlockSpec((tn, m), lambda j, k: (j, 0)), pl.BlockSpec((1, tn), lambda j, k: (0, j))),
        out_shape=(jax.ShapeDtypeStruct((n, m), BF16), jax.ShapeDtypeStruct((1, n), F32)),
        scratch_shapes=[pltpu.VMEM((tn, m), F32)],
        compiler_params=_params())(a, b)


def _matmul_tn_whole(name, a3, b3, rows, tn, transposed):
    nb, _, m = a3.shape
    n = b3.shape[2]

    def body(a_ref, b_ref, o_ref, s_ref):
        p, cs = None, None
        for e in range(nb):
            bv = b_ref[e]
            lhs, rhs = (bv, a_ref[e]) if transposed else (a_ref[e], bv)
            pe = lax.dot_general(lhs, rhs, (((0,), (0,)), ((), ())), preferred_element_type=F32)
            ce = jnp.sum(bv.astype(F32), axis=0, keepdims=True)
            p, cs = (pe, ce) if p is None else (p + pe, cs + ce)
        o_ref[...] = p.astype(o_ref.dtype)
        s_ref[...] = cs

    o_spec, o_shape = ((pl.BlockSpec((tn, m), lambda j: (j, 0)), (n, m)) if transposed
                       else (pl.BlockSpec((m, tn), lambda j: (0, j)), (m, n)))
    return pl.pallas_call(
        body, name=name, grid=(n // tn,),
        in_specs=[pl.BlockSpec((nb, rows, m), lambda j: (0, 0, 0)),
                  pl.BlockSpec((nb, rows, tn), lambda j: (0, 0, j))],
        out_specs=(o_spec, pl.BlockSpec((1, tn), lambda j: (0, j))),
        out_shape=(jax.ShapeDtypeStruct(o_shape, BF16), jax.ShapeDtypeStruct((1, n), F32)),
        compiler_params=_params())(a3, b3)


def _conv_window(pad_ref, r, shift, ktaps, width, horizontal):
    if horizontal:
        return pad_ref[r, pl.ds(16 + shift, width), :]
    return pad_ref[r + ktaps // 2 + shift]


def _conv_row(pad_ref, w, r, ktaps, width, horizontal, flip):
    half = ktaps // 2
    acc = None
    for t in range(ktaps):
        win = _conv_window(pad_ref, r, (half - t) if flip else (t - half), ktaps, width, horizontal)
        term = win * w[t:t + 1, :]
        acc = term if acc is None else acc + term
    return acc


def _fill_padded(ref, val, rows, width, ktaps, horizontal):
    half_k = ktaps // 2
    cb = val.shape[-1]
    if horizontal:
        ref[:, 0:16, :] = jnp.zeros((rows, 16, cb), F32)
        ref[:, 16 + width:32 + width, :] = jnp.zeros((rows, 16, cb), F32)
        ref[:, 16:16 + width, :] = val
    else:
        ref[0:half_k, :, :] = jnp.zeros((half_k, width, cb), F32)
        ref[half_k + rows:2 * half_k + rows, :, :] = jnp.zeros((half_k, width, cb), F32)
        ref[half_k:half_k + rows, :, :] = val


def _conv_fwd(pa, conv_w8, conv_b, nb, s):
    nblk, ktaps, cb = conv_w8.shape
    d = nblk * cb
    rows, width = s // GRID_W, GRID_W
    half_k = ktaps // 2
    nh = nblk // 2

    def body(glu_ref, w_ref, b_ref, o_ref, ph_ref, pv_ref):
        j = pl.program_id(1)
        a0 = (glu_ref[:, 0:cb] * _sigmoid(glu_ref[:, cb:2 * cb])).reshape(rows, width, cb)
        w = w_ref[...]

        bias = b_ref[...]

        def run(pad_ref, horizontal):
            _fill_padded(pad_ref, a0, rows, width, ktaps, horizontal)

            def row(r, carry):
                at = pl.ds(pl.multiple_of(r * width, width), width)
                o_ref[at, :] = _conv_row(pad_ref, w, r, ktaps, width, horizontal, False) + bias
                return carry

            lax.fori_loop(0, rows, row, 0)

        @pl.when(j < nh)
        def _():
            run(ph_ref, True)

        @pl.when(j >= nh)
        def _():
            run(pv_ref, False)

    return pl.pallas_call(
        body, name="conv_fwd", grid=(nb, nblk),
        in_specs=[pl.BlockSpec((s, 2 * cb), lambda b, j: (b, j)),
                  pl.BlockSpec((None, ktaps, cb), lambda b, j: (j, 0, 0)),
                  pl.BlockSpec((1, cb), lambda b, j: (0, j))],
        out_specs=pl.BlockSpec((s, cb), lambda b, j: (b, j)),
        out_shape=jax.ShapeDtypeStruct((nb * s, d), F32),
        scratch_shapes=[pltpu.VMEM((rows, width + 32, cb), F32), pltpu.VMEM((rows + 2 * half_k, width, cb), F32)],
        compiler_params=_params())(pa, conv_w8, conv_b)


def _conv_bwd(pa, da1, conv_w8, nb, s):
    nblk, ktaps, cb = conv_w8.shape
    d = nblk * cb
    rows, width = s // GRID_W, GRID_W
    half_k = ktaps // 2
    nh = nblk // 2

    def body(glu_ref, da_ref, w_ref, dp_ref, dw_ref, db_ref, pha_ref, phd_ref, pva_ref, pvd_ref):
        j = pl.program_id(0)
        b = pl.program_id(1)
        a0 = (glu_ref[:, 0:cb] * _sigmoid(glu_ref[:, cb:2 * cb])).reshape(rows, width, cb)
        da1v = da_ref[...]
        d3 = da1v.reshape(rows, width, cb)
        w = w_ref[...]

        @pl.when(b == 0)
        def _():
            dw_ref[...] = jnp.zeros_like(dw_ref)
            db_ref[...] = jnp.zeros_like(db_ref)

        db_ref[...] += jnp.sum(da1v, axis=0, keepdims=True)

        def run(pa_ref, pd_ref, horizontal):
            _fill_padded(pa_ref, a0, rows, width, ktaps, horizontal)
            _fill_padded(pd_ref, d3, rows, width, ktaps, horizontal)

            def row(r, accs):
                at = pl.ds(pl.multiple_of(r * width, width), width)
                da0 = _conv_row(pd_ref, w, r, ktaps, width, horizontal, True)
                gv = glu_ref[at, 0:cb]
                sg = _sigmoid(glu_ref[at, cb:2 * cb])
                dp_ref[at, 0:cb] = (da0 * sg).astype(BF16)
                dp_ref[at, cb:2 * cb] = (da0 * gv * sg * (1.0 - sg)).astype(BF16)
                d_row = da_ref[at, :]
                out = []
                for t in range(ktaps):
                    prod = _conv_window(pa_ref, r, t - half_k, ktaps, width, horizontal) * d_row
                    out.append(accs[t] + jnp.sum(prod.reshape(width // 8, 8, cb), axis=0))
                return tuple(out)

            accs = lax.fori_loop(0, rows, row, tuple(jnp.zeros((8, cb), F32) for _ in range(ktaps)))
            for t in range(ktaps):
                dw_ref[t:t + 1, :] += jnp.sum(accs[t], axis=0, keepdims=True)

        @pl.when(j < nh)
        def _():
            run(pha_ref, phd_ref, True)

        @pl.when(j >= nh)
        def _():
            run(pva_ref, pvd_ref, False)

    return pl.pallas_call(
        body, name="conv_bwd", grid=(nblk, nb),
        in_specs=[pl.BlockSpec((s, 2 * cb), lambda j, b: (b, j)),
                  pl.BlockSpec((s, cb), lambda j, b: (b, j)),
                  pl.BlockSpec((None, ktaps, cb), lambda j, b: (j, 0, 0))],
        out_specs=(pl.BlockSpec((s, 2 * cb), lambda j, b: (b, j)),
                   pl.BlockSpec((None, ktaps, cb), lambda j, b: (j, 0, 0)),
                   pl.BlockSpec((1, cb), lambda j, b: (0, j))),
        out_shape=(jax.ShapeDtypeStruct((nb * s, 2 * d), BF16),
                   jax.ShapeDtypeStruct((nblk, ktaps, cb), F32), jax.ShapeDtypeStruct((1, d), F32)),
        scratch_shapes=[pltpu.VMEM((rows, width + 32, cb), F32), pltpu.VMEM((rows, width + 32, cb), F32),
                        pltpu.VMEM((rows + 2 * half_k, width, cb), F32),
                        pltpu.VMEM((rows + 2 * half_k, width, cb), F32)],
        compiler_params=_params())(pa, da1, conv_w8)


def _decay_bwd(pb, up2, bias2, grads_f, grads_b, tiles, lr_blk, dk_, dv_):
    t_all = pb.shape[0]
    tm = tiles.tm
    n2 = up2.shape[1]
    nbw = 2 * dk_ + dv_ + LANE

    def body(lr_ref, up_ref, b_ref, dqf, dkf, dvf, dgf, dqb, dkb, dvb, dgb, dp_ref, dup_ref, dbias_ref):
        i = pl.program_id(0)
        pad = tiles.is_pad(i)
        live = lambda v: jnp.where(pad, 0.0, v)

        @pl.when(i == 0)
        def _():
            dup_ref[...] = jnp.zeros_like(dup_ref)
            dbias_ref[...] = jnp.zeros_like(dbias_ref)

        lr = lr_ref[...]
        up = up_ref[...]
        logits = _mm(lr, up) + b_ref[...]
        dg = live(jnp.concatenate([dgf[...], dgb[...]], axis=1))
        dlog = dg * (1.0 / GATE_TAU) * _sigmoid(-logits)
        dup_ref[...] += _mm_tn(lr, dlog)
        dbias_ref[...] += jnp.sum(dlog, axis=0, keepdims=True)
        both = lambda f, b: live(f[...].astype(F32) + b[...].astype(F32)).astype(BF16)
        dp_ref[:, 0:dk_] = both(dqf, dqb)
        dp_ref[:, dk_:2 * dk_] = both(dkf, dkb)
        dp_ref[:, 2 * dk_:2 * dk_ + dv_] = both(dvf, dvb)
        dp_ref[:, 2 * dk_ + dv_:nbw] = _mm_nt(dlog, up).astype(BF16)

    row = lambda w: pl.BlockSpec((tm, w), lambda i: (i, 0))
    return pl.pallas_call(
        body, name="decay_bwd", grid=(t_all // tm,),
        in_specs=[pl.BlockSpec((tm, LANE), lambda i: (i, lr_blk)),
                  pl.BlockSpec(up2.shape, lambda i: (0, 0)),
                  pl.BlockSpec((1, n2), lambda i: (0, 0)),
                  row(dk_), row(dk_), row(dv_), row(dk_), row(dk_), row(dk_), row(dv_), row(dk_)],
        out_specs=(row(nbw), pl.BlockSpec(up2.shape, lambda i: (0, 0)), pl.BlockSpec((1, n2), lambda i: (0, 0))),
        out_shape=(jax.ShapeDtypeStruct((t_all, nbw), BF16), jax.ShapeDtypeStruct(up2.shape, F32),
                   jax.ShapeDtypeStruct((1, n2), F32)),
        compiler_params=_params())(pb, up2, bias2, *grads_f, *grads_b)


def _scan_chunk(s, nl, nc, rev):
    if rev:
        return jnp.where(s < nc, nl + (nc - 1 - s), nl - 1 - (s - nc))
    return jnp.where(s < nc, nl + s, s - nc)


def _scan_lat_chunk(s, nl, nc, rev):
    first = nl - 1 if rev else 0
    return jnp.where(s < nc, first, _scan_chunk(s, nl, nc, rev))


def _tri_mm(m_bf, x):
    hi = x.astype(BF16)
    r1 = x - hi.astype(F32)
    mid = r1.astype(BF16)
    lo = (r1 - mid.astype(F32)).astype(BF16)
    dot = lambda p: jnp.dot(m_bf, p, preferred_element_type=F32)
    return dot(hi) + dot(mid) + dot(lo)


def _chunk_masks(c, rev):
    ii = lax.broadcasted_iota(jnp.int32, (c, c), 0)
    jj = lax.broadcasted_iota(jnp.int32, (c, c), 1)
    return ((ii <= jj), (ii >= jj)) if rev else ((ii >= jj), (ii <= jj))


def _chunk_terms(q, k, b, far, mid):
    bf, bm = b[far:far + 1, :], b[mid:mid + 1, :]
    e = jnp.exp(b)
    em = jnp.exp(b - bm)
    eim = jnp.exp(bm - b)
    ed = jnp.exp(bf - b)
    return dict(e=e, em=em, eim=eim, ed=ed, dec=jnp.exp(bf), qe=q * e, qem=q * em, kim=k * eim, kd=k * ed)


def _gla_fwd(pb3, pv3, g3, nb, s_len, c_len, dk_, dv_):
    c = CHUNK
    nl, nc = s_len // c, c_len // c
    ns = nl + nc
    hk, hv = dk_ // HEADS, dv_ // HEADS
    l_len = pb3.shape[1]
    scale = hk ** -0.5
    mid = c // 2

    def body(*refs):
        ins, outs, z_scr = refs[:8], refs[8:14], refs[14]
        s = pl.program_id(0)

        @pl.when(s == 0)
        def _():
            z_scr[...] = jnp.zeros_like(z_scr)

        qs = jnp.where(s >= nc, scale, 0.0)
        for di, rev in enumerate((False, True)):
            q_ref, k_ref, v_ref, g_ref = ins[4 * di:4 * di + 4]
            o_ref, zs_ref, b_ref = outs[3 * di:3 * di + 3]
            mask, _ = _chunk_masks(c, rev)
            m_bf = mask.astype(BF16)
            far = 0 if rev else c - 1
            for b in range(nb):
                bc = _tri_mm(m_bf, g_ref[b])
                b_ref[b] = bc
                for h in range(HEADS):
                    ks, vs = slice(h * hk, (h + 1) * hk), slice(h * hv, (h + 1) * hv)
                    zi = (di * nb + b) * HEADS + h
                    v = v_ref[b, :, vs]
                    t = _chunk_terms(q_ref[b, :, ks] * qs, k_ref[b, :, ks], bc[:, ks], far, mid)
                    a = jnp.where(mask, _mm_nt(t["qem"], t["kim"]), 0.0)
                    z = z_scr[zi]
                    zs_ref[0, b * HEADS + h] = z
                    o_ref[b, :, vs] = _mm(a, v) + _mm_nt(t["qe"], z)
                    z_scr[zi] = z * t["dec"] + _mm_tn(v, t["kd"])

    in_specs, out_specs, out_shape = [], [], []
    for di, rev in enumerate((False, True)):
        ch = functools.partial(_scan_chunk, nl=nl, nc=nc, rev=rev)
        lch = functools.partial(_scan_lat_chunk, nl=nl, nc=nc, rev=rev)
        in_specs += [pl.BlockSpec((nb, c, dk_), lambda s, ch=ch: (0, ch(s), 0)),
                     pl.BlockSpec((nb, c, dk_), lambda s, ch=ch: (0, ch(s), 1)),
                     pl.BlockSpec((nb, c, dv_), lambda s, ch=ch: (0, ch(s), 0)),
                     pl.BlockSpec((nb, c, dk_), lambda s, ch=ch, di=di: (0, ch(s), di))]
        out_specs += [pl.BlockSpec((nb, c, dv_), lambda s, lch=lch: (0, lch(s), 0)),
                      pl.BlockSpec((1, nb * HEADS, hv, hk), lambda s: (s, 0, 0, 0)),
                      pl.BlockSpec((nb, c, dk_), lambda s, ch=ch: (0, ch(s), 0))]
        out_shape += [jax.ShapeDtypeStruct((nb, s_len, dv_), F32),
                      jax.ShapeDtypeStruct((ns, nb * HEADS, hv, hk), F32),
                      jax.ShapeDtypeStruct((nb, l_len, dk_), F32)]
    return pl.pallas_call(
        body, name="gla_fwd", grid=(ns,), in_specs=in_specs, out_specs=tuple(out_specs), out_shape=tuple(out_shape),
        scratch_shapes=[pltpu.VMEM((2 * nb * HEADS, hv, hk), F32)],
        compiler_params=_params())(pb3, pb3, pv3, g3, pb3, pb3, pv3, g3)


def _gla_bwd(pb3, pv3, do3, fwd_saved, nb, s_len, c_len, dk_, dv_):
    c = CHUNK
    nl, nc = s_len // c, c_len // c
    ns = nl + nc
    hk, hv = dk_ // HEADS, dv_ // HEADS
    l_len = pb3.shape[1]
    scale = hk ** -0.5
    mid = c // 2
    zs_f, b_f, zs_b, b_b = fwd_saved

    def body(*refs):
        ins, outs, dz_scr = refs[:12], refs[12:20], refs[20]
        s = pl.program_id(0)
        step = ns - 1 - s

        @pl.when(s == 0)
        def _():
            dz_scr[...] = jnp.zeros_like(dz_scr)

        lat = step >= nc
        qs = jnp.where(lat, scale, 0.0)
        dmul = jnp.where(lat, 1.0, 0.0)
        for di, rev in enumerate((False, True)):
            q_ref, k_ref, v_ref, b_ref, do_ref, zs_ref = ins[6 * di:6 * di + 6]
            dq_ref, dk_ref, dv_ref, dg_ref = outs[4 * di:4 * di + 4]
            mask, mask_t = _chunk_masks(c, rev)
            mt_bf = mask_t.astype(BF16)
            far = 0 if rev else c - 1
            far_row = lax.broadcasted_iota(jnp.int32, (c, hk), 0) == far
            for b in range(nb):
                db_parts = []
                for h in range(HEADS):
                    ks, vs = slice(h * hk, (h + 1) * hk), slice(h * hv, (h + 1) * hv)
                    zi = (di * nb + b) * HEADS + h
                    v = v_ref[b, :, vs]
                    d_o = do_ref[b, :, vs] * dmul
                    t = _chunk_terms(q_ref[b, :, ks] * qs, k_ref[b, :, ks], b_ref[b, :, ks], far, mid)
                    qem, kim, qe, kd = t["qem"], t["kim"], t["qe"], t["kd"]
                    a_t = jnp.where(mask_t, _mm_nt(kim, qem), 0.0)
                    d_a = jnp.where(mask, _mm_nt(d_o, v), 0.0)
                    d_at = jnp.where(mask_t, _mm_nt(v, d_o), 0.0)
                    z = zs_ref[0, b * HEADS + h]
                    dzn = dz_scr[zi]
                    dv_ref[b, :, vs] = (_mm(a_t, d_o) + _mm_nt(kd, dzn)).astype(dv_ref.dtype)
                    dqem = _mm(d_a, kim)
                    dkim = _mm(d_at, qem)
                    dqe = _mm(d_o, z)
                    dkd = _mm(v, dzn)
                    ddec = jnp.sum(z * dzn, axis=0, keepdims=True)
                    dz_scr[zi] = dzn * t["dec"] + _mm_tn(d_o, qe)
                    dq_ref[b, :, ks] = ((dqem * t["em"] + dqe * t["e"]) * qs).astype(dq_ref.dtype)
                    dk_ref[b, :, ks] = (dkim * t["eim"] + dkd * t["ed"]).astype(dk_ref.dtype)
                    db = dqem * qem - dkim * kim + dqe * qe - dkd * kd
                    extra = jnp.sum(dkd * kd, axis=0, keepdims=True) + ddec * t["dec"]
                    db_parts.append(db + jnp.where(far_row, extra, 0.0))
                dg_ref[b] = _tri_mm(mt_bf, jnp.concatenate(db_parts, axis=1))

    in_specs, out_specs, out_shape, args = [], [], [], []
    for di, rev in enumerate((False, True)):
        ch = lambda s, rev=rev: _scan_chunk(ns - 1 - s, nl, nc, rev)
        lch = lambda s, rev=rev: _scan_lat_chunk(ns - 1 - s, nl, nc, rev)
        in_specs += [pl.BlockSpec((nb, c, dk_), lambda s, ch=ch: (0, ch(s), 0)),
                     pl.BlockSpec((nb, c, dk_), lambda s, ch=ch: (0, ch(s), 1)),
                     pl.BlockSpec((nb, c, dv_), lambda s, ch=ch: (0, ch(s), 0)),
                     pl.BlockSpec((nb, c, dk_), lambda s, ch=ch: (0, ch(s), 0)),
                     pl.BlockSpec((nb, c, dv_), lambda s, lch=lch: (0, lch(s), 0)),
                     pl.BlockSpec((1, nb * HEADS, hv, hk), lambda s: (ns - 1 - s, 0, 0, 0))]
        args += [pb3, pb3, pv3, (b_b if rev else b_f), do3, (zs_b if rev else zs_f)]
        for w, dt in ((dk_, BF16), (dk_, BF16), (dv_, BF16), (dk_, F32)):
            out_specs.append(pl.BlockSpec((nb, c, w), lambda s, ch=ch: (0, ch(s), 0)))
            out_shape.append(jax.ShapeDtypeStruct((nb, l_len, w), dt))
    return pl.pallas_call(
        body, name="gla_bwd", grid=(ns,), in_specs=in_specs, out_specs=tuple(out_specs), out_shape=tuple(out_shape),
        scratch_shapes=[pltpu.VMEM((2 * nb * HEADS, hv, hk), F32)],
        compiler_params=_params())(*args)


def _tail(a1, pa, o_f, o_b, x2, tgt, mod, wc, wg, wo, ln_g, ln_b, gn_t, fg, nb, tm, n_split):
    tl, d = x2.shape
    nt = tl // tm
    per_ex = nt // nb
    hv = d // HEADS
    nrow = mod.shape[0]

    def part(shared, a1_ref, z_ref, r_ref, mc_ref, mg_ref, of_ref, ob_ref, x_ref, t_ref,
             dp_ref, da1_ref, do_ref, gx_ref, mrg_ref, dmo_ref, yci_ref, dyc_ref, ogi_ref, dyg_ref, sm_ref):
        bidx, gate, lng, lnb, fgv, gn, wc_, wg_, wo_ = shared

        a1v = a1_ref[...]
        mu = jnp.mean(a1v, axis=-1, keepdims=True)
        xc = a1v - mu
        rs = lax.rsqrt(jnp.mean(xc * xc, axis=-1, keepdims=True) + EPS)
        xh = xc * rs
        a2 = xh * lng + lnb
        s2 = _sigmoid(a2)
        a3 = a2 * s2
        zv = z_ref[...]
        sz = _sigmoid(zv)
        siluz = zv * sz
        ycin = a3 * siluz
        yconv = _mm(ycin, wc_)

        o = of_ref[...] + ob_ref[...]
        ohat_parts, rn_parts = [], []
        for h in range(HEADS):
            oh = o[:, h * hv:(h + 1) * hv]
            rn = lax.rsqrt(jnp.mean(oh * oh, axis=-1, keepdims=True) + EPS)
            ohat_parts.append(oh * rn)
            rn_parts.append(rn)
        ohat = jnp.concatenate(ohat_parts, axis=1)
        on = ohat * gn
        rv = r_ref[...]
        sr = _sigmoid(rv)
        silur = rv * sr
        ogin = on * silur
        ygla = _mm(ogin, wg_)

        sc = _sigmoid(mc_ref[...])
        sg = _sigmoid(mg_ref[...])
        merged = sc * yconv + sg * ygla
        mo = _mm(merged, wo_)
        hn = x_ref[...] + gate * mo
        rf = lax.rsqrt(jnp.mean(hn * hn, axis=-1, keepdims=True) + EPS)
        yh = hn * rf
        err = yh * fgv - t_ref[...]
        loss_part = 0.5 * jnp.sum(err * err) * (1.0 / d)

        dy = err * (1.0 / d)
        dfg = jnp.sum(dy * yh, axis=0, keepdims=True)
        dyh = dy * fgv
        dhn = rf * (dyh - yh * jnp.mean(dyh * yh, axis=-1, keepdims=True))
        gx_ref[...] = dhn
        dgate = jnp.sum(dhn * mo, axis=0, keepdims=True)
        dmo = gate * dhn
        dmerged = _mm_nt(dmo, wo_)
        dyconv = dmerged * sc
        dygla = dmerged * sg
        dp_ref[:, 2 * d:3 * d] = (dmerged * yconv * sc * (1.0 - sc)).astype(BF16)
        dp_ref[:, 3 * d:4 * d] = (dmerged * ygla * sg * (1.0 - sg)).astype(BF16)
        dycin = _mm_nt(dyconv, wc_)
        dogin = _mm_nt(dygla, wg_)
        mrg_ref[...] = merged.astype(BF16)
        dmo_ref[...] = dmo.astype(BF16)
        yci_ref[...] = ycin.astype(BF16)
        dyc_ref[...] = dyconv.astype(BF16)
        ogi_ref[...] = ogin.astype(BF16)
        dyg_ref[...] = dygla.astype(BF16)

        da3 = dycin * siluz
        dp_ref[:, 0:d] = (dycin * a3 * _dsilu(zv, sz)).astype(BF16)
        da2 = da3 * _dsilu(a2, s2)
        dlng = jnp.sum(da2 * xh, axis=0, keepdims=True)
        dlnb = jnp.sum(da2, axis=0, keepdims=True)
        dxh = da2 * lng
        da1_ref[...] = rs * (dxh - jnp.mean(dxh, axis=-1, keepdims=True)
                             - xh * jnp.mean(dxh * xh, axis=-1, keepdims=True))

        don = dogin * silur
        dp_ref[:, d:2 * d] = (dogin * on * _dsilu(rv, sr)).astype(BF16)
        dgn = jnp.sum(don * ohat, axis=0, keepdims=True)
        dyn = don * gn
        for h in range(HEADS):
            vs = slice(h * hv, (h + 1) * hv)
            oh_hat = ohat_parts[h]
            dh = dyn[:, vs]
            do_ref[:, vs] = (rn_parts[h] * (dh - oh_hat * jnp.mean(dh * oh_hat, axis=-1, keepdims=True))
                             ).astype(BF16)

        sm_ref[0:1, :] += dfg
        sm_ref[1:2, :] += dlng
        sm_ref[2:3, :] += dlnb
        sm_ref[3:4, :] += dgn
        sm_ref[4:5, :] += jnp.zeros((1, d), F32) + loss_part
        for b in range(nb):
            sm_ref[8 + b:9 + b, :] += jnp.where(bidx == b, dgate, 0.0)

    def body(*refs):
        mod_ref, wc_ref, wg_ref, wo_ref, lng_ref, lnb_ref, gn_ref, fg_ref = refs[9:17]
        sm_ref = refs[27]
        i = pl.program_id(0)

        @pl.when(i == 0)
        def _():
            sm_ref[...] = jnp.zeros_like(sm_ref)

        bidx = i // per_ex
        shared = (bidx, _rowsel(mod_ref[...], bidx, nb)[:, 2 * d:3 * d], lng_ref[...], lnb_ref[...], fg_ref[...],
                  jnp.concatenate([gn_ref[...]] * HEADS, axis=1), wc_ref[...], wg_ref[...], wo_ref[...])
        rows_per = tm // n_split
        for p in range(n_split):
            rows = pl.ds(p * rows_per, rows_per)
            part(shared, *[r.at[rows] for r in refs[0:9]], *[r.at[rows] for r in refs[17:27]], sm_ref)

    row = pl.BlockSpec((tm, d), lambda i: (i, 0))
    pcol = lambda blk: pl.BlockSpec((tm, d), lambda i: (i, blk))
    full = lambda arr: pl.BlockSpec(arr.shape, lambda i: (0,) * arr.ndim)
    bfo = jax.ShapeDtypeStruct((tl, d), BF16)
    f32o = jax.ShapeDtypeStruct((tl, d), F32)
    return pl.pallas_call(
        body, name="tail", grid=(nt,),
        in_specs=[row, pcol(2), pcol(3), pcol(4), pcol(5), row, row, row, row, full(mod), full(wc), full(wg),
                  full(wo), full(ln_g), full(ln_b), full(gn_t), full(fg)],
        out_specs=(pl.BlockSpec((tm, 4 * d), lambda i: (i, 0)), row, row, row, row, row, row, row, row, row,
                   pl.BlockSpec((16, d), lambda i: (0, 0))),
        out_shape=(jax.ShapeDtypeStruct((tl, 4 * d), BF16), f32o, bfo, f32o, bfo, bfo, bfo, bfo, bfo, bfo,
                   jax.ShapeDtypeStruct((16, d), F32)),
        compiler_params=_params())(a1, pa, pa, pa, pa, o_f, o_b, x2, tgt, mod, wc, wg, wo, ln_g, ln_b, gn_t, fg)


def _local_step(x, c, ctx, tgt, c_ctx, ada_w8, ada_b, norm_g, w_a, b_a, w_b, b_b, conv_w8, conv_b, ln_g, ln_b,
                up2, bias2, gla_norm_g, final_norm_g, proj, on_grads=None, on_du_a1=None):
    nb, s_len, d = x.shape
    c_len = ctx.shape[1]
    dk_, dv_ = d // 2, d
    tl, tc = nb * s_len, nb * c_len
    nbw = 2 * dk_ + dv_ + LANE
    tm = math.gcd(256, c_len)
    tiles = _Tiles(nb, s_len, c_len, tm, 2)
    l_len = tiles.rows_per_ex
    t_all = nb * l_len
    x2, ctx2, tgt2 = x.reshape(tl, d), ctx.reshape(tc, d), tgt.reshape(tl, d)

    cv = jnp.zeros((8, d), F32).at[0:nb].set(c).at[nb].set(c_ctx.reshape(d))
    mod = _ada_fwd(cv, ada_w8, ada_b)
    u = _norm_fwd(x2, ctx2, mod, norm_g, tiles)
    u3 = u.reshape(nb, l_len, d)
    tma = math.gcd(1024, s_len)
    pa = _matmul_bias("inproj_a", u3, w_a, b_a, s_len, tma, _tile(6 * d, 2048))
    tmb = math.gcd(1024, t_all)
    pb, pv, g_all = _inproj_b(u, w_b, b_b, up2, bias2, tmb, dk_, dv_)

    a1 = _conv_fwd(pa, conv_w8, conv_b, nb, s_len)
    lr_blk = (2 * dk_) // LANE
    pb3, pv3 = pb.reshape(nb, l_len, 2 * dk_ + LANE), pv.reshape(nb, l_len, dv_)
    o_f, zs_f, b_f, o_b, zs_b, b_b2 = _gla_fwd(pb3, pv3, g_all.reshape(nb, l_len, 2 * dk_), nb, s_len, c_len,
                                               dk_, dv_)

    conv_proj, gla_proj, w_out = proj(a1) if callable(proj) else proj
    tt = math.gcd(256, s_len)
    (dp_a2, da1, d_o, gx1, merged, dmo, ycin, dyconv, ogin, dygla, small) = _tail(
        a1, pa, o_f.reshape(tl, dv_), o_b.reshape(tl, dv_), x2, tgt2, mod, conv_proj, gla_proj, w_out, ln_g, ln_b,
        gla_norm_g, final_norm_g, nb, tt, 2)

    lat3 = lambda a: a.reshape(nb, s_len, a.shape[-1])
    tnw = _tile(d, 512)
    d_w_out, _ = _matmul_tn_whole("dw_out", lat3(merged), lat3(dmo), s_len, tnw, False)
    d_conv_proj, _ = _matmul_tn_whole("dw_conv_proj", lat3(ycin), lat3(dyconv), s_len, tnw, False)
    d_gla_proj, _ = _matmul_tn_whole("dw_gla_proj", lat3(ogin), lat3(dygla), s_len, tnw, False)

    dp_a1, d_conv_w8, d_conv_b = _conv_bwd(pa, da1, conv_w8, nb, s_len)
    gl = _gla_bwd(pb3, pv3, d_o.reshape(nb, s_len, dv_), (zs_f, b_f, zs_b, b_b2), nb, s_len, c_len, dk_, dv_)
    gl = [g_.reshape(t_all, g_.shape[-1]) for g_ in gl]
    dp_b, d_up2, d_bias2 = _decay_bwd(pb, up2, bias2, gl[0:4], gl[4:8], tiles, lr_blk, dk_, dv_)

    dw_a1, db_a1 = _matmul_tn_whole("dw_a1", u3, lat3(dp_a1), s_len, tnw, True)
    dw_a2, db_a2 = _matmul_tn_whole("dw_a2", u3, lat3(dp_a2), s_len, tnw, True)
    dw_b, db_b = _matmul_tn("dw_b", u, dp_b, t_all, tmb, nbw)
    grads = dict(w_a1=dw_a1, w_a2=dw_a2, w_b=dw_b, conv_w8=d_conv_w8, conv_proj=d_conv_proj, up2=d_up2,
                 gla_proj=d_gla_proj, w_out=d_w_out)

    tka = _tile(2 * d, 2048)
    du_a1 = _matmul_nt("du_a1", dp_a1, w_a, 0, tma, tka, after=on_grads(grads) if on_grads else ())
    du_a2 = _matmul_nt("du_a2", dp_a2, w_a, (2 * d) // tka, tma, tka, after=on_du_a1(du_a1) if on_du_a1 else ())
    du_b = _matmul_nt("du_b", dp_b, w_b, 0, tmb, nbw)
    grad_x2, dmod_ss, d_norm_g = _norm_bwd(x2, ctx2, mod, norm_g, [du_a1, du_a2], du_b, gx1, tiles)
    d_ada_w8, d_ada_b, d_cv = _ada_bwd(cv, ada_w8, dmod_ss, small, nb)

    return dict(
        grads, grad_x=grad_x2.reshape(nb, s_len, d), small=small, cv=d_cv, ada_w8=d_ada_w8, ada_b=d_ada_b,
        norm_g=d_norm_g, b_a1=db_a1, b_a2=db_a2, b_b=db_b, conv_b=d_conv_b, bias2=d_bias2)


def _regroup_pieces(d, r, wshard):
    cb = d // N_DEV
    segs = []
    for j in range(N_DEV):
        segs.append((j * cb, cb, 0, 2 * j * cb))
    for j in range(N_DEV):
        segs.append((d + j * cb, cb, 0, (2 * j + 1) * cb))
    segs += [(2 * d, d, 0, 2 * d), (3 * d, 2 * d + 2 * r, 1, 0), (5 * d + 2 * r, 3 * d, 0, 3 * d)]
    pieces = []
    for o0, w, dst, d0 in segs:
        lo = o0
        while lo < o0 + w:
            j = lo // wshard
            hi = min(o0 + w, (j + 1) * wshard)
            pieces.append((j, lo - j * wshard, hi - lo, dst, d0 + lo - o0))
            lo = hi
    return pieces


def _regroup(o, d, r):
    n_in = 8 * d + 2 * r
    parts = ([], [])
    for _, s0, n, dst, _ in sorted(_regroup_pieces(d, r, n_in), key=lambda p: (p[3], p[4])):
        parts[dst].append(o[..., s0:s0 + n])
    pad = jnp.zeros(o.shape[:-1] + (LANE - 2 * r,), o.dtype)
    return jnp.concatenate(parts[0], axis=-1), jnp.concatenate(parts[1] + [pad], axis=-1)


def _unshard_w_in(g_win, d, r, after=()):
    n_sh, _, ws = g_win.shape
    nbw = 2 * d + LANE
    pieces = _regroup_pieces(d, r, ws)
    tr = math.gcd(d, 256)

    def body(g_ref, *rest):
        a_ref, b_ref = rest[len(after):]
        dsts = (a_ref, b_ref)
        for j, s0, n, dst, d0 in pieces:
            dsts[dst][:, pl.ds(d0, n)] = g_ref[j, :, pl.ds(s0, n)]
        b_ref[:, pl.ds(2 * d + 2 * r, LANE - 2 * r)] = jnp.zeros((tr, LANE - 2 * r), b_ref.dtype)

    return pl.pallas_call(
        body, name="unshard_w_in", grid=(d // tr,),
        in_specs=[pl.BlockSpec((n_sh, tr, ws), lambda i: (0, i, 0))] + [_ANY] * len(after),
        out_specs=(pl.BlockSpec((tr, 6 * d), lambda i: (i, 0)), pl.BlockSpec((tr, nbw), lambda i: (i, 0))),
        out_shape=(jax.ShapeDtypeStruct((d, 6 * d), g_win.dtype), jax.ShapeDtypeStruct((d, nbw), g_win.dtype)),
        compiler_params=_params())(g_win, *after)


def _reshard_w_in(dwt_a1, dwt_a2, dwt_b, d, r):
    ws = (8 * d + 2 * r) // N_DEV
    pieces = _regroup_pieces(d, r, ws)
    tc = math.gcd(d, 256)

    def body(a1_ref, a2_ref, b_ref, o_ref):
        for j, s0, n, dst, d0 in pieces:
            if dst == 1:
                src = b_ref[pl.ds(d0, n), :]
            elif d0 < 2 * d:
                src = a1_ref[pl.ds(d0, n), :]
            else:
                src = a2_ref[pl.ds(d0 - 2 * d, n), :]
            o_ref[j, pl.ds(s0, n), :] = src

    col = lambda h: pl.BlockSpec((h, tc), lambda i: (0, i))
    return pl.pallas_call(
        body, name="reshard_w_in", grid=(d // tc,),
        in_specs=[col(2 * d), col(4 * d), col(2 * d + LANE)],
        out_specs=pl.BlockSpec((N_DEV, ws, tc), lambda i: (0, 0, i)),
        out_shape=jax.ShapeDtypeStruct((N_DEV, ws, d), dwt_b.dtype),
        compiler_params=_params())(dwt_a1, dwt_a2, dwt_b)


_SMALL = ("c_ctx", "ada_b", "norm_g", "b_in", "conv_b", "conv_ln_g", "conv_ln_b", "decay_bias_fwd",
          "decay_bias_bwd", "gla_norm_g", "final_norm_g")


def _small_layout(d, r):
    sizes = dict(c_ctx=d, ada_b=3 * d, norm_g=d, b_in=8 * d + 2 * r, conv_b=d, conv_ln_g=d, conv_ln_b=d,
                 decay_bias_fwd=d // 2, decay_bias_bwd=d // 2, gla_norm_g=d // HEADS, final_norm_g=d, loss=1)
    table, off = {}, 0
    for name in _SMALL + ("loss",):
        table[name] = (off, sizes[name])
        off += -(-sizes[name] // LANE) * LANE
    return table, off


def _pack_small(g, nb, d, r, after):
    table, width = _small_layout(d, r)
    hv = d // HEADS
    pieces = _regroup_pieces(d, r, 8 * d + 2 * r)
    names = ("small", "cv", "ada_b", "norm_g", "b_a1", "b_a2", "b_b", "conv_b", "bias2")

    def body(sm, cv, ab, ng, ba1, ba2, bb, cvb, b2, _, o_ref):
        o_ref[...] = jnp.zeros_like(o_ref)

        def put(name, val):
            off, n = table[name]
            o_ref[:, pl.ds(off, n)] = val

        put("c_ctx", cv[nb:nb + 1, :])
        put("ada_b", ab[...])
        put("norm_g", ng[...])
        off_b = table["b_in"][0]
        for _, s0, n, dst, d0 in pieces:
            if dst == 1:
                src = bb[:, pl.ds(d0, n)]
            elif d0 < 2 * d:
                src = ba1[:, pl.ds(d0, n)]
            else:
                src = ba2[:, pl.ds(d0 - 2 * d, n)]
            o_ref[:, pl.ds(off_b + s0, n)] = src
        put("conv_b", cvb[...])
        put("conv_ln_g", sm[1:2, :])
        put("conv_ln_b", sm[2:3, :])
        put("decay_bias_fwd", b2[:, 0:d // 2])
        put("decay_bias_bwd", b2[:, d // 2:d])
        gn = sm[3:4, 0:hv]
        for h in range(1, HEADS):
            gn = gn + sm[3:4, h * hv:(h + 1) * hv]
        put("gla_norm_g", gn)
        put("final_norm_g", sm[0:1, :])
        put("loss", sm[4:5, 0:1])

    return pl.pallas_call(body, name="pack_small", out_shape=jax.ShapeDtypeStruct((1, width), F32),
                          compiler_params=_params())(*[g[k] for k in names], after)


def _small_adam(parts, ws, ms, vs, d, r):
    table, width = _small_layout(d, r)
    n_parts = parts.shape[0]
    k = len(_SMALL)

    def body(p_ref, *refs):
        w_refs, m_refs, v_refs = refs[0:k], refs[k:2 * k], refs[2 * k:3 * k]
        outs = refs[3 * k:]
        tot = p_ref[0]
        for i in range(1, n_parts):
            tot = tot + p_ref[i]
        for i, name in enumerate(_SMALL):
            off, n = table[name]
            g = tot[:, off:off + n]
            outs[i][...] = g
            outs[k + i][...], outs[2 * k + i][...], outs[3 * k + i][...] = _adamw(
                g, w_refs[i][...], m_refs[i][...], v_refs[i][...])
        off, _ = table["loss"]
        outs[4 * k][...] = tot[:, off:off + 1]

    shapes = [jax.ShapeDtypeStruct(w.shape, F32) for w in ws]
    res = pl.pallas_call(body, name="small_adam", out_shape=tuple(shapes * 4 + [jax.ShapeDtypeStruct((1, 1), F32)]),
                         compiler_params=_params())(parts, *ws, *ms, *vs)
    return res[0:k], res[k:2 * k], res[2 * k:3 * k], res[3 * k:4 * k], res[4 * k]


def _mesh_pos():
    return lax.axis_index("x"), lax.axis_index("y"), lax.axis_index("c")


def _all_gather(arrs):
    n = len(arrs)
    ns = 9
    split = [a.ndim == 2 and a.shape[0] % 32 == 0 for a in arrs]

    def body(*refs):
        ins, outs = refs[:n], refs[n:2 * n]
        send_sems, recv_sems, local_sems = refs[2 * n:]
        x, y, c = _mesh_pos()
        me, sibling = (x, y, c), (x, y, 1 - c)
        xn, yn, dg = (1 - x, y, c), (x, 1 - y, c), (1 - x, 1 - y, c)
        other = lambda pos: (pos[0], pos[1], 1 - c)

        def slot(a, pos, half):
            ref = outs[a].at[4 * pos[0] + 2 * pos[1] + pos[2]]
            if half is None:
                return ref
            rows = arrs[a].shape[0] // 2
            return ref.at[pl.ds(half * rows, rows)]

        def copy(a, k, block, to, src=None, half=None):
            dst = slot(a, block, half)
            return pltpu.make_async_remote_copy(
                src_ref=dst if src is None else src, dst_ref=dst,
                send_sem=send_sems.at[ns * a + k], recv_sem=recv_sems.at[ns * a + k],
                device_id=to, device_id_type=MESH)

        h0 = lambda a: 0 if split[a] else None
        mine = [pltpu.make_async_copy(ins[a], slot(a, me, None), local_sems.at[a]) for a in range(n)]
        for cp in mine:
            cp.start()
        sent = []
        for a in range(n):
            sent += [copy(a, 0, me, sibling, src=ins[a]), copy(a, 1, me, xn, src=ins[a]),
                     copy(a, 2, me, yn, src=ins[a])]
        for cp in sent:
            cp.start()

        def pass_on(cp):
            cp.start()
            sent.append(cp)

        for a in range(n):
            copy(a, 1, xn, me).wait_recv()
            pass_on(copy(a, 3, xn, sibling))
            pass_on(copy(a, 4, xn, yn, half=h0(a)))
        for a in range(n):
            copy(a, 2, yn, me).wait_recv()
            pass_on(copy(a, 5, yn, sibling))
            if split[a]:
                pass_on(copy(a, 6, yn, xn, half=1))
        for a in range(n):
            copy(a, 4, dg, me, half=h0(a)).wait_recv()
            pass_on(copy(a, 7, dg, sibling, half=h0(a)))
            if split[a]:
                copy(a, 6, dg, me, half=1).wait_recv()
                pass_on(copy(a, 8, dg, sibling, half=1))
        for a in range(n):
            copy(a, 0, sibling, me).wait_recv()
            copy(a, 3, other(xn), me).wait_recv()
            copy(a, 5, other(yn), me).wait_recv()
            copy(a, 7, other(dg), me, half=h0(a)).wait_recv()
            if split[a]:
                copy(a, 8, other(dg), me, half=1).wait_recv()
        for cp in sent:
            cp.wait_send()
        for cp in mine:
            cp.wait()

    return pl.pallas_call(
        body, name="all_gather",
        out_shape=tuple(jax.ShapeDtypeStruct((N_DEV,) + a.shape, a.dtype) for a in arrs),
        in_specs=[_ANY] * n, out_specs=tuple([_ANY] * n),
        scratch_shapes=[pltpu.SemaphoreType.DMA((ns * n,)), pltpu.SemaphoreType.DMA((ns * n,)),
                        pltpu.SemaphoreType.DMA((n,))],
    )(*arrs)


def _elementwise_tile(r, cdim, cols=2 * LANE):
    if r % 8 == 0 and r > 256:
        return math.gcd(r, 256), cdim
    if r > 256 and cdim % cols == 0:
        return r, cols
    return r, cdim


def _pair_sum(name, mine, theirs):
    _, r, cdim = mine.shape
    tr, tc = _elementwise_tile(r, cdim)

    def body(c_ref, m_ref, t_ref, o_ref):
        o_ref[...] = (m_ref[...].astype(F32) + t_ref[...].astype(F32)).astype(o_ref.dtype)

    return pl.pallas_call(
        body, name=name,
        grid_spec=pltpu.PrefetchScalarGridSpec(
            num_scalar_prefetch=1, grid=(r // tr, cdim // tc),
            in_specs=[pl.BlockSpec((4, None, tr, tc), lambda i, j, c_ref: (0, c_ref[0], i, j)),
                      pl.BlockSpec((4, tr, tc), lambda i, j, c_ref: (0, i, j))],
            out_specs=pl.BlockSpec((4, tr, tc), lambda i, j, c_ref: (0, i, j))),
        out_shape=jax.ShapeDtypeStruct((4, r, cdim), mine.dtype),
        compiler_params=_params())(lax.axis_index("c").reshape(1), mine.reshape(4, 2, r, cdim), theirs)


def _pair_sum_small(mines, theirs):
    n = len(mines)

    def body(*refs):
        c = lax.axis_index("c")
        for i in range(n):
            m_ref, t_ref, o_ref = refs[i], refs[n + i], refs[2 * n + i]
            own = jnp.where(c == 0, m_ref[:, 0].astype(F32), m_ref[:, 1].astype(F32))
            o_ref[...] = (own + t_ref[...].astype(F32)).astype(o_ref.dtype)

    return pl.pallas_call(
        body, name="pair_sum_small_weights",
        out_shape=tuple(jax.ShapeDtypeStruct(t.shape, m.dtype) for m, t in zip(mines, theirs)),
        compiler_params=_params())(*[m.reshape((4, 2) + m.shape[1:]) for m in mines], *theirs)


_HBM = pl.BlockSpec(memory_space=pltpu.HBM)
_SEM = pl.BlockSpec(memory_space=pltpu.SEMAPHORE)


def _copies_start(name, srcs, lands, make_copies, n_sems):
    n, m = len(srcs), len(lands)

    def body(*refs):
        ins = refs[:n + m]
        send_sems, recv_sems = refs[n + m], refs[n + m + 1]
        for cp in make_copies(ins[:n], ins[n:], send_sems, recv_sems):
            cp.start()
        refs[-1][...] = jnp.zeros_like(refs[-1])

    res = pl.pallas_call(
        body, name=name,
        out_shape=(pltpu.SemaphoreType.DMA((n_sems,)), pltpu.SemaphoreType.DMA((n_sems,)),
                   *[pltpu.HBM(a.shape, a.dtype) for a in (*srcs, *lands)], jax.ShapeDtypeStruct((8, LANE), F32)),
        in_specs=[_HBM] * (n + m),
        out_specs=(_SEM, _SEM, *[_HBM] * (n + m), pl.BlockSpec(memory_space=pltpu.VMEM)),
        input_output_aliases={i: 2 + i for i in range(n + m)},
        compiler_params=pltpu.CompilerParams(has_side_effects=pltpu.SideEffectType.DATAFLOW_SIDE_EFFECTING),
    )(*[pltpu.with_memory_space_constraint(a, pltpu.HBM) for a in (*srcs, *lands)])
    return res[0], res[1], res[2:2 + n], res[2 + n:2 + n + m], res[-1]


def _copies_wait(name, started, after, make_copies):
    send_sems, recv_sems, srcs, lands, _ = started
    n, m = len(srcs), len(lands)

    def body(*refs):
        ins = refs[:n + m]
        for cp in make_copies(ins[:n], ins[n:], refs[n + m], refs[n + m + 1]):
            cp.wait_send()
            cp.wait_recv()

    res = pl.pallas_call(
        body, name=name,
        out_shape=tuple(pltpu.HBM(a.shape, a.dtype) for a in (*srcs, *lands)),
        in_specs=[_HBM] * (n + m) + [_SEM, _SEM] + [_ANY] * len(after),
        out_specs=tuple([_HBM] * (n + m)),
        input_output_aliases={i: i for i in range(n + m)},
        compiler_params=pltpu.CompilerParams(has_side_effects=pltpu.SideEffectType.DATAFLOW_SIDE_EFFECTING),
    )(*srcs, *lands, send_sems, recv_sems, *after)
    return res[:n], res[n:]


def _gather_copies(srcs, lands, send_sems, recv_sems):
    x, y, c = _mesh_pos()
    me_i = 4 * x + 2 * y + c
    copies = []
    for rel in range(1, N_DEV):
        peer = (1 - x if rel & 4 else x, 1 - y if rel & 2 else y, 1 - c if rel & 1 else c)
        for a in range(len(srcs)):
            copies.append(pltpu.make_async_remote_copy(
                src_ref=srcs[a], dst_ref=lands[a].at[me_i], send_sem=send_sems.at[7 * a + rel - 1],
                recv_sem=recv_sems.at[7 * a + rel - 1], device_id=peer, device_id_type=MESH))
    return copies


def _sibling_copies(srcs, lands, send_sems, recv_sems):
    x, y, c = _mesh_pos()
    return [pltpu.make_async_remote_copy(
        src_ref=srcs[a].at[2 * k + (1 - c)], dst_ref=lands[a].at[k], send_sem=send_sems.at[4 * a + k],
        recv_sem=recv_sems.at[4 * a + k], device_id=(x, y, 1 - c), device_id_type=MESH)
        for a in range(len(srcs)) for k in range(4)]


def _chip_copies(srcs, lands, send_sems, recv_sems):
    x, y, c = _mesh_pos()
    my_chip = 2 * x + y
    copies = []
    for rel in range(1, 4):
        px = 1 - x if rel & 2 else x
        py = 1 - y if rel & 1 else y
        for a in range(len(srcs)):
            copies.append(pltpu.make_async_remote_copy(
                src_ref=srcs[a].at[2 * px + py], dst_ref=lands[a].at[my_chip], send_sem=send_sems.at[3 * a + rel - 1],
                recv_sem=recv_sems.at[3 * a + rel - 1], device_id=(px, py, c), device_id_type=MESH))
    return copies


def _sum_adam(name, parts, w, m, v, own=None):
    unit_mid = w.ndim == 3
    _, r, cdim = parts.shape
    n_parts = parts.shape[0]
    tr, tc = _elementwise_tile(r, cdim, (4 if unit_mid else 2) * LANE)
    extra = [] if own is None else [own]

    def body(p_ref, *refs):
        w_ref, m_ref, v_ref, g_ref, d_ref, nm_ref, nv_ref = refs[len(extra):]
        if own is None:
            part = lambda k: p_ref[k].astype(F32)
        else:
            my_chip = 2 * lax.axis_index("x") + lax.axis_index("y")
            part = lambda k: jnp.where(my_chip == k, refs[0][...], p_ref[k]).astype(F32)
        g = part(0)
        for k in range(1, n_parts):
            g = g + part(k)
        if unit_mid:
            g = g.reshape(tr, 1, tc)
        g_ref[...] = g
        d_ref[...], nm_ref[...], nv_ref[...] = _adamw(g, w_ref[...], m_ref[...], v_ref[...])

    blk = (pl.BlockSpec((tr, 1, tc), lambda i, j: (i, 0, j)) if unit_mid
           else pl.BlockSpec((tr, tc), lambda i, j: (i, j)))
    o = jax.ShapeDtypeStruct(w.shape, F32)
    return pl.pallas_call(
        body, name=name, grid=(r // tr, cdim // tc),
        in_specs=[pl.BlockSpec((n_parts, tr, tc), lambda i, j: (0, i, j))]
        + [pl.BlockSpec((None, tr, tc), lambda i, j: (2 * lax.axis_index("x") + lax.axis_index("y"), i, j))] * len(extra)
        + [blk, blk, blk],
        out_specs=(blk, blk, blk, blk), out_shape=(o, o, o, o),
        compiler_params=_params())(parts, *extra, w, m, v)


def _sum_adam_small(items):
    n = len(items)

    def body(*refs):
        my_chip = 2 * lax.axis_index("x") + lax.axis_index("y")
        for i in range(n):
            p_ref, own_ref, w_ref, m_ref, v_ref = refs[5 * i:5 * i + 5]
            g_ref, d_ref, nm_ref, nv_ref = refs[5 * n + 4 * i:5 * n + 4 * i + 4]
            g = None
            for k in range(p_ref.shape[0]):
                part = jnp.where(my_chip == k, own_ref[k], p_ref[k]).astype(F32)
                g = part if g is None else g + part
            g_ref[...] = g
            d_ref[...], nm_ref[...], nv_ref[...] = _adamw(g, w_ref[...], m_ref[...], v_ref[...])

    out_shape = tuple(jax.ShapeDtypeStruct(it[2].shape, F32) for it in items for _ in range(4))
    res = pl.pallas_call(body, name="adam_small_weights", out_shape=out_shape,
                         compiler_params=_params())(*[a for it in items for a in it])
    return [res[4 * i:4 * i + 4] for i in range(n)]


_WEIGHTS = ("c_ctx", "ada_w", "ada_b", "norm_g", "w_in", "b_in", "conv_w", "conv_b", "conv_ln_g", "conv_ln_b",
            "conv_proj", "decay_up_fwd", "decay_bias_fwd", "decay_up_bwd", "decay_bias_bwd", "gla_norm_g",
            "gla_proj", "w_out", "final_norm_g")


def _as2d(a):
    if a.ndim == 1:
        return a.reshape(1, -1)
    return a.reshape(-1, a.shape[-1])


def kernel(x, c, ctx, c_ctx, ada_w, ada_b, norm_g, w_in, b_in, conv_w, conv_b, conv_ln_g, conv_ln_b, conv_proj, decay_up_fwd, decay_bias_fwd, decay_up_bwd, decay_bias_bwd, gla_norm_g, gla_proj, w_out, final_norm_g, loss_target, m_c_ctx, m_ada_w, m_ada_b, m_norm_g, m_w_in, m_b_in, m_conv_w, m_conv_b, m_conv_ln_g, m_conv_ln_b, m_conv_proj, m_decay_up_fwd, m_decay_bias_fwd, m_decay_up_bwd, m_decay_bias_bwd, m_gla_norm_g, m_gla_proj, m_w_out, m_final_norm_g, v_c_ctx, v_ada_w, v_ada_b, v_norm_g, v_w_in, v_b_in, v_conv_w, v_conv_b, v_conv_ln_g, v_conv_ln_b, v_conv_proj, v_decay_up_fwd, v_decay_bias_fwd, v_decay_up_bwd, v_decay_bias_bwd, v_gla_norm_g, v_gla_proj, v_w_out, v_final_norm_g):
    env = dict(locals())
    wts = {k: env[k] for k in _WEIGHTS}
    d = x.shape[-1]
    r = decay_up_fwd.shape[1]
    dk_ = d // 2

    ds, dks = d // N_DEV, dk_ // N_DEV
    g_win, g_ada, conv_w8, g_up = _all_gather(
        [w_in[0].astype(BF16), ada_w[0].astype(BF16), conv_w[0],
         jnp.concatenate([decay_up_fwd[0], decay_up_bwd[0]], axis=1)])
    proj_own = [conv_proj[0].astype(BF16), gla_proj[0].astype(BF16), w_out[0].astype(BF16)]
    me_i = 4 * lax.axis_index("x") + 2 * lax.axis_index("y") + lax.axis_index("c")
    proj_lands = [lax.dynamic_update_slice(lax.empty((N_DEV,) + a.shape, a.dtype), a[None], (me_i, 0, 0))
                  for a in proj_own]
    proj_start = _copies_start("proj_gather_start", proj_own, proj_lands, _gather_copies, 7 * 3)

    def proj(after):
        _, lands = _copies_wait("proj_gather_wait", proj_start, (after,), _gather_copies)
        return [w.reshape(d, d) for w in lands]

    w_a, w_b = _unshard_w_in(g_win, d, r, after=(proj_start[4],))
    up_f = g_up[:, :, 0:dks].transpose(1, 0, 2).reshape(r, dk_)
    up_b = g_up[:, :, dks:].transpose(1, 0, 2).reshape(r, dk_)
    up2 = jnp.zeros((LANE, 2 * dk_), F32).at[0:r, 0:dk_].set(up_f).at[r:2 * r, dk_:].set(up_b)
    bias2 = jnp.concatenate([decay_bias_fwd, decay_bias_bwd], axis=1)
    b_a, b_b = _regroup(b_in, d, r)

    comm = {}

    def on_grads(gr):
        d_up = jnp.concatenate([gr["up2"][0:r, 0:dk_].reshape(r, N_DEV, dks).transpose(1, 0, 2),
                                gr["up2"][r:2 * r, dk_:].reshape(r, N_DEV, dks).transpose(1, 0, 2)], axis=2)
        mine = [_reshard_w_in(gr["w_a1"], gr["w_a2"], gr["w_b"], d, r), gr["conv_proj"].reshape(N_DEV, ds, d),
                gr["gla_proj"].reshape(N_DEV, ds, d), gr["w_out"].reshape(N_DEV, ds, d), gr["conv_w8"], d_up]
        lands = [lax.empty((4,) + a.shape[1:], a.dtype) for a in mine]
        comm["sibling"] = _copies_start("grad_sibling_start", mine, lands, _sibling_copies, 4 * len(mine))
        return (comm["sibling"][4],)

    def on_du_a1(du_a1):
        mine, theirs = _copies_wait("grad_sibling_wait", comm["sibling"], (du_a1,), _sibling_copies)
        sums = [_pair_sum("pair_sum_w_in", mine[0], theirs[0])] + list(_pair_sum_small(mine[1:], theirs[1:]))
        lands = [lax.empty(a.shape, a.dtype) for a in sums]
        comm["chips"] = _copies_start("grad_chips_start", sums, lands, _chip_copies, 3 * len(sums))
        return (comm["chips"][4],)

    g = _local_step(x, c, ctx, loss_target, c_ctx, g_ada, ada_b, norm_g[0:1], w_a, b_a, w_b, b_b,
                    conv_w8, conv_b, conv_ln_g, conv_ln_b, up2, bias2, gla_norm_g, final_norm_g.reshape(1, d),
                    proj, on_grads, on_du_a1)

    ada_w8 = g["ada_w8"]
    ada_sibling = _copies_start("ada_sibling_start", [ada_w8], [lax.empty((4,) + ada_w8.shape[1:], ada_w8.dtype)],
                                _sibling_copies, 4)

    pack = _pack_small(g, x.shape[0], d, r, ada_sibling[4])
    pack_lands = [lax.dynamic_update_slice(lax.empty((N_DEV,) + pack.shape, F32), pack[None], (me_i, 0, 0))]
    small_start = _copies_start("small_gather_start", [pack], pack_lands, _gather_copies, 7)

    (mine_ada,), (their_ada,) = _copies_wait("ada_sibling_wait", ada_sibling, (small_start[4],), _sibling_copies)
    ada_sum = _pair_sum("pair_sum_ada_w", mine_ada, their_ada)
    ada_start = _copies_start("ada_chips_start", [ada_sum], [lax.empty(ada_sum.shape, ada_sum.dtype)],
                              _chip_copies, 3)
    own, landed = _copies_wait("grad_chips_wait", comm["chips"], (ada_start[4],), _chip_copies)
    o_win, o_cp, o_gp, o_wo, o_cw, o_up = own
    x_win, x_cp, x_gp, x_wo, x_cw, x_up = landed

    out = {}

    def big(name, parts, wname, own=None):
        w2 = _as2d(wts[wname])
        res = _sum_adam(name, parts, w2, _as2d(env["m_" + wname]), _as2d(env["v_" + wname]), own)
        for pre, arr in zip(("grad_", "delta_", "new_m_", "new_v_"), res):
            out[pre + wname] = arr.reshape(wts[wname].shape)

    as_rows = lambda a: jnp.transpose(a, (2, 0, 1))
    res = _sum_adam("adam_w_in", x_win, as_rows(w_in), as_rows(m_w_in), as_rows(v_w_in), o_win)
    for pre, arr in zip(("grad_", "delta_", "new_m_", "new_v_"), res):
        out[pre + "w_in"] = jnp.transpose(arr, (1, 2, 0))
    small_w = (("conv_proj", x_cp, o_cp), ("gla_proj", x_gp, o_gp), ("w_out", x_wo, o_wo), ("conv_w", x_cw, o_cw),
               ("decay_up_fwd", x_up[:, :, 0:dks], o_up[:, :, 0:dks]),
               ("decay_up_bwd", x_up[:, :, dks:], o_up[:, :, dks:]))
    small_res = _sum_adam_small([(p, o, _as2d(wts[k]), _as2d(env["m_" + k]), _as2d(env["v_" + k]))
                                 for k, p, o in small_w])
    for (k, _, _), arrs in zip(small_w, small_res):
        for pre, arr in zip(("grad_", "delta_", "new_m_", "new_v_"), arrs):
            out[pre + k] = arr.reshape(wts[k].shape)

    _, (packs,) = _copies_wait("small_gather_wait", small_start, (res[0], out["grad_w_out"]), _gather_copies)
    row = lambda a: a.reshape(1, -1)
    sg, sd, sm, sv, loss = _small_adam(packs, [row(wts[k]) for k in _SMALL], [row(env["m_" + k]) for k in _SMALL],
                                       [row(env["v_" + k]) for k in _SMALL], d, r)
    for i, k in enumerate(_SMALL):
        for pre, arrs in (("grad_", sg), ("delta_", sd), ("new_m_", sm), ("new_v_", sv)):
            out[pre + k] = arrs[i].reshape(wts[k].shape)
    loss = loss.reshape(())

    (o_ada,), (x_ada,) = _copies_wait("ada_chips_wait", ada_start, (res[0], out["grad_w_out"], out["grad_b_in"]),
                                      _chip_copies)
    big("adam_ada_w", x_ada, "ada_w", o_ada)

    return (loss, g["grad_x"], *[out["grad_" + k] for k in _WEIGHTS], *[out["delta_" + k] for k in _WEIGHTS],
            *[out["new_m_" + k] for k in _WEIGHTS], *[out["new_v_" + k] for k in _WEIGHTS])
```

```python
import functools
import math

import jax
import jax.numpy as jnp
from jax import lax
from jax.experimental import pallas as pl
from jax.experimental.pallas import tpu as pltpu

F32 = jnp.float32
BF16 = jnp.bfloat16
MESH = pl.DeviceIdType.MESH

N_DEV = 8
GRID_W = 64
CHUNK = 128
HEADS = 4
EPS = 1e-6
GATE_TAU = 16.0
LANE = 128
ADAM_LR, ADAM_B1, ADAM_B2, ADAM_EPS, ADAM_WD, ADAM_STEP = 0.001, 0.9, 0.999, 1e-08, 0.01, 10
VMEM_LIMIT = 60 * 1024 * 1024
_ANY = pl.BlockSpec(memory_space=pl.ANY)


def _params(**kw):
    return pltpu.CompilerParams(vmem_limit_bytes=VMEM_LIMIT, **kw)


def _tile(n, pref):
    t = (min(pref, n) // LANE) * LANE
    while t >= LANE:
        if n % t == 0:
            return t
        t -= LANE
    return n


def _mm(a, b):
    return jnp.dot(a.astype(BF16), b.astype(BF16), preferred_element_type=F32)


def _mm_nt(a, b):
    return lax.dot_general(a.astype(BF16), b.astype(BF16), (((1,), (1,)), ((), ())), preferred_element_type=F32)


def _mm_tn(a, b):
    return lax.dot_general(a.astype(BF16), b.astype(BF16), (((0,), (0,)), ((), ())), preferred_element_type=F32)


def _sigmoid(x):
    return 0.5 * jnp.tanh(0.5 * x) + 0.5


def _dsilu(x, s):
    return s * (1.0 + x * (1.0 - s))


def _adamw(g, w, m, v):
    bc1 = 1.0 - ADAM_B1 ** ADAM_STEP
    bc2 = 1.0 - ADAM_B2 ** ADAM_STEP
    mn = ADAM_B1 * m + (1.0 - ADAM_B1) * g
    vn = ADAM_B2 * v + (1.0 - ADAM_B2) * (g * g)
    delta = -ADAM_LR * ((mn / bc1) / (jnp.sqrt(vn / bc2) + ADAM_EPS) + ADAM_WD * w)
    return delta, mn, vn


def _rowsel(table, idx, n):
    out = table[0:1, :]
    for r in range(1, n):
        out = jnp.where(idx == r, table[r:r + 1, :], out)
    return out


def _ada_fwd(cv, ada_w8, ada_b):
    n_sh, _, ws = ada_w8.shape

    def body(cv_ref, w_ref, b_ref, o_ref):
        c = cv_ref[...]
        sv = c * _sigmoid(c)
        for j in range(n_sh):
            cols = pl.ds(j * ws, ws)
            o_ref[:, cols] = _mm(sv, w_ref[j]) + b_ref[:, cols]

    return pl.pallas_call(body, name="ada_fwd", out_shape=jax.ShapeDtypeStruct((cv.shape[0], n_sh * ws), F32),
                          compiler_params=_params())(cv, ada_w8, ada_b)


def _ada_bwd(cv, ada_w8, dmod_ss, small):
    n_sh, d, ws = ada_w8.shape

    def body(cv_ref, w_ref, dm_ref, sm_ref, sv_ref, dmod_ref, db_ref, dc_ref):
        c = cv_ref[...]
        s = _sigmoid(c)
        sv_ref[...] = c * s
        dm = jnp.concatenate([dm_ref[:, 0:2 * d], sm_ref[8:16, :]], axis=1)
        dmod_ref[...] = dm
        db_ref[...] = jnp.sum(dm, axis=0, keepdims=True)
        dsv = None
        for j in range(n_sh):
            part = _mm_nt(dm[:, j * ws:(j + 1) * ws], w_ref[j])
            dsv = part if dsv is None else dsv + part
        dc_ref[...] = dsv * _dsilu(c, s)

    return pl.pallas_call(
        body, name="ada_bwd",
        out_shape=(jax.ShapeDtypeStruct(cv.shape, F32), jax.ShapeDtypeStruct(dmod_ss.shape, F32),
                   jax.ShapeDtypeStruct((1, n_sh * ws), F32), jax.ShapeDtypeStruct(cv.shape, F32)),
        compiler_params=_params())(cv, ada_w8, dmod_ss, small)


def _ada_adam(sv_all, dmod_all, w, m, v, n_terms):
    n_dev, rows, d = sv_all.shape
    ws = w.shape[1]

    def body(i_ref, sv_ref, dm_ref, w_ref, m_ref, v_ref, g_ref, d_ref, nm_ref, nv_ref):
        sv_t = jnp.transpose(sv_ref[...].reshape(n_dev * rows, d))
        dm = dm_ref[...]
        blk = 64
        for r0 in range(0, d, blk):
            g = None
            for k in range(n_dev):
                for row in range(n_terms):
                    col = k * rows + row
                    term = sv_t[r0:r0 + blk, col:col + 1] * dm[k, row:row + 1, :]
                    g = term if g is None else g + term
            sl = pl.ds(r0, blk)
            g_ref[sl, :] = g
            d_ref[sl, :], nm_ref[sl, :], nv_ref[sl, :] = _adamw(g, w_ref[sl, :], m_ref[sl, :], v_ref[sl, :])

    whole = lambda shape: pl.BlockSpec(shape, lambda i, i_ref: (0,) * len(shape))
    o = jax.ShapeDtypeStruct(w.shape, F32)
    me_i = 4 * lax.axis_index("x") + 2 * lax.axis_index("y") + lax.axis_index("c")
    return pl.pallas_call(
        body, name="adam_ada_w",
        grid_spec=pltpu.PrefetchScalarGridSpec(
            num_scalar_prefetch=1, grid=(1,),
            in_specs=[whole(sv_all.shape), pl.BlockSpec((n_dev, rows, ws), lambda i, i_ref: (0, 0, i_ref[0])),
                      whole(w.shape), whole(w.shape), whole(w.shape)],
            out_specs=(whole(w.shape),) * 4),
        out_shape=(o, o, o, o), compiler_params=_params())(me_i.reshape(1), sv_all, dmod_all, w, m, v)


class _Tiles:
    def __init__(self, nb, s_len, c_len, tm, big):
        self.nb, self.tm, self.big = nb, tm, big
        self.lat, self.ctx = s_len // tm, c_len // tm
        self.pad = -(self.lat + self.ctx) % big
        self.per_ex = self.lat + self.ctx + self.pad
        self.n_all = nb * self.per_ex
        self.rows_per_ex = self.per_ex * tm

    def is_lat(self, i):
        return i % self.per_ex < self.lat

    def is_pad(self, i):
        return i % self.per_ex >= self.lat + self.ctx

    def lat_of_all(self, i):
        return (i // self.per_ex) * self.lat + jnp.minimum(i % self.per_ex, self.lat - 1)

    def ctx_of_all(self, i):
        return (i // self.per_ex) * self.ctx + jnp.clip(i % self.per_ex - self.lat, 0, self.ctx - 1)


def _norm_fwd(x2, ctx2, mod, norm_g, tiles):
    tl, d = x2.shape
    tc = ctx2.shape[0]
    nb, tm = tiles.nb, tiles.tm

    def body(x_ref, c_ref, mod_ref, g_ref, u_ref):
        i = pl.program_id(0)
        lat = tiles.is_lat(i)
        xv = jnp.where(lat, x_ref[...], c_ref[...])
        row = jnp.where(lat, i // tiles.per_ex, nb)
        m = _rowsel(mod_ref[...], row, nb + 1)
        shift, scale = m[:, 0:d], m[:, d:2 * d]
        rstd = lax.rsqrt(jnp.mean(xv * xv, axis=-1, keepdims=True) + EPS)
        u = xv * rstd * g_ref[...] * (1.0 + scale) + shift
        u_ref[...] = jnp.where(tiles.is_pad(i), 0.0, u).astype(BF16)

    return pl.pallas_call(
        body, name="norm_fwd", grid=(tiles.n_all,),
        in_specs=[pl.BlockSpec((tm, d), lambda i: (tiles.lat_of_all(i), 0)),
                  pl.BlockSpec((tm, d), lambda i: (tiles.ctx_of_all(i), 0)),
                  pl.BlockSpec(mod.shape, lambda i: (0, 0)),
                  pl.BlockSpec((1, d), lambda i: (0, 0))],
        out_specs=pl.BlockSpec((tm, d), lambda i: (i, 0)),
        out_shape=jax.ShapeDtypeStruct((tiles.n_all * tm, d), BF16),
        compiler_params=_params())(x2, ctx2, mod, norm_g)


def _norm_bwd(x2, ctx2, mod, norm_g, du_lat, du_b, gx1, tiles):
    tl, d = x2.shape
    nb, tm = tiles.nb, tiles.tm
    nrow = mod.shape[0]
    n_lat_in = len(du_lat)

    def body(x_ref, c_ref, mod_ref, g_ref, *refs):
        dl_refs = refs[:n_lat_in]
        d3_ref, gx_ref, gxo_ref, dmod_ref, dg_ref = refs[n_lat_in:]
        i = pl.program_id(0)

        @pl.when(i == 0)
        def _():
            dmod_ref[...] = jnp.zeros_like(dmod_ref)
            dg_ref[...] = jnp.zeros_like(dg_ref)

        lat = tiles.is_lat(i)
        xv = jnp.where(lat, x_ref[...], c_ref[...])
        row = jnp.where(lat, i // tiles.per_ex, nb)
        m = _rowsel(mod_ref[...], row, nb + 1)
        scale = m[:, d:2 * d]
        g = g_ref[...]
        dl = dl_refs[0][...].astype(F32)
        for ref in dl_refs[1:]:
            dl = dl + ref[...].astype(F32)
        du = jnp.where(tiles.is_pad(i), 0.0, d3_ref[...].astype(F32) + jnp.where(lat, dl, 0.0))
        rstd = lax.rsqrt(jnp.mean(xv * xv, axis=-1, keepdims=True) + EPS)
        xh = xv * rstd
        dshift = jnp.sum(du, axis=0, keepdims=True)
        dscale = jnp.sum(du * xh * g, axis=0, keepdims=True)
        dxn = du * (1.0 + scale)
        dg_ref[...] += jnp.sum(dxn * xh, axis=0, keepdims=True)
        dxh = dxn * g
        dx = rstd * (dxh - xh * jnp.mean(dxh * xh, axis=-1, keepdims=True))

        @pl.when(lat)
        def _():
            gxo_ref[...] = dx + gx_ref[...]

        for r in range(nb + 1):
            dmod_ref[r:r + 1, 0:d] += jnp.where(row == r, dshift, 0.0)
            dmod_ref[r:r + 1, d:2 * d] += jnp.where(row == r, dscale, 0.0)

    lat_map = lambda i: (tiles.lat_of_all(i), 0)
    lat_spec = pl.BlockSpec((tm, d), lat_map)
    return pl.pallas_call(
        body, name="norm_bwd", grid=(tiles.n_all,),
        in_specs=[lat_spec,
                  pl.BlockSpec((tm, d), lambda i: (tiles.ctx_of_all(i), 0)),
                  pl.BlockSpec(mod.shape, lambda i: (0, 0)),
                  pl.BlockSpec((1, d), lambda i: (0, 0))]
                 + [lat_spec] * n_lat_in
                 + [pl.BlockSpec((tm, d), lambda i: (i, 0)), lat_spec],
        out_specs=(lat_spec,
                   pl.BlockSpec((nrow, 3 * d), lambda i: (0, 0)),
                   pl.BlockSpec((1, d), lambda i: (0, 0))),
        out_shape=(jax.ShapeDtypeStruct((tl, d), F32), jax.ShapeDtypeStruct((nrow, 3 * d), F32),
                   jax.ShapeDtypeStruct((1, d), F32)),
        compiler_params=_params())(x2, ctx2, mod, norm_g, *du_lat, du_b, gx1)


def _matmul_bias(name, u3, w, b, s_len, tm, tn):
    nb = u3.shape[0]
    d, n = w.shape
    per = s_len // tm
    rows = nb * s_len

    def body(u_ref, w_ref, b_ref, o_ref):
        o_ref[...] = jnp.dot(u_ref[...], w_ref[...], preferred_element_type=F32) + b_ref[...]

    return pl.pallas_call(
        body, name=name, grid=(n // tn, rows // tm),
        in_specs=[pl.BlockSpec((None, tm, d), lambda j, i: (i // per, i % per, 0)),
                  pl.BlockSpec((d, tn), lambda j, i: (0, j)),
                  pl.BlockSpec((1, tn), lambda j, i: (0, j))],
        out_specs=pl.BlockSpec((tm, tn), lambda j, i: (i, j)),
        out_shape=jax.ShapeDtypeStruct((rows, n), F32),
        compiler_params=_params())(u3, w, b)


def _log_sigmoid(x):
    return jnp.minimum(x, 0.0) - jnp.log(1.0 + jnp.exp(-jnp.abs(x)))


def _inproj_b(u, w_b, b_b, up2, bias2, tm, dk_, dv_):
    t_all, d = u.shape
    nbw = w_b.shape[1]
    n2 = up2.shape[1]

    def body(u_ref, w_ref, b_ref, up_ref, bias_ref, qk_ref, v_ref, g_ref):
        full = jnp.dot(u_ref[...], w_ref[...], preferred_element_type=F32) + b_ref[...]
        lr = full[:, 2 * dk_ + dv_:nbw]
        qk_ref[:, 0:2 * dk_] = full[:, 0:2 * dk_]
        qk_ref[:, 2 * dk_:2 * dk_ + LANE] = lr
        v_ref[...] = full[:, 2 * dk_:2 * dk_ + dv_].astype(BF16)
        g_ref[...] = _log_sigmoid(_mm(lr, up_ref[...]) + bias_ref[...]) * (1.0 / GATE_TAU)

    whole = lambda a: pl.BlockSpec(a.shape, lambda i: (0, 0))
    return pl.pallas_call(
        body, name="inproj_b", grid=(t_all // tm,),
        in_specs=[pl.BlockSpec((tm, d), lambda i: (i, 0)), whole(w_b), whole(b_b), whole(up2), whole(bias2)],
        out_specs=(pl.BlockSpec((tm, 2 * dk_ + LANE), lambda i: (i, 0)), pl.BlockSpec((tm, dv_), lambda i: (i, 0)),
                   pl.BlockSpec((tm, n2), lambda i: (i, 0))),
        out_shape=(jax.ShapeDtypeStruct((t_all, 2 * dk_ + LANE), F32), jax.ShapeDtypeStruct((t_all, dv_), BF16),
                   jax.ShapeDtypeStruct((t_all, n2), F32)),
        compiler_params=_params())(u, w_b, b_b, up2, bias2)


def _matmul_nt(name, a, w, koff, tm, tk, after=()):
    r, kc = a.shape
    d = w.shape[0]
    nk = kc // tk

    def body(a_ref, w_ref, *rest):
        o_ref = rest[len(after)]
        k = pl.program_id(1)
        p = lax.dot_general(a_ref[...], w_ref[...], (((1,), (1,)), ((), ())), preferred_element_type=F32)
        if nk == 1:
            o_ref[...] = p.astype(o_ref.dtype)
            return
        acc_ref = rest[len(after) + 1]

        @pl.when(k == 0)
        def _():
            acc_ref[...] = p

        @pl.when(k > 0)
        def _():
            acc_ref[...] += p

        @pl.when(k == nk - 1)
        def _():
            o_ref[...] = acc_ref[...].astype(o_ref.dtype)

    return pl.pallas_call(
        body, name=name, grid=(r // tm, nk),
        in_specs=[pl.BlockSpec((tm, tk), lambda i, k: (i, k)),
                  pl.BlockSpec((d, tk), lambda i, k: (0, koff + k))] + [_ANY] * len(after),
        out_specs=pl.BlockSpec((tm, d), lambda i, k: (i, 0)),
        out_shape=jax.ShapeDtypeStruct((r, d), BF16),
        scratch_shapes=[pltpu.VMEM((tm, d), F32)] if nk > 1 else [],
        compiler_params=_params())(a, w, *after)


def _matmul_tn(name, a, b, rows, tk, tn):
    m = a.shape[1]
    n = b.shape[1]
    nk = rows // tk

    def body(a_ref, b_ref, o_ref, s_ref, acc_ref):
        k = pl.program_id(1)
        bv = b_ref[...]
        p = lax.dot_general(bv, a_ref[...], (((0,), (0,)), ((), ())), preferred_element_type=F32)
        cs = jnp.sum(bv.astype(F32), axis=0, keepdims=True)

        @pl.when(k == 0)
        def _():
            acc_ref[...] = p
            s_ref[...] = cs

        @pl.when(k > 0)
        def _():
            acc_ref[...] += p
            s_ref[...] += cs

        @pl.when(k == nk - 1)
        def _():
            o_ref[...] = acc_ref[...].astype(o_ref.dtype)

    return pl.pallas_call(
        body, name=name, grid=(n // tn, nk),
        in_specs=[pl.BlockSpec((tk, m), lambda j, k: (k, 0)),
                  pl.BlockSpec((tk, tn), lambda j, k: (k, j))],
        out_specs=(pl.BlockSpec((tn, m), lambda j, k: (j, 0)), pl.BlockSpec((1, tn), lambda j, k: (0, j))),
        out_shape=(jax.ShapeDtypeStruct((n, m), BF16), jax.ShapeDtypeStruct((1, n), F32)),
        scratch_shapes=[pltpu.VMEM((tn, m), F32)],
        compiler_params=_params())(a, b)


def _matmul_tn_whole(name, a3, b3, rows, tn, transposed):
    nb, _, m = a3.shape
    n = b3.shape[2]

    def body(a_ref, b_ref, o_ref, s_ref):
        p, cs = None, None
        for e in range(nb):
            bv = b_ref[e]
            lhs, rhs = (bv, a_ref[e]) if transposed else (a_ref[e], bv)
            pe = lax.dot_general(lhs, rhs, (((0,), (0,)), ((), ())), preferred_element_type=F32)
            ce = jnp.sum(bv.astype(F32), axis=0, keepdims=True)
            p, cs = (pe, ce) if p is None else (p + pe, cs + ce)
        o_ref[...] = p.astype(o_ref.dtype)
        s_ref[...] = cs

    o_spec, o_shape = ((pl.BlockSpec((tn, m), lambda j: (j, 0)), (n, m)) if transposed
                       else (pl.BlockSpec((m, tn), lambda j: (0, j)), (m, n)))
    return pl.pallas_call(
        body, name=name, grid=(n // tn,),
        in_specs=[pl.BlockSpec((nb, rows, m), lambda j: (0, 0, 0)),
                  pl.BlockSpec((nb, rows, tn), lambda j: (0, 0, j))],
        out_specs=(o_spec, pl.BlockSpec((1, tn), lambda j: (0, j))),
        out_shape=(jax.ShapeDtypeStruct(o_shape, BF16), jax.ShapeDtypeStruct((1, n), F32)),
        compiler_params=_params())(a3, b3)


def _conv_window(pad_ref, r, shift, ktaps, width, horizontal):
    if horizontal:
        return pad_ref[r, pl.ds(16 + shift, width), :]
    return pad_ref[r + ktaps // 2 + shift]


def _conv_row(pad_ref, w, r, ktaps, width, horizontal, flip):
    half = ktaps // 2
    acc = None
    for t in range(ktaps):
        win = _conv_window(pad_ref, r, (half - t) if flip else (t - half), ktaps, width, horizontal)
        term = win * w[t:t + 1, :]
        acc = term if acc is None else acc + term
    return acc


def _fill_padded(ref, val, rows, width, ktaps, horizontal):
    half_k = ktaps // 2
    cb = val.shape[-1]
    if horizontal:
        ref[:, 0:16, :] = jnp.zeros((rows, 16, cb), F32)
        ref[:, 16 + width:32 + width, :] = jnp.zeros((rows, 16, cb), F32)
        ref[:, 16:16 + width, :] = val
    else:
        ref[0:half_k, :, :] = jnp.zeros((half_k, width, cb), F32)
        ref[half_k + rows:2 * half_k + rows, :, :] = jnp.zeros((half_k, width, cb), F32)
        ref[half_k:half_k + rows, :, :] = val


def _conv_fwd(pa, conv_w8, conv_b, nb, s):
    nblk, ktaps, cb = conv_w8.shape
    d = nblk * cb
    rows, width = s // GRID_W, GRID_W
    half_k = ktaps // 2
    nh = nblk // 2

    def body(glu_ref, w_ref, b_ref, o_ref, ph_ref, pv_ref):
        j = pl.program_id(1)
        a0 = (glu_ref[:, 0:cb] * _sigmoid(glu_ref[:, cb:2 * cb])).reshape(rows, width, cb)
        w = w_ref[...]

        bias = b_ref[...]

        def run(pad_ref, horizontal):
            _fill_padded(pad_ref, a0, rows, width, ktaps, horizontal)

            def row(r, carry):
                at = pl.ds(pl.multiple_of(r * width, width), width)
                o_ref[at, :] = _conv_row(pad_ref, w, r, ktaps, width, horizontal, False) + bias
                return carry

            lax.fori_loop(0, rows, row, 0)

        @pl.when(j < nh)
        def _():
            run(ph_ref, True)

        @pl.when(j >= nh)
        def _():
            run(pv_ref, False)

    return pl.pallas_call(
        body, name="conv_fwd", grid=(nb, nblk),
        in_specs=[pl.BlockSpec((s, 2 * cb), lambda b, j: (b, j)),
                  pl.BlockSpec((None, ktaps, cb), lambda b, j: (j, 0, 0)),
                  pl.BlockSpec((1, cb), lambda b, j: (0, j))],
        out_specs=pl.BlockSpec((s, cb), lambda b, j: (b, j)),
        out_shape=jax.ShapeDtypeStruct((nb * s, d), F32),
        scratch_shapes=[pltpu.VMEM((rows, width + 32, cb), F32), pltpu.VMEM((rows + 2 * half_k, width, cb), F32)],
        compiler_params=_params())(pa, conv_w8, conv_b)


def _conv_bwd(pa, da1, conv_w8, nb, s):
    nblk, ktaps, cb = conv_w8.shape
    d = nblk * cb
    rows, width = s // GRID_W, GRID_W
    half_k = ktaps // 2
    nh = nblk // 2

    def body(glu_ref, da_ref, w_ref, dp_ref, dw_ref, db_ref, pha_ref, phd_ref, pva_ref, pvd_ref):
        j = pl.program_id(0)
        b = pl.program_id(1)
        a0 = (glu_ref[:, 0:cb] * _sigmoid(glu_ref[:, cb:2 * cb])).reshape(rows, width, cb)
        da1v = da_ref[...]
        d3 = da1v.reshape(rows, width, cb)
        w = w_ref[...]

        @pl.when(b == 0)
        def _():
            dw_ref[...] = jnp.zeros_like(dw_ref)
            db_ref[...] = jnp.zeros_like(db_ref)

        db_ref[...] += jnp.sum(da1v, axis=0, keepdims=True)

        def run(pa_ref, pd_ref, horizontal):
            _fill_padded(pa_ref, a0, rows, width, ktaps, horizontal)
            _fill_padded(pd_ref, d3, rows, width, ktaps, horizontal)

            def row(r, accs):
                at = pl.ds(pl.multiple_of(r * width, width), width)
                da0 = _conv_row(pd_ref, w, r, ktaps, width, horizontal, True)
                gv = glu_ref[at, 0:cb]
                sg = _sigmoid(glu_ref[at, cb:2 * cb])
                dp_ref[at, 0:cb] = (da0 * sg).astype(BF16)
                dp_ref[at, cb:2 * cb] = (da0 * gv * sg * (1.0 - sg)).astype(BF16)
                d_row = da_ref[at, :]
                out = []
                for t in range(ktaps):
                    prod = _conv_window(pa_ref, r, t - half_k, ktaps, width, horizontal) * d_row
                    out.append(accs[t] + jnp.sum(prod.reshape(width // 8, 8, cb), axis=0))
                return tuple(out)

            accs = lax.fori_loop(0, rows, row, tuple(jnp.zeros((8, cb), F32) for _ in range(ktaps)))
            for t in range(ktaps):
                dw_ref[t:t + 1, :] += jnp.sum(accs[t], axis=0, keepdims=True)

        @pl.when(j < nh)
        def _():
            run(pha_ref, phd_ref, True)

        @pl.when(j >= nh)
        def _():
            run(pva_ref, pvd_ref, False)

    return pl.pallas_call(
        body, name="conv_bwd", grid=(nblk, nb),
        in_specs=[pl.BlockSpec((s, 2 * cb), lambda j, b: (b, j)),
                  pl.BlockSpec((s, cb), lambda j, b: (b, j)),
                  pl.BlockSpec((None, ktaps, cb), lambda j, b: (j, 0, 0))],
        out_specs=(pl.BlockSpec((s, 2 * cb), lambda j, b: (b, j)),
                   pl.BlockSpec((None, ktaps, cb), lambda j, b: (j, 0, 0)),
                   pl.BlockSpec((1, cb), lambda j, b: (0, j))),
        out_shape=(jax.ShapeDtypeStruct((nb * s, 2 * d), BF16),
                   jax.ShapeDtypeStruct((nblk, ktaps, cb), F32), jax.ShapeDtypeStruct((1, d), F32)),
        scratch_shapes=[pltpu.VMEM((rows, width + 32, cb), F32), pltpu.VMEM((rows, width + 32, cb), F32),
                        pltpu.VMEM((rows + 2 * half_k, width, cb), F32),
                        pltpu.VMEM((rows + 2 * half_k, width, cb), F32)],
        compiler_params=_params())(pa, da1, conv_w8)


def _decay_bwd(pb, up2, bias2, grads_f, grads_b, tiles, lr_blk, dk_, dv_):
    t_all = pb.shape[0]
    tm = tiles.tm
    n2 = up2.shape[1]
    nbw = 2 * dk_ + dv_ + LANE

    def body(lr_ref, up_ref, b_ref, dqf, dkf, dvf, dgf, dqb, dkb, dvb, dgb, dp_ref, dup_ref, dbias_ref):
        i = pl.program_id(0)
        pad = tiles.is_pad(i)
        live = lambda v: jnp.where(pad, 0.0, v)

        @pl.when(i == 0)
        def _():
            dup_ref[...] = jnp.zeros_like(dup_ref)
            dbias_ref[...] = jnp.zeros_like(dbias_ref)

        lr = lr_ref[...]
        up = up_ref[...]
        logits = _mm(lr, up) + b_ref[...]
        dg = live(jnp.concatenate([dgf[...], dgb[...]], axis=1))
        dlog = dg * (1.0 / GATE_TAU) * _sigmoid(-logits)
        dup_ref[...] += _mm_tn(lr, dlog)
        dbias_ref[...] += jnp.sum(dlog, axis=0, keepdims=True)
        both = lambda f, b: live(f[...].astype(F32) + b[...].astype(F32)).astype(BF16)
        dp_ref[:, 0:dk_] = both(dqf, dqb)
        dp_ref[:, dk_:2 * dk_] = both(dkf, dkb)
        dp_ref[:, 2 * dk_:2 * dk_ + dv_] = both(dvf, dvb)
        dp_ref[:, 2 * dk_ + dv_:nbw] = _mm_nt(dlog, up).astype(BF16)

    row = lambda w: pl.BlockSpec((tm, w), lambda i: (i, 0))
    return pl.pallas_call(
        body, name="decay_bwd", grid=(t_all // tm,),
        in_specs=[pl.BlockSpec((tm, LANE), lambda i: (i, lr_blk)),
                  pl.BlockSpec(up2.shape, lambda i: (0, 0)),
                  pl.BlockSpec((1, n2), lambda i: (0, 0)),
                  row(dk_), row(dk_), row(dv_), row(dk_), row(dk_), row(dk_), row(dv_), row(dk_)],
        out_specs=(row(nbw), pl.BlockSpec(up2.shape, lambda i: (0, 0)), pl.BlockSpec((1, n2), lambda i: (0, 0))),
        out_shape=(jax.ShapeDtypeStruct((t_all, nbw), BF16), jax.ShapeDtypeStruct(up2.shape, F32),
                   jax.ShapeDtypeStruct((1, n2), F32)),
        compiler_params=_params())(pb, up2, bias2, *grads_f, *grads_b)


def _scan_chunk(s, nl, nc, rev):
    if rev:
        return jnp.where(s < nc, nl + (nc - 1 - s), nl - 1 - (s - nc))
    return jnp.where(s < nc, nl + s, s - nc)


def _scan_lat_chunk(s, nl, nc, rev):
    first = nl - 1 if rev else 0
    return jnp.where(s < nc, first, _scan_chunk(s, nl, nc, rev))


def _tri_mm(m_bf, x):
    hi = x.astype(BF16)
    r1 = x - hi.astype(F32)
    mid = r1.astype(BF16)
    lo = (r1 - mid.astype(F32)).astype(BF16)
    dot = lambda p: jnp.dot(m_bf, p, preferred_element_type=F32)
    return dot(hi) + dot(mid) + dot(lo)


def _chunk_masks(c, rev):
    ii = lax.broadcasted_iota(jnp.int32, (c, c), 0)
    jj = lax.broadcasted_iota(jnp.int32, (c, c), 1)
    return ((ii <= jj), (ii >= jj)) if rev else ((ii >= jj), (ii <= jj))


def _chunk_terms(q, k, b, far, mid):
    bf, bm = b[far:far + 1, :], b[mid:mid + 1, :]
    e = jnp.exp(b)
    em = jnp.exp(b - bm)
    eim = jnp.exp(bm - b)
    ed = jnp.exp(bf - b)
    return dict(e=e, em=em, eim=eim, ed=ed, dec=jnp.exp(bf), qe=q * e, qem=q * em, kim=k * eim, kd=k * ed)


def _gla_fwd(pb3, pv3, g3, nb, s_len, c_len, dk_, dv_):
    c = CHUNK
    nl, nc = s_len // c, c_len // c
    ns = nl + nc
    hk, hv = dk_ // HEADS, dv_ // HEADS
    l_len = pb3.shape[1]
    scale = hk ** -0.5
    mid = c // 2

    def body(*refs):
        ins, outs, z_scr = refs[:8], refs[8:14], refs[14]
        s = pl.program_id(0)

        @pl.when(s == 0)
        def _():
            z_scr[...] = jnp.zeros_like(z_scr)

        qs = jnp.where(s >= nc, scale, 0.0)
        for di, rev in enumerate((False, True)):
            q_ref, k_ref, v_ref, g_ref = ins[4 * di:4 * di + 4]
            o_ref, zs_ref, b_ref = outs[3 * di:3 * di + 3]
            mask, _ = _chunk_masks(c, rev)
            m_bf = mask.astype(BF16)
            far = 0 if rev else c - 1
            for b in range(nb):
                bc = _tri_mm(m_bf, g_ref[b])
                b_ref[b] = bc
                for h in range(HEADS):
                    ks, vs = slice(h * hk, (h + 1) * hk), slice(h * hv, (h + 1) * hv)
                    zi = (di * nb + b) * HEADS + h
                    v = v_ref[b, :, vs]
                    t = _chunk_terms(q_ref[b, :, ks] * qs, k_ref[b, :, ks], bc[:, ks], far, mid)
                    a = jnp.where(mask, _mm_nt(t["qem"], t["kim"]), 0.0)
                    z = z_scr[zi]
                    zs_ref[0, b * HEADS + h] = z
                    o_ref[b, :, vs] = _mm(a, v) + _mm_nt(t["qe"], z)
                    z_scr[zi] = z * t["dec"] + _mm_tn(v, t["kd"])

    in_specs, out_specs, out_shape = [], [], []
    for di, rev in enumerate((False, True)):
        ch = functools.partial(_scan_chunk, nl=nl, nc=nc, rev=rev)
        lch = functools.partial(_scan_lat_chunk, nl=nl, nc=nc, rev=rev)
        in_specs += [pl.BlockSpec((nb, c, dk_), lambda s, ch=ch: (0, ch(s), 0)),
                     pl.BlockSpec((nb, c, dk_), lambda s, ch=ch: (0, ch(s), 1)),
                     pl.BlockSpec((nb, c, dv_), lambda s, ch=ch: (0, ch(s), 0)),
                     pl.BlockSpec((nb, c, dk_), lambda s, ch=ch, di=di: (0, ch(s), di))]
        out_specs += [pl.BlockSpec((nb, c, dv_), lambda s, lch=lch: (0, lch(s), 0)),
                      pl.BlockSpec((1, nb * HEADS, hv, hk), lambda s: (s, 0, 0, 0)),
                      pl.BlockSpec((nb, c, dk_), lambda s, ch=ch: (0, ch(s), 0))]
        out_shape += [jax.ShapeDtypeStruct((nb, s_len, dv_), F32),
                      jax.ShapeDtypeStruct((ns, nb * HEADS, hv, hk), F32),
                      jax.ShapeDtypeStruct((nb, l_len, dk_), F32)]
    return pl.pallas_call(
        body, name="gla_fwd", grid=(ns,), in_specs=in_specs, out_specs=tuple(out_specs), out_shape=tuple(out_shape),
        scratch_shapes=[pltpu.VMEM((2 * nb * HEADS, hv, hk), F32)],
        compiler_params=_params())(pb3, pb3, pv3, g3, pb3, pb3, pv3, g3)


def _gla_bwd(pb3, pv3, do3, fwd_saved, nb, s_len, c_len, dk_, dv_):
    c = CHUNK
    nl, nc = s_len // c, c_len // c
    ns = nl + nc
    hk, hv = dk_ // HEADS, dv_ // HEADS
    l_len = pb3.shape[1]
    scale = hk ** -0.5
    mid = c // 2
    zs_f, b_f, zs_b, b_b = fwd_saved

    def body(*refs):
        ins, outs, dz_scr = refs[:12], refs[12:20], refs[20]
        s = pl.program_id(0)
        step = ns - 1 - s

        @pl.when(s == 0)
        def _():
            dz_scr[...] = jnp.zeros_like(dz_scr)

        lat = step >= nc
        qs = jnp.where(lat, scale, 0.0)
        dmul = jnp.where(lat, 1.0, 0.0)
        for di, rev in enumerate((False, True)):
            q_ref, k_ref, v_ref, b_ref, do_ref, zs_ref = ins[6 * di:6 * di + 6]
            dq_ref, dk_ref, dv_ref, dg_ref = outs[4 * di:4 * di + 4]
            mask, mask_t = _chunk_masks(c, rev)
            mt_bf = mask_t.astype(BF16)
            far = 0 if rev else c - 1
            far_row = lax.broadcasted_iota(jnp.int32, (c, hk), 0) == far
            for b in range(nb):
                db_parts = []
                for h in range(HEADS):
                    ks, vs = slice(h * hk, (h + 1) * hk), slice(h * hv, (h + 1) * hv)
                    zi = (di * nb + b) * HEADS + h
                    v = v_ref[b, :, vs]
                    d_o = do_ref[b, :, vs] * dmul
                    t = _chunk_terms(q_ref[b, :, ks] * qs, k_ref[b, :, ks], b_ref[b, :, ks], far, mid)
                    qem, kim, qe, kd = t["qem"], t["kim"], t["qe"], t["kd"]
                    a_t = jnp.where(mask_t, _mm_nt(kim, qem), 0.0)
                    d_a = jnp.where(mask, _mm_nt(d_o, v), 0.0)
                    d_at = jnp.where(mask_t, _mm_nt(v, d_o), 0.0)
                    z = zs_ref[0, b * HEADS + h]
                    dzn = dz_scr[zi]
                    dv_ref[b, :, vs] = (_mm(a_t, d_o) + _mm_nt(kd, dzn)).astype(dv_ref.dtype)
                    dqem = _mm(d_a, kim)
                    dkim = _mm(d_at, qem)
                    dqe = _mm(d_o, z)
                    dkd = _mm(v, dzn)
                    ddec = jnp.sum(z * dzn, axis=0, keepdims=True)
                    dz_scr[zi] = dzn * t["dec"] + _mm_tn(d_o, qe)
                    dq_ref[b, :, ks] = ((dqem * t["em"] + dqe * t["e"]) * qs).astype(dq_ref.dtype)
                    dk_ref[b, :, ks] = (dkim * t["eim"] + dkd * t["ed"]).astype(dk_ref.dtype)
                    db = dqem * qem - dkim * kim + dqe * qe - dkd * kd
                    extra = jnp.sum(dkd * kd, axis=0, keepdims=True) + ddec * t["dec"]
                    db_parts.append(db + jnp.where(far_row, extra, 0.0))
                dg_ref[b] = _tri_mm(mt_bf, jnp.concatenate(db_parts, axis=1))

    in_specs, out_specs, out_shape, args = [], [], [], []
    for di, rev in enumerate((False, True)):
        ch = lambda s, rev=rev: _scan_chunk(ns - 1 - s, nl, nc, rev)
        lch = lambda s, rev=rev: _scan_lat_chunk(ns - 1 - s, nl, nc, rev)
        in_specs += [pl.BlockSpec((nb, c, dk_), lambda s, ch=ch: (0, ch(s), 0)),
                     pl.BlockSpec((nb, c, dk_), lambda s, ch=ch: (0, ch(s), 1)),
                     pl.BlockSpec((nb, c, dv_), lambda s, ch=ch: (0, ch(s), 0)),
                     pl.BlockSpec((nb, c, dk_), lambda s, ch=ch: (0, ch(s), 0)),
                     pl.BlockSpec((nb, c, dv_), lambda s, lch=lch: (0, lch(s), 0)),
                     pl.BlockSpec((1, nb * HEADS, hv, hk), lambda s: (ns - 1 - s, 0, 0, 0))]
        args += [pb3, pb3, pv3, (b_b if rev else b_f), do3, (zs_b if rev else zs_f)]
        for w, dt in ((dk_, BF16), (dk_, BF16), (dv_, BF16), (dk_, F32)):
            out_specs.append(pl.BlockSpec((nb, c, w), lambda s, ch=ch: (0, ch(s), 0)))
            out_shape.append(jax.ShapeDtypeStruct((nb, l_len, w), dt))
    return pl.pallas_call(
        body, name="gla_bwd", grid=(ns,), in_specs=in_specs, out_specs=tuple(out_specs), out_shape=tuple(out_shape),
        scratch_shapes=[pltpu.VMEM((2 * nb * HEADS, hv, hk), F32)],
        compiler_params=_params())(*args)


def _tail(a1, pa, o_f, o_b, x2, tgt, mod, wc, wg, wo, ln_g, ln_b, gn_t, fg, nb, tm, n_split):
    tl, d = x2.shape
    nt = tl // tm
    per_ex = nt // nb
    hv = d // HEADS
    nrow = mod.shape[0]

    def part(shared, a1_ref, z_ref, r_ref, mc_ref, mg_ref, of_ref, ob_ref, x_ref, t_ref,
             dp_ref, da1_ref, do_ref, gx_ref, mrg_ref, dmo_ref, yci_ref, dyc_ref, ogi_ref, dyg_ref, sm_ref):
        bidx, gate, lng, lnb, fgv, gn, wc_, wg_, wo_ = shared

        a1v = a1_ref[...]
        mu = jnp.mean(a1v, axis=-1, keepdims=True)
        xc = a1v - mu
        rs = lax.rsqrt(jnp.mean(xc * xc, axis=-1, keepdims=True) + EPS)
        xh = xc * rs
        a2 = xh * lng + lnb
        s2 = _sigmoid(a2)
        a3 = a2 * s2
        zv = z_ref[...]
        sz = _sigmoid(zv)
        siluz = zv * sz
        ycin = a3 * siluz
        yconv = _mm(ycin, wc_)

        o = of_ref[...] + ob_ref[...]
        ohat_parts, rn_parts = [], []
        for h in range(HEADS):
            oh = o[:, h * hv:(h + 1) * hv]
            rn = lax.rsqrt(jnp.mean(oh * oh, axis=-1, keepdims=True) + EPS)
            ohat_parts.append(oh * rn)
            rn_parts.append(rn)
        ohat = jnp.concatenate(ohat_parts, axis=1)
        on = ohat * gn
        rv = r_ref[...]
        sr = _sigmoid(rv)
        silur = rv * sr
        ogin = on * silur
        ygla = _mm(ogin, wg_)

        sc = _sigmoid(mc_ref[...])
        sg = _sigmoid(mg_ref[...])
        merged = sc * yconv + sg * ygla
        mo = _mm(merged, wo_)
        hn = x_ref[...] + gate * mo
        rf = lax.rsqrt(jnp.mean(hn * hn, axis=-1, keepdims=True) + EPS)
        yh = hn * rf
        err = yh * fgv - t_ref[...]
        loss_part = 0.5 * jnp.sum(err * err) * (1.0 / d)

        dy = err * (1.0 / d)
        dfg = jnp.sum(dy * yh, axis=0, keepdims=True)
        dyh = dy * fgv
        dhn = rf * (dyh - yh * jnp.mean(dyh * yh, axis=-1, keepdims=True))
        gx_ref[...] = dhn
        dgate = jnp.sum(dhn * mo, axis=0, keepdims=True)
        dmo = gate * dhn
        dmerged = _mm_nt(dmo, wo_)
        dyconv = dmerged * sc
        dygla = dmerged * sg
        dp_ref[:, 2 * d:3 * d] = (dmerged * yconv * sc * (1.0 - sc)).astype(BF16)
        dp_ref[:, 3 * d:4 * d] = (dmerged * ygla * sg * (1.0 - sg)).astype(BF16)
        dycin = _mm_nt(dyconv, wc_)
        dogin = _mm_nt(dygla, wg_)
        mrg_ref[...] = merged.astype(BF16)
        dmo_ref[...] = dmo.astype(BF16)
        yci_ref[...] = ycin.astype(BF16)
        dyc_ref[...] = dyconv.astype(BF16)
        ogi_ref[...] = ogin.astype(BF16)
        dyg_ref[...] = dygla.astype(BF16)

        da3 = dycin * siluz
        dp_ref[:, 0:d] = (dycin * a3 * _dsilu(zv, sz)).astype(BF16)
        da2 = da3 * _dsilu(a2, s2)
        dlng = jnp.sum(da2 * xh, axis=0, keepdims=True)
        dlnb = jnp.sum(da2, axis=0, keepdims=True)
        dxh = da2 * lng
        da1_ref[...] = rs * (dxh - jnp.mean(dxh, axis=-1, keepdims=True)
                             - xh * jnp.mean(dxh * xh, axis=-1, keepdims=True))

        don = dogin * silur
        dp_ref[:, d:2 * d] = (dogin * on * _dsilu(rv, sr)).astype(BF16)
        dgn = jnp.sum(don * ohat, axis=0, keepdims=True)
        dyn = don * gn
        for h in range(HEADS):
            vs = slice(h * hv, (h + 1) * hv)
            oh_hat = ohat_parts[h]
            dh = dyn[:, vs]
            do_ref[:, vs] = (rn_parts[h] * (dh - oh_hat * jnp.mean(dh * oh_hat, axis=-1, keepdims=True))
                             ).astype(BF16)

        sm_ref[0:1, :] += dfg
        sm_ref[1:2, :] += dlng
        sm_ref[2:3, :] += dlnb
        sm_ref[3:4, :] += dgn
        sm_ref[4:5, :] += jnp.zeros((1, d), F32) + loss_part
        for b in range(nb):
            sm_ref[8 + b:9 + b, :] += jnp.where(bidx == b, dgate, 0.0)

    def body(*refs):
        mod_ref, wc_ref, wg_ref, wo_ref, lng_ref, lnb_ref, gn_ref, fg_ref = refs[9:17]
        sm_ref = refs[27]
        i = pl.program_id(0)

        @pl.when(i == 0)
        def _():
            sm_ref[...] = jnp.zeros_like(sm_ref)

        bidx = i // per_ex
        shared = (bidx, _rowsel(mod_ref[...], bidx, nb)[:, 2 * d:3 * d], lng_ref[...], lnb_ref[...], fg_ref[...],
                  jnp.concatenate([gn_ref[...]] * HEADS, axis=1), wc_ref[...], wg_ref[...], wo_ref[...])
        rows_per = tm // n_split
        for p in range(n_split):
            rows = pl.ds(p * rows_per, rows_per)
            part(shared, *[r.at[rows] for r in refs[0:9]], *[r.at[rows] for r in refs[17:27]], sm_ref)

    row = pl.BlockSpec((tm, d), lambda i: (i, 0))
    pcol = lambda blk: pl.BlockSpec((tm, d), lambda i: (i, blk))
    full = lambda arr: pl.BlockSpec(arr.shape, lambda i: (0,) * arr.ndim)
    bfo = jax.ShapeDtypeStruct((tl, d), BF16)
    f32o = jax.ShapeDtypeStruct((tl, d), F32)
    return pl.pallas_call(
        body, name="tail", grid=(nt,),
        in_specs=[row, pcol(2), pcol(3), pcol(4), pcol(5), row, row, row, row, full(mod), full(wc), full(wg),
                  full(wo), full(ln_g), full(ln_b), full(gn_t), full(fg)],
        out_specs=(pl.BlockSpec((tm, 4 * d), lambda i: (i, 0)), row, row, row, row, row, row, row, row, row,
                   pl.BlockSpec((16, d), lambda i: (0, 0))),
        out_shape=(jax.ShapeDtypeStruct((tl, 4 * d), BF16), f32o, bfo, f32o, bfo, bfo, bfo, bfo, bfo, bfo,
                   jax.ShapeDtypeStruct((16, d), F32)),
        compiler_params=_params())(a1, pa, pa, pa, pa, o_f, o_b, x2, tgt, mod, wc, wg, wo, ln_g, ln_b, gn_t, fg)


def _local_step(x, c, ctx, tgt, c_ctx, ada_w8, ada_b, norm_g, w_a, b_a, w_b, b_b, conv_w8, conv_b, ln_g, ln_b,
                up2, bias2, gla_norm_g, final_norm_g, proj, on_grads=None, on_du_a1=None):
    nb, s_len, d = x.shape
    c_len = ctx.shape[1]
    dk_, dv_ = d // 2, d
    tl, tc = nb * s_len, nb * c_len
    nbw = 2 * dk_ + dv_ + LANE
    tm = math.gcd(256, c_len)
    tiles = _Tiles(nb, s_len, c_len, tm, 2)
    l_len = tiles.rows_per_ex
    t_all = nb * l_len
    x2, ctx2, tgt2 = x.reshape(tl, d), ctx.reshape(tc, d), tgt.reshape(tl, d)

    cv = jnp.zeros((8, d), F32).at[0:nb].set(c).at[nb].set(c_ctx.reshape(d))
    mod = _ada_fwd(cv, ada_w8, ada_b)
    u = _norm_fwd(x2, ctx2, mod, norm_g, tiles)
    u3 = u.reshape(nb, l_len, d)
    tma = math.gcd(1024, s_len)
    pa = _matmul_bias("inproj_a", u3, w_a, b_a, s_len, tma, _tile(6 * d, 2048))
    tmb = math.gcd(1024, t_all)
    pb, pv, g_all = _inproj_b(u, w_b, b_b, up2, bias2, tmb, dk_, dv_)

    a1 = _conv_fwd(pa, conv_w8, conv_b, nb, s_len)
    lr_blk = (2 * dk_) // LANE
    pb3, pv3 = pb.reshape(nb, l_len, 2 * dk_ + LANE), pv.reshape(nb, l_len, dv_)
    o_f, zs_f, b_f, o_b, zs_b, b_b2 = _gla_fwd(pb3, pv3, g_all.reshape(nb, l_len, 2 * dk_), nb, s_len, c_len,
                                               dk_, dv_)

    conv_proj, gla_proj, w_out = proj(a1) if callable(proj) else proj
    tt = math.gcd(256, s_len)
    (dp_a2, da1, d_o, gx1, merged, dmo, ycin, dyconv, ogin, dygla, small) = _tail(
        a1, pa, o_f.reshape(tl, dv_), o_b.reshape(tl, dv_), x2, tgt2, mod, conv_proj, gla_proj, w_out, ln_g, ln_b,
        gla_norm_g, final_norm_g, nb, tt, 2)

    lat3 = lambda a: a.reshape(nb, s_len, a.shape[-1])
    tnw = _tile(d, 512)
    d_w_out, _ = _matmul_tn_whole("dw_out", lat3(merged), lat3(dmo), s_len, tnw, False)
    d_conv_proj, _ = _matmul_tn_whole("dw_conv_proj", lat3(ycin), lat3(dyconv), s_len, tnw, False)
    d_gla_proj, _ = _matmul_tn_whole("dw_gla_proj", lat3(ogin), lat3(dygla), s_len, tnw, False)

    dp_a1, d_conv_w8, d_conv_b = _conv_bwd(pa, da1, conv_w8, nb, s_len)
    gl = _gla_bwd(pb3, pv3, d_o.reshape(nb, s_len, dv_), (zs_f, b_f, zs_b, b_b2), nb, s_len, c_len, dk_, dv_)
    gl = [g_.reshape(t_all, g_.shape[-1]) for g_ in gl]
    dp_b, d_up2, d_bias2 = _decay_bwd(pb, up2, bias2, gl[0:4], gl[4:8], tiles, lr_blk, dk_, dv_)

    dw_a1, db_a1 = _matmul_tn_whole("dw_a1", u3, lat3(dp_a1), s_len, tnw, True)
    dw_a2, db_a2 = _matmul_tn_whole("dw_a2", u3, lat3(dp_a2), s_len, tnw, True)
    dw_b, db_b = _matmul_tn("dw_b", u, dp_b, t_all, tmb, nbw)
    grads = dict(w_a1=dw_a1, w_a2=dw_a2, w_b=dw_b, conv_w8=d_conv_w8, conv_proj=d_conv_proj, up2=d_up2,
                 gla_proj=d_gla_proj, w_out=d_w_out)

    tka = _tile(2 * d, 2048)
    du_a1 = _matmul_nt("du_a1", dp_a1, w_a, 0, tma, tka, after=on_grads(grads) if on_grads else ())
    du_a2 = _matmul_nt("du_a2", dp_a2, w_a, (2 * d) // tka, tma, tka, after=on_du_a1(du_a1) if on_du_a1 else ())
    du_b = _matmul_nt("du_b", dp_b, w_b, 0, tmb, nbw)
    grad_x2, dmod_ss, d_norm_g = _norm_bwd(x2, ctx2, mod, norm_g, [du_a1, du_a2], du_b, gx1, tiles)
    ada_sv, ada_dmod, d_ada_b, d_cv = _ada_bwd(cv, ada_w8, dmod_ss, small)

    return dict(
        grads, grad_x=grad_x2.reshape(nb, s_len, d), small=small, cv=d_cv, ada_sv=ada_sv, ada_dmod=ada_dmod,
        ada_b=d_ada_b,
        norm_g=d_norm_g, b_a1=db_a1, b_a2=db_a2, b_b=db_b, conv_b=d_conv_b, bias2=d_bias2)


def _regroup_pieces(d, r, wshard):
    cb = d // N_DEV
    segs = []
    for j in range(N_DEV):
        segs.append((j * cb, cb, 0, 2 * j * cb))
    for j in range(N_DEV):
        segs.append((d + j * cb, cb, 0, (2 * j + 1) * cb))
    segs += [(2 * d, d, 0, 2 * d), (3 * d, 2 * d + 2 * r, 1, 0), (5 * d + 2 * r, 3 * d, 0, 3 * d)]
    pieces = []
    for o0, w, dst, d0 in segs:
        lo = o0
        while lo < o0 + w:
            j = lo // wshard
            hi = min(o0 + w, (j + 1) * wshard)
            pieces.append((j, lo - j * wshard, hi - lo, dst, d0 + lo - o0))
            lo = hi
    return pieces


def _regroup(o, d, r):
    n_in = 8 * d + 2 * r
    parts = ([], [])
    for _, s0, n, dst, _ in sorted(_regroup_pieces(d, r, n_in), key=lambda p: (p[3], p[4])):
        parts[dst].append(o[..., s0:s0 + n])
    pad = jnp.zeros(o.shape[:-1] + (LANE - 2 * r,), o.dtype)
    return jnp.concatenate(parts[0], axis=-1), jnp.concatenate(parts[1] + [pad], axis=-1)


def _unshard_w_in(g_win, d, r, after=()):
    n_sh, _, ws = g_win.shape
    nbw = 2 * d + LANE
    pieces = _regroup_pieces(d, r, ws)
    tr = math.gcd(d, 256)

    def body(g_ref, *rest):
        a_ref, b_ref = rest[len(after):]
        dsts = (a_ref, b_ref)
        for j, s0, n, dst, d0 in pieces:
            dsts[dst][:, pl.ds(d0, n)] = g_ref[j, :, pl.ds(s0, n)]
        b_ref[:, pl.ds(2 * d + 2 * r, LANE - 2 * r)] = jnp.zeros((tr, LANE - 2 * r), b_ref.dtype)

    return pl.pallas_call(
        body, name="unshard_w_in", grid=(d // tr,),
        in_specs=[pl.BlockSpec((n_sh, tr, ws), lambda i: (0, i, 0))] + [_ANY] * len(after),
        out_specs=(pl.BlockSpec((tr, 6 * d), lambda i: (i, 0)), pl.BlockSpec((tr, nbw), lambda i: (i, 0))),
        out_shape=(jax.ShapeDtypeStruct((d, 6 * d), g_win.dtype), jax.ShapeDtypeStruct((d, nbw), g_win.dtype)),
        compiler_params=_params())(g_win, *after)


def _reshard_w_in(dwt_a1, dwt_a2, dwt_b, d, r):
    ws = (8 * d + 2 * r) // N_DEV
    pieces = _regroup_pieces(d, r, ws)
    tc = math.gcd(d, 256)

    def body(a1_ref, a2_ref, b_ref, o_ref):
        for j, s0, n, dst, d0 in pieces:
            if dst == 1:
                src = b_ref[pl.ds(d0, n), :]
            elif d0 < 2 * d:
                src = a1_ref[pl.ds(d0, n), :]
            else:
                src = a2_ref[pl.ds(d0 - 2 * d, n), :]
            o_ref[j, pl.ds(s0, n), :] = src

    col = lambda h: pl.BlockSpec((h, tc), lambda i: (0, i))
    return pl.pallas_call(
        body, name="reshard_w_in", grid=(d // tc,),
        in_specs=[col(2 * d), col(4 * d), col(2 * d + LANE)],
        out_specs=pl.BlockSpec((N_DEV, ws, tc), lambda i: (0, 0, i)),
        out_shape=jax.ShapeDtypeStruct((N_DEV, ws, d), dwt_b.dtype),
        compiler_params=_params())(dwt_a1, dwt_a2, dwt_b)


_SMALL = ("c_ctx", "ada_b", "norm_g", "b_in", "conv_b", "conv_ln_g", "conv_ln_b", "decay_bias_fwd",
          "decay_bias_bwd", "gla_norm_g", "final_norm_g")


def _small_layout(d, r):
    sizes = dict(c_ctx=d, ada_b=3 * d, norm_g=d, b_in=8 * d + 2 * r, conv_b=d, conv_ln_g=d, conv_ln_b=d,
                 decay_bias_fwd=d // 2, decay_bias_bwd=d // 2, gla_norm_g=d // HEADS, final_norm_g=d, loss=1)
    table, off = {}, 0
    for name in _SMALL + ("loss",):
        table[name] = (off, sizes[name])
        off += -(-sizes[name] // LANE) * LANE
    return table, off


def _pack_small(g, nb, d, r):
    table, width = _small_layout(d, r)
    hv = d // HEADS
    pieces = _regroup_pieces(d, r, 8 * d + 2 * r)
    names = ("small", "cv", "ada_b", "norm_g", "b_a1", "b_a2", "b_b", "conv_b", "bias2")

    def body(sm, cv, ab, ng, ba1, ba2, bb, cvb, b2, o_ref):
        o_ref[...] = jnp.zeros_like(o_ref)

        def put(name, val):
            off, n = table[name]
            o_ref[:, pl.ds(off, n)] = val

        put("c_ctx", cv[nb:nb + 1, :])
        put("ada_b", ab[...])
        put("norm_g", ng[...])
        off_b = table["b_in"][0]
        for _, s0, n, dst, d0 in pieces:
            if dst == 1:
                src = bb[:, pl.ds(d0, n)]
            elif d0 < 2 * d:
                src = ba1[:, pl.ds(d0, n)]
            else:
                src = ba2[:, pl.ds(d0 - 2 * d, n)]
            o_ref[:, pl.ds(off_b + s0, n)] = src
        put("conv_b", cvb[...])
        put("conv_ln_g", sm[1:2, :])
        put("conv_ln_b", sm[2:3, :])
        put("decay_bias_fwd", b2[:, 0:d // 2])
        put("decay_bias_bwd", b2[:, d // 2:d])
        gn = sm[3:4, 0:hv]
        for h in range(1, HEADS):
            gn = gn + sm[3:4, h * hv:(h + 1) * hv]
        put("gla_norm_g", gn)
        put("final_norm_g", sm[0:1, :])
        put("loss", sm[4:5, 0:1])

    return pl.pallas_call(body, name="pack_small", out_shape=jax.ShapeDtypeStruct((1, width), F32),
                          compiler_params=_params())(*[g[k] for k in names])


def _small_adam(parts, ws, ms, vs, d, r):
    table, width = _small_layout(d, r)
    n_parts = parts.shape[0]
    k = len(_SMALL)

    def body(p_ref, *refs):
        w_refs, m_refs, v_refs = refs[0:k], refs[k:2 * k], refs[2 * k:3 * k]
        outs = refs[3 * k:]
        tot = p_ref[0]
        for i in range(1, n_parts):
            tot = tot + p_ref[i]
        for i, name in enumerate(_SMALL):
            off, n = table[name]
            g = tot[:, off:off + n]
            outs[i][...] = g
            outs[k + i][...], outs[2 * k + i][...], outs[3 * k + i][...] = _adamw(
                g, w_refs[i][...], m_refs[i][...], v_refs[i][...])
        off, _ = table["loss"]
        outs[4 * k][...] = tot[:, off:off + 1]

    shapes = [jax.ShapeDtypeStruct(w.shape, F32) for w in ws]
    res = pl.pallas_call(body, name="small_adam", out_shape=tuple(shapes * 4 + [jax.ShapeDtypeStruct((1, 1), F32)]),
                         compiler_params=_params())(parts, *ws, *ms, *vs)
    return res[0:k], res[k:2 * k], res[2 * k:3 * k], res[3 * k:4 * k], res[4 * k]


def _mesh_pos():
    return lax.axis_index("x"), lax.axis_index("y"), lax.axis_index("c")


def _all_gather(arrs):
    n = len(arrs)
    ns = 9
    split = [a.ndim == 2 and a.shape[0] % 32 == 0 for a in arrs]

    def body(*refs):
        ins, outs = refs[:n], refs[n:2 * n]
        send_sems, recv_sems, local_sems = refs[2 * n:]
        x, y, c = _mesh_pos()
        me, sibling = (x, y, c), (x, y, 1 - c)
        xn, yn, dg = (1 - x, y, c), (x, 1 - y, c), (1 - x, 1 - y, c)
        other = lambda pos: (pos[0], pos[1], 1 - c)

        def slot(a, pos, half):
            ref = outs[a].at[4 * pos[0] + 2 * pos[1] + pos[2]]
            if half is None:
                return ref
            rows = arrs[a].shape[0] // 2
            return ref.at[pl.ds(half * rows, rows)]

        def copy(a, k, block, to, src=None, half=None):
            dst = slot(a, block, half)
            return pltpu.make_async_remote_copy(
                src_ref=dst if src is None else src, dst_ref=dst,
                send_sem=send_sems.at[ns * a + k], recv_sem=recv_sems.at[ns * a + k],
                device_id=to, device_id_type=MESH)

        h0 = lambda a: 0 if split[a] else None
        mine = [pltpu.make_async_copy(ins[a], slot(a, me, None), local_sems.at[a]) for a in range(n)]
        for cp in mine:
            cp.start()
        sent = []
        for a in range(n):
            sent += [copy(a, 0, me, sibling, src=ins[a]), copy(a, 1, me, xn, src=ins[a]),
                     copy(a, 2, me, yn, src=ins[a])]
        for cp in sent:
            cp.start()

        def pass_on(cp):
            cp.start()
            sent.append(cp)

        for a in range(n):
            copy(a, 1, xn, me).wait_recv()
            pass_on(copy(a, 3, xn, sibling))
            pass_on(copy(a, 4, xn, yn, half=h0(a)))
        for a in range(n):
            copy(a, 2, yn, me).wait_recv()
            pass_on(copy(a, 5, yn, sibling))
            if split[a]:
                pass_on(copy(a, 6, yn, xn, half=1))
        for a in range(n):
            copy(a, 4, dg, me, half=h0(a)).wait_recv()
            pass_on(copy(a, 7, dg, sibling, half=h0(a)))
            if split[a]:
                copy(a, 6, dg, me, half=1).wait_recv()
                pass_on(copy(a, 8, dg, sibling, half=1))
        for a in range(n):
            copy(a, 0, sibling, me).wait_recv()
            copy(a, 3, other(xn), me).wait_recv()
            copy(a, 5, other(yn), me).wait_recv()
            copy(a, 7, other(dg), me, half=h0(a)).wait_recv()
            if split[a]:
                copy(a, 8, other(dg), me, half=1).wait_recv()
        for cp in sent:
            cp.wait_send()
        for cp in mine:
            cp.wait()

    return pl.pallas_call(
        body, name="all_gather",
        out_shape=tuple(jax.ShapeDtypeStruct((N_DEV,) + a.shape, a.dtype) for a in arrs),
        in_specs=[_ANY] * n, out_specs=tuple([_ANY] * n),
        scratch_shapes=[pltpu.SemaphoreType.DMA((ns * n,)), pltpu.SemaphoreType.DMA((ns * n,)),
                        pltpu.SemaphoreType.DMA((n,))],
    )(*arrs)


def _elementwise_tile(r, cdim, cols=2 * LANE):
    if r % 8 == 0 and r > 256:
        return math.gcd(r, 256), cdim
    if r > 256 and cdim % cols == 0:
        return r, cols
    return r, cdim


def _pair_sum(name, mine, theirs):
    _, r, cdim = mine.shape
    tr, tc = _elementwise_tile(r, cdim)

    def body(c_ref, m_ref, t_ref, o_ref):
        o_ref[...] = (m_ref[...].astype(F32) + t_ref[...].astype(F32)).astype(o_ref.dtype)

    return pl.pallas_call(
        body, name=name,
        grid_spec=pltpu.PrefetchScalarGridSpec(
            num_scalar_prefetch=1, grid=(r // tr, cdim // tc),
            in_specs=[pl.BlockSpec((4, None, tr, tc), lambda i, j, c_ref: (0, c_ref[0], i, j)),
                      pl.BlockSpec((4, tr, tc), lambda i, j, c_ref: (0, i, j))],
            out_specs=pl.BlockSpec((4, tr, tc), lambda i, j, c_ref: (0, i, j))),
        out_shape=jax.ShapeDtypeStruct((4, r, cdim), mine.dtype),
        compiler_params=_params())(lax.axis_index("c").reshape(1), mine.reshape(4, 2, r, cdim), theirs)


def _pair_sum_small(mines, theirs):
    n = len(mines)

    def body(*refs):
        c = lax.axis_index("c")
        for i in range(n):
            m_ref, t_ref, o_ref = refs[i], refs[n + i], refs[2 * n + i]
            own = jnp.where(c == 0, m_ref[:, 0].astype(F32), m_ref[:, 1].astype(F32))
            o_ref[...] = (own + t_ref[...].astype(F32)).astype(o_ref.dtype)

    return pl.pallas_call(
        body, name="pair_sum_small_weights",
        out_shape=tuple(jax.ShapeDtypeStruct(t.shape, m.dtype) for m, t in zip(mines, theirs)),
        compiler_params=_params())(*[m.reshape((4, 2) + m.shape[1:]) for m in mines], *theirs)


_HBM = pl.BlockSpec(memory_space=pltpu.HBM)
_SEM = pl.BlockSpec(memory_space=pltpu.SEMAPHORE)


def _copies_start(name, srcs, lands, make_copies, n_sems):
    n, m = len(srcs), len(lands)

    def body(*refs):
        ins = refs[:n + m]
        send_sems, recv_sems = refs[n + m], refs[n + m + 1]
        for cp in make_copies(ins[:n], ins[n:], send_sems, recv_sems):
            cp.start()
        refs[-1][...] = jnp.zeros_like(refs[-1])

    res = pl.pallas_call(
        body, name=name,
        out_shape=(pltpu.SemaphoreType.DMA((n_sems,)), pltpu.SemaphoreType.DMA((n_sems,)),
                   *[pltpu.HBM(a.shape, a.dtype) for a in (*srcs, *lands)], jax.ShapeDtypeStruct((8, LANE), F32)),
        in_specs=[_HBM] * (n + m),
        out_specs=(_SEM, _SEM, *[_HBM] * (n + m), pl.BlockSpec(memory_space=pltpu.VMEM)),
        input_output_aliases={i: 2 + i for i in range(n + m)},
        compiler_params=pltpu.CompilerParams(has_side_effects=pltpu.SideEffectType.DATAFLOW_SIDE_EFFECTING),
    )(*[pltpu.with_memory_space_constraint(a, pltpu.HBM) for a in (*srcs, *lands)])
    return res[0], res[1], res[2:2 + n], res[2 + n:2 + n + m], res[-1]


def _copies_wait(name, started, after, make_copies):
    send_sems, recv_sems, srcs, lands, _ = started
    n, m = len(srcs), len(lands)

    def body(*refs):
        ins = refs[:n + m]
        for cp in make_copies(ins[:n], ins[n:], refs[n + m], refs[n + m + 1]):
            cp.wait_send()
            cp.wait_recv()

    res = pl.pallas_call(
        body, name=name,
        out_shape=tuple(pltpu.HBM(a.shape, a.dtype) for a in (*srcs, *lands)),
        in_specs=[_HBM] * (n + m) + [_SEM, _SEM] + [_ANY] * len(after),
        out_specs=tuple([_HBM] * (n + m)),
        input_output_aliases={i: i for i in range(n + m)},
        compiler_params=pltpu.CompilerParams(has_side_effects=pltpu.SideEffectType.DATAFLOW_SIDE_EFFECTING),
    )(*srcs, *lands, send_sems, recv_sems, *after)
    return res[:n], res[n:]


def _gather_copies(srcs, lands, send_sems, recv_sems):
    x, y, c = _mesh_pos()
    me_i = 4 * x + 2 * y + c
    copies = []
    for rel in range(1, N_DEV):
        peer = (1 - x if rel & 4 else x, 1 - y if rel & 2 else y, 1 - c if rel & 1 else c)
        for a in range(len(srcs)):
            copies.append(pltpu.make_async_remote_copy(
                src_ref=srcs[a], dst_ref=lands[a].at[me_i], send_sem=send_sems.at[7 * a + rel - 1],
                recv_sem=recv_sems.at[7 * a + rel - 1], device_id=peer, device_id_type=MESH))
    return copies


def _sibling_copies(srcs, lands, send_sems, recv_sems):
    x, y, c = _mesh_pos()
    return [pltpu.make_async_remote_copy(
        src_ref=srcs[a].at[2 * k + (1 - c)], dst_ref=lands[a].at[k], send_sem=send_sems.at[4 * a + k],
        recv_sem=recv_sems.at[4 * a + k], device_id=(x, y, 1 - c), device_id_type=MESH)
        for a in range(len(srcs)) for k in range(4)]


def _chip_copies(srcs, lands, send_sems, recv_sems):
    x, y, c = _mesh_pos()
    my_chip = 2 * x + y
    copies = []
    for rel in range(1, 4):
        px = 1 - x if rel & 2 else x
        py = 1 - y if rel & 1 else y
        for a in range(len(srcs)):
            copies.append(pltpu.make_async_remote_copy(
                src_ref=srcs[a].at[2 * px + py], dst_ref=lands[a].at[my_chip], send_sem=send_sems.at[3 * a + rel - 1],
                recv_sem=recv_sems.at[3 * a + rel - 1], device_id=(px, py, c), device_id_type=MESH))
    return copies


def _sum_adam(name, parts, w, m, v, own=None):
    unit_mid = w.ndim == 3
    _, r, cdim = parts.shape
    n_parts = parts.shape[0]
    tr, tc = _elementwise_tile(r, cdim, (4 if unit_mid else 2) * LANE)
    extra = [] if own is None else [own]

    def body(p_ref, *refs):
        w_ref, m_ref, v_ref, g_ref, d_ref, nm_ref, nv_ref = refs[len(extra):]
        if own is None:
            part = lambda k: p_ref[k].astype(F32)
        else:
            my_chip = 2 * lax.axis_index("x") + lax.axis_index("y")
            part = lambda k: jnp.where(my_chip == k, refs[0][...], p_ref[k]).astype(F32)
        g = part(0)
        for k in range(1, n_parts):
            g = g + part(k)
        if unit_mid:
            g = g.reshape(tr, 1, tc)
        g_ref[...] = g
        d_ref[...], nm_ref[...], nv_ref[...] = _adamw(g, w_ref[...], m_ref[...], v_ref[...])

    blk = (pl.BlockSpec((tr, 1, tc), lambda i, j: (i, 0, j)) if unit_mid
           else pl.BlockSpec((tr, tc), lambda i, j: (i, j)))
    o = jax.ShapeDtypeStruct(w.shape, F32)
    return pl.pallas_call(
        body, name=name, grid=(r // tr, cdim // tc),
        in_specs=[pl.BlockSpec((n_parts, tr, tc), lambda i, j: (0, i, j))]
        + [pl.BlockSpec((None, tr, tc), lambda i, j: (2 * lax.axis_index("x") + lax.axis_index("y"), i, j))] * len(extra)
        + [blk, blk, blk],
        out_specs=(blk, blk, blk, blk), out_shape=(o, o, o, o),
        compiler_params=_params())(parts, *extra, w, m, v)


def _sum_adam_small(items):
    n = len(items)

    def body(*refs):
        my_chip = 2 * lax.axis_index("x") + lax.axis_index("y")
        for i in range(n):
            p_ref, own_ref, w_ref, m_ref, v_ref = refs[5 * i:5 * i + 5]
            g_ref, d_ref, nm_ref, nv_ref = refs[5 * n + 4 * i:5 * n + 4 * i + 4]
            g = None
            for k in range(p_ref.shape[0]):
                part = jnp.where(my_chip == k, own_ref[k], p_ref[k]).astype(F32)
                g = part if g is None else g + part
            g_ref[...] = g
            d_ref[...], nm_ref[...], nv_ref[...] = _adamw(g, w_ref[...], m_ref[...], v_ref[...])

    out_shape = tuple(jax.ShapeDtypeStruct(it[2].shape, F32) for it in items for _ in range(4))
    res = pl.pallas_call(body, name="adam_small_weights", out_shape=out_shape,
                         compiler_params=_params())(*[a for it in items for a in it])
    return [res[4 * i:4 * i + 4] for i in range(n)]


_WEIGHTS = ("c_ctx", "ada_w", "ada_b", "norm_g", "w_in", "b_in", "conv_w", "conv_b", "conv_ln_g", "conv_ln_b",
            "conv_proj", "decay_up_fwd", "decay_bias_fwd", "decay_up_bwd", "decay_bias_bwd", "gla_norm_g",
            "gla_proj", "w_out", "final_norm_g")


def _as2d(a):
    if a.ndim == 1:
        return a.reshape(1, -1)
    return a.reshape(-1, a.shape[-1])


def kernel(x, c, ctx, c_ctx, ada_w, ada_b, norm_g, w_in, b_in, conv_w, conv_b, conv_ln_g, conv_ln_b, conv_proj, decay_up_fwd, decay_bias_fwd, decay_up_bwd, decay_bias_bwd, gla_norm_g, gla_proj, w_out, final_norm_g, loss_target, m_c_ctx, m_ada_w, m_ada_b, m_norm_g, m_w_in, m_b_in, m_conv_w, m_conv_b, m_conv_ln_g, m_conv_ln_b, m_conv_proj, m_decay_up_fwd, m_decay_bias_fwd, m_decay_up_bwd, m_decay_bias_bwd, m_gla_norm_g, m_gla_proj, m_w_out, m_final_norm_g, v_c_ctx, v_ada_w, v_ada_b, v_norm_g, v_w_in, v_b_in, v_conv_w, v_conv_b, v_conv_ln_g, v_conv_ln_b, v_conv_proj, v_decay_up_fwd, v_decay_bias_fwd, v_decay_up_bwd, v_decay_bias_bwd, v_gla_norm_g, v_gla_proj, v_w_out, v_final_norm_g):
    env = dict(locals())
    wts = {k: env[k] for k in _WEIGHTS}
    d = x.shape[-1]
    r = decay_up_fwd.shape[1]
    dk_ = d // 2

    ds, dks = d // N_DEV, dk_ // N_DEV
    g_win, g_ada, conv_w8, g_up = _all_gather(
        [w_in[0].astype(BF16), ada_w[0].astype(BF16), conv_w[0],
         jnp.concatenate([decay_up_fwd[0], decay_up_bwd[0]], axis=1)])
    proj_own = [conv_proj[0].astype(BF16), gla_proj[0].astype(BF16), w_out[0].astype(BF16)]
    me_i = 4 * lax.axis_index("x") + 2 * lax.axis_index("y") + lax.axis_index("c")
    proj_lands = [lax.dynamic_update_slice(lax.empty((N_DEV,) + a.shape, a.dtype), a[None], (me_i, 0, 0))
                  for a in proj_own]
    proj_start = _copies_start("proj_gather_start", proj_own, proj_lands, _gather_copies, 7 * 3)

    def proj(after):
        _, lands = _copies_wait("proj_gather_wait", proj_start, (after,), _gather_copies)
        return [w.reshape(d, d) for w in lands]

    w_a, w_b = _unshard_w_in(g_win, d, r, after=(proj_start[4],))
    up_f = g_up[:, :, 0:dks].transpose(1, 0, 2).reshape(r, dk_)
    up_b = g_up[:, :, dks:].transpose(1, 0, 2).reshape(r, dk_)
    up2 = jnp.zeros((LANE, 2 * dk_), F32).at[0:r, 0:dk_].set(up_f).at[r:2 * r, dk_:].set(up_b)
    bias2 = jnp.concatenate([decay_bias_fwd, decay_bias_bwd], axis=1)
    b_a, b_b = _regroup(b_in, d, r)

    comm = {}

    def on_grads(gr):
        d_up = jnp.concatenate([gr["up2"][0:r, 0:dk_].reshape(r, N_DEV, dks).transpose(1, 0, 2),
                                gr["up2"][r:2 * r, dk_:].reshape(r, N_DEV, dks).transpose(1, 0, 2)], axis=2)
        mine = [_reshard_w_in(gr["w_a1"], gr["w_a2"], gr["w_b"], d, r), gr["conv_proj"].reshape(N_DEV, ds, d),
                gr["gla_proj"].reshape(N_DEV, ds, d), gr["w_out"].reshape(N_DEV, ds, d), gr["conv_w8"], d_up]
        lands = [lax.empty((4,) + a.shape[1:], a.dtype) for a in mine]
        comm["sibling"] = _copies_start("grad_sibling_start", mine, lands, _sibling_copies, 4 * len(mine))
        return (comm["sibling"][4],)

    def on_du_a1(du_a1):
        mine, theirs = _copies_wait("grad_sibling_wait", comm["sibling"], (du_a1,), _sibling_copies)
        sums = [_pair_sum("pair_sum_w_in", mine[0], theirs[0])] + list(_pair_sum_small(mine[1:], theirs[1:]))
        lands = [lax.empty(a.shape, a.dtype) for a in sums]
        comm["chips"] = _copies_start("grad_chips_start", sums, lands, _chip_copies, 3 * len(sums))
        return (comm["chips"][4],)

    g = _local_step(x, c, ctx, loss_target, c_ctx, g_ada, ada_b, norm_g[0:1], w_a, b_a, w_b, b_b,
                    conv_w8, conv_b, conv_ln_g, conv_ln_b, up2, bias2, gla_norm_g, final_norm_g.reshape(1, d),
                    proj, on_grads, on_du_a1)

    small_mine = [_pack_small(g, x.shape[0], d, r), g["ada_sv"], g["ada_dmod"]]
    small_lands = [lax.dynamic_update_slice(lax.empty((N_DEV,) + a.shape, F32), a[None], (me_i, 0, 0))
                   for a in small_mine]
    small_start = _copies_start("small_gather_start", small_mine, small_lands, _gather_copies, 7 * len(small_mine))
    own, landed = _copies_wait("grad_chips_wait", comm["chips"], (small_start[4],), _chip_copies)
    o_win, o_cp, o_gp, o_wo, o_cw, o_up = own
    x_win, x_cp, x_gp, x_wo, x_cw, x_up = landed

    out = {}

    as_rows = lambda a: jnp.transpose(a, (2, 0, 1))
    res = _sum_adam("adam_w_in", x_win, as_rows(w_in), as_rows(m_w_in), as_rows(v_w_in), o_win)
    for pre, arr in zip(("grad_", "delta_", "new_m_", "new_v_"), res):
        out[pre + "w_in"] = jnp.transpose(arr, (1, 2, 0))
    small_w = (("conv_proj", x_cp, o_cp), ("gla_proj", x_gp, o_gp), ("w_out", x_wo, o_wo), ("conv_w", x_cw, o_cw),
               ("decay_up_fwd", x_up[:, :, 0:dks], o_up[:, :, 0:dks]),
               ("decay_up_bwd", x_up[:, :, dks:], o_up[:, :, dks:]))
    small_res = _sum_adam_small([(p, o, _as2d(wts[k]), _as2d(env["m_" + k]), _as2d(env["v_" + k]))
                                 for k, p, o in small_w])
    for (k, _, _), arrs in zip(small_w, small_res):
        for pre, arr in zip(("grad_", "delta_", "new_m_", "new_v_"), arrs):
            out[pre + k] = arr.reshape(wts[k].shape)

    _, (packs, sv_all, dmod_all) = _copies_wait("small_gather_wait", small_start, (res[0], out["grad_w_out"]),
                                                _gather_copies)
    ada_res = _ada_adam(sv_all, dmod_all, _as2d(ada_w), _as2d(m_ada_w), _as2d(v_ada_w), x.shape[0] + 1)
    for pre, arr in zip(("grad_", "delta_", "new_m_", "new_v_"), ada_res):
        out[pre + "ada_w"] = arr.reshape(ada_w.shape)
    row = lambda a: a.reshape(1, -1)
    sg, sd, sm, sv, loss = _small_adam(packs, [row(wts[k]) for k in _SMALL], [row(env["m_" + k]) for k in _SMALL],
                                       [row(env["v_" + k]) for k in _SMALL], d, r)
    for i, k in enumerate(_SMALL):
        for pre, arrs in (("grad_", sg), ("delta_", sd), ("new_m_", sm), ("new_v_", sv)):
            out[pre + k] = arrs[i].reshape(wts[k].shape)
    loss = loss.reshape(())

    return (loss, g["grad_x"], *[out["grad_" + k] for k in _WEIGHTS], *[out["delta_" + k] for k in _WEIGHTS],
            *[out["new_m_" + k] for k in _WEIGHTS], *[out["new_v_" + k] for k in _WEIGHTS])
```

```python
import functools
import math

import jax
import jax.numpy as jnp
from jax import lax
from jax.experimental import pallas as pl
from jax.experimental.pallas import tpu as pltpu

F32 = jnp.float32
BF16 = jnp.bfloat16
MESH = pl.DeviceIdType.MESH

N_DEV = 8
GRID_W = 64
CHUNK = 128
HEADS = 4
EPS = 1e-6
GATE_TAU = 16.0
LANE = 128
ADAM_LR, ADAM_B1, ADAM_B2, ADAM_EPS, ADAM_WD, ADAM_STEP = 0.001, 0.9, 0.999, 1e-08, 0.01, 10
VMEM_LIMIT = 60 * 1024 * 1024
_ANY = pl.BlockSpec(memory_space=pl.ANY)


def _params(**kw):
    return pltpu.CompilerParams(vmem_limit_bytes=VMEM_LIMIT, **kw)


def _tile(n, pref):
    t = (min(pref, n) // LANE) * LANE
    while t >= LANE:
        if n % t == 0:
            return t
        t -= LANE
    return n


def _mm(a, b):
    return jnp.dot(a.astype(BF16), b.astype(BF16), preferred_element_type=F32)


def _mm_nt(a, b):
    return lax.dot_general(a.astype(BF16), b.astype(BF16), (((1,), (1,)), ((), ())), preferred_element_type=F32)


def _mm_tn(a, b):
    return lax.dot_general(a.astype(BF16), b.astype(BF16), (((0,), (0,)), ((), ())), preferred_element_type=F32)


def _sigmoid(x):
    return 0.5 * jnp.tanh(0.5 * x) + 0.5


def _dsilu(x, s):
    return s * (1.0 + x * (1.0 - s))


def _adamw(g, w, m, v):
    bc1 = 1.0 - ADAM_B1 ** ADAM_STEP
    bc2 = 1.0 - ADAM_B2 ** ADAM_STEP
    mn = ADAM_B1 * m + (1.0 - ADAM_B1) * g
    vn = ADAM_B2 * v + (1.0 - ADAM_B2) * (g * g)
    delta = -ADAM_LR * ((mn / bc1) / (jnp.sqrt(vn / bc2) + ADAM_EPS) + ADAM_WD * w)
    return delta, mn, vn


def _rowsel(table, idx, n):
    out = table[0:1, :]
    for r in range(1, n):
        out = jnp.where(idx == r, table[r:r + 1, :], out)
    return out


def _ada_fwd(cv, ada_w8, ada_b):
    n_sh, _, ws = ada_w8.shape

    def body(cv_ref, w_ref, b_ref, o_ref):
        c = cv_ref[...]
        sv = c * _sigmoid(c)
        for j in range(n_sh):
            cols = pl.ds(j * ws, ws)
            o_ref[:, cols] = _mm(sv, w_ref[j]) + b_ref[:, cols]

    return pl.pallas_call(body, name="ada_fwd", out_shape=jax.ShapeDtypeStruct((cv.shape[0], n_sh * ws), F32),
                          compiler_params=_params())(cv, ada_w8, ada_b)


def _ada_bwd(cv, ada_w8, dmod_ss, small):
    n_sh, d, ws = ada_w8.shape

    def body(cv_ref, w_ref, dm_ref, sm_ref, sv_ref, dmod_ref, db_ref, dc_ref):
        c = cv_ref[...]
        s = _sigmoid(c)
        sv_ref[...] = c * s
        dm = jnp.concatenate([dm_ref[:, 0:2 * d], sm_ref[8:16, :]], axis=1)
        dmod_ref[...] = dm
        db_ref[...] = jnp.sum(dm, axis=0, keepdims=True)
        dsv = None
        for j in range(n_sh):
            part = _mm_nt(dm[:, j * ws:(j + 1) * ws], w_ref[j])
            dsv = part if dsv is None else dsv + part
        dc_ref[...] = dsv * _dsilu(c, s)

    return pl.pallas_call(
        body, name="ada_bwd",
        out_shape=(jax.ShapeDtypeStruct(cv.shape, F32), jax.ShapeDtypeStruct(dmod_ss.shape, F32),
                   jax.ShapeDtypeStruct((1, n_sh * ws), F32), jax.ShapeDtypeStruct(cv.shape, F32)),
        compiler_params=_params())(cv, ada_w8, dmod_ss, small)


def _ada_adam(sv_all, dmod_all, w, m, v, n_terms):
    n_dev, rows, d = sv_all.shape
    ws = w.shape[1]

    tr = _tile(d, 256)

    def body(i_ref, sv_ref, dm_ref, w_ref, m_ref, v_ref, g_ref, d_ref, nm_ref, nv_ref):
        sv_t = jnp.transpose(sv_ref[...].reshape(n_dev * rows, tr))
        dm = dm_ref[...]
        blk = 64
        for r0 in range(0, tr, blk):
            g = None
            for k in range(n_dev):
                for row in range(n_terms):
                    col = k * rows + row
                    term = sv_t[r0:r0 + blk, col:col + 1] * dm[k, row:row + 1, :]
                    g = term if g is None else g + term
            sl = pl.ds(r0, blk)
            g_ref[sl, :] = g
            d_ref[sl, :], nm_ref[sl, :], nv_ref[sl, :] = _adamw(g, w_ref[sl, :], m_ref[sl, :], v_ref[sl, :])

    blk_w = pl.BlockSpec((tr, ws), lambda i, i_ref: (i, 0))
    o = jax.ShapeDtypeStruct(w.shape, F32)
    me_i = 4 * lax.axis_index("x") + 2 * lax.axis_index("y") + lax.axis_index("c")
    return pl.pallas_call(
        body, name="adam_ada_w",
        grid_spec=pltpu.PrefetchScalarGridSpec(
            num_scalar_prefetch=1, grid=(d // tr,),
            in_specs=[pl.BlockSpec((n_dev, rows, tr), lambda i, i_ref: (0, 0, i)),
                      pl.BlockSpec((n_dev, rows, ws), lambda i, i_ref: (0, 0, i_ref[0])),
                      blk_w, blk_w, blk_w],
            out_specs=(blk_w,) * 4),
        out_shape=(o, o, o, o), compiler_params=_params())(me_i.reshape(1), sv_all, dmod_all, w, m, v)


class _Tiles:
    def __init__(self, nb, s_len, c_len, tm, big):
        self.nb, self.tm, self.big = nb, tm, big
        self.lat, self.ctx = s_len // tm, c_len // tm
        self.pad = -(self.lat + self.ctx) % big
        self.per_ex = self.lat + self.ctx + self.pad
        self.n_all = nb * self.per_ex
        self.rows_per_ex = self.per_ex * tm

    def is_lat(self, i):
        return i % self.per_ex < self.lat

    def is_pad(self, i):
        return i % self.per_ex >= self.lat + self.ctx

    def lat_of_all(self, i):
        return (i // self.per_ex) * self.lat + jnp.minimum(i % self.per_ex, self.lat - 1)

    def ctx_of_all(self, i):
        return (i // self.per_ex) * self.ctx + jnp.clip(i % self.per_ex - self.lat, 0, self.ctx - 1)


def _norm_fwd(x2, ctx2, mod, norm_g, tiles):
    tl, d = x2.shape
    tc = ctx2.shape[0]
    nb, tm = tiles.nb, tiles.tm

    def body(x_ref, c_ref, mod_ref, g_ref, u_ref):
        i = pl.program_id(0)
        lat = tiles.is_lat(i)
        xv = jnp.where(lat, x_ref[...], c_ref[...])
        row = jnp.where(lat, i // tiles.per_ex, nb)
        m = _rowsel(mod_ref[...], row, nb + 1)
        shift, scale = m[:, 0:d], m[:, d:2 * d]
        rstd = lax.rsqrt(jnp.mean(xv * xv, axis=-1, keepdims=True) + EPS)
        u = xv * rstd * g_ref[...] * (1.0 + scale) + shift
        u_ref[...] = jnp.where(tiles.is_pad(i), 0.0, u).astype(BF16)

    return pl.pallas_call(
        body, name="norm_fwd", grid=(tiles.n_all,),
        in_specs=[pl.BlockSpec((tm, d), lambda i: (tiles.lat_of_all(i), 0)),
                  pl.BlockSpec((tm, d), lambda i: (tiles.ctx_of_all(i), 0)),
                  pl.BlockSpec(mod.shape, lambda i: (0, 0)),
                  pl.BlockSpec((1, d), lambda i: (0, 0))],
        out_specs=pl.BlockSpec((tm, d), lambda i: (i, 0)),
        out_shape=jax.ShapeDtypeStruct((tiles.n_all * tm, d), BF16),
        compiler_params=_params())(x2, ctx2, mod, norm_g)


def _norm_bwd(x2, ctx2, mod, norm_g, du_lat, du_b, gx1, tiles):
    tl, d = x2.shape
    nb, tm = tiles.nb, tiles.tm
    nrow = mod.shape[0]
    n_lat_in = len(du_lat)

    def body(x_ref, c_ref, mod_ref, g_ref, *refs):
        dl_refs = refs[:n_lat_in]
        d3_ref, gx_ref, gxo_ref, dmod_ref, dg_ref = refs[n_lat_in:]
        i = pl.program_id(0)

        @pl.when(i == 0)
        def _():
            dmod_ref[...] = jnp.zeros_like(dmod_ref)
            dg_ref[...] = jnp.zeros_like(dg_ref)

        lat = tiles.is_lat(i)
        xv = jnp.where(lat, x_ref[...], c_ref[...])
        row = jnp.where(lat, i // tiles.per_ex, nb)
        m = _rowsel(mod_ref[...], row, nb + 1)
        scale = m[:, d:2 * d]
        g = g_ref[...]
        dl = dl_refs[0][...].astype(F32)
        for ref in dl_refs[1:]:
            dl = dl + ref[...].astype(F32)
        du = jnp.where(tiles.is_pad(i), 0.0, d3_ref[...].astype(F32) + jnp.where(lat, dl, 0.0))
        rstd = lax.rsqrt(jnp.mean(xv * xv, axis=-1, keepdims=True) + EPS)
        xh = xv * rstd
        dshift = jnp.sum(du, axis=0, keepdims=True)
        dscale = jnp.sum(du * xh * g, axis=0, keepdims=True)
        dxn = du * (1.0 + scale)
        dg_ref[...] += jnp.sum(dxn * xh, axis=0, keepdims=True)
        dxh = dxn * g
        dx = rstd * (dxh - xh * jnp.mean(dxh * xh, axis=-1, keepdims=True))

        @pl.when(lat)
        def _():
            gxo_ref[...] = dx + gx_ref[...]

        for r in range(nb + 1):
            dmod_ref[r:r + 1, 0:d] += jnp.where(row == r, dshift, 0.0)
            dmod_ref[r:r + 1, d:2 * d] += jnp.where(row == r, dscale, 0.0)

    lat_map = lambda i: (tiles.lat_of_all(i), 0)
    lat_spec = pl.BlockSpec((tm, d), lat_map)
    return pl.pallas_call(
        body, name="norm_bwd", grid=(tiles.n_all,),
        in_specs=[lat_spec,
                  pl.BlockSpec((tm, d), lambda i: (tiles.ctx_of_all(i), 0)),
                  pl.BlockSpec(mod.shape, lambda i: (0, 0)),
                  pl.BlockSpec((1, d), lambda i: (0, 0))]
                 + [lat_spec] * n_lat_in
                 + [pl.BlockSpec((tm, d), lambda i: (i, 0)), lat_spec],
        out_specs=(lat_spec,
                   pl.BlockSpec((nrow, 3 * d), lambda i: (0, 0)),
                   pl.BlockSpec((1, d), lambda i: (0, 0))),
        out_shape=(jax.ShapeDtypeStruct((tl, d), F32), jax.ShapeDtypeStruct((nrow, 3 * d), F32),
                   jax.ShapeDtypeStruct((1, d), F32)),
        compiler_params=_params())(x2, ctx2, mod, norm_g, *du_lat, du_b, gx1)


def _matmul_bias(name, u3, w, b, s_len, tm, tn):
    nb = u3.shape[0]
    d, n = w.shape
    per = s_len // tm
    rows = nb * s_len

    def body(u_ref, w_ref, b_ref, o_ref):
        o_ref[...] = jnp.dot(u_ref[...], w_ref[...], preferred_element_type=F32) + b_ref[...]

    return pl.pallas_call(
        body, name=name, grid=(n // tn, rows // tm),
        in_specs=[pl.BlockSpec((None, tm, d), lambda j, i: (i // per, i % per, 0)),
                  pl.BlockSpec((d, tn), lambda j, i: (0, j)),
                  pl.BlockSpec((1, tn), lambda j, i: (0, j))],
        out_specs=pl.BlockSpec((tm, tn), lambda j, i: (i, j)),
        out_shape=jax.ShapeDtypeStruct((rows, n), F32),
        compiler_params=_params())(u3, w, b)


def _log_sigmoid(x):
    return jnp.minimum(x, 0.0) - jnp.log(1.0 + jnp.exp(-jnp.abs(x)))


def _inproj_b(u, w_b, b_b, up2, bias2, tm, dk_, dv_):
    t_all, d = u.shape
    nbw = w_b.shape[1]
    n2 = up2.shape[1]

    def body(u_ref, w_ref, b_ref, up_ref, bias_ref, qk_ref, v_ref, g_ref):
        full = jnp.dot(u_ref[...], w_ref[...], preferred_element_type=F32) + b_ref[...]
        lr = full[:, 2 * dk_ + dv_:nbw]
        qk_ref[:, 0:2 * dk_] = full[:, 0:2 * dk_]
        qk_ref[:, 2 * dk_:2 * dk_ + LANE] = lr
        v_ref[...] = full[:, 2 * dk_:2 * dk_ + dv_].astype(BF16)
        g_ref[...] = _log_sigmoid(_mm(lr, up_ref[...]) + bias_ref[...]) * (1.0 / GATE_TAU)

    whole = lambda a: pl.BlockSpec(a.shape, lambda i: (0, 0))
    return pl.pallas_call(
        body, name="inproj_b", grid=(t_all // tm,),
        in_specs=[pl.BlockSpec((tm, d), lambda i: (i, 0)), whole(w_b), whole(b_b), whole(up2), whole(bias2)],
        out_specs=(pl.BlockSpec((tm, 2 * dk_ + LANE), lambda i: (i, 0)), pl.BlockSpec((tm, dv_), lambda i: (i, 0)),
                   pl.BlockSpec((tm, n2), lambda i: (i, 0))),
        out_shape=(jax.ShapeDtypeStruct((t_all, 2 * dk_ + LANE), F32), jax.ShapeDtypeStruct((t_all, dv_), BF16),
                   jax.ShapeDtypeStruct((t_all, n2), F32)),
        compiler_params=_params())(u, w_b, b_b, up2, bias2)


def _matmul_nt(name, a, w, koff, tm, tk, after=()):
    r, kc = a.shape
    d = w.shape[0]
    nk = kc // tk

    def body(a_ref, w_ref, *rest):
        o_ref = rest[len(after)]
        k = pl.program_id(1)
        p = lax.dot_general(a_ref[...], w_ref[...], (((1,), (1,)), ((), ())), preferred_element_type=F32)
        if nk == 1:
            o_ref[...] = p.astype(o_ref.dtype)
            return
        acc_ref = rest[len(after) + 1]

        @pl.when(k == 0)
        def _():
            acc_ref[...] = p

        @pl.when(k > 0)
        def _():
            acc_ref[...] += p

        @pl.when(k == nk - 1)
        def _():
            o_ref[...] = acc_ref[...].astype(o_ref.dtype)

    return pl.pallas_call(
        body, name=name, grid=(r // tm, nk),
        in_specs=[pl.BlockSpec((tm, tk), lambda i, k: (i, k)),
                  pl.BlockSpec((d, tk), lambda i, k: (0, koff + k))] + [_ANY] * len(after),
        out_specs=pl.BlockSpec((tm, d), lambda i, k: (i, 0)),
        out_shape=jax.ShapeDtypeStruct((r, d), BF16),
        scratch_shapes=[pltpu.VMEM((tm, d), F32)] if nk > 1 else [],
        compiler_params=_params())(a, w, *after)


def _matmul_tn(name, a, b, rows, tk, tn):
    m = a.shape[1]
    n = b.shape[1]
    nk = rows // tk

    def body(a_ref, b_ref, o_ref, s_ref, acc_ref):
        k = pl.program_id(1)
        bv = b_ref[...]
        p = lax.dot_general(bv, a_ref[...], (((0,), (0,)), ((), ())), preferred_element_type=F32)
        cs = jnp.sum(bv.astype(F32), axis=0, keepdims=True)

        @pl.when(k == 0)
        def _():
            acc_ref[...] = p
            s_ref[...] = cs

        @pl.when(k > 0)
        def _():
            acc_ref[...] += p
            s_ref[...] += cs

        @pl.when(k == nk - 1)
        def _():
            o_ref[...] = acc_ref[...].astype(o_ref.dtype)

    return pl.pallas_call(
        body, name=name, grid=(n // tn, nk),
        in_specs=[pl.BlockSpec((tk, m), lambda j, k: (k, 0)),
                  pl.BlockSpec((tk, tn), lambda j, k: (k, j))],
        out_specs=(pl.BlockSpec((tn, m), lambda j, k: (j, 0)), pl.BlockSpec((1, tn), lambda j, k: (0, j))),
        out_shape=(jax.ShapeDtypeStruct((n, m), BF16), jax.ShapeDtypeStruct((1, n), F32)),
        scratch_shapes=[pltpu.VMEM((tn, m), F32)],
        compiler_params=_params())(a, b)


def _matmul_tn_whole(name, a3, b3, rows, tn, transposed):
    nb, _, m = a3.shape
    n = b3.shape[2]

    def body(a_ref, b_ref, o_ref, s_ref):
        p, cs = None, None
        for e in range(nb):
            bv = b_ref[e]
            lhs, rhs = (bv, a_ref[e]) if transposed else (a_ref[e], bv)
            pe = lax.dot_general(lhs, rhs, (((0,), (0,)), ((), ())), preferred_element_type=F32)
            ce = jnp.sum(bv.astype(F32), axis=0, keepdims=True)
            p, cs = (pe, ce) if p is None else (p + pe, cs + ce)
        o_ref[...] = p.astype(o_ref.dtype)
        s_ref[...] = cs

    o_spec, o_shape = ((pl.BlockSpec((tn, m), lambda j: (j, 0)), (n, m)) if transposed
                       else (pl.BlockSpec((m, tn), lambda j: (0, j)), (m, n)))
    return pl.pallas_call(
        body, name=name, grid=(n // tn,),
        in_specs=[pl.BlockSpec((nb, rows, m), lambda j: (0, 0, 0)),
                  pl.BlockSpec((nb, rows, tn), lambda j: (0, 0, j))],
        out_specs=(o_spec, pl.BlockSpec((1, tn), lambda j: (0, j))),
        out_shape=(jax.ShapeDtypeStruct(o_shape, BF16), jax.ShapeDtypeStruct((1, n), F32)),
        compiler_params=_params())(a3, b3)


def _conv_window(pad_ref, r, shift, ktaps, width, horizontal):
    if horizontal:
        return pad_ref[r, pl.ds(16 + shift, width), :]
    return pad_ref[r + ktaps // 2 + shift]


def _conv_row(pad_ref, w, r, ktaps, width, horizontal, flip):
    half = ktaps // 2
    acc = None
    for t in range(ktaps):
        win = _conv_window(pad_ref, r, (half - t) if flip else (t - half), ktaps, width, horizontal)
        term = win * w[t:t + 1, :]
        acc = term if acc is None else acc + term
    return acc


def _fill_padded(ref, val, rows, width, ktaps, horizontal):
    half_k = ktaps // 2
    cb = val.shape[-1]
    if horizontal:
        ref[:, 0:16, :] = jnp.zeros((rows, 16, cb), F32)
        ref[:, 16 + width:32 + width, :] = jnp.zeros((rows, 16, cb), F32)
        ref[:, 16:16 + width, :] = val
    else:
        ref[0:half_k, :, :] = jnp.zeros((half_k, width, cb), F32)
        ref[half_k + rows:2 * half_k + rows, :, :] = jnp.zeros((half_k, width, cb), F32)
        ref[half_k:half_k + rows, :, :] = val


def _conv_fwd(pa, conv_w8, conv_b, nb, s):
    nblk, ktaps, cb = conv_w8.shape
    d = nblk * cb
    rows, width = s // GRID_W, GRID_W
    half_k = ktaps // 2
    nh = nblk // 2

    def body(glu_ref, w_ref, b_ref, o_ref, ph_ref, pv_ref):
        j = pl.program_id(1)
        a0 = (glu_ref[:, 0:cb] * _sigmoid(glu_ref[:, cb:2 * cb])).reshape(rows, width, cb)
        w = w_ref[...]

        bias = b_ref[...]

        def run(pad_ref, horizontal):
            _fill_padded(pad_ref, a0, rows, width, ktaps, horizontal)

            def row(r, carry):
                at = pl.ds(pl.multiple_of(r * width, width), width)
                o_ref[at, :] = _conv_row(pad_ref, w, r, ktaps, width, horizontal, False) + bias
                return carry

            lax.fori_loop(0, rows, row, 0)

        @pl.when(j < nh)
        def _():
            run(ph_ref, True)

        @pl.when(j >= nh)
        def _():
            run(pv_ref, False)

    return pl.pallas_call(
        body, name="conv_fwd", grid=(nb, nblk),
        in_specs=[pl.BlockSpec((s, 2 * cb), lambda b, j: (b, j)),
                  pl.BlockSpec((None, ktaps, cb), lambda b, j: (j, 0, 0)),
                  pl.BlockSpec((1, cb), lambda b, j: (0, j))],
        out_specs=pl.BlockSpec((s, cb), lambda b, j: (b, j)),
        out_shape=jax.ShapeDtypeStruct((nb * s, d), F32),
        scratch_shapes=[pltpu.VMEM((rows, width + 32, cb), F32), pltpu.VMEM((rows + 2 * half_k, width, cb), F32)],
        compiler_params=_params())(pa, conv_w8, conv_b)


def _conv_bwd(pa, da1, conv_w8, nb, s):
    nblk, ktaps, cb = conv_w8.shape
    d = nblk * cb
    rows, width = s // GRID_W, GRID_W
    half_k = ktaps // 2
    nh = nblk // 2

    def body(glu_ref, da_ref, w_ref, dp_ref, dw_ref, db_ref, pha_ref, phd_ref, pva_ref, pvd_ref):
        j = pl.program_id(0)
        b = pl.program_id(1)
        a0 = (glu_ref[:, 0:cb] * _sigmoid(glu_ref[:, cb:2 * cb])).reshape(rows, width, cb)
        da1v = da_ref[...]
        d3 = da1v.reshape(rows, width, cb)
        w = w_ref[...]

        @pl.when(b == 0)
        def _():
            dw_ref[...] = jnp.zeros_like(dw_ref)
            db_ref[...] = jnp.zeros_like(db_ref)

        db_ref[...] += jnp.sum(da1v, axis=0, keepdims=True)

        def run(pa_ref, pd_ref, horizontal):
            _fill_padded(pa_ref, a0, rows, width, ktaps, horizontal)
            _fill_padded(pd_ref, d3, rows, width, ktaps, horizontal)

            def row(r, accs):
                at = pl.ds(pl.multiple_of(r * width, width), width)
                da0 = _conv_row(pd_ref, w, r, ktaps, width, horizontal, True)
                gv = glu_ref[at, 0:cb]
                sg = _sigmoid(glu_ref[at, cb:2 * cb])
                dp_ref[at, 0:cb] = (da0 * sg).astype(BF16)
                dp_ref[at, cb:2 * cb] = (da0 * gv * sg * (1.0 - sg)).astype(BF16)
                d_row = da_ref[at, :]
                out = []
                for t in range(ktaps):
                    prod = _conv_window(pa_ref, r, t - half_k, ktaps, width, horizontal) * d_row
                    out.append(accs[t] + jnp.sum(prod.reshape(width // 8, 8, cb), axis=0))
                return tuple(out)

            accs = lax.fori_loop(0, rows, row, tuple(jnp.zeros((8, cb), F32) for _ in range(ktaps)))
            for t in range(ktaps):
                dw_ref[t:t + 1, :] += jnp.sum(accs[t], axis=0, keepdims=True)

        @pl.when(j < nh)
        def _():
            run(pha_ref, phd_ref, True)

        @pl.when(j >= nh)
        def _():
            run(pva_ref, pvd_ref, False)

    return pl.pallas_call(
        body, name="conv_bwd", grid=(nblk, nb),
        in_specs=[pl.BlockSpec((s, 2 * cb), lambda j, b: (b, j)),
                  pl.BlockSpec((s, cb), lambda j, b: (b, j)),
                  pl.BlockSpec((None, ktaps, cb), lambda j, b: (j, 0, 0))],
        out_specs=(pl.BlockSpec((s, 2 * cb), lambda j, b: (b, j)),
                   pl.BlockSpec((None, ktaps, cb), lambda j, b: (j, 0, 0)),
                   pl.BlockSpec((1, cb), lambda j, b: (0, j))),
        out_shape=(jax.ShapeDtypeStruct((nb * s, 2 * d), BF16),
                   jax.ShapeDtypeStruct((nblk, ktaps, cb), F32), jax.ShapeDtypeStruct((1, d), F32)),
        scratch_shapes=[pltpu.VMEM((rows, width + 32, cb), F32), pltpu.VMEM((rows, width + 32, cb), F32),
                        pltpu.VMEM((rows + 2 * half_k, width, cb), F32),
                        pltpu.VMEM((rows + 2 * half_k, width, cb), F32)],
        compiler_params=_params())(pa, da1, conv_w8)


def _decay_bwd(pb, up2, bias2, grads_f, grads_b, tiles, lr_blk, dk_, dv_):
    t_all = pb.shape[0]
    tm = tiles.tm
    n2 = up2.shape[1]
    nbw = 2 * dk_ + dv_ + LANE

    def body(lr_ref, up_ref, b_ref, dqf, dkf, dvf, dgf, dqb, dkb, dvb, dgb, dp_ref, dup_ref, dbias_ref):
        i = pl.program_id(0)
        pad = tiles.is_pad(i)
        live = lambda v: jnp.where(pad, 0.0, v)

        @pl.when(i == 0)
        def _():
            dup_ref[...] = jnp.zeros_like(dup_ref)
            dbias_ref[...] = jnp.zeros_like(dbias_ref)

        lr = lr_ref[...]
        up = up_ref[...]
        logits = _mm(lr, up) + b_ref[...]
        dg = live(jnp.concatenate([dgf[...], dgb[...]], axis=1))
        dlog = dg * (1.0 / GATE_TAU) * _sigmoid(-logits)
        dup_ref[...] += _mm_tn(lr, dlog)
        dbias_ref[...] += jnp.sum(dlog, axis=0, keepdims=True)
        both = lambda f, b: live(f[...].astype(F32) + b[...].astype(F32)).astype(BF16)
        dp_ref[:, 0:dk_] = both(dqf, dqb)
        dp_ref[:, dk_:2 * dk_] = both(dkf, dkb)
        dp_ref[:, 2 * dk_:2 * dk_ + dv_] = both(dvf, dvb)
        dp_ref[:, 2 * dk_ + dv_:nbw] = _mm_nt(dlog, up).astype(BF16)

    row = lambda w: pl.BlockSpec((tm, w), lambda i: (i, 0))
    return pl.pallas_call(
        body, name="decay_bwd", grid=(t_all // tm,),
        in_specs=[pl.BlockSpec((tm, LANE), lambda i: (i, lr_blk)),
                  pl.BlockSpec(up2.shape, lambda i: (0, 0)),
                  pl.BlockSpec((1, n2), lambda i: (0, 0)),
                  row(dk_), row(dk_), row(dv_), row(dk_), row(dk_), row(dk_), row(dv_), row(dk_)],
        out_specs=(row(nbw), pl.BlockSpec(up2.shape, lambda i: (0, 0)), pl.BlockSpec((1, n2), lambda i: (0, 0))),
        out_shape=(jax.ShapeDtypeStruct((t_all, nbw), BF16), jax.ShapeDtypeStruct(up2.shape, F32),
                   jax.ShapeDtypeStruct((1, n2), F32)),
        compiler_params=_params())(pb, up2, bias2, *grads_f, *grads_b)


def _scan_chunk(s, nl, nc, rev):
    if rev:
        return jnp.where(s < nc, nl + (nc - 1 - s), nl - 1 - (s - nc))
    return jnp.where(s < nc, nl + s, s - nc)


def _scan_lat_chunk(s, nl, nc, rev):
    first = nl - 1 if rev else 0
    return jnp.where(s < nc, first, _scan_chunk(s, nl, nc, rev))


def _tri_mm(m_bf, x):
    hi = x.astype(BF16)
    r1 = x - hi.astype(F32)
    mid = r1.astype(BF16)
    lo = (r1 - mid.astype(F32)).astype(BF16)
    dot = lambda p: jnp.dot(m_bf, p, preferred_element_type=F32)
    return dot(hi) + dot(mid) + dot(lo)


def _chunk_masks(c, rev):
    ii = lax.broadcasted_iota(jnp.int32, (c, c), 0)
    jj = lax.broadcasted_iota(jnp.int32, (c, c), 1)
    return ((ii <= jj), (ii >= jj)) if rev else ((ii >= jj), (ii <= jj))


def _chunk_terms(q, k, b, far, mid):
    bf, bm = b[far:far + 1, :], b[mid:mid + 1, :]
    e = jnp.exp(b)
    em = jnp.exp(b - bm)
    eim = jnp.exp(bm - b)
    ed = jnp.exp(bf - b)
    return dict(e=e, em=em, eim=eim, ed=ed, dec=jnp.exp(bf), qe=q * e, qem=q * em, kim=k * eim, kd=k * ed)


def _gla_fwd(pb3, pv3, g3, nb, s_len, c_len, dk_, dv_):
    c = CHUNK
    nl, nc = s_len // c, c_len // c
    ns = nl + nc
    hk, hv = dk_ // HEADS, dv_ // HEADS
    l_len = pb3.shape[1]
    scale = hk ** -0.5
    mid = c // 2

    def body(*refs):
        ins, outs, z_scr = refs[:8], refs[8:14], refs[14]
        s = pl.program_id(0)

        @pl.when(s == 0)
        def _():
            z_scr[...] = jnp.zeros_like(z_scr)

        qs = jnp.where(s >= nc, scale, 0.0)
        for di, rev in enumerate((False, True)):
            q_ref, k_ref, v_ref, g_ref = ins[4 * di:4 * di + 4]
            o_ref, zs_ref, b_ref = outs[3 * di:3 * di + 3]
            mask, _ = _chunk_masks(c, rev)
            m_bf = mask.astype(BF16)
            far = 0 if rev else c - 1
            for b in range(nb):
                bc = _tri_mm(m_bf, g_ref[b])
                b_ref[b] = bc
                for h in range(HEADS):
                    ks, vs = slice(h * hk, (h + 1) * hk), slice(h * hv, (h + 1) * hv)
                    zi = (di * nb + b) * HEADS + h
                    v = v_ref[b, :, vs]
                    t = _chunk_terms(q_ref[b, :, ks] * qs, k_ref[b, :, ks], bc[:, ks], far, mid)
                    a = jnp.where(mask, _mm_nt(t["qem"], t["kim"]), 0.0)
                    z = z_scr[zi]
                    zs_ref[0, b * HEADS + h] = z
                    o_ref[b, :, vs] = _mm(a, v) + _mm_nt(t["qe"], z)
                    z_scr[zi] = z * t["dec"] + _mm_tn(v, t["kd"])

    in_specs, out_specs, out_shape = [], [], []
    for di, rev in enumerate((False, True)):
        ch = functools.partial(_scan_chunk, nl=nl, nc=nc, rev=rev)
        lch = functools.partial(_scan_lat_chunk, nl=nl, nc=nc, rev=rev)
        in_specs += [pl.BlockSpec((nb, c, dk_), lambda s, ch=ch: (0, ch(s), 0)),
                     pl.BlockSpec((nb, c, dk_), lambda s, ch=ch: (0, ch(s), 1)),
                     pl.BlockSpec((nb, c, dv_), lambda s, ch=ch: (0, ch(s), 0)),
                     pl.BlockSpec((nb, c, dk_), lambda s, ch=ch, di=di: (0, ch(s), di))]
        out_specs += [pl.BlockSpec((nb, c, dv_), lambda s, lch=lch: (0, lch(s), 0)),
                      pl.BlockSpec((1, nb * HEADS, hv, hk), lambda s: (s, 0, 0, 0)),
                      pl.BlockSpec((nb, c, dk_), lambda s, ch=ch: (0, ch(s), 0))]
        out_shape += [jax.ShapeDtypeStruct((nb, s_len, dv_), F32),
                      jax.ShapeDtypeStruct((ns, nb * HEADS, hv, hk), F32),
                      jax.ShapeDtypeStruct((nb, l_len, dk_), F32)]
    return pl.pallas_call(
        body, name="gla_fwd", grid=(ns,), in_specs=in_specs, out_specs=tuple(out_specs), out_shape=tuple(out_shape),
        scratch_shapes=[pltpu.VMEM((2 * nb * HEADS, hv, hk), F32)],
        compiler_params=_params())(pb3, pb3, pv3, g3, pb3, pb3, pv3, g3)


def _gla_bwd(pb3, pv3, do3, fwd_saved, nb, s_len, c_len, dk_, dv_):
    c = CHUNK
    nl, nc = s_len // c, c_len // c
    ns = nl + nc
    hk, hv = dk_ // HEADS, dv_ // HEADS
    l_len = pb3.shape[1]
    scale = hk ** -0.5
    mid = c // 2
    zs_f, b_f, zs_b, b_b = fwd_saved

    def body(*refs):
        ins, outs, dz_scr = refs[:12], refs[12:20], refs[20]
        s = pl.program_id(0)
        step = ns - 1 - s

        @pl.when(s == 0)
        def _():
            dz_scr[...] = jnp.zeros_like(dz_scr)

        lat = step >= nc
        qs = jnp.where(lat, scale, 0.0)
        dmul = jnp.where(lat, 1.0, 0.0)
        for di, rev in enumerate((False, True)):
            q_ref, k_ref, v_ref, b_ref, do_ref, zs_ref = ins[6 * di:6 * di + 6]
            dq_ref, dk_ref, dv_ref, dg_ref = outs[4 * di:4 * di + 4]
            mask, mask_t = _chunk_masks(c, rev)
            mt_bf = mask_t.astype(BF16)
            far = 0 if rev else c - 1
            far_row = lax.broadcasted_iota(jnp.int32, (c, hk), 0) == far
            for b in range(nb):
                db_parts = []
                for h in range(HEADS):
                    ks, vs = slice(h * hk, (h + 1) * hk), slice(h * hv, (h + 1) * hv)
                    zi = (di * nb + b) * HEADS + h
                    v = v_ref[b, :, vs]
                    d_o = do_ref[b, :, vs] * dmul
                    t = _chunk_terms(q_ref[b, :, ks] * qs, k_ref[b, :, ks], b_ref[b, :, ks], far, mid)
                    qem, kim, qe, kd = t["qem"], t["kim"], t["qe"], t["kd"]
                    a_t = jnp.where(mask_t, _mm_nt(kim, qem), 0.0)
                    d_a = jnp.where(mask, _mm_nt(d_o, v), 0.0)
                    d_at = jnp.where(mask_t, _mm_nt(v, d_o), 0.0)
                    z = zs_ref[0, b * HEADS + h]
                    dzn = dz_scr[zi]
                    dv_ref[b, :, vs] = (_mm(a_t, d_o) + _mm_nt(kd, dzn)).astype(dv_ref.dtype)
                    dqem = _mm(d_a, kim)
                    dkim = _mm(d_at, qem)
                    dqe = _mm(d_o, z)
                    dkd = _mm(v, dzn)
                    ddec = jnp.sum(z * dzn, axis=0, keepdims=True)
                    dz_scr[zi] = dzn * t["dec"] + _mm_tn(d_o, qe)
                    dq_ref[b, :, ks] = ((dqem * t["em"] + dqe * t["e"]) * qs).astype(dq_ref.dtype)
                    dk_ref[b, :, ks] = (dkim * t["eim"] + dkd * t["ed"]).astype(dk_ref.dtype)
                    db = dqem * qem - dkim * kim + dqe * qe - dkd * kd
                    extra = jnp.sum(dkd * kd, axis=0, keepdims=True) + ddec * t["dec"]
                    db_parts.append(db + jnp.where(far_row, extra, 0.0))
                dg_ref[b] = _tri_mm(mt_bf, jnp.concatenate(db_parts, axis=1))

    in_specs, out_specs, out_shape, args = [], [], [], []
    for di, rev in enumerate((False, True)):
        ch = lambda s, rev=rev: _scan_chunk(ns - 1 - s, nl, nc, rev)
        lch = lambda s, rev=rev: _scan_lat_chunk(ns - 1 - s, nl, nc, rev)
        in_specs += [pl.BlockSpec((nb, c, dk_), lambda s, ch=ch: (0, ch(s), 0)),
                     pl.BlockSpec((nb, c, dk_), lambda s, ch=ch: (0, ch(s), 1)),
                     pl.BlockSpec((nb, c, dv_), lambda s, ch=ch: (0, ch(s), 0)),
                     pl.BlockSpec((nb, c, dk_), lambda s, ch=ch: (0, ch(s), 0)),
                     pl.BlockSpec((nb, c, dv_), lambda s, lch=lch: (0, lch(s), 0)),
                     pl.BlockSpec((1, nb * HEADS, hv, hk), lambda s: (ns - 1 - s, 0, 0, 0))]
        args += [pb3, pb3, pv3, (b_b if rev else b_f), do3, (zs_b if rev else zs_f)]
        for w, dt in ((dk_, BF16), (dk_, BF16), (dv_, BF16), (dk_, F32)):
            out_specs.append(pl.BlockSpec((nb, c, w), lambda s, ch=ch: (0, ch(s), 0)))
            out_shape.append(jax.ShapeDtypeStruct((nb, l_len, w), dt))
    return pl.pallas_call(
        body, name="gla_bwd", grid=(ns,), in_specs=in_specs, out_specs=tuple(out_specs), out_shape=tuple(out_shape),
        scratch_shapes=[pltpu.VMEM((2 * nb * HEADS, hv, hk), F32)],
        compiler_params=_params())(*args)


def _tail(a1, pa, o_f, o_b, x2, tgt, mod, wc, wg, wo, ln_g, ln_b, gn_t, fg, nb, tm, n_split):
    tl, d = x2.shape
    nt = tl // tm
    per_ex = nt // nb
    hv = d // HEADS
    nrow = mod.shape[0]

    def part(shared, a1_ref, z_ref, r_ref, mc_ref, mg_ref, of_ref, ob_ref, x_ref, t_ref,
             dp_ref, da1_ref, do_ref, gx_ref, mrg_ref, dmo_ref, yci_ref, dyc_ref, ogi_ref, dyg_ref, sm_ref):
        bidx, gate, lng, lnb, fgv, gn, wc_, wg_, wo_ = shared

        a1v = a1_ref[...]
        mu = jnp.mean(a1v, axis=-1, keepdims=True)
        xc = a1v - mu
        rs = lax.rsqrt(jnp.mean(xc * xc, axis=-1, keepdims=True) + EPS)
        xh = xc * rs
        a2 = xh * lng + lnb
        s2 = _sigmoid(a2)
        a3 = a2 * s2
        zv = z_ref[...]
        sz = _sigmoid(zv)
        siluz = zv * sz
        ycin = a3 * siluz
        yconv = _mm(ycin, wc_)

        o = of_ref[...] + ob_ref[...]
        ohat_parts, rn_parts = [], []
        for h in range(HEADS):
            oh = o[:, h * hv:(h + 1) * hv]
            rn = lax.rsqrt(jnp.mean(oh * oh, axis=-1, keepdims=True) + EPS)
            ohat_parts.append(oh * rn)
            rn_parts.append(rn)
        ohat = jnp.concatenate(ohat_parts, axis=1)
        on = ohat * gn
        rv = r_ref[...]
        sr = _sigmoid(rv)
        silur = rv * sr
        ogin = on * silur
        ygla = _mm(ogin, wg_)

        sc = _sigmoid(mc_ref[...])
        sg = _sigmoid(mg_ref[...])
        merged = sc * yconv + sg * ygla
        mo = _mm(merged, wo_)
        hn = x_ref[...] + gate * mo
        rf = lax.rsqrt(jnp.mean(hn * hn, axis=-1, keepdims=True) + EPS)
        yh = hn * rf
        err = yh * fgv - t_ref[...]
        loss_part = 0.5 * jnp.sum(err * err) * (1.0 / d)

        dy = err * (1.0 / d)
        dfg = jnp.sum(dy * yh, axis=0, keepdims=True)
        dyh = dy * fgv
        dhn = rf * (dyh - yh * jnp.mean(dyh * yh, axis=-1, keepdims=True))
        gx_ref[...] = dhn
        dgate = jnp.sum(dhn * mo, axis=0, keepdims=True)
        dmo = gate * dhn
        dmerged = _mm_nt(dmo, wo_)
        dyconv = dmerged * sc
        dygla = dmerged * sg
        dp_ref[:, 2 * d:3 * d] = (dmerged * yconv * sc * (1.0 - sc)).astype(BF16)
        dp_ref[:, 3 * d:4 * d] = (dmerged * ygla * sg * (1.0 - sg)).astype(BF16)
        dycin = _mm_nt(dyconv, wc_)
        dogin = _mm_nt(dygla, wg_)
        mrg_ref[...] = merged.astype(BF16)
        dmo_ref[...] = dmo.astype(BF16)
        yci_ref[...] = ycin.astype(BF16)
        dyc_ref[...] = dyconv.astype(BF16)
        ogi_ref[...] = ogin.astype(BF16)
        dyg_ref[...] = dygla.astype(BF16)

        da3 = dycin * siluz
        dp_ref[:, 0:d] = (dycin * a3 * _dsilu(zv, sz)).astype(BF16)
        da2 = da3 * _dsilu(a2, s2)
        dlng = jnp.sum(da2 * xh, axis=0, keepdims=True)
        dlnb = jnp.sum(da2, axis=0, keepdims=True)
        dxh = da2 * lng
        da1_ref[...] = rs * (dxh - jnp.mean(dxh, axis=-1, keepdims=True)
                             - xh * jnp.mean(dxh * xh, axis=-1, keepdims=True))

        don = dogin * silur
        dp_ref[:, d:2 * d] = (dogin * on * _dsilu(rv, sr)).astype(BF16)
        dgn = jnp.sum(don * ohat, axis=0, keepdims=True)
        dyn = don * gn
        for h in range(HEADS):
            vs = slice(h * hv, (h + 1) * hv)
            oh_hat = ohat_parts[h]
            dh = dyn[:, vs]
            do_ref[:, vs] = (rn_parts[h] * (dh - oh_hat * jnp.mean(dh * oh_hat, axis=-1, keepdims=True))
                             ).astype(BF16)

        sm_ref[0:1, :] += dfg
        sm_ref[1:2, :] += dlng
        sm_ref[2:3, :] += dlnb
        sm_ref[3:4, :] += dgn
        sm_ref[4:5, :] += jnp.zeros((1, d), F32) + loss_part
        for b in range(nb):
            sm_ref[8 + b:9 + b, :] += jnp.where(bidx == b, dgate, 0.0)

    def body(*refs):
        mod_ref, wc_ref, wg_ref, wo_ref, lng_ref, lnb_ref, gn_ref, fg_ref = refs[9:17]
        sm_ref = refs[27]
        i = pl.program_id(0)

        @pl.when(i == 0)
        def _():
            sm_ref[...] = jnp.zeros_like(sm_ref)

        bidx = i // per_ex
        shared = (bidx, _rowsel(mod_ref[...], bidx, nb)[:, 2 * d:3 * d], lng_ref[...], lnb_ref[...], fg_ref[...],
                  jnp.concatenate([gn_ref[...]] * HEADS, axis=1), wc_ref[...], wg_ref[...], wo_ref[...])
        rows_per = tm // n_split
        for p in range(n_split):
            rows = pl.ds(p * rows_per, rows_per)
            part(shared, *[r.at[rows] for r in refs[0:9]], *[r.at[rows] for r in refs[17:27]], sm_ref)

    row = pl.BlockSpec((tm, d), lambda i: (i, 0))
    pcol = lambda blk: pl.BlockSpec((tm, d), lambda i: (i, blk))
    full = lambda arr: pl.BlockSpec(arr.shape, lambda i: (0,) * arr.ndim)
    bfo = jax.ShapeDtypeStruct((tl, d), BF16)
    f32o = jax.ShapeDtypeStruct((tl, d), F32)
    return pl.pallas_call(
        body, name="tail", grid=(nt,),
        in_specs=[row, pcol(2), pcol(3), pcol(4), pcol(5), row, row, row, row, full(mod), full(wc), full(wg),
                  full(wo), full(ln_g), full(ln_b), full(gn_t), full(fg)],
        out_specs=(pl.BlockSpec((tm, 4 * d), lambda i: (i, 0)), row, row, row, row, row, row, row, row, row,
                   pl.BlockSpec((16, d), lambda i: (0, 0))),
        out_shape=(jax.ShapeDtypeStruct((tl, 4 * d), BF16), f32o, bfo, f32o, bfo, bfo, bfo, bfo, bfo, bfo,
                   jax.ShapeDtypeStruct((16, d), F32)),
        compiler_params=_params())(a1, pa, pa, pa, pa, o_f, o_b, x2, tgt, mod, wc, wg, wo, ln_g, ln_b, gn_t, fg)


def _local_step(x, c, ctx, tgt, c_ctx, ada_w8, ada_b, norm_g, w_a, b_a, w_b, b_b, conv_w8, conv_b, ln_g, ln_b,
                up2, bias2, gla_norm_g, final_norm_g, proj, on_grads=None, on_du_a1=None):
    nb, s_len, d = x.shape
    c_len = ctx.shape[1]
    dk_, dv_ = d // 2, d
    tl, tc = nb * s_len, nb * c_len
    nbw = 2 * dk_ + dv_ + LANE
    tm = math.gcd(256, c_len)
    tiles = _Tiles(nb, s_len, c_len, tm, 2)
    l_len = tiles.rows_per_ex
    t_all = nb * l_len
    x2, ctx2, tgt2 = x.reshape(tl, d), ctx.reshape(tc, d), tgt.reshape(tl, d)

    cv = jnp.zeros((8, d), F32).at[0:nb].set(c).at[nb].set(c_ctx.reshape(d))
    mod = _ada_fwd(cv, ada_w8, ada_b)
    u = _norm_fwd(x2, ctx2, mod, norm_g, tiles)
    u3 = u.reshape(nb, l_len, d)
    tma = math.gcd(1024, s_len)
    pa = _matmul_bias("inproj_a", u3, w_a, b_a, s_len, tma, _tile(6 * d, 2048))
    tmb = math.gcd(1024, t_all)
    pb, pv, g_all = _inproj_b(u, w_b, b_b, up2, bias2, tmb, dk_, dv_)

    a1 = _conv_fwd(pa, conv_w8, conv_b, nb, s_len)
    lr_blk = (2 * dk_) // LANE
    pb3, pv3 = pb.reshape(nb, l_len, 2 * dk_ + LANE), pv.reshape(nb, l_len, dv_)
    o_f, zs_f, b_f, o_b, zs_b, b_b2 = _gla_fwd(pb3, pv3, g_all.reshape(nb, l_len, 2 * dk_), nb, s_len, c_len,
                                               dk_, dv_)

    conv_proj, gla_proj, w_out = proj(a1) if callable(proj) else proj
    tt = math.gcd(256, s_len)
    (dp_a2, da1, d_o, gx1, merged, dmo, ycin, dyconv, ogin, dygla, small) = _tail(
        a1, pa, o_f.reshape(tl, dv_), o_b.reshape(tl, dv_), x2, tgt2, mod, conv_proj, gla_proj, w_out, ln_g, ln_b,
        gla_norm_g, final_norm_g, nb, tt, 2)

    lat3 = lambda a: a.reshape(nb, s_len, a.shape[-1])
    tnw = _tile(d, 512)
    d_w_out, _ = _matmul_tn_whole("dw_out", lat3(merged), lat3(dmo), s_len, tnw, False)
    d_conv_proj, _ = _matmul_tn_whole("dw_conv_proj", lat3(ycin), lat3(dyconv), s_len, tnw, False)
    d_gla_proj, _ = _matmul_tn_whole("dw_gla_proj", lat3(ogin), lat3(dygla), s_len, tnw, False)

    dp_a1, d_conv_w8, d_conv_b = _conv_bwd(pa, da1, conv_w8, nb, s_len)
    gl = _gla_bwd(pb3, pv3, d_o.reshape(nb, s_len, dv_), (zs_f, b_f, zs_b, b_b2), nb, s_len, c_len, dk_, dv_)
    gl = [g_.reshape(t_all, g_.shape[-1]) for g_ in gl]
    dp_b, d_up2, d_bias2 = _decay_bwd(pb, up2, bias2, gl[0:4], gl[4:8], tiles, lr_blk, dk_, dv_)

    dw_a1, db_a1 = _matmul_tn_whole("dw_a1", u3, lat3(dp_a1), s_len, tnw, True)
    dw_a2, db_a2 = _matmul_tn_whole("dw_a2", u3, lat3(dp_a2), s_len, tnw, True)
    dw_b, db_b = _matmul_tn("dw_b", u, dp_b, t_all, tmb, nbw)
    grads = dict(w_a1=dw_a1, w_a2=dw_a2, w_b=dw_b, conv_w8=d_conv_w8, conv_proj=d_conv_proj, up2=d_up2,
                 gla_proj=d_gla_proj, w_out=d_w_out)

    tka = _tile(2 * d, 2048)
    du_a1 = _matmul_nt("du_a1", dp_a1, w_a, 0, tma, tka, after=on_grads(grads) if on_grads else ())
    du_a2 = _matmul_nt("du_a2", dp_a2, w_a, (2 * d) // tka, tma, tka, after=on_du_a1(du_a1) if on_du_a1 else ())
    du_b = _matmul_nt("du_b", dp_b, w_b, 0, tmb, nbw)
    grad_x2, dmod_ss, d_norm_g = _norm_bwd(x2, ctx2, mod, norm_g, [du_a1, du_a2], du_b, gx1, tiles)
    ada_sv, ada_dmod, d_ada_b, d_cv = _ada_bwd(cv, ada_w8, dmod_ss, small)

    return dict(
        grads, grad_x=grad_x2.reshape(nb, s_len, d), small=small, cv=d_cv, ada_sv=ada_sv, ada_dmod=ada_dmod,
        ada_b=d_ada_b,
        norm_g=d_norm_g, b_a1=db_a1, b_a2=db_a2, b_b=db_b, conv_b=d_conv_b, bias2=d_bias2)


def _regroup_pieces(d, r, wshard):
    cb = d // N_DEV
    segs = []
    for j in range(N_DEV):
        segs.append((j * cb, cb, 0, 2 * j * cb))
    for j in range(N_DEV):
        segs.append((d + j * cb, cb, 0, (2 * j + 1) * cb))
    segs += [(2 * d, d, 0, 2 * d), (3 * d, 2 * d + 2 * r, 1, 0), (5 * d + 2 * r, 3 * d, 0, 3 * d)]
    pieces = []
    for o0, w, dst, d0 in segs:
        lo = o0
        while lo < o0 + w:
            j = lo // wshard
            hi = min(o0 + w, (j + 1) * wshard)
            pieces.append((j, lo - j * wshard, hi - lo, dst, d0 + lo - o0))
            lo = hi
    return pieces


def _regroup(o, d, r):
    n_in = 8 * d + 2 * r
    parts = ([], [])
    for _, s0, n, dst, _ in sorted(_regroup_pieces(d, r, n_in), key=lambda p: (p[3], p[4])):
        parts[dst].append(o[..., s0:s0 + n])
    pad = jnp.zeros(o.shape[:-1] + (LANE - 2 * r,), o.dtype)
    return jnp.concatenate(parts[0], axis=-1), jnp.concatenate(parts[1] + [pad], axis=-1)


def _unshard_w_in(g_win, d, r, after=()):
    n_sh, _, ws = g_win.shape
    nbw = 2 * d + LANE
    pieces = _regroup_pieces(d, r, ws)
    tr = math.gcd(d, 256)

    def body(g_ref, *rest):
        a_ref, b_ref = rest[len(after):]
        dsts = (a_ref, b_ref)
        for j, s0, n, dst, d0 in pieces:
            dsts[dst][:, pl.ds(d0, n)] = g_ref[j, :, pl.ds(s0, n)]
        b_ref[:, pl.ds(2 * d + 2 * r, LANE - 2 * r)] = jnp.zeros((tr, LANE - 2 * r), b_ref.dtype)

    return pl.pallas_call(
        body, name="unshard_w_in", grid=(d // tr,),
        in_specs=[pl.BlockSpec((n_sh, tr, ws), lambda i: (0, i, 0))] + [_ANY] * len(after),
        out_specs=(pl.BlockSpec((tr, 6 * d), lambda i: (i, 0)), pl.BlockSpec((tr, nbw), lambda i: (i, 0))),
        out_shape=(jax.ShapeDtypeStruct((d, 6 * d), g_win.dtype), jax.ShapeDtypeStruct((d, nbw), g_win.dtype)),
        compiler_params=_params())(g_win, *after)


def _reshard_w_in(dwt_a1, dwt_a2, dwt_b, d, r):
    ws = (8 * d + 2 * r) // N_DEV
    pieces = _regroup_pieces(d, r, ws)
    tc = math.gcd(d, 256)

    def body(a1_ref, a2_ref, b_ref, o_ref):
        for j, s0, n, dst, d0 in pieces:
            if dst == 1:
                src = b_ref[pl.ds(d0, n), :]
            elif d0 < 2 * d:
                src = a1_ref[pl.ds(d0, n), :]
            else:
                src = a2_ref[pl.ds(d0 - 2 * d, n), :]
            o_ref[j, pl.ds(s0, n), :] = src

    col = lambda h: pl.BlockSpec((h, tc), lambda i: (0, i))
    return pl.pallas_call(
        body, name="reshard_w_in", grid=(d // tc,),
        in_specs=[col(2 * d), col(4 * d), col(2 * d + LANE)],
        out_specs=pl.BlockSpec((N_DEV, ws, tc), lambda i: (0, 0, i)),
        out_shape=jax.ShapeDtypeStruct((N_DEV, ws, d), dwt_b.dtype),
        compiler_params=_params())(dwt_a1, dwt_a2, dwt_b)


_SMALL = ("c_ctx", "ada_b", "norm_g", "b_in", "conv_b", "conv_ln_g", "conv_ln_b", "decay_bias_fwd",
          "decay_bias_bwd", "gla_norm_g", "final_norm_g")


def _small_layout(d, r):
    sizes = dict(c_ctx=d, ada_b=3 * d, norm_g=d, b_in=8 * d + 2 * r, conv_b=d, conv_ln_g=d, conv_ln_b=d,
                 decay_bias_fwd=d // 2, decay_bias_bwd=d // 2, gla_norm_g=d // HEADS, final_norm_g=d, loss=1)
    table, off = {}, 0
    for name in _SMALL + ("loss",):
        table[name] = (off, sizes[name])
        off += -(-sizes[name] // LANE) * LANE
    return table, off


def _pack_small(g, nb, d, r):
    table, width = _small_layout(d, r)
    hv = d // HEADS
    pieces = _regroup_pieces(d, r, 8 * d + 2 * r)
    names = ("small", "cv", "ada_b", "norm_g", "b_a1", "b_a2", "b_b", "conv_b", "bias2")

    def body(sm, cv, ab, ng, ba1, ba2, bb, cvb, b2, o_ref):
        o_ref[...] = jnp.zeros_like(o_ref)

        def put(name, val):
            off, n = table[name]
            o_ref[:, pl.ds(off, n)] = val

        put("c_ctx", cv[nb:nb + 1, :])
        put("ada_b", ab[...])
        put("norm_g", ng[...])
        off_b = table["b_in"][0]
        for _, s0, n, dst, d0 in pieces:
            if dst == 1:
                src = bb[:, pl.ds(d0, n)]
            elif d0 < 2 * d:
                src = ba1[:, pl.ds(d0, n)]
            else:
                src = ba2[:, pl.ds(d0 - 2 * d, n)]
            o_ref[:, pl.ds(off_b + s0, n)] = src
        put("conv_b", cvb[...])
        put("conv_ln_g", sm[1:2, :])
        put("conv_ln_b", sm[2:3, :])
        put("decay_bias_fwd", b2[:, 0:d // 2])
        put("decay_bias_bwd", b2[:, d // 2:d])
        gn = sm[3:4, 0:hv]
        for h in range(1, HEADS):
            gn = gn + sm[3:4, h * hv:(h + 1) * hv]
        put("gla_norm_g", gn)
        put("final_norm_g", sm[0:1, :])
        put("loss", sm[4:5, 0:1])

    return pl.pallas_call(body, name="pack_small", out_shape=jax.ShapeDtypeStruct((1, width), F32),
                          compiler_params=_params())(*[g[k] for k in names])


def _small_adam(parts, ws, ms, vs, d, r):
    table, width = _small_layout(d, r)
    n_parts = parts.shape[0]
    k = len(_SMALL)

    def body(p_ref, *refs):
        w_refs, m_refs, v_refs = refs[0:k], refs[k:2 * k], refs[2 * k:3 * k]
        outs = refs[3 * k:]
        tot = p_ref[0]
        for i in range(1, n_parts):
            tot = tot + p_ref[i]
        for i, name in enumerate(_SMALL):
            off, n = table[name]
            g = tot[:, off:off + n]
            outs[i][...] = g
            outs[k + i][...], outs[2 * k + i][...], outs[3 * k + i][...] = _adamw(
                g, w_refs[i][...], m_refs[i][...], v_refs[i][...])
        off, _ = table["loss"]
        outs[4 * k][...] = tot[:, off:off + 1]

    shapes = [jax.ShapeDtypeStruct(w.shape, F32) for w in ws]
    res = pl.pallas_call(body, name="small_adam", out_shape=tuple(shapes * 4 + [jax.ShapeDtypeStruct((1, 1), F32)]),
                         compiler_params=_params())(parts, *ws, *ms, *vs)
    return res[0:k], res[k:2 * k], res[2 * k:3 * k], res[3 * k:4 * k], res[4 * k]


def _mesh_pos():
    return lax.axis_index("x"), lax.axis_index("y"), lax.axis_index("c")


def _all_gather(arrs):
    n = len(arrs)
    ns = 9
    split = [a.ndim == 2 and a.shape[0] % 32 == 0 for a in arrs]

    def body(*refs):
        ins, outs = refs[:n], refs[n:2 * n]
        send_sems, recv_sems, local_sems = refs[2 * n:]
        x, y, c = _mesh_pos()
        me, sibling = (x, y, c), (x, y, 1 - c)
        xn, yn, dg = (1 - x, y, c), (x, 1 - y, c), (1 - x, 1 - y, c)
        other = lambda pos: (pos[0], pos[1], 1 - c)

        def slot(a, pos, half):
            ref = outs[a].at[4 * pos[0] + 2 * pos[1] + pos[2]]
            if half is None:
                return ref
            rows = arrs[a].shape[0] // 2
            return ref.at[pl.ds(half * rows, rows)]

        def copy(a, k, block, to, src=None, half=None):
            dst = slot(a, block, half)
            return pltpu.make_async_remote_copy(
                src_ref=dst if src is None else src, dst_ref=dst,
                send_sem=send_sems.at[ns * a + k], recv_sem=recv_sems.at[ns * a + k],
                device_id=to, device_id_type=MESH)

        h0 = lambda a: 0 if split[a] else None
        mine = [pltpu.make_async_copy(ins[a], slot(a, me, None), local_sems.at[a]) for a in range(n)]
        for cp in mine:
            cp.start()
        sent = []
        for a in range(n):
            sent += [copy(a, 0, me, sibling, src=ins[a]), copy(a, 1, me, xn, src=ins[a]),
                     copy(a, 2, me, yn, src=ins[a])]
        for cp in sent:
            cp.start()

        def pass_on(cp):
            cp.start()
            sent.append(cp)

        for a in range(n):
            copy(a, 1, xn, me).wait_recv()
            pass_on(copy(a, 3, xn, sibling))
            pass_on(copy(a, 4, xn, yn, half=h0(a)))
        for a in range(n):
            copy(a, 2, yn, me).wait_recv()
            pass_on(copy(a, 5, yn, sibling))
            if split[a]:
                pass_on(copy(a, 6, yn, xn, half=1))
        for a in range(n):
            copy(a, 4, dg, me, half=h0(a)).wait_recv()
            pass_on(copy(a, 7, dg, sibling, half=h0(a)))
            if split[a]:
                copy(a, 6, dg, me, half=1).wait_recv()
                pass_on(copy(a, 8, dg, sibling, half=1))
        for a in range(n):
            copy(a, 0, sibling, me).wait_recv()
            copy(a, 3, other(xn), me).wait_recv()
            copy(a, 5, other(yn), me).wait_recv()
            copy(a, 7, other(dg), me, half=h0(a)).wait_recv()
            if split[a]:
                copy(a, 8, other(dg), me, half=1).wait_recv()
        for cp in sent:
            cp.wait_send()
        for cp in mine:
            cp.wait()

    return pl.pallas_call(
        body, name="all_gather",
        out_shape=tuple(jax.ShapeDtypeStruct((N_DEV,) + a.shape, a.dtype) for a in arrs),
        in_specs=[_ANY] * n, out_specs=tuple([_ANY] * n),
        scratch_shapes=[pltpu.SemaphoreType.DMA((ns * n,)), pltpu.SemaphoreType.DMA((ns * n,)),
                        pltpu.SemaphoreType.DMA((n,))],
    )(*arrs)


def _elementwise_tile(r, cdim, cols=2 * LANE):
    if r % 8 == 0 and r > 256:
        return math.gcd(r, 256), cdim
    if r > 256 and cdim % cols == 0:
        return r, cols
    return r, cdim


def _pair_sum(name, mine, theirs):
    _, r, cdim = mine.shape
    tr, tc = _elementwise_tile(r, cdim)

    def body(c_ref, m_ref, t_ref, o_ref):
        o_ref[...] = (m_ref[...].astype(F32) + t_ref[...].astype(F32)).astype(o_ref.dtype)

    return pl.pallas_call(
        body, name=name,
        grid_spec=pltpu.PrefetchScalarGridSpec(
            num_scalar_prefetch=1, grid=(r // tr, cdim // tc),
            in_specs=[pl.BlockSpec((4, None, tr, tc), lambda i, j, c_ref: (0, c_ref[0], i, j)),
                      pl.BlockSpec((4, tr, tc), lambda i, j, c_ref: (0, i, j))],
            out_specs=pl.BlockSpec((4, tr, tc), lambda i, j, c_ref: (0, i, j))),
        out_shape=jax.ShapeDtypeStruct((4, r, cdim), mine.dtype),
        compiler_params=_params())(lax.axis_index("c").reshape(1), mine.reshape(4, 2, r, cdim), theirs)


def _pair_sum_small(mines, theirs):
    n = len(mines)

    def body(*refs):
        c = lax.axis_index("c")
        for i in range(n):
            m_ref, t_ref, o_ref = refs[i], refs[n + i], refs[2 * n + i]
            own = jnp.where(c == 0, m_ref[:, 0].astype(F32), m_ref[:, 1].astype(F32))
            o_ref[...] = (own + t_ref[...].astype(F32)).astype(o_ref.dtype)

    return pl.pallas_call(
        body, name="pair_sum_small_weights",
        out_shape=tuple(jax.ShapeDtypeStruct(t.shape, m.dtype) for m, t in zip(mines, theirs)),
        compiler_params=_params())(*[m.reshape((4, 2) + m.shape[1:]) for m in mines], *theirs)


_HBM = pl.BlockSpec(memory_space=pltpu.HBM)
_SEM = pl.BlockSpec(memory_space=pltpu.SEMAPHORE)


def _copies_start(name, srcs, lands, make_copies, n_sems):
    n, m = len(srcs), len(lands)

    def body(*refs):
        ins = refs[:n + m]
        send_sems, recv_sems = refs[n + m], refs[n + m + 1]
        for cp in make_copies(ins[:n], ins[n:], send_sems, recv_sems):
            cp.start()
        refs[-1][...] = jnp.zeros_like(refs[-1])

    res = pl.pallas_call(
        body, name=name,
        out_shape=(pltpu.SemaphoreType.DMA((n_sems,)), pltpu.SemaphoreType.DMA((n_sems,)),
                   *[pltpu.HBM(a.shape, a.dtype) for a in (*srcs, *lands)], jax.ShapeDtypeStruct((8, LANE), F32)),
        in_specs=[_HBM] * (n + m),
        out_specs=(_SEM, _SEM, *[_HBM] * (n + m), pl.BlockSpec(memory_space=pltpu.VMEM)),
        input_output_aliases={i: 2 + i for i in range(n + m)},
        compiler_params=pltpu.CompilerParams(has_side_effects=pltpu.SideEffectType.DATAFLOW_SIDE_EFFECTING),
    )(*[pltpu.with_memory_space_constraint(a, pltpu.HBM) for a in (*srcs, *lands)])
    return res[0], res[1], res[2:2 + n], res[2 + n:2 + n + m], res[-1]


def _copies_wait(name, started, after, make_copies):
    send_sems, recv_sems, srcs, lands, _ = started
    n, m = len(srcs), len(lands)

    def body(*refs):
        ins = refs[:n + m]
        for cp in make_copies(ins[:n], ins[n:], refs[n + m], refs[n + m + 1]):
            cp.wait_send()
            cp.wait_recv()

    res = pl.pallas_call(
        body, name=name,
        out_shape=tuple(pltpu.HBM(a.shape, a.dtype) for a in (*srcs, *lands)),
        in_specs=[_HBM] * (n + m) + [_SEM, _SEM] + [_ANY] * len(after),
        out_specs=tuple([_HBM] * (n + m)),
        input_output_aliases={i: i for i in range(n + m)},
        compiler_params=pltpu.CompilerParams(has_side_effects=pltpu.SideEffectType.DATAFLOW_SIDE_EFFECTING),
    )(*srcs, *lands, send_sems, recv_sems, *after)
    return res[:n], res[n:]


def _gather_copies(srcs, lands, send_sems, recv_sems):
    x, y, c = _mesh_pos()
    me_i = 4 * x + 2 * y + c
    copies = []
    for rel in range(1, N_DEV):
        peer = (1 - x if rel & 4 else x, 1 - y if rel & 2 else y, 1 - c if rel & 1 else c)
        for a in range(len(srcs)):
            copies.append(pltpu.make_async_remote_copy(
                src_ref=srcs[a], dst_ref=lands[a].at[me_i], send_sem=send_sems.at[7 * a + rel - 1],
                recv_sem=recv_sems.at[7 * a + rel - 1], device_id=peer, device_id_type=MESH))
    return copies


def _sibling_copies(srcs, lands, send_sems, recv_sems):
    x, y, c = _mesh_pos()
    return [pltpu.make_async_remote_copy(
        src_ref=srcs[a].at[2 * k + (1 - c)], dst_ref=lands[a].at[k], send_sem=send_sems.at[4 * a + k],
        recv_sem=recv_sems.at[4 * a + k], device_id=(x, y, 1 - c), device_id_type=MESH)
        for a in range(len(srcs)) for k in range(4)]


def _chip_copies(srcs, lands, send_sems, recv_sems):
    x, y, c = _mesh_pos()
    my_chip = 2 * x + y
    copies = []
    for rel in range(1, 4):
        px = 1 - x if rel & 2 else x
        py = 1 - y if rel & 1 else y
        for a in range(len(srcs)):
            copies.append(pltpu.make_async_remote_copy(
                src_ref=srcs[a].at[2 * px + py], dst_ref=lands[a].at[my_chip], send_sem=send_sems.at[3 * a + rel - 1],
                recv_sem=recv_sems.at[3 * a + rel - 1], device_id=(px, py, c), device_id_type=MESH))
    return copies


def _sum_adam(name, parts, w, m, v, own=None):
    unit_mid = w.ndim == 3
    _, r, cdim = parts.shape
    n_parts = parts.shape[0]
    tr, tc = _elementwise_tile(r, cdim, (4 if unit_mid else 2) * LANE)
    extra = [] if own is None else [own]

    def body(p_ref, *refs):
        w_ref, m_ref, v_ref, g_ref, d_ref, nm_ref, nv_ref = refs[len(extra):]
        if own is None:
            part = lambda k: p_ref[k].astype(F32)
        else:
            my_chip = 2 * lax.axis_index("x") + lax.axis_index("y")
            part = lambda k: jnp.where(my_chip == k, refs[0][...], p_ref[k]).astype(F32)
        g = part(0)
        for k in range(1, n_parts):
            g = g + part(k)
        if unit_mid:
            g = g.reshape(tr, 1, tc)
        g_ref[...] = g
        d_ref[...], nm_ref[...], nv_ref[...] = _adamw(g, w_ref[...], m_ref[...], v_ref[...])

    blk = (pl.BlockSpec((tr, 1, tc), lambda i, j: (i, 0, j)) if unit_mid
           else pl.BlockSpec((tr, tc), lambda i, j: (i, j)))
    o = jax.ShapeDtypeStruct(w.shape, F32)
    return pl.pallas_call(
        body, name=name, grid=(r // tr, cdim // tc),
        in_specs=[pl.BlockSpec((n_parts, tr, tc), lambda i, j: (0, i, j))]
        + [pl.BlockSpec((None, tr, tc), lambda i, j: (2 * lax.axis_index("x") + lax.axis_index("y"), i, j))] * len(extra)
        + [blk, blk, blk],
        out_specs=(blk, blk, blk, blk), out_shape=(o, o, o, o),
        compiler_params=_params())(parts, *extra, w, m, v)


def _sum_adam_small(items):
    n = len(items)

    def body(*refs):
        my_chip = 2 * lax.axis_index("x") + lax.axis_index("y")
        for i in range(n):
            p_ref, own_ref, w_ref, m_ref, v_ref = refs[5 * i:5 * i + 5]
            g_ref, d_ref, nm_ref, nv_ref = refs[5 * n + 4 * i:5 * n + 4 * i + 4]
            g = None
            for k in range(p_ref.shape[0]):
                part = jnp.where(my_chip == k, own_ref[k], p_ref[k]).astype(F32)
                g = part if g is None else g + part
            g_ref[...] = g
            d_ref[...], nm_ref[...], nv_ref[...] = _adamw(g, w_ref[...], m_ref[...], v_ref[...])

    out_shape = tuple(jax.ShapeDtypeStruct(it[2].shape, F32) for it in items for _ in range(4))
    res = pl.pallas_call(body, name="adam_small_weights", out_shape=out_shape,
                         compiler_params=_params())(*[a for it in items for a in it])
    return [res[4 * i:4 * i + 4] for i in range(n)]


_WEIGHTS = ("c_ctx", "ada_w", "ada_b", "norm_g", "w_in", "b_in", "conv_w", "conv_b", "conv_ln_g", "conv_ln_b",
            "conv_proj", "decay_up_fwd", "decay_bias_fwd", "decay_up_bwd", "decay_bias_bwd", "gla_norm_g",
            "gla_proj", "w_out", "final_norm_g")


def _as2d(a):
    if a.ndim == 1:
        return a.reshape(1, -1)
    return a.reshape(-1, a.shape[-1])


def kernel(x, c, ctx, c_ctx, ada_w, ada_b, norm_g, w_in, b_in, conv_w, conv_b, conv_ln_g, conv_ln_b, conv_proj, decay_up_fwd, decay_bias_fwd, decay_up_bwd, decay_bias_bwd, gla_norm_g, gla_proj, w_out, final_norm_g, loss_target, m_c_ctx, m_ada_w, m_ada_b, m_norm_g, m_w_in, m_b_in, m_conv_w, m_conv_b, m_conv_ln_g, m_conv_ln_b, m_conv_proj, m_decay_up_fwd, m_decay_bias_fwd, m_decay_up_bwd, m_decay_bias_bwd, m_gla_norm_g, m_gla_proj, m_w_out, m_final_norm_g, v_c_ctx, v_ada_w, v_ada_b, v_norm_g, v_w_in, v_b_in, v_conv_w, v_conv_b, v_conv_ln_g, v_conv_ln_b, v_conv_proj, v_decay_up_fwd, v_decay_bias_fwd, v_decay_up_bwd, v_decay_bias_bwd, v_gla_norm_g, v_gla_proj, v_w_out, v_final_norm_g):
    env = dict(locals())
    wts = {k: env[k] for k in _WEIGHTS}
    d = x.shape[-1]
    r = decay_up_fwd.shape[1]
    dk_ = d // 2

    ds, dks = d // N_DEV, dk_ // N_DEV
    g_win, g_ada, conv_w8, g_up = _all_gather(
        [w_in[0].astype(BF16), ada_w[0].astype(BF16), conv_w[0],
         jnp.concatenate([decay_up_fwd[0], decay_up_bwd[0]], axis=1)])
    proj_own = [conv_proj[0].astype(BF16), gla_proj[0].astype(BF16), w_out[0].astype(BF16)]
    me_i = 4 * lax.axis_index("x") + 2 * lax.axis_index("y") + lax.axis_index("c")
    proj_lands = [lax.dynamic_update_slice(lax.empty((N_DEV,) + a.shape, a.dtype), a[None], (me_i, 0, 0))
                  for a in proj_own]
    proj_start = _copies_start("proj_gather_start", proj_own, proj_lands, _gather_copies, 7 * 3)

    def proj(after):
        _, lands = _copies_wait("proj_gather_wait", proj_start, (after,), _gather_copies)
        return [w.reshape(d, d) for w in lands]

    w_a, w_b = _unshard_w_in(g_win, d, r, after=(proj_start[4],))
    up_f = g_up[:, :, 0:dks].transpose(1, 0, 2).reshape(r, dk_)
    up_b = g_up[:, :, dks:].transpose(1, 0, 2).reshape(r, dk_)
    up2 = jnp.zeros((LANE, 2 * dk_), F32).at[0:r, 0:dk_].set(up_f).at[r:2 * r, dk_:].set(up_b)
    bias2 = jnp.concatenate([decay_bias_fwd, decay_bias_bwd], axis=1)
    b_a, b_b = _regroup(b_in, d, r)

    comm = {}

    def on_grads(gr):
        d_up = jnp.concatenate([gr["up2"][0:r, 0:dk_].reshape(r, N_DEV, dks).transpose(1, 0, 2),
                                gr["up2"][r:2 * r, dk_:].reshape(r, N_DEV, dks).transpose(1, 0, 2)], axis=2)
        mine = [_reshard_w_in(gr["w_a1"], gr["w_a2"], gr["w_b"], d, r), gr["conv_proj"].reshape(N_DEV, ds, d),
                gr["gla_proj"].reshape(N_DEV, ds, d), gr["w_out"].reshape(N_DEV, ds, d), gr["conv_w8"], d_up]
        lands = [lax.empty((4,) + a.shape[1:], a.dtype) for a in mine]
        comm["sibling"] = _copies_start("grad_sibling_start", mine, lands, _sibling_copies, 4 * len(mine))
        return (comm["sibling"][4],)

    def on_du_a1(du_a1):
        mine, theirs = _copies_wait("grad_sibling_wait", comm["sibling"], (du_a1,), _sibling_copies)
        sums = [_pair_sum("pair_sum_w_in", mine[0], theirs[0])] + list(_pair_sum_small(mine[1:], theirs[1:]))
        lands = [lax.empty(a.shape, a.dtype) for a in sums]
        comm["chips"] = _copies_start("grad_chips_start", sums, lands, _chip_copies, 3 * len(sums))
        return (comm["chips"][4],)

    g = _local_step(x, c, ctx, loss_target, c_ctx, g_ada, ada_b, norm_g[0:1], w_a, b_a, w_b, b_b,
                    conv_w8, conv_b, conv_ln_g, conv_ln_b, up2, bias2, gla_norm_g, final_norm_g.reshape(1, d),
                    proj, on_grads, on_du_a1)

    small_mine = [_pack_small(g, x.shape[0], d, r), g["ada_sv"], g["ada_dmod"]]
    small_lands = [lax.dynamic_update_slice(lax.empty((N_DEV,) + a.shape, F32), a[None], (me_i, 0, 0))
                   for a in small_mine]
    small_start = _copies_start("small_gather_start", small_mine, small_lands, _gather_copies, 7 * len(small_mine))
    own, landed = _copies_wait("grad_chips_wait", comm["chips"], (small_start[4],), _chip_copies)
    o_win, o_cp, o_gp, o_wo, o_cw, o_up = own
    x_win, x_cp, x_gp, x_wo, x_cw, x_up = landed

    out = {}

    as_rows = lambda a: jnp.transpose(a, (2, 0, 1))
    res = _sum_adam("adam_w_in", x_win, as_rows(w_in), as_rows(m_w_in), as_rows(v_w_in), o_win)
    for pre, arr in zip(("grad_", "delta_", "new_m_", "new_v_"), res):
        out[pre + "w_in"] = jnp.transpose(arr, (1, 2, 0))
    small_w = (("conv_proj", x_cp, o_cp), ("gla_proj", x_gp, o_gp), ("w_out", x_wo, o_wo), ("conv_w", x_cw, o_cw),
               ("decay_up_fwd", x_up[:, :, 0:dks], o_up[:, :, 0:dks]),
               ("decay_up_bwd", x_up[:, :, dks:], o_up[:, :, dks:]))
    small_res = _sum_adam_small([(p, o, _as2d(wts[k]), _as2d(env["m_" + k]), _as2d(env["v_" + k]))
                                 for k, p, o in small_w])
    for (k, _, _), arrs in zip(small_w, small_res):
        for pre, arr in zip(("grad_", "delta_", "new_m_", "new_v_"), arrs):
            out[pre + k] = arr.reshape(wts[k].shape)

    _, (packs, sv_all, dmod_all) = _copies_wait("small_gather_wait", small_start, (res[0], out["grad_w_out"]),
                                                _gather_copies)
    ada_res = _ada_adam(sv_all, dmod_all, _as2d(ada_w), _as2d(m_ada_w), _as2d(v_ada_w), x.shape[0] + 1)
    for pre, arr in zip(("grad_", "delta_", "new_m_", "new_v_"), ada_res):
        out[pre + "ada_w"] = arr.reshape(ada_w.shape)
    row = lambda a: a.reshape(1, -1)
    sg, sd, sm, sv, loss = _small_adam(packs, [row(wts[k]) for k in _SMALL], [row(env["m_" + k]) for k in _SMALL],
                                       [row(env["v_" + k]) for k in _SMALL], d, r)
    for i, k in enumerate(_SMALL):
        for pre, arrs in (("grad_", sg), ("delta_", sd), ("new_m_", sm), ("new_v_", sv)):
            out[pre + k] = arrs[i].reshape(wts[k].shape)
    loss = loss.reshape(())

    return (loss, g["grad_x"], *[out["grad_" + k] for k in _WEIGHTS], *[out["delta_" + k] for k in _WEIGHTS],
            *[out["new_m_" + k] for k in _WEIGHTS], *[out["new_v_" + k] for k in _WEIGHTS])
```

```python
import functools
import math

import jax
import jax.numpy as jnp
from jax import lax
from jax.experimental import pallas as pl
from jax.experimental.pallas import tpu as pltpu

F32 = jnp.float32
BF16 = jnp.bfloat16
MESH = pl.DeviceIdType.MESH

N_DEV = 8
GRID_W = 64
CHUNK = 128
HEADS = 4
EPS = 1e-6
GATE_TAU = 16.0
LANE = 128
ADAM_LR, ADAM_B1, ADAM_B2, ADAM_EPS, ADAM_WD, ADAM_STEP = 0.001, 0.9, 0.999, 1e-08, 0.01, 10
VMEM_LIMIT = 60 * 1024 * 1024
_ANY = pl.BlockSpec(memory_space=pl.ANY)


def _params(**kw):
    return pltpu.CompilerParams(vmem_limit_bytes=VMEM_LIMIT, **kw)


def _tile(n, pref):
    t = (min(pref, n) // LANE) * LANE
    while t >= LANE:
        if n % t == 0:
            return t
        t -= LANE
    return n


def _mm(a, b):
    return jnp.dot(a.astype(BF16), b.astype(BF16), preferred_element_type=F32)


def _mm_nt(a, b):
    return lax.dot_general(a.astype(BF16), b.astype(BF16), (((1,), (1,)), ((), ())), preferred_element_type=F32)


def _mm_tn(a, b):
    return lax.dot_general(a.astype(BF16), b.astype(BF16), (((0,), (0,)), ((), ())), preferred_element_type=F32)


def _sigmoid(x):
    return 0.5 * jnp.tanh(0.5 * x) + 0.5


def _dsilu(x, s):
    return s * (1.0 + x * (1.0 - s))


def _adamw(g, w, m, v):
    bc1 = 1.0 - ADAM_B1 ** ADAM_STEP
    bc2 = 1.0 - ADAM_B2 ** ADAM_STEP
    mn = ADAM_B1 * m + (1.0 - ADAM_B1) * g
    vn = ADAM_B2 * v + (1.0 - ADAM_B2) * (g * g)
    delta = -ADAM_LR * ((mn / bc1) / (jnp.sqrt(vn / bc2) + ADAM_EPS) + ADAM_WD * w)
    return delta, mn, vn


def _rowsel(table, idx, n):
    out = table[0:1, :]
    for r in range(1, n):
        out = jnp.where(idx == r, table[r:r + 1, :], out)
    return out


def _ada_fwd(cv, ada_w8, ada_b):
    n_sh, _, ws = ada_w8.shape

    def body(cv_ref, w_ref, b_ref, o_ref):
        c = cv_ref[...]
        sv = c * _sigmoid(c)
        for j in range(n_sh):
            cols = pl.ds(j * ws, ws)
            o_ref[:, cols] = _mm(sv, w_ref[j]) + b_ref[:, cols]

    return pl.pallas_call(body, name="ada_fwd", out_shape=jax.ShapeDtypeStruct((cv.shape[0], n_sh * ws), F32),
                          compiler_params=_params())(cv, ada_w8, ada_b)


def _ada_bwd(cv, ada_w8, dmod_ss, small):
    n_sh, d, ws = ada_w8.shape

    def body(cv_ref, w_ref, dm_ref, sm_ref, sv_ref, dmod_ref, db_ref, dc_ref):
        c = cv_ref[...]
        s = _sigmoid(c)
        sv_ref[...] = c * s
        dm = jnp.concatenate([dm_ref[:, 0:2 * d], sm_ref[8:16, :]], axis=1)
        dmod_ref[...] = dm
        db_ref[...] = jnp.sum(dm, axis=0, keepdims=True)
        dsv = None
        for j in range(n_sh):
            part = _mm_nt(dm[:, j * ws:(j + 1) * ws], w_ref[j])
            dsv = part if dsv is None else dsv + part
        dc_ref[...] = dsv * _dsilu(c, s)

    return pl.pallas_call(
        body, name="ada_bwd",
        out_shape=(jax.ShapeDtypeStruct(cv.shape, F32), jax.ShapeDtypeStruct(dmod_ss.shape, F32),
                   jax.ShapeDtypeStruct((1, n_sh * ws), F32), jax.ShapeDtypeStruct(cv.shape, F32)),
        compiler_params=_params())(cv, ada_w8, dmod_ss, small)


def _ada_adam(sv_all, dmod_all, w, m, v, n_terms):
    n_dev, rows, d = sv_all.shape
    ws = w.shape[1]

    tr = _tile(d, 256)

    def body(i_ref, sv_ref, dm_ref, w_ref, m_ref, v_ref, g_ref, d_ref, nm_ref, nv_ref):
        sv_t = jnp.transpose(sv_ref[...].reshape(n_dev * rows, tr))
        dm = dm_ref[...]
        blk = 64
        for r0 in range(0, tr, blk):
            g = None
            for k in range(n_dev):
                for row in range(n_terms):
                    col = k * rows + row
                    term = sv_t[r0:r0 + blk, col:col + 1] * dm[k, row:row + 1, :]
                    g = term if g is None else g + term
            sl = pl.ds(r0, blk)
            g_ref[sl, :] = g
            d_ref[sl, :], nm_ref[sl, :], nv_ref[sl, :] = _adamw(g, w_ref[sl, :], m_ref[sl, :], v_ref[sl, :])

    blk_w = pl.BlockSpec((tr, ws), lambda i, i_ref: (i, 0))
    o = jax.ShapeDtypeStruct(w.shape, F32)
    me_i = 4 * lax.axis_index("x") + 2 * lax.axis_index("y") + lax.axis_index("c")
    return pl.pallas_call(
        body, name="adam_ada_w",
        grid_spec=pltpu.PrefetchScalarGridSpec(
            num_scalar_prefetch=1, grid=(d // tr,),
            in_specs=[pl.BlockSpec((n_dev, rows, tr), lambda i, i_ref: (0, 0, i)),
                      pl.BlockSpec((n_dev, rows, ws), lambda i, i_ref: (0, 0, i_ref[0])),
                      blk_w, blk_w, blk_w],
            out_specs=(blk_w,) * 4),
        out_shape=(o, o, o, o), compiler_params=_params())(me_i.reshape(1), sv_all, dmod_all, w, m, v)


class _Tiles:
    def __init__(self, nb, s_len, c_len, tm, big):
        self.nb, self.tm, self.big = nb, tm, big
        self.lat, self.ctx = s_len // tm, c_len // tm
        self.pad = -(self.lat + self.ctx) % big
        self.per_ex = self.lat + self.ctx + self.pad
        self.n_all = nb * self.per_ex
        self.rows_per_ex = self.per_ex * tm

    def is_lat(self, i):
        return i % self.per_ex < self.lat

    def is_pad(self, i):
        return i % self.per_ex >= self.lat + self.ctx

    def lat_of_all(self, i):
        return (i // self.per_ex) * self.lat + jnp.minimum(i % self.per_ex, self.lat - 1)

    def ctx_of_all(self, i):
        return (i // self.per_ex) * self.ctx + jnp.clip(i % self.per_ex - self.lat, 0, self.ctx - 1)


def _norm_fwd(x2, ctx2, mod, norm_g, tiles):
    tl, d = x2.shape
    tc = ctx2.shape[0]
    nb, tm = tiles.nb, tiles.tm

    def body(x_ref, c_ref, mod_ref, g_ref, u_ref):
        i = pl.program_id(0)
        lat = tiles.is_lat(i)
        xv = jnp.where(lat, x_ref[...], c_ref[...])
        row = jnp.where(lat, i // tiles.per_ex, nb)
        m = _rowsel(mod_ref[...], row, nb + 1)
        shift, scale = m[:, 0:d], m[:, d:2 * d]
        rstd = lax.rsqrt(jnp.mean(xv * xv, axis=-1, keepdims=True) + EPS)
        u = xv * rstd * g_ref[...] * (1.0 + scale) + shift
        u_ref[...] = jnp.where(tiles.is_pad(i), 0.0, u).astype(BF16)

    return pl.pallas_call(
        body, name="norm_fwd", grid=(tiles.n_all,),
        in_specs=[pl.BlockSpec((tm, d), lambda i: (tiles.lat_of_all(i), 0)),
                  pl.BlockSpec((tm, d), lambda i: (tiles.ctx_of_all(i), 0)),
                  pl.BlockSpec(mod.shape, lambda i: (0, 0)),
                  pl.BlockSpec((1, d), lambda i: (0, 0))],
        out_specs=pl.BlockSpec((tm, d), lambda i: (i, 0)),
        out_shape=jax.ShapeDtypeStruct((tiles.n_all * tm, d), BF16),
        compiler_params=_params())(x2, ctx2, mod, norm_g)


def _norm_bwd(x2, ctx2, mod, norm_g, du_lat, du_b, gx1, tiles):
    tl, d = x2.shape
    nb, tm = tiles.nb, tiles.tm
    nrow = mod.shape[0]
    n_lat_in = len(du_lat)

    def body(x_ref, c_ref, mod_ref, g_ref, *refs):
        dl_refs = refs[:n_lat_in]
        d3_ref, gx_ref, gxo_ref, dmod_ref, dg_ref = refs[n_lat_in:]
        i = pl.program_id(0)

        @pl.when(i == 0)
        def _():
            dmod_ref[...] = jnp.zeros_like(dmod_ref)
            dg_ref[...] = jnp.zeros_like(dg_ref)

        lat = tiles.is_lat(i)
        xv = jnp.where(lat, x_ref[...], c_ref[...])
        row = jnp.where(lat, i // tiles.per_ex, nb)
        m = _rowsel(mod_ref[...], row, nb + 1)
        scale = m[:, d:2 * d]
        g = g_ref[...]
        dl = dl_refs[0][...].astype(F32)
        for ref in dl_refs[1:]:
            dl = dl + ref[...].astype(F32)
        du = jnp.where(tiles.is_pad(i), 0.0, d3_ref[...].astype(F32) + jnp.where(lat, dl, 0.0))
        rstd = lax.rsqrt(jnp.mean(xv * xv, axis=-1, keepdims=True) + EPS)
        xh = xv * rstd
        dshift = jnp.sum(du, axis=0, keepdims=True)
        dscale = jnp.sum(du * xh * g, axis=0, keepdims=True)
        dxn = du * (1.0 + scale)
        dg_ref[...] += jnp.sum(dxn * xh, axis=0, keepdims=True)
        dxh = dxn * g
        dx = rstd * (dxh - xh * jnp.mean(dxh * xh, axis=-1, keepdims=True))

        @pl.when(lat)
        def _():
            gxo_ref[...] = dx + gx_ref[...]

        for r in range(nb + 1):
            dmod_ref[r:r + 1, 0:d] += jnp.where(row == r, dshift, 0.0)
            dmod_ref[r:r + 1, d:2 * d] += jnp.where(row == r, dscale, 0.0)

    lat_map = lambda i: (tiles.lat_of_all(i), 0)
    lat_spec = pl.BlockSpec((tm, d), lat_map)
    return pl.pallas_call(
        body, name="norm_bwd", grid=(tiles.n_all,),
        in_specs=[lat_spec,
                  pl.BlockSpec((tm, d), lambda i: (tiles.ctx_of_all(i), 0)),
                  pl.BlockSpec(mod.shape, lambda i: (0, 0)),
                  pl.BlockSpec((1, d), lambda i: (0, 0))]
                 + [lat_spec] * n_lat_in
                 + [pl.BlockSpec((tm, d), lambda i: (i, 0)), lat_spec],
        out_specs=(lat_spec,
                   pl.BlockSpec((nrow, 3 * d), lambda i: (0, 0)),
                   pl.BlockSpec((1, d), lambda i: (0, 0))),
        out_shape=(jax.ShapeDtypeStruct((tl, d), F32), jax.ShapeDtypeStruct((nrow, 3 * d), F32),
                   jax.ShapeDtypeStruct((1, d), F32)),
        compiler_params=_params())(x2, ctx2, mod, norm_g, *du_lat, du_b, gx1)


def _matmul_bias(name, u3, w, b, s_len, tm, tn):
    nb = u3.shape[0]
    d, n = w.shape
    per = s_len // tm
    rows = nb * s_len

    def body(u_ref, w_ref, b_ref, o_ref):
        o_ref[...] = jnp.dot(u_ref[...], w_ref[...], preferred_element_type=F32) + b_ref[...]

    return pl.pallas_call(
        body, name=name, grid=(n // tn, rows // tm),
        in_specs=[pl.BlockSpec((None, tm, d), lambda j, i: (i // per, i % per, 0)),
                  pl.BlockSpec((d, tn), lambda j, i: (0, j)),
                  pl.BlockSpec((1, tn), lambda j, i: (0, j))],
        out_specs=pl.BlockSpec((tm, tn), lambda j, i: (i, j)),
        out_shape=jax.ShapeDtypeStruct((rows, n), F32),
        compiler_params=_params())(u3, w, b)


def _log_sigmoid(x):
    return jnp.minimum(x, 0.0) - jnp.log(1.0 + jnp.exp(-jnp.abs(x)))


def _inproj_b(u, w_b, b_b, up2, bias2, tm, dk_, dv_):
    t_all, d = u.shape
    nbw = w_b.shape[1]
    n2 = up2.shape[1]

    def body(u_ref, w_ref, b_ref, up_ref, bias_ref, qk_ref, v_ref, g_ref):
        full = jnp.dot(u_ref[...], w_ref[...], preferred_element_type=F32) + b_ref[...]
        lr = full[:, 2 * dk_ + dv_:nbw]
        qk_ref[:, 0:2 * dk_] = full[:, 0:2 * dk_]
        qk_ref[:, 2 * dk_:2 * dk_ + LANE] = lr
        v_ref[...] = full[:, 2 * dk_:2 * dk_ + dv_].astype(BF16)
        g_ref[...] = _log_sigmoid(_mm(lr, up_ref[...]) + bias_ref[...]) * (1.0 / GATE_TAU)

    whole = lambda a: pl.BlockSpec(a.shape, lambda i: (0, 0))
    return pl.pallas_call(
        body, name="inproj_b", grid=(t_all // tm,),
        in_specs=[pl.BlockSpec((tm, d), lambda i: (i, 0)), whole(w_b), whole(b_b), whole(up2), whole(bias2)],
        out_specs=(pl.BlockSpec((tm, 2 * dk_ + LANE), lambda i: (i, 0)), pl.BlockSpec((tm, dv_), lambda i: (i, 0)),
                   pl.BlockSpec((tm, n2), lambda i: (i, 0))),
        out_shape=(jax.ShapeDtypeStruct((t_all, 2 * dk_ + LANE), F32), jax.ShapeDtypeStruct((t_all, dv_), BF16),
                   jax.ShapeDtypeStruct((t_all, n2), F32)),
        compiler_params=_params())(u, w_b, b_b, up2, bias2)


def _matmul_nt(name, a, w, koff, tm, tk, after=()):
    r, kc = a.shape
    d = w.shape[0]
    nk = kc // tk

    def body(a_ref, w_ref, *rest):
        o_ref = rest[len(after)]
        k = pl.program_id(1)
        p = lax.dot_general(a_ref[...], w_ref[...], (((1,), (1,)), ((), ())), preferred_element_type=F32)
        if nk == 1:
            o_ref[...] = p.astype(o_ref.dtype)
            return
        acc_ref = rest[len(after) + 1]

        @pl.when(k == 0)
        def _():
            acc_ref[...] = p

        @pl.when(k > 0)
        def _():
            acc_ref[...] += p

        @pl.when(k == nk - 1)
        def _():
            o_ref[...] = acc_ref[...].astype(o_ref.dtype)

    return pl.pallas_call(
        body, name=name, grid=(r // tm, nk),
        in_specs=[pl.BlockSpec((tm, tk), lambda i, k: (i, k)),
                  pl.BlockSpec((d, tk), lambda i, k: (0, koff + k))] + [_ANY] * len(after),
        out_specs=pl.BlockSpec((tm, d), lambda i, k: (i, 0)),
        out_shape=jax.ShapeDtypeStruct((r, d), BF16),
        scratch_shapes=[pltpu.VMEM((tm, d), F32)] if nk > 1 else [],
        compiler_params=_params())(a, w, *after)


def _matmul_tn(name, a, b, rows, tk, tn):
    m = a.shape[1]
    n = b.shape[1]
    nk = rows // tk

    def body(a_ref, b_ref, o_ref, s_ref, acc_ref):
        k = pl.program_id(1)
        bv = b_ref[...]
        p = lax.dot_general(bv, a_ref[...], (((0,), (0,)), ((), ())), preferred_element_type=F32)
        cs = jnp.sum(bv.astype(F32), axis=0, keepdims=True)

        @pl.when(k == 0)
        def _():
            acc_ref[...] = p
            s_ref[...] = cs

        @pl.when(k > 0)
        def _():
            acc_ref[...] += p
            s_ref[...] += cs

        @pl.when(k == nk - 1)
        def _():
            o_ref[...] = acc_ref[...].astype(o_ref.dtype)

    return pl.pallas_call(
        body, name=name, grid=(n // tn, nk),
        in_specs=[pl.BlockSpec((tk, m), lambda j, k: (k, 0)),
                  pl.BlockSpec((tk, tn), lambda j, k: (k, j))],
        out_specs=(pl.BlockSpec((tn, m), lambda j, k: (j, 0)), pl.BlockSpec((1, tn), lambda j, k: (0, j))),
        out_shape=(jax.ShapeDtypeStruct((n, m), BF16), jax.ShapeDtypeStruct((1, n), F32)),
        scratch_shapes=[pltpu.VMEM((tn, m), F32)],
        compiler_params=_params())(a, b)


def _matmul_tn_whole(name, a3, b3, rows, tn, transposed):
    nb, _, m = a3.shape
    n = b3.shape[2]

    def body(a_ref, b_ref, o_ref, s_ref):
        p, cs = None, None
        for e in range(nb):
            bv = b_ref[e]
            lhs, rhs = (bv, a_ref[e]) if transposed else (a_ref[e], bv)
            pe = lax.dot_general(lhs, rhs, (((0,), (0,)), ((), ())), preferred_element_type=F32)
            ce = jnp.sum(bv.astype(F32), axis=0, keepdims=True)
            p, cs = (pe, ce) if p is None else (p + pe, cs + ce)
        o_ref[...] = p.astype(o_ref.dtype)
        s_ref[...] = cs

    o_spec, o_shape = ((pl.BlockSpec((tn, m), lambda j: (j, 0)), (n, m)) if transposed
                       else (pl.BlockSpec((m, tn), lambda j: (0, j)), (m, n)))
    return pl.pallas_call(
        body, name=name, grid=(n // tn,),
        in_specs=[pl.BlockSpec((nb, rows, m), lambda j: (0, 0, 0)),
                  pl.BlockSpec((nb, rows, tn), lambda j: (0, 0, j))],
        out_specs=(o_spec, pl.BlockSpec((1, tn), lambda j: (0, j))),
        out_shape=(jax.ShapeDtypeStruct(o_shape, BF16), jax.ShapeDtypeStruct((1, n), F32)),
        compiler_params=_params())(a3, b3)


def _conv_window(pad_ref, r, shift, ktaps, width, horizontal):
    if horizontal:
        return pad_ref[r, pl.ds(16 + shift, width), :]
    return pad_ref[r + ktaps // 2 + shift]


def _conv_row(pad_ref, w, r, ktaps, width, horizontal, flip):
    half = ktaps // 2
    acc = None
    for t in range(ktaps):
        win = _conv_window(pad_ref, r, (half - t) if flip else (t - half), ktaps, width, horizontal)
        term = win * w[t:t + 1, :]
        acc = term if acc is None else acc + term
    return acc


def _fill_padded(ref, val, rows, width, ktaps, horizontal):
    half_k = ktaps // 2
    cb = val.shape[-1]
    if horizontal:
        ref[:, 0:16, :] = jnp.zeros((rows, 16, cb), F32)
        ref[:, 16 + width:32 + width, :] = jnp.zeros((rows, 16, cb), F32)
        ref[:, 16:16 + width, :] = val
    else:
        ref[0:half_k, :, :] = jnp.zeros((half_k, width, cb), F32)
        ref[half_k + rows:2 * half_k + rows, :, :] = jnp.zeros((half_k, width, cb), F32)
        ref[half_k:half_k + rows, :, :] = val


def _conv_fwd(pa, conv_w8, conv_b, nb, s):
    nblk, ktaps, cb = conv_w8.shape
    d = nblk * cb
    rows, width = s // GRID_W, GRID_W
    half_k = ktaps // 2
    nh = nblk // 2

    def body(glu_ref, w_ref, b_ref, o_ref, ph_ref, pv_ref):
        j = pl.program_id(1)
        a0 = (glu_ref[:, 0:cb] * _sigmoid(glu_ref[:, cb:2 * cb])).reshape(rows, width, cb)
        w = w_ref[...]

        bias = b_ref[...]

        def run(pad_ref, horizontal):
            _fill_padded(pad_ref, a0, rows, width, ktaps, horizontal)

            def row(r, carry):
                at = pl.ds(pl.multiple_of(r * width, width), width)
                o_ref[at, :] = _conv_row(pad_ref, w, r, ktaps, width, horizontal, False) + bias
                return carry

            lax.fori_loop(0, rows, row, 0)

        @pl.when(j < nh)
        def _():
            run(ph_ref, True)

        @pl.when(j >= nh)
        def _():
            run(pv_ref, False)

    return pl.pallas_call(
        body, name="conv_fwd", grid=(nb, nblk),
        in_specs=[pl.BlockSpec((s, 2 * cb), lambda b, j: (b, j)),
                  pl.BlockSpec((None, ktaps, cb), lambda b, j: (j, 0, 0)),
                  pl.BlockSpec((1, cb), lambda b, j: (0, j))],
        out_specs=pl.BlockSpec((s, cb), lambda b, j: (b, j)),
        out_shape=jax.ShapeDtypeStruct((nb * s, d), F32),
        scratch_shapes=[pltpu.VMEM((rows, width + 32, cb), F32), pltpu.VMEM((rows + 2 * half_k, width, cb), F32)],
        compiler_params=_params())(pa, conv_w8, conv_b)


def _conv_bwd(pa, da1, conv_w8, nb, s):
    nblk, ktaps, cb = conv_w8.shape
    d = nblk * cb
    rows, width = s // GRID_W, GRID_W
    half_k = ktaps // 2
    nh = nblk // 2

    def body(glu_ref, da_ref, w_ref, dp_ref, dw_ref, db_ref, pha_ref, phd_ref, pva_ref, pvd_ref):
        j = pl.program_id(0)
        b = pl.program_id(1)
        a0 = (glu_ref[:, 0:cb] * _sigmoid(glu_ref[:, cb:2 * cb])).reshape(rows, width, cb)
        da1v = da_ref[...]
        d3 = da1v.reshape(rows, width, cb)
        w = w_ref[...]

        @pl.when(b == 0)
        def _():
            dw_ref[...] = jnp.zeros_like(dw_ref)
            db_ref[...] = jnp.zeros_like(db_ref)

        db_ref[...] += jnp.sum(da1v, axis=0, keepdims=True)

        def run(pa_ref, pd_ref, horizontal):
            _fill_padded(pa_ref, a0, rows, width, ktaps, horizontal)
            _fill_padded(pd_ref, d3, rows, width, ktaps, horizontal)

            def row(r, accs):
                at = pl.ds(pl.multiple_of(r * width, width), width)
                da0 = _conv_row(pd_ref, w, r, ktaps, width, horizontal, True)
                gv = glu_ref[at, 0:cb]
                sg = _sigmoid(glu_ref[at, cb:2 * cb])
                dp_ref[at, 0:cb] = (da0 * sg).astype(BF16)
                dp_ref[at, cb:2 * cb] = (da0 * gv * sg * (1.0 - sg)).astype(BF16)
                d_row = da_ref[at, :]
                out = []
                for t in range(ktaps):
                    prod = _conv_window(pa_ref, r, t - half_k, ktaps, width, horizontal) * d_row
                    out.append(accs[t] + jnp.sum(prod.reshape(width // 8, 8, cb), axis=0))
                return tuple(out)

            accs = lax.fori_loop(0, rows, row, tuple(jnp.zeros((8, cb), F32) for _ in range(ktaps)))
            for t in range(ktaps):
                dw_ref[t:t + 1, :] += jnp.sum(accs[t], axis=0, keepdims=True)

        @pl.when(j < nh)
        def _():
            run(pha_ref, phd_ref, True)

        @pl.when(j >= nh)
        def _():
            run(pva_ref, pvd_ref, False)

    return pl.pallas_call(
        body, name="conv_bwd", grid=(nblk, nb),
        in_specs=[pl.BlockSpec((s, 2 * cb), lambda j, b: (b, j)),
                  pl.BlockSpec((s, cb), lambda j, b: (b, j)),
                  pl.BlockSpec((None, ktaps, cb), lambda j, b: (j, 0, 0))],
        out_specs=(pl.BlockSpec((s, 2 * cb), lambda j, b: (b, j)),
                   pl.BlockSpec((None, ktaps, cb), lambda j, b: (j, 0, 0)),
                   pl.BlockSpec((1, cb), lambda j, b: (0, j))),
        out_shape=(jax.ShapeDtypeStruct((nb * s, 2 * d), BF16),
                   jax.ShapeDtypeStruct((nblk, ktaps, cb), F32), jax.ShapeDtypeStruct((1, d), F32)),
        scratch_shapes=[pltpu.VMEM((rows, width + 32, cb), F32), pltpu.VMEM((rows, width + 32, cb), F32),
                        pltpu.VMEM((rows + 2 * half_k, width, cb), F32),
                        pltpu.VMEM((rows + 2 * half_k, width, cb), F32)],
        compiler_params=_params())(pa, da1, conv_w8)


def _decay_bwd(pb, up2, bias2, grads_f, grads_b, tiles, lr_blk, dk_, dv_):
    t_all = pb.shape[0]
    tm = tiles.tm
    n2 = up2.shape[1]
    nbw = 2 * dk_ + dv_ + LANE

    def body(lr_ref, up_ref, b_ref, dqf, dkf, dvf, dgf, dqb, dkb, dvb, dgb, dp_ref, dup_ref, dbias_ref):
        i = pl.program_id(0)
        pad = tiles.is_pad(i)
        live = lambda v: jnp.where(pad, 0.0, v)

        @pl.when(i == 0)
        def _():
            dup_ref[...] = jnp.zeros_like(dup_ref)
            dbias_ref[...] = jnp.zeros_like(dbias_ref)

        lr = lr_ref[...]
        up = up_ref[...]
        logits = _mm(lr, up) + b_ref[...]
        dg = live(jnp.concatenate([dgf[...], dgb[...]], axis=1))
        dlog = dg * (1.0 / GATE_TAU) * _sigmoid(-logits)
        dup_ref[...] += _mm_tn(lr, dlog)
        dbias_ref[...] += jnp.sum(dlog, axis=0, keepdims=True)
        both = lambda f, b: live(f[...].astype(F32) + b[...].astype(F32)).astype(BF16)
        dp_ref[:, 0:dk_] = both(dqf, dqb)
        dp_ref[:, dk_:2 * dk_] = both(dkf, dkb)
        dp_ref[:, 2 * dk_:2 * dk_ + dv_] = both(dvf, dvb)
        dp_ref[:, 2 * dk_ + dv_:nbw] = _mm_nt(dlog, up).astype(BF16)

    row = lambda w: pl.BlockSpec((tm, w), lambda i: (i, 0))
    return pl.pallas_call(
        body, name="decay_bwd", grid=(t_all // tm,),
        in_specs=[pl.BlockSpec((tm, LANE), lambda i: (i, lr_blk)),
                  pl.BlockSpec(up2.shape, lambda i: (0, 0)),
                  pl.BlockSpec((1, n2), lambda i: (0, 0)),
                  row(dk_), row(dk_), row(dv_), row(dk_), row(dk_), row(dk_), row(dv_), row(dk_)],
        out_specs=(row(nbw), pl.BlockSpec(up2.shape, lambda i: (0, 0)), pl.BlockSpec((1, n2), lambda i: (0, 0))),
        out_shape=(jax.ShapeDtypeStruct((t_all, nbw), BF16), jax.ShapeDtypeStruct(up2.shape, F32),
                   jax.ShapeDtypeStruct((1, n2), F32)),
        compiler_params=_params())(pb, up2, bias2, *grads_f, *grads_b)


def _scan_chunk(s, nl, nc, rev):
    if rev:
        return jnp.where(s < nc, nl + (nc - 1 - s), nl - 1 - (s - nc))
    return jnp.where(s < nc, nl + s, s - nc)


def _scan_lat_chunk(s, nl, nc, rev):
    first = nl - 1 if rev else 0
    return jnp.where(s < nc, first, _scan_chunk(s, nl, nc, rev))


def _tri_mm(m_bf, x):
    hi = x.astype(BF16)
    r1 = x - hi.astype(F32)
    mid = r1.astype(BF16)
    lo = (r1 - mid.astype(F32)).astype(BF16)
    dot = lambda p: jnp.dot(m_bf, p, preferred_element_type=F32)
    return dot(hi) + dot(mid) + dot(lo)


def _chunk_masks(c, rev):
    ii = lax.broadcasted_iota(jnp.int32, (c, c), 0)
    jj = lax.broadcasted_iota(jnp.int32, (c, c), 1)
    return ((ii <= jj), (ii >= jj)) if rev else ((ii >= jj), (ii <= jj))


def _chunk_terms(q, k, b, far, mid):
    bf, bm = b[far:far + 1, :], b[mid:mid + 1, :]
    e = jnp.exp(b)
    em = jnp.exp(b - bm)
    eim = jnp.exp(bm - b)
    ed = jnp.exp(bf - b)
    return dict(e=e, em=em, eim=eim, ed=ed, dec=jnp.exp(bf), qe=q * e, qem=q * em, kim=k * eim, kd=k * ed)


def _gla_fwd(pb3, pv3, g3, nb, s_len, c_len, dk_, dv_):
    c = CHUNK
    nl, nc = s_len // c, c_len // c
    ns = nl + nc
    hk, hv = dk_ // HEADS, dv_ // HEADS
    l_len = pb3.shape[1]
    scale = hk ** -0.5
    mid = c // 2

    def body(*refs):
        ins, outs, z_scr = refs[:8], refs[8:14], refs[14]
        s = pl.program_id(0)

        @pl.when(s == 0)
        def _():
            z_scr[...] = jnp.zeros_like(z_scr)

        qs = jnp.where(s >= nc, scale, 0.0)
        for di, rev in enumerate((False, True)):
            q_ref, k_ref, v_ref, g_ref = ins[4 * di:4 * di + 4]
            o_ref, zs_ref, b_ref = outs[3 * di:3 * di + 3]
            mask, _ = _chunk_masks(c, rev)
            m_bf = mask.astype(BF16)
            far = 0 if rev else c - 1
            for b in range(nb):
                bc = _tri_mm(m_bf, g_ref[b])
                b_ref[b] = bc
                for h in range(HEADS):
                    ks, vs = slice(h * hk, (h + 1) * hk), slice(h * hv, (h + 1) * hv)
                    zi = (di * nb + b) * HEADS + h
                    v = v_ref[b, :, vs]
                    t = _chunk_terms(q_ref[b, :, ks] * qs, k_ref[b, :, ks], bc[:, ks], far, mid)
                    a = jnp.where(mask, _mm_nt(t["qem"], t["kim"]), 0.0)
                    z = z_scr[zi]
                    zs_ref[0, b * HEADS + h] = z
                    o_ref[b, :, vs] = _mm(a, v) + _mm_nt(t["qe"], z)
                    z_scr[zi] = z * t["dec"] + _mm_tn(v, t["kd"])

    in_specs, out_specs, out_shape = [], [], []
    for di, rev in enumerate((False, True)):
        ch = functools.partial(_scan_chunk, nl=nl, nc=nc, rev=rev)
        lch = functools.partial(_scan_lat_chunk, nl=nl, nc=nc, rev=rev)
        in_specs += [pl.BlockSpec((nb, c, dk_), lambda s, ch=ch: (0, ch(s), 0)),
                     pl.BlockSpec((nb, c, dk_), lambda s, ch=ch: (0, ch(s), 1)),
                     pl.BlockSpec((nb, c, dv_), lambda s, ch=ch: (0, ch(s), 0)),
                     pl.BlockSpec((nb, c, dk_), lambda s, ch=ch, di=di: (0, ch(s), di))]
        out_specs += [pl.BlockSpec((nb, c, dv_), lambda s, lch=lch: (0, lch(s), 0)),
                      pl.BlockSpec((1, nb * HEADS, hv, hk), lambda s: (s, 0, 0, 0)),
                      pl.BlockSpec((nb, c, dk_), lambda s, ch=ch: (0, ch(s), 0))]
        out_shape += [jax.ShapeDtypeStruct((nb, s_len, dv_), F32),
                      jax.ShapeDtypeStruct((ns, nb * HEADS, hv, hk), F32),
                      jax.ShapeDtypeStruct((nb, l_len, dk_), F32)]
    return pl.pallas_call(
        body, name="gla_fwd", grid=(ns,), in_specs=in_specs, out_specs=tuple(out_specs), out_shape=tuple(out_shape),
        scratch_shapes=[pltpu.VMEM((2 * nb * HEADS, hv, hk), F32)],
        compiler_params=_params())(pb3, pb3, pv3, g3, pb3, pb3, pv3, g3)


def _gla_bwd(pb3, pv3, do3, fwd_saved, nb, s_len, c_len, dk_, dv_):
    c = CHUNK
    nl, nc = s_len // c, c_len // c
    ns = nl + nc
    hk, hv = dk_ // HEADS, dv_ // HEADS
    l_len = pb3.shape[1]
    scale = hk ** -0.5
    mid = c // 2
    zs_f, b_f, zs_b, b_b = fwd_saved

    def body(*refs):
        ins, outs, dz_scr = refs[:12], refs[12:20], refs[20]
        s = pl.program_id(0)
        step = ns - 1 - s

        @pl.when(s == 0)
        def _():
            dz_scr[...] = jnp.zeros_like(dz_scr)

        lat = step >= nc
        qs = jnp.where(lat, scale, 0.0)
        dmul = jnp.where(lat, 1.0, 0.0)
        for di, rev in enumerate((False, True)):
            q_ref, k_ref, v_ref, b_ref, do_ref, zs_ref = ins[6 * di:6 * di + 6]
            dq_ref, dk_ref, dv_ref, dg_ref = outs[4 * di:4 * di + 4]
            mask, mask_t = _chunk_masks(c, rev)
            mt_bf = mask_t.astype(BF16)
            far = 0 if rev else c - 1
            far_row = lax.broadcasted_iota(jnp.int32, (c, hk), 0) == far
            for b in range(nb):
                db_parts = []
                for h in range(HEADS):
                    ks, vs = slice(h * hk, (h + 1) * hk), slice(h * hv, (h + 1) * hv)
                    zi = (di * nb + b) * HEADS + h
                    v = v_ref[b, :, vs]
                    d_o = do_ref[b, :, vs] * dmul
                    t = _chunk_terms(q_ref[b, :, ks] * qs, k_ref[b, :, ks], b_ref[b, :, ks], far, mid)
                    qem, kim, qe, kd = t["qem"], t["kim"], t["qe"], t["kd"]
                    a_t = jnp.where(mask_t, _mm_nt(kim, qem), 0.0)
                    d_a = jnp.where(mask, _mm_nt(d_o, v), 0.0)
                    d_at = jnp.where(mask_t, _mm_nt(v, d_o), 0.0)
                    z = zs_ref[0, b * HEADS + h]
                    dzn = dz_scr[zi]
                    dv_ref[b, :, vs] = (_mm(a_t, d_o) + _mm_nt(kd, dzn)).astype(dv_ref.dtype)
                    dqem = _mm(d_a, kim)
                    dkim = _mm(d_at, qem)
                    dqe = _mm(d_o, z)
                    dkd = _mm(v, dzn)
                    ddec = jnp.sum(z * dzn, axis=0, keepdims=True)
                    dz_scr[zi] = dzn * t["dec"] + _mm_tn(d_o, qe)
                    dq_ref[b, :, ks] = ((dqem * t["em"] + dqe * t["e"]) * qs).astype(dq_ref.dtype)
                    dk_ref[b, :, ks] = (dkim * t["eim"] + dkd * t["ed"]).astype(dk_ref.dtype)
                    db = dqem * qem - dkim * kim + dqe * qe - dkd * kd
                    extra = jnp.sum(dkd * kd, axis=0, keepdims=True) + ddec * t["dec"]
                    db_parts.append(db + jnp.where(far_row, extra, 0.0))
                dg_ref[b] = _tri_mm(mt_bf, jnp.concatenate(db_parts, axis=1))

    in_specs, out_specs, out_shape, args = [], [], [], []
    for di, rev in enumerate((False, True)):
        ch = lambda s, rev=rev: _scan_chunk(ns - 1 - s, nl, nc, rev)
        lch = lambda s, rev=rev: _scan_lat_chunk(ns - 1 - s, nl, nc, rev)
        in_specs += [pl.BlockSpec((nb, c, dk_), lambda s, ch=ch: (0, ch(s), 0)),
                     pl.BlockSpec((nb, c, dk_), lambda s, ch=ch: (0, ch(s), 1)),
                     pl.BlockSpec((nb, c, dv_), lambda s, ch=ch: (0, ch(s), 0)),
                     pl.BlockSpec((nb, c, dk_), lambda s, ch=ch: (0, ch(s), 0)),
                     pl.BlockSpec((nb, c, dv_), lambda s, lch=lch: (0, lch(s), 0)),
                     pl.BlockSpec((1, nb * HEADS, hv, hk), lambda s: (ns - 1 - s, 0, 0, 0))]
        args += [pb3, pb3, pv3, (b_b if rev else b_f), do3, (zs_b if rev else zs_f)]
        for w, dt in ((dk_, BF16), (dk_, BF16), (dv_, BF16), (dk_, F32)):
            out_specs.append(pl.BlockSpec((nb, c, w), lambda s, ch=ch: (0, ch(s), 0)))
            out_shape.append(jax.ShapeDtypeStruct((nb, l_len, w), dt))
    return pl.pallas_call(
        body, name="gla_bwd", grid=(ns,), in_specs=in_specs, out_specs=tuple(out_specs), out_shape=tuple(out_shape),
        scratch_shapes=[pltpu.VMEM((2 * nb * HEADS, hv, hk), F32)],
        compiler_params=_params())(*args)


def _tail(a1, pa, o_f, o_b, x2, tgt, mod, wc, wg, wo, ln_g, ln_b, gn_t, fg, nb, tm, n_split):
    tl, d = x2.shape
    nt = tl // tm
    per_ex = nt // nb
    hv = d // HEADS
    nrow = mod.shape[0]

    def part(shared, a1_ref, z_ref, r_ref, mc_ref, mg_ref, of_ref, ob_ref, x_ref, t_ref,
             dp_ref, da1_ref, do_ref, gx_ref, mrg_ref, dmo_ref, yci_ref, dyc_ref, ogi_ref, dyg_ref, sm_ref):
        bidx, gate, lng, lnb, fgv, gn, wc_, wg_, wo_ = shared

        a1v = a1_ref[...]
        mu = jnp.mean(a1v, axis=-1, keepdims=True)
        xc = a1v - mu
        rs = lax.rsqrt(jnp.mean(xc * xc, axis=-1, keepdims=True) + EPS)
        xh = xc * rs
        a2 = xh * lng + lnb
        s2 = _sigmoid(a2)
        a3 = a2 * s2
        zv = z_ref[...]
        sz = _sigmoid(zv)
        siluz = zv * sz
        ycin = a3 * siluz
        yconv = _mm(ycin, wc_)

        o = of_ref[...] + ob_ref[...]
        ohat_parts, rn_parts = [], []
        for h in range(HEADS):
            oh = o[:, h * hv:(h + 1) * hv]
            rn = lax.rsqrt(jnp.mean(oh * oh, axis=-1, keepdims=True) + EPS)
            ohat_parts.append(oh * rn)
            rn_parts.append(rn)
        ohat = jnp.concatenate(ohat_parts, axis=1)
        on = ohat * gn
        rv = r_ref[...]
        sr = _sigmoid(rv)
        silur = rv * sr
        ogin = on * silur
        ygla = _mm(ogin, wg_)

        sc = _sigmoid(mc_ref[...])
        sg = _sigmoid(mg_ref[...])
        merged = sc * yconv + sg * ygla
        mo = _mm(merged, wo_)
        hn = x_ref[...] + gate * mo
        rf = lax.rsqrt(jnp.mean(hn * hn, axis=-1, keepdims=True) + EPS)
        yh = hn * rf
        err = yh * fgv - t_ref[...]
        loss_part = 0.5 * jnp.sum(err * err) * (1.0 / d)

        dy = err * (1.0 / d)
        dfg = jnp.sum(dy * yh, axis=0, keepdims=True)
        dyh = dy * fgv
        dhn = rf * (dyh - yh * jnp.mean(dyh * yh, axis=-1, keepdims=True))
        gx_ref[...] = dhn
        dgate = jnp.sum(dhn * mo, axis=0, keepdims=True)
        dmo = gate * dhn
        dmerged = _mm_nt(dmo, wo_)
        dyconv = dmerged * sc
        dygla = dmerged * sg
        dp_ref[:, 2 * d:3 * d] = (dmerged * yconv * sc * (1.0 - sc)).astype(BF16)
        dp_ref[:, 3 * d:4 * d] = (dmerged * ygla * sg * (1.0 - sg)).astype(BF16)
        dycin = _mm_nt(dyconv, wc_)
        dogin = _mm_nt(dygla, wg_)
        mrg_ref[...] = merged.astype(BF16)
        dmo_ref[...] = dmo.astype(BF16)
        yci_ref[...] = ycin.astype(BF16)
        dyc_ref[...] = dyconv.astype(BF16)
        ogi_ref[...] = ogin.astype(BF16)
        dyg_ref[...] = dygla.astype(BF16)

        da3 = dycin * siluz
        dp_ref[:, 0:d] = (dycin * a3 * _dsilu(zv, sz)).astype(BF16)
        da2 = da3 * _dsilu(a2, s2)
        dlng = jnp.sum(da2 * xh, axis=0, keepdims=True)
        dlnb = jnp.sum(da2, axis=0, keepdims=True)
        dxh = da2 * lng
        da1_ref[...] = rs * (dxh - jnp.mean(dxh, axis=-1, keepdims=True)
                             - xh * jnp.mean(dxh * xh, axis=-1, keepdims=True))

        don = dogin * silur
        dp_ref[:, d:2 * d] = (dogin * on * _dsilu(rv, sr)).astype(BF16)
        dgn = jnp.sum(don * ohat, axis=0, keepdims=True)
        dyn = don * gn
        for h in range(HEADS):
            vs = slice(h * hv, (h + 1) * hv)
            oh_hat = ohat_parts[h]
            dh = dyn[:, vs]
            do_ref[:, vs] = (rn_parts[h] * (dh - oh_hat * jnp.mean(dh * oh_hat, axis=-1, keepdims=True))
                             ).astype(BF16)

        sm_ref[0:1, :] += dfg
        sm_ref[1:2, :] += dlng
        sm_ref[2:3, :] += dlnb
        sm_ref[3:4, :] += dgn
        sm_ref[4:5, :] += jnp.zeros((1, d), F32) + loss_part
        for b in range(nb):
            sm_ref[8 + b:9 + b, :] += jnp.where(bidx == b, dgate, 0.0)

    def body(*refs):
        mod_ref, wc_ref, wg_ref, wo_ref, lng_ref, lnb_ref, gn_ref, fg_ref = refs[9:17]
        sm_ref = refs[27]
        i = pl.program_id(0)

        @pl.when(i == 0)
        def _():
            sm_ref[...] = jnp.zeros_like(sm_ref)

        bidx = i // per_ex
        shared = (bidx, _rowsel(mod_ref[...], bidx, nb)[:, 2 * d:3 * d], lng_ref[...], lnb_ref[...], fg_ref[...],
                  jnp.concatenate([gn_ref[...]] * HEADS, axis=1), wc_ref[...], wg_ref[...], wo_ref[...])
        rows_per = tm // n_split
        for p in range(n_split):
            rows = pl.ds(p * rows_per, rows_per)
            part(shared, *[r.at[rows] for r in refs[0:9]], *[r.at[rows] for r in refs[17:27]], sm_ref)

    row = pl.BlockSpec((tm, d), lambda i: (i, 0))
    pcol = lambda blk: pl.BlockSpec((tm, d), lambda i: (i, blk))
    full = lambda arr: pl.BlockSpec(arr.shape, lambda i: (0,) * arr.ndim)
    bfo = jax.ShapeDtypeStruct((tl, d), BF16)
    f32o = jax.ShapeDtypeStruct((tl, d), F32)
    return pl.pallas_call(
        body, name="tail", grid=(nt,),
        in_specs=[row, pcol(2), pcol(3), pcol(4), pcol(5), row, row, row, row, full(mod), full(wc), full(wg),
                  full(wo), full(ln_g), full(ln_b), full(gn_t), full(fg)],
        out_specs=(pl.BlockSpec((tm, 4 * d), lambda i: (i, 0)), row, row, row, row, row, row, row, row, row,
                   pl.BlockSpec((16, d), lambda i: (0, 0))),
        out_shape=(jax.ShapeDtypeStruct((tl, 4 * d), BF16), f32o, bfo, f32o, bfo, bfo, bfo, bfo, bfo, bfo,
                   jax.ShapeDtypeStruct((16, d), F32)),
        compiler_params=_params())(a1, pa, pa, pa, pa, o_f, o_b, x2, tgt, mod, wc, wg, wo, ln_g, ln_b, gn_t, fg)


def _local_step(x, c, ctx, tgt, c_ctx, ada_w8, ada_b, norm_g, w_a, b_a, w_b, b_b, conv_w8, conv_b, ln_g, ln_b,
                up2, bias2, gla_norm_g, final_norm_g, proj, on_grads=None, on_du_a1=None):
    nb, s_len, d = x.shape
    c_len = ctx.shape[1]
    dk_, dv_ = d // 2, d
    tl, tc = nb * s_len, nb * c_len
    nbw = 2 * dk_ + dv_ + LANE
    tm = math.gcd(256, c_len)
    tiles = _Tiles(nb, s_len, c_len, tm, 2)
    l_len = tiles.rows_per_ex
    t_all = nb * l_len
    x2, ctx2, tgt2 = x.reshape(tl, d), ctx.reshape(tc, d), tgt.reshape(tl, d)

    cv = jnp.zeros((8, d), F32).at[0:nb].set(c).at[nb].set(c_ctx.reshape(d))
    mod = _ada_fwd(cv, ada_w8, ada_b)
    u = _norm_fwd(x2, ctx2, mod, norm_g, tiles)
    u3 = u.reshape(nb, l_len, d)
    tma = math.gcd(1024, s_len)
    pa = _matmul_bias("inproj_a", u3, w_a, b_a, s_len, tma, _tile(6 * d, 2048))
    tmb = math.gcd(1024, t_all)
    pb, pv, g_all = _inproj_b(u, w_b, b_b, up2, bias2, tmb, dk_, dv_)

    a1 = _conv_fwd(pa, conv_w8, conv_b, nb, s_len)
    lr_blk = (2 * dk_) // LANE
    pb3, pv3 = pb.reshape(nb, l_len, 2 * dk_ + LANE), pv.reshape(nb, l_len, dv_)
    o_f, zs_f, b_f, o_b, zs_b, b_b2 = _gla_fwd(pb3, pv3, g_all.reshape(nb, l_len, 2 * dk_), nb, s_len, c_len,
                                               dk_, dv_)

    conv_proj, gla_proj, w_out = proj(a1) if callable(proj) else proj
    tt = math.gcd(256, s_len)
    (dp_a2, da1, d_o, gx1, merged, dmo, ycin, dyconv, ogin, dygla, small) = _tail(
        a1, pa, o_f.reshape(tl, dv_), o_b.reshape(tl, dv_), x2, tgt2, mod, conv_proj, gla_proj, w_out, ln_g, ln_b,
        gla_norm_g, final_norm_g, nb, tt, 2)

    lat3 = lambda a: a.reshape(nb, s_len, a.shape[-1])
    tnw = _tile(d, 512)
    d_w_out, _ = _matmul_tn_whole("dw_out", lat3(merged), lat3(dmo), s_len, tnw, False)
    d_conv_proj, _ = _matmul_tn_whole("dw_conv_proj", lat3(ycin), lat3(dyconv), s_len, tnw, False)
    d_gla_proj, _ = _matmul_tn_whole("dw_gla_proj", lat3(ogin), lat3(dygla), s_len, tnw, False)

    dp_a1, d_conv_w8, d_conv_b = _conv_bwd(pa, da1, conv_w8, nb, s_len)
    gl = _gla_bwd(pb3, pv3, d_o.reshape(nb, s_len, dv_), (zs_f, b_f, zs_b, b_b2), nb, s_len, c_len, dk_, dv_)
    gl = [g_.reshape(t_all, g_.shape[-1]) for g_ in gl]
    dp_b, d_up2, d_bias2 = _decay_bwd(pb, up2, bias2, gl[0:4], gl[4:8], tiles, lr_blk, dk_, dv_)

    dw_a1, db_a1 = _matmul_tn_whole("dw_a1", u3, lat3(dp_a1), s_len, tnw, True)
    dw_a2, db_a2 = _matmul_tn_whole("dw_a2", u3, lat3(dp_a2), s_len, tnw, True)
    dw_b, db_b = _matmul_tn("dw_b", u, dp_b, t_all, tmb, nbw)
    grads = dict(w_a1=dw_a1, w_a2=dw_a2, w_b=dw_b, conv_w8=d_conv_w8, conv_proj=d_conv_proj, up2=d_up2,
                 gla_proj=d_gla_proj, w_out=d_w_out)

    tka = _tile(2 * d, 2048)
    du_a1 = _matmul_nt("du_a1", dp_a1, w_a, 0, tma, tka, after=on_grads(grads) if on_grads else ())
    du_a2 = _matmul_nt("du_a2", dp_a2, w_a, (2 * d) // tka, tma, tka, after=on_du_a1(du_a1) if on_du_a1 else ())
    du_b = _matmul_nt("du_b", dp_b, w_b, 0, tmb, nbw)
    grad_x2, dmod_ss, d_norm_g = _norm_bwd(x2, ctx2, mod, norm_g, [du_a1, du_a2], du_b, gx1, tiles)
    ada_sv, ada_dmod, d_ada_b, d_cv = _ada_bwd(cv, ada_w8, dmod_ss, small)

    return dict(
        grads, grad_x=grad_x2.reshape(nb, s_len, d), small=small, cv=d_cv, ada_sv=ada_sv, ada_dmod=ada_dmod,
        ada_b=d_ada_b,
        norm_g=d_norm_g, b_a1=db_a1, b_a2=db_a2, b_b=db_b, conv_b=d_conv_b, bias2=d_bias2)


def _regroup_pieces(d, r, wshard):
    cb = d // N_DEV
    segs = []
    for j in range(N_DEV):
        segs.append((j * cb, cb, 0, 2 * j * cb))
    for j in range(N_DEV):
        segs.append((d + j * cb, cb, 0, (2 * j + 1) * cb))
    segs += [(2 * d, d, 0, 2 * d), (3 * d, 2 * d + 2 * r, 1, 0), (5 * d + 2 * r, 3 * d, 0, 3 * d)]
    pieces = []
    for o0, w, dst, d0 in segs:
        lo = o0
        while lo < o0 + w:
            j = lo // wshard
            hi = min(o0 + w, (j + 1) * wshard)
            pieces.append((j, lo - j * wshard, hi - lo, dst, d0 + lo - o0))
            lo = hi
    return pieces


def _regroup(o, d, r):
    n_in = 8 * d + 2 * r
    parts = ([], [])
    for _, s0, n, dst, _ in sorted(_regroup_pieces(d, r, n_in), key=lambda p: (p[3], p[4])):
        parts[dst].append(o[..., s0:s0 + n])
    pad = jnp.zeros(o.shape[:-1] + (LANE - 2 * r,), o.dtype)
    return jnp.concatenate(parts[0], axis=-1), jnp.concatenate(parts[1] + [pad], axis=-1)


def _unshard_w_in(g_win, d, r, after=()):
    n_sh, _, ws = g_win.shape
    nbw = 2 * d + LANE
    pieces = _regroup_pieces(d, r, ws)
    tr = math.gcd(d, 256)

    def body(g_ref, *rest):
        a_ref, b_ref = rest[len(after):]
        dsts = (a_ref, b_ref)
        for j, s0, n, dst, d0 in pieces:
            dsts[dst][:, pl.ds(d0, n)] = g_ref[j, :, pl.ds(s0, n)]
        b_ref[:, pl.ds(2 * d + 2 * r, LANE - 2 * r)] = jnp.zeros((tr, LANE - 2 * r), b_ref.dtype)

    return pl.pallas_call(
        body, name="unshard_w_in", grid=(d // tr,),
        in_specs=[pl.BlockSpec((n_sh, tr, ws), lambda i: (0, i, 0))] + [_ANY] * len(after),
        out_specs=(pl.BlockSpec((tr, 6 * d), lambda i: (i, 0)), pl.BlockSpec((tr, nbw), lambda i: (i, 0))),
        out_shape=(jax.ShapeDtypeStruct((d, 6 * d), g_win.dtype), jax.ShapeDtypeStruct((d, nbw), g_win.dtype)),
        compiler_params=_params())(g_win, *after)


def _reshard_w_in(dwt_a1, dwt_a2, dwt_b, d, r):
    ws = (8 * d + 2 * r) // N_DEV
    pieces = _regroup_pieces(d, r, ws)
    tc = math.gcd(d, 256)

    def body(a1_ref, a2_ref, b_ref, o_ref):
        for j, s0, n, dst, d0 in pieces:
            if dst == 1:
                src = b_ref[pl.ds(d0, n), :]
            elif d0 < 2 * d:
                src = a1_ref[pl.ds(d0, n), :]
            else:
                src = a2_ref[pl.ds(d0 - 2 * d, n), :]
            o_ref[j, pl.ds(s0, n), :] = src

    col = lambda h: pl.BlockSpec((h, tc), lambda i: (0, i))
    return pl.pallas_call(
        body, name="reshard_w_in", grid=(d // tc,),
        in_specs=[col(2 * d), col(4 * d), col(2 * d + LANE)],
        out_specs=pl.BlockSpec((N_DEV, ws, tc), lambda i: (0, 0, i)),
        out_shape=jax.ShapeDtypeStruct((N_DEV, ws, d), dwt_b.dtype),
        compiler_params=_params())(dwt_a1, dwt_a2, dwt_b)


_SMALL = ("c_ctx", "ada_b", "norm_g", "b_in", "conv_b", "conv_ln_g", "conv_ln_b", "decay_bias_fwd",
          "decay_bias_bwd", "gla_norm_g", "final_norm_g")


def _small_layout(d, r):
    sizes = dict(c_ctx=d, ada_b=3 * d, norm_g=d, b_in=8 * d + 2 * r, conv_b=d, conv_ln_g=d, conv_ln_b=d,
                 decay_bias_fwd=d // 2, decay_bias_bwd=d // 2, gla_norm_g=d // HEADS, final_norm_g=d, loss=1)
    table, off = {}, 0
    for name in _SMALL + ("loss",):
        table[name] = (off, sizes[name])
        off += -(-sizes[name] // LANE) * LANE
    return table, off


def _pack_small(g, nb, d, r):
    table, width = _small_layout(d, r)
    hv = d // HEADS
    pieces = _regroup_pieces(d, r, 8 * d + 2 * r)
    names = ("small", "cv", "ada_b", "norm_g", "b_a1", "b_a2", "b_b", "conv_b", "bias2")

    def body(sm, cv, ab, ng, ba1, ba2, bb, cvb, b2, o_ref):
        o_ref[...] = jnp.zeros_like(o_ref)

        def put(name, val):
            off, n = table[name]
            o_ref[:, pl.ds(off, n)] = val

        put("c_ctx", cv[nb:nb + 1, :])
        put("ada_b", ab[...])
        put("norm_g", ng[...])
        off_b = table["b_in"][0]
        for _, s0, n, dst, d0 in pieces:
            if dst == 1:
                src = bb[:, pl.ds(d0, n)]
            elif d0 < 2 * d:
                src = ba1[:, pl.ds(d0, n)]
            else:
                src = ba2[:, pl.ds(d0 - 2 * d, n)]
            o_ref[:, pl.ds(off_b + s0, n)] = src
        put("conv_b", cvb[...])
        put("conv_ln_g", sm[1:2, :])
        put("conv_ln_b", sm[2:3, :])
        put("decay_bias_fwd", b2[:, 0:d // 2])
        put("decay_bias_bwd", b2[:, d // 2:d])
        gn = sm[3:4, 0:hv]
        for h in range(1, HEADS):
            gn = gn + sm[3:4, h * hv:(h + 1) * hv]
        put("gla_norm_g", gn)
        put("final_norm_g", sm[0:1, :])
        put("loss", sm[4:5, 0:1])

    return pl.pallas_call(body, name="pack_small", out_shape=jax.ShapeDtypeStruct((1, width), F32),
                          compiler_params=_params())(*[g[k] for k in names])


def _small_adam(parts, ws, ms, vs, d, r):
    table, width = _small_layout(d, r)
    n_parts = parts.shape[0]
    k = len(_SMALL)

    def body(p_ref, *refs):
        w_refs, m_refs, v_refs = refs[0:k], refs[k:2 * k], refs[2 * k:3 * k]
        outs = refs[3 * k:]
        tot = p_ref[0]
        for i in range(1, n_parts):
            tot = tot + p_ref[i]
        for i, name in enumerate(_SMALL):
            off, n = table[name]
            g = tot[:, off:off + n]
            outs[i][...] = g
            outs[k + i][...], outs[2 * k + i][...], outs[3 * k + i][...] = _adamw(
                g, w_refs[i][...], m_refs[i][...], v_refs[i][...])
        off, _ = table["loss"]
        outs[4 * k][...] = tot[:, off:off + 1]

    shapes = [jax.ShapeDtypeStruct(w.shape, F32) for w in ws]
    res = pl.pallas_call(body, name="small_adam", out_shape=tuple(shapes * 4 + [jax.ShapeDtypeStruct((1, 1), F32)]),
                         compiler_params=_params())(parts, *ws, *ms, *vs)
    return res[0:k], res[k:2 * k], res[2 * k:3 * k], res[3 * k:4 * k], res[4 * k]


def _mesh_pos():
    return lax.axis_index("x"), lax.axis_index("y"), lax.axis_index("c")


def _all_gather(arrs):
    n = len(arrs)
    ns = 9
    split = [a.ndim == 2 and a.shape[0] % 32 == 0 for a in arrs]

    def body(*refs):
        ins, outs = refs[:n], refs[n:2 * n]
        send_sems, recv_sems, local_sems = refs[2 * n:]
        x, y, c = _mesh_pos()
        me, sibling = (x, y, c), (x, y, 1 - c)
        xn, yn, dg = (1 - x, y, c), (x, 1 - y, c), (1 - x, 1 - y, c)
        other = lambda pos: (pos[0], pos[1], 1 - c)

        def slot(a, pos, half):
            ref = outs[a].at[4 * pos[0] + 2 * pos[1] + pos[2]]
            if half is None:
                return ref
            rows = arrs[a].shape[0] // 2
            return ref.at[pl.ds(half * rows, rows)]

        def copy(a, k, block, to, src=None, half=None):
            dst = slot(a, block, half)
            return pltpu.make_async_remote_copy(
                src_ref=dst if src is None else src, dst_ref=dst,
                send_sem=send_sems.at[ns * a + k], recv_sem=recv_sems.at[ns * a + k],
                device_id=to, device_id_type=MESH)

        h0 = lambda a: 0 if split[a] else None
        mine = [pltpu.make_async_copy(ins[a], slot(a, me, None), local_sems.at[a]) for a in range(n)]
        for cp in mine:
            cp.start()
        sent = []
        for a in range(n):
            sent += [copy(a, 0, me, sibling, src=ins[a]), copy(a, 1, me, xn, src=ins[a]),
                     copy(a, 2, me, yn, src=ins[a])]
        for cp in sent:
            cp.start()

        def pass_on(cp):
            cp.start()
            sent.append(cp)

        for a in range(n):
            copy(a, 1, xn, me).wait_recv()
            pass_on(copy(a, 3, xn, sibling))
            pass_on(copy(a, 4, xn, yn, half=h0(a)))
        for a in range(n):
            copy(a, 2, yn, me).wait_recv()
            pass_on(copy(a, 5, yn, sibling))
            if split[a]:
                pass_on(copy(a, 6, yn, xn, half=1))
        for a in range(n):
            copy(a, 4, dg, me, half=h0(a)).wait_recv()
            pass_on(copy(a, 7, dg, sibling, half=h0(a)))
            if split[a]:
                copy(a, 6, dg, me, half=1).wait_recv()
                pass_on(copy(a, 8, dg, sibling, half=1))
        for a in range(n):
            copy(a, 0, sibling, me).wait_recv()
            copy(a, 3, other(xn), me).wait_recv()
            copy(a, 5, other(yn), me).wait_recv()
            copy(a, 7, other(dg), me, half=h0(a)).wait_recv()
            if split[a]:
                copy(a, 8, other(dg), me, half=1).wait_recv()
        for cp in sent:
            cp.wait_send()
        for cp in mine:
            cp.wait()

    return pl.pallas_call(
        body, name="all_gather",
        out_shape=tuple(jax.ShapeDtypeStruct((N_DEV,) + a.shape, a.dtype) for a in arrs),
        in_specs=[_ANY] * n, out_specs=tuple([_ANY] * n),
        scratch_shapes=[pltpu.SemaphoreType.DMA((ns * n,)), pltpu.SemaphoreType.DMA((ns * n,)),
                        pltpu.SemaphoreType.DMA((n,))],
    )(*arrs)


def _elementwise_tile(r, cdim, cols=2 * LANE):
    if r % 8 == 0 and r > 256:
        return math.gcd(r, 256), cdim
    if r > 256 and cdim % cols == 0:
        return r, cols
    return r, cdim


def _pair_sum(name, mine, theirs):
    _, r, cdim = mine.shape
    tr, tc = _elementwise_tile(r, cdim)

    def body(c_ref, m_ref, t_ref, o_ref):
        o_ref[...] = (m_ref[...].astype(F32) + t_ref[...].astype(F32)).astype(o_ref.dtype)

    return pl.pallas_call(
        body, name=name,
        grid_spec=pltpu.PrefetchScalarGridSpec(
            num_scalar_prefetch=1, grid=(r // tr, cdim // tc),
            in_specs=[pl.BlockSpec((4, None, tr, tc), lambda i, j, c_ref: (0, c_ref[0], i, j)),
                      pl.BlockSpec((4, tr, tc), lambda i, j, c_ref: (0, i, j))],
            out_specs=pl.BlockSpec((4, tr, tc), lambda i, j, c_ref: (0, i, j))),
        out_shape=jax.ShapeDtypeStruct((4, r, cdim), mine.dtype),
        compiler_params=_params())(lax.axis_index("c").reshape(1), mine.reshape(4, 2, r, cdim), theirs)


def _pair_sum_small(mines, theirs):
    n = len(mines)

    def body(*refs):
        c = lax.axis_index("c")
        for i in range(n):
            m_ref, t_ref, o_ref = refs[i], refs[n + i], refs[2 * n + i]
            own = jnp.where(c == 0, m_ref[:, 0].astype(F32), m_ref[:, 1].astype(F32))
            o_ref[...] = (own + t_ref[...].astype(F32)).astype(o_ref.dtype)

    return pl.pallas_call(
        body, name="pair_sum_small_weights",
        out_shape=tuple(jax.ShapeDtypeStruct(t.shape, m.dtype) for m, t in zip(mines, theirs)),
        compiler_params=_params())(*[m.reshape((4, 2) + m.shape[1:]) for m in mines], *theirs)


_HBM = pl.BlockSpec(memory_space=pltpu.HBM)
_SEM = pl.BlockSpec(memory_space=pltpu.SEMAPHORE)


def _copies_start(name, srcs, lands, make_copies, n_sems):
    n, m = len(srcs), len(lands)

    def body(*refs):
        ins = refs[:n + m]
        send_sems, recv_sems = refs[n + m], refs[n + m + 1]
        for cp in make_copies(ins[:n], ins[n:], send_sems, recv_sems):
            cp.start()
        refs[-1][...] = jnp.zeros_like(refs[-1])

    res = pl.pallas_call(
        body, name=name,
        out_shape=(pltpu.SemaphoreType.DMA((n_sems,)), pltpu.SemaphoreType.DMA((n_sems,)),
                   *[pltpu.HBM(a.shape, a.dtype) for a in (*srcs, *lands)], jax.ShapeDtypeStruct((8, LANE), F32)),
        in_specs=[_HBM] * (n + m),
        out_specs=(_SEM, _SEM, *[_HBM] * (n + m), pl.BlockSpec(memory_space=pltpu.VMEM)),
        input_output_aliases={i: 2 + i for i in range(n + m)},
        compiler_params=pltpu.CompilerParams(has_side_effects=pltpu.SideEffectType.DATAFLOW_SIDE_EFFECTING),
    )(*[pltpu.with_memory_space_constraint(a, pltpu.HBM) for a in (*srcs, *lands)])
    return res[0], res[1], res[2:2 + n], res[2 + n:2 + n + m], res[-1]


def _copies_wait(name, started, after, make_copies):
    send_sems, recv_sems, srcs, lands, _ = started
    n, m = len(srcs), len(lands)

    def body(*refs):
        ins = refs[:n + m]
        for cp in make_copies(ins[:n], ins[n:], refs[n + m], refs[n + m + 1]):
            cp.wait_send()
            cp.wait_recv()

    res = pl.pallas_call(
        body, name=name,
        out_shape=tuple(pltpu.HBM(a.shape, a.dtype) for a in (*srcs, *lands)),
        in_specs=[_HBM] * (n + m) + [_SEM, _SEM] + [_ANY] * len(after),
        out_specs=tuple([_HBM] * (n + m)),
        input_output_aliases={i: i for i in range(n + m)},
        compiler_params=pltpu.CompilerParams(has_side_effects=pltpu.SideEffectType.DATAFLOW_SIDE_EFFECTING),
    )(*srcs, *lands, send_sems, recv_sems, *after)
    return res[:n], res[n:]


def _gather_copies(srcs, lands, send_sems, recv_sems):
    x, y, c = _mesh_pos()
    me_i = 4 * x + 2 * y + c
    copies = []
    for rel in range(1, N_DEV):
        peer = (1 - x if rel & 4 else x, 1 - y if rel & 2 else y, 1 - c if rel & 1 else c)
        for a in range(len(srcs)):
            copies.append(pltpu.make_async_remote_copy(
                src_ref=srcs[a], dst_ref=lands[a].at[me_i], send_sem=send_sems.at[7 * a + rel - 1],
                recv_sem=recv_sems.at[7 * a + rel - 1], device_id=peer, device_id_type=MESH))
    return copies


def _sibling_copies(srcs, lands, send_sems, recv_sems):
    x, y, c = _mesh_pos()
    return [pltpu.make_async_remote_copy(
        src_ref=srcs[a].at[2 * k + (1 - c)], dst_ref=lands[a].at[k], send_sem=send_sems.at[4 * a + k],
        recv_sem=recv_sems.at[4 * a + k], device_id=(x, y, 1 - c), device_id_type=MESH)
        for a in range(len(srcs)) for k in range(4)]


def _chip_copies(srcs, lands, send_sems, recv_sems):
    x, y, c = _mesh_pos()
    my_chip = 2 * x + y
    copies = []
    for rel in range(1, 4):
        px = 1 - x if rel & 2 else x
        py = 1 - y if rel & 1 else y
        for a in range(len(srcs)):
            copies.append(pltpu.make_async_remote_copy(
                src_ref=srcs[a].at[2 * px + py], dst_ref=lands[a].at[my_chip], send_sem=send_sems.at[3 * a + rel - 1],
                recv_sem=recv_sems.at[3 * a + rel - 1], device_id=(px, py, c), device_id_type=MESH))
    return copies


def _sum_adam(name, parts, w, m, v, own=None):
    unit_mid = w.ndim == 3
    _, r, cdim = parts.shape
    n_parts = parts.shape[0]
    tr, tc = _elementwise_tile(r, cdim, (4 if unit_mid else 2) * LANE)
    extra = [] if own is None else [own]

    def body(p_ref, *refs):
        w_ref, m_ref, v_ref, g_ref, d_ref, nm_ref, nv_ref = refs[len(extra):]
        if own is None:
            part = lambda k: p_ref[k].astype(F32)
        else:
            my_chip = 2 * lax.axis_index("x") + lax.axis_index("y")
            part = lambda k: jnp.where(my_chip == k, refs[0][...], p_ref[k]).astype(F32)
        g = part(0)
        for k in range(1, n_parts):
            g = g + part(k)
        if unit_mid:
            g = g.reshape(tr, 1, tc)
        g_ref[...] = g
        d_ref[...], nm_ref[...], nv_ref[...] = _adamw(g, w_ref[...], m_ref[...], v_ref[...])

    blk = (pl.BlockSpec((tr, 1, tc), lambda i, j: (i, 0, j)) if unit_mid
           else pl.BlockSpec((tr, tc), lambda i, j: (i, j)))
    o = jax.ShapeDtypeStruct(w.shape, F32)
    return pl.pallas_call(
        body, name=name, grid=(r // tr, cdim // tc),
        in_specs=[pl.BlockSpec((n_parts, tr, tc), lambda i, j: (0, i, j))]
        + [pl.BlockSpec((None, tr, tc), lambda i, j: (2 * lax.axis_index("x") + lax.axis_index("y"), i, j))] * len(extra)
        + [blk, blk, blk],
        out_specs=(blk, blk, blk, blk), out_shape=(o, o, o, o),
        compiler_params=_params())(parts, *extra, w, m, v)


def _sum_adam_small(items):
    n = len(items)
    flat = [a for it in items for a in it]

    def body(*refs):
        ins, bufs, sems = refs[:5 * n], refs[9 * n:14 * n], refs[14 * n]
        copies = [pltpu.make_async_copy(ins[j], bufs[j], sems.at[j]) for j in range(5 * n)]
        for cp in copies:
            cp.start()
        my_chip = 2 * lax.axis_index("x") + lax.axis_index("y")
        for i in range(n):
            for cp in copies[5 * i:5 * i + 5]:
                cp.wait()
            p_ref, own_ref, w_ref, m_ref, v_ref = bufs[5 * i:5 * i + 5]
            g_ref, d_ref, nm_ref, nv_ref = refs[5 * n + 4 * i:5 * n + 4 * i + 4]
            g = None
            for k in range(p_ref.shape[0]):
                part = jnp.where(my_chip == k, own_ref[k], p_ref[k]).astype(F32)
                g = part if g is None else g + part
            g_ref[...] = g
            d_ref[...], nm_ref[...], nv_ref[...] = _adamw(g, w_ref[...], m_ref[...], v_ref[...])

    out_shape = tuple(jax.ShapeDtypeStruct(it[2].shape, F32) for it in items for _ in range(4))
    res = pl.pallas_call(body, name="adam_small_weights", out_shape=out_shape, in_specs=[_ANY] * len(flat),
                         scratch_shapes=[pltpu.VMEM(a.shape, a.dtype) for a in flat]
                         + [pltpu.SemaphoreType.DMA((len(flat),))],
                         compiler_params=_params())(*flat)
    return [res[4 * i:4 * i + 4] for i in range(n)]


_WEIGHTS = ("c_ctx", "ada_w", "ada_b", "norm_g", "w_in", "b_in", "conv_w", "conv_b", "conv_ln_g", "conv_ln_b",
            "conv_proj", "decay_up_fwd", "decay_bias_fwd", "decay_up_bwd", "decay_bias_bwd", "gla_norm_g",
            "gla_proj", "w_out", "final_norm_g")


def _as2d(a):
    if a.ndim == 1:
        return a.reshape(1, -1)
    return a.reshape(-1, a.shape[-1])


def kernel(x, c, ctx, c_ctx, ada_w, ada_b, norm_g, w_in, b_in, conv_w, conv_b, conv_ln_g, conv_ln_b, conv_proj, decay_up_fwd, decay_bias_fwd, decay_up_bwd, decay_bias_bwd, gla_norm_g, gla_proj, w_out, final_norm_g, loss_target, m_c_ctx, m_ada_w, m_ada_b, m_norm_g, m_w_in, m_b_in, m_conv_w, m_conv_b, m_conv_ln_g, m_conv_ln_b, m_conv_proj, m_decay_up_fwd, m_decay_bias_fwd, m_decay_up_bwd, m_decay_bias_bwd, m_gla_norm_g, m_gla_proj, m_w_out, m_final_norm_g, v_c_ctx, v_ada_w, v_ada_b, v_norm_g, v_w_in, v_b_in, v_conv_w, v_conv_b, v_conv_ln_g, v_conv_ln_b, v_conv_proj, v_decay_up_fwd, v_decay_bias_fwd, v_decay_up_bwd, v_decay_bias_bwd, v_gla_norm_g, v_gla_proj, v_w_out, v_final_norm_g):
    env = dict(locals())
    wts = {k: env[k] for k in _WEIGHTS}
    d = x.shape[-1]
    r = decay_up_fwd.shape[1]
    dk_ = d // 2

    ds, dks = d // N_DEV, dk_ // N_DEV
    g_win, g_ada, conv_w8, g_up = _all_gather(
        [w_in[0].astype(BF16), ada_w[0].astype(BF16), conv_w[0],
         jnp.concatenate([decay_up_fwd[0], decay_up_bwd[0]], axis=1)])
    proj_own = [conv_proj[0].astype(BF16), gla_proj[0].astype(BF16), w_out[0].astype(BF16)]
    me_i = 4 * lax.axis_index("x") + 2 * lax.axis_index("y") + lax.axis_index("c")
    proj_lands = [lax.dynamic_update_slice(lax.empty((N_DEV,) + a.shape, a.dtype), a[None], (me_i, 0, 0))
                  for a in proj_own]
    proj_start = _copies_start("proj_gather_start", proj_own, proj_lands, _gather_copies, 7 * 3)

    def proj(after):
        _, lands = _copies_wait("proj_gather_wait", proj_start, (after,), _gather_copies)
        return [w.reshape(d, d) for w in lands]

    w_a, w_b = _unshard_w_in(g_win, d, r, after=(proj_start[4],))
    up_f = g_up[:, :, 0:dks].transpose(1, 0, 2).reshape(r, dk_)
    up_b = g_up[:, :, dks:].transpose(1, 0, 2).reshape(r, dk_)
    up2 = jnp.zeros((LANE, 2 * dk_), F32).at[0:r, 0:dk_].set(up_f).at[r:2 * r, dk_:].set(up_b)
    bias2 = jnp.concatenate([decay_bias_fwd, decay_bias_bwd], axis=1)
    b_a, b_b = _regroup(b_in, d, r)

    comm = {}

    def on_grads(gr):
        d_up = jnp.concatenate([gr["up2"][0:r, 0:dk_].reshape(r, N_DEV, dks).transpose(1, 0, 2),
                                gr["up2"][r:2 * r, dk_:].reshape(r, N_DEV, dks).transpose(1, 0, 2)], axis=2)
        mine = [_reshard_w_in(gr["w_a1"], gr["w_a2"], gr["w_b"], d, r), gr["conv_proj"].reshape(N_DEV, ds, d),
                gr["gla_proj"].reshape(N_DEV, ds, d), gr["w_out"].reshape(N_DEV, ds, d), gr["conv_w8"], d_up]
        lands = [lax.empty((4,) + a.shape[1:], a.dtype) for a in mine]
        comm["sibling"] = _copies_start("grad_sibling_start", mine, lands, _sibling_copies, 4 * len(mine))
        return (comm["sibling"][4],)

    def on_du_a1(du_a1):
        mine, theirs = _copies_wait("grad_sibling_wait", comm["sibling"], (du_a1,), _sibling_copies)
        sums = [_pair_sum("pair_sum_w_in", mine[0], theirs[0])] + list(_pair_sum_small(mine[1:], theirs[1:]))
        lands = [lax.empty(a.shape, a.dtype) for a in sums]
        comm["chips"] = _copies_start("grad_chips_start", sums, lands, _chip_copies, 3 * len(sums))
        return (comm["chips"][4],)

    g = _local_step(x, c, ctx, loss_target, c_ctx, g_ada, ada_b, norm_g[0:1], w_a, b_a, w_b, b_b,
                    conv_w8, conv_b, conv_ln_g, conv_ln_b, up2, bias2, gla_norm_g, final_norm_g.reshape(1, d),
                    proj, on_grads, on_du_a1)

    small_mine = [_pack_small(g, x.shape[0], d, r), g["ada_sv"], g["ada_dmod"]]
    small_lands = [lax.dynamic_update_slice(lax.empty((N_DEV,) + a.shape, F32), a[None], (me_i, 0, 0))
                   for a in small_mine]
    small_start = _copies_start("small_gather_start", small_mine, small_lands, _gather_copies, 7 * len(small_mine))
    own, landed = _copies_wait("grad_chips_wait", comm["chips"], (small_start[4],), _chip_copies)
    o_win, o_cp, o_gp, o_wo, o_cw, o_up = own
    x_win, x_cp, x_gp, x_wo, x_cw, x_up = landed

    out = {}

    as_rows = lambda a: jnp.transpose(a, (2, 0, 1))
    res = _sum_adam("adam_w_in", x_win, as_rows(w_in), as_rows(m_w_in), as_rows(v_w_in), o_win)
    for pre, arr in zip(("grad_", "delta_", "new_m_", "new_v_"), res):
        out[pre + "w_in"] = jnp.transpose(arr, (1, 2, 0))
    small_w = (("conv_proj", x_cp, o_cp), ("gla_proj", x_gp, o_gp), ("w_out", x_wo, o_wo), ("conv_w", x_cw, o_cw),
               ("decay_up_fwd", x_up[:, :, 0:dks], o_up[:, :, 0:dks]),
               ("decay_up_bwd", x_up[:, :, dks:], o_up[:, :, dks:]))
    small_res = _sum_adam_small([(p, o, _as2d(wts[k]), _as2d(env["m_" + k]), _as2d(env["v_" + k]))
                                 for k, p, o in small_w])
    for (k, _, _), arrs in zip(small_w, small_res):
        for pre, arr in zip(("grad_", "delta_", "new_m_", "new_v_"), arrs):
            out[pre + k] = arr.reshape(wts[k].shape)

    _, (packs, sv_all, dmod_all) = _copies_wait("small_gather_wait", small_start, (res[0], out["grad_w_out"]),
                                                _gather_copies)
    ada_res = _ada_adam(sv_all, dmod_all, _as2d(ada_w), _as2d(m_ada_w), _as2d(v_ada_w), x.shape[0] + 1)
    for pre, arr in zip(("grad_", "delta_", "new_m_", "new_v_"), ada_res):
        out[pre + "ada_w"] = arr.reshape(ada_w.shape)
    row = lambda a: a.reshape(1, -1)
    sg, sd, sm, sv, loss = _small_adam(packs, [row(wts[k]) for k in _SMALL], [row(env["m_" + k]) for k in _SMALL],
                                       [row(env["v_" + k]) for k in _SMALL], d, r)
    for i, k in enumerate(_SMALL):
        for pre, arrs in (("grad_", sg), ("delta_", sd), ("new_m_", sm), ("new_v_", sv)):
            out[pre + k] = arrs[i].reshape(wts[k].shape)
    loss = loss.reshape(())

    return (loss, g["grad_x"], *[out["grad_" + k] for k in _WEIGHTS], *[out["delta_" + k] for k in _WEIGHTS],
            *[out["new_m_" + k] for k in _WEIGHTS], *[out["new_v_" + k] for k in _WEIGHTS])
```
